```python
import jax, jax.numpy as jnp
from jax import lax
import numpy as np

D_MODEL = 2048
BATCH = 8
SEQ = 2048
DEPTH = 2

CHUNK = 64
EPS = 1e-6
D_POOL = D_MODEL // 2
POOL_WINDOWS = (2, 4, 8, 16)
N_POOL_GROUPS = len(POOL_WINDOWS)
POOL_GROUP = D_POOL // N_POOL_GROUPS
D_CONV = D_MODEL // 2
CONV_K = 31
D_AB_IN = D_POOL + 2 * D_CONV
D_SHORT = D_MODEL
SHORT_K = 3
D_FF = 4 * D_MODEL

N_EVEN = (DEPTH + 1) // 2
N_ODD = DEPTH // 2

kernel_name = "hybrid_pool_conformer_shortconv_trunk"


def rms_norm(x, g):
    xf = x.astype(jnp.float32)
    y = xf * lax.rsqrt(jnp.mean(xf * xf, axis=-1, keepdims=True) + EPS)
    return (y * g.astype(jnp.float32)).astype(x.dtype)


def layer_norm(x, g, b):
    xf = x.astype(jnp.float32)
    mu = jnp.mean(xf, axis=-1, keepdims=True)
    xc = xf - mu
    var = jnp.mean(xc * xc, axis=-1, keepdims=True)
    y = xc * lax.rsqrt(var + EPS) * g.astype(jnp.float32) + b.astype(jnp.float32)
    return y.astype(x.dtype)


def causal_depthwise_conv(u, w):
    k = w.shape[0]
    return lax.conv_general_dilated(
        u, w[:, None, :].astype(u.dtype), window_strides=(1,), padding=[(k - 1, 0)],
        dimension_numbers=("NWC", "WIO", "NWC"), feature_group_count=u.shape[-1])


def multiscale_pool(u, pool_w, pool_scale):
    b, t, _ = u.shape
    uf = u.astype(jnp.float32)
    csp = jnp.pad(jnp.cumsum(uf, axis=1), ((0, 0), (1, 0), (0, 0)))
    n_valid = jnp.arange(1, t + 1, dtype=jnp.float32)
    means = []
    for g, w in enumerate(POOL_WINDOWS):
        c = csp[..., g * POOL_GROUP:(g + 1) * POOL_GROUP]
        lag = jnp.pad(c, ((0, 0), (w - 1, 0), (0, 0)))[:, :t]
        cnt = jnp.minimum(n_valid, float(w))[None, :, None]
        means.append((c[:, 1:] - lag) / cnt)
    pooled = (jnp.concatenate(means, axis=-1) - uf).astype(u.dtype)
    pooled = pooled.reshape(b, t, N_POOL_GROUPS, POOL_GROUP)
    mixed = jnp.einsum("btgc,gce->btge", pooled, pool_w).reshape(b, t, D_POOL)
    return mixed * pool_scale


def pool_conformer_mixer(h, w_in, pool_w, pool_scale, conv_w, conv_b, ln_g, ln_b, w_out):
    z = jnp.einsum("btd,de->bte", h, w_in)
    u_pool = z[..., :D_POOL]
    v = z[..., D_POOL:D_POOL + D_CONV]
    gate = z[..., D_POOL + D_CONV:]
    y_pool = multiscale_pool(u_pool, pool_w, pool_scale)
    c = causal_depthwise_conv(v * jax.nn.sigmoid(gate), conv_w) + conv_b
    y_conv = jax.nn.silu(layer_norm(c, ln_g, ln_b))
    y = jnp.concatenate([y_pool, y_conv], axis=-1)
    return jnp.einsum("bte,ed->btd", y, w_out)


def short_conv_mixer(h, w_in, conv_w, w_out):
    z = jnp.einsum("btd,de->bte", h, w_in)
    b_gate = z[..., :D_SHORT]
    c_gate = z[..., D_SHORT:2 * D_SHORT]
    u = z[..., 2 * D_SHORT:]
    y = b_gate * causal_depthwise_conv(c_gate * u, conv_w)
    return jnp.einsum("bte,ed->btd", y, w_out)


def sq_relu_mlp(h, w1, w2):
    a = jax.nn.relu(jnp.einsum("btd,df->btf", h, w1))
    return jnp.einsum("btf,fd->btd", a * a, w2)


def _fwd_setup_inputs(seed: int = 0) -> dict:
    key = jax.random.key(seed)
    ks = jax.random.split(key, 20)
    f32 = jnp.float32

    def nrm(k, shape, scale):
        return jax.random.normal(k, shape, f32) * scale

    def gain(k, shape):
        return 1.0 + 0.02 * jax.random.normal(k, shape, f32)

    return {
        "x": jax.random.normal(ks[0], (BATCH, SEQ, D_MODEL), f32),
        "mix_pre_g": gain(ks[1], (DEPTH, D_MODEL)),
        "mix_post_g": gain(ks[2], (DEPTH, D_MODEL)),
        "ffn_pre_g": gain(ks[3], (DEPTH, D_MODEL)),
        "ffn_post_g": gain(ks[4], (DEPTH, D_MODEL)),
        "ab_w_in": nrm(ks[5], (N_EVEN, D_MODEL, D_AB_IN), D_MODEL ** -0.5),
        "pool_w": nrm(ks[6], (N_EVEN, N_POOL_GROUPS, POOL_GROUP, POOL_GROUP), POOL_GROUP ** -0.5),
        "pool_scale": 1.0 + 0.1 * jax.random.normal(ks[7], (N_EVEN, D_POOL), f32),
        "conv_w": nrm(ks[8], (N_EVEN, CONV_K, D_CONV), CONV_K ** -0.5),
        "conv_b": nrm(ks[9], (N_EVEN, D_CONV), 0.02),
        "conv_ln_g": gain(ks[10], (N_EVEN, D_CONV)),
        "conv_ln_b": nrm(ks[11], (N_EVEN, D_CONV), 0.02),
        "ab_w_out": nrm(ks[12], (N_EVEN, D_POOL + D_CONV, D_MODEL), (D_POOL + D_CONV) ** -0.5),
        "sc_w_in": nrm(ks[13], (N_ODD, D_MODEL, 3 * D_SHORT), D_MODEL ** -0.5),
        "sc_conv_w": nrm(ks[14], (N_ODD, SHORT_K, D_SHORT), SHORT_K ** -0.5),
        "sc_w_out": nrm(ks[15], (N_ODD, D_SHORT, D_MODEL), D_SHORT ** -0.5),
        "ffn_w1": nrm(ks[16], (DEPTH, D_MODEL, D_FF), D_MODEL ** -0.5),
        "ffn_w2": nrm(ks[17], (DEPTH, D_FF, D_MODEL), D_FF ** -0.5),
    }


def _fwd_reference(x, mix_pre_g, mix_post_g, ffn_pre_g, ffn_post_g, ab_w_in, pool_w, pool_scale,
              conv_w, conv_b, conv_ln_g, conv_ln_b, ab_w_out, sc_w_in, sc_conv_w, sc_w_out,
              ffn_w1, ffn_w2):
    for layer in range(DEPTH):
        i = layer // 2
        h = rms_norm(x, mix_pre_g[layer])
        if layer % 2 == 0:
            m = pool_conformer_mixer(h, ab_w_in[i], pool_w[i], pool_scale[i], conv_w[i], conv_b[i],
                                     conv_ln_g[i], conv_ln_b[i], ab_w_out[i])
        else:
            m = short_conv_mixer(h, sc_w_in[i], sc_conv_w[i], sc_w_out[i])
        x = x + rms_norm(m, mix_post_g[layer])
        h = rms_norm(x, ffn_pre_g[layer])
        x = x + rms_norm(sq_relu_mlp(h, ffn_w1[layer], ffn_w2[layer]), ffn_post_g[layer])
    return x


import jax as _jax
import jax.numpy as _jnp

TWIN_FORMAT = 'train_step'
FWD_PARAMS = ['x', 'mix_pre_g', 'mix_post_g', 'ffn_pre_g', 'ffn_post_g', 'ab_w_in', 'pool_w', 'pool_scale', 'conv_w', 'conv_b', 'conv_ln_g', 'conv_ln_b', 'ab_w_out', 'sc_w_in', 'sc_conv_w', 'sc_w_out', 'ffn_w1', 'ffn_w2']
TWIN_WEIGHTS = ['mix_pre_g', 'mix_post_g', 'ffn_pre_g', 'ffn_post_g', 'ab_w_in', 'pool_w', 'pool_scale', 'conv_w', 'conv_b', 'conv_ln_g', 'conv_ln_b', 'ab_w_out', 'sc_w_in', 'sc_conv_w', 'sc_w_out', 'ffn_w1', 'ffn_w2']
TWIN_DIFF_INPUT = 'x'
TWIN_INPUTS = ['x', 'mix_pre_g', 'mix_post_g', 'ffn_pre_g', 'ffn_post_g', 'ab_w_in', 'pool_w', 'pool_scale', 'conv_w', 'conv_b', 'conv_ln_g', 'conv_ln_b', 'ab_w_out', 'sc_w_in', 'sc_conv_w', 'sc_w_out', 'ffn_w1', 'ffn_w2', 'loss_target', 'm_mix_pre_g', 'm_mix_post_g', 'm_ffn_pre_g', 'm_ffn_post_g', 'm_ab_w_in', 'm_pool_w', 'm_pool_scale', 'm_conv_w', 'm_conv_b', 'm_conv_ln_g', 'm_conv_ln_b', 'm_ab_w_out', 'm_sc_w_in', 'm_sc_conv_w', 'm_sc_w_out', 'm_ffn_w1', 'm_ffn_w2', 'v_mix_pre_g', 'v_mix_post_g', 'v_ffn_pre_g', 'v_ffn_post_g', 'v_ab_w_in', 'v_pool_w', 'v_pool_scale', 'v_conv_w', 'v_conv_b', 'v_conv_ln_g', 'v_conv_ln_b', 'v_ab_w_out', 'v_sc_w_in', 'v_sc_conv_w', 'v_sc_w_out', 'v_ffn_w1', 'v_ffn_w2']
TWIN_OUTPUTS = ['loss', 'grad_x', 'grad_mix_pre_g', 'grad_mix_post_g', 'grad_ffn_pre_g', 'grad_ffn_post_g', 'grad_ab_w_in', 'grad_pool_w', 'grad_pool_scale', 'grad_conv_w', 'grad_conv_b', 'grad_conv_ln_g', 'grad_conv_ln_b', 'grad_ab_w_out', 'grad_sc_w_in', 'grad_sc_conv_w', 'grad_sc_w_out', 'grad_ffn_w1', 'grad_ffn_w2', 'delta_mix_pre_g', 'delta_mix_post_g', 'delta_ffn_pre_g', 'delta_ffn_post_g', 'delta_ab_w_in', 'delta_pool_w', 'delta_pool_scale', 'delta_conv_w', 'delta_conv_b', 'delta_conv_ln_g', 'delta_conv_ln_b', 'delta_ab_w_out', 'delta_sc_w_in', 'delta_sc_conv_w', 'delta_sc_w_out', 'delta_ffn_w1', 'delta_ffn_w2', 'new_m_mix_pre_g', 'new_m_mix_post_g', 'new_m_ffn_pre_g', 'new_m_ffn_post_g', 'new_m_ab_w_in', 'new_m_pool_w', 'new_m_pool_scale', 'new_m_conv_w', 'new_m_conv_b', 'new_m_conv_ln_g', 'new_m_conv_ln_b', 'new_m_ab_w_out', 'new_m_sc_w_in', 'new_m_sc_conv_w', 'new_m_sc_w_out', 'new_m_ffn_w1', 'new_m_ffn_w2', 'new_v_mix_pre_g', 'new_v_mix_post_g', 'new_v_ffn_pre_g', 'new_v_ffn_post_g', 'new_v_ab_w_in', 'new_v_pool_w', 'new_v_pool_scale', 'new_v_conv_w', 'new_v_conv_b', 'new_v_conv_ln_g', 'new_v_conv_ln_b', 'new_v_ab_w_out', 'new_v_sc_w_in', 'new_v_sc_conv_w', 'new_v_sc_w_out', 'new_v_ffn_w1', 'new_v_ffn_w2']
TWIN_LEAF_KINDS = {'loss': 'loss', 'grad_x': 'grad_x', 'grad_mix_pre_g': 'grad_w', 'grad_mix_post_g': 'grad_w', 'grad_ffn_pre_g': 'grad_w', 'grad_ffn_post_g': 'grad_w', 'grad_ab_w_in': 'grad_w', 'grad_pool_w': 'grad_w', 'grad_pool_scale': 'grad_w', 'grad_conv_w': 'grad_w', 'grad_conv_b': 'grad_w', 'grad_conv_ln_g': 'grad_w', 'grad_conv_ln_b': 'grad_w', 'grad_ab_w_out': 'grad_w', 'grad_sc_w_in': 'grad_w', 'grad_sc_conv_w': 'grad_w', 'grad_sc_w_out': 'grad_w', 'grad_ffn_w1': 'grad_w', 'grad_ffn_w2': 'grad_w', 'delta_mix_pre_g': 'delta_w', 'delta_mix_post_g': 'delta_w', 'delta_ffn_pre_g': 'delta_w', 'delta_ffn_post_g': 'delta_w', 'delta_ab_w_in': 'delta_w', 'delta_pool_w': 'delta_w', 'delta_pool_scale': 'delta_w', 'delta_conv_w': 'delta_w', 'delta_conv_b': 'delta_w', 'delta_conv_ln_g': 'delta_w', 'delta_conv_ln_b': 'delta_w', 'delta_ab_w_out': 'delta_w', 'delta_sc_w_in': 'delta_w', 'delta_sc_conv_w': 'delta_w', 'delta_sc_w_out': 'delta_w', 'delta_ffn_w1': 'delta_w', 'delta_ffn_w2': 'delta_w', 'new_m_mix_pre_g': 'new_m', 'new_m_mix_post_g': 'new_m', 'new_m_ffn_pre_g': 'new_m', 'new_m_ffn_post_g': 'new_m', 'new_m_ab_w_in': 'new_m', 'new_m_pool_w': 'new_m', 'new_m_pool_scale': 'new_m', 'new_m_conv_w': 'new_m', 'new_m_conv_b': 'new_m', 'new_m_conv_ln_g': 'new_m', 'new_m_conv_ln_b': 'new_m', 'new_m_ab_w_out': 'new_m', 'new_m_sc_w_in': 'new_m', 'new_m_sc_conv_w': 'new_m', 'new_m_sc_w_out': 'new_m', 'new_m_ffn_w1': 'new_m', 'new_m_ffn_w2': 'new_m', 'new_v_mix_pre_g': 'new_v', 'new_v_mix_post_g': 'new_v', 'new_v_ffn_pre_g': 'new_v', 'new_v_ffn_post_g': 'new_v', 'new_v_ab_w_in': 'new_v', 'new_v_pool_w': 'new_v', 'new_v_pool_scale': 'new_v', 'new_v_conv_w': 'new_v', 'new_v_conv_b': 'new_v', 'new_v_conv_ln_g': 'new_v', 'new_v_conv_ln_b': 'new_v', 'new_v_ab_w_out': 'new_v', 'new_v_sc_w_in': 'new_v', 'new_v_sc_conv_w': 'new_v', 'new_v_sc_w_out': 'new_v', 'new_v_ffn_w1': 'new_v', 'new_v_ffn_w2': 'new_v'}


def _forward(args):
    return _fwd_reference(*[args[k] for k in FWD_PARAMS])


def _output_shape():
    out = _jax.eval_shape(lambda: _forward(_fwd_setup_inputs(0)))
    return out.shape, out.dtype

N_MICROBATCH = 1
ADAM_LR = 0.001
ADAM_B1 = 0.9
ADAM_B2 = 0.999
ADAM_EPS = 1e-08
ADAM_WD = 0.01
ADAM_STEP = 10
PER_EXAMPLE_BATCH_AXIS = {'x': 0, 'loss_target': 0}
SHARED_INPUTS = []
_WEIGHT_DTYPES = {'mix_pre_g': _jnp.float32, 'mix_post_g': _jnp.float32, 'ffn_pre_g': _jnp.float32, 'ffn_post_g': _jnp.float32, 'ab_w_in': _jnp.float32, 'pool_w': _jnp.float32, 'pool_scale': _jnp.float32, 'conv_w': _jnp.float32, 'conv_b': _jnp.float32, 'conv_ln_g': _jnp.float32, 'conv_ln_b': _jnp.float32, 'ab_w_out': _jnp.float32, 'sc_w_in': _jnp.float32, 'sc_conv_w': _jnp.float32, 'sc_w_out': _jnp.float32, 'ffn_w1': _jnp.float32, 'ffn_w2': _jnp.float32}
MOMENT_SCALE = {'mix_pre_g': 3.319250e-01, 'mix_post_g': 7.999951e+00, 'ffn_pre_g': 5.250372e-01, 'ffn_post_g': 8.484307e+00, 'ab_w_in': 2.762175e-01, 'pool_w': 4.272522e-01, 'pool_scale': 4.873585e-01, 'conv_w': 4.130771e-01, 'conv_b': 6.377944e+00, 'conv_ln_g': 2.482224e+00, 'conv_ln_b': 3.633355e+00, 'ab_w_out': 9.796965e-01, 'sc_w_in': 1.825395e-01, 'sc_conv_w': 1.908576e-01, 'sc_w_out': 1.999144e-01, 'ffn_w1': 2.526161e-01, 'ffn_w2': 1.608655e+00}


def _to_microbatches(a, axis):
    t = _jnp.moveaxis(a, axis, 0)
    t = t.reshape((N_MICROBATCH, t.shape[0] // N_MICROBATCH) + t.shape[1:])
    return _jnp.moveaxis(t, 1, axis + 1)


def setup_inputs(seed: int = 0) -> dict:
    inp = _fwd_setup_inputs(seed)
    key = _jax.random.fold_in(_jax.random.key(seed), 7919)
    shape, _ = _output_shape()
    out = dict(inp)
    out["loss_target"] = _jax.random.normal(_jax.random.fold_in(key, 0), shape, _jnp.float32)
    for i, name in enumerate(TWIN_WEIGHTS):
        w = inp[name].astype(_jnp.float32)
        if MOMENT_SCALE is None:
            s = _jnp.sqrt(_jnp.mean(_jnp.square(w)) + 1e-30)
        else:
            s = MOMENT_SCALE[name]
        km, kv = _jax.random.split(_jax.random.fold_in(key, i + 1))
        out[name] = w
        out["m_" + name] = s * _jax.random.normal(km, w.shape, _jnp.float32)
        out["v_" + name] = (s * s) * _jax.random.uniform(kv, w.shape, _jnp.float32, 0.5, 1.5)
    if N_MICROBATCH > 1:
        for name, axis in PER_EXAMPLE_BATCH_AXIS.items():
            out[name] = _to_microbatches(out[name], axis)
    return {'x': out['x'], 'mix_pre_g': out['mix_pre_g'], 'mix_post_g': out['mix_post_g'], 'ffn_pre_g': out['ffn_pre_g'], 'ffn_post_g': out['ffn_post_g'], 'ab_w_in': out['ab_w_in'], 'pool_w': out['pool_w'], 'pool_scale': out['pool_scale'], 'conv_w': out['conv_w'], 'conv_b': out['conv_b'], 'conv_ln_g': out['conv_ln_g'], 'conv_ln_b': out['conv_ln_b'], 'ab_w_out': out['ab_w_out'], 'sc_w_in': out['sc_w_in'], 'sc_conv_w': out['sc_conv_w'], 'sc_w_out': out['sc_w_out'], 'ffn_w1': out['ffn_w1'], 'ffn_w2': out['ffn_w2'], 'loss_target': out['loss_target'], 'm_mix_pre_g': out['m_mix_pre_g'], 'm_mix_post_g': out['m_mix_post_g'], 'm_ffn_pre_g': out['m_ffn_pre_g'], 'm_ffn_post_g': out['m_ffn_post_g'], 'm_ab_w_in': out['m_ab_w_in'], 'm_pool_w': out['m_pool_w'], 'm_pool_scale': out['m_pool_scale'], 'm_conv_w': out['m_conv_w'], 'm_conv_b': out['m_conv_b'], 'm_conv_ln_g': out['m_conv_ln_g'], 'm_conv_ln_b': out['m_conv_ln_b'], 'm_ab_w_out': out['m_ab_w_out'], 'm_sc_w_in': out['m_sc_w_in'], 'm_sc_conv_w': out['m_sc_conv_w'], 'm_sc_w_out': out['m_sc_w_out'], 'm_ffn_w1': out['m_ffn_w1'], 'm_ffn_w2': out['m_ffn_w2'], 'v_mix_pre_g': out['v_mix_pre_g'], 'v_mix_post_g': out['v_mix_post_g'], 'v_ffn_pre_g': out['v_ffn_pre_g'], 'v_ffn_post_g': out['v_ffn_post_g'], 'v_ab_w_in': out['v_ab_w_in'], 'v_pool_w': out['v_pool_w'], 'v_pool_scale': out['v_pool_scale'], 'v_conv_w': out['v_conv_w'], 'v_conv_b': out['v_conv_b'], 'v_conv_ln_g': out['v_conv_ln_g'], 'v_conv_ln_b': out['v_conv_ln_b'], 'v_ab_w_out': out['v_ab_w_out'], 'v_sc_w_in': out['v_sc_w_in'], 'v_sc_conv_w': out['v_sc_conv_w'], 'v_sc_w_out': out['v_sc_w_out'], 'v_ffn_w1': out['v_ffn_w1'], 'v_ffn_w2': out['v_ffn_w2']}


def _loss(weights, diff, rest, loss_target):
    with _jax.named_scope("forward"):
        args = {**rest, TWIN_DIFF_INPUT: diff, **{k: w.astype(_WEIGHT_DTYPES[k]) for k, w in weights.items()}}
        y = _forward(args)
    with _jax.named_scope("loss_head"):
        err = _jnp.square(y.astype(_jnp.float32) - loss_target)
        return 0.5 * _jnp.sum(_jnp.mean(err, axis=-1)) if err.ndim else 0.5 * err


def _adamw(w, g, m, v):
    m = ADAM_B1 * m + (1.0 - ADAM_B1) * g
    v = ADAM_B2 * v + (1.0 - ADAM_B2) * _jnp.square(g)
    m_hat = m / (1.0 - ADAM_B1 ** ADAM_STEP)
    v_hat = v / (1.0 - ADAM_B2 ** ADAM_STEP)
    delta = -ADAM_LR * (m_hat / (_jnp.sqrt(v_hat) + ADAM_EPS) + ADAM_WD * w)
    return delta, m, v


def reference(x, mix_pre_g, mix_post_g, ffn_pre_g, ffn_post_g, ab_w_in, pool_w, pool_scale, conv_w, conv_b, conv_ln_g, conv_ln_b, ab_w_out, sc_w_in, sc_conv_w, sc_w_out, ffn_w1, ffn_w2, loss_target, m_mix_pre_g, m_mix_post_g, m_ffn_pre_g, m_ffn_post_g, m_ab_w_in, m_pool_w, m_pool_scale, m_conv_w, m_conv_b, m_conv_ln_g, m_conv_ln_b, m_ab_w_out, m_sc_w_in, m_sc_conv_w, m_sc_w_out, m_ffn_w1, m_ffn_w2, v_mix_pre_g, v_mix_post_g, v_ffn_pre_g, v_ffn_post_g, v_ab_w_in, v_pool_w, v_pool_scale, v_conv_w, v_conv_b, v_conv_ln_g, v_conv_ln_b, v_ab_w_out, v_sc_w_in, v_sc_conv_w, v_sc_w_out, v_ffn_w1, v_ffn_w2):
    given = dict(x=x, mix_pre_g=mix_pre_g, mix_post_g=mix_post_g, ffn_pre_g=ffn_pre_g, ffn_post_g=ffn_post_g, ab_w_in=ab_w_in, pool_w=pool_w, pool_scale=pool_scale, conv_w=conv_w, conv_b=conv_b, conv_ln_g=conv_ln_g, conv_ln_b=conv_ln_b, ab_w_out=ab_w_out, sc_w_in=sc_w_in, sc_conv_w=sc_conv_w, sc_w_out=sc_w_out, ffn_w1=ffn_w1, ffn_w2=ffn_w2, loss_target=loss_target, m_mix_pre_g=m_mix_pre_g, m_mix_post_g=m_mix_post_g, m_ffn_pre_g=m_ffn_pre_g, m_ffn_post_g=m_ffn_post_g, m_ab_w_in=m_ab_w_in, m_pool_w=m_pool_w, m_pool_scale=m_pool_scale, m_conv_w=m_conv_w, m_conv_b=m_conv_b, m_conv_ln_g=m_conv_ln_g, m_conv_ln_b=m_conv_ln_b, m_ab_w_out=m_ab_w_out, m_sc_w_in=m_sc_w_in, m_sc_conv_w=m_sc_conv_w, m_sc_w_out=m_sc_w_out, m_ffn_w1=m_ffn_w1, m_ffn_w2=m_ffn_w2, v_mix_pre_g=v_mix_pre_g, v_mix_post_g=v_mix_post_g, v_ffn_pre_g=v_ffn_pre_g, v_ffn_post_g=v_ffn_post_g, v_ab_w_in=v_ab_w_in, v_pool_w=v_pool_w, v_pool_scale=v_pool_scale, v_conv_w=v_conv_w, v_conv_b=v_conv_b, v_conv_ln_g=v_conv_ln_g, v_conv_ln_b=v_conv_ln_b, v_ab_w_out=v_ab_w_out, v_sc_w_in=v_sc_w_in, v_sc_conv_w=v_sc_conv_w, v_sc_w_out=v_sc_w_out, v_ffn_w1=v_ffn_w1, v_ffn_w2=v_ffn_w2)
    weights = {n: given[n] for n in TWIN_WEIGHTS}
    shared = {n: given[n] for n in SHARED_INPUTS}
    per_example = {n: given[n] for n in ['x']}
    grad_fn = _jax.value_and_grad(_loss, argnums=(0, 1))

    def one_microbatch(ex, loss_target):
        ex = dict(ex)
        diff = ex.pop(TWIN_DIFF_INPUT)
        return grad_fn(weights, diff, {**shared, **ex}, loss_target)

    if N_MICROBATCH == 1:
        loss, (grad_w, grad_x) = one_microbatch(per_example, given["loss_target"])
    else:
        def body(carry, xs):
            loss_sum, grad_sum = carry
            l_k, (gw_k, gx_k) = one_microbatch(xs[0], xs[1])
            with _jax.named_scope("update"):
                return (loss_sum + l_k, _jax.tree.map(_jnp.add, grad_sum, gw_k)), gx_k

        init = (_jnp.zeros((), _jnp.float32), _jax.tree.map(_jnp.zeros_like, weights))
        (loss, grad_w), grad_x = _jax.lax.scan(body, init, (per_example, given["loss_target"]))
    with _jax.named_scope("update"):
        delta_w, new_m, new_v = {}, {}, {}
        for n in TWIN_WEIGHTS:
            delta_w[n], new_m[n], new_v[n] = _adamw(weights[n], grad_w[n], given["m_" + n], given["v_" + n])
    return (loss, grad_x, *[grad_w[n] for n in TWIN_WEIGHTS], *[delta_w[n] for n in TWIN_WEIGHTS],
            *[new_m[n] for n in TWIN_WEIGHTS], *[new_v[n] for n in TWIN_WEIGHTS])
```

```python
import functools

import jax
import jax.numpy as jnp
from jax import lax
from jax.experimental import pallas as pl
from jax.experimental.pallas import tpu as pltpu

F32 = jnp.float32
BF16 = jnp.bfloat16
MESH = pl.DeviceIdType.MESH
ANY = pl.BlockSpec(memory_space=pl.ANY)

NORM_EPS = 1e-6
POOL_WINDOWS = (2, 4, 8, 16)
MAX_POOL_WINDOW = 16
ADAM_LR = 0.001
ADAM_B1 = 0.9
ADAM_B2 = 0.999
ADAM_EPS = 1e-08
ADAM_WD = 0.01
ADAM_STEP = 10

N_DEV = 8
VMEM_LIMIT = 48 * 1024 * 1024
ROW_TILE = 256
CHANNEL_TILE = 256
TIME_CHUNK = 64
HALO = 32

NN = (((1,), (0,)), ((), ()))
NT = (((1,), (1,)), ((), ()))
TN = (((0,), (0,)), ((), ()))


def _params(sem):
    return pltpu.CompilerParams(dimension_semantics=sem, vmem_limit_bytes=VMEM_LIMIT)


def _place():
    x, y, c = lax.axis_index("x"), lax.axis_index("y"), lax.axis_index("c")
    return x, y, c


def _slot(px, py, pc):
    return 4 * px + 2 * py + pc


def all_gather(arrs, name):
    n = len(arrs)

    def body(*refs):
        ins, outs = refs[:n], refs[n:2 * n]
        send_sems, recv_sems, local_sems = refs[2 * n:]
        x, y, c = _place()
        me = _slot(x, y, c)
        sibling = (x, y, 1 - c)
        chips = [(1 - x, y), (x, 1 - y), (1 - x, 1 - y)]

        def copy(a, k, slot, to, src=None):
            return pltpu.make_async_remote_copy(
                src_ref=outs[a].at[slot] if src is None else src, dst_ref=outs[a].at[slot],
                send_sem=send_sems.at[a, k], recv_sem=recv_sems.at[a, k], device_id=to, device_id_type=MESH)

        started = []
        for a in range(n):
            mine = pltpu.make_async_copy(ins[a], outs[a].at[me], local_sems.at[a])
            mine.start()
            started.append(mine)
        sends = []
        for a in range(n):
            first = [copy(a, 0, me, sibling, src=ins[a])]
            first += [copy(a, 1 + j, me, (px, py, c), src=ins[a]) for j, (px, py) in enumerate(chips)]
            for cp in first:
                cp.start()
            sends += first
        for a in range(n):
            for j, (px, py) in enumerate(chips):
                copy(a, 1 + j, _slot(px, py, c), (x, y, c)).wait_recv()
                fwd = copy(a, 4 + j, _slot(px, py, c), sibling)
                fwd.start()
                sends.append(fwd)
        for a in range(n):
            copy(a, 0, _slot(x, y, 1 - c), (x, y, c)).wait_recv()
            for j, (px, py) in enumerate(chips):
                copy(a, 4 + j, _slot(px, py, 1 - c), (x, y, c)).wait_recv()
        for cp in sends:
            cp.wait_send()
        for mine in started:
            mine.wait()

    outs = pl.pallas_call(
        body, name=name,
        out_shape=[jax.ShapeDtypeStruct((N_DEV,) + a.shape, a.dtype) for a in arrs],
        in_specs=[ANY] * n, out_specs=[ANY] * n,
        scratch_shapes=[pltpu.SemaphoreType.DMA((n, 7)), pltpu.SemaphoreType.DMA((n, 7)),
                        pltpu.SemaphoreType.DMA((n,))],
    )(*arrs)
    return list(outs)


def exchange(groups, name):
    arrs = [a for grp in groups for a in grp]
    n = len(arrs)
    where = []
    for o, grp in enumerate(groups):
        off = 0
        for a in grp:
            where.append((o, off))
            off += a.shape[1]
    flips = [(dx, dy, dc) for dx in (0, 1) for dy in (0, 1) for dc in (0, 1)][1:]

    def body(*refs):
        ins, outs = refs[:n], refs[n:n + len(groups)]
        send_sems, recv_sems, local_sems = refs[n + len(groups):]
        x, y, c = _place()
        me = _slot(x, y, c)

        def landing(a, slot):
            o, off = where[a]
            return outs[o].at[slot, pl.ds(off, arrs[a].shape[1])]

        local = []
        for a in range(n):
            cp = pltpu.make_async_copy(ins[a].at[me], landing(a, me), local_sems.at[a])
            cp.start()
            local.append(cp)
        copies = []
        for a in range(n):
            for k, (dx, dy, dc) in enumerate(flips):
                px = 1 - x if dx else x
                py = 1 - y if dy else y
                pc = 1 - c if dc else c
                cp = pltpu.make_async_remote_copy(
                    src_ref=ins[a].at[_slot(px, py, pc)], dst_ref=landing(a, me),
                    send_sem=send_sems.at[a, k], recv_sem=recv_sems.at[a, k],
                    device_id=(px, py, pc), device_id_type=MESH)
                cp.start()
                copies.append(cp)
        for cp in copies:
            cp.wait()
        for cp in local:
            cp.wait()

    outs = pl.pallas_call(
        body, name=name,
        out_shape=[jax.ShapeDtypeStruct((N_DEV, sum(a.shape[1] for a in grp), grp[0].shape[2]), grp[0].dtype)
                   for grp in groups],
        in_specs=[ANY] * n, out_specs=[ANY] * len(groups),
        scratch_shapes=[pltpu.SemaphoreType.DMA((n, 7)), pltpu.SemaphoreType.DMA((n, 7)),
                        pltpu.SemaphoreType.DMA((n,))],
    )(*arrs)
    return list(outs)


def _matmul(name, lhs, rhs, *, out_shape, out_dtype, grid, lhs_spec, rhs_spec, out_spec, dims, acc_shape,
            lhs_fn=None, extra=(), extra_specs=(), epilogue=None):
    nk = grid[2]
    n_extra = len(extra)

    def body(*refs):
        lhs_ref, rhs_ref = refs[0], refs[1]
        extra_refs = refs[2:2 + n_extra]
        out_ref = refs[2 + n_extra]
        a = lhs_ref[...]
        if lhs_fn is not None:
            a = lhs_fn(a)
        p = lax.dot_general(a, rhs_ref[...], dims, preferred_element_type=F32)

        def finish(r):
            if epilogue is not None:
                r = epilogue(r, *[e[...] for e in extra_refs])
            out_ref[...] = r.astype(out_dtype)

        if nk == 1:
            finish(p)
        else:
            acc_ref = refs[3 + n_extra]
            k = pl.program_id(2)

            @pl.when(k == 0)
            def _():
                acc_ref[...] = p

            @pl.when(k > 0)
            def _():
                acc_ref[...] += p

            @pl.when(k == nk - 1)
            def _():
                finish(acc_ref[...])

    return pl.pallas_call(
        body, name=name, grid=grid,
        out_shape=jax.ShapeDtypeStruct(out_shape, out_dtype),
        in_specs=[lhs_spec, rhs_spec, *extra_specs], out_specs=out_spec,
        scratch_shapes=[pltpu.VMEM(acc_shape, F32)] if nk > 1 else [],
        compiler_params=_params(("parallel", "parallel", "arbitrary")),
    )(lhs, rhs, *extra)


def _tile(n, want):
    return want if n % want == 0 else n


def mm_nn(name, x, w, *, out_dtype, tn=512, tk=None, lhs_fn=None, epilogue=None):
    t, kdim = x.shape
    n = w.shape[1]
    tn = _tile(n, tn)
    tk = kdim if tk is None else _tile(kdim, tk)
    return _matmul(
        name, x, w, out_shape=(t, n), out_dtype=out_dtype, grid=(1, n // tn, kdim // tk),
        lhs_spec=pl.BlockSpec((t, tk), lambda i, j, k: (i, k)),
        rhs_spec=pl.BlockSpec((tk, tn), lambda i, j, k: (k, j)),
        out_spec=pl.BlockSpec((t, tn), lambda i, j, k: (i, j)),
        dims=NN, acc_shape=(t, tn), lhs_fn=lhs_fn, epilogue=epilogue)


def mm_nn_blocked(name, x, w, *, out_dtype, epilogue=None):
    t, kdim = x.shape
    nb = w.shape[2]
    tn = nb // 2 if nb >= 1024 else nb
    sub = nb // tn
    return _matmul(
        name, x, w, out_shape=(t, N_DEV * nb), out_dtype=out_dtype, grid=(1, N_DEV * sub, 1),
        lhs_spec=pl.BlockSpec((t, kdim), lambda i, j, k: (i, k)),
        rhs_spec=pl.BlockSpec((None, kdim, tn), lambda i, j, k: (j // sub, k, j % sub)),
        out_spec=pl.BlockSpec((t, tn), lambda i, j, k: (i, j)),
        dims=NN, acc_shape=(t, tn), epilogue=epilogue)


def mm_nt(name, dy, w, *, out_dtype, tn=512, extra=None, epilogue=None):
    t, n = dy.shape
    kdim = w.shape[0]
    tn = _tile(kdim, tn)
    extra_arrs = () if extra is None else (extra,)
    extra_specs = () if extra is None else (pl.BlockSpec((t, tn), lambda i, j, k: (i, j)),)
    return _matmul(
        name, dy, w, out_shape=(t, kdim), out_dtype=out_dtype, grid=(1, kdim // tn, 1),
        lhs_spec=pl.BlockSpec((t, n), lambda i, j, k: (i, k)),
        rhs_spec=pl.BlockSpec((tn, n), lambda i, j, k: (j, k)),
        out_spec=pl.BlockSpec((t, tn), lambda i, j, k: (i, j)),
        dims=NT, acc_shape=(t, tn), extra=extra_arrs, extra_specs=extra_specs, epilogue=epilogue)


def mm_nt_blocked(name, dz, w, *, out_dtype, tn=512):
    t = dz.shape[0]
    kdim, nb = w.shape[1], w.shape[2]
    tn = _tile(kdim, tn)
    return _matmul(
        name, dz, w, out_shape=(t, kdim), out_dtype=out_dtype, grid=(1, kdim // tn, N_DEV),
        lhs_spec=pl.BlockSpec((t, nb), lambda i, j, k: (i, k)),
        rhs_spec=pl.BlockSpec((None, tn, nb), lambda i, j, k: (k, j, 0)),
        out_spec=pl.BlockSpec((t, tn), lambda i, j, k: (i, j)),
        dims=NT, acc_shape=(t, tn))


def mm_tn(name, x, dy, *, out_dtype, tk=1024, tn=1024, lhs_fn=None):
    t, kdim = x.shape
    n = dy.shape[1]
    tk, tn = _tile(kdim, tk), _tile(n, tn)
    return _matmul(
        name, x, dy, out_shape=(kdim, n), out_dtype=out_dtype, grid=(kdim // tk, n // tn, 1),
        lhs_spec=pl.BlockSpec((t, tk), lambda i, j, k: (k, i)),
        rhs_spec=pl.BlockSpec((t, tn), lambda i, j, k: (k, j)),
        out_spec=pl.BlockSpec((tk, tn), lambda i, j, k: (i, j)),
        dims=TN, acc_shape=(tk, tn), lhs_fn=lhs_fn)


def mm_tn_blocked(name, x, dz, nb, *, out_dtype, tk=1024):
    t, kdim = x.shape
    tk = _tile(kdim, tk)
    return _matmul(
        name, x, dz, out_shape=(N_DEV, kdim, nb), out_dtype=out_dtype, grid=(kdim // tk, N_DEV, 1),
        lhs_spec=pl.BlockSpec((t, tk), lambda i, j, k: (k, i)),
        rhs_spec=pl.BlockSpec((t, nb), lambda i, j, k: (k, j)),
        out_spec=pl.BlockSpec((None, tk, nb), lambda i, j, k: (j, i, 0)),
        dims=TN, acc_shape=(tk, nb))


def _rstd(v):
    return lax.rsqrt(jnp.mean(v * v, axis=-1, keepdims=True) + NORM_EPS)


def _rms_bwd(v, g, dy):
    r = _rstd(v)
    vhat = v * r
    dvh = dy * g
    dv = r * (dvh - vhat * jnp.mean(dvh * vhat, axis=-1, keepdims=True))
    return dv, dy * vhat


def _fold8(v):
    rows, n = v.shape
    return jnp.sum(v.reshape(rows // 8, 8, n), axis=0)


def _fold_lanes(v):
    out = v[:, 0:128]
    for i in range(1, v.shape[1] // 128):
        out = out + v[:, 128 * i:128 * (i + 1)]
    return out


def _accumulate(ref, v):
    i = pl.program_id(0)

    @pl.when(i == 0)
    def _():
        ref[...] = v

    @pl.when(i > 0)
    def _():
        ref[...] += v


def _row_call(body, name, t, ins, row_in, outs, acc_outs=(), tr=ROW_TILE):
    tr = _tile(t, tr)
    in_specs = [pl.BlockSpec((tr, a.shape[1]), lambda i: (i, 0)) if tiled
                else pl.BlockSpec(a.shape, lambda i: (0, 0)) for a, tiled in zip(ins, row_in)]
    out_specs = [pl.BlockSpec((tr, n), lambda i: (i, 0)) for n, _ in outs]
    out_specs += [pl.BlockSpec((8, n), lambda i: (0, 0)) for n in acc_outs]
    out_shape = [jax.ShapeDtypeStruct((t, n), dt) for n, dt in outs]
    out_shape += [jax.ShapeDtypeStruct((8, n), F32) for n in acc_outs]
    return pl.pallas_call(
        body, name=name, grid=(t // tr,), in_specs=in_specs, out_specs=out_specs, out_shape=out_shape,
        compiler_params=_params(("arbitrary",) if acc_outs else ("parallel",)),
    )(*ins)


def norm_pre(name, x, g):
    t, d = x.shape

    def body(x_ref, g_ref, h_ref):
        v = x_ref[...]
        h_ref[...] = (v * _rstd(v) * g_ref[...]).astype(BF16)

    return _row_call(body, name, t, [x, g], [True, False], [(d, BF16)])[0]


def post_pre(name, x, m, g_post, g_pre):
    t, d = x.shape

    def body(x_ref, m_ref, gp_ref, gn_ref, xo_ref, h_ref):
        mv = m_ref[...]
        xn = x_ref[...] + mv * _rstd(mv) * gp_ref[...]
        xo_ref[...] = xn
        h_ref[...] = (xn * _rstd(xn) * gn_ref[...]).astype(BF16)

    return _row_call(body, name, t, [x, m, g_post, g_pre], [True, True, False, False], [(d, F32), (d, BF16)])


def post_loss(name, x, f, g_post, target):
    t, d = x.shape

    def body(x_ref, f_ref, g_ref, t_ref, dx_ref, df_ref, loss_ref, dg_ref):
        fv = f_ref[...]
        g = g_ref[...]
        out = x_ref[...] + fv * _rstd(fv) * g
        err = out - t_ref[...]
        dx = err * (1.0 / d)
        dx_ref[...] = dx
        dfv, dg_rows = _rms_bwd(fv, g, dx)
        df_ref[...] = dfv.astype(BF16)
        _accumulate(loss_ref, _fold8(_fold_lanes(err * err)))
        _accumulate(dg_ref, _fold8(dg_rows))

    return _row_call(body, name, t, [x, f, g_post, target], [True, True, False, True],
                     [(d, F32), (d, BF16)], acc_outs=(128, d))


def bwd_pre_post(name, dx_out, x_in, g_pre, dh, f_prev, g_post_prev):
    t, d = x_in.shape

    def body(dxo_ref, x_ref, gpre_ref, dh_ref, f_ref, gpost_ref, dxi_ref, df_ref, dgpre_ref, dgpost_ref):
        dxv, dgpre_rows = _rms_bwd(x_ref[...], gpre_ref[...], dh_ref[...].astype(F32))
        dxi = dxo_ref[...] + dxv
        dxi_ref[...] = dxi
        dfv, dgpost_rows = _rms_bwd(f_ref[...], gpost_ref[...], dxi)
        df_ref[...] = dfv.astype(BF16)
        _accumulate(dgpre_ref, _fold8(dgpre_rows))
        _accumulate(dgpost_ref, _fold8(dgpost_rows))

    return _row_call(body, name, t, [dx_out, x_in, g_pre, dh, f_prev, g_post_prev],
                     [True, True, False, True, True, False], [(d, F32), (d, BF16)], acc_outs=(d, d))


def bwd_pre_final(name, dx_out, x_in, g_pre, dh):
    t, d = x_in.shape

    def body(dxo_ref, x_ref, gpre_ref, dh_ref, dxi_ref, dgpre_ref):
        dxv, dgpre_rows = _rms_bwd(x_ref[...], gpre_ref[...], dh_ref[...].astype(F32))
        dxi_ref[...] = dxo_ref[...] + dxv
        _accumulate(dgpre_ref, _fold8(dgpre_rows))

    return _row_call(body, name, t, [dx_out, x_in, g_pre, dh], [True, True, False, True], [(d, F32)], acc_outs=(d,))


def _layer_norm_parts(cv):
    mu = jnp.mean(cv, axis=-1, keepdims=True)
    xc = cv - mu
    rstd = lax.rsqrt(jnp.mean(xc * xc, axis=-1, keepdims=True) + NORM_EPS)
    return xc * rstd, rstd


def ln_silu(name, cv, g, b):
    t, n = cv.shape

    def body(c_ref, g_ref, b_ref, y_ref):
        chat, _ = _layer_norm_parts(c_ref[...])
        ln = chat * g_ref[...] + b_ref[...]
        y_ref[...] = (ln * jax.nn.sigmoid(ln)).astype(BF16)

    return _row_call(body, name, t, [cv, g, b], [True, False, False], [(n, BF16)])[0]


def ln_silu_bwd(name, cv, g, b, dy):
    t, n = cv.shape

    def body(c_ref, g_ref, b_ref, dy_ref, dc_ref, dg_ref, db_ref):
        chat, rstd = _layer_norm_parts(c_ref[...])
        g = g_ref[...]
        ln = chat * g + b_ref[...]
        s = jax.nn.sigmoid(ln)
        dln = dy_ref[...].astype(F32) * (s * (1.0 + ln * (1.0 - s)))
        dchat = dln * g
        dc_ref[...] = rstd * (dchat - jnp.mean(dchat, axis=-1, keepdims=True)
                              - chat * jnp.mean(dchat * chat, axis=-1, keepdims=True))
        _accumulate(dg_ref, _fold8(dln * chat))
        _accumulate(db_ref, _fold8(dln))

    return _row_call(body, name, t, [cv, g, b, dy], [True, False, False, True], [(n, F32)], acc_outs=(n, n))


def _chunks(t, fn, tc=TIME_CHUNK):
    tc = _tile(t, tc)

    def step(i, carry):
        fn(pl.multiple_of(i * tc, tc), tc)
        return carry

    lax.fori_loop(0, t // tc, step, 0)


def _taps(window, w_ref, offsets, tc):
    acc = None
    for k, off in enumerate(offsets):
        term = w_ref[k:k + 1, :] * window[off:off + tc, :]
        acc = term if acc is None else acc + term
    return acc


def _window_sums(win, tc, causal):
    sums = []
    cur, rows, step = win, tc + HALO, 1
    for _ in POOL_WINDOWS:
        rows -= 8
        if causal:
            cur = cur[8:8 + rows, :] + cur[8 - step:8 - step + rows, :]
            sums.append(cur[rows - tc:rows, :])
        else:
            cur = cur[0:rows, :] + cur[step:step + rows, :]
            sums.append(cur[0:tc, :])
        step *= 2
    return sums


def _pick(vals, g):
    out = vals[-1]
    for i in range(len(vals) - 2, -1, -1):
        out = jnp.where(g == i, vals[i], out)
    return out


def _pool_count(s, tc, g):
    t1 = (lax.broadcasted_iota(jnp.int32, (tc, 1), 0) + (s + 1)).astype(F32)
    width = _pick([float(w) for w in POOL_WINDOWS], g)
    return jnp.minimum(t1, width)


def pool_fwd(name, z, pool_w, pool_scale, d_pool):
    t = z.shape[0]
    ng, pg = pool_w.shape[0], pool_w.shape[1]

    def body(u_ref, w_ref, s_ref, pooled_ref, y_ref, pad):
        g = pl.program_id(0)
        pad[pl.ds(0, HALO), :] = jnp.zeros((HALO, pg), F32)

        def fill(s, tc):
            pad[pl.ds(HALO + s, tc), :] = u_ref[pl.ds(s, tc), :].astype(F32)

        def chunk(s, tc):
            win = pad[pl.ds(s, tc + HALO), :]
            total = _pick(_window_sums(win, tc, causal=True), g)
            pooled = total / _pool_count(s, tc, g) - win[HALO:HALO + tc, :]
            pooled_ref[pl.ds(s, tc), :] = pooled.astype(BF16)

        _chunks(t, fill)
        _chunks(t, chunk)
        mixed = jnp.dot(pooled_ref[...], w_ref[...], preferred_element_type=F32)
        y_ref[...] = (mixed * s_ref[...]).astype(BF16)

    col = pl.BlockSpec((t, pg), lambda g: (0, g))
    return pl.pallas_call(
        body, name=name, grid=(ng,),
        in_specs=[col, pl.BlockSpec((None, pg, pg), lambda g: (g, 0, 0)), pl.BlockSpec((1, pg), lambda g: (0, g))],
        out_specs=[col, col],
        out_shape=[jax.ShapeDtypeStruct((t, d_pool), BF16), jax.ShapeDtypeStruct((t, d_pool), BF16)],
        scratch_shapes=[pltpu.VMEM((t + HALO, pg), F32)],
        compiler_params=_params(("parallel",)),
    )(z, pool_w, pool_scale)


def pool_bwd(name, pooled, dy, pool_w, pool_scale):
    t, d_pool = pooled.shape
    ng, pg = pool_w.shape[0], pool_w.shape[1]

    def body(p_ref, dy_ref, w_ref, s_ref, du_ref, dw_ref, ds_ref, pad):
        g = pl.program_id(0)
        w = w_ref[...]
        dyv = dy_ref[...].astype(F32)
        mixed = jnp.dot(p_ref[...], w, preferred_element_type=F32)
        ds_ref[...] = jnp.sum(dyv * mixed, axis=0, keepdims=True)
        dmixed = (dyv * s_ref[...]).astype(BF16)
        dw_ref[...] = lax.dot_general(p_ref[...], dmixed, TN, preferred_element_type=F32)
        pad[...] = jnp.zeros((t + HALO, pg), F32)
        pad[pl.ds(0, t), :] = lax.dot_general(dmixed, w, NT, preferred_element_type=F32)

        def scale(s, tc):
            pad[pl.ds(s, tc), :] = pad[pl.ds(s, tc), :] / _pool_count(s, tc, g)

        def chunk(s, tc):
            win = pad[pl.ds(s, tc + HALO), :]
            total = _pick(_window_sums(win, tc, causal=False), g)
            du_ref[pl.ds(s, tc), :] = (total - win[0:tc, :] * _pool_count(s, tc, g)).astype(BF16)

        _chunks(t, scale)
        _chunks(t, chunk)

    col = pl.BlockSpec((t, pg), lambda g: (0, g))
    vec = pl.BlockSpec((1, pg), lambda g: (0, g))
    mat = pl.BlockSpec((None, pg, pg), lambda g: (g, 0, 0))
    return pl.pallas_call(
        body, name=name, grid=(ng,),
        in_specs=[col, col, mat, vec], out_specs=[col, mat, vec],
        out_shape=[jax.ShapeDtypeStruct((t, d_pool), BF16), jax.ShapeDtypeStruct((ng, pg, pg), F32),
                   jax.ShapeDtypeStruct((1, d_pool), F32)],
        scratch_shapes=[pltpu.VMEM((t + HALO, pg), F32)],
        compiler_params=_params(("parallel",)),
    )(pooled, dy, pool_w, pool_scale)


def conv_fwd(name, z, conv_w, conv_b, d_pool, d_conv):
    t = z.shape[0]
    kw = conv_w.shape[0]
    tc_ch = _tile(d_conv, CHANNEL_TILE)
    v0, g0 = d_pool // tc_ch, (d_pool + d_conv) // tc_ch

    def body(v_ref, g_ref, w_ref, b_ref, c_ref, pad):
        pad[pl.ds(0, HALO), :] = jnp.zeros((HALO, tc_ch), F32)

        def fill(s, tc):
            pad[pl.ds(HALO + s, tc), :] = v_ref[pl.ds(s, tc), :].astype(F32) * jax.nn.sigmoid(g_ref[pl.ds(s, tc), :].astype(F32))

        def chunk(s, tc):
            win = pad[pl.ds(s, tc + HALO), :]
            c_ref[pl.ds(s, tc), :] = _taps(win, w_ref, [HALO - (kw - 1) + k for k in range(kw)], tc) + b_ref[...]

        _chunks(t, fill)
        _chunks(t, chunk)

    return pl.pallas_call(
        body, name=name, grid=(d_conv // tc_ch,),
        in_specs=[pl.BlockSpec((t, tc_ch), lambda j: (0, v0 + j)), pl.BlockSpec((t, tc_ch), lambda j: (0, g0 + j)),
                  pl.BlockSpec((kw, tc_ch), lambda j: (0, j)), pl.BlockSpec((1, tc_ch), lambda j: (0, j))],
        out_specs=pl.BlockSpec((t, tc_ch), lambda j: (0, j)),
        out_shape=jax.ShapeDtypeStruct((t, d_conv), F32),
        scratch_shapes=[pltpu.VMEM((t + HALO, tc_ch), F32)],
        compiler_params=_params(("parallel",)),
    )(z, z, conv_w, conv_b)


def conv_bwd(name, z, dc, conv_w, d_pool, d_conv):
    t = z.shape[0]
    kw = conv_w.shape[0]
    tc_ch = _tile(d_conv, CHANNEL_TILE)
    v0, g0 = d_pool // tc_ch, (d_pool + d_conv) // tc_ch

    def body(v_ref, g_ref, dc_ref, w_ref, dv_ref, dg_ref, dw_ref, db_ref, pad_a, pad_dc, acc_w, acc_b):
        pad_a[pl.ds(0, HALO), :] = jnp.zeros((HALO, tc_ch), F32)
        pad_dc[pl.ds(t, HALO), :] = jnp.zeros((HALO, tc_ch), F32)
        acc_w[...] = jnp.zeros_like(acc_w)
        acc_b[...] = jnp.zeros_like(acc_b)

        def fill(s, tc):
            pad_a[pl.ds(HALO + s, tc), :] = v_ref[pl.ds(s, tc), :].astype(F32) * jax.nn.sigmoid(g_ref[pl.ds(s, tc), :].astype(F32))
            pad_dc[pl.ds(s, tc), :] = dc_ref[pl.ds(s, tc), :]

        def chunk(s, tc):
            dcv = pad_dc[pl.ds(s, tc), :]
            win_a = pad_a[pl.ds(s, tc + HALO), :]
            for k in range(kw):
                off = HALO - (kw - 1) + k
                acc_w[pl.ds(8 * k, 8), :] += _fold8(dcv * win_a[off:off + tc, :])
            acc_b[...] += _fold8(dcv)
            win_dc = pad_dc[pl.ds(s, tc + HALO), :]
            da = None
            for j in range(kw):
                term = w_ref[kw - 1 - j:kw - j, :] * win_dc[j:j + tc, :]
                da = term if da is None else da + term
            vv = v_ref[pl.ds(s, tc), :].astype(F32)
            sg = jax.nn.sigmoid(g_ref[pl.ds(s, tc), :].astype(F32))
            dv_ref[pl.ds(s, tc), :] = (da * sg).astype(BF16)
            dg_ref[pl.ds(s, tc), :] = (da * vv * sg * (1.0 - sg)).astype(BF16)

        _chunks(t, fill)
        _chunks(t, chunk)
        for k in range(kw):
            dw_ref[k:k + 1, :] = jnp.sum(acc_w[pl.ds(8 * k, 8), :], axis=0, keepdims=True)
        db_ref[...] = jnp.sum(acc_b[...], axis=0, keepdims=True)

    col = pl.BlockSpec((t, tc_ch), lambda j: (0, j))
    return pl.pallas_call(
        body, name=name, grid=(d_conv // tc_ch,),
        in_specs=[pl.BlockSpec((t, tc_ch), lambda j: (0, v0 + j)), pl.BlockSpec((t, tc_ch), lambda j: (0, g0 + j)),
                  col, pl.BlockSpec((kw, tc_ch), lambda j: (0, j))],
        out_specs=[col, col, pl.BlockSpec((kw, tc_ch), lambda j: (0, j)), pl.BlockSpec((1, tc_ch), lambda j: (0, j))],
        out_shape=[jax.ShapeDtypeStruct((t, d_conv), BF16), jax.ShapeDtypeStruct((t, d_conv), BF16),
                   jax.ShapeDtypeStruct((kw, d_conv), F32), jax.ShapeDtypeStruct((1, d_conv), F32)],
        scratch_shapes=[pltpu.VMEM((t + HALO, tc_ch), F32), pltpu.VMEM((t + HALO, tc_ch), F32),
                        pltpu.VMEM((8 * kw, tc_ch), F32), pltpu.VMEM((8, tc_ch), F32)],
        compiler_params=_params(("parallel",)),
    )(z, z, dc, conv_w)


def short_fwd(name, z, conv_w, d_short):
    t = z.shape[0]
    kw = conv_w.shape[0]
    tc_ch = _tile(d_short, CHANNEL_TILE)
    nt = d_short // tc_ch

    def body(b_ref, c_ref, u_ref, w_ref, y_ref, pad):
        pad[pl.ds(0, HALO), :] = jnp.zeros((HALO, tc_ch), F32)

        def fill(s, tc):
            pad[pl.ds(HALO + s, tc), :] = c_ref[pl.ds(s, tc), :].astype(F32) * u_ref[pl.ds(s, tc), :].astype(F32)

        def chunk(s, tc):
            win = pad[pl.ds(s, tc + HALO), :]
            cq = _taps(win, w_ref, [HALO - (kw - 1) + k for k in range(kw)], tc)
            y_ref[pl.ds(s, tc), :] = (b_ref[pl.ds(s, tc), :].astype(F32) * cq).astype(BF16)

        _chunks(t, fill)
        _chunks(t, chunk)

    return pl.pallas_call(
        body, name=name, grid=(nt,),
        in_specs=[pl.BlockSpec((t, tc_ch), lambda j: (0, j)), pl.BlockSpec((t, tc_ch), lambda j: (0, nt + j)),
                  pl.BlockSpec((t, tc_ch), lambda j: (0, 2 * nt + j)), pl.BlockSpec((kw, tc_ch), lambda j: (0, j))],
        out_specs=pl.BlockSpec((t, tc_ch), lambda j: (0, j)),
        out_shape=jax.ShapeDtypeStruct((t, d_short), BF16),
        scratch_shapes=[pltpu.VMEM((t + HALO, tc_ch), F32)],
        compiler_params=_params(("parallel",)),
    )(z, z, z, conv_w)


def short_bwd(name, z, dy, conv_w, d_short):
    t = z.shape[0]
    kw = conv_w.shape[0]
    tc_ch = _tile(d_short, CHANNEL_TILE)
    nt = d_short // tc_ch

    def body(b_ref, c_ref, u_ref, dy_ref, w_ref, db_ref, dcg_ref, du_ref, dw_ref, pad_q, pad_dcq, acc_w):
        pad_q[pl.ds(0, HALO), :] = jnp.zeros((HALO, tc_ch), F32)
        pad_dcq[pl.ds(t, HALO), :] = jnp.zeros((HALO, tc_ch), F32)
        acc_w[...] = jnp.zeros_like(acc_w)

        def fill(s, tc):
            rows = pl.ds(s, tc)
            pad_q[pl.ds(HALO + s, tc), :] = c_ref[rows, :].astype(F32) * u_ref[rows, :].astype(F32)
            pad_dcq[rows, :] = dy_ref[rows, :].astype(F32) * b_ref[rows, :].astype(F32)

        def chunk(s, tc):
            rows = pl.ds(s, tc)
            win_q = pad_q[pl.ds(s, tc + HALO), :]
            dcq = pad_dcq[rows, :]
            cq = None
            for k in range(kw):
                off = HALO - (kw - 1) + k
                shifted = win_q[off:off + tc, :]
                acc_w[pl.ds(8 * k, 8), :] += _fold8(dcq * shifted)
                term = w_ref[k:k + 1, :] * shifted
                cq = term if cq is None else cq + term
            db_ref[rows, :] = (dy_ref[rows, :].astype(F32) * cq).astype(BF16)
            win_d = pad_dcq[pl.ds(s, tc + HALO), :]
            dq = None
            for j in range(kw):
                term = w_ref[kw - 1 - j:kw - j, :] * win_d[j:j + tc, :]
                dq = term if dq is None else dq + term
            dcg_ref[rows, :] = (dq * u_ref[rows, :].astype(F32)).astype(BF16)
            du_ref[rows, :] = (dq * c_ref[rows, :].astype(F32)).astype(BF16)

        _chunks(t, fill)
        _chunks(t, chunk)
        for k in range(kw):
            dw_ref[k:k + 1, :] = jnp.sum(acc_w[pl.ds(8 * k, 8), :], axis=0, keepdims=True)

    col = pl.BlockSpec((t, tc_ch), lambda j: (0, j))
    zspec = [pl.BlockSpec((t, tc_ch), lambda j, o=o: (0, o * nt + j)) for o in range(3)]
    return pl.pallas_call(
        body, name=name, grid=(nt,),
        in_specs=[*zspec, col, pl.BlockSpec((kw, tc_ch), lambda j: (0, j))],
        out_specs=[col, col, col, pl.BlockSpec((kw, tc_ch), lambda j: (0, j))],
        out_shape=[jax.ShapeDtypeStruct((t, d_short), BF16)] * 3 + [jax.ShapeDtypeStruct((kw, d_short), F32)],
        scratch_shapes=[pltpu.VMEM((t + HALO, tc_ch), F32), pltpu.VMEM((t + HALO, tc_ch), F32),
                        pltpu.VMEM((8 * kw, tc_ch), F32)],
        compiler_params=_params(("parallel",)),
    )(z, z, z, dy, conv_w)


def adamw(name, w, m, v, contributions):
    r, c = w.shape
    tr = _tile(r, 256 if c <= 1024 else 128)

    def body(w_ref, m_ref, v_ref, g_ref, grad_ref, delta_ref, nm_ref, nv_ref):
        g = g_ref[0].astype(F32)
        for s in range(1, N_DEV):
            g = g + g_ref[s].astype(F32)
        nm = ADAM_B1 * m_ref[...] + (1.0 - ADAM_B1) * g
        nv = ADAM_B2 * v_ref[...] + (1.0 - ADAM_B2) * (g * g)
        m_hat = nm / (1.0 - ADAM_B1 ** ADAM_STEP)
        v_hat = nv / (1.0 - ADAM_B2 ** ADAM_STEP)
        grad_ref[...] = g
        delta_ref[...] = -ADAM_LR * (m_hat / (jnp.sqrt(v_hat) + ADAM_EPS) + ADAM_WD * w_ref[...])
        nm_ref[...] = nm
        nv_ref[...] = nv

    blk = pl.BlockSpec((tr, c), lambda i: (i, 0))
    return pl.pallas_call(
        body, name=name, grid=(r // tr,),
        in_specs=[blk, blk, blk, pl.BlockSpec((N_DEV, tr, c), lambda i: (0, i, 0))],
        out_specs=[blk] * 4, out_shape=[jax.ShapeDtypeStruct((r, c), F32)] * 4,
        compiler_params=_params(("parallel",)),
    )(w, m, v, contributions)


def _pad_rows(a, rows):
    return jnp.pad(a, ((0, rows - a.shape[0]), (0, 0)))


def kernel(x, mix_pre_g, mix_post_g, ffn_pre_g, ffn_post_g, ab_w_in, pool_w, pool_scale, conv_w, conv_b, conv_ln_g, conv_ln_b, ab_w_out, sc_w_in, sc_conv_w, sc_w_out, ffn_w1, ffn_w2, loss_target, m_mix_pre_g, m_mix_post_g, m_ffn_pre_g, m_ffn_post_g, m_ab_w_in, m_pool_w, m_pool_scale, m_conv_w, m_conv_b, m_conv_ln_g, m_conv_ln_b, m_ab_w_out, m_sc_w_in, m_sc_conv_w, m_sc_w_out, m_ffn_w1, m_ffn_w2, v_mix_pre_g, v_mix_post_g, v_ffn_pre_g, v_ffn_post_g, v_ab_w_in, v_pool_w, v_pool_scale, v_conv_w, v_conv_b, v_conv_ln_g, v_conv_ln_b, v_ab_w_out, v_sc_w_in, v_sc_conv_w, v_sc_w_out, v_ffn_w1, v_ffn_w2):
    t, d = x.shape[1], x.shape[2]
    d_pool = pool_scale.shape[1]
    d_conv = conv_b.shape[1]
    d_short = d
    ng, pg = pool_w.shape[1], pool_w.shape[3]
    kw, ks = conv_w.shape[1], sc_conv_w.shape[1]
    nb_ab, nb_sc, nb_ff = ab_w_in.shape[2], sc_w_in.shape[2], ffn_w1.shape[2]

    xs = x[0]
    target = loss_target[0]

    gathered = all_gather(
        [ab_w_in[0].astype(BF16), ab_w_out[0].astype(BF16), sc_w_in[0].astype(BF16), sc_w_out[0].astype(BF16),
         ffn_w1[0].astype(BF16), ffn_w1[1].astype(BF16), ffn_w2[0].astype(BF16), ffn_w2[1].astype(BF16),
         pool_w[0].astype(BF16), conv_w[0], sc_conv_w[0]],
        name="gather_weights")
    w_ab_in, w_ab_out, w_sc_in, w_sc_out, w_ff1_0, w_ff1_1, w_ff2_0, w_ff2_1, w_pool, w_conv, w_sconv = gathered
    w_ab_out = w_ab_out.reshape(d_pool + d_conv, d)
    w_sc_out = w_sc_out.reshape(d_short, d)
    w_ff1 = [w_ff1_0, w_ff1_1]
    w_ff2 = [w.reshape(-1, d) for w in (w_ff2_0, w_ff2_1)]
    w_pool = w_pool.transpose(1, 0, 2, 3).reshape(ng, pg, pg)
    w_conv = w_conv.transpose(1, 0, 2).reshape(kw, d_conv)
    w_sconv = w_sconv.transpose(1, 0, 2).reshape(ks, d_short)

    relu = lambda r: jnp.maximum(r, 0.0)
    square = lambda a: a * a
    relu2_bwd = lambda r, a: r * (2.0 * a.astype(F32))

    def row(vec, l):
        return vec[l:l + 1]

    h0 = norm_pre("norm_pre", xs, row(mix_pre_g, 0))
    z0 = mm_nn_blocked("ab_in", h0, w_ab_in, out_dtype=BF16)
    pooled, y_pool = pool_fwd("pool_fwd", z0, w_pool, pool_scale, d_pool)
    cv = conv_fwd("conv_fwd", z0, w_conv, conv_b, d_pool, d_conv)
    y_conv = ln_silu("ln_silu", cv, conv_ln_g, conv_ln_b)
    y0 = jnp.concatenate([y_pool, y_conv], axis=1)
    m0 = mm_nn("ab_out", y0, w_ab_out, out_dtype=F32)
    x1, h1 = post_pre("post_pre_0", xs, m0, row(mix_post_g, 0), row(ffn_pre_g, 0))
    a0 = mm_nn_blocked("ffn0_up", h1, w_ff1[0], out_dtype=BF16, epilogue=relu)
    f0 = mm_nn("ffn0_down", a0, w_ff2[0], out_dtype=F32, tk=1024, lhs_fn=square)
    x2, h2 = post_pre("post_pre_1", x1, f0, row(ffn_post_g, 0), row(mix_pre_g, 1))
    z1 = mm_nn_blocked("sc_in", h2, w_sc_in, out_dtype=BF16)
    y1 = short_fwd("short_fwd", z1, w_sconv, d_short)
    m1 = mm_nn("sc_out", y1, w_sc_out, out_dtype=F32)
    x3, h3 = post_pre("post_pre_2", x2, m1, row(mix_post_g, 1), row(ffn_pre_g, 1))
    a1 = mm_nn_blocked("ffn1_up", h3, w_ff1[1], out_dtype=BF16, epilogue=relu)
    f1 = mm_nn("ffn1_down", a1, w_ff2[1], out_dtype=F32, tk=1024, lhs_fn=square)
    dx4, df1, loss_part, dg_ffn_post1 = post_loss("post_loss", x3, f1, row(ffn_post_g, 1), target)
    loss = lax.psum(jnp.sum(loss_part) * (0.5 / d), ("x", "y", "c"))

    def ffn_bwd(tag, df, a, h, w1, w2):
        dw2 = mm_tn(tag + "_dw2", a, df, out_dtype=BF16, lhs_fn=square)
        dpre = mm_nt(tag + "_da", df, w2, out_dtype=BF16, extra=a, epilogue=relu2_bwd)
        dw1 = mm_tn_blocked(tag + "_dw1", h, dpre, nb_ff, out_dtype=BF16)
        dh = mm_nt_blocked(tag + "_dh", dpre, w1, out_dtype=BF16)
        return dw1, dw2, dh

    dw_ff1_1, dw_ff2_1, dh3 = ffn_bwd("ffn1", df1, a1, h3, w_ff1[1], w_ff2[1])
    dx3, dm1, dg_ffn_pre1, dg_mix_post1 = bwd_pre_post("bwd_3", dx4, x3, row(ffn_pre_g, 1), dh3, m1, row(mix_post_g, 1))

    dw_sc_out = mm_tn("sc_dwout", y1, dm1, out_dtype=BF16)
    dy1 = mm_nt("sc_dy", dm1, w_sc_out, out_dtype=BF16)
    db1, dcg1, du1, dw_sconv = short_bwd("short_bwd", z1, dy1, w_sconv, d_short)
    dz1 = jnp.concatenate([db1, dcg1, du1], axis=1)
    dw_sc_in = mm_tn_blocked("sc_dwin", h2, dz1, nb_sc, out_dtype=BF16)
    dh2 = mm_nt_blocked("sc_dh", dz1, w_sc_in, out_dtype=BF16)
    dx2, df0, dg_mix_pre1, dg_ffn_post0 = bwd_pre_post("bwd_2", dx3, x2, row(mix_pre_g, 1), dh2, f0, row(ffn_post_g, 0))

    dw_ff1_0, dw_ff2_0, dh1 = ffn_bwd("ffn0", df0, a0, h1, w_ff1[0], w_ff2[0])
    dx1, dm0, dg_ffn_pre0, dg_mix_post0 = bwd_pre_post("bwd_1", dx2, x1, row(ffn_pre_g, 0), dh1, m0, row(mix_post_g, 0))

    dw_ab_out = mm_tn("ab_dwout", y0, dm0, out_dtype=BF16)
    dy0 = mm_nt("ab_dy", dm0, w_ab_out, out_dtype=BF16)
    dcv, dg_ln_g, dg_ln_b = ln_silu_bwd("ln_silu_bwd", cv, conv_ln_g, conv_ln_b, dy0[:, d_pool:])
    dv, dgate, dw_conv, dg_conv_b = conv_bwd("conv_bwd", z0, dcv, w_conv, d_pool, d_conv)
    du0, dw_pool, dg_pool_scale = pool_bwd("pool_bwd", pooled, dy0[:, :d_pool], w_pool, pool_scale)
    dz0 = jnp.concatenate([du0, dv, dgate], axis=1)
    dw_ab_in = mm_tn_blocked("ab_dwin", h0, dz0, nb_ab, out_dtype=BF16)
    dh0 = mm_nt_blocked("ab_dh", dz0, w_ab_in, out_dtype=BF16)
    grad_x, dg_mix_pre0 = bwd_pre_final("bwd_0", dx1, xs, row(mix_pre_g, 0), dh0)

    shard_rows = d // N_DEV
    big = [
        [dw_ab_in],
        [dw_ab_out.reshape(N_DEV, (d_pool + d_conv) // N_DEV, d)],
        [dw_sc_in],
        [dw_sc_out.reshape(N_DEV, shard_rows, d)],
        [dw_ff1_0, dw_ff1_1],
        [dw_ff2_0.reshape(N_DEV, -1, d), dw_ff2_1.reshape(N_DEV, -1, d)],
    ]
    lanes = min(128, d_conv // N_DEV)
    small_parts = [
        dw_conv.reshape(kw, N_DEV, -1).transpose(1, 0, 2).reshape(N_DEV, -1, lanes),
        dw_sconv.reshape(ks, N_DEV, -1).transpose(1, 0, 2).reshape(N_DEV, -1, lanes),
        dw_pool.reshape(ng, N_DEV, pg // N_DEV, pg).transpose(1, 0, 2, 3).reshape(N_DEV, -1, lanes),
    ]
    small_rows = [p.shape[1] for p in small_parts]
    small_total = -(-sum(small_rows) // 8) * 8
    small = jnp.pad(jnp.concatenate(small_parts, axis=1), ((0, 0), (0, small_total - sum(small_rows)), (0, 0)))
    landed = exchange(big + [[small]], name="exchange_grads")

    fold = lambda a: jnp.sum(a, axis=0, keepdims=True)
    rep_rows = [fold(dg_mix_pre0), fold(dg_mix_pre1), fold(dg_mix_post0), fold(dg_mix_post1),
                fold(dg_ffn_pre0), fold(dg_ffn_pre1), fold(dg_ffn_post0), fold(dg_ffn_post1)]
    tail = jnp.concatenate([dg_pool_scale, dg_conv_b, fold(dg_ln_g), fold(dg_ln_b)], axis=1).reshape(-1, d)
    rep = _pad_rows(jnp.concatenate(rep_rows + [tail], axis=0), 16)
    rep_all = all_gather([rep], name="gather_small_grads")[0]

    def pack_rep(a_mix_pre, a_mix_post, a_ffn_pre, a_ffn_post, a_scale, a_b, a_g, a_lb):
        tail_ = jnp.concatenate([a_scale, a_b, a_g, a_lb], axis=1).reshape(-1, d)
        return _pad_rows(jnp.concatenate([a_mix_pre, a_mix_post, a_ffn_pre, a_ffn_post, tail_], axis=0), 16)

    def pack_small(a_conv, a_sconv, a_pool):
        parts = [a_conv[0].reshape(-1, lanes), a_sconv[0].reshape(-1, lanes), a_pool[0].reshape(-1, lanes)]
        return _pad_rows(jnp.concatenate(parts, axis=0), small_total)

    def upd(name, w, m, v, contrib):
        shape = w.shape
        r2 = lambda a: a.reshape(-1, shape[-1])
        outs = adamw(name, r2(w), r2(m), r2(v), contrib.reshape(N_DEV, -1, shape[-1]))
        return [o.reshape(shape) for o in outs]

    o_ab_in = upd("adam_ab_in", ab_w_in, m_ab_w_in, v_ab_w_in, landed[0])
    o_ab_out = upd("adam_ab_out", ab_w_out, m_ab_w_out, v_ab_w_out, landed[1])
    o_sc_in = upd("adam_sc_in", sc_w_in, m_sc_w_in, v_sc_w_in, landed[2])
    o_sc_out = upd("adam_sc_out", sc_w_out, m_sc_w_out, v_sc_w_out, landed[3])
    o_ff1 = upd("adam_ffn_w1", ffn_w1, m_ffn_w1, v_ffn_w1, landed[4])
    o_ff2 = upd("adam_ffn_w2", ffn_w2, m_ffn_w2, v_ffn_w2, landed[5])

    o_small = adamw("adam_small", pack_small(conv_w, sc_conv_w, pool_w), pack_small(m_conv_w, m_sc_conv_w, m_pool_w),
                    pack_small(v_conv_w, v_sc_conv_w, v_pool_w), landed[6])
    o_rep = adamw("adam_replicated",
                  pack_rep(mix_pre_g, mix_post_g, ffn_pre_g, ffn_post_g, pool_scale, conv_b, conv_ln_g, conv_ln_b),
                  pack_rep(m_mix_pre_g, m_mix_post_g, m_ffn_pre_g, m_ffn_post_g, m_pool_scale, m_conv_b, m_conv_ln_g, m_conv_ln_b),
                  pack_rep(v_mix_pre_g, v_mix_post_g, v_ffn_pre_g, v_ffn_post_g, v_pool_scale, v_conv_b, v_conv_ln_g, v_conv_ln_b),
                  rep_all)

    def unpack_small(o):
        r0, r1 = small_rows[0], small_rows[0] + small_rows[1]
        return (o[:r0].reshape(conv_w.shape), o[r0:r1].reshape(sc_conv_w.shape),
                o[r1:r1 + small_rows[2]].reshape(pool_w.shape))

    def unpack_rep(o):
        tail_ = o[8:8 + tail.shape[0]].reshape(1, -1)
        n1 = d_pool
        return dict(mix_pre_g=o[0:2], mix_post_g=o[2:4], ffn_pre_g=o[4:6], ffn_post_g=o[6:8],
                    pool_scale=tail_[:, :n1], conv_b=tail_[:, n1:n1 + d_conv],
                    conv_ln_g=tail_[:, n1 + d_conv:n1 + 2 * d_conv], conv_ln_b=tail_[:, n1 + 2 * d_conv:n1 + 3 * d_conv])

    results = []
    for kind in range(4):
        rep_o = unpack_rep(o_rep[kind])
        s_conv, s_sconv, s_pool = unpack_small(o_small[kind])
        results.append([
            rep_o["mix_pre_g"], rep_o["mix_post_g"], rep_o["ffn_pre_g"], rep_o["ffn_post_g"],
            o_ab_in[kind], s_pool, rep_o["pool_scale"], s_conv, rep_o["conv_b"], rep_o["conv_ln_g"], rep_o["conv_ln_b"],
            o_ab_out[kind], o_sc_in[kind], s_sconv, o_sc_out[kind], o_ff1[kind], o_ff2[kind]])

    return (loss, grad_x[None], *results[0], *results[1], *results[2], *results[3])
```

```python
import functools

import jax
import jax.numpy as jnp
from jax import lax
from jax.experimental import pallas as pl
from jax.experimental.pallas import tpu as pltpu

F32 = jnp.float32
BF16 = jnp.bfloat16
MESH = pl.DeviceIdType.MESH
ANY = pl.BlockSpec(memory_space=pl.ANY)

NORM_EPS = 1e-6
POOL_WINDOWS = (2, 4, 8, 16)
MAX_POOL_WINDOW = 16
ADAM_LR = 0.001
ADAM_B1 = 0.9
ADAM_B2 = 0.999
ADAM_EPS = 1e-08
ADAM_WD = 0.01
ADAM_STEP = 10

N_DEV = 8
VMEM_LIMIT = 48 * 1024 * 1024
ROW_TILE = 256
CHANNEL_TILE = 256
TIME_CHUNK = 64
HALO = 32

NN = (((1,), (0,)), ((), ()))
NT = (((1,), (1,)), ((), ()))
TN = (((0,), (0,)), ((), ()))


def _params(sem):
    return pltpu.CompilerParams(dimension_semantics=sem, vmem_limit_bytes=VMEM_LIMIT)


def _place():
    x, y, c = lax.axis_index("x"), lax.axis_index("y"), lax.axis_index("c")
    return x, y, c


def _slot(px, py, pc):
    return 4 * px + 2 * py + pc


def all_gather(arrs, name):
    n = len(arrs)

    def body(*refs):
        ins, outs = refs[:n], refs[n:2 * n]
        send_sems, recv_sems, local_sems = refs[2 * n:]
        x, y, c = _place()
        me = _slot(x, y, c)
        sibling = (x, y, 1 - c)
        chips = [(1 - x, y), (x, 1 - y), (1 - x, 1 - y)]

        def copy(a, k, slot, to, src=None):
            return pltpu.make_async_remote_copy(
                src_ref=outs[a].at[slot] if src is None else src, dst_ref=outs[a].at[slot],
                send_sem=send_sems.at[a, k], recv_sem=recv_sems.at[a, k], device_id=to, device_id_type=MESH)

        started = []
        for a in range(n):
            mine = pltpu.make_async_copy(ins[a], outs[a].at[me], local_sems.at[a])
            mine.start()
            started.append(mine)
        sends = []
        for a in range(n):
            first = [copy(a, 0, me, sibling, src=ins[a])]
            first += [copy(a, 1 + j, me, (px, py, c), src=ins[a]) for j, (px, py) in enumerate(chips)]
            for cp in first:
                cp.start()
            sends += first
        for a in range(n):
            for j, (px, py) in enumerate(chips):
                copy(a, 1 + j, _slot(px, py, c), (x, y, c)).wait_recv()
                fwd = copy(a, 4 + j, _slot(px, py, c), sibling)
                fwd.start()
                sends.append(fwd)
        for a in range(n):
            copy(a, 0, _slot(x, y, 1 - c), (x, y, c)).wait_recv()
            for j, (px, py) in enumerate(chips):
                copy(a, 4 + j, _slot(px, py, 1 - c), (x, y, c)).wait_recv()
        for cp in sends:
            cp.wait_send()
        for mine in started:
            mine.wait()

    outs = pl.pallas_call(
        body, name=name,
        out_shape=[jax.ShapeDtypeStruct((N_DEV,) + a.shape, a.dtype) for a in arrs],
        in_specs=[ANY] * n, out_specs=[ANY] * n,
        scratch_shapes=[pltpu.SemaphoreType.DMA((n, 7)), pltpu.SemaphoreType.DMA((n, 7)),
                        pltpu.SemaphoreType.DMA((n,))],
    )(*arrs)
    return list(outs)


CHIPS = [(0, 0), (0, 1), (1, 0), (1, 1)]
N_CHIP = len(CHIPS)


def exchange_pair(arrs, name):
    n = len(arrs)

    def body(*refs):
        ins, outs = refs[:n], refs[n:2 * n]
        send_sems, recv_sems = refs[2 * n:]
        x, y, c = _place()
        copies = []
        for a in range(n):
            for q, (qx, qy) in enumerate(CHIPS):
                cp = pltpu.make_async_remote_copy(
                    src_ref=ins[a].at[_slot(qx, qy, 1 - c)], dst_ref=outs[a].at[q],
                    send_sem=send_sems.at[a, q], recv_sem=recv_sems.at[a, q],
                    device_id=(x, y, 1 - c), device_id_type=MESH)
                cp.start()
                copies.append(cp)
        for cp in copies:
            cp.wait()

    outs = pl.pallas_call(
        body, name=name,
        out_shape=[jax.ShapeDtypeStruct((N_CHIP,) + a.shape[1:], a.dtype) for a in arrs],
        in_specs=[ANY] * n, out_specs=[ANY] * n,
        scratch_shapes=[pltpu.SemaphoreType.DMA((n, N_CHIP)), pltpu.SemaphoreType.DMA((n, N_CHIP))],
    )(*arrs)
    return list(outs)


def pair_add(name, g, from_sibling):
    _, r, c_dim = g.shape
    tr = _tile(r, 256)
    core = lax.axis_index("c").astype(jnp.int32).reshape(1)

    def body(core_ref, g_ref, s_ref, o_ref):
        o_ref[...] = (g_ref[...].astype(F32) + s_ref[...].astype(F32)).astype(o_ref.dtype)

    return pl.pallas_call(
        body, name=name,
        grid_spec=pltpu.PrefetchScalarGridSpec(
            num_scalar_prefetch=1, grid=(N_CHIP, r // tr),
            in_specs=[pl.BlockSpec((None, None, tr, c_dim), lambda q, i, core_ref: (q, core_ref[0], i, 0)),
                      pl.BlockSpec((None, tr, c_dim), lambda q, i, core_ref: (q, i, 0))],
            out_specs=pl.BlockSpec((None, tr, c_dim), lambda q, i, core_ref: (q, i, 0))),
        out_shape=jax.ShapeDtypeStruct((N_CHIP, r, c_dim), g.dtype),
        compiler_params=_params(("parallel", "parallel")),
    )(core, g.reshape(N_CHIP, 2, r, c_dim), from_sibling)


def exchange_chips(groups, name):
    arrs = [a for grp in groups for a in grp]
    n = len(arrs)
    where = []
    for o, grp in enumerate(groups):
        off = 0
        for a in grp:
            where.append((o, off))
            off += a.shape[1]
    flips = [(1, 0), (0, 1), (1, 1)]

    def body(*refs):
        ins, outs = refs[:n], refs[n:n + len(groups)]
        send_sems, recv_sems, local_sems = refs[n + len(groups):]
        x, y, c = _place()
        my_chip = 2 * x + y

        def landing(a, slot):
            o, off = where[a]
            return outs[o].at[slot, pl.ds(off, arrs[a].shape[1])]

        local = []
        for a in range(n):
            cp = pltpu.make_async_copy(ins[a].at[my_chip], landing(a, my_chip), local_sems.at[a])
            cp.start()
            local.append(cp)
        copies = []
        for a in range(n):
            for k, (dx, dy) in enumerate(flips):
                px = 1 - x if dx else x
                py = 1 - y if dy else y
                cp = pltpu.make_async_remote_copy(
                    src_ref=ins[a].at[2 * px + py], dst_ref=landing(a, my_chip),
                    send_sem=send_sems.at[a, k], recv_sem=recv_sems.at[a, k],
                    device_id=(px, py, c), device_id_type=MESH)
                cp.start()
                copies.append(cp)
        for cp in copies:
            cp.wait()
        for cp in local:
            cp.wait()

    outs = pl.pallas_call(
        body, name=name,
        out_shape=[jax.ShapeDtypeStruct((N_CHIP, sum(a.shape[1] for a in grp), grp[0].shape[2]), grp[0].dtype)
                   for grp in groups],
        in_specs=[ANY] * n, out_specs=[ANY] * len(groups),
        scratch_shapes=[pltpu.SemaphoreType.DMA((n, 3)), pltpu.SemaphoreType.DMA((n, 3)),
                        pltpu.SemaphoreType.DMA((n,))],
    )(*arrs)
    return list(outs)


def _matmul(name, lhs, rhs, *, out_shape, out_dtype, grid, lhs_spec, rhs_spec, out_spec, dims, acc_shape,
            lhs_fn=None, extra=(), extra_specs=(), epilogue=None):
    nk = grid[2]
    n_extra = len(extra)

    def body(*refs):
        lhs_ref, rhs_ref = refs[0], refs[1]
        extra_refs = refs[2:2 + n_extra]
        out_ref = refs[2 + n_extra]
        a = lhs_ref[...]
        if lhs_fn is not None:
            a = lhs_fn(a)
        p = lax.dot_general(a, rhs_ref[...], dims, preferred_element_type=F32)

        def finish(r):
            if epilogue is not None:
                r = epilogue(r, *[e[...] for e in extra_refs])
            out_ref[...] = r.astype(out_dtype)

        if nk == 1:
            finish(p)
        else:
            acc_ref = refs[3 + n_extra]
            k = pl.program_id(2)

            @pl.when(k == 0)
            def _():
                acc_ref[...] = p

            @pl.when(k > 0)
            def _():
                acc_ref[...] += p

            @pl.when(k == nk - 1)
            def _():
                finish(acc_ref[...])

    return pl.pallas_call(
        body, name=name, grid=grid,
        out_shape=jax.ShapeDtypeStruct(out_shape, out_dtype),
        in_specs=[lhs_spec, rhs_spec, *extra_specs], out_specs=out_spec,
        scratch_shapes=[pltpu.VMEM(acc_shape, F32)] if nk > 1 else [],
        compiler_params=_params(("parallel", "parallel", "arbitrary")),
    )(lhs, rhs, *extra)


def _tile(n, want):
    return want if n % want == 0 else n


def mm_nn(name, x, w, *, out_dtype, tn=512, tk=None, lhs_fn=None, epilogue=None):
    t, kdim = x.shape
    n = w.shape[1]
    tn = _tile(n, tn)
    tk = kdim if tk is None else _tile(kdim, tk)
    return _matmul(
        name, x, w, out_shape=(t, n), out_dtype=out_dtype, grid=(1, n // tn, kdim // tk),
        lhs_spec=pl.BlockSpec((t, tk), lambda i, j, k: (i, k)),
        rhs_spec=pl.BlockSpec((tk, tn), lambda i, j, k: (k, j)),
        out_spec=pl.BlockSpec((t, tn), lambda i, j, k: (i, j)),
        dims=NN, acc_shape=(t, tn), lhs_fn=lhs_fn, epilogue=epilogue)


def mm_nn_blocked(name, x, w, *, out_dtype, epilogue=None):
    t, kdim = x.shape
    nb = w.shape[2]
    tn = nb // 2 if nb >= 1024 else nb
    sub = nb // tn
    return _matmul(
        name, x, w, out_shape=(t, N_DEV * nb), out_dtype=out_dtype, grid=(1, N_DEV * sub, 1),
        lhs_spec=pl.BlockSpec((t, kdim), lambda i, j, k: (i, k)),
        rhs_spec=pl.BlockSpec((None, kdim, tn), lambda i, j, k: (j // sub, k, j % sub)),
        out_spec=pl.BlockSpec((t, tn), lambda i, j, k: (i, j)),
        dims=NN, acc_shape=(t, tn), epilogue=epilogue)


def mm_nt(name, dy, w, *, out_dtype, tn=512, extra=None, epilogue=None):
    t, n = dy.shape
    kdim = w.shape[0]
    tn = _tile(kdim, tn)
    extra_arrs = () if extra is None else (extra,)
    extra_specs = () if extra is None else (pl.BlockSpec((t, tn), lambda i, j, k: (i, j)),)
    return _matmul(
        name, dy, w, out_shape=(t, kdim), out_dtype=out_dtype, grid=(1, kdim // tn, 1),
        lhs_spec=pl.BlockSpec((t, n), lambda i, j, k: (i, k)),
        rhs_spec=pl.BlockSpec((tn, n), lambda i, j, k: (j, k)),
        out_spec=pl.BlockSpec((t, tn), lambda i, j, k: (i, j)),
        dims=NT, acc_shape=(t, tn), extra=extra_arrs, extra_specs=extra_specs, epilogue=epilogue)


def mm_nt_blocked(name, dz, w, *, out_dtype, tn=512):
    t = dz.shape[0]
    kdim, nb = w.shape[1], w.shape[2]
    tn = _tile(kdim, tn)
    return _matmul(
        name, dz, w, out_shape=(t, kdim), out_dtype=out_dtype, grid=(1, kdim // tn, N_DEV),
        lhs_spec=pl.BlockSpec((t, nb), lambda i, j, k: (i, k)),
        rhs_spec=pl.BlockSpec((None, tn, nb), lambda i, j, k: (k, j, 0)),
        out_spec=pl.BlockSpec((t, tn), lambda i, j, k: (i, j)),
        dims=NT, acc_shape=(t, tn))


def mm_tn(name, x, dy, *, out_dtype, tk=1024, tn=1024, lhs_fn=None):
    t, kdim = x.shape
    n = dy.shape[1]
    tk, tn = _tile(kdim, tk), _tile(n, tn)
    return _matmul(
        name, x, dy, out_shape=(kdim, n), out_dtype=out_dtype, grid=(kdim // tk, n // tn, 1),
        lhs_spec=pl.BlockSpec((t, tk), lambda i, j, k: (k, i)),
        rhs_spec=pl.BlockSpec((t, tn), lambda i, j, k: (k, j)),
        out_spec=pl.BlockSpec((tk, tn), lambda i, j, k: (i, j)),
        dims=TN, acc_shape=(tk, tn), lhs_fn=lhs_fn)


def mm_tn_blocked(name, x, dz, nb, *, out_dtype, tk=1024):
    t, kdim = x.shape
    tk = _tile(kdim, tk)
    return _matmul(
        name, x, dz, out_shape=(N_DEV, kdim, nb), out_dtype=out_dtype, grid=(kdim // tk, N_DEV, 1),
        lhs_spec=pl.BlockSpec((t, tk), lambda i, j, k: (k, i)),
        rhs_spec=pl.BlockSpec((t, nb), lambda i, j, k: (k, j)),
        out_spec=pl.BlockSpec((None, tk, nb), lambda i, j, k: (j, i, 0)),
        dims=TN, acc_shape=(tk, nb))


def _rstd(v):
    return lax.rsqrt(jnp.mean(v * v, axis=-1, keepdims=True) + NORM_EPS)


def _rms_bwd(v, g, dy):
    r = _rstd(v)
    vhat = v * r
    dvh = dy * g
    dv = r * (dvh - vhat * jnp.mean(dvh * vhat, axis=-1, keepdims=True))
    return dv, dy * vhat


def _fold8(v):
    rows, n = v.shape
    return jnp.sum(v.reshape(rows // 8, 8, n), axis=0)


def _fold_lanes(v):
    out = v[:, 0:128]
    for i in range(1, v.shape[1] // 128):
        out = out + v[:, 128 * i:128 * (i + 1)]
    return out


def _accumulate(ref, v):
    i = pl.program_id(0)

    @pl.when(i == 0)
    def _():
        ref[...] = v

    @pl.when(i > 0)
    def _():
        ref[...] += v


def _row_call(body, name, t, ins, row_in, outs, acc_outs=(), tr=ROW_TILE):
    tr = _tile(t, tr)
    in_specs = [pl.BlockSpec((tr, a.shape[1]), lambda i: (i, 0)) if tiled
                else pl.BlockSpec(a.shape, lambda i: (0, 0)) for a, tiled in zip(ins, row_in)]
    out_specs = [pl.BlockSpec((tr, n), lambda i: (i, 0)) for n, _ in outs]
    out_specs += [pl.BlockSpec((8, n), lambda i: (0, 0)) for n in acc_outs]
    out_shape = [jax.ShapeDtypeStruct((t, n), dt) for n, dt in outs]
    out_shape += [jax.ShapeDtypeStruct((8, n), F32) for n in acc_outs]
    return pl.pallas_call(
        body, name=name, grid=(t // tr,), in_specs=in_specs, out_specs=out_specs, out_shape=out_shape,
        compiler_params=_params(("arbitrary",) if acc_outs else ("parallel",)),
    )(*ins)


def norm_pre(name, x, g):
    t, d = x.shape

    def body(x_ref, g_ref, h_ref):
        v = x_ref[...]
        h_ref[...] = (v * _rstd(v) * g_ref[...]).astype(BF16)

    return _row_call(body, name, t, [x, g], [True, False], [(d, BF16)])[0]


def post_pre(name, x, m, g_post, g_pre):
    t, d = x.shape

    def body(x_ref, m_ref, gp_ref, gn_ref, xo_ref, h_ref):
        mv = m_ref[...]
        xn = x_ref[...] + mv * _rstd(mv) * gp_ref[...]
        xo_ref[...] = xn
        h_ref[...] = (xn * _rstd(xn) * gn_ref[...]).astype(BF16)

    return _row_call(body, name, t, [x, m, g_post, g_pre], [True, True, False, False], [(d, F32), (d, BF16)])


def post_loss(name, x, f, g_post, target):
    t, d = x.shape

    def body(x_ref, f_ref, g_ref, t_ref, dx_ref, df_ref, loss_ref, dg_ref):
        fv = f_ref[...]
        g = g_ref[...]
        out = x_ref[...] + fv * _rstd(fv) * g
        err = out - t_ref[...]
        dx = err * (1.0 / d)
        dx_ref[...] = dx
        dfv, dg_rows = _rms_bwd(fv, g, dx)
        df_ref[...] = dfv.astype(BF16)
        _accumulate(loss_ref, _fold8(_fold_lanes(err * err)))
        _accumulate(dg_ref, _fold8(dg_rows))

    return _row_call(body, name, t, [x, f, g_post, target], [True, True, False, True],
                     [(d, F32), (d, BF16)], acc_outs=(128, d))


def bwd_pre_post(name, dx_out, x_in, g_pre, dh, f_prev, g_post_prev):
    t, d = x_in.shape

    def body(dxo_ref, x_ref, gpre_ref, dh_ref, f_ref, gpost_ref, dxi_ref, df_ref, dgpre_ref, dgpost_ref):
        dxv, dgpre_rows = _rms_bwd(x_ref[...], gpre_ref[...], dh_ref[...].astype(F32))
        dxi = dxo_ref[...] + dxv
        dxi_ref[...] = dxi
        dfv, dgpost_rows = _rms_bwd(f_ref[...], gpost_ref[...], dxi)
        df_ref[...] = dfv.astype(BF16)
        _accumulate(dgpre_ref, _fold8(dgpre_rows))
        _accumulate(dgpost_ref, _fold8(dgpost_rows))

    return _row_call(body, name, t, [dx_out, x_in, g_pre, dh, f_prev, g_post_prev],
                     [True, True, False, True, True, False], [(d, F32), (d, BF16)], acc_outs=(d, d))


def bwd_pre_final(name, dx_out, x_in, g_pre, dh):
    t, d = x_in.shape

    def body(dxo_ref, x_ref, gpre_ref, dh_ref, dxi_ref, dgpre_ref):
        dxv, dgpre_rows = _rms_bwd(x_ref[...], gpre_ref[...], dh_ref[...].astype(F32))
        dxi_ref[...] = dxo_ref[...] + dxv
        _accumulate(dgpre_ref, _fold8(dgpre_rows))

    return _row_call(body, name, t, [dx_out, x_in, g_pre, dh], [True, True, False, True], [(d, F32)], acc_outs=(d,))


def _layer_norm_parts(cv):
    mu = jnp.mean(cv, axis=-1, keepdims=True)
    xc = cv - mu
    rstd = lax.rsqrt(jnp.mean(xc * xc, axis=-1, keepdims=True) + NORM_EPS)
    return xc * rstd, rstd


def ln_silu(name, cv, g, b):
    t, n = cv.shape

    def body(c_ref, g_ref, b_ref, y_ref):
        chat, _ = _layer_norm_parts(c_ref[...])
        ln = chat * g_ref[...] + b_ref[...]
        y_ref[...] = (ln * jax.nn.sigmoid(ln)).astype(BF16)

    return _row_call(body, name, t, [cv, g, b], [True, False, False], [(n, BF16)])[0]


def ln_silu_bwd(name, cv, g, b, dy):
    t, n = cv.shape

    def body(c_ref, g_ref, b_ref, dy_ref, dc_ref, dg_ref, db_ref):
        chat, rstd = _layer_norm_parts(c_ref[...])
        g = g_ref[...]
        ln = chat * g + b_ref[...]
        s = jax.nn.sigmoid(ln)
        dln = dy_ref[...].astype(F32) * (s * (1.0 + ln * (1.0 - s)))
        dchat = dln * g
        dc_ref[...] = rstd * (dchat - jnp.mean(dchat, axis=-1, keepdims=True)
                              - chat * jnp.mean(dchat * chat, axis=-1, keepdims=True))
        _accumulate(dg_ref, _fold8(dln * chat))
        _accumulate(db_ref, _fold8(dln))

    return _row_call(body, name, t, [cv, g, b, dy], [True, False, False, True], [(n, F32)], acc_outs=(n, n))


def _chunks(t, fn, tc=TIME_CHUNK):
    tc = _tile(t, tc)

    def step(i, carry):
        fn(pl.multiple_of(i * tc, tc), tc)
        return carry

    lax.fori_loop(0, t // tc, step, 0)


def _taps(window, w_ref, offsets, tc):
    acc = None
    for k, off in enumerate(offsets):
        term = w_ref[k:k + 1, :] * window[off:off + tc, :]
        acc = term if acc is None else acc + term
    return acc


def _window_sums(win, tc, causal):
    sums = []
    cur, rows, step = win, tc + HALO, 1
    for _ in POOL_WINDOWS:
        rows -= 8
        if causal:
            cur = cur[8:8 + rows, :] + cur[8 - step:8 - step + rows, :]
            sums.append(cur[rows - tc:rows, :])
        else:
            cur = cur[0:rows, :] + cur[step:step + rows, :]
            sums.append(cur[0:tc, :])
        step *= 2
    return sums


def _pick(vals, g):
    out = vals[-1]
    for i in range(len(vals) - 2, -1, -1):
        out = jnp.where(g == i, vals[i], out)
    return out


def _pool_count(s, tc, g):
    t1 = (lax.broadcasted_iota(jnp.int32, (tc, 1), 0) + (s + 1)).astype(F32)
    width = _pick([float(w) for w in POOL_WINDOWS], g)
    return jnp.minimum(t1, width)


def pool_fwd(name, z, pool_w, pool_scale, d_pool):
    t = z.shape[0]
    ng, pg = pool_w.shape[0], pool_w.shape[1]

    def body(u_ref, w_ref, s_ref, pooled_ref, y_ref, pad):
        g = pl.program_id(0)
        pad[pl.ds(0, HALO), :] = jnp.zeros((HALO, pg), F32)

        def fill(s, tc):
            pad[pl.ds(HALO + s, tc), :] = u_ref[pl.ds(s, tc), :].astype(F32)

        def chunk(s, tc):
            win = pad[pl.ds(s, tc + HALO), :]
            total = _pick(_window_sums(win, tc, causal=True), g)
            pooled = total / _pool_count(s, tc, g) - win[HALO:HALO + tc, :]
            pooled_ref[pl.ds(s, tc), :] = pooled.astype(BF16)

        _chunks(t, fill)
        _chunks(t, chunk)
        mixed = jnp.dot(pooled_ref[...], w_ref[...], preferred_element_type=F32)
        y_ref[...] = (mixed * s_ref[...]).astype(BF16)

    col = pl.BlockSpec((t, pg), lambda g: (0, g))
    return pl.pallas_call(
        body, name=name, grid=(ng,),
        in_specs=[col, pl.BlockSpec((None, pg, pg), lambda g: (g, 0, 0)), pl.BlockSpec((1, pg), lambda g: (0, g))],
        out_specs=[col, col],
        out_shape=[jax.ShapeDtypeStruct((t, d_pool), BF16), jax.ShapeDtypeStruct((t, d_pool), BF16)],
        scratch_shapes=[pltpu.VMEM((t + HALO, pg), F32)],
        compiler_params=_params(("parallel",)),
    )(z, pool_w, pool_scale)


def pool_bwd(name, pooled, dy, pool_w, pool_scale):
    t, d_pool = pooled.shape
    ng, pg = pool_w.shape[0], pool_w.shape[1]

    def body(p_ref, dy_ref, w_ref, s_ref, du_ref, dw_ref, ds_ref, pad):
        g = pl.program_id(0)
        w = w_ref[...]
        dyv = dy_ref[...].astype(F32)
        mixed = jnp.dot(p_ref[...], w, preferred_element_type=F32)
        ds_ref[...] = jnp.sum(dyv * mixed, axis=0, keepdims=True)
        dmixed = (dyv * s_ref[...]).astype(BF16)
        dw_ref[...] = lax.dot_general(p_ref[...], dmixed, TN, preferred_element_type=F32)
        pad[...] = jnp.zeros((t + HALO, pg), F32)
        pad[pl.ds(0, t), :] = lax.dot_general(dmixed, w, NT, preferred_element_type=F32)

        def scale(s, tc):
            pad[pl.ds(s, tc), :] = pad[pl.ds(s, tc), :] / _pool_count(s, tc, g)

        def chunk(s, tc):
            win = pad[pl.ds(s, tc + HALO), :]
            total = _pick(_window_sums(win, tc, causal=False), g)
            du_ref[pl.ds(s, tc), :] = (total - win[0:tc, :] * _pool_count(s, tc, g)).astype(BF16)

        _chunks(t, scale)
        _chunks(t, chunk)

    col = pl.BlockSpec((t, pg), lambda g: (0, g))
    vec = pl.BlockSpec((1, pg), lambda g: (0, g))
    mat = pl.BlockSpec((None, pg, pg), lambda g: (g, 0, 0))
    return pl.pallas_call(
        body, name=name, grid=(ng,),
        in_specs=[col, col, mat, vec], out_specs=[col, mat, vec],
        out_shape=[jax.ShapeDtypeStruct((t, d_pool), BF16), jax.ShapeDtypeStruct((ng, pg, pg), F32),
                   jax.ShapeDtypeStruct((1, d_pool), F32)],
        scratch_shapes=[pltpu.VMEM((t + HALO, pg), F32)],
        compiler_params=_params(("parallel",)),
    )(pooled, dy, pool_w, pool_scale)


def conv_fwd(name, z, conv_w, conv_b, d_pool, d_conv):
    t = z.shape[0]
    kw = conv_w.shape[0]
    tc_ch = _tile(d_conv, CHANNEL_TILE)
    v0, g0 = d_pool // tc_ch, (d_pool + d_conv) // tc_ch

    def body(v_ref, g_ref, w_ref, b_ref, c_ref, pad):
        pad[pl.ds(0, HALO), :] = jnp.zeros((HALO, tc_ch), F32)

        def fill(s, tc):
            pad[pl.ds(HALO + s, tc), :] = v_ref[pl.ds(s, tc), :].astype(F32) * jax.nn.sigmoid(g_ref[pl.ds(s, tc), :].astype(F32))

        def chunk(s, tc):
            win = pad[pl.ds(s, tc + HALO), :]
            c_ref[pl.ds(s, tc), :] = _taps(win, w_ref, [HALO - (kw - 1) + k for k in range(kw)], tc) + b_ref[...]

        _chunks(t, fill)
        _chunks(t, chunk)

    return pl.pallas_call(
        body, name=name, grid=(d_conv // tc_ch,),
        in_specs=[pl.BlockSpec((t, tc_ch), lambda j: (0, v0 + j)), pl.BlockSpec((t, tc_ch), lambda j: (0, g0 + j)),
                  pl.BlockSpec((kw, tc_ch), lambda j: (0, j)), pl.BlockSpec((1, tc_ch), lambda j: (0, j))],
        out_specs=pl.BlockSpec((t, tc_ch), lambda j: (0, j)),
        out_shape=jax.ShapeDtypeStruct((t, d_conv), F32),
        scratch_shapes=[pltpu.VMEM((t + HALO, tc_ch), F32)],
        compiler_params=_params(("parallel",)),
    )(z, z, conv_w, conv_b)


def conv_bwd(name, z, dc, conv_w, d_pool, d_conv):
    t = z.shape[0]
    kw = conv_w.shape[0]
    tc_ch = _tile(d_conv, CHANNEL_TILE)
    v0, g0 = d_pool // tc_ch, (d_pool + d_conv) // tc_ch

    def body(v_ref, g_ref, dc_ref, w_ref, dv_ref, dg_ref, dw_ref, db_ref, pad_a, pad_dc, acc_w, acc_b):
        pad_a[pl.ds(0, HALO), :] = jnp.zeros((HALO, tc_ch), F32)
        pad_dc[pl.ds(t, HALO), :] = jnp.zeros((HALO, tc_ch), F32)
        acc_w[...] = jnp.zeros_like(acc_w)
        acc_b[...] = jnp.zeros_like(acc_b)

        def fill(s, tc):
            pad_a[pl.ds(HALO + s, tc), :] = v_ref[pl.ds(s, tc), :].astype(F32) * jax.nn.sigmoid(g_ref[pl.ds(s, tc), :].astype(F32))
            pad_dc[pl.ds(s, tc), :] = dc_ref[pl.ds(s, tc), :]

        def chunk(s, tc):
            dcv = pad_dc[pl.ds(s, tc), :]
            win_a = pad_a[pl.ds(s, tc + HALO), :]
            for k in range(kw):
                off = HALO - (kw - 1) + k
                acc_w[pl.ds(8 * k, 8), :] += _fold8(dcv * win_a[off:off + tc, :])
            acc_b[...] += _fold8(dcv)
            win_dc = pad_dc[pl.ds(s, tc + HALO), :]
            da = None
            for j in range(kw):
                term = w_ref[kw - 1 - j:kw - j, :] * win_dc[j:j + tc, :]
                da = term if da is None else da + term
            vv = v_ref[pl.ds(s, tc), :].astype(F32)
            sg = jax.nn.sigmoid(g_ref[pl.ds(s, tc), :].astype(F32))
            dv_ref[pl.ds(s, tc), :] = (da * sg).astype(BF16)
            dg_ref[pl.ds(s, tc), :] = (da * vv * sg * (1.0 - sg)).astype(BF16)

        _chunks(t, fill)
        _chunks(t, chunk)
        for k in range(kw):
            dw_ref[k:k + 1, :] = jnp.sum(acc_w[pl.ds(8 * k, 8), :], axis=0, keepdims=True)
        db_ref[...] = jnp.sum(acc_b[...], axis=0, keepdims=True)

    col = pl.BlockSpec((t, tc_ch), lambda j: (0, j))
    return pl.pallas_call(
        body, name=name, grid=(d_conv // tc_ch,),
        in_specs=[pl.BlockSpec((t, tc_ch), lambda j: (0, v0 + j)), pl.BlockSpec((t, tc_ch), lambda j: (0, g0 + j)),
                  col, pl.BlockSpec((kw, tc_ch), lambda j: (0, j))],
        out_specs=[col, col, pl.BlockSpec((kw, tc_ch), lambda j: (0, j)), pl.BlockSpec((1, tc_ch), lambda j: (0, j))],
        out_shape=[jax.ShapeDtypeStruct((t, d_conv), BF16), jax.ShapeDtypeStruct((t, d_conv), BF16),
                   jax.ShapeDtypeStruct((kw, d_conv), F32), jax.ShapeDtypeStruct((1, d_conv), F32)],
        scratch_shapes=[pltpu.VMEM((t + HALO, tc_ch), F32), pltpu.VMEM((t + HALO, tc_ch), F32),
                        pltpu.VMEM((8 * kw, tc_ch), F32), pltpu.VMEM((8, tc_ch), F32)],
        compiler_params=_params(("parallel",)),
    )(z, z, dc, conv_w)


def short_fwd(name, z, conv_w, d_short):
    t = z.shape[0]
    kw = conv_w.shape[0]
    tc_ch = _tile(d_short, CHANNEL_TILE)
    nt = d_short // tc_ch

    def body(b_ref, c_ref, u_ref, w_ref, y_ref, pad):
        pad[pl.ds(0, HALO), :] = jnp.zeros((HALO, tc_ch), F32)

        def fill(s, tc):
            pad[pl.ds(HALO + s, tc), :] = c_ref[pl.ds(s, tc), :].astype(F32) * u_ref[pl.ds(s, tc), :].astype(F32)

        def chunk(s, tc):
            win = pad[pl.ds(s, tc + HALO), :]
            cq = _taps(win, w_ref, [HALO - (kw - 1) + k for k in range(kw)], tc)
            y_ref[pl.ds(s, tc), :] = (b_ref[pl.ds(s, tc), :].astype(F32) * cq).astype(BF16)

        _chunks(t, fill)
        _chunks(t, chunk)

    return pl.pallas_call(
        body, name=name, grid=(nt,),
        in_specs=[pl.BlockSpec((t, tc_ch), lambda j: (0, j)), pl.BlockSpec((t, tc_ch), lambda j: (0, nt + j)),
                  pl.BlockSpec((t, tc_ch), lambda j: (0, 2 * nt + j)), pl.BlockSpec((kw, tc_ch), lambda j: (0, j))],
        out_specs=pl.BlockSpec((t, tc_ch), lambda j: (0, j)),
        out_shape=jax.ShapeDtypeStruct((t, d_short), BF16),
        scratch_shapes=[pltpu.VMEM((t + HALO, tc_ch), F32)],
        compiler_params=_params(("parallel",)),
    )(z, z, z, conv_w)


def short_bwd(name, z, dy, conv_w, d_short):
    t = z.shape[0]
    kw = conv_w.shape[0]
    tc_ch = _tile(d_short, CHANNEL_TILE)
    nt = d_short // tc_ch

    def body(b_ref, c_ref, u_ref, dy_ref, w_ref, db_ref, dcg_ref, du_ref, dw_ref, pad_q, pad_dcq, acc_w):
        pad_q[pl.ds(0, HALO), :] = jnp.zeros((HALO, tc_ch), F32)
        pad_dcq[pl.ds(t, HALO), :] = jnp.zeros((HALO, tc_ch), F32)
        acc_w[...] = jnp.zeros_like(acc_w)

        def fill(s, tc):
            rows = pl.ds(s, tc)
            pad_q[pl.ds(HALO + s, tc), :] = c_ref[rows, :].astype(F32) * u_ref[rows, :].astype(F32)
            pad_dcq[rows, :] = dy_ref[rows, :].astype(F32) * b_ref[rows, :].astype(F32)

        def chunk(s, tc):
            rows = pl.ds(s, tc)
            win_q = pad_q[pl.ds(s, tc + HALO), :]
            dcq = pad_dcq[rows, :]
            cq = None
            for k in range(kw):
                off = HALO - (kw - 1) + k
                shifted = win_q[off:off + tc, :]
                acc_w[pl.ds(8 * k, 8), :] += _fold8(dcq * shifted)
                term = w_ref[k:k + 1, :] * shifted
                cq = term if cq is None else cq + term
            db_ref[rows, :] = (dy_ref[rows, :].astype(F32) * cq).astype(BF16)
            win_d = pad_dcq[pl.ds(s, tc + HALO), :]
            dq = None
            for j in range(kw):
                term = w_ref[kw - 1 - j:kw - j, :] * win_d[j:j + tc, :]
                dq = term if dq is None else dq + term
            dcg_ref[rows, :] = (dq * u_ref[rows, :].astype(F32)).astype(BF16)
            du_ref[rows, :] = (dq * c_ref[rows, :].astype(F32)).astype(BF16)

        _chunks(t, fill)
        _chunks(t, chunk)
        for k in range(kw):
            dw_ref[k:k + 1, :] = jnp.sum(acc_w[pl.ds(8 * k, 8), :], axis=0, keepdims=True)

    col = pl.BlockSpec((t, tc_ch), lambda j: (0, j))
    zspec = [pl.BlockSpec((t, tc_ch), lambda j, o=o: (0, o * nt + j)) for o in range(3)]
    return pl.pallas_call(
        body, name=name, grid=(nt,),
        in_specs=[*zspec, col, pl.BlockSpec((kw, tc_ch), lambda j: (0, j))],
        out_specs=[col, col, col, pl.BlockSpec((kw, tc_ch), lambda j: (0, j))],
        out_shape=[jax.ShapeDtypeStruct((t, d_short), BF16)] * 3 + [jax.ShapeDtypeStruct((kw, d_short), F32)],
        scratch_shapes=[pltpu.VMEM((t + HALO, tc_ch), F32), pltpu.VMEM((t + HALO, tc_ch), F32),
                        pltpu.VMEM((8 * kw, tc_ch), F32)],
        compiler_params=_params(("parallel",)),
    )(z, z, z, dy, conv_w)


def adamw(name, w, m, v, contributions):
    r, c = w.shape
    n_slots = contributions.shape[0]
    tr = _tile(r, 256 if c <= 1024 else 128)

    def body(w_ref, m_ref, v_ref, g_ref, grad_ref, delta_ref, nm_ref, nv_ref):
        g = g_ref[0].astype(F32)
        for s in range(1, n_slots):
            g = g + g_ref[s].astype(F32)
        nm = ADAM_B1 * m_ref[...] + (1.0 - ADAM_B1) * g
        nv = ADAM_B2 * v_ref[...] + (1.0 - ADAM_B2) * (g * g)
        m_hat = nm / (1.0 - ADAM_B1 ** ADAM_STEP)
        v_hat = nv / (1.0 - ADAM_B2 ** ADAM_STEP)
        grad_ref[...] = g
        delta_ref[...] = -ADAM_LR * (m_hat / (jnp.sqrt(v_hat) + ADAM_EPS) + ADAM_WD * w_ref[...])
        nm_ref[...] = nm
        nv_ref[...] = nv

    blk = pl.BlockSpec((tr, c), lambda i: (i, 0))
    return pl.pallas_call(
        body, name=name, grid=(r // tr,),
        in_specs=[blk, blk, blk, pl.BlockSpec((n_slots, tr, c), lambda i: (0, i, 0))],
        out_specs=[blk] * 4, out_shape=[jax.ShapeDtypeStruct((r, c), F32)] * 4,
        compiler_params=_params(("parallel",)),
    )(w, m, v, contributions)


def _pad_rows(a, rows):
    return jnp.pad(a, ((0, rows - a.shape[0]), (0, 0)))


def kernel(x, mix_pre_g, mix_post_g, ffn_pre_g, ffn_post_g, ab_w_in, pool_w, pool_scale, conv_w, conv_b, conv_ln_g, conv_ln_b, ab_w_out, sc_w_in, sc_conv_w, sc_w_out, ffn_w1, ffn_w2, loss_target, m_mix_pre_g, m_mix_post_g, m_ffn_pre_g, m_ffn_post_g, m_ab_w_in, m_pool_w, m_pool_scale, m_conv_w, m_conv_b, m_conv_ln_g, m_conv_ln_b, m_ab_w_out, m_sc_w_in, m_sc_conv_w, m_sc_w_out, m_ffn_w1, m_ffn_w2, v_mix_pre_g, v_mix_post_g, v_ffn_pre_g, v_ffn_post_g, v_ab_w_in, v_pool_w, v_pool_scale, v_conv_w, v_conv_b, v_conv_ln_g, v_conv_ln_b, v_ab_w_out, v_sc_w_in, v_sc_conv_w, v_sc_w_out, v_ffn_w1, v_ffn_w2):
    t, d = x.shape[1], x.shape[2]
    d_pool = pool_scale.shape[1]
    d_conv = conv_b.shape[1]
    d_short = d
    ng, pg = pool_w.shape[1], pool_w.shape[3]
    kw, ks = conv_w.shape[1], sc_conv_w.shape[1]
    nb_ab, nb_sc, nb_ff = ab_w_in.shape[2], sc_w_in.shape[2], ffn_w1.shape[2]

    xs = x[0]
    target = loss_target[0]

    gathered = all_gather(
        [ab_w_in[0].astype(BF16), ab_w_out[0].astype(BF16), sc_w_in[0].astype(BF16), sc_w_out[0].astype(BF16),
         ffn_w1[0].astype(BF16), ffn_w1[1].astype(BF16), ffn_w2[0].astype(BF16), ffn_w2[1].astype(BF16),
         pool_w[0].astype(BF16), conv_w[0], sc_conv_w[0]],
        name="gather_weights")
    w_ab_in, w_ab_out, w_sc_in, w_sc_out, w_ff1_0, w_ff1_1, w_ff2_0, w_ff2_1, w_pool, w_conv, w_sconv = gathered
    w_ab_out = w_ab_out.reshape(d_pool + d_conv, d)
    w_sc_out = w_sc_out.reshape(d_short, d)
    w_ff1 = [w_ff1_0, w_ff1_1]
    w_ff2 = [w.reshape(-1, d) for w in (w_ff2_0, w_ff2_1)]
    w_pool = w_pool.transpose(1, 0, 2, 3).reshape(ng, pg, pg)
    w_conv = w_conv.transpose(1, 0, 2).reshape(kw, d_conv)
    w_sconv = w_sconv.transpose(1, 0, 2).reshape(ks, d_short)

    relu = lambda r: jnp.maximum(r, 0.0)
    square = lambda a: a * a
    relu2_bwd = lambda r, a: r * (2.0 * a.astype(F32))

    def row(vec, l):
        return vec[l:l + 1]

    h0 = norm_pre("norm_pre", xs, row(mix_pre_g, 0))
    z0 = mm_nn_blocked("ab_in", h0, w_ab_in, out_dtype=BF16)
    pooled, y_pool = pool_fwd("pool_fwd", z0, w_pool, pool_scale, d_pool)
    cv = conv_fwd("conv_fwd", z0, w_conv, conv_b, d_pool, d_conv)
    y_conv = ln_silu("ln_silu", cv, conv_ln_g, conv_ln_b)
    y0 = jnp.concatenate([y_pool, y_conv], axis=1)
    m0 = mm_nn("ab_out", y0, w_ab_out, out_dtype=F32)
    x1, h1 = post_pre("post_pre_0", xs, m0, row(mix_post_g, 0), row(ffn_pre_g, 0))
    a0 = mm_nn_blocked("ffn0_up", h1, w_ff1[0], out_dtype=BF16, epilogue=relu)
    f0 = mm_nn("ffn0_down", a0, w_ff2[0], out_dtype=F32, tk=1024, lhs_fn=square)
    x2, h2 = post_pre("post_pre_1", x1, f0, row(ffn_post_g, 0), row(mix_pre_g, 1))
    z1 = mm_nn_blocked("sc_in", h2, w_sc_in, out_dtype=BF16)
    y1 = short_fwd("short_fwd", z1, w_sconv, d_short)
    m1 = mm_nn("sc_out", y1, w_sc_out, out_dtype=F32)
    x3, h3 = post_pre("post_pre_2", x2, m1, row(mix_post_g, 1), row(ffn_pre_g, 1))
    a1 = mm_nn_blocked("ffn1_up", h3, w_ff1[1], out_dtype=BF16, epilogue=relu)
    f1 = mm_nn("ffn1_down", a1, w_ff2[1], out_dtype=F32, tk=1024, lhs_fn=square)
    dx4, df1, loss_part, dg_ffn_post1 = post_loss("post_loss", x3, f1, row(ffn_post_g, 1), target)
    loss = lax.psum(jnp.sum(loss_part) * (0.5 / d), ("x", "y", "c"))

    def ffn_bwd(tag, df, a, h, w1, w2):
        dw2 = mm_tn(tag + "_dw2", a, df, out_dtype=BF16, lhs_fn=square)
        dpre = mm_nt(tag + "_da", df, w2, out_dtype=BF16, extra=a, epilogue=relu2_bwd)
        dw1 = mm_tn_blocked(tag + "_dw1", h, dpre, nb_ff, out_dtype=BF16)
        dh = mm_nt_blocked(tag + "_dh", dpre, w1, out_dtype=BF16)
        return dw1, dw2, dh

    dw_ff1_1, dw_ff2_1, dh3 = ffn_bwd("ffn1", df1, a1, h3, w_ff1[1], w_ff2[1])
    dx3, dm1, dg_ffn_pre1, dg_mix_post1 = bwd_pre_post("bwd_3", dx4, x3, row(ffn_pre_g, 1), dh3, m1, row(mix_post_g, 1))

    dw_sc_out = mm_tn("sc_dwout", y1, dm1, out_dtype=BF16)
    dy1 = mm_nt("sc_dy", dm1, w_sc_out, out_dtype=BF16)
    db1, dcg1, du1, dw_sconv = short_bwd("short_bwd", z1, dy1, w_sconv, d_short)
    dz1 = jnp.concatenate([db1, dcg1, du1], axis=1)
    dw_sc_in = mm_tn_blocked("sc_dwin", h2, dz1, nb_sc, out_dtype=BF16)
    dh2 = mm_nt_blocked("sc_dh", dz1, w_sc_in, out_dtype=BF16)
    dx2, df0, dg_mix_pre1, dg_ffn_post0 = bwd_pre_post("bwd_2", dx3, x2, row(mix_pre_g, 1), dh2, f0, row(ffn_post_g, 0))

    dw_ff1_0, dw_ff2_0, dh1 = ffn_bwd("ffn0", df0, a0, h1, w_ff1[0], w_ff2[0])
    dx1, dm0, dg_ffn_pre0, dg_mix_post0 = bwd_pre_post("bwd_1", dx2, x1, row(ffn_pre_g, 0), dh1, m0, row(mix_post_g, 0))

    dw_ab_out = mm_tn("ab_dwout", y0, dm0, out_dtype=BF16)
    dy0 = mm_nt("ab_dy", dm0, w_ab_out, out_dtype=BF16)
    dcv, dg_ln_g, dg_ln_b = ln_silu_bwd("ln_silu_bwd", cv, conv_ln_g, conv_ln_b, dy0[:, d_pool:])
    dv, dgate, dw_conv, dg_conv_b = conv_bwd("conv_bwd", z0, dcv, w_conv, d_pool, d_conv)
    du0, dw_pool, dg_pool_scale = pool_bwd("pool_bwd", pooled, dy0[:, :d_pool], w_pool, pool_scale)
    dz0 = jnp.concatenate([du0, dv, dgate], axis=1)
    dw_ab_in = mm_tn_blocked("ab_dwin", h0, dz0, nb_ab, out_dtype=BF16)
    dh0 = mm_nt_blocked("ab_dh", dz0, w_ab_in, out_dtype=BF16)
    grad_x, dg_mix_pre0 = bwd_pre_final("bwd_0", dx1, xs, row(mix_pre_g, 0), dh0)

    shard_rows = d // N_DEV
    big = [
        [dw_ab_in],
        [dw_ab_out.reshape(N_DEV, (d_pool + d_conv) // N_DEV, d)],
        [dw_sc_in],
        [dw_sc_out.reshape(N_DEV, shard_rows, d)],
        [dw_ff1_0, dw_ff1_1],
        [dw_ff2_0.reshape(N_DEV, -1, d), dw_ff2_1.reshape(N_DEV, -1, d)],
    ]
    lanes = min(128, d_conv // N_DEV)
    small_parts = [
        dw_conv.reshape(kw, N_DEV, -1).transpose(1, 0, 2).reshape(N_DEV, -1, lanes),
        dw_sconv.reshape(ks, N_DEV, -1).transpose(1, 0, 2).reshape(N_DEV, -1, lanes),
        dw_pool.reshape(ng, N_DEV, pg // N_DEV, pg).transpose(1, 0, 2, 3).reshape(N_DEV, -1, lanes),
    ]
    small_rows = [p.shape[1] for p in small_parts]
    small_total = -(-sum(small_rows) // 8) * 8
    small = jnp.pad(jnp.concatenate(small_parts, axis=1), ((0, 0), (0, small_total - sum(small_rows)), (0, 0)))
    groups = big + [[small]]
    flat = [a for grp in groups for a in grp]
    from_sibling = exchange_pair(flat, name="exchange_pair")
    pair_sums = [pair_add(f"pair_add_{i}", g, s) for i, (g, s) in enumerate(zip(flat, from_sibling))]
    it = iter(pair_sums)
    landed = exchange_chips([[next(it) for _ in grp] for grp in groups], name="exchange_chips")

    fold = lambda a: jnp.sum(a, axis=0, keepdims=True)
    rep_rows = [fold(dg_mix_pre0), fold(dg_mix_pre1), fold(dg_mix_post0), fold(dg_mix_post1),
                fold(dg_ffn_pre0), fold(dg_ffn_pre1), fold(dg_ffn_post0), fold(dg_ffn_post1)]
    tail = jnp.concatenate([dg_pool_scale, dg_conv_b, fold(dg_ln_g), fold(dg_ln_b)], axis=1).reshape(-1, d)
    rep = _pad_rows(jnp.concatenate(rep_rows + [tail], axis=0), 16)
    rep_all = all_gather([rep], name="gather_small_grads")[0]

    def pack_rep(a_mix_pre, a_mix_post, a_ffn_pre, a_ffn_post, a_scale, a_b, a_g, a_lb):
        tail_ = jnp.concatenate([a_scale, a_b, a_g, a_lb], axis=1).reshape(-1, d)
        return _pad_rows(jnp.concatenate([a_mix_pre, a_mix_post, a_ffn_pre, a_ffn_post, tail_], axis=0), 16)

    def pack_small(a_conv, a_sconv, a_pool):
        parts = [a_conv[0].reshape(-1, lanes), a_sconv[0].reshape(-1, lanes), a_pool[0].reshape(-1, lanes)]
        return _pad_rows(jnp.concatenate(parts, axis=0), small_total)

    def upd(name, w, m, v, contrib):
        shape = w.shape
        r2 = lambda a: a.reshape(-1, shape[-1])
        outs = adamw(name, r2(w), r2(m), r2(v), contrib)
        return [o.reshape(shape) for o in outs]

    o_ab_in = upd("adam_ab_in", ab_w_in, m_ab_w_in, v_ab_w_in, landed[0])
    o_ab_out = upd("adam_ab_out", ab_w_out, m_ab_w_out, v_ab_w_out, landed[1])
    o_sc_in = upd("adam_sc_in", sc_w_in, m_sc_w_in, v_sc_w_in, landed[2])
    o_sc_out = upd("adam_sc_out", sc_w_out, m_sc_w_out, v_sc_w_out, landed[3])
    o_ff1 = upd("adam_ffn_w1", ffn_w1, m_ffn_w1, v_ffn_w1, landed[4])
    o_ff2 = upd("adam_ffn_w2", ffn_w2, m_ffn_w2, v_ffn_w2, landed[5])

    o_small = adamw("adam_small", pack_small(conv_w, sc_conv_w, pool_w), pack_small(m_conv_w, m_sc_conv_w, m_pool_w),
                    pack_small(v_conv_w, v_sc_conv_w, v_pool_w), landed[6])
    o_rep = adamw("adam_replicated",
                  pack_rep(mix_pre_g, mix_post_g, ffn_pre_g, ffn_post_g, pool_scale, conv_b, conv_ln_g, conv_ln_b),
                  pack_rep(m_mix_pre_g, m_mix_post_g, m_ffn_pre_g, m_ffn_post_g, m_pool_scale, m_conv_b, m_conv_ln_g, m_conv_ln_b),
                  pack_rep(v_mix_pre_g, v_mix_post_g, v_ffn_pre_g, v_ffn_post_g, v_pool_scale, v_conv_b, v_conv_ln_g, v_conv_ln_b),
                  rep_all)

    def unpack_small(o):
        r0, r1 = small_rows[0], small_rows[0] + small_rows[1]
        return (o[:r0].reshape(conv_w.shape), o[r0:r1].reshape(sc_conv_w.shape),
                o[r1:r1 + small_rows[2]].reshape(pool_w.shape))

    def unpack_rep(o):
        tail_ = o[8:8 + tail.shape[0]].reshape(1, -1)
        n1 = d_pool
        return dict(mix_pre_g=o[0:2], mix_post_g=o[2:4], ffn_pre_g=o[4:6], ffn_post_g=o[6:8],
                    pool_scale=tail_[:, :n1], conv_b=tail_[:, n1:n1 + d_conv],
                    conv_ln_g=tail_[:, n1 + d_conv:n1 + 2 * d_conv], conv_ln_b=tail_[:, n1 + 2 * d_conv:n1 + 3 * d_conv])

    results = []
    for kind in range(4):
        rep_o = unpack_rep(o_rep[kind])
        s_conv, s_sconv, s_pool = unpack_small(o_small[kind])
        results.append([
            rep_o["mix_pre_g"], rep_o["mix_post_g"], rep_o["ffn_pre_g"], rep_o["ffn_post_g"],
            o_ab_in[kind], s_pool, rep_o["pool_scale"], s_conv, rep_o["conv_b"], rep_o["conv_ln_g"], rep_o["conv_ln_b"],
            o_ab_out[kind], o_sc_in[kind], s_sconv, o_sc_out[kind], o_ff1[kind], o_ff2[kind]])

    return (loss, grad_x[None], *results[0], *results[1], *results[2], *results[3])
```

```python
import functools

import jax
import jax.numpy as jnp
from jax import lax
from jax.experimental import pallas as pl
from jax.experimental.pallas import tpu as pltpu

F32 = jnp.float32
BF16 = jnp.bfloat16
MESH = pl.DeviceIdType.MESH
ANY = pl.BlockSpec(memory_space=pl.ANY)

NORM_EPS = 1e-6
POOL_WINDOWS = (2, 4, 8, 16)
MAX_POOL_WINDOW = 16
ADAM_LR = 0.001
ADAM_B1 = 0.9
ADAM_B2 = 0.999
ADAM_EPS = 1e-08
ADAM_WD = 0.01
ADAM_STEP = 10

N_DEV = 8
VMEM_LIMIT = 48 * 1024 * 1024
ROW_TILE = 256
CHANNEL_TILE = 256
TIME_CHUNK = 64
HALO = 32

NN = (((1,), (0,)), ((), ()))
NT = (((1,), (1,)), ((), ()))
TN = (((0,), (0,)), ((), ()))


def _params(sem):
    return pltpu.CompilerParams(dimension_semantics=sem, vmem_limit_bytes=VMEM_LIMIT)


def _place():
    x, y, c = lax.axis_index("x"), lax.axis_index("y"), lax.axis_index("c")
    return x, y, c


def _slot(px, py, pc):
    return 4 * px + 2 * py + pc


def all_gather(arrs, name):
    n = len(arrs)

    def body(*refs):
        ins, outs = refs[:n], refs[n:2 * n]
        send_sems, recv_sems, local_sems = refs[2 * n:]
        x, y, c = _place()
        me = _slot(x, y, c)
        sibling = (x, y, 1 - c)
        chips = [(1 - x, y), (x, 1 - y), (1 - x, 1 - y)]

        def copy(a, k, slot, to, src=None):
            return pltpu.make_async_remote_copy(
                src_ref=outs[a].at[slot] if src is None else src, dst_ref=outs[a].at[slot],
                send_sem=send_sems.at[a, k], recv_sem=recv_sems.at[a, k], device_id=to, device_id_type=MESH)

        started = []
        for a in range(n):
            mine = pltpu.make_async_copy(ins[a], outs[a].at[me], local_sems.at[a])
            mine.start()
            started.append(mine)
        sends = []
        for a in range(n):
            first = [copy(a, 0, me, sibling, src=ins[a])]
            first += [copy(a, 1 + j, me, (px, py, c), src=ins[a]) for j, (px, py) in enumerate(chips)]
            for cp in first:
                cp.start()
            sends += first
        for a in range(n):
            for j, (px, py) in enumerate(chips):
                copy(a, 1 + j, _slot(px, py, c), (x, y, c)).wait_recv()
                fwd = copy(a, 4 + j, _slot(px, py, c), sibling)
                fwd.start()
                sends.append(fwd)
        for a in range(n):
            copy(a, 0, _slot(x, y, 1 - c), (x, y, c)).wait_recv()
            for j, (px, py) in enumerate(chips):
                copy(a, 4 + j, _slot(px, py, 1 - c), (x, y, c)).wait_recv()
        for cp in sends:
            cp.wait_send()
        for mine in started:
            mine.wait()

    outs = pl.pallas_call(
        body, name=name,
        out_shape=[jax.ShapeDtypeStruct((N_DEV,) + a.shape, a.dtype) for a in arrs],
        in_specs=[ANY] * n, out_specs=[ANY] * n,
        scratch_shapes=[pltpu.SemaphoreType.DMA((n, 7)), pltpu.SemaphoreType.DMA((n, 7)),
                        pltpu.SemaphoreType.DMA((n,))],
    )(*arrs)
    return list(outs)


HBM = pl.BlockSpec(memory_space=pltpu.HBM)
SEM = pl.BlockSpec(memory_space=pltpu.SEMAPHORE)
EFFECT = pltpu.SideEffectType.DATAFLOW_SIDE_EFFECTING
TOKEN = jax.ShapeDtypeStruct((8, 128), F32)


def _in_hbm(a):
    return pltpu.with_memory_space_constraint(a, pltpu.HBM)


def _first_hop_peers(x, y, c):
    return [(x, y, 1 - c), (1 - x, y, c), (x, 1 - y, c), (1 - x, 1 - y, c)]


def gather_start(shards, name):
    n = len(shards)

    def body(*refs):
        ins, lands = refs[:n], refs[n:2 * n]
        sends, recvs = refs[2 * n:3 * n], refs[3 * n:4 * n]
        token = refs[-1]
        x, y, c = _place()
        me = _slot(x, y, c)
        for a in range(n):
            for k, to in enumerate(_first_hop_peers(x, y, c)):
                pltpu.make_async_remote_copy(
                    src_ref=ins[a], dst_ref=lands[a].at[me], send_sem=sends[a].at[k], recv_sem=recvs[a].at[k],
                    device_id=to, device_id_type=MESH).start()
        token[...] = jnp.zeros_like(token)

    lands = [lax.empty((N_DEV,) + s.shape, s.dtype) for s in shards]
    outs = pl.pallas_call(
        body, name=name,
        out_shape=([pltpu.SemaphoreType.DMA((4,))] * (2 * n) + [pltpu.HBM(s.shape, s.dtype) for s in shards]
                   + [pltpu.HBM(l.shape, l.dtype) for l in lands] + [TOKEN]),
        in_specs=[HBM] * (2 * n),
        out_specs=[SEM] * (2 * n) + [HBM] * (2 * n) + [pl.BlockSpec(memory_space=pltpu.VMEM)],
        input_output_aliases={i: 2 * n + i for i in range(2 * n)},
        compiler_params=pltpu.CompilerParams(has_side_effects=EFFECT),
    )(*[_in_hbm(s) for s in shards], *[_in_hbm(l) for l in lands])
    per_shard = [(outs[a], outs[n + a], outs[2 * n + a], outs[3 * n + a]) for a in range(n)]
    return per_shard, outs[-1]


def gather_wait(name, started, after):
    send_sems, recv_sems, shard, land = started

    def body(shard_ref, land_ref, sends, recvs, after_ref, shard_out, land_out):
        x, y, c = _place()
        for k, (px, py, pc) in enumerate(_first_hop_peers(x, y, c)):
            cp = pltpu.make_async_remote_copy(
                src_ref=shard_ref, dst_ref=land_ref.at[_slot(px, py, pc)], send_sem=sends.at[k], recv_sem=recvs.at[k],
                device_id=(px, py, pc), device_id_type=MESH)
            cp.wait_send()
            cp.wait_recv()

    return pl.pallas_call(
        body, name=name,
        out_shape=(pltpu.HBM(shard.shape, shard.dtype), pltpu.HBM(land.shape, land.dtype)),
        in_specs=(HBM, HBM, SEM, SEM, ANY), out_specs=(HBM, HBM), input_output_aliases={0: 0, 1: 1},
        compiler_params=pltpu.CompilerParams(has_side_effects=EFFECT),
    )(shard, land, send_sems, recv_sems, after)


def gather_finish(waited, name):
    n = len(waited)

    def body(*refs):
        shards, lands = refs[:n], refs[2 * n:3 * n]
        send_sems, recv_sems, local_sems = refs[3 * n:]
        x, y, c = _place()
        chips = [(1 - x, y), (x, 1 - y), (1 - x, 1 - y)]
        local, copies = [], []
        for a in range(n):
            cp = pltpu.make_async_copy(shards[a], lands[a].at[_slot(x, y, c)], local_sems.at[a])
            cp.start()
            local.append(cp)
            for k, (px, py) in enumerate(chips):
                cp = pltpu.make_async_remote_copy(
                    src_ref=lands[a].at[_slot(px, py, c)], dst_ref=lands[a].at[_slot(px, py, c)],
                    send_sem=send_sems.at[a, k], recv_sem=recv_sems.at[a, k],
                    device_id=(x, y, 1 - c), device_id_type=MESH)
                cp.start()
                copies.append(cp)
        for cp in copies:
            cp.wait()
        for cp in local:
            cp.wait()

    outs = pl.pallas_call(
        body, name=name,
        out_shape=[jax.ShapeDtypeStruct(l.shape, l.dtype) for _, l in waited],
        in_specs=[ANY] * (2 * n), out_specs=[ANY] * n,
        input_output_aliases={n + a: a for a in range(n)},
        scratch_shapes=[pltpu.SemaphoreType.DMA((n, 3)), pltpu.SemaphoreType.DMA((n, 3)),
                        pltpu.SemaphoreType.DMA((n,))],
    )(*[s for s, _ in waited], *[l for _, l in waited])
    return list(outs)


CHIPS = [(0, 0), (0, 1), (1, 0), (1, 1)]
N_CHIP = len(CHIPS)


def exchange_pair(arrs, name):
    n = len(arrs)

    def body(*refs):
        ins, outs = refs[:n], refs[n:2 * n]
        send_sems, recv_sems = refs[2 * n:]
        x, y, c = _place()
        copies = []
        for a in range(n):
            for q, (qx, qy) in enumerate(CHIPS):
                cp = pltpu.make_async_remote_copy(
                    src_ref=ins[a].at[_slot(qx, qy, 1 - c)], dst_ref=outs[a].at[q],
                    send_sem=send_sems.at[a, q], recv_sem=recv_sems.at[a, q],
                    device_id=(x, y, 1 - c), device_id_type=MESH)
                cp.start()
                copies.append(cp)
        for cp in copies:
            cp.wait()

    outs = pl.pallas_call(
        body, name=name,
        out_shape=[jax.ShapeDtypeStruct((N_CHIP,) + a.shape[1:], a.dtype) for a in arrs],
        in_specs=[ANY] * n, out_specs=[ANY] * n,
        scratch_shapes=[pltpu.SemaphoreType.DMA((n, N_CHIP)), pltpu.SemaphoreType.DMA((n, N_CHIP))],
    )(*arrs)
    return list(outs)


def pair_add(name, g, from_sibling):
    _, r, c_dim = g.shape
    tr = _tile(r, 256)
    core = lax.axis_index("c").astype(jnp.int32).reshape(1)

    def body(core_ref, g_ref, s_ref, o_ref):
        o_ref[...] = (g_ref[...].astype(F32) + s_ref[...].astype(F32)).astype(o_ref.dtype)

    return pl.pallas_call(
        body, name=name,
        grid_spec=pltpu.PrefetchScalarGridSpec(
            num_scalar_prefetch=1, grid=(N_CHIP, r // tr),
            in_specs=[pl.BlockSpec((None, None, tr, c_dim), lambda q, i, core_ref: (q, core_ref[0], i, 0)),
                      pl.BlockSpec((None, tr, c_dim), lambda q, i, core_ref: (q, i, 0))],
            out_specs=pl.BlockSpec((None, tr, c_dim), lambda q, i, core_ref: (q, i, 0))),
        out_shape=jax.ShapeDtypeStruct((N_CHIP, r, c_dim), g.dtype),
        compiler_params=_params(("parallel", "parallel")),
    )(core, g.reshape(N_CHIP, 2, r, c_dim), from_sibling)


def exchange_chips(groups, name):
    arrs = [a for grp in groups for a in grp]
    n = len(arrs)
    where = []
    for o, grp in enumerate(groups):
        off = 0
        for a in grp:
            where.append((o, off))
            off += a.shape[1]
    flips = [(1, 0), (0, 1), (1, 1)]

    def body(*refs):
        ins, outs = refs[:n], refs[n:n + len(groups)]
        send_sems, recv_sems, local_sems = refs[n + len(groups):]
        x, y, c = _place()
        my_chip = 2 * x + y

        def landing(a, slot):
            o, off = where[a]
            return outs[o].at[slot, pl.ds(off, arrs[a].shape[1])]

        local = []
        for a in range(n):
            cp = pltpu.make_async_copy(ins[a].at[my_chip], landing(a, my_chip), local_sems.at[a])
            cp.start()
            local.append(cp)
        copies = []
        for a in range(n):
            for k, (dx, dy) in enumerate(flips):
                px = 1 - x if dx else x
                py = 1 - y if dy else y
                cp = pltpu.make_async_remote_copy(
                    src_ref=ins[a].at[2 * px + py], dst_ref=landing(a, my_chip),
                    send_sem=send_sems.at[a, k], recv_sem=recv_sems.at[a, k],
                    device_id=(px, py, c), device_id_type=MESH)
                cp.start()
                copies.append(cp)
        for cp in copies:
            cp.wait()
        for cp in local:
            cp.wait()

    outs = pl.pallas_call(
        body, name=name,
        out_shape=[jax.ShapeDtypeStruct((N_CHIP, sum(a.shape[1] for a in grp), grp[0].shape[2]), grp[0].dtype)
                   for grp in groups],
        in_specs=[ANY] * n, out_specs=[ANY] * len(groups),
        scratch_shapes=[pltpu.SemaphoreType.DMA((n, 3)), pltpu.SemaphoreType.DMA((n, 3)),
                        pltpu.SemaphoreType.DMA((n,))],
    )(*arrs)
    return list(outs)


def _matmul(name, lhs, rhs, *, out_shape, out_dtype, grid, lhs_spec, rhs_spec, out_spec, dims, acc_shape,
            lhs_fn=None, extra=(), extra_specs=(), epilogue=None):
    nk = grid[2]
    n_extra = len(extra)

    def body(*refs):
        lhs_ref, rhs_ref = refs[0], refs[1]
        extra_refs = refs[2:2 + n_extra]
        out_ref = refs[2 + n_extra]
        a = lhs_ref[...]
        if lhs_fn is not None:
            a = lhs_fn(a)
        p = lax.dot_general(a, rhs_ref[...], dims, preferred_element_type=F32)

        def finish(r):
            if epilogue is not None:
                r = epilogue(r, *[e[...] for e in extra_refs])
            out_ref[...] = r.astype(out_dtype)

        if nk == 1:
            finish(p)
        else:
            acc_ref = refs[3 + n_extra]
            k = pl.program_id(2)

            @pl.when(k == 0)
            def _():
                acc_ref[...] = p

            @pl.when(k > 0)
            def _():
                acc_ref[...] += p

            @pl.when(k == nk - 1)
            def _():
                finish(acc_ref[...])

    return pl.pallas_call(
        body, name=name, grid=grid,
        out_shape=jax.ShapeDtypeStruct(out_shape, out_dtype),
        in_specs=[lhs_spec, rhs_spec, *extra_specs], out_specs=out_spec,
        scratch_shapes=[pltpu.VMEM(acc_shape, F32)] if nk > 1 else [],
        compiler_params=_params(("parallel", "parallel", "arbitrary")),
    )(lhs, rhs, *extra)


def _tile(n, want):
    return want if n % want == 0 else n


def mm_nn(name, x, w, *, out_dtype, tn=512, tk=None, lhs_fn=None, epilogue=None):
    t, kdim = x.shape
    n = w.shape[1]
    tn = _tile(n, tn)
    tk = kdim if tk is None else _tile(kdim, tk)
    return _matmul(
        name, x, w, out_shape=(t, n), out_dtype=out_dtype, grid=(1, n // tn, kdim // tk),
        lhs_spec=pl.BlockSpec((t, tk), lambda i, j, k: (i, k)),
        rhs_spec=pl.BlockSpec((tk, tn), lambda i, j, k: (k, j)),
        out_spec=pl.BlockSpec((t, tn), lambda i, j, k: (i, j)),
        dims=NN, acc_shape=(t, tn), lhs_fn=lhs_fn, epilogue=epilogue)


def mm_nn_blocked(name, x, w, *, out_dtype, epilogue=None):
    t, kdim = x.shape
    nb = w.shape[2]
    tn = nb // 2 if nb >= 1024 else nb
    sub = nb // tn
    return _matmul(
        name, x, w, out_shape=(t, N_DEV * nb), out_dtype=out_dtype, grid=(1, N_DEV * sub, 1),
        lhs_spec=pl.BlockSpec((t, kdim), lambda i, j, k: (i, k)),
        rhs_spec=pl.BlockSpec((None, kdim, tn), lambda i, j, k: (j // sub, k, j % sub)),
        out_spec=pl.BlockSpec((t, tn), lambda i, j, k: (i, j)),
        dims=NN, acc_shape=(t, tn), epilogue=epilogue)


def mm_nt(name, dy, w, *, out_dtype, tn=512, extra=None, epilogue=None):
    t, n = dy.shape
    kdim = w.shape[0]
    tn = _tile(kdim, tn)
    extra_arrs = () if extra is None else (extra,)
    extra_specs = () if extra is None else (pl.BlockSpec((t, tn), lambda i, j, k: (i, j)),)
    return _matmul(
        name, dy, w, out_shape=(t, kdim), out_dtype=out_dtype, grid=(1, kdim // tn, 1),
        lhs_spec=pl.BlockSpec((t, n), lambda i, j, k: (i, k)),
        rhs_spec=pl.BlockSpec((tn, n), lambda i, j, k: (j, k)),
        out_spec=pl.BlockSpec((t, tn), lambda i, j, k: (i, j)),
        dims=NT, acc_shape=(t, tn), extra=extra_arrs, extra_specs=extra_specs, epilogue=epilogue)


def mm_nt_blocked(name, dz, w, *, out_dtype, tn=512):
    t = dz.shape[0]
    kdim, nb = w.shape[1], w.shape[2]
    tn = _tile(kdim, tn)
    return _matmul(
        name, dz, w, out_shape=(t, kdim), out_dtype=out_dtype, grid=(1, kdim // tn, N_DEV),
        lhs_spec=pl.BlockSpec((t, nb), lambda i, j, k: (i, k)),
        rhs_spec=pl.BlockSpec((None, tn, nb), lambda i, j, k: (k, j, 0)),
        out_spec=pl.BlockSpec((t, tn), lambda i, j, k: (i, j)),
        dims=NT, acc_shape=(t, tn))


def mm_tn(name, x, dy, *, out_dtype, tk=1024, tn=1024, lhs_fn=None):
    t, kdim = x.shape
    n = dy.shape[1]
    tk, tn = _tile(kdim, tk), _tile(n, tn)
    return _matmul(
        name, x, dy, out_shape=(kdim, n), out_dtype=out_dtype, grid=(kdim // tk, n // tn, 1),
        lhs_spec=pl.BlockSpec((t, tk), lambda i, j, k: (k, i)),
        rhs_spec=pl.BlockSpec((t, tn), lambda i, j, k: (k, j)),
        out_spec=pl.BlockSpec((tk, tn), lambda i, j, k: (i, j)),
        dims=TN, acc_shape=(tk, tn), lhs_fn=lhs_fn)


def mm_tn_blocked(name, x, dz, nb, *, out_dtype, tk=1024):
    t, kdim = x.shape
    tk = _tile(kdim, tk)
    return _matmul(
        name, x, dz, out_shape=(N_DEV, kdim, nb), out_dtype=out_dtype, grid=(kdim // tk, N_DEV, 1),
        lhs_spec=pl.BlockSpec((t, tk), lambda i, j, k: (k, i)),
        rhs_spec=pl.BlockSpec((t, nb), lambda i, j, k: (k, j)),
        out_spec=pl.BlockSpec((None, tk, nb), lambda i, j, k: (j, i, 0)),
        dims=TN, acc_shape=(tk, nb))


def _rstd(v):
    return lax.rsqrt(jnp.mean(v * v, axis=-1, keepdims=True) + NORM_EPS)


def _rms_bwd(v, g, dy):
    r = _rstd(v)
    vhat = v * r
    dvh = dy * g
    dv = r * (dvh - vhat * jnp.mean(dvh * vhat, axis=-1, keepdims=True))
    return dv, dy * vhat


def _fold8(v):
    rows, n = v.shape
    return jnp.sum(v.reshape(rows // 8, 8, n), axis=0)


def _fold_lanes(v):
    out = v[:, 0:128]
    for i in range(1, v.shape[1] // 128):
        out = out + v[:, 128 * i:128 * (i + 1)]
    return out


def _accumulate(ref, v):
    i = pl.program_id(0)

    @pl.when(i == 0)
    def _():
        ref[...] = v

    @pl.when(i > 0)
    def _():
        ref[...] += v


def _row_call(body, name, t, ins, row_in, outs, acc_outs=(), tr=ROW_TILE):
    tr = _tile(t, tr)
    in_specs = [pl.BlockSpec((tr, a.shape[1]), lambda i: (i, 0)) if tiled
                else pl.BlockSpec(a.shape, lambda i: (0, 0)) for a, tiled in zip(ins, row_in)]
    out_specs = [pl.BlockSpec((tr, n), lambda i: (i, 0)) for n, _ in outs]
    out_specs += [pl.BlockSpec((8, n), lambda i: (0, 0)) for n in acc_outs]
    out_shape = [jax.ShapeDtypeStruct((t, n), dt) for n, dt in outs]
    out_shape += [jax.ShapeDtypeStruct((8, n), F32) for n in acc_outs]
    return pl.pallas_call(
        body, name=name, grid=(t // tr,), in_specs=in_specs, out_specs=out_specs, out_shape=out_shape,
        compiler_params=_params(("arbitrary",) if acc_outs else ("parallel",)),
    )(*ins)


def norm_pre(name, x, g):
    t, d = x.shape

    def body(x_ref, g_ref, h_ref):
        v = x_ref[...]
        h_ref[...] = (v * _rstd(v) * g_ref[...]).astype(BF16)

    return _row_call(body, name, t, [x, g], [True, False], [(d, BF16)])[0]


def post_pre(name, x, m, g_post, g_pre):
    t, d = x.shape

    def body(x_ref, m_ref, gp_ref, gn_ref, xo_ref, h_ref):
        mv = m_ref[...]
        xn = x_ref[...] + mv * _rstd(mv) * gp_ref[...]
        xo_ref[...] = xn
        h_ref[...] = (xn * _rstd(xn) * gn_ref[...]).astype(BF16)

    return _row_call(body, name, t, [x, m, g_post, g_pre], [True, True, False, False], [(d, F32), (d, BF16)])


def post_loss(name, x, f, g_post, target):
    t, d = x.shape

    def body(x_ref, f_ref, g_ref, t_ref, dx_ref, df_ref, loss_ref, dg_ref):
        fv = f_ref[...]
        g = g_ref[...]
        out = x_ref[...] + fv * _rstd(fv) * g
        err = out - t_ref[...]
        dx = err * (1.0 / d)
        dx_ref[...] = dx
        dfv, dg_rows = _rms_bwd(fv, g, dx)
        df_ref[...] = dfv.astype(BF16)
        _accumulate(loss_ref, _fold8(_fold_lanes(err * err)))
        _accumulate(dg_ref, _fold8(dg_rows))

    return _row_call(body, name, t, [x, f, g_post, target], [True, True, False, True],
                     [(d, F32), (d, BF16)], acc_outs=(128, d))


def bwd_pre_post(name, dx_out, x_in, g_pre, dh, f_prev, g_post_prev):
    t, d = x_in.shape

    def body(dxo_ref, x_ref, gpre_ref, dh_ref, f_ref, gpost_ref, dxi_ref, df_ref, dgpre_ref, dgpost_ref):
        dxv, dgpre_rows = _rms_bwd(x_ref[...], gpre_ref[...], dh_ref[...].astype(F32))
        dxi = dxo_ref[...] + dxv
        dxi_ref[...] = dxi
        dfv, dgpost_rows = _rms_bwd(f_ref[...], gpost_ref[...], dxi)
        df_ref[...] = dfv.astype(BF16)
        _accumulate(dgpre_ref, _fold8(dgpre_rows))
        _accumulate(dgpost_ref, _fold8(dgpost_rows))

    return _row_call(body, name, t, [dx_out, x_in, g_pre, dh, f_prev, g_post_prev],
                     [True, True, False, True, True, False], [(d, F32), (d, BF16)], acc_outs=(d, d))


def bwd_pre_final(name, dx_out, x_in, g_pre, dh):
    t, d = x_in.shape

    def body(dxo_ref, x_ref, gpre_ref, dh_ref, dxi_ref, dgpre_ref):
        dxv, dgpre_rows = _rms_bwd(x_ref[...], gpre_ref[...], dh_ref[...].astype(F32))
        dxi_ref[...] = dxo_ref[...] + dxv
        _accumulate(dgpre_ref, _fold8(dgpre_rows))

    return _row_call(body, name, t, [dx_out, x_in, g_pre, dh], [True, True, False, True], [(d, F32)], acc_outs=(d,))


def _layer_norm_parts(cv):
    mu = jnp.mean(cv, axis=-1, keepdims=True)
    xc = cv - mu
    rstd = lax.rsqrt(jnp.mean(xc * xc, axis=-1, keepdims=True) + NORM_EPS)
    return xc * rstd, rstd


def ln_silu(name, cv, g, b):
    t, n = cv.shape

    def body(c_ref, g_ref, b_ref, y_ref):
        chat, _ = _layer_norm_parts(c_ref[...])
        ln = chat * g_ref[...] + b_ref[...]
        y_ref[...] = (ln * jax.nn.sigmoid(ln)).astype(BF16)

    return _row_call(body, name, t, [cv, g, b], [True, False, False], [(n, BF16)])[0]


def ln_silu_bwd(name, cv, g, b, dy):
    t, n = cv.shape

    def body(c_ref, g_ref, b_ref, dy_ref, dc_ref, dg_ref, db_ref):
        chat, rstd = _layer_norm_parts(c_ref[...])
        g = g_ref[...]
        ln = chat * g + b_ref[...]
        s = jax.nn.sigmoid(ln)
        dln = dy_ref[...].astype(F32) * (s * (1.0 + ln * (1.0 - s)))
        dchat = dln * g
        dc_ref[...] = rstd * (dchat - jnp.mean(dchat, axis=-1, keepdims=True)
                              - chat * jnp.mean(dchat * chat, axis=-1, keepdims=True))
        _accumulate(dg_ref, _fold8(dln * chat))
        _accumulate(db_ref, _fold8(dln))

    return _row_call(body, name, t, [cv, g, b, dy], [True, False, False, True], [(n, F32)], acc_outs=(n, n))


def _chunks(t, fn, tc=TIME_CHUNK):
    tc = _tile(t, tc)

    def step(i, carry):
        fn(pl.multiple_of(i * tc, tc), tc)
        return carry

    lax.fori_loop(0, t // tc, step, 0)


def _taps(window, w_ref, offsets, tc):
    acc = None
    for k, off in enumerate(offsets):
        term = w_ref[k:k + 1, :] * window[off:off + tc, :]
        acc = term if acc is None else acc + term
    return acc


def _window_sums(win, tc, causal):
    sums = []
    cur, rows, step = win, tc + HALO, 1
    for _ in POOL_WINDOWS:
        rows -= 8
        if causal:
            cur = cur[8:8 + rows, :] + cur[8 - step:8 - step + rows, :]
            sums.append(cur[rows - tc:rows, :])
        else:
            cur = cur[0:rows, :] + cur[step:step + rows, :]
            sums.append(cur[0:tc, :])
        step *= 2
    return sums


def _pick(vals, g):
    out = vals[-1]
    for i in range(len(vals) - 2, -1, -1):
        out = jnp.where(g == i, vals[i], out)
    return out


def _pool_count(s, tc, g):
    t1 = (lax.broadcasted_iota(jnp.int32, (tc, 1), 0) + (s + 1)).astype(F32)
    width = _pick([float(w) for w in POOL_WINDOWS], g)
    return jnp.minimum(t1, width)


def pool_fwd(name, z, pool_w, pool_scale, d_pool):
    t = z.shape[0]
    ng, pg = pool_w.shape[0], pool_w.shape[1]

    def body(u_ref, w_ref, s_ref, pooled_ref, y_ref, pad):
        g = pl.program_id(0)
        pad[pl.ds(0, HALO), :] = jnp.zeros((HALO, pg), F32)

        def fill(s, tc):
            pad[pl.ds(HALO + s, tc), :] = u_ref[pl.ds(s, tc), :].astype(F32)

        def chunk(s, tc):
            win = pad[pl.ds(s, tc + HALO), :]
            total = _pick(_window_sums(win, tc, causal=True), g)
            pooled = total / _pool_count(s, tc, g) - win[HALO:HALO + tc, :]
            pooled_ref[pl.ds(s, tc), :] = pooled.astype(BF16)

        _chunks(t, fill)
        _chunks(t, chunk)
        mixed = jnp.dot(pooled_ref[...], w_ref[...], preferred_element_type=F32)
        y_ref[...] = (mixed * s_ref[...]).astype(BF16)

    col = pl.BlockSpec((t, pg), lambda g: (0, g))
    return pl.pallas_call(
        body, name=name, grid=(ng,),
        in_specs=[col, pl.BlockSpec((None, pg, pg), lambda g: (g, 0, 0)), pl.BlockSpec((1, pg), lambda g: (0, g))],
        out_specs=[col, col],
        out_shape=[jax.ShapeDtypeStruct((t, d_pool), BF16), jax.ShapeDtypeStruct((t, d_pool), BF16)],
        scratch_shapes=[pltpu.VMEM((t + HALO, pg), F32)],
        compiler_params=_params(("parallel",)),
    )(z, pool_w, pool_scale)


def pool_bwd(name, pooled, dy, pool_w, pool_scale):
    t, d_pool = pooled.shape
    ng, pg = pool_w.shape[0], pool_w.shape[1]

    def body(p_ref, dy_ref, w_ref, s_ref, du_ref, dw_ref, ds_ref, pad):
        g = pl.program_id(0)
        w = w_ref[...]
        dyv = dy_ref[...].astype(F32)
        mixed = jnp.dot(p_ref[...], w, preferred_element_type=F32)
        ds_ref[...] = jnp.sum(dyv * mixed, axis=0, keepdims=True)
        dmixed = (dyv * s_ref[...]).astype(BF16)
        dw_ref[...] = lax.dot_general(p_ref[...], dmixed, TN, preferred_element_type=F32)
        pad[...] = jnp.zeros((t + HALO, pg), F32)
        pad[pl.ds(0, t), :] = lax.dot_general(dmixed, w, NT, preferred_element_type=F32)

        def scale(s, tc):
            pad[pl.ds(s, tc), :] = pad[pl.ds(s, tc), :] / _pool_count(s, tc, g)

        def chunk(s, tc):
            win = pad[pl.ds(s, tc + HALO), :]
            total = _pick(_window_sums(win, tc, causal=False), g)
            du_ref[pl.ds(s, tc), :] = (total - win[0:tc, :] * _pool_count(s, tc, g)).astype(BF16)

        _chunks(t, scale)
        _chunks(t, chunk)

    col = pl.BlockSpec((t, pg), lambda g: (0, g))
    vec = pl.BlockSpec((1, pg), lambda g: (0, g))
    mat = pl.BlockSpec((None, pg, pg), lambda g: (g, 0, 0))
    return pl.pallas_call(
        body, name=name, grid=(ng,),
        in_specs=[col, col, mat, vec], out_specs=[col, mat, vec],
        out_shape=[jax.ShapeDtypeStruct((t, d_pool), BF16), jax.ShapeDtypeStruct((ng, pg, pg), F32),
                   jax.ShapeDtypeStruct((1, d_pool), F32)],
        scratch_shapes=[pltpu.VMEM((t + HALO, pg), F32)],
        compiler_params=_params(("parallel",)),
    )(pooled, dy, pool_w, pool_scale)


def conv_fwd(name, z, conv_w, conv_b, d_pool, d_conv):
    t = z.shape[0]
    kw = conv_w.shape[0]
    tc_ch = _tile(d_conv, CHANNEL_TILE)
    v0, g0 = d_pool // tc_ch, (d_pool + d_conv) // tc_ch

    def body(v_ref, g_ref, w_ref, b_ref, c_ref, pad):
        pad[pl.ds(0, HALO), :] = jnp.zeros((HALO, tc_ch), F32)

        def fill(s, tc):
            pad[pl.ds(HALO + s, tc), :] = v_ref[pl.ds(s, tc), :].astype(F32) * jax.nn.sigmoid(g_ref[pl.ds(s, tc), :].astype(F32))

        def chunk(s, tc):
            win = pad[pl.ds(s, tc + HALO), :]
            c_ref[pl.ds(s, tc), :] = _taps(win, w_ref, [HALO - (kw - 1) + k for k in range(kw)], tc) + b_ref[...]

        _chunks(t, fill)
        _chunks(t, chunk)

    return pl.pallas_call(
        body, name=name, grid=(d_conv // tc_ch,),
        in_specs=[pl.BlockSpec((t, tc_ch), lambda j: (0, v0 + j)), pl.BlockSpec((t, tc_ch), lambda j: (0, g0 + j)),
                  pl.BlockSpec((kw, tc_ch), lambda j: (0, j)), pl.BlockSpec((1, tc_ch), lambda j: (0, j))],
        out_specs=pl.BlockSpec((t, tc_ch), lambda j: (0, j)),
        out_shape=jax.ShapeDtypeStruct((t, d_conv), F32),
        scratch_shapes=[pltpu.VMEM((t + HALO, tc_ch), F32)],
        compiler_params=_params(("parallel",)),
    )(z, z, conv_w, conv_b)


def conv_bwd(name, z, dc, conv_w, d_pool, d_conv):
    t = z.shape[0]
    kw = conv_w.shape[0]
    tc_ch = _tile(d_conv, CHANNEL_TILE)
    v0, g0 = d_pool // tc_ch, (d_pool + d_conv) // tc_ch

    def body(v_ref, g_ref, dc_ref, w_ref, dv_ref, dg_ref, dw_ref, db_ref, pad_a, pad_dc, acc_w, acc_b):
        pad_a[pl.ds(0, HALO), :] = jnp.zeros((HALO, tc_ch), F32)
        pad_dc[pl.ds(t, HALO), :] = jnp.zeros((HALO, tc_ch), F32)
        acc_w[...] = jnp.zeros_like(acc_w)
        acc_b[...] = jnp.zeros_like(acc_b)

        def fill(s, tc):
            pad_a[pl.ds(HALO + s, tc), :] = v_ref[pl.ds(s, tc), :].astype(F32) * jax.nn.sigmoid(g_ref[pl.ds(s, tc), :].astype(F32))
            pad_dc[pl.ds(s, tc), :] = dc_ref[pl.ds(s, tc), :]

        def chunk(s, tc):
            dcv = pad_dc[pl.ds(s, tc), :]
            win_a = pad_a[pl.ds(s, tc + HALO), :]
            for k in range(kw):
                off = HALO - (kw - 1) + k
                acc_w[pl.ds(8 * k, 8), :] += _fold8(dcv * win_a[off:off + tc, :])
            acc_b[...] += _fold8(dcv)
            win_dc = pad_dc[pl.ds(s, tc + HALO), :]
            da = None
            for j in range(kw):
                term = w_ref[kw - 1 - j:kw - j, :] * win_dc[j:j + tc, :]
                da = term if da is None else da + term
            vv = v_ref[pl.ds(s, tc), :].astype(F32)
            sg = jax.nn.sigmoid(g_ref[pl.ds(s, tc), :].astype(F32))
            dv_ref[pl.ds(s, tc), :] = (da * sg).astype(BF16)
            dg_ref[pl.ds(s, tc), :] = (da * vv * sg * (1.0 - sg)).astype(BF16)

        _chunks(t, fill)
        _chunks(t, chunk)
        for k in range(kw):
            dw_ref[k:k + 1, :] = jnp.sum(acc_w[pl.ds(8 * k, 8), :], axis=0, keepdims=True)
        db_ref[...] = jnp.sum(acc_b[...], axis=0, keepdims=True)

    col = pl.BlockSpec((t, tc_ch), lambda j: (0, j))
    return pl.pallas_call(
        body, name=name, grid=(d_conv // tc_ch,),
        in_specs=[pl.BlockSpec((t, tc_ch), lambda j: (0, v0 + j)), pl.BlockSpec((t, tc_ch), lambda j: (0, g0 + j)),
                  col, pl.BlockSpec((kw, tc_ch), lambda j: (0, j))],
        out_specs=[col, col, pl.BlockSpec((kw, tc_ch), lambda j: (0, j)), pl.BlockSpec((1, tc_ch), lambda j: (0, j))],
        out_shape=[jax.ShapeDtypeStruct((t, d_conv), BF16), jax.ShapeDtypeStruct((t, d_conv), BF16),
                   jax.ShapeDtypeStruct((kw, d_conv), F32), jax.ShapeDtypeStruct((1, d_conv), F32)],
        scratch_shapes=[pltpu.VMEM((t + HALO, tc_ch), F32), pltpu.VMEM((t + HALO, tc_ch), F32),
                        pltpu.VMEM((8 * kw, tc_ch), F32), pltpu.VMEM((8, tc_ch), F32)],
        compiler_params=_params(("parallel",)),
    )(z, z, dc, conv_w)


def short_fwd(name, z, conv_w, d_short):
    t = z.shape[0]
    kw = conv_w.shape[0]
    tc_ch = _tile(d_short, CHANNEL_TILE)
    nt = d_short // tc_ch

    def body(b_ref, c_ref, u_ref, w_ref, y_ref, pad):
        pad[pl.ds(0, HALO), :] = jnp.zeros((HALO, tc_ch), F32)

        def fill(s, tc):
            pad[pl.ds(HALO + s, tc), :] = c_ref[pl.ds(s, tc), :].astype(F32) * u_ref[pl.ds(s, tc), :].astype(F32)

        def chunk(s, tc):
            win = pad[pl.ds(s, tc + HALO), :]
            cq = _taps(win, w_ref, [HALO - (kw - 1) + k for k in range(kw)], tc)
            y_ref[pl.ds(s, tc), :] = (b_ref[pl.ds(s, tc), :].astype(F32) * cq).astype(BF16)

        _chunks(t, fill)
        _chunks(t, chunk)

    return pl.pallas_call(
        body, name=name, grid=(nt,),
        in_specs=[pl.BlockSpec((t, tc_ch), lambda j: (0, j)), pl.BlockSpec((t, tc_ch), lambda j: (0, nt + j)),
                  pl.BlockSpec((t, tc_ch), lambda j: (0, 2 * nt + j)), pl.BlockSpec((kw, tc_ch), lambda j: (0, j))],
        out_specs=pl.BlockSpec((t, tc_ch), lambda j: (0, j)),
        out_shape=jax.ShapeDtypeStruct((t, d_short), BF16),
        scratch_shapes=[pltpu.VMEM((t + HALO, tc_ch), F32)],
        compiler_params=_params(("parallel",)),
    )(z, z, z, conv_w)


def short_bwd(name, z, dy, conv_w, d_short):
    t = z.shape[0]
    kw = conv_w.shape[0]
    tc_ch = _tile(d_short, CHANNEL_TILE)
    nt = d_short // tc_ch

    def body(b_ref, c_ref, u_ref, dy_ref, w_ref, db_ref, dcg_ref, du_ref, dw_ref, pad_q, pad_dcq, acc_w):
        pad_q[pl.ds(0, HALO), :] = jnp.zeros((HALO, tc_ch), F32)
        pad_dcq[pl.ds(t, HALO), :] = jnp.zeros((HALO, tc_ch), F32)
        acc_w[...] = jnp.zeros_like(acc_w)

        def fill(s, tc):
            rows = pl.ds(s, tc)
            pad_q[pl.ds(HALO + s, tc), :] = c_ref[rows, :].astype(F32) * u_ref[rows, :].astype(F32)
            pad_dcq[rows, :] = dy_ref[rows, :].astype(F32) * b_ref[rows, :].astype(F32)

        def chunk(s, tc):
            rows = pl.ds(s, tc)
            win_q = pad_q[pl.ds(s, tc + HALO), :]
            dcq = pad_dcq[rows, :]
            cq = None
            for k in range(kw):
                off = HALO - (kw - 1) + k
                shifted = win_q[off:off + tc, :]
                acc_w[pl.ds(8 * k, 8), :] += _fold8(dcq * shifted)
                term = w_ref[k:k + 1, :] * shifted
                cq = term if cq is None else cq + term
            db_ref[rows, :] = (dy_ref[rows, :].astype(F32) * cq).astype(BF16)
            win_d = pad_dcq[pl.ds(s, tc + HALO), :]
            dq = None
            for j in range(kw):
                term = w_ref[kw - 1 - j:kw - j, :] * win_d[j:j + tc, :]
                dq = term if dq is None else dq + term
            dcg_ref[rows, :] = (dq * u_ref[rows, :].astype(F32)).astype(BF16)
            du_ref[rows, :] = (dq * c_ref[rows, :].astype(F32)).astype(BF16)

        _chunks(t, fill)
        _chunks(t, chunk)
        for k in range(kw):
            dw_ref[k:k + 1, :] = jnp.sum(acc_w[pl.ds(8 * k, 8), :], axis=0, keepdims=True)

    col = pl.BlockSpec((t, tc_ch), lambda j: (0, j))
    zspec = [pl.BlockSpec((t, tc_ch), lambda j, o=o: (0, o * nt + j)) for o in range(3)]
    return pl.pallas_call(
        body, name=name, grid=(nt,),
        in_specs=[*zspec, col, pl.BlockSpec((kw, tc_ch), lambda j: (0, j))],
        out_specs=[col, col, col, pl.BlockSpec((kw, tc_ch), lambda j: (0, j))],
        out_shape=[jax.ShapeDtypeStruct((t, d_short), BF16)] * 3 + [jax.ShapeDtypeStruct((kw, d_short), F32)],
        scratch_shapes=[pltpu.VMEM((t + HALO, tc_ch), F32), pltpu.VMEM((t + HALO, tc_ch), F32),
                        pltpu.VMEM((8 * kw, tc_ch), F32)],
        compiler_params=_params(("parallel",)),
    )(z, z, z, dy, conv_w)


def adamw(name, w, m, v, contributions):
    r, c = w.shape
    n_slots = contributions.shape[0]
    tr = _tile(r, 256 if c <= 1024 else 128)

    def body(w_ref, m_ref, v_ref, g_ref, grad_ref, delta_ref, nm_ref, nv_ref):
        g = g_ref[0].astype(F32)
        for s in range(1, n_slots):
            g = g + g_ref[s].astype(F32)
        nm = ADAM_B1 * m_ref[...] + (1.0 - ADAM_B1) * g
        nv = ADAM_B2 * v_ref[...] + (1.0 - ADAM_B2) * (g * g)
        m_hat = nm / (1.0 - ADAM_B1 ** ADAM_STEP)
        v_hat = nv / (1.0 - ADAM_B2 ** ADAM_STEP)
        grad_ref[...] = g
        delta_ref[...] = -ADAM_LR * (m_hat / (jnp.sqrt(v_hat) + ADAM_EPS) + ADAM_WD * w_ref[...])
        nm_ref[...] = nm
        nv_ref[...] = nv

    blk = pl.BlockSpec((tr, c), lambda i: (i, 0))
    return pl.pallas_call(
        body, name=name, grid=(r // tr,),
        in_specs=[blk, blk, blk, pl.BlockSpec((n_slots, tr, c), lambda i: (0, i, 0))],
        out_specs=[blk] * 4, out_shape=[jax.ShapeDtypeStruct((r, c), F32)] * 4,
        compiler_params=_params(("parallel",)),
    )(w, m, v, contributions)


def _pad_rows(a, rows):
    return jnp.pad(a, ((0, rows - a.shape[0]), (0, 0)))


def kernel(x, mix_pre_g, mix_post_g, ffn_pre_g, ffn_post_g, ab_w_in, pool_w, pool_scale, conv_w, conv_b, conv_ln_g, conv_ln_b, ab_w_out, sc_w_in, sc_conv_w, sc_w_out, ffn_w1, ffn_w2, loss_target, m_mix_pre_g, m_mix_post_g, m_ffn_pre_g, m_ffn_post_g, m_ab_w_in, m_pool_w, m_pool_scale, m_conv_w, m_conv_b, m_conv_ln_g, m_conv_ln_b, m_ab_w_out, m_sc_w_in, m_sc_conv_w, m_sc_w_out, m_ffn_w1, m_ffn_w2, v_mix_pre_g, v_mix_post_g, v_ffn_pre_g, v_ffn_post_g, v_ab_w_in, v_pool_w, v_pool_scale, v_conv_w, v_conv_b, v_conv_ln_g, v_conv_ln_b, v_ab_w_out, v_sc_w_in, v_sc_conv_w, v_sc_w_out, v_ffn_w1, v_ffn_w2):
    t, d = x.shape[1], x.shape[2]
    d_pool = pool_scale.shape[1]
    d_conv = conv_b.shape[1]
    d_short = d
    ng, pg = pool_w.shape[1], pool_w.shape[3]
    kw, ks = conv_w.shape[1], sc_conv_w.shape[1]
    nb_ab, nb_sc, nb_ff = ab_w_in.shape[2], sc_w_in.shape[2], ffn_w1.shape[2]

    xs = x[0]
    target = loss_target[0]

    names = ["ab_in", "pool", "conv", "sconv", "ab_out", "ff1_0", "ff2_0", "sc_in", "sc_out", "ff1_1", "ff2_1"]
    started, token = gather_start(
        [ab_w_in[0].astype(BF16), pool_w[0].astype(BF16), conv_w[0], sc_conv_w[0], ab_w_out[0].astype(BF16),
         ffn_w1[0].astype(BF16), ffn_w2[0].astype(BF16), sc_w_in[0].astype(BF16), sc_w_out[0].astype(BF16),
         ffn_w1[1].astype(BF16), ffn_w2[1].astype(BF16)],
        name="gather_start")
    started = dict(zip(names, started))

    def fetch(which, after):
        waited = [gather_wait("gather_wait_" + nm, started[nm], after) for nm in which]
        return gather_finish(waited, name="gather_finish_" + which[0])

    relu = lambda r: jnp.maximum(r, 0.0)
    square = lambda a: a * a
    relu2_bwd = lambda r, a: r * (2.0 * a.astype(F32))

    def row(vec, l):
        return vec[l:l + 1]

    w_ab_in, w_pool, w_conv, w_sconv = fetch(["ab_in", "pool", "conv", "sconv"], token)
    w_pool = w_pool.transpose(1, 0, 2, 3).reshape(ng, pg, pg)
    w_conv = w_conv.transpose(1, 0, 2).reshape(kw, d_conv)
    w_sconv = w_sconv.transpose(1, 0, 2).reshape(ks, d_short)
    h0 = norm_pre("norm_pre", xs, row(mix_pre_g, 0))
    z0 = mm_nn_blocked("ab_in", h0, w_ab_in, out_dtype=BF16)
    pooled, y_pool = pool_fwd("pool_fwd", z0, w_pool, pool_scale, d_pool)
    cv = conv_fwd("conv_fwd", z0, w_conv, conv_b, d_pool, d_conv)
    y_conv = ln_silu("ln_silu", cv, conv_ln_g, conv_ln_b)
    y0 = jnp.concatenate([y_pool, y_conv], axis=1)
    w_ab_out = fetch(["ab_out"], y0)[0].reshape(d_pool + d_conv, d)
    m0 = mm_nn("ab_out", y0, w_ab_out, out_dtype=F32)
    x1, h1 = post_pre("post_pre_0", xs, m0, row(mix_post_g, 0), row(ffn_pre_g, 0))
    w_ff1 = [fetch(["ff1_0"], h1)[0]]
    a0 = mm_nn_blocked("ffn0_up", h1, w_ff1[0], out_dtype=BF16, epilogue=relu)
    w_ff2 = [fetch(["ff2_0"], a0)[0].reshape(-1, d)]
    f0 = mm_nn("ffn0_down", a0, w_ff2[0], out_dtype=F32, tk=1024, lhs_fn=square)
    x2, h2 = post_pre("post_pre_1", x1, f0, row(ffn_post_g, 0), row(mix_pre_g, 1))
    w_sc_in = fetch(["sc_in"], h2)[0]
    z1 = mm_nn_blocked("sc_in", h2, w_sc_in, out_dtype=BF16)
    y1 = short_fwd("short_fwd", z1, w_sconv, d_short)
    w_sc_out = fetch(["sc_out"], y1)[0].reshape(d_short, d)
    m1 = mm_nn("sc_out", y1, w_sc_out, out_dtype=F32)
    x3, h3 = post_pre("post_pre_2", x2, m1, row(mix_post_g, 1), row(ffn_pre_g, 1))
    w_ff1.append(fetch(["ff1_1"], h3)[0])
    a1 = mm_nn_blocked("ffn1_up", h3, w_ff1[1], out_dtype=BF16, epilogue=relu)
    w_ff2.append(fetch(["ff2_1"], a1)[0].reshape(-1, d))
    f1 = mm_nn("ffn1_down", a1, w_ff2[1], out_dtype=F32, tk=1024, lhs_fn=square)
    dx4, df1, loss_part, dg_ffn_post1 = post_loss("post_loss", x3, f1, row(ffn_post_g, 1), target)
    loss = lax.psum(jnp.sum(loss_part) * (0.5 / d), ("x", "y", "c"))

    def ffn_bwd(tag, df, a, h, w1, w2):
        dw2 = mm_tn(tag + "_dw2", a, df, out_dtype=BF16, lhs_fn=square)
        dpre = mm_nt(tag + "_da", df, w2, out_dtype=BF16, extra=a, epilogue=relu2_bwd)
        dw1 = mm_tn_blocked(tag + "_dw1", h, dpre, nb_ff, out_dtype=BF16)
        dh = mm_nt_blocked(tag + "_dh", dpre, w1, out_dtype=BF16)
        return dw1, dw2, dh

    dw_ff1_1, dw_ff2_1, dh3 = ffn_bwd("ffn1", df1, a1, h3, w_ff1[1], w_ff2[1])
    dx3, dm1, dg_ffn_pre1, dg_mix_post1 = bwd_pre_post("bwd_3", dx4, x3, row(ffn_pre_g, 1), dh3, m1, row(mix_post_g, 1))

    dw_sc_out = mm_tn("sc_dwout", y1, dm1, out_dtype=BF16)
    dy1 = mm_nt("sc_dy", dm1, w_sc_out, out_dtype=BF16)
    db1, dcg1, du1, dw_sconv = short_bwd("short_bwd", z1, dy1, w_sconv, d_short)
    dz1 = jnp.concatenate([db1, dcg1, du1], axis=1)
    dw_sc_in = mm_tn_blocked("sc_dwin", h2, dz1, nb_sc, out_dtype=BF16)
    dh2 = mm_nt_blocked("sc_dh", dz1, w_sc_in, out_dtype=BF16)
    dx2, df0, dg_mix_pre1, dg_ffn_post0 = bwd_pre_post("bwd_2", dx3, x2, row(mix_pre_g, 1), dh2, f0, row(ffn_post_g, 0))

    dw_ff1_0, dw_ff2_0, dh1 = ffn_bwd("ffn0", df0, a0, h1, w_ff1[0], w_ff2[0])
    dx1, dm0, dg_ffn_pre0, dg_mix_post0 = bwd_pre_post("bwd_1", dx2, x1, row(ffn_pre_g, 0), dh1, m0, row(mix_post_g, 0))

    dw_ab_out = mm_tn("ab_dwout", y0, dm0, out_dtype=BF16)
    dy0 = mm_nt("ab_dy", dm0, w_ab_out, out_dtype=BF16)
    dcv, dg_ln_g, dg_ln_b = ln_silu_bwd("ln_silu_bwd", cv, conv_ln_g, conv_ln_b, dy0[:, d_pool:])
    dv, dgate, dw_conv, dg_conv_b = conv_bwd("conv_bwd", z0, dcv, w_conv, d_pool, d_conv)
    du0, dw_pool, dg_pool_scale = pool_bwd("pool_bwd", pooled, dy0[:, :d_pool], w_pool, pool_scale)
    dz0 = jnp.concatenate([du0, dv, dgate], axis=1)
    dw_ab_in = mm_tn_blocked("ab_dwin", h0, dz0, nb_ab, out_dtype=BF16)
    dh0 = mm_nt_blocked("ab_dh", dz0, w_ab_in, out_dtype=BF16)
    grad_x, dg_mix_pre0 = bwd_pre_final("bwd_0", dx1, xs, row(mix_pre_g, 0), dh0)

    shard_rows = d // N_DEV
    big = [
        [dw_ab_in],
        [dw_ab_out.reshape(N_DEV, (d_pool + d_conv) // N_DEV, d)],
        [dw_sc_in],
        [dw_sc_out.reshape(N_DEV, shard_rows, d)],
        [dw_ff1_0, dw_ff1_1],
        [dw_ff2_0.reshape(N_DEV, -1, d), dw_ff2_1.reshape(N_DEV, -1, d)],
    ]
    lanes = min(128, d_conv // N_DEV)
    small_parts = [
        dw_conv.reshape(kw, N_DEV, -1).transpose(1, 0, 2).reshape(N_DEV, -1, lanes),
        dw_sconv.reshape(ks, N_DEV, -1).transpose(1, 0, 2).reshape(N_DEV, -1, lanes),
        dw_pool.reshape(ng, N_DEV, pg // N_DEV, pg).transpose(1, 0, 2, 3).reshape(N_DEV, -1, lanes),
    ]
    small_rows = [p.shape[1] for p in small_parts]
    small_total = -(-sum(small_rows) // 8) * 8
    small = jnp.pad(jnp.concatenate(small_parts, axis=1), ((0, 0), (0, small_total - sum(small_rows)), (0, 0)))
    groups = big + [[small]]
    flat = [a for grp in groups for a in grp]
    from_sibling = exchange_pair(flat, name="exchange_pair")
    pair_sums = [pair_add(f"pair_add_{i}", g, s) for i, (g, s) in enumerate(zip(flat, from_sibling))]
    it = iter(pair_sums)
    landed = exchange_chips([[next(it) for _ in grp] for grp in groups], name="exchange_chips")

    fold = lambda a: jnp.sum(a, axis=0, keepdims=True)
    rep_rows = [fold(dg_mix_pre0), fold(dg_mix_pre1), fold(dg_mix_post0), fold(dg_mix_post1),
                fold(dg_ffn_pre0), fold(dg_ffn_pre1), fold(dg_ffn_post0), fold(dg_ffn_post1)]
    tail = jnp.concatenate([dg_pool_scale, dg_conv_b, fold(dg_ln_g), fold(dg_ln_b)], axis=1).reshape(-1, d)
    rep = _pad_rows(jnp.concatenate(rep_rows + [tail], axis=0), 16)
    rep_all = all_gather([rep], name="gather_small_grads")[0]

    def pack_rep(a_mix_pre, a_mix_post, a_ffn_pre, a_ffn_post, a_scale, a_b, a_g, a_lb):
        tail_ = jnp.concatenate([a_scale, a_b, a_g, a_lb], axis=1).reshape(-1, d)
        return _pad_rows(jnp.concatenate([a_mix_pre, a_mix_post, a_ffn_pre, a_ffn_post, tail_], axis=0), 16)

    def pack_small(a_conv, a_sconv, a_pool):
        parts = [a_conv[0].reshape(-1, lanes), a_sconv[0].reshape(-1, lanes), a_pool[0].reshape(-1, lanes)]
        return _pad_rows(jnp.concatenate(parts, axis=0), small_total)

    def upd(name, w, m, v, contrib):
        shape = w.shape
        r2 = lambda a: a.reshape(-1, shape[-1])
        outs = adamw(name, r2(w), r2(m), r2(v), contrib)
        return [o.reshape(shape) for o in outs]

    o_ab_in = upd("adam_ab_in", ab_w_in, m_ab_w_in, v_ab_w_in, landed[0])
    o_ab_out = upd("adam_ab_out", ab_w_out, m_ab_w_out, v_ab_w_out, landed[1])
    o_sc_in = upd("adam_sc_in", sc_w_in, m_sc_w_in, v_sc_w_in, landed[2])
    o_sc_out = upd("adam_sc_out", sc_w_out, m_sc_w_out, v_sc_w_out, landed[3])
    o_ff1 = upd("adam_ffn_w1", ffn_w1, m_ffn_w1, v_ffn_w1, landed[4])
    o_ff2 = upd("adam_ffn_w2", ffn_w2, m_ffn_w2, v_ffn_w2, landed[5])

    o_small = adamw("adam_small", pack_small(conv_w, sc_conv_w, pool_w), pack_small(m_conv_w, m_sc_conv_w, m_pool_w),
                    pack_small(v_conv_w, v_sc_conv_w, v_pool_w), landed[6])
    o_rep = adamw("adam_replicated",
                  pack_rep(mix_pre_g, mix_post_g, ffn_pre_g, ffn_post_g, pool_scale, conv_b, conv_ln_g, conv_ln_b),
                  pack_rep(m_mix_pre_g, m_mix_post_g, m_ffn_pre_g, m_ffn_post_g, m_pool_scale, m_conv_b, m_conv_ln_g, m_conv_ln_b),
                  pack_rep(v_mix_pre_g, v_mix_post_g, v_ffn_pre_g, v_ffn_post_g, v_pool_scale, v_conv_b, v_conv_ln_g, v_conv_ln_b),
                  rep_all)

    def unpack_small(o):
        r0, r1 = small_rows[0], small_rows[0] + small_rows[1]
        return (o[:r0].reshape(conv_w.shape), o[r0:r1].reshape(sc_conv_w.shape),
                o[r1:r1 + small_rows[2]].reshape(pool_w.shape))

    def unpack_rep(o):
        tail_ = o[8:8 + tail.shape[0]].reshape(1, -1)
        n1 = d_pool
        return dict(mix_pre_g=o[0:2], mix_post_g=o[2:4], ffn_pre_g=o[4:6], ffn_post_g=o[6:8],
                    pool_scale=tail_[:, :n1], conv_b=tail_[:, n1:n1 + d_conv],
                    conv_ln_g=tail_[:, n1 + d_conv:n1 + 2 * d_conv], conv_ln_b=tail_[:, n1 + 2 * d_conv:n1 + 3 * d_conv])

    results = []
    for kind in range(4):
        rep_o = unpack_rep(o_rep[kind])
        s_conv, s_sconv, s_pool = unpack_small(o_small[kind])
        results.append([
            rep_o["mix_pre_g"], rep_o["mix_post_g"], rep_o["ffn_pre_g"], rep_o["ffn_post_g"],
            o_ab_in[kind], s_pool, rep_o["pool_scale"], s_conv, rep_o["conv_b"], rep_o["conv_ln_g"], rep_o["conv_ln_b"],
            o_ab_out[kind], o_sc_in[kind], s_sconv, o_sc_out[kind], o_ff1[kind], o_ff2[kind]])

    return (loss, grad_x[None], *results[0], *results[1], *results[2], *results[3])
```

```python
import functools

import jax
import jax.numpy as jnp
from jax import lax
from jax.experimental import pallas as pl
from jax.experimental.pallas import tpu as pltpu

F32 = jnp.float32
BF16 = jnp.bfloat16
MESH = pl.DeviceIdType.MESH
ANY = pl.BlockSpec(memory_space=pl.ANY)

NORM_EPS = 1e-6
POOL_WINDOWS = (2, 4, 8, 16)
MAX_POOL_WINDOW = 16
ADAM_LR = 0.001
ADAM_B1 = 0.9
ADAM_B2 = 0.999
ADAM_EPS = 1e-08
ADAM_WD = 0.01
ADAM_STEP = 10

N_DEV = 8
VMEM_LIMIT = 48 * 1024 * 1024
ROW_TILE = 256
CHANNEL_TILE = 256
TIME_CHUNK = 64
HALO = 32

NN = (((1,), (0,)), ((), ()))
NT = (((1,), (1,)), ((), ()))
TN = (((0,), (0,)), ((), ()))


def _params(sem):
    return pltpu.CompilerParams(dimension_semantics=sem, vmem_limit_bytes=VMEM_LIMIT)


def _place():
    x, y, c = lax.axis_index("x"), lax.axis_index("y"), lax.axis_index("c")
    return x, y, c


def _slot(px, py, pc):
    return 4 * px + 2 * py + pc


def all_gather(arrs, name):
    n = len(arrs)

    def body(*refs):
        ins, outs = refs[:n], refs[n:2 * n]
        send_sems, recv_sems, local_sems = refs[2 * n:]
        x, y, c = _place()
        me = _slot(x, y, c)
        sibling = (x, y, 1 - c)
        chips = [(1 - x, y), (x, 1 - y), (1 - x, 1 - y)]

        def copy(a, k, slot, to, src=None):
            return pltpu.make_async_remote_copy(
                src_ref=outs[a].at[slot] if src is None else src, dst_ref=outs[a].at[slot],
                send_sem=send_sems.at[a, k], recv_sem=recv_sems.at[a, k], device_id=to, device_id_type=MESH)

        started = []
        for a in range(n):
            mine = pltpu.make_async_copy(ins[a], outs[a].at[me], local_sems.at[a])
            mine.start()
            started.append(mine)
        sends = []
        for a in range(n):
            first = [copy(a, 0, me, sibling, src=ins[a])]
            first += [copy(a, 1 + j, me, (px, py, c), src=ins[a]) for j, (px, py) in enumerate(chips)]
            for cp in first:
                cp.start()
            sends += first
        for a in range(n):
            for j, (px, py) in enumerate(chips):
                copy(a, 1 + j, _slot(px, py, c), (x, y, c)).wait_recv()
                fwd = copy(a, 4 + j, _slot(px, py, c), sibling)
                fwd.start()
                sends.append(fwd)
        for a in range(n):
            copy(a, 0, _slot(x, y, 1 - c), (x, y, c)).wait_recv()
            for j, (px, py) in enumerate(chips):
                copy(a, 4 + j, _slot(px, py, 1 - c), (x, y, c)).wait_recv()
        for cp in sends:
            cp.wait_send()
        for mine in started:
            mine.wait()

    outs = pl.pallas_call(
        body, name=name,
        out_shape=[jax.ShapeDtypeStruct((N_DEV,) + a.shape, a.dtype) for a in arrs],
        in_specs=[ANY] * n, out_specs=[ANY] * n,
        scratch_shapes=[pltpu.SemaphoreType.DMA((n, 7)), pltpu.SemaphoreType.DMA((n, 7)),
                        pltpu.SemaphoreType.DMA((n,))],
    )(*arrs)
    return list(outs)


HBM = pl.BlockSpec(memory_space=pltpu.HBM)
SEM = pl.BlockSpec(memory_space=pltpu.SEMAPHORE)
EFFECT = pltpu.SideEffectType.DATAFLOW_SIDE_EFFECTING
TOKEN = jax.ShapeDtypeStruct((8, 128), F32)


def _in_hbm(a):
    return pltpu.with_memory_space_constraint(a, pltpu.HBM)


def _first_hop_peers(x, y, c):
    return [(x, y, 1 - c), (1 - x, y, c), (x, 1 - y, c), (1 - x, 1 - y, c)]


def gather_start(shards, name):
    n = len(shards)

    def body(*refs):
        ins, lands = refs[:n], refs[n:2 * n]
        sends, recvs = refs[2 * n:3 * n], refs[3 * n:4 * n]
        token = refs[-1]
        x, y, c = _place()
        me = _slot(x, y, c)
        for a in range(n):
            for k, to in enumerate(_first_hop_peers(x, y, c)):
                pltpu.make_async_remote_copy(
                    src_ref=ins[a], dst_ref=lands[a].at[me], send_sem=sends[a].at[k], recv_sem=recvs[a].at[k],
                    device_id=to, device_id_type=MESH).start()
        token[...] = jnp.zeros_like(token)

    lands = [lax.empty((N_DEV,) + s.shape, s.dtype) for s in shards]
    outs = pl.pallas_call(
        body, name=name,
        out_shape=([pltpu.SemaphoreType.DMA((4,))] * (2 * n) + [pltpu.HBM(s.shape, s.dtype) for s in shards]
                   + [pltpu.HBM(l.shape, l.dtype) for l in lands] + [TOKEN]),
        in_specs=[HBM] * (2 * n),
        out_specs=[SEM] * (2 * n) + [HBM] * (2 * n) + [pl.BlockSpec(memory_space=pltpu.VMEM)],
        input_output_aliases={i: 2 * n + i for i in range(2 * n)},
        compiler_params=pltpu.CompilerParams(has_side_effects=EFFECT),
    )(*[_in_hbm(s) for s in shards], *[_in_hbm(l) for l in lands])
    per_shard = [(outs[a], outs[n + a], outs[2 * n + a], outs[3 * n + a]) for a in range(n)]
    return per_shard, outs[-1]


def gather_wait(name, started, after):
    send_sems, recv_sems, shard, land = started

    def body(shard_ref, land_ref, sends, recvs, after_ref, shard_out, land_out):
        x, y, c = _place()
        for k, (px, py, pc) in enumerate(_first_hop_peers(x, y, c)):
            cp = pltpu.make_async_remote_copy(
                src_ref=shard_ref, dst_ref=land_ref.at[_slot(px, py, pc)], send_sem=sends.at[k], recv_sem=recvs.at[k],
                device_id=(px, py, pc), device_id_type=MESH)
            cp.wait_send()
            cp.wait_recv()

    return pl.pallas_call(
        body, name=name,
        out_shape=(pltpu.HBM(shard.shape, shard.dtype), pltpu.HBM(land.shape, land.dtype)),
        in_specs=(HBM, HBM, SEM, SEM, ANY), out_specs=(HBM, HBM), input_output_aliases={0: 0, 1: 1},
        compiler_params=pltpu.CompilerParams(has_side_effects=EFFECT),
    )(shard, land, send_sems, recv_sems, after)


def gather_finish(waited, name):
    n = len(waited)

    def body(*refs):
        shards, lands = refs[:n], refs[2 * n:3 * n]
        send_sems, recv_sems, local_sems = refs[3 * n:]
        x, y, c = _place()
        chips = [(1 - x, y), (x, 1 - y), (1 - x, 1 - y)]
        local, copies = [], []
        for a in range(n):
            cp = pltpu.make_async_copy(shards[a], lands[a].at[_slot(x, y, c)], local_sems.at[a])
            cp.start()
            local.append(cp)
            for k, (px, py) in enumerate(chips):
                cp = pltpu.make_async_remote_copy(
                    src_ref=lands[a].at[_slot(px, py, c)], dst_ref=lands[a].at[_slot(px, py, c)],
                    send_sem=send_sems.at[a, k], recv_sem=recv_sems.at[a, k],
                    device_id=(x, y, 1 - c), device_id_type=MESH)
                cp.start()
                copies.append(cp)
        for cp in copies:
            cp.wait()
        for cp in local:
            cp.wait()

    outs = pl.pallas_call(
        body, name=name,
        out_shape=[jax.ShapeDtypeStruct(l.shape, l.dtype) for _, l in waited],
        in_specs=[ANY] * (2 * n), out_specs=[ANY] * n,
        input_output_aliases={n + a: a for a in range(n)},
        scratch_shapes=[pltpu.SemaphoreType.DMA((n, 3)), pltpu.SemaphoreType.DMA((n, 3)),
                        pltpu.SemaphoreType.DMA((n,))],
    )(*[s for s, _ in waited], *[l for _, l in waited])
    return list(outs)


CHIPS = [(0, 0), (0, 1), (1, 0), (1, 1)]
N_CHIP = len(CHIPS)


def _chip(px, py):
    return 2 * px + py


def _first_hop(bufs, sends, recvs, waiting):
    shard, land = bufs
    x, y, c = _place()
    peers = [(x, y, 1 - c), (1 - x, y, c), (x, 1 - y, c), (1 - x, 1 - y, c)]
    return [pltpu.make_async_remote_copy(
        src_ref=shard, dst_ref=land.at[_slot(*p) if waiting else _slot(x, y, c)],
        send_sem=sends.at[k], recv_sem=recvs.at[k], device_id=p, device_id_type=MESH) for k, p in enumerate(peers)]


def _second_hop(bufs, sends, recvs, waiting):
    (land,) = bufs
    x, y, c = _place()
    return [pltpu.make_async_remote_copy(
        src_ref=land.at[_slot(px, py, c)], dst_ref=land.at[_slot(px, py, 1 - c if waiting else c)],
        send_sem=sends.at[k], recv_sem=recvs.at[k], device_id=(x, y, 1 - c), device_id_type=MESH)
        for k, (px, py) in enumerate([(1 - x, y), (x, 1 - y), (1 - x, 1 - y)])]


def _pair_hop(bufs, sends, recvs, waiting):
    g, land = bufs
    x, y, c = _place()
    return [pltpu.make_async_remote_copy(
        src_ref=g.at[_slot(qx, qy, 1 - c)], dst_ref=land.at[q],
        send_sem=sends.at[q], recv_sem=recvs.at[q], device_id=(x, y, 1 - c), device_id_type=MESH)
        for q, (qx, qy) in enumerate(CHIPS)]


def _chip_hop(bufs, sends, recvs, waiting):
    p, land = bufs
    x, y, c = _place()
    return [pltpu.make_async_remote_copy(
        src_ref=p.at[_chip(px, py)], dst_ref=land.at[_chip(px, py) if waiting else _chip(x, y)],
        send_sem=sends.at[k], recv_sem=recvs.at[k], device_id=(px, py, c), device_id_type=MESH)
        for k, (px, py) in enumerate([(1 - x, y), (x, 1 - y), (1 - x, 1 - y)])]


def copies_start(name, groups, hop, n_copies):
    flat = [b for grp in groups for b in grp]
    nb, ng = len(flat), len(groups)

    def body(*refs):
        ins, sems, token = refs[:nb], refs[nb:nb + 2 * ng], refs[-1]
        i = 0
        for gi, grp in enumerate(groups):
            for cp in hop(ins[i:i + len(grp)], sems[2 * gi], sems[2 * gi + 1], False):
                cp.start()
            i += len(grp)
        token[...] = jnp.zeros_like(token)

    outs = pl.pallas_call(
        body, name=name,
        out_shape=([pltpu.SemaphoreType.DMA((n_copies,))] * (2 * ng) + [pltpu.HBM(b.shape, b.dtype) for b in flat]
                   + [TOKEN]),
        in_specs=[HBM] * nb,
        out_specs=[SEM] * (2 * ng) + [HBM] * nb + [pl.BlockSpec(memory_space=pltpu.VMEM)],
        input_output_aliases={i: 2 * ng + i for i in range(nb)},
        compiler_params=pltpu.CompilerParams(has_side_effects=EFFECT),
    )(*[_in_hbm(b) for b in flat])
    started, i = [], 0
    for gi, grp in enumerate(groups):
        started.append((outs[2 * gi], outs[2 * gi + 1], list(outs[2 * ng + i:2 * ng + i + len(grp)])))
        i += len(grp)
    return started, outs[-1]


def copies_wait(name, started, hop, after):
    sends, recvs, bufs = started
    nb = len(bufs)

    def body(*refs):
        for cp in hop(refs[:nb], refs[nb], refs[nb + 1], True):
            cp.wait_send()
            cp.wait_recv()

    outs = pl.pallas_call(
        body, name=name,
        out_shape=[pltpu.HBM(b.shape, b.dtype) for b in bufs],
        in_specs=[HBM] * nb + [SEM, SEM, ANY], out_specs=[HBM] * nb,
        input_output_aliases={i: i for i in range(nb)},
        compiler_params=pltpu.CompilerParams(has_side_effects=EFFECT),
    )(*bufs, sends, recvs, after)
    return list(outs)


def place_slot(name, srcs, n_slots, by_chip, from_slot=False):
    n = len(srcs)

    def body(*refs):
        ins, outs, sems = refs[:n], refs[n:2 * n], refs[2 * n]
        x, y, c = _place()
        mine = _chip(x, y) if by_chip else _slot(x, y, c)
        copies = [pltpu.make_async_copy(ins[a].at[mine] if from_slot else ins[a], outs[a].at[mine], sems.at[a])
                  for a in range(n)]
        for cp in copies:
            cp.start()
        for cp in copies:
            cp.wait()

    shapes = [s.shape[1:] if from_slot else s.shape for s in srcs]
    return list(pl.pallas_call(
        body, name=name,
        out_shape=[jax.ShapeDtypeStruct((n_slots,) + shp, s.dtype) for shp, s in zip(shapes, srcs)],
        in_specs=[ANY] * n, out_specs=[ANY] * n, scratch_shapes=[pltpu.SemaphoreType.DMA((n,))],
    )(*srcs))


def tie(name, x, *deps):
    def body(*refs):
        del refs

    return pl.pallas_call(
        body, name=name, out_shape=jax.ShapeDtypeStruct(x.shape, x.dtype),
        in_specs=[ANY] * (1 + len(deps)), out_specs=ANY, input_output_aliases={0: 0},
    )(x, *deps)


def exchange_pair(arrs, name):
    n = len(arrs)

    def body(*refs):
        ins, outs = refs[:n], refs[n:2 * n]
        send_sems, recv_sems = refs[2 * n:]
        x, y, c = _place()
        copies = []
        for a in range(n):
            for q, (qx, qy) in enumerate(CHIPS):
                cp = pltpu.make_async_remote_copy(
                    src_ref=ins[a].at[_slot(qx, qy, 1 - c)], dst_ref=outs[a].at[q],
                    send_sem=send_sems.at[a, q], recv_sem=recv_sems.at[a, q],
                    device_id=(x, y, 1 - c), device_id_type=MESH)
                cp.start()
                copies.append(cp)
        for cp in copies:
            cp.wait()

    outs = pl.pallas_call(
        body, name=name,
        out_shape=[jax.ShapeDtypeStruct((N_CHIP,) + a.shape[1:], a.dtype) for a in arrs],
        in_specs=[ANY] * n, out_specs=[ANY] * n,
        scratch_shapes=[pltpu.SemaphoreType.DMA((n, N_CHIP)), pltpu.SemaphoreType.DMA((n, N_CHIP))],
    )(*arrs)
    return list(outs)


def pair_add(name, g, from_sibling):
    _, r, c_dim = g.shape
    tr = _tile(r, 256)
    core = lax.axis_index("c").astype(jnp.int32).reshape(1)

    def body(core_ref, g_ref, s_ref, o_ref):
        o_ref[...] = (g_ref[...].astype(F32) + s_ref[...].astype(F32)).astype(o_ref.dtype)

    return pl.pallas_call(
        body, name=name,
        grid_spec=pltpu.PrefetchScalarGridSpec(
            num_scalar_prefetch=1, grid=(N_CHIP, r // tr),
            in_specs=[pl.BlockSpec((None, None, tr, c_dim), lambda q, i, core_ref: (q, core_ref[0], i, 0)),
                      pl.BlockSpec((None, tr, c_dim), lambda q, i, core_ref: (q, i, 0))],
            out_specs=pl.BlockSpec((None, tr, c_dim), lambda q, i, core_ref: (q, i, 0))),
        out_shape=jax.ShapeDtypeStruct((N_CHIP, r, c_dim), g.dtype),
        compiler_params=_params(("parallel", "parallel")),
    )(core, g.reshape(N_CHIP, 2, r, c_dim), from_sibling)


def exchange_chips(groups, name):
    arrs = [a for grp in groups for a in grp]
    n = len(arrs)
    where = []
    for o, grp in enumerate(groups):
        off = 0
        for a in grp:
            where.append((o, off))
            off += a.shape[1]
    flips = [(1, 0), (0, 1), (1, 1)]

    def body(*refs):
        ins, outs = refs[:n], refs[n:n + len(groups)]
        send_sems, recv_sems, local_sems = refs[n + len(groups):]
        x, y, c = _place()
        my_chip = 2 * x + y

        def landing(a, slot):
            o, off = where[a]
            return outs[o].at[slot, pl.ds(off, arrs[a].shape[1])]

        local = []
        for a in range(n):
            cp = pltpu.make_async_copy(ins[a].at[my_chip], landing(a, my_chip), local_sems.at[a])
            cp.start()
            local.append(cp)
        copies = []
        for a in range(n):
            for k, (dx, dy) in enumerate(flips):
                px = 1 - x if dx else x
                py = 1 - y if dy else y
                cp = pltpu.make_async_remote_copy(
                    src_ref=ins[a].at[2 * px + py], dst_ref=landing(a, my_chip),
                    send_sem=send_sems.at[a, k], recv_sem=recv_sems.at[a, k],
                    device_id=(px, py, c), device_id_type=MESH)
                cp.start()
                copies.append(cp)
        for cp in copies:
            cp.wait()
        for cp in local:
            cp.wait()

    outs = pl.pallas_call(
        body, name=name,
        out_shape=[jax.ShapeDtypeStruct((N_CHIP, sum(a.shape[1] for a in grp), grp[0].shape[2]), grp[0].dtype)
                   for grp in groups],
        in_specs=[ANY] * n, out_specs=[ANY] * len(groups),
        scratch_shapes=[pltpu.SemaphoreType.DMA((n, 3)), pltpu.SemaphoreType.DMA((n, 3)),
                        pltpu.SemaphoreType.DMA((n,))],
    )(*arrs)
    return list(outs)


def _matmul(name, lhs, rhs, *, out_shape, out_dtype, grid, lhs_spec, rhs_spec, out_spec, dims, acc_shape,
            lhs_fn=None, extra=(), extra_specs=(), epilogue=None):
    nk = grid[2]
    n_extra = len(extra)

    def body(*refs):
        lhs_ref, rhs_ref = refs[0], refs[1]
        extra_refs = refs[2:2 + n_extra]
        out_ref = refs[2 + n_extra]
        a = lhs_ref[...]
        if lhs_fn is not None:
            a = lhs_fn(a)
        p = lax.dot_general(a, rhs_ref[...], dims, preferred_element_type=F32)

        def finish(r):
            if epilogue is not None:
                r = epilogue(r, *[e[...] for e in extra_refs])
            out_ref[...] = r.astype(out_dtype)

        if nk == 1:
            finish(p)
        else:
            acc_ref = refs[3 + n_extra]
            k = pl.program_id(2)

            @pl.when(k == 0)
            def _():
                acc_ref[...] = p

            @pl.when(k > 0)
            def _():
                acc_ref[...] += p

            @pl.when(k == nk - 1)
            def _():
                finish(acc_ref[...])

    return pl.pallas_call(
        body, name=name, grid=grid,
        out_shape=jax.ShapeDtypeStruct(out_shape, out_dtype),
        in_specs=[lhs_spec, rhs_spec, *extra_specs], out_specs=out_spec,
        scratch_shapes=[pltpu.VMEM(acc_shape, F32)] if nk > 1 else [],
        compiler_params=_params(("parallel", "parallel", "arbitrary")),
    )(lhs, rhs, *extra)


def _tile(n, want):
    return want if n % want == 0 else n


def mm_nn(name, x, w, *, out_dtype, tn=512, tk=None, lhs_fn=None, epilogue=None):
    t, kdim = x.shape
    n = w.shape[1]
    tn = _tile(n, tn)
    tk = kdim if tk is None else _tile(kdim, tk)
    return _matmul(
        name, x, w, out_shape=(t, n), out_dtype=out_dtype, grid=(1, n // tn, kdim // tk),
        lhs_spec=pl.BlockSpec((t, tk), lambda i, j, k: (i, k)),
        rhs_spec=pl.BlockSpec((tk, tn), lambda i, j, k: (k, j)),
        out_spec=pl.BlockSpec((t, tn), lambda i, j, k: (i, j)),
        dims=NN, acc_shape=(t, tn), lhs_fn=lhs_fn, epilogue=epilogue)


def mm_nn_blocked(name, x, w, *, out_dtype, epilogue=None):
    t, kdim = x.shape
    nb = w.shape[2]
    tn = nb // 2 if nb >= 1024 else nb
    sub = nb // tn
    return _matmul(
        name, x, w, out_shape=(t, N_DEV * nb), out_dtype=out_dtype, grid=(1, N_DEV * sub, 1),
        lhs_spec=pl.BlockSpec((t, kdim), lambda i, j, k: (i, k)),
        rhs_spec=pl.BlockSpec((None, kdim, tn), lambda i, j, k: (j // sub, k, j % sub)),
        out_spec=pl.BlockSpec((t, tn), lambda i, j, k: (i, j)),
        dims=NN, acc_shape=(t, tn), epilogue=epilogue)


def mm_nt(name, dy, w, *, out_dtype, tn=512, extra=None, epilogue=None):
    t, n = dy.shape
    kdim = w.shape[0]
    tn = _tile(kdim, tn)
    extra_arrs = () if extra is None else (extra,)
    extra_specs = () if extra is None else (pl.BlockSpec((t, tn), lambda i, j, k: (i, j)),)
    return _matmul(
        name, dy, w, out_shape=(t, kdim), out_dtype=out_dtype, grid=(1, kdim // tn, 1),
        lhs_spec=pl.BlockSpec((t, n), lambda i, j, k: (i, k)),
        rhs_spec=pl.BlockSpec((tn, n), lambda i, j, k: (j, k)),
        out_spec=pl.BlockSpec((t, tn), lambda i, j, k: (i, j)),
        dims=NT, acc_shape=(t, tn), extra=extra_arrs, extra_specs=extra_specs, epilogue=epilogue)


def mm_nt_blocked(name, dz, w, *, out_dtype, tn=512):
    t = dz.shape[0]
    kdim, nb = w.shape[1], w.shape[2]
    tn = _tile(kdim, tn)
    return _matmul(
        name, dz, w, out_shape=(t, kdim), out_dtype=out_dtype, grid=(1, kdim // tn, N_DEV),
        lhs_spec=pl.BlockSpec((t, nb), lambda i, j, k: (i, k)),
        rhs_spec=pl.BlockSpec((None, tn, nb), lambda i, j, k: (k, j, 0)),
        out_spec=pl.BlockSpec((t, tn), lambda i, j, k: (i, j)),
        dims=NT, acc_shape=(t, tn))


def mm_tn(name, x, dy, *, out_dtype, tk=1024, tn=1024, lhs_fn=None):
    t, kdim = x.shape
    n = dy.shape[1]
    tk, tn = _tile(kdim, tk), _tile(n, tn)
    return _matmul(
        name, x, dy, out_shape=(kdim, n), out_dtype=out_dtype, grid=(kdim // tk, n // tn, 1),
        lhs_spec=pl.BlockSpec((t, tk), lambda i, j, k: (k, i)),
        rhs_spec=pl.BlockSpec((t, tn), lambda i, j, k: (k, j)),
        out_spec=pl.BlockSpec((tk, tn), lambda i, j, k: (i, j)),
        dims=TN, acc_shape=(tk, tn), lhs_fn=lhs_fn)


def mm_tn_blocked(name, x, dz, nb, *, out_dtype, tk=1024):
    t, kdim = x.shape
    tk = _tile(kdim, tk)
    return _matmul(
        name, x, dz, out_shape=(N_DEV, kdim, nb), out_dtype=out_dtype, grid=(kdim // tk, N_DEV, 1),
        lhs_spec=pl.BlockSpec((t, tk), lambda i, j, k: (k, i)),
        rhs_spec=pl.BlockSpec((t, nb), lambda i, j, k: (k, j)),
        out_spec=pl.BlockSpec((None, tk, nb), lambda i, j, k: (j, i, 0)),
        dims=TN, acc_shape=(tk, nb))


def _rstd(v):
    return lax.rsqrt(jnp.mean(v * v, axis=-1, keepdims=True) + NORM_EPS)


def _rms_bwd(v, g, dy):
    r = _rstd(v)
    vhat = v * r
    dvh = dy * g
    dv = r * (dvh - vhat * jnp.mean(dvh * vhat, axis=-1, keepdims=True))
    return dv, dy * vhat


def _fold8(v):
    rows, n = v.shape
    return jnp.sum(v.reshape(rows // 8, 8, n), axis=0)


def _fold_lanes(v):
    out = v[:, 0:128]
    for i in range(1, v.shape[1] // 128):
        out = out + v[:, 128 * i:128 * (i + 1)]
    return out


def _accumulate(ref, v):
    i = pl.program_id(0)

    @pl.when(i == 0)
    def _():
        ref[...] = v

    @pl.when(i > 0)
    def _():
        ref[...] += v


def _row_call(body, name, t, ins, row_in, outs, acc_outs=(), tr=ROW_TILE):
    tr = _tile(t, tr)
    in_specs = [pl.BlockSpec((tr, a.shape[1]), lambda i: (i, 0)) if tiled
                else pl.BlockSpec(a.shape, lambda i: (0, 0)) for a, tiled in zip(ins, row_in)]
    out_specs = [pl.BlockSpec((tr, n), lambda i: (i, 0)) for n, _ in outs]
    out_specs += [pl.BlockSpec((8, n), lambda i: (0, 0)) for n in acc_outs]
    out_shape = [jax.ShapeDtypeStruct((t, n), dt) for n, dt in outs]
    out_shape += [jax.ShapeDtypeStruct((8, n), F32) for n in acc_outs]
    return pl.pallas_call(
        body, name=name, grid=(t // tr,), in_specs=in_specs, out_specs=out_specs, out_shape=out_shape,
        compiler_params=_params(("arbitrary",) if acc_outs else ("parallel",)),
    )(*ins)


def norm_pre(name, x, g):
    t, d = x.shape

    def body(x_ref, g_ref, h_ref):
        v = x_ref[...]
        h_ref[...] = (v * _rstd(v) * g_ref[...]).astype(BF16)

    return _row_call(body, name, t, [x, g], [True, False], [(d, BF16)])[0]


def post_pre(name, x, m, g_post, g_pre):
    t, d = x.shape

    def body(x_ref, m_ref, gp_ref, gn_ref, xo_ref, h_ref):
        mv = m_ref[...]
        xn = x_ref[...] + mv * _rstd(mv) * gp_ref[...]
        xo_ref[...] = xn
        h_ref[...] = (xn * _rstd(xn) * gn_ref[...]).astype(BF16)

    return _row_call(body, name, t, [x, m, g_post, g_pre], [True, True, False, False], [(d, F32), (d, BF16)])


def post_loss(name, x, f, g_post, target):
    t, d = x.shape

    def body(x_ref, f_ref, g_ref, t_ref, dx_ref, df_ref, loss_ref, dg_ref):
        fv = f_ref[...]
        g = g_ref[...]
        out = x_ref[...] + fv * _rstd(fv) * g
        err = out - t_ref[...]
        dx = err * (1.0 / d)
        dx_ref[...] = dx
        dfv, dg_rows = _rms_bwd(fv, g, dx)
        df_ref[...] = dfv.astype(BF16)
        _accumulate(loss_ref, _fold8(_fold_lanes(err * err)))
        _accumulate(dg_ref, _fold8(dg_rows))

    return _row_call(body, name, t, [x, f, g_post, target], [True, True, False, True],
                     [(d, F32), (d, BF16)], acc_outs=(128, d))


def bwd_pre_post(name, dx_out, x_in, g_pre, dh, f_prev, g_post_prev):
    t, d = x_in.shape

    def body(dxo_ref, x_ref, gpre_ref, dh_ref, f_ref, gpost_ref, dxi_ref, df_ref, dgpre_ref, dgpost_ref):
        dxv, dgpre_rows = _rms_bwd(x_ref[...], gpre_ref[...], dh_ref[...].astype(F32))
        dxi = dxo_ref[...] + dxv
        dxi_ref[...] = dxi
        dfv, dgpost_rows = _rms_bwd(f_ref[...], gpost_ref[...], dxi)
        df_ref[...] = dfv.astype(BF16)
        _accumulate(dgpre_ref, _fold8(dgpre_rows))
        _accumulate(dgpost_ref, _fold8(dgpost_rows))

    return _row_call(body, name, t, [dx_out, x_in, g_pre, dh, f_prev, g_post_prev],
                     [True, True, False, True, True, False], [(d, F32), (d, BF16)], acc_outs=(d, d))


def bwd_pre_final(name, dx_out, x_in, g_pre, dh):
    t, d = x_in.shape

    def body(dxo_ref, x_ref, gpre_ref, dh_ref, dxi_ref, dgpre_ref):
        dxv, dgpre_rows = _rms_bwd(x_ref[...], gpre_ref[...], dh_ref[...].astype(F32))
        dxi_ref[...] = dxo_ref[...] + dxv
        _accumulate(dgpre_ref, _fold8(dgpre_rows))

    return _row_call(body, name, t, [dx_out, x_in, g_pre, dh], [True, True, False, True], [(d, F32)], acc_outs=(d,))


def _layer_norm_parts(cv):
    mu = jnp.mean(cv, axis=-1, keepdims=True)
    xc = cv - mu
    rstd = lax.rsqrt(jnp.mean(xc * xc, axis=-1, keepdims=True) + NORM_EPS)
    return xc * rstd, rstd


def ln_silu(name, cv, g, b):
    t, n = cv.shape

    def body(c_ref, g_ref, b_ref, y_ref):
        chat, _ = _layer_norm_parts(c_ref[...])
        ln = chat * g_ref[...] + b_ref[...]
        y_ref[...] = (ln * jax.nn.sigmoid(ln)).astype(BF16)

    return _row_call(body, name, t, [cv, g, b], [True, False, False], [(n, BF16)])[0]


def ln_silu_bwd(name, cv, g, b, dy):
    t, n = cv.shape

    def body(c_ref, g_ref, b_ref, dy_ref, dc_ref, dg_ref, db_ref):
        chat, rstd = _layer_norm_parts(c_ref[...])
        g = g_ref[...]
        ln = chat * g + b_ref[...]
        s = jax.nn.sigmoid(ln)
        dln = dy_ref[...].astype(F32) * (s * (1.0 + ln * (1.0 - s)))
        dchat = dln * g
        dc_ref[...] = rstd * (dchat - jnp.mean(dchat, axis=-1, keepdims=True)
                              - chat * jnp.mean(dchat * chat, axis=-1, keepdims=True))
        _accumulate(dg_ref, _fold8(dln * chat))
        _accumulate(db_ref, _fold8(dln))

    return _row_call(body, name, t, [cv, g, b, dy], [True, False, False, True], [(n, F32)], acc_outs=(n, n))


def _chunks(t, fn, tc=TIME_CHUNK):
    tc = _tile(t, tc)

    def step(i, carry):
        fn(pl.multiple_of(i * tc, tc), tc)
        return carry

    lax.fori_loop(0, t // tc, step, 0)


def _taps(window, w_ref, offsets, tc):
    acc = None
    for k, off in enumerate(offsets):
        term = w_ref[k:k + 1, :] * window[off:off + tc, :]
        acc = term if acc is None else acc + term
    return acc


def _window_sums(win, tc, causal):
    sums = []
    cur, rows, step = win, tc + HALO, 1
    for _ in POOL_WINDOWS:
        rows -= 8
        if causal:
            cur = cur[8:8 + rows, :] + cur[8 - step:8 - step + rows, :]
            sums.append(cur[rows - tc:rows, :])
        else:
            cur = cur[0:rows, :] + cur[step:step + rows, :]
            sums.append(cur[0:tc, :])
        step *= 2
    return sums


def _pick(vals, g):
    out = vals[-1]
    for i in range(len(vals) - 2, -1, -1):
        out = jnp.where(g == i, vals[i], out)
    return out


def _pool_count(s, tc, g):
    t1 = (lax.broadcasted_iota(jnp.int32, (tc, 1), 0) + (s + 1)).astype(F32)
    width = _pick([float(w) for w in POOL_WINDOWS], g)
    return jnp.minimum(t1, width)


def pool_fwd(name, z, pool_w, pool_scale, d_pool):
    t = z.shape[0]
    ng, pg = pool_w.shape[0], pool_w.shape[1]

    def body(u_ref, w_ref, s_ref, pooled_ref, y_ref, pad):
        g = pl.program_id(0)
        pad[pl.ds(0, HALO), :] = jnp.zeros((HALO, pg), F32)

        def fill(s, tc):
            pad[pl.ds(HALO + s, tc), :] = u_ref[pl.ds(s, tc), :].astype(F32)

        def chunk(s, tc):
            win = pad[pl.ds(s, tc + HALO), :]
            total = _pick(_window_sums(win, tc, causal=True), g)
            pooled = total / _pool_count(s, tc, g) - win[HALO:HALO + tc, :]
            pooled_ref[pl.ds(s, tc), :] = pooled.astype(BF16)

        _chunks(t, fill)
        _chunks(t, chunk)
        mixed = jnp.dot(pooled_ref[...], w_ref[...], preferred_element_type=F32)
        y_ref[...] = (mixed * s_ref[...]).astype(BF16)

    col = pl.BlockSpec((t, pg), lambda g: (0, g))
    return pl.pallas_call(
        body, name=name, grid=(ng,),
        in_specs=[col, pl.BlockSpec((None, pg, pg), lambda g: (g, 0, 0)), pl.BlockSpec((1, pg), lambda g: (0, g))],
        out_specs=[col, col],
        out_shape=[jax.ShapeDtypeStruct((t, d_pool), BF16), jax.ShapeDtypeStruct((t, d_pool), BF16)],
        scratch_shapes=[pltpu.VMEM((t + HALO, pg), F32)],
        compiler_params=_params(("parallel",)),
    )(z, pool_w, pool_scale)


def pool_bwd(name, pooled, dy, pool_w, pool_scale):
    t, d_pool = pooled.shape
    ng, pg = pool_w.shape[0], pool_w.shape[1]

    def body(p_ref, dy_ref, w_ref, s_ref, du_ref, dw_ref, ds_ref, pad):
        g = pl.program_id(0)
        w = w_ref[...]
        dyv = dy_ref[...].astype(F32)
        mixed = jnp.dot(p_ref[...], w, preferred_element_type=F32)
        ds_ref[...] = jnp.sum(dyv * mixed, axis=0, keepdims=True)
        dmixed = (dyv * s_ref[...]).astype(BF16)
        dw_ref[...] = lax.dot_general(p_ref[...], dmixed, TN, preferred_element_type=F32)
        pad[...] = jnp.zeros((t + HALO, pg), F32)
        pad[pl.ds(0, t), :] = lax.dot_general(dmixed, w, NT, preferred_element_type=F32)

        def scale(s, tc):
            pad[pl.ds(s, tc), :] = pad[pl.ds(s, tc), :] / _pool_count(s, tc, g)

        def chunk(s, tc):
            win = pad[pl.ds(s, tc + HALO), :]
            total = _pick(_window_sums(win, tc, causal=False), g)
            du_ref[pl.ds(s, tc), :] = (total - win[0:tc, :] * _pool_count(s, tc, g)).astype(BF16)

        _chunks(t, scale)
        _chunks(t, chunk)

    col = pl.BlockSpec((t, pg), lambda g: (0, g))
    vec = pl.BlockSpec((1, pg), lambda g: (0, g))
    mat = pl.BlockSpec((None, pg, pg), lambda g: (g, 0, 0))
    return pl.pallas_call(
        body, name=name, grid=(ng,),
        in_specs=[col, col, mat, vec], out_specs=[col, mat, vec],
        out_shape=[jax.ShapeDtypeStruct((t, d_pool), BF16), jax.ShapeDtypeStruct((ng, pg, pg), F32),
                   jax.ShapeDtypeStruct((1, d_pool), F32)],
        scratch_shapes=[pltpu.VMEM((t + HALO, pg), F32)],
        compiler_params=_params(("parallel",)),
    )(pooled, dy, pool_w, pool_scale)


def conv_fwd(name, z, conv_w, conv_b, d_pool, d_conv):
    t = z.shape[0]
    kw = conv_w.shape[0]
    tc_ch = _tile(d_conv, CHANNEL_TILE)
    v0, g0 = d_pool // tc_ch, (d_pool + d_conv) // tc_ch

    def body(v_ref, g_ref, w_ref, b_ref, c_ref, pad):
        pad[pl.ds(0, HALO), :] = jnp.zeros((HALO, tc_ch), F32)

        def fill(s, tc):
            pad[pl.ds(HALO + s, tc), :] = v_ref[pl.ds(s, tc), :].astype(F32) * jax.nn.sigmoid(g_ref[pl.ds(s, tc), :].astype(F32))

        def chunk(s, tc):
            win = pad[pl.ds(s, tc + HALO), :]
            c_ref[pl.ds(s, tc), :] = _taps(win, w_ref, [HALO - (kw - 1) + k for k in range(kw)], tc) + b_ref[...]

        _chunks(t, fill)
        _chunks(t, chunk)

    return pl.pallas_call(
        body, name=name, grid=(d_conv // tc_ch,),
        in_specs=[pl.BlockSpec((t, tc_ch), lambda j: (0, v0 + j)), pl.BlockSpec((t, tc_ch), lambda j: (0, g0 + j)),
                  pl.BlockSpec((kw, tc_ch), lambda j: (0, j)), pl.BlockSpec((1, tc_ch), lambda j: (0, j))],
        out_specs=pl.BlockSpec((t, tc_ch), lambda j: (0, j)),
        out_shape=jax.ShapeDtypeStruct((t, d_conv), F32),
        scratch_shapes=[pltpu.VMEM((t + HALO, tc_ch), F32)],
        compiler_params=_params(("parallel",)),
    )(z, z, conv_w, conv_b)


def conv_bwd(name, z, dc, conv_w, d_pool, d_conv):
    t = z.shape[0]
    kw = conv_w.shape[0]
    tc_ch = _tile(d_conv, CHANNEL_TILE)
    v0, g0 = d_pool // tc_ch, (d_pool + d_conv) // tc_ch

    def body(v_ref, g_ref, dc_ref, w_ref, dv_ref, dg_ref, dw_ref, db_ref, pad_a, pad_dc, acc_w, acc_b):
        pad_a[pl.ds(0, HALO), :] = jnp.zeros((HALO, tc_ch), F32)
        pad_dc[pl.ds(t, HALO), :] = jnp.zeros((HALO, tc_ch), F32)
        acc_w[...] = jnp.zeros_like(acc_w)
        acc_b[...] = jnp.zeros_like(acc_b)

        def fill(s, tc):
            pad_a[pl.ds(HALO + s, tc), :] = v_ref[pl.ds(s, tc), :].astype(F32) * jax.nn.sigmoid(g_ref[pl.ds(s, tc), :].astype(F32))
            pad_dc[pl.ds(s, tc), :] = dc_ref[pl.ds(s, tc), :]

        def chunk(s, tc):
            dcv = pad_dc[pl.ds(s, tc), :]
            win_a = pad_a[pl.ds(s, tc + HALO), :]
            for k in range(kw):
                off = HALO - (kw - 1) + k
                acc_w[pl.ds(8 * k, 8), :] += _fold8(dcv * win_a[off:off + tc, :])
            acc_b[...] += _fold8(dcv)
            win_dc = pad_dc[pl.ds(s, tc + HALO), :]
            da = None
            for j in range(kw):
                term = w_ref[kw - 1 - j:kw - j, :] * win_dc[j:j + tc, :]
                da = term if da is None else da + term
            vv = v_ref[pl.ds(s, tc), :].astype(F32)
            sg = jax.nn.sigmoid(g_ref[pl.ds(s, tc), :].astype(F32))
            dv_ref[pl.ds(s, tc), :] = (da * sg).astype(BF16)
            dg_ref[pl.ds(s, tc), :] = (da * vv * sg * (1.0 - sg)).astype(BF16)

        _chunks(t, fill)
        _chunks(t, chunk)
        for k in range(kw):
            dw_ref[k:k + 1, :] = jnp.sum(acc_w[pl.ds(8 * k, 8), :], axis=0, keepdims=True)
        db_ref[...] = jnp.sum(acc_b[...], axis=0, keepdims=True)

    col = pl.BlockSpec((t, tc_ch), lambda j: (0, j))
    return pl.pallas_call(
        body, name=name, grid=(d_conv // tc_ch,),
        in_specs=[pl.BlockSpec((t, tc_ch), lambda j: (0, v0 + j)), pl.BlockSpec((t, tc_ch), lambda j: (0, g0 + j)),
                  col, pl.BlockSpec((kw, tc_ch), lambda j: (0, j))],
        out_specs=[col, col, pl.BlockSpec((kw, tc_ch), lambda j: (0, j)), pl.BlockSpec((1, tc_ch), lambda j: (0, j))],
        out_shape=[jax.ShapeDtypeStruct((t, d_conv), BF16), jax.ShapeDtypeStruct((t, d_conv), BF16),
                   jax.ShapeDtypeStruct((kw, d_conv), F32), jax.ShapeDtypeStruct((1, d_conv), F32)],
        scratch_shapes=[pltpu.VMEM((t + HALO, tc_ch), F32), pltpu.VMEM((t + HALO, tc_ch), F32),
                        pltpu.VMEM((8 * kw, tc_ch), F32), pltpu.VMEM((8, tc_ch), F32)],
        compiler_params=_params(("parallel",)),
    )(z, z, dc, conv_w)


def short_fwd(name, z, conv_w, d_short):
    t = z.shape[0]
    kw = conv_w.shape[0]
    tc_ch = _tile(d_short, CHANNEL_TILE)
    nt = d_short // tc_ch

    def body(b_ref, c_ref, u_ref, w_ref, y_ref, pad):
        pad[pl.ds(0, HALO), :] = jnp.zeros((HALO, tc_ch), F32)

        def fill(s, tc):
            pad[pl.ds(HALO + s, tc), :] = c_ref[pl.ds(s, tc), :].astype(F32) * u_ref[pl.ds(s, tc), :].astype(F32)

        def chunk(s, tc):
            win = pad[pl.ds(s, tc + HALO), :]
            cq = _taps(win, w_ref, [HALO - (kw - 1) + k for k in range(kw)], tc)
            y_ref[pl.ds(s, tc), :] = (b_ref[pl.ds(s, tc), :].astype(F32) * cq).astype(BF16)

        _chunks(t, fill)
        _chunks(t, chunk)

    return pl.pallas_call(
        body, name=name, grid=(nt,),
        in_specs=[pl.BlockSpec((t, tc_ch), lambda j: (0, j)), pl.BlockSpec((t, tc_ch), lambda j: (0, nt + j)),
                  pl.BlockSpec((t, tc_ch), lambda j: (0, 2 * nt + j)), pl.BlockSpec((kw, tc_ch), lambda j: (0, j))],
        out_specs=pl.BlockSpec((t, tc_ch), lambda j: (0, j)),
        out_shape=jax.ShapeDtypeStruct((t, d_short), BF16),
        scratch_shapes=[pltpu.VMEM((t + HALO, tc_ch), F32)],
        compiler_params=_params(("parallel",)),
    )(z, z, z, conv_w)


def short_bwd(name, z, dy, conv_w, d_short):
    t = z.shape[0]
    kw = conv_w.shape[0]
    tc_ch = _tile(d_short, CHANNEL_TILE)
    nt = d_short // tc_ch

    def body(b_ref, c_ref, u_ref, dy_ref, w_ref, db_ref, dcg_ref, du_ref, dw_ref, pad_q, pad_dcq, acc_w):
        pad_q[pl.ds(0, HALO), :] = jnp.zeros((HALO, tc_ch), F32)
        pad_dcq[pl.ds(t, HALO), :] = jnp.zeros((HALO, tc_ch), F32)
        acc_w[...] = jnp.zeros_like(acc_w)

        def fill(s, tc):
            rows = pl.ds(s, tc)
            pad_q[pl.ds(HALO + s, tc), :] = c_ref[rows, :].astype(F32) * u_ref[rows, :].astype(F32)
            pad_dcq[rows, :] = dy_ref[rows, :].astype(F32) * b_ref[rows, :].astype(F32)

        def chunk(s, tc):
            rows = pl.ds(s, tc)
            win_q = pad_q[pl.ds(s, tc + HALO), :]
            dcq = pad_dcq[rows, :]
            cq = None
            for k in range(kw):
                off = HALO - (kw - 1) + k
                shifted = win_q[off:off + tc, :]
                acc_w[pl.ds(8 * k, 8), :] += _fold8(dcq * shifted)
                term = w_ref[k:k + 1, :] * shifted
                cq = term if cq is None else cq + term
            db_ref[rows, :] = (dy_ref[rows, :].astype(F32) * cq).astype(BF16)
            win_d = pad_dcq[pl.ds(s, tc + HALO), :]
            dq = None
            for j in range(kw):
                term = w_ref[kw - 1 - j:kw - j, :] * win_d[j:j + tc, :]
                dq = term if dq is None else dq + term
            dcg_ref[rows, :] = (dq * u_ref[rows, :].astype(F32)).astype(BF16)
            du_ref[rows, :] = (dq * c_ref[rows, :].astype(F32)).astype(BF16)

        _chunks(t, fill)
        _chunks(t, chunk)
        for k in range(kw):
            dw_ref[k:k + 1, :] = jnp.sum(acc_w[pl.ds(8 * k, 8), :], axis=0, keepdims=True)

    col = pl.BlockSpec((t, tc_ch), lambda j: (0, j))
    zspec = [pl.BlockSpec((t, tc_ch), lambda j, o=o: (0, o * nt + j)) for o in range(3)]
    return pl.pallas_call(
        body, name=name, grid=(nt,),
        in_specs=[*zspec, col, pl.BlockSpec((kw, tc_ch), lambda j: (0, j))],
        out_specs=[col, col, col, pl.BlockSpec((kw, tc_ch), lambda j: (0, j))],
        out_shape=[jax.ShapeDtypeStruct((t, d_short), BF16)] * 3 + [jax.ShapeDtypeStruct((kw, d_short), F32)],
        scratch_shapes=[pltpu.VMEM((t + HALO, tc_ch), F32), pltpu.VMEM((t + HALO, tc_ch), F32),
                        pltpu.VMEM((8 * kw, tc_ch), F32)],
        compiler_params=_params(("parallel",)),
    )(z, z, z, dy, conv_w)


def adamw(name, w, m, v, contributions):
    r, c = w.shape
    nc = len(contributions)
    n_slots = contributions[0].shape[0]
    tr = 256 if c <= 1024 else 128
    if any(a.shape[1] % tr for a in contributions):
        assert nc == 1
        tr = r
    tiles = [a.shape[1] // tr for a in contributions]
    first = [sum(tiles[:j]) for j in range(nc)]

    def body(w_ref, m_ref, v_ref, *rest):
        g_refs, (grad_ref, delta_ref, nm_ref, nv_ref) = rest[:nc], rest[nc:]
        i = pl.program_id(0)
        g = None
        for j, g_ref in enumerate(g_refs):
            s = g_ref[0].astype(F32)
            for slot in range(1, n_slots):
                s = s + g_ref[slot].astype(F32)
            g = s if g is None else jnp.where(i >= first[j], s, g)
        nm = ADAM_B1 * m_ref[...] + (1.0 - ADAM_B1) * g
        nv = ADAM_B2 * v_ref[...] + (1.0 - ADAM_B2) * (g * g)
        m_hat = nm / (1.0 - ADAM_B1 ** ADAM_STEP)
        v_hat = nv / (1.0 - ADAM_B2 ** ADAM_STEP)
        grad_ref[...] = g
        delta_ref[...] = -ADAM_LR * (m_hat / (jnp.sqrt(v_hat) + ADAM_EPS) + ADAM_WD * w_ref[...])
        nm_ref[...] = nm
        nv_ref[...] = nv

    blk = pl.BlockSpec((tr, c), lambda i: (i, 0))
    g_specs = [pl.BlockSpec((n_slots, tr, c), lambda i, j=j: (0, jnp.clip(i - first[j], 0, tiles[j] - 1), 0))
               for j in range(nc)]
    return pl.pallas_call(
        body, name=name, grid=(r // tr,),
        in_specs=[blk, blk, blk, *g_specs],
        out_specs=[blk] * 4, out_shape=[jax.ShapeDtypeStruct((r, c), F32)] * 4,
        compiler_params=_params(("parallel",)),
    )(w, m, v, *contributions)


def _pad_rows(a, rows):
    return jnp.pad(a, ((0, rows - a.shape[0]), (0, 0)))


def kernel(x, mix_pre_g, mix_post_g, ffn_pre_g, ffn_post_g, ab_w_in, pool_w, pool_scale, conv_w, conv_b, conv_ln_g, conv_ln_b, ab_w_out, sc_w_in, sc_conv_w, sc_w_out, ffn_w1, ffn_w2, loss_target, m_mix_pre_g, m_mix_post_g, m_ffn_pre_g, m_ffn_post_g, m_ab_w_in, m_pool_w, m_pool_scale, m_conv_w, m_conv_b, m_conv_ln_g, m_conv_ln_b, m_ab_w_out, m_sc_w_in, m_sc_conv_w, m_sc_w_out, m_ffn_w1, m_ffn_w2, v_mix_pre_g, v_mix_post_g, v_ffn_pre_g, v_ffn_post_g, v_ab_w_in, v_pool_w, v_pool_scale, v_conv_w, v_conv_b, v_conv_ln_g, v_conv_ln_b, v_ab_w_out, v_sc_w_in, v_sc_conv_w, v_sc_w_out, v_ffn_w1, v_ffn_w2):
    t, d = x.shape[1], x.shape[2]
    d_pool = pool_scale.shape[1]
    d_conv = conv_b.shape[1]
    d_short = d
    ng, pg = pool_w.shape[1], pool_w.shape[3]
    kw, ks = conv_w.shape[1], sc_conv_w.shape[1]
    nb_ab, nb_sc, nb_ff = ab_w_in.shape[2], sc_w_in.shape[2], ffn_w1.shape[2]

    xs = x[0]
    target = loss_target[0]

    lanes = min(128, d_conv // N_DEV)
    small_rows = [kw * (d_conv // N_DEV) // lanes, ks * (d_short // N_DEV) // lanes, ng * (pg // N_DEV) * pg // lanes]
    small_total = -(-sum(small_rows) // 8) * 8
    r0, r1, r2 = small_rows[0], small_rows[0] + small_rows[1], sum(small_rows)

    def pack_small(a_conv, a_sconv, a_pool):
        parts = [a_conv[0].reshape(-1, lanes), a_sconv[0].reshape(-1, lanes), a_pool[0].reshape(-1, lanes)]
        return _pad_rows(jnp.concatenate(parts, axis=0), small_total)

    shards = {
        "ab_in": ab_w_in[0].astype(BF16), "small": pack_small(conv_w, sc_conv_w, pool_w),
        "ab_out": ab_w_out[0].astype(BF16), "ff1_0": ffn_w1[0].astype(BF16), "ff2_0": ffn_w2[0].astype(BF16),
        "sc_in": sc_w_in[0].astype(BF16), "sc_out": sc_w_out[0].astype(BF16),
        "ff1_1": ffn_w1[1].astype(BF16), "ff2_1": ffn_w2[1].astype(BF16)}
    zones = place_slot("place_shards", list(shards.values()), N_DEV, by_chip=False)
    started, token = copies_start("gather_start", [[s, z] for s, z in zip(shards.values(), zones)], _first_hop, 4)
    started = dict(zip(shards, started))

    ties = [0]

    def after(v, *deps):
        ties[0] += 1
        return tie(f"tie_{ties[0]}", v, *deps)

    def fetch_begin(nm, dep):
        _, zone = copies_wait("gather_wait_" + nm, started[nm], _first_hop, dep)
        (hop,), tok = copies_start("forward_start_" + nm, [[zone]], _second_hop, 3)
        return hop, tok

    def fetch_end(nm, hop, dep):
        return copies_wait("forward_wait_" + nm, hop, _second_hop, dep)[0]

    relu = lambda r: jnp.maximum(r, 0.0)
    square = lambda a: a * a
    relu2_bwd = lambda r, a: r * (2.0 * a.astype(F32))

    def row(vec, l):
        return vec[l:l + 1]

    hop_small, _ = fetch_begin("small", token)
    hop_ab_in, tok = fetch_begin("ab_in", token)
    w_small = fetch_end("small", hop_small, tok)
    w_ab_in = fetch_end("ab_in", hop_ab_in, tok)
    w_conv = w_small[:, :r0].reshape(N_DEV, kw, -1).transpose(1, 0, 2).reshape(kw, d_conv)
    w_sconv = w_small[:, r0:r1].reshape(N_DEV, ks, -1).transpose(1, 0, 2).reshape(ks, d_short)
    w_pool = w_small[:, r1:r2].reshape(N_DEV, ng, -1, pg).transpose(1, 0, 2, 3).reshape(ng, pg, pg).astype(BF16)
    hop, tok = fetch_begin("ab_out", w_ab_in)
    h0 = after(norm_pre("norm_pre", xs, row(mix_pre_g, 0)), tok)
    z0 = mm_nn_blocked("ab_in", h0, w_ab_in, out_dtype=BF16)
    pooled, y_pool = pool_fwd("pool_fwd", z0, w_pool, pool_scale, d_pool)
    cv = conv_fwd("conv_fwd", z0, w_conv, conv_b, d_pool, d_conv)
    y_conv = ln_silu("ln_silu", cv, conv_ln_g, conv_ln_b)
    y0 = jnp.concatenate([y_pool, y_conv], axis=1)
    w_ab_out = fetch_end("ab_out", hop, y0)
    hop, tok = fetch_begin("ff1_0", w_ab_out)
    y0 = after(y0, tok)
    m0 = mm_nn("ab_out", y0, w_ab_out.reshape(d_pool + d_conv, d), out_dtype=F32)
    x1, h1 = post_pre("post_pre_0", xs, m0, row(mix_post_g, 0), row(ffn_pre_g, 0))
    w_ff1_0 = fetch_end("ff1_0", hop, h1)
    hop, tok = fetch_begin("ff2_0", w_ff1_0)
    h1 = after(h1, tok)
    a0 = mm_nn_blocked("ffn0_up", h1, w_ff1_0, out_dtype=BF16, epilogue=relu)
    w_ff2_0 = fetch_end("ff2_0", hop, a0).reshape(-1, d)
    hop, tok = fetch_begin("sc_in", w_ff2_0)
    a0 = after(a0, tok)
    f0 = mm_nn("ffn0_down", a0, w_ff2_0, out_dtype=F32, tk=1024, lhs_fn=square)
    x2, h2 = post_pre("post_pre_1", x1, f0, row(ffn_post_g, 0), row(mix_pre_g, 1))
    w_sc_in = fetch_end("sc_in", hop, h2)
    hop, tok = fetch_begin("sc_out", w_sc_in)
    h2 = after(h2, tok)
    z1 = mm_nn_blocked("sc_in", h2, w_sc_in, out_dtype=BF16)
    y1 = short_fwd("short_fwd", z1, w_sconv, d_short)
    w_sc_out = fetch_end("sc_out", hop, y1).reshape(d_short, d)
    hop, tok = fetch_begin("ff1_1", w_sc_out)
    y1 = after(y1, tok)
    m1 = mm_nn("sc_out", y1, w_sc_out, out_dtype=F32)
    x3, h3 = post_pre("post_pre_2", x2, m1, row(mix_post_g, 1), row(ffn_pre_g, 1))
    w_ff1_1 = fetch_end("ff1_1", hop, h3)
    hop, tok = fetch_begin("ff2_1", w_ff1_1)
    h3 = after(h3, tok)
    a1 = mm_nn_blocked("ffn1_up", h3, w_ff1_1, out_dtype=BF16, epilogue=relu)
    w_ff2_1 = fetch_end("ff2_1", hop, a1).reshape(-1, d)
    f1 = mm_nn("ffn1_down", a1, w_ff2_1, out_dtype=F32, tk=1024, lhs_fn=square)
    dx4, df1, loss_part, dg_ffn_post1 = post_loss("post_loss", x3, f1, row(ffn_post_g, 1), target)
    loss = lax.psum(jnp.sum(loss_part) * (0.5 / d), ("x", "y", "c"))

    def reduce_begin(tag, g):
        zone = lax.empty((N_CHIP,) + g.shape[1:], g.dtype)
        (hop,), tok = copies_start("pair_start_" + tag, [[g, zone]], _pair_hop, N_CHIP)
        return hop, tok

    def reduce_middle(tag, hop, dep):
        g, from_sibling = copies_wait("pair_wait_" + tag, hop, _pair_hop, dep)
        pair_sum = pair_add("pair_add_" + tag, g, from_sibling)
        (zone,) = place_slot("place_" + tag, [pair_sum], N_CHIP, by_chip=True, from_slot=True)
        (hop2,), tok = copies_start("chips_start_" + tag, [[pair_sum, zone]], _chip_hop, 3)
        return hop2, tok

    def reduce_end(tag, hop2, dep):
        return copies_wait("chips_wait_" + tag, hop2, _chip_hop, dep)[1]

    dw = mm_tn("ffn1_dw2", a1, df1, out_dtype=BF16, lhs_fn=square)
    red_ff2_1, tok = reduce_begin("ff2_1", dw.reshape(N_DEV, -1, d))
    df1 = after(df1, tok)
    dpre = mm_nt("ffn1_da", df1, w_ff2_1, out_dtype=BF16, extra=a1, epilogue=relu2_bwd)
    dw = mm_tn_blocked("ffn1_dw1", h3, dpre, nb_ff, out_dtype=BF16)
    red_ff1_1, tok = reduce_begin("ff1_1", dw)
    dpre = after(dpre, tok)
    dh3 = mm_nt_blocked("ffn1_dh", dpre, w_ff1_1, out_dtype=BF16)
    red_ff2_1, tok = reduce_middle("ff2_1", red_ff2_1, dh3)
    dh3 = after(dh3, tok)
    dx3, dm1, dg_ffn_pre1, dg_mix_post1 = bwd_pre_post("bwd_3", dx4, x3, row(ffn_pre_g, 1), dh3, m1, row(mix_post_g, 1))

    dw = mm_tn("sc_dwout", y1, dm1, out_dtype=BF16)
    red_sc_out, tok = reduce_begin("sc_out", dw.reshape(N_DEV, -1, d))
    dm1 = after(dm1, tok)
    dy1 = mm_nt("sc_dy", dm1, w_sc_out, out_dtype=BF16)
    red_ff1_1, tok = reduce_middle("ff1_1", red_ff1_1, dy1)
    dy1 = after(dy1, tok)
    db1, dcg1, du1, dw_sconv = short_bwd("short_bwd", z1, dy1, w_sconv, d_short)
    dz1 = jnp.concatenate([db1, dcg1, du1], axis=1)
    dw = mm_tn_blocked("sc_dwin", h2, dz1, nb_sc, out_dtype=BF16)
    red_sc_in, tok = reduce_begin("sc_in", dw)
    dz1 = after(dz1, tok)
    dh2 = mm_nt_blocked("sc_dh", dz1, w_sc_in, out_dtype=BF16)
    red_sc_out, tok = reduce_middle("sc_out", red_sc_out, dh2)
    dh2 = after(dh2, tok)
    dx2, df0, dg_mix_pre1, dg_ffn_post0 = bwd_pre_post("bwd_2", dx3, x2, row(mix_pre_g, 1), dh2, f0, row(ffn_post_g, 0))

    dw = mm_tn("ffn0_dw2", a0, df0, out_dtype=BF16, lhs_fn=square)
    red_ff2_0, tok = reduce_begin("ff2_0", dw.reshape(N_DEV, -1, d))
    df0 = after(df0, tok)
    dpre = mm_nt("ffn0_da", df0, w_ff2_0, out_dtype=BF16, extra=a0, epilogue=relu2_bwd)
    red_sc_in, tok = reduce_middle("sc_in", red_sc_in, dpre)
    dpre = after(dpre, tok)
    dw = mm_tn_blocked("ffn0_dw1", h1, dpre, nb_ff, out_dtype=BF16)
    red_ff1_0, tok = reduce_begin("ff1_0", dw)
    dpre = after(dpre, tok)
    dh1 = mm_nt_blocked("ffn0_dh", dpre, w_ff1_0, out_dtype=BF16)
    red_ff2_0, tok = reduce_middle("ff2_0", red_ff2_0, dh1)
    dh1 = after(dh1, tok)
    dx1, dm0, dg_ffn_pre0, dg_mix_post0 = bwd_pre_post("bwd_1", dx2, x1, row(ffn_pre_g, 0), dh1, m0, row(mix_post_g, 0))

    dw = mm_tn("ab_dwout", y0, dm0, out_dtype=BF16)
    red_ab_out, tok = reduce_begin("ab_out", dw.reshape(N_DEV, -1, d))
    dm0 = after(dm0, tok)
    dy0 = mm_nt("ab_dy", dm0, w_ab_out.reshape(d_pool + d_conv, d), out_dtype=BF16)
    red_ff1_0, tok = reduce_middle("ff1_0", red_ff1_0, dy0)
    dy0 = after(dy0, tok)
    dcv, dg_ln_g, dg_ln_b = ln_silu_bwd("ln_silu_bwd", cv, conv_ln_g, conv_ln_b, dy0[:, d_pool:])
    dv, dgate, dw_conv, dg_conv_b = conv_bwd("conv_bwd", z0, dcv, w_conv, d_pool, d_conv)
    du0, dw_pool, dg_pool_scale = pool_bwd("pool_bwd", pooled, dy0[:, :d_pool], w_pool, pool_scale)
    dz0 = jnp.concatenate([du0, dv, dgate], axis=1)
    small_parts = [
        dw_conv.reshape(kw, N_DEV, -1).transpose(1, 0, 2).reshape(N_DEV, -1, lanes),
        dw_sconv.reshape(ks, N_DEV, -1).transpose(1, 0, 2).reshape(N_DEV, -1, lanes),
        dw_pool.reshape(ng, N_DEV, pg // N_DEV, pg).transpose(1, 0, 2, 3).reshape(N_DEV, -1, lanes),
    ]
    small = jnp.pad(jnp.concatenate(small_parts, axis=1), ((0, 0), (0, small_total - r2), (0, 0)))
    red_small, tok = reduce_begin("small", small)
    red_ab_out, tok2 = reduce_middle("ab_out", red_ab_out, dz0)
    dz0 = after(dz0, tok, tok2)
    dw = mm_tn_blocked("ab_dwin", h0, dz0, nb_ab, out_dtype=BF16)
    red_ab_in, tok = reduce_begin("ab_in", dw)
    dz0 = after(dz0, tok)
    dh0 = mm_nt_blocked("ab_dh", dz0, w_ab_in, out_dtype=BF16)
    red_small, tok = reduce_middle("small", red_small, dh0)
    dh0 = after(dh0, tok)
    grad_x, dg_mix_pre0 = bwd_pre_final("bwd_0", dx1, xs, row(mix_pre_g, 0), dh0)
    red_ab_in, tok = reduce_middle("ab_in", red_ab_in, grad_x)

    fold = lambda a: jnp.sum(a, axis=0, keepdims=True)
    rep_rows = [fold(dg_mix_pre0), fold(dg_mix_pre1), fold(dg_mix_post0), fold(dg_mix_post1),
                fold(dg_ffn_pre0), fold(dg_ffn_pre1), fold(dg_ffn_post0), fold(dg_ffn_post1)]
    tail = jnp.concatenate([dg_pool_scale, dg_conv_b, fold(dg_ln_g), fold(dg_ln_b)], axis=1).reshape(-1, d)
    rep = _pad_rows(jnp.concatenate(rep_rows + [tail], axis=0), 16)
    rep_all = all_gather([rep], name="gather_small_grads")[0]

    def pack_rep(a_mix_pre, a_mix_post, a_ffn_pre, a_ffn_post, a_scale, a_b, a_g, a_lb):
        tail_ = jnp.concatenate([a_scale, a_b, a_g, a_lb], axis=1).reshape(-1, d)
        return _pad_rows(jnp.concatenate([a_mix_pre, a_mix_post, a_ffn_pre, a_ffn_post, tail_], axis=0), 16)

    def upd(name, w, m, v, contribs):
        shape = w.shape
        flat2 = lambda a: a.reshape(-1, shape[-1])
        outs = adamw(name, flat2(w), flat2(m), flat2(v), contribs)
        return [o.reshape(shape) for o in outs]

    o_rep = adamw("adam_replicated",
                  pack_rep(mix_pre_g, mix_post_g, ffn_pre_g, ffn_post_g, pool_scale, conv_b, conv_ln_g, conv_ln_b),
                  pack_rep(m_mix_pre_g, m_mix_post_g, m_ffn_pre_g, m_ffn_post_g, m_pool_scale, m_conv_b, m_conv_ln_g, m_conv_ln_b),
                  pack_rep(v_mix_pre_g, v_mix_post_g, v_ffn_pre_g, v_ffn_post_g, v_pool_scale, v_conv_b, v_conv_ln_g, v_conv_ln_b),
                  [rep_all])
    g_ff2 = [reduce_end("ff2_0", red_ff2_0, tok), reduce_end("ff2_1", red_ff2_1, tok)]
    o_ff2 = upd("adam_ffn_w2", ffn_w2, m_ffn_w2, v_ffn_w2, g_ff2)
    g_ff1 = [reduce_end("ff1_0", red_ff1_0, tok), reduce_end("ff1_1", red_ff1_1, tok)]
    o_ff1 = upd("adam_ffn_w1", ffn_w1, m_ffn_w1, v_ffn_w1, g_ff1)
    o_sc_out = upd("adam_sc_out", sc_w_out, m_sc_w_out, v_sc_w_out, [reduce_end("sc_out", red_sc_out, tok)])
    o_sc_in = upd("adam_sc_in", sc_w_in, m_sc_w_in, v_sc_w_in, [reduce_end("sc_in", red_sc_in, tok)])
    o_ab_out = upd("adam_ab_out", ab_w_out, m_ab_w_out, v_ab_w_out, [reduce_end("ab_out", red_ab_out, tok)])
    o_small = adamw("adam_small", pack_small(conv_w, sc_conv_w, pool_w), pack_small(m_conv_w, m_sc_conv_w, m_pool_w),
                    pack_small(v_conv_w, v_sc_conv_w, v_pool_w), [reduce_end("small", red_small, tok)])
    o_ab_in = upd("adam_ab_in", ab_w_in, m_ab_w_in, v_ab_w_in, [reduce_end("ab_in", red_ab_in, tok)])

    def unpack_small(o):
        return o[:r0].reshape(conv_w.shape), o[r0:r1].reshape(sc_conv_w.shape), o[r1:r2].reshape(pool_w.shape)

    def unpack_rep(o):
        tail_ = o[8:8 + tail.shape[0]].reshape(1, -1)
        n1 = d_pool
        return dict(mix_pre_g=o[0:2], mix_post_g=o[2:4], ffn_pre_g=o[4:6], ffn_post_g=o[6:8],
                    pool_scale=tail_[:, :n1], conv_b=tail_[:, n1:n1 + d_conv],
                    conv_ln_g=tail_[:, n1 + d_conv:n1 + 2 * d_conv], conv_ln_b=tail_[:, n1 + 2 * d_conv:n1 + 3 * d_conv])

    results = []
    for kind in range(4):
        rep_o = unpack_rep(o_rep[kind])
        s_conv, s_sconv, s_pool = unpack_small(o_small[kind])
        results.append([
            rep_o["mix_pre_g"], rep_o["mix_post_g"], rep_o["ffn_pre_g"], rep_o["ffn_post_g"],
            o_ab_in[kind], s_pool, rep_o["pool_scale"], s_conv, rep_o["conv_b"], rep_o["conv_ln_g"], rep_o["conv_ln_b"],
            o_ab_out[kind], o_sc_in[kind], s_sconv, o_sc_out[kind], o_ff1[kind], o_ff2[kind]])

    return (loss, grad_x[None], *results[0], *results[1], *results[2], *results[3])
```

```python
import functools

import jax
import jax.numpy as jnp
from jax import lax
from jax.experimental import pallas as pl
from jax.experimental.pallas import tpu as pltpu

F32 = jnp.float32
BF16 = jnp.bfloat16
MESH = pl.DeviceIdType.MESH
ANY = pl.BlockSpec(memory_space=pl.ANY)

NORM_EPS = 1e-6
POOL_WINDOWS = (2, 4, 8, 16)
MAX_POOL_WINDOW = 16
ADAM_LR = 0.001
ADAM_B1 = 0.9
ADAM_B2 = 0.999
ADAM_EPS = 1e-08
ADAM_WD = 0.01
ADAM_STEP = 10

N_DEV = 8
VMEM_LIMIT = 48 * 1024 * 1024
ROW_TILE = 256
CHANNEL_TILE = 256
TIME_CHUNK = 64
HALO = 32

NN = (((1,), (0,)), ((), ()))
NT = (((1,), (1,)), ((), ()))
TN = (((0,), (0,)), ((), ()))


def _params(sem):
    return pltpu.CompilerParams(dimension_semantics=sem, vmem_limit_bytes=VMEM_LIMIT)


def _place():
    x, y, c = lax.axis_index("x"), lax.axis_index("y"), lax.axis_index("c")
    return x, y, c


def _slot(px, py, pc):
    return 4 * px + 2 * py + pc


def all_gather(arrs, name):
    n = len(arrs)

    def body(*refs):
        ins, outs = refs[:n], refs[n:2 * n]
        send_sems, recv_sems, local_sems = refs[2 * n:]
        x, y, c = _place()
        me = _slot(x, y, c)
        sibling = (x, y, 1 - c)
        chips = [(1 - x, y), (x, 1 - y), (1 - x, 1 - y)]

        def copy(a, k, slot, to, src=None):
            return pltpu.make_async_remote_copy(
                src_ref=outs[a].at[slot] if src is None else src, dst_ref=outs[a].at[slot],
                send_sem=send_sems.at[a, k], recv_sem=recv_sems.at[a, k], device_id=to, device_id_type=MESH)

        started = []
        for a in range(n):
            mine = pltpu.make_async_copy(ins[a], outs[a].at[me], local_sems.at[a])
            mine.start()
            started.append(mine)
        sends = []
        for a in range(n):
            first = [copy(a, 0, me, sibling, src=ins[a])]
            first += [copy(a, 1 + j, me, (px, py, c), src=ins[a]) for j, (px, py) in enumerate(chips)]
            for cp in first:
                cp.start()
            sends += first
        for a in range(n):
            for j, (px, py) in enumerate(chips):
                copy(a, 1 + j, _slot(px, py, c), (x, y, c)).wait_recv()
                fwd = copy(a, 4 + j, _slot(px, py, c), sibling)
                fwd.start()
                sends.append(fwd)
        for a in range(n):
            copy(a, 0, _slot(x, y, 1 - c), (x, y, c)).wait_recv()
            for j, (px, py) in enumerate(chips):
                copy(a, 4 + j, _slot(px, py, 1 - c), (x, y, c)).wait_recv()
        for cp in sends:
            cp.wait_send()
        for mine in started:
            mine.wait()

    outs = pl.pallas_call(
        body, name=name,
        out_shape=[jax.ShapeDtypeStruct((N_DEV,) + a.shape, a.dtype) for a in arrs],
        in_specs=[ANY] * n, out_specs=[ANY] * n,
        scratch_shapes=[pltpu.SemaphoreType.DMA((n, 7)), pltpu.SemaphoreType.DMA((n, 7)),
                        pltpu.SemaphoreType.DMA((n,))],
    )(*arrs)
    return list(outs)


HBM = pl.BlockSpec(memory_space=pltpu.HBM)
SEM = pl.BlockSpec(memory_space=pltpu.SEMAPHORE)
EFFECT = pltpu.SideEffectType.DATAFLOW_SIDE_EFFECTING
TOKEN = jax.ShapeDtypeStruct((8, 128), F32)


def _in_hbm(a):
    return pltpu.with_memory_space_constraint(a, pltpu.HBM)


def _first_hop_peers(x, y, c):
    return [(x, y, 1 - c), (1 - x, y, c), (x, 1 - y, c), (1 - x, 1 - y, c)]


def gather_start(shards, name):
    n = len(shards)

    def body(*refs):
        ins, lands = refs[:n], refs[n:2 * n]
        sends, recvs = refs[2 * n:3 * n], refs[3 * n:4 * n]
        token = refs[-1]
        x, y, c = _place()
        me = _slot(x, y, c)
        for a in range(n):
            for k, to in enumerate(_first_hop_peers(x, y, c)):
                pltpu.make_async_remote_copy(
                    src_ref=ins[a], dst_ref=lands[a].at[me], send_sem=sends[a].at[k], recv_sem=recvs[a].at[k],
                    device_id=to, device_id_type=MESH).start()
        token[...] = jnp.zeros_like(token)

    lands = [lax.empty((N_DEV,) + s.shape, s.dtype) for s in shards]
    outs = pl.pallas_call(
        body, name=name,
        out_shape=([pltpu.SemaphoreType.DMA((4,))] * (2 * n) + [pltpu.HBM(s.shape, s.dtype) for s in shards]
                   + [pltpu.HBM(l.shape, l.dtype) for l in lands] + [TOKEN]),
        in_specs=[HBM] * (2 * n),
        out_specs=[SEM] * (2 * n) + [HBM] * (2 * n) + [pl.BlockSpec(memory_space=pltpu.VMEM)],
        input_output_aliases={i: 2 * n + i for i in range(2 * n)},
        compiler_params=pltpu.CompilerParams(has_side_effects=EFFECT),
    )(*[_in_hbm(s) for s in shards], *[_in_hbm(l) for l in lands])
    per_shard = [(outs[a], outs[n + a], outs[2 * n + a], outs[3 * n + a]) for a in range(n)]
    return per_shard, outs[-1]


def gather_wait(name, started, after):
    send_sems, recv_sems, shard, land = started

    def body(shard_ref, land_ref, sends, recvs, after_ref, shard_out, land_out):
        x, y, c = _place()
        for k, (px, py, pc) in enumerate(_first_hop_peers(x, y, c)):
            cp = pltpu.make_async_remote_copy(
                src_ref=shard_ref, dst_ref=land_ref.at[_slot(px, py, pc)], send_sem=sends.at[k], recv_sem=recvs.at[k],
                device_id=(px, py, pc), device_id_type=MESH)
            cp.wait_send()
            cp.wait_recv()

    return pl.pallas_call(
        body, name=name,
        out_shape=(pltpu.HBM(shard.shape, shard.dtype), pltpu.HBM(land.shape, land.dtype)),
        in_specs=(HBM, HBM, SEM, SEM, ANY), out_specs=(HBM, HBM), input_output_aliases={0: 0, 1: 1},
        compiler_params=pltpu.CompilerParams(has_side_effects=EFFECT),
    )(shard, land, send_sems, recv_sems, after)


def gather_finish(waited, name):
    n = len(waited)

    def body(*refs):
        shards, lands = refs[:n], refs[2 * n:3 * n]
        send_sems, recv_sems, local_sems = refs[3 * n:]
        x, y, c = _place()
        chips = [(1 - x, y), (x, 1 - y), (1 - x, 1 - y)]
        local, copies = [], []
        for a in range(n):
            cp = pltpu.make_async_copy(shards[a], lands[a].at[_slot(x, y, c)], local_sems.at[a])
            cp.start()
            local.append(cp)
            for k, (px, py) in enumerate(chips):
                cp = pltpu.make_async_remote_copy(
                    src_ref=lands[a].at[_slot(px, py, c)], dst_ref=lands[a].at[_slot(px, py, c)],
                    send_sem=send_sems.at[a, k], recv_sem=recv_sems.at[a, k],
                    device_id=(x, y, 1 - c), device_id_type=MESH)
                cp.start()
                copies.append(cp)
        for cp in copies:
            cp.wait()
        for cp in local:
            cp.wait()

    outs = pl.pallas_call(
        body, name=name,
        out_shape=[jax.ShapeDtypeStruct(l.shape, l.dtype) for _, l in waited],
        in_specs=[ANY] * (2 * n), out_specs=[ANY] * n,
        input_output_aliases={n + a: a for a in range(n)},
        scratch_shapes=[pltpu.SemaphoreType.DMA((n, 3)), pltpu.SemaphoreType.DMA((n, 3)),
                        pltpu.SemaphoreType.DMA((n,))],
    )(*[s for s, _ in waited], *[l for _, l in waited])
    return list(outs)


CHIPS = [(0, 0), (0, 1), (1, 0), (1, 1)]
N_CHIP = len(CHIPS)


def _chip(px, py):
    return 2 * px + py


def _first_hop(bufs, sends, recvs, waiting):
    (land,) = bufs
    x, y, c = _place()
    me = _slot(x, y, c)
    peers = [(x, y, 1 - c), (1 - x, y, c), (x, 1 - y, c), (1 - x, 1 - y, c)]
    return [pltpu.make_async_remote_copy(
        src_ref=land.at[me], dst_ref=land.at[_slot(*p) if waiting else me],
        send_sem=sends.at[k], recv_sem=recvs.at[k], device_id=p, device_id_type=MESH) for k, p in enumerate(peers)]


def _second_hop(bufs, sends, recvs, waiting):
    (land,) = bufs
    x, y, c = _place()
    return [pltpu.make_async_remote_copy(
        src_ref=land.at[_slot(px, py, c)], dst_ref=land.at[_slot(px, py, 1 - c if waiting else c)],
        send_sem=sends.at[k], recv_sem=recvs.at[k], device_id=(x, y, 1 - c), device_id_type=MESH)
        for k, (px, py) in enumerate([(1 - x, y), (x, 1 - y), (1 - x, 1 - y)])]


def _pair_hop(bufs, sends, recvs, waiting):
    g, land = bufs
    x, y, c = _place()
    return [pltpu.make_async_remote_copy(
        src_ref=g.at[_slot(qx, qy, 1 - c)], dst_ref=land.at[q],
        send_sem=sends.at[q], recv_sem=recvs.at[q], device_id=(x, y, 1 - c), device_id_type=MESH)
        for q, (qx, qy) in enumerate(CHIPS)]


def _chip_hop(bufs, sends, recvs, waiting):
    p, land = bufs
    x, y, c = _place()
    return [pltpu.make_async_remote_copy(
        src_ref=p.at[_chip(px, py)], dst_ref=land.at[_chip(px, py) if waiting else _chip(x, y)],
        send_sem=sends.at[k], recv_sem=recvs.at[k], device_id=(px, py, c), device_id_type=MESH)
        for k, (px, py) in enumerate([(1 - x, y), (x, 1 - y), (1 - x, 1 - y)])]


def copies_start(name, groups, hop, n_copies):
    flat = [b for grp in groups for b in grp]
    nb, ng = len(flat), len(groups)

    def body(*refs):
        ins, sems, token = refs[:nb], refs[nb:nb + 2 * ng], refs[-1]
        i = 0
        for gi, grp in enumerate(groups):
            for cp in hop(ins[i:i + len(grp)], sems[2 * gi], sems[2 * gi + 1], False):
                cp.start()
            i += len(grp)
        token[...] = jnp.zeros_like(token)

    outs = pl.pallas_call(
        body, name=name,
        out_shape=([pltpu.SemaphoreType.DMA((n_copies,))] * (2 * ng) + [pltpu.HBM(b.shape, b.dtype) for b in flat]
                   + [TOKEN]),
        in_specs=[HBM] * nb,
        out_specs=[SEM] * (2 * ng) + [HBM] * nb + [pl.BlockSpec(memory_space=pltpu.VMEM)],
        input_output_aliases={i: 2 * ng + i for i in range(nb)},
        compiler_params=pltpu.CompilerParams(has_side_effects=EFFECT),
    )(*[_in_hbm(b) for b in flat])
    started, i = [], 0
    for gi, grp in enumerate(groups):
        started.append((outs[2 * gi], outs[2 * gi + 1], list(outs[2 * ng + i:2 * ng + i + len(grp)])))
        i += len(grp)
    return started, outs[-1]


def copies_wait(name, started, hop, after):
    sends, recvs, bufs = started
    nb = len(bufs)

    def body(*refs):
        for cp in hop(refs[:nb], refs[nb], refs[nb + 1], True):
            cp.wait_send()
            cp.wait_recv()

    outs = pl.pallas_call(
        body, name=name,
        out_shape=[pltpu.HBM(b.shape, b.dtype) for b in bufs],
        in_specs=[HBM] * nb + [SEM, SEM, ANY], out_specs=[HBM] * nb,
        input_output_aliases={i: i for i in range(nb)},
        compiler_params=pltpu.CompilerParams(has_side_effects=EFFECT),
    )(*bufs, sends, recvs, after)
    return list(outs)


def place_shard(name, w, dtype):
    r, c = w.shape
    tr = _tile(r, 256)
    x, y, core = _place()
    me = _slot(x, y, core).astype(jnp.int32).reshape(1)

    def body(me_ref, w_ref, o_ref):
        o_ref[...] = w_ref[...].astype(dtype)

    return pl.pallas_call(
        body, name=name,
        grid_spec=pltpu.PrefetchScalarGridSpec(
            num_scalar_prefetch=1, grid=(r // tr,),
            in_specs=[pl.BlockSpec((tr, c), lambda i, me_ref: (i, 0))],
            out_specs=pl.BlockSpec((None, tr, c), lambda i, me_ref: (me_ref[0], i, 0))),
        out_shape=jax.ShapeDtypeStruct((N_DEV, r, c), dtype),
        compiler_params=_params(("parallel",)),
    )(me, w)


def tie(name, x, *deps):
    def body(*refs):
        del refs

    return pl.pallas_call(
        body, name=name, out_shape=jax.ShapeDtypeStruct(x.shape, x.dtype),
        in_specs=[ANY] * (1 + len(deps)), out_specs=ANY, input_output_aliases={0: 0},
    )(x, *deps)


def exchange_pair(arrs, name):
    n = len(arrs)

    def body(*refs):
        ins, outs = refs[:n], refs[n:2 * n]
        send_sems, recv_sems = refs[2 * n:]
        x, y, c = _place()
        copies = []
        for a in range(n):
            for q, (qx, qy) in enumerate(CHIPS):
                cp = pltpu.make_async_remote_copy(
                    src_ref=ins[a].at[_slot(qx, qy, 1 - c)], dst_ref=outs[a].at[q],
                    send_sem=send_sems.at[a, q], recv_sem=recv_sems.at[a, q],
                    device_id=(x, y, 1 - c), device_id_type=MESH)
                cp.start()
                copies.append(cp)
        for cp in copies:
            cp.wait()

    outs = pl.pallas_call(
        body, name=name,
        out_shape=[jax.ShapeDtypeStruct((N_CHIP,) + a.shape[1:], a.dtype) for a in arrs],
        in_specs=[ANY] * n, out_specs=[ANY] * n,
        scratch_shapes=[pltpu.SemaphoreType.DMA((n, N_CHIP)), pltpu.SemaphoreType.DMA((n, N_CHIP))],
    )(*arrs)
    return list(outs)


def pair_add(name, g, from_sibling):
    _, r, c_dim = g.shape
    tr = _tile(r, 256)
    x, y, core = _place()
    where = jnp.stack([core, _chip(x, y)]).astype(jnp.int32)

    def body(where_ref, g_ref, s_ref, o_ref, zone_ref):
        total = (g_ref[...].astype(F32) + s_ref[...].astype(F32)).astype(o_ref.dtype)
        o_ref[...] = total

        @pl.when(pl.program_id(1) == where_ref[1])
        def _():
            zone_ref[...] = total

    blk = pl.BlockSpec((None, tr, c_dim), lambda i, q, where_ref: (q, i, 0))
    return pl.pallas_call(
        body, name=name,
        grid_spec=pltpu.PrefetchScalarGridSpec(
            num_scalar_prefetch=1, grid=(r // tr, N_CHIP),
            in_specs=[pl.BlockSpec((None, None, tr, c_dim), lambda i, q, where_ref: (q, where_ref[0], i, 0)), blk],
            out_specs=[blk, pl.BlockSpec((None, tr, c_dim), lambda i, q, where_ref: (where_ref[1], i, 0))]),
        out_shape=[jax.ShapeDtypeStruct((N_CHIP, r, c_dim), g.dtype)] * 2,
        compiler_params=_params(("parallel", "arbitrary")),
    )(where, g.reshape(N_CHIP, 2, r, c_dim), from_sibling)


def exchange_chips(groups, name):
    arrs = [a for grp in groups for a in grp]
    n = len(arrs)
    where = []
    for o, grp in enumerate(groups):
        off = 0
        for a in grp:
            where.append((o, off))
            off += a.shape[1]
    flips = [(1, 0), (0, 1), (1, 1)]

    def body(*refs):
        ins, outs = refs[:n], refs[n:n + len(groups)]
        send_sems, recv_sems, local_sems = refs[n + len(groups):]
        x, y, c = _place()
        my_chip = 2 * x + y

        def landing(a, slot):
            o, off = where[a]
            return outs[o].at[slot, pl.ds(off, arrs[a].shape[1])]

        local = []
        for a in range(n):
            cp = pltpu.make_async_copy(ins[a].at[my_chip], landing(a, my_chip), local_sems.at[a])
            cp.start()
            local.append(cp)
        copies = []
        for a in range(n):
            for k, (dx, dy) in enumerate(flips):
                px = 1 - x if dx else x
                py = 1 - y if dy else y
                cp = pltpu.make_async_remote_copy(
                    src_ref=ins[a].at[2 * px + py], dst_ref=landing(a, my_chip),
                    send_sem=send_sems.at[a, k], recv_sem=recv_sems.at[a, k],
                    device_id=(px, py, c), device_id_type=MESH)
                cp.start()
                copies.append(cp)
        for cp in copies:
            cp.wait()
        for cp in local:
            cp.wait()

    outs = pl.pallas_call(
        body, name=name,
        out_shape=[jax.ShapeDtypeStruct((N_CHIP, sum(a.shape[1] for a in grp), grp[0].shape[2]), grp[0].dtype)
                   for grp in groups],
        in_specs=[ANY] * n, out_specs=[ANY] * len(groups),
        scratch_shapes=[pltpu.SemaphoreType.DMA((n, 3)), pltpu.SemaphoreType.DMA((n, 3)),
                        pltpu.SemaphoreType.DMA((n,))],
    )(*arrs)
    return list(outs)


def _matmul(name, lhs, rhs, *, out_shape, out_dtype, grid, lhs_spec, rhs_spec, out_spec, dims, acc_shape,
            lhs_fn=None, extra=(), extra_specs=(), epilogue=None):
    nk = grid[2]
    n_extra = len(extra)

    def body(*refs):
        lhs_ref, rhs_ref = refs[0], refs[1]
        extra_refs = refs[2:2 + n_extra]
        out_ref = refs[2 + n_extra]
        a = lhs_ref[...]
        if lhs_fn is not None:
            a = lhs_fn(a)
        p = lax.dot_general(a, rhs_ref[...], dims, preferred_element_type=F32)

        def finish(r):
            if epilogue is not None:
                r = epilogue(r, *[e[...] for e in extra_refs])
            out_ref[...] = r.astype(out_dtype)

        if nk == 1:
            finish(p)
        else:
            acc_ref = refs[3 + n_extra]
            k = pl.program_id(2)

            @pl.when(k == 0)
            def _():
                acc_ref[...] = p

            @pl.when(k > 0)
            def _():
                acc_ref[...] += p

            @pl.when(k == nk - 1)
            def _():
                finish(acc_ref[...])

    return pl.pallas_call(
        body, name=name, grid=grid,
        out_shape=jax.ShapeDtypeStruct(out_shape, out_dtype),
        in_specs=[lhs_spec, rhs_spec, *extra_specs], out_specs=out_spec,
        scratch_shapes=[pltpu.VMEM(acc_shape, F32)] if nk > 1 else [],
        compiler_params=_params(("parallel", "parallel", "arbitrary")),
    )(lhs, rhs, *extra)


def _tile(n, want):
    return want if n % want == 0 else n


def mm_nn(name, x, w, *, out_dtype, tn=512, tk=None, lhs_fn=None, epilogue=None):
    t, kdim = x.shape
    n = w.shape[1]
    tn = _tile(n, tn)
    tk = kdim if tk is None else _tile(kdim, tk)
    return _matmul(
        name, x, w, out_shape=(t, n), out_dtype=out_dtype, grid=(1, n // tn, kdim // tk),
        lhs_spec=pl.BlockSpec((t, tk), lambda i, j, k: (i, k)),
        rhs_spec=pl.BlockSpec((tk, tn), lambda i, j, k: (k, j)),
        out_spec=pl.BlockSpec((t, tn), lambda i, j, k: (i, j)),
        dims=NN, acc_shape=(t, tn), lhs_fn=lhs_fn, epilogue=epilogue)


def mm_nn_blocked(name, x, w, *, out_dtype, epilogue=None):
    t, kdim = x.shape
    nb = w.shape[2]
    tn = nb // 2 if nb >= 1024 else nb
    sub = nb // tn
    return _matmul(
        name, x, w, out_shape=(t, N_DEV * nb), out_dtype=out_dtype, grid=(1, N_DEV * sub, 1),
        lhs_spec=pl.BlockSpec((t, kdim), lambda i, j, k: (i, k)),
        rhs_spec=pl.BlockSpec((None, kdim, tn), lambda i, j, k: (j // sub, k, j % sub)),
        out_spec=pl.BlockSpec((t, tn), lambda i, j, k: (i, j)),
        dims=NN, acc_shape=(t, tn), epilogue=epilogue)


def mm_nt(name, dy, w, *, out_dtype, tn=512, extra=None, epilogue=None):
    t, n = dy.shape
    kdim = w.shape[0]
    tn = _tile(kdim, tn)
    extra_arrs = () if extra is None else (extra,)
    extra_specs = () if extra is None else (pl.BlockSpec((t, tn), lambda i, j, k: (i, j)),)
    return _matmul(
        name, dy, w, out_shape=(t, kdim), out_dtype=out_dtype, grid=(1, kdim // tn, 1),
        lhs_spec=pl.BlockSpec((t, n), lambda i, j, k: (i, k)),
        rhs_spec=pl.BlockSpec((tn, n), lambda i, j, k: (j, k)),
        out_spec=pl.BlockSpec((t, tn), lambda i, j, k: (i, j)),
        dims=NT, acc_shape=(t, tn), extra=extra_arrs, extra_specs=extra_specs, epilogue=epilogue)


def mm_nt_blocked(name, dz, w, *, out_dtype, tn=512):
    t = dz.shape[0]
    kdim, nb = w.shape[1], w.shape[2]
    tn = _tile(kdim, tn)
    return _matmul(
        name, dz, w, out_shape=(t, kdim), out_dtype=out_dtype, grid=(1, kdim // tn, N_DEV),
        lhs_spec=pl.BlockSpec((t, nb), lambda i, j, k: (i, k)),
        rhs_spec=pl.BlockSpec((None, tn, nb), lambda i, j, k: (k, j, 0)),
        out_spec=pl.BlockSpec((t, tn), lambda i, j, k: (i, j)),
        dims=NT, acc_shape=(t, tn))


def mm_tn(name, x, dy, *, out_dtype, tk=1024, tn=1024, lhs_fn=None):
    t, kdim = x.shape
    n = dy.shape[1]
    tk, tn = _tile(kdim, tk), _tile(n, tn)
    return _matmul(
        name, x, dy, out_shape=(kdim, n), out_dtype=out_dtype, grid=(kdim // tk, n // tn, 1),
        lhs_spec=pl.BlockSpec((t, tk), lambda i, j, k: (k, i)),
        rhs_spec=pl.BlockSpec((t, tn), lambda i, j, k: (k, j)),
        out_spec=pl.BlockSpec((tk, tn), lambda i, j, k: (i, j)),
        dims=TN, acc_shape=(tk, tn), lhs_fn=lhs_fn)


def mm_tn_blocked(name, x, dz, nb, *, out_dtype, tk=1024):
    t, kdim = x.shape
    tk = _tile(kdim, tk)
    return _matmul(
        name, x, dz, out_shape=(N_DEV, kdim, nb), out_dtype=out_dtype, grid=(kdim // tk, N_DEV, 1),
        lhs_spec=pl.BlockSpec((t, tk), lambda i, j, k: (k, i)),
        rhs_spec=pl.BlockSpec((t, nb), lambda i, j, k: (k, j)),
        out_spec=pl.BlockSpec((None, tk, nb), lambda i, j, k: (j, i, 0)),
        dims=TN, acc_shape=(tk, nb))


def _rstd(v):
    return lax.rsqrt(jnp.mean(v * v, axis=-1, keepdims=True) + NORM_EPS)


def _rms_bwd(v, g, dy):
    r = _rstd(v)
    vhat = v * r
    dvh = dy * g
    dv = r * (dvh - vhat * jnp.mean(dvh * vhat, axis=-1, keepdims=True))
    return dv, dy * vhat


def _fold8(v):
    rows, n = v.shape
    return jnp.sum(v.reshape(rows // 8, 8, n), axis=0)


def _fold_lanes(v):
    out = v[:, 0:128]
    for i in range(1, v.shape[1] // 128):
        out = out + v[:, 128 * i:128 * (i + 1)]
    return out


def _accumulate(ref, v):
    i = pl.program_id(0)

    @pl.when(i == 0)
    def _():
        ref[...] = v

    @pl.when(i > 0)
    def _():
        ref[...] += v


def _row_call(body, name, t, ins, row_in, outs, acc_outs=(), tr=ROW_TILE):
    tr = _tile(t, tr)
    in_specs = [pl.BlockSpec((tr, a.shape[1]), lambda i: (i, 0)) if tiled
                else pl.BlockSpec(a.shape, lambda i: (0, 0)) for a, tiled in zip(ins, row_in)]
    out_specs = [pl.BlockSpec((tr, n), lambda i: (i, 0)) for n, _ in outs]
    out_specs += [pl.BlockSpec((8, n), lambda i: (0, 0)) for n in acc_outs]
    out_shape = [jax.ShapeDtypeStruct((t, n), dt) for n, dt in outs]
    out_shape += [jax.ShapeDtypeStruct((8, n), F32) for n in acc_outs]
    return pl.pallas_call(
        body, name=name, grid=(t // tr,), in_specs=in_specs, out_specs=out_specs, out_shape=out_shape,
        compiler_params=_params(("arbitrary",) if acc_outs else ("parallel",)),
    )(*ins)


def norm_pre(name, x, g):
    t, d = x.shape

    def body(x_ref, g_ref, h_ref):
        v = x_ref[...]
        h_ref[...] = (v * _rstd(v) * g_ref[...]).astype(BF16)

    return _row_call(body, name, t, [x, g], [True, False], [(d, BF16)])[0]


def post_pre(name, x, m, g_post, g_pre):
    t, d = x.shape

    def body(x_ref, m_ref, gp_ref, gn_ref, xo_ref, h_ref):
        mv = m_ref[...]
        xn = x_ref[...] + mv * _rstd(mv) * gp_ref[...]
        xo_ref[...] = xn
        h_ref[...] = (xn * _rstd(xn) * gn_ref[...]).astype(BF16)

    return _row_call(body, name, t, [x, m, g_post, g_pre], [True, True, False, False], [(d, F32), (d, BF16)])


def post_loss(name, x, f, g_post, target):
    t, d = x.shape

    def body(x_ref, f_ref, g_ref, t_ref, dx_ref, df_ref, loss_ref, dg_ref):
        fv = f_ref[...]
        g = g_ref[...]
        out = x_ref[...] + fv * _rstd(fv) * g
        err = out - t_ref[...]
        dx = err * (1.0 / d)
        dx_ref[...] = dx
        dfv, dg_rows = _rms_bwd(fv, g, dx)
        df_ref[...] = dfv.astype(BF16)
        _accumulate(loss_ref, _fold8(_fold_lanes(err * err)))
        _accumulate(dg_ref, _fold8(dg_rows))

    return _row_call(body, name, t, [x, f, g_post, target], [True, True, False, True],
                     [(d, F32), (d, BF16)], acc_outs=(128, d))


def bwd_pre_post(name, dx_out, x_in, g_pre, dh, f_prev, g_post_prev):
    t, d = x_in.shape

    def body(dxo_ref, x_ref, gpre_ref, dh_ref, f_ref, gpost_ref, dxi_ref, df_ref, dgpre_ref, dgpost_ref):
        dxv, dgpre_rows = _rms_bwd(x_ref[...], gpre_ref[...], dh_ref[...].astype(F32))
        dxi = dxo_ref[...] + dxv
        dxi_ref[...] = dxi
        dfv, dgpost_rows = _rms_bwd(f_ref[...], gpost_ref[...], dxi)
        df_ref[...] = dfv.astype(BF16)
        _accumulate(dgpre_ref, _fold8(dgpre_rows))
        _accumulate(dgpost_ref, _fold8(dgpost_rows))

    return _row_call(body, name, t, [dx_out, x_in, g_pre, dh, f_prev, g_post_prev],
                     [True, True, False, True, True, False], [(d, F32), (d, BF16)], acc_outs=(d, d))


def bwd_pre_final(name, dx_out, x_in, g_pre, dh):
    t, d = x_in.shape

    def body(dxo_ref, x_ref, gpre_ref, dh_ref, dxi_ref, dgpre_ref):
        dxv, dgpre_rows = _rms_bwd(x_ref[...], gpre_ref[...], dh_ref[...].astype(F32))
        dxi_ref[...] = dxo_ref[...] + dxv
        _accumulate(dgpre_ref, _fold8(dgpre_rows))

    return _row_call(body, name, t, [dx_out, x_in, g_pre, dh], [True, True, False, True], [(d, F32)], acc_outs=(d,))


def _layer_norm_parts(cv):
    mu = jnp.mean(cv, axis=-1, keepdims=True)
    xc = cv - mu
    rstd = lax.rsqrt(jnp.mean(xc * xc, axis=-1, keepdims=True) + NORM_EPS)
    return xc * rstd, rstd


def ln_silu(name, cv, g, b):
    t, n = cv.shape

    def body(c_ref, g_ref, b_ref, y_ref):
        chat, _ = _layer_norm_parts(c_ref[...])
        ln = chat * g_ref[...] + b_ref[...]
        y_ref[...] = (ln * jax.nn.sigmoid(ln)).astype(BF16)

    return _row_call(body, name, t, [cv, g, b], [True, False, False], [(n, BF16)])[0]


def ln_silu_bwd(name, cv, g, b, dy):
    t, n = cv.shape

    def body(c_ref, g_ref, b_ref, dy_ref, dc_ref, dg_ref, db_ref):
        chat, rstd = _layer_norm_parts(c_ref[...])
        g = g_ref[...]
        ln = chat * g + b_ref[...]
        s = jax.nn.sigmoid(ln)
        dln = dy_ref[...].astype(F32) * (s * (1.0 + ln * (1.0 - s)))
        dchat = dln * g
        dc_ref[...] = rstd * (dchat - jnp.mean(dchat, axis=-1, keepdims=True)
                              - chat * jnp.mean(dchat * chat, axis=-1, keepdims=True))
        _accumulate(dg_ref, _fold8(dln * chat))
        _accumulate(db_ref, _fold8(dln))

    return _row_call(body, name, t, [cv, g, b, dy], [True, False, False, True], [(n, F32)], acc_outs=(n, n))


def _chunks(t, fn, tc=TIME_CHUNK):
    tc = _tile(t, tc)

    def step(i, carry):
        fn(pl.multiple_of(i * tc, tc), tc)
        return carry

    lax.fori_loop(0, t // tc, step, 0)


def _taps(window, w_ref, offsets, tc):
    acc = None
    for k, off in enumerate(offsets):
        term = w_ref[k:k + 1, :] * window[off:off + tc, :]
        acc = term if acc is None else acc + term
    return acc


def _window_sums(win, tc, causal):
    sums = []
    cur, rows, step = win, tc + HALO, 1
    for _ in POOL_WINDOWS:
        rows -= 8
        if causal:
            cur = cur[8:8 + rows, :] + cur[8 - step:8 - step + rows, :]
            sums.append(cur[rows - tc:rows, :])
        else:
            cur = cur[0:rows, :] + cur[step:step + rows, :]
            sums.append(cur[0:tc, :])
        step *= 2
    return sums


def _pick(vals, g):
    out = vals[-1]
    for i in range(len(vals) - 2, -1, -1):
        out = jnp.where(g == i, vals[i], out)
    return out


def _pool_count(s, tc, g):
    t1 = (lax.broadcasted_iota(jnp.int32, (tc, 1), 0) + (s + 1)).astype(F32)
    width = _pick([float(w) for w in POOL_WINDOWS], g)
    return jnp.minimum(t1, width)


def pool_fwd(name, z, pool_w, pool_scale, d_pool):
    t = z.shape[0]
    ng, pg = pool_w.shape[0], pool_w.shape[1]

    def body(u_ref, w_ref, s_ref, pooled_ref, y_ref, pad):
        g = pl.program_id(0)
        pad[pl.ds(0, HALO), :] = jnp.zeros((HALO, pg), F32)

        def fill(s, tc):
            pad[pl.ds(HALO + s, tc), :] = u_ref[pl.ds(s, tc), :].astype(F32)

        def chunk(s, tc):
            win = pad[pl.ds(s, tc + HALO), :]
            total = _pick(_window_sums(win, tc, causal=True), g)
            pooled = total / _pool_count(s, tc, g) - win[HALO:HALO + tc, :]
            pooled_ref[pl.ds(s, tc), :] = pooled.astype(BF16)

        _chunks(t, fill)
        _chunks(t, chunk)
        mixed = jnp.dot(pooled_ref[...], w_ref[...], preferred_element_type=F32)
        y_ref[...] = (mixed * s_ref[...]).astype(BF16)

    col = pl.BlockSpec((t, pg), lambda g: (0, g))
    return pl.pallas_call(
        body, name=name, grid=(ng,),
        in_specs=[col, pl.BlockSpec((None, pg, pg), lambda g: (g, 0, 0)), pl.BlockSpec((1, pg), lambda g: (0, g))],
        out_specs=[col, col],
        out_shape=[jax.ShapeDtypeStruct((t, d_pool), BF16), jax.ShapeDtypeStruct((t, d_pool), BF16)],
        scratch_shapes=[pltpu.VMEM((t + HALO, pg), F32)],
        compiler_params=_params(("parallel",)),
    )(z, pool_w, pool_scale)


def pool_bwd(name, pooled, dy, pool_w, pool_scale):
    t, d_pool = pooled.shape
    ng, pg = pool_w.shape[0], pool_w.shape[1]

    def body(p_ref, dy_ref, w_ref, s_ref, du_ref, dw_ref, ds_ref, pad):
        g = pl.program_id(0)
        w = w_ref[...]
        dyv = dy_ref[...].astype(F32)
        mixed = jnp.dot(p_ref[...], w, preferred_element_type=F32)
        ds_ref[...] = jnp.sum(dyv * mixed, axis=0, keepdims=True)
        dmixed = (dyv * s_ref[...]).astype(BF16)
        dw_ref[...] = lax.dot_general(p_ref[...], dmixed, TN, preferred_element_type=F32)
        pad[...] = jnp.zeros((t + HALO, pg), F32)
        pad[pl.ds(0, t), :] = lax.dot_general(dmixed, w, NT, preferred_element_type=F32)

        def scale(s, tc):
            pad[pl.ds(s, tc), :] = pad[pl.ds(s, tc), :] / _pool_count(s, tc, g)

        def chunk(s, tc):
            win = pad[pl.ds(s, tc + HALO), :]
            total = _pick(_window_sums(win, tc, causal=False), g)
            du_ref[pl.ds(s, tc), :] = (total - win[0:tc, :] * _pool_count(s, tc, g)).astype(BF16)

        _chunks(t, scale)
        _chunks(t, chunk)

    col = pl.BlockSpec((t, pg), lambda g: (0, g))
    vec = pl.BlockSpec((1, pg), lambda g: (0, g))
    mat = pl.BlockSpec((None, pg, pg), lambda g: (g, 0, 0))
    return pl.pallas_call(
        body, name=name, grid=(ng,),
        in_specs=[col, col, mat, vec], out_specs=[col, mat, vec],
        out_shape=[jax.ShapeDtypeStruct((t, d_pool), BF16), jax.ShapeDtypeStruct((ng, pg, pg), F32),
                   jax.ShapeDtypeStruct((1, d_pool), F32)],
        scratch_shapes=[pltpu.VMEM((t + HALO, pg), F32)],
        compiler_params=_params(("parallel",)),
    )(pooled, dy, pool_w, pool_scale)


def conv_fwd(name, z, conv_w, conv_b, d_pool, d_conv):
    t = z.shape[0]
    kw = conv_w.shape[0]
    tc_ch = _tile(d_conv, CHANNEL_TILE)
    v0, g0 = d_pool // tc_ch, (d_pool + d_conv) // tc_ch

    def body(v_ref, g_ref, w_ref, b_ref, c_ref, pad):
        pad[pl.ds(0, HALO), :] = jnp.zeros((HALO, tc_ch), F32)

        def fill(s, tc):
            pad[pl.ds(HALO + s, tc), :] = v_ref[pl.ds(s, tc), :].astype(F32) * jax.nn.sigmoid(g_ref[pl.ds(s, tc), :].astype(F32))

        def chunk(s, tc):
            win = pad[pl.ds(s, tc + HALO), :]
            c_ref[pl.ds(s, tc), :] = _taps(win, w_ref, [HALO - (kw - 1) + k for k in range(kw)], tc) + b_ref[...]

        _chunks(t, fill)
        _chunks(t, chunk)

    return pl.pallas_call(
        body, name=name, grid=(d_conv // tc_ch,),
        in_specs=[pl.BlockSpec((t, tc_ch), lambda j: (0, v0 + j)), pl.BlockSpec((t, tc_ch), lambda j: (0, g0 + j)),
                  pl.BlockSpec((kw, tc_ch), lambda j: (0, j)), pl.BlockSpec((1, tc_ch), lambda j: (0, j))],
        out_specs=pl.BlockSpec((t, tc_ch), lambda j: (0, j)),
        out_shape=jax.ShapeDtypeStruct((t, d_conv), F32),
        scratch_shapes=[pltpu.VMEM((t + HALO, tc_ch), F32)],
        compiler_params=_params(("parallel",)),
    )(z, z, conv_w, conv_b)


def conv_bwd(name, z, dc, conv_w, d_pool, d_conv):
    t = z.shape[0]
    kw = conv_w.shape[0]
    tc_ch = _tile(d_conv, CHANNEL_TILE)
    v0, g0 = d_pool // tc_ch, (d_pool + d_conv) // tc_ch

    def body(v_ref, g_ref, dc_ref, w_ref, dv_ref, dg_ref, dw_ref, db_ref, pad_a, pad_dc, acc_w, acc_b):
        pad_a[pl.ds(0, HALO), :] = jnp.zeros((HALO, tc_ch), F32)
        pad_dc[pl.ds(t, HALO), :] = jnp.zeros((HALO, tc_ch), F32)
        acc_w[...] = jnp.zeros_like(acc_w)
        acc_b[...] = jnp.zeros_like(acc_b)

        def fill(s, tc):
            pad_a[pl.ds(HALO + s, tc), :] = v_ref[pl.ds(s, tc), :].astype(F32) * jax.nn.sigmoid(g_ref[pl.ds(s, tc), :].astype(F32))
            pad_dc[pl.ds(s, tc), :] = dc_ref[pl.ds(s, tc), :]

        def chunk(s, tc):
            dcv = pad_dc[pl.ds(s, tc), :]
            win_a = pad_a[pl.ds(s, tc + HALO), :]
            for k in range(kw):
                off = HALO - (kw - 1) + k
                acc_w[pl.ds(8 * k, 8), :] += _fold8(dcv * win_a[off:off + tc, :])
            acc_b[...] += _fold8(dcv)
            win_dc = pad_dc[pl.ds(s, tc + HALO), :]
            da = None
            for j in range(kw):
                term = w_ref[kw - 1 - j:kw - j, :] * win_dc[j:j + tc, :]
                da = term if da is None else da + term
            vv = v_ref[pl.ds(s, tc), :].astype(F32)
            sg = jax.nn.sigmoid(g_ref[pl.ds(s, tc), :].astype(F32))
            dv_ref[pl.ds(s, tc), :] = (da * sg).astype(BF16)
            dg_ref[pl.ds(s, tc), :] = (da * vv * sg * (1.0 - sg)).astype(BF16)

        _chunks(t, fill)
        _chunks(t, chunk)
        for k in range(kw):
            dw_ref[k:k + 1, :] = jnp.sum(acc_w[pl.ds(8 * k, 8), :], axis=0, keepdims=True)
        db_ref[...] = jnp.sum(acc_b[...], axis=0, keepdims=True)

    col = pl.BlockSpec((t, tc_ch), lambda j: (0, j))
    return pl.pallas_call(
        body, name=name, grid=(d_conv // tc_ch,),
        in_specs=[pl.BlockSpec((t, tc_ch), lambda j: (0, v0 + j)), pl.BlockSpec((t, tc_ch), lambda j: (0, g0 + j)),
                  col, pl.BlockSpec((kw, tc_ch), lambda j: (0, j))],
        out_specs=[col, col, pl.BlockSpec((kw, tc_ch), lambda j: (0, j)), pl.BlockSpec((1, tc_ch), lambda j: (0, j))],
        out_shape=[jax.ShapeDtypeStruct((t, d_conv), BF16), jax.ShapeDtypeStruct((t, d_conv), BF16),
                   jax.ShapeDtypeStruct((kw, d_conv), F32), jax.ShapeDtypeStruct((1, d_conv), F32)],
        scratch_shapes=[pltpu.VMEM((t + HALO, tc_ch), F32), pltpu.VMEM((t + HALO, tc_ch), F32),
                        pltpu.VMEM((8 * kw, tc_ch), F32), pltpu.VMEM((8, tc_ch), F32)],
        compiler_params=_params(("parallel",)),
    )(z, z, dc, conv_w)


def short_fwd(name, z, conv_w, d_short):
    t = z.shape[0]
    kw = conv_w.shape[0]
    tc_ch = _tile(d_short, CHANNEL_TILE)
    nt = d_short // tc_ch

    def body(b_ref, c_ref, u_ref, w_ref, y_ref, pad):
        pad[pl.ds(0, HALO), :] = jnp.zeros((HALO, tc_ch), F32)

        def fill(s, tc):
            pad[pl.ds(HALO + s, tc), :] = c_ref[pl.ds(s, tc), :].astype(F32) * u_ref[pl.ds(s, tc), :].astype(F32)

        def chunk(s, tc):
            win = pad[pl.ds(s, tc + HALO), :]
            cq = _taps(win, w_ref, [HALO - (kw - 1) + k for k in range(kw)], tc)
            y_ref[pl.ds(s, tc), :] = (b_ref[pl.ds(s, tc), :].astype(F32) * cq).astype(BF16)

        _chunks(t, fill)
        _chunks(t, chunk)

    return pl.pallas_call(
        body, name=name, grid=(nt,),
        in_specs=[pl.BlockSpec((t, tc_ch), lambda j: (0, j)), pl.BlockSpec((t, tc_ch), lambda j: (0, nt + j)),
                  pl.BlockSpec((t, tc_ch), lambda j: (0, 2 * nt + j)), pl.BlockSpec((kw, tc_ch), lambda j: (0, j))],
        out_specs=pl.BlockSpec((t, tc_ch), lambda j: (0, j)),
        out_shape=jax.ShapeDtypeStruct((t, d_short), BF16),
        scratch_shapes=[pltpu.VMEM((t + HALO, tc_ch), F32)],
        compiler_params=_params(("parallel",)),
    )(z, z, z, conv_w)


def short_bwd(name, z, dy, conv_w, d_short):
    t = z.shape[0]
    kw = conv_w.shape[0]
    tc_ch = _tile(d_short, CHANNEL_TILE)
    nt = d_short // tc_ch

    def body(b_ref, c_ref, u_ref, dy_ref, w_ref, db_ref, dcg_ref, du_ref, dw_ref, pad_q, pad_dcq, acc_w):
        pad_q[pl.ds(0, HALO), :] = jnp.zeros((HALO, tc_ch), F32)
        pad_dcq[pl.ds(t, HALO), :] = jnp.zeros((HALO, tc_ch), F32)
        acc_w[...] = jnp.zeros_like(acc_w)

        def fill(s, tc):
            rows = pl.ds(s, tc)
            pad_q[pl.ds(HALO + s, tc), :] = c_ref[rows, :].astype(F32) * u_ref[rows, :].astype(F32)
            pad_dcq[rows, :] = dy_ref[rows, :].astype(F32) * b_ref[rows, :].astype(F32)

        def chunk(s, tc):
            rows = pl.ds(s, tc)
            win_q = pad_q[pl.ds(s, tc + HALO), :]
            dcq = pad_dcq[rows, :]
            cq = None
            for k in range(kw):
                off = HALO - (kw - 1) + k
                shifted = win_q[off:off + tc, :]
                acc_w[pl.ds(8 * k, 8), :] += _fold8(dcq * shifted)
                term = w_ref[k:k + 1, :] * shifted
                cq = term if cq is None else cq + term
            db_ref[rows, :] = (dy_ref[rows, :].astype(F32) * cq).astype(BF16)
            win_d = pad_dcq[pl.ds(s, tc + HALO), :]
            dq = None
            for j in range(kw):
                term = w_ref[kw - 1 - j:kw - j, :] * win_d[j:j + tc, :]
                dq = term if dq is None else dq + term
            dcg_ref[rows, :] = (dq * u_ref[rows, :].astype(F32)).astype(BF16)
            du_ref[rows, :] = (dq * c_ref[rows, :].astype(F32)).astype(BF16)

        _chunks(t, fill)
        _chunks(t, chunk)
        for k in range(kw):
            dw_ref[k:k + 1, :] = jnp.sum(acc_w[pl.ds(8 * k, 8), :], axis=0, keepdims=True)

    col = pl.BlockSpec((t, tc_ch), lambda j: (0, j))
    zspec = [pl.BlockSpec((t, tc_ch), lambda j, o=o: (0, o * nt + j)) for o in range(3)]
    return pl.pallas_call(
        body, name=name, grid=(nt,),
        in_specs=[*zspec, col, pl.BlockSpec((kw, tc_ch), lambda j: (0, j))],
        out_specs=[col, col, col, pl.BlockSpec((kw, tc_ch), lambda j: (0, j))],
        out_shape=[jax.ShapeDtypeStruct((t, d_short), BF16)] * 3 + [jax.ShapeDtypeStruct((kw, d_short), F32)],
        scratch_shapes=[pltpu.VMEM((t + HALO, tc_ch), F32), pltpu.VMEM((t + HALO, tc_ch), F32),
                        pltpu.VMEM((8 * kw, tc_ch), F32)],
        compiler_params=_params(("parallel",)),
    )(z, z, z, dy, conv_w)


def adamw(name, w, m, v, contributions):
    r, c = w.shape
    nc = len(contributions)
    n_slots = contributions[0].shape[0]
    tr = 256 if c <= 1024 else 128
    if any(a.shape[1] % tr for a in contributions):
        assert nc == 1
        tr = r
    tiles = [a.shape[1] // tr for a in contributions]
    first = [sum(tiles[:j]) for j in range(nc)]

    def body(w_ref, m_ref, v_ref, *rest):
        g_refs, (grad_ref, delta_ref, nm_ref, nv_ref) = rest[:nc], rest[nc:]
        i = pl.program_id(0)
        g = None
        for j, g_ref in enumerate(g_refs):
            s = g_ref[0].astype(F32)
            for slot in range(1, n_slots):
                s = s + g_ref[slot].astype(F32)
            g = s if g is None else jnp.where(i >= first[j], s, g)
        nm = ADAM_B1 * m_ref[...] + (1.0 - ADAM_B1) * g
        nv = ADAM_B2 * v_ref[...] + (1.0 - ADAM_B2) * (g * g)
        m_hat = nm / (1.0 - ADAM_B1 ** ADAM_STEP)
        v_hat = nv / (1.0 - ADAM_B2 ** ADAM_STEP)
        grad_ref[...] = g
        delta_ref[...] = -ADAM_LR * (m_hat / (jnp.sqrt(v_hat) + ADAM_EPS) + ADAM_WD * w_ref[...])
        nm_ref[...] = nm
        nv_ref[...] = nv

    blk = pl.BlockSpec((tr, c), lambda i: (i, 0))
    g_specs = [pl.BlockSpec((n_slots, tr, c), lambda i, j=j: (0, jnp.clip(i - first[j], 0, tiles[j] - 1), 0))
               for j in range(nc)]
    return pl.pallas_call(
        body, name=name, grid=(r // tr,),
        in_specs=[blk, blk, blk, *g_specs],
        out_specs=[blk] * 4, out_shape=[jax.ShapeDtypeStruct((r, c), F32)] * 4,
        compiler_params=_params(("parallel",)),
    )(w, m, v, *contributions)


def _pad_rows(a, rows):
    return jnp.pad(a, ((0, rows - a.shape[0]), (0, 0)))


def kernel(x, mix_pre_g, mix_post_g, ffn_pre_g, ffn_post_g, ab_w_in, pool_w, pool_scale, conv_w, conv_b, conv_ln_g, conv_ln_b, ab_w_out, sc_w_in, sc_conv_w, sc_w_out, ffn_w1, ffn_w2, loss_target, m_mix_pre_g, m_mix_post_g, m_ffn_pre_g, m_ffn_post_g, m_ab_w_in, m_pool_w, m_pool_scale, m_conv_w, m_conv_b, m_conv_ln_g, m_conv_ln_b, m_ab_w_out, m_sc_w_in, m_sc_conv_w, m_sc_w_out, m_ffn_w1, m_ffn_w2, v_mix_pre_g, v_mix_post_g, v_ffn_pre_g, v_ffn_post_g, v_ab_w_in, v_pool_w, v_pool_scale, v_conv_w, v_conv_b, v_conv_ln_g, v_conv_ln_b, v_ab_w_out, v_sc_w_in, v_sc_conv_w, v_sc_w_out, v_ffn_w1, v_ffn_w2):
    t, d = x.shape[1], x.shape[2]
    d_pool = pool_scale.shape[1]
    d_conv = conv_b.shape[1]
    d_short = d
    ng, pg = pool_w.shape[1], pool_w.shape[3]
    kw, ks = conv_w.shape[1], sc_conv_w.shape[1]
    nb_ab, nb_sc, nb_ff = ab_w_in.shape[2], sc_w_in.shape[2], ffn_w1.shape[2]

    xs = x[0]
    target = loss_target[0]

    lanes = min(128, d_conv // N_DEV)
    small_rows = [kw * (d_conv // N_DEV) // lanes, ks * (d_short // N_DEV) // lanes, ng * (pg // N_DEV) * pg // lanes]
    small_total = -(-sum(small_rows) // 8) * 8
    r0, r1, r2 = small_rows[0], small_rows[0] + small_rows[1], sum(small_rows)

    def pack_small(a_conv, a_sconv, a_pool):
        parts = [a_conv[0].reshape(-1, lanes), a_sconv[0].reshape(-1, lanes), a_pool[0].reshape(-1, lanes)]
        return _pad_rows(jnp.concatenate(parts, axis=0), small_total)

    shards = {
        "ab_in": (ab_w_in[0], BF16), "small": (pack_small(conv_w, sc_conv_w, pool_w), F32),
        "ab_out": (ab_w_out[0], BF16), "ff1_0": (ffn_w1[0], BF16), "ff2_0": (ffn_w2[0], BF16),
        "sc_in": (sc_w_in[0], BF16), "sc_out": (sc_w_out[0], BF16),
        "ff1_1": (ffn_w1[1], BF16), "ff2_1": (ffn_w2[1], BF16)}
    zones = [place_shard("place_" + nm, w, dt) for nm, (w, dt) in shards.items()]
    started, token = copies_start("gather_start", [[z] for z in zones], _first_hop, 4)
    started = dict(zip(shards, started))

    ties = [0]

    def after(v, *deps):
        ties[0] += 1
        return tie(f"tie_{ties[0]}", v, *deps)

    def fetch_begin(nm, dep):
        (zone,) = copies_wait("gather_wait_" + nm, started[nm], _first_hop, dep)
        (hop,), tok = copies_start("forward_start_" + nm, [[zone]], _second_hop, 3)
        return hop, tok

    def fetch_end(nm, hop, dep):
        return copies_wait("forward_wait_" + nm, hop, _second_hop, dep)[0]

    relu = lambda r: jnp.maximum(r, 0.0)
    square = lambda a: a * a
    relu2_bwd = lambda r, a: r * (2.0 * a.astype(F32))

    def row(vec, l):
        return vec[l:l + 1]

    hop_small, _ = fetch_begin("small", token)
    hop_ab_in, tok = fetch_begin("ab_in", token)
    w_small = fetch_end("small", hop_small, tok)
    w_ab_in = fetch_end("ab_in", hop_ab_in, tok)
    w_conv = w_small[:, :r0].reshape(N_DEV, kw, -1).transpose(1, 0, 2).reshape(kw, d_conv)
    w_sconv = w_small[:, r0:r1].reshape(N_DEV, ks, -1).transpose(1, 0, 2).reshape(ks, d_short)
    w_pool = w_small[:, r1:r2].reshape(N_DEV, ng, -1, pg).transpose(1, 0, 2, 3).reshape(ng, pg, pg).astype(BF16)
    hop, tok = fetch_begin("ab_out", w_ab_in)
    h0 = after(norm_pre("norm_pre", xs, row(mix_pre_g, 0)), tok)
    z0 = mm_nn_blocked("ab_in", h0, w_ab_in, out_dtype=BF16)
    pooled, y_pool = pool_fwd("pool_fwd", z0, w_pool, pool_scale, d_pool)
    cv = conv_fwd("conv_fwd", z0, w_conv, conv_b, d_pool, d_conv)
    y_conv = ln_silu("ln_silu", cv, conv_ln_g, conv_ln_b)
    y0 = jnp.concatenate([y_pool, y_conv], axis=1)
    w_ab_out = fetch_end("ab_out", hop, y0)
    hop, tok = fetch_begin("ff1_0", w_ab_out)
    y0 = after(y0, tok)
    m0 = mm_nn("ab_out", y0, w_ab_out.reshape(d_pool + d_conv, d), out_dtype=F32)
    x1, h1 = post_pre("post_pre_0", xs, m0, row(mix_post_g, 0), row(ffn_pre_g, 0))
    w_ff1_0 = fetch_end("ff1_0", hop, h1)
    hop, tok = fetch_begin("ff2_0", w_ff1_0)
    h1 = after(h1, tok)
    a0 = mm_nn_blocked("ffn0_up", h1, w_ff1_0, out_dtype=BF16, epilogue=relu)
    w_ff2_0 = fetch_end("ff2_0", hop, a0).reshape(-1, d)
    hop, tok = fetch_begin("sc_in", w_ff2_0)
    a0 = after(a0, tok)
    f0 = mm_nn("ffn0_down", a0, w_ff2_0, out_dtype=F32, tk=1024, lhs_fn=square)
    x2, h2 = post_pre("post_pre_1", x1, f0, row(ffn_post_g, 0), row(mix_pre_g, 1))
    w_sc_in = fetch_end("sc_in", hop, h2)
    hop, tok = fetch_begin("sc_out", w_sc_in)
    h2 = after(h2, tok)
    z1 = mm_nn_blocked("sc_in", h2, w_sc_in, out_dtype=BF16)
    y1 = short_fwd("short_fwd", z1, w_sconv, d_short)
    w_sc_out = fetch_end("sc_out", hop, y1).reshape(d_short, d)
    hop, tok = fetch_begin("ff1_1", w_sc_out)
    y1 = after(y1, tok)
    m1 = mm_nn("sc_out", y1, w_sc_out, out_dtype=F32)
    x3, h3 = post_pre("post_pre_2", x2, m1, row(mix_post_g, 1), row(ffn_pre_g, 1))
    w_ff1_1 = fetch_end("ff1_1", hop, h3)
    hop, tok = fetch_begin("ff2_1", w_ff1_1)
    h3 = after(h3, tok)
    a1 = mm_nn_blocked("ffn1_up", h3, w_ff1_1, out_dtype=BF16, epilogue=relu)
    w_ff2_1 = fetch_end("ff2_1", hop, a1).reshape(-1, d)
    f1 = mm_nn("ffn1_down", a1, w_ff2_1, out_dtype=F32, tk=1024, lhs_fn=square)
    dx4, df1, loss_part, dg_ffn_post1 = post_loss("post_loss", x3, f1, row(ffn_post_g, 1), target)
    loss = lax.psum(jnp.sum(loss_part) * (0.5 / d), ("x", "y", "c"))

    def reduce_begin(tag, g):
        zone = lax.empty((N_CHIP,) + g.shape[1:], g.dtype)
        (hop,), tok = copies_start("pair_start_" + tag, [[g, zone]], _pair_hop, N_CHIP)
        return hop, tok

    def reduce_middle(tag, hop, dep):
        g, from_sibling = copies_wait("pair_wait_" + tag, hop, _pair_hop, dep)
        pair_sum, zone = pair_add("pair_add_" + tag, g, from_sibling)
        (hop2,), tok = copies_start("chips_start_" + tag, [[pair_sum, zone]], _chip_hop, 3)
        return hop2, tok

    def reduce_end(tag, hop2, dep):
        return copies_wait("chips_wait_" + tag, hop2, _chip_hop, dep)[1]

    dw = mm_tn("ffn1_dw2", a1, df1, out_dtype=BF16, lhs_fn=square)
    red_ff2_1, tok = reduce_begin("ff2_1", dw.reshape(N_DEV, -1, d))
    df1 = after(df1, tok)
    dpre = mm_nt("ffn1_da", df1, w_ff2_1, out_dtype=BF16, extra=a1, epilogue=relu2_bwd)
    dw = mm_tn_blocked("ffn1_dw1", h3, dpre, nb_ff, out_dtype=BF16)
    red_ff1_1, tok = reduce_begin("ff1_1", dw)
    dpre = after(dpre, tok)
    dh3 = mm_nt_blocked("ffn1_dh", dpre, w_ff1_1, out_dtype=BF16)
    red_ff2_1, tok = reduce_middle("ff2_1", red_ff2_1, dh3)
    dh3 = after(dh3, tok)
    dx3, dm1, dg_ffn_pre1, dg_mix_post1 = bwd_pre_post("bwd_3", dx4, x3, row(ffn_pre_g, 1), dh3, m1, row(mix_post_g, 1))

    dw = mm_tn("sc_dwout", y1, dm1, out_dtype=BF16)
    red_sc_out, tok = reduce_begin("sc_out", dw.reshape(N_DEV, -1, d))
    dm1 = after(dm1, tok)
    dy1 = mm_nt("sc_dy", dm1, w_sc_out, out_dtype=BF16)
    red_ff1_1, tok = reduce_middle("ff1_1", red_ff1_1, dy1)
    dy1 = after(dy1, tok)
    db1, dcg1, du1, dw_sconv = short_bwd("short_bwd", z1, dy1, w_sconv, d_short)
    dz1 = jnp.concatenate([db1, dcg1, du1], axis=1)
    dw = mm_tn_blocked("sc_dwin", h2, dz1, nb_sc, out_dtype=BF16)
    red_sc_in, tok = reduce_begin("sc_in", dw)
    dz1 = after(dz1, tok)
    dh2 = mm_nt_blocked("sc_dh", dz1, w_sc_in, out_dtype=BF16)
    red_sc_out, tok = reduce_middle("sc_out", red_sc_out, dh2)
    dh2 = after(dh2, tok)
    dx2, df0, dg_mix_pre1, dg_ffn_post0 = bwd_pre_post("bwd_2", dx3, x2, row(mix_pre_g, 1), dh2, f0, row(ffn_post_g, 0))

    dw = mm_tn("ffn0_dw2", a0, df0, out_dtype=BF16, lhs_fn=square)
    red_ff2_0, tok = reduce_begin("ff2_0", dw.reshape(N_DEV, -1, d))
    df0 = after(df0, tok)
    dpre = mm_nt("ffn0_da", df0, w_ff2_0, out_dtype=BF16, extra=a0, epilogue=relu2_bwd)
    red_sc_in, tok = reduce_middle("sc_in", red_sc_in, dpre)
    dpre = after(dpre, tok)
    dw = mm_tn_blocked("ffn0_dw1", h1, dpre, nb_ff, out_dtype=BF16)
    red_ff1_0, tok = reduce_begin("ff1_0", dw)
    dpre = after(dpre, tok)
    dh1 = mm_nt_blocked("ffn0_dh", dpre, w_ff1_0, out_dtype=BF16)
    red_ff2_0, tok = reduce_middle("ff2_0", red_ff2_0, dh1)
    dh1 = after(dh1, tok)
    dx1, dm0, dg_ffn_pre0, dg_mix_post0 = bwd_pre_post("bwd_1", dx2, x1, row(ffn_pre_g, 0), dh1, m0, row(mix_post_g, 0))

    dw = mm_tn("ab_dwout", y0, dm0, out_dtype=BF16)
    red_ab_out, tok = reduce_begin("ab_out", dw.reshape(N_DEV, -1, d))
    dm0 = after(dm0, tok)
    dy0 = mm_nt("ab_dy", dm0, w_ab_out.reshape(d_pool + d_conv, d), out_dtype=BF16)
    red_ff1_0, tok = reduce_middle("ff1_0", red_ff1_0, dy0)
    dy0 = after(dy0, tok)
    dcv, dg_ln_g, dg_ln_b = ln_silu_bwd("ln_silu_bwd", cv, conv_ln_g, conv_ln_b, dy0[:, d_pool:])
    dv, dgate, dw_conv, dg_conv_b = conv_bwd("conv_bwd", z0, dcv, w_conv, d_pool, d_conv)
    du0, dw_pool, dg_pool_scale = pool_bwd("pool_bwd", pooled, dy0[:, :d_pool], w_pool, pool_scale)
    dz0 = jnp.concatenate([du0, dv, dgate], axis=1)
    small_parts = [
        dw_conv.reshape(kw, N_DEV, -1).transpose(1, 0, 2).reshape(N_DEV, -1, lanes),
        dw_sconv.reshape(ks, N_DEV, -1).transpose(1, 0, 2).reshape(N_DEV, -1, lanes),
        dw_pool.reshape(ng, N_DEV, pg // N_DEV, pg).transpose(1, 0, 2, 3).reshape(N_DEV, -1, lanes),
    ]
    small = jnp.pad(jnp.concatenate(small_parts, axis=1), ((0, 0), (0, small_total - r2), (0, 0)))
    red_small, tok = reduce_begin("small", small)
    red_ab_out, tok2 = reduce_middle("ab_out", red_ab_out, dz0)
    dz0 = after(dz0, tok, tok2)
    dw = mm_tn_blocked("ab_dwin", h0, dz0, nb_ab, out_dtype=BF16)
    red_ab_in, tok = reduce_begin("ab_in", dw)
    dz0 = after(dz0, tok)
    dh0 = mm_nt_blocked("ab_dh", dz0, w_ab_in, out_dtype=BF16)
    red_small, tok = reduce_middle("small", red_small, dh0)
    dh0 = after(dh0, tok)
    grad_x, dg_mix_pre0 = bwd_pre_final("bwd_0", dx1, xs, row(mix_pre_g, 0), dh0)
    red_ab_in, tok = reduce_middle("ab_in", red_ab_in, grad_x)

    fold = lambda a: jnp.sum(a, axis=0, keepdims=True)
    rep_rows = [fold(dg_mix_pre0), fold(dg_mix_pre1), fold(dg_mix_post0), fold(dg_mix_post1),
                fold(dg_ffn_pre0), fold(dg_ffn_pre1), fold(dg_ffn_post0), fold(dg_ffn_post1)]
    tail = jnp.concatenate([dg_pool_scale, dg_conv_b, fold(dg_ln_g), fold(dg_ln_b)], axis=1).reshape(-1, d)
    rep = _pad_rows(jnp.concatenate(rep_rows + [tail], axis=0), 16)
    rep_all = all_gather([rep], name="gather_small_grads")[0]

    def pack_rep(a_mix_pre, a_mix_post, a_ffn_pre, a_ffn_post, a_scale, a_b, a_g, a_lb):
        tail_ = jnp.concatenate([a_scale, a_b, a_g, a_lb], axis=1).reshape(-1, d)
        return _pad_rows(jnp.concatenate([a_mix_pre, a_mix_post, a_ffn_pre, a_ffn_post, tail_], axis=0), 16)

    def upd(name, w, m, v, contribs):
        shape = w.shape
        flat2 = lambda a: a.reshape(-1, shape[-1])
        outs = adamw(name, flat2(w), flat2(m), flat2(v), contribs)
        return [o.reshape(shape) for o in outs]

    o_rep = adamw("adam_replicated",
                  pack_rep(mix_pre_g, mix_post_g, ffn_pre_g, ffn_post_g, pool_scale, conv_b, conv_ln_g, conv_ln_b),
                  pack_rep(m_mix_pre_g, m_mix_post_g, m_ffn_pre_g, m_ffn_post_g, m_pool_scale, m_conv_b, m_conv_ln_g, m_conv_ln_b),
                  pack_rep(v_mix_pre_g, v_mix_post_g, v_ffn_pre_g, v_ffn_post_g, v_pool_scale, v_conv_b, v_conv_ln_g, v_conv_ln_b),
                  [rep_all])
    g_ff2 = [reduce_end("ff2_0", red_ff2_0, tok), reduce_end("ff2_1", red_ff2_1, tok)]
    o_ff2 = upd("adam_ffn_w2", ffn_w2, m_ffn_w2, v_ffn_w2, g_ff2)
    g_ff1 = [reduce_end("ff1_0", red_ff1_0, tok), reduce_end("ff1_1", red_ff1_1, tok)]
    o_ff1 = upd("adam_ffn_w1", ffn_w1, m_ffn_w1, v_ffn_w1, g_ff1)
    o_sc_out = upd("adam_sc_out", sc_w_out, m_sc_w_out, v_sc_w_out, [reduce_end("sc_out", red_sc_out, tok)])
    o_sc_in = upd("adam_sc_in", sc_w_in, m_sc_w_in, v_sc_w_in, [reduce_end("sc_in", red_sc_in, tok)])
    o_ab_out = upd("adam_ab_out", ab_w_out, m_ab_w_out, v_ab_w_out, [reduce_end("ab_out", red_ab_out, tok)])
    o_small = adamw("adam_small", pack_small(conv_w, sc_conv_w, pool_w), pack_small(m_conv_w, m_sc_conv_w, m_pool_w),
                    pack_small(v_conv_w, v_sc_conv_w, v_pool_w), [reduce_end("small", red_small, tok)])
    o_ab_in = upd("adam_ab_in", ab_w_in, m_ab_w_in, v_ab_w_in, [reduce_end("ab_in", red_ab_in, tok)])

    def unpack_small(o):
        return o[:r0].reshape(conv_w.shape), o[r0:r1].reshape(sc_conv_w.shape), o[r1:r2].reshape(pool_w.shape)

    def unpack_rep(o):
        tail_ = o[8:8 + tail.shape[0]].reshape(1, -1)
        n1 = d_pool
        return dict(mix_pre_g=o[0:2], mix_post_g=o[2:4], ffn_pre_g=o[4:6], ffn_post_g=o[6:8],
                    pool_scale=tail_[:, :n1], conv_b=tail_[:, n1:n1 + d_conv],
                    conv_ln_g=tail_[:, n1 + d_conv:n1 + 2 * d_conv], conv_ln_b=tail_[:, n1 + 2 * d_conv:n1 + 3 * d_conv])

    results = []
    for kind in range(4):
        rep_o = unpack_rep(o_rep[kind])
        s_conv, s_sconv, s_pool = unpack_small(o_small[kind])
        results.append([
            rep_o["mix_pre_g"], rep_o["mix_post_g"], rep_o["ffn_pre_g"], rep_o["ffn_post_g"],
            o_ab_in[kind], s_pool, rep_o["pool_scale"], s_conv, rep_o["conv_b"], rep_o["conv_ln_g"], rep_o["conv_ln_b"],
            o_ab_out[kind], o_sc_in[kind], s_sconv, o_sc_out[kind], o_ff1[kind], o_ff2[kind]])

    return (loss, grad_x[None], *results[0], *results[1], *results[2], *results[3])
```

```python
import functools

import jax
import jax.numpy as jnp
from jax import lax
from jax.experimental import pallas as pl
from jax.experimental.pallas import tpu as pltpu

F32 = jnp.float32
BF16 = jnp.bfloat16
MESH = pl.DeviceIdType.MESH
ANY = pl.BlockSpec(memory_space=pl.ANY)

NORM_EPS = 1e-6
POOL_WINDOWS = (2, 4, 8, 16)
MAX_POOL_WINDOW = 16
ADAM_LR = 0.001
ADAM_B1 = 0.9
ADAM_B2 = 0.999
ADAM_EPS = 1e-08
ADAM_WD = 0.01
ADAM_STEP = 10

N_DEV = 8
VMEM_LIMIT = 56 * 1024 * 1024
PAIR_ADD_BLOCK = 1 << 20
ROW_TILE = 256
CHANNEL_TILE = 256
TIME_CHUNK = 64
HALO = 32

NN = (((1,), (0,)), ((), ()))
NT = (((1,), (1,)), ((), ()))
TN = (((0,), (0,)), ((), ()))


def _params(sem):
    return pltpu.CompilerParams(dimension_semantics=sem, vmem_limit_bytes=VMEM_LIMIT)


def _place():
    x, y, c = lax.axis_index("x"), lax.axis_index("y"), lax.axis_index("c")
    return x, y, c


def _slot(px, py, pc):
    return 4 * px + 2 * py + pc


def all_gather(arrs, name):
    n = len(arrs)

    def body(*refs):
        ins, outs = refs[:n], refs[n:2 * n]
        send_sems, recv_sems, local_sems = refs[2 * n:]
        x, y, c = _place()
        me = _slot(x, y, c)
        sibling = (x, y, 1 - c)
        chips = [(1 - x, y), (x, 1 - y), (1 - x, 1 - y)]

        def copy(a, k, slot, to, src=None):
            return pltpu.make_async_remote_copy(
                src_ref=outs[a].at[slot] if src is None else src, dst_ref=outs[a].at[slot],
                send_sem=send_sems.at[a, k], recv_sem=recv_sems.at[a, k], device_id=to, device_id_type=MESH)

        started = []
        for a in range(n):
            mine = pltpu.make_async_copy(ins[a], outs[a].at[me], local_sems.at[a])
            mine.start()
            started.append(mine)
        sends = []
        for a in range(n):
            first = [copy(a, 0, me, sibling, src=ins[a])]
            first += [copy(a, 1 + j, me, (px, py, c), src=ins[a]) for j, (px, py) in enumerate(chips)]
            for cp in first:
                cp.start()
            sends += first
        for a in range(n):
            for j, (px, py) in enumerate(chips):
                copy(a, 1 + j, _slot(px, py, c), (x, y, c)).wait_recv()
                fwd = copy(a, 4 + j, _slot(px, py, c), sibling)
                fwd.start()
                sends.append(fwd)
        for a in range(n):
            copy(a, 0, _slot(x, y, 1 - c), (x, y, c)).wait_recv()
            for j, (px, py) in enumerate(chips):
                copy(a, 4 + j, _slot(px, py, 1 - c), (x, y, c)).wait_recv()
        for cp in sends:
            cp.wait_send()
        for mine in started:
            mine.wait()

    outs = pl.pallas_call(
        body, name=name,
        out_shape=[jax.ShapeDtypeStruct((N_DEV,) + a.shape, a.dtype) for a in arrs],
        in_specs=[ANY] * n, out_specs=[ANY] * n,
        scratch_shapes=[pltpu.SemaphoreType.DMA((n, 7)), pltpu.SemaphoreType.DMA((n, 7)),
                        pltpu.SemaphoreType.DMA((n,))],
    )(*arrs)
    return list(outs)


HBM = pl.BlockSpec(memory_space=pltpu.HBM)
SEM = pl.BlockSpec(memory_space=pltpu.SEMAPHORE)
EFFECT = pltpu.SideEffectType.DATAFLOW_SIDE_EFFECTING
TOKEN = jax.ShapeDtypeStruct((8, 128), F32)


def _in_hbm(a):
    return pltpu.with_memory_space_constraint(a, pltpu.HBM)


def _first_hop_peers(x, y, c):
    return [(x, y, 1 - c), (1 - x, y, c), (x, 1 - y, c), (1 - x, 1 - y, c)]


def gather_start(shards, name):
    n = len(shards)

    def body(*refs):
        ins, lands = refs[:n], refs[n:2 * n]
        sends, recvs = refs[2 * n:3 * n], refs[3 * n:4 * n]
        token = refs[-1]
        x, y, c = _place()
        me = _slot(x, y, c)
        for a in range(n):
            for k, to in enumerate(_first_hop_peers(x, y, c)):
                pltpu.make_async_remote_copy(
                    src_ref=ins[a], dst_ref=lands[a].at[me], send_sem=sends[a].at[k], recv_sem=recvs[a].at[k],
                    device_id=to, device_id_type=MESH).start()
        token[...] = jnp.zeros_like(token)

    lands = [lax.empty((N_DEV,) + s.shape, s.dtype) for s in shards]
    outs = pl.pallas_call(
        body, name=name,
        out_shape=([pltpu.SemaphoreType.DMA((4,))] * (2 * n) + [pltpu.HBM(s.shape, s.dtype) for s in shards]
                   + [pltpu.HBM(l.shape, l.dtype) for l in lands] + [TOKEN]),
        in_specs=[HBM] * (2 * n),
        out_specs=[SEM] * (2 * n) + [HBM] * (2 * n) + [pl.BlockSpec(memory_space=pltpu.VMEM)],
        input_output_aliases={i: 2 * n + i for i in range(2 * n)},
        compiler_params=pltpu.CompilerParams(has_side_effects=EFFECT),
    )(*[_in_hbm(s) for s in shards], *[_in_hbm(l) for l in lands])
    per_shard = [(outs[a], outs[n + a], outs[2 * n + a], outs[3 * n + a]) for a in range(n)]
    return per_shard, outs[-1]


def gather_wait(name, started, after):
    send_sems, recv_sems, shard, land = started

    def body(shard_ref, land_ref, sends, recvs, after_ref, shard_out, land_out):
        x, y, c = _place()
        for k, (px, py, pc) in enumerate(_first_hop_peers(x, y, c)):
            cp = pltpu.make_async_remote_copy(
                src_ref=shard_ref, dst_ref=land_ref.at[_slot(px, py, pc)], send_sem=sends.at[k], recv_sem=recvs.at[k],
                device_id=(px, py, pc), device_id_type=MESH)
            cp.wait_send()
            cp.wait_recv()

    return pl.pallas_call(
        body, name=name,
        out_shape=(pltpu.HBM(shard.shape, shard.dtype), pltpu.HBM(land.shape, land.dtype)),
        in_specs=(HBM, HBM, SEM, SEM, ANY), out_specs=(HBM, HBM), input_output_aliases={0: 0, 1: 1},
        compiler_params=pltpu.CompilerParams(has_side_effects=EFFECT),
    )(shard, land, send_sems, recv_sems, after)


def gather_finish(waited, name):
    n = len(waited)

    def body(*refs):
        shards, lands = refs[:n], refs[2 * n:3 * n]
        send_sems, recv_sems, local_sems = refs[3 * n:]
        x, y, c = _place()
        chips = [(1 - x, y), (x, 1 - y), (1 - x, 1 - y)]
        local, copies = [], []
        for a in range(n):
            cp = pltpu.make_async_copy(shards[a], lands[a].at[_slot(x, y, c)], local_sems.at[a])
            cp.start()
            local.append(cp)
            for k, (px, py) in enumerate(chips):
                cp = pltpu.make_async_remote_copy(
                    src_ref=lands[a].at[_slot(px, py, c)], dst_ref=lands[a].at[_slot(px, py, c)],
                    send_sem=send_sems.at[a, k], recv_sem=recv_sems.at[a, k],
                    device_id=(x, y, 1 - c), device_id_type=MESH)
                cp.start()
                copies.append(cp)
        for cp in copies:
            cp.wait()
        for cp in local:
            cp.wait()

    outs = pl.pallas_call(
        body, name=name,
        out_shape=[jax.ShapeDtypeStruct(l.shape, l.dtype) for _, l in waited],
        in_specs=[ANY] * (2 * n), out_specs=[ANY] * n,
        input_output_aliases={n + a: a for a in range(n)},
        scratch_shapes=[pltpu.SemaphoreType.DMA((n, 3)), pltpu.SemaphoreType.DMA((n, 3)),
                        pltpu.SemaphoreType.DMA((n,))],
    )(*[s for s, _ in waited], *[l for _, l in waited])
    return list(outs)


CHIPS = [(0, 0), (0, 1), (1, 0), (1, 1)]
N_CHIP = len(CHIPS)


def _chip(px, py):
    return 2 * px + py


def _first_hop(bufs, sends, recvs, waiting):
    (land,) = bufs
    x, y, c = _place()
    me = _slot(x, y, c)
    peers = [(x, y, 1 - c), (1 - x, y, c), (x, 1 - y, c), (1 - x, 1 - y, c)]
    return [pltpu.make_async_remote_copy(
        src_ref=land.at[me], dst_ref=land.at[_slot(*p) if waiting else me],
        send_sem=sends.at[k], recv_sem=recvs.at[k], device_id=p, device_id_type=MESH) for k, p in enumerate(peers)]


def _second_hop(bufs, sends, recvs, waiting):
    (land,) = bufs
    x, y, c = _place()
    return [pltpu.make_async_remote_copy(
        src_ref=land.at[_slot(px, py, c)], dst_ref=land.at[_slot(px, py, 1 - c if waiting else c)],
        send_sem=sends.at[k], recv_sem=recvs.at[k], device_id=(x, y, 1 - c), device_id_type=MESH)
        for k, (px, py) in enumerate([(1 - x, y), (x, 1 - y), (1 - x, 1 - y)])]


def _pair_hop(bufs, sends, recvs, waiting):
    g, land = bufs
    x, y, c = _place()
    return [pltpu.make_async_remote_copy(
        src_ref=g.at[_slot(qx, qy, 1 - c)], dst_ref=land.at[q],
        send_sem=sends.at[q], recv_sem=recvs.at[q], device_id=(x, y, 1 - c), device_id_type=MESH)
        for q, (qx, qy) in enumerate(CHIPS)]


def _chip_hop(bufs, sends, recvs, waiting):
    p, land = bufs
    x, y, c = _place()
    return [pltpu.make_async_remote_copy(
        src_ref=p.at[_chip(px, py)], dst_ref=land.at[_chip(px, py) if waiting else _chip(x, y)],
        send_sem=sends.at[k], recv_sem=recvs.at[k], device_id=(px, py, c), device_id_type=MESH)
        for k, (px, py) in enumerate([(1 - x, y), (x, 1 - y), (1 - x, 1 - y)])]


def copies_start(name, groups, hop, n_copies):
    flat = [b for grp in groups for b in grp]
    nb, ng = len(flat), len(groups)

    def body(*refs):
        ins, sems, token = refs[:nb], refs[nb:nb + 2 * ng], refs[-1]
        i = 0
        for gi, grp in enumerate(groups):
            for cp in hop(ins[i:i + len(grp)], sems[2 * gi], sems[2 * gi + 1], False):
                cp.start()
            i += len(grp)
        token[...] = jnp.zeros_like(token)

    outs = pl.pallas_call(
        body, name=name,
        out_shape=([pltpu.SemaphoreType.DMA((n_copies,))] * (2 * ng) + [pltpu.HBM(b.shape, b.dtype) for b in flat]
                   + [TOKEN]),
        in_specs=[HBM] * nb,
        out_specs=[SEM] * (2 * ng) + [HBM] * nb + [pl.BlockSpec(memory_space=pltpu.VMEM)],
        input_output_aliases={i: 2 * ng + i for i in range(nb)},
        compiler_params=pltpu.CompilerParams(has_side_effects=EFFECT),
    )(*[_in_hbm(b) for b in flat])
    started, i = [], 0
    for gi, grp in enumerate(groups):
        started.append((outs[2 * gi], outs[2 * gi + 1], list(outs[2 * ng + i:2 * ng + i + len(grp)])))
        i += len(grp)
    return started, outs[-1]


def copies_wait(name, started, hop, after):
    sends, recvs, bufs = started
    nb = len(bufs)

    def body(*refs):
        for cp in hop(refs[:nb], refs[nb], refs[nb + 1], True):
            cp.wait_send()
            cp.wait_recv()

    outs = pl.pallas_call(
        body, name=name,
        out_shape=[pltpu.HBM(b.shape, b.dtype) for b in bufs],
        in_specs=[HBM] * nb + [SEM, SEM, ANY], out_specs=[HBM] * nb,
        input_output_aliases={i: i for i in range(nb)},
        compiler_params=pltpu.CompilerParams(has_side_effects=EFFECT),
    )(*bufs, sends, recvs, after)
    return list(outs)


def place_shard(name, w, dtype):
    r, c = w.shape
    tr = _tile(r, 256)
    x, y, core = _place()
    me = _slot(x, y, core).astype(jnp.int32).reshape(1)

    def body(me_ref, w_ref, o_ref):
        o_ref[...] = w_ref[...].astype(dtype)

    return pl.pallas_call(
        body, name=name,
        grid_spec=pltpu.PrefetchScalarGridSpec(
            num_scalar_prefetch=1, grid=(r // tr,),
            in_specs=[pl.BlockSpec((tr, c), lambda i, me_ref: (i, 0))],
            out_specs=pl.BlockSpec((None, tr, c), lambda i, me_ref: (me_ref[0], i, 0))),
        out_shape=jax.ShapeDtypeStruct((N_DEV, r, c), dtype),
        compiler_params=_params(("parallel",)),
    )(me, w)


def tie(name, x, *deps):
    def body(*refs):
        del refs

    return pl.pallas_call(
        body, name=name, out_shape=jax.ShapeDtypeStruct(x.shape, x.dtype),
        in_specs=[ANY] * (1 + len(deps)), out_specs=ANY, input_output_aliases={0: 0},
    )(x, *deps)


def exchange_pair(arrs, name):
    n = len(arrs)

    def body(*refs):
        ins, outs = refs[:n], refs[n:2 * n]
        send_sems, recv_sems = refs[2 * n:]
        x, y, c = _place()
        copies = []
        for a in range(n):
            for q, (qx, qy) in enumerate(CHIPS):
                cp = pltpu.make_async_remote_copy(
                    src_ref=ins[a].at[_slot(qx, qy, 1 - c)], dst_ref=outs[a].at[q],
                    send_sem=send_sems.at[a, q], recv_sem=recv_sems.at[a, q],
                    device_id=(x, y, 1 - c), device_id_type=MESH)
                cp.start()
                copies.append(cp)
        for cp in copies:
            cp.wait()

    outs = pl.pallas_call(
        body, name=name,
        out_shape=[jax.ShapeDtypeStruct((N_CHIP,) + a.shape[1:], a.dtype) for a in arrs],
        in_specs=[ANY] * n, out_specs=[ANY] * n,
        scratch_shapes=[pltpu.SemaphoreType.DMA((n, N_CHIP)), pltpu.SemaphoreType.DMA((n, N_CHIP))],
    )(*arrs)
    return list(outs)


def pair_add(name, g, from_sibling):
    _, r, c_dim = g.shape
    tr = r
    while tr * c_dim > PAIR_ADD_BLOCK and tr % 16 == 0:
        tr //= 2
    x, y, core = _place()
    where = jnp.stack([core, _chip(x, y)]).astype(jnp.int32)

    def body(where_ref, g_ref, s_ref, o_ref, zone_ref):
        total = (g_ref[...].astype(F32) + s_ref[...].astype(F32)).astype(o_ref.dtype)
        o_ref[...] = total

        @pl.when(pl.program_id(1) == where_ref[1])
        def _():
            zone_ref[...] = total

    blk = pl.BlockSpec((None, tr, c_dim), lambda i, q, where_ref: (q, i, 0))
    return pl.pallas_call(
        body, name=name,
        grid_spec=pltpu.PrefetchScalarGridSpec(
            num_scalar_prefetch=1, grid=(r // tr, N_CHIP),
            in_specs=[pl.BlockSpec((None, None, tr, c_dim), lambda i, q, where_ref: (q, where_ref[0], i, 0)), blk],
            out_specs=[blk, pl.BlockSpec((None, tr, c_dim), lambda i, q, where_ref: (where_ref[1], i, 0))]),
        out_shape=[jax.ShapeDtypeStruct((N_CHIP, r, c_dim), g.dtype)] * 2,
        compiler_params=_params(("parallel", "arbitrary")),
    )(where, g.reshape(N_CHIP, 2, r, c_dim), from_sibling)


def exchange_chips(groups, name):
    arrs = [a for grp in groups for a in grp]
    n = len(arrs)
    where = []
    for o, grp in enumerate(groups):
        off = 0
        for a in grp:
            where.append((o, off))
            off += a.shape[1]
    flips = [(1, 0), (0, 1), (1, 1)]

    def body(*refs):
        ins, outs = refs[:n], refs[n:n + len(groups)]
        send_sems, recv_sems, local_sems = refs[n + len(groups):]
        x, y, c = _place()
        my_chip = 2 * x + y

        def landing(a, slot):
            o, off = where[a]
            return outs[o].at[slot, pl.ds(off, arrs[a].shape[1])]

        local = []
        for a in range(n):
            cp = pltpu.make_async_copy(ins[a].at[my_chip], landing(a, my_chip), local_sems.at[a])
            cp.start()
            local.append(cp)
        copies = []
        for a in range(n):
            for k, (dx, dy) in enumerate(flips):
                px = 1 - x if dx else x
                py = 1 - y if dy else y
                cp = pltpu.make_async_remote_copy(
                    src_ref=ins[a].at[2 * px + py], dst_ref=landing(a, my_chip),
                    send_sem=send_sems.at[a, k], recv_sem=recv_sems.at[a, k],
                    device_id=(px, py, c), device_id_type=MESH)
                cp.start()
                copies.append(cp)
        for cp in copies:
            cp.wait()
        for cp in local:
            cp.wait()

    outs = pl.pallas_call(
        body, name=name,
        out_shape=[jax.ShapeDtypeStruct((N_CHIP, sum(a.shape[1] for a in grp), grp[0].shape[2]), grp[0].dtype)
                   for grp in groups],
        in_specs=[ANY] * n, out_specs=[ANY] * len(groups),
        scratch_shapes=[pltpu.SemaphoreType.DMA((n, 3)), pltpu.SemaphoreType.DMA((n, 3)),
                        pltpu.SemaphoreType.DMA((n,))],
    )(*arrs)
    return list(outs)


def _matmul(name, lhs, rhs, *, out_shape, out_dtype, grid, lhs_spec, rhs_spec, out_spec, dims, acc_shape,
            lhs_fn=None, extra=(), extra_specs=(), epilogue=None, parts=1):
    nk = grid[2]
    n_extra = len(extra)

    def body(*refs):
        lhs_ref, rhs_ref = refs[0], refs[1]
        extra_refs = refs[2:2 + n_extra]
        out_ref = refs[2 + n_extra]

        def product():
            if parts == 1:
                a = lhs_ref[...]
                if lhs_fn is not None:
                    a = lhs_fn(a)
                return lax.dot_general(a, rhs_ref[...], dims, preferred_element_type=F32)
            width = lhs_ref.shape[1] // parts
            total = None
            for b in range(parts):
                term = lax.dot_general(lhs_ref[:, b * width:(b + 1) * width], rhs_ref[b], dims,
                                       preferred_element_type=F32)
                total = term if total is None else total + term
            return total

        def finish(r):
            if epilogue is not None:
                r = epilogue(r, *[e[...] for e in extra_refs])
            out_ref[...] = r.astype(out_dtype)

        if nk == 1:
            finish(product())
        else:
            acc_ref = refs[3 + n_extra]
            k = pl.program_id(2)

            @pl.when(k == 0)
            def _():
                acc_ref[...] = product()

            @pl.when(jnp.logical_and(k > 0, k < nk - 1))
            def _():
                acc_ref[...] += product()

            @pl.when(k == nk - 1)
            def _():
                finish(acc_ref[...] + product())

    return pl.pallas_call(
        body, name=name, grid=grid,
        out_shape=jax.ShapeDtypeStruct(out_shape, out_dtype),
        in_specs=[lhs_spec, rhs_spec, *extra_specs], out_specs=out_spec,
        scratch_shapes=[pltpu.VMEM(acc_shape, F32)] if nk > 1 else [],
        compiler_params=_params(("parallel", "parallel", "arbitrary")),
    )(lhs, rhs, *extra)


def _tile(n, want):
    return want if n % want == 0 else n


def mm_nn(name, x, w, *, out_dtype, tn=512, tk=None, lhs_fn=None, epilogue=None):
    t, kdim = x.shape
    n = w.shape[1]
    tn = _tile(n, tn)
    tk = kdim if tk is None else _tile(kdim, tk)
    return _matmul(
        name, x, w, out_shape=(t, n), out_dtype=out_dtype, grid=(1, n // tn, kdim // tk),
        lhs_spec=pl.BlockSpec((t, tk), lambda i, j, k: (i, k)),
        rhs_spec=pl.BlockSpec((tk, tn), lambda i, j, k: (k, j)),
        out_spec=pl.BlockSpec((t, tn), lambda i, j, k: (i, j)),
        dims=NN, acc_shape=(t, tn), lhs_fn=lhs_fn, epilogue=epilogue)


def mm_nn_blocked(name, x, w, *, out_dtype, epilogue=None):
    t, kdim = x.shape
    nb = w.shape[2]
    tn = nb // 2 if nb >= 1024 else nb
    sub = nb // tn
    return _matmul(
        name, x, w, out_shape=(t, N_DEV * nb), out_dtype=out_dtype, grid=(1, N_DEV * sub, 1),
        lhs_spec=pl.BlockSpec((t, kdim), lambda i, j, k: (i, k)),
        rhs_spec=pl.BlockSpec((None, kdim, tn), lambda i, j, k: (j // sub, k, j % sub)),
        out_spec=pl.BlockSpec((t, tn), lambda i, j, k: (i, j)),
        dims=NN, acc_shape=(t, tn), epilogue=epilogue)


def mm_nt(name, dy, w, *, out_dtype, tn=512, extra=None, epilogue=None):
    t, n = dy.shape
    kdim = w.shape[0]
    tn = _tile(kdim, tn)
    extra_arrs = () if extra is None else (extra,)
    extra_specs = () if extra is None else (pl.BlockSpec((t, tn), lambda i, j, k: (i, j)),)
    return _matmul(
        name, dy, w, out_shape=(t, kdim), out_dtype=out_dtype, grid=(1, kdim // tn, 1),
        lhs_spec=pl.BlockSpec((t, n), lambda i, j, k: (i, k)),
        rhs_spec=pl.BlockSpec((tn, n), lambda i, j, k: (j, k)),
        out_spec=pl.BlockSpec((t, tn), lambda i, j, k: (i, j)),
        dims=NT, acc_shape=(t, tn), extra=extra_arrs, extra_specs=extra_specs, epilogue=epilogue)


def mm_nt_blocked(name, dz, w, *, out_dtype, tn=512):
    t = dz.shape[0]
    kdim, nb = w.shape[1], w.shape[2]
    tn = _tile(kdim, tn)
    parts = 2
    return _matmul(
        name, dz, w, out_shape=(t, kdim), out_dtype=out_dtype, grid=(1, kdim // tn, N_DEV // parts),
        lhs_spec=pl.BlockSpec((t, parts * nb), lambda i, j, k: (i, k)),
        rhs_spec=pl.BlockSpec((parts, tn, nb), lambda i, j, k: (k, j, 0)),
        out_spec=pl.BlockSpec((t, tn), lambda i, j, k: (i, j)),
        dims=NT, acc_shape=(t, tn), parts=parts)


def mm_tn(name, x, dy, *, out_dtype, tk=1024, tn=1024, lhs_fn=None):
    t, kdim = x.shape
    n = dy.shape[1]
    tk, tn = _tile(kdim, tk), _tile(n, tn)
    return _matmul(
        name, x, dy, out_shape=(kdim, n), out_dtype=out_dtype, grid=(kdim // tk, n // tn, 1),
        lhs_spec=pl.BlockSpec((t, tk), lambda i, j, k: (k, i)),
        rhs_spec=pl.BlockSpec((t, tn), lambda i, j, k: (k, j)),
        out_spec=pl.BlockSpec((tk, tn), lambda i, j, k: (i, j)),
        dims=TN, acc_shape=(tk, tn), lhs_fn=lhs_fn)


def mm_tn_blocked(name, x, dz, nb, *, out_dtype, tk=1024):
    t, kdim = x.shape
    tk = _tile(kdim, tk)
    return _matmul(
        name, x, dz, out_shape=(N_DEV, kdim, nb), out_dtype=out_dtype, grid=(kdim // tk, N_DEV, 1),
        lhs_spec=pl.BlockSpec((t, tk), lambda i, j, k: (k, i)),
        rhs_spec=pl.BlockSpec((t, nb), lambda i, j, k: (k, j)),
        out_spec=pl.BlockSpec((None, tk, nb), lambda i, j, k: (j, i, 0)),
        dims=TN, acc_shape=(tk, nb))


def _rstd(v):
    return lax.rsqrt(jnp.mean(v * v, axis=-1, keepdims=True) + NORM_EPS)


def _rms_bwd(v, g, dy):
    r = _rstd(v)
    vhat = v * r
    dvh = dy * g
    dv = r * (dvh - vhat * jnp.mean(dvh * vhat, axis=-1, keepdims=True))
    return dv, dy * vhat


def _fold8(v):
    rows, n = v.shape
    return jnp.sum(v.reshape(rows // 8, 8, n), axis=0)


def _fold_lanes(v):
    out = v[:, 0:128]
    for i in range(1, v.shape[1] // 128):
        out = out + v[:, 128 * i:128 * (i + 1)]
    return out


def _accumulate(ref, v):
    i = pl.program_id(0)

    @pl.when(i == 0)
    def _():
        ref[...] = v

    @pl.when(i > 0)
    def _():
        ref[...] += v


def _row_call(body, name, t, ins, row_in, outs, acc_outs=(), tr=ROW_TILE):
    tr = _tile(t, tr)
    in_specs = [pl.BlockSpec((tr, a.shape[1]), lambda i: (i, 0)) if tiled
                else pl.BlockSpec(a.shape, lambda i: (0, 0)) for a, tiled in zip(ins, row_in)]
    out_specs = [pl.BlockSpec((tr, n), lambda i: (i, 0)) for n, _ in outs]
    out_specs += [pl.BlockSpec((8, n), lambda i: (0, 0)) for n in acc_outs]
    out_shape = [jax.ShapeDtypeStruct((t, n), dt) for n, dt in outs]
    out_shape += [jax.ShapeDtypeStruct((8, n), F32) for n in acc_outs]
    return pl.pallas_call(
        body, name=name, grid=(t // tr,), in_specs=in_specs, out_specs=out_specs, out_shape=out_shape,
        compiler_params=_params(("arbitrary",) if acc_outs else ("parallel",)),
    )(*ins)


def norm_pre(name, x, g):
    t, d = x.shape

    def body(x_ref, g_ref, h_ref):
        v = x_ref[...]
        h_ref[...] = (v * _rstd(v) * g_ref[...]).astype(BF16)

    return _row_call(body, name, t, [x, g], [True, False], [(d, BF16)])[0]


def post_pre(name, x, m, g_post, g_pre):
    t, d = x.shape

    def body(x_ref, m_ref, gp_ref, gn_ref, xo_ref, h_ref):
        mv = m_ref[...]
        xn = x_ref[...] + mv * _rstd(mv) * gp_ref[...]
        xo_ref[...] = xn
        h_ref[...] = (xn * _rstd(xn) * gn_ref[...]).astype(BF16)

    return _row_call(body, name, t, [x, m, g_post, g_pre], [True, True, False, False], [(d, F32), (d, BF16)])


def post_loss(name, x, f, g_post, target):
    t, d = x.shape

    def body(x_ref, f_ref, g_ref, t_ref, dx_ref, df_ref, loss_ref, dg_ref):
        fv = f_ref[...]
        g = g_ref[...]
        out = x_ref[...] + fv * _rstd(fv) * g
        err = out - t_ref[...]
        dx = err * (1.0 / d)
        dx_ref[...] = dx
        dfv, dg_rows = _rms_bwd(fv, g, dx)
        df_ref[...] = dfv.astype(BF16)
        _accumulate(loss_ref, _fold8(_fold_lanes(err * err)))
        _accumulate(dg_ref, _fold8(dg_rows))

    return _row_call(body, name, t, [x, f, g_post, target], [True, True, False, True],
                     [(d, F32), (d, BF16)], acc_outs=(128, d))


def bwd_pre_post(name, dx_out, x_in, g_pre, dh, f_prev, g_post_prev):
    t, d = x_in.shape

    def body(dxo_ref, x_ref, gpre_ref, dh_ref, f_ref, gpost_ref, dxi_ref, df_ref, dgpre_ref, dgpost_ref):
        dxv, dgpre_rows = _rms_bwd(x_ref[...], gpre_ref[...], dh_ref[...].astype(F32))
        dxi = dxo_ref[...] + dxv
        dxi_ref[...] = dxi
        dfv, dgpost_rows = _rms_bwd(f_ref[...], gpost_ref[...], dxi)
        df_ref[...] = dfv.astype(BF16)
        _accumulate(dgpre_ref, _fold8(dgpre_rows))
        _accumulate(dgpost_ref, _fold8(dgpost_rows))

    return _row_call(body, name, t, [dx_out, x_in, g_pre, dh, f_prev, g_post_prev],
                     [True, True, False, True, True, False], [(d, F32), (d, BF16)], acc_outs=(d, d))


def bwd_pre_final(name, dx_out, x_in, g_pre, dh):
    t, d = x_in.shape

    def body(dxo_ref, x_ref, gpre_ref, dh_ref, dxi_ref, dgpre_ref):
        dxv, dgpre_rows = _rms_bwd(x_ref[...], gpre_ref[...], dh_ref[...].astype(F32))
        dxi_ref[...] = dxo_ref[...] + dxv
        _accumulate(dgpre_ref, _fold8(dgpre_rows))

    return _row_call(body, name, t, [dx_out, x_in, g_pre, dh], [True, True, False, True], [(d, F32)], acc_outs=(d,))


def _layer_norm_parts(cv):
    mu = jnp.mean(cv, axis=-1, keepdims=True)
    xc = cv - mu
    rstd = lax.rsqrt(jnp.mean(xc * xc, axis=-1, keepdims=True) + NORM_EPS)
    return xc * rstd, rstd


def ln_silu(name, cv, g, b):
    t, n = cv.shape

    def body(c_ref, g_ref, b_ref, y_ref):
        chat, _ = _layer_norm_parts(c_ref[...])
        ln = chat * g_ref[...] + b_ref[...]
        y_ref[...] = (ln * jax.nn.sigmoid(ln)).astype(BF16)

    return _row_call(body, name, t, [cv, g, b], [True, False, False], [(n, BF16)])[0]


def ln_silu_bwd(name, cv, g, b, dy):
    t, n = cv.shape

    def body(c_ref, g_ref, b_ref, dy_ref, dc_ref, dg_ref, db_ref):
        chat, rstd = _layer_norm_parts(c_ref[...])
        g = g_ref[...]
        ln = chat * g + b_ref[...]
        s = jax.nn.sigmoid(ln)
        dln = dy_ref[...].astype(F32) * (s * (1.0 + ln * (1.0 - s)))
        dchat = dln * g
        dc_ref[...] = rstd * (dchat - jnp.mean(dchat, axis=-1, keepdims=True)
                              - chat * jnp.mean(dchat * chat, axis=-1, keepdims=True))
        _accumulate(dg_ref, _fold8(dln * chat))
        _accumulate(db_ref, _fold8(dln))

    return _row_call(body, name, t, [cv, g, b, dy], [True, False, False, True], [(n, F32)], acc_outs=(n, n))


def _chunks(t, fn, tc=TIME_CHUNK):
    tc = _tile(t, tc)

    def step(i, carry):
        fn(pl.multiple_of(i * tc, tc), tc)
        return carry

    lax.fori_loop(0, t // tc, step, 0)


def _taps(window, w_ref, offsets, tc):
    acc = None
    for k, off in enumerate(offsets):
        term = w_ref[k:k + 1, :] * window[off:off + tc, :]
        acc = term if acc is None else acc + term
    return acc


def _window_sums(win, tc, causal):
    sums = []
    cur, rows, step = win, tc + HALO, 1
    for _ in POOL_WINDOWS:
        rows -= 8
        if causal:
            cur = cur[8:8 + rows, :] + cur[8 - step:8 - step + rows, :]
            sums.append(cur[rows - tc:rows, :])
        else:
            cur = cur[0:rows, :] + cur[step:step + rows, :]
            sums.append(cur[0:tc, :])
        step *= 2
    return sums


def _pick(vals, g):
    out = vals[-1]
    for i in range(len(vals) - 2, -1, -1):
        out = jnp.where(g == i, vals[i], out)
    return out


def _pool_count(s, tc, g):
    t1 = (lax.broadcasted_iota(jnp.int32, (tc, 1), 0) + (s + 1)).astype(F32)
    width = _pick([float(w) for w in POOL_WINDOWS], g)
    return jnp.minimum(t1, width)


def pool_fwd(name, z, pool_w, pool_scale, d_pool):
    t = z.shape[0]
    ng, pg = pool_w.shape[0], pool_w.shape[1]

    def body(u_ref, w_ref, s_ref, pooled_ref, y_ref, pad):
        g = pl.program_id(0)
        pad[pl.ds(0, HALO), :] = jnp.zeros((HALO, pg), F32)

        def fill(s, tc):
            pad[pl.ds(HALO + s, tc), :] = u_ref[pl.ds(s, tc), :].astype(F32)

        def chunk(s, tc):
            win = pad[pl.ds(s, tc + HALO), :]
            total = _pick(_window_sums(win, tc, causal=True), g)
            pooled = total / _pool_count(s, tc, g) - win[HALO:HALO + tc, :]
            pooled_ref[pl.ds(s, tc), :] = pooled.astype(BF16)

        _chunks(t, fill)
        _chunks(t, chunk)
        mixed = jnp.dot(pooled_ref[...], w_ref[...], preferred_element_type=F32)
        y_ref[...] = (mixed * s_ref[...]).astype(BF16)

    col = pl.BlockSpec((t, pg), lambda g: (0, g))
    return pl.pallas_call(
        body, name=name, grid=(ng,),
        in_specs=[col, pl.BlockSpec((None, pg, pg), lambda g: (g, 0, 0)), pl.BlockSpec((1, pg), lambda g: (0, g))],
        out_specs=[col, col],
        out_shape=[jax.ShapeDtypeStruct((t, d_pool), BF16), jax.ShapeDtypeStruct((t, d_pool), BF16)],
        scratch_shapes=[pltpu.VMEM((t + HALO, pg), F32)],
        compiler_params=_params(("parallel",)),
    )(z, pool_w, pool_scale)


def pool_bwd(name, pooled, dy, pool_w, pool_scale):
    t, d_pool = pooled.shape
    ng, pg = pool_w.shape[0], pool_w.shape[1]

    def body(p_ref, dy_ref, w_ref, s_ref, du_ref, dw_ref, ds_ref, pad):
        g = pl.program_id(0)
        w = w_ref[...]
        dyv = dy_ref[...].astype(F32)
        mixed = jnp.dot(p_ref[...], w, preferred_element_type=F32)
        ds_ref[...] = jnp.sum(dyv * mixed, axis=0, keepdims=True)
        dmixed = (dyv * s_ref[...]).astype(BF16)
        dw_ref[...] = lax.dot_general(p_ref[...], dmixed, TN, preferred_element_type=F32)
        pad[...] = jnp.zeros((t + HALO, pg), F32)
        pad[pl.ds(0, t), :] = lax.dot_general(dmixed, w, NT, preferred_element_type=F32)

        def scale(s, tc):
            pad[pl.ds(s, tc), :] = pad[pl.ds(s, tc), :] / _pool_count(s, tc, g)

        def chunk(s, tc):
            win = pad[pl.ds(s, tc + HALO), :]
            total = _pick(_window_sums(win, tc, causal=False), g)
            du_ref[pl.ds(s, tc), :] = (total - win[0:tc, :] * _pool_count(s, tc, g)).astype(BF16)

        _chunks(t, scale)
        _chunks(t, chunk)

    col = pl.BlockSpec((t, pg), lambda g: (0, g))
    vec = pl.BlockSpec((1, pg), lambda g: (0, g))
    mat = pl.BlockSpec((None, pg, pg), lambda g: (g, 0, 0))
    return pl.pallas_call(
        body, name=name, grid=(ng,),
        in_specs=[col, col, mat, vec], out_specs=[col, mat, vec],
        out_shape=[jax.ShapeDtypeStruct((t, d_pool), BF16), jax.ShapeDtypeStruct((ng, pg, pg), F32),
                   jax.ShapeDtypeStruct((1, d_pool), F32)],
        scratch_shapes=[pltpu.VMEM((t + HALO, pg), F32)],
        compiler_params=_params(("parallel",)),
    )(pooled, dy, pool_w, pool_scale)


def conv_fwd(name, z, conv_w, conv_b, d_pool, d_conv):
    t = z.shape[0]
    kw = conv_w.shape[0]
    tc_ch = _tile(d_conv, CHANNEL_TILE)
    v0, g0 = d_pool // tc_ch, (d_pool + d_conv) // tc_ch

    def body(v_ref, g_ref, w_ref, b_ref, c_ref, pad):
        pad[pl.ds(0, HALO), :] = jnp.zeros((HALO, tc_ch), F32)

        def fill(s, tc):
            pad[pl.ds(HALO + s, tc), :] = v_ref[pl.ds(s, tc), :].astype(F32) * jax.nn.sigmoid(g_ref[pl.ds(s, tc), :].astype(F32))

        def chunk(s, tc):
            win = pad[pl.ds(s, tc + HALO), :]
            c_ref[pl.ds(s, tc), :] = _taps(win, w_ref, [HALO - (kw - 1) + k for k in range(kw)], tc) + b_ref[...]

        _chunks(t, fill)
        _chunks(t, chunk)

    return pl.pallas_call(
        body, name=name, grid=(d_conv // tc_ch,),
        in_specs=[pl.BlockSpec((t, tc_ch), lambda j: (0, v0 + j)), pl.BlockSpec((t, tc_ch), lambda j: (0, g0 + j)),
                  pl.BlockSpec((kw, tc_ch), lambda j: (0, j)), pl.BlockSpec((1, tc_ch), lambda j: (0, j))],
        out_specs=pl.BlockSpec((t, tc_ch), lambda j: (0, j)),
        out_shape=jax.ShapeDtypeStruct((t, d_conv), F32),
        scratch_shapes=[pltpu.VMEM((t + HALO, tc_ch), F32)],
        compiler_params=_params(("parallel",)),
    )(z, z, conv_w, conv_b)


def conv_bwd(name, z, dc, conv_w, d_pool, d_conv):
    t = z.shape[0]
    kw = conv_w.shape[0]
    tc_ch = _tile(d_conv, CHANNEL_TILE)
    v0, g0 = d_pool // tc_ch, (d_pool + d_conv) // tc_ch

    def body(v_ref, g_ref, dc_ref, w_ref, dv_ref, dg_ref, dw_ref, db_ref, pad_a, pad_dc, acc_w, acc_b):
        pad_a[pl.ds(0, HALO), :] = jnp.zeros((HALO, tc_ch), F32)
        pad_dc[pl.ds(t, HALO), :] = jnp.zeros((HALO, tc_ch), F32)
        acc_w[...] = jnp.zeros_like(acc_w)
        acc_b[...] = jnp.zeros_like(acc_b)

        def fill(s, tc):
            pad_a[pl.ds(HALO + s, tc), :] = v_ref[pl.ds(s, tc), :].astype(F32) * jax.nn.sigmoid(g_ref[pl.ds(s, tc), :].astype(F32))
            pad_dc[pl.ds(s, tc), :] = dc_ref[pl.ds(s, tc), :]

        def chunk(s, tc):
            dcv = pad_dc[pl.ds(s, tc), :]
            win_a = pad_a[pl.ds(s, tc + HALO), :]
            for k in range(kw):
                off = HALO - (kw - 1) + k
                acc_w[pl.ds(8 * k, 8), :] += _fold8(dcv * win_a[off:off + tc, :])
            acc_b[...] += _fold8(dcv)
            win_dc = pad_dc[pl.ds(s, tc + HALO), :]
            da = None
            for j in range(kw):
                term = w_ref[kw - 1 - j:kw - j, :] * win_dc[j:j + tc, :]
                da = term if da is None else da + term
            vv = v_ref[pl.ds(s, tc), :].astype(F32)
            sg = jax.nn.sigmoid(g_ref[pl.ds(s, tc), :].astype(F32))
            dv_ref[pl.ds(s, tc), :] = (da * sg).astype(BF16)
            dg_ref[pl.ds(s, tc), :] = (da * vv * sg * (1.0 - sg)).astype(BF16)

        _chunks(t, fill)
        _chunks(t, chunk)
        for k in range(kw):
            dw_ref[k:k + 1, :] = jnp.sum(acc_w[pl.ds(8 * k, 8), :], axis=0, keepdims=True)
        db_ref[...] = jnp.sum(acc_b[...], axis=0, keepdims=True)

    col = pl.BlockSpec((t, tc_ch), lambda j: (0, j))
    return pl.pallas_call(
        body, name=name, grid=(d_conv // tc_ch,),
        in_specs=[pl.BlockSpec((t, tc_ch), lambda j: (0, v0 + j)), pl.BlockSpec((t, tc_ch), lambda j: (0, g0 + j)),
                  col, pl.BlockSpec((kw, tc_ch), lambda j: (0, j))],
        out_specs=[col, col, pl.BlockSpec((kw, tc_ch), lambda j: (0, j)), pl.BlockSpec((1, tc_ch), lambda j: (0, j))],
        out_shape=[jax.ShapeDtypeStruct((t, d_conv), BF16), jax.ShapeDtypeStruct((t, d_conv), BF16),
                   jax.ShapeDtypeStruct((kw, d_conv), F32), jax.ShapeDtypeStruct((1, d_conv), F32)],
        scratch_shapes=[pltpu.VMEM((t + HALO, tc_ch), F32), pltpu.VMEM((t + HALO, tc_ch), F32),
                        pltpu.VMEM((8 * kw, tc_ch), F32), pltpu.VMEM((8, tc_ch), F32)],
        compiler_params=_params(("parallel",)),
    )(z, z, dc, conv_w)


def short_fwd(name, z, conv_w, d_short):
    t = z.shape[0]
    kw = conv_w.shape[0]
    tc_ch = _tile(d_short, CHANNEL_TILE)
    nt = d_short // tc_ch

    def body(b_ref, c_ref, u_ref, w_ref, y_ref, pad):
        pad[pl.ds(0, HALO), :] = jnp.zeros((HALO, tc_ch), F32)

        def fill(s, tc):
            pad[pl.ds(HALO + s, tc), :] = c_ref[pl.ds(s, tc), :].astype(F32) * u_ref[pl.ds(s, tc), :].astype(F32)

        def chunk(s, tc):
            win = pad[pl.ds(s, tc + HALO), :]
            cq = _taps(win, w_ref, [HALO - (kw - 1) + k for k in range(kw)], tc)
            y_ref[pl.ds(s, tc), :] = (b_ref[pl.ds(s, tc), :].astype(F32) * cq).astype(BF16)

        _chunks(t, fill)
        _chunks(t, chunk)

    return pl.pallas_call(
        body, name=name, grid=(nt,),
        in_specs=[pl.BlockSpec((t, tc_ch), lambda j: (0, j)), pl.BlockSpec((t, tc_ch), lambda j: (0, nt + j)),
                  pl.BlockSpec((t, tc_ch), lambda j: (0, 2 * nt + j)), pl.BlockSpec((kw, tc_ch), lambda j: (0, j))],
        out_specs=pl.BlockSpec((t, tc_ch), lambda j: (0, j)),
        out_shape=jax.ShapeDtypeStruct((t, d_short), BF16),
        scratch_shapes=[pltpu.VMEM((t + HALO, tc_ch), F32)],
        compiler_params=_params(("parallel",)),
    )(z, z, z, conv_w)


def short_bwd(name, z, dy, conv_w, d_short):
    t = z.shape[0]
    kw = conv_w.shape[0]
    tc_ch = _tile(d_short, CHANNEL_TILE)
    nt = d_short // tc_ch

    def body(b_ref, c_ref, u_ref, dy_ref, w_ref, db_ref, dcg_ref, du_ref, dw_ref, pad_q, pad_dcq, acc_w):
        pad_q[pl.ds(0, HALO), :] = jnp.zeros((HALO, tc_ch), F32)
        pad_dcq[pl.ds(t, HALO), :] = jnp.zeros((HALO, tc_ch), F32)
        acc_w[...] = jnp.zeros_like(acc_w)

        def fill(s, tc):
            rows = pl.ds(s, tc)
            pad_q[pl.ds(HALO + s, tc), :] = c_ref[rows, :].astype(F32) * u_ref[rows, :].astype(F32)
            pad_dcq[rows, :] = dy_ref[rows, :].astype(F32) * b_ref[rows, :].astype(F32)

        def chunk(s, tc):
            rows = pl.ds(s, tc)
            win_q = pad_q[pl.ds(s, tc + HALO), :]
            dcq = pad_dcq[rows, :]
            cq = None
            for k in range(kw):
                off = HALO - (kw - 1) + k
                shifted = win_q[off:off + tc, :]
                acc_w[pl.ds(8 * k, 8), :] += _fold8(dcq * shifted)
                term = w_ref[k:k + 1, :] * shifted
                cq = term if cq is None else cq + term
            db_ref[rows, :] = (dy_ref[rows, :].astype(F32) * cq).astype(BF16)
            win_d = pad_dcq[pl.ds(s, tc + HALO), :]
            dq = None
            for j in range(kw):
                term = w_ref[kw - 1 - j:kw - j, :] * win_d[j:j + tc, :]
                dq = term if dq is None else dq + term
            dcg_ref[rows, :] = (dq * u_ref[rows, :].astype(F32)).astype(BF16)
            du_ref[rows, :] = (dq * c_ref[rows, :].astype(F32)).astype(BF16)

        _chunks(t, fill)
        _chunks(t, chunk)
        for k in range(kw):
            dw_ref[k:k + 1, :] = jnp.sum(acc_w[pl.ds(8 * k, 8), :], axis=0, keepdims=True)

    col = pl.BlockSpec((t, tc_ch), lambda j: (0, j))
    zspec = [pl.BlockSpec((t, tc_ch), lambda j, o=o: (0, o * nt + j)) for o in range(3)]
    return pl.pallas_call(
        body, name=name, grid=(nt,),
        in_specs=[*zspec, col, pl.BlockSpec((kw, tc_ch), lambda j: (0, j))],
        out_specs=[col, col, col, pl.BlockSpec((kw, tc_ch), lambda j: (0, j))],
        out_shape=[jax.ShapeDtypeStruct((t, d_short), BF16)] * 3 + [jax.ShapeDtypeStruct((kw, d_short), F32)],
        scratch_shapes=[pltpu.VMEM((t + HALO, tc_ch), F32), pltpu.VMEM((t + HALO, tc_ch), F32),
                        pltpu.VMEM((8 * kw, tc_ch), F32)],
        compiler_params=_params(("parallel",)),
    )(z, z, z, dy, conv_w)


def adamw(name, w, m, v, contributions):
    r, c = w.shape
    nc = len(contributions)
    n_slots = contributions[0].shape[0]
    tr = 256 if c <= 1024 else 128
    if any(a.shape[1] % tr for a in contributions):
        assert nc == 1
        tr = r
    tiles = [a.shape[1] // tr for a in contributions]
    first = [sum(tiles[:j]) for j in range(nc)]

    def body(w_ref, m_ref, v_ref, *rest):
        g_refs, (grad_ref, delta_ref, nm_ref, nv_ref) = rest[:nc], rest[nc:]
        i = pl.program_id(0)
        g = None
        for j, g_ref in enumerate(g_refs):
            s = g_ref[0].astype(F32)
            for slot in range(1, n_slots):
                s = s + g_ref[slot].astype(F32)
            g = s if g is None else jnp.where(i >= first[j], s, g)
        nm = ADAM_B1 * m_ref[...] + (1.0 - ADAM_B1) * g
        nv = ADAM_B2 * v_ref[...] + (1.0 - ADAM_B2) * (g * g)
        m_hat = nm / (1.0 - ADAM_B1 ** ADAM_STEP)
        v_hat = nv / (1.0 - ADAM_B2 ** ADAM_STEP)
        grad_ref[...] = g
        delta_ref[...] = -ADAM_LR * (m_hat / (jnp.sqrt(v_hat) + ADAM_EPS) + ADAM_WD * w_ref[...])
        nm_ref[...] = nm
        nv_ref[...] = nv

    blk = pl.BlockSpec((tr, c), lambda i: (i, 0))
    g_specs = [pl.BlockSpec((n_slots, tr, c), lambda i, j=j: (0, jnp.clip(i - first[j], 0, tiles[j] - 1), 0))
               for j in range(nc)]
    return pl.pallas_call(
        body, name=name, grid=(r // tr,),
        in_specs=[blk, blk, blk, *g_specs],
        out_specs=[blk] * 4, out_shape=[jax.ShapeDtypeStruct((r, c), F32)] * 4,
        compiler_params=_params(("parallel",)),
    )(w, m, v, *contributions)


def _pad_rows(a, rows):
    return jnp.pad(a, ((0, rows - a.shape[0]), (0, 0)))


def kernel(x, mix_pre_g, mix_post_g, ffn_pre_g, ffn_post_g, ab_w_in, pool_w, pool_scale, conv_w, conv_b, conv_ln_g, conv_ln_b, ab_w_out, sc_w_in, sc_conv_w, sc_w_out, ffn_w1, ffn_w2, loss_target, m_mix_pre_g, m_mix_post_g, m_ffn_pre_g, m_ffn_post_g, m_ab_w_in, m_pool_w, m_pool_scale, m_conv_w, m_conv_b, m_conv_ln_g, m_conv_ln_b, m_ab_w_out, m_sc_w_in, m_sc_conv_w, m_sc_w_out, m_ffn_w1, m_ffn_w2, v_mix_pre_g, v_mix_post_g, v_ffn_pre_g, v_ffn_post_g, v_ab_w_in, v_pool_w, v_pool_scale, v_conv_w, v_conv_b, v_conv_ln_g, v_conv_ln_b, v_ab_w_out, v_sc_w_in, v_sc_conv_w, v_sc_w_out, v_ffn_w1, v_ffn_w2):
    t, d = x.shape[1], x.shape[2]
    d_pool = pool_scale.shape[1]
    d_conv = conv_b.shape[1]
    d_short = d
    ng, pg = pool_w.shape[1], pool_w.shape[3]
    kw, ks = conv_w.shape[1], sc_conv_w.shape[1]
    nb_ab, nb_sc, nb_ff = ab_w_in.shape[2], sc_w_in.shape[2], ffn_w1.shape[2]

    xs = x[0]
    target = loss_target[0]

    lanes = min(128, d_conv // N_DEV)
    small_rows = [kw * (d_conv // N_DEV) // lanes, ks * (d_short // N_DEV) // lanes, ng * (pg // N_DEV) * pg // lanes]
    small_total = -(-sum(small_rows) // 8) * 8
    r0, r1, r2 = small_rows[0], small_rows[0] + small_rows[1], sum(small_rows)

    def pack_small(a_conv, a_sconv, a_pool):
        parts = [a_conv[0].reshape(-1, lanes), a_sconv[0].reshape(-1, lanes), a_pool[0].reshape(-1, lanes)]
        return _pad_rows(jnp.concatenate(parts, axis=0), small_total)

    shards = {
        "ab_in": (ab_w_in[0], BF16), "small": (pack_small(conv_w, sc_conv_w, pool_w), F32),
        "ab_out": (ab_w_out[0], BF16), "ff1_0": (ffn_w1[0], BF16), "ff2_0": (ffn_w2[0], BF16),
        "sc_in": (sc_w_in[0], BF16), "sc_out": (sc_w_out[0], BF16),
        "ff1_1": (ffn_w1[1], BF16), "ff2_1": (ffn_w2[1], BF16)}
    zones = [place_shard("place_" + nm, w, dt) for nm, (w, dt) in shards.items()]
    started, token = copies_start("gather_start", [[z] for z in zones], _first_hop, 4)
    started = dict(zip(shards, started))

    ties = [0]

    def after(v, *deps):
        ties[0] += 1
        return tie(f"tie_{ties[0]}", v, *deps)

    def fetch_begin(nm, dep):
        (zone,) = copies_wait("gather_wait_" + nm, started[nm], _first_hop, dep)
        (hop,), tok = copies_start("forward_start_" + nm, [[zone]], _second_hop, 3)
        return hop, tok

    def fetch_end(nm, hop, dep):
        return copies_wait("forward_wait_" + nm, hop, _second_hop, dep)[0]

    relu = lambda r: jnp.maximum(r, 0.0)
    square = lambda a: a * a
    relu2_bwd = lambda r, a: r * (2.0 * a.astype(F32))

    def row(vec, l):
        return vec[l:l + 1]

    hop_small, _ = fetch_begin("small", token)
    hop_ab_in, tok = fetch_begin("ab_in", token)
    w_small = fetch_end("small", hop_small, tok)
    w_ab_in = fetch_end("ab_in", hop_ab_in, tok)
    w_conv = w_small[:, :r0].reshape(N_DEV, kw, -1).transpose(1, 0, 2).reshape(kw, d_conv)
    w_sconv = w_small[:, r0:r1].reshape(N_DEV, ks, -1).transpose(1, 0, 2).reshape(ks, d_short)
    w_pool = w_small[:, r1:r2].reshape(N_DEV, ng, -1, pg).transpose(1, 0, 2, 3).reshape(ng, pg, pg).astype(BF16)
    hop, tok = fetch_begin("ab_out", w_ab_in)
    h0 = after(norm_pre("norm_pre", xs, row(mix_pre_g, 0)), tok)
    z0 = mm_nn_blocked("ab_in", h0, w_ab_in, out_dtype=BF16)
    pooled, y_pool = pool_fwd("pool_fwd", z0, w_pool, pool_scale, d_pool)
    cv = conv_fwd("conv_fwd", z0, w_conv, conv_b, d_pool, d_conv)
    y_conv = ln_silu("ln_silu", cv, conv_ln_g, conv_ln_b)
    y0 = jnp.concatenate([y_pool, y_conv], axis=1)
    w_ab_out = fetch_end("ab_out", hop, y0)
    hop, tok = fetch_begin("ff1_0", w_ab_out)
    y0 = after(y0, tok)
    m0 = mm_nn("ab_out", y0, w_ab_out.reshape(d_pool + d_conv, d), out_dtype=F32)
    x1, h1 = post_pre("post_pre_0", xs, m0, row(mix_post_g, 0), row(ffn_pre_g, 0))
    w_ff1_0 = fetch_end("ff1_0", hop, h1)
    hop, tok = fetch_begin("ff2_0", w_ff1_0)
    h1 = after(h1, tok)
    a0 = mm_nn_blocked("ffn0_up", h1, w_ff1_0, out_dtype=BF16, epilogue=relu)
    w_ff2_0 = fetch_end("ff2_0", hop, a0).reshape(-1, d)
    hop, tok = fetch_begin("sc_in", w_ff2_0)
    a0 = after(a0, tok)
    f0 = mm_nn("ffn0_down", a0, w_ff2_0, out_dtype=F32, tk=2048, lhs_fn=square)
    x2, h2 = post_pre("post_pre_1", x1, f0, row(ffn_post_g, 0), row(mix_pre_g, 1))
    w_sc_in = fetch_end("sc_in", hop, h2)
    hop, tok = fetch_begin("sc_out", w_sc_in)
    h2 = after(h2, tok)
    z1 = mm_nn_blocked("sc_in", h2, w_sc_in, out_dtype=BF16)
    y1 = short_fwd("short_fwd", z1, w_sconv, d_short)
    w_sc_out = fetch_end("sc_out", hop, y1).reshape(d_short, d)
    hop, tok = fetch_begin("ff1_1", w_sc_out)
    y1 = after(y1, tok)
    m1 = mm_nn("sc_out", y1, w_sc_out, out_dtype=F32)
    x3, h3 = post_pre("post_pre_2", x2, m1, row(mix_post_g, 1), row(ffn_pre_g, 1))
    w_ff1_1 = fetch_end("ff1_1", hop, h3)
    hop, tok = fetch_begin("ff2_1", w_ff1_1)
    h3 = after(h3, tok)
    a1 = mm_nn_blocked("ffn1_up", h3, w_ff1_1, out_dtype=BF16, epilogue=relu)
    w_ff2_1 = fetch_end("ff2_1", hop, a1).reshape(-1, d)
    f1 = mm_nn("ffn1_down", a1, w_ff2_1, out_dtype=F32, tk=2048, lhs_fn=square)
    dx4, df1, loss_part, dg_ffn_post1 = post_loss("post_loss", x3, f1, row(ffn_post_g, 1), target)
    loss = lax.psum(jnp.sum(loss_part) * (0.5 / d), ("x", "y", "c"))

    def reduce_begin(tag, g):
        zone = lax.empty((N_CHIP,) + g.shape[1:], g.dtype)
        (hop,), tok = copies_start("pair_start_" + tag, [[g, zone]], _pair_hop, N_CHIP)
        return hop, tok

    def reduce_middle(tag, hop, dep):
        g, from_sibling = copies_wait("pair_wait_" + tag, hop, _pair_hop, dep)
        pair_sum, zone = pair_add("pair_add_" + tag, g, from_sibling)
        (hop2,), tok = copies_start("chips_start_" + tag, [[pair_sum, zone]], _chip_hop, 3)
        return hop2, tok

    def reduce_end(tag, hop2, dep):
        return copies_wait("chips_wait_" + tag, hop2, _chip_hop, dep)[1]

    dw = mm_tn("ffn1_dw2", a1, df1, out_dtype=BF16, lhs_fn=square)
    red_ff2_1, tok = reduce_begin("ff2_1", dw.reshape(N_DEV, -1, d))
    df1 = after(df1, tok)
    dpre = mm_nt("ffn1_da", df1, w_ff2_1, out_dtype=BF16, extra=a1, epilogue=relu2_bwd)
    dw = mm_tn_blocked("ffn1_dw1", h3, dpre, nb_ff, out_dtype=BF16)
    red_ff1_1, tok = reduce_begin("ff1_1", dw)
    dpre = after(dpre, tok)
    dh3 = mm_nt_blocked("ffn1_dh", dpre, w_ff1_1, out_dtype=BF16)
    red_ff2_1, tok = reduce_middle("ff2_1", red_ff2_1, dh3)
    dh3 = after(dh3, tok)
    dx3, dm1, dg_ffn_pre1, dg_mix_post1 = bwd_pre_post("bwd_3", dx4, x3, row(ffn_pre_g, 1), dh3, m1, row(mix_post_g, 1))

    dw = mm_tn("sc_dwout", y1, dm1, out_dtype=BF16)
    red_sc_out, tok = reduce_begin("sc_out", dw.reshape(N_DEV, -1, d))
    dm1 = after(dm1, tok)
    dy1 = mm_nt("sc_dy", dm1, w_sc_out, out_dtype=BF16)
    red_ff1_1, tok = reduce_middle("ff1_1", red_ff1_1, dy1)
    dy1 = after(dy1, tok)
    db1, dcg1, du1, dw_sconv = short_bwd("short_bwd", z1, dy1, w_sconv, d_short)
    dz1 = jnp.concatenate([db1, dcg1, du1], axis=1)
    dw = mm_tn_blocked("sc_dwin", h2, dz1, nb_sc, out_dtype=BF16)
    red_sc_in, tok = reduce_begin("sc_in", dw)
    dz1 = after(dz1, tok)
    dh2 = mm_nt_blocked("sc_dh", dz1, w_sc_in, out_dtype=BF16)
    red_sc_out, tok = reduce_middle("sc_out", red_sc_out, dh2)
    dh2 = after(dh2, tok)
    dx2, df0, dg_mix_pre1, dg_ffn_post0 = bwd_pre_post("bwd_2", dx3, x2, row(mix_pre_g, 1), dh2, f0, row(ffn_post_g, 0))

    dw = mm_tn("ffn0_dw2", a0, df0, out_dtype=BF16, lhs_fn=square)
    red_ff2_0, tok = reduce_begin("ff2_0", dw.reshape(N_DEV, -1, d))
    df0 = after(df0, tok)
    dpre = mm_nt("ffn0_da", df0, w_ff2_0, out_dtype=BF16, extra=a0, epilogue=relu2_bwd)
    red_sc_in, tok = reduce_middle("sc_in", red_sc_in, dpre)
    dpre = after(dpre, tok)
    dw = mm_tn_blocked("ffn0_dw1", h1, dpre, nb_ff, out_dtype=BF16)
    red_ff1_0, tok = reduce_begin("ff1_0", dw)
    dpre = after(dpre, tok)
    dh1 = mm_nt_blocked("ffn0_dh", dpre, w_ff1_0, out_dtype=BF16)
    red_ff2_0, tok = reduce_middle("ff2_0", red_ff2_0, dh1)
    dh1 = after(dh1, tok)
    dx1, dm0, dg_ffn_pre0, dg_mix_post0 = bwd_pre_post("bwd_1", dx2, x1, row(ffn_pre_g, 0), dh1, m0, row(mix_post_g, 0))

    dw = mm_tn("ab_dwout", y0, dm0, out_dtype=BF16)
    red_ab_out, tok = reduce_begin("ab_out", dw.reshape(N_DEV, -1, d))
    dm0 = after(dm0, tok)
    dy0 = mm_nt("ab_dy", dm0, w_ab_out.reshape(d_pool + d_conv, d), out_dtype=BF16)
    red_ff1_0, tok = reduce_middle("ff1_0", red_ff1_0, dy0)
    dy0 = after(dy0, tok)
    dcv, dg_ln_g, dg_ln_b = ln_silu_bwd("ln_silu_bwd", cv, conv_ln_g, conv_ln_b, dy0[:, d_pool:])
    dv, dgate, dw_conv, dg_conv_b = conv_bwd("conv_bwd", z0, dcv, w_conv, d_pool, d_conv)
    du0, dw_pool, dg_pool_scale = pool_bwd("pool_bwd", pooled, dy0[:, :d_pool], w_pool, pool_scale)
    dz0 = jnp.concatenate([du0, dv, dgate], axis=1)
    small_parts = [
        dw_conv.reshape(kw, N_DEV, -1).transpose(1, 0, 2).reshape(N_DEV, -1, lanes),
        dw_sconv.reshape(ks, N_DEV, -1).transpose(1, 0, 2).reshape(N_DEV, -1, lanes),
        dw_pool.reshape(ng, N_DEV, pg // N_DEV, pg).transpose(1, 0, 2, 3).reshape(N_DEV, -1, lanes),
    ]
    small = jnp.pad(jnp.concatenate(small_parts, axis=1), ((0, 0), (0, small_total - r2), (0, 0)))
    red_small, tok = reduce_begin("small", small)
    red_ab_out, tok2 = reduce_middle("ab_out", red_ab_out, dz0)
    dz0 = after(dz0, tok, tok2)
    dw = mm_tn_blocked("ab_dwin", h0, dz0, nb_ab, out_dtype=BF16)
    red_ab_in, tok = reduce_begin("ab_in", dw)
    dz0 = after(dz0, tok)
    dh0 = mm_nt_blocked("ab_dh", dz0, w_ab_in, out_dtype=BF16)
    red_small, tok = reduce_middle("small", red_small, dh0)
    dh0 = after(dh0, tok)
    grad_x, dg_mix_pre0 = bwd_pre_final("bwd_0", dx1, xs, row(mix_pre_g, 0), dh0)
    red_ab_in, tok = reduce_middle("ab_in", red_ab_in, grad_x)

    fold = lambda a: jnp.sum(a, axis=0, keepdims=True)
    rep_rows = [fold(dg_mix_pre0), fold(dg_mix_pre1), fold(dg_mix_post0), fold(dg_mix_post1),
                fold(dg_ffn_pre0), fold(dg_ffn_pre1), fold(dg_ffn_post0), fold(dg_ffn_post1)]
    tail = jnp.concatenate([dg_pool_scale, dg_conv_b, fold(dg_ln_g), fold(dg_ln_b)], axis=1).reshape(-1, d)
    rep = _pad_rows(jnp.concatenate(rep_rows + [tail], axis=0), 16)
    rep_all = all_gather([rep], name="gather_small_grads")[0]

    def pack_rep(a_mix_pre, a_mix_post, a_ffn_pre, a_ffn_post, a_scale, a_b, a_g, a_lb):
        tail_ = jnp.concatenate([a_scale, a_b, a_g, a_lb], axis=1).reshape(-1, d)
        return _pad_rows(jnp.concatenate([a_mix_pre, a_mix_post, a_ffn_pre, a_ffn_post, tail_], axis=0), 16)

    def upd(name, w, m, v, contribs):
        shape = w.shape
        flat2 = lambda a: a.reshape(-1, shape[-1])
        outs = adamw(name, flat2(w), flat2(m), flat2(v), contribs)
        return [o.reshape(shape) for o in outs]

    o_rep = adamw("adam_replicated",
                  pack_rep(mix_pre_g, mix_post_g, ffn_pre_g, ffn_post_g, pool_scale, conv_b, conv_ln_g, conv_ln_b),
                  pack_rep(m_mix_pre_g, m_mix_post_g, m_ffn_pre_g, m_ffn_post_g, m_pool_scale, m_conv_b, m_conv_ln_g, m_conv_ln_b),
                  pack_rep(v_mix_pre_g, v_mix_post_g, v_ffn_pre_g, v_ffn_post_g, v_pool_scale, v_conv_b, v_conv_ln_g, v_conv_ln_b),
                  [rep_all])
    g_ff2 = [reduce_end("ff2_0", red_ff2_0, tok), reduce_end("ff2_1", red_ff2_1, tok)]
    o_ff2 = upd("adam_ffn_w2", ffn_w2, m_ffn_w2, v_ffn_w2, g_ff2)
    g_ff1 = [reduce_end("ff1_0", red_ff1_0, o_ff2[0]), reduce_end("ff1_1", red_ff1_1, o_ff2[0])]
    o_ff1 = upd("adam_ffn_w1", ffn_w1, m_ffn_w1, v_ffn_w1, g_ff1)
    o_sc_out = upd("adam_sc_out", sc_w_out, m_sc_w_out, v_sc_w_out, [reduce_end("sc_out", red_sc_out, o_ff1[0])])
    o_sc_in = upd("adam_sc_in", sc_w_in, m_sc_w_in, v_sc_w_in, [reduce_end("sc_in", red_sc_in, o_sc_out[0])])
    o_ab_out = upd("adam_ab_out", ab_w_out, m_ab_w_out, v_ab_w_out, [reduce_end("ab_out", red_ab_out, o_sc_in[0])])
    o_small = adamw("adam_small", pack_small(conv_w, sc_conv_w, pool_w), pack_small(m_conv_w, m_sc_conv_w, m_pool_w),
                    pack_small(v_conv_w, v_sc_conv_w, v_pool_w), [reduce_end("small", red_small, o_ab_out[0])])
    o_ab_in = upd("adam_ab_in", ab_w_in, m_ab_w_in, v_ab_w_in, [reduce_end("ab_in", red_ab_in, o_small[0])])

    def unpack_small(o):
        return o[:r0].reshape(conv_w.shape), o[r0:r1].reshape(sc_conv_w.shape), o[r1:r2].reshape(pool_w.shape)

    def unpack_rep(o):
        tail_ = o[8:8 + tail.shape[0]].reshape(1, -1)
        n1 = d_pool
        return dict(mix_pre_g=o[0:2], mix_post_g=o[2:4], ffn_pre_g=o[4:6], ffn_post_g=o[6:8],
                    pool_scale=tail_[:, :n1], conv_b=tail_[:, n1:n1 + d_conv],
                    conv_ln_g=tail_[:, n1 + d_conv:n1 + 2 * d_conv], conv_ln_b=tail_[:, n1 + 2 * d_conv:n1 + 3 * d_conv])

    results = []
    for kind in range(4):
        rep_o = unpack_rep(o_rep[kind])
        s_conv, s_sconv, s_pool = unpack_small(o_small[kind])
        results.append([
            rep_o["mix_pre_g"], rep_o["mix_post_g"], rep_o["ffn_pre_g"], rep_o["ffn_post_g"],
            o_ab_in[kind], s_pool, rep_o["pool_scale"], s_conv, rep_o["conv_b"], rep_o["conv_ln_g"], rep_o["conv_ln_b"],
            o_ab_out[kind], o_sc_in[kind], s_sconv, o_sc_out[kind], o_ff1[kind], o_ff2[kind]])

    return (loss, grad_x[None], *results[0], *results[1], *results[2], *results[3])
```

```python
import jax
import jax.numpy as jnp
from jax import lax
from jax.experimental import pallas as pl
from jax.experimental.pallas import tpu as pltpu

F32 = jnp.float32
BF16 = jnp.bfloat16
MESH = pl.DeviceIdType.MESH
ANY = pl.BlockSpec(memory_space=pl.ANY)

NORM_EPS = 1e-6
POOL_WINDOWS = (2, 4, 8, 16)
ADAM_LR = 0.001
ADAM_B1 = 0.9
ADAM_B2 = 0.999
ADAM_EPS = 1e-08
ADAM_WD = 0.01
ADAM_STEP = 10

N_DEV = 8
VMEM_LIMIT = 56 * 1024 * 1024
PAIR_ADD_BLOCK = 1 << 20
ROW_TILE = 256
CHANNEL_TILE = 256
TIME_CHUNK = 64
HALO = 32

NN = (((1,), (0,)), ((), ()))
NT = (((1,), (1,)), ((), ()))
TN = (((0,), (0,)), ((), ()))


def _params(sem):
    return pltpu.CompilerParams(dimension_semantics=sem, vmem_limit_bytes=VMEM_LIMIT)


def _place():
    x, y, c = lax.axis_index("x"), lax.axis_index("y"), lax.axis_index("c")
    return x, y, c


def _slot(px, py, pc):
    return 4 * px + 2 * py + pc


HBM = pl.BlockSpec(memory_space=pltpu.HBM)
SEM = pl.BlockSpec(memory_space=pltpu.SEMAPHORE)
EFFECT = pltpu.SideEffectType.DATAFLOW_SIDE_EFFECTING
TOKEN = jax.ShapeDtypeStruct((8, 128), F32)


def _in_hbm(a):
    return pltpu.with_memory_space_constraint(a, pltpu.HBM)


CHIPS = [(0, 0), (0, 1), (1, 0), (1, 1)]
N_CHIP = len(CHIPS)


def _chip(px, py):
    return 2 * px + py


def _first_hop(bufs, sends, recvs, waiting):
    (land,) = bufs
    x, y, c = _place()
    me = _slot(x, y, c)
    peers = [(x, y, 1 - c), (1 - x, y, c), (x, 1 - y, c), (1 - x, 1 - y, c)]
    return [pltpu.make_async_remote_copy(
        src_ref=land.at[me], dst_ref=land.at[_slot(*p) if waiting else me],
        send_sem=sends.at[k], recv_sem=recvs.at[k], device_id=p, device_id_type=MESH) for k, p in enumerate(peers)]


def _second_hop(bufs, sends, recvs, waiting):
    (land,) = bufs
    x, y, c = _place()
    return [pltpu.make_async_remote_copy(
        src_ref=land.at[_slot(px, py, c)], dst_ref=land.at[_slot(px, py, 1 - c if waiting else c)],
        send_sem=sends.at[k], recv_sem=recvs.at[k], device_id=(x, y, 1 - c), device_id_type=MESH)
        for k, (px, py) in enumerate([(1 - x, y), (x, 1 - y), (1 - x, 1 - y)])]


def _pair_hop(bufs, sends, recvs, waiting):
    g, land = bufs
    x, y, c = _place()
    return [pltpu.make_async_remote_copy(
        src_ref=g.at[_slot(qx, qy, 1 - c)], dst_ref=land.at[q],
        send_sem=sends.at[q], recv_sem=recvs.at[q], device_id=(x, y, 1 - c), device_id_type=MESH)
        for q, (qx, qy) in enumerate(CHIPS)]


def _chip_hop(bufs, sends, recvs, waiting):
    p, land = bufs
    x, y, c = _place()
    return [pltpu.make_async_remote_copy(
        src_ref=p.at[_chip(px, py)], dst_ref=land.at[_chip(px, py) if waiting else _chip(x, y)],
        send_sem=sends.at[k], recv_sem=recvs.at[k], device_id=(px, py, c), device_id_type=MESH)
        for k, (px, py) in enumerate([(1 - x, y), (x, 1 - y), (1 - x, 1 - y)])]


def copies_start(name, groups, hop, n_copies):
    flat = [b for grp in groups for b in grp]
    nb, ng = len(flat), len(groups)

    def body(*refs):
        ins, sems, token = refs[:nb], refs[nb:nb + 2 * ng], refs[-1]
        i = 0
        for gi, grp in enumerate(groups):
            for cp in hop(ins[i:i + len(grp)], sems[2 * gi], sems[2 * gi + 1], False):
                cp.start()
            i += len(grp)
        token[...] = jnp.zeros_like(token)

    outs = pl.pallas_call(
        body, name=name,
        out_shape=([pltpu.SemaphoreType.DMA((n_copies,))] * (2 * ng) + [pltpu.HBM(b.shape, b.dtype) for b in flat]
                   + [TOKEN]),
        in_specs=[HBM] * nb,
        out_specs=[SEM] * (2 * ng) + [HBM] * nb + [pl.BlockSpec(memory_space=pltpu.VMEM)],
        input_output_aliases={i: 2 * ng + i for i in range(nb)},
        compiler_params=pltpu.CompilerParams(has_side_effects=EFFECT),
    )(*[_in_hbm(b) for b in flat])
    started, i = [], 0
    for gi, grp in enumerate(groups):
        started.append((outs[2 * gi], outs[2 * gi + 1], list(outs[2 * ng + i:2 * ng + i + len(grp)])))
        i += len(grp)
    return started, outs[-1]


def copies_wait(name, started, hop, after):
    sends, recvs, bufs = started
    nb = len(bufs)

    def body(*refs):
        for cp in hop(refs[:nb], refs[nb], refs[nb + 1], True):
            cp.wait_send()
            cp.wait_recv()

    outs = pl.pallas_call(
        body, name=name,
        out_shape=[pltpu.HBM(b.shape, b.dtype) for b in bufs],
        in_specs=[HBM] * nb + [SEM, SEM, ANY], out_specs=[HBM] * nb,
        input_output_aliases={i: i for i in range(nb)},
        compiler_params=pltpu.CompilerParams(has_side_effects=EFFECT),
    )(*bufs, sends, recvs, after)
    return list(outs)


def place_shard(name, w, dtype):
    r, c = w.shape
    tr = _tile(r, 256)
    x, y, core = _place()
    me = _slot(x, y, core).astype(jnp.int32).reshape(1)

    def body(me_ref, w_ref, o_ref):
        o_ref[...] = w_ref[...].astype(dtype)

    return pl.pallas_call(
        body, name=name,
        grid_spec=pltpu.PrefetchScalarGridSpec(
            num_scalar_prefetch=1, grid=(r // tr,),
            in_specs=[pl.BlockSpec((tr, c), lambda i, me_ref: (i, 0))],
            out_specs=pl.BlockSpec((None, tr, c), lambda i, me_ref: (me_ref[0], i, 0))),
        out_shape=jax.ShapeDtypeStruct((N_DEV, r, c), dtype),
        compiler_params=_params(("parallel",)),
    )(me, w)


def tie(name, x, *deps):
    def body(*refs):
        del refs

    return pl.pallas_call(
        body, name=name, out_shape=jax.ShapeDtypeStruct(x.shape, x.dtype),
        in_specs=[ANY] * (1 + len(deps)), out_specs=ANY, input_output_aliases={0: 0},
    )(x, *deps)


def pair_add(name, g, from_sibling):
    _, r, c_dim = g.shape
    tr = r
    while tr * c_dim > PAIR_ADD_BLOCK and tr % 16 == 0:
        tr //= 2
    x, y, core = _place()
    where = jnp.stack([core, _chip(x, y)]).astype(jnp.int32)

    def body(where_ref, g_ref, s_ref, o_ref, zone_ref):
        total = (g_ref[...].astype(F32) + s_ref[...].astype(F32)).astype(o_ref.dtype)
        o_ref[...] = total

        @pl.when(pl.program_id(1) == where_ref[1])
        def _():
            zone_ref[...] = total

    blk = pl.BlockSpec((None, tr, c_dim), lambda i, q, where_ref: (q, i, 0))
    return pl.pallas_call(
        body, name=name,
        grid_spec=pltpu.PrefetchScalarGridSpec(
            num_scalar_prefetch=1, grid=(r // tr, N_CHIP),
            in_specs=[pl.BlockSpec((None, None, tr, c_dim), lambda i, q, where_ref: (q, where_ref[0], i, 0)), blk],
            out_specs=[blk, pl.BlockSpec((None, tr, c_dim), lambda i, q, where_ref: (where_ref[1], i, 0))]),
        out_shape=[jax.ShapeDtypeStruct((N_CHIP, r, c_dim), g.dtype)] * 2,
        compiler_params=_params(("parallel", "arbitrary")),
    )(where, g.reshape(N_CHIP, 2, r, c_dim), from_sibling)


def _matmul(name, lhs, rhs, *, out_shape, out_dtype, grid, lhs_spec, rhs_spec, out_spec, dims, acc_shape,
            lhs_fn=None, extra=(), extra_specs=(), epilogue=None, parts=1):
    nk = grid[2]
    n_extra = len(extra)

    def body(*refs):
        lhs_ref, rhs_ref = refs[0], refs[1]
        extra_refs = refs[2:2 + n_extra]
        out_ref = refs[2 + n_extra]

        def product():
            if parts == 1:
                a = lhs_ref[...]
                if lhs_fn is not None:
                    a = lhs_fn(a)
                return lax.dot_general(a, rhs_ref[...], dims, preferred_element_type=F32)
            width = lhs_ref.shape[1] // parts
            total = None
            for b in range(parts):
                term = lax.dot_general(lhs_ref[:, b * width:(b + 1) * width], rhs_ref[b], dims,
                                       preferred_element_type=F32)
                total = term if total is None else total + term
            return total

        def finish(r):
            if epilogue is not None:
                r = epilogue(r, *[e[...] for e in extra_refs])
            out_ref[...] = r.astype(out_dtype)

        if nk == 1:
            finish(product())
        else:
            acc_ref = refs[3 + n_extra]
            k = pl.program_id(2)

            @pl.when(k == 0)
            def _():
                acc_ref[...] = product()

            @pl.when(jnp.logical_and(k > 0, k < nk - 1))
            def _():
                acc_ref[...] += product()

            @pl.when(k == nk - 1)
            def _():
                finish(acc_ref[...] + product())

    return pl.pallas_call(
        body, name=name, grid=grid,
        out_shape=jax.ShapeDtypeStruct(out_shape, out_dtype),
        in_specs=[lhs_spec, rhs_spec, *extra_specs], out_specs=out_spec,
        scratch_shapes=[pltpu.VMEM(acc_shape, F32)] if nk > 1 else [],
        compiler_params=_params(("parallel", "parallel", "arbitrary")),
    )(lhs, rhs, *extra)


def _tile(n, want):
    return want if n % want == 0 else n


def mm_nn(name, x, w, *, out_dtype, tn=512, tk=None, lhs_fn=None, epilogue=None):
    t, kdim = x.shape
    n = w.shape[1]
    tn = _tile(n, tn)
    tk = kdim if tk is None else _tile(kdim, tk)
    return _matmul(
        name, x, w, out_shape=(t, n), out_dtype=out_dtype, grid=(1, n // tn, kdim // tk),
        lhs_spec=pl.BlockSpec((t, tk), lambda i, j, k: (i, k)),
        rhs_spec=pl.BlockSpec((tk, tn), lambda i, j, k: (k, j)),
        out_spec=pl.BlockSpec((t, tn), lambda i, j, k: (i, j)),
        dims=NN, acc_shape=(t, tn), lhs_fn=lhs_fn, epilogue=epilogue)


def mm_nn_blocked(name, x, w, *, out_dtype, epilogue=None):
    t, kdim = x.shape
    nb = w.shape[2]
    tn = nb // 2 if nb >= 1024 else nb
    sub = nb // tn
    return _matmul(
        name, x, w, out_shape=(t, N_DEV * nb), out_dtype=out_dtype, grid=(1, N_DEV * sub, 1),
        lhs_spec=pl.BlockSpec((t, kdim), lambda i, j, k: (i, k)),
        rhs_spec=pl.BlockSpec((None, kdim, tn), lambda i, j, k: (j // sub, k, j % sub)),
        out_spec=pl.BlockSpec((t, tn), lambda i, j, k: (i, j)),
        dims=NN, acc_shape=(t, tn), epilogue=epilogue)


def mm_nt(name, dy, w, *, out_dtype, tn=512, extra=None, epilogue=None):
    t, n = dy.shape
    kdim = w.shape[0]
    tn = _tile(kdim, tn)
    extra_arrs = () if extra is None else (extra,)
    extra_specs = () if extra is None else (pl.BlockSpec((t, tn), lambda i, j, k: (i, j)),)
    return _matmul(
        name, dy, w, out_shape=(t, kdim), out_dtype=out_dtype, grid=(1, kdim // tn, 1),
        lhs_spec=pl.BlockSpec((t, n), lambda i, j, k: (i, k)),
        rhs_spec=pl.BlockSpec((tn, n), lambda i, j, k: (j, k)),
        out_spec=pl.BlockSpec((t, tn), lambda i, j, k: (i, j)),
        dims=NT, acc_shape=(t, tn), extra=extra_arrs, extra_specs=extra_specs, epilogue=epilogue)


def mm_nt_blocked(name, dz, w, *, out_dtype, tn=512):
    t = dz.shape[0]
    kdim, nb = w.shape[1], w.shape[2]
    tn = _tile(kdim, tn)
    parts = 2
    return _matmul(
        name, dz, w, out_shape=(t, kdim), out_dtype=out_dtype, grid=(1, kdim // tn, N_DEV // parts),
        lhs_spec=pl.BlockSpec((t, parts * nb), lambda i, j, k: (i, k)),
        rhs_spec=pl.BlockSpec((parts, tn, nb), lambda i, j, k: (k, j, 0)),
        out_spec=pl.BlockSpec((t, tn), lambda i, j, k: (i, j)),
        dims=NT, acc_shape=(t, tn), parts=parts)


def mm_tn(name, x, dy, *, out_dtype, tk=1024, tn=1024, lhs_fn=None):
    t, kdim = x.shape
    n = dy.shape[1]
    tk, tn = _tile(kdim, tk), _tile(n, tn)
    return _matmul(
        name, x, dy, out_shape=(kdim, n), out_dtype=out_dtype, grid=(kdim // tk, n // tn, 1),
        lhs_spec=pl.BlockSpec((t, tk), lambda i, j, k: (k, i)),
        rhs_spec=pl.BlockSpec((t, tn), lambda i, j, k: (k, j)),
        out_spec=pl.BlockSpec((tk, tn), lambda i, j, k: (i, j)),
        dims=TN, acc_shape=(tk, tn), lhs_fn=lhs_fn)


def mm_tn_blocked(name, x, dz, nb, *, out_dtype, tk=1024):
    t, kdim = x.shape
    tk = _tile(kdim, tk)
    return _matmul(
        name, x, dz, out_shape=(N_DEV, kdim, nb), out_dtype=out_dtype, grid=(kdim // tk, N_DEV, 1),
        lhs_spec=pl.BlockSpec((t, tk), lambda i, j, k: (k, i)),
        rhs_spec=pl.BlockSpec((t, nb), lambda i, j, k: (k, j)),
        out_spec=pl.BlockSpec((None, tk, nb), lambda i, j, k: (j, i, 0)),
        dims=TN, acc_shape=(tk, nb))


def _rstd(v):
    return lax.rsqrt(jnp.mean(v * v, axis=-1, keepdims=True) + NORM_EPS)


def _rms_bwd(v, g, dy):
    r = _rstd(v)
    vhat = v * r
    dvh = dy * g
    dv = r * (dvh - vhat * jnp.mean(dvh * vhat, axis=-1, keepdims=True))
    return dv, dy * vhat


def _fold8(v):
    rows, n = v.shape
    return jnp.sum(v.reshape(rows // 8, 8, n), axis=0)


def _fold_lanes(v):
    out = v[:, 0:128]
    for i in range(1, v.shape[1] // 128):
        out = out + v[:, 128 * i:128 * (i + 1)]
    return out


def _accumulate(ref, v):
    i = pl.program_id(0)

    @pl.when(i == 0)
    def _():
        ref[...] = v

    @pl.when(i > 0)
    def _():
        ref[...] += v


def _row_call(body, name, t, ins, row_in, outs, acc_outs=(), tr=ROW_TILE):
    tr = _tile(t, tr)
    in_specs = [pl.BlockSpec((tr, a.shape[1]), lambda i: (i, 0)) if tiled
                else pl.BlockSpec(a.shape, lambda i: (0, 0)) for a, tiled in zip(ins, row_in)]
    out_specs = [pl.BlockSpec((tr, n), lambda i: (i, 0)) for n, _ in outs]
    out_specs += [pl.BlockSpec((8, n), lambda i: (0, 0)) for n in acc_outs]
    out_shape = [jax.ShapeDtypeStruct((t, n), dt) for n, dt in outs]
    out_shape += [jax.ShapeDtypeStruct((8, n), F32) for n in acc_outs]
    return pl.pallas_call(
        body, name=name, grid=(t // tr,), in_specs=in_specs, out_specs=out_specs, out_shape=out_shape,
        compiler_params=_params(("arbitrary",) if acc_outs else ("parallel",)),
    )(*ins)


def norm_pre(name, x, g):
    t, d = x.shape

    def body(x_ref, g_ref, h_ref):
        v = x_ref[...]
        h_ref[...] = (v * _rstd(v) * g_ref[...]).astype(BF16)

    return _row_call(body, name, t, [x, g], [True, False], [(d, BF16)])[0]


def post_pre(name, x, m, g_post, g_pre):
    t, d = x.shape

    def body(x_ref, m_ref, gp_ref, gn_ref, xo_ref, h_ref):
        mv = m_ref[...]
        xn = x_ref[...] + mv * _rstd(mv) * gp_ref[...]
        xo_ref[...] = xn
        h_ref[...] = (xn * _rstd(xn) * gn_ref[...]).astype(BF16)

    return _row_call(body, name, t, [x, m, g_post, g_pre], [True, True, False, False], [(d, F32), (d, BF16)])


def post_loss(name, x, f, g_post, target):
    t, d = x.shape

    def body(x_ref, f_ref, g_ref, t_ref, dx_ref, df_ref, loss_ref, dg_ref):
        fv = f_ref[...]
        g = g_ref[...]
        out = x_ref[...] + fv * _rstd(fv) * g
        err = out - t_ref[...]
        dx = err * (1.0 / d)
        dx_ref[...] = dx
        dfv, dg_rows = _rms_bwd(fv, g, dx)
        df_ref[...] = dfv.astype(BF16)
        _accumulate(loss_ref, _fold8(_fold_lanes(err * err)))
        _accumulate(dg_ref, _fold8(dg_rows))

    return _row_call(body, name, t, [x, f, g_post, target], [True, True, False, True],
                     [(d, F32), (d, BF16)], acc_outs=(128, d))


def bwd_pre_post(name, dx_out, x_in, g_pre, dh, f_prev, g_post_prev):
    t, d = x_in.shape

    def body(dxo_ref, x_ref, gpre_ref, dh_ref, f_ref, gpost_ref, dxi_ref, df_ref, dgpre_ref, dgpost_ref):
        dxv, dgpre_rows = _rms_bwd(x_ref[...], gpre_ref[...], dh_ref[...].astype(F32))
        dxi = dxo_ref[...] + dxv
        dxi_ref[...] = dxi
        dfv, dgpost_rows = _rms_bwd(f_ref[...], gpost_ref[...], dxi)
        df_ref[...] = dfv.astype(BF16)
        _accumulate(dgpre_ref, _fold8(dgpre_rows))
        _accumulate(dgpost_ref, _fold8(dgpost_rows))

    return _row_call(body, name, t, [dx_out, x_in, g_pre, dh, f_prev, g_post_prev],
                     [True, True, False, True, True, False], [(d, F32), (d, BF16)], acc_outs=(d, d))


def bwd_pre_final(name, dx_out, x_in, g_pre, dh):
    t, d = x_in.shape

    def body(dxo_ref, x_ref, gpre_ref, dh_ref, dxi_ref, dgpre_ref):
        dxv, dgpre_rows = _rms_bwd(x_ref[...], gpre_ref[...], dh_ref[...].astype(F32))
        dxi_ref[...] = dxo_ref[...] + dxv
        _accumulate(dgpre_ref, _fold8(dgpre_rows))

    return _row_call(body, name, t, [dx_out, x_in, g_pre, dh], [True, True, False, True], [(d, F32)], acc_outs=(d,))


def _layer_norm_parts(cv):
    mu = jnp.mean(cv, axis=-1, keepdims=True)
    xc = cv - mu
    rstd = lax.rsqrt(jnp.mean(xc * xc, axis=-1, keepdims=True) + NORM_EPS)
    return xc * rstd, rstd


def ln_silu(name, cv, g, b):
    t, n = cv.shape

    def body(c_ref, g_ref, b_ref, y_ref):
        chat, _ = _layer_norm_parts(c_ref[...])
        ln = chat * g_ref[...] + b_ref[...]
        y_ref[...] = (ln * jax.nn.sigmoid(ln)).astype(BF16)

    return _row_call(body, name, t, [cv, g, b], [True, False, False], [(n, BF16)])[0]


def ln_silu_bwd(name, cv, g, b, dy):
    t, n = cv.shape

    def body(c_ref, g_ref, b_ref, dy_ref, dc_ref, dg_ref, db_ref):
        chat, rstd = _layer_norm_parts(c_ref[...])
        g = g_ref[...]
        ln = chat * g + b_ref[...]
        s = jax.nn.sigmoid(ln)
        dln = dy_ref[...].astype(F32) * (s * (1.0 + ln * (1.0 - s)))
        dchat = dln * g
        dc_ref[...] = rstd * (dchat - jnp.mean(dchat, axis=-1, keepdims=True)
                              - chat * jnp.mean(dchat * chat, axis=-1, keepdims=True))
        _accumulate(dg_ref, _fold8(dln * chat))
        _accumulate(db_ref, _fold8(dln))

    return _row_call(body, name, t, [cv, g, b, dy], [True, False, False, True], [(n, F32)], acc_outs=(n, n))


def _chunks(t, fn, tc=TIME_CHUNK):
    tc = _tile(t, tc)

    def step(i, carry):
        fn(pl.multiple_of(i * tc, tc), tc)
        return carry

    lax.fori_loop(0, t // tc, step, 0)


def _shifted(window, offsets, tc):
    by_residue = {}
    for k, off in enumerate(offsets):
        by_residue.setdefault(off % 8, []).append((k, off))
    for res, taps in by_residue.items():
        base = window[res:res + tc + max(off for _, off in taps) - res, :]
        for k, off in taps:
            yield k, base[off - res:off - res + tc, :]


def _taps(window, w_ref, offsets, tc, flip=False):
    acc = None
    for k, rows in _shifted(window, offsets, tc):
        kk = len(offsets) - 1 - k if flip else k
        term = w_ref[kk:kk + 1, :] * rows
        acc = term if acc is None else acc + term
    return acc


def _window_sums(win, tc, causal):
    sums = []
    cur, rows, step = win, tc + HALO, 1
    for _ in POOL_WINDOWS:
        rows -= 8
        if causal:
            cur = cur[8:8 + rows, :] + cur[8 - step:8 - step + rows, :]
            sums.append(cur[rows - tc:rows, :])
        else:
            cur = cur[0:rows, :] + cur[step:step + rows, :]
            sums.append(cur[0:tc, :])
        step *= 2
    return sums


def _pick(vals, g):
    out = vals[-1]
    for i in range(len(vals) - 2, -1, -1):
        out = jnp.where(g == i, vals[i], out)
    return out


def _pool_count(s, tc, g):
    t1 = (lax.broadcasted_iota(jnp.int32, (tc, 1), 0) + (s + 1)).astype(F32)
    width = _pick([float(w) for w in POOL_WINDOWS], g)
    return jnp.minimum(t1, width)


def pool_fwd(name, z, pool_w, pool_scale, d_pool):
    t = z.shape[0]
    ng, pg = pool_w.shape[0], pool_w.shape[1]

    def body(u_ref, w_ref, s_ref, pooled_ref, y_ref, pad):
        g = pl.program_id(0)
        pad[pl.ds(0, HALO), :] = jnp.zeros((HALO, pg), F32)

        def fill(s, tc):
            pad[pl.ds(HALO + s, tc), :] = u_ref[pl.ds(s, tc), :].astype(F32)

        def chunk(s, tc):
            win = pad[pl.ds(s, tc + HALO), :]
            total = _pick(_window_sums(win, tc, causal=True), g)
            pooled = total / _pool_count(s, tc, g) - win[HALO:HALO + tc, :]
            pooled_ref[pl.ds(s, tc), :] = pooled.astype(BF16)

        _chunks(t, fill)
        _chunks(t, chunk)
        mixed = jnp.dot(pooled_ref[...], w_ref[...], preferred_element_type=F32)
        y_ref[...] = (mixed * s_ref[...]).astype(BF16)

    col = pl.BlockSpec((t, pg), lambda g: (0, g))
    return pl.pallas_call(
        body, name=name, grid=(ng,),
        in_specs=[col, pl.BlockSpec((None, pg, pg), lambda g: (g, 0, 0)), pl.BlockSpec((1, pg), lambda g: (0, g))],
        out_specs=[col, col],
        out_shape=[jax.ShapeDtypeStruct((t, d_pool), BF16), jax.ShapeDtypeStruct((t, d_pool), BF16)],
        scratch_shapes=[pltpu.VMEM((t + HALO, pg), F32)],
        compiler_params=_params(("parallel",)),
    )(z, pool_w, pool_scale)


def pool_bwd(name, pooled, dy, pool_w, pool_scale):
    t, d_pool = pooled.shape
    ng, pg = pool_w.shape[0], pool_w.shape[1]

    def body(p_ref, dy_ref, w_ref, s_ref, du_ref, dw_ref, ds_ref, pad):
        g = pl.program_id(0)
        w = w_ref[...]
        dyv = dy_ref[...].astype(F32)
        mixed = jnp.dot(p_ref[...], w, preferred_element_type=F32)
        ds_ref[...] = jnp.sum(dyv * mixed, axis=0, keepdims=True)
        dmixed = (dyv * s_ref[...]).astype(BF16)
        dw_ref[...] = lax.dot_general(p_ref[...], dmixed, TN, preferred_element_type=F32)
        pad[...] = jnp.zeros((t + HALO, pg), F32)
        pad[pl.ds(0, t), :] = lax.dot_general(dmixed, w, NT, preferred_element_type=F32)

        def scale(s, tc):
            pad[pl.ds(s, tc), :] = pad[pl.ds(s, tc), :] / _pool_count(s, tc, g)

        def chunk(s, tc):
            win = pad[pl.ds(s, tc + HALO), :]
            total = _pick(_window_sums(win, tc, causal=False), g)
            du_ref[pl.ds(s, tc), :] = (total - win[0:tc, :] * _pool_count(s, tc, g)).astype(BF16)

        _chunks(t, scale)
        _chunks(t, chunk)

    col = pl.BlockSpec((t, pg), lambda g: (0, g))
    vec = pl.BlockSpec((1, pg), lambda g: (0, g))
    mat = pl.BlockSpec((None, pg, pg), lambda g: (g, 0, 0))
    return pl.pallas_call(
        body, name=name, grid=(ng,),
        in_specs=[col, col, mat, vec], out_specs=[col, mat, vec],
        out_shape=[jax.ShapeDtypeStruct((t, d_pool), BF16), jax.ShapeDtypeStruct((ng, pg, pg), F32),
                   jax.ShapeDtypeStruct((1, d_pool), F32)],
        scratch_shapes=[pltpu.VMEM((t + HALO, pg), F32)],
        compiler_params=_params(("parallel",)),
    )(pooled, dy, pool_w, pool_scale)


def conv_fwd(name, z, conv_w, conv_b, d_pool, d_conv):
    t = z.shape[0]
    kw = conv_w.shape[0]
    tc_ch = _tile(d_conv, CHANNEL_TILE)
    v0, g0 = d_pool // tc_ch, (d_pool + d_conv) // tc_ch

    def body(v_ref, g_ref, w_ref, b_ref, c_ref, pad):
        pad[pl.ds(0, HALO), :] = jnp.zeros((HALO, tc_ch), F32)

        def fill(s, tc):
            pad[pl.ds(HALO + s, tc), :] = v_ref[pl.ds(s, tc), :].astype(F32) * jax.nn.sigmoid(g_ref[pl.ds(s, tc), :].astype(F32))

        def chunk(s, tc):
            win = pad[pl.ds(s, tc + HALO), :]
            c_ref[pl.ds(s, tc), :] = _taps(win, w_ref, [HALO - (kw - 1) + k for k in range(kw)], tc) + b_ref[...]

        _chunks(t, fill)
        _chunks(t, chunk)

    return pl.pallas_call(
        body, name=name, grid=(d_conv // tc_ch,),
        in_specs=[pl.BlockSpec((t, tc_ch), lambda j: (0, v0 + j)), pl.BlockSpec((t, tc_ch), lambda j: (0, g0 + j)),
                  pl.BlockSpec((kw, tc_ch), lambda j: (0, j)), pl.BlockSpec((1, tc_ch), lambda j: (0, j))],
        out_specs=pl.BlockSpec((t, tc_ch), lambda j: (0, j)),
        out_shape=jax.ShapeDtypeStruct((t, d_conv), F32),
        scratch_shapes=[pltpu.VMEM((t + HALO, tc_ch), F32)],
        compiler_params=_params(("parallel",)),
    )(z, z, conv_w, conv_b)


def conv_bwd(name, z, dc, conv_w, d_pool, d_conv):
    t = z.shape[0]
    kw = conv_w.shape[0]
    tc_ch = _tile(d_conv, CHANNEL_TILE)
    v0, g0 = d_pool // tc_ch, (d_pool + d_conv) // tc_ch

    def body(v_ref, g_ref, dc_ref, w_ref, dv_ref, dg_ref, dw_ref, db_ref, pad_a, pad_dc, acc_w, acc_b):
        pad_a[pl.ds(0, HALO), :] = jnp.zeros((HALO, tc_ch), F32)
        pad_dc[pl.ds(t, HALO), :] = jnp.zeros((HALO, tc_ch), F32)
        acc_w[...] = jnp.zeros_like(acc_w)
        acc_b[...] = jnp.zeros_like(acc_b)

        def fill(s, tc):
            pad_a[pl.ds(HALO + s, tc), :] = v_ref[pl.ds(s, tc), :].astype(F32) * jax.nn.sigmoid(g_ref[pl.ds(s, tc), :].astype(F32))
            pad_dc[pl.ds(s, tc), :] = dc_ref[pl.ds(s, tc), :]

        def chunk(s, tc):
            dcv = pad_dc[pl.ds(s, tc), :]
            win_a = pad_a[pl.ds(s, tc + HALO), :]
            for k, rows in _shifted(win_a, [HALO - (kw - 1) + k for k in range(kw)], tc):
                acc_w[pl.ds(8 * k, 8), :] += _fold8(dcv * rows)
            acc_b[...] += _fold8(dcv)
            da = _taps(pad_dc[pl.ds(s, tc + HALO), :], w_ref, list(range(kw)), tc, flip=True)
            vv = v_ref[pl.ds(s, tc), :].astype(F32)
            sg = jax.nn.sigmoid(g_ref[pl.ds(s, tc), :].astype(F32))
            dv_ref[pl.ds(s, tc), :] = (da * sg).astype(BF16)
            dg_ref[pl.ds(s, tc), :] = (da * vv * sg * (1.0 - sg)).astype(BF16)

        _chunks(t, fill)
        _chunks(t, chunk)
        for k in range(kw):
            dw_ref[k:k + 1, :] = jnp.sum(acc_w[pl.ds(8 * k, 8), :], axis=0, keepdims=True)
        db_ref[...] = jnp.sum(acc_b[...], axis=0, keepdims=True)

    col = pl.BlockSpec((t, tc_ch), lambda j: (0, j))
    return pl.pallas_call(
        body, name=name, grid=(d_conv // tc_ch,),
        in_specs=[pl.BlockSpec((t, tc_ch), lambda j: (0, v0 + j)), pl.BlockSpec((t, tc_ch), lambda j: (0, g0 + j)),
                  col, pl.BlockSpec((kw, tc_ch), lambda j: (0, j))],
        out_specs=[col, col, pl.BlockSpec((kw, tc_ch), lambda j: (0, j)), pl.BlockSpec((1, tc_ch), lambda j: (0, j))],
        out_shape=[jax.ShapeDtypeStruct((t, d_conv), BF16), jax.ShapeDtypeStruct((t, d_conv), BF16),
                   jax.ShapeDtypeStruct((kw, d_conv), F32), jax.ShapeDtypeStruct((1, d_conv), F32)],
        scratch_shapes=[pltpu.VMEM((t + HALO, tc_ch), F32), pltpu.VMEM((t + HALO, tc_ch), F32),
                        pltpu.VMEM((8 * kw, tc_ch), F32), pltpu.VMEM((8, tc_ch), F32)],
        compiler_params=_params(("parallel",)),
    )(z, z, dc, conv_w)


def short_fwd(name, z, conv_w, d_short):
    t = z.shape[0]
    kw = conv_w.shape[0]
    tc_ch = _tile(d_short, CHANNEL_TILE)
    nt = d_short // tc_ch

    def body(b_ref, c_ref, u_ref, w_ref, y_ref, pad):
        pad[pl.ds(0, HALO), :] = jnp.zeros((HALO, tc_ch), F32)

        def fill(s, tc):
            pad[pl.ds(HALO + s, tc), :] = c_ref[pl.ds(s, tc), :].astype(F32) * u_ref[pl.ds(s, tc), :].astype(F32)

        def chunk(s, tc):
            win = pad[pl.ds(s, tc + HALO), :]
            cq = _taps(win, w_ref, [HALO - (kw - 1) + k for k in range(kw)], tc)
            y_ref[pl.ds(s, tc), :] = (b_ref[pl.ds(s, tc), :].astype(F32) * cq).astype(BF16)

        _chunks(t, fill)
        _chunks(t, chunk)

    return pl.pallas_call(
        body, name=name, grid=(nt,),
        in_specs=[pl.BlockSpec((t, tc_ch), lambda j: (0, j)), pl.BlockSpec((t, tc_ch), lambda j: (0, nt + j)),
                  pl.BlockSpec((t, tc_ch), lambda j: (0, 2 * nt + j)), pl.BlockSpec((kw, tc_ch), lambda j: (0, j))],
        out_specs=pl.BlockSpec((t, tc_ch), lambda j: (0, j)),
        out_shape=jax.ShapeDtypeStruct((t, d_short), BF16),
        scratch_shapes=[pltpu.VMEM((t + HALO, tc_ch), F32)],
        compiler_params=_params(("parallel",)),
    )(z, z, z, conv_w)


def short_bwd(name, z, dy, conv_w, d_short):
    t = z.shape[0]
    kw = conv_w.shape[0]
    tc_ch = _tile(d_short, CHANNEL_TILE)
    nt = d_short // tc_ch

    def body(b_ref, c_ref, u_ref, dy_ref, w_ref, db_ref, dcg_ref, du_ref, dw_ref, pad_q, pad_dcq, acc_w):
        pad_q[pl.ds(0, HALO), :] = jnp.zeros((HALO, tc_ch), F32)
        pad_dcq[pl.ds(t, HALO), :] = jnp.zeros((HALO, tc_ch), F32)
        acc_w[...] = jnp.zeros_like(acc_w)

        def fill(s, tc):
            rows = pl.ds(s, tc)
            pad_q[pl.ds(HALO + s, tc), :] = c_ref[rows, :].astype(F32) * u_ref[rows, :].astype(F32)
            pad_dcq[rows, :] = dy_ref[rows, :].astype(F32) * b_ref[rows, :].astype(F32)

        def chunk(s, tc):
            rows = pl.ds(s, tc)
            win_q = pad_q[pl.ds(s, tc + HALO), :]
            dcq = pad_dcq[rows, :]
            cq = None
            for k in range(kw):
                off = HALO - (kw - 1) + k
                shifted = win_q[off:off + tc, :]
                acc_w[pl.ds(8 * k, 8), :] += _fold8(dcq * shifted)
                term = w_ref[k:k + 1, :] * shifted
                cq = term if cq is None else cq + term
            db_ref[rows, :] = (dy_ref[rows, :].astype(F32) * cq).astype(BF16)
            win_d = pad_dcq[pl.ds(s, tc + HALO), :]
            dq = None
            for j in range(kw):
                term = w_ref[kw - 1 - j:kw - j, :] * win_d[j:j + tc, :]
                dq = term if dq is None else dq + term
            dcg_ref[rows, :] = (dq * u_ref[rows, :].astype(F32)).astype(BF16)
            du_ref[rows, :] = (dq * c_ref[rows, :].astype(F32)).astype(BF16)

        _chunks(t, fill)
        _chunks(t, chunk)
        for k in range(kw):
            dw_ref[k:k + 1, :] = jnp.sum(acc_w[pl.ds(8 * k, 8), :], axis=0, keepdims=True)

    col = pl.BlockSpec((t, tc_ch), lambda j: (0, j))
    zspec = [pl.BlockSpec((t, tc_ch), lambda j, o=o: (0, o * nt + j)) for o in range(3)]
    return pl.pallas_call(
        body, name=name, grid=(nt,),
        in_specs=[*zspec, col, pl.BlockSpec((kw, tc_ch), lambda j: (0, j))],
        out_specs=[col, col, col, pl.BlockSpec((kw, tc_ch), lambda j: (0, j))],
        out_shape=[jax.ShapeDtypeStruct((t, d_short), BF16)] * 3 + [jax.ShapeDtypeStruct((kw, d_short), F32)],
        scratch_shapes=[pltpu.VMEM((t + HALO, tc_ch), F32), pltpu.VMEM((t + HALO, tc_ch), F32),
                        pltpu.VMEM((8 * kw, tc_ch), F32)],
        compiler_params=_params(("parallel",)),
    )(z, z, z, dy, conv_w)


def adamw(name, w, m, v, contributions):
    r, c = w.shape
    nc = len(contributions)
    n_slots = contributions[0].shape[0]
    tr = 256 if c <= 1024 else 128
    if any(a.shape[1] % tr for a in contributions):
        assert nc == 1
        tr = r
    tiles = [a.shape[1] // tr for a in contributions]
    first = [sum(tiles[:j]) for j in range(nc)]

    def body(w_ref, m_ref, v_ref, *rest):
        g_refs, (grad_ref, delta_ref, nm_ref, nv_ref) = rest[:nc], rest[nc:]
        i = pl.program_id(0)
        g = None
        for j, g_ref in enumerate(g_refs):
            s = g_ref[0].astype(F32)
            for slot in range(1, n_slots):
                s = s + g_ref[slot].astype(F32)
            g = s if g is None else jnp.where(i >= first[j], s, g)
        nm = ADAM_B1 * m_ref[...] + (1.0 - ADAM_B1) * g
        nv = ADAM_B2 * v_ref[...] + (1.0 - ADAM_B2) * (g * g)
        m_hat = nm / (1.0 - ADAM_B1 ** ADAM_STEP)
        v_hat = nv / (1.0 - ADAM_B2 ** ADAM_STEP)
        grad_ref[...] = g
        delta_ref[...] = -ADAM_LR * (m_hat / (jnp.sqrt(v_hat) + ADAM_EPS) + ADAM_WD * w_ref[...])
        nm_ref[...] = nm
        nv_ref[...] = nv

    blk = pl.BlockSpec((tr, c), lambda i: (i, 0))
    g_specs = [pl.BlockSpec((n_slots, tr, c), lambda i, j=j: (0, jnp.clip(i - first[j], 0, tiles[j] - 1), 0))
               for j in range(nc)]
    return pl.pallas_call(
        body, name=name, grid=(r // tr,),
        in_specs=[blk, blk, blk, *g_specs],
        out_specs=[blk] * 4, out_shape=[jax.ShapeDtypeStruct((r, c), F32)] * 4,
        compiler_params=_params(("parallel",)),
    )(w, m, v, *contributions)


def _pad_rows(a, rows):
    return jnp.pad(a, ((0, rows - a.shape[0]), (0, 0)))


def kernel(x, mix_pre_g, mix_post_g, ffn_pre_g, ffn_post_g, ab_w_in, pool_w, pool_scale, conv_w, conv_b, conv_ln_g, conv_ln_b, ab_w_out, sc_w_in, sc_conv_w, sc_w_out, ffn_w1, ffn_w2, loss_target, m_mix_pre_g, m_mix_post_g, m_ffn_pre_g, m_ffn_post_g, m_ab_w_in, m_pool_w, m_pool_scale, m_conv_w, m_conv_b, m_conv_ln_g, m_conv_ln_b, m_ab_w_out, m_sc_w_in, m_sc_conv_w, m_sc_w_out, m_ffn_w1, m_ffn_w2, v_mix_pre_g, v_mix_post_g, v_ffn_pre_g, v_ffn_post_g, v_ab_w_in, v_pool_w, v_pool_scale, v_conv_w, v_conv_b, v_conv_ln_g, v_conv_ln_b, v_ab_w_out, v_sc_w_in, v_sc_conv_w, v_sc_w_out, v_ffn_w1, v_ffn_w2):
    t, d = x.shape[1], x.shape[2]
    d_pool = pool_scale.shape[1]
    d_conv = conv_b.shape[1]
    d_short = d
    ng, pg = pool_w.shape[1], pool_w.shape[3]
    kw, ks = conv_w.shape[1], sc_conv_w.shape[1]
    nb_ab, nb_sc, nb_ff = ab_w_in.shape[2], sc_w_in.shape[2], ffn_w1.shape[2]

    xs = x[0]
    target = loss_target[0]

    lanes = min(128, d_conv // N_DEV)
    small_rows = [kw * (d_conv // N_DEV) // lanes, ks * (d_short // N_DEV) // lanes, ng * (pg // N_DEV) * pg // lanes]
    small_total = -(-sum(small_rows) // 8) * 8
    r0, r1, r2 = small_rows[0], small_rows[0] + small_rows[1], sum(small_rows)

    def pack_small(a_conv, a_sconv, a_pool):
        parts = [a_conv[0].reshape(-1, lanes), a_sconv[0].reshape(-1, lanes), a_pool[0].reshape(-1, lanes)]
        return _pad_rows(jnp.concatenate(parts, axis=0), small_total)

    shards = {
        "ab_in": (ab_w_in[0], BF16), "small": (pack_small(conv_w, sc_conv_w, pool_w), F32),
        "ab_out": (ab_w_out[0], BF16), "ff1_0": (ffn_w1[0], BF16), "ff2_0": (ffn_w2[0], BF16),
        "sc_in": (sc_w_in[0], BF16), "sc_out": (sc_w_out[0], BF16),
        "ff1_1": (ffn_w1[1], BF16), "ff2_1": (ffn_w2[1], BF16)}
    zones = [place_shard("place_" + nm, w, dt) for nm, (w, dt) in shards.items()]
    started, token = copies_start("gather_start", [[z] for z in zones], _first_hop, 4)
    started = dict(zip(shards, started))

    ties = [0]

    def after(v, *deps):
        ties[0] += 1
        return tie(f"tie_{ties[0]}", v, *deps)

    def fetch_begin(nm, dep):
        (zone,) = copies_wait("gather_wait_" + nm, started[nm], _first_hop, dep)
        (hop,), tok = copies_start("forward_start_" + nm, [[zone]], _second_hop, 3)
        return hop, tok

    def fetch_end(nm, hop, dep):
        return copies_wait("forward_wait_" + nm, hop, _second_hop, dep)[0]

    relu = lambda r: jnp.maximum(r, 0.0)
    square = lambda a: a * a
    relu2_bwd = lambda r, a: r * (2.0 * a.astype(F32))

    def row(vec, l):
        return vec[l:l + 1]

    hop_small, _ = fetch_begin("small", token)
    hop_ab_in, tok = fetch_begin("ab_in", token)
    w_small = fetch_end("small", hop_small, tok)
    w_ab_in = fetch_end("ab_in", hop_ab_in, tok)
    w_conv = w_small[:, :r0].reshape(N_DEV, kw, -1).transpose(1, 0, 2).reshape(kw, d_conv)
    w_sconv = w_small[:, r0:r1].reshape(N_DEV, ks, -1).transpose(1, 0, 2).reshape(ks, d_short)
    w_pool = w_small[:, r1:r2].reshape(N_DEV, ng, -1, pg).transpose(1, 0, 2, 3).reshape(ng, pg, pg).astype(BF16)
    hop, tok = fetch_begin("ab_out", w_ab_in)
    h0 = after(norm_pre("norm_pre", xs, row(mix_pre_g, 0)), tok)
    z0 = mm_nn_blocked("ab_in", h0, w_ab_in, out_dtype=BF16)
    pooled, y_pool = pool_fwd("pool_fwd", z0, w_pool, pool_scale, d_pool)
    cv = conv_fwd("conv_fwd", z0, w_conv, conv_b, d_pool, d_conv)
    y_conv = ln_silu("ln_silu", cv, conv_ln_g, conv_ln_b)
    y0 = jnp.concatenate([y_pool, y_conv], axis=1)
    w_ab_out = fetch_end("ab_out", hop, y0)
    hop, tok = fetch_begin("ff1_0", w_ab_out)
    y0 = after(y0, tok)
    m0 = mm_nn("ab_out", y0, w_ab_out.reshape(d_pool + d_conv, d), out_dtype=F32)
    x1, h1 = post_pre("post_pre_0", xs, m0, row(mix_post_g, 0), row(ffn_pre_g, 0))
    w_ff1_0 = fetch_end("ff1_0", hop, h1)
    hop, tok = fetch_begin("ff2_0", w_ff1_0)
    h1 = after(h1, tok)
    a0 = mm_nn_blocked("ffn0_up", h1, w_ff1_0, out_dtype=BF16, epilogue=relu)
    w_ff2_0 = fetch_end("ff2_0", hop, a0).reshape(-1, d)
    hop, tok = fetch_begin("sc_in", w_ff2_0)
    a0 = after(a0, tok)
    f0 = mm_nn("ffn0_down", a0, w_ff2_0, out_dtype=F32, tk=2048, lhs_fn=square)
    x2, h2 = post_pre("post_pre_1", x1, f0, row(ffn_post_g, 0), row(mix_pre_g, 1))
    w_sc_in = fetch_end("sc_in", hop, h2)
    hop, tok = fetch_begin("sc_out", w_sc_in)
    h2 = after(h2, tok)
    z1 = mm_nn_blocked("sc_in", h2, w_sc_in, out_dtype=BF16)
    y1 = short_fwd("short_fwd", z1, w_sconv, d_short)
    w_sc_out = fetch_end("sc_out", hop, y1).reshape(d_short, d)
    hop, tok = fetch_begin("ff1_1", w_sc_out)
    y1 = after(y1, tok)
    m1 = mm_nn("sc_out", y1, w_sc_out, out_dtype=F32)
    x3, h3 = post_pre("post_pre_2", x2, m1, row(mix_post_g, 1), row(ffn_pre_g, 1))
    w_ff1_1 = fetch_end("ff1_1", hop, h3)
    hop, tok = fetch_begin("ff2_1", w_ff1_1)
    h3 = after(h3, tok)
    a1 = mm_nn_blocked("ffn1_up", h3, w_ff1_1, out_dtype=BF16, epilogue=relu)
    w_ff2_1 = fetch_end("ff2_1", hop, a1).reshape(-1, d)
    f1 = mm_nn("ffn1_down", a1, w_ff2_1, out_dtype=F32, tk=2048, lhs_fn=square)
    dx4, df1, loss_part, dg_ffn_post1 = post_loss("post_loss", x3, f1, row(ffn_post_g, 1), target)
    loss = lax.psum(jnp.sum(loss_part) * (0.5 / d), ("x", "y", "c"))

    def reduce_begin(tag, g):
        zone = lax.empty((N_CHIP,) + g.shape[1:], g.dtype)
        (hop,), tok = copies_start("pair_start_" + tag, [[g, zone]], _pair_hop, N_CHIP)
        return hop, tok

    def reduce_middle(tag, hop, dep):
        g, from_sibling = copies_wait("pair_wait_" + tag, hop, _pair_hop, dep)
        pair_sum, zone = pair_add("pair_add_" + tag, g, from_sibling)
        (hop2,), tok = copies_start("chips_start_" + tag, [[pair_sum, zone]], _chip_hop, 3)
        return hop2, tok

    def reduce_end(tag, hop2, dep):
        return copies_wait("chips_wait_" + tag, hop2, _chip_hop, dep)[1]

    dw = mm_tn("ffn1_dw2", a1, df1, out_dtype=BF16, lhs_fn=square)
    red_ff2_1, tok = reduce_begin("ff2_1", dw.reshape(N_DEV, -1, d))
    df1 = after(df1, tok)
    dpre = mm_nt("ffn1_da", df1, w_ff2_1, out_dtype=BF16, extra=a1, epilogue=relu2_bwd)
    dw = mm_tn_blocked("ffn1_dw1", h3, dpre, nb_ff, out_dtype=BF16)
    red_ff1_1, tok = reduce_begin("ff1_1", dw)
    dpre = after(dpre, tok)
    dh3 = mm_nt_blocked("ffn1_dh", dpre, w_ff1_1, out_dtype=BF16)
    red_ff2_1, tok = reduce_middle("ff2_1", red_ff2_1, dh3)
    dh3 = after(dh3, tok)
    dx3, dm1, dg_ffn_pre1, dg_mix_post1 = bwd_pre_post("bwd_3", dx4, x3, row(ffn_pre_g, 1), dh3, m1, row(mix_post_g, 1))

    dw = mm_tn("sc_dwout", y1, dm1, out_dtype=BF16)
    red_sc_out, tok = reduce_begin("sc_out", dw.reshape(N_DEV, -1, d))
    dm1 = after(dm1, tok)
    dy1 = mm_nt("sc_dy", dm1, w_sc_out, out_dtype=BF16)
    red_ff1_1, tok = reduce_middle("ff1_1", red_ff1_1, dy1)
    dy1 = after(dy1, tok)
    db1, dcg1, du1, dw_sconv = short_bwd("short_bwd", z1, dy1, w_sconv, d_short)
    dz1 = jnp.concatenate([db1, dcg1, du1], axis=1)
    dw = mm_tn_blocked("sc_dwin", h2, dz1, nb_sc, out_dtype=BF16)
    red_sc_in, tok = reduce_begin("sc_in", dw)
    dz1 = after(dz1, tok)
    dh2 = mm_nt_blocked("sc_dh", dz1, w_sc_in, out_dtype=BF16)
    red_sc_out, tok = reduce_middle("sc_out", red_sc_out, dh2)
    dh2 = after(dh2, tok)
    dx2, df0, dg_mix_pre1, dg_ffn_post0 = bwd_pre_post("bwd_2", dx3, x2, row(mix_pre_g, 1), dh2, f0, row(ffn_post_g, 0))

    dw = mm_tn("ffn0_dw2", a0, df0, out_dtype=BF16, lhs_fn=square)
    red_ff2_0, tok = reduce_begin("ff2_0", dw.reshape(N_DEV, -1, d))
    df0 = after(df0, tok)
    dpre = mm_nt("ffn0_da", df0, w_ff2_0, out_dtype=BF16, extra=a0, epilogue=relu2_bwd)
    red_sc_in, tok = reduce_middle("sc_in", red_sc_in, dpre)
    dpre = after(dpre, tok)
    dw = mm_tn_blocked("ffn0_dw1", h1, dpre, nb_ff, out_dtype=BF16)
    red_ff1_0, tok = reduce_begin("ff1_0", dw)
    dpre = after(dpre, tok)
    dh1 = mm_nt_blocked("ffn0_dh", dpre, w_ff1_0, out_dtype=BF16)
    red_ff2_0, tok = reduce_middle("ff2_0", red_ff2_0, dh1)
    dh1 = after(dh1, tok)
    dx1, dm0, dg_ffn_pre0, dg_mix_post0 = bwd_pre_post("bwd_1", dx2, x1, row(ffn_pre_g, 0), dh1, m0, row(mix_post_g, 0))

    dw = mm_tn("ab_dwout", y0, dm0, out_dtype=BF16)
    red_ab_out, tok = reduce_begin("ab_out", dw.reshape(N_DEV, -1, d))
    dm0 = after(dm0, tok)
    dy0 = mm_nt("ab_dy", dm0, w_ab_out.reshape(d_pool + d_conv, d), out_dtype=BF16)
    red_ff1_0, tok = reduce_middle("ff1_0", red_ff1_0, dy0)
    dy0 = after(dy0, tok)
    dcv, dg_ln_g, dg_ln_b = ln_silu_bwd("ln_silu_bwd", cv, conv_ln_g, conv_ln_b, dy0[:, d_pool:])
    dv, dgate, dw_conv, dg_conv_b = conv_bwd("conv_bwd", z0, dcv, w_conv, d_pool, d_conv)
    du0, dw_pool, dg_pool_scale = pool_bwd("pool_bwd", pooled, dy0[:, :d_pool], w_pool, pool_scale)
    dz0 = jnp.concatenate([du0, dv, dgate], axis=1)
    small_parts = [
        dw_conv.reshape(kw, N_DEV, -1).transpose(1, 0, 2).reshape(N_DEV, -1, lanes),
        dw_sconv.reshape(ks, N_DEV, -1).transpose(1, 0, 2).reshape(N_DEV, -1, lanes),
        dw_pool.reshape(ng, N_DEV, pg // N_DEV, pg).transpose(1, 0, 2, 3).reshape(N_DEV, -1, lanes),
    ]
    small = jnp.pad(jnp.concatenate(small_parts, axis=1), ((0, 0), (0, small_total - r2), (0, 0)))
    red_small, tok = reduce_begin("small", small)
    red_ab_out, tok2 = reduce_middle("ab_out", red_ab_out, dz0)
    dz0 = after(dz0, tok, tok2)
    dw = mm_tn_blocked("ab_dwin", h0, dz0, nb_ab, out_dtype=BF16)
    red_ab_in, tok = reduce_begin("ab_in", dw)
    dz0 = after(dz0, tok)
    dh0 = mm_nt_blocked("ab_dh", dz0, w_ab_in, out_dtype=BF16)
    red_small, tok = reduce_middle("small", red_small, dh0)
    dh0 = after(dh0, tok)
    grad_x, dg_mix_pre0 = bwd_pre_final("bwd_0", dx1, xs, row(mix_pre_g, 0), dh0)
    red_ab_in, tok = reduce_middle("ab_in", red_ab_in, grad_x)

    fold = lambda a: jnp.sum(a, axis=0, keepdims=True)
    rep_rows = [fold(dg_mix_pre0), fold(dg_mix_pre1), fold(dg_mix_post0), fold(dg_mix_post1),
                fold(dg_ffn_pre0), fold(dg_ffn_pre1), fold(dg_ffn_post0), fold(dg_ffn_post1)]
    tail = jnp.concatenate([dg_pool_scale, dg_conv_b, fold(dg_ln_g), fold(dg_ln_b)], axis=1).reshape(-1, d)
    rep = _pad_rows(jnp.concatenate(rep_rows + [tail], axis=0), 16)
    (rep_hop,), _ = copies_start("rep_start", [[place_shard("place_rep", rep, F32)]], _first_hop, 4)

    def pack_rep(a_mix_pre, a_mix_post, a_ffn_pre, a_ffn_post, a_scale, a_b, a_g, a_lb):
        tail_ = jnp.concatenate([a_scale, a_b, a_g, a_lb], axis=1).reshape(-1, d)
        return _pad_rows(jnp.concatenate([a_mix_pre, a_mix_post, a_ffn_pre, a_ffn_post, tail_], axis=0), 16)

    def upd(name, w, m, v, contribs):
        shape = w.shape
        flat2 = lambda a: a.reshape(-1, shape[-1])
        outs = adamw(name, flat2(w), flat2(m), flat2(v), contribs)
        return [o.reshape(shape) for o in outs]

    g_ff2 = [reduce_end("ff2_0", red_ff2_0, tok), reduce_end("ff2_1", red_ff2_1, tok)]
    o_ff2 = upd("adam_ffn_w2", ffn_w2, m_ffn_w2, v_ffn_w2, g_ff2)
    (rep_zone,) = copies_wait("rep_wait", rep_hop, _first_hop, o_ff2[0])
    (rep_hop,), _ = copies_start("rep_forward_start", [[rep_zone]], _second_hop, 3)
    g_ff1 = [reduce_end("ff1_0", red_ff1_0, o_ff2[0]), reduce_end("ff1_1", red_ff1_1, o_ff2[0])]
    o_ff1 = upd("adam_ffn_w1", ffn_w1, m_ffn_w1, v_ffn_w1, g_ff1)
    (rep_all,) = copies_wait("rep_forward_wait", rep_hop, _second_hop, o_ff1[0])
    o_rep = adamw("adam_replicated",
                  pack_rep(mix_pre_g, mix_post_g, ffn_pre_g, ffn_post_g, pool_scale, conv_b, conv_ln_g, conv_ln_b),
                  pack_rep(m_mix_pre_g, m_mix_post_g, m_ffn_pre_g, m_ffn_post_g, m_pool_scale, m_conv_b, m_conv_ln_g, m_conv_ln_b),
                  pack_rep(v_mix_pre_g, v_mix_post_g, v_ffn_pre_g, v_ffn_post_g, v_pool_scale, v_conv_b, v_conv_ln_g, v_conv_ln_b),
                  [rep_all])
    o_sc_out = upd("adam_sc_out", sc_w_out, m_sc_w_out, v_sc_w_out, [reduce_end("sc_out", red_sc_out, o_ff1[0])])
    o_sc_in = upd("adam_sc_in", sc_w_in, m_sc_w_in, v_sc_w_in, [reduce_end("sc_in", red_sc_in, o_sc_out[0])])
    o_ab_out = upd("adam_ab_out", ab_w_out, m_ab_w_out, v_ab_w_out, [reduce_end("ab_out", red_ab_out, o_sc_in[0])])
    o_small = adamw("adam_small", pack_small(conv_w, sc_conv_w, pool_w), pack_small(m_conv_w, m_sc_conv_w, m_pool_w),
                    pack_small(v_conv_w, v_sc_conv_w, v_pool_w), [reduce_end("small", red_small, o_ab_out[0])])
    o_ab_in = upd("adam_ab_in", ab_w_in, m_ab_w_in, v_ab_w_in, [reduce_end("ab_in", red_ab_in, o_small[0])])

    def unpack_small(o):
        return o[:r0].reshape(conv_w.shape), o[r0:r1].reshape(sc_conv_w.shape), o[r1:r2].reshape(pool_w.shape)

    def unpack_rep(o):
        tail_ = o[8:8 + tail.shape[0]].reshape(1, -1)
        n1 = d_pool
        return dict(mix_pre_g=o[0:2], mix_post_g=o[2:4], ffn_pre_g=o[4:6], ffn_post_g=o[6:8],
                    pool_scale=tail_[:, :n1], conv_b=tail_[:, n1:n1 + d_conv],
                    conv_ln_g=tail_[:, n1 + d_conv:n1 + 2 * d_conv], conv_ln_b=tail_[:, n1 + 2 * d_conv:n1 + 3 * d_conv])

    results = []
    for kind in range(4):
        rep_o = unpack_rep(o_rep[kind])
        s_conv, s_sconv, s_pool = unpack_small(o_small[kind])
        results.append([
            rep_o["mix_pre_g"], rep_o["mix_post_g"], rep_o["ffn_pre_g"], rep_o["ffn_post_g"],
            o_ab_in[kind], s_pool, rep_o["pool_scale"], s_conv, rep_o["conv_b"], rep_o["conv_ln_g"], rep_o["conv_ln_b"],
            o_ab_out[kind], o_sc_in[kind], s_sconv, o_sc_out[kind], o_ff1[kind], o_ff2[kind]])

    return (loss, grad_x[None], *results[0], *results[1], *results[2], *results[3])
```

```python
import jax
import jax.numpy as jnp
from jax import lax
from jax.experimental import pallas as pl
from jax.experimental.pallas import tpu as pltpu

F32 = jnp.float32
BF16 = jnp.bfloat16
MESH = pl.DeviceIdType.MESH
ANY = pl.BlockSpec(memory_space=pl.ANY)

NORM_EPS = 1e-6
POOL_WINDOWS = (2, 4, 8, 16)
ADAM_LR = 0.001
ADAM_B1 = 0.9
ADAM_B2 = 0.999
ADAM_EPS = 1e-08
ADAM_WD = 0.01
ADAM_STEP = 10

N_DEV = 8
VMEM_LIMIT = 56 * 1024 * 1024
PAIR_ADD_BLOCK = 1 << 20
ROW_TILE = 256
CHANNEL_TILE = 256
TIME_CHUNK = 64
HALO = 32

NN = (((1,), (0,)), ((), ()))
NT = (((1,), (1,)), ((), ()))
TN = (((0,), (0,)), ((), ()))


def _params(sem):
    return pltpu.CompilerParams(dimension_semantics=sem, vmem_limit_bytes=VMEM_LIMIT)


def _place():
    x, y, c = lax.axis_index("x"), lax.axis_index("y"), lax.axis_index("c")
    return x, y, c


def _slot(px, py, pc):
    return 4 * px + 2 * py + pc


HBM = pl.BlockSpec(memory_space=pltpu.HBM)
SEM = pl.BlockSpec(memory_space=pltpu.SEMAPHORE)
EFFECT = pltpu.SideEffectType.DATAFLOW_SIDE_EFFECTING
TOKEN = jax.ShapeDtypeStruct((8, 128), F32)


def _in_hbm(a):
    return pltpu.with_memory_space_constraint(a, pltpu.HBM)


CHIPS = [(0, 0), (0, 1), (1, 0), (1, 1)]
N_CHIP = len(CHIPS)


def _chip(px, py):
    return 2 * px + py


def _first_hop(bufs, sends, recvs, waiting):
    (land,) = bufs
    x, y, c = _place()
    me = _slot(x, y, c)
    peers = [(x, y, 1 - c), (1 - x, y, c), (x, 1 - y, c), (1 - x, 1 - y, c)]
    return [pltpu.make_async_remote_copy(
        src_ref=land.at[me], dst_ref=land.at[_slot(*p) if waiting else me],
        send_sem=sends.at[k], recv_sem=recvs.at[k], device_id=p, device_id_type=MESH) for k, p in enumerate(peers)]


def _second_hop(bufs, sends, recvs, waiting):
    (land,) = bufs
    x, y, c = _place()
    return [pltpu.make_async_remote_copy(
        src_ref=land.at[_slot(px, py, c)], dst_ref=land.at[_slot(px, py, 1 - c if waiting else c)],
        send_sem=sends.at[k], recv_sem=recvs.at[k], device_id=(x, y, 1 - c), device_id_type=MESH)
        for k, (px, py) in enumerate([(1 - x, y), (x, 1 - y), (1 - x, 1 - y)])]


def _ring_hop1(bufs, sends, recvs, waiting):
    (land,) = bufs
    x, y, c = _place()
    me = _slot(x, y, c)
    peers = [(1 - x, y, c), (x, 1 - y, c), (x, y, 1 - c)]
    return [pltpu.make_async_remote_copy(
        src_ref=land.at[me], dst_ref=land.at[_slot(*p) if waiting else me],
        send_sem=sends.at[k], recv_sem=recvs.at[k], device_id=p, device_id_type=MESH) for k, p in enumerate(peers)]


def _ring_hop2(bufs, sends, recvs, waiting):
    (land,) = bufs
    x, y, c = _place()
    half = land.shape[1] // 2
    first, second = pl.ds(0, half), pl.ds(half, half)
    nx, ny, diag = _slot(1 - x, y, c), _slot(x, 1 - y, c), _slot(1 - x, 1 - y, c)
    plan = [
        (land.at[ny, first], land.at[diag, first], (1 - x, y, c)),
        (land.at[nx, second], land.at[diag, second], (x, 1 - y, c)),
        (land.at[nx], land.at[_slot(1 - x, y, 1 - c)], (x, y, 1 - c)),
        (land.at[ny], land.at[_slot(x, 1 - y, 1 - c)], (x, y, 1 - c))]
    return [pltpu.make_async_remote_copy(
        src_ref=src, dst_ref=mine if waiting else src, send_sem=sends.at[k], recv_sem=recvs.at[k],
        device_id=to, device_id_type=MESH) for k, (src, mine, to) in enumerate(plan)]


def _ring_hop3(bufs, sends, recvs, waiting):
    (land,) = bufs
    x, y, c = _place()
    return [pltpu.make_async_remote_copy(
        src_ref=land.at[_slot(1 - x, 1 - y, c)], dst_ref=land.at[_slot(1 - x, 1 - y, 1 - c if waiting else c)],
        send_sem=sends.at[0], recv_sem=recvs.at[0], device_id=(x, y, 1 - c), device_id_type=MESH)]


def _pair_hop(bufs, sends, recvs, waiting):
    g, land = bufs
    x, y, c = _place()
    return [pltpu.make_async_remote_copy(
        src_ref=g.at[_slot(qx, qy, 1 - c)], dst_ref=land.at[q],
        send_sem=sends.at[q], recv_sem=recvs.at[q], device_id=(x, y, 1 - c), device_id_type=MESH)
        for q, (qx, qy) in enumerate(CHIPS)]


def _chip_hop(bufs, sends, recvs, waiting):
    p, land = bufs
    x, y, c = _place()
    return [pltpu.make_async_remote_copy(
        src_ref=p.at[_chip(px, py)], dst_ref=land.at[_chip(px, py) if waiting else _chip(x, y)],
        send_sem=sends.at[k], recv_sem=recvs.at[k], device_id=(px, py, c), device_id_type=MESH)
        for k, (px, py) in enumerate([(1 - x, y), (x, 1 - y), (1 - x, 1 - y)])]


def copies_start(name, groups, hop, n_copies, deps=()):
    flat = [b for grp in groups for b in grp]
    nb, ng = len(flat), len(groups)
    deps = list(deps)

    def body(*refs):
        ins, token = refs[:nb], refs[-1]
        sems = refs[nb + len(deps):nb + len(deps) + 2 * ng]
        i = 0
        for gi, grp in enumerate(groups):
            for cp in hop(ins[i:i + len(grp)], sems[2 * gi], sems[2 * gi + 1], False):
                cp.start()
            i += len(grp)
        token[...] = jnp.zeros_like(token)

    outs = pl.pallas_call(
        body, name=name,
        out_shape=([pltpu.SemaphoreType.DMA((n_copies,))] * (2 * ng) + [pltpu.HBM(b.shape, b.dtype) for b in flat]
                   + [TOKEN]),
        in_specs=[HBM] * nb + [ANY] * len(deps),
        out_specs=[SEM] * (2 * ng) + [HBM] * nb + [pl.BlockSpec(memory_space=pltpu.VMEM)],
        input_output_aliases={i: 2 * ng + i for i in range(nb)},
        compiler_params=pltpu.CompilerParams(has_side_effects=EFFECT),
    )(*[_in_hbm(b) for b in flat], *deps)
    started, i = [], 0
    for gi, grp in enumerate(groups):
        started.append((outs[2 * gi], outs[2 * gi + 1], list(outs[2 * ng + i:2 * ng + i + len(grp)])))
        i += len(grp)
    return started, outs[-1]


def copies_wait(name, started, hop, after):
    sends, recvs, bufs = started
    nb = len(bufs)

    def body(*refs):
        for cp in hop(refs[:nb], refs[nb], refs[nb + 1], True):
            cp.wait_send()
            cp.wait_recv()

    outs = pl.pallas_call(
        body, name=name,
        out_shape=[pltpu.HBM(b.shape, b.dtype) for b in bufs],
        in_specs=[HBM] * nb + [SEM, SEM, ANY], out_specs=[HBM] * nb,
        input_output_aliases={i: i for i in range(nb)},
        compiler_params=pltpu.CompilerParams(has_side_effects=EFFECT),
    )(*bufs, sends, recvs, after)
    return list(outs)


def place_shard(name, w, dtype):
    r, c = w.shape
    tr = _tile(r, 256)
    x, y, core = _place()
    me = _slot(x, y, core).astype(jnp.int32).reshape(1)

    def body(me_ref, w_ref, o_ref):
        o_ref[...] = w_ref[...].astype(dtype)

    return pl.pallas_call(
        body, name=name,
        grid_spec=pltpu.PrefetchScalarGridSpec(
            num_scalar_prefetch=1, grid=(r // tr,),
            in_specs=[pl.BlockSpec((tr, c), lambda i, me_ref: (i, 0))],
            out_specs=pl.BlockSpec((None, tr, c), lambda i, me_ref: (me_ref[0], i, 0))),
        out_shape=jax.ShapeDtypeStruct((N_DEV, r, c), dtype),
        compiler_params=_params(("parallel",)),
    )(me, w)


def tie(name, x, *deps):
    def body(*refs):
        del refs

    return pl.pallas_call(
        body, name=name, out_shape=jax.ShapeDtypeStruct(x.shape, x.dtype),
        in_specs=[ANY] * (1 + len(deps)), out_specs=ANY, input_output_aliases={0: 0},
    )(x, *deps)


def pair_add(name, g, from_sibling):
    _, r, c_dim = g.shape
    tr = r
    while tr * c_dim > PAIR_ADD_BLOCK and tr % 16 == 0:
        tr //= 2
    x, y, core = _place()
    where = jnp.stack([core, _chip(x, y)]).astype(jnp.int32)

    def body(where_ref, g_ref, s_ref, o_ref, zone_ref):
        total = (g_ref[...].astype(F32) + s_ref[...].astype(F32)).astype(o_ref.dtype)
        o_ref[...] = total

        @pl.when(pl.program_id(1) == where_ref[1])
        def _():
            zone_ref[...] = total

    blk = pl.BlockSpec((None, tr, c_dim), lambda i, q, where_ref: (q, i, 0))
    return pl.pallas_call(
        body, name=name,
        grid_spec=pltpu.PrefetchScalarGridSpec(
            num_scalar_prefetch=1, grid=(r // tr, N_CHIP),
            in_specs=[pl.BlockSpec((None, None, tr, c_dim), lambda i, q, where_ref: (q, where_ref[0], i, 0)), blk],
            out_specs=[blk, pl.BlockSpec((None, tr, c_dim), lambda i, q, where_ref: (where_ref[1], i, 0))]),
        out_shape=[jax.ShapeDtypeStruct((N_CHIP, r, c_dim), g.dtype)] * 2,
        compiler_params=_params(("parallel", "arbitrary")),
    )(where, g.reshape(N_CHIP, 2, r, c_dim), from_sibling)


def _matmul(name, lhs, rhs, *, out_shape, out_dtype, grid, lhs_spec, rhs_spec, out_spec, dims, acc_shape,
            lhs_fn=None, extra=(), extra_specs=(), epilogue=None, parts=1):
    nk = grid[2]
    n_extra = len(extra)

    def body(*refs):
        lhs_ref, rhs_ref = refs[0], refs[1]
        extra_refs = refs[2:2 + n_extra]
        out_ref = refs[2 + n_extra]

        def product():
            if parts == 1:
                a = lhs_ref[...]
                if lhs_fn is not None:
                    a = lhs_fn(a)
                return lax.dot_general(a, rhs_ref[...], dims, preferred_element_type=F32)
            width = lhs_ref.shape[1] // parts
            total = None
            for b in range(parts):
                term = lax.dot_general(lhs_ref[:, b * width:(b + 1) * width], rhs_ref[b], dims,
                                       preferred_element_type=F32)
                total = term if total is None else total + term
            return total

        def finish(r):
            if epilogue is not None:
                r = epilogue(r, *[e[...] for e in extra_refs])
            out_ref[...] = r.astype(out_dtype)

        if nk == 1:
            finish(product())
        else:
            acc_ref = refs[3 + n_extra]
            k = pl.program_id(2)

            @pl.when(k == 0)
            def _():
                acc_ref[...] = product()

            @pl.when(jnp.logical_and(k > 0, k < nk - 1))
            def _():
                acc_ref[...] += product()

            @pl.when(k == nk - 1)
            def _():
                finish(acc_ref[...] + product())

    return pl.pallas_call(
        body, name=name, grid=grid,
        out_shape=jax.ShapeDtypeStruct(out_shape, out_dtype),
        in_specs=[lhs_spec, rhs_spec, *extra_specs], out_specs=out_spec,
        scratch_shapes=[pltpu.VMEM(acc_shape, F32)] if nk > 1 else [],
        compiler_params=_params(("parallel", "parallel", "arbitrary")),
    )(lhs, rhs, *extra)


def _tile(n, want):
    return want if n % want == 0 else n


def mm_nn(name, x, w, *, out_dtype, tn=512, tk=None, lhs_fn=None, epilogue=None):
    t, kdim = x.shape
    n = w.shape[1]
    tn = _tile(n, tn)
    tk = kdim if tk is None else _tile(kdim, tk)
    return _matmul(
        name, x, w, out_shape=(t, n), out_dtype=out_dtype, grid=(1, n // tn, kdim // tk),
        lhs_spec=pl.BlockSpec((t, tk), lambda i, j, k: (i, k)),
        rhs_spec=pl.BlockSpec((tk, tn), lambda i, j, k: (k, j)),
        out_spec=pl.BlockSpec((t, tn), lambda i, j, k: (i, j)),
        dims=NN, acc_shape=(t, tn), lhs_fn=lhs_fn, epilogue=epilogue)


def mm_nn_blocked(name, x, w, *, out_dtype, epilogue=None):
    t, kdim = x.shape
    nb = w.shape[2]
    tn = nb // 2 if nb >= 1024 else nb
    sub = nb // tn
    return _matmul(
        name, x, w, out_shape=(t, N_DEV * nb), out_dtype=out_dtype, grid=(1, N_DEV * sub, 1),
        lhs_spec=pl.BlockSpec((t, kdim), lambda i, j, k: (i, k)),
        rhs_spec=pl.BlockSpec((None, kdim, tn), lambda i, j, k: (j // sub, k, j % sub)),
        out_spec=pl.BlockSpec((t, tn), lambda i, j, k: (i, j)),
        dims=NN, acc_shape=(t, tn), epilogue=epilogue)


def mm_nt(name, dy, w, *, out_dtype, tn=512, extra=None, epilogue=None):
    t, n = dy.shape
    kdim = w.shape[0]
    tn = _tile(kdim, tn)
    extra_arrs = () if extra is None else (extra,)
    extra_specs = () if extra is None else (pl.BlockSpec((t, tn), lambda i, j, k: (i, j)),)
    return _matmul(
        name, dy, w, out_shape=(t, kdim), out_dtype=out_dtype, grid=(1, kdim // tn, 1),
        lhs_spec=pl.BlockSpec((t, n), lambda i, j, k: (i, k)),
        rhs_spec=pl.BlockSpec((tn, n), lambda i, j, k: (j, k)),
        out_spec=pl.BlockSpec((t, tn), lambda i, j, k: (i, j)),
        dims=NT, acc_shape=(t, tn), extra=extra_arrs, extra_specs=extra_specs, epilogue=epilogue)


def mm_nt_blocked(name, dz, w, *, out_dtype, tn=512):
    t = dz.shape[0]
    kdim, nb = w.shape[1], w.shape[2]
    tn = _tile(kdim, tn)
    parts = 2
    return _matmul(
        name, dz, w, out_shape=(t, kdim), out_dtype=out_dtype, grid=(1, kdim // tn, N_DEV // parts),
        lhs_spec=pl.BlockSpec((t, parts * nb), lambda i, j, k: (i, k)),
        rhs_spec=pl.BlockSpec((parts, tn, nb), lambda i, j, k: (k, j, 0)),
        out_spec=pl.BlockSpec((t, tn), lambda i, j, k: (i, j)),
        dims=NT, acc_shape=(t, tn), parts=parts)


def mm_tn(name, x, dy, *, out_dtype, tk=1024, tn=1024, lhs_fn=None):
    t, kdim = x.shape
    n = dy.shape[1]
    tk, tn = _tile(kdim, tk), _tile(n, tn)
    return _matmul(
        name, x, dy, out_shape=(kdim, n), out_dtype=out_dtype, grid=(kdim // tk, n // tn, 1),
        lhs_spec=pl.BlockSpec((t, tk), lambda i, j, k: (k, i)),
        rhs_spec=pl.BlockSpec((t, tn), lambda i, j, k: (k, j)),
        out_spec=pl.BlockSpec((tk, tn), lambda i, j, k: (i, j)),
        dims=TN, acc_shape=(tk, tn), lhs_fn=lhs_fn)


def mm_tn_blocked(name, x, dz, nb, *, out_dtype, tk=1024):
    t, kdim = x.shape
    tk = _tile(kdim, tk)
    return _matmul(
        name, x, dz, out_shape=(N_DEV, kdim, nb), out_dtype=out_dtype, grid=(kdim // tk, N_DEV, 1),
        lhs_spec=pl.BlockSpec((t, tk), lambda i, j, k: (k, i)),
        rhs_spec=pl.BlockSpec((t, nb), lambda i, j, k: (k, j)),
        out_spec=pl.BlockSpec((None, tk, nb), lambda i, j, k: (j, i, 0)),
        dims=TN, acc_shape=(tk, nb))


def _rstd(v):
    return lax.rsqrt(jnp.mean(v * v, axis=-1, keepdims=True) + NORM_EPS)


def _rms_bwd(v, g, dy):
    r = _rstd(v)
    vhat = v * r
    dvh = dy * g
    dv = r * (dvh - vhat * jnp.mean(dvh * vhat, axis=-1, keepdims=True))
    return dv, dy * vhat


def _fold8(v):
    rows, n = v.shape
    return jnp.sum(v.reshape(rows // 8, 8, n), axis=0)


def _fold_lanes(v):
    out = v[:, 0:128]
    for i in range(1, v.shape[1] // 128):
        out = out + v[:, 128 * i:128 * (i + 1)]
    return out


def _accumulate(ref, v):
    i = pl.program_id(0)

    @pl.when(i == 0)
    def _():
        ref[...] = v

    @pl.when(i > 0)
    def _():
        ref[...] += v


def _row_call(body, name, t, ins, row_in, outs, acc_outs=(), tr=ROW_TILE):
    tr = _tile(t, tr)
    in_specs = [pl.BlockSpec((tr, a.shape[1]), lambda i: (i, 0)) if tiled
                else pl.BlockSpec(a.shape, lambda i: (0, 0)) for a, tiled in zip(ins, row_in)]
    out_specs = [pl.BlockSpec((tr, n), lambda i: (i, 0)) for n, _ in outs]
    out_specs += [pl.BlockSpec((8, n), lambda i: (0, 0)) for n in acc_outs]
    out_shape = [jax.ShapeDtypeStruct((t, n), dt) for n, dt in outs]
    out_shape += [jax.ShapeDtypeStruct((8, n), F32) for n in acc_outs]
    return pl.pallas_call(
        body, name=name, grid=(t // tr,), in_specs=in_specs, out_specs=out_specs, out_shape=out_shape,
        compiler_params=_params(("arbitrary",) if acc_outs else ("parallel",)),
    )(*ins)


def norm_pre(name, x, g):
    t, d = x.shape

    def body(x_ref, g_ref, h_ref):
        v = x_ref[...]
        h_ref[...] = (v * _rstd(v) * g_ref[...]).astype(BF16)

    return _row_call(body, name, t, [x, g], [True, False], [(d, BF16)])[0]


def post_pre(name, x, m, g_post, g_pre):
    t, d = x.shape

    def body(x_ref, m_ref, gp_ref, gn_ref, xo_ref, h_ref):
        mv = m_ref[...]
        xn = x_ref[...] + mv * _rstd(mv) * gp_ref[...]
        xo_ref[...] = xn
        h_ref[...] = (xn * _rstd(xn) * gn_ref[...]).astype(BF16)

    return _row_call(body, name, t, [x, m, g_post, g_pre], [True, True, False, False], [(d, F32), (d, BF16)])


def post_loss(name, x, f, g_post, target):
    t, d = x.shape

    def body(x_ref, f_ref, g_ref, t_ref, dx_ref, df_ref, loss_ref, dg_ref):
        fv = f_ref[...]
        g = g_ref[...]
        out = x_ref[...] + fv * _rstd(fv) * g
        err = out - t_ref[...]
        dx = err * (1.0 / d)
        dx_ref[...] = dx
        dfv, dg_rows = _rms_bwd(fv, g, dx)
        df_ref[...] = dfv.astype(BF16)
        _accumulate(loss_ref, _fold8(_fold_lanes(err * err)))
        _accumulate(dg_ref, _fold8(dg_rows))

    return _row_call(body, name, t, [x, f, g_post, target], [True, True, False, True],
                     [(d, F32), (d, BF16)], acc_outs=(128, d))


def bwd_pre_post(name, dx_out, x_in, g_pre, dh, f_prev, g_post_prev):
    t, d = x_in.shape

    def body(dxo_ref, x_ref, gpre_ref, dh_ref, f_ref, gpost_ref, dxi_ref, df_ref, dgpre_ref, dgpost_ref):
        dxv, dgpre_rows = _rms_bwd(x_ref[...], gpre_ref[...], dh_ref[...].astype(F32))
        dxi = dxo_ref[...] + dxv
        dxi_ref[...] = dxi
        dfv, dgpost_rows = _rms_bwd(f_ref[...], gpost_ref[...], dxi)
        df_ref[...] = dfv.astype(BF16)
        _accumulate(dgpre_ref, _fold8(dgpre_rows))
        _accumulate(dgpost_ref, _fold8(dgpost_rows))

    return _row_call(body, name, t, [dx_out, x_in, g_pre, dh, f_prev, g_post_prev],
                     [True, True, False, True, True, False], [(d, F32), (d, BF16)], acc_outs=(d, d))


def bwd_pre_final(name, dx_out, x_in, g_pre, dh):
    t, d = x_in.shape

    def body(dxo_ref, x_ref, gpre_ref, dh_ref, dxi_ref, dgpre_ref):
        dxv, dgpre_rows = _rms_bwd(x_ref[...], gpre_ref[...], dh_ref[...].astype(F32))
        dxi_ref[...] = dxo_ref[...] + dxv
        _accumulate(dgpre_ref, _fold8(dgpre_rows))

    return _row_call(body, name, t, [dx_out, x_in, g_pre, dh], [True, True, False, True], [(d, F32)], acc_outs=(d,))


def _layer_norm_parts(cv):
    mu = jnp.mean(cv, axis=-1, keepdims=True)
    xc = cv - mu
    rstd = lax.rsqrt(jnp.mean(xc * xc, axis=-1, keepdims=True) + NORM_EPS)
    return xc * rstd, rstd


def ln_silu(name, cv, g, b):
    t, n = cv.shape

    def body(c_ref, g_ref, b_ref, y_ref):
        chat, _ = _layer_norm_parts(c_ref[...])
        ln = chat * g_ref[...] + b_ref[...]
        y_ref[...] = (ln * jax.nn.sigmoid(ln)).astype(BF16)

    return _row_call(body, name, t, [cv, g, b], [True, False, False], [(n, BF16)])[0]


def ln_silu_bwd(name, cv, g, b, dy):
    t, n = cv.shape

    def body(c_ref, g_ref, b_ref, dy_ref, dc_ref, dg_ref, db_ref):
        chat, rstd = _layer_norm_parts(c_ref[...])
        g = g_ref[...]
        ln = chat * g + b_ref[...]
        s = jax.nn.sigmoid(ln)
        dln = dy_ref[...].astype(F32) * (s * (1.0 + ln * (1.0 - s)))
        dchat = dln * g
        dc_ref[...] = rstd * (dchat - jnp.mean(dchat, axis=-1, keepdims=True)
                              - chat * jnp.mean(dchat * chat, axis=-1, keepdims=True))
        _accumulate(dg_ref, _fold8(dln * chat))
        _accumulate(db_ref, _fold8(dln))

    return _row_call(body, name, t, [cv, g, b, dy], [True, False, False, True], [(n, F32)], acc_outs=(n, n))


def _chunks(t, fn, tc=TIME_CHUNK):
    tc = _tile(t, tc)

    def step(i, carry):
        fn(pl.multiple_of(i * tc, tc), tc)
        return carry

    lax.fori_loop(0, t // tc, step, 0)


def _shifted(window, offsets, tc):
    by_residue = {}
    for k, off in enumerate(offsets):
        by_residue.setdefault(off % 8, []).append((k, off))
    for res, taps in by_residue.items():
        base = window[res:res + tc + max(off for _, off in taps) - res, :]
        for k, off in taps:
            yield k, base[off - res:off - res + tc, :]


def _taps(window, w_ref, offsets, tc, flip=False):
    acc = None
    for k, rows in _shifted(window, offsets, tc):
        kk = len(offsets) - 1 - k if flip else k
        term = w_ref[kk:kk + 1, :] * rows
        acc = term if acc is None else acc + term
    return acc


def _window_sums(win, tc, causal):
    sums = []
    cur, rows, step = win, tc + HALO, 1
    for _ in POOL_WINDOWS:
        rows -= 8
        if causal:
            cur = cur[8:8 + rows, :] + cur[8 - step:8 - step + rows, :]
            sums.append(cur[rows - tc:rows, :])
        else:
            cur = cur[0:rows, :] + cur[step:step + rows, :]
            sums.append(cur[0:tc, :])
        step *= 2
    return sums


def _pick(vals, g):
    out = vals[-1]
    for i in range(len(vals) - 2, -1, -1):
        out = jnp.where(g == i, vals[i], out)
    return out


def _pool_count(s, tc, g):
    t1 = (lax.broadcasted_iota(jnp.int32, (tc, 1), 0) + (s + 1)).astype(F32)
    width = _pick([float(w) for w in POOL_WINDOWS], g)
    return jnp.minimum(t1, width)


def pool_fwd(name, z, pool_w, pool_scale, d_pool):
    t = z.shape[0]
    ng, pg = pool_w.shape[0], pool_w.shape[1]

    def body(u_ref, w_ref, s_ref, pooled_ref, y_ref, pad):
        g = pl.program_id(0)
        pad[pl.ds(0, HALO), :] = jnp.zeros((HALO, pg), F32)

        def fill(s, tc):
            pad[pl.ds(HALO + s, tc), :] = u_ref[pl.ds(s, tc), :].astype(F32)

        def chunk(s, tc):
            win = pad[pl.ds(s, tc + HALO), :]
            total = _pick(_window_sums(win, tc, causal=True), g)
            pooled = total / _pool_count(s, tc, g) - win[HALO:HALO + tc, :]
            pooled_ref[pl.ds(s, tc), :] = pooled.astype(BF16)

        _chunks(t, fill)
        _chunks(t, chunk)
        mixed = jnp.dot(pooled_ref[...], w_ref[...], preferred_element_type=F32)
        y_ref[...] = (mixed * s_ref[...]).astype(BF16)

    col = pl.BlockSpec((t, pg), lambda g: (0, g))
    return pl.pallas_call(
        body, name=name, grid=(ng,),
        in_specs=[col, pl.BlockSpec((None, pg, pg), lambda g: (g, 0, 0)), pl.BlockSpec((1, pg), lambda g: (0, g))],
        out_specs=[col, col],
        out_shape=[jax.ShapeDtypeStruct((t, d_pool), BF16), jax.ShapeDtypeStruct((t, d_pool), BF16)],
        scratch_shapes=[pltpu.VMEM((t + HALO, pg), F32)],
        compiler_params=_params(("parallel",)),
    )(z, pool_w, pool_scale)


def pool_bwd(name, pooled, dy, pool_w, pool_scale):
    t, d_pool = pooled.shape
    ng, pg = pool_w.shape[0], pool_w.shape[1]

    def body(p_ref, dy_ref, w_ref, s_ref, du_ref, dw_ref, ds_ref, pad):
        g = pl.program_id(0)
        w = w_ref[...]
        dyv = dy_ref[...].astype(F32)
        mixed = jnp.dot(p_ref[...], w, preferred_element_type=F32)
        ds_ref[...] = jnp.sum(dyv * mixed, axis=0, keepdims=True)
        dmixed = (dyv * s_ref[...]).astype(BF16)
        dw_ref[...] = lax.dot_general(p_ref[...], dmixed, TN, preferred_element_type=F32)
        pad[...] = jnp.zeros((t + HALO, pg), F32)
        pad[pl.ds(0, t), :] = lax.dot_general(dmixed, w, NT, preferred_element_type=F32)

        def scale(s, tc):
            pad[pl.ds(s, tc), :] = pad[pl.ds(s, tc), :] / _pool_count(s, tc, g)

        def chunk(s, tc):
            win = pad[pl.ds(s, tc + HALO), :]
            total = _pick(_window_sums(win, tc, causal=False), g)
            du_ref[pl.ds(s, tc), :] = (total - win[0:tc, :] * _pool_count(s, tc, g)).astype(BF16)

        _chunks(t, scale)
        _chunks(t, chunk)

    col = pl.BlockSpec((t, pg), lambda g: (0, g))
    vec = pl.BlockSpec((1, pg), lambda g: (0, g))
    mat = pl.BlockSpec((None, pg, pg), lambda g: (g, 0, 0))
    return pl.pallas_call(
        body, name=name, grid=(ng,),
        in_specs=[col, col, mat, vec], out_specs=[col, mat, vec],
        out_shape=[jax.ShapeDtypeStruct((t, d_pool), BF16), jax.ShapeDtypeStruct((ng, pg, pg), F32),
                   jax.ShapeDtypeStruct((1, d_pool), F32)],
        scratch_shapes=[pltpu.VMEM((t + HALO, pg), F32)],
        compiler_params=_params(("parallel",)),
    )(pooled, dy, pool_w, pool_scale)


def conv_fwd(name, z, conv_w, conv_b, d_pool, d_conv):
    t = z.shape[0]
    kw = conv_w.shape[0]
    tc_ch = _tile(d_conv, CHANNEL_TILE)
    v0, g0 = d_pool // tc_ch, (d_pool + d_conv) // tc_ch

    def body(v_ref, g_ref, w_ref, b_ref, c_ref, pad):
        pad[pl.ds(0, HALO), :] = jnp.zeros((HALO, tc_ch), F32)

        def fill(s, tc):
            pad[pl.ds(HALO + s, tc), :] = v_ref[pl.ds(s, tc), :].astype(F32) * jax.nn.sigmoid(g_ref[pl.ds(s, tc), :].astype(F32))

        def chunk(s, tc):
            win = pad[pl.ds(s, tc + HALO), :]
            c_ref[pl.ds(s, tc), :] = _taps(win, w_ref, [HALO - (kw - 1) + k for k in range(kw)], tc) + b_ref[...]

        _chunks(t, fill)
        _chunks(t, chunk)

    return pl.pallas_call(
        body, name=name, grid=(d_conv // tc_ch,),
        in_specs=[pl.BlockSpec((t, tc_ch), lambda j: (0, v0 + j)), pl.BlockSpec((t, tc_ch), lambda j: (0, g0 + j)),
                  pl.BlockSpec((kw, tc_ch), lambda j: (0, j)), pl.BlockSpec((1, tc_ch), lambda j: (0, j))],
        out_specs=pl.BlockSpec((t, tc_ch), lambda j: (0, j)),
        out_shape=jax.ShapeDtypeStruct((t, d_conv), F32),
        scratch_shapes=[pltpu.VMEM((t + HALO, tc_ch), F32)],
        compiler_params=_params(("parallel",)),
    )(z, z, conv_w, conv_b)


def conv_bwd(name, z, dc, conv_w, d_pool, d_conv):
    t = z.shape[0]
    kw = conv_w.shape[0]
    tc_ch = _tile(d_conv, CHANNEL_TILE)
    v0, g0 = d_pool // tc_ch, (d_pool + d_conv) // tc_ch

    def body(v_ref, g_ref, dc_ref, w_ref, dv_ref, dg_ref, dw_ref, db_ref, pad_a, pad_dc, acc_w, acc_b):
        pad_a[pl.ds(0, HALO), :] = jnp.zeros((HALO, tc_ch), F32)
        pad_dc[pl.ds(t, HALO), :] = jnp.zeros((HALO, tc_ch), F32)
        acc_w[...] = jnp.zeros_like(acc_w)
        acc_b[...] = jnp.zeros_like(acc_b)

        def fill(s, tc):
            pad_a[pl.ds(HALO + s, tc), :] = v_ref[pl.ds(s, tc), :].astype(F32) * jax.nn.sigmoid(g_ref[pl.ds(s, tc), :].astype(F32))
            pad_dc[pl.ds(s, tc), :] = dc_ref[pl.ds(s, tc), :]

        def chunk(s, tc):
            dcv = pad_dc[pl.ds(s, tc), :]
            win_a = pad_a[pl.ds(s, tc + HALO), :]
            for k, rows in _shifted(win_a, [HALO - (kw - 1) + k for k in range(kw)], tc):
                acc_w[pl.ds(8 * k, 8), :] += _fold8(dcv * rows)
            acc_b[...] += _fold8(dcv)
            da = _taps(pad_dc[pl.ds(s, tc + HALO), :], w_ref, list(range(kw)), tc, flip=True)
            vv = v_ref[pl.ds(s, tc), :].astype(F32)
            sg = jax.nn.sigmoid(g_ref[pl.ds(s, tc), :].astype(F32))
            dv_ref[pl.ds(s, tc), :] = (da * sg).astype(BF16)
            dg_ref[pl.ds(s, tc), :] = (da * vv * sg * (1.0 - sg)).astype(BF16)

        _chunks(t, fill)
        _chunks(t, chunk)
        for k in range(kw):
            dw_ref[k:k + 1, :] = jnp.sum(acc_w[pl.ds(8 * k, 8), :], axis=0, keepdims=True)
        db_ref[...] = jnp.sum(acc_b[...], axis=0, keepdims=True)

    col = pl.BlockSpec((t, tc_ch), lambda j: (0, j))
    return pl.pallas_call(
        body, name=name, grid=(d_conv // tc_ch,),
        in_specs=[pl.BlockSpec((t, tc_ch), lambda j: (0, v0 + j)), pl.BlockSpec((t, tc_ch), lambda j: (0, g0 + j)),
                  col, pl.BlockSpec((kw, tc_ch), lambda j: (0, j))],
        out_specs=[col, col, pl.BlockSpec((kw, tc_ch), lambda j: (0, j)), pl.BlockSpec((1, tc_ch), lambda j: (0, j))],
        out_shape=[jax.ShapeDtypeStruct((t, d_conv), BF16), jax.ShapeDtypeStruct((t, d_conv), BF16),
                   jax.ShapeDtypeStruct((kw, d_conv), F32), jax.ShapeDtypeStruct((1, d_conv), F32)],
        scratch_shapes=[pltpu.VMEM((t + HALO, tc_ch), F32), pltpu.VMEM((t + HALO, tc_ch), F32),
                        pltpu.VMEM((8 * kw, tc_ch), F32), pltpu.VMEM((8, tc_ch), F32)],
        compiler_params=_params(("parallel",)),
    )(z, z, dc, conv_w)


def short_fwd(name, z, conv_w, d_short):
    t = z.shape[0]
    kw = conv_w.shape[0]
    tc_ch = _tile(d_short, CHANNEL_TILE)
    nt = d_short // tc_ch

    def body(b_ref, c_ref, u_ref, w_ref, y_ref, pad):
        pad[pl.ds(0, HALO), :] = jnp.zeros((HALO, tc_ch), F32)

        def fill(s, tc):
            pad[pl.ds(HALO + s, tc), :] = c_ref[pl.ds(s, tc), :].astype(F32) * u_ref[pl.ds(s, tc), :].astype(F32)

        def chunk(s, tc):
            win = pad[pl.ds(s, tc + HALO), :]
            cq = _taps(win, w_ref, [HALO - (kw - 1) + k for k in range(kw)], tc)
            y_ref[pl.ds(s, tc), :] = (b_ref[pl.ds(s, tc), :].astype(F32) * cq).astype(BF16)

        _chunks(t, fill)
        _chunks(t, chunk)

    return pl.pallas_call(
        body, name=name, grid=(nt,),
        in_specs=[pl.BlockSpec((t, tc_ch), lambda j: (0, j)), pl.BlockSpec((t, tc_ch), lambda j: (0, nt + j)),
                  pl.BlockSpec((t, tc_ch), lambda j: (0, 2 * nt + j)), pl.BlockSpec((kw, tc_ch), lambda j: (0, j))],
        out_specs=pl.BlockSpec((t, tc_ch), lambda j: (0, j)),
        out_shape=jax.ShapeDtypeStruct((t, d_short), BF16),
        scratch_shapes=[pltpu.VMEM((t + HALO, tc_ch), F32)],
        compiler_params=_params(("parallel",)),
    )(z, z, z, conv_w)


def short_bwd(name, z, dy, conv_w, d_short):
    t = z.shape[0]
    kw = conv_w.shape[0]
    tc_ch = _tile(d_short, CHANNEL_TILE)
    nt = d_short // tc_ch

    def body(b_ref, c_ref, u_ref, dy_ref, w_ref, db_ref, dcg_ref, du_ref, dw_ref, pad_q, pad_dcq, acc_w):
        pad_q[pl.ds(0, HALO), :] = jnp.zeros((HALO, tc_ch), F32)
        pad_dcq[pl.ds(t, HALO), :] = jnp.zeros((HALO, tc_ch), F32)
        acc_w[...] = jnp.zeros_like(acc_w)

        def fill(s, tc):
            rows = pl.ds(s, tc)
            pad_q[pl.ds(HALO + s, tc), :] = c_ref[rows, :].astype(F32) * u_ref[rows, :].astype(F32)
            pad_dcq[rows, :] = dy_ref[rows, :].astype(F32) * b_ref[rows, :].astype(F32)

        def chunk(s, tc):
            rows = pl.ds(s, tc)
            win_q = pad_q[pl.ds(s, tc + HALO), :]
            dcq = pad_dcq[rows, :]
            cq = None
            for k in range(kw):
                off = HALO - (kw - 1) + k
                shifted = win_q[off:off + tc, :]
                acc_w[pl.ds(8 * k, 8), :] += _fold8(dcq * shifted)
                term = w_ref[k:k + 1, :] * shifted
                cq = term if cq is None else cq + term
            db_ref[rows, :] = (dy_ref[rows, :].astype(F32) * cq).astype(BF16)
            win_d = pad_dcq[pl.ds(s, tc + HALO), :]
            dq = None
            for j in range(kw):
                term = w_ref[kw - 1 - j:kw - j, :] * win_d[j:j + tc, :]
                dq = term if dq is None else dq + term
            dcg_ref[rows, :] = (dq * u_ref[rows, :].astype(F32)).astype(BF16)
            du_ref[rows, :] = (dq * c_ref[rows, :].astype(F32)).astype(BF16)

        _chunks(t, fill)
        _chunks(t, chunk)
        for k in range(kw):
            dw_ref[k:k + 1, :] = jnp.sum(acc_w[pl.ds(8 * k, 8), :], axis=0, keepdims=True)

    col = pl.BlockSpec((t, tc_ch), lambda j: (0, j))
    zspec = [pl.BlockSpec((t, tc_ch), lambda j, o=o: (0, o * nt + j)) for o in range(3)]
    return pl.pallas_call(
        body, name=name, grid=(nt,),
        in_specs=[*zspec, col, pl.BlockSpec((kw, tc_ch), lambda j: (0, j))],
        out_specs=[col, col, col, pl.BlockSpec((kw, tc_ch), lambda j: (0, j))],
        out_shape=[jax.ShapeDtypeStruct((t, d_short), BF16)] * 3 + [jax.ShapeDtypeStruct((kw, d_short), F32)],
        scratch_shapes=[pltpu.VMEM((t + HALO, tc_ch), F32), pltpu.VMEM((t + HALO, tc_ch), F32),
                        pltpu.VMEM((8 * kw, tc_ch), F32)],
        compiler_params=_params(("parallel",)),
    )(z, z, z, dy, conv_w)


def adamw(name, w, m, v, contributions):
    r, c = w.shape
    nc = len(contributions)
    n_slots = contributions[0].shape[0]
    tr = 256 if c <= 1024 else 128
    if any(a.shape[1] % tr for a in contributions):
        assert nc == 1
        tr = r
    tiles = [a.shape[1] // tr for a in contributions]
    first = [sum(tiles[:j]) for j in range(nc)]

    def body(w_ref, m_ref, v_ref, *rest):
        g_refs, (grad_ref, delta_ref, nm_ref, nv_ref) = rest[:nc], rest[nc:]
        i = pl.program_id(0)
        g = None
        for j, g_ref in enumerate(g_refs):
            s = g_ref[0].astype(F32)
            for slot in range(1, n_slots):
                s = s + g_ref[slot].astype(F32)
            g = s if g is None else jnp.where(i >= first[j], s, g)
        nm = ADAM_B1 * m_ref[...] + (1.0 - ADAM_B1) * g
        nv = ADAM_B2 * v_ref[...] + (1.0 - ADAM_B2) * (g * g)
        m_hat = nm / (1.0 - ADAM_B1 ** ADAM_STEP)
        v_hat = nv / (1.0 - ADAM_B2 ** ADAM_STEP)
        grad_ref[...] = g
        delta_ref[...] = -ADAM_LR * (m_hat / (jnp.sqrt(v_hat) + ADAM_EPS) + ADAM_WD * w_ref[...])
        nm_ref[...] = nm
        nv_ref[...] = nv

    blk = pl.BlockSpec((tr, c), lambda i: (i, 0))
    g_specs = [pl.BlockSpec((n_slots, tr, c), lambda i, j=j: (0, jnp.clip(i - first[j], 0, tiles[j] - 1), 0))
               for j in range(nc)]
    return pl.pallas_call(
        body, name=name, grid=(r // tr,),
        in_specs=[blk, blk, blk, *g_specs],
        out_specs=[blk] * 4, out_shape=[jax.ShapeDtypeStruct((r, c), F32)] * 4,
        compiler_params=_params(("parallel",)),
    )(w, m, v, *contributions)


def _pad_rows(a, rows):
    return jnp.pad(a, ((0, rows - a.shape[0]), (0, 0)))


def kernel(x, mix_pre_g, mix_post_g, ffn_pre_g, ffn_post_g, ab_w_in, pool_w, pool_scale, conv_w, conv_b, conv_ln_g, conv_ln_b, ab_w_out, sc_w_in, sc_conv_w, sc_w_out, ffn_w1, ffn_w2, loss_target, m_mix_pre_g, m_mix_post_g, m_ffn_pre_g, m_ffn_post_g, m_ab_w_in, m_pool_w, m_pool_scale, m_conv_w, m_conv_b, m_conv_ln_g, m_conv_ln_b, m_ab_w_out, m_sc_w_in, m_sc_conv_w, m_sc_w_out, m_ffn_w1, m_ffn_w2, v_mix_pre_g, v_mix_post_g, v_ffn_pre_g, v_ffn_post_g, v_ab_w_in, v_pool_w, v_pool_scale, v_conv_w, v_conv_b, v_conv_ln_g, v_conv_ln_b, v_ab_w_out, v_sc_w_in, v_sc_conv_w, v_sc_w_out, v_ffn_w1, v_ffn_w2):
    t, d = x.shape[1], x.shape[2]
    d_pool = pool_scale.shape[1]
    d_conv = conv_b.shape[1]
    d_short = d
    ng, pg = pool_w.shape[1], pool_w.shape[3]
    kw, ks = conv_w.shape[1], sc_conv_w.shape[1]
    nb_ab, nb_sc, nb_ff = ab_w_in.shape[2], sc_w_in.shape[2], ffn_w1.shape[2]

    xs = x[0]
    target = loss_target[0]

    lanes = min(128, d_conv // N_DEV)
    small_rows = [kw * (d_conv // N_DEV) // lanes, ks * (d_short // N_DEV) // lanes, ng * (pg // N_DEV) * pg // lanes]
    small_total = -(-sum(small_rows) // 8) * 8
    r0, r1, r2 = small_rows[0], small_rows[0] + small_rows[1], sum(small_rows)

    def pack_small(a_conv, a_sconv, a_pool):
        parts = [a_conv[0].reshape(-1, lanes), a_sconv[0].reshape(-1, lanes), a_pool[0].reshape(-1, lanes)]
        return _pad_rows(jnp.concatenate(parts, axis=0), small_total)

    shards = {
        "ab_in": (ab_w_in[0], BF16), "small": (pack_small(conv_w, sc_conv_w, pool_w), F32),
        "ab_out": (ab_w_out[0], BF16), "ff1_0": (ffn_w1[0], BF16), "ff2_0": (ffn_w2[0], BF16),
        "sc_in": (sc_w_in[0], BF16), "sc_out": (sc_w_out[0], BF16),
        "ff1_1": (ffn_w1[1], BF16), "ff2_1": (ffn_w2[1], BF16)}
    zones = {nm: place_shard("place_" + nm, w, dt) for nm, (w, dt) in shards.items()}
    direct = ["ab_in", "small", "ab_out"]
    started, token = copies_start("gather_start", [[zones[nm]] for nm in direct], _first_hop, 4)
    started = dict(zip(direct, started))

    ties = [0]

    def after(v, *deps):
        ties[0] += 1
        return tie(f"tie_{ties[0]}", v, *deps)

    def fetch_begin(nm, dep):
        (zone,) = copies_wait("gather_wait_" + nm, started[nm], _first_hop, dep)
        (hop,), tok = copies_start("forward_start_" + nm, [[zone]], _second_hop, 3)
        return hop, tok

    def fetch_end(nm, hop, dep):
        return copies_wait("forward_wait_" + nm, hop, _second_hop, dep)[0]

    ring = {}

    def ring_first(nm, dep):
        (ring[nm],), tok = copies_start("ring1_start_" + nm, [[zones[nm]]], _ring_hop1, 3, deps=[dep])
        return tok

    def ring_second(nm, dep):
        (zone,) = copies_wait("ring1_wait_" + nm, ring[nm], _ring_hop1, dep)
        (ring[nm],), tok = copies_start("ring2_start_" + nm, [[zone]], _ring_hop2, 4)
        return tok

    def ring_third(nm, dep):
        (zone,) = copies_wait("ring2_wait_" + nm, ring[nm], _ring_hop2, dep)
        (ring[nm],), tok = copies_start("ring3_start_" + nm, [[zone]], _ring_hop3, 1)
        return tok

    def ring_done(nm, dep):
        return copies_wait("ring3_wait_" + nm, ring[nm], _ring_hop3, dep)[0]

    relu = lambda r: jnp.maximum(r, 0.0)
    square = lambda a: a * a
    relu2_bwd = lambda r, a: r * (2.0 * a.astype(F32))

    def row(vec, l):
        return vec[l:l + 1]

    tok = ring_first("ff1_0", token)
    tok = ring_first("ff2_0", tok)
    hop_small, _ = fetch_begin("small", tok)
    hop_ab_in, tok = fetch_begin("ab_in", tok)
    w_small = fetch_end("small", hop_small, tok)
    w_ab_in = fetch_end("ab_in", hop_ab_in, tok)
    w_conv = w_small[:, :r0].reshape(N_DEV, kw, -1).transpose(1, 0, 2).reshape(kw, d_conv)
    w_sconv = w_small[:, r0:r1].reshape(N_DEV, ks, -1).transpose(1, 0, 2).reshape(ks, d_short)
    w_pool = w_small[:, r1:r2].reshape(N_DEV, ng, -1, pg).transpose(1, 0, 2, 3).reshape(ng, pg, pg).astype(BF16)
    h0 = norm_pre("norm_pre", xs, row(mix_pre_g, 0))
    z0 = mm_nn_blocked("ab_in", h0, w_ab_in, out_dtype=BF16)
    hop, tok = fetch_begin("ab_out", z0)
    z0 = after(z0, tok)
    pooled, y_pool = pool_fwd("pool_fwd", z0, w_pool, pool_scale, d_pool)
    cv = conv_fwd("conv_fwd", z0, w_conv, conv_b, d_pool, d_conv)
    y_conv = ln_silu("ln_silu", cv, conv_ln_g, conv_ln_b)
    y0 = jnp.concatenate([y_pool, y_conv], axis=1)
    w_ab_out = fetch_end("ab_out", hop, y0)
    tok = ring_second("ff1_0", w_ab_out)
    tok = ring_first("sc_in", tok)
    y0 = after(y0, tok)
    m0 = mm_nn("ab_out", y0, w_ab_out.reshape(d_pool + d_conv, d), out_dtype=F32)
    x1, h1 = post_pre("post_pre_0", xs, m0, row(mix_post_g, 0), row(ffn_pre_g, 0))
    tok = ring_second("ff2_0", h1)
    tok = ring_first("sc_out", tok)
    tok = ring_third("ff1_0", tok)
    w_ff1_0 = ring_done("ff1_0", tok)
    a0 = mm_nn_blocked("ffn0_up", h1, w_ff1_0, out_dtype=BF16, epilogue=relu)
    tok = ring_second("sc_in", a0)
    tok = ring_first("ff1_1", tok)
    tok = ring_third("ff2_0", tok)
    w_ff2_0 = ring_done("ff2_0", tok).reshape(-1, d)
    f0 = mm_nn("ffn0_down", a0, w_ff2_0, out_dtype=F32, tk=2048, lhs_fn=square)
    tok = ring_second("sc_out", f0)
    tok = ring_first("ff2_1", tok)
    tok = ring_third("sc_in", tok)
    f0 = after(f0, tok)
    x2, h2 = post_pre("post_pre_1", x1, f0, row(ffn_post_g, 0), row(mix_pre_g, 1))
    w_sc_in = ring_done("sc_in", h2)
    z1 = mm_nn_blocked("sc_in", h2, w_sc_in, out_dtype=BF16)
    tok = ring_second("ff1_1", z1)
    tok = ring_third("sc_out", tok)
    z1 = after(z1, tok)
    y1 = short_fwd("short_fwd", z1, w_sconv, d_short)
    w_sc_out = ring_done("sc_out", y1).reshape(d_short, d)
    m1 = mm_nn("sc_out", y1, w_sc_out, out_dtype=F32)
    tok = ring_second("ff2_1", m1)
    m1 = after(m1, tok)
    x3, h3 = post_pre("post_pre_2", x2, m1, row(mix_post_g, 1), row(ffn_pre_g, 1))
    tok = ring_third("ff1_1", h3)
    w_ff1_1 = ring_done("ff1_1", tok)
    a1 = mm_nn_blocked("ffn1_up", h3, w_ff1_1, out_dtype=BF16, epilogue=relu)
    tok = ring_third("ff2_1", a1)
    w_ff2_1 = ring_done("ff2_1", tok).reshape(-1, d)
    f1 = mm_nn("ffn1_down", a1, w_ff2_1, out_dtype=F32, tk=2048, lhs_fn=square)
    dx4, df1, loss_part, dg_ffn_post1 = post_loss("post_loss", x3, f1, row(ffn_post_g, 1), target)
    loss = lax.psum(jnp.sum(loss_part) * (0.5 / d), ("x", "y", "c"))

    def reduce_begin(tag, g):
        zone = lax.empty((N_CHIP,) + g.shape[1:], g.dtype)
        (hop,), tok = copies_start("pair_start_" + tag, [[g, zone]], _pair_hop, N_CHIP)
        return hop, tok

    def reduce_middle(tag, hop, dep):
        g, from_sibling = copies_wait("pair_wait_" + tag, hop, _pair_hop, dep)
        pair_sum, zone = pair_add("pair_add_" + tag, g, from_sibling)
        (hop2,), tok = copies_start("chips_start_" + tag, [[pair_sum, zone]], _chip_hop, 3)
        return hop2, tok

    def reduce_end(tag, hop2, dep):
        return copies_wait("chips_wait_" + tag, hop2, _chip_hop, dep)[1]

    dw = mm_tn("ffn1_dw2", a1, df1, out_dtype=BF16, lhs_fn=square)
    red_ff2_1, tok = reduce_begin("ff2_1", dw.reshape(N_DEV, -1, d))
    df1 = after(df1, tok)
    dpre = mm_nt("ffn1_da", df1, w_ff2_1, out_dtype=BF16, extra=a1, epilogue=relu2_bwd)
    dw = mm_tn_blocked("ffn1_dw1", h3, dpre, nb_ff, out_dtype=BF16)
    red_ff1_1, tok = reduce_begin("ff1_1", dw)
    dpre = after(dpre, tok)
    dh3 = mm_nt_blocked("ffn1_dh", dpre, w_ff1_1, out_dtype=BF16)
    red_ff2_1, tok = reduce_middle("ff2_1", red_ff2_1, dh3)
    dh3 = after(dh3, tok)
    dx3, dm1, dg_ffn_pre1, dg_mix_post1 = bwd_pre_post("bwd_3", dx4, x3, row(ffn_pre_g, 1), dh3, m1, row(mix_post_g, 1))

    dw = mm_tn("sc_dwout", y1, dm1, out_dtype=BF16)
    red_sc_out, tok = reduce_begin("sc_out", dw.reshape(N_DEV, -1, d))
    dm1 = after(dm1, tok)
    dy1 = mm_nt("sc_dy", dm1, w_sc_out, out_dtype=BF16)
    red_ff1_1, tok = reduce_middle("ff1_1", red_ff1_1, dy1)
    dy1 = after(dy1, tok)
    db1, dcg1, du1, dw_sconv = short_bwd("short_bwd", z1, dy1, w_sconv, d_short)
    dz1 = jnp.concatenate([db1, dcg1, du1], axis=1)
    dw = mm_tn_blocked("sc_dwin", h2, dz1, nb_sc, out_dtype=BF16)
    red_sc_in, tok = reduce_begin("sc_in", dw)
    dz1 = after(dz1, tok)
    dh2 = mm_nt_blocked("sc_dh", dz1, w_sc_in, out_dtype=BF16)
    red_sc_out, tok = reduce_middle("sc_out", red_sc_out, dh2)
    dh2 = after(dh2, tok)
    dx2, df0, dg_mix_pre1, dg_ffn_post0 = bwd_pre_post("bwd_2", dx3, x2, row(mix_pre_g, 1), dh2, f0, row(ffn_post_g, 0))

    dw = mm_tn("ffn0_dw2", a0, df0, out_dtype=BF16, lhs_fn=square)
    red_ff2_0, tok = reduce_begin("ff2_0", dw.reshape(N_DEV, -1, d))
    df0 = after(df0, tok)
    dpre = mm_nt("ffn0_da", df0, w_ff2_0, out_dtype=BF16, extra=a0, epilogue=relu2_bwd)
    red_sc_in, tok = reduce_middle("sc_in", red_sc_in, dpre)
    dpre = after(dpre, tok)
    dw = mm_tn_blocked("ffn0_dw1", h1, dpre, nb_ff, out_dtype=BF16)
    red_ff1_0, tok = reduce_begin("ff1_0", dw)
    dpre = after(dpre, tok)
    dh1 = mm_nt_blocked("ffn0_dh", dpre, w_ff1_0, out_dtype=BF16)
    red_ff2_0, tok = reduce_middle("ff2_0", red_ff2_0, dh1)
    dh1 = after(dh1, tok)
    dx1, dm0, dg_ffn_pre0, dg_mix_post0 = bwd_pre_post("bwd_1", dx2, x1, row(ffn_pre_g, 0), dh1, m0, row(mix_post_g, 0))

    dw = mm_tn("ab_dwout", y0, dm0, out_dtype=BF16)
    red_ab_out, tok = reduce_begin("ab_out", dw.reshape(N_DEV, -1, d))
    dm0 = after(dm0, tok)
    dy0 = mm_nt("ab_dy", dm0, w_ab_out.reshape(d_pool + d_conv, d), out_dtype=BF16)
    red_ff1_0, tok = reduce_middle("ff1_0", red_ff1_0, dy0)
    dy0 = after(dy0, tok)
    dcv, dg_ln_g, dg_ln_b = ln_silu_bwd("ln_silu_bwd", cv, conv_ln_g, conv_ln_b, dy0[:, d_pool:])
    dv, dgate, dw_conv, dg_conv_b = conv_bwd("conv_bwd", z0, dcv, w_conv, d_pool, d_conv)
    du0, dw_pool, dg_pool_scale = pool_bwd("pool_bwd", pooled, dy0[:, :d_pool], w_pool, pool_scale)
    dz0 = jnp.concatenate([du0, dv, dgate], axis=1)
    small_parts = [
        dw_conv.reshape(kw, N_DEV, -1).transpose(1, 0, 2).reshape(N_DEV, -1, lanes),
        dw_sconv.reshape(ks, N_DEV, -1).transpose(1, 0, 2).reshape(N_DEV, -1, lanes),
        dw_pool.reshape(ng, N_DEV, pg // N_DEV, pg).transpose(1, 0, 2, 3).reshape(N_DEV, -1, lanes),
    ]
    small = jnp.pad(jnp.concatenate(small_parts, axis=1), ((0, 0), (0, small_total - r2), (0, 0)))
    red_small, tok = reduce_begin("small", small)
    red_ab_out, tok2 = reduce_middle("ab_out", red_ab_out, dz0)
    dz0 = after(dz0, tok, tok2)
    dw = mm_tn_blocked("ab_dwin", h0, dz0, nb_ab, out_dtype=BF16)
    red_ab_in, tok = reduce_begin("ab_in", dw)
    dz0 = after(dz0, tok)
    dh0 = mm_nt_blocked("ab_dh", dz0, w_ab_in, out_dtype=BF16)
    red_small, tok = reduce_middle("small", red_small, dh0)
    dh0 = after(dh0, tok)
    grad_x, dg_mix_pre0 = bwd_pre_final("bwd_0", dx1, xs, row(mix_pre_g, 0), dh0)
    red_ab_in, tok = reduce_middle("ab_in", red_ab_in, grad_x)

    fold = lambda a: jnp.sum(a, axis=0, keepdims=True)
    rep_rows = [fold(dg_mix_pre0), fold(dg_mix_pre1), fold(dg_mix_post0), fold(dg_mix_post1),
                fold(dg_ffn_pre0), fold(dg_ffn_pre1), fold(dg_ffn_post0), fold(dg_ffn_post1)]
    tail = jnp.concatenate([dg_pool_scale, dg_conv_b, fold(dg_ln_g), fold(dg_ln_b)], axis=1).reshape(-1, d)
    rep = _pad_rows(jnp.concatenate(rep_rows + [tail], axis=0), 16)
    (rep_hop,), _ = copies_start("rep_start", [[place_shard("place_rep", rep, F32)]], _first_hop, 4)

    def pack_rep(a_mix_pre, a_mix_post, a_ffn_pre, a_ffn_post, a_scale, a_b, a_g, a_lb):
        tail_ = jnp.concatenate([a_scale, a_b, a_g, a_lb], axis=1).reshape(-1, d)
        return _pad_rows(jnp.concatenate([a_mix_pre, a_mix_post, a_ffn_pre, a_ffn_post, tail_], axis=0), 16)

    def upd(name, w, m, v, contribs):
        shape = w.shape
        flat2 = lambda a: a.reshape(-1, shape[-1])
        outs = adamw(name, flat2(w), flat2(m), flat2(v), contribs)
        return [o.reshape(shape) for o in outs]

    g_ff2 = [reduce_end("ff2_0", red_ff2_0, tok), reduce_end("ff2_1", red_ff2_1, tok)]
    o_ff2 = upd("adam_ffn_w2", ffn_w2, m_ffn_w2, v_ffn_w2, g_ff2)
    (rep_zone,) = copies_wait("rep_wait", rep_hop, _first_hop, o_ff2[0])
    (rep_hop,), _ = copies_start("rep_forward_start", [[rep_zone]], _second_hop, 3)
    g_ff1 = [reduce_end("ff1_0", red_ff1_0, o_ff2[0]), reduce_end("ff1_1", red_ff1_1, o_ff2[0])]
    o_ff1 = upd("adam_ffn_w1", ffn_w1, m_ffn_w1, v_ffn_w1, g_ff1)
    (rep_all,) = copies_wait("rep_forward_wait", rep_hop, _second_hop, o_ff1[0])
    o_rep = adamw("adam_replicated",
                  pack_rep(mix_pre_g, mix_post_g, ffn_pre_g, ffn_post_g, pool_scale, conv_b, conv_ln_g, conv_ln_b),
                  pack_rep(m_mix_pre_g, m_mix_post_g, m_ffn_pre_g, m_ffn_post_g, m_pool_scale, m_conv_b, m_conv_ln_g, m_conv_ln_b),
                  pack_rep(v_mix_pre_g, v_mix_post_g, v_ffn_pre_g, v_ffn_post_g, v_pool_scale, v_conv_b, v_conv_ln_g, v_conv_ln_b),
                  [rep_all])
    o_sc_out = upd("adam_sc_out", sc_w_out, m_sc_w_out, v_sc_w_out, [reduce_end("sc_out", red_sc_out, o_ff1[0])])
    o_sc_in = upd("adam_sc_in", sc_w_in, m_sc_w_in, v_sc_w_in, [reduce_end("sc_in", red_sc_in, o_sc_out[0])])
    o_ab_out = upd("adam_ab_out", ab_w_out, m_ab_w_out, v_ab_w_out, [reduce_end("ab_out", red_ab_out, o_sc_in[0])])
    o_small = adamw("adam_small", pack_small(conv_w, sc_conv_w, pool_w), pack_small(m_conv_w, m_sc_conv_w, m_pool_w),
                    pack_small(v_conv_w, v_sc_conv_w, v_pool_w), [reduce_end("small", red_small, o_ab_out[0])])
    o_ab_in = upd("adam_ab_in", ab_w_in, m_ab_w_in, v_ab_w_in, [reduce_end("ab_in", red_ab_in, o_small[0])])

    def unpack_small(o):
        return o[:r0].reshape(conv_w.shape), o[r0:r1].reshape(sc_conv_w.shape), o[r1:r2].reshape(pool_w.shape)

    def unpack_rep(o):
        tail_ = o[8:8 + tail.shape[0]].reshape(1, -1)
        n1 = d_pool
        return dict(mix_pre_g=o[0:2], mix_post_g=o[2:4], ffn_pre_g=o[4:6], ffn_post_g=o[6:8],
                    pool_scale=tail_[:, :n1], conv_b=tail_[:, n1:n1 + d_conv],
                    conv_ln_g=tail_[:, n1 + d_conv:n1 + 2 * d_conv], conv_ln_b=tail_[:, n1 + 2 * d_conv:n1 + 3 * d_conv])

    results = []
    for kind in range(4):
        rep_o = unpack_rep(o_rep[kind])
        s_conv, s_sconv, s_pool = unpack_small(o_small[kind])
        results.append([
            rep_o["mix_pre_g"], rep_o["mix_post_g"], rep_o["ffn_pre_g"], rep_o["ffn_post_g"],
            o_ab_in[kind], s_pool, rep_o["pool_scale"], s_conv, rep_o["conv_b"], rep_o["conv_ln_g"], rep_o["conv_ln_b"],
            o_ab_out[kind], o_sc_in[kind], s_sconv, o_sc_out[kind], o_ff1[kind], o_ff2[kind]])

    return (loss, grad_x[None], *results[0], *results[1], *results[2], *results[3])
```

```python
import jax
import jax.numpy as jnp
from jax import lax
from jax.experimental import pallas as pl
from jax.experimental.pallas import tpu as pltpu

F32 = jnp.float32
BF16 = jnp.bfloat16
MESH = pl.DeviceIdType.MESH
ANY = pl.BlockSpec(memory_space=pl.ANY)

NORM_EPS = 1e-6
POOL_WINDOWS = (2, 4, 8, 16)
ADAM_LR = 0.001
ADAM_B1 = 0.9
ADAM_B2 = 0.999
ADAM_EPS = 1e-08
ADAM_WD = 0.01
ADAM_STEP = 10

N_DEV = 8
VMEM_LIMIT = 56 * 1024 * 1024
PAIR_ADD_BLOCK = 1 << 20
MATMUL_ROWS = 1024
ROW_TILE = 256
CHANNEL_TILE = 256
TIME_CHUNK = 64
HALO = 32

NN = (((1,), (0,)), ((), ()))
NT = (((1,), (1,)), ((), ()))
TN = (((0,), (0,)), ((), ()))


def _params(sem):
    return pltpu.CompilerParams(dimension_semantics=sem, vmem_limit_bytes=VMEM_LIMIT)


def _place():
    x, y, c = lax.axis_index("x"), lax.axis_index("y"), lax.axis_index("c")
    return x, y, c


def _slot(px, py, pc):
    return 4 * px + 2 * py + pc


HBM = pl.BlockSpec(memory_space=pltpu.HBM)
SEM = pl.BlockSpec(memory_space=pltpu.SEMAPHORE)
EFFECT = pltpu.SideEffectType.DATAFLOW_SIDE_EFFECTING
TOKEN = jax.ShapeDtypeStruct((8, 128), F32)


def _in_hbm(a):
    return pltpu.with_memory_space_constraint(a, pltpu.HBM)


CHIPS = [(0, 0), (0, 1), (1, 0), (1, 1)]
N_CHIP = len(CHIPS)


def _chip(px, py):
    return 2 * px + py


def _first_hop(bufs, sends, recvs, waiting):
    (land,) = bufs
    x, y, c = _place()
    me = _slot(x, y, c)
    peers = [(x, y, 1 - c), (1 - x, y, c), (x, 1 - y, c), (1 - x, 1 - y, c)]
    return [pltpu.make_async_remote_copy(
        src_ref=land.at[me], dst_ref=land.at[_slot(*p) if waiting else me],
        send_sem=sends.at[k], recv_sem=recvs.at[k], device_id=p, device_id_type=MESH) for k, p in enumerate(peers)]


def _second_hop(bufs, sends, recvs, waiting):
    (land,) = bufs
    x, y, c = _place()
    return [pltpu.make_async_remote_copy(
        src_ref=land.at[_slot(px, py, c)], dst_ref=land.at[_slot(px, py, 1 - c if waiting else c)],
        send_sem=sends.at[k], recv_sem=recvs.at[k], device_id=(x, y, 1 - c), device_id_type=MESH)
        for k, (px, py) in enumerate([(1 - x, y), (x, 1 - y), (1 - x, 1 - y)])]


def _ring_hop1(bufs, sends, recvs, waiting):
    (land,) = bufs
    x, y, c = _place()
    me = _slot(x, y, c)
    peers = [(1 - x, y, c), (x, 1 - y, c), (x, y, 1 - c)]
    return [pltpu.make_async_remote_copy(
        src_ref=land.at[me], dst_ref=land.at[_slot(*p) if waiting else me],
        send_sem=sends.at[k], recv_sem=recvs.at[k], device_id=p, device_id_type=MESH) for k, p in enumerate(peers)]


def _ring_hop2(bufs, sends, recvs, waiting):
    (land,) = bufs
    x, y, c = _place()
    half = land.shape[1] // 2
    first, second = pl.ds(0, half), pl.ds(half, half)
    nx, ny, diag = _slot(1 - x, y, c), _slot(x, 1 - y, c), _slot(1 - x, 1 - y, c)
    plan = [
        (land.at[ny, first], land.at[diag, first], (1 - x, y, c)),
        (land.at[nx, second], land.at[diag, second], (x, 1 - y, c)),
        (land.at[nx], land.at[_slot(1 - x, y, 1 - c)], (x, y, 1 - c)),
        (land.at[ny], land.at[_slot(x, 1 - y, 1 - c)], (x, y, 1 - c))]
    return [pltpu.make_async_remote_copy(
        src_ref=src, dst_ref=mine if waiting else src, send_sem=sends.at[k], recv_sem=recvs.at[k],
        device_id=to, device_id_type=MESH) for k, (src, mine, to) in enumerate(plan)]


def _ring_hop3(bufs, sends, recvs, waiting):
    (land,) = bufs
    x, y, c = _place()
    return [pltpu.make_async_remote_copy(
        src_ref=land.at[_slot(1 - x, 1 - y, c)], dst_ref=land.at[_slot(1 - x, 1 - y, 1 - c if waiting else c)],
        send_sem=sends.at[0], recv_sem=recvs.at[0], device_id=(x, y, 1 - c), device_id_type=MESH)]


def _pair_hop(bufs, sends, recvs, waiting):
    g, land = bufs
    x, y, c = _place()
    return [pltpu.make_async_remote_copy(
        src_ref=g.at[_slot(qx, qy, 1 - c)], dst_ref=land.at[q],
        send_sem=sends.at[q], recv_sem=recvs.at[q], device_id=(x, y, 1 - c), device_id_type=MESH)
        for q, (qx, qy) in enumerate(CHIPS)]


def _chip_hop(bufs, sends, recvs, waiting):
    p, land = bufs
    x, y, c = _place()
    return [pltpu.make_async_remote_copy(
        src_ref=p.at[_chip(px, py)], dst_ref=land.at[_chip(px, py) if waiting else _chip(x, y)],
        send_sem=sends.at[k], recv_sem=recvs.at[k], device_id=(px, py, c), device_id_type=MESH)
        for k, (px, py) in enumerate([(1 - x, y), (x, 1 - y), (1 - x, 1 - y)])]


def copies_start(name, groups, hop, n_copies, deps=()):
    flat = [b for grp in groups for b in grp]
    nb, ng = len(flat), len(groups)
    deps = list(deps)

    def body(*refs):
        ins, token = refs[:nb], refs[-1]
        sems = refs[nb + len(deps):nb + len(deps) + 2 * ng]
        i = 0
        for gi, grp in enumerate(groups):
            for cp in hop(ins[i:i + len(grp)], sems[2 * gi], sems[2 * gi + 1], False):
                cp.start()
            i += len(grp)
        token[...] = jnp.zeros_like(token)

    outs = pl.pallas_call(
        body, name=name,
        out_shape=([pltpu.SemaphoreType.DMA((n_copies,))] * (2 * ng) + [pltpu.HBM(b.shape, b.dtype) for b in flat]
                   + [TOKEN]),
        in_specs=[HBM] * nb + [ANY] * len(deps),
        out_specs=[SEM] * (2 * ng) + [HBM] * nb + [pl.BlockSpec(memory_space=pltpu.VMEM)],
        input_output_aliases={i: 2 * ng + i for i in range(nb)},
        compiler_params=pltpu.CompilerParams(has_side_effects=EFFECT),
    )(*[_in_hbm(b) for b in flat], *deps)
    started, i = [], 0
    for gi, grp in enumerate(groups):
        started.append((outs[2 * gi], outs[2 * gi + 1], list(outs[2 * ng + i:2 * ng + i + len(grp)])))
        i += len(grp)
    return started, outs[-1]


def copies_wait(name, started, hop, after):
    sends, recvs, bufs = started
    nb = len(bufs)

    def body(*refs):
        for cp in hop(refs[:nb], refs[nb], refs[nb + 1], True):
            cp.wait_send()
            cp.wait_recv()

    outs = pl.pallas_call(
        body, name=name,
        out_shape=[pltpu.HBM(b.shape, b.dtype) for b in bufs],
        in_specs=[HBM] * nb + [SEM, SEM, ANY], out_specs=[HBM] * nb,
        input_output_aliases={i: i for i in range(nb)},
        compiler_params=pltpu.CompilerParams(has_side_effects=EFFECT),
    )(*bufs, sends, recvs, after)
    return list(outs)


def place_shard(name, w, dtype):
    r, c = w.shape
    tr = _tile(r, 1024)
    x, y, core = _place()
    me = _slot(x, y, core).astype(jnp.int32).reshape(1)

    def body(me_ref, w_ref, o_ref):
        o_ref[...] = w_ref[...].astype(dtype)

    return pl.pallas_call(
        body, name=name,
        grid_spec=pltpu.PrefetchScalarGridSpec(
            num_scalar_prefetch=1, grid=(r // tr,),
            in_specs=[pl.BlockSpec((tr, c), lambda i, me_ref: (i, 0))],
            out_specs=pl.BlockSpec((None, tr, c), lambda i, me_ref: (me_ref[0], i, 0))),
        out_shape=jax.ShapeDtypeStruct((N_DEV, r, c), dtype),
        compiler_params=_params(("parallel",)),
    )(me, w)


def tie(name, x, *deps):
    def body(*refs):
        del refs

    return pl.pallas_call(
        body, name=name, out_shape=jax.ShapeDtypeStruct(x.shape, x.dtype),
        in_specs=[ANY] * (1 + len(deps)), out_specs=ANY, input_output_aliases={0: 0},
    )(x, *deps)


def pair_add(name, g, from_sibling):
    _, r, c_dim = g.shape
    tr = r
    while tr * c_dim > PAIR_ADD_BLOCK and tr % 16 == 0:
        tr //= 2
    x, y, core = _place()
    where = jnp.stack([core, _chip(x, y)]).astype(jnp.int32)

    def body(where_ref, g_ref, s_ref, o_ref, zone_ref):
        total = (g_ref[...].astype(F32) + s_ref[...].astype(F32)).astype(o_ref.dtype)
        o_ref[...] = total

        @pl.when(pl.program_id(1) == where_ref[1])
        def _():
            zone_ref[...] = total

    blk = pl.BlockSpec((None, tr, c_dim), lambda i, q, where_ref: (q, i, 0))
    return pl.pallas_call(
        body, name=name,
        grid_spec=pltpu.PrefetchScalarGridSpec(
            num_scalar_prefetch=1, grid=(r // tr, N_CHIP),
            in_specs=[pl.BlockSpec((None, None, tr, c_dim), lambda i, q, where_ref: (q, where_ref[0], i, 0)), blk],
            out_specs=[blk, pl.BlockSpec((None, tr, c_dim), lambda i, q, where_ref: (where_ref[1], i, 0))]),
        out_shape=[jax.ShapeDtypeStruct((N_CHIP, r, c_dim), g.dtype)] * 2,
        compiler_params=_params(("parallel", "arbitrary")),
    )(where, g.reshape(N_CHIP, 2, r, c_dim), from_sibling)


def _matmul(name, lhs, rhs, *, out_shape, out_dtype, grid, lhs_spec, rhs_spec, out_spec, dims, acc_shape,
            lhs_fn=None, extra=(), extra_specs=(), epilogue=None, parts=1):
    nk = grid[2]
    n_extra = len(extra)

    def body(*refs):
        lhs_ref, rhs_ref = refs[0], refs[1]
        extra_refs = refs[2:2 + n_extra]
        out_ref = refs[2 + n_extra]

        def product():
            if parts == 1:
                a = lhs_ref[...]
                if lhs_fn is not None:
                    a = lhs_fn(a)
                return lax.dot_general(a, rhs_ref[...], dims, preferred_element_type=F32)
            width = lhs_ref.shape[1] // parts
            total = None
            for b in range(parts):
                term = lax.dot_general(lhs_ref[:, b * width:(b + 1) * width], rhs_ref[b], dims,
                                       preferred_element_type=F32)
                total = term if total is None else total + term
            return total

        def finish(r):
            if epilogue is not None:
                r = epilogue(r, *[e[...] for e in extra_refs])
            out_ref[...] = r.astype(out_dtype)

        if nk == 1:
            finish(product())
        else:
            acc_ref = refs[3 + n_extra]
            k = pl.program_id(2)

            @pl.when(k == 0)
            def _():
                acc_ref[...] = product()

            @pl.when(jnp.logical_and(k > 0, k < nk - 1))
            def _():
                acc_ref[...] += product()

            @pl.when(k == nk - 1)
            def _():
                finish(acc_ref[...] + product())

    return pl.pallas_call(
        body, name=name, grid=grid,
        out_shape=jax.ShapeDtypeStruct(out_shape, out_dtype),
        in_specs=[lhs_spec, rhs_spec, *extra_specs], out_specs=out_spec,
        scratch_shapes=[pltpu.VMEM(acc_shape, F32)] if nk > 1 else [],
        compiler_params=_params(("parallel", "parallel", "arbitrary")),
    )(lhs, rhs, *extra)


def _tile(n, want):
    return want if n % want == 0 else n


def mm_nn(name, x, w, *, out_dtype, tn=512, tk=None, lhs_fn=None, epilogue=None):
    t, kdim = x.shape
    n = w.shape[1]
    tm, tn = _tile(t, MATMUL_ROWS), _tile(n, tn)
    tk = kdim if tk is None else _tile(kdim, tk)
    return _matmul(
        name, x, w, out_shape=(t, n), out_dtype=out_dtype, grid=(t // tm, n // tn, kdim // tk),
        lhs_spec=pl.BlockSpec((tm, tk), lambda i, j, k: (i, k)),
        rhs_spec=pl.BlockSpec((tk, tn), lambda i, j, k: (k, j)),
        out_spec=pl.BlockSpec((tm, tn), lambda i, j, k: (i, j)),
        dims=NN, acc_shape=(tm, tn), lhs_fn=lhs_fn, epilogue=epilogue)


def mm_nn_blocked(name, x, w, *, out_dtype, epilogue=None):
    t, kdim = x.shape
    nb = w.shape[2]
    tm = _tile(t, MATMUL_ROWS)
    tn = nb // 2 if nb >= 1024 else nb
    sub = nb // tn
    return _matmul(
        name, x, w, out_shape=(t, N_DEV * nb), out_dtype=out_dtype, grid=(t // tm, N_DEV * sub, 1),
        lhs_spec=pl.BlockSpec((tm, kdim), lambda i, j, k: (i, k)),
        rhs_spec=pl.BlockSpec((None, kdim, tn), lambda i, j, k: (j // sub, k, j % sub)),
        out_spec=pl.BlockSpec((tm, tn), lambda i, j, k: (i, j)),
        dims=NN, acc_shape=(tm, tn), epilogue=epilogue)


def mm_nt(name, dy, w, *, out_dtype, tn=512, extra=None, epilogue=None):
    t, n = dy.shape
    kdim = w.shape[0]
    tm, tn = _tile(t, MATMUL_ROWS), _tile(kdim, tn)
    extra_arrs = () if extra is None else (extra,)
    extra_specs = () if extra is None else (pl.BlockSpec((tm, tn), lambda i, j, k: (i, j)),)
    return _matmul(
        name, dy, w, out_shape=(t, kdim), out_dtype=out_dtype, grid=(t // tm, kdim // tn, 1),
        lhs_spec=pl.BlockSpec((tm, n), lambda i, j, k: (i, k)),
        rhs_spec=pl.BlockSpec((tn, n), lambda i, j, k: (j, k)),
        out_spec=pl.BlockSpec((tm, tn), lambda i, j, k: (i, j)),
        dims=NT, acc_shape=(tm, tn), extra=extra_arrs, extra_specs=extra_specs, epilogue=epilogue)


def mm_nt_blocked(name, dz, w, *, out_dtype, tn=512):
    t = dz.shape[0]
    kdim, nb = w.shape[1], w.shape[2]
    tm, tn = _tile(t, MATMUL_ROWS), _tile(kdim, tn)
    parts = 2
    return _matmul(
        name, dz, w, out_shape=(t, kdim), out_dtype=out_dtype, grid=(t // tm, kdim // tn, N_DEV // parts),
        lhs_spec=pl.BlockSpec((tm, parts * nb), lambda i, j, k: (i, k)),
        rhs_spec=pl.BlockSpec((parts, tn, nb), lambda i, j, k: (k, j, 0)),
        out_spec=pl.BlockSpec((tm, tn), lambda i, j, k: (i, j)),
        dims=NT, acc_shape=(tm, tn), parts=parts)


def mm_tn(name, x, dy, *, out_dtype, tk=1024, tn=1024, lhs_fn=None):
    t, kdim = x.shape
    n = dy.shape[1]
    tk, tn = _tile(kdim, tk), _tile(n, tn)
    return _matmul(
        name, x, dy, out_shape=(kdim, n), out_dtype=out_dtype, grid=(kdim // tk, n // tn, 1),
        lhs_spec=pl.BlockSpec((t, tk), lambda i, j, k: (k, i)),
        rhs_spec=pl.BlockSpec((t, tn), lambda i, j, k: (k, j)),
        out_spec=pl.BlockSpec((tk, tn), lambda i, j, k: (i, j)),
        dims=TN, acc_shape=(tk, tn), lhs_fn=lhs_fn)


def mm_tn_blocked(name, x, dz, nb, *, out_dtype, tk=1024):
    t, kdim = x.shape
    tk = _tile(kdim, tk)
    return _matmul(
        name, x, dz, out_shape=(N_DEV, kdim, nb), out_dtype=out_dtype, grid=(kdim // tk, N_DEV, 1),
        lhs_spec=pl.BlockSpec((t, tk), lambda i, j, k: (k, i)),
        rhs_spec=pl.BlockSpec((t, nb), lambda i, j, k: (k, j)),
        out_spec=pl.BlockSpec((None, tk, nb), lambda i, j, k: (j, i, 0)),
        dims=TN, acc_shape=(tk, nb))


def _rstd(v):
    return lax.rsqrt(jnp.mean(v * v, axis=-1, keepdims=True) + NORM_EPS)


def _rms_bwd(v, g, dy):
    r = _rstd(v)
    vhat = v * r
    dvh = dy * g
    dv = r * (dvh - vhat * jnp.mean(dvh * vhat, axis=-1, keepdims=True))
    return dv, dy * vhat


def _fold8(v):
    rows, n = v.shape
    return jnp.sum(v.reshape(rows // 8, 8, n), axis=0)


def _fold_lanes(v):
    out = v[:, 0:128]
    for i in range(1, v.shape[1] // 128):
        out = out + v[:, 128 * i:128 * (i + 1)]
    return out


def _accumulate(ref, v):
    i = pl.program_id(0)

    @pl.when(i == 0)
    def _():
        ref[...] = v

    @pl.when(i > 0)
    def _():
        ref[...] += v


def _row_call(body, name, t, ins, row_in, outs, acc_outs=(), tr=ROW_TILE):
    tr = _tile(t, tr)

    def in_spec(a, tiled):
        if isinstance(tiled, tuple):
            width, j = tiled
            return pl.BlockSpec((tr, width), lambda i: (i, j))
        return pl.BlockSpec((tr, a.shape[1]), lambda i: (i, 0)) if tiled else pl.BlockSpec(a.shape, lambda i: (0, 0))

    in_specs = [in_spec(a, tiled) for a, tiled in zip(ins, row_in)]
    out_specs = [pl.BlockSpec((tr, n), lambda i: (i, 0)) for n, _ in outs]
    out_specs += [pl.BlockSpec((8, n), lambda i: (0, 0)) for n in acc_outs]
    out_shape = [jax.ShapeDtypeStruct((t, n), dt) for n, dt in outs]
    out_shape += [jax.ShapeDtypeStruct((8, n), F32) for n in acc_outs]
    return pl.pallas_call(
        body, name=name, grid=(t // tr,), in_specs=in_specs, out_specs=out_specs, out_shape=out_shape,
        compiler_params=_params(("arbitrary",) if acc_outs else ("parallel",)),
    )(*ins)


def norm_pre(name, x, g):
    t, d = x.shape

    def body(x_ref, g_ref, h_ref):
        v = x_ref[...]
        h_ref[...] = (v * _rstd(v) * g_ref[...]).astype(BF16)

    return _row_call(body, name, t, [x, g], [True, False], [(d, BF16)])[0]


def post_pre(name, x, m, g_post, g_pre):
    t, d = x.shape

    def body(x_ref, m_ref, gp_ref, gn_ref, xo_ref, h_ref):
        mv = m_ref[...]
        xn = x_ref[...] + mv * _rstd(mv) * gp_ref[...]
        xo_ref[...] = xn
        h_ref[...] = (xn * _rstd(xn) * gn_ref[...]).astype(BF16)

    return _row_call(body, name, t, [x, m, g_post, g_pre], [True, True, False, False], [(d, F32), (d, BF16)])


def post_loss(name, x, f, g_post, target):
    t, d = x.shape

    def body(x_ref, f_ref, g_ref, t_ref, dx_ref, df_ref, loss_ref, dg_ref):
        fv = f_ref[...]
        g = g_ref[...]
        out = x_ref[...] + fv * _rstd(fv) * g
        err = out - t_ref[...]
        dx = err * (1.0 / d)
        dx_ref[...] = dx
        dfv, dg_rows = _rms_bwd(fv, g, dx)
        df_ref[...] = dfv.astype(BF16)
        _accumulate(loss_ref, _fold8(_fold_lanes(err * err)))
        _accumulate(dg_ref, _fold8(dg_rows))

    return _row_call(body, name, t, [x, f, g_post, target], [True, True, False, True],
                     [(d, F32), (d, BF16)], acc_outs=(128, d))


def bwd_pre_post(name, dx_out, x_in, g_pre, dh, f_prev, g_post_prev):
    t, d = x_in.shape

    def body(dxo_ref, x_ref, gpre_ref, dh_ref, f_ref, gpost_ref, dxi_ref, df_ref, dgpre_ref, dgpost_ref):
        dxv, dgpre_rows = _rms_bwd(x_ref[...], gpre_ref[...], dh_ref[...].astype(F32))
        dxi = dxo_ref[...] + dxv
        dxi_ref[...] = dxi
        dfv, dgpost_rows = _rms_bwd(f_ref[...], gpost_ref[...], dxi)
        df_ref[...] = dfv.astype(BF16)
        _accumulate(dgpre_ref, _fold8(dgpre_rows))
        _accumulate(dgpost_ref, _fold8(dgpost_rows))

    return _row_call(body, name, t, [dx_out, x_in, g_pre, dh, f_prev, g_post_prev],
                     [True, True, False, True, True, False], [(d, F32), (d, BF16)], acc_outs=(d, d))


def bwd_pre_final(name, dx_out, x_in, g_pre, dh):
    t, d = x_in.shape

    def body(dxo_ref, x_ref, gpre_ref, dh_ref, dxi_ref, dgpre_ref):
        dxv, dgpre_rows = _rms_bwd(x_ref[...], gpre_ref[...], dh_ref[...].astype(F32))
        dxi_ref[...] = dxo_ref[...] + dxv
        _accumulate(dgpre_ref, _fold8(dgpre_rows))

    return _row_call(body, name, t, [dx_out, x_in, g_pre, dh], [True, True, False, True], [(d, F32)], acc_outs=(d,))


def _layer_norm_parts(cv):
    mu = jnp.mean(cv, axis=-1, keepdims=True)
    xc = cv - mu
    rstd = lax.rsqrt(jnp.mean(xc * xc, axis=-1, keepdims=True) + NORM_EPS)
    return xc * rstd, rstd


def ln_silu(name, cv, g, b):
    t, n = cv.shape

    def body(c_ref, g_ref, b_ref, y_ref):
        chat, _ = _layer_norm_parts(c_ref[...])
        ln = chat * g_ref[...] + b_ref[...]
        y_ref[...] = (ln * jax.nn.sigmoid(ln)).astype(BF16)

    return _row_call(body, name, t, [cv, g, b], [True, False, False], [(n, BF16)])[0]


def ln_silu_bwd(name, cv, g, b, dy, dy_block):
    t, n = cv.shape

    def body(c_ref, g_ref, b_ref, dy_ref, dc_ref, dg_ref, db_ref):
        chat, rstd = _layer_norm_parts(c_ref[...])
        g = g_ref[...]
        ln = chat * g + b_ref[...]
        s = jax.nn.sigmoid(ln)
        dln = dy_ref[...].astype(F32) * (s * (1.0 + ln * (1.0 - s)))
        dchat = dln * g
        dc_ref[...] = rstd * (dchat - jnp.mean(dchat, axis=-1, keepdims=True)
                              - chat * jnp.mean(dchat * chat, axis=-1, keepdims=True))
        _accumulate(dg_ref, _fold8(dln * chat))
        _accumulate(db_ref, _fold8(dln))

    return _row_call(body, name, t, [cv, g, b, dy], [True, False, False, (n, dy_block)], [(n, F32)], acc_outs=(n, n))


def _chunks(t, fn, tc=TIME_CHUNK):
    tc = _tile(t, tc)

    def step(i, carry):
        fn(pl.multiple_of(i * tc, tc), tc)
        return carry

    lax.fori_loop(0, t // tc, step, 0)


def _shifted(window, offsets, tc):
    by_residue = {}
    for k, off in enumerate(offsets):
        by_residue.setdefault(off % 8, []).append((k, off))
    for res, taps in by_residue.items():
        base = window[res:res + tc + max(off for _, off in taps) - res, :]
        for k, off in taps:
            yield k, base[off - res:off - res + tc, :]


def _taps(window, w_ref, offsets, tc, flip=False):
    acc = None
    for k, rows in _shifted(window, offsets, tc):
        kk = len(offsets) - 1 - k if flip else k
        term = w_ref[kk:kk + 1, :] * rows
        acc = term if acc is None else acc + term
    return acc


def _window_sums(win, tc, causal):
    sums = []
    cur, rows, step = win, tc + HALO, 1
    for _ in POOL_WINDOWS:
        rows -= 8
        if causal:
            cur = cur[8:8 + rows, :] + cur[8 - step:8 - step + rows, :]
            sums.append(cur[rows - tc:rows, :])
        else:
            cur = cur[0:rows, :] + cur[step:step + rows, :]
            sums.append(cur[0:tc, :])
        step *= 2
    return sums


def _pick(vals, g):
    out = vals[-1]
    for i in range(len(vals) - 2, -1, -1):
        out = jnp.where(g == i, vals[i], out)
    return out


def _pool_count(s, tc, g):
    t1 = (lax.broadcasted_iota(jnp.int32, (tc, 1), 0) + (s + 1)).astype(F32)
    width = _pick([float(w) for w in POOL_WINDOWS], g)
    return jnp.minimum(t1, width)


def pool_fwd(name, z, pool_w, pool_scale, d_pool):
    t = z.shape[0]
    ng, pg = pool_w.shape[0], pool_w.shape[1]

    def body(u_ref, w_ref, s_ref, pooled_ref, y_ref, pad):
        g = pl.program_id(0)
        pad[pl.ds(0, HALO), :] = jnp.zeros((HALO, pg), F32)

        def fill(s, tc):
            pad[pl.ds(HALO + s, tc), :] = u_ref[pl.ds(s, tc), :].astype(F32)

        def chunk(s, tc):
            win = pad[pl.ds(s, tc + HALO), :]
            total = _pick(_window_sums(win, tc, causal=True), g)
            pooled = total / _pool_count(s, tc, g) - win[HALO:HALO + tc, :]
            pooled_ref[pl.ds(s, tc), :] = pooled.astype(BF16)

        _chunks(t, fill)
        _chunks(t, chunk)
        mixed = jnp.dot(pooled_ref[...], w_ref[...], preferred_element_type=F32)
        y_ref[...] = (mixed * s_ref[...]).astype(BF16)

    col = pl.BlockSpec((t, pg), lambda g: (0, g))
    return pl.pallas_call(
        body, name=name, grid=(ng,),
        in_specs=[col, pl.BlockSpec((None, pg, pg), lambda g: (g, 0, 0)), pl.BlockSpec((1, pg), lambda g: (0, g))],
        out_specs=[col, col],
        out_shape=[jax.ShapeDtypeStruct((t, d_pool), BF16), jax.ShapeDtypeStruct((t, d_pool), BF16)],
        scratch_shapes=[pltpu.VMEM((t + HALO, pg), F32)],
        compiler_params=_params(("parallel",)),
    )(z, pool_w, pool_scale)


def pool_bwd(name, pooled, dy, pool_w, pool_scale):
    t, d_pool = pooled.shape
    ng, pg = pool_w.shape[0], pool_w.shape[1]

    def body(p_ref, dy_ref, w_ref, s_ref, du_ref, dw_ref, ds_ref, pad):
        g = pl.program_id(0)
        w = w_ref[...]
        dyv = dy_ref[...].astype(F32)
        mixed = jnp.dot(p_ref[...], w, preferred_element_type=F32)
        ds_ref[...] = jnp.sum(dyv * mixed, axis=0, keepdims=True)
        dmixed = (dyv * s_ref[...]).astype(BF16)
        dw_ref[...] = lax.dot_general(p_ref[...], dmixed, TN, preferred_element_type=F32)
        pad[...] = jnp.zeros((t + HALO, pg), F32)
        pad[pl.ds(0, t), :] = lax.dot_general(dmixed, w, NT, preferred_element_type=F32)

        def scale(s, tc):
            pad[pl.ds(s, tc), :] = pad[pl.ds(s, tc), :] / _pool_count(s, tc, g)

        def chunk(s, tc):
            win = pad[pl.ds(s, tc + HALO), :]
            total = _pick(_window_sums(win, tc, causal=False), g)
            du_ref[pl.ds(s, tc), :] = (total - win[0:tc, :] * _pool_count(s, tc, g)).astype(BF16)

        _chunks(t, scale)
        _chunks(t, chunk)

    col = pl.BlockSpec((t, pg), lambda g: (0, g))
    vec = pl.BlockSpec((1, pg), lambda g: (0, g))
    mat = pl.BlockSpec((None, pg, pg), lambda g: (g, 0, 0))
    return pl.pallas_call(
        body, name=name, grid=(ng,),
        in_specs=[col, col, mat, vec], out_specs=[col, mat, vec],
        out_shape=[jax.ShapeDtypeStruct((t, d_pool), BF16), jax.ShapeDtypeStruct((ng, pg, pg), F32),
                   jax.ShapeDtypeStruct((1, d_pool), F32)],
        scratch_shapes=[pltpu.VMEM((t + HALO, pg), F32)],
        compiler_params=_params(("parallel",)),
    )(pooled, dy, pool_w, pool_scale)


def conv_fwd(name, z, conv_w, conv_b, d_pool, d_conv):
    t = z.shape[0]
    kw = conv_w.shape[0]
    tc_ch = _tile(d_conv, CHANNEL_TILE)
    v0, g0 = d_pool // tc_ch, (d_pool + d_conv) // tc_ch

    def body(v_ref, g_ref, w_ref, b_ref, c_ref, pad):
        pad[pl.ds(0, HALO), :] = jnp.zeros((HALO, tc_ch), F32)

        def fill(s, tc):
            pad[pl.ds(HALO + s, tc), :] = v_ref[pl.ds(s, tc), :].astype(F32) * jax.nn.sigmoid(g_ref[pl.ds(s, tc), :].astype(F32))

        def chunk(s, tc):
            win = pad[pl.ds(s, tc + HALO), :]
            c_ref[pl.ds(s, tc), :] = _taps(win, w_ref, [HALO - (kw - 1) + k for k in range(kw)], tc) + b_ref[...]

        _chunks(t, fill)
        _chunks(t, chunk)

    return pl.pallas_call(
        body, name=name, grid=(d_conv // tc_ch,),
        in_specs=[pl.BlockSpec((t, tc_ch), lambda j: (0, v0 + j)), pl.BlockSpec((t, tc_ch), lambda j: (0, g0 + j)),
                  pl.BlockSpec((kw, tc_ch), lambda j: (0, j)), pl.BlockSpec((1, tc_ch), lambda j: (0, j))],
        out_specs=pl.BlockSpec((t, tc_ch), lambda j: (0, j)),
        out_shape=jax.ShapeDtypeStruct((t, d_conv), F32),
        scratch_shapes=[pltpu.VMEM((t + HALO, tc_ch), F32)],
        compiler_params=_params(("parallel",)),
    )(z, z, conv_w, conv_b)


def conv_bwd(name, z, dc, conv_w, d_pool, d_conv):
    t = z.shape[0]
    kw = conv_w.shape[0]
    tc_ch = _tile(d_conv, CHANNEL_TILE)
    v0, g0 = d_pool // tc_ch, (d_pool + d_conv) // tc_ch

    def body(v_ref, g_ref, dc_ref, w_ref, dv_ref, dg_ref, dw_ref, db_ref, pad_a, pad_dc, acc_w, acc_b):
        pad_a[pl.ds(0, HALO), :] = jnp.zeros((HALO, tc_ch), F32)
        pad_dc[pl.ds(t, HALO), :] = jnp.zeros((HALO, tc_ch), F32)
        acc_w[...] = jnp.zeros_like(acc_w)
        acc_b[...] = jnp.zeros_like(acc_b)

        def fill(s, tc):
            pad_a[pl.ds(HALO + s, tc), :] = v_ref[pl.ds(s, tc), :].astype(F32) * jax.nn.sigmoid(g_ref[pl.ds(s, tc), :].astype(F32))
            pad_dc[pl.ds(s, tc), :] = dc_ref[pl.ds(s, tc), :]

        def chunk(s, tc):
            dcv = pad_dc[pl.ds(s, tc), :]
            win_a = pad_a[pl.ds(s, tc + HALO), :]
            for k, rows in _shifted(win_a, [HALO - (kw - 1) + k for k in range(kw)], tc):
                acc_w[pl.ds(8 * k, 8), :] += _fold8(dcv * rows)
            acc_b[...] += _fold8(dcv)
            da = _taps(pad_dc[pl.ds(s, tc + HALO), :], w_ref, list(range(kw)), tc, flip=True)
            vv = v_ref[pl.ds(s, tc), :].astype(F32)
            sg = jax.nn.sigmoid(g_ref[pl.ds(s, tc), :].astype(F32))
            dv_ref[pl.ds(s, tc), :] = (da * sg).astype(BF16)
            dg_ref[pl.ds(s, tc), :] = (da * vv * sg * (1.0 - sg)).astype(BF16)

        _chunks(t, fill)
        _chunks(t, chunk)
        for k in range(kw):
            dw_ref[k:k + 1, :] = jnp.sum(acc_w[pl.ds(8 * k, 8), :], axis=0, keepdims=True)
        db_ref[...] = jnp.sum(acc_b[...], axis=0, keepdims=True)

    col = pl.BlockSpec((t, tc_ch), lambda j: (0, j))
    return pl.pallas_call(
        body, name=name, grid=(d_conv // tc_ch,),
        in_specs=[pl.BlockSpec((t, tc_ch), lambda j: (0, v0 + j)), pl.BlockSpec((t, tc_ch), lambda j: (0, g0 + j)),
                  col, pl.BlockSpec((kw, tc_ch), lambda j: (0, j))],
        out_specs=[col, col, pl.BlockSpec((kw, tc_ch), lambda j: (0, j)), pl.BlockSpec((1, tc_ch), lambda j: (0, j))],
        out_shape=[jax.ShapeDtypeStruct((t, d_conv), BF16), jax.ShapeDtypeStruct((t, d_conv), BF16),
                   jax.ShapeDtypeStruct((kw, d_conv), F32), jax.ShapeDtypeStruct((1, d_conv), F32)],
        scratch_shapes=[pltpu.VMEM((t + HALO, tc_ch), F32), pltpu.VMEM((t + HALO, tc_ch), F32),
                        pltpu.VMEM((8 * kw, tc_ch), F32), pltpu.VMEM((8, tc_ch), F32)],
        compiler_params=_params(("parallel",)),
    )(z, z, dc, conv_w)


def short_fwd(name, z, conv_w, d_short):
    t = z.shape[0]
    kw = conv_w.shape[0]
    tc_ch = _tile(d_short, CHANNEL_TILE)
    nt = d_short // tc_ch

    def body(b_ref, c_ref, u_ref, w_ref, y_ref, pad):
        pad[pl.ds(0, HALO), :] = jnp.zeros((HALO, tc_ch), F32)

        def fill(s, tc):
            pad[pl.ds(HALO + s, tc), :] = c_ref[pl.ds(s, tc), :].astype(F32) * u_ref[pl.ds(s, tc), :].astype(F32)

        def chunk(s, tc):
            win = pad[pl.ds(s, tc + HALO), :]
            cq = _taps(win, w_ref, [HALO - (kw - 1) + k for k in range(kw)], tc)
            y_ref[pl.ds(s, tc), :] = (b_ref[pl.ds(s, tc), :].astype(F32) * cq).astype(BF16)

        _chunks(t, fill)
        _chunks(t, chunk)

    return pl.pallas_call(
        body, name=name, grid=(nt,),
        in_specs=[pl.BlockSpec((t, tc_ch), lambda j: (0, j)), pl.BlockSpec((t, tc_ch), lambda j: (0, nt + j)),
                  pl.BlockSpec((t, tc_ch), lambda j: (0, 2 * nt + j)), pl.BlockSpec((kw, tc_ch), lambda j: (0, j))],
        out_specs=pl.BlockSpec((t, tc_ch), lambda j: (0, j)),
        out_shape=jax.ShapeDtypeStruct((t, d_short), BF16),
        scratch_shapes=[pltpu.VMEM((t + HALO, tc_ch), F32)],
        compiler_params=_params(("parallel",)),
    )(z, z, z, conv_w)


def short_bwd(name, z, dy, conv_w, d_short):
    t = z.shape[0]
    kw = conv_w.shape[0]
    tc_ch = _tile(d_short, CHANNEL_TILE)
    nt = d_short // tc_ch

    def body(b_ref, c_ref, u_ref, dy_ref, w_ref, db_ref, dcg_ref, du_ref, dw_ref, pad_q, pad_dcq, acc_w):
        pad_q[pl.ds(0, HALO), :] = jnp.zeros((HALO, tc_ch), F32)
        pad_dcq[pl.ds(t, HALO), :] = jnp.zeros((HALO, tc_ch), F32)
        acc_w[...] = jnp.zeros_like(acc_w)

        def fill(s, tc):
            rows = pl.ds(s, tc)
            pad_q[pl.ds(HALO + s, tc), :] = c_ref[rows, :].astype(F32) * u_ref[rows, :].astype(F32)
            pad_dcq[rows, :] = dy_ref[rows, :].astype(F32) * b_ref[rows, :].astype(F32)

        def chunk(s, tc):
            rows = pl.ds(s, tc)
            win_q = pad_q[pl.ds(s, tc + HALO), :]
            dcq = pad_dcq[rows, :]
            cq = None
            for k in range(kw):
                off = HALO - (kw - 1) + k
                shifted = win_q[off:off + tc, :]
                acc_w[pl.ds(8 * k, 8), :] += _fold8(dcq * shifted)
                term = w_ref[k:k + 1, :] * shifted
                cq = term if cq is None else cq + term
            db_ref[rows, :] = (dy_ref[rows, :].astype(F32) * cq).astype(BF16)
            win_d = pad_dcq[pl.ds(s, tc + HALO), :]
            dq = None
            for j in range(kw):
                term = w_ref[kw - 1 - j:kw - j, :] * win_d[j:j + tc, :]
                dq = term if dq is None else dq + term
            dcg_ref[rows, :] = (dq * u_ref[rows, :].astype(F32)).astype(BF16)
            du_ref[rows, :] = (dq * c_ref[rows, :].astype(F32)).astype(BF16)

        _chunks(t, fill)
        _chunks(t, chunk)
        for k in range(kw):
            dw_ref[k:k + 1, :] = jnp.sum(acc_w[pl.ds(8 * k, 8), :], axis=0, keepdims=True)

    col = pl.BlockSpec((t, tc_ch), lambda j: (0, j))
    zspec = [pl.BlockSpec((t, tc_ch), lambda j, o=o: (0, o * nt + j)) for o in range(3)]
    return pl.pallas_call(
        body, name=name, grid=(nt,),
        in_specs=[*zspec, col, pl.BlockSpec((kw, tc_ch), lambda j: (0, j))],
        out_specs=[col, col, col, pl.BlockSpec((kw, tc_ch), lambda j: (0, j))],
        out_shape=[jax.ShapeDtypeStruct((t, d_short), BF16)] * 3 + [jax.ShapeDtypeStruct((kw, d_short), F32)],
        scratch_shapes=[pltpu.VMEM((t + HALO, tc_ch), F32), pltpu.VMEM((t + HALO, tc_ch), F32),
                        pltpu.VMEM((8 * kw, tc_ch), F32)],
        compiler_params=_params(("parallel",)),
    )(z, z, z, dy, conv_w)


def adamw(name, w, m, v, contributions):
    r, c = w.shape
    nc = len(contributions)
    n_slots = contributions[0].shape[0]
    tr = 256 if c <= 1024 else 128
    if any(a.shape[1] % tr for a in contributions):
        assert nc == 1
        tr = r
    tiles = [a.shape[1] // tr for a in contributions]
    first = [sum(tiles[:j]) for j in range(nc)]

    def body(w_ref, m_ref, v_ref, *rest):
        g_refs, (grad_ref, delta_ref, nm_ref, nv_ref) = rest[:nc], rest[nc:]
        i = pl.program_id(0)
        g = None
        for j, g_ref in enumerate(g_refs):
            s = g_ref[0].astype(F32)
            for slot in range(1, n_slots):
                s = s + g_ref[slot].astype(F32)
            g = s if g is None else jnp.where(i >= first[j], s, g)
        nm = ADAM_B1 * m_ref[...] + (1.0 - ADAM_B1) * g
        nv = ADAM_B2 * v_ref[...] + (1.0 - ADAM_B2) * (g * g)
        m_hat = nm / (1.0 - ADAM_B1 ** ADAM_STEP)
        v_hat = nv / (1.0 - ADAM_B2 ** ADAM_STEP)
        grad_ref[...] = g
        delta_ref[...] = -ADAM_LR * (m_hat / (jnp.sqrt(v_hat) + ADAM_EPS) + ADAM_WD * w_ref[...])
        nm_ref[...] = nm
        nv_ref[...] = nv

    blk = pl.BlockSpec((tr, c), lambda i: (i, 0))
    g_specs = [pl.BlockSpec((n_slots, tr, c), lambda i, j=j: (0, jnp.clip(i - first[j], 0, tiles[j] - 1), 0))
               for j in range(nc)]
    return pl.pallas_call(
        body, name=name, grid=(r // tr,),
        in_specs=[blk, blk, blk, *g_specs],
        out_specs=[blk] * 4, out_shape=[jax.ShapeDtypeStruct((r, c), F32)] * 4,
        compiler_params=_params(("parallel",)),
    )(w, m, v, *contributions)


def _pad_rows(a, rows):
    return jnp.pad(a, ((0, rows - a.shape[0]), (0, 0)))


def kernel(x, mix_pre_g, mix_post_g, ffn_pre_g, ffn_post_g, ab_w_in, pool_w, pool_scale, conv_w, conv_b, conv_ln_g, conv_ln_b, ab_w_out, sc_w_in, sc_conv_w, sc_w_out, ffn_w1, ffn_w2, loss_target, m_mix_pre_g, m_mix_post_g, m_ffn_pre_g, m_ffn_post_g, m_ab_w_in, m_pool_w, m_pool_scale, m_conv_w, m_conv_b, m_conv_ln_g, m_conv_ln_b, m_ab_w_out, m_sc_w_in, m_sc_conv_w, m_sc_w_out, m_ffn_w1, m_ffn_w2, v_mix_pre_g, v_mix_post_g, v_ffn_pre_g, v_ffn_post_g, v_ab_w_in, v_pool_w, v_pool_scale, v_conv_w, v_conv_b, v_conv_ln_g, v_conv_ln_b, v_ab_w_out, v_sc_w_in, v_sc_conv_w, v_sc_w_out, v_ffn_w1, v_ffn_w2):
    t, d = x.shape[1], x.shape[2]
    d_pool = pool_scale.shape[1]
    d_conv = conv_b.shape[1]
    d_short = d
    ng, pg = pool_w.shape[1], pool_w.shape[3]
    kw, ks = conv_w.shape[1], sc_conv_w.shape[1]
    nb_ab, nb_sc, nb_ff = ab_w_in.shape[2], sc_w_in.shape[2], ffn_w1.shape[2]

    xs = x[0]
    target = loss_target[0]

    lanes = min(128, d_conv // N_DEV)
    small_rows = [kw * (d_conv // N_DEV) // lanes, ks * (d_short // N_DEV) // lanes, ng * (pg // N_DEV) * pg // lanes]
    small_total = -(-sum(small_rows) // 8) * 8
    r0, r1, r2 = small_rows[0], small_rows[0] + small_rows[1], sum(small_rows)

    def pack_small(a_conv, a_sconv, a_pool):
        parts = [a_conv[0].reshape(-1, lanes), a_sconv[0].reshape(-1, lanes), a_pool[0].reshape(-1, lanes)]
        return _pad_rows(jnp.concatenate(parts, axis=0), small_total)

    shards = {
        "ab_in": (ab_w_in[0], BF16), "small": (pack_small(conv_w, sc_conv_w, pool_w), F32),
        "ab_out": (ab_w_out[0], BF16), "ff1_0": (ffn_w1[0], BF16), "ff2_0": (ffn_w2[0], BF16),
        "sc_in": (sc_w_in[0], BF16), "sc_out": (sc_w_out[0], BF16),
        "ff1_1": (ffn_w1[1], BF16), "ff2_1": (ffn_w2[1], BF16)}
    zones = {nm: place_shard("place_" + nm, w, dt) for nm, (w, dt) in shards.items()}
    direct = ["ab_in", "small", "ab_out"]
    started, token = copies_start("gather_start", [[zones[nm]] for nm in direct], _first_hop, 4)
    started = dict(zip(direct, started))

    ties = [0]

    def after(v, *deps):
        ties[0] += 1
        return tie(f"tie_{ties[0]}", v, *deps)

    def fetch_begin(nm, dep):
        (zone,) = copies_wait("gather_wait_" + nm, started[nm], _first_hop, dep)
        (hop,), tok = copies_start("forward_start_" + nm, [[zone]], _second_hop, 3)
        return hop, tok

    def fetch_end(nm, hop, dep):
        return copies_wait("forward_wait_" + nm, hop, _second_hop, dep)[0]

    ring = {}

    def ring_first(nm, dep):
        (ring[nm],), tok = copies_start("ring1_start_" + nm, [[zones[nm]]], _ring_hop1, 3, deps=[dep])
        return tok

    def ring_second(nm, dep):
        (zone,) = copies_wait("ring1_wait_" + nm, ring[nm], _ring_hop1, dep)
        (ring[nm],), tok = copies_start("ring2_start_" + nm, [[zone]], _ring_hop2, 4)
        return tok

    def ring_third(nm, dep):
        (zone,) = copies_wait("ring2_wait_" + nm, ring[nm], _ring_hop2, dep)
        (ring[nm],), tok = copies_start("ring3_start_" + nm, [[zone]], _ring_hop3, 1)
        return tok

    def ring_done(nm, dep):
        return copies_wait("ring3_wait_" + nm, ring[nm], _ring_hop3, dep)[0]

    relu = lambda r: jnp.maximum(r, 0.0)
    square = lambda a: a * a
    relu2_bwd = lambda r, a: r * (2.0 * a.astype(F32))

    def row(vec, l):
        return vec[l:l + 1]

    tok = ring_first("ff1_0", token)
    tok = ring_first("ff2_0", tok)
    hop_small, _ = fetch_begin("small", tok)
    hop_ab_in, tok = fetch_begin("ab_in", tok)
    w_small = fetch_end("small", hop_small, tok)
    w_ab_in = fetch_end("ab_in", hop_ab_in, tok)
    w_conv = w_small[:, :r0].reshape(N_DEV, kw, -1).transpose(1, 0, 2).reshape(kw, d_conv)
    w_sconv = w_small[:, r0:r1].reshape(N_DEV, ks, -1).transpose(1, 0, 2).reshape(ks, d_short)
    w_pool = w_small[:, r1:r2].reshape(N_DEV, ng, -1, pg).transpose(1, 0, 2, 3).reshape(ng, pg, pg).astype(BF16)
    h0 = norm_pre("norm_pre", xs, row(mix_pre_g, 0))
    z0 = mm_nn_blocked("ab_in", h0, w_ab_in, out_dtype=BF16)
    hop, tok = fetch_begin("ab_out", z0)
    z0 = after(z0, tok)
    pooled, y_pool = pool_fwd("pool_fwd", z0, w_pool, pool_scale, d_pool)
    cv = conv_fwd("conv_fwd", z0, w_conv, conv_b, d_pool, d_conv)
    y_conv = ln_silu("ln_silu", cv, conv_ln_g, conv_ln_b)
    y0 = jnp.concatenate([y_pool, y_conv], axis=1)
    w_ab_out = fetch_end("ab_out", hop, y0)
    tok = ring_second("ff1_0", w_ab_out)
    tok = ring_first("sc_in", tok)
    y0 = after(y0, tok)
    m0 = mm_nn("ab_out", y0, w_ab_out.reshape(d_pool + d_conv, d), out_dtype=F32)
    x1, h1 = post_pre("post_pre_0", xs, m0, row(mix_post_g, 0), row(ffn_pre_g, 0))
    tok = ring_second("ff2_0", h1)
    tok = ring_first("sc_out", tok)
    tok = ring_third("ff1_0", tok)
    w_ff1_0 = ring_done("ff1_0", tok)
    a0 = mm_nn_blocked("ffn0_up", h1, w_ff1_0, out_dtype=BF16, epilogue=relu)
    tok = ring_second("sc_in", a0)
    tok = ring_first("ff1_1", tok)
    tok = ring_third("ff2_0", tok)
    w_ff2_0 = ring_done("ff2_0", tok).reshape(-1, d)
    f0 = mm_nn("ffn0_down", a0, w_ff2_0, out_dtype=F32, tk=2048, lhs_fn=square)
    tok = ring_second("sc_out", f0)
    tok = ring_first("ff2_1", tok)
    tok = ring_third("sc_in", tok)
    f0 = after(f0, tok)
    x2, h2 = post_pre("post_pre_1", x1, f0, row(ffn_post_g, 0), row(mix_pre_g, 1))
    w_sc_in = ring_done("sc_in", h2)
    z1 = mm_nn_blocked("sc_in", h2, w_sc_in, out_dtype=BF16)
    tok = ring_second("ff1_1", z1)
    tok = ring_third("sc_out", tok)
    z1 = after(z1, tok)
    y1 = short_fwd("short_fwd", z1, w_sconv, d_short)
    w_sc_out = ring_done("sc_out", y1).reshape(d_short, d)
    m1 = mm_nn("sc_out", y1, w_sc_out, out_dtype=F32)
    tok = ring_second("ff2_1", m1)
    m1 = after(m1, tok)
    x3, h3 = post_pre("post_pre_2", x2, m1, row(mix_post_g, 1), row(ffn_pre_g, 1))
    tok = ring_third("ff1_1", h3)
    w_ff1_1 = ring_done("ff1_1", tok)
    a1 = mm_nn_blocked("ffn1_up", h3, w_ff1_1, out_dtype=BF16, epilogue=relu)
    tok = ring_third("ff2_1", a1)
    w_ff2_1 = ring_done("ff2_1", tok).reshape(-1, d)
    f1 = mm_nn("ffn1_down", a1, w_ff2_1, out_dtype=F32, tk=2048, lhs_fn=square)
    dx4, df1, loss_part, dg_ffn_post1 = post_loss("post_loss", x3, f1, row(ffn_post_g, 1), target)
    loss = lax.psum(jnp.sum(loss_part) * (0.5 / d), ("x", "y", "c"))

    def reduce_begin(tag, g):
        zone = lax.empty((N_CHIP,) + g.shape[1:], g.dtype)
        (hop,), tok = copies_start("pair_start_" + tag, [[g, zone]], _pair_hop, N_CHIP)
        return hop, tok

    def reduce_middle(tag, hop, dep):
        g, from_sibling = copies_wait("pair_wait_" + tag, hop, _pair_hop, dep)
        pair_sum, zone = pair_add("pair_add_" + tag, g, from_sibling)
        (hop2,), tok = copies_start("chips_start_" + tag, [[pair_sum, zone]], _chip_hop, 3)
        return hop2, tok

    def reduce_end(tag, hop2, dep):
        return copies_wait("chips_wait_" + tag, hop2, _chip_hop, dep)[1]

    dw = mm_tn("ffn1_dw2", a1, df1, out_dtype=BF16, lhs_fn=square)
    red_ff2_1, tok = reduce_begin("ff2_1", dw.reshape(N_DEV, -1, d))
    df1 = after(df1, tok)
    dpre = mm_nt("ffn1_da", df1, w_ff2_1, out_dtype=BF16, extra=a1, epilogue=relu2_bwd)
    dw = mm_tn_blocked("ffn1_dw1", h3, dpre, nb_ff, out_dtype=BF16)
    red_ff1_1, tok = reduce_begin("ff1_1", dw)
    dpre = after(dpre, tok)
    dh3 = mm_nt_blocked("ffn1_dh", dpre, w_ff1_1, out_dtype=BF16)
    red_ff2_1, tok = reduce_middle("ff2_1", red_ff2_1, dh3)
    dh3 = after(dh3, tok)
    dx3, dm1, dg_ffn_pre1, dg_mix_post1 = bwd_pre_post("bwd_3", dx4, x3, row(ffn_pre_g, 1), dh3, m1, row(mix_post_g, 1))

    dw = mm_tn("sc_dwout", y1, dm1, out_dtype=BF16)
    red_sc_out, tok = reduce_begin("sc_out", dw.reshape(N_DEV, -1, d))
    dm1 = after(dm1, tok)
    dy1 = mm_nt("sc_dy", dm1, w_sc_out, out_dtype=BF16)
    red_ff1_1, tok = reduce_middle("ff1_1", red_ff1_1, dy1)
    dy1 = after(dy1, tok)
    db1, dcg1, du1, dw_sconv = short_bwd("short_bwd", z1, dy1, w_sconv, d_short)
    dz1 = jnp.concatenate([db1, dcg1, du1], axis=1)
    dw = mm_tn_blocked("sc_dwin", h2, dz1, nb_sc, out_dtype=BF16)
    red_sc_in, tok = reduce_begin("sc_in", dw)
    dz1 = after(dz1, tok)
    dh2 = mm_nt_blocked("sc_dh", dz1, w_sc_in, out_dtype=BF16)
    red_sc_out, tok = reduce_middle("sc_out", red_sc_out, dh2)
    dh2 = after(dh2, tok)
    dx2, df0, dg_mix_pre1, dg_ffn_post0 = bwd_pre_post("bwd_2", dx3, x2, row(mix_pre_g, 1), dh2, f0, row(ffn_post_g, 0))

    dw = mm_tn("ffn0_dw2", a0, df0, out_dtype=BF16, lhs_fn=square)
    red_ff2_0, tok = reduce_begin("ff2_0", dw.reshape(N_DEV, -1, d))
    df0 = after(df0, tok)
    dpre = mm_nt("ffn0_da", df0, w_ff2_0, out_dtype=BF16, extra=a0, epilogue=relu2_bwd)
    red_sc_in, tok = reduce_middle("sc_in", red_sc_in, dpre)
    dpre = after(dpre, tok)
    dw = mm_tn_blocked("ffn0_dw1", h1, dpre, nb_ff, out_dtype=BF16)
    red_ff1_0, tok = reduce_begin("ff1_0", dw)
    dpre = after(dpre, tok)
    dh1 = mm_nt_blocked("ffn0_dh", dpre, w_ff1_0, out_dtype=BF16)
    red_ff2_0, tok = reduce_middle("ff2_0", red_ff2_0, dh1)
    dh1 = after(dh1, tok)
    dx1, dm0, dg_ffn_pre0, dg_mix_post0 = bwd_pre_post("bwd_1", dx2, x1, row(ffn_pre_g, 0), dh1, m0, row(mix_post_g, 0))

    dw = mm_tn("ab_dwout", y0, dm0, out_dtype=BF16)
    red_ab_out, tok = reduce_begin("ab_out", dw.reshape(N_DEV, -1, d))
    dm0 = after(dm0, tok)
    dy0 = mm_nt("ab_dy", dm0, w_ab_out.reshape(d_pool + d_conv, d), out_dtype=BF16)
    red_ff1_0, tok = reduce_middle("ff1_0", red_ff1_0, dy0)
    dy0 = after(dy0, tok)
    dcv, dg_ln_g, dg_ln_b = ln_silu_bwd("ln_silu_bwd", cv, conv_ln_g, conv_ln_b, dy0, d_pool // d_conv)
    dv, dgate, dw_conv, dg_conv_b = conv_bwd("conv_bwd", z0, dcv, w_conv, d_pool, d_conv)
    du0, dw_pool, dg_pool_scale = pool_bwd("pool_bwd", pooled, dy0, w_pool, pool_scale)
    dz0 = jnp.concatenate([du0, dv, dgate], axis=1)
    small_parts = [
        dw_conv.reshape(kw, N_DEV, -1).transpose(1, 0, 2).reshape(N_DEV, -1, lanes),
        dw_sconv.reshape(ks, N_DEV, -1).transpose(1, 0, 2).reshape(N_DEV, -1, lanes),
        dw_pool.reshape(ng, N_DEV, pg // N_DEV, pg).transpose(1, 0, 2, 3).reshape(N_DEV, -1, lanes),
    ]
    small = jnp.pad(jnp.concatenate(small_parts, axis=1), ((0, 0), (0, small_total - r2), (0, 0)))
    red_small, tok = reduce_begin("small", small)
    red_ab_out, tok2 = reduce_middle("ab_out", red_ab_out, dz0)
    dz0 = after(dz0, tok, tok2)
    dw = mm_tn_blocked("ab_dwin", h0, dz0, nb_ab, out_dtype=BF16)
    red_ab_in, tok = reduce_begin("ab_in", dw)
    dz0 = after(dz0, tok)
    dh0 = mm_nt_blocked("ab_dh", dz0, w_ab_in, out_dtype=BF16)
    red_small, tok = reduce_middle("small", red_small, dh0)
    dh0 = after(dh0, tok)
    grad_x, dg_mix_pre0 = bwd_pre_final("bwd_0", dx1, xs, row(mix_pre_g, 0), dh0)
    red_ab_in, tok = reduce_middle("ab_in", red_ab_in, grad_x)

    fold = lambda a: jnp.sum(a, axis=0, keepdims=True)
    rep_rows = [fold(dg_mix_pre0), fold(dg_mix_pre1), fold(dg_mix_post0), fold(dg_mix_post1),
                fold(dg_ffn_pre0), fold(dg_ffn_pre1), fold(dg_ffn_post0), fold(dg_ffn_post1)]
    tail = jnp.concatenate([dg_pool_scale, dg_conv_b, fold(dg_ln_g), fold(dg_ln_b)], axis=1).reshape(-1, d)
    rep = _pad_rows(jnp.concatenate(rep_rows + [tail], axis=0), 16)
    (rep_hop,), _ = copies_start("rep_start", [[place_shard("place_rep", rep, F32)]], _first_hop, 4)

    def pack_rep(a_mix_pre, a_mix_post, a_ffn_pre, a_ffn_post, a_scale, a_b, a_g, a_lb):
        tail_ = jnp.concatenate([a_scale, a_b, a_g, a_lb], axis=1).reshape(-1, d)
        return _pad_rows(jnp.concatenate([a_mix_pre, a_mix_post, a_ffn_pre, a_ffn_post, tail_], axis=0), 16)

    def upd(name, w, m, v, contribs):
        shape = w.shape
        flat2 = lambda a: a.reshape(-1, shape[-1])
        outs = adamw(name, flat2(w), flat2(m), flat2(v), contribs)
        return [o.reshape(shape) for o in outs]

    g_ff2 = [reduce_end("ff2_0", red_ff2_0, tok), reduce_end("ff2_1", red_ff2_1, tok)]
    o_ff2 = upd("adam_ffn_w2", ffn_w2, m_ffn_w2, v_ffn_w2, g_ff2)
    (rep_zone,) = copies_wait("rep_wait", rep_hop, _first_hop, o_ff2[0])
    (rep_hop,), _ = copies_start("rep_forward_start", [[rep_zone]], _second_hop, 3)
    g_ff1 = [reduce_end("ff1_0", red_ff1_0, o_ff2[0]), reduce_end("ff1_1", red_ff1_1, o_ff2[0])]
    o_ff1 = upd("adam_ffn_w1", ffn_w1, m_ffn_w1, v_ffn_w1, g_ff1)
    (rep_all,) = copies_wait("rep_forward_wait", rep_hop, _second_hop, o_ff1[0])
    o_rep = adamw("adam_replicated",
                  pack_rep(mix_pre_g, mix_post_g, ffn_pre_g, ffn_post_g, pool_scale, conv_b, conv_ln_g, conv_ln_b),
                  pack_rep(m_mix_pre_g, m_mix_post_g, m_ffn_pre_g, m_ffn_post_g, m_pool_scale, m_conv_b, m_conv_ln_g, m_conv_ln_b),
                  pack_rep(v_mix_pre_g, v_mix_post_g, v_ffn_pre_g, v_ffn_post_g, v_pool_scale, v_conv_b, v_conv_ln_g, v_conv_ln_b),
                  [rep_all])
    o_sc_out = upd("adam_sc_out", sc_w_out, m_sc_w_out, v_sc_w_out, [reduce_end("sc_out", red_sc_out, o_ff1[0])])
    o_sc_in = upd("adam_sc_in", sc_w_in, m_sc_w_in, v_sc_w_in, [reduce_end("sc_in", red_sc_in, o_sc_out[0])])
    o_ab_out = upd("adam_ab_out", ab_w_out, m_ab_w_out, v_ab_w_out, [reduce_end("ab_out", red_ab_out, o_sc_in[0])])
    o_small = adamw("adam_small", pack_small(conv_w, sc_conv_w, pool_w), pack_small(m_conv_w, m_sc_conv_w, m_pool_w),
                    pack_small(v_conv_w, v_sc_conv_w, v_pool_w), [reduce_end("small", red_small, o_ab_out[0])])
    o_ab_in = upd("adam_ab_in", ab_w_in, m_ab_w_in, v_ab_w_in, [reduce_end("ab_in", red_ab_in, o_small[0])])

    def unpack_small(o):
        return o[:r0].reshape(conv_w.shape), o[r0:r1].reshape(sc_conv_w.shape), o[r1:r2].reshape(pool_w.shape)

    def unpack_rep(o):
        tail_ = o[8:8 + tail.shape[0]].reshape(1, -1)
        n1 = d_pool
        return dict(mix_pre_g=o[0:2], mix_post_g=o[2:4], ffn_pre_g=o[4:6], ffn_post_g=o[6:8],
                    pool_scale=tail_[:, :n1], conv_b=tail_[:, n1:n1 + d_conv],
                    conv_ln_g=tail_[:, n1 + d_conv:n1 + 2 * d_conv], conv_ln_b=tail_[:, n1 + 2 * d_conv:n1 + 3 * d_conv])

    results = []
    for kind in range(4):
        rep_o = unpack_rep(o_rep[kind])
        s_conv, s_sconv, s_pool = unpack_small(o_small[kind])
        results.append([
            rep_o["mix_pre_g"], rep_o["mix_post_g"], rep_o["ffn_pre_g"], rep_o["ffn_post_g"],
            o_ab_in[kind], s_pool, rep_o["pool_scale"], s_conv, rep_o["conv_b"], rep_o["conv_ln_g"], rep_o["conv_ln_b"],
            o_ab_out[kind], o_sc_in[kind], s_sconv, o_sc_out[kind], o_ff1[kind], o_ff2[kind]])

    return (loss, grad_x[None], *results[0], *results[1], *results[2], *results[3])
```

```python
import jax
import jax.numpy as jnp
from jax import lax
from jax.experimental import pallas as pl
from jax.experimental.pallas import tpu as pltpu

F32 = jnp.float32
BF16 = jnp.bfloat16
MESH = pl.DeviceIdType.MESH
ANY = pl.BlockSpec(memory_space=pl.ANY)

NORM_EPS = 1e-6
POOL_WINDOWS = (2, 4, 8, 16)
ADAM_LR = 0.001
ADAM_B1 = 0.9
ADAM_B2 = 0.999
ADAM_EPS = 1e-08
ADAM_WD = 0.01
ADAM_STEP = 10

N_DEV = 8
VMEM_LIMIT = 56 * 1024 * 1024
PAIR_ADD_BLOCK = 1 << 20
MATMUL_ROWS = 2048
ROW_TILE = 256
CHANNEL_TILE = 256
TIME_CHUNK = 64
HALO = 32

NN = (((1,), (0,)), ((), ()))
NT = (((1,), (1,)), ((), ()))
TN = (((0,), (0,)), ((), ()))


def _params(sem):
    return pltpu.CompilerParams(dimension_semantics=sem, vmem_limit_bytes=VMEM_LIMIT)


def _place():
    x, y, c = lax.axis_index("x"), lax.axis_index("y"), lax.axis_index("c")
    return x, y, c


def _slot(px, py, pc):
    return 4 * px + 2 * py + pc


HBM = pl.BlockSpec(memory_space=pltpu.HBM)
SEM = pl.BlockSpec(memory_space=pltpu.SEMAPHORE)
EFFECT = pltpu.SideEffectType.DATAFLOW_SIDE_EFFECTING
TOKEN = jax.ShapeDtypeStruct((8, 128), F32)


def _in_hbm(a):
    return pltpu.with_memory_space_constraint(a, pltpu.HBM)


CHIPS = [(0, 0), (0, 1), (1, 0), (1, 1)]
N_CHIP = len(CHIPS)


def _chip(px, py):
    return 2 * px + py


def _first_hop(bufs, sends, recvs, waiting):
    (land,) = bufs
    x, y, c = _place()
    me = _slot(x, y, c)
    peers = [(x, y, 1 - c), (1 - x, y, c), (x, 1 - y, c), (1 - x, 1 - y, c)]
    return [pltpu.make_async_remote_copy(
        src_ref=land.at[me], dst_ref=land.at[_slot(*p) if waiting else me],
        send_sem=sends.at[k], recv_sem=recvs.at[k], device_id=p, device_id_type=MESH) for k, p in enumerate(peers)]


def _second_hop(bufs, sends, recvs, waiting):
    (land,) = bufs
    x, y, c = _place()
    return [pltpu.make_async_remote_copy(
        src_ref=land.at[_slot(px, py, c)], dst_ref=land.at[_slot(px, py, 1 - c if waiting else c)],
        send_sem=sends.at[k], recv_sem=recvs.at[k], device_id=(x, y, 1 - c), device_id_type=MESH)
        for k, (px, py) in enumerate([(1 - x, y), (x, 1 - y), (1 - x, 1 - y)])]


def _ring_hop1(bufs, sends, recvs, waiting):
    (land,) = bufs
    x, y, c = _place()
    me = _slot(x, y, c)
    peers = [(1 - x, y, c), (x, 1 - y, c), (x, y, 1 - c)]
    return [pltpu.make_async_remote_copy(
        src_ref=land.at[me], dst_ref=land.at[_slot(*p) if waiting else me],
        send_sem=sends.at[k], recv_sem=recvs.at[k], device_id=p, device_id_type=MESH) for k, p in enumerate(peers)]


def _ring_hop2(bufs, sends, recvs, waiting):
    (land,) = bufs
    x, y, c = _place()
    half = land.shape[1] // 2
    first, second = pl.ds(0, half), pl.ds(half, half)
    nx, ny, diag = _slot(1 - x, y, c), _slot(x, 1 - y, c), _slot(1 - x, 1 - y, c)
    plan = [
        (land.at[ny, first], land.at[diag, first], (1 - x, y, c)),
        (land.at[nx, second], land.at[diag, second], (x, 1 - y, c)),
        (land.at[nx], land.at[_slot(1 - x, y, 1 - c)], (x, y, 1 - c)),
        (land.at[ny], land.at[_slot(x, 1 - y, 1 - c)], (x, y, 1 - c))]
    return [pltpu.make_async_remote_copy(
        src_ref=src, dst_ref=mine if waiting else src, send_sem=sends.at[k], recv_sem=recvs.at[k],
        device_id=to, device_id_type=MESH) for k, (src, mine, to) in enumerate(plan)]


def _ring_hop3(bufs, sends, recvs, waiting):
    (land,) = bufs
    x, y, c = _place()
    return [pltpu.make_async_remote_copy(
        src_ref=land.at[_slot(1 - x, 1 - y, c)], dst_ref=land.at[_slot(1 - x, 1 - y, 1 - c if waiting else c)],
        send_sem=sends.at[0], recv_sem=recvs.at[0], device_id=(x, y, 1 - c), device_id_type=MESH)]


def _pair_hop(bufs, sends, recvs, waiting):
    g, land = bufs
    x, y, c = _place()
    return [pltpu.make_async_remote_copy(
        src_ref=g.at[_slot(qx, qy, 1 - c)], dst_ref=land.at[q],
        send_sem=sends.at[q], recv_sem=recvs.at[q], device_id=(x, y, 1 - c), device_id_type=MESH)
        for q, (qx, qy) in enumerate(CHIPS)]


def _chip_hop(bufs, sends, recvs, waiting):
    p, land = bufs
    x, y, c = _place()
    return [pltpu.make_async_remote_copy(
        src_ref=p.at[_chip(px, py)], dst_ref=land.at[_chip(px, py) if waiting else _chip(x, y)],
        send_sem=sends.at[k], recv_sem=recvs.at[k], device_id=(px, py, c), device_id_type=MESH)
        for k, (px, py) in enumerate([(1 - x, y), (x, 1 - y), (1 - x, 1 - y)])]


def copies_start(name, groups, hop, n_copies, deps=()):
    flat = [b for grp in groups for b in grp]
    nb, ng = len(flat), len(groups)
    deps = list(deps)
    hops = list(hop) if isinstance(hop, (list, tuple)) else [hop] * ng
    counts = list(n_copies) if isinstance(n_copies, (list, tuple)) else [n_copies] * ng

    def body(*refs):
        ins, token = refs[:nb], refs[-1]
        sems = refs[nb + len(deps):nb + len(deps) + 2 * ng]
        i = 0
        for gi, grp in enumerate(groups):
            for cp in hops[gi](ins[i:i + len(grp)], sems[2 * gi], sems[2 * gi + 1], False):
                cp.start()
            i += len(grp)
        token[...] = jnp.zeros_like(token)

    outs = pl.pallas_call(
        body, name=name,
        out_shape=([pltpu.SemaphoreType.DMA((n,)) for n in counts for _ in range(2)]
                   + [pltpu.HBM(b.shape, b.dtype) for b in flat] + [TOKEN]),
        in_specs=[HBM] * nb + [ANY] * len(deps),
        out_specs=[SEM] * (2 * ng) + [HBM] * nb + [pl.BlockSpec(memory_space=pltpu.VMEM)],
        input_output_aliases={i: 2 * ng + i for i in range(nb)},
        compiler_params=pltpu.CompilerParams(has_side_effects=EFFECT),
    )(*[_in_hbm(b) for b in flat], *deps)
    started, i = [], 0
    for gi, grp in enumerate(groups):
        started.append((outs[2 * gi], outs[2 * gi + 1], list(outs[2 * ng + i:2 * ng + i + len(grp)])))
        i += len(grp)
    return started, outs[-1]


def copies_wait(name, started, hop, after):
    sends, recvs, bufs = started
    nb = len(bufs)

    def body(*refs):
        for cp in hop(refs[:nb], refs[nb], refs[nb + 1], True):
            cp.wait_send()
            cp.wait_recv()

    outs = pl.pallas_call(
        body, name=name,
        out_shape=[pltpu.HBM(b.shape, b.dtype) for b in bufs],
        in_specs=[HBM] * nb + [SEM, SEM, ANY], out_specs=[HBM] * nb,
        input_output_aliases={i: i for i in range(nb)},
        compiler_params=pltpu.CompilerParams(has_side_effects=EFFECT),
    )(*bufs, sends, recvs, after)
    return list(outs)


def place_shard(name, w, layer, dtype):
    _, r, c = w.shape
    tr = _tile(r, 1024)
    x, y, core = _place()
    me = _slot(x, y, core).astype(jnp.int32).reshape(1)

    def body(me_ref, w_ref, o_ref):
        o_ref[...] = w_ref[...].astype(dtype)

    return pl.pallas_call(
        body, name=name,
        grid_spec=pltpu.PrefetchScalarGridSpec(
            num_scalar_prefetch=1, grid=(r // tr,),
            in_specs=[pl.BlockSpec((None, tr, c), lambda i, me_ref: (layer, i, 0))],
            out_specs=pl.BlockSpec((None, tr, c), lambda i, me_ref: (me_ref[0], i, 0))),
        out_shape=jax.ShapeDtypeStruct((N_DEV, r, c), dtype),
        compiler_params=_params(("parallel",)),
    )(me, w)


def tie(name, x, *deps):
    def body(*refs):
        del refs

    return pl.pallas_call(
        body, name=name, out_shape=jax.ShapeDtypeStruct(x.shape, x.dtype),
        in_specs=[ANY] * (1 + len(deps)), out_specs=ANY, input_output_aliases={0: 0},
    )(x, *deps)


def pair_add(name, g, from_sibling):
    _, r, c_dim = g.shape
    tr = r
    while tr * c_dim > PAIR_ADD_BLOCK and tr % 16 == 0:
        tr //= 2
    x, y, core = _place()
    where = jnp.stack([core, _chip(x, y)]).astype(jnp.int32)

    def body(where_ref, g_ref, s_ref, o_ref, zone_ref):
        total = (g_ref[...].astype(F32) + s_ref[...].astype(F32)).astype(o_ref.dtype)
        o_ref[...] = total

        @pl.when(pl.program_id(1) == where_ref[1])
        def _():
            zone_ref[...] = total

    blk = pl.BlockSpec((None, tr, c_dim), lambda i, q, where_ref: (q, i, 0))
    return pl.pallas_call(
        body, name=name,
        grid_spec=pltpu.PrefetchScalarGridSpec(
            num_scalar_prefetch=1, grid=(r // tr, N_CHIP),
            in_specs=[pl.BlockSpec((None, None, tr, c_dim), lambda i, q, where_ref: (q, where_ref[0], i, 0)), blk],
            out_specs=[blk, pl.BlockSpec((None, tr, c_dim), lambda i, q, where_ref: (where_ref[1], i, 0))]),
        out_shape=[jax.ShapeDtypeStruct((N_CHIP, r, c_dim), g.dtype)] * 2,
        compiler_params=_params(("parallel", "arbitrary")),
    )(where, g.reshape(N_CHIP, 2, r, c_dim), from_sibling)


def _matmul(name, lhs, rhs, *, out_shape, out_dtype, grid, lhs_spec, rhs_spec, out_spec, dims, acc_shape,
            lhs_fn=None, extra=(), extra_specs=(), epilogue=None, parts=1):
    nk = grid[2]
    n_extra = len(extra)

    def body(*refs):
        lhs_ref, rhs_ref = refs[0], refs[1]
        extra_refs = refs[2:2 + n_extra]
        out_ref = refs[2 + n_extra]

        def product():
            if parts == 1:
                a = lhs_ref[...]
                if lhs_fn is not None:
                    a = lhs_fn(a)
                return lax.dot_general(a, rhs_ref[...], dims, preferred_element_type=F32)
            width = lhs_ref.shape[1] // parts
            total = None
            for b in range(parts):
                term = lax.dot_general(lhs_ref[:, b * width:(b + 1) * width], rhs_ref[b], dims,
                                       preferred_element_type=F32)
                total = term if total is None else total + term
            return total

        def finish(r):
            if epilogue is not None:
                r = epilogue(r, *[e[...] for e in extra_refs])
            out_ref[...] = r.astype(out_dtype)

        if nk == 1:
            finish(product())
        else:
            acc_ref = refs[3 + n_extra]
            k = pl.program_id(2)

            @pl.when(k == 0)
            def _():
                acc_ref[...] = product()

            @pl.when(jnp.logical_and(k > 0, k < nk - 1))
            def _():
                acc_ref[...] += product()

            @pl.when(k == nk - 1)
            def _():
                finish(acc_ref[...] + product())

    return pl.pallas_call(
        body, name=name, grid=grid,
        out_shape=jax.ShapeDtypeStruct(out_shape, out_dtype),
        in_specs=[lhs_spec, rhs_spec, *extra_specs], out_specs=out_spec,
        scratch_shapes=[pltpu.VMEM(acc_shape, F32)] if nk > 1 else [],
        compiler_params=_params(("parallel", "parallel", "arbitrary")),
    )(lhs, rhs, *extra)


def _tile(n, want):
    return want if n % want == 0 else n


def mm_nn(name, x, w, *, out_dtype, tn=512, tk=None, lhs_fn=None, epilogue=None):
    t, kdim = x.shape
    n = w.shape[1]
    tm, tn = _tile(t, MATMUL_ROWS), _tile(n, tn)
    tk = kdim if tk is None else _tile(kdim, tk)
    return _matmul(
        name, x, w, out_shape=(t, n), out_dtype=out_dtype, grid=(t // tm, n // tn, kdim // tk),
        lhs_spec=pl.BlockSpec((tm, tk), lambda i, j, k: (i, k)),
        rhs_spec=pl.BlockSpec((tk, tn), lambda i, j, k: (k, j)),
        out_spec=pl.BlockSpec((tm, tn), lambda i, j, k: (i, j)),
        dims=NN, acc_shape=(tm, tn), lhs_fn=lhs_fn, epilogue=epilogue)


def mm_nn_blocked(name, x, w, *, out_dtype, epilogue=None):
    t, kdim = x.shape
    nb = w.shape[2]
    tm = _tile(t, MATMUL_ROWS)
    tn = nb // 2 if nb >= 1024 else nb
    sub = nb // tn
    return _matmul(
        name, x, w, out_shape=(t, N_DEV * nb), out_dtype=out_dtype, grid=(t // tm, N_DEV * sub, 1),
        lhs_spec=pl.BlockSpec((tm, kdim), lambda i, j, k: (i, k)),
        rhs_spec=pl.BlockSpec((None, kdim, tn), lambda i, j, k: (j // sub, k, j % sub)),
        out_spec=pl.BlockSpec((tm, tn), lambda i, j, k: (i, j)),
        dims=NN, acc_shape=(tm, tn), epilogue=epilogue)


def mm_nt(name, dy, w, *, out_dtype, tn=512, extra=None, epilogue=None):
    t, n = dy.shape
    kdim = w.shape[0]
    tm, tn = _tile(t, MATMUL_ROWS), _tile(kdim, tn)
    extra_arrs = () if extra is None else (extra,)
    extra_specs = () if extra is None else (pl.BlockSpec((tm, tn), lambda i, j, k: (i, j)),)
    return _matmul(
        name, dy, w, out_shape=(t, kdim), out_dtype=out_dtype, grid=(t // tm, kdim // tn, 1),
        lhs_spec=pl.BlockSpec((tm, n), lambda i, j, k: (i, k)),
        rhs_spec=pl.BlockSpec((tn, n), lambda i, j, k: (j, k)),
        out_spec=pl.BlockSpec((tm, tn), lambda i, j, k: (i, j)),
        dims=NT, acc_shape=(tm, tn), extra=extra_arrs, extra_specs=extra_specs, epilogue=epilogue)


def mm_nt_blocked(name, dz, w, *, out_dtype, tn=512):
    t = dz.shape[0]
    kdim, nb = w.shape[1], w.shape[2]
    tm, tn = _tile(t, MATMUL_ROWS), _tile(kdim, tn)
    parts = 2
    return _matmul(
        name, dz, w, out_shape=(t, kdim), out_dtype=out_dtype, grid=(t // tm, kdim // tn, N_DEV // parts),
        lhs_spec=pl.BlockSpec((tm, parts * nb), lambda i, j, k: (i, k)),
        rhs_spec=pl.BlockSpec((parts, tn, nb), lambda i, j, k: (k, j, 0)),
        out_spec=pl.BlockSpec((tm, tn), lambda i, j, k: (i, j)),
        dims=NT, acc_shape=(tm, tn), parts=parts)


def mm_tn(name, x, dy, *, out_dtype, tk=1024, tn=1024, lhs_fn=None):
    t, kdim = x.shape
    n = dy.shape[1]
    tk, tn = _tile(kdim, tk), _tile(n, tn)
    return _matmul(
        name, x, dy, out_shape=(kdim, n), out_dtype=out_dtype, grid=(kdim // tk, n // tn, 1),
        lhs_spec=pl.BlockSpec((t, tk), lambda i, j, k: (k, i)),
        rhs_spec=pl.BlockSpec((t, tn), lambda i, j, k: (k, j)),
        out_spec=pl.BlockSpec((tk, tn), lambda i, j, k: (i, j)),
        dims=TN, acc_shape=(tk, tn), lhs_fn=lhs_fn)


def mm_tn_blocked(name, x, dz, nb, *, out_dtype, tk=1024):
    t, kdim = x.shape
    tk = _tile(kdim, tk)
    return _matmul(
        name, x, dz, out_shape=(N_DEV, kdim, nb), out_dtype=out_dtype, grid=(kdim // tk, N_DEV, 1),
        lhs_spec=pl.BlockSpec((t, tk), lambda i, j, k: (k, i)),
        rhs_spec=pl.BlockSpec((t, nb), lambda i, j, k: (k, j)),
        out_spec=pl.BlockSpec((None, tk, nb), lambda i, j, k: (j, i, 0)),
        dims=TN, acc_shape=(tk, nb))


def _rstd(v):
    return lax.rsqrt(jnp.mean(v * v, axis=-1, keepdims=True) + NORM_EPS)


def _rms_bwd(v, g, dy):
    r = _rstd(v)
    vhat = v * r
    dvh = dy * g
    dv = r * (dvh - vhat * jnp.mean(dvh * vhat, axis=-1, keepdims=True))
    return dv, dy * vhat


def _fold8(v):
    rows, n = v.shape
    return jnp.sum(v.reshape(rows // 8, 8, n), axis=0)


def _fold_lanes(v):
    out = v[:, 0:128]
    for i in range(1, v.shape[1] // 128):
        out = out + v[:, 128 * i:128 * (i + 1)]
    return out


def _accumulate(ref, v):
    i = pl.program_id(0)

    @pl.when(i == 0)
    def _():
        ref[...] = v

    @pl.when(i > 0)
    def _():
        ref[...] += v


def _row_call(body, name, t, ins, row_in, outs, acc_outs=(), tr=ROW_TILE):
    tr = _tile(t, tr)

    def in_spec(a, tiled):
        if isinstance(tiled, tuple):
            width, j = tiled
            return pl.BlockSpec((tr, width), lambda i: (i, j))
        return pl.BlockSpec((tr, a.shape[1]), lambda i: (i, 0)) if tiled else pl.BlockSpec(a.shape, lambda i: (0, 0))

    in_specs = [in_spec(a, tiled) for a, tiled in zip(ins, row_in)]
    out_specs = [pl.BlockSpec((tr, n), lambda i: (i, 0)) for n, _ in outs]
    out_specs += [pl.BlockSpec((8, n), lambda i: (0, 0)) for n in acc_outs]
    out_shape = [jax.ShapeDtypeStruct((t, n), dt) for n, dt in outs]
    out_shape += [jax.ShapeDtypeStruct((8, n), F32) for n in acc_outs]
    return pl.pallas_call(
        body, name=name, grid=(t // tr,), in_specs=in_specs, out_specs=out_specs, out_shape=out_shape,
        compiler_params=_params(("arbitrary",) if acc_outs else ("parallel",)),
    )(*ins)


def norm_pre(name, x, g):
    t, d = x.shape

    def body(x_ref, g_ref, h_ref):
        v = x_ref[...]
        h_ref[...] = (v * _rstd(v) * g_ref[...]).astype(BF16)

    return _row_call(body, name, t, [x, g], [True, False], [(d, BF16)])[0]


def post_pre(name, x, m, g_post, g_pre):
    t, d = x.shape

    def body(x_ref, m_ref, gp_ref, gn_ref, xo_ref, h_ref):
        mv = m_ref[...]
        xn = x_ref[...] + mv * _rstd(mv) * gp_ref[...]
        xo_ref[...] = xn
        h_ref[...] = (xn * _rstd(xn) * gn_ref[...]).astype(BF16)

    return _row_call(body, name, t, [x, m, g_post, g_pre], [True, True, False, False], [(d, F32), (d, BF16)])


def post_loss(name, x, f, g_post, target):
    t, d = x.shape

    def body(x_ref, f_ref, g_ref, t_ref, dx_ref, df_ref, loss_ref, dg_ref):
        fv = f_ref[...]
        g = g_ref[...]
        out = x_ref[...] + fv * _rstd(fv) * g
        err = out - t_ref[...]
        dx = err * (1.0 / d)
        dx_ref[...] = dx
        dfv, dg_rows = _rms_bwd(fv, g, dx)
        df_ref[...] = dfv.astype(BF16)
        _accumulate(loss_ref, _fold8(_fold_lanes(err * err)))
        _accumulate(dg_ref, _fold8(dg_rows))

    return _row_call(body, name, t, [x, f, g_post, target], [True, True, False, True],
                     [(d, F32), (d, BF16)], acc_outs=(128, d))


def bwd_pre_post(name, dx_out, x_in, g_pre, dh, f_prev, g_post_prev):
    t, d = x_in.shape

    def body(dxo_ref, x_ref, gpre_ref, dh_ref, f_ref, gpost_ref, dxi_ref, df_ref, dgpre_ref, dgpost_ref):
        dxv, dgpre_rows = _rms_bwd(x_ref[...], gpre_ref[...], dh_ref[...].astype(F32))
        dxi = dxo_ref[...] + dxv
        dxi_ref[...] = dxi
        dfv, dgpost_rows = _rms_bwd(f_ref[...], gpost_ref[...], dxi)
        df_ref[...] = dfv.astype(BF16)
        _accumulate(dgpre_ref, _fold8(dgpre_rows))
        _accumulate(dgpost_ref, _fold8(dgpost_rows))

    return _row_call(body, name, t, [dx_out, x_in, g_pre, dh, f_prev, g_post_prev],
                     [True, True, False, True, True, False], [(d, F32), (d, BF16)], acc_outs=(d, d))


def bwd_pre_final(name, dx_out, x_in, g_pre, dh):
    t, d = x_in.shape

    def body(dxo_ref, x_ref, gpre_ref, dh_ref, dxi_ref, dgpre_ref):
        dxv, dgpre_rows = _rms_bwd(x_ref[...], gpre_ref[...], dh_ref[...].astype(F32))
        dxi_ref[...] = dxo_ref[...] + dxv
        _accumulate(dgpre_ref, _fold8(dgpre_rows))

    return _row_call(body, name, t, [dx_out, x_in, g_pre, dh], [True, True, False, True], [(d, F32)], acc_outs=(d,))


def _layer_norm_parts(cv):
    mu = jnp.mean(cv, axis=-1, keepdims=True)
    xc = cv - mu
    rstd = lax.rsqrt(jnp.mean(xc * xc, axis=-1, keepdims=True) + NORM_EPS)
    return xc * rstd, rstd


def ln_silu(name, cv, g, b, y, y_block):
    t, n = cv.shape
    tr = _tile(t, ROW_TILE)

    def body(c_ref, g_ref, b_ref, y_in_ref, y_ref):
        chat, _ = _layer_norm_parts(c_ref[...])
        ln = chat * g_ref[...] + b_ref[...]
        y_ref[...] = (ln * jax.nn.sigmoid(ln)).astype(BF16)

    vec = pl.BlockSpec((1, n), lambda i: (0, 0))
    return pl.pallas_call(
        body, name=name, grid=(t // tr,),
        in_specs=[pl.BlockSpec((tr, n), lambda i: (i, 0)), vec, vec, ANY],
        out_specs=pl.BlockSpec((tr, n), lambda i: (i, y_block)),
        out_shape=jax.ShapeDtypeStruct(y.shape, y.dtype), input_output_aliases={3: 0},
        compiler_params=_params(("parallel",)),
    )(cv, g, b, y)


def ln_silu_bwd(name, cv, g, b, dy, dy_block):
    t, n = cv.shape

    def body(c_ref, g_ref, b_ref, dy_ref, dc_ref, dg_ref, db_ref):
        chat, rstd = _layer_norm_parts(c_ref[...])
        g = g_ref[...]
        ln = chat * g + b_ref[...]
        s = jax.nn.sigmoid(ln)
        dln = dy_ref[...].astype(F32) * (s * (1.0 + ln * (1.0 - s)))
        dchat = dln * g
        dc_ref[...] = rstd * (dchat - jnp.mean(dchat, axis=-1, keepdims=True)
                              - chat * jnp.mean(dchat * chat, axis=-1, keepdims=True))
        _accumulate(dg_ref, _fold8(dln * chat))
        _accumulate(db_ref, _fold8(dln))

    return _row_call(body, name, t, [cv, g, b, dy], [True, False, False, (n, dy_block)], [(n, F32)], acc_outs=(n, n))


def _chunks(t, fn, tc=TIME_CHUNK):
    tc = _tile(t, tc)

    def step(i, carry):
        fn(pl.multiple_of(i * tc, tc), tc)
        return carry

    lax.fori_loop(0, t // tc, step, 0)


def _shifted(window, offsets, tc):
    by_residue = {}
    for k, off in enumerate(offsets):
        by_residue.setdefault(off % 8, []).append((k, off))
    for res, taps in by_residue.items():
        base = window[res:res + tc + max(off for _, off in taps) - res, :]
        for k, off in taps:
            yield k, base[off - res:off - res + tc, :]


def _taps(window, w_ref, offsets, tc, flip=False):
    acc = None
    for k, rows in _shifted(window, offsets, tc):
        kk = len(offsets) - 1 - k if flip else k
        term = w_ref[kk:kk + 1, :] * rows
        acc = term if acc is None else acc + term
    return acc


def _window_sums(win, tc, causal):
    sums = []
    cur, rows, step = win, tc + HALO, 1
    for _ in POOL_WINDOWS:
        rows -= 8
        if causal:
            cur = cur[8:8 + rows, :] + cur[8 - step:8 - step + rows, :]
            sums.append(cur[rows - tc:rows, :])
        else:
            cur = cur[0:rows, :] + cur[step:step + rows, :]
            sums.append(cur[0:tc, :])
        step *= 2
    return sums


def _pick(vals, g):
    out = vals[-1]
    for i in range(len(vals) - 2, -1, -1):
        out = jnp.where(g == i, vals[i], out)
    return out


def _pool_count(s, tc, g):
    t1 = (lax.broadcasted_iota(jnp.int32, (tc, 1), 0) + (s + 1)).astype(F32)
    width = _pick([float(w) for w in POOL_WINDOWS], g)
    return jnp.minimum(t1, width)


def pool_fwd(name, z, pool_w, pool_scale, d_pool, y_width):
    t = z.shape[0]
    ng, pg = pool_w.shape[0], pool_w.shape[1]

    def body(u_ref, w_ref, s_ref, pooled_ref, y_ref, pad):
        g = pl.program_id(0)
        pad[pl.ds(0, HALO), :] = jnp.zeros((HALO, pg), F32)

        def fill(s, tc):
            pad[pl.ds(HALO + s, tc), :] = u_ref[pl.ds(s, tc), :].astype(F32)

        def chunk(s, tc):
            win = pad[pl.ds(s, tc + HALO), :]
            total = _pick(_window_sums(win, tc, causal=True), g)
            pooled = total / _pool_count(s, tc, g) - win[HALO:HALO + tc, :]
            pooled_ref[pl.ds(s, tc), :] = pooled.astype(BF16)

        _chunks(t, fill)
        _chunks(t, chunk)
        mixed = jnp.dot(pooled_ref[...], w_ref[...], preferred_element_type=F32)
        y_ref[...] = (mixed * s_ref[...]).astype(BF16)

    col = pl.BlockSpec((t, pg), lambda g: (0, g))
    return pl.pallas_call(
        body, name=name, grid=(ng,),
        in_specs=[col, pl.BlockSpec((None, pg, pg), lambda g: (g, 0, 0)), pl.BlockSpec((1, pg), lambda g: (0, g))],
        out_specs=[col, col],
        out_shape=[jax.ShapeDtypeStruct((t, d_pool), BF16), jax.ShapeDtypeStruct((t, y_width), BF16)],
        scratch_shapes=[pltpu.VMEM((t + HALO, pg), F32)],
        compiler_params=_params(("parallel",)),
    )(z, pool_w, pool_scale)


def pool_bwd(name, pooled, dy, pool_w, pool_scale, dz):
    t, d_pool = pooled.shape
    ng, pg = pool_w.shape[0], pool_w.shape[1]

    def body(p_ref, dy_ref, w_ref, s_ref, dz_ref, du_ref, dw_ref, ds_ref, pad):
        g = pl.program_id(0)
        w = w_ref[...]
        dyv = dy_ref[...].astype(F32)
        mixed = jnp.dot(p_ref[...], w, preferred_element_type=F32)
        ds_ref[...] = jnp.sum(dyv * mixed, axis=0, keepdims=True)
        dmixed = (dyv * s_ref[...]).astype(BF16)
        dw_ref[...] = lax.dot_general(p_ref[...], dmixed, TN, preferred_element_type=F32)
        pad[...] = jnp.zeros((t + HALO, pg), F32)
        pad[pl.ds(0, t), :] = lax.dot_general(dmixed, w, NT, preferred_element_type=F32)

        def scale(s, tc):
            pad[pl.ds(s, tc), :] = pad[pl.ds(s, tc), :] / _pool_count(s, tc, g)

        def chunk(s, tc):
            win = pad[pl.ds(s, tc + HALO), :]
            total = _pick(_window_sums(win, tc, causal=False), g)
            du_ref[pl.ds(s, tc), :] = (total - win[0:tc, :] * _pool_count(s, tc, g)).astype(BF16)

        _chunks(t, scale)
        _chunks(t, chunk)

    col = pl.BlockSpec((t, pg), lambda g: (0, g))
    vec = pl.BlockSpec((1, pg), lambda g: (0, g))
    mat = pl.BlockSpec((None, pg, pg), lambda g: (g, 0, 0))
    return pl.pallas_call(
        body, name=name, grid=(ng,),
        in_specs=[col, col, mat, vec, ANY], out_specs=[col, mat, vec],
        out_shape=[jax.ShapeDtypeStruct(dz.shape, dz.dtype), jax.ShapeDtypeStruct((ng, pg, pg), F32),
                   jax.ShapeDtypeStruct((1, d_pool), F32)],
        input_output_aliases={4: 0},
        scratch_shapes=[pltpu.VMEM((t + HALO, pg), F32)],
        compiler_params=_params(("parallel",)),
    )(pooled, dy, pool_w, pool_scale, dz)


def conv_fwd(name, z, conv_w, conv_b, d_pool, d_conv):
    t = z.shape[0]
    kw = conv_w.shape[0]
    tc_ch = _tile(d_conv, CHANNEL_TILE)
    v0, g0 = d_pool // tc_ch, (d_pool + d_conv) // tc_ch

    def body(v_ref, g_ref, w_ref, b_ref, c_ref, pad):
        pad[pl.ds(0, HALO), :] = jnp.zeros((HALO, tc_ch), F32)

        def fill(s, tc):
            pad[pl.ds(HALO + s, tc), :] = v_ref[pl.ds(s, tc), :].astype(F32) * jax.nn.sigmoid(g_ref[pl.ds(s, tc), :].astype(F32))

        def chunk(s, tc):
            win = pad[pl.ds(s, tc + HALO), :]
            c_ref[pl.ds(s, tc), :] = _taps(win, w_ref, [HALO - (kw - 1) + k for k in range(kw)], tc) + b_ref[...]

        _chunks(t, fill)
        _chunks(t, chunk)

    return pl.pallas_call(
        body, name=name, grid=(d_conv // tc_ch,),
        in_specs=[pl.BlockSpec((t, tc_ch), lambda j: (0, v0 + j)), pl.BlockSpec((t, tc_ch), lambda j: (0, g0 + j)),
                  pl.BlockSpec((kw, tc_ch), lambda j: (0, j)), pl.BlockSpec((1, tc_ch), lambda j: (0, j))],
        out_specs=pl.BlockSpec((t, tc_ch), lambda j: (0, j)),
        out_shape=jax.ShapeDtypeStruct((t, d_conv), F32),
        scratch_shapes=[pltpu.VMEM((t + HALO, tc_ch), F32)],
        compiler_params=_params(("parallel",)),
    )(z, z, conv_w, conv_b)


def conv_bwd(name, z, dc, conv_w, d_pool, d_conv):
    t = z.shape[0]
    kw = conv_w.shape[0]
    tc_ch = _tile(d_conv, CHANNEL_TILE)
    v0, g0 = d_pool // tc_ch, (d_pool + d_conv) // tc_ch

    def body(v_ref, g_ref, dc_ref, w_ref, dz_ref, dw_ref, db_ref, pad_a, pad_dc, acc_w, acc_b, stash):
        @pl.when(pl.program_id(1) == 0)
        def _():
            compute(v_ref, g_ref, dc_ref, w_ref, dz_ref, stash, dw_ref, db_ref, pad_a, pad_dc, acc_w, acc_b)

        @pl.when(pl.program_id(1) == 1)
        def _():
            dz_ref[...] = stash[...]

    def compute(v_ref, g_ref, dc_ref, w_ref, dv_ref, dg_ref, dw_ref, db_ref, pad_a, pad_dc, acc_w, acc_b):
        pad_a[pl.ds(0, HALO), :] = jnp.zeros((HALO, tc_ch), F32)
        pad_dc[pl.ds(t, HALO), :] = jnp.zeros((HALO, tc_ch), F32)
        acc_w[...] = jnp.zeros_like(acc_w)
        acc_b[...] = jnp.zeros_like(acc_b)

        def fill(s, tc):
            pad_a[pl.ds(HALO + s, tc), :] = v_ref[pl.ds(s, tc), :].astype(F32) * jax.nn.sigmoid(g_ref[pl.ds(s, tc), :].astype(F32))
            pad_dc[pl.ds(s, tc), :] = dc_ref[pl.ds(s, tc), :]

        def chunk(s, tc):
            dcv = pad_dc[pl.ds(s, tc), :]
            win_a = pad_a[pl.ds(s, tc + HALO), :]
            for k, rows in _shifted(win_a, [HALO - (kw - 1) + k for k in range(kw)], tc):
                acc_w[pl.ds(8 * k, 8), :] += _fold8(dcv * rows)
            acc_b[...] += _fold8(dcv)
            da = _taps(pad_dc[pl.ds(s, tc + HALO), :], w_ref, list(range(kw)), tc, flip=True)
            vv = v_ref[pl.ds(s, tc), :].astype(F32)
            sg = jax.nn.sigmoid(g_ref[pl.ds(s, tc), :].astype(F32))
            dv_ref[pl.ds(s, tc), :] = (da * sg).astype(BF16)
            dg_ref[pl.ds(s, tc), :] = (da * vv * sg * (1.0 - sg)).astype(BF16)

        _chunks(t, fill)
        _chunks(t, chunk)
        for k in range(kw):
            dw_ref[k:k + 1, :] = jnp.sum(acc_w[pl.ds(8 * k, 8), :], axis=0, keepdims=True)
        db_ref[...] = jnp.sum(acc_b[...], axis=0, keepdims=True)

    return pl.pallas_call(
        body, name=name, grid=(d_conv // tc_ch, 2),
        in_specs=[pl.BlockSpec((t, tc_ch), lambda j, p: (0, v0 + j)), pl.BlockSpec((t, tc_ch), lambda j, p: (0, g0 + j)),
                  pl.BlockSpec((t, tc_ch), lambda j, p: (0, j)), pl.BlockSpec((kw, tc_ch), lambda j, p: (0, j))],
        out_specs=[pl.BlockSpec((t, tc_ch), lambda j, p: (0, v0 + p * (g0 - v0) + j)),
                   pl.BlockSpec((kw, tc_ch), lambda j, p: (0, j)), pl.BlockSpec((1, tc_ch), lambda j, p: (0, j))],
        out_shape=[jax.ShapeDtypeStruct((t, d_pool + 2 * d_conv), BF16),
                   jax.ShapeDtypeStruct((kw, d_conv), F32), jax.ShapeDtypeStruct((1, d_conv), F32)],
        scratch_shapes=[pltpu.VMEM((t + HALO, tc_ch), F32), pltpu.VMEM((t + HALO, tc_ch), F32),
                        pltpu.VMEM((8 * kw, tc_ch), F32), pltpu.VMEM((8, tc_ch), F32), pltpu.VMEM((t, tc_ch), BF16)],
        compiler_params=_params(("parallel", "arbitrary")),
    )(z, z, dc, conv_w)


def short_fwd(name, z, conv_w, d_short):
    t = z.shape[0]
    kw = conv_w.shape[0]
    tc_ch = _tile(d_short, CHANNEL_TILE)
    nt = d_short // tc_ch

    def body(b_ref, c_ref, u_ref, w_ref, y_ref, pad):
        pad[pl.ds(0, HALO), :] = jnp.zeros((HALO, tc_ch), F32)

        def fill(s, tc):
            pad[pl.ds(HALO + s, tc), :] = c_ref[pl.ds(s, tc), :].astype(F32) * u_ref[pl.ds(s, tc), :].astype(F32)

        def chunk(s, tc):
            win = pad[pl.ds(s, tc + HALO), :]
            cq = _taps(win, w_ref, [HALO - (kw - 1) + k for k in range(kw)], tc)
            y_ref[pl.ds(s, tc), :] = (b_ref[pl.ds(s, tc), :].astype(F32) * cq).astype(BF16)

        _chunks(t, fill)
        _chunks(t, chunk)

    return pl.pallas_call(
        body, name=name, grid=(nt,),
        in_specs=[pl.BlockSpec((t, tc_ch), lambda j: (0, j)), pl.BlockSpec((t, tc_ch), lambda j: (0, nt + j)),
                  pl.BlockSpec((t, tc_ch), lambda j: (0, 2 * nt + j)), pl.BlockSpec((kw, tc_ch), lambda j: (0, j))],
        out_specs=pl.BlockSpec((t, tc_ch), lambda j: (0, j)),
        out_shape=jax.ShapeDtypeStruct((t, d_short), BF16),
        scratch_shapes=[pltpu.VMEM((t + HALO, tc_ch), F32)],
        compiler_params=_params(("parallel",)),
    )(z, z, z, conv_w)


def short_bwd(name, z, dy, conv_w, d_short):
    t = z.shape[0]
    kw = conv_w.shape[0]
    tc_ch = _tile(d_short, CHANNEL_TILE)
    nt = d_short // tc_ch

    def body(b_ref, c_ref, u_ref, dy_ref, w_ref, dz_ref, dw_ref, pad_q, pad_dcq, acc_w, stash):
        part = pl.program_id(1)

        @pl.when(part == 0)
        def _():
            compute(b_ref, c_ref, u_ref, dy_ref, w_ref, dz_ref, stash.at[0], stash.at[1], dw_ref, pad_q, pad_dcq, acc_w)

        @pl.when(part > 0)
        def _():
            dz_ref[...] = stash[part - 1]

    def compute(b_ref, c_ref, u_ref, dy_ref, w_ref, db_ref, dcg_ref, du_ref, dw_ref, pad_q, pad_dcq, acc_w):
        pad_q[pl.ds(0, HALO), :] = jnp.zeros((HALO, tc_ch), F32)
        pad_dcq[pl.ds(t, HALO), :] = jnp.zeros((HALO, tc_ch), F32)
        acc_w[...] = jnp.zeros_like(acc_w)

        def fill(s, tc):
            rows = pl.ds(s, tc)
            pad_q[pl.ds(HALO + s, tc), :] = c_ref[rows, :].astype(F32) * u_ref[rows, :].astype(F32)
            pad_dcq[rows, :] = dy_ref[rows, :].astype(F32) * b_ref[rows, :].astype(F32)

        def chunk(s, tc):
            rows = pl.ds(s, tc)
            win_q = pad_q[pl.ds(s, tc + HALO), :]
            dcq = pad_dcq[rows, :]
            cq = None
            for k in range(kw):
                off = HALO - (kw - 1) + k
                shifted = win_q[off:off + tc, :]
                acc_w[pl.ds(8 * k, 8), :] += _fold8(dcq * shifted)
                term = w_ref[k:k + 1, :] * shifted
                cq = term if cq is None else cq + term
            db_ref[rows, :] = (dy_ref[rows, :].astype(F32) * cq).astype(BF16)
            win_d = pad_dcq[pl.ds(s, tc + HALO), :]
            dq = None
            for j in range(kw):
                term = w_ref[kw - 1 - j:kw - j, :] * win_d[j:j + tc, :]
                dq = term if dq is None else dq + term
            dcg_ref[rows, :] = (dq * u_ref[rows, :].astype(F32)).astype(BF16)
            du_ref[rows, :] = (dq * c_ref[rows, :].astype(F32)).astype(BF16)

        _chunks(t, fill)
        _chunks(t, chunk)
        for k in range(kw):
            dw_ref[k:k + 1, :] = jnp.sum(acc_w[pl.ds(8 * k, 8), :], axis=0, keepdims=True)

    zspec = [pl.BlockSpec((t, tc_ch), lambda j, p, o=o: (0, o * nt + j)) for o in range(3)]
    return pl.pallas_call(
        body, name=name, grid=(nt, 3),
        in_specs=[*zspec, pl.BlockSpec((t, tc_ch), lambda j, p: (0, j)), pl.BlockSpec((kw, tc_ch), lambda j, p: (0, j))],
        out_specs=[pl.BlockSpec((t, tc_ch), lambda j, p: (0, p * nt + j)),
                   pl.BlockSpec((kw, tc_ch), lambda j, p: (0, j))],
        out_shape=[jax.ShapeDtypeStruct((t, 3 * d_short), BF16), jax.ShapeDtypeStruct((kw, d_short), F32)],
        scratch_shapes=[pltpu.VMEM((t + HALO, tc_ch), F32), pltpu.VMEM((t + HALO, tc_ch), F32),
                        pltpu.VMEM((8 * kw, tc_ch), F32), pltpu.VMEM((2, t, tc_ch), BF16)],
        compiler_params=_params(("parallel", "arbitrary")),
    )(z, z, z, dy, conv_w)


def adamw(name, w, m, v, contributions):
    r, c = w.shape
    nc = len(contributions)
    n_slots = contributions[0].shape[0]
    tr = 256 if c <= 1024 else 128
    if any(a.shape[1] % tr for a in contributions):
        assert nc == 1
        tr = r
    tiles = [a.shape[1] // tr for a in contributions]
    first = [sum(tiles[:j]) for j in range(nc)]

    def body(w_ref, m_ref, v_ref, *rest):
        g_refs, (grad_ref, delta_ref, nm_ref, nv_ref) = rest[:nc], rest[nc:]
        i = pl.program_id(0)
        g = None
        for j, g_ref in enumerate(g_refs):
            s = g_ref[0].astype(F32)
            for slot in range(1, n_slots):
                s = s + g_ref[slot].astype(F32)
            g = s if g is None else jnp.where(i >= first[j], s, g)
        nm = ADAM_B1 * m_ref[...] + (1.0 - ADAM_B1) * g
        nv = ADAM_B2 * v_ref[...] + (1.0 - ADAM_B2) * (g * g)
        m_hat = nm / (1.0 - ADAM_B1 ** ADAM_STEP)
        v_hat = nv / (1.0 - ADAM_B2 ** ADAM_STEP)
        grad_ref[...] = g
        delta_ref[...] = -ADAM_LR * (m_hat / (jnp.sqrt(v_hat) + ADAM_EPS) + ADAM_WD * w_ref[...])
        nm_ref[...] = nm
        nv_ref[...] = nv

    blk = pl.BlockSpec((tr, c), lambda i: (i, 0))
    g_specs = [pl.BlockSpec((n_slots, tr, c), lambda i, j=j: (0, jnp.clip(i - first[j], 0, tiles[j] - 1), 0))
               for j in range(nc)]
    return pl.pallas_call(
        body, name=name, grid=(r // tr,),
        in_specs=[blk, blk, blk, *g_specs],
        out_specs=[blk] * 4, out_shape=[jax.ShapeDtypeStruct((r, c), F32)] * 4,
        compiler_params=_params(("parallel",)),
    )(w, m, v, *contributions)


def _pad_rows(a, rows):
    return jnp.pad(a, ((0, rows - a.shape[0]), (0, 0)))


def kernel(x, mix_pre_g, mix_post_g, ffn_pre_g, ffn_post_g, ab_w_in, pool_w, pool_scale, conv_w, conv_b, conv_ln_g, conv_ln_b, ab_w_out, sc_w_in, sc_conv_w, sc_w_out, ffn_w1, ffn_w2, loss_target, m_mix_pre_g, m_mix_post_g, m_ffn_pre_g, m_ffn_post_g, m_ab_w_in, m_pool_w, m_pool_scale, m_conv_w, m_conv_b, m_conv_ln_g, m_conv_ln_b, m_ab_w_out, m_sc_w_in, m_sc_conv_w, m_sc_w_out, m_ffn_w1, m_ffn_w2, v_mix_pre_g, v_mix_post_g, v_ffn_pre_g, v_ffn_post_g, v_ab_w_in, v_pool_w, v_pool_scale, v_conv_w, v_conv_b, v_conv_ln_g, v_conv_ln_b, v_ab_w_out, v_sc_w_in, v_sc_conv_w, v_sc_w_out, v_ffn_w1, v_ffn_w2):
    t, d = x.shape[1], x.shape[2]
    d_pool = pool_scale.shape[1]
    d_conv = conv_b.shape[1]
    d_short = d
    ng, pg = pool_w.shape[1], pool_w.shape[3]
    kw, ks = conv_w.shape[1], sc_conv_w.shape[1]
    nb_ab, nb_sc, nb_ff = ab_w_in.shape[2], sc_w_in.shape[2], ffn_w1.shape[2]

    xs = x[0]
    target = loss_target[0]

    lanes = min(128, d_conv // N_DEV)
    small_rows = [kw * (d_conv // N_DEV) // lanes, ks * (d_short // N_DEV) // lanes, ng * (pg // N_DEV) * pg // lanes]
    small_total = -(-sum(small_rows) // 8) * 8
    r0, r1, r2 = small_rows[0], small_rows[0] + small_rows[1], sum(small_rows)

    def pack_small(a_conv, a_sconv, a_pool):
        parts = [a_conv[0].reshape(-1, lanes), a_sconv[0].reshape(-1, lanes), a_pool[0].reshape(-1, lanes)]
        return _pad_rows(jnp.concatenate(parts, axis=0), small_total)

    shards = {
        "ab_in": (ab_w_in, 0, BF16), "small": (pack_small(conv_w, sc_conv_w, pool_w)[None], 0, F32),
        "ab_out": (ab_w_out, 0, BF16), "ff1_0": (ffn_w1, 0, BF16), "ff2_0": (ffn_w2, 0, BF16),
        "sc_in": (sc_w_in, 0, BF16), "sc_out": (sc_w_out, 0, BF16),
        "ff1_1": (ffn_w1, 1, BF16), "ff2_1": (ffn_w2, 1, BF16)}
    zones = {nm: place_shard("place_" + nm, w, layer, dt) for nm, (w, layer, dt) in shards.items()}
    direct = ["ab_in", "small", "ab_out"]
    ring_head = ["ff1_0", "ff2_0"]
    started, token = copies_start(
        "gather_start", [[zones[nm]] for nm in direct + ring_head],
        [_first_hop] * len(direct) + [_ring_hop1] * len(ring_head), [4] * len(direct) + [3] * len(ring_head))
    ring = dict(zip(ring_head, started[len(direct):]))
    started = dict(zip(direct, started))

    ties = [0]

    def after(v, *deps):
        ties[0] += 1
        return tie(f"tie_{ties[0]}", v, *deps)

    def fetch_begin(nm, dep):
        (zone,) = copies_wait("gather_wait_" + nm, started[nm], _first_hop, dep)
        (hop,), tok = copies_start("forward_start_" + nm, [[zone]], _second_hop, 3)
        return hop, tok

    def fetch_end(nm, hop, dep):
        return copies_wait("forward_wait_" + nm, hop, _second_hop, dep)[0]

    def ring_step(tag, dep, second=None, first=None, third=None):
        names, groups, hops, counts = [], [], [], []
        if second is not None:
            groups.append(copies_wait("ring1_wait_" + second, ring[second], _ring_hop1, dep))
            names, hops, counts = names + [second], hops + [_ring_hop2], counts + [4]
        if first is not None:
            groups.append([zones[first]])
            names, hops, counts = names + [first], hops + [_ring_hop1], counts + [3]
        if third is not None:
            groups.append(copies_wait("ring2_wait_" + third, ring[third], _ring_hop2, dep))
            names, hops, counts = names + [third], hops + [_ring_hop3], counts + [1]
        begun, tok = copies_start("ring_start_" + tag, groups, hops, counts, deps=[dep])
        ring.update(zip(names, begun))
        return tok

    def ring_done(nm, dep):
        return copies_wait("ring3_wait_" + nm, ring[nm], _ring_hop3, dep)[0]

    relu = lambda r: jnp.maximum(r, 0.0)
    square = lambda a: a * a
    relu2_bwd = lambda r, a: r * (2.0 * a.astype(F32))

    def row(vec, l):
        return vec[l:l + 1]

    hop_small, _ = fetch_begin("small", token)
    hop_ab_in, tok = fetch_begin("ab_in", token)
    w_small = fetch_end("small", hop_small, tok)
    w_ab_in = fetch_end("ab_in", hop_ab_in, tok)
    w_conv = w_small[:, :r0].reshape(N_DEV, kw, -1).transpose(1, 0, 2).reshape(kw, d_conv)
    w_sconv = w_small[:, r0:r1].reshape(N_DEV, ks, -1).transpose(1, 0, 2).reshape(ks, d_short)
    w_pool = w_small[:, r1:r2].reshape(N_DEV, ng, -1, pg).transpose(1, 0, 2, 3).reshape(ng, pg, pg).astype(BF16)
    h0 = norm_pre("norm_pre", xs, row(mix_pre_g, 0))
    z0 = mm_nn_blocked("ab_in", h0, w_ab_in, out_dtype=BF16)
    hop, tok = fetch_begin("ab_out", z0)
    z0 = after(z0, tok)
    pooled, y0 = pool_fwd("pool_fwd", z0, w_pool, pool_scale, d_pool, d_pool + d_conv)
    cv = conv_fwd("conv_fwd", z0, w_conv, conv_b, d_pool, d_conv)
    y0 = ln_silu("ln_silu", cv, conv_ln_g, conv_ln_b, y0, d_pool // d_conv)
    w_ab_out = fetch_end("ab_out", hop, y0)
    tok = ring_step("a", w_ab_out, second="ff1_0", first="sc_in")
    y0 = after(y0, tok)
    m0 = mm_nn("ab_out", y0, w_ab_out.reshape(d_pool + d_conv, d), out_dtype=F32)
    x1, h1 = post_pre("post_pre_0", xs, m0, row(mix_post_g, 0), row(ffn_pre_g, 0))
    tok = ring_step("b", h1, second="ff2_0", first="sc_out", third="ff1_0")
    w_ff1_0 = ring_done("ff1_0", tok)
    a0 = mm_nn_blocked("ffn0_up", h1, w_ff1_0, out_dtype=BF16, epilogue=relu)
    tok = ring_step("c", a0, second="sc_in", first="ff1_1", third="ff2_0")
    w_ff2_0 = ring_done("ff2_0", tok).reshape(-1, d)
    f0 = mm_nn("ffn0_down", a0, w_ff2_0, out_dtype=F32, tk=2048, lhs_fn=square)
    tok = ring_step("d", f0, second="sc_out", first="ff2_1", third="sc_in")
    f0 = after(f0, tok)
    x2, h2 = post_pre("post_pre_1", x1, f0, row(ffn_post_g, 0), row(mix_pre_g, 1))
    w_sc_in = ring_done("sc_in", h2)
    z1 = mm_nn_blocked("sc_in", h2, w_sc_in, out_dtype=BF16)
    tok = ring_step("e", z1, second="ff1_1", third="sc_out")
    z1 = after(z1, tok)
    y1 = short_fwd("short_fwd", z1, w_sconv, d_short)
    w_sc_out = ring_done("sc_out", y1).reshape(d_short, d)
    m1 = mm_nn("sc_out", y1, w_sc_out, out_dtype=F32)
    tok = ring_step("f", m1, second="ff2_1")
    m1 = after(m1, tok)
    x3, h3 = post_pre("post_pre_2", x2, m1, row(mix_post_g, 1), row(ffn_pre_g, 1))
    tok = ring_step("g", h3, third="ff1_1")
    w_ff1_1 = ring_done("ff1_1", tok)
    a1 = mm_nn_blocked("ffn1_up", h3, w_ff1_1, out_dtype=BF16, epilogue=relu)
    tok = ring_step("h", a1, third="ff2_1")
    w_ff2_1 = ring_done("ff2_1", tok).reshape(-1, d)
    f1 = mm_nn("ffn1_down", a1, w_ff2_1, out_dtype=F32, tk=2048, lhs_fn=square)
    dx4, df1, loss_part, dg_ffn_post1 = post_loss("post_loss", x3, f1, row(ffn_post_g, 1), target)
    loss = lax.psum(jnp.sum(loss_part) * (0.5 / d), ("x", "y", "c"))

    def reduce_begin(tag, g):
        zone = lax.empty((N_CHIP,) + g.shape[1:], g.dtype)
        (hop,), tok = copies_start("pair_start_" + tag, [[g, zone]], _pair_hop, N_CHIP)
        return hop, tok

    def reduce_middle(tag, hop, dep):
        g, from_sibling = copies_wait("pair_wait_" + tag, hop, _pair_hop, dep)
        pair_sum, zone = pair_add("pair_add_" + tag, g, from_sibling)
        (hop2,), tok = copies_start("chips_start_" + tag, [[pair_sum, zone]], _chip_hop, 3)
        return hop2, tok

    def reduce_end(tag, hop2, dep):
        return copies_wait("chips_wait_" + tag, hop2, _chip_hop, dep)[1]

    dw = mm_tn("ffn1_dw2", a1, df1, out_dtype=BF16, lhs_fn=square)
    red_ff2_1, tok = reduce_begin("ff2_1", dw.reshape(N_DEV, -1, d))
    df1 = after(df1, tok)
    dpre = mm_nt("ffn1_da", df1, w_ff2_1, out_dtype=BF16, extra=a1, epilogue=relu2_bwd)
    dw = mm_tn_blocked("ffn1_dw1", h3, dpre, nb_ff, out_dtype=BF16)
    red_ff1_1, tok = reduce_begin("ff1_1", dw)
    dpre = after(dpre, tok)
    dh3 = mm_nt_blocked("ffn1_dh", dpre, w_ff1_1, out_dtype=BF16)
    red_ff2_1, tok = reduce_middle("ff2_1", red_ff2_1, dh3)
    dh3 = after(dh3, tok)
    dx3, dm1, dg_ffn_pre1, dg_mix_post1 = bwd_pre_post("bwd_3", dx4, x3, row(ffn_pre_g, 1), dh3, m1, row(mix_post_g, 1))

    dw = mm_tn("sc_dwout", y1, dm1, out_dtype=BF16)
    red_sc_out, tok = reduce_begin("sc_out", dw.reshape(N_DEV, -1, d))
    dm1 = after(dm1, tok)
    dy1 = mm_nt("sc_dy", dm1, w_sc_out, out_dtype=BF16)
    red_ff1_1, tok = reduce_middle("ff1_1", red_ff1_1, dy1)
    dy1 = after(dy1, tok)
    dz1, dw_sconv = short_bwd("short_bwd", z1, dy1, w_sconv, d_short)
    dw = mm_tn_blocked("sc_dwin", h2, dz1, nb_sc, out_dtype=BF16)
    red_sc_in, tok = reduce_begin("sc_in", dw)
    dz1 = after(dz1, tok)
    dh2 = mm_nt_blocked("sc_dh", dz1, w_sc_in, out_dtype=BF16)
    red_sc_out, tok = reduce_middle("sc_out", red_sc_out, dh2)
    dh2 = after(dh2, tok)
    dx2, df0, dg_mix_pre1, dg_ffn_post0 = bwd_pre_post("bwd_2", dx3, x2, row(mix_pre_g, 1), dh2, f0, row(ffn_post_g, 0))

    dw = mm_tn("ffn0_dw2", a0, df0, out_dtype=BF16, lhs_fn=square)
    red_ff2_0, tok = reduce_begin("ff2_0", dw.reshape(N_DEV, -1, d))
    df0 = after(df0, tok)
    dpre = mm_nt("ffn0_da", df0, w_ff2_0, out_dtype=BF16, extra=a0, epilogue=relu2_bwd)
    red_sc_in, tok = reduce_middle("sc_in", red_sc_in, dpre)
    dpre = after(dpre, tok)
    dw = mm_tn_blocked("ffn0_dw1", h1, dpre, nb_ff, out_dtype=BF16)
    red_ff1_0, tok = reduce_begin("ff1_0", dw)
    dpre = after(dpre, tok)
    dh1 = mm_nt_blocked("ffn0_dh", dpre, w_ff1_0, out_dtype=BF16)
    red_ff2_0, tok = reduce_middle("ff2_0", red_ff2_0, dh1)
    dh1 = after(dh1, tok)
    dx1, dm0, dg_ffn_pre0, dg_mix_post0 = bwd_pre_post("bwd_1", dx2, x1, row(ffn_pre_g, 0), dh1, m0, row(mix_post_g, 0))

    dw = mm_tn("ab_dwout", y0, dm0, out_dtype=BF16)
    red_ab_out, tok = reduce_begin("ab_out", dw.reshape(N_DEV, -1, d))
    dm0 = after(dm0, tok)
    dy0 = mm_nt("ab_dy", dm0, w_ab_out.reshape(d_pool + d_conv, d), out_dtype=BF16)
    red_ff1_0, tok = reduce_middle("ff1_0", red_ff1_0, dy0)
    dy0 = after(dy0, tok)
    dcv, dg_ln_g, dg_ln_b = ln_silu_bwd("ln_silu_bwd", cv, conv_ln_g, conv_ln_b, dy0, d_pool // d_conv)
    dz0, dw_conv, dg_conv_b = conv_bwd("conv_bwd", z0, dcv, w_conv, d_pool, d_conv)
    dz0, dw_pool, dg_pool_scale = pool_bwd("pool_bwd", pooled, dy0, w_pool, pool_scale, dz0)
    small_parts = [
        dw_conv.reshape(kw, N_DEV, -1).transpose(1, 0, 2).reshape(N_DEV, -1, lanes),
        dw_sconv.reshape(ks, N_DEV, -1).transpose(1, 0, 2).reshape(N_DEV, -1, lanes),
        dw_pool.reshape(ng, N_DEV, pg // N_DEV, pg).transpose(1, 0, 2, 3).reshape(N_DEV, -1, lanes),
    ]
    small = jnp.pad(jnp.concatenate(small_parts, axis=1), ((0, 0), (0, small_total - r2), (0, 0)))
    red_small, tok = reduce_begin("small", small)
    red_ab_out, tok2 = reduce_middle("ab_out", red_ab_out, dz0)
    dz0 = after(dz0, tok, tok2)
    dw = mm_tn_blocked("ab_dwin", h0, dz0, nb_ab, out_dtype=BF16)
    red_ab_in, tok = reduce_begin("ab_in", dw)
    dz0 = after(dz0, tok)
    dh0 = mm_nt_blocked("ab_dh", dz0, w_ab_in, out_dtype=BF16)
    red_small, tok = reduce_middle("small", red_small, dh0)
    dh0 = after(dh0, tok)
    grad_x, dg_mix_pre0 = bwd_pre_final("bwd_0", dx1, xs, row(mix_pre_g, 0), dh0)
    red_ab_in, tok = reduce_middle("ab_in", red_ab_in, grad_x)

    fold = lambda a: jnp.sum(a, axis=0, keepdims=True)
    rep_rows = [fold(dg_mix_pre0), fold(dg_mix_pre1), fold(dg_mix_post0), fold(dg_mix_post1),
                fold(dg_ffn_pre0), fold(dg_ffn_pre1), fold(dg_ffn_post0), fold(dg_ffn_post1)]
    tail = jnp.concatenate([dg_pool_scale, dg_conv_b, fold(dg_ln_g), fold(dg_ln_b)], axis=1).reshape(-1, d)
    rep = _pad_rows(jnp.concatenate(rep_rows + [tail], axis=0), 16)
    (rep_hop,), _ = copies_start("rep_start", [[place_shard("place_rep", rep[None], 0, F32)]], _first_hop, 4)

    def pack_rep(a_mix_pre, a_mix_post, a_ffn_pre, a_ffn_post, a_scale, a_b, a_g, a_lb):
        tail_ = jnp.concatenate([a_scale, a_b, a_g, a_lb], axis=1).reshape(-1, d)
        return _pad_rows(jnp.concatenate([a_mix_pre, a_mix_post, a_ffn_pre, a_ffn_post, tail_], axis=0), 16)

    def upd(name, w, m, v, contribs):
        shape = w.shape
        flat2 = lambda a: a.reshape(-1, shape[-1])
        outs = adamw(name, flat2(w), flat2(m), flat2(v), contribs)
        return [o.reshape(shape) for o in outs]

    g_ff2 = [reduce_end("ff2_0", red_ff2_0, tok), reduce_end("ff2_1", red_ff2_1, tok)]
    o_ff2 = upd("adam_ffn_w2", ffn_w2, m_ffn_w2, v_ffn_w2, g_ff2)
    (rep_zone,) = copies_wait("rep_wait", rep_hop, _first_hop, o_ff2[0])
    (rep_hop,), _ = copies_start("rep_forward_start", [[rep_zone]], _second_hop, 3)
    g_ff1 = [reduce_end("ff1_0", red_ff1_0, o_ff2[0]), reduce_end("ff1_1", red_ff1_1, o_ff2[0])]
    o_ff1 = upd("adam_ffn_w1", ffn_w1, m_ffn_w1, v_ffn_w1, g_ff1)
    (rep_all,) = copies_wait("rep_forward_wait", rep_hop, _second_hop, o_ff1[0])
    o_rep = adamw("adam_replicated",
                  pack_rep(mix_pre_g, mix_post_g, ffn_pre_g, ffn_post_g, pool_scale, conv_b, conv_ln_g, conv_ln_b),
                  pack_rep(m_mix_pre_g, m_mix_post_g, m_ffn_pre_g, m_ffn_post_g, m_pool_scale, m_conv_b, m_conv_ln_g, m_conv_ln_b),
                  pack_rep(v_mix_pre_g, v_mix_post_g, v_ffn_pre_g, v_ffn_post_g, v_pool_scale, v_conv_b, v_conv_ln_g, v_conv_ln_b),
                  [rep_all])
    o_sc_out = upd("adam_sc_out", sc_w_out, m_sc_w_out, v_sc_w_out, [reduce_end("sc_out", red_sc_out, o_ff1[0])])
    o_sc_in = upd("adam_sc_in", sc_w_in, m_sc_w_in, v_sc_w_in, [reduce_end("sc_in", red_sc_in, o_sc_out[0])])
    o_ab_out = upd("adam_ab_out", ab_w_out, m_ab_w_out, v_ab_w_out, [reduce_end("ab_out", red_ab_out, o_sc_in[0])])
    o_small = adamw("adam_small", pack_small(conv_w, sc_conv_w, pool_w), pack_small(m_conv_w, m_sc_conv_w, m_pool_w),
                    pack_small(v_conv_w, v_sc_conv_w, v_pool_w), [reduce_end("small", red_small, o_ab_out[0])])
    o_ab_in = upd("adam_ab_in", ab_w_in, m_ab_w_in, v_ab_w_in, [reduce_end("ab_in", red_ab_in, o_small[0])])

    def unpack_small(o):
        return o[:r0].reshape(conv_w.shape), o[r0:r1].reshape(sc_conv_w.shape), o[r1:r2].reshape(pool_w.shape)

    def unpack_rep(o):
        tail_ = o[8:8 + tail.shape[0]].reshape(1, -1)
        n1 = d_pool
        return dict(mix_pre_g=o[0:2], mix_post_g=o[2:4], ffn_pre_g=o[4:6], ffn_post_g=o[6:8],
                    pool_scale=tail_[:, :n1], conv_b=tail_[:, n1:n1 + d_conv],
                    conv_ln_g=tail_[:, n1 + d_conv:n1 + 2 * d_conv], conv_ln_b=tail_[:, n1 + 2 * d_conv:n1 + 3 * d_conv])

    results = []
    for kind in range(4):
        rep_o = unpack_rep(o_rep[kind])
        s_conv, s_sconv, s_pool = unpack_small(o_small[kind])
        results.append([
            rep_o["mix_pre_g"], rep_o["mix_post_g"], rep_o["ffn_pre_g"], rep_o["ffn_post_g"],
            o_ab_in[kind], s_pool, rep_o["pool_scale"], s_conv, rep_o["conv_b"], rep_o["conv_ln_g"], rep_o["conv_ln_b"],
            o_ab_out[kind], o_sc_in[kind], s_sconv, o_sc_out[kind], o_ff1[kind], o_ff2[kind]])

    return (loss, grad_x[None], *results[0], *results[1], *results[2], *results[3])
```

```python
import jax
import jax.numpy as jnp
from jax import lax
from jax.experimental import pallas as pl
from jax.experimental.pallas import tpu as pltpu

F32 = jnp.float32
BF16 = jnp.bfloat16
MESH = pl.DeviceIdType.MESH
ANY = pl.BlockSpec(memory_space=pl.ANY)

NORM_EPS = 1e-6
POOL_WINDOWS = (2, 4, 8, 16)
ADAM_LR = 0.001
ADAM_B1 = 0.9
ADAM_B2 = 0.999
ADAM_EPS = 1e-08
ADAM_WD = 0.01
ADAM_STEP = 10

N_DEV = 8
VMEM_LIMIT = 56 * 1024 * 1024
PAIR_ADD_BLOCK = 1 << 20
MATMUL_ROWS = 2048
ROW_TILE = 256
CHANNEL_TILE = 256
TIME_CHUNK = 64
HALO = 32

NN = (((1,), (0,)), ((), ()))
NT = (((1,), (1,)), ((), ()))
TN = (((0,), (0,)), ((), ()))


def _params(sem):
    return pltpu.CompilerParams(dimension_semantics=sem, vmem_limit_bytes=VMEM_LIMIT)


def _place():
    x, y, c = lax.axis_index("x"), lax.axis_index("y"), lax.axis_index("c")
    return x, y, c


def _slot(px, py, pc):
    return 4 * px + 2 * py + pc


HBM = pl.BlockSpec(memory_space=pltpu.HBM)
SEM = pl.BlockSpec(memory_space=pltpu.SEMAPHORE)
EFFECT = pltpu.SideEffectType.DATAFLOW_SIDE_EFFECTING
TOKEN = jax.ShapeDtypeStruct((8, 128), F32)


def _in_hbm(a):
    return pltpu.with_memory_space_constraint(a, pltpu.HBM)


CHIPS = [(0, 0), (0, 1), (1, 0), (1, 1)]
N_CHIP = len(CHIPS)


def _chip(px, py):
    return 2 * px + py


def _first_hop(bufs, sends, recvs, waiting):
    (land,) = bufs
    x, y, c = _place()
    me = _slot(x, y, c)
    peers = [(x, y, 1 - c), (1 - x, y, c), (x, 1 - y, c), (1 - x, 1 - y, c)]
    return [pltpu.make_async_remote_copy(
        src_ref=land.at[me], dst_ref=land.at[_slot(*p) if waiting else me],
        send_sem=sends.at[k], recv_sem=recvs.at[k], device_id=p, device_id_type=MESH) for k, p in enumerate(peers)]


def _second_hop(bufs, sends, recvs, waiting):
    (land,) = bufs
    x, y, c = _place()
    return [pltpu.make_async_remote_copy(
        src_ref=land.at[_slot(px, py, c)], dst_ref=land.at[_slot(px, py, 1 - c if waiting else c)],
        send_sem=sends.at[k], recv_sem=recvs.at[k], device_id=(x, y, 1 - c), device_id_type=MESH)
        for k, (px, py) in enumerate([(1 - x, y), (x, 1 - y), (1 - x, 1 - y)])]


def _ring_hop1(bufs, sends, recvs, waiting):
    (land,) = bufs
    x, y, c = _place()
    me = _slot(x, y, c)
    peers = [(1 - x, y, c), (x, 1 - y, c), (x, y, 1 - c)]
    return [pltpu.make_async_remote_copy(
        src_ref=land.at[me], dst_ref=land.at[_slot(*p) if waiting else me],
        send_sem=sends.at[k], recv_sem=recvs.at[k], device_id=p, device_id_type=MESH) for k, p in enumerate(peers)]


def _ring_hop2(bufs, sends, recvs, waiting):
    (land,) = bufs
    x, y, c = _place()
    half = land.shape[1] // 2
    first, second = pl.ds(0, half), pl.ds(half, half)
    nx, ny, diag = _slot(1 - x, y, c), _slot(x, 1 - y, c), _slot(1 - x, 1 - y, c)
    plan = [
        (land.at[ny, first], land.at[diag, first], (1 - x, y, c)),
        (land.at[nx, second], land.at[diag, second], (x, 1 - y, c)),
        (land.at[nx], land.at[_slot(1 - x, y, 1 - c)], (x, y, 1 - c)),
        (land.at[ny], land.at[_slot(x, 1 - y, 1 - c)], (x, y, 1 - c))]
    return [pltpu.make_async_remote_copy(
        src_ref=src, dst_ref=mine if waiting else src, send_sem=sends.at[k], recv_sem=recvs.at[k],
        device_id=to, device_id_type=MESH) for k, (src, mine, to) in enumerate(plan)]


def _ring_hop3(bufs, sends, recvs, waiting):
    (land,) = bufs
    x, y, c = _place()
    return [pltpu.make_async_remote_copy(
        src_ref=land.at[_slot(1 - x, 1 - y, c)], dst_ref=land.at[_slot(1 - x, 1 - y, 1 - c if waiting else c)],
        send_sem=sends.at[0], recv_sem=recvs.at[0], device_id=(x, y, 1 - c), device_id_type=MESH)]


def _pair_hop(bufs, sends, recvs, waiting):
    g, land = bufs
    x, y, c = _place()
    return [pltpu.make_async_remote_copy(
        src_ref=g.at[_slot(qx, qy, 1 - c)], dst_ref=land.at[q],
        send_sem=sends.at[q], recv_sem=recvs.at[q], device_id=(x, y, 1 - c), device_id_type=MESH)
        for q, (qx, qy) in enumerate(CHIPS)]


def _chip_hop(bufs, sends, recvs, waiting):
    p, land = bufs
    x, y, c = _place()
    return [pltpu.make_async_remote_copy(
        src_ref=p.at[_chip(px, py)], dst_ref=land.at[_chip(px, py) if waiting else _chip(x, y)],
        send_sem=sends.at[k], recv_sem=recvs.at[k], device_id=(px, py, c), device_id_type=MESH)
        for k, (px, py) in enumerate([(1 - x, y), (x, 1 - y), (1 - x, 1 - y)])]


def copies_start(name, groups, hop, n_copies, deps=()):
    flat = [b for grp in groups for b in grp]
    nb, ng = len(flat), len(groups)
    deps = list(deps)
    hops = list(hop) if isinstance(hop, (list, tuple)) else [hop] * ng
    counts = list(n_copies) if isinstance(n_copies, (list, tuple)) else [n_copies] * ng

    def body(*refs):
        ins, token = refs[:nb], refs[-1]
        sems = refs[nb + len(deps):nb + len(deps) + 2 * ng]
        i = 0
        for gi, grp in enumerate(groups):
            for cp in hops[gi](ins[i:i + len(grp)], sems[2 * gi], sems[2 * gi + 1], False):
                cp.start()
            i += len(grp)
        token[...] = jnp.zeros_like(token)

    outs = pl.pallas_call(
        body, name=name,
        out_shape=([pltpu.SemaphoreType.DMA((n,)) for n in counts for _ in range(2)]
                   + [pltpu.HBM(b.shape, b.dtype) for b in flat] + [TOKEN]),
        in_specs=[HBM] * nb + [ANY] * len(deps),
        out_specs=[SEM] * (2 * ng) + [HBM] * nb + [pl.BlockSpec(memory_space=pltpu.VMEM)],
        input_output_aliases={i: 2 * ng + i for i in range(nb)},
        compiler_params=pltpu.CompilerParams(has_side_effects=EFFECT),
    )(*[_in_hbm(b) for b in flat], *deps)
    started, i = [], 0
    for gi, grp in enumerate(groups):
        started.append((outs[2 * gi], outs[2 * gi + 1], list(outs[2 * ng + i:2 * ng + i + len(grp)])))
        i += len(grp)
    return started, outs[-1]


def copies_wait(name, started, hop, after):
    sends, recvs, bufs = started
    nb = len(bufs)

    def body(*refs):
        for cp in hop(refs[:nb], refs[nb], refs[nb + 1], True):
            cp.wait_send()
            cp.wait_recv()

    outs = pl.pallas_call(
        body, name=name,
        out_shape=[pltpu.HBM(b.shape, b.dtype) for b in bufs],
        in_specs=[HBM] * nb + [SEM, SEM, ANY], out_specs=[HBM] * nb,
        input_output_aliases={i: i for i in range(nb)},
        compiler_params=pltpu.CompilerParams(has_side_effects=EFFECT),
    )(*bufs, sends, recvs, after)
    return list(outs)


def place_shard(name, w, layer, dtype, deps=()):
    _, r, c = w.shape
    tr = _tile(r, 1024)
    x, y, core = _place()
    me = _slot(x, y, core).astype(jnp.int32).reshape(1)

    def body(me_ref, w_ref, *rest):
        rest[-1][...] = w_ref[...].astype(dtype)

    return pl.pallas_call(
        body, name=name,
        grid_spec=pltpu.PrefetchScalarGridSpec(
            num_scalar_prefetch=1, grid=(r // tr,),
            in_specs=[pl.BlockSpec((None, tr, c), lambda i, me_ref: (layer, i, 0))] + [ANY] * len(deps),
            out_specs=pl.BlockSpec((None, tr, c), lambda i, me_ref: (me_ref[0], i, 0))),
        out_shape=jax.ShapeDtypeStruct((N_DEV, r, c), dtype),
        compiler_params=_params(("parallel",)),
    )(me, w, *deps)


def tie(name, x, *deps):
    def body(*refs):
        del refs

    return pl.pallas_call(
        body, name=name, out_shape=jax.ShapeDtypeStruct(x.shape, x.dtype),
        in_specs=[ANY] * (1 + len(deps)), out_specs=ANY, input_output_aliases={0: 0},
    )(x, *deps)


def pair_add(name, g, from_sibling):
    _, r, c_dim = g.shape
    tr = r
    while tr * c_dim > PAIR_ADD_BLOCK and tr % 16 == 0:
        tr //= 2
    x, y, core = _place()
    where = jnp.stack([core, _chip(x, y)]).astype(jnp.int32)

    def body(where_ref, g_ref, s_ref, o_ref, zone_ref):
        total = (g_ref[...].astype(F32) + s_ref[...].astype(F32)).astype(o_ref.dtype)
        o_ref[...] = total

        @pl.when(pl.program_id(1) == where_ref[1])
        def _():
            zone_ref[...] = total

    blk = pl.BlockSpec((None, tr, c_dim), lambda i, q, where_ref: (q, i, 0))
    return pl.pallas_call(
        body, name=name,
        grid_spec=pltpu.PrefetchScalarGridSpec(
            num_scalar_prefetch=1, grid=(r // tr, N_CHIP),
            in_specs=[pl.BlockSpec((None, None, tr, c_dim), lambda i, q, where_ref: (q, where_ref[0], i, 0)), blk],
            out_specs=[blk, pl.BlockSpec((None, tr, c_dim), lambda i, q, where_ref: (where_ref[1], i, 0))]),
        out_shape=[jax.ShapeDtypeStruct((N_CHIP, r, c_dim), g.dtype)] * 2,
        compiler_params=_params(("parallel", "arbitrary")),
    )(where, g.reshape(N_CHIP, 2, r, c_dim), from_sibling)


def _matmul(name, lhs, rhs, *, out_shape, out_dtype, grid, lhs_spec, rhs_spec, out_spec, dims, acc_shape,
            lhs_fn=None, extra=(), extra_specs=(), epilogue=None, parts=1):
    nk = grid[2]
    n_extra = len(extra)

    def body(*refs):
        lhs_ref, rhs_ref = refs[0], refs[1]
        extra_refs = refs[2:2 + n_extra]
        out_ref = refs[2 + n_extra]

        def product():
            if parts == 1:
                a = lhs_ref[...]
                if lhs_fn is not None:
                    a = lhs_fn(a)
                return lax.dot_general(a, rhs_ref[...], dims, preferred_element_type=F32)
            width = lhs_ref.shape[1] // parts
            total = None
            for b in range(parts):
                term = lax.dot_general(lhs_ref[:, b * width:(b + 1) * width], rhs_ref[b], dims,
                                       preferred_element_type=F32)
                total = term if total is None else total + term
            return total

        def finish(r):
            if epilogue is not None:
                r = epilogue(r, *[e[...] for e in extra_refs])
            out_ref[...] = r.astype(out_dtype)

        if nk == 1:
            finish(product())
        else:
            acc_ref = refs[3 + n_extra]
            k = pl.program_id(2)

            @pl.when(k == 0)
            def _():
                acc_ref[...] = product()

            @pl.when(jnp.logical_and(k > 0, k < nk - 1))
            def _():
                acc_ref[...] += product()

            @pl.when(k == nk - 1)
            def _():
                finish(acc_ref[...] + product())

    return pl.pallas_call(
        body, name=name, grid=grid,
        out_shape=jax.ShapeDtypeStruct(out_shape, out_dtype),
        in_specs=[lhs_spec, rhs_spec, *extra_specs], out_specs=out_spec,
        scratch_shapes=[pltpu.VMEM(acc_shape, F32)] if nk > 1 else [],
        compiler_params=_params(("parallel", "parallel", "arbitrary")),
    )(lhs, rhs, *extra)


def _tile(n, want):
    return want if n % want == 0 else n


def mm_nn(name, x, w, *, out_dtype, tn=512, tk=None, lhs_fn=None, epilogue=None):
    t, kdim = x.shape
    n = w.shape[1]
    tm, tn = _tile(t, MATMUL_ROWS), _tile(n, tn)
    tk = kdim if tk is None else _tile(kdim, tk)
    return _matmul(
        name, x, w, out_shape=(t, n), out_dtype=out_dtype, grid=(t // tm, n // tn, kdim // tk),
        lhs_spec=pl.BlockSpec((tm, tk), lambda i, j, k: (i, k)),
        rhs_spec=pl.BlockSpec((tk, tn), lambda i, j, k: (k, j)),
        out_spec=pl.BlockSpec((tm, tn), lambda i, j, k: (i, j)),
        dims=NN, acc_shape=(tm, tn), lhs_fn=lhs_fn, epilogue=epilogue)


def mm_nn_blocked(name, x, w, *, out_dtype, epilogue=None):
    t, kdim = x.shape
    nb = w.shape[2]
    tm = _tile(t, MATMUL_ROWS)
    tn = nb // 2 if nb >= 1024 else nb
    sub = nb // tn
    return _matmul(
        name, x, w, out_shape=(t, N_DEV * nb), out_dtype=out_dtype, grid=(t // tm, N_DEV * sub, 1),
        lhs_spec=pl.BlockSpec((tm, kdim), lambda i, j, k: (i, k)),
        rhs_spec=pl.BlockSpec((None, kdim, tn), lambda i, j, k: (j // sub, k, j % sub)),
        out_spec=pl.BlockSpec((tm, tn), lambda i, j, k: (i, j)),
        dims=NN, acc_shape=(tm, tn), epilogue=epilogue)


def mm_nt(name, dy, w, *, out_dtype, tn=512, extra=None, epilogue=None):
    t, n = dy.shape
    kdim = w.shape[0]
    tm, tn = _tile(t, MATMUL_ROWS), _tile(kdim, tn)
    extra_arrs = () if extra is None else (extra,)
    extra_specs = () if extra is None else (pl.BlockSpec((tm, tn), lambda i, j, k: (i, j)),)
    return _matmul(
        name, dy, w, out_shape=(t, kdim), out_dtype=out_dtype, grid=(t // tm, kdim // tn, 1),
        lhs_spec=pl.BlockSpec((tm, n), lambda i, j, k: (i, k)),
        rhs_spec=pl.BlockSpec((tn, n), lambda i, j, k: (j, k)),
        out_spec=pl.BlockSpec((tm, tn), lambda i, j, k: (i, j)),
        dims=NT, acc_shape=(tm, tn), extra=extra_arrs, extra_specs=extra_specs, epilogue=epilogue)


def mm_nt_blocked(name, dz, w, *, out_dtype, tn=512):
    t = dz.shape[0]
    kdim, nb = w.shape[1], w.shape[2]
    tm, tn = _tile(t, MATMUL_ROWS), _tile(kdim, tn)
    parts = 2
    return _matmul(
        name, dz, w, out_shape=(t, kdim), out_dtype=out_dtype, grid=(t // tm, kdim // tn, N_DEV // parts),
        lhs_spec=pl.BlockSpec((tm, parts * nb), lambda i, j, k: (i, k)),
        rhs_spec=pl.BlockSpec((parts, tn, nb), lambda i, j, k: (k, j, 0)),
        out_spec=pl.BlockSpec((tm, tn), lambda i, j, k: (i, j)),
        dims=NT, acc_shape=(tm, tn), parts=parts)


def mm_tn(name, x, dy, *, out_dtype, tk=1024, tn=1024, lhs_fn=None):
    t, kdim = x.shape
    n = dy.shape[1]
    tk, tn = _tile(kdim, tk), _tile(n, tn)
    return _matmul(
        name, x, dy, out_shape=(kdim, n), out_dtype=out_dtype, grid=(kdim // tk, n // tn, 1),
        lhs_spec=pl.BlockSpec((t, tk), lambda i, j, k: (k, i)),
        rhs_spec=pl.BlockSpec((t, tn), lambda i, j, k: (k, j)),
        out_spec=pl.BlockSpec((tk, tn), lambda i, j, k: (i, j)),
        dims=TN, acc_shape=(tk, tn), lhs_fn=lhs_fn)


def mm_tn_blocked(name, x, dz, nb, *, out_dtype, tk=1024):
    t, kdim = x.shape
    tk = _tile(kdim, tk)
    return _matmul(
        name, x, dz, out_shape=(N_DEV, kdim, nb), out_dtype=out_dtype, grid=(kdim // tk, N_DEV, 1),
        lhs_spec=pl.BlockSpec((t, tk), lambda i, j, k: (k, i)),
        rhs_spec=pl.BlockSpec((t, nb), lambda i, j, k: (k, j)),
        out_spec=pl.BlockSpec((None, tk, nb), lambda i, j, k: (j, i, 0)),
        dims=TN, acc_shape=(tk, nb))


def _rstd(v):
    return lax.rsqrt(jnp.mean(v * v, axis=-1, keepdims=True) + NORM_EPS)


def _rms_bwd(v, g, dy):
    r = _rstd(v)
    vhat = v * r
    dvh = dy * g
    dv = r * (dvh - vhat * jnp.mean(dvh * vhat, axis=-1, keepdims=True))
    return dv, dy * vhat


def _fold8(v):
    rows, n = v.shape
    return jnp.sum(v.reshape(rows // 8, 8, n), axis=0)


def _fold_lanes(v):
    out = v[:, 0:128]
    for i in range(1, v.shape[1] // 128):
        out = out + v[:, 128 * i:128 * (i + 1)]
    return out


def _accumulate(ref, v):
    i = pl.program_id(0)

    @pl.when(i == 0)
    def _():
        ref[...] = v

    @pl.when(i > 0)
    def _():
        ref[...] += v


def _row_call(body, name, t, ins, row_in, outs, acc_outs=(), tr=ROW_TILE):
    tr = _tile(t, tr)

    def in_spec(a, tiled):
        if isinstance(tiled, tuple):
            width, j = tiled
            return pl.BlockSpec((tr, width), lambda i: (i, j))
        return pl.BlockSpec((tr, a.shape[1]), lambda i: (i, 0)) if tiled else pl.BlockSpec(a.shape, lambda i: (0, 0))

    in_specs = [in_spec(a, tiled) for a, tiled in zip(ins, row_in)]
    out_specs = [pl.BlockSpec((tr, n), lambda i: (i, 0)) for n, _ in outs]
    out_specs += [pl.BlockSpec((8, n), lambda i: (0, 0)) for n in acc_outs]
    out_shape = [jax.ShapeDtypeStruct((t, n), dt) for n, dt in outs]
    out_shape += [jax.ShapeDtypeStruct((8, n), F32) for n in acc_outs]
    return pl.pallas_call(
        body, name=name, grid=(t // tr,), in_specs=in_specs, out_specs=out_specs, out_shape=out_shape,
        compiler_params=_params(("arbitrary",) if acc_outs else ("parallel",)),
    )(*ins)


def norm_pre(name, x, g):
    t, d = x.shape

    def body(x_ref, g_ref, h_ref):
        v = x_ref[...]
        h_ref[...] = (v * _rstd(v) * g_ref[...]).astype(BF16)

    return _row_call(body, name, t, [x, g], [True, False], [(d, BF16)])[0]


def post_pre(name, x, m, g_post, g_pre):
    t, d = x.shape

    def body(x_ref, m_ref, gp_ref, gn_ref, xo_ref, h_ref):
        mv = m_ref[...]
        xn = x_ref[...] + mv * _rstd(mv) * gp_ref[...]
        xo_ref[...] = xn
        h_ref[...] = (xn * _rstd(xn) * gn_ref[...]).astype(BF16)

    return _row_call(body, name, t, [x, m, g_post, g_pre], [True, True, False, False], [(d, F32), (d, BF16)])


def post_loss(name, x, f, g_post, target):
    t, d = x.shape

    def body(x_ref, f_ref, g_ref, t_ref, dx_ref, df_ref, loss_ref, dg_ref):
        fv = f_ref[...]
        g = g_ref[...]
        out = x_ref[...] + fv * _rstd(fv) * g
        err = out - t_ref[...]
        dx = err * (1.0 / d)
        dx_ref[...] = dx
        dfv, dg_rows = _rms_bwd(fv, g, dx)
        df_ref[...] = dfv.astype(BF16)
        _accumulate(loss_ref, _fold8(_fold_lanes(err * err)))
        _accumulate(dg_ref, _fold8(dg_rows))

    return _row_call(body, name, t, [x, f, g_post, target], [True, True, False, True],
                     [(d, F32), (d, BF16)], acc_outs=(128, d))


def bwd_pre_post(name, dx_out, x_in, g_pre, dh, f_prev, g_post_prev):
    t, d = x_in.shape

    def body(dxo_ref, x_ref, gpre_ref, dh_ref, f_ref, gpost_ref, dxi_ref, df_ref, dgpre_ref, dgpost_ref):
        dxv, dgpre_rows = _rms_bwd(x_ref[...], gpre_ref[...], dh_ref[...].astype(F32))
        dxi = dxo_ref[...] + dxv
        dxi_ref[...] = dxi
        dfv, dgpost_rows = _rms_bwd(f_ref[...], gpost_ref[...], dxi)
        df_ref[...] = dfv.astype(BF16)
        _accumulate(dgpre_ref, _fold8(dgpre_rows))
        _accumulate(dgpost_ref, _fold8(dgpost_rows))

    return _row_call(body, name, t, [dx_out, x_in, g_pre, dh, f_prev, g_post_prev],
                     [True, True, False, True, True, False], [(d, F32), (d, BF16)], acc_outs=(d, d))


def bwd_pre_final(name, dx_out, x_in, g_pre, dh):
    t, d = x_in.shape

    def body(dxo_ref, x_ref, gpre_ref, dh_ref, dxi_ref, dgpre_ref):
        dxv, dgpre_rows = _rms_bwd(x_ref[...], gpre_ref[...], dh_ref[...].astype(F32))
        dxi_ref[...] = dxo_ref[...] + dxv
        _accumulate(dgpre_ref, _fold8(dgpre_rows))

    return _row_call(body, name, t, [dx_out, x_in, g_pre, dh], [True, True, False, True], [(d, F32)], acc_outs=(d,))


def _layer_norm_parts(cv):
    mu = jnp.mean(cv, axis=-1, keepdims=True)
    xc = cv - mu
    rstd = lax.rsqrt(jnp.mean(xc * xc, axis=-1, keepdims=True) + NORM_EPS)
    return xc * rstd, rstd


def ln_silu(name, cv, g, b, y, y_block):
    t, n = cv.shape
    tr = _tile(t, ROW_TILE)

    def body(c_ref, g_ref, b_ref, y_in_ref, y_ref):
        chat, _ = _layer_norm_parts(c_ref[...])
        ln = chat * g_ref[...] + b_ref[...]
        y_ref[...] = (ln * jax.nn.sigmoid(ln)).astype(BF16)

    vec = pl.BlockSpec((1, n), lambda i: (0, 0))
    return pl.pallas_call(
        body, name=name, grid=(t // tr,),
        in_specs=[pl.BlockSpec((tr, n), lambda i: (i, 0)), vec, vec, ANY],
        out_specs=pl.BlockSpec((tr, n), lambda i: (i, y_block)),
        out_shape=jax.ShapeDtypeStruct(y.shape, y.dtype), input_output_aliases={3: 0},
        compiler_params=_params(("parallel",)),
    )(cv, g, b, y)


def ln_silu_bwd(name, cv, g, b, dy, dy_block):
    t, n = cv.shape

    def body(c_ref, g_ref, b_ref, dy_ref, dc_ref, dg_ref, db_ref):
        chat, rstd = _layer_norm_parts(c_ref[...])
        g = g_ref[...]
        ln = chat * g + b_ref[...]
        s = jax.nn.sigmoid(ln)
        dln = dy_ref[...].astype(F32) * (s * (1.0 + ln * (1.0 - s)))
        dchat = dln * g
        dc_ref[...] = rstd * (dchat - jnp.mean(dchat, axis=-1, keepdims=True)
                              - chat * jnp.mean(dchat * chat, axis=-1, keepdims=True))
        _accumulate(dg_ref, _fold8(dln * chat))
        _accumulate(db_ref, _fold8(dln))

    return _row_call(body, name, t, [cv, g, b, dy], [True, False, False, (n, dy_block)], [(n, F32)], acc_outs=(n, n))


def _chunks(t, fn, tc=TIME_CHUNK):
    tc = _tile(t, tc)

    def step(i, carry):
        fn(pl.multiple_of(i * tc, tc), tc)
        return carry

    lax.fori_loop(0, t // tc, step, 0)


def _shifted(window, offsets, tc):
    by_residue = {}
    for k, off in enumerate(offsets):
        by_residue.setdefault(off % 8, []).append((k, off))
    for res, taps in by_residue.items():
        base = window[res:res + tc + max(off for _, off in taps) - res, :]
        for k, off in taps:
            yield k, base[off - res:off - res + tc, :]


def _taps(window, w_ref, offsets, tc, flip=False):
    acc = None
    for k, rows in _shifted(window, offsets, tc):
        kk = len(offsets) - 1 - k if flip else k
        term = w_ref[kk:kk + 1, :] * rows
        acc = term if acc is None else acc + term
    return acc


def _window_sums(win, tc, causal):
    sums = []
    cur, rows, step = win, tc + HALO, 1
    for _ in POOL_WINDOWS:
        rows -= 8
        if causal:
            cur = cur[8:8 + rows, :] + cur[8 - step:8 - step + rows, :]
            sums.append(cur[rows - tc:rows, :])
        else:
            cur = cur[0:rows, :] + cur[step:step + rows, :]
            sums.append(cur[0:tc, :])
        step *= 2
    return sums


def _pick(vals, g):
    out = vals[-1]
    for i in range(len(vals) - 2, -1, -1):
        out = jnp.where(g == i, vals[i], out)
    return out


def _pool_count(s, tc, g):
    t1 = (lax.broadcasted_iota(jnp.int32, (tc, 1), 0) + (s + 1)).astype(F32)
    width = _pick([float(w) for w in POOL_WINDOWS], g)
    return jnp.minimum(t1, width)


def pool_fwd(name, z, pool_w, pool_scale, d_pool, y_width):
    t = z.shape[0]
    ng, pg = pool_w.shape[0], pool_w.shape[1]

    def body(u_ref, w_ref, s_ref, pooled_ref, y_ref, pad):
        g = pl.program_id(0)
        pad[pl.ds(0, HALO), :] = jnp.zeros((HALO, pg), F32)

        def fill(s, tc):
            pad[pl.ds(HALO + s, tc), :] = u_ref[pl.ds(s, tc), :].astype(F32)

        def chunk(s, tc):
            win = pad[pl.ds(s, tc + HALO), :]
            total = _pick(_window_sums(win, tc, causal=True), g)
            pooled = total / _pool_count(s, tc, g) - win[HALO:HALO + tc, :]
            pooled_ref[pl.ds(s, tc), :] = pooled.astype(BF16)

        _chunks(t, fill)
        _chunks(t, chunk)
        mixed = jnp.dot(pooled_ref[...], w_ref[...], preferred_element_type=F32)
        y_ref[...] = (mixed * s_ref[...]).astype(BF16)

    col = pl.BlockSpec((t, pg), lambda g: (0, g))
    return pl.pallas_call(
        body, name=name, grid=(ng,),
        in_specs=[col, pl.BlockSpec((None, pg, pg), lambda g: (g, 0, 0)), pl.BlockSpec((1, pg), lambda g: (0, g))],
        out_specs=[col, col],
        out_shape=[jax.ShapeDtypeStruct((t, d_pool), BF16), jax.ShapeDtypeStruct((t, y_width), BF16)],
        scratch_shapes=[pltpu.VMEM((t + HALO, pg), F32)],
        compiler_params=_params(("parallel",)),
    )(z, pool_w, pool_scale)


def pool_bwd(name, pooled, dy, pool_w, pool_scale, dz):
    t, d_pool = pooled.shape
    ng, pg = pool_w.shape[0], pool_w.shape[1]

    def body(p_ref, dy_ref, w_ref, s_ref, dz_ref, du_ref, dw_ref, ds_ref, pad):
        g = pl.program_id(0)
        w = w_ref[...]
        dyv = dy_ref[...].astype(F32)
        mixed = jnp.dot(p_ref[...], w, preferred_element_type=F32)
        ds_ref[...] = jnp.sum(dyv * mixed, axis=0, keepdims=True)
        dmixed = (dyv * s_ref[...]).astype(BF16)
        dw_ref[...] = lax.dot_general(p_ref[...], dmixed, TN, preferred_element_type=F32)
        pad[...] = jnp.zeros((t + HALO, pg), F32)
        pad[pl.ds(0, t), :] = lax.dot_general(dmixed, w, NT, preferred_element_type=F32)

        def scale(s, tc):
            pad[pl.ds(s, tc), :] = pad[pl.ds(s, tc), :] / _pool_count(s, tc, g)

        def chunk(s, tc):
            win = pad[pl.ds(s, tc + HALO), :]
            total = _pick(_window_sums(win, tc, causal=False), g)
            du_ref[pl.ds(s, tc), :] = (total - win[0:tc, :] * _pool_count(s, tc, g)).astype(BF16)

        _chunks(t, scale)
        _chunks(t, chunk)

    col = pl.BlockSpec((t, pg), lambda g: (0, g))
    vec = pl.BlockSpec((1, pg), lambda g: (0, g))
    mat = pl.BlockSpec((None, pg, pg), lambda g: (g, 0, 0))
    return pl.pallas_call(
        body, name=name, grid=(ng,),
        in_specs=[col, col, mat, vec, ANY], out_specs=[col, mat, vec],
        out_shape=[jax.ShapeDtypeStruct(dz.shape, dz.dtype), jax.ShapeDtypeStruct((ng, pg, pg), F32),
                   jax.ShapeDtypeStruct((1, d_pool), F32)],
        input_output_aliases={4: 0},
        scratch_shapes=[pltpu.VMEM((t + HALO, pg), F32)],
        compiler_params=_params(("parallel",)),
    )(pooled, dy, pool_w, pool_scale, dz)


def conv_fwd(name, z, conv_w, conv_b, d_pool, d_conv):
    t = z.shape[0]
    kw = conv_w.shape[0]
    tc_ch = _tile(d_conv, CHANNEL_TILE)
    v0, g0 = d_pool // tc_ch, (d_pool + d_conv) // tc_ch

    def body(v_ref, g_ref, w_ref, b_ref, c_ref, pad):
        pad[pl.ds(0, HALO), :] = jnp.zeros((HALO, tc_ch), F32)

        def fill(s, tc):
            pad[pl.ds(HALO + s, tc), :] = v_ref[pl.ds(s, tc), :].astype(F32) * jax.nn.sigmoid(g_ref[pl.ds(s, tc), :].astype(F32))

        def chunk(s, tc):
            win = pad[pl.ds(s, tc + HALO), :]
            c_ref[pl.ds(s, tc), :] = _taps(win, w_ref, [HALO - (kw - 1) + k for k in range(kw)], tc) + b_ref[...]

        _chunks(t, fill)
        _chunks(t, chunk)

    return pl.pallas_call(
        body, name=name, grid=(d_conv // tc_ch,),
        in_specs=[pl.BlockSpec((t, tc_ch), lambda j: (0, v0 + j)), pl.BlockSpec((t, tc_ch), lambda j: (0, g0 + j)),
                  pl.BlockSpec((kw, tc_ch), lambda j: (0, j)), pl.BlockSpec((1, tc_ch), lambda j: (0, j))],
        out_specs=pl.BlockSpec((t, tc_ch), lambda j: (0, j)),
        out_shape=jax.ShapeDtypeStruct((t, d_conv), F32),
        scratch_shapes=[pltpu.VMEM((t + HALO, tc_ch), F32)],
        compiler_params=_params(("parallel",)),
    )(z, z, conv_w, conv_b)


def conv_bwd(name, z, dc, conv_w, d_pool, d_conv):
    t = z.shape[0]
    kw = conv_w.shape[0]
    tc_ch = _tile(d_conv, CHANNEL_TILE)
    v0, g0 = d_pool // tc_ch, (d_pool + d_conv) // tc_ch

    def body(v_ref, g_ref, dc_ref, w_ref, dz_ref, dw_ref, db_ref, pad_a, pad_dc, acc_w, acc_b, stash):
        @pl.when(pl.program_id(1) == 0)
        def _():
            compute(v_ref, g_ref, dc_ref, w_ref, dz_ref, stash, dw_ref, db_ref, pad_a, pad_dc, acc_w, acc_b)

        @pl.when(pl.program_id(1) == 1)
        def _():
            dz_ref[...] = stash[...]

    def compute(v_ref, g_ref, dc_ref, w_ref, dv_ref, dg_ref, dw_ref, db_ref, pad_a, pad_dc, acc_w, acc_b):
        pad_a[pl.ds(0, HALO), :] = jnp.zeros((HALO, tc_ch), F32)
        pad_dc[pl.ds(t, HALO), :] = jnp.zeros((HALO, tc_ch), F32)
        acc_w[...] = jnp.zeros_like(acc_w)
        acc_b[...] = jnp.zeros_like(acc_b)

        def fill(s, tc):
            pad_a[pl.ds(HALO + s, tc), :] = v_ref[pl.ds(s, tc), :].astype(F32) * jax.nn.sigmoid(g_ref[pl.ds(s, tc), :].astype(F32))
            pad_dc[pl.ds(s, tc), :] = dc_ref[pl.ds(s, tc), :]

        def chunk(s, tc):
            dcv = pad_dc[pl.ds(s, tc), :]
            win_a = pad_a[pl.ds(s, tc + HALO), :]
            for k, rows in _shifted(win_a, [HALO - (kw - 1) + k for k in range(kw)], tc):
                acc_w[pl.ds(8 * k, 8), :] += _fold8(dcv * rows)
            acc_b[...] += _fold8(dcv)
            da = _taps(pad_dc[pl.ds(s, tc + HALO), :], w_ref, list(range(kw)), tc, flip=True)
            vv = v_ref[pl.ds(s, tc), :].astype(F32)
            sg = jax.nn.sigmoid(g_ref[pl.ds(s, tc), :].astype(F32))
            dv_ref[pl.ds(s, tc), :] = (da * sg).astype(BF16)
            dg_ref[pl.ds(s, tc), :] = (da * vv * sg * (1.0 - sg)).astype(BF16)

        _chunks(t, fill)
        _chunks(t, chunk)
        for k in range(kw):
            dw_ref[k:k + 1, :] = jnp.sum(acc_w[pl.ds(8 * k, 8), :], axis=0, keepdims=True)
        db_ref[...] = jnp.sum(acc_b[...], axis=0, keepdims=True)

    return pl.pallas_call(
        body, name=name, grid=(d_conv // tc_ch, 2),
        in_specs=[pl.BlockSpec((t, tc_ch), lambda j, p: (0, v0 + j)), pl.BlockSpec((t, tc_ch), lambda j, p: (0, g0 + j)),
                  pl.BlockSpec((t, tc_ch), lambda j, p: (0, j)), pl.BlockSpec((kw, tc_ch), lambda j, p: (0, j))],
        out_specs=[pl.BlockSpec((t, tc_ch), lambda j, p: (0, v0 + p * (g0 - v0) + j)),
                   pl.BlockSpec((kw, tc_ch), lambda j, p: (0, j)), pl.BlockSpec((1, tc_ch), lambda j, p: (0, j))],
        out_shape=[jax.ShapeDtypeStruct((t, d_pool + 2 * d_conv), BF16),
                   jax.ShapeDtypeStruct((kw, d_conv), F32), jax.ShapeDtypeStruct((1, d_conv), F32)],
        scratch_shapes=[pltpu.VMEM((t + HALO, tc_ch), F32), pltpu.VMEM((t + HALO, tc_ch), F32),
                        pltpu.VMEM((8 * kw, tc_ch), F32), pltpu.VMEM((8, tc_ch), F32), pltpu.VMEM((t, tc_ch), BF16)],
        compiler_params=_params(("parallel", "arbitrary")),
    )(z, z, dc, conv_w)


def short_fwd(name, z, conv_w, d_short):
    t = z.shape[0]
    kw = conv_w.shape[0]
    tc_ch = _tile(d_short, CHANNEL_TILE)
    nt = d_short // tc_ch

    def body(b_ref, c_ref, u_ref, w_ref, y_ref, pad):
        pad[pl.ds(0, HALO), :] = jnp.zeros((HALO, tc_ch), F32)

        def fill(s, tc):
            pad[pl.ds(HALO + s, tc), :] = c_ref[pl.ds(s, tc), :].astype(F32) * u_ref[pl.ds(s, tc), :].astype(F32)

        def chunk(s, tc):
            win = pad[pl.ds(s, tc + HALO), :]
            cq = _taps(win, w_ref, [HALO - (kw - 1) + k for k in range(kw)], tc)
            y_ref[pl.ds(s, tc), :] = (b_ref[pl.ds(s, tc), :].astype(F32) * cq).astype(BF16)

        _chunks(t, fill)
        _chunks(t, chunk)

    return pl.pallas_call(
        body, name=name, grid=(nt,),
        in_specs=[pl.BlockSpec((t, tc_ch), lambda j: (0, j)), pl.BlockSpec((t, tc_ch), lambda j: (0, nt + j)),
                  pl.BlockSpec((t, tc_ch), lambda j: (0, 2 * nt + j)), pl.BlockSpec((kw, tc_ch), lambda j: (0, j))],
        out_specs=pl.BlockSpec((t, tc_ch), lambda j: (0, j)),
        out_shape=jax.ShapeDtypeStruct((t, d_short), BF16),
        scratch_shapes=[pltpu.VMEM((t + HALO, tc_ch), F32)],
        compiler_params=_params(("parallel",)),
    )(z, z, z, conv_w)


def short_bwd(name, z, dy, conv_w, d_short):
    t = z.shape[0]
    kw = conv_w.shape[0]
    tc_ch = _tile(d_short, CHANNEL_TILE)
    nt = d_short // tc_ch

    def body(b_ref, c_ref, u_ref, dy_ref, w_ref, dz_ref, dw_ref, pad_q, pad_dcq, acc_w, stash):
        part = pl.program_id(1)

        @pl.when(part == 0)
        def _():
            compute(b_ref, c_ref, u_ref, dy_ref, w_ref, dz_ref, stash.at[0], stash.at[1], dw_ref, pad_q, pad_dcq, acc_w)

        @pl.when(part > 0)
        def _():
            dz_ref[...] = stash[part - 1]

    def compute(b_ref, c_ref, u_ref, dy_ref, w_ref, db_ref, dcg_ref, du_ref, dw_ref, pad_q, pad_dcq, acc_w):
        pad_q[pl.ds(0, HALO), :] = jnp.zeros((HALO, tc_ch), F32)
        pad_dcq[pl.ds(t, HALO), :] = jnp.zeros((HALO, tc_ch), F32)
        acc_w[...] = jnp.zeros_like(acc_w)

        def fill(s, tc):
            rows = pl.ds(s, tc)
            pad_q[pl.ds(HALO + s, tc), :] = c_ref[rows, :].astype(F32) * u_ref[rows, :].astype(F32)
            pad_dcq[rows, :] = dy_ref[rows, :].astype(F32) * b_ref[rows, :].astype(F32)

        def chunk(s, tc):
            rows = pl.ds(s, tc)
            win_q = pad_q[pl.ds(s, tc + HALO), :]
            dcq = pad_dcq[rows, :]
            cq = None
            for k in range(kw):
                off = HALO - (kw - 1) + k
                shifted = win_q[off:off + tc, :]
                acc_w[pl.ds(8 * k, 8), :] += _fold8(dcq * shifted)
                term = w_ref[k:k + 1, :] * shifted
                cq = term if cq is None else cq + term
            db_ref[rows, :] = (dy_ref[rows, :].astype(F32) * cq).astype(BF16)
            win_d = pad_dcq[pl.ds(s, tc + HALO), :]
            dq = None
            for j in range(kw):
                term = w_ref[kw - 1 - j:kw - j, :] * win_d[j:j + tc, :]
                dq = term if dq is None else dq + term
            dcg_ref[rows, :] = (dq * u_ref[rows, :].astype(F32)).astype(BF16)
            du_ref[rows, :] = (dq * c_ref[rows, :].astype(F32)).astype(BF16)

        _chunks(t, fill)
        _chunks(t, chunk)
        for k in range(kw):
            dw_ref[k:k + 1, :] = jnp.sum(acc_w[pl.ds(8 * k, 8), :], axis=0, keepdims=True)

    zspec = [pl.BlockSpec((t, tc_ch), lambda j, p, o=o: (0, o * nt + j)) for o in range(3)]
    return pl.pallas_call(
        body, name=name, grid=(nt, 3),
        in_specs=[*zspec, pl.BlockSpec((t, tc_ch), lambda j, p: (0, j)), pl.BlockSpec((kw, tc_ch), lambda j, p: (0, j))],
        out_specs=[pl.BlockSpec((t, tc_ch), lambda j, p: (0, p * nt + j)),
                   pl.BlockSpec((kw, tc_ch), lambda j, p: (0, j))],
        out_shape=[jax.ShapeDtypeStruct((t, 3 * d_short), BF16), jax.ShapeDtypeStruct((kw, d_short), F32)],
        scratch_shapes=[pltpu.VMEM((t + HALO, tc_ch), F32), pltpu.VMEM((t + HALO, tc_ch), F32),
                        pltpu.VMEM((8 * kw, tc_ch), F32), pltpu.VMEM((2, t, tc_ch), BF16)],
        compiler_params=_params(("parallel", "arbitrary")),
    )(z, z, z, dy, conv_w)


def adamw(name, w, m, v, contributions):
    r, c = w.shape
    nc = len(contributions)
    n_slots = contributions[0].shape[0]
    tr = 256 if c <= 1024 else 128
    if any(a.shape[1] % tr for a in contributions):
        assert nc == 1
        tr = r
    tiles = [a.shape[1] // tr for a in contributions]
    first = [sum(tiles[:j]) for j in range(nc)]

    def body(w_ref, m_ref, v_ref, *rest):
        g_refs, (grad_ref, delta_ref, nm_ref, nv_ref) = rest[:nc], rest[nc:]
        i = pl.program_id(0)
        g = None
        for j, g_ref in enumerate(g_refs):
            s = g_ref[0].astype(F32)
            for slot in range(1, n_slots):
                s = s + g_ref[slot].astype(F32)
            g = s if g is None else jnp.where(i >= first[j], s, g)
        nm = ADAM_B1 * m_ref[...] + (1.0 - ADAM_B1) * g
        nv = ADAM_B2 * v_ref[...] + (1.0 - ADAM_B2) * (g * g)
        m_hat = nm / (1.0 - ADAM_B1 ** ADAM_STEP)
        v_hat = nv / (1.0 - ADAM_B2 ** ADAM_STEP)
        grad_ref[...] = g
        delta_ref[...] = -ADAM_LR * (m_hat / (jnp.sqrt(v_hat) + ADAM_EPS) + ADAM_WD * w_ref[...])
        nm_ref[...] = nm
        nv_ref[...] = nv

    blk = pl.BlockSpec((tr, c), lambda i: (i, 0))
    g_specs = [pl.BlockSpec((n_slots, tr, c), lambda i, j=j: (0, jnp.clip(i - first[j], 0, tiles[j] - 1), 0))
               for j in range(nc)]
    return pl.pallas_call(
        body, name=name, grid=(r // tr,),
        in_specs=[blk, blk, blk, *g_specs],
        out_specs=[blk] * 4, out_shape=[jax.ShapeDtypeStruct((r, c), F32)] * 4,
        compiler_params=_params(("parallel",)),
    )(w, m, v, *contributions)


def _pad_rows(a, rows):
    return jnp.pad(a, ((0, rows - a.shape[0]), (0, 0)))


def kernel(x, mix_pre_g, mix_post_g, ffn_pre_g, ffn_post_g, ab_w_in, pool_w, pool_scale, conv_w, conv_b, conv_ln_g, conv_ln_b, ab_w_out, sc_w_in, sc_conv_w, sc_w_out, ffn_w1, ffn_w2, loss_target, m_mix_pre_g, m_mix_post_g, m_ffn_pre_g, m_ffn_post_g, m_ab_w_in, m_pool_w, m_pool_scale, m_conv_w, m_conv_b, m_conv_ln_g, m_conv_ln_b, m_ab_w_out, m_sc_w_in, m_sc_conv_w, m_sc_w_out, m_ffn_w1, m_ffn_w2, v_mix_pre_g, v_mix_post_g, v_ffn_pre_g, v_ffn_post_g, v_ab_w_in, v_pool_w, v_pool_scale, v_conv_w, v_conv_b, v_conv_ln_g, v_conv_ln_b, v_ab_w_out, v_sc_w_in, v_sc_conv_w, v_sc_w_out, v_ffn_w1, v_ffn_w2):
    t, d = x.shape[1], x.shape[2]
    d_pool = pool_scale.shape[1]
    d_conv = conv_b.shape[1]
    d_short = d
    ng, pg = pool_w.shape[1], pool_w.shape[3]
    kw, ks = conv_w.shape[1], sc_conv_w.shape[1]
    nb_ab, nb_sc, nb_ff = ab_w_in.shape[2], sc_w_in.shape[2], ffn_w1.shape[2]

    xs = x[0]
    target = loss_target[0]

    lanes = min(128, d_conv // N_DEV)
    small_rows = [kw * (d_conv // N_DEV) // lanes, ks * (d_short // N_DEV) // lanes, ng * (pg // N_DEV) * pg // lanes]
    small_total = -(-sum(small_rows) // 8) * 8
    r0, r1, r2 = small_rows[0], small_rows[0] + small_rows[1], sum(small_rows)

    def pack_small(a_conv, a_sconv, a_pool):
        parts = [a_conv[0].reshape(-1, lanes), a_sconv[0].reshape(-1, lanes), a_pool[0].reshape(-1, lanes)]
        return _pad_rows(jnp.concatenate(parts, axis=0), small_total)

    shards = {
        "ab_in": (ab_w_in, 0, BF16), "small": (pack_small(conv_w, sc_conv_w, pool_w)[None], 0, F32),
        "ab_out": (ab_w_out, 0, BF16), "ff1_0": (ffn_w1, 0, BF16), "ff2_0": (ffn_w2, 0, BF16),
        "sc_in": (sc_w_in, 0, BF16), "sc_out": (sc_w_out, 0, BF16),
        "ff1_1": (ffn_w1, 1, BF16), "ff2_1": (ffn_w2, 1, BF16)}
    direct = ["ab_in", "small", "ab_out"]
    zones = {nm: place_shard("place_" + nm, *shards[nm]) for nm in direct}
    started, token = copies_start("gather_start", [[zones[nm]] for nm in direct], _first_hop, 4)
    started = dict(zip(direct, started))
    ring = {}
    for nm in ["ff1_0", "ff2_0"]:
        zones[nm] = place_shard("place_" + nm, *shards[nm], deps=[token])
        (ring[nm],), token = copies_start("ring_start_" + nm, [[zones[nm]]], _ring_hop1, 3, deps=[token])
    for nm in shards:
        if nm not in zones:
            zones[nm] = place_shard("place_" + nm, *shards[nm], deps=[token])

    ties = [0]

    def after(v, *deps):
        ties[0] += 1
        return tie(f"tie_{ties[0]}", v, *deps)

    def fetch_begin(nm, dep):
        (zone,) = copies_wait("gather_wait_" + nm, started[nm], _first_hop, dep)
        (hop,), tok = copies_start("forward_start_" + nm, [[zone]], _second_hop, 3)
        return hop, tok

    def fetch_end(nm, hop, dep):
        return copies_wait("forward_wait_" + nm, hop, _second_hop, dep)[0]

    def ring_step(tag, dep, second=None, first=None, third=None):
        names, groups, hops, counts = [], [], [], []
        if second is not None:
            groups.append(copies_wait("ring1_wait_" + second, ring[second], _ring_hop1, dep))
            names, hops, counts = names + [second], hops + [_ring_hop2], counts + [4]
        if first is not None:
            groups.append([zones[first]])
            names, hops, counts = names + [first], hops + [_ring_hop1], counts + [3]
        if third is not None:
            groups.append(copies_wait("ring2_wait_" + third, ring[third], _ring_hop2, dep))
            names, hops, counts = names + [third], hops + [_ring_hop3], counts + [1]
        begun, tok = copies_start("ring_start_" + tag, groups, hops, counts, deps=[dep])
        ring.update(zip(names, begun))
        return tok

    def ring_done(nm, dep):
        return copies_wait("ring3_wait_" + nm, ring[nm], _ring_hop3, dep)[0]

    relu = lambda r: jnp.maximum(r, 0.0)
    square = lambda a: a * a
    relu2_bwd = lambda r, a: r * (2.0 * a.astype(F32))

    def row(vec, l):
        return vec[l:l + 1]

    hop_small, _ = fetch_begin("small", token)
    hop_ab_in, tok = fetch_begin("ab_in", token)
    w_small = fetch_end("small", hop_small, tok)
    w_ab_in = fetch_end("ab_in", hop_ab_in, tok)
    w_conv = w_small[:, :r0].reshape(N_DEV, kw, -1).transpose(1, 0, 2).reshape(kw, d_conv)
    w_sconv = w_small[:, r0:r1].reshape(N_DEV, ks, -1).transpose(1, 0, 2).reshape(ks, d_short)
    w_pool = w_small[:, r1:r2].reshape(N_DEV, ng, -1, pg).transpose(1, 0, 2, 3).reshape(ng, pg, pg).astype(BF16)
    h0 = norm_pre("norm_pre", xs, after(row(mix_pre_g, 0), token))
    z0 = mm_nn_blocked("ab_in", h0, w_ab_in, out_dtype=BF16)
    hop, tok = fetch_begin("ab_out", z0)
    z0 = after(z0, tok)
    pooled, y0 = pool_fwd("pool_fwd", z0, w_pool, pool_scale, d_pool, d_pool + d_conv)
    cv = conv_fwd("conv_fwd", z0, w_conv, conv_b, d_pool, d_conv)
    y0 = ln_silu("ln_silu", cv, conv_ln_g, conv_ln_b, y0, d_pool // d_conv)
    w_ab_out = fetch_end("ab_out", hop, y0)
    tok = ring_step("a", w_ab_out, second="ff1_0", first="sc_in")
    y0 = after(y0, tok)
    m0 = mm_nn("ab_out", y0, w_ab_out.reshape(d_pool + d_conv, d), out_dtype=F32)
    x1, h1 = post_pre("post_pre_0", xs, m0, row(mix_post_g, 0), row(ffn_pre_g, 0))
    tok = ring_step("b", h1, second="ff2_0", first="sc_out", third="ff1_0")
    w_ff1_0 = ring_done("ff1_0", tok)
    a0 = mm_nn_blocked("ffn0_up", h1, w_ff1_0, out_dtype=BF16, epilogue=relu)
    tok = ring_step("c", a0, second="sc_in", first="ff1_1", third="ff2_0")
    w_ff2_0 = ring_done("ff2_0", tok).reshape(-1, d)
    f0 = mm_nn("ffn0_down", a0, w_ff2_0, out_dtype=F32, tk=2048, lhs_fn=square)
    tok = ring_step("d", f0, second="sc_out", first="ff2_1", third="sc_in")
    f0 = after(f0, tok)
    x2, h2 = post_pre("post_pre_1", x1, f0, row(ffn_post_g, 0), row(mix_pre_g, 1))
    w_sc_in = ring_done("sc_in", h2)
    z1 = mm_nn_blocked("sc_in", h2, w_sc_in, out_dtype=BF16)
    tok = ring_step("e", z1, second="ff1_1", third="sc_out")
    z1 = after(z1, tok)
    y1 = short_fwd("short_fwd", z1, w_sconv, d_short)
    w_sc_out = ring_done("sc_out", y1).reshape(d_short, d)
    m1 = mm_nn("sc_out", y1, w_sc_out, out_dtype=F32)
    tok = ring_step("f", m1, second="ff2_1")
    m1 = after(m1, tok)
    x3, h3 = post_pre("post_pre_2", x2, m1, row(mix_post_g, 1), row(ffn_pre_g, 1))
    tok = ring_step("g", h3, third="ff1_1")
    w_ff1_1 = ring_done("ff1_1", tok)
    a1 = mm_nn_blocked("ffn1_up", h3, w_ff1_1, out_dtype=BF16, epilogue=relu)
    tok = ring_step("h", a1, third="ff2_1")
    w_ff2_1 = ring_done("ff2_1", tok).reshape(-1, d)
    f1 = mm_nn("ffn1_down", a1, w_ff2_1, out_dtype=F32, tk=2048, lhs_fn=square)
    dx4, df1, loss_part, dg_ffn_post1 = post_loss("post_loss", x3, f1, row(ffn_post_g, 1), target)
    loss = lax.psum(jnp.sum(loss_part) * (0.5 / d), ("x", "y", "c"))

    def reduce_begin(tag, g):
        zone = lax.empty((N_CHIP,) + g.shape[1:], g.dtype)
        (hop,), tok = copies_start("pair_start_" + tag, [[g, zone]], _pair_hop, N_CHIP)
        return hop, tok

    def reduce_middle(tag, hop, dep):
        g, from_sibling = copies_wait("pair_wait_" + tag, hop, _pair_hop, dep)
        pair_sum, zone = pair_add("pair_add_" + tag, g, from_sibling)
        (hop2,), tok = copies_start("chips_start_" + tag, [[pair_sum, zone]], _chip_hop, 3)
        return hop2, tok

    def reduce_end(tag, hop2, dep):
        return copies_wait("chips_wait_" + tag, hop2, _chip_hop, dep)[1]

    dw = mm_tn("ffn1_dw2", a1, df1, out_dtype=BF16, lhs_fn=square)
    red_ff2_1, tok = reduce_begin("ff2_1", dw.reshape(N_DEV, -1, d))
    df1 = after(df1, tok)
    dpre = mm_nt("ffn1_da", df1, w_ff2_1, out_dtype=BF16, extra=a1, epilogue=relu2_bwd)
    dw = mm_tn_blocked("ffn1_dw1", h3, dpre, nb_ff, out_dtype=BF16)
    red_ff1_1, tok = reduce_begin("ff1_1", dw)
    dpre = after(dpre, tok)
    dh3 = mm_nt_blocked("ffn1_dh", dpre, w_ff1_1, out_dtype=BF16)
    red_ff2_1, tok = reduce_middle("ff2_1", red_ff2_1, dh3)
    dh3 = after(dh3, tok)
    dx3, dm1, dg_ffn_pre1, dg_mix_post1 = bwd_pre_post("bwd_3", dx4, x3, row(ffn_pre_g, 1), dh3, m1, row(mix_post_g, 1))

    dw = mm_tn("sc_dwout", y1, dm1, out_dtype=BF16)
    red_sc_out, tok = reduce_begin("sc_out", dw.reshape(N_DEV, -1, d))
    dm1 = after(dm1, tok)
    dy1 = mm_nt("sc_dy", dm1, w_sc_out, out_dtype=BF16)
    red_ff1_1, tok = reduce_middle("ff1_1", red_ff1_1, dy1)
    dy1 = after(dy1, tok)
    dz1, dw_sconv = short_bwd("short_bwd", z1, dy1, w_sconv, d_short)
    dw = mm_tn_blocked("sc_dwin", h2, dz1, nb_sc, out_dtype=BF16)
    red_sc_in, tok = reduce_begin("sc_in", dw)
    dz1 = after(dz1, tok)
    dh2 = mm_nt_blocked("sc_dh", dz1, w_sc_in, out_dtype=BF16)
    red_sc_out, tok = reduce_middle("sc_out", red_sc_out, dh2)
    dh2 = after(dh2, tok)
    dx2, df0, dg_mix_pre1, dg_ffn_post0 = bwd_pre_post("bwd_2", dx3, x2, row(mix_pre_g, 1), dh2, f0, row(ffn_post_g, 0))

    dw = mm_tn("ffn0_dw2", a0, df0, out_dtype=BF16, lhs_fn=square)
    red_ff2_0, tok = reduce_begin("ff2_0", dw.reshape(N_DEV, -1, d))
    df0 = after(df0, tok)
    dpre = mm_nt("ffn0_da", df0, w_ff2_0, out_dtype=BF16, extra=a0, epilogue=relu2_bwd)
    red_sc_in, tok = reduce_middle("sc_in", red_sc_in, dpre)
    dpre = after(dpre, tok)
    dw = mm_tn_blocked("ffn0_dw1", h1, dpre, nb_ff, out_dtype=BF16)
    red_ff1_0, tok = reduce_begin("ff1_0", dw)
    dpre = after(dpre, tok)
    dh1 = mm_nt_blocked("ffn0_dh", dpre, w_ff1_0, out_dtype=BF16)
    red_ff2_0, tok = reduce_middle("ff2_0", red_ff2_0, dh1)
    dh1 = after(dh1, tok)
    dx1, dm0, dg_ffn_pre0, dg_mix_post0 = bwd_pre_post("bwd_1", dx2, x1, row(ffn_pre_g, 0), dh1, m0, row(mix_post_g, 0))

    dw = mm_tn("ab_dwout", y0, dm0, out_dtype=BF16)
    red_ab_out, tok = reduce_begin("ab_out", dw.reshape(N_DEV, -1, d))
    dm0 = after(dm0, tok)
    dy0 = mm_nt("ab_dy", dm0, w_ab_out.reshape(d_pool + d_conv, d), out_dtype=BF16)
    red_ff1_0, tok = reduce_middle("ff1_0", red_ff1_0, dy0)
    dy0 = after(dy0, tok)
    dcv, dg_ln_g, dg_ln_b = ln_silu_bwd("ln_silu_bwd", cv, conv_ln_g, conv_ln_b, dy0, d_pool // d_conv)
    dz0, dw_conv, dg_conv_b = conv_bwd("conv_bwd", z0, dcv, w_conv, d_pool, d_conv)
    dz0, dw_pool, dg_pool_scale = pool_bwd("pool_bwd", pooled, dy0, w_pool, pool_scale, dz0)
    small_parts = [
        dw_conv.reshape(kw, N_DEV, -1).transpose(1, 0, 2).reshape(N_DEV, -1, lanes),
        dw_sconv.reshape(ks, N_DEV, -1).transpose(1, 0, 2).reshape(N_DEV, -1, lanes),
        dw_pool.reshape(ng, N_DEV, pg // N_DEV, pg).transpose(1, 0, 2, 3).reshape(N_DEV, -1, lanes),
    ]
    small = jnp.pad(jnp.concatenate(small_parts, axis=1), ((0, 0), (0, small_total - r2), (0, 0)))
    red_small, tok = reduce_begin("small", small)
    red_ab_out, tok2 = reduce_middle("ab_out", red_ab_out, dz0)
    dz0 = after(dz0, tok, tok2)
    dw = mm_tn_blocked("ab_dwin", h0, dz0, nb_ab, out_dtype=BF16)
    red_ab_in, tok = reduce_begin("ab_in", dw)
    dz0 = after(dz0, tok)
    dh0 = mm_nt_blocked("ab_dh", dz0, w_ab_in, out_dtype=BF16)
    red_small, tok = reduce_middle("small", red_small, dh0)
    dh0 = after(dh0, tok)
    grad_x, dg_mix_pre0 = bwd_pre_final("bwd_0", dx1, xs, row(mix_pre_g, 0), dh0)
    red_ab_in, tok = reduce_middle("ab_in", red_ab_in, grad_x)

    fold = lambda a: jnp.sum(a, axis=0, keepdims=True)
    rep_rows = [fold(dg_mix_pre0), fold(dg_mix_pre1), fold(dg_mix_post0), fold(dg_mix_post1),
                fold(dg_ffn_pre0), fold(dg_ffn_pre1), fold(dg_ffn_post0), fold(dg_ffn_post1)]
    tail = jnp.concatenate([dg_pool_scale, dg_conv_b, fold(dg_ln_g), fold(dg_ln_b)], axis=1).reshape(-1, d)
    rep = _pad_rows(jnp.concatenate(rep_rows + [tail], axis=0), 16)
    (rep_hop,), _ = copies_start("rep_start", [[place_shard("place_rep", rep[None], 0, F32)]], _first_hop, 4)

    def pack_rep(a_mix_pre, a_mix_post, a_ffn_pre, a_ffn_post, a_scale, a_b, a_g, a_lb):
        tail_ = jnp.concatenate([a_scale, a_b, a_g, a_lb], axis=1).reshape(-1, d)
        return _pad_rows(jnp.concatenate([a_mix_pre, a_mix_post, a_ffn_pre, a_ffn_post, tail_], axis=0), 16)

    def upd(name, w, m, v, contribs):
        shape = w.shape
        flat2 = lambda a: a.reshape(-1, shape[-1])
        outs = adamw(name, flat2(w), flat2(m), flat2(v), contribs)
        return [o.reshape(shape) for o in outs]

    g_ff2 = [reduce_end("ff2_0", red_ff2_0, tok), reduce_end("ff2_1", red_ff2_1, tok)]
    o_ff2 = upd("adam_ffn_w2", ffn_w2, m_ffn_w2, v_ffn_w2, g_ff2)
    (rep_zone,) = copies_wait("rep_wait", rep_hop, _first_hop, o_ff2[0])
    (rep_hop,), _ = copies_start("rep_forward_start", [[rep_zone]], _second_hop, 3)
    g_ff1 = [reduce_end("ff1_0", red_ff1_0, o_ff2[0]), reduce_end("ff1_1", red_ff1_1, o_ff2[0])]
    o_ff1 = upd("adam_ffn_w1", ffn_w1, m_ffn_w1, v_ffn_w1, g_ff1)
    (rep_all,) = copies_wait("rep_forward_wait", rep_hop, _second_hop, o_ff1[0])
    o_rep = adamw("adam_replicated",
                  pack_rep(mix_pre_g, mix_post_g, ffn_pre_g, ffn_post_g, pool_scale, conv_b, conv_ln_g, conv_ln_b),
                  pack_rep(m_mix_pre_g, m_mix_post_g, m_ffn_pre_g, m_ffn_post_g, m_pool_scale, m_conv_b, m_conv_ln_g, m_conv_ln_b),
                  pack_rep(v_mix_pre_g, v_mix_post_g, v_ffn_pre_g, v_ffn_post_g, v_pool_scale, v_conv_b, v_conv_ln_g, v_conv_ln_b),
                  [rep_all])
    o_sc_out = upd("adam_sc_out", sc_w_out, m_sc_w_out, v_sc_w_out, [reduce_end("sc_out", red_sc_out, o_ff1[0])])
    o_sc_in = upd("adam_sc_in", sc_w_in, m_sc_w_in, v_sc_w_in, [reduce_end("sc_in", red_sc_in, o_sc_out[0])])
    o_ab_out = upd("adam_ab_out", ab_w_out, m_ab_w_out, v_ab_w_out, [reduce_end("ab_out", red_ab_out, o_sc_in[0])])
    o_small = adamw("adam_small", pack_small(conv_w, sc_conv_w, pool_w), pack_small(m_conv_w, m_sc_conv_w, m_pool_w),
                    pack_small(v_conv_w, v_sc_conv_w, v_pool_w), [reduce_end("small", red_small, o_ab_out[0])])
    o_ab_in = upd("adam_ab_in", ab_w_in, m_ab_w_in, v_ab_w_in, [reduce_end("ab_in", red_ab_in, o_small[0])])

    def unpack_small(o):
        return o[:r0].reshape(conv_w.shape), o[r0:r1].reshape(sc_conv_w.shape), o[r1:r2].reshape(pool_w.shape)

    def unpack_rep(o):
        tail_ = o[8:8 + tail.shape[0]].reshape(1, -1)
        n1 = d_pool
        return dict(mix_pre_g=o[0:2], mix_post_g=o[2:4], ffn_pre_g=o[4:6], ffn_post_g=o[6:8],
                    pool_scale=tail_[:, :n1], conv_b=tail_[:, n1:n1 + d_conv],
                    conv_ln_g=tail_[:, n1 + d_conv:n1 + 2 * d_conv], conv_ln_b=tail_[:, n1 + 2 * d_conv:n1 + 3 * d_conv])

    results = []
    for kind in range(4):
        rep_o = unpack_rep(o_rep[kind])
        s_conv, s_sconv, s_pool = unpack_small(o_small[kind])
        results.append([
            rep_o["mix_pre_g"], rep_o["mix_post_g"], rep_o["ffn_pre_g"], rep_o["ffn_post_g"],
            o_ab_in[kind], s_pool, rep_o["pool_scale"], s_conv, rep_o["conv_b"], rep_o["conv_ln_g"], rep_o["conv_ln_b"],
            o_ab_out[kind], o_sc_in[kind], s_sconv, o_sc_out[kind], o_ff1[kind], o_ff2[kind]])

    return (loss, grad_x[None], *results[0], *results[1], *results[2], *results[3])
```

```python
import jax
import jax.numpy as jnp
from jax import lax
from jax.experimental import pallas as pl
from jax.experimental.pallas import tpu as pltpu

F32 = jnp.float32
BF16 = jnp.bfloat16
MESH = pl.DeviceIdType.MESH
ANY = pl.BlockSpec(memory_space=pl.ANY)

NORM_EPS = 1e-6
POOL_WINDOWS = (2, 4, 8, 16)
ADAM_LR = 0.001
ADAM_B1 = 0.9
ADAM_B2 = 0.999
ADAM_EPS = 1e-08
ADAM_WD = 0.01
ADAM_STEP = 10

N_DEV = 8
VMEM_LIMIT = 56 * 1024 * 1024
PAIR_ADD_BLOCK = 1 << 20
MATMUL_ROWS = 2048
ROW_TILE = 256
CHANNEL_TILE = 256
TIME_CHUNK = 64
HALO = 32

NN = (((1,), (0,)), ((), ()))
NT = (((1,), (1,)), ((), ()))
TN = (((0,), (0,)), ((), ()))


def _params(sem):
    return pltpu.CompilerParams(dimension_semantics=sem, vmem_limit_bytes=VMEM_LIMIT)


def _place():
    x, y, c = lax.axis_index("x"), lax.axis_index("y"), lax.axis_index("c")
    return x, y, c


def _slot(px, py, pc):
    return 4 * px + 2 * py + pc


HBM = pl.BlockSpec(memory_space=pltpu.HBM)
SEM = pl.BlockSpec(memory_space=pltpu.SEMAPHORE)
EFFECT = pltpu.SideEffectType.DATAFLOW_SIDE_EFFECTING
TOKEN = jax.ShapeDtypeStruct((8, 128), F32)


def _in_hbm(a):
    return pltpu.with_memory_space_constraint(a, pltpu.HBM)


CHIPS = [(0, 0), (0, 1), (1, 0), (1, 1)]
N_CHIP = len(CHIPS)


def _chip(px, py):
    return 2 * px + py


def _first_hop(bufs, sends, recvs, waiting):
    (land,) = bufs
    x, y, c = _place()
    me = _slot(x, y, c)
    peers = [(x, y, 1 - c), (1 - x, y, c), (x, 1 - y, c), (1 - x, 1 - y, c)]
    return [pltpu.make_async_remote_copy(
        src_ref=land.at[me], dst_ref=land.at[_slot(*p) if waiting else me],
        send_sem=sends.at[k], recv_sem=recvs.at[k], device_id=p, device_id_type=MESH) for k, p in enumerate(peers)]


def _second_hop(bufs, sends, recvs, waiting):
    (land,) = bufs
    x, y, c = _place()
    return [pltpu.make_async_remote_copy(
        src_ref=land.at[_slot(px, py, c)], dst_ref=land.at[_slot(px, py, 1 - c if waiting else c)],
        send_sem=sends.at[k], recv_sem=recvs.at[k], device_id=(x, y, 1 - c), device_id_type=MESH)
        for k, (px, py) in enumerate([(1 - x, y), (x, 1 - y), (1 - x, 1 - y)])]


def _ring_hop1(bufs, sends, recvs, waiting):
    (land,) = bufs
    x, y, c = _place()
    me = _slot(x, y, c)
    peers = [(1 - x, y, c), (x, 1 - y, c), (x, y, 1 - c)]
    return [pltpu.make_async_remote_copy(
        src_ref=land.at[me], dst_ref=land.at[_slot(*p) if waiting else me],
        send_sem=sends.at[k], recv_sem=recvs.at[k], device_id=p, device_id_type=MESH) for k, p in enumerate(peers)]


def _ring_hop2(bufs, sends, recvs, waiting):
    (land,) = bufs
    x, y, c = _place()
    half = land.shape[1] // 2
    first, second = pl.ds(0, half), pl.ds(half, half)
    nx, ny, diag = _slot(1 - x, y, c), _slot(x, 1 - y, c), _slot(1 - x, 1 - y, c)
    plan = [
        (land.at[ny, first], land.at[diag, first], (1 - x, y, c)),
        (land.at[nx, second], land.at[diag, second], (x, 1 - y, c)),
        (land.at[nx], land.at[_slot(1 - x, y, 1 - c)], (x, y, 1 - c)),
        (land.at[ny], land.at[_slot(x, 1 - y, 1 - c)], (x, y, 1 - c))]
    return [pltpu.make_async_remote_copy(
        src_ref=src, dst_ref=mine if waiting else src, send_sem=sends.at[k], recv_sem=recvs.at[k],
        device_id=to, device_id_type=MESH) for k, (src, mine, to) in enumerate(plan)]


def _ring_hop3(bufs, sends, recvs, waiting):
    (land,) = bufs
    x, y, c = _place()
    return [pltpu.make_async_remote_copy(
        src_ref=land.at[_slot(1 - x, 1 - y, c)], dst_ref=land.at[_slot(1 - x, 1 - y, 1 - c if waiting else c)],
        send_sem=sends.at[0], recv_sem=recvs.at[0], device_id=(x, y, 1 - c), device_id_type=MESH)]


def _pair_hop(bufs, sends, recvs, waiting):
    g, land = bufs
    x, y, c = _place()
    return [pltpu.make_async_remote_copy(
        src_ref=g.at[_slot(qx, qy, 1 - c)], dst_ref=land.at[q],
        send_sem=sends.at[q], recv_sem=recvs.at[q], device_id=(x, y, 1 - c), device_id_type=MESH)
        for q, (qx, qy) in enumerate(CHIPS)]


def _chip_hop(bufs, sends, recvs, waiting):
    p, land = bufs
    x, y, c = _place()
    return [pltpu.make_async_remote_copy(
        src_ref=p.at[_chip(px, py)], dst_ref=land.at[_chip(px, py) if waiting else _chip(x, y)],
        send_sem=sends.at[k], recv_sem=recvs.at[k], device_id=(px, py, c), device_id_type=MESH)
        for k, (px, py) in enumerate([(1 - x, y), (x, 1 - y), (1 - x, 1 - y)])]


def copies_start(name, groups, hop, n_copies, deps=()):
    flat = [b for grp in groups for b in grp]
    nb, ng = len(flat), len(groups)
    deps = list(deps)
    hops = list(hop) if isinstance(hop, (list, tuple)) else [hop] * ng
    counts = list(n_copies) if isinstance(n_copies, (list, tuple)) else [n_copies] * ng

    def body(*refs):
        ins, token = refs[:nb], refs[-1]
        sems = refs[nb + len(deps):nb + len(deps) + 2 * ng]
        i = 0
        for gi, grp in enumerate(groups):
            for cp in hops[gi](ins[i:i + len(grp)], sems[2 * gi], sems[2 * gi + 1], False):
                cp.start()
            i += len(grp)
        token[...] = jnp.zeros_like(token)

    outs = pl.pallas_call(
        body, name=name,
        out_shape=([pltpu.SemaphoreType.DMA((n,)) for n in counts for _ in range(2)]
                   + [pltpu.HBM(b.shape, b.dtype) for b in flat] + [TOKEN]),
        in_specs=[HBM] * nb + [ANY] * len(deps),
        out_specs=[SEM] * (2 * ng) + [HBM] * nb + [pl.BlockSpec(memory_space=pltpu.VMEM)],
        input_output_aliases={i: 2 * ng + i for i in range(nb)},
        compiler_params=pltpu.CompilerParams(has_side_effects=EFFECT),
    )(*[_in_hbm(b) for b in flat], *deps)
    started, i = [], 0
    for gi, grp in enumerate(groups):
        started.append((outs[2 * gi], outs[2 * gi + 1], list(outs[2 * ng + i:2 * ng + i + len(grp)])))
        i += len(grp)
    return started, outs[-1]


def copies_wait(name, started, hop, after):
    sends, recvs, bufs = started
    nb = len(bufs)

    def body(*refs):
        for cp in hop(refs[:nb], refs[nb], refs[nb + 1], True):
            cp.wait_send()
            cp.wait_recv()

    outs = pl.pallas_call(
        body, name=name,
        out_shape=[pltpu.HBM(b.shape, b.dtype) for b in bufs],
        in_specs=[HBM] * nb + [SEM, SEM, ANY], out_specs=[HBM] * nb,
        input_output_aliases={i: i for i in range(nb)},
        compiler_params=pltpu.CompilerParams(has_side_effects=EFFECT),
    )(*bufs, sends, recvs, after)
    return list(outs)


def place_shard(name, w, layer, dtype, deps=()):
    _, r, c = w.shape
    tr = _tile(r, 1024)
    x, y, core = _place()
    me = _slot(x, y, core).astype(jnp.int32).reshape(1)

    def body(me_ref, w_ref, *rest):
        rest[-1][...] = w_ref[...].astype(dtype)

    return pl.pallas_call(
        body, name=name,
        grid_spec=pltpu.PrefetchScalarGridSpec(
            num_scalar_prefetch=1, grid=(r // tr,),
            in_specs=[pl.BlockSpec((None, tr, c), lambda i, me_ref: (layer, i, 0))] + [ANY] * len(deps),
            out_specs=pl.BlockSpec((None, tr, c), lambda i, me_ref: (me_ref[0], i, 0))),
        out_shape=jax.ShapeDtypeStruct((N_DEV, r, c), dtype),
        compiler_params=_params(("parallel",)),
    )(me, w, *deps)


def tie(name, x, *deps):
    def body(*refs):
        del refs

    return pl.pallas_call(
        body, name=name, out_shape=jax.ShapeDtypeStruct(x.shape, x.dtype),
        in_specs=[ANY] * (1 + len(deps)), out_specs=ANY, input_output_aliases={0: 0},
    )(x, *deps)


def pair_add(name, g, from_sibling):
    _, r, c_dim = g.shape
    tr = r
    while tr * c_dim > PAIR_ADD_BLOCK and tr % 16 == 0:
        tr //= 2
    x, y, core = _place()
    where = jnp.stack([core, _chip(x, y)]).astype(jnp.int32)

    def body(where_ref, g_ref, s_ref, o_ref, zone_ref):
        total = (g_ref[...].astype(F32) + s_ref[...].astype(F32)).astype(o_ref.dtype)
        o_ref[...] = total

        @pl.when(pl.program_id(1) == where_ref[1])
        def _():
            zone_ref[...] = total

    blk = pl.BlockSpec((None, tr, c_dim), lambda i, q, where_ref: (q, i, 0))
    return pl.pallas_call(
        body, name=name,
        grid_spec=pltpu.PrefetchScalarGridSpec(
            num_scalar_prefetch=1, grid=(r // tr, N_CHIP),
            in_specs=[pl.BlockSpec((None, None, tr, c_dim), lambda i, q, where_ref: (q, where_ref[0], i, 0)), blk],
            out_specs=[blk, pl.BlockSpec((None, tr, c_dim), lambda i, q, where_ref: (where_ref[1], i, 0))]),
        out_shape=[jax.ShapeDtypeStruct((N_CHIP, r, c_dim), g.dtype)] * 2,
        compiler_params=_params(("parallel", "arbitrary")),
    )(where, g.reshape(N_CHIP, 2, r, c_dim), from_sibling)


def _matmul(name, lhs, rhs, *, out_shape, out_dtype, grid, lhs_spec, rhs_spec, out_spec, dims, acc_shape,
            lhs_fn=None, extra=(), extra_specs=(), epilogue=None, parts=1):
    nk = grid[2]
    n_extra = len(extra)

    def body(*refs):
        lhs_ref, rhs_ref = refs[0], refs[1]
        extra_refs = refs[2:2 + n_extra]
        out_ref = refs[2 + n_extra]

        def product():
            if parts == 1:
                a = lhs_ref[...]
                if lhs_fn is not None:
                    a = lhs_fn(a)
                return lax.dot_general(a, rhs_ref[...], dims, preferred_element_type=F32)
            width = lhs_ref.shape[1] // parts
            total = None
            for b in range(parts):
                term = lax.dot_general(lhs_ref[:, b * width:(b + 1) * width], rhs_ref[b], dims,
                                       preferred_element_type=F32)
                total = term if total is None else total + term
            return total

        def finish(r):
            if epilogue is not None:
                r = epilogue(r, *[e[...] for e in extra_refs])
            out_ref[...] = r.astype(out_dtype)

        if nk == 1:
            finish(product())
        else:
            acc_ref = refs[3 + n_extra]
            k = pl.program_id(2)

            @pl.when(k == 0)
            def _():
                acc_ref[...] = product()

            @pl.when(jnp.logical_and(k > 0, k < nk - 1))
            def _():
                acc_ref[...] += product()

            @pl.when(k == nk - 1)
            def _():
                finish(acc_ref[...] + product())

    return pl.pallas_call(
        body, name=name, grid=grid,
        out_shape=jax.ShapeDtypeStruct(out_shape, out_dtype),
        in_specs=[lhs_spec, rhs_spec, *extra_specs], out_specs=out_spec,
        scratch_shapes=[pltpu.VMEM(acc_shape, F32)] if nk > 1 else [],
        compiler_params=_params(("parallel", "parallel", "arbitrary")),
    )(lhs, rhs, *extra)


def _tile(n, want):
    return want if n % want == 0 else n


def mm_nn(name, x, w, *, out_dtype, tn=512, tk=None, lhs_fn=None, epilogue=None):
    t, kdim = x.shape
    n = w.shape[1]
    tm, tn = _tile(t, MATMUL_ROWS), _tile(n, tn)
    tk = kdim if tk is None else _tile(kdim, tk)
    return _matmul(
        name, x, w, out_shape=(t, n), out_dtype=out_dtype, grid=(t // tm, n // tn, kdim // tk),
        lhs_spec=pl.BlockSpec((tm, tk), lambda i, j, k: (i, k)),
        rhs_spec=pl.BlockSpec((tk, tn), lambda i, j, k: (k, j)),
        out_spec=pl.BlockSpec((tm, tn), lambda i, j, k: (i, j)),
        dims=NN, acc_shape=(tm, tn), lhs_fn=lhs_fn, epilogue=epilogue)


def mm_nn_blocked(name, x, w, *, out_dtype, epilogue=None):
    t, kdim = x.shape
    nb = w.shape[2]
    tm = _tile(t, MATMUL_ROWS)
    tn = nb // 2 if nb >= 1024 else nb
    sub = nb // tn
    return _matmul(
        name, x, w, out_shape=(t, N_DEV * nb), out_dtype=out_dtype, grid=(t // tm, N_DEV * sub, 1),
        lhs_spec=pl.BlockSpec((tm, kdim), lambda i, j, k: (i, k)),
        rhs_spec=pl.BlockSpec((None, kdim, tn), lambda i, j, k: (j // sub, k, j % sub)),
        out_spec=pl.BlockSpec((tm, tn), lambda i, j, k: (i, j)),
        dims=NN, acc_shape=(tm, tn), epilogue=epilogue)


def mm_nt(name, dy, w, *, out_dtype, tn=512, extra=None, epilogue=None):
    t, n = dy.shape
    kdim = w.shape[0]
    tm, tn = _tile(t, MATMUL_ROWS), _tile(kdim, tn)
    extra_arrs = () if extra is None else (extra,)
    extra_specs = () if extra is None else (pl.BlockSpec((tm, tn), lambda i, j, k: (i, j)),)
    return _matmul(
        name, dy, w, out_shape=(t, kdim), out_dtype=out_dtype, grid=(t // tm, kdim // tn, 1),
        lhs_spec=pl.BlockSpec((tm, n), lambda i, j, k: (i, k)),
        rhs_spec=pl.BlockSpec((tn, n), lambda i, j, k: (j, k)),
        out_spec=pl.BlockSpec((tm, tn), lambda i, j, k: (i, j)),
        dims=NT, acc_shape=(tm, tn), extra=extra_arrs, extra_specs=extra_specs, epilogue=epilogue)


def mm_nt_blocked(name, dz, w, *, out_dtype, tn=512):
    t = dz.shape[0]
    kdim, nb = w.shape[1], w.shape[2]
    tm, tn = _tile(t, MATMUL_ROWS), _tile(kdim, tn)
    parts = 2
    return _matmul(
        name, dz, w, out_shape=(t, kdim), out_dtype=out_dtype, grid=(t // tm, kdim // tn, N_DEV // parts),
        lhs_spec=pl.BlockSpec((tm, parts * nb), lambda i, j, k: (i, k)),
        rhs_spec=pl.BlockSpec((parts, tn, nb), lambda i, j, k: (k, j, 0)),
        out_spec=pl.BlockSpec((tm, tn), lambda i, j, k: (i, j)),
        dims=NT, acc_shape=(tm, tn), parts=parts)


def mm_tn(name, x, dy, *, out_dtype, tk=1024, tn=1024, lhs_fn=None):
    t, kdim = x.shape
    n = dy.shape[1]
    tk, tn = _tile(kdim, tk), _tile(n, tn)
    return _matmul(
        name, x, dy, out_shape=(kdim, n), out_dtype=out_dtype, grid=(kdim // tk, n // tn, 1),
        lhs_spec=pl.BlockSpec((t, tk), lambda i, j, k: (k, i)),
        rhs_spec=pl.BlockSpec((t, tn), lambda i, j, k: (k, j)),
        out_spec=pl.BlockSpec((tk, tn), lambda i, j, k: (i, j)),
        dims=TN, acc_shape=(tk, tn), lhs_fn=lhs_fn)


def mm_tn_blocked(name, x, dz, nb, *, out_dtype, tk=1024):
    t, kdim = x.shape
    tk = _tile(kdim, tk)
    return _matmul(
        name, x, dz, out_shape=(N_DEV, kdim, nb), out_dtype=out_dtype, grid=(kdim // tk, N_DEV, 1),
        lhs_spec=pl.BlockSpec((t, tk), lambda i, j, k: (k, i)),
        rhs_spec=pl.BlockSpec((t, nb), lambda i, j, k: (k, j)),
        out_spec=pl.BlockSpec((None, tk, nb), lambda i, j, k: (j, i, 0)),
        dims=TN, acc_shape=(tk, nb))


def _rstd(v):
    return lax.rsqrt(jnp.mean(v * v, axis=-1, keepdims=True) + NORM_EPS)


def _rms_bwd(v, g, dy):
    r = _rstd(v)
    vhat = v * r
    dvh = dy * g
    dv = r * (dvh - vhat * jnp.mean(dvh * vhat, axis=-1, keepdims=True))
    return dv, dy * vhat


def _fold8(v):
    rows, n = v.shape
    return jnp.sum(v.reshape(rows // 8, 8, n), axis=0)


def _fold_lanes(v):
    out = v[:, 0:128]
    for i in range(1, v.shape[1] // 128):
        out = out + v[:, 128 * i:128 * (i + 1)]
    return out


def _accumulate(ref, v):
    i = pl.program_id(0)

    @pl.when(i == 0)
    def _():
        ref[...] = v

    @pl.when(i > 0)
    def _():
        ref[...] += v


def _row_call(body, name, t, ins, row_in, outs, acc_outs=(), tr=ROW_TILE):
    tr = _tile(t, tr)

    def in_spec(a, tiled):
        if isinstance(tiled, tuple):
            width, j = tiled
            return pl.BlockSpec((tr, width), lambda i: (i, j))
        return pl.BlockSpec((tr, a.shape[1]), lambda i: (i, 0)) if tiled else pl.BlockSpec(a.shape, lambda i: (0, 0))

    in_specs = [in_spec(a, tiled) for a, tiled in zip(ins, row_in)]
    out_specs = [pl.BlockSpec((tr, n), lambda i: (i, 0)) for n, _ in outs]
    out_specs += [pl.BlockSpec((8, n), lambda i: (0, 0)) for n in acc_outs]
    out_shape = [jax.ShapeDtypeStruct((t, n), dt) for n, dt in outs]
    out_shape += [jax.ShapeDtypeStruct((8, n), F32) for n in acc_outs]
    return pl.pallas_call(
        body, name=name, grid=(t // tr,), in_specs=in_specs, out_specs=out_specs, out_shape=out_shape,
        compiler_params=_params(("arbitrary",) if acc_outs else ("parallel",)),
    )(*ins)


def norm_pre(name, x, g):
    t, d = x.shape

    def body(x_ref, g_ref, h_ref):
        v = x_ref[...]
        h_ref[...] = (v * _rstd(v) * g_ref[...]).astype(BF16)

    return _row_call(body, name, t, [x, g], [True, False], [(d, BF16)])[0]


def post_pre(name, x, m, g_post, g_pre):
    t, d = x.shape

    def body(x_ref, m_ref, gp_ref, gn_ref, xo_ref, h_ref):
        mv = m_ref[...]
        xn = x_ref[...] + mv * _rstd(mv) * gp_ref[...]
        xo_ref[...] = xn
        h_ref[...] = (xn * _rstd(xn) * gn_ref[...]).astype(BF16)

    return _row_call(body, name, t, [x, m, g_post, g_pre], [True, True, False, False], [(d, F32), (d, BF16)])


def post_loss(name, x, f, g_post, target):
    t, d = x.shape

    def body(x_ref, f_ref, g_ref, t_ref, dx_ref, df_ref, loss_ref, dg_ref):
        fv = f_ref[...]
        g = g_ref[...]
        out = x_ref[...] + fv * _rstd(fv) * g
        err = out - t_ref[...]
        dx = err * (1.0 / d)
        dx_ref[...] = dx
        dfv, dg_rows = _rms_bwd(fv, g, dx)
        df_ref[...] = dfv.astype(BF16)
        _accumulate(loss_ref, _fold8(_fold_lanes(err * err)))
        _accumulate(dg_ref, _fold8(dg_rows))

    return _row_call(body, name, t, [x, f, g_post, target], [True, True, False, True],
                     [(d, F32), (d, BF16)], acc_outs=(128, d))


def bwd_pre_post(name, dx_out, x_in, g_pre, dh, f_prev, g_post_prev):
    t, d = x_in.shape

    def body(dxo_ref, x_ref, gpre_ref, dh_ref, f_ref, gpost_ref, dxi_ref, df_ref, dgpre_ref, dgpost_ref):
        dxv, dgpre_rows = _rms_bwd(x_ref[...], gpre_ref[...], dh_ref[...].astype(F32))
        dxi = dxo_ref[...] + dxv
        dxi_ref[...] = dxi
        dfv, dgpost_rows = _rms_bwd(f_ref[...], gpost_ref[...], dxi)
        df_ref[...] = dfv.astype(BF16)
        _accumulate(dgpre_ref, _fold8(dgpre_rows))
        _accumulate(dgpost_ref, _fold8(dgpost_rows))

    return _row_call(body, name, t, [dx_out, x_in, g_pre, dh, f_prev, g_post_prev],
                     [True, True, False, True, True, False], [(d, F32), (d, BF16)], acc_outs=(d, d))


def bwd_pre_final(name, dx_out, x_in, g_pre, dh):
    t, d = x_in.shape

    def body(dxo_ref, x_ref, gpre_ref, dh_ref, dxi_ref, dgpre_ref):
        dxv, dgpre_rows = _rms_bwd(x_ref[...], gpre_ref[...], dh_ref[...].astype(F32))
        dxi_ref[...] = dxo_ref[...] + dxv
        _accumulate(dgpre_ref, _fold8(dgpre_rows))

    return _row_call(body, name, t, [dx_out, x_in, g_pre, dh], [True, True, False, True], [(d, F32)], acc_outs=(d,))


def _layer_norm_parts(cv):
    mu = jnp.mean(cv, axis=-1, keepdims=True)
    xc = cv - mu
    rstd = lax.rsqrt(jnp.mean(xc * xc, axis=-1, keepdims=True) + NORM_EPS)
    return xc * rstd, rstd


def ln_silu(name, cv, g, b, y, y_block):
    t, n = cv.shape
    tr = _tile(t, ROW_TILE)

    def body(c_ref, g_ref, b_ref, y_in_ref, y_ref):
        chat, _ = _layer_norm_parts(c_ref[...])
        ln = chat * g_ref[...] + b_ref[...]
        y_ref[...] = (ln * jax.nn.sigmoid(ln)).astype(BF16)

    vec = pl.BlockSpec((1, n), lambda i: (0, 0))
    return pl.pallas_call(
        body, name=name, grid=(t // tr,),
        in_specs=[pl.BlockSpec((tr, n), lambda i: (i, 0)), vec, vec, ANY],
        out_specs=pl.BlockSpec((tr, n), lambda i: (i, y_block)),
        out_shape=jax.ShapeDtypeStruct(y.shape, y.dtype), input_output_aliases={3: 0},
        compiler_params=_params(("parallel",)),
    )(cv, g, b, y)


def ln_silu_bwd(name, cv, g, b, dy, dy_block):
    t, n = cv.shape

    def body(c_ref, g_ref, b_ref, dy_ref, dc_ref, dg_ref, db_ref):
        chat, rstd = _layer_norm_parts(c_ref[...])
        g = g_ref[...]
        ln = chat * g + b_ref[...]
        s = jax.nn.sigmoid(ln)
        dln = dy_ref[...].astype(F32) * (s * (1.0 + ln * (1.0 - s)))
        dchat = dln * g
        dc_ref[...] = rstd * (dchat - jnp.mean(dchat, axis=-1, keepdims=True)
                              - chat * jnp.mean(dchat * chat, axis=-1, keepdims=True))
        _accumulate(dg_ref, _fold8(dln * chat))
        _accumulate(db_ref, _fold8(dln))

    return _row_call(body, name, t, [cv, g, b, dy], [True, False, False, (n, dy_block)], [(n, F32)], acc_outs=(n, n))


def _chunks(t, fn, tc=TIME_CHUNK):
    tc = _tile(t, tc)

    def step(i, carry):
        fn(pl.multiple_of(i * tc, tc), tc)
        return carry

    lax.fori_loop(0, t // tc, step, 0)


def _shifted(window, offsets, tc):
    by_residue = {}
    for k, off in enumerate(offsets):
        by_residue.setdefault(off % 8, []).append((k, off))
    for res, taps in by_residue.items():
        base = window[res:res + tc + max(off for _, off in taps) - res, :]
        for k, off in taps:
            yield k, base[off - res:off - res + tc, :]


def _taps(window, w_ref, offsets, tc, flip=False):
    acc = None
    for k, rows in _shifted(window, offsets, tc):
        kk = len(offsets) - 1 - k if flip else k
        term = w_ref[kk:kk + 1, :] * rows
        acc = term if acc is None else acc + term
    return acc


def _window_sums(win, tc, causal):
    sums = []
    cur, rows, step = win, tc + HALO, 1
    for _ in POOL_WINDOWS:
        rows -= 8
        if causal:
            cur = cur[8:8 + rows, :] + cur[8 - step:8 - step + rows, :]
            sums.append(cur[rows - tc:rows, :])
        else:
            cur = cur[0:rows, :] + cur[step:step + rows, :]
            sums.append(cur[0:tc, :])
        step *= 2
    return sums


def _pick(vals, g):
    out = vals[-1]
    for i in range(len(vals) - 2, -1, -1):
        out = jnp.where(g == i, vals[i], out)
    return out


def _pool_count(s, tc, g):
    t1 = (lax.broadcasted_iota(jnp.int32, (tc, 1), 0) + (s + 1)).astype(F32)
    width = _pick([float(w) for w in POOL_WINDOWS], g)
    return jnp.minimum(t1, width)


def pool_fwd(name, z, pool_w, pool_scale, d_pool, y_width):
    t = z.shape[0]
    ng, pg = pool_w.shape[0], pool_w.shape[1]

    def body(u_ref, w_ref, s_ref, pooled_ref, y_ref, pad):
        g = pl.program_id(0)
        pad[pl.ds(0, HALO), :] = jnp.zeros((HALO, pg), F32)

        def fill(s, tc):
            pad[pl.ds(HALO + s, tc), :] = u_ref[pl.ds(s, tc), :].astype(F32)

        def chunk(s, tc):
            win = pad[pl.ds(s, tc + HALO), :]
            total = _pick(_window_sums(win, tc, causal=True), g)
            pooled = total / _pool_count(s, tc, g) - win[HALO:HALO + tc, :]
            pooled_ref[pl.ds(s, tc), :] = pooled.astype(BF16)

        _chunks(t, fill)
        _chunks(t, chunk)
        mixed = jnp.dot(pooled_ref[...], w_ref[...], preferred_element_type=F32)
        y_ref[...] = (mixed * s_ref[...]).astype(BF16)

    col = pl.BlockSpec((t, pg), lambda g: (0, g))
    return pl.pallas_call(
        body, name=name, grid=(ng,),
        in_specs=[col, pl.BlockSpec((None, pg, pg), lambda g: (g, 0, 0)), pl.BlockSpec((1, pg), lambda g: (0, g))],
        out_specs=[col, col],
        out_shape=[jax.ShapeDtypeStruct((t, d_pool), BF16), jax.ShapeDtypeStruct((t, y_width), BF16)],
        scratch_shapes=[pltpu.VMEM((t + HALO, pg), F32)],
        compiler_params=_params(("parallel",)),
    )(z, pool_w, pool_scale)


def pool_bwd(name, pooled, dy, pool_w, pool_scale, dz):
    t, d_pool = pooled.shape
    ng, pg = pool_w.shape[0], pool_w.shape[1]

    def body(p_ref, dy_ref, w_ref, s_ref, dz_ref, du_ref, dw_ref, ds_ref, pad):
        g = pl.program_id(0)
        w = w_ref[...]
        dyv = dy_ref[...].astype(F32)
        mixed = jnp.dot(p_ref[...], w, preferred_element_type=F32)
        ds_ref[...] = jnp.sum(dyv * mixed, axis=0, keepdims=True)
        dmixed = (dyv * s_ref[...]).astype(BF16)
        dw_ref[...] = lax.dot_general(p_ref[...], dmixed, TN, preferred_element_type=F32)
        pad[...] = jnp.zeros((t + HALO, pg), F32)
        pad[pl.ds(0, t), :] = lax.dot_general(dmixed, w, NT, preferred_element_type=F32)

        def scale(s, tc):
            pad[pl.ds(s, tc), :] = pad[pl.ds(s, tc), :] / _pool_count(s, tc, g)

        def chunk(s, tc):
            win = pad[pl.ds(s, tc + HALO), :]
            total = _pick(_window_sums(win, tc, causal=False), g)
            du_ref[pl.ds(s, tc), :] = (total - win[0:tc, :] * _pool_count(s, tc, g)).astype(BF16)

        _chunks(t, scale)
        _chunks(t, chunk)

    col = pl.BlockSpec((t, pg), lambda g: (0, g))
    vec = pl.BlockSpec((1, pg), lambda g: (0, g))
    mat = pl.BlockSpec((None, pg, pg), lambda g: (g, 0, 0))
    return pl.pallas_call(
        body, name=name, grid=(ng,),
        in_specs=[col, col, mat, vec, ANY], out_specs=[col, mat, vec],
        out_shape=[jax.ShapeDtypeStruct(dz.shape, dz.dtype), jax.ShapeDtypeStruct((ng, pg, pg), F32),
                   jax.ShapeDtypeStruct((1, d_pool), F32)],
        input_output_aliases={4: 0},
        scratch_shapes=[pltpu.VMEM((t + HALO, pg), F32)],
        compiler_params=_params(("parallel",)),
    )(pooled, dy, pool_w, pool_scale, dz)


def conv_fwd(name, z, conv_w, conv_b, d_pool, d_conv):
    t = z.shape[0]
    kw = conv_w.shape[0]
    tc_ch = _tile(d_conv, CHANNEL_TILE)
    v0, g0 = d_pool // tc_ch, (d_pool + d_conv) // tc_ch

    def body(v_ref, g_ref, w_ref, b_ref, c_ref, pad):
        pad[pl.ds(0, HALO), :] = jnp.zeros((HALO, tc_ch), F32)

        def fill(s, tc):
            pad[pl.ds(HALO + s, tc), :] = v_ref[pl.ds(s, tc), :].astype(F32) * jax.nn.sigmoid(g_ref[pl.ds(s, tc), :].astype(F32))

        def chunk(s, tc):
            win = pad[pl.ds(s, tc + HALO), :]
            c_ref[pl.ds(s, tc), :] = _taps(win, w_ref, [HALO - (kw - 1) + k for k in range(kw)], tc) + b_ref[...]

        _chunks(t, fill)
        _chunks(t, chunk)

    return pl.pallas_call(
        body, name=name, grid=(d_conv // tc_ch,),
        in_specs=[pl.BlockSpec((t, tc_ch), lambda j: (0, v0 + j)), pl.BlockSpec((t, tc_ch), lambda j: (0, g0 + j)),
                  pl.BlockSpec((kw, tc_ch), lambda j: (0, j)), pl.BlockSpec((1, tc_ch), lambda j: (0, j))],
        out_specs=pl.BlockSpec((t, tc_ch), lambda j: (0, j)),
        out_shape=jax.ShapeDtypeStruct((t, d_conv), F32),
        scratch_shapes=[pltpu.VMEM((t + HALO, tc_ch), F32)],
        compiler_params=_params(("parallel",)),
    )(z, z, conv_w, conv_b)


def conv_bwd(name, z, dc, conv_w, d_pool, d_conv):
    t = z.shape[0]
    kw = conv_w.shape[0]
    tc_ch = _tile(d_conv, CHANNEL_TILE)
    v0, g0 = d_pool // tc_ch, (d_pool + d_conv) // tc_ch

    def body(v_ref, g_ref, dc_ref, w_ref, dz_ref, dw_ref, db_ref, pad_a, pad_dc, acc_w, acc_b, tiles, sems):
        j = pl.program_id(0)
        dv_ref, dg_ref = tiles.at[0], tiles.at[1]
        writes = [pltpu.make_async_copy(tiles.at[p], dz_ref.at[:, pl.ds((first + j) * tc_ch, tc_ch)], sems.at[p])
                  for p, first in enumerate([v0, g0])]

        def wait_writes():
            for cp in writes:
                cp.wait()

        pad_a[pl.ds(0, HALO), :] = jnp.zeros((HALO, tc_ch), F32)
        pad_dc[pl.ds(t, HALO), :] = jnp.zeros((HALO, tc_ch), F32)
        acc_w[...] = jnp.zeros_like(acc_w)
        acc_b[...] = jnp.zeros_like(acc_b)

        def fill(s, tc):
            pad_a[pl.ds(HALO + s, tc), :] = v_ref[pl.ds(s, tc), :].astype(F32) * jax.nn.sigmoid(g_ref[pl.ds(s, tc), :].astype(F32))
            pad_dc[pl.ds(s, tc), :] = dc_ref[pl.ds(s, tc), :]

        def chunk(s, tc):
            dcv = pad_dc[pl.ds(s, tc), :]
            win_a = pad_a[pl.ds(s, tc + HALO), :]
            for k, rows in _shifted(win_a, [HALO - (kw - 1) + k for k in range(kw)], tc):
                acc_w[pl.ds(8 * k, 8), :] += _fold8(dcv * rows)
            acc_b[...] += _fold8(dcv)
            da = _taps(pad_dc[pl.ds(s, tc + HALO), :], w_ref, list(range(kw)), tc, flip=True)
            vv = v_ref[pl.ds(s, tc), :].astype(F32)
            sg = jax.nn.sigmoid(g_ref[pl.ds(s, tc), :].astype(F32))
            dv_ref[pl.ds(s, tc), :] = (da * sg).astype(BF16)
            dg_ref[pl.ds(s, tc), :] = (da * vv * sg * (1.0 - sg)).astype(BF16)

        _chunks(t, fill)
        pl.when(j > 0)(wait_writes)
        _chunks(t, chunk)
        for cp in writes:
            cp.start()
        pl.when(j == n_tiles - 1)(wait_writes)
        for k in range(kw):
            dw_ref[k:k + 1, :] = jnp.sum(acc_w[pl.ds(8 * k, 8), :], axis=0, keepdims=True)
        db_ref[...] = jnp.sum(acc_b[...], axis=0, keepdims=True)

    n_tiles = d_conv // tc_ch
    return pl.pallas_call(
        body, name=name, grid=(n_tiles,),
        in_specs=[pl.BlockSpec((t, tc_ch), lambda j: (0, v0 + j)), pl.BlockSpec((t, tc_ch), lambda j: (0, g0 + j)),
                  pl.BlockSpec((t, tc_ch), lambda j: (0, j)), pl.BlockSpec((kw, tc_ch), lambda j: (0, j))],
        out_specs=[ANY, pl.BlockSpec((kw, tc_ch), lambda j: (0, j)), pl.BlockSpec((1, tc_ch), lambda j: (0, j))],
        out_shape=[jax.ShapeDtypeStruct((t, d_pool + 2 * d_conv), BF16),
                   jax.ShapeDtypeStruct((kw, d_conv), F32), jax.ShapeDtypeStruct((1, d_conv), F32)],
        scratch_shapes=[pltpu.VMEM((t + HALO, tc_ch), F32), pltpu.VMEM((t + HALO, tc_ch), F32),
                        pltpu.VMEM((8 * kw, tc_ch), F32), pltpu.VMEM((8, tc_ch), F32),
                        pltpu.VMEM((2, t, tc_ch), BF16), pltpu.SemaphoreType.DMA((2,))],
        compiler_params=_params(("arbitrary",)),
    )(z, z, dc, conv_w)


def short_fwd(name, z, conv_w, d_short):
    t = z.shape[0]
    kw = conv_w.shape[0]
    tc_ch = _tile(d_short, CHANNEL_TILE)
    nt = d_short // tc_ch

    def body(b_ref, c_ref, u_ref, w_ref, y_ref, pad):
        pad[pl.ds(0, HALO), :] = jnp.zeros((HALO, tc_ch), F32)

        def fill(s, tc):
            pad[pl.ds(HALO + s, tc), :] = c_ref[pl.ds(s, tc), :].astype(F32) * u_ref[pl.ds(s, tc), :].astype(F32)

        def chunk(s, tc):
            win = pad[pl.ds(s, tc + HALO), :]
            cq = _taps(win, w_ref, [HALO - (kw - 1) + k for k in range(kw)], tc)
            y_ref[pl.ds(s, tc), :] = (b_ref[pl.ds(s, tc), :].astype(F32) * cq).astype(BF16)

        _chunks(t, fill)
        _chunks(t, chunk)

    return pl.pallas_call(
        body, name=name, grid=(nt,),
        in_specs=[pl.BlockSpec((t, tc_ch), lambda j: (0, j)), pl.BlockSpec((t, tc_ch), lambda j: (0, nt + j)),
                  pl.BlockSpec((t, tc_ch), lambda j: (0, 2 * nt + j)), pl.BlockSpec((kw, tc_ch), lambda j: (0, j))],
        out_specs=pl.BlockSpec((t, tc_ch), lambda j: (0, j)),
        out_shape=jax.ShapeDtypeStruct((t, d_short), BF16),
        scratch_shapes=[pltpu.VMEM((t + HALO, tc_ch), F32)],
        compiler_params=_params(("parallel",)),
    )(z, z, z, conv_w)


def short_bwd(name, z, dy, conv_w, d_short):
    t = z.shape[0]
    kw = conv_w.shape[0]
    tc_ch = _tile(d_short, CHANNEL_TILE)
    nt = d_short // tc_ch

    def body(b_ref, c_ref, u_ref, dy_ref, w_ref, dz_ref, dw_ref, pad_q, pad_dcq, acc_w, tiles, sems):
        j = pl.program_id(0)
        db_ref, dcg_ref, du_ref = tiles.at[0], tiles.at[1], tiles.at[2]
        writes = [pltpu.make_async_copy(tiles.at[p], dz_ref.at[:, pl.ds((p * nt + j) * tc_ch, tc_ch)], sems.at[p])
                  for p in range(3)]

        def wait_writes():
            for cp in writes:
                cp.wait()

        pad_q[pl.ds(0, HALO), :] = jnp.zeros((HALO, tc_ch), F32)
        pad_dcq[pl.ds(t, HALO), :] = jnp.zeros((HALO, tc_ch), F32)
        acc_w[...] = jnp.zeros_like(acc_w)

        def fill(s, tc):
            rows = pl.ds(s, tc)
            pad_q[pl.ds(HALO + s, tc), :] = c_ref[rows, :].astype(F32) * u_ref[rows, :].astype(F32)
            pad_dcq[rows, :] = dy_ref[rows, :].astype(F32) * b_ref[rows, :].astype(F32)

        def chunk(s, tc):
            rows = pl.ds(s, tc)
            win_q = pad_q[pl.ds(s, tc + HALO), :]
            dcq = pad_dcq[rows, :]
            cq = None
            for k in range(kw):
                off = HALO - (kw - 1) + k
                shifted = win_q[off:off + tc, :]
                acc_w[pl.ds(8 * k, 8), :] += _fold8(dcq * shifted)
                term = w_ref[k:k + 1, :] * shifted
                cq = term if cq is None else cq + term
            db_ref[rows, :] = (dy_ref[rows, :].astype(F32) * cq).astype(BF16)
            win_d = pad_dcq[pl.ds(s, tc + HALO), :]
            dq = None
            for j in range(kw):
                term = w_ref[kw - 1 - j:kw - j, :] * win_d[j:j + tc, :]
                dq = term if dq is None else dq + term
            dcg_ref[rows, :] = (dq * u_ref[rows, :].astype(F32)).astype(BF16)
            du_ref[rows, :] = (dq * c_ref[rows, :].astype(F32)).astype(BF16)

        _chunks(t, fill)
        pl.when(j > 0)(wait_writes)
        _chunks(t, chunk)
        for cp in writes:
            cp.start()
        pl.when(j == nt - 1)(wait_writes)
        for k in range(kw):
            dw_ref[k:k + 1, :] = jnp.sum(acc_w[pl.ds(8 * k, 8), :], axis=0, keepdims=True)

    zspec = [pl.BlockSpec((t, tc_ch), lambda j, o=o: (0, o * nt + j)) for o in range(3)]
    return pl.pallas_call(
        body, name=name, grid=(nt,),
        in_specs=[*zspec, pl.BlockSpec((t, tc_ch), lambda j: (0, j)), pl.BlockSpec((kw, tc_ch), lambda j: (0, j))],
        out_specs=[ANY, pl.BlockSpec((kw, tc_ch), lambda j: (0, j))],
        out_shape=[jax.ShapeDtypeStruct((t, 3 * d_short), BF16), jax.ShapeDtypeStruct((kw, d_short), F32)],
        scratch_shapes=[pltpu.VMEM((t + HALO, tc_ch), F32), pltpu.VMEM((t + HALO, tc_ch), F32),
                        pltpu.VMEM((8 * kw, tc_ch), F32), pltpu.VMEM((3, t, tc_ch), BF16),
                        pltpu.SemaphoreType.DMA((3,))],
        compiler_params=_params(("arbitrary",)),
    )(z, z, z, dy, conv_w)


def adamw(name, w, m, v, contributions):
    r, c = w.shape
    nc = len(contributions)
    n_slots = contributions[0].shape[0]
    tr = 256 if c <= 1024 else 128
    if any(a.shape[1] % tr for a in contributions):
        assert nc == 1
        tr = r
    tiles = [a.shape[1] // tr for a in contributions]
    first = [sum(tiles[:j]) for j in range(nc)]

    def body(w_ref, m_ref, v_ref, *rest):
        g_refs, (grad_ref, delta_ref, nm_ref, nv_ref) = rest[:nc], rest[nc:]
        i = pl.program_id(0)
        g = None
        for j, g_ref in enumerate(g_refs):
            s = g_ref[0].astype(F32)
            for slot in range(1, n_slots):
                s = s + g_ref[slot].astype(F32)
            g = s if g is None else jnp.where(i >= first[j], s, g)
        nm = ADAM_B1 * m_ref[...] + (1.0 - ADAM_B1) * g
        nv = ADAM_B2 * v_ref[...] + (1.0 - ADAM_B2) * (g * g)
        m_hat = nm / (1.0 - ADAM_B1 ** ADAM_STEP)
        v_hat = nv / (1.0 - ADAM_B2 ** ADAM_STEP)
        grad_ref[...] = g
        delta_ref[...] = -ADAM_LR * (m_hat / (jnp.sqrt(v_hat) + ADAM_EPS) + ADAM_WD * w_ref[...])
        nm_ref[...] = nm
        nv_ref[...] = nv

    blk = pl.BlockSpec((tr, c), lambda i: (i, 0))
    g_specs = [pl.BlockSpec((n_slots, tr, c), lambda i, j=j: (0, jnp.clip(i - first[j], 0, tiles[j] - 1), 0))
               for j in range(nc)]
    return pl.pallas_call(
        body, name=name, grid=(r // tr,),
        in_specs=[blk, blk, blk, *g_specs],
        out_specs=[blk] * 4, out_shape=[jax.ShapeDtypeStruct((r, c), F32)] * 4,
        compiler_params=_params(("parallel",)),
    )(w, m, v, *contributions)


def _pad_rows(a, rows):
    return jnp.pad(a, ((0, rows - a.shape[0]), (0, 0)))


def kernel(x, mix_pre_g, mix_post_g, ffn_pre_g, ffn_post_g, ab_w_in, pool_w, pool_scale, conv_w, conv_b, conv_ln_g, conv_ln_b, ab_w_out, sc_w_in, sc_conv_w, sc_w_out, ffn_w1, ffn_w2, loss_target, m_mix_pre_g, m_mix_post_g, m_ffn_pre_g, m_ffn_post_g, m_ab_w_in, m_pool_w, m_pool_scale, m_conv_w, m_conv_b, m_conv_ln_g, m_conv_ln_b, m_ab_w_out, m_sc_w_in, m_sc_conv_w, m_sc_w_out, m_ffn_w1, m_ffn_w2, v_mix_pre_g, v_mix_post_g, v_ffn_pre_g, v_ffn_post_g, v_ab_w_in, v_pool_w, v_pool_scale, v_conv_w, v_conv_b, v_conv_ln_g, v_conv_ln_b, v_ab_w_out, v_sc_w_in, v_sc_conv_w, v_sc_w_out, v_ffn_w1, v_ffn_w2):
    t, d = x.shape[1], x.shape[2]
    d_pool = pool_scale.shape[1]
    d_conv = conv_b.shape[1]
    d_short = d
    ng, pg = pool_w.shape[1], pool_w.shape[3]
    kw, ks = conv_w.shape[1], sc_conv_w.shape[1]
    nb_ab, nb_sc, nb_ff = ab_w_in.shape[2], sc_w_in.shape[2], ffn_w1.shape[2]

    xs = x[0]
    target = loss_target[0]

    lanes = min(128, d_conv // N_DEV)
    small_rows = [kw * (d_conv // N_DEV) // lanes, ks * (d_short // N_DEV) // lanes, ng * (pg // N_DEV) * pg // lanes]
    small_total = -(-sum(small_rows) // 8) * 8
    r0, r1, r2 = small_rows[0], small_rows[0] + small_rows[1], sum(small_rows)

    def pack_small(a_conv, a_sconv, a_pool):
        parts = [a_conv[0].reshape(-1, lanes), a_sconv[0].reshape(-1, lanes), a_pool[0].reshape(-1, lanes)]
        return _pad_rows(jnp.concatenate(parts, axis=0), small_total)

    shards = {
        "ab_in": (ab_w_in, 0, BF16), "small": (pack_small(conv_w, sc_conv_w, pool_w)[None], 0, F32),
        "ab_out": (ab_w_out, 0, BF16), "ff1_0": (ffn_w1, 0, BF16), "ff2_0": (ffn_w2, 0, BF16),
        "sc_in": (sc_w_in, 0, BF16), "sc_out": (sc_w_out, 0, BF16),
        "ff1_1": (ffn_w1, 1, BF16), "ff2_1": (ffn_w2, 1, BF16)}
    direct = ["ab_in", "small", "ab_out"]
    zones = {nm: place_shard("place_" + nm, *shards[nm]) for nm in direct}
    started, token = copies_start("gather_start", [[zones[nm]] for nm in direct], _first_hop, 4)
    started = dict(zip(direct, started))
    ring = {}
    for nm in ["ff1_0", "ff2_0"]:
        zones[nm] = place_shard("place_" + nm, *shards[nm], deps=[token])
        (ring[nm],), token = copies_start("ring_start_" + nm, [[zones[nm]]], _ring_hop1, 3, deps=[token])
    for nm in shards:
        if nm not in zones:
            zones[nm] = place_shard("place_" + nm, *shards[nm], deps=[token])

    ties = [0]

    def after(v, *deps):
        ties[0] += 1
        return tie(f"tie_{ties[0]}", v, *deps)

    def fetch_begin(nm, dep):
        (zone,) = copies_wait("gather_wait_" + nm, started[nm], _first_hop, dep)
        (hop,), tok = copies_start("forward_start_" + nm, [[zone]], _second_hop, 3)
        return hop, tok

    def fetch_end(nm, hop, dep):
        return copies_wait("forward_wait_" + nm, hop, _second_hop, dep)[0]

    def ring_step(tag, dep, second=None, first=None, third=None):
        names, groups, hops, counts = [], [], [], []
        if second is not None:
            groups.append(copies_wait("ring1_wait_" + second, ring[second], _ring_hop1, dep))
            names, hops, counts = names + [second], hops + [_ring_hop2], counts + [4]
        if first is not None:
            groups.append([zones[first]])
            names, hops, counts = names + [first], hops + [_ring_hop1], counts + [3]
        if third is not None:
            groups.append(copies_wait("ring2_wait_" + third, ring[third], _ring_hop2, dep))
            names, hops, counts = names + [third], hops + [_ring_hop3], counts + [1]
        begun, tok = copies_start("ring_start_" + tag, groups, hops, counts, deps=[dep])
        ring.update(zip(names, begun))
        return tok

    def ring_done(nm, dep):
        return copies_wait("ring3_wait_" + nm, ring[nm], _ring_hop3, dep)[0]

    relu = lambda r: jnp.maximum(r, 0.0)
    square = lambda a: a * a
    relu2_bwd = lambda r, a: r * (2.0 * a.astype(F32))

    def row(vec, l):
        return vec[l:l + 1]

    hop_small, _ = fetch_begin("small", token)
    hop_ab_in, tok = fetch_begin("ab_in", token)
    w_small = fetch_end("small", hop_small, tok)
    w_ab_in = fetch_end("ab_in", hop_ab_in, tok)
    w_conv = w_small[:, :r0].reshape(N_DEV, kw, -1).transpose(1, 0, 2).reshape(kw, d_conv)
    w_sconv = w_small[:, r0:r1].reshape(N_DEV, ks, -1).transpose(1, 0, 2).reshape(ks, d_short)
    w_pool = w_small[:, r1:r2].reshape(N_DEV, ng, -1, pg).transpose(1, 0, 2, 3).reshape(ng, pg, pg).astype(BF16)
    h0 = norm_pre("norm_pre", xs, after(row(mix_pre_g, 0), token))
    z0 = mm_nn_blocked("ab_in", h0, w_ab_in, out_dtype=BF16)
    hop, tok = fetch_begin("ab_out", z0)
    z0 = after(z0, tok)
    pooled, y0 = pool_fwd("pool_fwd", z0, w_pool, pool_scale, d_pool, d_pool + d_conv)
    cv = conv_fwd("conv_fwd", z0, w_conv, conv_b, d_pool, d_conv)
    y0 = ln_silu("ln_silu", cv, conv_ln_g, conv_ln_b, y0, d_pool // d_conv)
    w_ab_out = fetch_end("ab_out", hop, y0)
    tok = ring_step("a", w_ab_out, second="ff1_0", first="sc_in")
    y0 = after(y0, tok)
    m0 = mm_nn("ab_out", y0, w_ab_out.reshape(d_pool + d_conv, d), out_dtype=F32)
    x1, h1 = post_pre("post_pre_0", xs, m0, row(mix_post_g, 0), row(ffn_pre_g, 0))
    tok = ring_step("b", h1, second="ff2_0", first="sc_out", third="ff1_0")
    w_ff1_0 = ring_done("ff1_0", tok)
    a0 = mm_nn_blocked("ffn0_up", h1, w_ff1_0, out_dtype=BF16, epilogue=relu)
    tok = ring_step("c", a0, second="sc_in", first="ff1_1", third="ff2_0")
    w_ff2_0 = ring_done("ff2_0", tok).reshape(-1, d)
    f0 = mm_nn("ffn0_down", a0, w_ff2_0, out_dtype=F32, tk=2048, lhs_fn=square)
    tok = ring_step("d", f0, second="sc_out", first="ff2_1", third="sc_in")
    f0 = after(f0, tok)
    x2, h2 = post_pre("post_pre_1", x1, f0, row(ffn_post_g, 0), row(mix_pre_g, 1))
    w_sc_in = ring_done("sc_in", h2)
    z1 = mm_nn_blocked("sc_in", h2, w_sc_in, out_dtype=BF16)
    tok = ring_step("e", z1, second="ff1_1", third="sc_out")
    z1 = after(z1, tok)
    y1 = short_fwd("short_fwd", z1, w_sconv, d_short)
    w_sc_out = ring_done("sc_out", y1).reshape(d_short, d)
    m1 = mm_nn("sc_out", y1, w_sc_out, out_dtype=F32)
    tok = ring_step("f", m1, second="ff2_1")
    m1 = after(m1, tok)
    x3, h3 = post_pre("post_pre_2", x2, m1, row(mix_post_g, 1), row(ffn_pre_g, 1))
    tok = ring_step("g", h3, third="ff1_1")
    w_ff1_1 = ring_done("ff1_1", tok)
    a1 = mm_nn_blocked("ffn1_up", h3, w_ff1_1, out_dtype=BF16, epilogue=relu)
    tok = ring_step("h", a1, third="ff2_1")
    w_ff2_1 = ring_done("ff2_1", tok).reshape(-1, d)
    f1 = mm_nn("ffn1_down", a1, w_ff2_1, out_dtype=F32, tk=2048, lhs_fn=square)
    dx4, df1, loss_part, dg_ffn_post1 = post_loss("post_loss", x3, f1, row(ffn_post_g, 1), target)
    loss = lax.psum(jnp.sum(loss_part) * (0.5 / d), ("x", "y", "c"))

    def reduce_begin(tag, g):
        zone = lax.empty((N_CHIP,) + g.shape[1:], g.dtype)
        (hop,), tok = copies_start("pair_start_" + tag, [[g, zone]], _pair_hop, N_CHIP)
        return hop, tok

    def reduce_middle(tag, hop, dep):
        g, from_sibling = copies_wait("pair_wait_" + tag, hop, _pair_hop, dep)
        pair_sum, zone = pair_add("pair_add_" + tag, g, from_sibling)
        (hop2,), tok = copies_start("chips_start_" + tag, [[pair_sum, zone]], _chip_hop, 3)
        return hop2, tok

    def reduce_end(tag, hop2, dep):
        return copies_wait("chips_wait_" + tag, hop2, _chip_hop, dep)[1]

    dw = mm_tn("ffn1_dw2", a1, df1, out_dtype=BF16, lhs_fn=square)
    red_ff2_1, tok = reduce_begin("ff2_1", dw.reshape(N_DEV, -1, d))
    df1 = after(df1, tok)
    dpre = mm_nt("ffn1_da", df1, w_ff2_1, out_dtype=BF16, extra=a1, epilogue=relu2_bwd)
    dw = mm_tn_blocked("ffn1_dw1", h3, dpre, nb_ff, out_dtype=BF16)
    red_ff1_1, tok = reduce_begin("ff1_1", dw)
    dpre = after(dpre, tok)
    dh3 = mm_nt_blocked("ffn1_dh", dpre, w_ff1_1, out_dtype=BF16)
    red_ff2_1, tok = reduce_middle("ff2_1", red_ff2_1, dh3)
    dh3 = after(dh3, tok)
    dx3, dm1, dg_ffn_pre1, dg_mix_post1 = bwd_pre_post("bwd_3", dx4, x3, row(ffn_pre_g, 1), dh3, m1, row(mix_post_g, 1))

    dw = mm_tn("sc_dwout", y1, dm1, out_dtype=BF16)
    red_sc_out, tok = reduce_begin("sc_out", dw.reshape(N_DEV, -1, d))
    dm1 = after(dm1, tok)
    dy1 = mm_nt("sc_dy", dm1, w_sc_out, out_dtype=BF16)
    red_ff1_1, tok = reduce_middle("ff1_1", red_ff1_1, dy1)
    dy1 = after(dy1, tok)
    dz1, dw_sconv = short_bwd("short_bwd", z1, dy1, w_sconv, d_short)
    dw = mm_tn_blocked("sc_dwin", h2, dz1, nb_sc, out_dtype=BF16)
    red_sc_in, tok = reduce_begin("sc_in", dw)
    dz1 = after(dz1, tok)
    dh2 = mm_nt_blocked("sc_dh", dz1, w_sc_in, out_dtype=BF16)
    red_sc_out, tok = reduce_middle("sc_out", red_sc_out, dh2)
    dh2 = after(dh2, tok)
    dx2, df0, dg_mix_pre1, dg_ffn_post0 = bwd_pre_post("bwd_2", dx3, x2, row(mix_pre_g, 1), dh2, f0, row(ffn_post_g, 0))

    dw = mm_tn("ffn0_dw2", a0, df0, out_dtype=BF16, lhs_fn=square)
    red_ff2_0, tok = reduce_begin("ff2_0", dw.reshape(N_DEV, -1, d))
    df0 = after(df0, tok)
    dpre = mm_nt("ffn0_da", df0, w_ff2_0, out_dtype=BF16, extra=a0, epilogue=relu2_bwd)
    red_sc_in, tok = reduce_middle("sc_in", red_sc_in, dpre)
    dpre = after(dpre, tok)
    dw = mm_tn_blocked("ffn0_dw1", h1, dpre, nb_ff, out_dtype=BF16)
    red_ff1_0, tok = reduce_begin("ff1_0", dw)
    dpre = after(dpre, tok)
    dh1 = mm_nt_blocked("ffn0_dh", dpre, w_ff1_0, out_dtype=BF16)
    red_ff2_0, tok = reduce_middle("ff2_0", red_ff2_0, dh1)
    dh1 = after(dh1, tok)
    dx1, dm0, dg_ffn_pre0, dg_mix_post0 = bwd_pre_post("bwd_1", dx2, x1, row(ffn_pre_g, 0), dh1, m0, row(mix_post_g, 0))

    dw = mm_tn("ab_dwout", y0, dm0, out_dtype=BF16)
    red_ab_out, tok = reduce_begin("ab_out", dw.reshape(N_DEV, -1, d))
    dm0 = after(dm0, tok)
    dy0 = mm_nt("ab_dy", dm0, w_ab_out.reshape(d_pool + d_conv, d), out_dtype=BF16)
    red_ff1_0, tok = reduce_middle("ff1_0", red_ff1_0, dy0)
    dy0 = after(dy0, tok)
    dcv, dg_ln_g, dg_ln_b = ln_silu_bwd("ln_silu_bwd", cv, conv_ln_g, conv_ln_b, dy0, d_pool // d_conv)
    dz0, dw_conv, dg_conv_b = conv_bwd("conv_bwd", z0, dcv, w_conv, d_pool, d_conv)
    dz0, dw_pool, dg_pool_scale = pool_bwd("pool_bwd", pooled, dy0, w_pool, pool_scale, dz0)
    small_parts = [
        dw_conv.reshape(kw, N_DEV, -1).transpose(1, 0, 2).reshape(N_DEV, -1, lanes),
        dw_sconv.reshape(ks, N_DEV, -1).transpose(1, 0, 2).reshape(N_DEV, -1, lanes),
        dw_pool.reshape(ng, N_DEV, pg // N_DEV, pg).transpose(1, 0, 2, 3).reshape(N_DEV, -1, lanes),
    ]
    small = jnp.pad(jnp.concatenate(small_parts, axis=1), ((0, 0), (0, small_total - r2), (0, 0)))
    red_small, tok = reduce_begin("small", small)
    red_ab_out, tok2 = reduce_middle("ab_out", red_ab_out, dz0)
    dz0 = after(dz0, tok, tok2)
    dw = mm_tn_blocked("ab_dwin", h0, dz0, nb_ab, out_dtype=BF16)
    red_ab_in, tok = reduce_begin("ab_in", dw)
    dz0 = after(dz0, tok)
    dh0 = mm_nt_blocked("ab_dh", dz0, w_ab_in, out_dtype=BF16)
    red_small, tok = reduce_middle("small", red_small, dh0)
    dh0 = after(dh0, tok)
    grad_x, dg_mix_pre0 = bwd_pre_final("bwd_0", dx1, xs, row(mix_pre_g, 0), dh0)
    red_ab_in, tok = reduce_middle("ab_in", red_ab_in, grad_x)

    fold = lambda a: jnp.sum(a, axis=0, keepdims=True)
    rep_rows = [fold(dg_mix_pre0), fold(dg_mix_pre1), fold(dg_mix_post0), fold(dg_mix_post1),
                fold(dg_ffn_pre0), fold(dg_ffn_pre1), fold(dg_ffn_post0), fold(dg_ffn_post1)]
    tail = jnp.concatenate([dg_pool_scale, dg_conv_b, fold(dg_ln_g), fold(dg_ln_b)], axis=1).reshape(-1, d)
    rep = _pad_rows(jnp.concatenate(rep_rows + [tail], axis=0), 16)
    (rep_hop,), _ = copies_start("rep_start", [[place_shard("place_rep", rep[None], 0, F32)]], _first_hop, 4)

    def pack_rep(a_mix_pre, a_mix_post, a_ffn_pre, a_ffn_post, a_scale, a_b, a_g, a_lb):
        tail_ = jnp.concatenate([a_scale, a_b, a_g, a_lb], axis=1).reshape(-1, d)
        return _pad_rows(jnp.concatenate([a_mix_pre, a_mix_post, a_ffn_pre, a_ffn_post, tail_], axis=0), 16)

    def upd(name, w, m, v, contribs):
        shape = w.shape
        flat2 = lambda a: a.reshape(-1, shape[-1])
        outs = adamw(name, flat2(w), flat2(m), flat2(v), contribs)
        return [o.reshape(shape) for o in outs]

    g_ff2 = [reduce_end("ff2_0", red_ff2_0, tok), reduce_end("ff2_1", red_ff2_1, tok)]
    o_ff2 = upd("adam_ffn_w2", ffn_w2, m_ffn_w2, v_ffn_w2, g_ff2)
    (rep_zone,) = copies_wait("rep_wait", rep_hop, _first_hop, o_ff2[0])
    (rep_hop,), _ = copies_start("rep_forward_start", [[rep_zone]], _second_hop, 3)
    g_ff1 = [reduce_end("ff1_0", red_ff1_0, o_ff2[0]), reduce_end("ff1_1", red_ff1_1, o_ff2[0])]
    o_ff1 = upd("adam_ffn_w1", ffn_w1, m_ffn_w1, v_ffn_w1, g_ff1)
    (rep_all,) = copies_wait("rep_forward_wait", rep_hop, _second_hop, o_ff1[0])
    o_rep = adamw("adam_replicated",
                  pack_rep(mix_pre_g, mix_post_g, ffn_pre_g, ffn_post_g, pool_scale, conv_b, conv_ln_g, conv_ln_b),
                  pack_rep(m_mix_pre_g, m_mix_post_g, m_ffn_pre_g, m_ffn_post_g, m_pool_scale, m_conv_b, m_conv_ln_g, m_conv_ln_b),
                  pack_rep(v_mix_pre_g, v_mix_post_g, v_ffn_pre_g, v_ffn_post_g, v_pool_scale, v_conv_b, v_conv_ln_g, v_conv_ln_b),
                  [rep_all])
    o_sc_out = upd("adam_sc_out", sc_w_out, m_sc_w_out, v_sc_w_out, [reduce_end("sc_out", red_sc_out, o_ff1[0])])
    o_sc_in = upd("adam_sc_in", sc_w_in, m_sc_w_in, v_sc_w_in, [reduce_end("sc_in", red_sc_in, o_sc_out[0])])
    o_ab_out = upd("adam_ab_out", ab_w_out, m_ab_w_out, v_ab_w_out, [reduce_end("ab_out", red_ab_out, o_sc_in[0])])
    o_small = adamw("adam_small", pack_small(conv_w, sc_conv_w, pool_w), pack_small(m_conv_w, m_sc_conv_w, m_pool_w),
                    pack_small(v_conv_w, v_sc_conv_w, v_pool_w), [reduce_end("small", red_small, o_ab_out[0])])
    o_ab_in = upd("adam_ab_in", ab_w_in, m_ab_w_in, v_ab_w_in, [reduce_end("ab_in", red_ab_in, o_small[0])])

    def unpack_small(o):
        return o[:r0].reshape(conv_w.shape), o[r0:r1].reshape(sc_conv_w.shape), o[r1:r2].reshape(pool_w.shape)

    def unpack_rep(o):
        tail_ = o[8:8 + tail.shape[0]].reshape(1, -1)
        n1 = d_pool
        return dict(mix_pre_g=o[0:2], mix_post_g=o[2:4], ffn_pre_g=o[4:6], ffn_post_g=o[6:8],
                    pool_scale=tail_[:, :n1], conv_b=tail_[:, n1:n1 + d_conv],
                    conv_ln_g=tail_[:, n1 + d_conv:n1 + 2 * d_conv], conv_ln_b=tail_[:, n1 + 2 * d_conv:n1 + 3 * d_conv])

    results = []
    for kind in range(4):
        rep_o = unpack_rep(o_rep[kind])
        s_conv, s_sconv, s_pool = unpack_small(o_small[kind])
        results.append([
            rep_o["mix_pre_g"], rep_o["mix_post_g"], rep_o["ffn_pre_g"], rep_o["ffn_post_g"],
            o_ab_in[kind], s_pool, rep_o["pool_scale"], s_conv, rep_o["conv_b"], rep_o["conv_ln_g"], rep_o["conv_ln_b"],
            o_ab_out[kind], o_sc_in[kind], s_sconv, o_sc_out[kind], o_ff1[kind], o_ff2[kind]])

    return (loss, grad_x[None], *results[0], *results[1], *results[2], *results[3])
```

```python
from typing import Callable, NamedTuple

import jax
import jax.numpy as jnp
from jax import lax
from jax.experimental import pallas as pl
from jax.experimental.pallas import tpu as pltpu

F32 = jnp.float32
BF16 = jnp.bfloat16
MESH = pl.DeviceIdType.MESH
ANY = pl.BlockSpec(memory_space=pl.ANY)

NORM_EPS = 1e-6
POOL_WINDOWS = (2, 4, 8, 16)
ADAM_LR = 0.001
ADAM_B1 = 0.9
ADAM_B2 = 0.999
ADAM_EPS = 1e-08
ADAM_WD = 0.01
ADAM_STEP = 10

N_DEV = 8
VMEM_LIMIT = 56 * 1024 * 1024
PAIR_ADD_BLOCK = 1 << 20
MATMUL_ROWS = 2048
ROW_TILE = 256
CHANNEL_TILE = 256
TIME_CHUNK = 64
HALO = 32

NN = (((1,), (0,)), ((), ()))
NT = (((1,), (1,)), ((), ()))
TN = (((0,), (0,)), ((), ()))


def _params(sem):
    return pltpu.CompilerParams(dimension_semantics=sem, vmem_limit_bytes=VMEM_LIMIT)


def _place():
    x, y, c = lax.axis_index("x"), lax.axis_index("y"), lax.axis_index("c")
    return x, y, c


def _slot(px, py, pc):
    return 4 * px + 2 * py + pc


HBM = pl.BlockSpec(memory_space=pltpu.HBM)
SEM = pl.BlockSpec(memory_space=pltpu.SEMAPHORE)
EFFECT = pltpu.SideEffectType.DATAFLOW_SIDE_EFFECTING
TOKEN = jax.ShapeDtypeStruct((8, 128), F32)


def _in_hbm(a):
    return pltpu.with_memory_space_constraint(a, pltpu.HBM)


CHIPS = [(0, 0), (0, 1), (1, 0), (1, 1)]
N_CHIP = len(CHIPS)


def _chip(px, py):
    return 2 * px + py


def _first_hop(bufs, sends, recvs, waiting):
    (land,) = bufs
    x, y, c = _place()
    me = _slot(x, y, c)
    peers = [(x, y, 1 - c), (1 - x, y, c), (x, 1 - y, c), (1 - x, 1 - y, c)]
    return [pltpu.make_async_remote_copy(
        src_ref=land.at[me], dst_ref=land.at[_slot(*p) if waiting else me],
        send_sem=sends.at[k], recv_sem=recvs.at[k], device_id=p, device_id_type=MESH) for k, p in enumerate(peers)]


def _second_hop(bufs, sends, recvs, waiting):
    (land,) = bufs
    x, y, c = _place()
    return [pltpu.make_async_remote_copy(
        src_ref=land.at[_slot(px, py, c)], dst_ref=land.at[_slot(px, py, 1 - c if waiting else c)],
        send_sem=sends.at[k], recv_sem=recvs.at[k], device_id=(x, y, 1 - c), device_id_type=MESH)
        for k, (px, py) in enumerate([(1 - x, y), (x, 1 - y), (1 - x, 1 - y)])]


def _ring_hop1(bufs, sends, recvs, waiting):
    (land,) = bufs
    x, y, c = _place()
    me = _slot(x, y, c)
    peers = [(1 - x, y, c), (x, 1 - y, c), (x, y, 1 - c)]
    return [pltpu.make_async_remote_copy(
        src_ref=land.at[me], dst_ref=land.at[_slot(*p) if waiting else me],
        send_sem=sends.at[k], recv_sem=recvs.at[k], device_id=p, device_id_type=MESH) for k, p in enumerate(peers)]


def _ring_hop2(bufs, sends, recvs, waiting):
    (land,) = bufs
    x, y, c = _place()
    half = land.shape[1] // 2
    first, second = pl.ds(0, half), pl.ds(half, half)
    nx, ny, diag = _slot(1 - x, y, c), _slot(x, 1 - y, c), _slot(1 - x, 1 - y, c)
    plan = [
        (land.at[ny, first], land.at[diag, first], (1 - x, y, c)),
        (land.at[nx, second], land.at[diag, second], (x, 1 - y, c)),
        (land.at[nx], land.at[_slot(1 - x, y, 1 - c)], (x, y, 1 - c)),
        (land.at[ny], land.at[_slot(x, 1 - y, 1 - c)], (x, y, 1 - c))]
    return [pltpu.make_async_remote_copy(
        src_ref=src, dst_ref=mine if waiting else src, send_sem=sends.at[k], recv_sem=recvs.at[k],
        device_id=to, device_id_type=MESH) for k, (src, mine, to) in enumerate(plan)]


def _ring_hop3(bufs, sends, recvs, waiting):
    (land,) = bufs
    x, y, c = _place()
    return [pltpu.make_async_remote_copy(
        src_ref=land.at[_slot(1 - x, 1 - y, c)], dst_ref=land.at[_slot(1 - x, 1 - y, 1 - c if waiting else c)],
        send_sem=sends.at[0], recv_sem=recvs.at[0], device_id=(x, y, 1 - c), device_id_type=MESH)]


def _pair_hop(bufs, sends, recvs, waiting):
    g, land = bufs
    x, y, c = _place()
    return [pltpu.make_async_remote_copy(
        src_ref=g.at[_slot(qx, qy, 1 - c)], dst_ref=land.at[q],
        send_sem=sends.at[q], recv_sem=recvs.at[q], device_id=(x, y, 1 - c), device_id_type=MESH)
        for q, (qx, qy) in enumerate(CHIPS)]


def _chip_hop(bufs, sends, recvs, waiting):
    p, land = bufs
    x, y, c = _place()
    return [pltpu.make_async_remote_copy(
        src_ref=p.at[_chip(px, py)], dst_ref=land.at[_chip(px, py) if waiting else _chip(x, y)],
        send_sem=sends.at[k], recv_sem=recvs.at[k], device_id=(px, py, c), device_id_type=MESH)
        for k, (px, py) in enumerate([(1 - x, y), (x, 1 - y), (1 - x, 1 - y)])]


def copies_start(name, groups, hop, n_copies, deps=()):
    flat = [b for grp in groups for b in grp]
    nb, ng = len(flat), len(groups)
    deps = list(deps)
    hops = list(hop) if isinstance(hop, (list, tuple)) else [hop] * ng
    counts = list(n_copies) if isinstance(n_copies, (list, tuple)) else [n_copies] * ng

    def body(*refs):
        ins, token = refs[:nb], refs[-1]
        sems = refs[nb + len(deps):nb + len(deps) + 2 * ng]
        i = 0
        for gi, grp in enumerate(groups):
            for cp in hops[gi](ins[i:i + len(grp)], sems[2 * gi], sems[2 * gi + 1], False):
                cp.start()
            i += len(grp)
        token[...] = jnp.zeros_like(token)

    outs = pl.pallas_call(
        body, name=name,
        out_shape=([pltpu.SemaphoreType.DMA((n,)) for n in counts for _ in range(2)]
                   + [pltpu.HBM(b.shape, b.dtype) for b in flat] + [TOKEN]),
        in_specs=[HBM] * nb + [ANY] * len(deps),
        out_specs=[SEM] * (2 * ng) + [HBM] * nb + [pl.BlockSpec(memory_space=pltpu.VMEM)],
        input_output_aliases={i: 2 * ng + i for i in range(nb)},
        compiler_params=pltpu.CompilerParams(has_side_effects=EFFECT),
    )(*[_in_hbm(b) for b in flat], *deps)
    started, i = [], 0
    for gi, grp in enumerate(groups):
        started.append((outs[2 * gi], outs[2 * gi + 1], list(outs[2 * ng + i:2 * ng + i + len(grp)])))
        i += len(grp)
    return started, outs[-1]


def copies_wait(name, started, hop, after):
    sends, recvs, bufs = started
    nb = len(bufs)

    def body(*refs):
        for cp in hop(refs[:nb], refs[nb], refs[nb + 1], True):
            cp.wait_send()
            cp.wait_recv()

    outs = pl.pallas_call(
        body, name=name,
        out_shape=[pltpu.HBM(b.shape, b.dtype) for b in bufs],
        in_specs=[HBM] * nb + [SEM, SEM, ANY], out_specs=[HBM] * nb,
        input_output_aliases={i: i for i in range(nb)},
        compiler_params=pltpu.CompilerParams(has_side_effects=EFFECT),
    )(*bufs, sends, recvs, after)
    return list(outs)


def place_shard(name, w, layer, dtype, deps=()):
    _, r, c = w.shape
    tr = _tile(r, 1024)
    x, y, core = _place()
    me = _slot(x, y, core).astype(jnp.int32).reshape(1)

    def body(me_ref, w_ref, *rest):
        rest[-1][...] = w_ref[...].astype(dtype)

    return pl.pallas_call(
        body, name=name,
        grid_spec=pltpu.PrefetchScalarGridSpec(
            num_scalar_prefetch=1, grid=(r // tr,),
            in_specs=[pl.BlockSpec((None, tr, c), lambda i, me_ref: (layer, i, 0))] + [ANY] * len(deps),
            out_specs=pl.BlockSpec((None, tr, c), lambda i, me_ref: (me_ref[0], i, 0))),
        out_shape=jax.ShapeDtypeStruct((N_DEV, r, c), dtype),
        compiler_params=_params(("parallel",)),
    )(me, w, *deps)


def tie(name, x, *deps):
    def body(*refs):
        del refs

    return pl.pallas_call(
        body, name=name, out_shape=jax.ShapeDtypeStruct(x.shape, x.dtype),
        in_specs=[ANY] * (1 + len(deps)), out_specs=ANY, input_output_aliases={0: 0},
    )(x, *deps)


def pair_add(name, g, from_sibling):
    _, r, c_dim = g.shape
    tr = r
    while tr * c_dim > PAIR_ADD_BLOCK and tr % 16 == 0:
        tr //= 2
    x, y, core = _place()
    where = jnp.stack([core, _chip(x, y)]).astype(jnp.int32)

    def body(where_ref, g_ref, s_ref, o_ref, zone_ref):
        total = (g_ref[...].astype(F32) + s_ref[...].astype(F32)).astype(o_ref.dtype)
        o_ref[...] = total

        @pl.when(pl.program_id(1) == where_ref[1])
        def _():
            zone_ref[...] = total

    blk = pl.BlockSpec((None, tr, c_dim), lambda i, q, where_ref: (q, i, 0))
    return pl.pallas_call(
        body, name=name,
        grid_spec=pltpu.PrefetchScalarGridSpec(
            num_scalar_prefetch=1, grid=(r // tr, N_CHIP),
            in_specs=[pl.BlockSpec((None, None, tr, c_dim), lambda i, q, where_ref: (q, where_ref[0], i, 0)), blk],
            out_specs=[blk, pl.BlockSpec((None, tr, c_dim), lambda i, q, where_ref: (where_ref[1], i, 0))]),
        out_shape=[jax.ShapeDtypeStruct((N_CHIP, r, c_dim), g.dtype)] * 2,
        compiler_params=_params(("parallel", "arbitrary")),
    )(where, g.reshape(N_CHIP, 2, r, c_dim), from_sibling)


class Side(NamedTuple):
    prefetch: jax.Array
    ins: list
    out_shapes: list
    specs: Callable
    fn: Callable


def pair_add_side(g, from_sibling, n_steps):
    _, r, c_dim = g.shape
    rb = r * N_CHIP // n_steps
    assert rb * n_steps == r * N_CHIP and rb % 16 == 0, (g.shape, n_steps)
    x, y, core = _place()
    where = jnp.stack([core, _chip(x, y)]).astype(jnp.int32)

    def fn(step, where_ref, g_ref, s_ref, o_ref, zone_ref):
        total = (g_ref[...].astype(F32) + s_ref[...].astype(F32)).astype(o_ref.dtype)
        o_ref[...] = total

        @pl.when(step % N_CHIP == where_ref[1])
        def _():
            zone_ref[...] = total

    def specs(step_of):
        chip = lambda i, j, k: step_of(i, j, k) % N_CHIP
        rows = lambda i, j, k: step_of(i, j, k) // N_CHIP
        blk = pl.BlockSpec((None, rb, c_dim), lambda i, j, k, w: (chip(i, j, k), rows(i, j, k), 0))
        mine = pl.BlockSpec((None, None, rb, c_dim), lambda i, j, k, w: (chip(i, j, k), w[0], rows(i, j, k), 0))
        zone = pl.BlockSpec((None, rb, c_dim), lambda i, j, k, w: (w[1], rows(i, j, k), 0))
        return [mine, blk], [blk, zone]

    shape = jax.ShapeDtypeStruct((N_CHIP, r, c_dim), g.dtype)
    return Side(where, [g.reshape(N_CHIP, 2, r, c_dim), from_sibling], [shape, shape], specs, fn)


def _matmul(name, lhs, rhs, *, out_shape, out_dtype, grid, lhs_spec, rhs_spec, out_spec, dims, acc_shape,
            lhs_fn=None, extra=(), extra_specs=(), epilogue=None, parts=1, side_of=None):
    nk = grid[2]
    n_extra = len(extra)
    side = side_of(grid[0] * grid[1] * grid[2]) if side_of else None
    n_side_in = len(side.ins) if side else 0
    n_side_out = len(side.out_shapes) if side else 0

    def body(*refs):
        if side:
            prefetch_ref, refs = refs[0], refs[1:]
        lhs_ref, rhs_ref = refs[0], refs[1]
        extra_refs = refs[2:2 + n_extra]
        side_in = refs[2 + n_extra:2 + n_extra + n_side_in]
        out_ref = refs[2 + n_extra + n_side_in]
        side_out = refs[3 + n_extra + n_side_in:3 + n_extra + n_side_in + n_side_out]
        if side:
            step = (pl.program_id(0) * grid[1] + pl.program_id(1)) * grid[2] + pl.program_id(2)
            side.fn(step, prefetch_ref, *side_in, *side_out)

        def product():
            if parts == 1:
                a = lhs_ref[...]
                if lhs_fn is not None:
                    a = lhs_fn(a)
                return lax.dot_general(a, rhs_ref[...], dims, preferred_element_type=F32)
            width = lhs_ref.shape[1] // parts
            total = None
            for b in range(parts):
                term = lax.dot_general(lhs_ref[:, b * width:(b + 1) * width], rhs_ref[b], dims,
                                       preferred_element_type=F32)
                total = term if total is None else total + term
            return total

        def finish(r):
            if epilogue is not None:
                r = epilogue(r, *[e[...] for e in extra_refs])
            out_ref[...] = r.astype(out_dtype)

        if nk == 1:
            finish(product())
        else:
            acc_ref = refs[-1]
            k = pl.program_id(2)

            @pl.when(k == 0)
            def _():
                acc_ref[...] = product()

            @pl.when(jnp.logical_and(k > 0, k < nk - 1))
            def _():
                acc_ref[...] += product()

            @pl.when(k == nk - 1)
            def _():
                finish(acc_ref[...] + product())

    scratch = [pltpu.VMEM(acc_shape, F32)] if nk > 1 else []
    if side is None:
        return pl.pallas_call(
            body, name=name, grid=grid,
            out_shape=jax.ShapeDtypeStruct(out_shape, out_dtype),
            in_specs=[lhs_spec, rhs_spec, *extra_specs], out_specs=out_spec, scratch_shapes=scratch,
            compiler_params=_params(("parallel", "parallel", "arbitrary")),
        )(lhs, rhs, *extra)
    side_in_specs, side_out_specs = side.specs(lambda i, j, k: (i * grid[1] + j) * grid[2] + k)
    return pl.pallas_call(
        body, name=name,
        grid_spec=pltpu.PrefetchScalarGridSpec(
            num_scalar_prefetch=1, grid=grid,
            in_specs=[lhs_spec, rhs_spec, *extra_specs, *side_in_specs],
            out_specs=[out_spec, *side_out_specs], scratch_shapes=scratch),
        out_shape=[jax.ShapeDtypeStruct(out_shape, out_dtype), *side.out_shapes],
        compiler_params=_params(("arbitrary", "arbitrary", "arbitrary")),
    )(side.prefetch, lhs, rhs, *extra, *side.ins)


def _tile(n, want):
    return want if n % want == 0 else n


def mm_nn(name, x, w, *, out_dtype, tn=512, tk=None, lhs_fn=None, epilogue=None):
    t, kdim = x.shape
    n = w.shape[1]
    tm, tn = _tile(t, MATMUL_ROWS), _tile(n, tn)
    tk = kdim if tk is None else _tile(kdim, tk)
    return _matmul(
        name, x, w, out_shape=(t, n), out_dtype=out_dtype, grid=(t // tm, n // tn, kdim // tk),
        lhs_spec=pl.BlockSpec((tm, tk), lambda i, j, k, *_: (i, k)),
        rhs_spec=pl.BlockSpec((tk, tn), lambda i, j, k, *_: (k, j)),
        out_spec=pl.BlockSpec((tm, tn), lambda i, j, k, *_: (i, j)),
        dims=NN, acc_shape=(tm, tn), lhs_fn=lhs_fn, epilogue=epilogue)


def mm_nn_blocked(name, x, w, *, out_dtype, epilogue=None):
    t, kdim = x.shape
    nb = w.shape[2]
    tm = _tile(t, MATMUL_ROWS)
    tn = nb // 2 if nb >= 1024 else nb
    sub = nb // tn
    return _matmul(
        name, x, w, out_shape=(t, N_DEV * nb), out_dtype=out_dtype, grid=(t // tm, N_DEV * sub, 1),
        lhs_spec=pl.BlockSpec((tm, kdim), lambda i, j, k, *_: (i, k)),
        rhs_spec=pl.BlockSpec((None, kdim, tn), lambda i, j, k, *_: (j // sub, k, j % sub)),
        out_spec=pl.BlockSpec((tm, tn), lambda i, j, k, *_: (i, j)),
        dims=NN, acc_shape=(tm, tn), epilogue=epilogue)


def mm_nt(name, dy, w, *, out_dtype, tn=512, extra=None, epilogue=None, side_of=None):
    t, n = dy.shape
    kdim = w.shape[0]
    tm, tn = _tile(t, MATMUL_ROWS), _tile(kdim, tn)
    extra_arrs = () if extra is None else (extra,)
    extra_specs = () if extra is None else (pl.BlockSpec((tm, tn), lambda i, j, k, *_: (i, j)),)
    return _matmul(
        name, dy, w, out_shape=(t, kdim), out_dtype=out_dtype, grid=(t // tm, kdim // tn, 1),
        lhs_spec=pl.BlockSpec((tm, n), lambda i, j, k, *_: (i, k)),
        rhs_spec=pl.BlockSpec((tn, n), lambda i, j, k, *_: (j, k)),
        out_spec=pl.BlockSpec((tm, tn), lambda i, j, k, *_: (i, j)),
        dims=NT, acc_shape=(tm, tn), extra=extra_arrs, extra_specs=extra_specs, epilogue=epilogue, side_of=side_of)


def mm_nt_blocked(name, dz, w, *, out_dtype, tn=512, side_of=None):
    t = dz.shape[0]
    kdim, nb = w.shape[1], w.shape[2]
    tm, tn = _tile(t, MATMUL_ROWS), _tile(kdim, tn)
    parts = 2
    return _matmul(
        name, dz, w, out_shape=(t, kdim), out_dtype=out_dtype, grid=(t // tm, kdim // tn, N_DEV // parts),
        lhs_spec=pl.BlockSpec((tm, parts * nb), lambda i, j, k, *_: (i, k)),
        rhs_spec=pl.BlockSpec((parts, tn, nb), lambda i, j, k, *_: (k, j, 0)),
        out_spec=pl.BlockSpec((tm, tn), lambda i, j, k, *_: (i, j)),
        dims=NT, acc_shape=(tm, tn), parts=parts, side_of=side_of)


def mm_tn(name, x, dy, *, out_dtype, tk=1024, tn=1024, lhs_fn=None, side_of=None):
    t, kdim = x.shape
    n = dy.shape[1]
    tk, tn = _tile(kdim, tk), _tile(n, tn)
    return _matmul(
        name, x, dy, out_shape=(kdim, n), out_dtype=out_dtype, grid=(kdim // tk, n // tn, 1),
        lhs_spec=pl.BlockSpec((t, tk), lambda i, j, k, *_: (k, i)),
        rhs_spec=pl.BlockSpec((t, tn), lambda i, j, k, *_: (k, j)),
        out_spec=pl.BlockSpec((tk, tn), lambda i, j, k, *_: (i, j)),
        dims=TN, acc_shape=(tk, tn), lhs_fn=lhs_fn, side_of=side_of)


def mm_tn_blocked(name, x, dz, nb, *, out_dtype, tk=1024, side_of=None):
    t, kdim = x.shape
    tk = _tile(kdim, tk)
    return _matmul(
        name, x, dz, out_shape=(N_DEV, kdim, nb), out_dtype=out_dtype, grid=(kdim // tk, N_DEV, 1),
        lhs_spec=pl.BlockSpec((t, tk), lambda i, j, k, *_: (k, i)),
        rhs_spec=pl.BlockSpec((t, nb), lambda i, j, k, *_: (k, j)),
        out_spec=pl.BlockSpec((None, tk, nb), lambda i, j, k, *_: (j, i, 0)),
        dims=TN, acc_shape=(tk, nb), side_of=side_of)


def _rstd(v):
    return lax.rsqrt(jnp.mean(v * v, axis=-1, keepdims=True) + NORM_EPS)


def _rms_bwd(v, g, dy):
    r = _rstd(v)
    vhat = v * r
    dvh = dy * g
    dv = r * (dvh - vhat * jnp.mean(dvh * vhat, axis=-1, keepdims=True))
    return dv, dy * vhat


def _fold8(v):
    rows, n = v.shape
    return jnp.sum(v.reshape(rows // 8, 8, n), axis=0)


def _fold_lanes(v):
    out = v[:, 0:128]
    for i in range(1, v.shape[1] // 128):
        out = out + v[:, 128 * i:128 * (i + 1)]
    return out


def _accumulate(ref, v):
    i = pl.program_id(0)

    @pl.when(i == 0)
    def _():
        ref[...] = v

    @pl.when(i > 0)
    def _():
        ref[...] += v


def _row_call(body, name, t, ins, row_in, outs, acc_outs=(), tr=ROW_TILE):
    tr = _tile(t, tr)

    def in_spec(a, tiled):
        if isinstance(tiled, tuple):
            width, j = tiled
            return pl.BlockSpec((tr, width), lambda i: (i, j))
        return pl.BlockSpec((tr, a.shape[1]), lambda i: (i, 0)) if tiled else pl.BlockSpec(a.shape, lambda i: (0, 0))

    in_specs = [in_spec(a, tiled) for a, tiled in zip(ins, row_in)]
    out_specs = [pl.BlockSpec((tr, n), lambda i: (i, 0)) for n, _ in outs]
    out_specs += [pl.BlockSpec((8, n), lambda i: (0, 0)) for n in acc_outs]
    out_shape = [jax.ShapeDtypeStruct((t, n), dt) for n, dt in outs]
    out_shape += [jax.ShapeDtypeStruct((8, n), F32) for n in acc_outs]
    return pl.pallas_call(
        body, name=name, grid=(t // tr,), in_specs=in_specs, out_specs=out_specs, out_shape=out_shape,
        compiler_params=_params(("arbitrary",) if acc_outs else ("parallel",)),
    )(*ins)


def norm_pre(name, x, g):
    t, d = x.shape

    def body(x_ref, g_ref, h_ref):
        v = x_ref[...]
        h_ref[...] = (v * _rstd(v) * g_ref[...]).astype(BF16)

    return _row_call(body, name, t, [x, g], [True, False], [(d, BF16)])[0]


def post_pre(name, x, m, g_post, g_pre):
    t, d = x.shape

    def body(x_ref, m_ref, gp_ref, gn_ref, xo_ref, h_ref):
        mv = m_ref[...]
        xn = x_ref[...] + mv * _rstd(mv) * gp_ref[...]
        xo_ref[...] = xn
        h_ref[...] = (xn * _rstd(xn) * gn_ref[...]).astype(BF16)

    return _row_call(body, name, t, [x, m, g_post, g_pre], [True, True, False, False], [(d, F32), (d, BF16)])


def post_loss(name, x, f, g_post, target):
    t, d = x.shape

    def body(x_ref, f_ref, g_ref, t_ref, dx_ref, df_ref, loss_ref, dg_ref):
        fv = f_ref[...]
        g = g_ref[...]
        out = x_ref[...] + fv * _rstd(fv) * g
        err = out - t_ref[...]
        dx = err * (1.0 / d)
        dx_ref[...] = dx
        dfv, dg_rows = _rms_bwd(fv, g, dx)
        df_ref[...] = dfv.astype(BF16)
        _accumulate(loss_ref, _fold8(_fold_lanes(err * err)))
        _accumulate(dg_ref, _fold8(dg_rows))

    return _row_call(body, name, t, [x, f, g_post, target], [True, True, False, True],
                     [(d, F32), (d, BF16)], acc_outs=(128, d))


def bwd_pre_post(name, dx_out, x_in, g_pre, dh, f_prev, g_post_prev):
    t, d = x_in.shape

    def body(dxo_ref, x_ref, gpre_ref, dh_ref, f_ref, gpost_ref, dxi_ref, df_ref, dgpre_ref, dgpost_ref):
        dxv, dgpre_rows = _rms_bwd(x_ref[...], gpre_ref[...], dh_ref[...].astype(F32))
        dxi = dxo_ref[...] + dxv
        dxi_ref[...] = dxi
        dfv, dgpost_rows = _rms_bwd(f_ref[...], gpost_ref[...], dxi)
        df_ref[...] = dfv.astype(BF16)
        _accumulate(dgpre_ref, _fold8(dgpre_rows))
        _accumulate(dgpost_ref, _fold8(dgpost_rows))

    return _row_call(body, name, t, [dx_out, x_in, g_pre, dh, f_prev, g_post_prev],
                     [True, True, False, True, True, False], [(d, F32), (d, BF16)], acc_outs=(d, d))


def bwd_pre_final(name, dx_out, x_in, g_pre, dh):
    t, d = x_in.shape

    def body(dxo_ref, x_ref, gpre_ref, dh_ref, dxi_ref, dgpre_ref):
        dxv, dgpre_rows = _rms_bwd(x_ref[...], gpre_ref[...], dh_ref[...].astype(F32))
        dxi_ref[...] = dxo_ref[...] + dxv
        _accumulate(dgpre_ref, _fold8(dgpre_rows))

    return _row_call(body, name, t, [dx_out, x_in, g_pre, dh], [True, True, False, True], [(d, F32)], acc_outs=(d,))


def _layer_norm_parts(cv):
    mu = jnp.mean(cv, axis=-1, keepdims=True)
    xc = cv - mu
    rstd = lax.rsqrt(jnp.mean(xc * xc, axis=-1, keepdims=True) + NORM_EPS)
    return xc * rstd, rstd


def ln_silu(name, cv, g, b, y, y_block):
    t, n = cv.shape
    tr = _tile(t, ROW_TILE)

    def body(c_ref, g_ref, b_ref, y_in_ref, y_ref):
        chat, _ = _layer_norm_parts(c_ref[...])
        ln = chat * g_ref[...] + b_ref[...]
        y_ref[...] = (ln * jax.nn.sigmoid(ln)).astype(BF16)

    vec = pl.BlockSpec((1, n), lambda i: (0, 0))
    return pl.pallas_call(
        body, name=name, grid=(t // tr,),
        in_specs=[pl.BlockSpec((tr, n), lambda i: (i, 0)), vec, vec, ANY],
        out_specs=pl.BlockSpec((tr, n), lambda i: (i, y_block)),
        out_shape=jax.ShapeDtypeStruct(y.shape, y.dtype), input_output_aliases={3: 0},
        compiler_params=_params(("parallel",)),
    )(cv, g, b, y)


def ln_silu_bwd(name, cv, g, b, dy, dy_block):
    t, n = cv.shape

    def body(c_ref, g_ref, b_ref, dy_ref, dc_ref, dg_ref, db_ref):
        chat, rstd = _layer_norm_parts(c_ref[...])
        g = g_ref[...]
        ln = chat * g + b_ref[...]
        s = jax.nn.sigmoid(ln)
        dln = dy_ref[...].astype(F32) * (s * (1.0 + ln * (1.0 - s)))
        dchat = dln * g
        dc_ref[...] = rstd * (dchat - jnp.mean(dchat, axis=-1, keepdims=True)
                              - chat * jnp.mean(dchat * chat, axis=-1, keepdims=True))
        _accumulate(dg_ref, _fold8(dln * chat))
        _accumulate(db_ref, _fold8(dln))

    return _row_call(body, name, t, [cv, g, b, dy], [True, False, False, (n, dy_block)], [(n, F32)], acc_outs=(n, n))


def _chunks(t, fn, tc=TIME_CHUNK):
    tc = _tile(t, tc)

    def step(i, carry):
        fn(pl.multiple_of(i * tc, tc), tc)
        return carry

    lax.fori_loop(0, t // tc, step, 0)


def _shifted(window, offsets, tc):
    by_residue = {}
    for k, off in enumerate(offsets):
        by_residue.setdefault(off % 8, []).append((k, off))
    for res, taps in by_residue.items():
        base = window[res:res + tc + max(off for _, off in taps) - res, :]
        for k, off in taps:
            yield k, base[off - res:off - res + tc, :]


def _taps(window, w_ref, offsets, tc, flip=False):
    acc = None
    for k, rows in _shifted(window, offsets, tc):
        kk = len(offsets) - 1 - k if flip else k
        term = w_ref[kk:kk + 1, :] * rows
        acc = term if acc is None else acc + term
    return acc


def _window_sums(win, tc, causal):
    sums = []
    cur, rows, step = win, tc + HALO, 1
    for _ in POOL_WINDOWS:
        rows -= 8
        if causal:
            cur = cur[8:8 + rows, :] + cur[8 - step:8 - step + rows, :]
            sums.append(cur[rows - tc:rows, :])
        else:
            cur = cur[0:rows, :] + cur[step:step + rows, :]
            sums.append(cur[0:tc, :])
        step *= 2
    return sums


def _pick(vals, g):
    out = vals[-1]
    for i in range(len(vals) - 2, -1, -1):
        out = jnp.where(g == i, vals[i], out)
    return out


def _pool_count(s, tc, g):
    t1 = (lax.broadcasted_iota(jnp.int32, (tc, 1), 0) + (s + 1)).astype(F32)
    width = _pick([float(w) for w in POOL_WINDOWS], g)
    return jnp.minimum(t1, width)


def pool_fwd(name, z, pool_w, pool_scale, d_pool, y_width):
    t = z.shape[0]
    ng, pg = pool_w.shape[0], pool_w.shape[1]

    def body(u_ref, w_ref, s_ref, pooled_ref, y_ref, pad):
        g = pl.program_id(0)
        pad[pl.ds(0, HALO), :] = jnp.zeros((HALO, pg), F32)

        def fill(s, tc):
            pad[pl.ds(HALO + s, tc), :] = u_ref[pl.ds(s, tc), :].astype(F32)

        def chunk(s, tc):
            win = pad[pl.ds(s, tc + HALO), :]
            total = _pick(_window_sums(win, tc, causal=True), g)
            pooled = total / _pool_count(s, tc, g) - win[HALO:HALO + tc, :]
            pooled_ref[pl.ds(s, tc), :] = pooled.astype(BF16)

        _chunks(t, fill)
        _chunks(t, chunk)
        mixed = jnp.dot(pooled_ref[...], w_ref[...], preferred_element_type=F32)
        y_ref[...] = (mixed * s_ref[...]).astype(BF16)

    col = pl.BlockSpec((t, pg), lambda g: (0, g))
    return pl.pallas_call(
        body, name=name, grid=(ng,),
        in_specs=[col, pl.BlockSpec((None, pg, pg), lambda g: (g, 0, 0)), pl.BlockSpec((1, pg), lambda g: (0, g))],
        out_specs=[col, col],
        out_shape=[jax.ShapeDtypeStruct((t, d_pool), BF16), jax.ShapeDtypeStruct((t, y_width), BF16)],
        scratch_shapes=[pltpu.VMEM((t + HALO, pg), F32)],
        compiler_params=_params(("parallel",)),
    )(z, pool_w, pool_scale)


def pool_bwd(name, pooled, dy, pool_w, pool_scale, dz):
    t, d_pool = pooled.shape
    ng, pg = pool_w.shape[0], pool_w.shape[1]

    def body(p_ref, dy_ref, w_ref, s_ref, dz_ref, du_ref, dw_ref, ds_ref, pad):
        g = pl.program_id(0)
        w = w_ref[...]
        dyv = dy_ref[...].astype(F32)
        mixed = jnp.dot(p_ref[...], w, preferred_element_type=F32)
        ds_ref[...] = jnp.sum(dyv * mixed, axis=0, keepdims=True)
        dmixed = (dyv * s_ref[...]).astype(BF16)
        dw_ref[...] = lax.dot_general(p_ref[...], dmixed, TN, preferred_element_type=F32)
        pad[...] = jnp.zeros((t + HALO, pg), F32)
        pad[pl.ds(0, t), :] = lax.dot_general(dmixed, w, NT, preferred_element_type=F32)

        def scale(s, tc):
            pad[pl.ds(s, tc), :] = pad[pl.ds(s, tc), :] / _pool_count(s, tc, g)

        def chunk(s, tc):
            win = pad[pl.ds(s, tc + HALO), :]
            total = _pick(_window_sums(win, tc, causal=False), g)
            du_ref[pl.ds(s, tc), :] = (total - win[0:tc, :] * _pool_count(s, tc, g)).astype(BF16)

        _chunks(t, scale)
        _chunks(t, chunk)

    col = pl.BlockSpec((t, pg), lambda g: (0, g))
    vec = pl.BlockSpec((1, pg), lambda g: (0, g))
    mat = pl.BlockSpec((None, pg, pg), lambda g: (g, 0, 0))
    return pl.pallas_call(
        body, name=name, grid=(ng,),
        in_specs=[col, col, mat, vec, ANY], out_specs=[col, mat, vec],
        out_shape=[jax.ShapeDtypeStruct(dz.shape, dz.dtype), jax.ShapeDtypeStruct((ng, pg, pg), F32),
                   jax.ShapeDtypeStruct((1, d_pool), F32)],
        input_output_aliases={4: 0},
        scratch_shapes=[pltpu.VMEM((t + HALO, pg), F32)],
        compiler_params=_params(("parallel",)),
    )(pooled, dy, pool_w, pool_scale, dz)


def conv_fwd(name, z, conv_w, conv_b, d_pool, d_conv):
    t = z.shape[0]
    kw = conv_w.shape[0]
    tc_ch = _tile(d_conv, CHANNEL_TILE)
    v0, g0 = d_pool // tc_ch, (d_pool + d_conv) // tc_ch

    def body(v_ref, g_ref, w_ref, b_ref, c_ref, pad):
        pad[pl.ds(0, HALO), :] = jnp.zeros((HALO, tc_ch), F32)

        def fill(s, tc):
            pad[pl.ds(HALO + s, tc), :] = v_ref[pl.ds(s, tc), :].astype(F32) * jax.nn.sigmoid(g_ref[pl.ds(s, tc), :].astype(F32))

        def chunk(s, tc):
            win = pad[pl.ds(s, tc + HALO), :]
            c_ref[pl.ds(s, tc), :] = _taps(win, w_ref, [HALO - (kw - 1) + k for k in range(kw)], tc) + b_ref[...]

        _chunks(t, fill)
        _chunks(t, chunk)

    return pl.pallas_call(
        body, name=name, grid=(d_conv // tc_ch,),
        in_specs=[pl.BlockSpec((t, tc_ch), lambda j: (0, v0 + j)), pl.BlockSpec((t, tc_ch), lambda j: (0, g0 + j)),
                  pl.BlockSpec((kw, tc_ch), lambda j: (0, j)), pl.BlockSpec((1, tc_ch), lambda j: (0, j))],
        out_specs=pl.BlockSpec((t, tc_ch), lambda j: (0, j)),
        out_shape=jax.ShapeDtypeStruct((t, d_conv), F32),
        scratch_shapes=[pltpu.VMEM((t + HALO, tc_ch), F32)],
        compiler_params=_params(("parallel",)),
    )(z, z, conv_w, conv_b)


def conv_bwd(name, z, dc, conv_w, d_pool, d_conv):
    t = z.shape[0]
    kw = conv_w.shape[0]
    tc_ch = _tile(d_conv, CHANNEL_TILE)
    v0, g0 = d_pool // tc_ch, (d_pool + d_conv) // tc_ch

    def body(v_ref, g_ref, dc_ref, w_ref, dz_ref, dw_ref, db_ref, pad_a, pad_dc, acc_w, acc_b, tiles, sems):
        j = pl.program_id(0)
        dv_ref, dg_ref = tiles.at[0], tiles.at[1]
        writes = [pltpu.make_async_copy(tiles.at[p], dz_ref.at[:, pl.ds((first + j) * tc_ch, tc_ch)], sems.at[p])
                  for p, first in enumerate([v0, g0])]

        def wait_writes():
            for cp in writes:
                cp.wait()

        pad_a[pl.ds(0, HALO), :] = jnp.zeros((HALO, tc_ch), F32)
        pad_dc[pl.ds(t, HALO), :] = jnp.zeros((HALO, tc_ch), F32)
        acc_w[...] = jnp.zeros_like(acc_w)
        acc_b[...] = jnp.zeros_like(acc_b)

        def fill(s, tc):
            pad_a[pl.ds(HALO + s, tc), :] = v_ref[pl.ds(s, tc), :].astype(F32) * jax.nn.sigmoid(g_ref[pl.ds(s, tc), :].astype(F32))
            pad_dc[pl.ds(s, tc), :] = dc_ref[pl.ds(s, tc), :]

        def chunk(s, tc):
            dcv = pad_dc[pl.ds(s, tc), :]
            win_a = pad_a[pl.ds(s, tc + HALO), :]
            for k, rows in _shifted(win_a, [HALO - (kw - 1) + k for k in range(kw)], tc):
                acc_w[pl.ds(8 * k, 8), :] += _fold8(dcv * rows)
            acc_b[...] += _fold8(dcv)
            da = _taps(pad_dc[pl.ds(s, tc + HALO), :], w_ref, list(range(kw)), tc, flip=True)
            vv = v_ref[pl.ds(s, tc), :].astype(F32)
            sg = jax.nn.sigmoid(g_ref[pl.ds(s, tc), :].astype(F32))
            dv_ref[pl.ds(s, tc), :] = (da * sg).astype(BF16)
            dg_ref[pl.ds(s, tc), :] = (da * vv * sg * (1.0 - sg)).astype(BF16)

        _chunks(t, fill)
        pl.when(j > 0)(wait_writes)
        _chunks(t, chunk)
        for cp in writes:
            cp.start()
        pl.when(j == n_tiles - 1)(wait_writes)
        for k in range(kw):
            dw_ref[k:k + 1, :] = jnp.sum(acc_w[pl.ds(8 * k, 8), :], axis=0, keepdims=True)
        db_ref[...] = jnp.sum(acc_b[...], axis=0, keepdims=True)

    n_tiles = d_conv // tc_ch
    return pl.pallas_call(
        body, name=name, grid=(n_tiles,),
        in_specs=[pl.BlockSpec((t, tc_ch), lambda j: (0, v0 + j)), pl.BlockSpec((t, tc_ch), lambda j: (0, g0 + j)),
                  pl.BlockSpec((t, tc_ch), lambda j: (0, j)), pl.BlockSpec((kw, tc_ch), lambda j: (0, j))],
        out_specs=[ANY, pl.BlockSpec((kw, tc_ch), lambda j: (0, j)), pl.BlockSpec((1, tc_ch), lambda j: (0, j))],
        out_shape=[jax.ShapeDtypeStruct((t, d_pool + 2 * d_conv), BF16),
                   jax.ShapeDtypeStruct((kw, d_conv), F32), jax.ShapeDtypeStruct((1, d_conv), F32)],
        scratch_shapes=[pltpu.VMEM((t + HALO, tc_ch), F32), pltpu.VMEM((t + HALO, tc_ch), F32),
                        pltpu.VMEM((8 * kw, tc_ch), F32), pltpu.VMEM((8, tc_ch), F32),
                        pltpu.VMEM((2, t, tc_ch), BF16), pltpu.SemaphoreType.DMA((2,))],
        compiler_params=_params(("arbitrary",)),
    )(z, z, dc, conv_w)


def short_fwd(name, z, conv_w, d_short):
    t = z.shape[0]
    kw = conv_w.shape[0]
    tc_ch = _tile(d_short, CHANNEL_TILE)
    nt = d_short // tc_ch

    def body(b_ref, c_ref, u_ref, w_ref, y_ref, pad):
        pad[pl.ds(0, HALO), :] = jnp.zeros((HALO, tc_ch), F32)

        def fill(s, tc):
            pad[pl.ds(HALO + s, tc), :] = c_ref[pl.ds(s, tc), :].astype(F32) * u_ref[pl.ds(s, tc), :].astype(F32)

        def chunk(s, tc):
            win = pad[pl.ds(s, tc + HALO), :]
            cq = _taps(win, w_ref, [HALO - (kw - 1) + k for k in range(kw)], tc)
            y_ref[pl.ds(s, tc), :] = (b_ref[pl.ds(s, tc), :].astype(F32) * cq).astype(BF16)

        _chunks(t, fill)
        _chunks(t, chunk)

    return pl.pallas_call(
        body, name=name, grid=(nt,),
        in_specs=[pl.BlockSpec((t, tc_ch), lambda j: (0, j)), pl.BlockSpec((t, tc_ch), lambda j: (0, nt + j)),
                  pl.BlockSpec((t, tc_ch), lambda j: (0, 2 * nt + j)), pl.BlockSpec((kw, tc_ch), lambda j: (0, j))],
        out_specs=pl.BlockSpec((t, tc_ch), lambda j: (0, j)),
        out_shape=jax.ShapeDtypeStruct((t, d_short), BF16),
        scratch_shapes=[pltpu.VMEM((t + HALO, tc_ch), F32)],
        compiler_params=_params(("parallel",)),
    )(z, z, z, conv_w)


def short_bwd(name, z, dy, conv_w, d_short):
    t = z.shape[0]
    kw = conv_w.shape[0]
    tc_ch = _tile(d_short, CHANNEL_TILE)
    nt = d_short // tc_ch

    def body(b_ref, c_ref, u_ref, dy_ref, w_ref, dz_ref, dw_ref, pad_q, pad_dcq, acc_w, tiles, sems):
        j = pl.program_id(0)
        db_ref, dcg_ref, du_ref = tiles.at[0], tiles.at[1], tiles.at[2]
        writes = [pltpu.make_async_copy(tiles.at[p], dz_ref.at[:, pl.ds((p * nt + j) * tc_ch, tc_ch)], sems.at[p])
                  for p in range(3)]

        def wait_writes():
            for cp in writes:
                cp.wait()

        pad_q[pl.ds(0, HALO), :] = jnp.zeros((HALO, tc_ch), F32)
        pad_dcq[pl.ds(t, HALO), :] = jnp.zeros((HALO, tc_ch), F32)
        acc_w[...] = jnp.zeros_like(acc_w)

        def fill(s, tc):
            rows = pl.ds(s, tc)
            pad_q[pl.ds(HALO + s, tc), :] = c_ref[rows, :].astype(F32) * u_ref[rows, :].astype(F32)
            pad_dcq[rows, :] = dy_ref[rows, :].astype(F32) * b_ref[rows, :].astype(F32)

        def chunk(s, tc):
            rows = pl.ds(s, tc)
            win_q = pad_q[pl.ds(s, tc + HALO), :]
            dcq = pad_dcq[rows, :]
            cq = None
            for k in range(kw):
                off = HALO - (kw - 1) + k
                shifted = win_q[off:off + tc, :]
                acc_w[pl.ds(8 * k, 8), :] += _fold8(dcq * shifted)
                term = w_ref[k:k + 1, :] * shifted
                cq = term if cq is None else cq + term
            db_ref[rows, :] = (dy_ref[rows, :].astype(F32) * cq).astype(BF16)
            win_d = pad_dcq[pl.ds(s, tc + HALO), :]
            dq = None
            for j in range(kw):
                term = w_ref[kw - 1 - j:kw - j, :] * win_d[j:j + tc, :]
                dq = term if dq is None else dq + term
            dcg_ref[rows, :] = (dq * u_ref[rows, :].astype(F32)).astype(BF16)
            du_ref[rows, :] = (dq * c_ref[rows, :].astype(F32)).astype(BF16)

        _chunks(t, fill)
        pl.when(j > 0)(wait_writes)
        _chunks(t, chunk)
        for cp in writes:
            cp.start()
        pl.when(j == nt - 1)(wait_writes)
        for k in range(kw):
            dw_ref[k:k + 1, :] = jnp.sum(acc_w[pl.ds(8 * k, 8), :], axis=0, keepdims=True)

    zspec = [pl.BlockSpec((t, tc_ch), lambda j, o=o: (0, o * nt + j)) for o in range(3)]
    return pl.pallas_call(
        body, name=name, grid=(nt,),
        in_specs=[*zspec, pl.BlockSpec((t, tc_ch), lambda j: (0, j)), pl.BlockSpec((kw, tc_ch), lambda j: (0, j))],
        out_specs=[ANY, pl.BlockSpec((kw, tc_ch), lambda j: (0, j))],
        out_shape=[jax.ShapeDtypeStruct((t, 3 * d_short), BF16), jax.ShapeDtypeStruct((kw, d_short), F32)],
        scratch_shapes=[pltpu.VMEM((t + HALO, tc_ch), F32), pltpu.VMEM((t + HALO, tc_ch), F32),
                        pltpu.VMEM((8 * kw, tc_ch), F32), pltpu.VMEM((3, t, tc_ch), BF16),
                        pltpu.SemaphoreType.DMA((3,))],
        compiler_params=_params(("arbitrary",)),
    )(z, z, z, dy, conv_w)


def adamw(name, w, m, v, contributions):
    r, c = w.shape
    nc = len(contributions)
    n_slots = contributions[0].shape[0]
    tr = 256 if c <= 1024 else 128
    if any(a.shape[1] % tr for a in contributions):
        assert nc == 1
        tr = r
    tiles = [a.shape[1] // tr for a in contributions]
    first = [sum(tiles[:j]) for j in range(nc)]

    def body(w_ref, m_ref, v_ref, *rest):
        g_refs, (grad_ref, delta_ref, nm_ref, nv_ref) = rest[:nc], rest[nc:]
        i = pl.program_id(0)
        g = None
        for j, g_ref in enumerate(g_refs):
            s = g_ref[0].astype(F32)
            for slot in range(1, n_slots):
                s = s + g_ref[slot].astype(F32)
            g = s if g is None else jnp.where(i >= first[j], s, g)
        nm = ADAM_B1 * m_ref[...] + (1.0 - ADAM_B1) * g
        nv = ADAM_B2 * v_ref[...] + (1.0 - ADAM_B2) * (g * g)
        m_hat = nm / (1.0 - ADAM_B1 ** ADAM_STEP)
        v_hat = nv / (1.0 - ADAM_B2 ** ADAM_STEP)
        grad_ref[...] = g
        delta_ref[...] = -ADAM_LR * (m_hat / (jnp.sqrt(v_hat) + ADAM_EPS) + ADAM_WD * w_ref[...])
        nm_ref[...] = nm
        nv_ref[...] = nv

    blk = pl.BlockSpec((tr, c), lambda i: (i, 0))
    g_specs = [pl.BlockSpec((n_slots, tr, c), lambda i, j=j: (0, jnp.clip(i - first[j], 0, tiles[j] - 1), 0))
               for j in range(nc)]
    return pl.pallas_call(
        body, name=name, grid=(r // tr,),
        in_specs=[blk, blk, blk, *g_specs],
        out_specs=[blk] * 4, out_shape=[jax.ShapeDtypeStruct((r, c), F32)] * 4,
        compiler_params=_params(("parallel",)),
    )(w, m, v, *contributions)


def _pad_rows(a, rows):
    return jnp.pad(a, ((0, rows - a.shape[0]), (0, 0)))


def kernel(x, mix_pre_g, mix_post_g, ffn_pre_g, ffn_post_g, ab_w_in, pool_w, pool_scale, conv_w, conv_b, conv_ln_g, conv_ln_b, ab_w_out, sc_w_in, sc_conv_w, sc_w_out, ffn_w1, ffn_w2, loss_target, m_mix_pre_g, m_mix_post_g, m_ffn_pre_g, m_ffn_post_g, m_ab_w_in, m_pool_w, m_pool_scale, m_conv_w, m_conv_b, m_conv_ln_g, m_conv_ln_b, m_ab_w_out, m_sc_w_in, m_sc_conv_w, m_sc_w_out, m_ffn_w1, m_ffn_w2, v_mix_pre_g, v_mix_post_g, v_ffn_pre_g, v_ffn_post_g, v_ab_w_in, v_pool_w, v_pool_scale, v_conv_w, v_conv_b, v_conv_ln_g, v_conv_ln_b, v_ab_w_out, v_sc_w_in, v_sc_conv_w, v_sc_w_out, v_ffn_w1, v_ffn_w2):
    t, d = x.shape[1], x.shape[2]
    d_pool = pool_scale.shape[1]
    d_conv = conv_b.shape[1]
    d_short = d
    ng, pg = pool_w.shape[1], pool_w.shape[3]
    kw, ks = conv_w.shape[1], sc_conv_w.shape[1]
    nb_ab, nb_sc, nb_ff = ab_w_in.shape[2], sc_w_in.shape[2], ffn_w1.shape[2]

    xs = x[0]
    target = loss_target[0]

    lanes = min(128, d_conv // N_DEV)
    small_rows = [kw * (d_conv // N_DEV) // lanes, ks * (d_short // N_DEV) // lanes, ng * (pg // N_DEV) * pg // lanes]
    small_total = -(-sum(small_rows) // 8) * 8
    r0, r1, r2 = small_rows[0], small_rows[0] + small_rows[1], sum(small_rows)

    def pack_small(a_conv, a_sconv, a_pool):
        parts = [a_conv[0].reshape(-1, lanes), a_sconv[0].reshape(-1, lanes), a_pool[0].reshape(-1, lanes)]
        return _pad_rows(jnp.concatenate(parts, axis=0), small_total)

    shards = {
        "ab_in": (ab_w_in, 0, BF16), "small": (pack_small(conv_w, sc_conv_w, pool_w)[None], 0, F32),
        "ab_out": (ab_w_out, 0, BF16), "ff1_0": (ffn_w1, 0, BF16), "ff2_0": (ffn_w2, 0, BF16),
        "sc_in": (sc_w_in, 0, BF16), "sc_out": (sc_w_out, 0, BF16),
        "ff1_1": (ffn_w1, 1, BF16), "ff2_1": (ffn_w2, 1, BF16)}
    direct = ["ab_in", "small", "ab_out"]
    zones = {nm: place_shard("place_" + nm, *shards[nm]) for nm in direct}
    started, token = copies_start("gather_start", [[zones[nm]] for nm in direct], _first_hop, 4)
    started = dict(zip(direct, started))
    ring = {}
    for nm in ["ff1_0", "ff2_0"]:
        zones[nm] = place_shard("place_" + nm, *shards[nm], deps=[token])
        (ring[nm],), token = copies_start("ring_start_" + nm, [[zones[nm]]], _ring_hop1, 3, deps=[token])
    for nm in shards:
        if nm not in zones:
            zones[nm] = place_shard("place_" + nm, *shards[nm], deps=[token])

    ties = [0]

    def after(v, *deps):
        ties[0] += 1
        return tie(f"tie_{ties[0]}", v, *deps)

    def fetch_begin(nm, dep):
        (zone,) = copies_wait("gather_wait_" + nm, started[nm], _first_hop, dep)
        (hop,), tok = copies_start("forward_start_" + nm, [[zone]], _second_hop, 3)
        return hop, tok

    def fetch_end(nm, hop, dep):
        return copies_wait("forward_wait_" + nm, hop, _second_hop, dep)[0]

    def ring_step(tag, dep, second=None, first=None, third=None):
        names, groups, hops, counts = [], [], [], []
        if second is not None:
            groups.append(copies_wait("ring1_wait_" + second, ring[second], _ring_hop1, dep))
            names, hops, counts = names + [second], hops + [_ring_hop2], counts + [4]
        if first is not None:
            groups.append([zones[first]])
            names, hops, counts = names + [first], hops + [_ring_hop1], counts + [3]
        if third is not None:
            groups.append(copies_wait("ring2_wait_" + third, ring[third], _ring_hop2, dep))
            names, hops, counts = names + [third], hops + [_ring_hop3], counts + [1]
        begun, tok = copies_start("ring_start_" + tag, groups, hops, counts, deps=[dep])
        ring.update(zip(names, begun))
        return tok

    def ring_done(nm, dep):
        return copies_wait("ring3_wait_" + nm, ring[nm], _ring_hop3, dep)[0]

    relu = lambda r: jnp.maximum(r, 0.0)
    square = lambda a: a * a
    relu2_bwd = lambda r, a: r * (2.0 * a.astype(F32))

    def row(vec, l):
        return vec[l:l + 1]

    hop_small, _ = fetch_begin("small", token)
    hop_ab_in, tok = fetch_begin("ab_in", token)
    w_small = fetch_end("small", hop_small, tok)
    w_ab_in = fetch_end("ab_in", hop_ab_in, tok)
    w_conv = w_small[:, :r0].reshape(N_DEV, kw, -1).transpose(1, 0, 2).reshape(kw, d_conv)
    w_sconv = w_small[:, r0:r1].reshape(N_DEV, ks, -1).transpose(1, 0, 2).reshape(ks, d_short)
    w_pool = w_small[:, r1:r2].reshape(N_DEV, ng, -1, pg).transpose(1, 0, 2, 3).reshape(ng, pg, pg).astype(BF16)
    h0 = norm_pre("norm_pre", xs, after(row(mix_pre_g, 0), token))
    z0 = mm_nn_blocked("ab_in", h0, w_ab_in, out_dtype=BF16)
    hop, tok = fetch_begin("ab_out", z0)
    z0 = after(z0, tok)
    pooled, y0 = pool_fwd("pool_fwd", z0, w_pool, pool_scale, d_pool, d_pool + d_conv)
    cv = conv_fwd("conv_fwd", z0, w_conv, conv_b, d_pool, d_conv)
    y0 = ln_silu("ln_silu", cv, conv_ln_g, conv_ln_b, y0, d_pool // d_conv)
    w_ab_out = fetch_end("ab_out", hop, y0)
    tok = ring_step("a", w_ab_out, second="ff1_0", first="sc_in")
    y0 = after(y0, tok)
    m0 = mm_nn("ab_out", y0, w_ab_out.reshape(d_pool + d_conv, d), out_dtype=F32)
    x1, h1 = post_pre("post_pre_0", xs, m0, row(mix_post_g, 0), row(ffn_pre_g, 0))
    tok = ring_step("b", h1, second="ff2_0", first="sc_out", third="ff1_0")
    w_ff1_0 = ring_done("ff1_0", tok)
    a0 = mm_nn_blocked("ffn0_up", h1, w_ff1_0, out_dtype=BF16, epilogue=relu)
    tok = ring_step("c", a0, second="sc_in", first="ff1_1", third="ff2_0")
    w_ff2_0 = ring_done("ff2_0", tok).reshape(-1, d)
    f0 = mm_nn("ffn0_down", a0, w_ff2_0, out_dtype=F32, tk=2048, lhs_fn=square)
    tok = ring_step("d", f0, second="sc_out", first="ff2_1", third="sc_in")
    f0 = after(f0, tok)
    x2, h2 = post_pre("post_pre_1", x1, f0, row(ffn_post_g, 0), row(mix_pre_g, 1))
    w_sc_in = ring_done("sc_in", h2)
    z1 = mm_nn_blocked("sc_in", h2, w_sc_in, out_dtype=BF16)
    tok = ring_step("e", z1, second="ff1_1", third="sc_out")
    z1 = after(z1, tok)
    y1 = short_fwd("short_fwd", z1, w_sconv, d_short)
    w_sc_out = ring_done("sc_out", y1).reshape(d_short, d)
    m1 = mm_nn("sc_out", y1, w_sc_out, out_dtype=F32)
    tok = ring_step("f", m1, second="ff2_1")
    m1 = after(m1, tok)
    x3, h3 = post_pre("post_pre_2", x2, m1, row(mix_post_g, 1), row(ffn_pre_g, 1))
    tok = ring_step("g", h3, third="ff1_1")
    w_ff1_1 = ring_done("ff1_1", tok)
    a1 = mm_nn_blocked("ffn1_up", h3, w_ff1_1, out_dtype=BF16, epilogue=relu)
    tok = ring_step("h", a1, third="ff2_1")
    w_ff2_1 = ring_done("ff2_1", tok).reshape(-1, d)
    f1 = mm_nn("ffn1_down", a1, w_ff2_1, out_dtype=F32, tk=2048, lhs_fn=square)
    dx4, df1, loss_part, dg_ffn_post1 = post_loss("post_loss", x3, f1, row(ffn_post_g, 1), target)
    loss = lax.psum(jnp.sum(loss_part) * (0.5 / d), ("x", "y", "c"))

    def reduce_begin(tag, g):
        zone = lax.empty((N_CHIP,) + g.shape[1:], g.dtype)
        (hop,), tok = copies_start("pair_start_" + tag, [[g, zone]], _pair_hop, N_CHIP)
        return hop, tok

    def reduce_middle(tag, hop, dep):
        g, from_sibling = copies_wait("pair_wait_" + tag, hop, _pair_hop, dep)
        pair_sum, zone = pair_add("pair_add_" + tag, g, from_sibling)
        (hop2,), tok = copies_start("chips_start_" + tag, [[pair_sum, zone]], _chip_hop, 3)
        return hop2, tok

    def reduce_wait(tag, hop, dep):
        g, from_sibling = copies_wait("pair_wait_" + tag, hop, _pair_hop, dep)
        return lambda n_steps: pair_add_side(g, from_sibling, n_steps)

    def reduce_send(tag, pair_sum, zone):
        (hop2,), tok = copies_start("chips_start_" + tag, [[pair_sum, zone]], _chip_hop, 3)
        return hop2, tok

    def reduce_end(tag, hop2, dep):
        return copies_wait("chips_wait_" + tag, hop2, _chip_hop, dep)[1]

    dw = mm_tn("ffn1_dw2", a1, df1, out_dtype=BF16, lhs_fn=square)
    red_ff2_1, tok = reduce_begin("ff2_1", dw.reshape(N_DEV, -1, d))
    df1 = after(df1, tok)
    dpre = mm_nt("ffn1_da", df1, w_ff2_1, out_dtype=BF16, extra=a1, epilogue=relu2_bwd)
    dw = mm_tn_blocked("ffn1_dw1", h3, dpre, nb_ff, out_dtype=BF16)
    red_ff1_1, tok = reduce_begin("ff1_1", dw)
    dpre = after(dpre, tok)
    dh3 = mm_nt_blocked("ffn1_dh", dpre, w_ff1_1, out_dtype=BF16)
    dx3, dm1, dg_ffn_pre1, dg_mix_post1 = bwd_pre_post("bwd_3", dx4, x3, row(ffn_pre_g, 1), dh3, m1, row(mix_post_g, 1))

    dw = mm_tn("sc_dwout", y1, dm1, out_dtype=BF16)
    red_sc_out, tok = reduce_begin("sc_out", dw.reshape(N_DEV, -1, d))
    dm1 = after(dm1, tok)
    dy1 = mm_nt("sc_dy", dm1, w_sc_out, out_dtype=BF16)
    dz1, dw_sconv = short_bwd("short_bwd", z1, dy1, w_sconv, d_short)
    dw, pair_sum, zone = mm_tn_blocked("sc_dwin", h2, dz1, nb_sc, out_dtype=BF16,
                                       side_of=reduce_wait("ff2_1", red_ff2_1, dz1))
    red_ff2_1, tok2 = reduce_send("ff2_1", pair_sum, zone)
    red_sc_in, tok = reduce_begin("sc_in", dw)
    dz1 = after(dz1, tok, tok2)
    dh2, pair_sum, zone = mm_nt_blocked("sc_dh", dz1, w_sc_in, out_dtype=BF16,
                                        side_of=reduce_wait("ff1_1", red_ff1_1, dz1))
    red_ff1_1, tok = reduce_send("ff1_1", pair_sum, zone)
    dh2 = after(dh2, tok)
    dx2, df0, dg_mix_pre1, dg_ffn_post0 = bwd_pre_post("bwd_2", dx3, x2, row(mix_pre_g, 1), dh2, f0, row(ffn_post_g, 0))

    dw, pair_sum, zone = mm_tn("ffn0_dw2", a0, df0, out_dtype=BF16, lhs_fn=square,
                               side_of=reduce_wait("sc_out", red_sc_out, df0))
    red_sc_out, tok2 = reduce_send("sc_out", pair_sum, zone)
    red_ff2_0, tok = reduce_begin("ff2_0", dw.reshape(N_DEV, -1, d))
    df0 = after(df0, tok, tok2)
    dpre, pair_sum, zone = mm_nt("ffn0_da", df0, w_ff2_0, out_dtype=BF16, extra=a0, epilogue=relu2_bwd,
                                 side_of=reduce_wait("sc_in", red_sc_in, df0))
    red_sc_in, tok = reduce_send("sc_in", pair_sum, zone)
    dpre = after(dpre, tok)
    dw = mm_tn_blocked("ffn0_dw1", h1, dpre, nb_ff, out_dtype=BF16)
    red_ff1_0, tok = reduce_begin("ff1_0", dw)
    dpre = after(dpre, tok)
    dh1, pair_sum, zone = mm_nt_blocked("ffn0_dh", dpre, w_ff1_0, out_dtype=BF16,
                                        side_of=reduce_wait("ff2_0", red_ff2_0, dpre))
    red_ff2_0, tok = reduce_send("ff2_0", pair_sum, zone)
    dh1 = after(dh1, tok)
    dx1, dm0, dg_ffn_pre0, dg_mix_post0 = bwd_pre_post("bwd_1", dx2, x1, row(ffn_pre_g, 0), dh1, m0, row(mix_post_g, 0))

    dw = mm_tn("ab_dwout", y0, dm0, out_dtype=BF16)
    red_ab_out, tok = reduce_begin("ab_out", dw.reshape(N_DEV, -1, d))
    dm0 = after(dm0, tok)
    dy0 = mm_nt("ab_dy", dm0, w_ab_out.reshape(d_pool + d_conv, d), out_dtype=BF16)
    dcv, dg_ln_g, dg_ln_b = ln_silu_bwd("ln_silu_bwd", cv, conv_ln_g, conv_ln_b, dy0, d_pool // d_conv)
    dz0, dw_conv, dg_conv_b = conv_bwd("conv_bwd", z0, dcv, w_conv, d_pool, d_conv)
    dz0, dw_pool, dg_pool_scale = pool_bwd("pool_bwd", pooled, dy0, w_pool, pool_scale, dz0)
    small_parts = [
        dw_conv.reshape(kw, N_DEV, -1).transpose(1, 0, 2).reshape(N_DEV, -1, lanes),
        dw_sconv.reshape(ks, N_DEV, -1).transpose(1, 0, 2).reshape(N_DEV, -1, lanes),
        dw_pool.reshape(ng, N_DEV, pg // N_DEV, pg).transpose(1, 0, 2, 3).reshape(N_DEV, -1, lanes),
    ]
    small = jnp.pad(jnp.concatenate(small_parts, axis=1), ((0, 0), (0, small_total - r2), (0, 0)))
    red_small, tok = reduce_begin("small", small)
    dz0 = after(dz0, tok)
    dw, pair_sum, zone = mm_tn_blocked("ab_dwin", h0, dz0, nb_ab, out_dtype=BF16,
                                       side_of=reduce_wait("ff1_0", red_ff1_0, dz0))
    red_ff1_0, tok2 = reduce_send("ff1_0", pair_sum, zone)
    red_ab_in, tok = reduce_begin("ab_in", dw)
    dz0 = after(dz0, tok, tok2)
    dh0, pair_sum, zone = mm_nt_blocked("ab_dh", dz0, w_ab_in, out_dtype=BF16,
                                        side_of=reduce_wait("ab_out", red_ab_out, dz0))
    red_ab_out, tok = reduce_send("ab_out", pair_sum, zone)
    red_small, tok2 = reduce_middle("small", red_small, dh0)
    dh0 = after(dh0, tok, tok2)
    grad_x, dg_mix_pre0 = bwd_pre_final("bwd_0", dx1, xs, row(mix_pre_g, 0), dh0)
    red_ab_in, tok = reduce_middle("ab_in", red_ab_in, grad_x)

    fold = lambda a: jnp.sum(a, axis=0, keepdims=True)
    rep_rows = [fold(dg_mix_pre0), fold(dg_mix_pre1), fold(dg_mix_post0), fold(dg_mix_post1),
                fold(dg_ffn_pre0), fold(dg_ffn_pre1), fold(dg_ffn_post0), fold(dg_ffn_post1)]
    tail = jnp.concatenate([dg_pool_scale, dg_conv_b, fold(dg_ln_g), fold(dg_ln_b)], axis=1).reshape(-1, d)
    rep = _pad_rows(jnp.concatenate(rep_rows + [tail], axis=0), 16)
    (rep_hop,), _ = copies_start("rep_start", [[place_shard("place_rep", rep[None], 0, F32)]], _first_hop, 4)

    def pack_rep(a_mix_pre, a_mix_post, a_ffn_pre, a_ffn_post, a_scale, a_b, a_g, a_lb):
        tail_ = jnp.concatenate([a_scale, a_b, a_g, a_lb], axis=1).reshape(-1, d)
        return _pad_rows(jnp.concatenate([a_mix_pre, a_mix_post, a_ffn_pre, a_ffn_post, tail_], axis=0), 16)

    def upd(name, w, m, v, contribs):
        shape = w.shape
        flat2 = lambda a: a.reshape(-1, shape[-1])
        outs = adamw(name, flat2(w), flat2(m), flat2(v), contribs)
        return [o.reshape(shape) for o in outs]

    g_ff2 = [reduce_end("ff2_0", red_ff2_0, tok), reduce_end("ff2_1", red_ff2_1, tok)]
    o_ff2 = upd("adam_ffn_w2", ffn_w2, m_ffn_w2, v_ffn_w2, g_ff2)
    (rep_zone,) = copies_wait("rep_wait", rep_hop, _first_hop, o_ff2[0])
    (rep_hop,), _ = copies_start("rep_forward_start", [[rep_zone]], _second_hop, 3)
    g_ff1 = [reduce_end("ff1_0", red_ff1_0, o_ff2[0]), reduce_end("ff1_1", red_ff1_1, o_ff2[0])]
    o_ff1 = upd("adam_ffn_w1", ffn_w1, m_ffn_w1, v_ffn_w1, g_ff1)
    (rep_all,) = copies_wait("rep_forward_wait", rep_hop, _second_hop, o_ff1[0])
    o_rep = adamw("adam_replicated",
                  pack_rep(mix_pre_g, mix_post_g, ffn_pre_g, ffn_post_g, pool_scale, conv_b, conv_ln_g, conv_ln_b),
                  pack_rep(m_mix_pre_g, m_mix_post_g, m_ffn_pre_g, m_ffn_post_g, m_pool_scale, m_conv_b, m_conv_ln_g, m_conv_ln_b),
                  pack_rep(v_mix_pre_g, v_mix_post_g, v_ffn_pre_g, v_ffn_post_g, v_pool_scale, v_conv_b, v_conv_ln_g, v_conv_ln_b),
                  [rep_all])
    o_sc_out = upd("adam_sc_out", sc_w_out, m_sc_w_out, v_sc_w_out, [reduce_end("sc_out", red_sc_out, o_ff1[0])])
    o_sc_in = upd("adam_sc_in", sc_w_in, m_sc_w_in, v_sc_w_in, [reduce_end("sc_in", red_sc_in, o_sc_out[0])])
    o_ab_out = upd("adam_ab_out", ab_w_out, m_ab_w_out, v_ab_w_out, [reduce_end("ab_out", red_ab_out, o_sc_in[0])])
    o_small = adamw("adam_small", pack_small(conv_w, sc_conv_w, pool_w), pack_small(m_conv_w, m_sc_conv_w, m_pool_w),
                    pack_small(v_conv_w, v_sc_conv_w, v_pool_w), [reduce_end("small", red_small, o_ab_out[0])])
    o_ab_in = upd("adam_ab_in", ab_w_in, m_ab_w_in, v_ab_w_in, [reduce_end("ab_in", red_ab_in, o_small[0])])

    def unpack_small(o):
        return o[:r0].reshape(conv_w.shape), o[r0:r1].reshape(sc_conv_w.shape), o[r1:r2].reshape(pool_w.shape)

    def unpack_rep(o):
        tail_ = o[8:8 + tail.shape[0]].reshape(1, -1)
        n1 = d_pool
        return dict(mix_pre_g=o[0:2], mix_post_g=o[2:4], ffn_pre_g=o[4:6], ffn_post_g=o[6:8],
                    pool_scale=tail_[:, :n1], conv_b=tail_[:, n1:n1 + d_conv],
                    conv_ln_g=tail_[:, n1 + d_conv:n1 + 2 * d_conv], conv_ln_b=tail_[:, n1 + 2 * d_conv:n1 + 3 * d_conv])

    results = []
    for kind in range(4):
        rep_o = unpack_rep(o_rep[kind])
        s_conv, s_sconv, s_pool = unpack_small(o_small[kind])
        results.append([
            rep_o["mix_pre_g"], rep_o["mix_post_g"], rep_o["ffn_pre_g"], rep_o["ffn_post_g"],
            o_ab_in[kind], s_pool, rep_o["pool_scale"], s_conv, rep_o["conv_b"], rep_o["conv_ln_g"], rep_o["conv_ln_b"],
            o_ab_out[kind], o_sc_in[kind], s_sconv, o_sc_out[kind], o_ff1[kind], o_ff2[kind]])

    return (loss, grad_x[None], *results[0], *results[1], *results[2], *results[3])
```

```python
import jax
import jax.numpy as jnp
from jax import lax
from jax.experimental import pallas as pl
from jax.experimental.pallas import tpu as pltpu

F32 = jnp.float32
BF16 = jnp.bfloat16
MESH = pl.DeviceIdType.MESH
ANY = pl.BlockSpec(memory_space=pl.ANY)

NORM_EPS = 1e-6
POOL_WINDOWS = (2, 4, 8, 16)
ADAM_LR = 0.001
ADAM_B1 = 0.9
ADAM_B2 = 0.999
ADAM_EPS = 1e-08
ADAM_WD = 0.01
ADAM_STEP = 10

N_DEV = 8
VMEM_LIMIT = 56 * 1024 * 1024
PAIR_ADD_BLOCK = 1 << 20
MATMUL_ROWS = 2048
ROW_TILE = 256
CHANNEL_TILE = 256
TIME_CHUNK = 64
HALO = 32

NN = (((1,), (0,)), ((), ()))
NT = (((1,), (1,)), ((), ()))
TN = (((0,), (0,)), ((), ()))


def _params(sem):
    return pltpu.CompilerParams(dimension_semantics=sem, vmem_limit_bytes=VMEM_LIMIT)


def _place():
    x, y, c = lax.axis_index("x"), lax.axis_index("y"), lax.axis_index("c")
    return x, y, c


def _slot(px, py, pc):
    return 4 * px + 2 * py + pc


HBM = pl.BlockSpec(memory_space=pltpu.HBM)
SEM = pl.BlockSpec(memory_space=pltpu.SEMAPHORE)
EFFECT = pltpu.SideEffectType.DATAFLOW_SIDE_EFFECTING
TOKEN = jax.ShapeDtypeStruct((8, 128), F32)


def _in_hbm(a):
    return pltpu.with_memory_space_constraint(a, pltpu.HBM)


CHIPS = [(0, 0), (0, 1), (1, 0), (1, 1)]
N_CHIP = len(CHIPS)


def _chip(px, py):
    return 2 * px + py


def _first_hop(bufs, sends, recvs, waiting):
    (land,) = bufs
    x, y, c = _place()
    me = _slot(x, y, c)
    peers = [(x, y, 1 - c), (1 - x, y, c), (x, 1 - y, c), (1 - x, 1 - y, c)]
    return [pltpu.make_async_remote_copy(
        src_ref=land.at[me], dst_ref=land.at[_slot(*p) if waiting else me],
        send_sem=sends.at[k], recv_sem=recvs.at[k], device_id=p, device_id_type=MESH) for k, p in enumerate(peers)]


def _second_hop(bufs, sends, recvs, waiting):
    (land,) = bufs
    x, y, c = _place()
    return [pltpu.make_async_remote_copy(
        src_ref=land.at[_slot(px, py, c)], dst_ref=land.at[_slot(px, py, 1 - c if waiting else c)],
        send_sem=sends.at[k], recv_sem=recvs.at[k], device_id=(x, y, 1 - c), device_id_type=MESH)
        for k, (px, py) in enumerate([(1 - x, y), (x, 1 - y), (1 - x, 1 - y)])]


def _ring_hop1(bufs, sends, recvs, waiting):
    (land,) = bufs
    x, y, c = _place()
    me = _slot(x, y, c)
    peers = [(1 - x, y, c), (x, 1 - y, c), (x, y, 1 - c)]
    return [pltpu.make_async_remote_copy(
        src_ref=land.at[me], dst_ref=land.at[_slot(*p) if waiting else me],
        send_sem=sends.at[k], recv_sem=recvs.at[k], device_id=p, device_id_type=MESH) for k, p in enumerate(peers)]


def _ring_hop2(bufs, sends, recvs, waiting):
    (land,) = bufs
    x, y, c = _place()
    half = land.shape[1] // 2
    first, second = pl.ds(0, half), pl.ds(half, half)
    nx, ny, diag = _slot(1 - x, y, c), _slot(x, 1 - y, c), _slot(1 - x, 1 - y, c)
    plan = [
        (land.at[ny, first], land.at[diag, first], (1 - x, y, c)),
        (land.at[nx, second], land.at[diag, second], (x, 1 - y, c)),
        (land.at[nx], land.at[_slot(1 - x, y, 1 - c)], (x, y, 1 - c)),
        (land.at[ny], land.at[_slot(x, 1 - y, 1 - c)], (x, y, 1 - c))]
    return [pltpu.make_async_remote_copy(
        src_ref=src, dst_ref=mine if waiting else src, send_sem=sends.at[k], recv_sem=recvs.at[k],
        device_id=to, device_id_type=MESH) for k, (src, mine, to) in enumerate(plan)]


def _ring_hop3(bufs, sends, recvs, waiting):
    (land,) = bufs
    x, y, c = _place()
    return [pltpu.make_async_remote_copy(
        src_ref=land.at[_slot(1 - x, 1 - y, c)], dst_ref=land.at[_slot(1 - x, 1 - y, 1 - c if waiting else c)],
        send_sem=sends.at[0], recv_sem=recvs.at[0], device_id=(x, y, 1 - c), device_id_type=MESH)]


def _pair_hop(bufs, sends, recvs, waiting):
    g, land = bufs
    x, y, c = _place()
    return [pltpu.make_async_remote_copy(
        src_ref=g.at[_slot(qx, qy, 1 - c)], dst_ref=land.at[q],
        send_sem=sends.at[q], recv_sem=recvs.at[q], device_id=(x, y, 1 - c), device_id_type=MESH)
        for q, (qx, qy) in enumerate(CHIPS)]


def _chip_hop(bufs, sends, recvs, waiting):
    p, land = bufs
    x, y, c = _place()
    return [pltpu.make_async_remote_copy(
        src_ref=p.at[_chip(px, py)], dst_ref=land.at[_chip(px, py) if waiting else _chip(x, y)],
        send_sem=sends.at[k], recv_sem=recvs.at[k], device_id=(px, py, c), device_id_type=MESH)
        for k, (px, py) in enumerate([(1 - x, y), (x, 1 - y), (1 - x, 1 - y)])]


def copies_start(name, groups, hop, n_copies, deps=()):
    flat = [b for grp in groups for b in grp]
    nb, ng = len(flat), len(groups)
    deps = list(deps)
    hops = list(hop) if isinstance(hop, (list, tuple)) else [hop] * ng
    counts = list(n_copies) if isinstance(n_copies, (list, tuple)) else [n_copies] * ng

    def body(*refs):
        ins, token = refs[:nb], refs[-1]
        sems = refs[nb + len(deps):nb + len(deps) + 2 * ng]
        i = 0
        for gi, grp in enumerate(groups):
            for cp in hops[gi](ins[i:i + len(grp)], sems[2 * gi], sems[2 * gi + 1], False):
                cp.start()
            i += len(grp)
        token[...] = jnp.zeros_like(token)

    outs = pl.pallas_call(
        body, name=name,
        out_shape=([pltpu.SemaphoreType.DMA((n,)) for n in counts for _ in range(2)]
                   + [pltpu.HBM(b.shape, b.dtype) for b in flat] + [TOKEN]),
        in_specs=[HBM] * nb + [ANY] * len(deps),
        out_specs=[SEM] * (2 * ng) + [HBM] * nb + [pl.BlockSpec(memory_space=pltpu.VMEM)],
        input_output_aliases={i: 2 * ng + i for i in range(nb)},
        compiler_params=pltpu.CompilerParams(has_side_effects=EFFECT),
    )(*[_in_hbm(b) for b in flat], *deps)
    started, i = [], 0
    for gi, grp in enumerate(groups):
        started.append((outs[2 * gi], outs[2 * gi + 1], list(outs[2 * ng + i:2 * ng + i + len(grp)])))
        i += len(grp)
    return started, outs[-1]


def copies_wait(name, started, hop, after):
    sends, recvs, bufs = started
    nb = len(bufs)

    def body(*refs):
        for cp in hop(refs[:nb], refs[nb], refs[nb + 1], True):
            cp.wait_send()
            cp.wait_recv()

    outs = pl.pallas_call(
        body, name=name,
        out_shape=[pltpu.HBM(b.shape, b.dtype) for b in bufs],
        in_specs=[HBM] * nb + [SEM, SEM, ANY], out_specs=[HBM] * nb,
        input_output_aliases={i: i for i in range(nb)},
        compiler_params=pltpu.CompilerParams(has_side_effects=EFFECT),
    )(*bufs, sends, recvs, after)
    return list(outs)


def place_shard(name, w, layer, dtype, deps=()):
    _, r, c = w.shape
    tr = _tile(r, 1024)
    x, y, core = _place()
    me = _slot(x, y, core).astype(jnp.int32).reshape(1)

    def body(me_ref, w_ref, *rest):
        rest[-1][...] = w_ref[...].astype(dtype)

    return pl.pallas_call(
        body, name=name,
        grid_spec=pltpu.PrefetchScalarGridSpec(
            num_scalar_prefetch=1, grid=(r // tr,),
            in_specs=[pl.BlockSpec((None, tr, c), lambda i, me_ref: (layer, i, 0))] + [ANY] * len(deps),
            out_specs=pl.BlockSpec((None, tr, c), lambda i, me_ref: (me_ref[0], i, 0))),
        out_shape=jax.ShapeDtypeStruct((N_DEV, r, c), dtype),
        compiler_params=_params(("parallel",)),
    )(me, w, *deps)


def tie(name, x, *deps):
    def body(*refs):
        del refs

    return pl.pallas_call(
        body, name=name, out_shape=jax.ShapeDtypeStruct(x.shape, x.dtype),
        in_specs=[ANY] * (1 + len(deps)), out_specs=ANY, input_output_aliases={0: 0},
    )(x, *deps)


def pair_add(name, g, from_sibling):
    _, r, c_dim = g.shape
    tr = r
    while tr * c_dim > PAIR_ADD_BLOCK and tr % 16 == 0:
        tr //= 2
    x, y, core = _place()
    where = jnp.stack([core, _chip(x, y)]).astype(jnp.int32)

    def body(where_ref, g_ref, s_ref, o_ref, zone_ref):
        total = (g_ref[...].astype(F32) + s_ref[...].astype(F32)).astype(o_ref.dtype)
        o_ref[...] = total

        @pl.when(pl.program_id(1) == where_ref[1])
        def _():
            zone_ref[...] = total

    blk = pl.BlockSpec((None, tr, c_dim), lambda i, q, where_ref: (q, i, 0))
    return pl.pallas_call(
        body, name=name,
        grid_spec=pltpu.PrefetchScalarGridSpec(
            num_scalar_prefetch=1, grid=(r // tr, N_CHIP),
            in_specs=[pl.BlockSpec((None, None, tr, c_dim), lambda i, q, where_ref: (q, where_ref[0], i, 0)), blk],
            out_specs=[blk, pl.BlockSpec((None, tr, c_dim), lambda i, q, where_ref: (where_ref[1], i, 0))]),
        out_shape=[jax.ShapeDtypeStruct((N_CHIP, r, c_dim), g.dtype)] * 2,
        compiler_params=_params(("parallel", "arbitrary")),
    )(where, g.reshape(N_CHIP, 2, r, c_dim), from_sibling)


def _matmul(name, lhs, rhs, *, out_shape, out_dtype, grid, lhs_spec, rhs_spec, out_spec, dims, acc_shape,
            lhs_fn=None, extra=(), extra_specs=(), epilogue=None, parts=1):
    nk = grid[2]
    n_extra = len(extra)

    def body(*refs):
        lhs_ref, rhs_ref = refs[0], refs[1]
        extra_refs = refs[2:2 + n_extra]
        out_ref = refs[2 + n_extra]

        def product():
            if parts == 1:
                a = lhs_ref[...]
                if lhs_fn is not None:
                    a = lhs_fn(a)
                return lax.dot_general(a, rhs_ref[...], dims, preferred_element_type=F32)
            width = lhs_ref.shape[1] // parts
            total = None
            for b in range(parts):
                term = lax.dot_general(lhs_ref[:, b * width:(b + 1) * width], rhs_ref[b], dims,
                                       preferred_element_type=F32)
                total = term if total is None else total + term
            return total

        def finish(r):
            if epilogue is not None:
                r = epilogue(r, *[e[...] for e in extra_refs])
            out_ref[...] = r.astype(out_dtype)

        if nk == 1:
            finish(product())
        else:
            acc_ref = refs[3 + n_extra]
            k = pl.program_id(2)

            @pl.when(k == 0)
            def _():
                acc_ref[...] = product()

            @pl.when(jnp.logical_and(k > 0, k < nk - 1))
            def _():
                acc_ref[...] += product()

            @pl.when(k == nk - 1)
            def _():
                finish(acc_ref[...] + product())

    return pl.pallas_call(
        body, name=name, grid=grid,
        out_shape=jax.ShapeDtypeStruct(out_shape, out_dtype),
        in_specs=[lhs_spec, rhs_spec, *extra_specs], out_specs=out_spec,
        scratch_shapes=[pltpu.VMEM(acc_shape, F32)] if nk > 1 else [],
        compiler_params=_params(("parallel", "parallel", "arbitrary")),
    )(lhs, rhs, *extra)


def _tile(n, want):
    return want if n % want == 0 else n


def mm_nn(name, x, w, *, out_dtype, tn=512, tk=None, lhs_fn=None, epilogue=None):
    t, kdim = x.shape
    n = w.shape[1]
    tm, tn = _tile(t, MATMUL_ROWS), _tile(n, tn)
    tk = kdim if tk is None else _tile(kdim, tk)
    return _matmul(
        name, x, w, out_shape=(t, n), out_dtype=out_dtype, grid=(t // tm, n // tn, kdim // tk),
        lhs_spec=pl.BlockSpec((tm, tk), lambda i, j, k: (i, k)),
        rhs_spec=pl.BlockSpec((tk, tn), lambda i, j, k: (k, j)),
        out_spec=pl.BlockSpec((tm, tn), lambda i, j, k: (i, j)),
        dims=NN, acc_shape=(tm, tn), lhs_fn=lhs_fn, epilogue=epilogue)


def mm_nn_blocked(name, x, w, *, out_dtype, epilogue=None):
    t, kdim = x.shape
    nb = w.shape[2]
    tm = _tile(t, MATMUL_ROWS)
    tn = nb // 2 if nb >= 1024 else nb
    sub = nb // tn
    return _matmul(
        name, x, w, out_shape=(t, N_DEV * nb), out_dtype=out_dtype, grid=(t // tm, N_DEV * sub, 1),
        lhs_spec=pl.BlockSpec((tm, kdim), lambda i, j, k: (i, k)),
        rhs_spec=pl.BlockSpec((None, kdim, tn), lambda i, j, k: (j // sub, k, j % sub)),
        out_spec=pl.BlockSpec((tm, tn), lambda i, j, k: (i, j)),
        dims=NN, acc_shape=(tm, tn), epilogue=epilogue)


def mm_nt(name, dy, w, *, out_dtype, tn=512, extra=None, epilogue=None):
    t, n = dy.shape
    kdim = w.shape[0]
    tm, tn = _tile(t, MATMUL_ROWS), _tile(kdim, tn)
    extra_arrs = () if extra is None else (extra,)
    extra_specs = () if extra is None else (pl.BlockSpec((tm, tn), lambda i, j, k: (i, j)),)
    return _matmul(
        name, dy, w, out_shape=(t, kdim), out_dtype=out_dtype, grid=(t // tm, kdim // tn, 1),
        lhs_spec=pl.BlockSpec((tm, n), lambda i, j, k: (i, k)),
        rhs_spec=pl.BlockSpec((tn, n), lambda i, j, k: (j, k)),
        out_spec=pl.BlockSpec((tm, tn), lambda i, j, k: (i, j)),
        dims=NT, acc_shape=(tm, tn), extra=extra_arrs, extra_specs=extra_specs, epilogue=epilogue)


def mm_nt_blocked(name, dz, w, *, out_dtype, tn=512):
    t = dz.shape[0]
    kdim, nb = w.shape[1], w.shape[2]
    tm, tn = _tile(t, MATMUL_ROWS), _tile(kdim, tn)
    parts = 2
    return _matmul(
        name, dz, w, out_shape=(t, kdim), out_dtype=out_dtype, grid=(t // tm, kdim // tn, N_DEV // parts),
        lhs_spec=pl.BlockSpec((tm, parts * nb), lambda i, j, k: (i, k)),
        rhs_spec=pl.BlockSpec((parts, tn, nb), lambda i, j, k: (k, j, 0)),
        out_spec=pl.BlockSpec((tm, tn), lambda i, j, k: (i, j)),
        dims=NT, acc_shape=(tm, tn), parts=parts)


def mm_tn(name, x, dy, *, out_dtype, tk=1024, tn=1024, lhs_fn=None):
    t, kdim = x.shape
    n = dy.shape[1]
    tk, tn = _tile(kdim, tk), _tile(n, tn)
    return _matmul(
        name, x, dy, out_shape=(kdim, n), out_dtype=out_dtype, grid=(kdim // tk, n // tn, 1),
        lhs_spec=pl.BlockSpec((t, tk), lambda i, j, k: (k, i)),
        rhs_spec=pl.BlockSpec((t, tn), lambda i, j, k: (k, j)),
        out_spec=pl.BlockSpec((tk, tn), lambda i, j, k: (i, j)),
        dims=TN, acc_shape=(tk, tn), lhs_fn=lhs_fn)


def mm_tn_blocked(name, x, dz, nb, *, out_dtype, tk=1024):
    t, kdim = x.shape
    tk = _tile(kdim, tk)
    return _matmul(
        name, x, dz, out_shape=(N_DEV, kdim, nb), out_dtype=out_dtype, grid=(kdim // tk, N_DEV, 1),
        lhs_spec=pl.BlockSpec((t, tk), lambda i, j, k: (k, i)),
        rhs_spec=pl.BlockSpec((t, nb), lambda i, j, k: (k, j)),
        out_spec=pl.BlockSpec((None, tk, nb), lambda i, j, k: (j, i, 0)),
        dims=TN, acc_shape=(tk, nb))


def _rstd(v):
    return lax.rsqrt(jnp.mean(v * v, axis=-1, keepdims=True) + NORM_EPS)


def _rms_bwd(v, g, dy):
    r = _rstd(v)
    vhat = v * r
    dvh = dy * g
    dv = r * (dvh - vhat * jnp.mean(dvh * vhat, axis=-1, keepdims=True))
    return dv, dy * vhat


def _fold8(v):
    rows, n = v.shape
    return jnp.sum(v.reshape(rows // 8, 8, n), axis=0)


def _fold_lanes(v):
    out = v[:, 0:128]
    for i in range(1, v.shape[1] // 128):
        out = out + v[:, 128 * i:128 * (i + 1)]
    return out


def _accumulate(ref, v):
    i = pl.program_id(0)

    @pl.when(i == 0)
    def _():
        ref[...] = v

    @pl.when(i > 0)
    def _():
        ref[...] += v


def _row_call(body, name, t, ins, row_in, outs, acc_outs=(), tr=ROW_TILE):
    tr = _tile(t, tr)

    def in_spec(a, tiled):
        if isinstance(tiled, tuple):
            width, j = tiled
            return pl.BlockSpec((tr, width), lambda i: (i, j))
        return pl.BlockSpec((tr, a.shape[1]), lambda i: (i, 0)) if tiled else pl.BlockSpec(a.shape, lambda i: (0, 0))

    in_specs = [in_spec(a, tiled) for a, tiled in zip(ins, row_in)]
    out_specs = [pl.BlockSpec((tr, n), lambda i: (i, 0)) for n, _ in outs]
    out_specs += [pl.BlockSpec((8, n), lambda i: (0, 0)) for n in acc_outs]
    out_shape = [jax.ShapeDtypeStruct((t, n), dt) for n, dt in outs]
    out_shape += [jax.ShapeDtypeStruct((8, n), F32) for n in acc_outs]
    return pl.pallas_call(
        body, name=name, grid=(t // tr,), in_specs=in_specs, out_specs=out_specs, out_shape=out_shape,
        compiler_params=_params(("arbitrary",) if acc_outs else ("parallel",)),
    )(*ins)


def norm_pre(name, x, g):
    t, d = x.shape

    def body(x_ref, g_ref, h_ref):
        v = x_ref[...]
        h_ref[...] = (v * _rstd(v) * g_ref[...]).astype(BF16)

    return _row_call(body, name, t, [x, g], [True, False], [(d, BF16)])[0]


def post_pre(name, x, m, g_post, g_pre):
    t, d = x.shape

    def body(x_ref, m_ref, gp_ref, gn_ref, xo_ref, h_ref):
        mv = m_ref[...]
        xn = x_ref[...] + mv * _rstd(mv) * gp_ref[...]
        xo_ref[...] = xn
        h_ref[...] = (xn * _rstd(xn) * gn_ref[...]).astype(BF16)

    return _row_call(body, name, t, [x, m, g_post, g_pre], [True, True, False, False], [(d, F32), (d, BF16)])


def post_loss(name, x, f, g_post, target):
    t, d = x.shape

    def body(x_ref, f_ref, g_ref, t_ref, dx_ref, df_ref, loss_ref, dg_ref):
        fv = f_ref[...]
        g = g_ref[...]
        out = x_ref[...] + fv * _rstd(fv) * g
        err = out - t_ref[...]
        dx = err * (1.0 / d)
        dx_ref[...] = dx
        dfv, dg_rows = _rms_bwd(fv, g, dx)
        df_ref[...] = dfv.astype(BF16)
        _accumulate(loss_ref, _fold8(_fold_lanes(err * err)))
        _accumulate(dg_ref, _fold8(dg_rows))

    return _row_call(body, name, t, [x, f, g_post, target], [True, True, False, True],
                     [(d, F32), (d, BF16)], acc_outs=(128, d))


def bwd_pre_post(name, dx_out, x_in, g_pre, dh, f_prev, g_post_prev):
    t, d = x_in.shape

    def body(dxo_ref, x_ref, gpre_ref, dh_ref, f_ref, gpost_ref, dxi_ref, df_ref, dgpre_ref, dgpost_ref):
        dxv, dgpre_rows = _rms_bwd(x_ref[...], gpre_ref[...], dh_ref[...].astype(F32))
        dxi = dxo_ref[...] + dxv
        dxi_ref[...] = dxi
        dfv, dgpost_rows = _rms_bwd(f_ref[...], gpost_ref[...], dxi)
        df_ref[...] = dfv.astype(BF16)
        _accumulate(dgpre_ref, _fold8(dgpre_rows))
        _accumulate(dgpost_ref, _fold8(dgpost_rows))

    return _row_call(body, name, t, [dx_out, x_in, g_pre, dh, f_prev, g_post_prev],
                     [True, True, False, True, True, False], [(d, F32), (d, BF16)], acc_outs=(d, d))


def bwd_pre_final(name, dx_out, x_in, g_pre, dh):
    t, d = x_in.shape

    def body(dxo_ref, x_ref, gpre_ref, dh_ref, dxi_ref, dgpre_ref):
        dxv, dgpre_rows = _rms_bwd(x_ref[...], gpre_ref[...], dh_ref[...].astype(F32))
        dxi_ref[...] = dxo_ref[...] + dxv
        _accumulate(dgpre_ref, _fold8(dgpre_rows))

    return _row_call(body, name, t, [dx_out, x_in, g_pre, dh], [True, True, False, True], [(d, F32)], acc_outs=(d,))


def _layer_norm_parts(cv):
    mu = jnp.mean(cv, axis=-1, keepdims=True)
    xc = cv - mu
    rstd = lax.rsqrt(jnp.mean(xc * xc, axis=-1, keepdims=True) + NORM_EPS)
    return xc * rstd, rstd


def ln_silu(name, cv, g, b, y, y_block):
    t, n = cv.shape
    tr = _tile(t, ROW_TILE)

    def body(c_ref, g_ref, b_ref, y_in_ref, y_ref):
        chat, _ = _layer_norm_parts(c_ref[...])
        ln = chat * g_ref[...] + b_ref[...]
        y_ref[...] = (ln * jax.nn.sigmoid(ln)).astype(BF16)

    vec = pl.BlockSpec((1, n), lambda i: (0, 0))
    return pl.pallas_call(
        body, name=name, grid=(t // tr,),
        in_specs=[pl.BlockSpec((tr, n), lambda i: (i, 0)), vec, vec, ANY],
        out_specs=pl.BlockSpec((tr, n), lambda i: (i, y_block)),
        out_shape=jax.ShapeDtypeStruct(y.shape, y.dtype), input_output_aliases={3: 0},
        compiler_params=_params(("parallel",)),
    )(cv, g, b, y)


def ln_silu_bwd(name, cv, g, b, dy, dy_block):
    t, n = cv.shape

    def body(c_ref, g_ref, b_ref, dy_ref, dc_ref, dg_ref, db_ref):
        chat, rstd = _layer_norm_parts(c_ref[...])
        g = g_ref[...]
        ln = chat * g + b_ref[...]
        s = jax.nn.sigmoid(ln)
        dln = dy_ref[...].astype(F32) * (s * (1.0 + ln * (1.0 - s)))
        dchat = dln * g
        dc_ref[...] = rstd * (dchat - jnp.mean(dchat, axis=-1, keepdims=True)
                              - chat * jnp.mean(dchat * chat, axis=-1, keepdims=True))
        _accumulate(dg_ref, _fold8(dln * chat))
        _accumulate(db_ref, _fold8(dln))

    return _row_call(body, name, t, [cv, g, b, dy], [True, False, False, (n, dy_block)], [(n, F32)], acc_outs=(n, n))


def _chunks(t, fn, tc=TIME_CHUNK):
    tc = _tile(t, tc)

    def step(i, carry):
        fn(pl.multiple_of(i * tc, tc), tc)
        return carry

    lax.fori_loop(0, t // tc, step, 0)


def _rows_from(v, start, n):
    res = start % 8
    base = v if res == 0 else pltpu.roll(v, v.shape[0] - res, axis=0)
    return base[start - res:start - res + n, :]


def _shifted(window, offsets, tc):
    rows = window.shape[0]
    by_residue = {}
    for k, off in enumerate(offsets):
        by_residue.setdefault(off % 8, []).append((k, off))
    for res, taps in by_residue.items():
        base = window if res == 0 else pltpu.roll(window, rows - res, axis=0)
        for k, off in taps:
            yield k, base[off - res:off - res + tc, :]


def _taps(window, w_ref, offsets, tc, flip=False):
    acc = None
    for k, rows in _shifted(window, offsets, tc):
        kk = len(offsets) - 1 - k if flip else k
        term = w_ref[kk:kk + 1, :] * rows
        acc = term if acc is None else acc + term
    return acc


def _window_sums(win, tc, causal):
    sums = []
    cur, rows, step = win, tc + HALO, 1
    for _ in POOL_WINDOWS:
        rows -= 8
        if causal:
            cur = cur[8:8 + rows, :] + _rows_from(cur, 8 - step, rows)
            sums.append(cur[rows - tc:rows, :])
        else:
            cur = cur[0:rows, :] + _rows_from(cur, step, rows)
            sums.append(cur[0:tc, :])
        step *= 2
    return sums


def _pick(vals, g):
    out = vals[-1]
    for i in range(len(vals) - 2, -1, -1):
        out = jnp.where(g == i, vals[i], out)
    return out


def _pool_count(s, tc, g):
    t1 = (lax.broadcasted_iota(jnp.int32, (tc, 1), 0) + (s + 1)).astype(F32)
    width = _pick([float(w) for w in POOL_WINDOWS], g)
    return jnp.minimum(t1, width)


def pool_fwd(name, z, pool_w, pool_scale, d_pool, y_width):
    t = z.shape[0]
    ng, pg = pool_w.shape[0], pool_w.shape[1]

    def body(u_ref, w_ref, s_ref, pooled_ref, y_ref, pad):
        g = pl.program_id(0)
        pad[pl.ds(0, HALO), :] = jnp.zeros((HALO, pg), F32)

        def fill(s, tc):
            pad[pl.ds(HALO + s, tc), :] = u_ref[pl.ds(s, tc), :].astype(F32)

        def chunk(s, tc):
            win = pad[pl.ds(s, tc + HALO), :]
            total = _pick(_window_sums(win, tc, causal=True), g)
            pooled = total / _pool_count(s, tc, g) - win[HALO:HALO + tc, :]
            pooled_ref[pl.ds(s, tc), :] = pooled.astype(BF16)

        _chunks(t, fill)
        _chunks(t, chunk)
        mixed = jnp.dot(pooled_ref[...], w_ref[...], preferred_element_type=F32)
        y_ref[...] = (mixed * s_ref[...]).astype(BF16)

    col = pl.BlockSpec((t, pg), lambda g: (0, g))
    return pl.pallas_call(
        body, name=name, grid=(ng,),
        in_specs=[col, pl.BlockSpec((None, pg, pg), lambda g: (g, 0, 0)), pl.BlockSpec((1, pg), lambda g: (0, g))],
        out_specs=[col, col],
        out_shape=[jax.ShapeDtypeStruct((t, d_pool), BF16), jax.ShapeDtypeStruct((t, y_width), BF16)],
        scratch_shapes=[pltpu.VMEM((t + HALO, pg), F32)],
        compiler_params=_params(("parallel",)),
    )(z, pool_w, pool_scale)


def pool_bwd(name, pooled, dy, pool_w, pool_scale, dz):
    t, d_pool = pooled.shape
    ng, pg = pool_w.shape[0], pool_w.shape[1]

    def body(p_ref, dy_ref, w_ref, s_ref, dz_ref, du_ref, dw_ref, ds_ref, pad):
        g = pl.program_id(0)
        w = w_ref[...]
        dyv = dy_ref[...].astype(F32)
        mixed = jnp.dot(p_ref[...], w, preferred_element_type=F32)
        ds_ref[...] = jnp.sum(dyv * mixed, axis=0, keepdims=True)
        dmixed = (dyv * s_ref[...]).astype(BF16)
        dw_ref[...] = lax.dot_general(p_ref[...], dmixed, TN, preferred_element_type=F32)
        pad[...] = jnp.zeros((t + HALO, pg), F32)
        pad[pl.ds(0, t), :] = lax.dot_general(dmixed, w, NT, preferred_element_type=F32)

        def scale(s, tc):
            pad[pl.ds(s, tc), :] = pad[pl.ds(s, tc), :] / _pool_count(s, tc, g)

        def chunk(s, tc):
            win = pad[pl.ds(s, tc + HALO), :]
            total = _pick(_window_sums(win, tc, causal=False), g)
            du_ref[pl.ds(s, tc), :] = (total - win[0:tc, :] * _pool_count(s, tc, g)).astype(BF16)

        _chunks(t, scale)
        _chunks(t, chunk)

    col = pl.BlockSpec((t, pg), lambda g: (0, g))
    vec = pl.BlockSpec((1, pg), lambda g: (0, g))
    mat = pl.BlockSpec((None, pg, pg), lambda g: (g, 0, 0))
    return pl.pallas_call(
        body, name=name, grid=(ng,),
        in_specs=[col, col, mat, vec, ANY], out_specs=[col, mat, vec],
        out_shape=[jax.ShapeDtypeStruct(dz.shape, dz.dtype), jax.ShapeDtypeStruct((ng, pg, pg), F32),
                   jax.ShapeDtypeStruct((1, d_pool), F32)],
        input_output_aliases={4: 0},
        scratch_shapes=[pltpu.VMEM((t + HALO, pg), F32)],
        compiler_params=_params(("parallel",)),
    )(pooled, dy, pool_w, pool_scale, dz)


def conv_fwd(name, z, conv_w, conv_b, d_pool, d_conv):
    t = z.shape[0]
    kw = conv_w.shape[0]
    tc_ch = _tile(d_conv, CHANNEL_TILE)
    v0, g0 = d_pool // tc_ch, (d_pool + d_conv) // tc_ch

    def body(v_ref, g_ref, w_ref, b_ref, c_ref, pad):
        pad[pl.ds(0, HALO), :] = jnp.zeros((HALO, tc_ch), F32)

        def fill(s, tc):
            pad[pl.ds(HALO + s, tc), :] = v_ref[pl.ds(s, tc), :].astype(F32) * jax.nn.sigmoid(g_ref[pl.ds(s, tc), :].astype(F32))

        def chunk(s, tc):
            win = pad[pl.ds(s, tc + HALO), :]
            c_ref[pl.ds(s, tc), :] = _taps(win, w_ref, [HALO - (kw - 1) + k for k in range(kw)], tc) + b_ref[...]

        _chunks(t, fill)
        _chunks(t, chunk)

    return pl.pallas_call(
        body, name=name, grid=(d_conv // tc_ch,),
        in_specs=[pl.BlockSpec((t, tc_ch), lambda j: (0, v0 + j)), pl.BlockSpec((t, tc_ch), lambda j: (0, g0 + j)),
                  pl.BlockSpec((kw, tc_ch), lambda j: (0, j)), pl.BlockSpec((1, tc_ch), lambda j: (0, j))],
        out_specs=pl.BlockSpec((t, tc_ch), lambda j: (0, j)),
        out_shape=jax.ShapeDtypeStruct((t, d_conv), F32),
        scratch_shapes=[pltpu.VMEM((t + HALO, tc_ch), F32)],
        compiler_params=_params(("parallel",)),
    )(z, z, conv_w, conv_b)


def conv_bwd(name, z, dc, conv_w, d_pool, d_conv):
    t = z.shape[0]
    kw = conv_w.shape[0]
    tc_ch = _tile(d_conv, CHANNEL_TILE)
    v0, g0 = d_pool // tc_ch, (d_pool + d_conv) // tc_ch

    def body(v_ref, g_ref, dc_ref, w_ref, dz_ref, dw_ref, db_ref, pad_a, pad_dc, acc_w, acc_b, tiles, sems):
        j = pl.program_id(0)
        dv_ref, dg_ref = tiles.at[0], tiles.at[1]
        writes = [pltpu.make_async_copy(tiles.at[p], dz_ref.at[:, pl.ds((first + j) * tc_ch, tc_ch)], sems.at[p])
                  for p, first in enumerate([v0, g0])]

        def wait_writes():
            for cp in writes:
                cp.wait()

        pad_a[pl.ds(0, HALO), :] = jnp.zeros((HALO, tc_ch), F32)
        pad_dc[pl.ds(t, HALO), :] = jnp.zeros((HALO, tc_ch), F32)
        acc_w[...] = jnp.zeros_like(acc_w)
        acc_b[...] = jnp.zeros_like(acc_b)

        def fill(s, tc):
            pad_a[pl.ds(HALO + s, tc), :] = v_ref[pl.ds(s, tc), :].astype(F32) * jax.nn.sigmoid(g_ref[pl.ds(s, tc), :].astype(F32))
            pad_dc[pl.ds(s, tc), :] = dc_ref[pl.ds(s, tc), :]

        def chunk(s, tc):
            dcv = pad_dc[pl.ds(s, tc), :]
            win_a = pad_a[pl.ds(s, tc + HALO), :]
            for k, rows in _shifted(win_a, [HALO - (kw - 1) + k for k in range(kw)], tc):
                acc_w[pl.ds(8 * k, 8), :] += _fold8(dcv * rows)
            acc_b[...] += _fold8(dcv)
            da = _taps(pad_dc[pl.ds(s, tc + HALO), :], w_ref, list(range(kw)), tc, flip=True)
            vv = v_ref[pl.ds(s, tc), :].astype(F32)
            sg = jax.nn.sigmoid(g_ref[pl.ds(s, tc), :].astype(F32))
            dv_ref[pl.ds(s, tc), :] = (da * sg).astype(BF16)
            dg_ref[pl.ds(s, tc), :] = (da * vv * sg * (1.0 - sg)).astype(BF16)

        _chunks(t, fill)
        pl.when(j > 0)(wait_writes)
        _chunks(t, chunk)
        for cp in writes:
            cp.start()
        pl.when(j == n_tiles - 1)(wait_writes)
        for k in range(kw):
            dw_ref[k:k + 1, :] = jnp.sum(acc_w[pl.ds(8 * k, 8), :], axis=0, keepdims=True)
        db_ref[...] = jnp.sum(acc_b[...], axis=0, keepdims=True)

    n_tiles = d_conv // tc_ch
    return pl.pallas_call(
        body, name=name, grid=(n_tiles,),
        in_specs=[pl.BlockSpec((t, tc_ch), lambda j: (0, v0 + j)), pl.BlockSpec((t, tc_ch), lambda j: (0, g0 + j)),
                  pl.BlockSpec((t, tc_ch), lambda j: (0, j)), pl.BlockSpec((kw, tc_ch), lambda j: (0, j))],
        out_specs=[ANY, pl.BlockSpec((kw, tc_ch), lambda j: (0, j)), pl.BlockSpec((1, tc_ch), lambda j: (0, j))],
        out_shape=[jax.ShapeDtypeStruct((t, d_pool + 2 * d_conv), BF16),
                   jax.ShapeDtypeStruct((kw, d_conv), F32), jax.ShapeDtypeStruct((1, d_conv), F32)],
        scratch_shapes=[pltpu.VMEM((t + HALO, tc_ch), F32), pltpu.VMEM((t + HALO, tc_ch), F32),
                        pltpu.VMEM((8 * kw, tc_ch), F32), pltpu.VMEM((8, tc_ch), F32),
                        pltpu.VMEM((2, t, tc_ch), BF16), pltpu.SemaphoreType.DMA((2,))],
        compiler_params=_params(("arbitrary",)),
    )(z, z, dc, conv_w)


def short_fwd(name, z, conv_w, d_short):
    t = z.shape[0]
    kw = conv_w.shape[0]
    tc_ch = _tile(d_short, CHANNEL_TILE)
    nt = d_short // tc_ch

    def body(b_ref, c_ref, u_ref, w_ref, y_ref, pad):
        pad[pl.ds(0, HALO), :] = jnp.zeros((HALO, tc_ch), F32)

        def fill(s, tc):
            pad[pl.ds(HALO + s, tc), :] = c_ref[pl.ds(s, tc), :].astype(F32) * u_ref[pl.ds(s, tc), :].astype(F32)

        def chunk(s, tc):
            win = pad[pl.ds(s, tc + HALO), :]
            cq = _taps(win, w_ref, [HALO - (kw - 1) + k for k in range(kw)], tc)
            y_ref[pl.ds(s, tc), :] = (b_ref[pl.ds(s, tc), :].astype(F32) * cq).astype(BF16)

        _chunks(t, fill)
        _chunks(t, chunk)

    return pl.pallas_call(
        body, name=name, grid=(nt,),
        in_specs=[pl.BlockSpec((t, tc_ch), lambda j: (0, j)), pl.BlockSpec((t, tc_ch), lambda j: (0, nt + j)),
                  pl.BlockSpec((t, tc_ch), lambda j: (0, 2 * nt + j)), pl.BlockSpec((kw, tc_ch), lambda j: (0, j))],
        out_specs=pl.BlockSpec((t, tc_ch), lambda j: (0, j)),
        out_shape=jax.ShapeDtypeStruct((t, d_short), BF16),
        scratch_shapes=[pltpu.VMEM((t + HALO, tc_ch), F32)],
        compiler_params=_params(("parallel",)),
    )(z, z, z, conv_w)


def short_bwd(name, z, dy, conv_w, d_short):
    t = z.shape[0]
    kw = conv_w.shape[0]
    tc_ch = _tile(d_short, CHANNEL_TILE)
    nt = d_short // tc_ch

    def body(b_ref, c_ref, u_ref, dy_ref, w_ref, dz_ref, dw_ref, pad_q, pad_dcq, acc_w, tiles, sems):
        j = pl.program_id(0)
        db_ref, dcg_ref, du_ref = tiles.at[0], tiles.at[1], tiles.at[2]
        writes = [pltpu.make_async_copy(tiles.at[p], dz_ref.at[:, pl.ds((p * nt + j) * tc_ch, tc_ch)], sems.at[p])
                  for p in range(3)]

        def wait_writes():
            for cp in writes:
                cp.wait()

        pad_q[pl.ds(0, HALO), :] = jnp.zeros((HALO, tc_ch), F32)
        pad_dcq[pl.ds(t, HALO), :] = jnp.zeros((HALO, tc_ch), F32)
        acc_w[...] = jnp.zeros_like(acc_w)

        def fill(s, tc):
            rows = pl.ds(s, tc)
            pad_q[pl.ds(HALO + s, tc), :] = c_ref[rows, :].astype(F32) * u_ref[rows, :].astype(F32)
            pad_dcq[rows, :] = dy_ref[rows, :].astype(F32) * b_ref[rows, :].astype(F32)

        def chunk(s, tc):
            rows = pl.ds(s, tc)
            win_q = pad_q[pl.ds(s, tc + HALO), :]
            dcq = pad_dcq[rows, :]
            cq = None
            for k, shifted in _shifted(win_q, [HALO - (kw - 1) + k for k in range(kw)], tc):
                acc_w[pl.ds(8 * k, 8), :] += _fold8(dcq * shifted)
                term = w_ref[k:k + 1, :] * shifted
                cq = term if cq is None else cq + term
            db_ref[rows, :] = (dy_ref[rows, :].astype(F32) * cq).astype(BF16)
            dq = _taps(pad_dcq[pl.ds(s, tc + HALO), :], w_ref, list(range(kw)), tc, flip=True)
            dcg_ref[rows, :] = (dq * u_ref[rows, :].astype(F32)).astype(BF16)
            du_ref[rows, :] = (dq * c_ref[rows, :].astype(F32)).astype(BF16)

        _chunks(t, fill)
        pl.when(j > 0)(wait_writes)
        _chunks(t, chunk)
        for cp in writes:
            cp.start()
        pl.when(j == nt - 1)(wait_writes)
        for k in range(kw):
            dw_ref[k:k + 1, :] = jnp.sum(acc_w[pl.ds(8 * k, 8), :], axis=0, keepdims=True)

    zspec = [pl.BlockSpec((t, tc_ch), lambda j, o=o: (0, o * nt + j)) for o in range(3)]
    return pl.pallas_call(
        body, name=name, grid=(nt,),
        in_specs=[*zspec, pl.BlockSpec((t, tc_ch), lambda j: (0, j)), pl.BlockSpec((kw, tc_ch), lambda j: (0, j))],
        out_specs=[ANY, pl.BlockSpec((kw, tc_ch), lambda j: (0, j))],
        out_shape=[jax.ShapeDtypeStruct((t, 3 * d_short), BF16), jax.ShapeDtypeStruct((kw, d_short), F32)],
        scratch_shapes=[pltpu.VMEM((t + HALO, tc_ch), F32), pltpu.VMEM((t + HALO, tc_ch), F32),
                        pltpu.VMEM((8 * kw, tc_ch), F32), pltpu.VMEM((3, t, tc_ch), BF16),
                        pltpu.SemaphoreType.DMA((3,))],
        compiler_params=_params(("arbitrary",)),
    )(z, z, z, dy, conv_w)


def adamw(name, w, m, v, contributions):
    r, c = w.shape
    nc = len(contributions)
    n_slots = contributions[0].shape[0]
    tr = 256 if c <= 1024 else 128
    if any(a.shape[1] % tr for a in contributions):
        assert nc == 1
        tr = r
    tiles = [a.shape[1] // tr for a in contributions]
    first = [sum(tiles[:j]) for j in range(nc)]

    def body(w_ref, m_ref, v_ref, *rest):
        g_refs, (grad_ref, delta_ref, nm_ref, nv_ref) = rest[:nc], rest[nc:]
        i = pl.program_id(0)
        g = None
        for j, g_ref in enumerate(g_refs):
            s = g_ref[0].astype(F32)
            for slot in range(1, n_slots):
                s = s + g_ref[slot].astype(F32)
            g = s if g is None else jnp.where(i >= first[j], s, g)
        nm = ADAM_B1 * m_ref[...] + (1.0 - ADAM_B1) * g
        nv = ADAM_B2 * v_ref[...] + (1.0 - ADAM_B2) * (g * g)
        m_hat = nm / (1.0 - ADAM_B1 ** ADAM_STEP)
        v_hat = nv / (1.0 - ADAM_B2 ** ADAM_STEP)
        grad_ref[...] = g
        delta_ref[...] = -ADAM_LR * (m_hat / (jnp.sqrt(v_hat) + ADAM_EPS) + ADAM_WD * w_ref[...])
        nm_ref[...] = nm
        nv_ref[...] = nv

    blk = pl.BlockSpec((tr, c), lambda i: (i, 0))
    g_specs = [pl.BlockSpec((n_slots, tr, c), lambda i, j=j: (0, jnp.clip(i - first[j], 0, tiles[j] - 1), 0))
               for j in range(nc)]
    return pl.pallas_call(
        body, name=name, grid=(r // tr,),
        in_specs=[blk, blk, blk, *g_specs],
        out_specs=[blk] * 4, out_shape=[jax.ShapeDtypeStruct((r, c), F32)] * 4,
        compiler_params=_params(("parallel",)),
    )(w, m, v, *contributions)


def _pad_rows(a, rows):
    return jnp.pad(a, ((0, rows - a.shape[0]), (0, 0)))


def kernel(x, mix_pre_g, mix_post_g, ffn_pre_g, ffn_post_g, ab_w_in, pool_w, pool_scale, conv_w, conv_b, conv_ln_g, conv_ln_b, ab_w_out, sc_w_in, sc_conv_w, sc_w_out, ffn_w1, ffn_w2, loss_target, m_mix_pre_g, m_mix_post_g, m_ffn_pre_g, m_ffn_post_g, m_ab_w_in, m_pool_w, m_pool_scale, m_conv_w, m_conv_b, m_conv_ln_g, m_conv_ln_b, m_ab_w_out, m_sc_w_in, m_sc_conv_w, m_sc_w_out, m_ffn_w1, m_ffn_w2, v_mix_pre_g, v_mix_post_g, v_ffn_pre_g, v_ffn_post_g, v_ab_w_in, v_pool_w, v_pool_scale, v_conv_w, v_conv_b, v_conv_ln_g, v_conv_ln_b, v_ab_w_out, v_sc_w_in, v_sc_conv_w, v_sc_w_out, v_ffn_w1, v_ffn_w2):
    t, d = x.shape[1], x.shape[2]
    d_pool = pool_scale.shape[1]
    d_conv = conv_b.shape[1]
    d_short = d
    ng, pg = pool_w.shape[1], pool_w.shape[3]
    kw, ks = conv_w.shape[1], sc_conv_w.shape[1]
    nb_ab, nb_sc, nb_ff = ab_w_in.shape[2], sc_w_in.shape[2], ffn_w1.shape[2]

    xs = x[0]
    target = loss_target[0]

    lanes = min(128, d_conv // N_DEV)
    small_rows = [kw * (d_conv // N_DEV) // lanes, ks * (d_short // N_DEV) // lanes, ng * (pg // N_DEV) * pg // lanes]
    small_total = -(-sum(small_rows) // 8) * 8
    r0, r1, r2 = small_rows[0], small_rows[0] + small_rows[1], sum(small_rows)

    def pack_small(a_conv, a_sconv, a_pool):
        parts = [a_conv[0].reshape(-1, lanes), a_sconv[0].reshape(-1, lanes), a_pool[0].reshape(-1, lanes)]
        return _pad_rows(jnp.concatenate(parts, axis=0), small_total)

    shards = {
        "ab_in": (ab_w_in, 0, BF16), "small": (pack_small(conv_w, sc_conv_w, pool_w)[None], 0, F32),
        "ab_out": (ab_w_out, 0, BF16), "ff1_0": (ffn_w1, 0, BF16), "ff2_0": (ffn_w2, 0, BF16),
        "sc_in": (sc_w_in, 0, BF16), "sc_out": (sc_w_out, 0, BF16),
        "ff1_1": (ffn_w1, 1, BF16), "ff2_1": (ffn_w2, 1, BF16)}
    direct = ["ab_in", "small", "ab_out"]
    zones = {nm: place_shard("place_" + nm, *shards[nm]) for nm in direct}
    started, token = copies_start("gather_start", [[zones[nm]] for nm in direct], _first_hop, 4)
    started = dict(zip(direct, started))
    ring = {}
    for nm in ["ff1_0", "ff2_0"]:
        zones[nm] = place_shard("place_" + nm, *shards[nm], deps=[token])
        (ring[nm],), token = copies_start("ring_start_" + nm, [[zones[nm]]], _ring_hop1, 3, deps=[token])
    for nm in shards:
        if nm not in zones:
            zones[nm] = place_shard("place_" + nm, *shards[nm], deps=[token])

    ties = [0]

    def after(v, *deps):
        ties[0] += 1
        return tie(f"tie_{ties[0]}", v, *deps)

    def fetch_begin(nm, dep):
        (zone,) = copies_wait("gather_wait_" + nm, started[nm], _first_hop, dep)
        (hop,), tok = copies_start("forward_start_" + nm, [[zone]], _second_hop, 3)
        return hop, tok

    def fetch_end(nm, hop, dep):
        return copies_wait("forward_wait_" + nm, hop, _second_hop, dep)[0]

    def ring_step(tag, dep, second=None, first=None, third=None):
        names, groups, hops, counts = [], [], [], []
        if second is not None:
            groups.append(copies_wait("ring1_wait_" + second, ring[second], _ring_hop1, dep))
            names, hops, counts = names + [second], hops + [_ring_hop2], counts + [4]
        if first is not None:
            groups.append([zones[first]])
            names, hops, counts = names + [first], hops + [_ring_hop1], counts + [3]
        if third is not None:
            groups.append(copies_wait("ring2_wait_" + third, ring[third], _ring_hop2, dep))
            names, hops, counts = names + [third], hops + [_ring_hop3], counts + [1]
        begun, tok = copies_start("ring_start_" + tag, groups, hops, counts, deps=[dep])
        ring.update(zip(names, begun))
        return tok

    def ring_done(nm, dep):
        return copies_wait("ring3_wait_" + nm, ring[nm], _ring_hop3, dep)[0]

    relu = lambda r: jnp.maximum(r, 0.0)
    square = lambda a: a * a
    relu2_bwd = lambda r, a: r * (2.0 * a.astype(F32))

    def row(vec, l):
        return vec[l:l + 1]

    hop_small, _ = fetch_begin("small", token)
    hop_ab_in, tok = fetch_begin("ab_in", token)
    w_small = fetch_end("small", hop_small, tok)
    w_ab_in = fetch_end("ab_in", hop_ab_in, tok)
    w_conv = w_small[:, :r0].reshape(N_DEV, kw, -1).transpose(1, 0, 2).reshape(kw, d_conv)
    w_sconv = w_small[:, r0:r1].reshape(N_DEV, ks, -1).transpose(1, 0, 2).reshape(ks, d_short)
    w_pool = w_small[:, r1:r2].reshape(N_DEV, ng, -1, pg).transpose(1, 0, 2, 3).reshape(ng, pg, pg).astype(BF16)
    h0 = norm_pre("norm_pre", xs, after(row(mix_pre_g, 0), token))
    z0 = mm_nn_blocked("ab_in", h0, w_ab_in, out_dtype=BF16)
    hop, tok = fetch_begin("ab_out", z0)
    z0 = after(z0, tok)
    pooled, y0 = pool_fwd("pool_fwd", z0, w_pool, pool_scale, d_pool, d_pool + d_conv)
    cv = conv_fwd("conv_fwd", z0, w_conv, conv_b, d_pool, d_conv)
    y0 = ln_silu("ln_silu", cv, conv_ln_g, conv_ln_b, y0, d_pool // d_conv)
    w_ab_out = fetch_end("ab_out", hop, y0)
    tok = ring_step("a", w_ab_out, second="ff1_0", first="sc_in")
    y0 = after(y0, tok)
    m0 = mm_nn("ab_out", y0, w_ab_out.reshape(d_pool + d_conv, d), out_dtype=F32)
    x1, h1 = post_pre("post_pre_0", xs, m0, row(mix_post_g, 0), row(ffn_pre_g, 0))
    tok = ring_step("b", h1, second="ff2_0", first="sc_out", third="ff1_0")
    w_ff1_0 = ring_done("ff1_0", tok)
    a0 = mm_nn_blocked("ffn0_up", h1, w_ff1_0, out_dtype=BF16, epilogue=relu)
    tok = ring_step("c", a0, second="sc_in", first="ff1_1", third="ff2_0")
    w_ff2_0 = ring_done("ff2_0", tok).reshape(-1, d)
    f0 = mm_nn("ffn0_down", a0, w_ff2_0, out_dtype=F32, tk=2048, lhs_fn=square)
    tok = ring_step("d", f0, second="sc_out", first="ff2_1", third="sc_in")
    f0 = after(f0, tok)
    x2, h2 = post_pre("post_pre_1", x1, f0, row(ffn_post_g, 0), row(mix_pre_g, 1))
    w_sc_in = ring_done("sc_in", h2)
    z1 = mm_nn_blocked("sc_in", h2, w_sc_in, out_dtype=BF16)
    tok = ring_step("e", z1, second="ff1_1", third="sc_out")
    z1 = after(z1, tok)
    y1 = short_fwd("short_fwd", z1, w_sconv, d_short)
    w_sc_out = ring_done("sc_out", y1).reshape(d_short, d)
    m1 = mm_nn("sc_out", y1, w_sc_out, out_dtype=F32)
    tok = ring_step("f", m1, second="ff2_1")
    m1 = after(m1, tok)
    x3, h3 = post_pre("post_pre_2", x2, m1, row(mix_post_g, 1), row(ffn_pre_g, 1))
    tok = ring_step("g", h3, third="ff1_1")
    w_ff1_1 = ring_done("ff1_1", tok)
    a1 = mm_nn_blocked("ffn1_up", h3, w_ff1_1, out_dtype=BF16, epilogue=relu)
    tok = ring_step("h", a1, third="ff2_1")
    w_ff2_1 = ring_done("ff2_1", tok).reshape(-1, d)
    f1 = mm_nn("ffn1_down", a1, w_ff2_1, out_dtype=F32, tk=2048, lhs_fn=square)
    dx4, df1, loss_part, dg_ffn_post1 = post_loss("post_loss", x3, f1, row(ffn_post_g, 1), target)
    loss = lax.psum(jnp.sum(loss_part) * (0.5 / d), ("x", "y", "c"))

    def reduce_begin(tag, g):
        zone = lax.empty((N_CHIP,) + g.shape[1:], g.dtype)
        (hop,), tok = copies_start("pair_start_" + tag, [[g, zone]], _pair_hop, N_CHIP)
        return hop, tok

    def reduce_middle(tag, hop, dep):
        g, from_sibling = copies_wait("pair_wait_" + tag, hop, _pair_hop, dep)
        pair_sum, zone = pair_add("pair_add_" + tag, g, from_sibling)
        (hop2,), tok = copies_start("chips_start_" + tag, [[pair_sum, zone]], _chip_hop, 3)
        return hop2, tok

    def reduce_end(tag, hop2, dep):
        return copies_wait("chips_wait_" + tag, hop2, _chip_hop, dep)[1]

    dw = mm_tn("ffn1_dw2", a1, df1, out_dtype=BF16, lhs_fn=square)
    red_ff2_1, tok = reduce_begin("ff2_1", dw.reshape(N_DEV, -1, d))
    df1 = after(df1, tok)
    dpre = mm_nt("ffn1_da", df1, w_ff2_1, out_dtype=BF16, extra=a1, epilogue=relu2_bwd)
    dw = mm_tn_blocked("ffn1_dw1", h3, dpre, nb_ff, out_dtype=BF16)
    red_ff1_1, tok = reduce_begin("ff1_1", dw)
    dpre = after(dpre, tok)
    dh3 = mm_nt_blocked("ffn1_dh", dpre, w_ff1_1, out_dtype=BF16)
    red_ff2_1, tok = reduce_middle("ff2_1", red_ff2_1, dh3)
    dh3 = after(dh3, tok)
    dx3, dm1, dg_ffn_pre1, dg_mix_post1 = bwd_pre_post("bwd_3", dx4, x3, row(ffn_pre_g, 1), dh3, m1, row(mix_post_g, 1))

    dw = mm_tn("sc_dwout", y1, dm1, out_dtype=BF16)
    red_sc_out, tok = reduce_begin("sc_out", dw.reshape(N_DEV, -1, d))
    dm1 = after(dm1, tok)
    dy1 = mm_nt("sc_dy", dm1, w_sc_out, out_dtype=BF16)
    red_ff1_1, tok = reduce_middle("ff1_1", red_ff1_1, dy1)
    dy1 = after(dy1, tok)
    dz1, dw_sconv = short_bwd("short_bwd", z1, dy1, w_sconv, d_short)
    dw = mm_tn_blocked("sc_dwin", h2, dz1, nb_sc, out_dtype=BF16)
    red_sc_in, tok = reduce_begin("sc_in", dw)
    dz1 = after(dz1, tok)
    dh2 = mm_nt_blocked("sc_dh", dz1, w_sc_in, out_dtype=BF16)
    red_sc_out, tok = reduce_middle("sc_out", red_sc_out, dh2)
    dh2 = after(dh2, tok)
    dx2, df0, dg_mix_pre1, dg_ffn_post0 = bwd_pre_post("bwd_2", dx3, x2, row(mix_pre_g, 1), dh2, f0, row(ffn_post_g, 0))

    dw = mm_tn("ffn0_dw2", a0, df0, out_dtype=BF16, lhs_fn=square)
    red_ff2_0, tok = reduce_begin("ff2_0", dw.reshape(N_DEV, -1, d))
    df0 = after(df0, tok)
    dpre = mm_nt("ffn0_da", df0, w_ff2_0, out_dtype=BF16, extra=a0, epilogue=relu2_bwd)
    red_sc_in, tok = reduce_middle("sc_in", red_sc_in, dpre)
    dpre = after(dpre, tok)
    dw = mm_tn_blocked("ffn0_dw1", h1, dpre, nb_ff, out_dtype=BF16)
    red_ff1_0, tok = reduce_begin("ff1_0", dw)
    dpre = after(dpre, tok)
    dh1 = mm_nt_blocked("ffn0_dh", dpre, w_ff1_0, out_dtype=BF16)
    red_ff2_0, tok = reduce_middle("ff2_0", red_ff2_0, dh1)
    dh1 = after(dh1, tok)
    dx1, dm0, dg_ffn_pre0, dg_mix_post0 = bwd_pre_post("bwd_1", dx2, x1, row(ffn_pre_g, 0), dh1, m0, row(mix_post_g, 0))

    dw = mm_tn("ab_dwout", y0, dm0, out_dtype=BF16)
    red_ab_out, tok = reduce_begin("ab_out", dw.reshape(N_DEV, -1, d))
    dm0 = after(dm0, tok)
    dy0 = mm_nt("ab_dy", dm0, w_ab_out.reshape(d_pool + d_conv, d), out_dtype=BF16)
    red_ff1_0, tok = reduce_middle("ff1_0", red_ff1_0, dy0)
    dy0 = after(dy0, tok)
    dcv, dg_ln_g, dg_ln_b = ln_silu_bwd("ln_silu_bwd", cv, conv_ln_g, conv_ln_b, dy0, d_pool // d_conv)
    dz0, dw_conv, dg_conv_b = conv_bwd("conv_bwd", z0, dcv, w_conv, d_pool, d_conv)
    dz0, dw_pool, dg_pool_scale = pool_bwd("pool_bwd", pooled, dy0, w_pool, pool_scale, dz0)
    small_parts = [
        dw_conv.reshape(kw, N_DEV, -1).transpose(1, 0, 2).reshape(N_DEV, -1, lanes),
        dw_sconv.reshape(ks, N_DEV, -1).transpose(1, 0, 2).reshape(N_DEV, -1, lanes),
        dw_pool.reshape(ng, N_DEV, pg // N_DEV, pg).transpose(1, 0, 2, 3).reshape(N_DEV, -1, lanes),
    ]
    small = jnp.pad(jnp.concatenate(small_parts, axis=1), ((0, 0), (0, small_total - r2), (0, 0)))
    red_small, tok = reduce_begin("small", small)
    red_ab_out, tok2 = reduce_middle("ab_out", red_ab_out, dz0)
    dz0 = after(dz0, tok, tok2)
    dw = mm_tn_blocked("ab_dwin", h0, dz0, nb_ab, out_dtype=BF16)
    red_ab_in, tok = reduce_begin("ab_in", dw)
    dz0 = after(dz0, tok)
    dh0 = mm_nt_blocked("ab_dh", dz0, w_ab_in, out_dtype=BF16)
    red_small, tok = reduce_middle("small", red_small, dh0)
    dh0 = after(dh0, tok)
    grad_x, dg_mix_pre0 = bwd_pre_final("bwd_0", dx1, xs, row(mix_pre_g, 0), dh0)
    red_ab_in, tok = reduce_middle("ab_in", red_ab_in, grad_x)

    fold = lambda a: jnp.sum(a, axis=0, keepdims=True)
    rep_rows = [fold(dg_mix_pre0), fold(dg_mix_pre1), fold(dg_mix_post0), fold(dg_mix_post1),
                fold(dg_ffn_pre0), fold(dg_ffn_pre1), fold(dg_ffn_post0), fold(dg_ffn_post1)]
    tail = jnp.concatenate([dg_pool_scale, dg_conv_b, fold(dg_ln_g), fold(dg_ln_b)], axis=1).reshape(-1, d)
    rep = _pad_rows(jnp.concatenate(rep_rows + [tail], axis=0), 16)
    (rep_hop,), _ = copies_start("rep_start", [[place_shard("place_rep", rep[None], 0, F32)]], _first_hop, 4)

    def pack_rep(a_mix_pre, a_mix_post, a_ffn_pre, a_ffn_post, a_scale, a_b, a_g, a_lb):
        tail_ = jnp.concatenate([a_scale, a_b, a_g, a_lb], axis=1).reshape(-1, d)
        return _pad_rows(jnp.concatenate([a_mix_pre, a_mix_post, a_ffn_pre, a_ffn_post, tail_], axis=0), 16)

    def upd(name, w, m, v, contribs):
        shape = w.shape
        flat2 = lambda a: a.reshape(-1, shape[-1])
        outs = adamw(name, flat2(w), flat2(m), flat2(v), contribs)
        return [o.reshape(shape) for o in outs]

    g_ff2 = [reduce_end("ff2_0", red_ff2_0, tok), reduce_end("ff2_1", red_ff2_1, tok)]
    o_ff2 = upd("adam_ffn_w2", ffn_w2, m_ffn_w2, v_ffn_w2, g_ff2)
    (rep_zone,) = copies_wait("rep_wait", rep_hop, _first_hop, o_ff2[0])
    (rep_hop,), _ = copies_start("rep_forward_start", [[rep_zone]], _second_hop, 3)
    g_ff1 = [reduce_end("ff1_0", red_ff1_0, o_ff2[0]), reduce_end("ff1_1", red_ff1_1, o_ff2[0])]
    o_ff1 = upd("adam_ffn_w1", ffn_w1, m_ffn_w1, v_ffn_w1, g_ff1)
    (rep_all,) = copies_wait("rep_forward_wait", rep_hop, _second_hop, o_ff1[0])
    o_rep = adamw("adam_replicated",
                  pack_rep(mix_pre_g, mix_post_g, ffn_pre_g, ffn_post_g, pool_scale, conv_b, conv_ln_g, conv_ln_b),
                  pack_rep(m_mix_pre_g, m_mix_post_g, m_ffn_pre_g, m_ffn_post_g, m_pool_scale, m_conv_b, m_conv_ln_g, m_conv_ln_b),
                  pack_rep(v_mix_pre_g, v_mix_post_g, v_ffn_pre_g, v_ffn_post_g, v_pool_scale, v_conv_b, v_conv_ln_g, v_conv_ln_b),
                  [rep_all])
    o_sc_out = upd("adam_sc_out", sc_w_out, m_sc_w_out, v_sc_w_out, [reduce_end("sc_out", red_sc_out, o_ff1[0])])
    o_sc_in = upd("adam_sc_in", sc_w_in, m_sc_w_in, v_sc_w_in, [reduce_end("sc_in", red_sc_in, o_sc_out[0])])
    o_ab_out = upd("adam_ab_out", ab_w_out, m_ab_w_out, v_ab_w_out, [reduce_end("ab_out", red_ab_out, o_sc_in[0])])
    o_small = adamw("adam_small", pack_small(conv_w, sc_conv_w, pool_w), pack_small(m_conv_w, m_sc_conv_w, m_pool_w),
                    pack_small(v_conv_w, v_sc_conv_w, v_pool_w), [reduce_end("small", red_small, o_ab_out[0])])
    o_ab_in = upd("adam_ab_in", ab_w_in, m_ab_w_in, v_ab_w_in, [reduce_end("ab_in", red_ab_in, o_small[0])])

    def unpack_small(o):
        return o[:r0].reshape(conv_w.shape), o[r0:r1].reshape(sc_conv_w.shape), o[r1:r2].reshape(pool_w.shape)

    def unpack_rep(o):
        tail_ = o[8:8 + tail.shape[0]].reshape(1, -1)
        n1 = d_pool
        return dict(mix_pre_g=o[0:2], mix_post_g=o[2:4], ffn_pre_g=o[4:6], ffn_post_g=o[6:8],
                    pool_scale=tail_[:, :n1], conv_b=tail_[:, n1:n1 + d_conv],
                    conv_ln_g=tail_[:, n1 + d_conv:n1 + 2 * d_conv], conv_ln_b=tail_[:, n1 + 2 * d_conv:n1 + 3 * d_conv])

    results = []
    for kind in range(4):
        rep_o = unpack_rep(o_rep[kind])
        s_conv, s_sconv, s_pool = unpack_small(o_small[kind])
        results.append([
            rep_o["mix_pre_g"], rep_o["mix_post_g"], rep_o["ffn_pre_g"], rep_o["ffn_post_g"],
            o_ab_in[kind], s_pool, rep_o["pool_scale"], s_conv, rep_o["conv_b"], rep_o["conv_ln_g"], rep_o["conv_ln_b"],
            o_ab_out[kind], o_sc_in[kind], s_sconv, o_sc_out[kind], o_ff1[kind], o_ff2[kind]])

    return (loss, grad_x[None], *results[0], *results[1], *results[2], *results[3])
```

```python
import jax
import jax.numpy as jnp
from jax import lax
from jax.experimental import pallas as pl
from jax.experimental.pallas import tpu as pltpu

F32 = jnp.float32
BF16 = jnp.bfloat16
MESH = pl.DeviceIdType.MESH
ANY = pl.BlockSpec(memory_space=pl.ANY)

NORM_EPS = 1e-6
POOL_WINDOWS = (2, 4, 8, 16)
ADAM_LR = 0.001
ADAM_B1 = 0.9
ADAM_B2 = 0.999
ADAM_EPS = 1e-08
ADAM_WD = 0.01
ADAM_STEP = 10

N_DEV = 8
VMEM_LIMIT = 56 * 1024 * 1024
PAIR_ADD_BLOCK = 1 << 20
MATMUL_ROWS = 2048
ROW_TILE = 256
CHANNEL_TILE = 256
TIME_CHUNK = 64
HALO = 32

NN = (((1,), (0,)), ((), ()))
NT = (((1,), (1,)), ((), ()))
TN = (((0,), (0,)), ((), ()))


def _params(sem):
    return pltpu.CompilerParams(dimension_semantics=sem, vmem_limit_bytes=VMEM_LIMIT)


def _place():
    x, y, c = lax.axis_index("x"), lax.axis_index("y"), lax.axis_index("c")
    return x, y, c


def _slot(px, py, pc):
    return 4 * px + 2 * py + pc


HBM = pl.BlockSpec(memory_space=pltpu.HBM)
SEM = pl.BlockSpec(memory_space=pltpu.SEMAPHORE)
EFFECT = pltpu.SideEffectType.DATAFLOW_SIDE_EFFECTING
TOKEN = jax.ShapeDtypeStruct((8, 128), F32)


def _in_hbm(a):
    return pltpu.with_memory_space_constraint(a, pltpu.HBM)


CHIPS = [(0, 0), (0, 1), (1, 0), (1, 1)]
N_CHIP = len(CHIPS)


def _chip(px, py):
    return 2 * px + py


def _first_hop(bufs, sends, recvs, waiting):
    (land,) = bufs
    x, y, c = _place()
    me = _slot(x, y, c)
    peers = [(x, y, 1 - c), (1 - x, y, c), (x, 1 - y, c), (1 - x, 1 - y, c)]
    return [pltpu.make_async_remote_copy(
        src_ref=land.at[me], dst_ref=land.at[_slot(*p) if waiting else me],
        send_sem=sends.at[k], recv_sem=recvs.at[k], device_id=p, device_id_type=MESH) for k, p in enumerate(peers)]


def _second_hop(bufs, sends, recvs, waiting):
    (land,) = bufs
    x, y, c = _place()
    return [pltpu.make_async_remote_copy(
        src_ref=land.at[_slot(px, py, c)], dst_ref=land.at[_slot(px, py, 1 - c if waiting else c)],
        send_sem=sends.at[k], recv_sem=recvs.at[k], device_id=(x, y, 1 - c), device_id_type=MESH)
        for k, (px, py) in enumerate([(1 - x, y), (x, 1 - y), (1 - x, 1 - y)])]


def _ring_hop1(bufs, sends, recvs, waiting):
    (land,) = bufs
    x, y, c = _place()
    me = _slot(x, y, c)
    peers = [(1 - x, y, c), (x, 1 - y, c), (x, y, 1 - c)]
    return [pltpu.make_async_remote_copy(
        src_ref=land.at[me], dst_ref=land.at[_slot(*p) if waiting else me],
        send_sem=sends.at[k], recv_sem=recvs.at[k], device_id=p, device_id_type=MESH) for k, p in enumerate(peers)]


def _ring_hop2(bufs, sends, recvs, waiting):
    (land,) = bufs
    x, y, c = _place()
    half = land.shape[1] // 2
    first, second = pl.ds(0, half), pl.ds(half, half)
    nx, ny, diag = _slot(1 - x, y, c), _slot(x, 1 - y, c), _slot(1 - x, 1 - y, c)
    plan = [
        (land.at[ny, first], land.at[diag, first], (1 - x, y, c)),
        (land.at[nx, second], land.at[diag, second], (x, 1 - y, c)),
        (land.at[nx], land.at[_slot(1 - x, y, 1 - c)], (x, y, 1 - c)),
        (land.at[ny], land.at[_slot(x, 1 - y, 1 - c)], (x, y, 1 - c))]
    return [pltpu.make_async_remote_copy(
        src_ref=src, dst_ref=mine if waiting else src, send_sem=sends.at[k], recv_sem=recvs.at[k],
        device_id=to, device_id_type=MESH) for k, (src, mine, to) in enumerate(plan)]


def _ring_hop3(bufs, sends, recvs, waiting):
    (land,) = bufs
    x, y, c = _place()
    return [pltpu.make_async_remote_copy(
        src_ref=land.at[_slot(1 - x, 1 - y, c)], dst_ref=land.at[_slot(1 - x, 1 - y, 1 - c if waiting else c)],
        send_sem=sends.at[0], recv_sem=recvs.at[0], device_id=(x, y, 1 - c), device_id_type=MESH)]


def _pair_hop(bufs, sends, recvs, waiting):
    g, land = bufs
    x, y, c = _place()
    return [pltpu.make_async_remote_copy(
        src_ref=g.at[_slot(qx, qy, 1 - c)], dst_ref=land.at[q],
        send_sem=sends.at[q], recv_sem=recvs.at[q], device_id=(x, y, 1 - c), device_id_type=MESH)
        for q, (qx, qy) in enumerate(CHIPS)]


def _chip_hop(bufs, sends, recvs, waiting):
    p, land = bufs
    x, y, c = _place()
    return [pltpu.make_async_remote_copy(
        src_ref=p.at[_chip(px, py)], dst_ref=land.at[_chip(px, py) if waiting else _chip(x, y)],
        send_sem=sends.at[k], recv_sem=recvs.at[k], device_id=(px, py, c), device_id_type=MESH)
        for k, (px, py) in enumerate([(1 - x, y), (x, 1 - y), (1 - x, 1 - y)])]


def copies_start(name, groups, hop, n_copies, deps=()):
    flat = [b for grp in groups for b in grp]
    nb, ng = len(flat), len(groups)
    deps = list(deps)
    hops = list(hop) if isinstance(hop, (list, tuple)) else [hop] * ng
    counts = list(n_copies) if isinstance(n_copies, (list, tuple)) else [n_copies] * ng

    def body(*refs):
        ins, token = refs[:nb], refs[-1]
        sems = refs[nb + len(deps):nb + len(deps) + 2 * ng]
        i = 0
        for gi, grp in enumerate(groups):
            for cp in hops[gi](ins[i:i + len(grp)], sems[2 * gi], sems[2 * gi + 1], False):
                cp.start()
            i += len(grp)
        token[...] = jnp.zeros_like(token)

    outs = pl.pallas_call(
        body, name=name,
        out_shape=([pltpu.SemaphoreType.DMA((n,)) for n in counts for _ in range(2)]
                   + [pltpu.HBM(b.shape, b.dtype) for b in flat] + [TOKEN]),
        in_specs=[HBM] * nb + [ANY] * len(deps),
        out_specs=[SEM] * (2 * ng) + [HBM] * nb + [pl.BlockSpec(memory_space=pltpu.VMEM)],
        input_output_aliases={i: 2 * ng + i for i in range(nb)},
        compiler_params=pltpu.CompilerParams(has_side_effects=EFFECT),
    )(*[_in_hbm(b) for b in flat], *deps)
    started, i = [], 0
    for gi, grp in enumerate(groups):
        started.append((outs[2 * gi], outs[2 * gi + 1], list(outs[2 * ng + i:2 * ng + i + len(grp)])))
        i += len(grp)
    return started, outs[-1]


def copies_wait(name, started, hop, after):
    sends, recvs, bufs = started
    nb = len(bufs)

    def body(*refs):
        for cp in hop(refs[:nb], refs[nb], refs[nb + 1], True):
            cp.wait_send()
            cp.wait_recv()

    outs = pl.pallas_call(
        body, name=name,
        out_shape=[pltpu.HBM(b.shape, b.dtype) for b in bufs],
        in_specs=[HBM] * nb + [SEM, SEM, ANY], out_specs=[HBM] * nb,
        input_output_aliases={i: i for i in range(nb)},
        compiler_params=pltpu.CompilerParams(has_side_effects=EFFECT),
    )(*bufs, sends, recvs, after)
    return list(outs)


def place_shard(name, w, layer, dtype, deps=()):
    _, r, c = w.shape
    tr = _tile(r, 1024)
    x, y, core = _place()
    me = _slot(x, y, core).astype(jnp.int32).reshape(1)

    def body(me_ref, w_ref, *rest):
        rest[-1][...] = w_ref[...].astype(dtype)

    return pl.pallas_call(
        body, name=name,
        grid_spec=pltpu.PrefetchScalarGridSpec(
            num_scalar_prefetch=1, grid=(r // tr,),
            in_specs=[pl.BlockSpec((None, tr, c), lambda i, me_ref: (layer, i, 0))] + [ANY] * len(deps),
            out_specs=pl.BlockSpec((None, tr, c), lambda i, me_ref: (me_ref[0], i, 0))),
        out_shape=jax.ShapeDtypeStruct((N_DEV, r, c), dtype),
        compiler_params=_params(("parallel",)),
    )(me, w, *deps)


def tie(name, x, *deps):
    def body(*refs):
        del refs

    return pl.pallas_call(
        body, name=name, out_shape=jax.ShapeDtypeStruct(x.shape, x.dtype),
        in_specs=[ANY] * (1 + len(deps)), out_specs=ANY, input_output_aliases={0: 0},
    )(x, *deps)


def pair_add(name, g, from_sibling):
    _, r, c_dim = g.shape
    tr = r
    while tr * c_dim > PAIR_ADD_BLOCK and tr % 16 == 0:
        tr //= 2
    x, y, core = _place()
    where = jnp.stack([core, _chip(x, y)]).astype(jnp.int32)

    def body(where_ref, g_ref, s_ref, o_ref, zone_ref):
        total = (g_ref[...].astype(F32) + s_ref[...].astype(F32)).astype(o_ref.dtype)
        o_ref[...] = total

        @pl.when(pl.program_id(1) == where_ref[1])
        def _():
            zone_ref[...] = total

    blk = pl.BlockSpec((None, tr, c_dim), lambda i, q, where_ref: (q, i, 0))
    return pl.pallas_call(
        body, name=name,
        grid_spec=pltpu.PrefetchScalarGridSpec(
            num_scalar_prefetch=1, grid=(r // tr, N_CHIP),
            in_specs=[pl.BlockSpec((None, None, tr, c_dim), lambda i, q, where_ref: (q, where_ref[0], i, 0)), blk],
            out_specs=[blk, pl.BlockSpec((None, tr, c_dim), lambda i, q, where_ref: (where_ref[1], i, 0))]),
        out_shape=[jax.ShapeDtypeStruct((N_CHIP, r, c_dim), g.dtype)] * 2,
        compiler_params=_params(("parallel", "arbitrary")),
    )(where, g.reshape(N_CHIP, 2, r, c_dim), from_sibling)


def _matmul(name, lhs, rhs, *, out_shape, out_dtype, grid, lhs_spec, rhs_spec, out_spec, dims, acc_shape,
            lhs_fn=None, extra=(), extra_specs=(), epilogue=None, parts=1):
    nk = grid[2]
    n_extra = len(extra)

    def body(*refs):
        lhs_ref, rhs_ref = refs[0], refs[1]
        extra_refs = refs[2:2 + n_extra]
        out_ref = refs[2 + n_extra]

        def product():
            if parts == 1:
                a = lhs_ref[...]
                if lhs_fn is not None:
                    a = lhs_fn(a)
                return lax.dot_general(a, rhs_ref[...], dims, preferred_element_type=F32)
            width = lhs_ref.shape[1] // parts
            total = None
            for b in range(parts):
                term = lax.dot_general(lhs_ref[:, b * width:(b + 1) * width], rhs_ref[b], dims,
                                       preferred_element_type=F32)
                total = term if total is None else total + term
            return total

        def finish(r):
            if epilogue is not None:
                r = epilogue(r, *[e[...] for e in extra_refs])
            out_ref[...] = r.astype(out_dtype)

        if nk == 1:
            finish(product())
        else:
            acc_ref = refs[3 + n_extra]
            k = pl.program_id(2)

            @pl.when(k == 0)
            def _():
                acc_ref[...] = product()

            @pl.when(jnp.logical_and(k > 0, k < nk - 1))
            def _():
                acc_ref[...] += product()

            @pl.when(k == nk - 1)
            def _():
                finish(acc_ref[...] + product())

    return pl.pallas_call(
        body, name=name, grid=grid,
        out_shape=jax.ShapeDtypeStruct(out_shape, out_dtype),
        in_specs=[lhs_spec, rhs_spec, *extra_specs], out_specs=out_spec,
        scratch_shapes=[pltpu.VMEM(acc_shape, F32)] if nk > 1 else [],
        compiler_params=_params(("parallel", "parallel", "arbitrary")),
    )(lhs, rhs, *extra)


def _tile(n, want):
    return want if n % want == 0 else n


def mm_nn(name, x, w, *, out_dtype, tn=512, tk=None, lhs_fn=None, epilogue=None):
    t, kdim = x.shape
    n = w.shape[1]
    tm, tn = _tile(t, MATMUL_ROWS), _tile(n, tn)
    tk = kdim if tk is None else _tile(kdim, tk)
    return _matmul(
        name, x, w, out_shape=(t, n), out_dtype=out_dtype, grid=(t // tm, n // tn, kdim // tk),
        lhs_spec=pl.BlockSpec((tm, tk), lambda i, j, k: (i, k)),
        rhs_spec=pl.BlockSpec((tk, tn), lambda i, j, k: (k, j)),
        out_spec=pl.BlockSpec((tm, tn), lambda i, j, k: (i, j)),
        dims=NN, acc_shape=(tm, tn), lhs_fn=lhs_fn, epilogue=epilogue)


def mm_nn_blocked(name, x, w, *, out_dtype, epilogue=None):
    t, kdim = x.shape
    nb = w.shape[2]
    tm = _tile(t, MATMUL_ROWS)
    tn = nb // 2 if nb >= 1024 else nb
    sub = nb // tn
    return _matmul(
        name, x, w, out_shape=(t, N_DEV * nb), out_dtype=out_dtype, grid=(t // tm, N_DEV * sub, 1),
        lhs_spec=pl.BlockSpec((tm, kdim), lambda i, j, k: (i, k)),
        rhs_spec=pl.BlockSpec((None, kdim, tn), lambda i, j, k: (j // sub, k, j % sub)),
        out_spec=pl.BlockSpec((tm, tn), lambda i, j, k: (i, j)),
        dims=NN, acc_shape=(tm, tn), epilogue=epilogue)


def mm_nt_blocked(name, dz, w, *, out_dtype, tn=512):
    t = dz.shape[0]
    kdim, nb = w.shape[1], w.shape[2]
    tm, tn = _tile(t, MATMUL_ROWS), _tile(kdim, tn)
    parts = 2
    return _matmul(
        name, dz, w, out_shape=(t, kdim), out_dtype=out_dtype, grid=(t // tm, kdim // tn, N_DEV // parts),
        lhs_spec=pl.BlockSpec((tm, parts * nb), lambda i, j, k: (i, k)),
        rhs_spec=pl.BlockSpec((parts, tn, nb), lambda i, j, k: (k, j, 0)),
        out_spec=pl.BlockSpec((tm, tn), lambda i, j, k: (i, j)),
        dims=NT, acc_shape=(tm, tn), parts=parts)


def mm_tn_blocked(name, x, dz, nb, *, out_dtype, tk=1024):
    t, kdim = x.shape
    tk = _tile(kdim, tk)
    return _matmul(
        name, x, dz, out_shape=(N_DEV, kdim, nb), out_dtype=out_dtype, grid=(kdim // tk, N_DEV, 1),
        lhs_spec=pl.BlockSpec((t, tk), lambda i, j, k: (k, i)),
        rhs_spec=pl.BlockSpec((t, nb), lambda i, j, k: (k, j)),
        out_spec=pl.BlockSpec((None, tk, nb), lambda i, j, k: (j, i, 0)),
        dims=TN, acc_shape=(tk, nb))


def mm_bwd_pair(name, dy, w, act, *, out_dtype, tile=512, act_fn=None, epilogue=None):
    t, n = dy.shape
    kdim = w.shape[0]
    tile = _tile(kdim, tile)

    def body(dy_ref, w_ref, act_ref, dx_ref, dw_ref):
        a = act_ref[...]
        dx = lax.dot_general(dy_ref[...], w_ref[...], NT, preferred_element_type=F32)
        if epilogue is not None:
            dx = epilogue(dx, a)
        dx_ref[...] = dx.astype(out_dtype)
        if act_fn is not None:
            a = act_fn(a)
        dw_ref[...] = lax.dot_general(a, dy_ref[...], TN, preferred_element_type=F32).astype(out_dtype)

    return pl.pallas_call(
        body, name=name, grid=(kdim // tile,),
        in_specs=[pl.BlockSpec((t, n), lambda j: (0, 0)), pl.BlockSpec((tile, n), lambda j: (j, 0)),
                  pl.BlockSpec((t, tile), lambda j: (0, j))],
        out_specs=[pl.BlockSpec((t, tile), lambda j: (0, j)), pl.BlockSpec((tile, n), lambda j: (j, 0))],
        out_shape=[jax.ShapeDtypeStruct((t, kdim), out_dtype), jax.ShapeDtypeStruct((kdim, n), out_dtype)],
        compiler_params=_params(("parallel",)),
    )(dy, w, act)


def _rstd(v):
    return lax.rsqrt(jnp.mean(v * v, axis=-1, keepdims=True) + NORM_EPS)


def _rms_bwd(v, g, dy):
    r = _rstd(v)
    vhat = v * r
    dvh = dy * g
    dv = r * (dvh - vhat * jnp.mean(dvh * vhat, axis=-1, keepdims=True))
    return dv, dy * vhat


def _fold8(v):
    rows, n = v.shape
    return jnp.sum(v.reshape(rows // 8, 8, n), axis=0)


def _fold_lanes(v):
    out = v[:, 0:128]
    for i in range(1, v.shape[1] // 128):
        out = out + v[:, 128 * i:128 * (i + 1)]
    return out


def _accumulate(ref, v):
    i = pl.program_id(0)

    @pl.when(i == 0)
    def _():
        ref[...] = v

    @pl.when(i > 0)
    def _():
        ref[...] += v


def _row_call(body, name, t, ins, row_in, outs, acc_outs=(), tr=ROW_TILE):
    tr = _tile(t, tr)

    def in_spec(a, tiled):
        if isinstance(tiled, tuple):
            width, j = tiled
            return pl.BlockSpec((tr, width), lambda i: (i, j))
        return pl.BlockSpec((tr, a.shape[1]), lambda i: (i, 0)) if tiled else pl.BlockSpec(a.shape, lambda i: (0, 0))

    in_specs = [in_spec(a, tiled) for a, tiled in zip(ins, row_in)]
    out_specs = [pl.BlockSpec((tr, n), lambda i: (i, 0)) for n, _ in outs]
    out_specs += [pl.BlockSpec((8, n), lambda i: (0, 0)) for n in acc_outs]
    out_shape = [jax.ShapeDtypeStruct((t, n), dt) for n, dt in outs]
    out_shape += [jax.ShapeDtypeStruct((8, n), F32) for n in acc_outs]
    return pl.pallas_call(
        body, name=name, grid=(t // tr,), in_specs=in_specs, out_specs=out_specs, out_shape=out_shape,
        compiler_params=_params(("arbitrary",) if acc_outs else ("parallel",)),
    )(*ins)


def norm_pre(name, x, g):
    t, d = x.shape

    def body(x_ref, g_ref, h_ref):
        v = x_ref[...]
        h_ref[...] = (v * _rstd(v) * g_ref[...]).astype(BF16)

    return _row_call(body, name, t, [x, g], [True, False], [(d, BF16)])[0]


def post_pre(name, x, m, g_post, g_pre):
    t, d = x.shape

    def body(x_ref, m_ref, gp_ref, gn_ref, xo_ref, h_ref):
        mv = m_ref[...]
        xn = x_ref[...] + mv * _rstd(mv) * gp_ref[...]
        xo_ref[...] = xn
        h_ref[...] = (xn * _rstd(xn) * gn_ref[...]).astype(BF16)

    return _row_call(body, name, t, [x, m, g_post, g_pre], [True, True, False, False], [(d, F32), (d, BF16)])


def post_loss(name, x, f, g_post, target):
    t, d = x.shape

    def body(x_ref, f_ref, g_ref, t_ref, dx_ref, df_ref, loss_ref, dg_ref):
        fv = f_ref[...]
        g = g_ref[...]
        out = x_ref[...] + fv * _rstd(fv) * g
        err = out - t_ref[...]
        dx = err * (1.0 / d)
        dx_ref[...] = dx
        dfv, dg_rows = _rms_bwd(fv, g, dx)
        df_ref[...] = dfv.astype(BF16)
        _accumulate(loss_ref, _fold8(_fold_lanes(err * err)))
        _accumulate(dg_ref, _fold8(dg_rows))

    return _row_call(body, name, t, [x, f, g_post, target], [True, True, False, True],
                     [(d, F32), (d, BF16)], acc_outs=(128, d))


def bwd_pre_post(name, dx_out, x_in, g_pre, dh, f_prev, g_post_prev):
    t, d = x_in.shape

    def body(dxo_ref, x_ref, gpre_ref, dh_ref, f_ref, gpost_ref, dxi_ref, df_ref, dgpre_ref, dgpost_ref):
        dxv, dgpre_rows = _rms_bwd(x_ref[...], gpre_ref[...], dh_ref[...].astype(F32))
        dxi = dxo_ref[...] + dxv
        dxi_ref[...] = dxi
        dfv, dgpost_rows = _rms_bwd(f_ref[...], gpost_ref[...], dxi)
        df_ref[...] = dfv.astype(BF16)
        _accumulate(dgpre_ref, _fold8(dgpre_rows))
        _accumulate(dgpost_ref, _fold8(dgpost_rows))

    return _row_call(body, name, t, [dx_out, x_in, g_pre, dh, f_prev, g_post_prev],
                     [True, True, False, True, True, False], [(d, F32), (d, BF16)], acc_outs=(d, d))


def bwd_pre_final(name, dx_out, x_in, g_pre, dh):
    t, d = x_in.shape

    def body(dxo_ref, x_ref, gpre_ref, dh_ref, dxi_ref, dgpre_ref):
        dxv, dgpre_rows = _rms_bwd(x_ref[...], gpre_ref[...], dh_ref[...].astype(F32))
        dxi_ref[...] = dxo_ref[...] + dxv
        _accumulate(dgpre_ref, _fold8(dgpre_rows))

    return _row_call(body, name, t, [dx_out, x_in, g_pre, dh], [True, True, False, True], [(d, F32)], acc_outs=(d,))


def _layer_norm_parts(cv):
    mu = jnp.mean(cv, axis=-1, keepdims=True)
    xc = cv - mu
    rstd = lax.rsqrt(jnp.mean(xc * xc, axis=-1, keepdims=True) + NORM_EPS)
    return xc * rstd, rstd


def ln_silu(name, cv, g, b, y, y_block):
    t, n = cv.shape
    tr = _tile(t, ROW_TILE)

    def body(c_ref, g_ref, b_ref, y_in_ref, y_ref):
        chat, _ = _layer_norm_parts(c_ref[...])
        ln = chat * g_ref[...] + b_ref[...]
        y_ref[...] = (ln * jax.nn.sigmoid(ln)).astype(BF16)

    vec = pl.BlockSpec((1, n), lambda i: (0, 0))
    return pl.pallas_call(
        body, name=name, grid=(t // tr,),
        in_specs=[pl.BlockSpec((tr, n), lambda i: (i, 0)), vec, vec, ANY],
        out_specs=pl.BlockSpec((tr, n), lambda i: (i, y_block)),
        out_shape=jax.ShapeDtypeStruct(y.shape, y.dtype), input_output_aliases={3: 0},
        compiler_params=_params(("parallel",)),
    )(cv, g, b, y)


def ln_silu_bwd(name, cv, g, b, dy, dy_block):
    t, n = cv.shape

    def body(c_ref, g_ref, b_ref, dy_ref, dc_ref, dg_ref, db_ref):
        chat, rstd = _layer_norm_parts(c_ref[...])
        g = g_ref[...]
        ln = chat * g + b_ref[...]
        s = jax.nn.sigmoid(ln)
        dln = dy_ref[...].astype(F32) * (s * (1.0 + ln * (1.0 - s)))
        dchat = dln * g
        dc_ref[...] = rstd * (dchat - jnp.mean(dchat, axis=-1, keepdims=True)
                              - chat * jnp.mean(dchat * chat, axis=-1, keepdims=True))
        _accumulate(dg_ref, _fold8(dln * chat))
        _accumulate(db_ref, _fold8(dln))

    return _row_call(body, name, t, [cv, g, b, dy], [True, False, False, (n, dy_block)], [(n, F32)], acc_outs=(n, n))


def _chunks(t, fn, tc=TIME_CHUNK):
    tc = _tile(t, tc)

    def step(i, carry):
        fn(pl.multiple_of(i * tc, tc), tc)
        return carry

    lax.fori_loop(0, t // tc, step, 0)


def _rows_from(v, start, n):
    res = start % 8
    base = v if res == 0 else pltpu.roll(v, v.shape[0] - res, axis=0)
    return base[start - res:start - res + n, :]


def _shifted(window, offsets, tc):
    rows = window.shape[0]
    by_residue = {}
    for k, off in enumerate(offsets):
        by_residue.setdefault(off % 8, []).append((k, off))
    for res, taps in by_residue.items():
        base = window if res == 0 else pltpu.roll(window, rows - res, axis=0)
        for k, off in taps:
            yield k, base[off - res:off - res + tc, :]


def _taps(window, w_ref, offsets, tc, flip=False):
    acc = None
    for k, rows in _shifted(window, offsets, tc):
        kk = len(offsets) - 1 - k if flip else k
        term = w_ref[kk:kk + 1, :] * rows
        acc = term if acc is None else acc + term
    return acc


def _window_sums(win, tc, causal):
    sums = []
    cur, rows, step = win, tc + HALO, 1
    for _ in POOL_WINDOWS:
        rows -= 8
        if causal:
            cur = cur[8:8 + rows, :] + _rows_from(cur, 8 - step, rows)
            sums.append(cur[rows - tc:rows, :])
        else:
            cur = cur[0:rows, :] + _rows_from(cur, step, rows)
            sums.append(cur[0:tc, :])
        step *= 2
    return sums


def _pick(vals, g):
    out = vals[-1]
    for i in range(len(vals) - 2, -1, -1):
        out = jnp.where(g == i, vals[i], out)
    return out


def _pool_count(s, tc, g):
    t1 = (lax.broadcasted_iota(jnp.int32, (tc, 1), 0) + (s + 1)).astype(F32)
    width = _pick([float(w) for w in POOL_WINDOWS], g)
    return jnp.minimum(t1, width)


def pool_fwd(name, z, pool_w, pool_scale, d_pool, y_width):
    t = z.shape[0]
    ng, pg = pool_w.shape[0], pool_w.shape[1]

    def body(u_ref, w_ref, s_ref, pooled_ref, y_ref, pad):
        g = pl.program_id(0)
        pad[pl.ds(0, HALO), :] = jnp.zeros((HALO, pg), F32)

        def fill(s, tc):
            pad[pl.ds(HALO + s, tc), :] = u_ref[pl.ds(s, tc), :].astype(F32)

        def chunk(s, tc):
            win = pad[pl.ds(s, tc + HALO), :]
            total = _pick(_window_sums(win, tc, causal=True), g)
            pooled = total / _pool_count(s, tc, g) - win[HALO:HALO + tc, :]
            pooled_ref[pl.ds(s, tc), :] = pooled.astype(BF16)

        _chunks(t, fill)
        _chunks(t, chunk)
        mixed = jnp.dot(pooled_ref[...], w_ref[...], preferred_element_type=F32)
        y_ref[...] = (mixed * s_ref[...]).astype(BF16)

    col = pl.BlockSpec((t, pg), lambda g: (0, g))
    return pl.pallas_call(
        body, name=name, grid=(ng,),
        in_specs=[col, pl.BlockSpec((None, pg, pg), lambda g: (g, 0, 0)), pl.BlockSpec((1, pg), lambda g: (0, g))],
        out_specs=[col, col],
        out_shape=[jax.ShapeDtypeStruct((t, d_pool), BF16), jax.ShapeDtypeStruct((t, y_width), BF16)],
        scratch_shapes=[pltpu.VMEM((t + HALO, pg), F32)],
        compiler_params=_params(("parallel",)),
    )(z, pool_w, pool_scale)


def pool_bwd(name, pooled, dy, pool_w, pool_scale, dz):
    t, d_pool = pooled.shape
    ng, pg = pool_w.shape[0], pool_w.shape[1]

    def body(p_ref, dy_ref, w_ref, s_ref, dz_ref, du_ref, dw_ref, ds_ref, pad):
        g = pl.program_id(0)
        w = w_ref[...]
        dyv = dy_ref[...].astype(F32)
        mixed = jnp.dot(p_ref[...], w, preferred_element_type=F32)
        ds_ref[...] = jnp.sum(dyv * mixed, axis=0, keepdims=True)
        dmixed = (dyv * s_ref[...]).astype(BF16)
        dw_ref[...] = lax.dot_general(p_ref[...], dmixed, TN, preferred_element_type=F32)
        pad[...] = jnp.zeros((t + HALO, pg), F32)
        pad[pl.ds(0, t), :] = lax.dot_general(dmixed, w, NT, preferred_element_type=F32)

        def scale(s, tc):
            pad[pl.ds(s, tc), :] = pad[pl.ds(s, tc), :] / _pool_count(s, tc, g)

        def chunk(s, tc):
            win = pad[pl.ds(s, tc + HALO), :]
            total = _pick(_window_sums(win, tc, causal=False), g)
            du_ref[pl.ds(s, tc), :] = (total - win[0:tc, :] * _pool_count(s, tc, g)).astype(BF16)

        _chunks(t, scale)
        _chunks(t, chunk)

    col = pl.BlockSpec((t, pg), lambda g: (0, g))
    vec = pl.BlockSpec((1, pg), lambda g: (0, g))
    mat = pl.BlockSpec((None, pg, pg), lambda g: (g, 0, 0))
    return pl.pallas_call(
        body, name=name, grid=(ng,),
        in_specs=[col, col, mat, vec, ANY], out_specs=[col, mat, vec],
        out_shape=[jax.ShapeDtypeStruct(dz.shape, dz.dtype), jax.ShapeDtypeStruct((ng, pg, pg), F32),
                   jax.ShapeDtypeStruct((1, d_pool), F32)],
        input_output_aliases={4: 0},
        scratch_shapes=[pltpu.VMEM((t + HALO, pg), F32)],
        compiler_params=_params(("parallel",)),
    )(pooled, dy, pool_w, pool_scale, dz)


def conv_fwd(name, z, conv_w, conv_b, d_pool, d_conv):
    t = z.shape[0]
    kw = conv_w.shape[0]
    tc_ch = _tile(d_conv, CHANNEL_TILE)
    v0, g0 = d_pool // tc_ch, (d_pool + d_conv) // tc_ch

    def body(v_ref, g_ref, w_ref, b_ref, c_ref, pad):
        pad[pl.ds(0, HALO), :] = jnp.zeros((HALO, tc_ch), F32)

        def fill(s, tc):
            pad[pl.ds(HALO + s, tc), :] = v_ref[pl.ds(s, tc), :].astype(F32) * jax.nn.sigmoid(g_ref[pl.ds(s, tc), :].astype(F32))

        def chunk(s, tc):
            win = pad[pl.ds(s, tc + HALO), :]
            c_ref[pl.ds(s, tc), :] = _taps(win, w_ref, [HALO - (kw - 1) + k for k in range(kw)], tc) + b_ref[...]

        _chunks(t, fill)
        _chunks(t, chunk)

    return pl.pallas_call(
        body, name=name, grid=(d_conv // tc_ch,),
        in_specs=[pl.BlockSpec((t, tc_ch), lambda j: (0, v0 + j)), pl.BlockSpec((t, tc_ch), lambda j: (0, g0 + j)),
                  pl.BlockSpec((kw, tc_ch), lambda j: (0, j)), pl.BlockSpec((1, tc_ch), lambda j: (0, j))],
        out_specs=pl.BlockSpec((t, tc_ch), lambda j: (0, j)),
        out_shape=jax.ShapeDtypeStruct((t, d_conv), F32),
        scratch_shapes=[pltpu.VMEM((t + HALO, tc_ch), F32)],
        compiler_params=_params(("parallel",)),
    )(z, z, conv_w, conv_b)


def conv_bwd(name, z, dc, conv_w, d_pool, d_conv):
    t = z.shape[0]
    kw = conv_w.shape[0]
    tc_ch = _tile(d_conv, CHANNEL_TILE)
    v0, g0 = d_pool // tc_ch, (d_pool + d_conv) // tc_ch

    def body(v_ref, g_ref, dc_ref, w_ref, dz_ref, dw_ref, db_ref, pad_a, pad_dc, acc_w, acc_b, tiles, sems):
        j = pl.program_id(0)
        dv_ref, dg_ref = tiles.at[0], tiles.at[1]
        writes = [pltpu.make_async_copy(tiles.at[p], dz_ref.at[:, pl.ds((first + j) * tc_ch, tc_ch)], sems.at[p])
                  for p, first in enumerate([v0, g0])]

        def wait_writes():
            for cp in writes:
                cp.wait()

        pad_a[pl.ds(0, HALO), :] = jnp.zeros((HALO, tc_ch), F32)
        pad_dc[pl.ds(t, HALO), :] = jnp.zeros((HALO, tc_ch), F32)
        acc_w[...] = jnp.zeros_like(acc_w)
        acc_b[...] = jnp.zeros_like(acc_b)

        def fill(s, tc):
            pad_a[pl.ds(HALO + s, tc), :] = v_ref[pl.ds(s, tc), :].astype(F32) * jax.nn.sigmoid(g_ref[pl.ds(s, tc), :].astype(F32))
            pad_dc[pl.ds(s, tc), :] = dc_ref[pl.ds(s, tc), :]

        def chunk(s, tc):
            dcv = pad_dc[pl.ds(s, tc), :]
            win_a = pad_a[pl.ds(s, tc + HALO), :]
            for k, rows in _shifted(win_a, [HALO - (kw - 1) + k for k in range(kw)], tc):
                acc_w[pl.ds(8 * k, 8), :] += _fold8(dcv * rows)
            acc_b[...] += _fold8(dcv)
            da = _taps(pad_dc[pl.ds(s, tc + HALO), :], w_ref, list(range(kw)), tc, flip=True)
            vv = v_ref[pl.ds(s, tc), :].astype(F32)
            sg = jax.nn.sigmoid(g_ref[pl.ds(s, tc), :].astype(F32))
            dv_ref[pl.ds(s, tc), :] = (da * sg).astype(BF16)
            dg_ref[pl.ds(s, tc), :] = (da * vv * sg * (1.0 - sg)).astype(BF16)

        _chunks(t, fill)
        pl.when(j > 0)(wait_writes)
        _chunks(t, chunk)
        for cp in writes:
            cp.start()
        pl.when(j == n_tiles - 1)(wait_writes)
        for k in range(kw):
            dw_ref[k:k + 1, :] = jnp.sum(acc_w[pl.ds(8 * k, 8), :], axis=0, keepdims=True)
        db_ref[...] = jnp.sum(acc_b[...], axis=0, keepdims=True)

    n_tiles = d_conv // tc_ch
    return pl.pallas_call(
        body, name=name, grid=(n_tiles,),
        in_specs=[pl.BlockSpec((t, tc_ch), lambda j: (0, v0 + j)), pl.BlockSpec((t, tc_ch), lambda j: (0, g0 + j)),
                  pl.BlockSpec((t, tc_ch), lambda j: (0, j)), pl.BlockSpec((kw, tc_ch), lambda j: (0, j))],
        out_specs=[ANY, pl.BlockSpec((kw, tc_ch), lambda j: (0, j)), pl.BlockSpec((1, tc_ch), lambda j: (0, j))],
        out_shape=[jax.ShapeDtypeStruct((t, d_pool + 2 * d_conv), BF16),
                   jax.ShapeDtypeStruct((kw, d_conv), F32), jax.ShapeDtypeStruct((1, d_conv), F32)],
        scratch_shapes=[pltpu.VMEM((t + HALO, tc_ch), F32), pltpu.VMEM((t + HALO, tc_ch), F32),
                        pltpu.VMEM((8 * kw, tc_ch), F32), pltpu.VMEM((8, tc_ch), F32),
                        pltpu.VMEM((2, t, tc_ch), BF16), pltpu.SemaphoreType.DMA((2,))],
        compiler_params=_params(("arbitrary",)),
    )(z, z, dc, conv_w)


def short_fwd(name, z, conv_w, d_short):
    t = z.shape[0]
    kw = conv_w.shape[0]
    tc_ch = _tile(d_short, CHANNEL_TILE)
    nt = d_short // tc_ch

    def body(b_ref, c_ref, u_ref, w_ref, y_ref, pad):
        pad[pl.ds(0, HALO), :] = jnp.zeros((HALO, tc_ch), F32)

        def fill(s, tc):
            pad[pl.ds(HALO + s, tc), :] = c_ref[pl.ds(s, tc), :].astype(F32) * u_ref[pl.ds(s, tc), :].astype(F32)

        def chunk(s, tc):
            win = pad[pl.ds(s, tc + HALO), :]
            cq = _taps(win, w_ref, [HALO - (kw - 1) + k for k in range(kw)], tc)
            y_ref[pl.ds(s, tc), :] = (b_ref[pl.ds(s, tc), :].astype(F32) * cq).astype(BF16)

        _chunks(t, fill)
        _chunks(t, chunk)

    return pl.pallas_call(
        body, name=name, grid=(nt,),
        in_specs=[pl.BlockSpec((t, tc_ch), lambda j: (0, j)), pl.BlockSpec((t, tc_ch), lambda j: (0, nt + j)),
                  pl.BlockSpec((t, tc_ch), lambda j: (0, 2 * nt + j)), pl.BlockSpec((kw, tc_ch), lambda j: (0, j))],
        out_specs=pl.BlockSpec((t, tc_ch), lambda j: (0, j)),
        out_shape=jax.ShapeDtypeStruct((t, d_short), BF16),
        scratch_shapes=[pltpu.VMEM((t + HALO, tc_ch), F32)],
        compiler_params=_params(("parallel",)),
    )(z, z, z, conv_w)


def short_bwd(name, z, dy, conv_w, d_short):
    t = z.shape[0]
    kw = conv_w.shape[0]
    tc_ch = _tile(d_short, CHANNEL_TILE)
    nt = d_short // tc_ch

    def body(b_ref, c_ref, u_ref, dy_ref, w_ref, dz_ref, dw_ref, pad_q, pad_dcq, acc_w, tiles, sems):
        j = pl.program_id(0)
        db_ref, dcg_ref, du_ref = tiles.at[0], tiles.at[1], tiles.at[2]
        writes = [pltpu.make_async_copy(tiles.at[p], dz_ref.at[:, pl.ds((p * nt + j) * tc_ch, tc_ch)], sems.at[p])
                  for p in range(3)]

        def wait_writes():
            for cp in writes:
                cp.wait()

        pad_q[pl.ds(0, HALO), :] = jnp.zeros((HALO, tc_ch), F32)
        pad_dcq[pl.ds(t, HALO), :] = jnp.zeros((HALO, tc_ch), F32)
        acc_w[...] = jnp.zeros_like(acc_w)

        def fill(s, tc):
            rows = pl.ds(s, tc)
            pad_q[pl.ds(HALO + s, tc), :] = c_ref[rows, :].astype(F32) * u_ref[rows, :].astype(F32)
            pad_dcq[rows, :] = dy_ref[rows, :].astype(F32) * b_ref[rows, :].astype(F32)

        def chunk(s, tc):
            rows = pl.ds(s, tc)
            win_q = pad_q[pl.ds(s, tc + HALO), :]
            dcq = pad_dcq[rows, :]
            cq = None
            for k, shifted in _shifted(win_q, [HALO - (kw - 1) + k for k in range(kw)], tc):
                acc_w[pl.ds(8 * k, 8), :] += _fold8(dcq * shifted)
                term = w_ref[k:k + 1, :] * shifted
                cq = term if cq is None else cq + term
            db_ref[rows, :] = (dy_ref[rows, :].astype(F32) * cq).astype(BF16)
            dq = _taps(pad_dcq[pl.ds(s, tc + HALO), :], w_ref, list(range(kw)), tc, flip=True)
            dcg_ref[rows, :] = (dq * u_ref[rows, :].astype(F32)).astype(BF16)
            du_ref[rows, :] = (dq * c_ref[rows, :].astype(F32)).astype(BF16)

        _chunks(t, fill)
        pl.when(j > 0)(wait_writes)
        _chunks(t, chunk)
        for cp in writes:
            cp.start()
        pl.when(j == nt - 1)(wait_writes)
        for k in range(kw):
            dw_ref[k:k + 1, :] = jnp.sum(acc_w[pl.ds(8 * k, 8), :], axis=0, keepdims=True)

    zspec = [pl.BlockSpec((t, tc_ch), lambda j, o=o: (0, o * nt + j)) for o in range(3)]
    return pl.pallas_call(
        body, name=name, grid=(nt,),
        in_specs=[*zspec, pl.BlockSpec((t, tc_ch), lambda j: (0, j)), pl.BlockSpec((kw, tc_ch), lambda j: (0, j))],
        out_specs=[ANY, pl.BlockSpec((kw, tc_ch), lambda j: (0, j))],
        out_shape=[jax.ShapeDtypeStruct((t, 3 * d_short), BF16), jax.ShapeDtypeStruct((kw, d_short), F32)],
        scratch_shapes=[pltpu.VMEM((t + HALO, tc_ch), F32), pltpu.VMEM((t + HALO, tc_ch), F32),
                        pltpu.VMEM((8 * kw, tc_ch), F32), pltpu.VMEM((3, t, tc_ch), BF16),
                        pltpu.SemaphoreType.DMA((3,))],
        compiler_params=_params(("arbitrary",)),
    )(z, z, z, dy, conv_w)


def adamw(name, w, m, v, contributions):
    r, c = w.shape
    nc = len(contributions)
    n_slots = contributions[0].shape[0]
    tr = 256 if c <= 1024 else 128
    if any(a.shape[1] % tr for a in contributions):
        assert nc == 1
        tr = r
    tiles = [a.shape[1] // tr for a in contributions]
    first = [sum(tiles[:j]) for j in range(nc)]

    def body(w_ref, m_ref, v_ref, *rest):
        g_refs, (grad_ref, delta_ref, nm_ref, nv_ref) = rest[:nc], rest[nc:]
        i = pl.program_id(0)
        g = None
        for j, g_ref in enumerate(g_refs):
            s = g_ref[0].astype(F32)
            for slot in range(1, n_slots):
                s = s + g_ref[slot].astype(F32)
            g = s if g is None else jnp.where(i >= first[j], s, g)
        nm = ADAM_B1 * m_ref[...] + (1.0 - ADAM_B1) * g
        nv = ADAM_B2 * v_ref[...] + (1.0 - ADAM_B2) * (g * g)
        m_hat = nm / (1.0 - ADAM_B1 ** ADAM_STEP)
        v_hat = nv / (1.0 - ADAM_B2 ** ADAM_STEP)
        grad_ref[...] = g
        delta_ref[...] = -ADAM_LR * (m_hat / (jnp.sqrt(v_hat) + ADAM_EPS) + ADAM_WD * w_ref[...])
        nm_ref[...] = nm
        nv_ref[...] = nv

    blk = pl.BlockSpec((tr, c), lambda i: (i, 0))
    g_specs = [pl.BlockSpec((n_slots, tr, c), lambda i, j=j: (0, jnp.clip(i - first[j], 0, tiles[j] - 1), 0))
               for j in range(nc)]
    return pl.pallas_call(
        body, name=name, grid=(r // tr,),
        in_specs=[blk, blk, blk, *g_specs],
        out_specs=[blk] * 4, out_shape=[jax.ShapeDtypeStruct((r, c), F32)] * 4,
        compiler_params=_params(("parallel",)),
    )(w, m, v, *contributions)


def _pad_rows(a, rows):
    return jnp.pad(a, ((0, rows - a.shape[0]), (0, 0)))


def kernel(x, mix_pre_g, mix_post_g, ffn_pre_g, ffn_post_g, ab_w_in, pool_w, pool_scale, conv_w, conv_b, conv_ln_g, conv_ln_b, ab_w_out, sc_w_in, sc_conv_w, sc_w_out, ffn_w1, ffn_w2, loss_target, m_mix_pre_g, m_mix_post_g, m_ffn_pre_g, m_ffn_post_g, m_ab_w_in, m_pool_w, m_pool_scale, m_conv_w, m_conv_b, m_conv_ln_g, m_conv_ln_b, m_ab_w_out, m_sc_w_in, m_sc_conv_w, m_sc_w_out, m_ffn_w1, m_ffn_w2, v_mix_pre_g, v_mix_post_g, v_ffn_pre_g, v_ffn_post_g, v_ab_w_in, v_pool_w, v_pool_scale, v_conv_w, v_conv_b, v_conv_ln_g, v_conv_ln_b, v_ab_w_out, v_sc_w_in, v_sc_conv_w, v_sc_w_out, v_ffn_w1, v_ffn_w2):
    t, d = x.shape[1], x.shape[2]
    d_pool = pool_scale.shape[1]
    d_conv = conv_b.shape[1]
    d_short = d
    ng, pg = pool_w.shape[1], pool_w.shape[3]
    kw, ks = conv_w.shape[1], sc_conv_w.shape[1]
    nb_ab, nb_sc, nb_ff = ab_w_in.shape[2], sc_w_in.shape[2], ffn_w1.shape[2]

    xs = x[0]
    target = loss_target[0]

    lanes = min(128, d_conv // N_DEV)
    small_rows = [kw * (d_conv // N_DEV) // lanes, ks * (d_short // N_DEV) // lanes, ng * (pg // N_DEV) * pg // lanes]
    small_total = -(-sum(small_rows) // 8) * 8
    r0, r1, r2 = small_rows[0], small_rows[0] + small_rows[1], sum(small_rows)

    def pack_small(a_conv, a_sconv, a_pool):
        parts = [a_conv[0].reshape(-1, lanes), a_sconv[0].reshape(-1, lanes), a_pool[0].reshape(-1, lanes)]
        return _pad_rows(jnp.concatenate(parts, axis=0), small_total)

    shards = {
        "ab_in": (ab_w_in, 0, BF16), "small": (pack_small(conv_w, sc_conv_w, pool_w)[None], 0, F32),
        "ab_out": (ab_w_out, 0, BF16), "ff1_0": (ffn_w1, 0, BF16), "ff2_0": (ffn_w2, 0, BF16),
        "sc_in": (sc_w_in, 0, BF16), "sc_out": (sc_w_out, 0, BF16),
        "ff1_1": (ffn_w1, 1, BF16), "ff2_1": (ffn_w2, 1, BF16)}
    direct = ["ab_in", "small", "ab_out"]
    zones = {nm: place_shard("place_" + nm, *shards[nm]) for nm in direct}
    started, token = copies_start("gather_start", [[zones[nm]] for nm in direct], _first_hop, 4)
    started = dict(zip(direct, started))
    ring = {}
    for nm in ["ff1_0", "ff2_0"]:
        zones[nm] = place_shard("place_" + nm, *shards[nm], deps=[token])
        (ring[nm],), token = copies_start("ring_start_" + nm, [[zones[nm]]], _ring_hop1, 3, deps=[token])
    for nm in shards:
        if nm not in zones:
            zones[nm] = place_shard("place_" + nm, *shards[nm], deps=[token])

    ties = [0]

    def after(v, *deps):
        ties[0] += 1
        return tie(f"tie_{ties[0]}", v, *deps)

    def fetch_begin(nm, dep):
        (zone,) = copies_wait("gather_wait_" + nm, started[nm], _first_hop, dep)
        (hop,), tok = copies_start("forward_start_" + nm, [[zone]], _second_hop, 3)
        return hop, tok

    def fetch_end(nm, hop, dep):
        return copies_wait("forward_wait_" + nm, hop, _second_hop, dep)[0]

    def ring_step(tag, dep, second=None, first=None, third=None):
        names, groups, hops, counts = [], [], [], []
        if second is not None:
            groups.append(copies_wait("ring1_wait_" + second, ring[second], _ring_hop1, dep))
            names, hops, counts = names + [second], hops + [_ring_hop2], counts + [4]
        if first is not None:
            groups.append([zones[first]])
            names, hops, counts = names + [first], hops + [_ring_hop1], counts + [3]
        if third is not None:
            groups.append(copies_wait("ring2_wait_" + third, ring[third], _ring_hop2, dep))
            names, hops, counts = names + [third], hops + [_ring_hop3], counts + [1]
        begun, tok = copies_start("ring_start_" + tag, groups, hops, counts, deps=[dep])
        ring.update(zip(names, begun))
        return tok

    def ring_done(nm, dep):
        return copies_wait("ring3_wait_" + nm, ring[nm], _ring_hop3, dep)[0]

    relu = lambda r: jnp.maximum(r, 0.0)
    square = lambda a: a * a
    relu2_bwd = lambda r, a: r * (2.0 * a.astype(F32))

    def row(vec, l):
        return vec[l:l + 1]

    hop_small, _ = fetch_begin("small", token)
    hop_ab_in, tok = fetch_begin("ab_in", token)
    w_small = fetch_end("small", hop_small, tok)
    w_ab_in = fetch_end("ab_in", hop_ab_in, tok)
    w_conv = w_small[:, :r0].reshape(N_DEV, kw, -1).transpose(1, 0, 2).reshape(kw, d_conv)
    w_sconv = w_small[:, r0:r1].reshape(N_DEV, ks, -1).transpose(1, 0, 2).reshape(ks, d_short)
    w_pool = w_small[:, r1:r2].reshape(N_DEV, ng, -1, pg).transpose(1, 0, 2, 3).reshape(ng, pg, pg).astype(BF16)
    h0 = norm_pre("norm_pre", xs, after(row(mix_pre_g, 0), token))
    z0 = mm_nn_blocked("ab_in", h0, w_ab_in, out_dtype=BF16)
    hop, tok = fetch_begin("ab_out", z0)
    z0 = after(z0, tok)
    pooled, y0 = pool_fwd("pool_fwd", z0, w_pool, pool_scale, d_pool, d_pool + d_conv)
    cv = conv_fwd("conv_fwd", z0, w_conv, conv_b, d_pool, d_conv)
    y0 = ln_silu("ln_silu", cv, conv_ln_g, conv_ln_b, y0, d_pool // d_conv)
    w_ab_out = fetch_end("ab_out", hop, y0)
    tok = ring_step("a", w_ab_out, second="ff1_0", first="sc_in")
    y0 = after(y0, tok)
    m0 = mm_nn("ab_out", y0, w_ab_out.reshape(d_pool + d_conv, d), out_dtype=F32)
    x1, h1 = post_pre("post_pre_0", xs, m0, row(mix_post_g, 0), row(ffn_pre_g, 0))
    tok = ring_step("b", h1, second="ff2_0", first="sc_out", third="ff1_0")
    w_ff1_0 = ring_done("ff1_0", tok)
    a0 = mm_nn_blocked("ffn0_up", h1, w_ff1_0, out_dtype=BF16, epilogue=relu)
    tok = ring_step("c", a0, second="sc_in", first="ff1_1", third="ff2_0")
    w_ff2_0 = ring_done("ff2_0", tok).reshape(-1, d)
    f0 = mm_nn("ffn0_down", a0, w_ff2_0, out_dtype=F32, tk=2048, lhs_fn=square)
    tok = ring_step("d", f0, second="sc_out", first="ff2_1", third="sc_in")
    f0 = after(f0, tok)
    x2, h2 = post_pre("post_pre_1", x1, f0, row(ffn_post_g, 0), row(mix_pre_g, 1))
    w_sc_in = ring_done("sc_in", h2)
    z1 = mm_nn_blocked("sc_in", h2, w_sc_in, out_dtype=BF16)
    tok = ring_step("e", z1, second="ff1_1", third="sc_out")
    z1 = after(z1, tok)
    y1 = short_fwd("short_fwd", z1, w_sconv, d_short)
    w_sc_out = ring_done("sc_out", y1).reshape(d_short, d)
    m1 = mm_nn("sc_out", y1, w_sc_out, out_dtype=F32)
    tok = ring_step("f", m1, second="ff2_1")
    m1 = after(m1, tok)
    x3, h3 = post_pre("post_pre_2", x2, m1, row(mix_post_g, 1), row(ffn_pre_g, 1))
    tok = ring_step("g", h3, third="ff1_1")
    w_ff1_1 = ring_done("ff1_1", tok)
    a1 = mm_nn_blocked("ffn1_up", h3, w_ff1_1, out_dtype=BF16, epilogue=relu)
    tok = ring_step("h", a1, third="ff2_1")
    w_ff2_1 = ring_done("ff2_1", tok).reshape(-1, d)
    f1 = mm_nn("ffn1_down", a1, w_ff2_1, out_dtype=F32, tk=2048, lhs_fn=square)
    dx4, df1, loss_part, dg_ffn_post1 = post_loss("post_loss", x3, f1, row(ffn_post_g, 1), target)
    loss = lax.psum(jnp.sum(loss_part) * (0.5 / d), ("x", "y", "c"))

    def reduce_begin(tag, g):
        zone = lax.empty((N_CHIP,) + g.shape[1:], g.dtype)
        (hop,), tok = copies_start("pair_start_" + tag, [[g, zone]], _pair_hop, N_CHIP)
        return hop, tok

    def reduce_middle(tag, hop, dep):
        g, from_sibling = copies_wait("pair_wait_" + tag, hop, _pair_hop, dep)
        pair_sum, zone = pair_add("pair_add_" + tag, g, from_sibling)
        (hop2,), tok = copies_start("chips_start_" + tag, [[pair_sum, zone]], _chip_hop, 3)
        return hop2, tok

    def reduce_end(tag, hop2, dep):
        return copies_wait("chips_wait_" + tag, hop2, _chip_hop, dep)[1]

    dpre, dw = mm_bwd_pair("ffn1_da_dw2", df1, w_ff2_1, a1, out_dtype=BF16, act_fn=square, epilogue=relu2_bwd)
    red_ff2_1, tok = reduce_begin("ff2_1", dw.reshape(N_DEV, -1, d))
    dpre = after(dpre, tok)
    dw = mm_tn_blocked("ffn1_dw1", h3, dpre, nb_ff, out_dtype=BF16)
    red_ff1_1, tok = reduce_begin("ff1_1", dw)
    dpre = after(dpre, tok)
    dh3 = mm_nt_blocked("ffn1_dh", dpre, w_ff1_1, out_dtype=BF16)
    red_ff2_1, tok = reduce_middle("ff2_1", red_ff2_1, dh3)
    dh3 = after(dh3, tok)
    dx3, dm1, dg_ffn_pre1, dg_mix_post1 = bwd_pre_post("bwd_3", dx4, x3, row(ffn_pre_g, 1), dh3, m1, row(mix_post_g, 1))

    dy1, dw = mm_bwd_pair("sc_dy_dwout", dm1, w_sc_out, y1, out_dtype=BF16)
    red_sc_out, tok = reduce_begin("sc_out", dw.reshape(N_DEV, -1, d))
    red_ff1_1, tok2 = reduce_middle("ff1_1", red_ff1_1, dy1)
    dy1 = after(dy1, tok, tok2)
    dz1, dw_sconv = short_bwd("short_bwd", z1, dy1, w_sconv, d_short)
    dw = mm_tn_blocked("sc_dwin", h2, dz1, nb_sc, out_dtype=BF16)
    red_sc_in, tok = reduce_begin("sc_in", dw)
    dz1 = after(dz1, tok)
    dh2 = mm_nt_blocked("sc_dh", dz1, w_sc_in, out_dtype=BF16)
    red_sc_out, tok = reduce_middle("sc_out", red_sc_out, dh2)
    dh2 = after(dh2, tok)
    dx2, df0, dg_mix_pre1, dg_ffn_post0 = bwd_pre_post("bwd_2", dx3, x2, row(mix_pre_g, 1), dh2, f0, row(ffn_post_g, 0))

    dpre, dw = mm_bwd_pair("ffn0_da_dw2", df0, w_ff2_0, a0, out_dtype=BF16, act_fn=square, epilogue=relu2_bwd)
    red_ff2_0, tok = reduce_begin("ff2_0", dw.reshape(N_DEV, -1, d))
    red_sc_in, tok2 = reduce_middle("sc_in", red_sc_in, dpre)
    dpre = after(dpre, tok, tok2)
    dw = mm_tn_blocked("ffn0_dw1", h1, dpre, nb_ff, out_dtype=BF16)
    red_ff1_0, tok = reduce_begin("ff1_0", dw)
    dpre = after(dpre, tok)
    dh1 = mm_nt_blocked("ffn0_dh", dpre, w_ff1_0, out_dtype=BF16)
    red_ff2_0, tok = reduce_middle("ff2_0", red_ff2_0, dh1)
    dh1 = after(dh1, tok)
    dx1, dm0, dg_ffn_pre0, dg_mix_post0 = bwd_pre_post("bwd_1", dx2, x1, row(ffn_pre_g, 0), dh1, m0, row(mix_post_g, 0))

    dy0, dw = mm_bwd_pair("ab_dy_dwout", dm0, w_ab_out.reshape(d_pool + d_conv, d), y0, out_dtype=BF16)
    red_ab_out, tok = reduce_begin("ab_out", dw.reshape(N_DEV, -1, d))
    red_ff1_0, tok2 = reduce_middle("ff1_0", red_ff1_0, dy0)
    dy0 = after(dy0, tok, tok2)
    dcv, dg_ln_g, dg_ln_b = ln_silu_bwd("ln_silu_bwd", cv, conv_ln_g, conv_ln_b, dy0, d_pool // d_conv)
    dz0, dw_conv, dg_conv_b = conv_bwd("conv_bwd", z0, dcv, w_conv, d_pool, d_conv)
    dz0, dw_pool, dg_pool_scale = pool_bwd("pool_bwd", pooled, dy0, w_pool, pool_scale, dz0)
    small_parts = [
        dw_conv.reshape(kw, N_DEV, -1).transpose(1, 0, 2).reshape(N_DEV, -1, lanes),
        dw_sconv.reshape(ks, N_DEV, -1).transpose(1, 0, 2).reshape(N_DEV, -1, lanes),
        dw_pool.reshape(ng, N_DEV, pg // N_DEV, pg).transpose(1, 0, 2, 3).reshape(N_DEV, -1, lanes),
    ]
    small = jnp.pad(jnp.concatenate(small_parts, axis=1), ((0, 0), (0, small_total - r2), (0, 0)))
    red_small, tok = reduce_begin("small", small)
    red_ab_out, tok2 = reduce_middle("ab_out", red_ab_out, dz0)
    dz0 = after(dz0, tok, tok2)
    dw = mm_tn_blocked("ab_dwin", h0, dz0, nb_ab, out_dtype=BF16)
    red_ab_in, tok = reduce_begin("ab_in", dw)
    dz0 = after(dz0, tok)
    dh0 = mm_nt_blocked("ab_dh", dz0, w_ab_in, out_dtype=BF16)
    red_small, tok = reduce_middle("small", red_small, dh0)
    dh0 = after(dh0, tok)
    grad_x, dg_mix_pre0 = bwd_pre_final("bwd_0", dx1, xs, row(mix_pre_g, 0), dh0)
    red_ab_in, tok = reduce_middle("ab_in", red_ab_in, grad_x)

    fold = lambda a: jnp.sum(a, axis=0, keepdims=True)
    rep_rows = [fold(dg_mix_pre0), fold(dg_mix_pre1), fold(dg_mix_post0), fold(dg_mix_post1),
                fold(dg_ffn_pre0), fold(dg_ffn_pre1), fold(dg_ffn_post0), fold(dg_ffn_post1)]
    tail = jnp.concatenate([dg_pool_scale, dg_conv_b, fold(dg_ln_g), fold(dg_ln_b)], axis=1).reshape(-1, d)
    rep = _pad_rows(jnp.concatenate(rep_rows + [tail], axis=0), 16)
    (rep_hop,), _ = copies_start("rep_start", [[place_shard("place_rep", rep[None], 0, F32)]], _first_hop, 4)

    def pack_rep(a_mix_pre, a_mix_post, a_ffn_pre, a_ffn_post, a_scale, a_b, a_g, a_lb):
        tail_ = jnp.concatenate([a_scale, a_b, a_g, a_lb], axis=1).reshape(-1, d)
        return _pad_rows(jnp.concatenate([a_mix_pre, a_mix_post, a_ffn_pre, a_ffn_post, tail_], axis=0), 16)

    def upd(name, w, m, v, contribs):
        shape = w.shape
        flat2 = lambda a: a.reshape(-1, shape[-1])
        outs = adamw(name, flat2(w), flat2(m), flat2(v), contribs)
        return [o.reshape(shape) for o in outs]

    g_ff2 = [reduce_end("ff2_0", red_ff2_0, tok), reduce_end("ff2_1", red_ff2_1, tok)]
    o_ff2 = upd("adam_ffn_w2", ffn_w2, m_ffn_w2, v_ffn_w2, g_ff2)
    (rep_zone,) = copies_wait("rep_wait", rep_hop, _first_hop, o_ff2[0])
    (rep_hop,), _ = copies_start("rep_forward_start", [[rep_zone]], _second_hop, 3)
    g_ff1 = [reduce_end("ff1_0", red_ff1_0, o_ff2[0]), reduce_end("ff1_1", red_ff1_1, o_ff2[0])]
    o_ff1 = upd("adam_ffn_w1", ffn_w1, m_ffn_w1, v_ffn_w1, g_ff1)
    (rep_all,) = copies_wait("rep_forward_wait", rep_hop, _second_hop, o_ff1[0])
    o_rep = adamw("adam_replicated",
                  pack_rep(mix_pre_g, mix_post_g, ffn_pre_g, ffn_post_g, pool_scale, conv_b, conv_ln_g, conv_ln_b),
                  pack_rep(m_mix_pre_g, m_mix_post_g, m_ffn_pre_g, m_ffn_post_g, m_pool_scale, m_conv_b, m_conv_ln_g, m_conv_ln_b),
                  pack_rep(v_mix_pre_g, v_mix_post_g, v_ffn_pre_g, v_ffn_post_g, v_pool_scale, v_conv_b, v_conv_ln_g, v_conv_ln_b),
                  [rep_all])
    o_sc_out = upd("adam_sc_out", sc_w_out, m_sc_w_out, v_sc_w_out, [reduce_end("sc_out", red_sc_out, o_ff1[0])])
    o_sc_in = upd("adam_sc_in", sc_w_in, m_sc_w_in, v_sc_w_in, [reduce_end("sc_in", red_sc_in, o_sc_out[0])])
    o_ab_out = upd("adam_ab_out", ab_w_out, m_ab_w_out, v_ab_w_out, [reduce_end("ab_out", red_ab_out, o_sc_in[0])])
    o_small = adamw("adam_small", pack_small(conv_w, sc_conv_w, pool_w), pack_small(m_conv_w, m_sc_conv_w, m_pool_w),
                    pack_small(v_conv_w, v_sc_conv_w, v_pool_w), [reduce_end("small", red_small, o_ab_out[0])])
    o_ab_in = upd("adam_ab_in", ab_w_in, m_ab_w_in, v_ab_w_in, [reduce_end("ab_in", red_ab_in, o_small[0])])

    def unpack_small(o):
        return o[:r0].reshape(conv_w.shape), o[r0:r1].reshape(sc_conv_w.shape), o[r1:r2].reshape(pool_w.shape)

    def unpack_rep(o):
        tail_ = o[8:8 + tail.shape[0]].reshape(1, -1)
        n1 = d_pool
        return dict(mix_pre_g=o[0:2], mix_post_g=o[2:4], ffn_pre_g=o[4:6], ffn_post_g=o[6:8],
                    pool_scale=tail_[:, :n1], conv_b=tail_[:, n1:n1 + d_conv],
                    conv_ln_g=tail_[:, n1 + d_conv:n1 + 2 * d_conv], conv_ln_b=tail_[:, n1 + 2 * d_conv:n1 + 3 * d_conv])

    results = []
    for kind in range(4):
        rep_o = unpack_rep(o_rep[kind])
        s_conv, s_sconv, s_pool = unpack_small(o_small[kind])
        results.append([
            rep_o["mix_pre_g"], rep_o["mix_post_g"], rep_o["ffn_pre_g"], rep_o["ffn_post_g"],
            o_ab_in[kind], s_pool, rep_o["pool_scale"], s_conv, rep_o["conv_b"], rep_o["conv_ln_g"], rep_o["conv_ln_b"],
            o_ab_out[kind], o_sc_in[kind], s_sconv, o_sc_out[kind], o_ff1[kind], o_ff2[kind]])

    return (loss, grad_x[None], *results[0], *results[1], *results[2], *results[3])
```

```python
import jax
import jax.numpy as jnp
from jax import lax
from jax.experimental import pallas as pl
from jax.experimental.pallas import tpu as pltpu

F32 = jnp.float32
BF16 = jnp.bfloat16
MESH = pl.DeviceIdType.MESH
ANY = pl.BlockSpec(memory_space=pl.ANY)

NORM_EPS = 1e-6
POOL_WINDOWS = (2, 4, 8, 16)
ADAM_LR = 0.001
ADAM_B1 = 0.9
ADAM_B2 = 0.999
ADAM_EPS = 1e-08
ADAM_WD = 0.01
ADAM_STEP = 10

N_DEV = 8
VMEM_LIMIT = 56 * 1024 * 1024
PAIR_ADD_BLOCK = 1 << 20
MATMUL_ROWS = 2048
ROW_TILE = 256
CHANNEL_TILE = 256
TIME_CHUNK = 64
HALO = 32

NN = (((1,), (0,)), ((), ()))
NT = (((1,), (1,)), ((), ()))
TN = (((0,), (0,)), ((), ()))


def _params(sem):
    return pltpu.CompilerParams(dimension_semantics=sem, vmem_limit_bytes=VMEM_LIMIT)


def _place():
    x, y, c = lax.axis_index("x"), lax.axis_index("y"), lax.axis_index("c")
    return x, y, c


def _slot(px, py, pc):
    return 4 * px + 2 * py + pc


HBM = pl.BlockSpec(memory_space=pltpu.HBM)
SEM = pl.BlockSpec(memory_space=pltpu.SEMAPHORE)
EFFECT = pltpu.SideEffectType.DATAFLOW_SIDE_EFFECTING
TOKEN = jax.ShapeDtypeStruct((8, 128), F32)


def _in_hbm(a):
    return pltpu.with_memory_space_constraint(a, pltpu.HBM)


CHIPS = [(0, 0), (0, 1), (1, 0), (1, 1)]
N_CHIP = len(CHIPS)


def _chip(px, py):
    return 2 * px + py


def _first_hop(bufs, sends, recvs, waiting):
    (land,) = bufs
    x, y, c = _place()
    me = _slot(x, y, c)
    peers = [(x, y, 1 - c), (1 - x, y, c), (x, 1 - y, c), (1 - x, 1 - y, c)]
    return [pltpu.make_async_remote_copy(
        src_ref=land.at[me], dst_ref=land.at[_slot(*p) if waiting else me],
        send_sem=sends.at[k], recv_sem=recvs.at[k], device_id=p, device_id_type=MESH) for k, p in enumerate(peers)]


def _second_hop(bufs, sends, recvs, waiting):
    (land,) = bufs
    x, y, c = _place()
    return [pltpu.make_async_remote_copy(
        src_ref=land.at[_slot(px, py, c)], dst_ref=land.at[_slot(px, py, 1 - c if waiting else c)],
        send_sem=sends.at[k], recv_sem=recvs.at[k], device_id=(x, y, 1 - c), device_id_type=MESH)
        for k, (px, py) in enumerate([(1 - x, y), (x, 1 - y), (1 - x, 1 - y)])]


def _ring_hop1(bufs, sends, recvs, waiting):
    (land,) = bufs
    x, y, c = _place()
    me = _slot(x, y, c)
    peers = [(1 - x, y, c), (x, 1 - y, c), (x, y, 1 - c)]
    return [pltpu.make_async_remote_copy(
        src_ref=land.at[me], dst_ref=land.at[_slot(*p) if waiting else me],
        send_sem=sends.at[k], recv_sem=recvs.at[k], device_id=p, device_id_type=MESH) for k, p in enumerate(peers)]


def _ring_hop2(bufs, sends, recvs, waiting):
    (land,) = bufs
    x, y, c = _place()
    half = land.shape[1] // 2
    first, second = pl.ds(0, half), pl.ds(half, half)
    nx, ny, diag = _slot(1 - x, y, c), _slot(x, 1 - y, c), _slot(1 - x, 1 - y, c)
    plan = [
        (land.at[ny, first], land.at[diag, first], (1 - x, y, c)),
        (land.at[nx, second], land.at[diag, second], (x, 1 - y, c)),
        (land.at[nx], land.at[_slot(1 - x, y, 1 - c)], (x, y, 1 - c)),
        (land.at[ny], land.at[_slot(x, 1 - y, 1 - c)], (x, y, 1 - c))]
    return [pltpu.make_async_remote_copy(
        src_ref=src, dst_ref=mine if waiting else src, send_sem=sends.at[k], recv_sem=recvs.at[k],
        device_id=to, device_id_type=MESH) for k, (src, mine, to) in enumerate(plan)]


def _ring_hop3(bufs, sends, recvs, waiting):
    (land,) = bufs
    x, y, c = _place()
    return [pltpu.make_async_remote_copy(
        src_ref=land.at[_slot(1 - x, 1 - y, c)], dst_ref=land.at[_slot(1 - x, 1 - y, 1 - c if waiting else c)],
        send_sem=sends.at[0], recv_sem=recvs.at[0], device_id=(x, y, 1 - c), device_id_type=MESH)]


def _pair_hop(bufs, sends, recvs, waiting):
    g, land = bufs
    x, y, c = _place()
    return [pltpu.make_async_remote_copy(
        src_ref=g.at[_slot(qx, qy, 1 - c)], dst_ref=land.at[q],
        send_sem=sends.at[q], recv_sem=recvs.at[q], device_id=(x, y, 1 - c), device_id_type=MESH)
        for q, (qx, qy) in enumerate(CHIPS)]


def _chip_hop(bufs, sends, recvs, waiting):
    p, land = bufs
    x, y, c = _place()
    return [pltpu.make_async_remote_copy(
        src_ref=p.at[_chip(px, py)], dst_ref=land.at[_chip(px, py) if waiting else _chip(x, y)],
        send_sem=sends.at[k], recv_sem=recvs.at[k], device_id=(px, py, c), device_id_type=MESH)
        for k, (px, py) in enumerate([(1 - x, y), (x, 1 - y), (1 - x, 1 - y)])]


def copies_start(name, groups, hop, n_copies, deps=()):
    flat = [b for grp in groups for b in grp]
    nb, ng = len(flat), len(groups)
    deps = list(deps)
    hops = list(hop) if isinstance(hop, (list, tuple)) else [hop] * ng
    counts = list(n_copies) if isinstance(n_copies, (list, tuple)) else [n_copies] * ng

    def body(*refs):
        ins, token = refs[:nb], refs[-1]
        sems = refs[nb + len(deps):nb + len(deps) + 2 * ng]
        i = 0
        for gi, grp in enumerate(groups):
            for cp in hops[gi](ins[i:i + len(grp)], sems[2 * gi], sems[2 * gi + 1], False):
                cp.start()
            i += len(grp)
        token[...] = jnp.zeros_like(token)

    outs = pl.pallas_call(
        body, name=name,
        out_shape=([pltpu.SemaphoreType.DMA((n,)) for n in counts for _ in range(2)]
                   + [pltpu.HBM(b.shape, b.dtype) for b in flat] + [TOKEN]),
        in_specs=[HBM] * nb + [ANY] * len(deps),
        out_specs=[SEM] * (2 * ng) + [HBM] * nb + [pl.BlockSpec(memory_space=pltpu.VMEM)],
        input_output_aliases={i: 2 * ng + i for i in range(nb)},
        compiler_params=pltpu.CompilerParams(has_side_effects=EFFECT),
    )(*[_in_hbm(b) for b in flat], *deps)
    started, i = [], 0
    for gi, grp in enumerate(groups):
        started.append((outs[2 * gi], outs[2 * gi + 1], list(outs[2 * ng + i:2 * ng + i + len(grp)])))
        i += len(grp)
    return started, outs[-1]


def copies_wait(name, started, hop, after):
    sends, recvs, bufs = started
    nb = len(bufs)

    def body(*refs):
        for cp in hop(refs[:nb], refs[nb], refs[nb + 1], True):
            cp.wait_send()
            cp.wait_recv()

    outs = pl.pallas_call(
        body, name=name,
        out_shape=[pltpu.HBM(b.shape, b.dtype) for b in bufs],
        in_specs=[HBM] * nb + [SEM, SEM, ANY], out_specs=[HBM] * nb,
        input_output_aliases={i: i for i in range(nb)},
        compiler_params=pltpu.CompilerParams(has_side_effects=EFFECT),
    )(*bufs, sends, recvs, after)
    return list(outs)


def place_shard(name, w, layer, dtype, deps=()):
    _, r, c = w.shape
    tr = _tile(r, 1024)
    x, y, core = _place()
    me = _slot(x, y, core).astype(jnp.int32).reshape(1)

    def body(me_ref, w_ref, *rest):
        rest[-1][...] = w_ref[...].astype(dtype)

    return pl.pallas_call(
        body, name=name,
        grid_spec=pltpu.PrefetchScalarGridSpec(
            num_scalar_prefetch=1, grid=(r // tr,),
            in_specs=[pl.BlockSpec((None, tr, c), lambda i, me_ref: (layer, i, 0))] + [ANY] * len(deps),
            out_specs=pl.BlockSpec((None, tr, c), lambda i, me_ref: (me_ref[0], i, 0))),
        out_shape=jax.ShapeDtypeStruct((N_DEV, r, c), dtype),
        compiler_params=_params(("parallel",)),
    )(me, w, *deps)


def tie(name, x, *deps):
    def body(*refs):
        del refs

    return pl.pallas_call(
        body, name=name, out_shape=jax.ShapeDtypeStruct(x.shape, x.dtype),
        in_specs=[ANY] * (1 + len(deps)), out_specs=ANY, input_output_aliases={0: 0},
    )(x, *deps)


def pair_add(name, g, from_sibling):
    _, r, c_dim = g.shape
    tr = r
    while tr * c_dim > PAIR_ADD_BLOCK and tr % 16 == 0:
        tr //= 2
    x, y, core = _place()
    where = jnp.stack([core, _chip(x, y)]).astype(jnp.int32)

    def body(where_ref, g_ref, s_ref, o_ref, zone_ref):
        total = (g_ref[...].astype(F32) + s_ref[...].astype(F32)).astype(o_ref.dtype)
        o_ref[...] = total

        @pl.when(pl.program_id(1) == where_ref[1])
        def _():
            zone_ref[...] = total

    blk = pl.BlockSpec((None, tr, c_dim), lambda i, q, where_ref: (q, i, 0))
    return pl.pallas_call(
        body, name=name,
        grid_spec=pltpu.PrefetchScalarGridSpec(
            num_scalar_prefetch=1, grid=(r // tr, N_CHIP),
            in_specs=[pl.BlockSpec((None, None, tr, c_dim), lambda i, q, where_ref: (q, where_ref[0], i, 0)), blk],
            out_specs=[blk, pl.BlockSpec((None, tr, c_dim), lambda i, q, where_ref: (where_ref[1], i, 0))]),
        out_shape=[jax.ShapeDtypeStruct((N_CHIP, r, c_dim), g.dtype)] * 2,
        compiler_params=_params(("parallel", "arbitrary")),
    )(where, g.reshape(N_CHIP, 2, r, c_dim), from_sibling)


def _matmul(name, lhs, rhs, *, out_shape, out_dtype, grid, lhs_spec, rhs_spec, out_spec, acc_shape,
            lhs_fn=None, epilogue=None):
    nk = grid[2]

    def body(lhs_ref, rhs_ref, out_ref, *scratch):
        def product():
            a = lhs_ref[...]
            if lhs_fn is not None:
                a = lhs_fn(a)
            return lax.dot_general(a, rhs_ref[...], NN, preferred_element_type=F32)

        def finish(r):
            if epilogue is not None:
                r = epilogue(r)
            out_ref[...] = r.astype(out_dtype)

        if nk == 1:
            finish(product())
        else:
            (acc_ref,) = scratch
            k = pl.program_id(2)

            @pl.when(k == 0)
            def _():
                acc_ref[...] = product()

            @pl.when(jnp.logical_and(k > 0, k < nk - 1))
            def _():
                acc_ref[...] += product()

            @pl.when(k == nk - 1)
            def _():
                finish(acc_ref[...] + product())

    return pl.pallas_call(
        body, name=name, grid=grid,
        out_shape=jax.ShapeDtypeStruct(out_shape, out_dtype),
        in_specs=[lhs_spec, rhs_spec], out_specs=out_spec,
        scratch_shapes=[pltpu.VMEM(acc_shape, F32)] if nk > 1 else [],
        compiler_params=_params(("parallel", "parallel", "arbitrary")),
    )(lhs, rhs)


def _tile(n, want):
    return want if n % want == 0 else n


def mm_nn(name, x, w, *, out_dtype, tn=512, tk=None, lhs_fn=None, epilogue=None):
    t, kdim = x.shape
    n = w.shape[1]
    tm, tn = _tile(t, MATMUL_ROWS), _tile(n, tn)
    tk = kdim if tk is None else _tile(kdim, tk)
    return _matmul(
        name, x, w, out_shape=(t, n), out_dtype=out_dtype, grid=(t // tm, n // tn, kdim // tk),
        lhs_spec=pl.BlockSpec((tm, tk), lambda i, j, k: (i, k)),
        rhs_spec=pl.BlockSpec((tk, tn), lambda i, j, k: (k, j)),
        out_spec=pl.BlockSpec((tm, tn), lambda i, j, k: (i, j)),
        acc_shape=(tm, tn), lhs_fn=lhs_fn, epilogue=epilogue)


def mm_nn_blocked(name, x, w, *, out_dtype, epilogue=None):
    t, kdim = x.shape
    nb = w.shape[2]
    tm = _tile(t, MATMUL_ROWS)
    tn = nb // 2 if nb >= 1024 else nb
    sub = nb // tn
    return _matmul(
        name, x, w, out_shape=(t, N_DEV * nb), out_dtype=out_dtype, grid=(t // tm, N_DEV * sub, 1),
        lhs_spec=pl.BlockSpec((tm, kdim), lambda i, j, k: (i, k)),
        rhs_spec=pl.BlockSpec((None, kdim, tn), lambda i, j, k: (j // sub, k, j % sub)),
        out_spec=pl.BlockSpec((tm, tn), lambda i, j, k: (i, j)),
        acc_shape=(tm, tn), epilogue=epilogue)


def mm_bwd_pair(name, dy, w, act, *, out_dtype, tile=512, act_fn=None, epilogue=None):
    t, n = dy.shape
    kdim = w.shape[0]
    tile = _tile(kdim, tile)

    def body(dy_ref, w_ref, act_ref, dx_ref, dw_ref):
        a = act_ref[...]
        dx = lax.dot_general(dy_ref[...], w_ref[...], NT, preferred_element_type=F32)
        if epilogue is not None:
            dx = epilogue(dx, a)
        dx_ref[...] = dx.astype(out_dtype)
        if act_fn is not None:
            a = act_fn(a)
        dw_ref[...] = lax.dot_general(a, dy_ref[...], TN, preferred_element_type=F32).astype(out_dtype)

    return pl.pallas_call(
        body, name=name, grid=(kdim // tile,),
        in_specs=[pl.BlockSpec((t, n), lambda j: (0, 0)), pl.BlockSpec((tile, n), lambda j: (j, 0)),
                  pl.BlockSpec((t, tile), lambda j: (0, j))],
        out_specs=[pl.BlockSpec((t, tile), lambda j: (0, j)), pl.BlockSpec((tile, n), lambda j: (j, 0))],
        out_shape=[jax.ShapeDtypeStruct((t, kdim), out_dtype), jax.ShapeDtypeStruct((kdim, n), out_dtype)],
        compiler_params=_params(("parallel",)),
    )(dy, w, act)


def mm_bwd_pair_blocked(name, dz, w, act, *, out_dtype, tile=1024):
    t = dz.shape[0]
    kdim, nb = w.shape[1], w.shape[2]
    tile = _tile(kdim, tile)

    def body(dz_ref, w_ref, act_ref, dx_ref, dw_ref, acc_ref):
        j = pl.program_id(1)
        dw_ref[...] = lax.dot_general(act_ref[...], dz_ref[...], TN, preferred_element_type=F32).astype(out_dtype)

        def product():
            return lax.dot_general(dz_ref[...], w_ref[...], NT, preferred_element_type=F32)

        @pl.when(j == 0)
        def _():
            acc_ref[...] = product()

        @pl.when(jnp.logical_and(j > 0, j < N_DEV - 1))
        def _():
            acc_ref[...] += product()

        @pl.when(j == N_DEV - 1)
        def _():
            dx_ref[...] = (acc_ref[...] + product()).astype(out_dtype)

    return pl.pallas_call(
        body, name=name, grid=(kdim // tile, N_DEV),
        in_specs=[pl.BlockSpec((t, nb), lambda i, j: (0, j)), pl.BlockSpec((None, tile, nb), lambda i, j: (j, i, 0)),
                  pl.BlockSpec((t, tile), lambda i, j: (0, i))],
        out_specs=[pl.BlockSpec((t, tile), lambda i, j: (0, i)),
                   pl.BlockSpec((None, tile, nb), lambda i, j: (j, i, 0))],
        out_shape=[jax.ShapeDtypeStruct((t, kdim), out_dtype), jax.ShapeDtypeStruct((N_DEV, kdim, nb), out_dtype)],
        scratch_shapes=[pltpu.VMEM((t, tile), F32)],
        compiler_params=_params(("parallel", "arbitrary")),
    )(dz, w, act)


def _rstd(v):
    return lax.rsqrt(jnp.mean(v * v, axis=-1, keepdims=True) + NORM_EPS)


def _rms_bwd(v, g, dy):
    r = _rstd(v)
    vhat = v * r
    dvh = dy * g
    dv = r * (dvh - vhat * jnp.mean(dvh * vhat, axis=-1, keepdims=True))
    return dv, dy * vhat


def _fold8(v):
    rows, n = v.shape
    return jnp.sum(v.reshape(rows // 8, 8, n), axis=0)


def _fold_lanes(v):
    out = v[:, 0:128]
    for i in range(1, v.shape[1] // 128):
        out = out + v[:, 128 * i:128 * (i + 1)]
    return out


def _accumulate(ref, v):
    i = pl.program_id(0)

    @pl.when(i == 0)
    def _():
        ref[...] = v

    @pl.when(i > 0)
    def _():
        ref[...] += v


def _row_call(body, name, t, ins, row_in, outs, acc_outs=(), tr=ROW_TILE):
    tr = _tile(t, tr)

    def in_spec(a, tiled):
        if isinstance(tiled, tuple):
            width, j = tiled
            return pl.BlockSpec((tr, width), lambda i: (i, j))
        return pl.BlockSpec((tr, a.shape[1]), lambda i: (i, 0)) if tiled else pl.BlockSpec(a.shape, lambda i: (0, 0))

    in_specs = [in_spec(a, tiled) for a, tiled in zip(ins, row_in)]
    out_specs = [pl.BlockSpec((tr, n), lambda i: (i, 0)) for n, _ in outs]
    out_specs += [pl.BlockSpec((8, n), lambda i: (0, 0)) for n in acc_outs]
    out_shape = [jax.ShapeDtypeStruct((t, n), dt) for n, dt in outs]
    out_shape += [jax.ShapeDtypeStruct((8, n), F32) for n in acc_outs]
    return pl.pallas_call(
        body, name=name, grid=(t // tr,), in_specs=in_specs, out_specs=out_specs, out_shape=out_shape,
        compiler_params=_params(("arbitrary",) if acc_outs else ("parallel",)),
    )(*ins)


def norm_pre(name, x, g):
    t, d = x.shape

    def body(x_ref, g_ref, h_ref):
        v = x_ref[...]
        h_ref[...] = (v * _rstd(v) * g_ref[...]).astype(BF16)

    return _row_call(body, name, t, [x, g], [True, False], [(d, BF16)])[0]


def post_pre(name, x, m, g_post, g_pre):
    t, d = x.shape

    def body(x_ref, m_ref, gp_ref, gn_ref, xo_ref, h_ref):
        mv = m_ref[...]
        xn = x_ref[...] + mv * _rstd(mv) * gp_ref[...]
        xo_ref[...] = xn
        h_ref[...] = (xn * _rstd(xn) * gn_ref[...]).astype(BF16)

    return _row_call(body, name, t, [x, m, g_post, g_pre], [True, True, False, False], [(d, F32), (d, BF16)])


def post_loss(name, x, f, g_post, target):
    t, d = x.shape

    def body(x_ref, f_ref, g_ref, t_ref, dx_ref, df_ref, loss_ref, dg_ref):
        fv = f_ref[...]
        g = g_ref[...]
        out = x_ref[...] + fv * _rstd(fv) * g
        err = out - t_ref[...]
        dx = err * (1.0 / d)
        dx_ref[...] = dx
        dfv, dg_rows = _rms_bwd(fv, g, dx)
        df_ref[...] = dfv.astype(BF16)
        _accumulate(loss_ref, _fold8(_fold_lanes(err * err)))
        _accumulate(dg_ref, _fold8(dg_rows))

    return _row_call(body, name, t, [x, f, g_post, target], [True, True, False, True],
                     [(d, F32), (d, BF16)], acc_outs=(128, d))


def bwd_pre_post(name, dx_out, x_in, g_pre, dh, f_prev, g_post_prev):
    t, d = x_in.shape

    def body(dxo_ref, x_ref, gpre_ref, dh_ref, f_ref, gpost_ref, dxi_ref, df_ref, dgpre_ref, dgpost_ref):
        dxv, dgpre_rows = _rms_bwd(x_ref[...], gpre_ref[...], dh_ref[...].astype(F32))
        dxi = dxo_ref[...] + dxv
        dxi_ref[...] = dxi
        dfv, dgpost_rows = _rms_bwd(f_ref[...], gpost_ref[...], dxi)
        df_ref[...] = dfv.astype(BF16)
        _accumulate(dgpre_ref, _fold8(dgpre_rows))
        _accumulate(dgpost_ref, _fold8(dgpost_rows))

    return _row_call(body, name, t, [dx_out, x_in, g_pre, dh, f_prev, g_post_prev],
                     [True, True, False, True, True, False], [(d, F32), (d, BF16)], acc_outs=(d, d))


def bwd_pre_final(name, dx_out, x_in, g_pre, dh):
    t, d = x_in.shape

    def body(dxo_ref, x_ref, gpre_ref, dh_ref, dxi_ref, dgpre_ref):
        dxv, dgpre_rows = _rms_bwd(x_ref[...], gpre_ref[...], dh_ref[...].astype(F32))
        dxi_ref[...] = dxo_ref[...] + dxv
        _accumulate(dgpre_ref, _fold8(dgpre_rows))

    return _row_call(body, name, t, [dx_out, x_in, g_pre, dh], [True, True, False, True], [(d, F32)], acc_outs=(d,))


def _layer_norm_parts(cv):
    mu = jnp.mean(cv, axis=-1, keepdims=True)
    xc = cv - mu
    rstd = lax.rsqrt(jnp.mean(xc * xc, axis=-1, keepdims=True) + NORM_EPS)
    return xc * rstd, rstd


def ln_silu(name, cv, g, b, y, y_block):
    t, n = cv.shape
    tr = _tile(t, ROW_TILE)

    def body(c_ref, g_ref, b_ref, y_in_ref, y_ref):
        chat, _ = _layer_norm_parts(c_ref[...])
        ln = chat * g_ref[...] + b_ref[...]
        y_ref[...] = (ln * jax.nn.sigmoid(ln)).astype(BF16)

    vec = pl.BlockSpec((1, n), lambda i: (0, 0))
    return pl.pallas_call(
        body, name=name, grid=(t // tr,),
        in_specs=[pl.BlockSpec((tr, n), lambda i: (i, 0)), vec, vec, ANY],
        out_specs=pl.BlockSpec((tr, n), lambda i: (i, y_block)),
        out_shape=jax.ShapeDtypeStruct(y.shape, y.dtype), input_output_aliases={3: 0},
        compiler_params=_params(("parallel",)),
    )(cv, g, b, y)


def ln_silu_bwd(name, cv, g, b, dy, dy_block):
    t, n = cv.shape

    def body(c_ref, g_ref, b_ref, dy_ref, dc_ref, dg_ref, db_ref):
        chat, rstd = _layer_norm_parts(c_ref[...])
        g = g_ref[...]
        ln = chat * g + b_ref[...]
        s = jax.nn.sigmoid(ln)
        dln = dy_ref[...].astype(F32) * (s * (1.0 + ln * (1.0 - s)))
        dchat = dln * g
        dc_ref[...] = rstd * (dchat - jnp.mean(dchat, axis=-1, keepdims=True)
                              - chat * jnp.mean(dchat * chat, axis=-1, keepdims=True))
        _accumulate(dg_ref, _fold8(dln * chat))
        _accumulate(db_ref, _fold8(dln))

    return _row_call(body, name, t, [cv, g, b, dy], [True, False, False, (n, dy_block)], [(n, F32)], acc_outs=(n, n))


def _chunks(t, fn, tc=TIME_CHUNK):
    tc = _tile(t, tc)

    def step(i, carry):
        fn(pl.multiple_of(i * tc, tc), tc)
        return carry

    lax.fori_loop(0, t // tc, step, 0)


def _rows_from(v, start, n):
    res = start % 8
    base = v if res == 0 else pltpu.roll(v, v.shape[0] - res, axis=0)
    return base[start - res:start - res + n, :]


def _shifted(window, offsets, tc):
    rows = window.shape[0]
    by_residue = {}
    for k, off in enumerate(offsets):
        by_residue.setdefault(off % 8, []).append((k, off))
    for res, taps in by_residue.items():
        base = window if res == 0 else pltpu.roll(window, rows - res, axis=0)
        for k, off in taps:
            yield k, base[off - res:off - res + tc, :]


def _taps(window, w_ref, offsets, tc, flip=False):
    acc = None
    for k, rows in _shifted(window, offsets, tc):
        kk = len(offsets) - 1 - k if flip else k
        term = w_ref[kk:kk + 1, :] * rows
        acc = term if acc is None else acc + term
    return acc


def _window_sums(win, tc, causal):
    sums = []
    cur, rows, step = win, tc + HALO, 1
    for _ in POOL_WINDOWS:
        rows -= 8
        if causal:
            cur = cur[8:8 + rows, :] + _rows_from(cur, 8 - step, rows)
            sums.append(cur[rows - tc:rows, :])
        else:
            cur = cur[0:rows, :] + _rows_from(cur, step, rows)
            sums.append(cur[0:tc, :])
        step *= 2
    return sums


def _pick(vals, g):
    out = vals[-1]
    for i in range(len(vals) - 2, -1, -1):
        out = jnp.where(g == i, vals[i], out)
    return out


def _pool_count(s, tc, g):
    t1 = (lax.broadcasted_iota(jnp.int32, (tc, 1), 0) + (s + 1)).astype(F32)
    width = _pick([float(w) for w in POOL_WINDOWS], g)
    return jnp.minimum(t1, width)


def pool_fwd(name, z, pool_w, pool_scale, d_pool, y_width):
    t = z.shape[0]
    ng, pg = pool_w.shape[0], pool_w.shape[1]

    def body(u_ref, w_ref, s_ref, pooled_ref, y_ref, pad):
        g = pl.program_id(0)
        pad[pl.ds(0, HALO), :] = jnp.zeros((HALO, pg), F32)

        def fill(s, tc):
            pad[pl.ds(HALO + s, tc), :] = u_ref[pl.ds(s, tc), :].astype(F32)

        def chunk(s, tc):
            win = pad[pl.ds(s, tc + HALO), :]
            total = _pick(_window_sums(win, tc, causal=True), g)
            pooled = total / _pool_count(s, tc, g) - win[HALO:HALO + tc, :]
            pooled_ref[pl.ds(s, tc), :] = pooled.astype(BF16)

        _chunks(t, fill)
        _chunks(t, chunk)
        mixed = jnp.dot(pooled_ref[...], w_ref[...], preferred_element_type=F32)
        y_ref[...] = (mixed * s_ref[...]).astype(BF16)

    col = pl.BlockSpec((t, pg), lambda g: (0, g))
    return pl.pallas_call(
        body, name=name, grid=(ng,),
        in_specs=[col, pl.BlockSpec((None, pg, pg), lambda g: (g, 0, 0)), pl.BlockSpec((1, pg), lambda g: (0, g))],
        out_specs=[col, col],
        out_shape=[jax.ShapeDtypeStruct((t, d_pool), BF16), jax.ShapeDtypeStruct((t, y_width), BF16)],
        scratch_shapes=[pltpu.VMEM((t + HALO, pg), F32)],
        compiler_params=_params(("parallel",)),
    )(z, pool_w, pool_scale)


def pool_bwd(name, pooled, dy, pool_w, pool_scale, dz):
    t, d_pool = pooled.shape
    ng, pg = pool_w.shape[0], pool_w.shape[1]

    def body(p_ref, dy_ref, w_ref, s_ref, dz_ref, du_ref, dw_ref, ds_ref, pad):
        g = pl.program_id(0)
        w = w_ref[...]
        dyv = dy_ref[...].astype(F32)
        mixed = jnp.dot(p_ref[...], w, preferred_element_type=F32)
        ds_ref[...] = jnp.sum(dyv * mixed, axis=0, keepdims=True)
        dmixed = (dyv * s_ref[...]).astype(BF16)
        dw_ref[...] = lax.dot_general(p_ref[...], dmixed, TN, preferred_element_type=F32)
        pad[...] = jnp.zeros((t + HALO, pg), F32)
        pad[pl.ds(0, t), :] = lax.dot_general(dmixed, w, NT, preferred_element_type=F32)

        def scale(s, tc):
            pad[pl.ds(s, tc), :] = pad[pl.ds(s, tc), :] / _pool_count(s, tc, g)

        def chunk(s, tc):
            win = pad[pl.ds(s, tc + HALO), :]
            total = _pick(_window_sums(win, tc, causal=False), g)
            du_ref[pl.ds(s, tc), :] = (total - win[0:tc, :] * _pool_count(s, tc, g)).astype(BF16)

        _chunks(t, scale)
        _chunks(t, chunk)

    col = pl.BlockSpec((t, pg), lambda g: (0, g))
    vec = pl.BlockSpec((1, pg), lambda g: (0, g))
    mat = pl.BlockSpec((None, pg, pg), lambda g: (g, 0, 0))
    return pl.pallas_call(
        body, name=name, grid=(ng,),
        in_specs=[col, col, mat, vec, ANY], out_specs=[col, mat, vec],
        out_shape=[jax.ShapeDtypeStruct(dz.shape, dz.dtype), jax.ShapeDtypeStruct((ng, pg, pg), F32),
                   jax.ShapeDtypeStruct((1, d_pool), F32)],
        input_output_aliases={4: 0},
        scratch_shapes=[pltpu.VMEM((t + HALO, pg), F32)],
        compiler_params=_params(("parallel",)),
    )(pooled, dy, pool_w, pool_scale, dz)


def conv_fwd(name, z, conv_w, conv_b, d_pool, d_conv):
    t = z.shape[0]
    kw = conv_w.shape[0]
    tc_ch = _tile(d_conv, CHANNEL_TILE)
    v0, g0 = d_pool // tc_ch, (d_pool + d_conv) // tc_ch

    def body(v_ref, g_ref, w_ref, b_ref, c_ref, pad):
        pad[pl.ds(0, HALO), :] = jnp.zeros((HALO, tc_ch), F32)

        def fill(s, tc):
            pad[pl.ds(HALO + s, tc), :] = v_ref[pl.ds(s, tc), :].astype(F32) * jax.nn.sigmoid(g_ref[pl.ds(s, tc), :].astype(F32))

        def chunk(s, tc):
            win = pad[pl.ds(s, tc + HALO), :]
            c_ref[pl.ds(s, tc), :] = _taps(win, w_ref, [HALO - (kw - 1) + k for k in range(kw)], tc) + b_ref[...]

        _chunks(t, fill)
        _chunks(t, chunk)

    return pl.pallas_call(
        body, name=name, grid=(d_conv // tc_ch,),
        in_specs=[pl.BlockSpec((t, tc_ch), lambda j: (0, v0 + j)), pl.BlockSpec((t, tc_ch), lambda j: (0, g0 + j)),
                  pl.BlockSpec((kw, tc_ch), lambda j: (0, j)), pl.BlockSpec((1, tc_ch), lambda j: (0, j))],
        out_specs=pl.BlockSpec((t, tc_ch), lambda j: (0, j)),
        out_shape=jax.ShapeDtypeStruct((t, d_conv), F32),
        scratch_shapes=[pltpu.VMEM((t + HALO, tc_ch), F32)],
        compiler_params=_params(("parallel",)),
    )(z, z, conv_w, conv_b)


def conv_bwd(name, z, dc, conv_w, d_pool, d_conv):
    t = z.shape[0]
    kw = conv_w.shape[0]
    tc_ch = _tile(d_conv, CHANNEL_TILE)
    v0, g0 = d_pool // tc_ch, (d_pool + d_conv) // tc_ch

    def body(v_ref, g_ref, dc_ref, w_ref, dz_ref, dw_ref, db_ref, pad_a, pad_dc, acc_w, acc_b, tiles, sems):
        j = pl.program_id(0)
        dv_ref, dg_ref = tiles.at[0], tiles.at[1]
        writes = [pltpu.make_async_copy(tiles.at[p], dz_ref.at[:, pl.ds((first + j) * tc_ch, tc_ch)], sems.at[p])
                  for p, first in enumerate([v0, g0])]

        def wait_writes():
            for cp in writes:
                cp.wait()

        pad_a[pl.ds(0, HALO), :] = jnp.zeros((HALO, tc_ch), F32)
        pad_dc[pl.ds(t, HALO), :] = jnp.zeros((HALO, tc_ch), F32)
        acc_w[...] = jnp.zeros_like(acc_w)
        acc_b[...] = jnp.zeros_like(acc_b)

        def fill(s, tc):
            pad_a[pl.ds(HALO + s, tc), :] = v_ref[pl.ds(s, tc), :].astype(F32) * jax.nn.sigmoid(g_ref[pl.ds(s, tc), :].astype(F32))
            pad_dc[pl.ds(s, tc), :] = dc_ref[pl.ds(s, tc), :]

        def chunk(s, tc):
            dcv = pad_dc[pl.ds(s, tc), :]
            win_a = pad_a[pl.ds(s, tc + HALO), :]
            for k, rows in _shifted(win_a, [HALO - (kw - 1) + k for k in range(kw)], tc):
                acc_w[pl.ds(8 * k, 8), :] += _fold8(dcv * rows)
            acc_b[...] += _fold8(dcv)
            da = _taps(pad_dc[pl.ds(s, tc + HALO), :], w_ref, list(range(kw)), tc, flip=True)
            vv = v_ref[pl.ds(s, tc), :].astype(F32)
            sg = jax.nn.sigmoid(g_ref[pl.ds(s, tc), :].astype(F32))
            dv_ref[pl.ds(s, tc), :] = (da * sg).astype(BF16)
            dg_ref[pl.ds(s, tc), :] = (da * vv * sg * (1.0 - sg)).astype(BF16)

        _chunks(t, fill)
        pl.when(j > 0)(wait_writes)
        _chunks(t, chunk)
        for cp in writes:
            cp.start()
        pl.when(j == n_tiles - 1)(wait_writes)
        for k in range(kw):
            dw_ref[k:k + 1, :] = jnp.sum(acc_w[pl.ds(8 * k, 8), :], axis=0, keepdims=True)
        db_ref[...] = jnp.sum(acc_b[...], axis=0, keepdims=True)

    n_tiles = d_conv // tc_ch
    return pl.pallas_call(
        body, name=name, grid=(n_tiles,),
        in_specs=[pl.BlockSpec((t, tc_ch), lambda j: (0, v0 + j)), pl.BlockSpec((t, tc_ch), lambda j: (0, g0 + j)),
                  pl.BlockSpec((t, tc_ch), lambda j: (0, j)), pl.BlockSpec((kw, tc_ch), lambda j: (0, j))],
        out_specs=[ANY, pl.BlockSpec((kw, tc_ch), lambda j: (0, j)), pl.BlockSpec((1, tc_ch), lambda j: (0, j))],
        out_shape=[jax.ShapeDtypeStruct((t, d_pool + 2 * d_conv), BF16),
                   jax.ShapeDtypeStruct((kw, d_conv), F32), jax.ShapeDtypeStruct((1, d_conv), F32)],
        scratch_shapes=[pltpu.VMEM((t + HALO, tc_ch), F32), pltpu.VMEM((t + HALO, tc_ch), F32),
                        pltpu.VMEM((8 * kw, tc_ch), F32), pltpu.VMEM((8, tc_ch), F32),
                        pltpu.VMEM((2, t, tc_ch), BF16), pltpu.SemaphoreType.DMA((2,))],
        compiler_params=_params(("arbitrary",)),
    )(z, z, dc, conv_w)


def short_fwd(name, z, conv_w, d_short):
    t = z.shape[0]
    kw = conv_w.shape[0]
    tc_ch = _tile(d_short, CHANNEL_TILE)
    nt = d_short // tc_ch

    def body(b_ref, c_ref, u_ref, w_ref, y_ref, pad):
        pad[pl.ds(0, HALO), :] = jnp.zeros((HALO, tc_ch), F32)

        def fill(s, tc):
            pad[pl.ds(HALO + s, tc), :] = c_ref[pl.ds(s, tc), :].astype(F32) * u_ref[pl.ds(s, tc), :].astype(F32)

        def chunk(s, tc):
            win = pad[pl.ds(s, tc + HALO), :]
            cq = _taps(win, w_ref, [HALO - (kw - 1) + k for k in range(kw)], tc)
            y_ref[pl.ds(s, tc), :] = (b_ref[pl.ds(s, tc), :].astype(F32) * cq).astype(BF16)

        _chunks(t, fill)
        _chunks(t, chunk)

    return pl.pallas_call(
        body, name=name, grid=(nt,),
        in_specs=[pl.BlockSpec((t, tc_ch), lambda j: (0, j)), pl.BlockSpec((t, tc_ch), lambda j: (0, nt + j)),
                  pl.BlockSpec((t, tc_ch), lambda j: (0, 2 * nt + j)), pl.BlockSpec((kw, tc_ch), lambda j: (0, j))],
        out_specs=pl.BlockSpec((t, tc_ch), lambda j: (0, j)),
        out_shape=jax.ShapeDtypeStruct((t, d_short), BF16),
        scratch_shapes=[pltpu.VMEM((t + HALO, tc_ch), F32)],
        compiler_params=_params(("parallel",)),
    )(z, z, z, conv_w)


def short_bwd(name, z, dy, conv_w, d_short):
    t = z.shape[0]
    kw = conv_w.shape[0]
    tc_ch = _tile(d_short, CHANNEL_TILE)
    nt = d_short // tc_ch

    def body(b_ref, c_ref, u_ref, dy_ref, w_ref, dz_ref, dw_ref, pad_q, pad_dcq, acc_w, tiles, sems):
        j = pl.program_id(0)
        db_ref, dcg_ref, du_ref = tiles.at[0], tiles.at[1], tiles.at[2]
        writes = [pltpu.make_async_copy(tiles.at[p], dz_ref.at[:, pl.ds((p * nt + j) * tc_ch, tc_ch)], sems.at[p])
                  for p in range(3)]

        def wait_writes():
            for cp in writes:
                cp.wait()

        pad_q[pl.ds(0, HALO), :] = jnp.zeros((HALO, tc_ch), F32)
        pad_dcq[pl.ds(t, HALO), :] = jnp.zeros((HALO, tc_ch), F32)
        acc_w[...] = jnp.zeros_like(acc_w)

        def fill(s, tc):
            rows = pl.ds(s, tc)
            pad_q[pl.ds(HALO + s, tc), :] = c_ref[rows, :].astype(F32) * u_ref[rows, :].astype(F32)
            pad_dcq[rows, :] = dy_ref[rows, :].astype(F32) * b_ref[rows, :].astype(F32)

        def chunk(s, tc):
            rows = pl.ds(s, tc)
            win_q = pad_q[pl.ds(s, tc + HALO), :]
            dcq = pad_dcq[rows, :]
            cq = None
            for k, shifted in _shifted(win_q, [HALO - (kw - 1) + k for k in range(kw)], tc):
                acc_w[pl.ds(8 * k, 8), :] += _fold8(dcq * shifted)
                term = w_ref[k:k + 1, :] * shifted
                cq = term if cq is None else cq + term
            db_ref[rows, :] = (dy_ref[rows, :].astype(F32) * cq).astype(BF16)
            dq = _taps(pad_dcq[pl.ds(s, tc + HALO), :], w_ref, list(range(kw)), tc, flip=True)
            dcg_ref[rows, :] = (dq * u_ref[rows, :].astype(F32)).astype(BF16)
            du_ref[rows, :] = (dq * c_ref[rows, :].astype(F32)).astype(BF16)

        _chunks(t, fill)
        pl.when(j > 0)(wait_writes)
        _chunks(t, chunk)
        for cp in writes:
            cp.start()
        pl.when(j == nt - 1)(wait_writes)
        for k in range(kw):
            dw_ref[k:k + 1, :] = jnp.sum(acc_w[pl.ds(8 * k, 8), :], axis=0, keepdims=True)

    zspec = [pl.BlockSpec((t, tc_ch), lambda j, o=o: (0, o * nt + j)) for o in range(3)]
    return pl.pallas_call(
        body, name=name, grid=(nt,),
        in_specs=[*zspec, pl.BlockSpec((t, tc_ch), lambda j: (0, j)), pl.BlockSpec((kw, tc_ch), lambda j: (0, j))],
        out_specs=[ANY, pl.BlockSpec((kw, tc_ch), lambda j: (0, j))],
        out_shape=[jax.ShapeDtypeStruct((t, 3 * d_short), BF16), jax.ShapeDtypeStruct((kw, d_short), F32)],
        scratch_shapes=[pltpu.VMEM((t + HALO, tc_ch), F32), pltpu.VMEM((t + HALO, tc_ch), F32),
                        pltpu.VMEM((8 * kw, tc_ch), F32), pltpu.VMEM((3, t, tc_ch), BF16),
                        pltpu.SemaphoreType.DMA((3,))],
        compiler_params=_params(("arbitrary",)),
    )(z, z, z, dy, conv_w)


def adamw(name, w, m, v, contributions):
    r, c = w.shape
    nc = len(contributions)
    n_slots = contributions[0].shape[0]
    tr = 256 if c <= 1024 else 128
    if any(a.shape[1] % tr for a in contributions):
        assert nc == 1
        tr = r
    tiles = [a.shape[1] // tr for a in contributions]
    first = [sum(tiles[:j]) for j in range(nc)]

    def body(w_ref, m_ref, v_ref, *rest):
        g_refs, (grad_ref, delta_ref, nm_ref, nv_ref) = rest[:nc], rest[nc:]
        i = pl.program_id(0)
        g = None
        for j, g_ref in enumerate(g_refs):
            s = g_ref[0].astype(F32)
            for slot in range(1, n_slots):
                s = s + g_ref[slot].astype(F32)
            g = s if g is None else jnp.where(i >= first[j], s, g)
        nm = ADAM_B1 * m_ref[...] + (1.0 - ADAM_B1) * g
        nv = ADAM_B2 * v_ref[...] + (1.0 - ADAM_B2) * (g * g)
        m_hat = nm / (1.0 - ADAM_B1 ** ADAM_STEP)
        v_hat = nv / (1.0 - ADAM_B2 ** ADAM_STEP)
        grad_ref[...] = g
        delta_ref[...] = -ADAM_LR * (m_hat / (jnp.sqrt(v_hat) + ADAM_EPS) + ADAM_WD * w_ref[...])
        nm_ref[...] = nm
        nv_ref[...] = nv

    blk = pl.BlockSpec((tr, c), lambda i: (i, 0))
    g_specs = [pl.BlockSpec((n_slots, tr, c), lambda i, j=j: (0, jnp.clip(i - first[j], 0, tiles[j] - 1), 0))
               for j in range(nc)]
    return pl.pallas_call(
        body, name=name, grid=(r // tr,),
        in_specs=[blk, blk, blk, *g_specs],
        out_specs=[blk] * 4, out_shape=[jax.ShapeDtypeStruct((r, c), F32)] * 4,
        compiler_params=_params(("parallel",)),
    )(w, m, v, *contributions)


def _pad_rows(a, rows):
    return jnp.pad(a, ((0, rows - a.shape[0]), (0, 0)))


def kernel(x, mix_pre_g, mix_post_g, ffn_pre_g, ffn_post_g, ab_w_in, pool_w, pool_scale, conv_w, conv_b, conv_ln_g, conv_ln_b, ab_w_out, sc_w_in, sc_conv_w, sc_w_out, ffn_w1, ffn_w2, loss_target, m_mix_pre_g, m_mix_post_g, m_ffn_pre_g, m_ffn_post_g, m_ab_w_in, m_pool_w, m_pool_scale, m_conv_w, m_conv_b, m_conv_ln_g, m_conv_ln_b, m_ab_w_out, m_sc_w_in, m_sc_conv_w, m_sc_w_out, m_ffn_w1, m_ffn_w2, v_mix_pre_g, v_mix_post_g, v_ffn_pre_g, v_ffn_post_g, v_ab_w_in, v_pool_w, v_pool_scale, v_conv_w, v_conv_b, v_conv_ln_g, v_conv_ln_b, v_ab_w_out, v_sc_w_in, v_sc_conv_w, v_sc_w_out, v_ffn_w1, v_ffn_w2):
    t, d = x.shape[1], x.shape[2]
    d_pool = pool_scale.shape[1]
    d_conv = conv_b.shape[1]
    d_short = d
    ng, pg = pool_w.shape[1], pool_w.shape[3]
    kw, ks = conv_w.shape[1], sc_conv_w.shape[1]
    nb_ab, nb_sc, nb_ff = ab_w_in.shape[2], sc_w_in.shape[2], ffn_w1.shape[2]

    xs = x[0]
    target = loss_target[0]

    lanes = min(128, d_conv // N_DEV)
    small_rows = [kw * (d_conv // N_DEV) // lanes, ks * (d_short // N_DEV) // lanes, ng * (pg // N_DEV) * pg // lanes]
    small_total = -(-sum(small_rows) // 8) * 8
    r0, r1, r2 = small_rows[0], small_rows[0] + small_rows[1], sum(small_rows)

    def pack_small(a_conv, a_sconv, a_pool):
        parts = [a_conv[0].reshape(-1, lanes), a_sconv[0].reshape(-1, lanes), a_pool[0].reshape(-1, lanes)]
        return _pad_rows(jnp.concatenate(parts, axis=0), small_total)

    shards = {
        "ab_in": (ab_w_in, 0, BF16), "small": (pack_small(conv_w, sc_conv_w, pool_w)[None], 0, F32),
        "ab_out": (ab_w_out, 0, BF16), "ff1_0": (ffn_w1, 0, BF16), "ff2_0": (ffn_w2, 0, BF16),
        "sc_in": (sc_w_in, 0, BF16), "sc_out": (sc_w_out, 0, BF16),
        "ff1_1": (ffn_w1, 1, BF16), "ff2_1": (ffn_w2, 1, BF16)}
    direct = ["ab_in", "small", "ab_out"]
    zones = {nm: place_shard("place_" + nm, *shards[nm]) for nm in direct}
    started, token = copies_start("gather_start", [[zones[nm]] for nm in direct], _first_hop, 4)
    started = dict(zip(direct, started))
    ring = {}
    for nm in ["ff1_0", "ff2_0"]:
        zones[nm] = place_shard("place_" + nm, *shards[nm], deps=[token])
        (ring[nm],), token = copies_start("ring_start_" + nm, [[zones[nm]]], _ring_hop1, 3, deps=[token])
    for nm in shards:
        if nm not in zones:
            zones[nm] = place_shard("place_" + nm, *shards[nm], deps=[token])

    ties = [0]

    def after(v, *deps):
        ties[0] += 1
        return tie(f"tie_{ties[0]}", v, *deps)

    def fetch_begin(nm, dep):
        (zone,) = copies_wait("gather_wait_" + nm, started[nm], _first_hop, dep)
        (hop,), tok = copies_start("forward_start_" + nm, [[zone]], _second_hop, 3)
        return hop, tok

    def fetch_end(nm, hop, dep):
        return copies_wait("forward_wait_" + nm, hop, _second_hop, dep)[0]

    def ring_step(tag, dep, second=None, first=None, third=None):
        names, groups, hops, counts = [], [], [], []
        if second is not None:
            groups.append(copies_wait("ring1_wait_" + second, ring[second], _ring_hop1, dep))
            names, hops, counts = names + [second], hops + [_ring_hop2], counts + [4]
        if first is not None:
            groups.append([zones[first]])
            names, hops, counts = names + [first], hops + [_ring_hop1], counts + [3]
        if third is not None:
            groups.append(copies_wait("ring2_wait_" + third, ring[third], _ring_hop2, dep))
            names, hops, counts = names + [third], hops + [_ring_hop3], counts + [1]
        begun, tok = copies_start("ring_start_" + tag, groups, hops, counts, deps=[dep])
        ring.update(zip(names, begun))
        return tok

    def ring_done(nm, dep):
        return copies_wait("ring3_wait_" + nm, ring[nm], _ring_hop3, dep)[0]

    relu = lambda r: jnp.maximum(r, 0.0)
    square = lambda a: a * a
    relu2_bwd = lambda r, a: r * (2.0 * a.astype(F32))

    def row(vec, l):
        return vec[l:l + 1]

    hop_small, _ = fetch_begin("small", token)
    hop_ab_in, tok = fetch_begin("ab_in", token)
    w_small = fetch_end("small", hop_small, tok)
    w_ab_in = fetch_end("ab_in", hop_ab_in, tok)
    w_conv = w_small[:, :r0].reshape(N_DEV, kw, -1).transpose(1, 0, 2).reshape(kw, d_conv)
    w_sconv = w_small[:, r0:r1].reshape(N_DEV, ks, -1).transpose(1, 0, 2).reshape(ks, d_short)
    w_pool = w_small[:, r1:r2].reshape(N_DEV, ng, -1, pg).transpose(1, 0, 2, 3).reshape(ng, pg, pg).astype(BF16)
    h0 = norm_pre("norm_pre", xs, after(row(mix_pre_g, 0), token))
    z0 = mm_nn_blocked("ab_in", h0, w_ab_in, out_dtype=BF16)
    hop, tok = fetch_begin("ab_out", z0)
    z0 = after(z0, tok)
    pooled, y0 = pool_fwd("pool_fwd", z0, w_pool, pool_scale, d_pool, d_pool + d_conv)
    cv = conv_fwd("conv_fwd", z0, w_conv, conv_b, d_pool, d_conv)
    y0 = ln_silu("ln_silu", cv, conv_ln_g, conv_ln_b, y0, d_pool // d_conv)
    w_ab_out = fetch_end("ab_out", hop, y0)
    tok = ring_step("a", w_ab_out, second="ff1_0", first="sc_in")
    y0 = after(y0, tok)
    m0 = mm_nn("ab_out", y0, w_ab_out.reshape(d_pool + d_conv, d), out_dtype=F32)
    x1, h1 = post_pre("post_pre_0", xs, m0, row(mix_post_g, 0), row(ffn_pre_g, 0))
    tok = ring_step("b", h1, second="ff2_0", first="sc_out", third="ff1_0")
    w_ff1_0 = ring_done("ff1_0", tok)
    a0 = mm_nn_blocked("ffn0_up", h1, w_ff1_0, out_dtype=BF16, epilogue=relu)
    tok = ring_step("c", a0, second="sc_in", first="ff1_1", third="ff2_0")
    w_ff2_0 = ring_done("ff2_0", tok).reshape(-1, d)
    f0 = mm_nn("ffn0_down", a0, w_ff2_0, out_dtype=F32, tk=2048, lhs_fn=square)
    tok = ring_step("d", f0, second="sc_out", first="ff2_1", third="sc_in")
    f0 = after(f0, tok)
    x2, h2 = post_pre("post_pre_1", x1, f0, row(ffn_post_g, 0), row(mix_pre_g, 1))
    w_sc_in = ring_done("sc_in", h2)
    z1 = mm_nn_blocked("sc_in", h2, w_sc_in, out_dtype=BF16)
    tok = ring_step("e", z1, second="ff1_1", third="sc_out")
    z1 = after(z1, tok)
    y1 = short_fwd("short_fwd", z1, w_sconv, d_short)
    w_sc_out = ring_done("sc_out", y1).reshape(d_short, d)
    m1 = mm_nn("sc_out", y1, w_sc_out, out_dtype=F32)
    tok = ring_step("f", m1, second="ff2_1")
    m1 = after(m1, tok)
    x3, h3 = post_pre("post_pre_2", x2, m1, row(mix_post_g, 1), row(ffn_pre_g, 1))
    tok = ring_step("g", h3, third="ff1_1")
    w_ff1_1 = ring_done("ff1_1", tok)
    a1 = mm_nn_blocked("ffn1_up", h3, w_ff1_1, out_dtype=BF16, epilogue=relu)
    tok = ring_step("h", a1, third="ff2_1")
    w_ff2_1 = ring_done("ff2_1", tok).reshape(-1, d)
    f1 = mm_nn("ffn1_down", a1, w_ff2_1, out_dtype=F32, tk=2048, lhs_fn=square)
    dx4, df1, loss_part, dg_ffn_post1 = post_loss("post_loss", x3, f1, row(ffn_post_g, 1), target)
    loss = lax.psum(jnp.sum(loss_part) * (0.5 / d), ("x", "y", "c"))

    def reduce_begin(tag, g):
        zone = lax.empty((N_CHIP,) + g.shape[1:], g.dtype)
        (hop,), tok = copies_start("pair_start_" + tag, [[g, zone]], _pair_hop, N_CHIP)
        return hop, tok

    def reduce_middle(tag, hop, dep):
        g, from_sibling = copies_wait("pair_wait_" + tag, hop, _pair_hop, dep)
        pair_sum, zone = pair_add("pair_add_" + tag, g, from_sibling)
        (hop2,), tok = copies_start("chips_start_" + tag, [[pair_sum, zone]], _chip_hop, 3)
        return hop2, tok

    def reduce_end(tag, hop2, dep):
        return copies_wait("chips_wait_" + tag, hop2, _chip_hop, dep)[1]

    dpre, dw = mm_bwd_pair("ffn1_da_dw2", df1, w_ff2_1, a1, out_dtype=BF16, act_fn=square, epilogue=relu2_bwd)
    red_ff2_1, tok = reduce_begin("ff2_1", dw.reshape(N_DEV, -1, d))
    dpre = after(dpre, tok)
    dh3, dw = mm_bwd_pair_blocked("ffn1_dh_dw1", dpre, w_ff1_1, h3, out_dtype=BF16)
    red_ff1_1, tok = reduce_begin("ff1_1", dw)
    red_ff2_1, tok2 = reduce_middle("ff2_1", red_ff2_1, dh3)
    dh3 = after(dh3, tok, tok2)
    dx3, dm1, dg_ffn_pre1, dg_mix_post1 = bwd_pre_post("bwd_3", dx4, x3, row(ffn_pre_g, 1), dh3, m1, row(mix_post_g, 1))

    dy1, dw = mm_bwd_pair("sc_dy_dwout", dm1, w_sc_out, y1, out_dtype=BF16)
    red_sc_out, tok = reduce_begin("sc_out", dw.reshape(N_DEV, -1, d))
    red_ff1_1, tok2 = reduce_middle("ff1_1", red_ff1_1, dy1)
    dy1 = after(dy1, tok, tok2)
    dz1, dw_sconv = short_bwd("short_bwd", z1, dy1, w_sconv, d_short)
    dh2, dw = mm_bwd_pair_blocked("sc_dh_dwin", dz1, w_sc_in, h2, out_dtype=BF16)
    red_sc_in, tok = reduce_begin("sc_in", dw)
    red_sc_out, tok2 = reduce_middle("sc_out", red_sc_out, dh2)
    dh2 = after(dh2, tok, tok2)
    dx2, df0, dg_mix_pre1, dg_ffn_post0 = bwd_pre_post("bwd_2", dx3, x2, row(mix_pre_g, 1), dh2, f0, row(ffn_post_g, 0))

    dpre, dw = mm_bwd_pair("ffn0_da_dw2", df0, w_ff2_0, a0, out_dtype=BF16, act_fn=square, epilogue=relu2_bwd)
    red_ff2_0, tok = reduce_begin("ff2_0", dw.reshape(N_DEV, -1, d))
    red_sc_in, tok2 = reduce_middle("sc_in", red_sc_in, dpre)
    dpre = after(dpre, tok, tok2)
    dh1, dw = mm_bwd_pair_blocked("ffn0_dh_dw1", dpre, w_ff1_0, h1, out_dtype=BF16)
    red_ff1_0, tok = reduce_begin("ff1_0", dw)
    red_ff2_0, tok2 = reduce_middle("ff2_0", red_ff2_0, dh1)
    dh1 = after(dh1, tok, tok2)
    dx1, dm0, dg_ffn_pre0, dg_mix_post0 = bwd_pre_post("bwd_1", dx2, x1, row(ffn_pre_g, 0), dh1, m0, row(mix_post_g, 0))

    dy0, dw = mm_bwd_pair("ab_dy_dwout", dm0, w_ab_out.reshape(d_pool + d_conv, d), y0, out_dtype=BF16)
    red_ab_out, tok = reduce_begin("ab_out", dw.reshape(N_DEV, -1, d))
    red_ff1_0, tok2 = reduce_middle("ff1_0", red_ff1_0, dy0)
    dy0 = after(dy0, tok, tok2)
    dcv, dg_ln_g, dg_ln_b = ln_silu_bwd("ln_silu_bwd", cv, conv_ln_g, conv_ln_b, dy0, d_pool // d_conv)
    dz0, dw_conv, dg_conv_b = conv_bwd("conv_bwd", z0, dcv, w_conv, d_pool, d_conv)
    dz0, dw_pool, dg_pool_scale = pool_bwd("pool_bwd", pooled, dy0, w_pool, pool_scale, dz0)
    small_parts = [
        dw_conv.reshape(kw, N_DEV, -1).transpose(1, 0, 2).reshape(N_DEV, -1, lanes),
        dw_sconv.reshape(ks, N_DEV, -1).transpose(1, 0, 2).reshape(N_DEV, -1, lanes),
        dw_pool.reshape(ng, N_DEV, pg // N_DEV, pg).transpose(1, 0, 2, 3).reshape(N_DEV, -1, lanes),
    ]
    small = jnp.pad(jnp.concatenate(small_parts, axis=1), ((0, 0), (0, small_total - r2), (0, 0)))
    red_small, tok = reduce_begin("small", small)
    red_ab_out, tok2 = reduce_middle("ab_out", red_ab_out, dz0)
    dz0 = after(dz0, tok, tok2)
    dh0, dw = mm_bwd_pair_blocked("ab_dh_dwin", dz0, w_ab_in, h0, out_dtype=BF16)
    red_ab_in, tok = reduce_begin("ab_in", dw)
    red_small, tok2 = reduce_middle("small", red_small, dh0)
    dh0 = after(dh0, tok, tok2)
    grad_x, dg_mix_pre0 = bwd_pre_final("bwd_0", dx1, xs, row(mix_pre_g, 0), dh0)
    red_ab_in, tok = reduce_middle("ab_in", red_ab_in, grad_x)

    fold = lambda a: jnp.sum(a, axis=0, keepdims=True)
    rep_rows = [fold(dg_mix_pre0), fold(dg_mix_pre1), fold(dg_mix_post0), fold(dg_mix_post1),
                fold(dg_ffn_pre0), fold(dg_ffn_pre1), fold(dg_ffn_post0), fold(dg_ffn_post1)]
    tail = jnp.concatenate([dg_pool_scale, dg_conv_b, fold(dg_ln_g), fold(dg_ln_b)], axis=1).reshape(-1, d)
    rep = _pad_rows(jnp.concatenate(rep_rows + [tail], axis=0), 16)
    (rep_hop,), _ = copies_start("rep_start", [[place_shard("place_rep", rep[None], 0, F32)]], _first_hop, 4)

    def pack_rep(a_mix_pre, a_mix_post, a_ffn_pre, a_ffn_post, a_scale, a_b, a_g, a_lb):
        tail_ = jnp.concatenate([a_scale, a_b, a_g, a_lb], axis=1).reshape(-1, d)
        return _pad_rows(jnp.concatenate([a_mix_pre, a_mix_post, a_ffn_pre, a_ffn_post, tail_], axis=0), 16)

    def upd(name, w, m, v, contribs):
        shape = w.shape
        flat2 = lambda a: a.reshape(-1, shape[-1])
        outs = adamw(name, flat2(w), flat2(m), flat2(v), contribs)
        return [o.reshape(shape) for o in outs]

    g_ff2 = [reduce_end("ff2_0", red_ff2_0, tok), reduce_end("ff2_1", red_ff2_1, tok)]
    o_ff2 = upd("adam_ffn_w2", ffn_w2, m_ffn_w2, v_ffn_w2, g_ff2)
    (rep_zone,) = copies_wait("rep_wait", rep_hop, _first_hop, o_ff2[0])
    (rep_hop,), _ = copies_start("rep_forward_start", [[rep_zone]], _second_hop, 3)
    g_ff1 = [reduce_end("ff1_0", red_ff1_0, o_ff2[0]), reduce_end("ff1_1", red_ff1_1, o_ff2[0])]
    o_ff1 = upd("adam_ffn_w1", ffn_w1, m_ffn_w1, v_ffn_w1, g_ff1)
    (rep_all,) = copies_wait("rep_forward_wait", rep_hop, _second_hop, o_ff1[0])
    o_rep = adamw("adam_replicated",
                  pack_rep(mix_pre_g, mix_post_g, ffn_pre_g, ffn_post_g, pool_scale, conv_b, conv_ln_g, conv_ln_b),
                  pack_rep(m_mix_pre_g, m_mix_post_g, m_ffn_pre_g, m_ffn_post_g, m_pool_scale, m_conv_b, m_conv_ln_g, m_conv_ln_b),
                  pack_rep(v_mix_pre_g, v_mix_post_g, v_ffn_pre_g, v_ffn_post_g, v_pool_scale, v_conv_b, v_conv_ln_g, v_conv_ln_b),
                  [rep_all])
    o_sc_out = upd("adam_sc_out", sc_w_out, m_sc_w_out, v_sc_w_out, [reduce_end("sc_out", red_sc_out, o_ff1[0])])
    o_sc_in = upd("adam_sc_in", sc_w_in, m_sc_w_in, v_sc_w_in, [reduce_end("sc_in", red_sc_in, o_sc_out[0])])
    o_ab_out = upd("adam_ab_out", ab_w_out, m_ab_w_out, v_ab_w_out, [reduce_end("ab_out", red_ab_out, o_sc_in[0])])
    o_small = adamw("adam_small", pack_small(conv_w, sc_conv_w, pool_w), pack_small(m_conv_w, m_sc_conv_w, m_pool_w),
                    pack_small(v_conv_w, v_sc_conv_w, v_pool_w), [reduce_end("small", red_small, o_ab_out[0])])
    o_ab_in = upd("adam_ab_in", ab_w_in, m_ab_w_in, v_ab_w_in, [reduce_end("ab_in", red_ab_in, o_small[0])])

    def unpack_small(o):
        return o[:r0].reshape(conv_w.shape), o[r0:r1].reshape(sc_conv_w.shape), o[r1:r2].reshape(pool_w.shape)

    def unpack_rep(o):
        tail_ = o[8:8 + tail.shape[0]].reshape(1, -1)
        n1 = d_pool
        return dict(mix_pre_g=o[0:2], mix_post_g=o[2:4], ffn_pre_g=o[4:6], ffn_post_g=o[6:8],
                    pool_scale=tail_[:, :n1], conv_b=tail_[:, n1:n1 + d_conv],
                    conv_ln_g=tail_[:, n1 + d_conv:n1 + 2 * d_conv], conv_ln_b=tail_[:, n1 + 2 * d_conv:n1 + 3 * d_conv])

    results = []
    for kind in range(4):
        rep_o = unpack_rep(o_rep[kind])
        s_conv, s_sconv, s_pool = unpack_small(o_small[kind])
        results.append([
            rep_o["mix_pre_g"], rep_o["mix_post_g"], rep_o["ffn_pre_g"], rep_o["ffn_post_g"],
            o_ab_in[kind], s_pool, rep_o["pool_scale"], s_conv, rep_o["conv_b"], rep_o["conv_ln_g"], rep_o["conv_ln_b"],
            o_ab_out[kind], o_sc_in[kind], s_sconv, o_sc_out[kind], o_ff1[kind], o_ff2[kind]])

    return (loss, grad_x[None], *results[0], *results[1], *results[2], *results[3])
```

```python
import jax
import jax.numpy as jnp
from jax import lax
from jax.experimental import pallas as pl
from jax.experimental.pallas import tpu as pltpu

F32 = jnp.float32
BF16 = jnp.bfloat16
MESH = pl.DeviceIdType.MESH
ANY = pl.BlockSpec(memory_space=pl.ANY)

NORM_EPS = 1e-6
POOL_WINDOWS = (2, 4, 8, 16)
ADAM_LR = 0.001
ADAM_B1 = 0.9
ADAM_B2 = 0.999
ADAM_EPS = 1e-08
ADAM_WD = 0.01
ADAM_STEP = 10

N_DEV = 8
VMEM_LIMIT = 56 * 1024 * 1024
PAIR_ADD_BLOCK = 1 << 20
MATMUL_ROWS = 2048
ROW_TILE = 256
CHANNEL_TILE = 256
TIME_CHUNK = 64
HALO = 32

NN = (((1,), (0,)), ((), ()))
NT = (((1,), (1,)), ((), ()))
TN = (((0,), (0,)), ((), ()))


def _params(sem):
    return pltpu.CompilerParams(dimension_semantics=sem, vmem_limit_bytes=VMEM_LIMIT)


def _place():
    x, y, c = lax.axis_index("x"), lax.axis_index("y"), lax.axis_index("c")
    return x, y, c


def _slot(px, py, pc):
    return 4 * px + 2 * py + pc


HBM = pl.BlockSpec(memory_space=pltpu.HBM)
SEM = pl.BlockSpec(memory_space=pltpu.SEMAPHORE)
EFFECT = pltpu.SideEffectType.DATAFLOW_SIDE_EFFECTING
TOKEN = jax.ShapeDtypeStruct((8, 128), F32)


def _in_hbm(a):
    return pltpu.with_memory_space_constraint(a, pltpu.HBM)


CHIPS = [(0, 0), (0, 1), (1, 0), (1, 1)]
N_CHIP = len(CHIPS)


def _chip(px, py):
    return 2 * px + py


def _first_hop(bufs, sends, recvs, waiting):
    (land,) = bufs
    x, y, c = _place()
    me = _slot(x, y, c)
    peers = [(x, y, 1 - c), (1 - x, y, c), (x, 1 - y, c), (1 - x, 1 - y, c)]
    return [pltpu.make_async_remote_copy(
        src_ref=land.at[me], dst_ref=land.at[_slot(*p) if waiting else me],
        send_sem=sends.at[k], recv_sem=recvs.at[k], device_id=p, device_id_type=MESH) for k, p in enumerate(peers)]


def _second_hop(bufs, sends, recvs, waiting):
    (land,) = bufs
    x, y, c = _place()
    return [pltpu.make_async_remote_copy(
        src_ref=land.at[_slot(px, py, c)], dst_ref=land.at[_slot(px, py, 1 - c if waiting else c)],
        send_sem=sends.at[k], recv_sem=recvs.at[k], device_id=(x, y, 1 - c), device_id_type=MESH)
        for k, (px, py) in enumerate([(1 - x, y), (x, 1 - y), (1 - x, 1 - y)])]


def _ring_hop1(bufs, sends, recvs, waiting):
    (land,) = bufs
    x, y, c = _place()
    me = _slot(x, y, c)
    peers = [(1 - x, y, c), (x, 1 - y, c), (x, y, 1 - c)]
    return [pltpu.make_async_remote_copy(
        src_ref=land.at[me], dst_ref=land.at[_slot(*p) if waiting else me],
        send_sem=sends.at[k], recv_sem=recvs.at[k], device_id=p, device_id_type=MESH) for k, p in enumerate(peers)]


def _ring_hop2(bufs, sends, recvs, waiting):
    (land,) = bufs
    x, y, c = _place()
    half = land.shape[1] // 2
    first, second = pl.ds(0, half), pl.ds(half, half)
    nx, ny, diag = _slot(1 - x, y, c), _slot(x, 1 - y, c), _slot(1 - x, 1 - y, c)
    plan = [
        (land.at[ny, first], land.at[diag, first], (1 - x, y, c)),
        (land.at[nx, second], land.at[diag, second], (x, 1 - y, c)),
        (land.at[nx], land.at[_slot(1 - x, y, 1 - c)], (x, y, 1 - c)),
        (land.at[ny], land.at[_slot(x, 1 - y, 1 - c)], (x, y, 1 - c))]
    return [pltpu.make_async_remote_copy(
        src_ref=src, dst_ref=mine if waiting else src, send_sem=sends.at[k], recv_sem=recvs.at[k],
        device_id=to, device_id_type=MESH) for k, (src, mine, to) in enumerate(plan)]


def _ring_hop3(bufs, sends, recvs, waiting):
    (land,) = bufs
    x, y, c = _place()
    return [pltpu.make_async_remote_copy(
        src_ref=land.at[_slot(1 - x, 1 - y, c)], dst_ref=land.at[_slot(1 - x, 1 - y, 1 - c if waiting else c)],
        send_sem=sends.at[0], recv_sem=recvs.at[0], device_id=(x, y, 1 - c), device_id_type=MESH)]


def _pair_hop(bufs, sends, recvs, waiting):
    g, land = bufs
    x, y, c = _place()
    return [pltpu.make_async_remote_copy(
        src_ref=g.at[_slot(qx, qy, 1 - c)], dst_ref=land.at[q],
        send_sem=sends.at[q], recv_sem=recvs.at[q], device_id=(x, y, 1 - c), device_id_type=MESH)
        for q, (qx, qy) in enumerate(CHIPS)]


def _chip_hop(bufs, sends, recvs, waiting):
    p, land = bufs
    x, y, c = _place()
    return [pltpu.make_async_remote_copy(
        src_ref=p.at[_chip(px, py)], dst_ref=land.at[_chip(px, py) if waiting else _chip(x, y)],
        send_sem=sends.at[k], recv_sem=recvs.at[k], device_id=(px, py, c), device_id_type=MESH)
        for k, (px, py) in enumerate([(1 - x, y), (x, 1 - y), (1 - x, 1 - y)])]


def copies_start(name, groups, hop, n_copies, deps=()):
    flat = [b for grp in groups for b in grp]
    nb, ng = len(flat), len(groups)
    deps = list(deps)
    hops = list(hop) if isinstance(hop, (list, tuple)) else [hop] * ng
    counts = list(n_copies) if isinstance(n_copies, (list, tuple)) else [n_copies] * ng

    def body(*refs):
        ins, token = refs[:nb], refs[-1]
        sems = refs[nb + len(deps):nb + len(deps) + 2 * ng]
        i = 0
        for gi, grp in enumerate(groups):
            for cp in hops[gi](ins[i:i + len(grp)], sems[2 * gi], sems[2 * gi + 1], False):
                cp.start()
            i += len(grp)
        token[...] = jnp.zeros_like(token)

    outs = pl.pallas_call(
        body, name=name,
        out_shape=([pltpu.SemaphoreType.DMA((n,)) for n in counts for _ in range(2)]
                   + [pltpu.HBM(b.shape, b.dtype) for b in flat] + [TOKEN]),
        in_specs=[HBM] * nb + [ANY] * len(deps),
        out_specs=[SEM] * (2 * ng) + [HBM] * nb + [pl.BlockSpec(memory_space=pltpu.VMEM)],
        input_output_aliases={i: 2 * ng + i for i in range(nb)},
        compiler_params=pltpu.CompilerParams(has_side_effects=EFFECT),
    )(*[_in_hbm(b) for b in flat], *deps)
    started, i = [], 0
    for gi, grp in enumerate(groups):
        started.append((outs[2 * gi], outs[2 * gi + 1], list(outs[2 * ng + i:2 * ng + i + len(grp)])))
        i += len(grp)
    return started, outs[-1]


def copies_wait(name, started, hop, after):
    sends, recvs, bufs = started
    nb = len(bufs)

    def body(*refs):
        for cp in hop(refs[:nb], refs[nb], refs[nb + 1], True):
            cp.wait_send()
            cp.wait_recv()

    outs = pl.pallas_call(
        body, name=name,
        out_shape=[pltpu.HBM(b.shape, b.dtype) for b in bufs],
        in_specs=[HBM] * nb + [SEM, SEM, ANY], out_specs=[HBM] * nb,
        input_output_aliases={i: i for i in range(nb)},
        compiler_params=pltpu.CompilerParams(has_side_effects=EFFECT),
    )(*bufs, sends, recvs, after)
    return list(outs)


def place_shard(name, w, layer, dtype, deps=()):
    _, r, c = w.shape
    tr = _tile(r, 1024)
    x, y, core = _place()
    me = _slot(x, y, core).astype(jnp.int32).reshape(1)

    def body(me_ref, w_ref, *rest):
        rest[-1][...] = w_ref[...].astype(dtype)

    return pl.pallas_call(
        body, name=name,
        grid_spec=pltpu.PrefetchScalarGridSpec(
            num_scalar_prefetch=1, grid=(r // tr,),
            in_specs=[pl.BlockSpec((None, tr, c), lambda i, me_ref: (layer, i, 0))] + [ANY] * len(deps),
            out_specs=pl.BlockSpec((None, tr, c), lambda i, me_ref: (me_ref[0], i, 0))),
        out_shape=jax.ShapeDtypeStruct((N_DEV, r, c), dtype),
        compiler_params=_params(("parallel",)),
    )(me, w, *deps)


def tie(name, x, *deps):
    def body(*refs):
        del refs

    return pl.pallas_call(
        body, name=name, out_shape=jax.ShapeDtypeStruct(x.shape, x.dtype),
        in_specs=[ANY] * (1 + len(deps)), out_specs=ANY, input_output_aliases={0: 0},
    )(x, *deps)


def pair_add(name, g, from_sibling):
    _, r, c_dim = g.shape
    tr = r
    while tr * c_dim > PAIR_ADD_BLOCK and tr % 16 == 0:
        tr //= 2
    x, y, core = _place()
    where = jnp.stack([core, _chip(x, y)]).astype(jnp.int32)

    def body(where_ref, g_ref, s_ref, o_ref, zone_ref):
        total = (g_ref[...].astype(F32) + s_ref[...].astype(F32)).astype(o_ref.dtype)
        o_ref[...] = total

        @pl.when(pl.program_id(1) == where_ref[1])
        def _():
            zone_ref[...] = total

    blk = pl.BlockSpec((None, tr, c_dim), lambda i, q, where_ref: (q, i, 0))
    return pl.pallas_call(
        body, name=name,
        grid_spec=pltpu.PrefetchScalarGridSpec(
            num_scalar_prefetch=1, grid=(r // tr, N_CHIP),
            in_specs=[pl.BlockSpec((None, None, tr, c_dim), lambda i, q, where_ref: (q, where_ref[0], i, 0)), blk],
            out_specs=[blk, pl.BlockSpec((None, tr, c_dim), lambda i, q, where_ref: (where_ref[1], i, 0))]),
        out_shape=[jax.ShapeDtypeStruct((N_CHIP, r, c_dim), g.dtype)] * 2,
        compiler_params=_params(("parallel", "arbitrary")),
    )(where, g.reshape(N_CHIP, 2, r, c_dim), from_sibling)


def _matmul(name, lhs, rhs, *, out_shape, out_dtype, grid, lhs_spec, rhs_spec, out_spec, acc_shape,
            lhs_fn=None, epilogue=None):
    nk = grid[2]

    def body(lhs_ref, rhs_ref, out_ref, *scratch):
        def product():
            a = lhs_ref[...]
            if lhs_fn is not None:
                a = lhs_fn(a)
            return lax.dot_general(a, rhs_ref[...], NN, preferred_element_type=F32)

        def finish(r):
            if epilogue is not None:
                r = epilogue(r)
            out_ref[...] = r.astype(out_dtype)

        if nk == 1:
            finish(product())
        else:
            (acc_ref,) = scratch
            k = pl.program_id(2)

            @pl.when(k == 0)
            def _():
                acc_ref[...] = product()

            @pl.when(jnp.logical_and(k > 0, k < nk - 1))
            def _():
                acc_ref[...] += product()

            @pl.when(k == nk - 1)
            def _():
                finish(acc_ref[...] + product())

    return pl.pallas_call(
        body, name=name, grid=grid,
        out_shape=jax.ShapeDtypeStruct(out_shape, out_dtype),
        in_specs=[lhs_spec, rhs_spec], out_specs=out_spec,
        scratch_shapes=[pltpu.VMEM(acc_shape, F32)] if nk > 1 else [],
        compiler_params=_params(("parallel", "parallel", "arbitrary")),
    )(lhs, rhs)


def _tile(n, want):
    return want if n % want == 0 else n


def mm_nn(name, x, w, *, out_dtype, tn=512, tk=None, lhs_fn=None, epilogue=None):
    t, kdim = x.shape
    n = w.shape[1]
    tm, tn = _tile(t, MATMUL_ROWS), _tile(n, tn)
    tk = kdim if tk is None else _tile(kdim, tk)
    return _matmul(
        name, x, w, out_shape=(t, n), out_dtype=out_dtype, grid=(t // tm, n // tn, kdim // tk),
        lhs_spec=pl.BlockSpec((tm, tk), lambda i, j, k: (i, k)),
        rhs_spec=pl.BlockSpec((tk, tn), lambda i, j, k: (k, j)),
        out_spec=pl.BlockSpec((tm, tn), lambda i, j, k: (i, j)),
        acc_shape=(tm, tn), lhs_fn=lhs_fn, epilogue=epilogue)


def mm_nn_blocked(name, x, w, *, out_dtype, epilogue=None):
    t, kdim = x.shape
    nb = w.shape[2]
    tm = _tile(t, MATMUL_ROWS)
    tn = nb // 2 if nb >= 1024 else nb
    sub = nb // tn
    return _matmul(
        name, x, w, out_shape=(t, N_DEV * nb), out_dtype=out_dtype, grid=(t // tm, N_DEV * sub, 1),
        lhs_spec=pl.BlockSpec((tm, kdim), lambda i, j, k: (i, k)),
        rhs_spec=pl.BlockSpec((None, kdim, tn), lambda i, j, k: (j // sub, k, j % sub)),
        out_spec=pl.BlockSpec((tm, tn), lambda i, j, k: (i, j)),
        acc_shape=(tm, tn), epilogue=epilogue)


def mm_bwd_pair(name, dy, w, act, *, out_dtype, tile=512, act_fn=None, epilogue=None):
    t, n = dy.shape
    kdim = w.shape[0]
    tile = _tile(kdim, tile)

    def body(dy_ref, w_ref, act_ref, dx_ref, dw_ref):
        a = act_ref[...]
        dx = lax.dot_general(dy_ref[...], w_ref[...], NT, preferred_element_type=F32)
        if epilogue is not None:
            dx = epilogue(dx, a)
        dx_ref[...] = dx.astype(out_dtype)
        if act_fn is not None:
            a = act_fn(a)
        dw_ref[...] = lax.dot_general(a, dy_ref[...], TN, preferred_element_type=F32).astype(out_dtype)

    return pl.pallas_call(
        body, name=name, grid=(kdim // tile,),
        in_specs=[pl.BlockSpec((t, n), lambda j: (0, 0)), pl.BlockSpec((tile, n), lambda j: (j, 0)),
                  pl.BlockSpec((t, tile), lambda j: (0, j))],
        out_specs=[pl.BlockSpec((t, tile), lambda j: (0, j)), pl.BlockSpec((tile, n), lambda j: (j, 0))],
        out_shape=[jax.ShapeDtypeStruct((t, kdim), out_dtype), jax.ShapeDtypeStruct((kdim, n), out_dtype)],
        compiler_params=_params(("parallel",)),
    )(dy, w, act)


def mm_bwd_pair_blocked(name, dz, w, act, *, out_dtype, tile=1024):
    t = dz.shape[0]
    kdim, nb = w.shape[1], w.shape[2]
    tile = _tile(kdim, tile)

    def body(dz_ref, w_ref, act_ref, dx_ref, dw_ref, acc_ref):
        j = pl.program_id(1)
        dw_ref[...] = lax.dot_general(act_ref[...], dz_ref[...], TN, preferred_element_type=F32).astype(out_dtype)

        def product():
            return lax.dot_general(dz_ref[...], w_ref[...], NT, preferred_element_type=F32)

        @pl.when(j == 0)
        def _():
            acc_ref[...] = product()

        @pl.when(jnp.logical_and(j > 0, j < N_DEV - 1))
        def _():
            acc_ref[...] += product()

        @pl.when(j == N_DEV - 1)
        def _():
            dx_ref[...] = (acc_ref[...] + product()).astype(out_dtype)

    return pl.pallas_call(
        body, name=name, grid=(kdim // tile, N_DEV),
        in_specs=[pl.BlockSpec((t, nb), lambda i, j: (0, j)), pl.BlockSpec((None, tile, nb), lambda i, j: (j, i, 0)),
                  pl.BlockSpec((t, tile), lambda i, j: (0, i))],
        out_specs=[pl.BlockSpec((t, tile), lambda i, j: (0, i)),
                   pl.BlockSpec((None, tile, nb), lambda i, j: (j, i, 0))],
        out_shape=[jax.ShapeDtypeStruct((t, kdim), out_dtype), jax.ShapeDtypeStruct((N_DEV, kdim, nb), out_dtype)],
        scratch_shapes=[pltpu.VMEM((t, tile), F32)],
        compiler_params=_params(("parallel", "arbitrary")),
    )(dz, w, act)


def _rstd(v):
    return lax.rsqrt(jnp.mean(v * v, axis=-1, keepdims=True) + NORM_EPS)


def _rms_bwd(v, g, dy):
    r = _rstd(v)
    vhat = v * r
    dvh = dy * g
    dv = r * (dvh - vhat * jnp.mean(dvh * vhat, axis=-1, keepdims=True))
    return dv, dy * vhat


def _fold8(v):
    rows, n = v.shape
    return jnp.sum(v.reshape(rows // 8, 8, n), axis=0)


def _fold_lanes(v):
    out = v[:, 0:128]
    for i in range(1, v.shape[1] // 128):
        out = out + v[:, 128 * i:128 * (i + 1)]
    return out


def _accumulate(ref, v):
    i = pl.program_id(0)

    @pl.when(i == 0)
    def _():
        ref[...] = v

    @pl.when(i > 0)
    def _():
        ref[...] += v


def _row_call(body, name, t, ins, row_in, outs, acc_outs=(), tr=ROW_TILE):
    tr = _tile(t, tr)

    def in_spec(a, tiled):
        if isinstance(tiled, tuple):
            width, j = tiled
            return pl.BlockSpec((tr, width), lambda i: (i, j))
        return pl.BlockSpec((tr, a.shape[1]), lambda i: (i, 0)) if tiled else pl.BlockSpec(a.shape, lambda i: (0, 0))

    in_specs = [in_spec(a, tiled) for a, tiled in zip(ins, row_in)]
    out_specs = [pl.BlockSpec((tr, n), lambda i: (i, 0)) for n, _ in outs]
    out_specs += [pl.BlockSpec((8, n), lambda i: (0, 0)) for n in acc_outs]
    out_shape = [jax.ShapeDtypeStruct((t, n), dt) for n, dt in outs]
    out_shape += [jax.ShapeDtypeStruct((8, n), F32) for n in acc_outs]
    return pl.pallas_call(
        body, name=name, grid=(t // tr,), in_specs=in_specs, out_specs=out_specs, out_shape=out_shape,
        compiler_params=_params(("arbitrary",) if acc_outs else ("parallel",)),
    )(*ins)


def norm_pre(name, x, g):
    t, d = x.shape

    def body(x_ref, g_ref, h_ref):
        v = x_ref[...]
        h_ref[...] = (v * _rstd(v) * g_ref[...]).astype(BF16)

    return _row_call(body, name, t, [x, g], [True, False], [(d, BF16)])[0]


def post_pre(name, x, m, g_post, g_pre):
    t, d = x.shape

    def body(x_ref, m_ref, gp_ref, gn_ref, xo_ref, h_ref):
        mv = m_ref[...]
        xn = x_ref[...] + mv * _rstd(mv) * gp_ref[...]
        xo_ref[...] = xn
        h_ref[...] = (xn * _rstd(xn) * gn_ref[...]).astype(BF16)

    return _row_call(body, name, t, [x, m, g_post, g_pre], [True, True, False, False], [(d, F32), (d, BF16)])


def post_loss(name, x, f, g_post, target):
    t, d = x.shape

    def body(x_ref, f_ref, g_ref, t_ref, dx_ref, df_ref, loss_ref, dg_ref):
        fv = f_ref[...]
        g = g_ref[...]
        out = x_ref[...] + fv * _rstd(fv) * g
        err = out - t_ref[...]
        dx = err * (1.0 / d)
        dx_ref[...] = dx
        dfv, dg_rows = _rms_bwd(fv, g, dx)
        df_ref[...] = dfv.astype(BF16)
        _accumulate(loss_ref, _fold8(_fold_lanes(err * err)))
        _accumulate(dg_ref, _fold8(dg_rows))

    return _row_call(body, name, t, [x, f, g_post, target], [True, True, False, True],
                     [(d, F32), (d, BF16)], acc_outs=(128, d))


def bwd_pre_post(name, dx_out, x_in, g_pre, dh, f_prev, g_post_prev):
    t, d = x_in.shape

    def body(dxo_ref, x_ref, gpre_ref, dh_ref, f_ref, gpost_ref, dxi_ref, df_ref, dgpre_ref, dgpost_ref):
        dxv, dgpre_rows = _rms_bwd(x_ref[...], gpre_ref[...], dh_ref[...].astype(F32))
        dxi = dxo_ref[...] + dxv
        dxi_ref[...] = dxi
        dfv, dgpost_rows = _rms_bwd(f_ref[...], gpost_ref[...], dxi)
        df_ref[...] = dfv.astype(BF16)
        _accumulate(dgpre_ref, _fold8(dgpre_rows))
        _accumulate(dgpost_ref, _fold8(dgpost_rows))

    return _row_call(body, name, t, [dx_out, x_in, g_pre, dh, f_prev, g_post_prev],
                     [True, True, False, True, True, False], [(d, F32), (d, BF16)], acc_outs=(d, d))


def bwd_pre_final(name, dx_out, x_in, g_pre, dh):
    t, d = x_in.shape

    def body(dxo_ref, x_ref, gpre_ref, dh_ref, dxi_ref, dgpre_ref):
        dxv, dgpre_rows = _rms_bwd(x_ref[...], gpre_ref[...], dh_ref[...].astype(F32))
        dxi_ref[...] = dxo_ref[...] + dxv
        _accumulate(dgpre_ref, _fold8(dgpre_rows))

    return _row_call(body, name, t, [dx_out, x_in, g_pre, dh], [True, True, False, True], [(d, F32)], acc_outs=(d,))


def _layer_norm_parts(cv):
    mu = jnp.mean(cv, axis=-1, keepdims=True)
    xc = cv - mu
    rstd = lax.rsqrt(jnp.mean(xc * xc, axis=-1, keepdims=True) + NORM_EPS)
    return xc * rstd, rstd


def ln_silu(name, cv, g, b, y, y_block):
    t, n = cv.shape
    tr = _tile(t, ROW_TILE)

    def body(c_ref, g_ref, b_ref, y_in_ref, y_ref):
        chat, _ = _layer_norm_parts(c_ref[...])
        ln = chat * g_ref[...] + b_ref[...]
        y_ref[...] = (ln * jax.nn.sigmoid(ln)).astype(BF16)

    vec = pl.BlockSpec((1, n), lambda i: (0, 0))
    return pl.pallas_call(
        body, name=name, grid=(t // tr,),
        in_specs=[pl.BlockSpec((tr, n), lambda i: (i, 0)), vec, vec, ANY],
        out_specs=pl.BlockSpec((tr, n), lambda i: (i, y_block)),
        out_shape=jax.ShapeDtypeStruct(y.shape, y.dtype), input_output_aliases={3: 0},
        compiler_params=_params(("parallel",)),
    )(cv, g, b, y)


def ln_silu_bwd(name, cv, g, b, dy, dy_block):
    t, n = cv.shape

    def body(c_ref, g_ref, b_ref, dy_ref, dc_ref, dg_ref, db_ref):
        chat, rstd = _layer_norm_parts(c_ref[...])
        g = g_ref[...]
        ln = chat * g + b_ref[...]
        s = jax.nn.sigmoid(ln)
        dln = dy_ref[...].astype(F32) * (s * (1.0 + ln * (1.0 - s)))
        dchat = dln * g
        dc_ref[...] = rstd * (dchat - jnp.mean(dchat, axis=-1, keepdims=True)
                              - chat * jnp.mean(dchat * chat, axis=-1, keepdims=True))
        _accumulate(dg_ref, _fold8(dln * chat))
        _accumulate(db_ref, _fold8(dln))

    return _row_call(body, name, t, [cv, g, b, dy], [True, False, False, (n, dy_block)], [(n, F32)], acc_outs=(n, n))


def _chunks(t, fn, tc=TIME_CHUNK):
    tc = _tile(t, tc)

    def step(i, carry):
        fn(pl.multiple_of(i * tc, tc), tc)
        return carry

    lax.fori_loop(0, t // tc, step, 0)


def _rows_from(v, start, n):
    res = start % 8
    base = v if res == 0 else pltpu.roll(v, v.shape[0] - res, axis=0)
    return base[start - res:start - res + n, :]


def _shifted(window, offsets, tc):
    rows = window.shape[0]
    by_residue = {}
    for k, off in enumerate(offsets):
        by_residue.setdefault(off % 8, []).append((k, off))
    for res, taps in by_residue.items():
        base = window if res == 0 else pltpu.roll(window, rows - res, axis=0)
        for k, off in taps:
            yield k, base[off - res:off - res + tc, :]


def _taps(window, w_ref, offsets, tc, flip=False):
    acc = None
    for k, rows in _shifted(window, offsets, tc):
        kk = len(offsets) - 1 - k if flip else k
        term = w_ref[kk:kk + 1, :] * rows
        acc = term if acc is None else acc + term
    return acc


def _window_sums(win, tc, causal):
    sums = []
    cur, rows, step = win, tc + HALO, 1
    for _ in POOL_WINDOWS:
        rows -= 8
        if causal:
            cur = cur[8:8 + rows, :] + _rows_from(cur, 8 - step, rows)
            sums.append(cur[rows - tc:rows, :])
        else:
            cur = cur[0:rows, :] + _rows_from(cur, step, rows)
            sums.append(cur[0:tc, :])
        step *= 2
    return sums


def _pick(vals, g):
    out = vals[-1]
    for i in range(len(vals) - 2, -1, -1):
        out = jnp.where(g == i, vals[i], out)
    return out


def _pool_count(s, tc, g):
    t1 = (lax.broadcasted_iota(jnp.int32, (tc, 1), 0) + (s + 1)).astype(F32)
    width = _pick([float(w) for w in POOL_WINDOWS], g)
    return jnp.minimum(t1, width)


def pool_fwd(name, z, pool_w, pool_scale, d_pool, y_width):
    t = z.shape[0]
    ng, pg = pool_w.shape[0], pool_w.shape[1]

    def body(u_ref, w_ref, s_ref, pooled_ref, y_ref, pad):
        g = pl.program_id(0)
        pad[pl.ds(0, HALO), :] = jnp.zeros((HALO, pg), F32)

        def fill(s, tc):
            pad[pl.ds(HALO + s, tc), :] = u_ref[pl.ds(s, tc), :].astype(F32)

        def chunk(s, tc):
            win = pad[pl.ds(s, tc + HALO), :]
            total = _pick(_window_sums(win, tc, causal=True), g)
            pooled = total / _pool_count(s, tc, g) - win[HALO:HALO + tc, :]
            pooled_ref[pl.ds(s, tc), :] = pooled.astype(BF16)

        _chunks(t, fill)
        _chunks(t, chunk)
        mixed = jnp.dot(pooled_ref[...], w_ref[...], preferred_element_type=F32)
        y_ref[...] = (mixed * s_ref[...]).astype(BF16)

    col = pl.BlockSpec((t, pg), lambda g: (0, g))
    return pl.pallas_call(
        body, name=name, grid=(ng,),
        in_specs=[col, pl.BlockSpec((None, pg, pg), lambda g: (g, 0, 0)), pl.BlockSpec((1, pg), lambda g: (0, g))],
        out_specs=[col, col],
        out_shape=[jax.ShapeDtypeStruct((t, d_pool), BF16), jax.ShapeDtypeStruct((t, y_width), BF16)],
        scratch_shapes=[pltpu.VMEM((t + HALO, pg), F32)],
        compiler_params=_params(("parallel",)),
    )(z, pool_w, pool_scale)


def pool_bwd(name, pooled, dy, pool_w, pool_scale, dz):
    t, d_pool = pooled.shape
    ng, pg = pool_w.shape[0], pool_w.shape[1]

    def body(p_ref, dy_ref, w_ref, s_ref, dz_ref, du_ref, dw_ref, ds_ref, pad):
        g = pl.program_id(0)
        w = w_ref[...]
        dyv = dy_ref[...].astype(F32)
        mixed = jnp.dot(p_ref[...], w, preferred_element_type=F32)
        ds_ref[...] = jnp.sum(dyv * mixed, axis=0, keepdims=True)
        dmixed = (dyv * s_ref[...]).astype(BF16)
        dw_ref[...] = lax.dot_general(p_ref[...], dmixed, TN, preferred_element_type=F32)
        pad[...] = jnp.zeros((t + HALO, pg), F32)
        pad[pl.ds(0, t), :] = lax.dot_general(dmixed, w, NT, preferred_element_type=F32)

        def scale(s, tc):
            pad[pl.ds(s, tc), :] = pad[pl.ds(s, tc), :] / _pool_count(s, tc, g)

        def chunk(s, tc):
            win = pad[pl.ds(s, tc + HALO), :]
            total = _pick(_window_sums(win, tc, causal=False), g)
            du_ref[pl.ds(s, tc), :] = (total - win[0:tc, :] * _pool_count(s, tc, g)).astype(BF16)

        _chunks(t, scale)
        _chunks(t, chunk)

    col = pl.BlockSpec((t, pg), lambda g: (0, g))
    vec = pl.BlockSpec((1, pg), lambda g: (0, g))
    mat = pl.BlockSpec((None, pg, pg), lambda g: (g, 0, 0))
    return pl.pallas_call(
        body, name=name, grid=(ng,),
        in_specs=[col, col, mat, vec, ANY], out_specs=[col, mat, vec],
        out_shape=[jax.ShapeDtypeStruct(dz.shape, dz.dtype), jax.ShapeDtypeStruct((ng, pg, pg), F32),
                   jax.ShapeDtypeStruct((1, d_pool), F32)],
        input_output_aliases={4: 0},
        scratch_shapes=[pltpu.VMEM((t + HALO, pg), F32)],
        compiler_params=_params(("parallel",)),
    )(pooled, dy, pool_w, pool_scale, dz)


def conv_fwd(name, z, conv_w, conv_b, d_pool, d_conv):
    t = z.shape[0]
    kw = conv_w.shape[0]
    tc_ch = _tile(d_conv, CHANNEL_TILE)
    v0, g0 = d_pool // tc_ch, (d_pool + d_conv) // tc_ch

    def body(v_ref, g_ref, w_ref, b_ref, c_ref, pad):
        pad[pl.ds(0, HALO), :] = jnp.zeros((HALO, tc_ch), F32)

        def fill(s, tc):
            pad[pl.ds(HALO + s, tc), :] = v_ref[pl.ds(s, tc), :].astype(F32) * jax.nn.sigmoid(g_ref[pl.ds(s, tc), :].astype(F32))

        def chunk(s, tc):
            win = pad[pl.ds(s, tc + HALO), :]
            c_ref[pl.ds(s, tc), :] = _taps(win, w_ref, [HALO - (kw - 1) + k for k in range(kw)], tc) + b_ref[...]

        _chunks(t, fill)
        _chunks(t, chunk)

    return pl.pallas_call(
        body, name=name, grid=(d_conv // tc_ch,),
        in_specs=[pl.BlockSpec((t, tc_ch), lambda j: (0, v0 + j)), pl.BlockSpec((t, tc_ch), lambda j: (0, g0 + j)),
                  pl.BlockSpec((kw, tc_ch), lambda j: (0, j)), pl.BlockSpec((1, tc_ch), lambda j: (0, j))],
        out_specs=pl.BlockSpec((t, tc_ch), lambda j: (0, j)),
        out_shape=jax.ShapeDtypeStruct((t, d_conv), F32),
        scratch_shapes=[pltpu.VMEM((t + HALO, tc_ch), F32)],
        compiler_params=_params(("parallel",)),
    )(z, z, conv_w, conv_b)


def conv_bwd(name, z, dc, conv_w, d_pool, d_conv):
    t = z.shape[0]
    kw = conv_w.shape[0]
    tc_ch = _tile(d_conv, CHANNEL_TILE)
    v0, g0 = d_pool // tc_ch, (d_pool + d_conv) // tc_ch

    def body(v_ref, g_ref, dc_ref, w_ref, dz_ref, dw_ref, db_ref, pad_a, pad_dc, acc_w, acc_b, tiles, sems):
        j = pl.program_id(0)
        dv_ref, dg_ref = tiles.at[0], tiles.at[1]
        writes = [pltpu.make_async_copy(tiles.at[p], dz_ref.at[:, pl.ds((first + j) * tc_ch, tc_ch)], sems.at[p])
                  for p, first in enumerate([v0, g0])]

        def wait_writes():
            for cp in writes:
                cp.wait()

        pad_a[pl.ds(0, HALO), :] = jnp.zeros((HALO, tc_ch), F32)
        pad_dc[pl.ds(t, HALO), :] = jnp.zeros((HALO, tc_ch), F32)
        acc_w[...] = jnp.zeros_like(acc_w)
        acc_b[...] = jnp.zeros_like(acc_b)

        def fill(s, tc):
            pad_a[pl.ds(HALO + s, tc), :] = v_ref[pl.ds(s, tc), :].astype(F32) * jax.nn.sigmoid(g_ref[pl.ds(s, tc), :].astype(F32))
            pad_dc[pl.ds(s, tc), :] = dc_ref[pl.ds(s, tc), :]

        def chunk(s, tc):
            dcv = pad_dc[pl.ds(s, tc), :]
            win_a = pad_a[pl.ds(s, tc + HALO), :]
            for k, rows in _shifted(win_a, [HALO - (kw - 1) + k for k in range(kw)], tc):
                acc_w[pl.ds(8 * k, 8), :] += _fold8(dcv * rows)
            acc_b[...] += _fold8(dcv)
            da = _taps(pad_dc[pl.ds(s, tc + HALO), :], w_ref, list(range(kw)), tc, flip=True)
            vv = v_ref[pl.ds(s, tc), :].astype(F32)
            sg = jax.nn.sigmoid(g_ref[pl.ds(s, tc), :].astype(F32))
            dv_ref[pl.ds(s, tc), :] = (da * sg).astype(BF16)
            dg_ref[pl.ds(s, tc), :] = (da * vv * sg * (1.0 - sg)).astype(BF16)

        _chunks(t, fill)
        pl.when(j > 0)(wait_writes)
        _chunks(t, chunk)
        for cp in writes:
            cp.start()
        pl.when(j == n_tiles - 1)(wait_writes)
        for k in range(kw):
            dw_ref[k:k + 1, :] = jnp.sum(acc_w[pl.ds(8 * k, 8), :], axis=0, keepdims=True)
        db_ref[...] = jnp.sum(acc_b[...], axis=0, keepdims=True)

    n_tiles = d_conv // tc_ch
    return pl.pallas_call(
        body, name=name, grid=(n_tiles,),
        in_specs=[pl.BlockSpec((t, tc_ch), lambda j: (0, v0 + j)), pl.BlockSpec((t, tc_ch), lambda j: (0, g0 + j)),
                  pl.BlockSpec((t, tc_ch), lambda j: (0, j)), pl.BlockSpec((kw, tc_ch), lambda j: (0, j))],
        out_specs=[ANY, pl.BlockSpec((kw, tc_ch), lambda j: (0, j)), pl.BlockSpec((1, tc_ch), lambda j: (0, j))],
        out_shape=[jax.ShapeDtypeStruct((t, d_pool + 2 * d_conv), BF16),
                   jax.ShapeDtypeStruct((kw, d_conv), F32), jax.ShapeDtypeStruct((1, d_conv), F32)],
        scratch_shapes=[pltpu.VMEM((t + HALO, tc_ch), F32), pltpu.VMEM((t + HALO, tc_ch), F32),
                        pltpu.VMEM((8 * kw, tc_ch), F32), pltpu.VMEM((8, tc_ch), F32),
                        pltpu.VMEM((2, t, tc_ch), BF16), pltpu.SemaphoreType.DMA((2,))],
        compiler_params=_params(("arbitrary",)),
    )(z, z, dc, conv_w)


def short_fwd(name, z, conv_w, d_short):
    t = z.shape[0]
    kw = conv_w.shape[0]
    tc_ch = _tile(d_short, CHANNEL_TILE)
    nt = d_short // tc_ch

    def body(b_ref, c_ref, u_ref, w_ref, y_ref, pad):
        pad[pl.ds(0, HALO), :] = jnp.zeros((HALO, tc_ch), F32)

        def fill(s, tc):
            pad[pl.ds(HALO + s, tc), :] = c_ref[pl.ds(s, tc), :].astype(F32) * u_ref[pl.ds(s, tc), :].astype(F32)

        def chunk(s, tc):
            win = pad[pl.ds(s, tc + HALO), :]
            cq = _taps(win, w_ref, [HALO - (kw - 1) + k for k in range(kw)], tc)
            y_ref[pl.ds(s, tc), :] = (b_ref[pl.ds(s, tc), :].astype(F32) * cq).astype(BF16)

        _chunks(t, fill)
        _chunks(t, chunk)

    return pl.pallas_call(
        body, name=name, grid=(nt,),
        in_specs=[pl.BlockSpec((t, tc_ch), lambda j: (0, j)), pl.BlockSpec((t, tc_ch), lambda j: (0, nt + j)),
                  pl.BlockSpec((t, tc_ch), lambda j: (0, 2 * nt + j)), pl.BlockSpec((kw, tc_ch), lambda j: (0, j))],
        out_specs=pl.BlockSpec((t, tc_ch), lambda j: (0, j)),
        out_shape=jax.ShapeDtypeStruct((t, d_short), BF16),
        scratch_shapes=[pltpu.VMEM((t + HALO, tc_ch), F32)],
        compiler_params=_params(("parallel",)),
    )(z, z, z, conv_w)


def short_bwd(name, z, dy, conv_w, d_short):
    t = z.shape[0]
    kw = conv_w.shape[0]
    tc_ch = _tile(d_short, CHANNEL_TILE)
    nt = d_short // tc_ch

    def body(b_ref, c_ref, u_ref, dy_ref, w_ref, dz_ref, dw_ref, pad_q, pad_dcq, acc_w, tiles, sems):
        j = pl.program_id(0)
        db_ref, dcg_ref, du_ref = tiles.at[0], tiles.at[1], tiles.at[2]
        writes = [pltpu.make_async_copy(tiles.at[p], dz_ref.at[:, pl.ds((p * nt + j) * tc_ch, tc_ch)], sems.at[p])
                  for p in range(3)]

        def wait_writes():
            for cp in writes:
                cp.wait()

        pad_q[pl.ds(0, HALO), :] = jnp.zeros((HALO, tc_ch), F32)
        pad_dcq[pl.ds(t, HALO), :] = jnp.zeros((HALO, tc_ch), F32)
        acc_w[...] = jnp.zeros_like(acc_w)

        def fill(s, tc):
            rows = pl.ds(s, tc)
            pad_q[pl.ds(HALO + s, tc), :] = c_ref[rows, :].astype(F32) * u_ref[rows, :].astype(F32)
            pad_dcq[rows, :] = dy_ref[rows, :].astype(F32) * b_ref[rows, :].astype(F32)

        def chunk(s, tc):
            rows = pl.ds(s, tc)
            win_q = pad_q[pl.ds(s, tc + HALO), :]
            dcq = pad_dcq[rows, :]
            cq = None
            for k, shifted in _shifted(win_q, [HALO - (kw - 1) + k for k in range(kw)], tc):
                acc_w[pl.ds(8 * k, 8), :] += _fold8(dcq * shifted)
                term = w_ref[k:k + 1, :] * shifted
                cq = term if cq is None else cq + term
            db_ref[rows, :] = (dy_ref[rows, :].astype(F32) * cq).astype(BF16)
            dq = _taps(pad_dcq[pl.ds(s, tc + HALO), :], w_ref, list(range(kw)), tc, flip=True)
            dcg_ref[rows, :] = (dq * u_ref[rows, :].astype(F32)).astype(BF16)
            du_ref[rows, :] = (dq * c_ref[rows, :].astype(F32)).astype(BF16)

        _chunks(t, fill)
        pl.when(j > 0)(wait_writes)
        _chunks(t, chunk)
        for cp in writes:
            cp.start()
        pl.when(j == nt - 1)(wait_writes)
        for k in range(kw):
            dw_ref[k:k + 1, :] = jnp.sum(acc_w[pl.ds(8 * k, 8), :], axis=0, keepdims=True)

    zspec = [pl.BlockSpec((t, tc_ch), lambda j, o=o: (0, o * nt + j)) for o in range(3)]
    return pl.pallas_call(
        body, name=name, grid=(nt,),
        in_specs=[*zspec, pl.BlockSpec((t, tc_ch), lambda j: (0, j)), pl.BlockSpec((kw, tc_ch), lambda j: (0, j))],
        out_specs=[ANY, pl.BlockSpec((kw, tc_ch), lambda j: (0, j))],
        out_shape=[jax.ShapeDtypeStruct((t, 3 * d_short), BF16), jax.ShapeDtypeStruct((kw, d_short), F32)],
        scratch_shapes=[pltpu.VMEM((t + HALO, tc_ch), F32), pltpu.VMEM((t + HALO, tc_ch), F32),
                        pltpu.VMEM((8 * kw, tc_ch), F32), pltpu.VMEM((3, t, tc_ch), BF16),
                        pltpu.SemaphoreType.DMA((3,))],
        compiler_params=_params(("arbitrary",)),
    )(z, z, z, dy, conv_w)


def adamw(name, w, m, v, contributions):
    r, c = w.shape
    nc = len(contributions)
    n_slots = contributions[0].shape[0]
    tr = 256 if c <= 1024 else 128
    if any(a.shape[1] % tr for a in contributions):
        assert nc == 1
        tr = r
    tiles = [a.shape[1] // tr for a in contributions]
    first = [sum(tiles[:j]) for j in range(nc)]

    def body(w_ref, m_ref, v_ref, *rest):
        g_refs, (grad_ref, delta_ref, nm_ref, nv_ref) = rest[:nc], rest[nc:]
        i = pl.program_id(0)
        g = None
        for j, g_ref in enumerate(g_refs):
            s = g_ref[0].astype(F32)
            for slot in range(1, n_slots):
                s = s + g_ref[slot].astype(F32)
            g = s if g is None else jnp.where(i >= first[j], s, g)
        nm = ADAM_B1 * m_ref[...] + (1.0 - ADAM_B1) * g
        nv = ADAM_B2 * v_ref[...] + (1.0 - ADAM_B2) * (g * g)
        m_hat = nm / (1.0 - ADAM_B1 ** ADAM_STEP)
        v_hat = nv / (1.0 - ADAM_B2 ** ADAM_STEP)
        grad_ref[...] = g
        delta_ref[...] = -ADAM_LR * (m_hat / (jnp.sqrt(v_hat) + ADAM_EPS) + ADAM_WD * w_ref[...])
        nm_ref[...] = nm
        nv_ref[...] = nv

    blk = pl.BlockSpec((tr, c), lambda i: (i, 0))
    g_specs = [pl.BlockSpec((n_slots, tr, c), lambda i, j=j: (0, jnp.clip(i - first[j], 0, tiles[j] - 1), 0))
               for j in range(nc)]
    return pl.pallas_call(
        body, name=name, grid=(r // tr,),
        in_specs=[blk, blk, blk, *g_specs],
        out_specs=[blk] * 4, out_shape=[jax.ShapeDtypeStruct((r, c), F32)] * 4,
        compiler_params=_params(("parallel",)),
    )(w, m, v, *contributions)


def _pad_rows(a, rows):
    return jnp.pad(a, ((0, rows - a.shape[0]), (0, 0)))


def kernel(x, mix_pre_g, mix_post_g, ffn_pre_g, ffn_post_g, ab_w_in, pool_w, pool_scale, conv_w, conv_b, conv_ln_g, conv_ln_b, ab_w_out, sc_w_in, sc_conv_w, sc_w_out, ffn_w1, ffn_w2, loss_target, m_mix_pre_g, m_mix_post_g, m_ffn_pre_g, m_ffn_post_g, m_ab_w_in, m_pool_w, m_pool_scale, m_conv_w, m_conv_b, m_conv_ln_g, m_conv_ln_b, m_ab_w_out, m_sc_w_in, m_sc_conv_w, m_sc_w_out, m_ffn_w1, m_ffn_w2, v_mix_pre_g, v_mix_post_g, v_ffn_pre_g, v_ffn_post_g, v_ab_w_in, v_pool_w, v_pool_scale, v_conv_w, v_conv_b, v_conv_ln_g, v_conv_ln_b, v_ab_w_out, v_sc_w_in, v_sc_conv_w, v_sc_w_out, v_ffn_w1, v_ffn_w2):
    t, d = x.shape[1], x.shape[2]
    d_pool = pool_scale.shape[1]
    d_conv = conv_b.shape[1]
    d_short = d
    ng, pg = pool_w.shape[1], pool_w.shape[3]
    kw, ks = conv_w.shape[1], sc_conv_w.shape[1]
    nb_ab, nb_sc, nb_ff = ab_w_in.shape[2], sc_w_in.shape[2], ffn_w1.shape[2]

    xs = x[0]
    target = loss_target[0]

    lanes = min(128, d_conv // N_DEV)
    small_rows = [kw * (d_conv // N_DEV) // lanes, ks * (d_short // N_DEV) // lanes, ng * (pg // N_DEV) * pg // lanes]
    small_total = -(-sum(small_rows) // 8) * 8
    r0, r1, r2 = small_rows[0], small_rows[0] + small_rows[1], sum(small_rows)

    def pack_small(a_conv, a_sconv, a_pool):
        parts = [a_conv[0].reshape(-1, lanes), a_sconv[0].reshape(-1, lanes), a_pool[0].reshape(-1, lanes)]
        return _pad_rows(jnp.concatenate(parts, axis=0), small_total)

    shards = {
        "ab_in": (ab_w_in, 0, BF16), "small": (pack_small(conv_w, sc_conv_w, pool_w)[None], 0, F32),
        "ab_out": (ab_w_out, 0, BF16), "ff1_0": (ffn_w1, 0, BF16), "ff2_0": (ffn_w2, 0, BF16),
        "sc_in": (sc_w_in, 0, BF16), "sc_out": (sc_w_out, 0, BF16),
        "ff1_1": (ffn_w1, 1, BF16), "ff2_1": (ffn_w2, 1, BF16)}
    direct = ["ab_in", "small", "ab_out"]
    zones = {nm: place_shard("place_" + nm, *shards[nm]) for nm in direct}
    started, token = copies_start("gather_start", [[zones[nm]] for nm in direct], _first_hop, 4)
    started = dict(zip(direct, started))
    ring = {}
    for nm in ["ff1_0", "ff2_0"]:
        zones[nm] = place_shard("place_" + nm, *shards[nm], deps=[token])
        (ring[nm],), token = copies_start("ring_start_" + nm, [[zones[nm]]], _ring_hop1, 3, deps=[token])
    for nm in shards:
        if nm not in zones:
            zones[nm] = place_shard("place_" + nm, *shards[nm], deps=[token])

    ties = [0]

    def after(v, *deps):
        ties[0] += 1
        return tie(f"tie_{ties[0]}", v, *deps)

    def fetch_begin(nm, dep):
        (zone,) = copies_wait("gather_wait_" + nm, started[nm], _first_hop, dep)
        (hop,), tok = copies_start("forward_start_" + nm, [[zone]], _second_hop, 3)
        return hop, tok

    def fetch_end(nm, hop, dep):
        return copies_wait("forward_wait_" + nm, hop, _second_hop, dep)[0]

    def ring_step(tag, dep, second=None, first=None, third=None):
        names, groups, hops, counts = [], [], [], []
        if second is not None:
            groups.append(copies_wait("ring1_wait_" + second, ring[second], _ring_hop1, dep))
            names, hops, counts = names + [second], hops + [_ring_hop2], counts + [4]
        if first is not None:
            groups.append([zones[first]])
            names, hops, counts = names + [first], hops + [_ring_hop1], counts + [3]
        if third is not None:
            groups.append(copies_wait("ring2_wait_" + third, ring[third], _ring_hop2, dep))
            names, hops, counts = names + [third], hops + [_ring_hop3], counts + [1]
        begun, tok = copies_start("ring_start_" + tag, groups, hops, counts, deps=[dep])
        ring.update(zip(names, begun))
        return tok

    def ring_done(nm, dep):
        return copies_wait("ring3_wait_" + nm, ring[nm], _ring_hop3, dep)[0]

    relu = lambda r: jnp.maximum(r, 0.0)
    square = lambda a: a * a
    relu2_bwd = lambda r, a: r * (2.0 * a.astype(F32))

    def row(vec, l):
        return vec[l:l + 1]

    hop_small, _ = fetch_begin("small", token)
    hop_ab_in, tok = fetch_begin("ab_in", token)
    w_small = fetch_end("small", hop_small, tok)
    w_ab_in = fetch_end("ab_in", hop_ab_in, tok)
    w_conv = w_small[:, :r0].reshape(N_DEV, kw, -1).transpose(1, 0, 2).reshape(kw, d_conv)
    w_sconv = w_small[:, r0:r1].reshape(N_DEV, ks, -1).transpose(1, 0, 2).reshape(ks, d_short)
    w_pool = w_small[:, r1:r2].reshape(N_DEV, ng, -1, pg).transpose(1, 0, 2, 3).reshape(ng, pg, pg).astype(BF16)
    h0 = norm_pre("norm_pre", xs, after(row(mix_pre_g, 0), token))
    z0 = mm_nn_blocked("ab_in", h0, w_ab_in, out_dtype=BF16)
    hop, tok = fetch_begin("ab_out", z0)
    z0 = after(z0, tok)
    pooled, y0 = pool_fwd("pool_fwd", z0, w_pool, pool_scale, d_pool, d_pool + d_conv)
    cv = conv_fwd("conv_fwd", z0, w_conv, conv_b, d_pool, d_conv)
    y0 = ln_silu("ln_silu", cv, conv_ln_g, conv_ln_b, y0, d_pool // d_conv)
    w_ab_out = fetch_end("ab_out", hop, y0)
    tok = ring_step("a", w_ab_out, second="ff1_0", first="sc_in")
    y0 = after(y0, tok)
    m0 = mm_nn("ab_out", y0, w_ab_out.reshape(d_pool + d_conv, d), out_dtype=F32)
    x1, h1 = post_pre("post_pre_0", xs, m0, row(mix_post_g, 0), row(ffn_pre_g, 0))
    tok = ring_step("b", h1, second="ff2_0", first="sc_out", third="ff1_0")
    w_ff1_0 = ring_done("ff1_0", tok)
    a0 = mm_nn_blocked("ffn0_up", h1, w_ff1_0, out_dtype=BF16, epilogue=relu)
    tok = ring_step("c", a0, second="sc_in", first="ff1_1", third="ff2_0")
    w_ff2_0 = ring_done("ff2_0", tok).reshape(-1, d)
    f0 = mm_nn("ffn0_down", a0, w_ff2_0, out_dtype=F32, tk=2048, lhs_fn=square)
    tok = ring_step("d", f0, second="sc_out", first="ff2_1", third="sc_in")
    f0 = after(f0, tok)
    x2, h2 = post_pre("post_pre_1", x1, f0, row(ffn_post_g, 0), row(mix_pre_g, 1))
    w_sc_in = ring_done("sc_in", h2)
    z1 = mm_nn_blocked("sc_in", h2, w_sc_in, out_dtype=BF16)
    tok = ring_step("e", z1, second="ff1_1", third="sc_out")
    z1 = after(z1, tok)
    y1 = short_fwd("short_fwd", z1, w_sconv, d_short)
    w_sc_out = ring_done("sc_out", y1).reshape(d_short, d)
    m1 = mm_nn("sc_out", y1, w_sc_out, out_dtype=F32)
    tok = ring_step("f", m1, second="ff2_1")
    m1 = after(m1, tok)
    x3, h3 = post_pre("post_pre_2", x2, m1, row(mix_post_g, 1), row(ffn_pre_g, 1))
    tok = ring_step("g", h3, third="ff1_1")
    w_ff1_1 = ring_done("ff1_1", tok)
    a1 = mm_nn_blocked("ffn1_up", h3, w_ff1_1, out_dtype=BF16, epilogue=relu)
    tok = ring_step("h", a1, third="ff2_1")
    w_ff2_1 = ring_done("ff2_1", tok).reshape(-1, d)
    f1 = mm_nn("ffn1_down", a1, w_ff2_1, out_dtype=F32, tk=2048, lhs_fn=square)
    dx4, df1, loss_part, dg_ffn_post1 = post_loss("post_loss", x3, f1, row(ffn_post_g, 1), target)
    loss = lax.psum(jnp.sum(loss_part) * (0.5 / d), ("x", "y", "c"))

    red = {}

    def reduce_step(dep, begin=None, middle=None):
        tags, groups, hops, counts = [], [], [], []
        if begin is not None:
            tag, g = begin
            tags, groups = tags + [tag], groups + [[g, lax.empty((N_CHIP,) + g.shape[1:], g.dtype)]]
            hops, counts = hops + [_pair_hop], counts + [N_CHIP]
        if middle is not None:
            g, from_sibling = copies_wait("pair_wait_" + middle, red[middle], _pair_hop, dep)
            tags, groups = tags + [middle], groups + [list(pair_add("pair_add_" + middle, g, from_sibling))]
            hops, counts = hops + [_chip_hop], counts + [3]
        begun, tok = copies_start("reduce_start_" + "_".join(tags), groups, hops, counts, deps=[dep])
        red.update(zip(tags, begun))
        return tok

    def reduce_end(tag, dep):
        return copies_wait("chips_wait_" + tag, red[tag], _chip_hop, dep)[1]

    dpre, dw = mm_bwd_pair("ffn1_da_dw2", df1, w_ff2_1, a1, out_dtype=BF16, act_fn=square, epilogue=relu2_bwd)
    dpre = after(dpre, reduce_step(dpre, begin=("ff2_1", dw.reshape(N_DEV, -1, d))))
    dh3, dw = mm_bwd_pair_blocked("ffn1_dh_dw1", dpre, w_ff1_1, h3, out_dtype=BF16)
    dh3 = after(dh3, reduce_step(dh3, begin=("ff1_1", dw), middle="ff2_1"))
    dx3, dm1, dg_ffn_pre1, dg_mix_post1 = bwd_pre_post("bwd_3", dx4, x3, row(ffn_pre_g, 1), dh3, m1, row(mix_post_g, 1))

    dy1, dw = mm_bwd_pair("sc_dy_dwout", dm1, w_sc_out, y1, out_dtype=BF16)
    dy1 = after(dy1, reduce_step(dy1, begin=("sc_out", dw.reshape(N_DEV, -1, d)), middle="ff1_1"))
    dz1, dw_sconv = short_bwd("short_bwd", z1, dy1, w_sconv, d_short)
    dh2, dw = mm_bwd_pair_blocked("sc_dh_dwin", dz1, w_sc_in, h2, out_dtype=BF16)
    dh2 = after(dh2, reduce_step(dh2, begin=("sc_in", dw), middle="sc_out"))
    dx2, df0, dg_mix_pre1, dg_ffn_post0 = bwd_pre_post("bwd_2", dx3, x2, row(mix_pre_g, 1), dh2, f0, row(ffn_post_g, 0))

    dpre, dw = mm_bwd_pair("ffn0_da_dw2", df0, w_ff2_0, a0, out_dtype=BF16, act_fn=square, epilogue=relu2_bwd)
    dpre = after(dpre, reduce_step(dpre, begin=("ff2_0", dw.reshape(N_DEV, -1, d)), middle="sc_in"))
    dh1, dw = mm_bwd_pair_blocked("ffn0_dh_dw1", dpre, w_ff1_0, h1, out_dtype=BF16)
    dh1 = after(dh1, reduce_step(dh1, begin=("ff1_0", dw), middle="ff2_0"))
    dx1, dm0, dg_ffn_pre0, dg_mix_post0 = bwd_pre_post("bwd_1", dx2, x1, row(ffn_pre_g, 0), dh1, m0, row(mix_post_g, 0))

    dy0, dw = mm_bwd_pair("ab_dy_dwout", dm0, w_ab_out.reshape(d_pool + d_conv, d), y0, out_dtype=BF16)
    dy0 = after(dy0, reduce_step(dy0, begin=("ab_out", dw.reshape(N_DEV, -1, d)), middle="ff1_0"))
    dcv, dg_ln_g, dg_ln_b = ln_silu_bwd("ln_silu_bwd", cv, conv_ln_g, conv_ln_b, dy0, d_pool // d_conv)
    dz0, dw_conv, dg_conv_b = conv_bwd("conv_bwd", z0, dcv, w_conv, d_pool, d_conv)
    dz0, dw_pool, dg_pool_scale = pool_bwd("pool_bwd", pooled, dy0, w_pool, pool_scale, dz0)
    small_parts = [
        dw_conv.reshape(kw, N_DEV, -1).transpose(1, 0, 2).reshape(N_DEV, -1, lanes),
        dw_sconv.reshape(ks, N_DEV, -1).transpose(1, 0, 2).reshape(N_DEV, -1, lanes),
        dw_pool.reshape(ng, N_DEV, pg // N_DEV, pg).transpose(1, 0, 2, 3).reshape(N_DEV, -1, lanes),
    ]
    small = jnp.pad(jnp.concatenate(small_parts, axis=1), ((0, 0), (0, small_total - r2), (0, 0)))
    dz0 = after(dz0, reduce_step(dz0, begin=("small", small), middle="ab_out"))
    dh0, dw = mm_bwd_pair_blocked("ab_dh_dwin", dz0, w_ab_in, h0, out_dtype=BF16)
    dh0 = after(dh0, reduce_step(dh0, begin=("ab_in", dw), middle="small"))
    grad_x, dg_mix_pre0 = bwd_pre_final("bwd_0", dx1, xs, row(mix_pre_g, 0), dh0)
    tok = reduce_step(grad_x, middle="ab_in")

    fold = lambda a: jnp.sum(a, axis=0, keepdims=True)
    rep_rows = [fold(dg_mix_pre0), fold(dg_mix_pre1), fold(dg_mix_post0), fold(dg_mix_post1),
                fold(dg_ffn_pre0), fold(dg_ffn_pre1), fold(dg_ffn_post0), fold(dg_ffn_post1)]
    tail = jnp.concatenate([dg_pool_scale, dg_conv_b, fold(dg_ln_g), fold(dg_ln_b)], axis=1).reshape(-1, d)
    rep = _pad_rows(jnp.concatenate(rep_rows + [tail], axis=0), 16)
    (rep_hop,), _ = copies_start("rep_start", [[place_shard("place_rep", rep[None], 0, F32)]], _first_hop, 4)

    def pack_rep(a_mix_pre, a_mix_post, a_ffn_pre, a_ffn_post, a_scale, a_b, a_g, a_lb):
        tail_ = jnp.concatenate([a_scale, a_b, a_g, a_lb], axis=1).reshape(-1, d)
        return _pad_rows(jnp.concatenate([a_mix_pre, a_mix_post, a_ffn_pre, a_ffn_post, tail_], axis=0), 16)

    def upd(name, w, m, v, contribs):
        shape = w.shape
        flat2 = lambda a: a.reshape(-1, shape[-1])
        outs = adamw(name, flat2(w), flat2(m), flat2(v), contribs)
        return [o.reshape(shape) for o in outs]

    g_ff2 = [reduce_end("ff2_0", tok), reduce_end("ff2_1", tok)]
    o_ff2 = upd("adam_ffn_w2", ffn_w2, m_ffn_w2, v_ffn_w2, g_ff2)
    (rep_zone,) = copies_wait("rep_wait", rep_hop, _first_hop, o_ff2[0])
    (rep_hop,), _ = copies_start("rep_forward_start", [[rep_zone]], _second_hop, 3)
    g_ff1 = [reduce_end("ff1_0", o_ff2[0]), reduce_end("ff1_1", o_ff2[0])]
    o_ff1 = upd("adam_ffn_w1", ffn_w1, m_ffn_w1, v_ffn_w1, g_ff1)
    (rep_all,) = copies_wait("rep_forward_wait", rep_hop, _second_hop, o_ff1[0])
    o_rep = adamw("adam_replicated",
                  pack_rep(mix_pre_g, mix_post_g, ffn_pre_g, ffn_post_g, pool_scale, conv_b, conv_ln_g, conv_ln_b),
                  pack_rep(m_mix_pre_g, m_mix_post_g, m_ffn_pre_g, m_ffn_post_g, m_pool_scale, m_conv_b, m_conv_ln_g, m_conv_ln_b),
                  pack_rep(v_mix_pre_g, v_mix_post_g, v_ffn_pre_g, v_ffn_post_g, v_pool_scale, v_conv_b, v_conv_ln_g, v_conv_ln_b),
                  [rep_all])
    o_sc_out = upd("adam_sc_out", sc_w_out, m_sc_w_out, v_sc_w_out, [reduce_end("sc_out", o_ff1[0])])
    o_sc_in = upd("adam_sc_in", sc_w_in, m_sc_w_in, v_sc_w_in, [reduce_end("sc_in", o_sc_out[0])])
    o_ab_out = upd("adam_ab_out", ab_w_out, m_ab_w_out, v_ab_w_out, [reduce_end("ab_out", o_sc_in[0])])
    o_small = adamw("adam_small", pack_small(conv_w, sc_conv_w, pool_w), pack_small(m_conv_w, m_sc_conv_w, m_pool_w),
                    pack_small(v_conv_w, v_sc_conv_w, v_pool_w), [reduce_end("small", o_ab_out[0])])
    o_ab_in = upd("adam_ab_in", ab_w_in, m_ab_w_in, v_ab_w_in, [reduce_end("ab_in", o_small[0])])

    def unpack_small(o):
        return o[:r0].reshape(conv_w.shape), o[r0:r1].reshape(sc_conv_w.shape), o[r1:r2].reshape(pool_w.shape)

    def unpack_rep(o):
        tail_ = o[8:8 + tail.shape[0]].reshape(1, -1)
        n1 = d_pool
        return dict(mix_pre_g=o[0:2], mix_post_g=o[2:4], ffn_pre_g=o[4:6], ffn_post_g=o[6:8],
                    pool_scale=tail_[:, :n1], conv_b=tail_[:, n1:n1 + d_conv],
                    conv_ln_g=tail_[:, n1 + d_conv:n1 + 2 * d_conv], conv_ln_b=tail_[:, n1 + 2 * d_conv:n1 + 3 * d_conv])

    results = []
    for kind in range(4):
        rep_o = unpack_rep(o_rep[kind])
        s_conv, s_sconv, s_pool = unpack_small(o_small[kind])
        results.append([
            rep_o["mix_pre_g"], rep_o["mix_post_g"], rep_o["ffn_pre_g"], rep_o["ffn_post_g"],
            o_ab_in[kind], s_pool, rep_o["pool_scale"], s_conv, rep_o["conv_b"], rep_o["conv_ln_g"], rep_o["conv_ln_b"],
            o_ab_out[kind], o_sc_in[kind], s_sconv, o_sc_out[kind], o_ff1[kind], o_ff2[kind]])

    return (loss, grad_x[None], *results[0], *results[1], *results[2], *results[3])
```

```python
import jax
import jax.numpy as jnp
from jax import lax
from jax.experimental import pallas as pl
from jax.experimental.pallas import tpu as pltpu

F32 = jnp.float32
BF16 = jnp.bfloat16
MESH = pl.DeviceIdType.MESH
ANY = pl.BlockSpec(memory_space=pl.ANY)

NORM_EPS = 1e-6
POOL_WINDOWS = (2, 4, 8, 16)
ADAM_LR = 0.001
ADAM_B1 = 0.9
ADAM_B2 = 0.999
ADAM_EPS = 1e-08
ADAM_WD = 0.01
ADAM_STEP = 10

N_DEV = 8
VMEM_LIMIT = 56 * 1024 * 1024
PAIR_ADD_BLOCK = 1 << 20
MATMUL_ROWS = 2048
ROW_TILE = 256
CHANNEL_TILE = 256
TIME_CHUNK = 64
HALO = 32

NN = (((1,), (0,)), ((), ()))
NT = (((1,), (1,)), ((), ()))
TN = (((0,), (0,)), ((), ()))


def _params(sem):
    return pltpu.CompilerParams(dimension_semantics=sem, vmem_limit_bytes=VMEM_LIMIT)


def _place():
    x, y, c = lax.axis_index("x"), lax.axis_index("y"), lax.axis_index("c")
    return x, y, c


def _slot(px, py, pc):
    return 4 * px + 2 * py + pc


HBM = pl.BlockSpec(memory_space=pltpu.HBM)
SEM = pl.BlockSpec(memory_space=pltpu.SEMAPHORE)
EFFECT = pltpu.SideEffectType.DATAFLOW_SIDE_EFFECTING
TOKEN = jax.ShapeDtypeStruct((8, 128), F32)


def _in_hbm(a):
    return pltpu.with_memory_space_constraint(a, pltpu.HBM)


CHIPS = [(0, 0), (0, 1), (1, 0), (1, 1)]
N_CHIP = len(CHIPS)


def _chip(px, py):
    return 2 * px + py


def _first_hop(bufs, sends, recvs, waiting):
    (land,) = bufs
    x, y, c = _place()
    me = _slot(x, y, c)
    peers = [(x, y, 1 - c), (1 - x, y, c), (x, 1 - y, c), (1 - x, 1 - y, c)]
    return [pltpu.make_async_remote_copy(
        src_ref=land.at[me], dst_ref=land.at[_slot(*p) if waiting else me],
        send_sem=sends.at[k], recv_sem=recvs.at[k], device_id=p, device_id_type=MESH) for k, p in enumerate(peers)]


def _second_hop(bufs, sends, recvs, waiting):
    (land,) = bufs
    x, y, c = _place()
    return [pltpu.make_async_remote_copy(
        src_ref=land.at[_slot(px, py, c)], dst_ref=land.at[_slot(px, py, 1 - c if waiting else c)],
        send_sem=sends.at[k], recv_sem=recvs.at[k], device_id=(x, y, 1 - c), device_id_type=MESH)
        for k, (px, py) in enumerate([(1 - x, y), (x, 1 - y), (1 - x, 1 - y)])]


def _ring_hop1(bufs, sends, recvs, waiting):
    (land,) = bufs
    x, y, c = _place()
    me = _slot(x, y, c)
    peers = [(1 - x, y, c), (x, 1 - y, c), (x, y, 1 - c)]
    return [pltpu.make_async_remote_copy(
        src_ref=land.at[me], dst_ref=land.at[_slot(*p) if waiting else me],
        send_sem=sends.at[k], recv_sem=recvs.at[k], device_id=p, device_id_type=MESH) for k, p in enumerate(peers)]


def _ring_hop2(bufs, sends, recvs, waiting):
    (land,) = bufs
    x, y, c = _place()
    half = land.shape[1] // 2
    first, second = pl.ds(0, half), pl.ds(half, half)
    nx, ny, diag = _slot(1 - x, y, c), _slot(x, 1 - y, c), _slot(1 - x, 1 - y, c)
    plan = [
        (land.at[ny, first], land.at[diag, first], (1 - x, y, c)),
        (land.at[nx, second], land.at[diag, second], (x, 1 - y, c)),
        (land.at[nx], land.at[_slot(1 - x, y, 1 - c)], (x, y, 1 - c)),
        (land.at[ny], land.at[_slot(x, 1 - y, 1 - c)], (x, y, 1 - c))]
    return [pltpu.make_async_remote_copy(
        src_ref=src, dst_ref=mine if waiting else src, send_sem=sends.at[k], recv_sem=recvs.at[k],
        device_id=to, device_id_type=MESH) for k, (src, mine, to) in enumerate(plan)]


def _ring_hop3(bufs, sends, recvs, waiting):
    (land,) = bufs
    x, y, c = _place()
    return [pltpu.make_async_remote_copy(
        src_ref=land.at[_slot(1 - x, 1 - y, c)], dst_ref=land.at[_slot(1 - x, 1 - y, 1 - c if waiting else c)],
        send_sem=sends.at[0], recv_sem=recvs.at[0], device_id=(x, y, 1 - c), device_id_type=MESH)]


def _pair_hop(bufs, sends, recvs, waiting):
    g, land = bufs
    x, y, c = _place()
    return [pltpu.make_async_remote_copy(
        src_ref=g.at[_slot(qx, qy, 1 - c)], dst_ref=land.at[q],
        send_sem=sends.at[q], recv_sem=recvs.at[q], device_id=(x, y, 1 - c), device_id_type=MESH)
        for q, (qx, qy) in enumerate(CHIPS)]


def _chip_hop(bufs, sends, recvs, waiting):
    p, land = bufs
    x, y, c = _place()
    return [pltpu.make_async_remote_copy(
        src_ref=p.at[_chip(px, py)], dst_ref=land.at[_chip(px, py) if waiting else _chip(x, y)],
        send_sem=sends.at[k], recv_sem=recvs.at[k], device_id=(px, py, c), device_id_type=MESH)
        for k, (px, py) in enumerate([(1 - x, y), (x, 1 - y), (1 - x, 1 - y)])]


def copies_start(name, groups, hop, n_copies, deps=()):
    flat = [b for grp in groups for b in grp]
    nb, ng = len(flat), len(groups)
    deps = list(deps)
    hops = list(hop) if isinstance(hop, (list, tuple)) else [hop] * ng
    counts = list(n_copies) if isinstance(n_copies, (list, tuple)) else [n_copies] * ng

    def body(*refs):
        ins, token = refs[:nb], refs[-1]
        sems = refs[nb + len(deps):nb + len(deps) + 2 * ng]
        i = 0
        for gi, grp in enumerate(groups):
            for cp in hops[gi](ins[i:i + len(grp)], sems[2 * gi], sems[2 * gi + 1], False):
                cp.start()
            i += len(grp)
        token[...] = jnp.zeros_like(token)

    outs = pl.pallas_call(
        body, name=name,
        out_shape=([pltpu.SemaphoreType.DMA((n,)) for n in counts for _ in range(2)]
                   + [pltpu.HBM(b.shape, b.dtype) for b in flat] + [TOKEN]),
        in_specs=[HBM] * nb + [ANY] * len(deps),
        out_specs=[SEM] * (2 * ng) + [HBM] * nb + [pl.BlockSpec(memory_space=pltpu.VMEM)],
        input_output_aliases={i: 2 * ng + i for i in range(nb)},
        compiler_params=pltpu.CompilerParams(has_side_effects=EFFECT),
    )(*[_in_hbm(b) for b in flat], *deps)
    started, i = [], 0
    for gi, grp in enumerate(groups):
        started.append((outs[2 * gi], outs[2 * gi + 1], list(outs[2 * ng + i:2 * ng + i + len(grp)])))
        i += len(grp)
    return started, outs[-1]


def copies_wait(name, started, hop, after):
    sends, recvs, bufs = started
    nb = len(bufs)

    def body(*refs):
        for cp in hop(refs[:nb], refs[nb], refs[nb + 1], True):
            cp.wait_send()
            cp.wait_recv()

    outs = pl.pallas_call(
        body, name=name,
        out_shape=[pltpu.HBM(b.shape, b.dtype) for b in bufs],
        in_specs=[HBM] * nb + [SEM, SEM, ANY], out_specs=[HBM] * nb,
        input_output_aliases={i: i for i in range(nb)},
        compiler_params=pltpu.CompilerParams(has_side_effects=EFFECT),
    )(*bufs, sends, recvs, after)
    return list(outs)


def place_shard(name, w, layer, dtype, deps=()):
    _, r, c = w.shape
    tr = _tile(r, 1024)
    x, y, core = _place()
    me = _slot(x, y, core).astype(jnp.int32).reshape(1)

    def body(me_ref, w_ref, *rest):
        rest[-1][...] = w_ref[...].astype(dtype)

    return pl.pallas_call(
        body, name=name,
        grid_spec=pltpu.PrefetchScalarGridSpec(
            num_scalar_prefetch=1, grid=(r // tr,),
            in_specs=[pl.BlockSpec((None, tr, c), lambda i, me_ref: (layer, i, 0))] + [ANY] * len(deps),
            out_specs=pl.BlockSpec((None, tr, c), lambda i, me_ref: (me_ref[0], i, 0))),
        out_shape=jax.ShapeDtypeStruct((N_DEV, r, c), dtype),
        compiler_params=_params(("parallel",)),
    )(me, w, *deps)


def tie(name, x, *deps):
    def body(*refs):
        del refs

    return pl.pallas_call(
        body, name=name, out_shape=jax.ShapeDtypeStruct(x.shape, x.dtype),
        in_specs=[ANY] * (1 + len(deps)), out_specs=ANY, input_output_aliases={0: 0},
    )(x, *deps)


def pair_add(name, g, from_sibling):
    _, r, c_dim = g.shape
    tr = r
    while tr * c_dim > PAIR_ADD_BLOCK and tr % 16 == 0:
        tr //= 2
    x, y, core = _place()
    where = jnp.stack([core, _chip(x, y)]).astype(jnp.int32)

    def body(where_ref, g_ref, s_ref, o_ref, zone_ref):
        total = (g_ref[...].astype(F32) + s_ref[...].astype(F32)).astype(o_ref.dtype)
        o_ref[...] = total

        @pl.when(pl.program_id(1) == where_ref[1])
        def _():
            zone_ref[...] = total

    blk = pl.BlockSpec((None, tr, c_dim), lambda i, q, where_ref: (q, i, 0))
    return pl.pallas_call(
        body, name=name,
        grid_spec=pltpu.PrefetchScalarGridSpec(
            num_scalar_prefetch=1, grid=(r // tr, N_CHIP),
            in_specs=[pl.BlockSpec((None, None, tr, c_dim), lambda i, q, where_ref: (q, where_ref[0], i, 0)), blk],
            out_specs=[blk, pl.BlockSpec((None, tr, c_dim), lambda i, q, where_ref: (where_ref[1], i, 0))]),
        out_shape=[jax.ShapeDtypeStruct((N_CHIP, r, c_dim), g.dtype)] * 2,
        compiler_params=_params(("parallel", "arbitrary")),
    )(where, g.reshape(N_CHIP, 2, r, c_dim), from_sibling)


def _matmul(name, lhs, rhs, *, out_shape, out_dtype, grid, lhs_spec, rhs_spec, out_spec, acc_shape,
            lhs_fn=None, epilogue=None):
    nk = grid[2]

    def body(lhs_ref, rhs_ref, out_ref, *scratch):
        def product():
            a = lhs_ref[...]
            if lhs_fn is not None:
                a = lhs_fn(a)
            return lax.dot_general(a, rhs_ref[...], NN, preferred_element_type=F32)

        def finish(r):
            if epilogue is not None:
                r = epilogue(r)
            out_ref[...] = r.astype(out_dtype)

        if nk == 1:
            finish(product())
        else:
            (acc_ref,) = scratch
            k = pl.program_id(2)

            @pl.when(k == 0)
            def _():
                acc_ref[...] = product()

            @pl.when(jnp.logical_and(k > 0, k < nk - 1))
            def _():
                acc_ref[...] += product()

            @pl.when(k == nk - 1)
            def _():
                finish(acc_ref[...] + product())

    return pl.pallas_call(
        body, name=name, grid=grid,
        out_shape=jax.ShapeDtypeStruct(out_shape, out_dtype),
        in_specs=[lhs_spec, rhs_spec], out_specs=out_spec,
        scratch_shapes=[pltpu.VMEM(acc_shape, F32)] if nk > 1 else [],
        compiler_params=_params(("parallel", "parallel", "arbitrary")),
    )(lhs, rhs)


def _tile(n, want):
    return want if n % want == 0 else n


def mm_nn(name, x, w, *, out_dtype, tn=512, tk=None, lhs_fn=None, epilogue=None):
    t, kdim = x.shape
    n = w.shape[1]
    tm, tn = _tile(t, MATMUL_ROWS), _tile(n, tn)
    tk = kdim if tk is None else _tile(kdim, tk)
    return _matmul(
        name, x, w, out_shape=(t, n), out_dtype=out_dtype, grid=(t // tm, n // tn, kdim // tk),
        lhs_spec=pl.BlockSpec((tm, tk), lambda i, j, k: (i, k)),
        rhs_spec=pl.BlockSpec((tk, tn), lambda i, j, k: (k, j)),
        out_spec=pl.BlockSpec((tm, tn), lambda i, j, k: (i, j)),
        acc_shape=(tm, tn), lhs_fn=lhs_fn, epilogue=epilogue)


def mm_nn_blocked(name, x, w, *, out_dtype, epilogue=None):
    t, kdim = x.shape
    nb = w.shape[2]
    tm = _tile(t, MATMUL_ROWS)
    tn = nb // 2 if nb >= 1024 else nb
    sub = nb // tn
    return _matmul(
        name, x, w, out_shape=(t, N_DEV * nb), out_dtype=out_dtype, grid=(t // tm, N_DEV * sub, 1),
        lhs_spec=pl.BlockSpec((tm, kdim), lambda i, j, k: (i, k)),
        rhs_spec=pl.BlockSpec((None, kdim, tn), lambda i, j, k: (j // sub, k, j % sub)),
        out_spec=pl.BlockSpec((tm, tn), lambda i, j, k: (i, j)),
        acc_shape=(tm, tn), epilogue=epilogue)


def mm_bwd_pair(name, dy, w, act, *, out_dtype, tile=512, act_fn=None, epilogue=None):
    t, n = dy.shape
    kdim = w.shape[0]
    tile = _tile(kdim, tile)

    def body(dy_ref, w_ref, act_ref, dx_ref, dw_ref):
        a = act_ref[...]
        dx = lax.dot_general(dy_ref[...], w_ref[...], NT, preferred_element_type=F32)
        if epilogue is not None:
            dx = epilogue(dx, a)
        dx_ref[...] = dx.astype(out_dtype)
        if act_fn is not None:
            a = act_fn(a)
        dw_ref[...] = lax.dot_general(a, dy_ref[...], TN, preferred_element_type=F32).astype(out_dtype)

    return pl.pallas_call(
        body, name=name, grid=(kdim // tile,),
        in_specs=[pl.BlockSpec((t, n), lambda j: (0, 0)), pl.BlockSpec((tile, n), lambda j: (j, 0)),
                  pl.BlockSpec((t, tile), lambda j: (0, j))],
        out_specs=[pl.BlockSpec((t, tile), lambda j: (0, j)), pl.BlockSpec((tile, n), lambda j: (j, 0))],
        out_shape=[jax.ShapeDtypeStruct((t, kdim), out_dtype), jax.ShapeDtypeStruct((kdim, n), out_dtype)],
        compiler_params=_params(("parallel",)),
    )(dy, w, act)


def mm_bwd_pair_blocked(name, dz, w, act, *, out_dtype, tile=1024):
    t = dz.shape[0]
    kdim, nb = w.shape[1], w.shape[2]
    tile = _tile(kdim, tile)

    def body(dz_ref, w_ref, act_ref, dx_ref, dw_ref, acc_ref):
        j = pl.program_id(1)
        dw_ref[...] = lax.dot_general(act_ref[...], dz_ref[...], TN, preferred_element_type=F32).astype(out_dtype)

        def product():
            return lax.dot_general(dz_ref[...], w_ref[...], NT, preferred_element_type=F32)

        @pl.when(j == 0)
        def _():
            acc_ref[...] = product()

        @pl.when(jnp.logical_and(j > 0, j < N_DEV - 1))
        def _():
            acc_ref[...] += product()

        @pl.when(j == N_DEV - 1)
        def _():
            dx_ref[...] = (acc_ref[...] + product()).astype(out_dtype)

    return pl.pallas_call(
        body, name=name, grid=(kdim // tile, N_DEV),
        in_specs=[pl.BlockSpec((t, nb), lambda i, j: (0, j)), pl.BlockSpec((None, tile, nb), lambda i, j: (j, i, 0)),
                  pl.BlockSpec((t, tile), lambda i, j: (0, i))],
        out_specs=[pl.BlockSpec((t, tile), lambda i, j: (0, i)),
                   pl.BlockSpec((None, tile, nb), lambda i, j: (j, i, 0))],
        out_shape=[jax.ShapeDtypeStruct((t, kdim), out_dtype), jax.ShapeDtypeStruct((N_DEV, kdim, nb), out_dtype)],
        scratch_shapes=[pltpu.VMEM((t, tile), F32)],
        compiler_params=_params(("parallel", "arbitrary")),
    )(dz, w, act)


def _rstd(v):
    return lax.rsqrt(jnp.mean(v * v, axis=-1, keepdims=True) + NORM_EPS)


def _rms_bwd(v, g, dy):
    r = _rstd(v)
    vhat = v * r
    dvh = dy * g
    dv = r * (dvh - vhat * jnp.mean(dvh * vhat, axis=-1, keepdims=True))
    return dv, dy * vhat


def _fold8(v):
    rows, n = v.shape
    return jnp.sum(v.reshape(rows // 8, 8, n), axis=0)


def _fold_lanes(v):
    out = v[:, 0:128]
    for i in range(1, v.shape[1] // 128):
        out = out + v[:, 128 * i:128 * (i + 1)]
    return out


def _accumulate(ref, v):
    i = pl.program_id(0)

    @pl.when(i == 0)
    def _():
        ref[...] = v

    @pl.when(i > 0)
    def _():
        ref[...] += v


def _row_call(body, name, t, ins, row_in, outs, acc_outs=(), tr=ROW_TILE):
    tr = _tile(t, tr)

    def in_spec(a, tiled):
        if isinstance(tiled, tuple):
            width, j = tiled
            return pl.BlockSpec((tr, width), lambda i: (i, j))
        return pl.BlockSpec((tr, a.shape[1]), lambda i: (i, 0)) if tiled else pl.BlockSpec(a.shape, lambda i: (0, 0))

    in_specs = [in_spec(a, tiled) for a, tiled in zip(ins, row_in)]
    out_specs = [pl.BlockSpec((tr, n), lambda i: (i, 0)) for n, _ in outs]
    out_specs += [pl.BlockSpec((8, n), lambda i: (0, 0)) for n in acc_outs]
    out_shape = [jax.ShapeDtypeStruct((t, n), dt) for n, dt in outs]
    out_shape += [jax.ShapeDtypeStruct((8, n), F32) for n in acc_outs]
    return pl.pallas_call(
        body, name=name, grid=(t // tr,), in_specs=in_specs, out_specs=out_specs, out_shape=out_shape,
        compiler_params=_params(("arbitrary",) if acc_outs else ("parallel",)),
    )(*ins)


def norm_pre(name, x, g):
    t, d = x.shape

    def body(x_ref, g_ref, h_ref):
        v = x_ref[...]
        h_ref[...] = (v * _rstd(v) * g_ref[...]).astype(BF16)

    return _row_call(body, name, t, [x, g], [True, False], [(d, BF16)])[0]


def post_pre(name, x, m, g_post, g_pre):
    t, d = x.shape

    def body(x_ref, m_ref, gp_ref, gn_ref, xo_ref, h_ref):
        mv = m_ref[...]
        xn = x_ref[...] + mv * _rstd(mv) * gp_ref[...]
        xo_ref[...] = xn
        h_ref[...] = (xn * _rstd(xn) * gn_ref[...]).astype(BF16)

    return _row_call(body, name, t, [x, m, g_post, g_pre], [True, True, False, False], [(d, F32), (d, BF16)])


def post_loss(name, x, f, g_post, target):
    t, d = x.shape

    def body(x_ref, f_ref, g_ref, t_ref, dx_ref, df_ref, loss_ref, dg_ref):
        fv = f_ref[...]
        g = g_ref[...]
        out = x_ref[...] + fv * _rstd(fv) * g
        err = out - t_ref[...]
        dx = err * (1.0 / d)
        dx_ref[...] = dx
        dfv, dg_rows = _rms_bwd(fv, g, dx)
        df_ref[...] = dfv.astype(BF16)
        _accumulate(loss_ref, _fold8(_fold_lanes(err * err)))
        _accumulate(dg_ref, _fold8(dg_rows))

    return _row_call(body, name, t, [x, f, g_post, target], [True, True, False, True],
                     [(d, F32), (d, BF16)], acc_outs=(128, d))


def bwd_pre_post(name, dx_out, x_in, g_pre, dh, f_prev, g_post_prev):
    t, d = x_in.shape

    def body(dxo_ref, x_ref, gpre_ref, dh_ref, f_ref, gpost_ref, dxi_ref, df_ref, dgpre_ref, dgpost_ref):
        dxv, dgpre_rows = _rms_bwd(x_ref[...], gpre_ref[...], dh_ref[...].astype(F32))
        dxi = dxo_ref[...] + dxv
        dxi_ref[...] = dxi
        dfv, dgpost_rows = _rms_bwd(f_ref[...], gpost_ref[...], dxi)
        df_ref[...] = dfv.astype(BF16)
        _accumulate(dgpre_ref, _fold8(dgpre_rows))
        _accumulate(dgpost_ref, _fold8(dgpost_rows))

    return _row_call(body, name, t, [dx_out, x_in, g_pre, dh, f_prev, g_post_prev],
                     [True, True, False, True, True, False], [(d, F32), (d, BF16)], acc_outs=(d, d))


def bwd_pre_final(name, dx_out, x_in, g_pre, dh):
    t, d = x_in.shape

    def body(dxo_ref, x_ref, gpre_ref, dh_ref, dxi_ref, dgpre_ref):
        dxv, dgpre_rows = _rms_bwd(x_ref[...], gpre_ref[...], dh_ref[...].astype(F32))
        dxi_ref[...] = dxo_ref[...] + dxv
        _accumulate(dgpre_ref, _fold8(dgpre_rows))

    return _row_call(body, name, t, [dx_out, x_in, g_pre, dh], [True, True, False, True], [(d, F32)], acc_outs=(d,))


def _layer_norm_parts(cv):
    mu = jnp.mean(cv, axis=-1, keepdims=True)
    xc = cv - mu
    rstd = lax.rsqrt(jnp.mean(xc * xc, axis=-1, keepdims=True) + NORM_EPS)
    return xc * rstd, rstd


def ln_silu(name, cv, g, b, y, y_block):
    t, n = cv.shape
    tr = _tile(t, ROW_TILE)

    def body(c_ref, g_ref, b_ref, y_in_ref, y_ref):
        chat, _ = _layer_norm_parts(c_ref[...])
        ln = chat * g_ref[...] + b_ref[...]
        y_ref[...] = (ln * jax.nn.sigmoid(ln)).astype(BF16)

    vec = pl.BlockSpec((1, n), lambda i: (0, 0))
    return pl.pallas_call(
        body, name=name, grid=(t // tr,),
        in_specs=[pl.BlockSpec((tr, n), lambda i: (i, 0)), vec, vec, ANY],
        out_specs=pl.BlockSpec((tr, n), lambda i: (i, y_block)),
        out_shape=jax.ShapeDtypeStruct(y.shape, y.dtype), input_output_aliases={3: 0},
        compiler_params=_params(("parallel",)),
    )(cv, g, b, y)


def ln_silu_bwd(name, cv, g, b, dy, dy_block):
    t, n = cv.shape

    def body(c_ref, g_ref, b_ref, dy_ref, dc_ref, dg_ref, db_ref):
        chat, rstd = _layer_norm_parts(c_ref[...])
        g = g_ref[...]
        ln = chat * g + b_ref[...]
        s = jax.nn.sigmoid(ln)
        dln = dy_ref[...].astype(F32) * (s * (1.0 + ln * (1.0 - s)))
        dchat = dln * g
        dc_ref[...] = rstd * (dchat - jnp.mean(dchat, axis=-1, keepdims=True)
                              - chat * jnp.mean(dchat * chat, axis=-1, keepdims=True))
        _accumulate(dg_ref, _fold8(dln * chat))
        _accumulate(db_ref, _fold8(dln))

    return _row_call(body, name, t, [cv, g, b, dy], [True, False, False, (n, dy_block)], [(n, F32)], acc_outs=(n, n))


def _chunks(t, fn, tc=TIME_CHUNK):
    tc = _tile(t, tc)

    def step(i, carry):
        fn(pl.multiple_of(i * tc, tc), tc)
        return carry

    lax.fori_loop(0, t // tc, step, 0)


def _rows_from(v, start, n):
    res = start % 8
    base = v if res == 0 else pltpu.roll(v, v.shape[0] - res, axis=0)
    return base[start - res:start - res + n, :]


def _shifted(window, offsets, tc):
    rows = window.shape[0]
    by_residue = {}
    for k, off in enumerate(offsets):
        by_residue.setdefault(off % 8, []).append((k, off))
    for res, taps in by_residue.items():
        base = window if res == 0 else pltpu.roll(window, rows - res, axis=0)
        for k, off in taps:
            yield k, base[off - res:off - res + tc, :]


def _taps(window, w_ref, offsets, tc, flip=False):
    acc = None
    for k, rows in _shifted(window, offsets, tc):
        kk = len(offsets) - 1 - k if flip else k
        term = w_ref[kk:kk + 1, :] * rows
        acc = term if acc is None else acc + term
    return acc


def _window_sums(win, tc, causal):
    sums = []
    cur, rows, step = win, tc + HALO, 1
    for _ in POOL_WINDOWS:
        rows -= 8
        if causal:
            cur = cur[8:8 + rows, :] + _rows_from(cur, 8 - step, rows)
            sums.append(cur[rows - tc:rows, :])
        else:
            cur = cur[0:rows, :] + _rows_from(cur, step, rows)
            sums.append(cur[0:tc, :])
        step *= 2
    return sums


def _pick(vals, g):
    out = vals[-1]
    for i in range(len(vals) - 2, -1, -1):
        out = jnp.where(g == i, vals[i], out)
    return out


def _pool_count(s, tc, g):
    t1 = (lax.broadcasted_iota(jnp.int32, (tc, 1), 0) + (s + 1)).astype(F32)
    width = _pick([float(w) for w in POOL_WINDOWS], g)
    return jnp.minimum(t1, width)


def pool_fwd(name, z, pool_w, pool_scale, d_pool, y_width):
    t = z.shape[0]
    ng, pg = pool_w.shape[0], pool_w.shape[1]

    def body(u_ref, w_ref, s_ref, pooled_ref, y_ref, pad):
        g = pl.program_id(0)
        pad[pl.ds(0, HALO), :] = jnp.zeros((HALO, pg), F32)

        def fill(s, tc):
            pad[pl.ds(HALO + s, tc), :] = u_ref[pl.ds(s, tc), :].astype(F32)

        def chunk(s, tc):
            win = pad[pl.ds(s, tc + HALO), :]
            total = _pick(_window_sums(win, tc, causal=True), g)
            pooled = total / _pool_count(s, tc, g) - win[HALO:HALO + tc, :]
            pooled_ref[pl.ds(s, tc), :] = pooled.astype(BF16)

        _chunks(t, fill)
        _chunks(t, chunk)
        mixed = jnp.dot(pooled_ref[...], w_ref[...], preferred_element_type=F32)
        y_ref[...] = (mixed * s_ref[...]).astype(BF16)

    col = pl.BlockSpec((t, pg), lambda g: (0, g))
    return pl.pallas_call(
        body, name=name, grid=(ng,),
        in_specs=[col, pl.BlockSpec((None, pg, pg), lambda g: (g, 0, 0)), pl.BlockSpec((1, pg), lambda g: (0, g))],
        out_specs=[col, col],
        out_shape=[jax.ShapeDtypeStruct((t, d_pool), BF16), jax.ShapeDtypeStruct((t, y_width), BF16)],
        scratch_shapes=[pltpu.VMEM((t + HALO, pg), F32)],
        compiler_params=_params(("parallel",)),
    )(z, pool_w, pool_scale)


def pool_bwd(name, pooled, dy, pool_w, pool_scale, dz):
    t, d_pool = pooled.shape
    ng, pg = pool_w.shape[0], pool_w.shape[1]

    def body(p_ref, dy_ref, w_ref, s_ref, dz_ref, du_ref, dw_ref, ds_ref, pad):
        g = pl.program_id(0)
        w = w_ref[...]
        dyv = dy_ref[...].astype(F32)
        mixed = jnp.dot(p_ref[...], w, preferred_element_type=F32)
        ds_ref[...] = jnp.sum(dyv * mixed, axis=0, keepdims=True)
        dmixed = (dyv * s_ref[...]).astype(BF16)
        dw_ref[...] = lax.dot_general(p_ref[...], dmixed, TN, preferred_element_type=F32)
        pad[...] = jnp.zeros((t + HALO, pg), F32)
        pad[pl.ds(0, t), :] = lax.dot_general(dmixed, w, NT, preferred_element_type=F32)

        def scale(s, tc):
            pad[pl.ds(s, tc), :] = pad[pl.ds(s, tc), :] / _pool_count(s, tc, g)

        def chunk(s, tc):
            win = pad[pl.ds(s, tc + HALO), :]
            total = _pick(_window_sums(win, tc, causal=False), g)
            du_ref[pl.ds(s, tc), :] = (total - win[0:tc, :] * _pool_count(s, tc, g)).astype(BF16)

        _chunks(t, scale)
        _chunks(t, chunk)

    col = pl.BlockSpec((t, pg), lambda g: (0, g))
    vec = pl.BlockSpec((1, pg), lambda g: (0, g))
    mat = pl.BlockSpec((None, pg, pg), lambda g: (g, 0, 0))
    return pl.pallas_call(
        body, name=name, grid=(ng,),
        in_specs=[col, col, mat, vec, ANY], out_specs=[col, mat, vec],
        out_shape=[jax.ShapeDtypeStruct(dz.shape, dz.dtype), jax.ShapeDtypeStruct((ng, pg, pg), F32),
                   jax.ShapeDtypeStruct((1, d_pool), F32)],
        input_output_aliases={4: 0},
        scratch_shapes=[pltpu.VMEM((t + HALO, pg), F32)],
        compiler_params=_params(("parallel",)),
    )(pooled, dy, pool_w, pool_scale, dz)


def conv_fwd(name, z, conv_w, conv_b, d_pool, d_conv):
    t = z.shape[0]
    kw = conv_w.shape[0]
    tc_ch = _tile(d_conv, CHANNEL_TILE)
    v0, g0 = d_pool // tc_ch, (d_pool + d_conv) // tc_ch

    def body(v_ref, g_ref, w_ref, b_ref, c_ref, pad):
        pad[pl.ds(0, HALO), :] = jnp.zeros((HALO, tc_ch), F32)

        def fill(s, tc):
            pad[pl.ds(HALO + s, tc), :] = v_ref[pl.ds(s, tc), :].astype(F32) * jax.nn.sigmoid(g_ref[pl.ds(s, tc), :].astype(F32))

        def chunk(s, tc):
            win = pad[pl.ds(s, tc + HALO), :]
            c_ref[pl.ds(s, tc), :] = _taps(win, w_ref, [HALO - (kw - 1) + k for k in range(kw)], tc) + b_ref[...]

        _chunks(t, fill)
        _chunks(t, chunk)

    return pl.pallas_call(
        body, name=name, grid=(d_conv // tc_ch,),
        in_specs=[pl.BlockSpec((t, tc_ch), lambda j: (0, v0 + j)), pl.BlockSpec((t, tc_ch), lambda j: (0, g0 + j)),
                  pl.BlockSpec((kw, tc_ch), lambda j: (0, j)), pl.BlockSpec((1, tc_ch), lambda j: (0, j))],
        out_specs=pl.BlockSpec((t, tc_ch), lambda j: (0, j)),
        out_shape=jax.ShapeDtypeStruct((t, d_conv), F32),
        scratch_shapes=[pltpu.VMEM((t + HALO, tc_ch), F32)],
        compiler_params=_params(("parallel",)),
    )(z, z, conv_w, conv_b)


def conv_bwd(name, z, dc, conv_w, d_pool, d_conv):
    t = z.shape[0]
    kw = conv_w.shape[0]
    tc_ch = _tile(d_conv, CHANNEL_TILE)
    v0, g0 = d_pool // tc_ch, (d_pool + d_conv) // tc_ch

    def body(v_ref, g_ref, dc_ref, w_ref, dz_ref, dw_ref, db_ref, pad_a, pad_dc, acc_w, acc_b, tiles, sems):
        j = pl.program_id(0)
        dv_ref, dg_ref = tiles.at[0], tiles.at[1]
        writes = [pltpu.make_async_copy(tiles.at[p], dz_ref.at[:, pl.ds((first + j) * tc_ch, tc_ch)], sems.at[p])
                  for p, first in enumerate([v0, g0])]

        def wait_writes():
            for cp in writes:
                cp.wait()

        pad_a[pl.ds(0, HALO), :] = jnp.zeros((HALO, tc_ch), F32)
        pad_dc[pl.ds(t, HALO), :] = jnp.zeros((HALO, tc_ch), F32)
        acc_w[...] = jnp.zeros_like(acc_w)
        acc_b[...] = jnp.zeros_like(acc_b)

        def fill(s, tc):
            pad_a[pl.ds(HALO + s, tc), :] = v_ref[pl.ds(s, tc), :].astype(F32) * jax.nn.sigmoid(g_ref[pl.ds(s, tc), :].astype(F32))
            pad_dc[pl.ds(s, tc), :] = dc_ref[pl.ds(s, tc), :]

        def chunk(s, tc):
            dcv = pad_dc[pl.ds(s, tc), :]
            win_a = pad_a[pl.ds(s, tc + HALO), :]
            for k, rows in _shifted(win_a, [HALO - (kw - 1) + k for k in range(kw)], tc):
                acc_w[pl.ds(8 * k, 8), :] += _fold8(dcv * rows)
            acc_b[...] += _fold8(dcv)
            da = _taps(pad_dc[pl.ds(s, tc + HALO), :], w_ref, list(range(kw)), tc, flip=True)
            vv = v_ref[pl.ds(s, tc), :].astype(F32)
            sg = jax.nn.sigmoid(g_ref[pl.ds(s, tc), :].astype(F32))
            dv_ref[pl.ds(s, tc), :] = (da * sg).astype(BF16)
            dg_ref[pl.ds(s, tc), :] = (da * vv * sg * (1.0 - sg)).astype(BF16)

        _chunks(t, fill)
        pl.when(j > 0)(wait_writes)
        _chunks(t, chunk)
        for cp in writes:
            cp.start()
        pl.when(j == n_tiles - 1)(wait_writes)
        for k in range(kw):
            dw_ref[k:k + 1, :] = jnp.sum(acc_w[pl.ds(8 * k, 8), :], axis=0, keepdims=True)
        db_ref[...] = jnp.sum(acc_b[...], axis=0, keepdims=True)

    n_tiles = d_conv // tc_ch
    return pl.pallas_call(
        body, name=name, grid=(n_tiles,),
        in_specs=[pl.BlockSpec((t, tc_ch), lambda j: (0, v0 + j)), pl.BlockSpec((t, tc_ch), lambda j: (0, g0 + j)),
                  pl.BlockSpec((t, tc_ch), lambda j: (0, j)), pl.BlockSpec((kw, tc_ch), lambda j: (0, j))],
        out_specs=[ANY, pl.BlockSpec((kw, tc_ch), lambda j: (0, j)), pl.BlockSpec((1, tc_ch), lambda j: (0, j))],
        out_shape=[jax.ShapeDtypeStruct((t, d_pool + 2 * d_conv), BF16),
                   jax.ShapeDtypeStruct((kw, d_conv), F32), jax.ShapeDtypeStruct((1, d_conv), F32)],
        scratch_shapes=[pltpu.VMEM((t + HALO, tc_ch), F32), pltpu.VMEM((t + HALO, tc_ch), F32),
                        pltpu.VMEM((8 * kw, tc_ch), F32), pltpu.VMEM((8, tc_ch), F32),
                        pltpu.VMEM((2, t, tc_ch), BF16), pltpu.SemaphoreType.DMA((2,))],
        compiler_params=_params(("arbitrary",)),
    )(z, z, dc, conv_w)


def short_fwd(name, z, conv_w, d_short):
    t = z.shape[0]
    kw = conv_w.shape[0]
    tc_ch = _tile(d_short, CHANNEL_TILE)
    nt = d_short // tc_ch

    def body(b_ref, c_ref, u_ref, w_ref, y_ref, pad):
        pad[pl.ds(0, HALO), :] = jnp.zeros((HALO, tc_ch), F32)

        def fill(s, tc):
            pad[pl.ds(HALO + s, tc), :] = c_ref[pl.ds(s, tc), :].astype(F32) * u_ref[pl.ds(s, tc), :].astype(F32)

        def chunk(s, tc):
            win = pad[pl.ds(s, tc + HALO), :]
            cq = _taps(win, w_ref, [HALO - (kw - 1) + k for k in range(kw)], tc)
            y_ref[pl.ds(s, tc), :] = (b_ref[pl.ds(s, tc), :].astype(F32) * cq).astype(BF16)

        _chunks(t, fill)
        _chunks(t, chunk)

    return pl.pallas_call(
        body, name=name, grid=(nt,),
        in_specs=[pl.BlockSpec((t, tc_ch), lambda j: (0, j)), pl.BlockSpec((t, tc_ch), lambda j: (0, nt + j)),
                  pl.BlockSpec((t, tc_ch), lambda j: (0, 2 * nt + j)), pl.BlockSpec((kw, tc_ch), lambda j: (0, j))],
        out_specs=pl.BlockSpec((t, tc_ch), lambda j: (0, j)),
        out_shape=jax.ShapeDtypeStruct((t, d_short), BF16),
        scratch_shapes=[pltpu.VMEM((t + HALO, tc_ch), F32)],
        compiler_params=_params(("parallel",)),
    )(z, z, z, conv_w)


def short_bwd(name, z, dy, conv_w, d_short):
    t = z.shape[0]
    kw = conv_w.shape[0]
    tc_ch = _tile(d_short, CHANNEL_TILE)
    nt = d_short // tc_ch

    def body(b_ref, c_ref, u_ref, dy_ref, w_ref, dz_ref, dw_ref, pad_q, pad_dcq, acc_w, tiles, sems):
        j = pl.program_id(0)
        db_ref, dcg_ref, du_ref = tiles.at[0], tiles.at[1], tiles.at[2]
        writes = [pltpu.make_async_copy(tiles.at[p], dz_ref.at[:, pl.ds((p * nt + j) * tc_ch, tc_ch)], sems.at[p])
                  for p in range(3)]

        def wait_writes():
            for cp in writes:
                cp.wait()

        pad_q[pl.ds(0, HALO), :] = jnp.zeros((HALO, tc_ch), F32)
        pad_dcq[pl.ds(t, HALO), :] = jnp.zeros((HALO, tc_ch), F32)
        acc_w[...] = jnp.zeros_like(acc_w)

        def fill(s, tc):
            rows = pl.ds(s, tc)
            pad_q[pl.ds(HALO + s, tc), :] = c_ref[rows, :].astype(F32) * u_ref[rows, :].astype(F32)
            pad_dcq[rows, :] = dy_ref[rows, :].astype(F32) * b_ref[rows, :].astype(F32)

        def chunk(s, tc):
            rows = pl.ds(s, tc)
            win_q = pad_q[pl.ds(s, tc + HALO), :]
            dcq = pad_dcq[rows, :]
            cq = None
            for k, shifted in _shifted(win_q, [HALO - (kw - 1) + k for k in range(kw)], tc):
                acc_w[pl.ds(8 * k, 8), :] += _fold8(dcq * shifted)
                term = w_ref[k:k + 1, :] * shifted
                cq = term if cq is None else cq + term
            db_ref[rows, :] = (dy_ref[rows, :].astype(F32) * cq).astype(BF16)
            dq = _taps(pad_dcq[pl.ds(s, tc + HALO), :], w_ref, list(range(kw)), tc, flip=True)
            dcg_ref[rows, :] = (dq * u_ref[rows, :].astype(F32)).astype(BF16)
            du_ref[rows, :] = (dq * c_ref[rows, :].astype(F32)).astype(BF16)

        _chunks(t, fill)
        pl.when(j > 0)(wait_writes)
        _chunks(t, chunk)
        for cp in writes:
            cp.start()
        pl.when(j == nt - 1)(wait_writes)
        for k in range(kw):
            dw_ref[k:k + 1, :] = jnp.sum(acc_w[pl.ds(8 * k, 8), :], axis=0, keepdims=True)

    zspec = [pl.BlockSpec((t, tc_ch), lambda j, o=o: (0, o * nt + j)) for o in range(3)]
    return pl.pallas_call(
        body, name=name, grid=(nt,),
        in_specs=[*zspec, pl.BlockSpec((t, tc_ch), lambda j: (0, j)), pl.BlockSpec((kw, tc_ch), lambda j: (0, j))],
        out_specs=[ANY, pl.BlockSpec((kw, tc_ch), lambda j: (0, j))],
        out_shape=[jax.ShapeDtypeStruct((t, 3 * d_short), BF16), jax.ShapeDtypeStruct((kw, d_short), F32)],
        scratch_shapes=[pltpu.VMEM((t + HALO, tc_ch), F32), pltpu.VMEM((t + HALO, tc_ch), F32),
                        pltpu.VMEM((8 * kw, tc_ch), F32), pltpu.VMEM((3, t, tc_ch), BF16),
                        pltpu.SemaphoreType.DMA((3,))],
        compiler_params=_params(("arbitrary",)),
    )(z, z, z, dy, conv_w)


def _adamw_update(w, m, v, g):
    nm = ADAM_B1 * m + (1.0 - ADAM_B1) * g
    nv = ADAM_B2 * v + (1.0 - ADAM_B2) * (g * g)
    m_hat = nm / (1.0 - ADAM_B1 ** ADAM_STEP)
    v_hat = nv / (1.0 - ADAM_B2 ** ADAM_STEP)
    return -ADAM_LR * (m_hat / (jnp.sqrt(v_hat) + ADAM_EPS) + ADAM_WD * w), nm, nv


def adamw_replicated(name, params, first_moments, second_moments, contributions, layout, scalar_at):
    n = len(params)
    n_slots = contributions.shape[0]

    def total(c_ref, row, lane, rows, lanes):
        acc = c_ref[0, row:row + rows, lane:lane + lanes]
        for slot in range(1, n_slots):
            acc = acc + c_ref[slot, row:row + rows, lane:lane + lanes]
        return acc

    def body(*refs):
        ws, ms, vs, c_ref = refs[:n], refs[n:2 * n], refs[2 * n:3 * n], refs[3 * n]
        outs = refs[3 * n + 1:]
        outs[0][...] = total(c_ref, *scalar_at, 1, 128)
        for i, (row, lane) in enumerate(layout):
            g = total(c_ref, row, lane, *params[i].shape)
            grad_ref, delta_ref, nm_ref, nv_ref = outs[1 + 4 * i:5 + 4 * i]
            grad_ref[...] = g
            delta_ref[...], nm_ref[...], nv_ref[...] = _adamw_update(ws[i][...], ms[i][...], vs[i][...], g)

    out_shape = [jax.ShapeDtypeStruct((1, 128), F32)]
    for p in params:
        out_shape += [jax.ShapeDtypeStruct(p.shape, F32)] * 4
    return pl.pallas_call(body, name=name, out_shape=out_shape)(*params, *first_moments, *second_moments, contributions)


def adamw(name, w, m, v, contributions):
    r, c = w.shape
    nc = len(contributions)
    n_slots = contributions[0].shape[0]
    tr = 256 if c <= 1024 else 128
    if any(a.shape[1] % tr for a in contributions):
        assert nc == 1
        tr = r
    tiles = [a.shape[1] // tr for a in contributions]
    first = [sum(tiles[:j]) for j in range(nc)]

    def body(w_ref, m_ref, v_ref, *rest):
        g_refs, (grad_ref, delta_ref, nm_ref, nv_ref) = rest[:nc], rest[nc:]
        i = pl.program_id(0)
        g = None
        for j, g_ref in enumerate(g_refs):
            s = g_ref[0].astype(F32)
            for slot in range(1, n_slots):
                s = s + g_ref[slot].astype(F32)
            g = s if g is None else jnp.where(i >= first[j], s, g)
        grad_ref[...] = g
        delta_ref[...], nm_ref[...], nv_ref[...] = _adamw_update(w_ref[...], m_ref[...], v_ref[...], g)

    blk = pl.BlockSpec((tr, c), lambda i: (i, 0))
    g_specs = [pl.BlockSpec((n_slots, tr, c), lambda i, j=j: (0, jnp.clip(i - first[j], 0, tiles[j] - 1), 0))
               for j in range(nc)]
    return pl.pallas_call(
        body, name=name, grid=(r // tr,),
        in_specs=[blk, blk, blk, *g_specs],
        out_specs=[blk] * 4, out_shape=[jax.ShapeDtypeStruct((r, c), F32)] * 4,
        compiler_params=_params(("parallel",)),
    )(w, m, v, *contributions)


def _pad_rows(a, rows):
    return jnp.pad(a, ((0, rows - a.shape[0]), (0, 0)))


def kernel(x, mix_pre_g, mix_post_g, ffn_pre_g, ffn_post_g, ab_w_in, pool_w, pool_scale, conv_w, conv_b, conv_ln_g, conv_ln_b, ab_w_out, sc_w_in, sc_conv_w, sc_w_out, ffn_w1, ffn_w2, loss_target, m_mix_pre_g, m_mix_post_g, m_ffn_pre_g, m_ffn_post_g, m_ab_w_in, m_pool_w, m_pool_scale, m_conv_w, m_conv_b, m_conv_ln_g, m_conv_ln_b, m_ab_w_out, m_sc_w_in, m_sc_conv_w, m_sc_w_out, m_ffn_w1, m_ffn_w2, v_mix_pre_g, v_mix_post_g, v_ffn_pre_g, v_ffn_post_g, v_ab_w_in, v_pool_w, v_pool_scale, v_conv_w, v_conv_b, v_conv_ln_g, v_conv_ln_b, v_ab_w_out, v_sc_w_in, v_sc_conv_w, v_sc_w_out, v_ffn_w1, v_ffn_w2):
    t, d = x.shape[1], x.shape[2]
    d_pool = pool_scale.shape[1]
    d_conv = conv_b.shape[1]
    d_short = d
    ng, pg = pool_w.shape[1], pool_w.shape[3]
    kw, ks = conv_w.shape[1], sc_conv_w.shape[1]
    nb_ab, nb_sc, nb_ff = ab_w_in.shape[2], sc_w_in.shape[2], ffn_w1.shape[2]

    xs = x[0]
    target = loss_target[0]

    lanes = min(128, d_conv // N_DEV)
    small_rows = [kw * (d_conv // N_DEV) // lanes, ks * (d_short // N_DEV) // lanes, ng * (pg // N_DEV) * pg // lanes]
    small_total = -(-sum(small_rows) // 8) * 8
    r0, r1, r2 = small_rows[0], small_rows[0] + small_rows[1], sum(small_rows)

    def pack_small(a_conv, a_sconv, a_pool):
        parts = [a_conv[0].reshape(-1, lanes), a_sconv[0].reshape(-1, lanes), a_pool[0].reshape(-1, lanes)]
        return _pad_rows(jnp.concatenate(parts, axis=0), small_total)

    shards = {
        "ab_in": (ab_w_in, 0, BF16), "small": (pack_small(conv_w, sc_conv_w, pool_w)[None], 0, F32),
        "ab_out": (ab_w_out, 0, BF16), "ff1_0": (ffn_w1, 0, BF16), "ff2_0": (ffn_w2, 0, BF16),
        "sc_in": (sc_w_in, 0, BF16), "sc_out": (sc_w_out, 0, BF16),
        "ff1_1": (ffn_w1, 1, BF16), "ff2_1": (ffn_w2, 1, BF16)}
    direct = ["ab_in", "small", "ab_out"]
    zones = {nm: place_shard("place_" + nm, *shards[nm]) for nm in direct}
    started, token = copies_start("gather_start", [[zones[nm]] for nm in direct], _first_hop, 4)
    started = dict(zip(direct, started))
    ring = {}
    for nm in ["ff1_0", "ff2_0"]:
        zones[nm] = place_shard("place_" + nm, *shards[nm], deps=[token])
        (ring[nm],), token = copies_start("ring_start_" + nm, [[zones[nm]]], _ring_hop1, 3, deps=[token])
    for nm in shards:
        if nm not in zones:
            zones[nm] = place_shard("place_" + nm, *shards[nm], deps=[token])

    ties = [0]

    def after(v, *deps):
        ties[0] += 1
        return tie(f"tie_{ties[0]}", v, *deps)

    def fetch_begin(nm, dep):
        (zone,) = copies_wait("gather_wait_" + nm, started[nm], _first_hop, dep)
        (hop,), tok = copies_start("forward_start_" + nm, [[zone]], _second_hop, 3)
        return hop, tok

    def fetch_end(nm, hop, dep):
        return copies_wait("forward_wait_" + nm, hop, _second_hop, dep)[0]

    def ring_step(tag, dep, second=None, first=None, third=None):
        names, groups, hops, counts = [], [], [], []
        if second is not None:
            groups.append(copies_wait("ring1_wait_" + second, ring[second], _ring_hop1, dep))
            names, hops, counts = names + [second], hops + [_ring_hop2], counts + [4]
        if first is not None:
            groups.append([zones[first]])
            names, hops, counts = names + [first], hops + [_ring_hop1], counts + [3]
        if third is not None:
            groups.append(copies_wait("ring2_wait_" + third, ring[third], _ring_hop2, dep))
            names, hops, counts = names + [third], hops + [_ring_hop3], counts + [1]
        begun, tok = copies_start("ring_start_" + tag, groups, hops, counts, deps=[dep])
        ring.update(zip(names, begun))
        return tok

    def ring_done(nm, dep):
        return copies_wait("ring3_wait_" + nm, ring[nm], _ring_hop3, dep)[0]

    relu = lambda r: jnp.maximum(r, 0.0)
    square = lambda a: a * a
    relu2_bwd = lambda r, a: r * (2.0 * a.astype(F32))

    def row(vec, l):
        return vec[l:l + 1]

    hop_small, _ = fetch_begin("small", token)
    hop_ab_in, tok = fetch_begin("ab_in", token)
    w_small = fetch_end("small", hop_small, tok)
    w_ab_in = fetch_end("ab_in", hop_ab_in, tok)
    w_conv = w_small[:, :r0].reshape(N_DEV, kw, -1).transpose(1, 0, 2).reshape(kw, d_conv)
    w_sconv = w_small[:, r0:r1].reshape(N_DEV, ks, -1).transpose(1, 0, 2).reshape(ks, d_short)
    w_pool = w_small[:, r1:r2].reshape(N_DEV, ng, -1, pg).transpose(1, 0, 2, 3).reshape(ng, pg, pg).astype(BF16)
    h0 = norm_pre("norm_pre", xs, after(row(mix_pre_g, 0), token))
    z0 = mm_nn_blocked("ab_in", h0, w_ab_in, out_dtype=BF16)
    hop, tok = fetch_begin("ab_out", z0)
    z0 = after(z0, tok)
    pooled, y0 = pool_fwd("pool_fwd", z0, w_pool, pool_scale, d_pool, d_pool + d_conv)
    cv = conv_fwd("conv_fwd", z0, w_conv, conv_b, d_pool, d_conv)
    y0 = ln_silu("ln_silu", cv, conv_ln_g, conv_ln_b, y0, d_pool // d_conv)
    w_ab_out = fetch_end("ab_out", hop, y0)
    tok = ring_step("a", w_ab_out, second="ff1_0", first="sc_in")
    y0 = after(y0, tok)
    m0 = mm_nn("ab_out", y0, w_ab_out.reshape(d_pool + d_conv, d), out_dtype=F32)
    x1, h1 = post_pre("post_pre_0", xs, m0, row(mix_post_g, 0), row(ffn_pre_g, 0))
    tok = ring_step("b", h1, second="ff2_0", first="sc_out", third="ff1_0")
    w_ff1_0 = ring_done("ff1_0", tok)
    a0 = mm_nn_blocked("ffn0_up", h1, w_ff1_0, out_dtype=BF16, epilogue=relu)
    tok = ring_step("c", a0, second="sc_in", first="ff1_1", third="ff2_0")
    w_ff2_0 = ring_done("ff2_0", tok).reshape(-1, d)
    f0 = mm_nn("ffn0_down", a0, w_ff2_0, out_dtype=F32, tk=2048, lhs_fn=square)
    tok = ring_step("d", f0, second="sc_out", first="ff2_1", third="sc_in")
    f0 = after(f0, tok)
    x2, h2 = post_pre("post_pre_1", x1, f0, row(ffn_post_g, 0), row(mix_pre_g, 1))
    w_sc_in = ring_done("sc_in", h2)
    z1 = mm_nn_blocked("sc_in", h2, w_sc_in, out_dtype=BF16)
    tok = ring_step("e", z1, second="ff1_1", third="sc_out")
    z1 = after(z1, tok)
    y1 = short_fwd("short_fwd", z1, w_sconv, d_short)
    w_sc_out = ring_done("sc_out", y1).reshape(d_short, d)
    m1 = mm_nn("sc_out", y1, w_sc_out, out_dtype=F32)
    tok = ring_step("f", m1, second="ff2_1")
    m1 = after(m1, tok)
    x3, h3 = post_pre("post_pre_2", x2, m1, row(mix_post_g, 1), row(ffn_pre_g, 1))
    tok = ring_step("g", h3, third="ff1_1")
    w_ff1_1 = ring_done("ff1_1", tok)
    a1 = mm_nn_blocked("ffn1_up", h3, w_ff1_1, out_dtype=BF16, epilogue=relu)
    tok = ring_step("h", a1, third="ff2_1")
    w_ff2_1 = ring_done("ff2_1", tok).reshape(-1, d)
    f1 = mm_nn("ffn1_down", a1, w_ff2_1, out_dtype=F32, tk=2048, lhs_fn=square)
    dx4, df1, loss_part, dg_ffn_post1 = post_loss("post_loss", x3, f1, row(ffn_post_g, 1), target)

    red = {}

    def reduce_step(dep, begin=None, middle=None):
        tags, groups, hops, counts = [], [], [], []
        if begin is not None:
            tag, g = begin
            tags, groups = tags + [tag], groups + [[g, lax.empty((N_CHIP,) + g.shape[1:], g.dtype)]]
            hops, counts = hops + [_pair_hop], counts + [N_CHIP]
        if middle is not None:
            g, from_sibling = copies_wait("pair_wait_" + middle, red[middle], _pair_hop, dep)
            tags, groups = tags + [middle], groups + [list(pair_add("pair_add_" + middle, g, from_sibling))]
            hops, counts = hops + [_chip_hop], counts + [3]
        begun, tok = copies_start("reduce_start_" + "_".join(tags), groups, hops, counts, deps=[dep])
        red.update(zip(tags, begun))
        return tok

    def reduce_end(tag, dep):
        return copies_wait("chips_wait_" + tag, red[tag], _chip_hop, dep)[1]

    dpre, dw = mm_bwd_pair("ffn1_da_dw2", df1, w_ff2_1, a1, out_dtype=BF16, act_fn=square, epilogue=relu2_bwd)
    dpre = after(dpre, reduce_step(dpre, begin=("ff2_1", dw.reshape(N_DEV, -1, d))))
    dh3, dw = mm_bwd_pair_blocked("ffn1_dh_dw1", dpre, w_ff1_1, h3, out_dtype=BF16)
    dh3 = after(dh3, reduce_step(dh3, begin=("ff1_1", dw), middle="ff2_1"))
    dx3, dm1, dg_ffn_pre1, dg_mix_post1 = bwd_pre_post("bwd_3", dx4, x3, row(ffn_pre_g, 1), dh3, m1, row(mix_post_g, 1))

    dy1, dw = mm_bwd_pair("sc_dy_dwout", dm1, w_sc_out, y1, out_dtype=BF16)
    dy1 = after(dy1, reduce_step(dy1, begin=("sc_out", dw.reshape(N_DEV, -1, d)), middle="ff1_1"))
    dz1, dw_sconv = short_bwd("short_bwd", z1, dy1, w_sconv, d_short)
    dh2, dw = mm_bwd_pair_blocked("sc_dh_dwin", dz1, w_sc_in, h2, out_dtype=BF16)
    dh2 = after(dh2, reduce_step(dh2, begin=("sc_in", dw), middle="sc_out"))
    dx2, df0, dg_mix_pre1, dg_ffn_post0 = bwd_pre_post("bwd_2", dx3, x2, row(mix_pre_g, 1), dh2, f0, row(ffn_post_g, 0))

    dpre, dw = mm_bwd_pair("ffn0_da_dw2", df0, w_ff2_0, a0, out_dtype=BF16, act_fn=square, epilogue=relu2_bwd)
    dpre = after(dpre, reduce_step(dpre, begin=("ff2_0", dw.reshape(N_DEV, -1, d)), middle="sc_in"))
    dh1, dw = mm_bwd_pair_blocked("ffn0_dh_dw1", dpre, w_ff1_0, h1, out_dtype=BF16)
    dh1 = after(dh1, reduce_step(dh1, begin=("ff1_0", dw), middle="ff2_0"))
    dx1, dm0, dg_ffn_pre0, dg_mix_post0 = bwd_pre_post("bwd_1", dx2, x1, row(ffn_pre_g, 0), dh1, m0, row(mix_post_g, 0))

    dy0, dw = mm_bwd_pair("ab_dy_dwout", dm0, w_ab_out.reshape(d_pool + d_conv, d), y0, out_dtype=BF16)
    dy0 = after(dy0, reduce_step(dy0, begin=("ab_out", dw.reshape(N_DEV, -1, d)), middle="ff1_0"))
    dcv, dg_ln_g, dg_ln_b = ln_silu_bwd("ln_silu_bwd", cv, conv_ln_g, conv_ln_b, dy0, d_pool // d_conv)
    dz0, dw_conv, dg_conv_b = conv_bwd("conv_bwd", z0, dcv, w_conv, d_pool, d_conv)
    dz0, dw_pool, dg_pool_scale = pool_bwd("pool_bwd", pooled, dy0, w_pool, pool_scale, dz0)
    small_parts = [
        dw_conv.reshape(kw, N_DEV, -1).transpose(1, 0, 2).reshape(N_DEV, -1, lanes),
        dw_sconv.reshape(ks, N_DEV, -1).transpose(1, 0, 2).reshape(N_DEV, -1, lanes),
        dw_pool.reshape(ng, N_DEV, pg // N_DEV, pg).transpose(1, 0, 2, 3).reshape(N_DEV, -1, lanes),
    ]
    small = jnp.pad(jnp.concatenate(small_parts, axis=1), ((0, 0), (0, small_total - r2), (0, 0)))
    dz0 = after(dz0, reduce_step(dz0, begin=("small", small), middle="ab_out"))
    dh0, dw = mm_bwd_pair_blocked("ab_dh_dwin", dz0, w_ab_in, h0, out_dtype=BF16)
    dh0 = after(dh0, reduce_step(dh0, begin=("ab_in", dw), middle="small"))
    grad_x, dg_mix_pre0 = bwd_pre_final("bwd_0", dx1, xs, row(mix_pre_g, 0), dh0)
    tok = reduce_step(grad_x, middle="ab_in")

    fold = lambda a: jnp.sum(a, axis=0, keepdims=True)
    rep_rows = [fold(dg_mix_pre0), fold(dg_mix_pre1), fold(dg_mix_post0), fold(dg_mix_post1),
                fold(dg_ffn_pre0), fold(dg_ffn_pre1), fold(dg_ffn_post0), fold(dg_ffn_post1)]
    tail = jnp.concatenate([dg_pool_scale, dg_conv_b, fold(dg_ln_g), fold(dg_ln_b)], axis=1).reshape(-1, d)
    loss_row = jnp.pad(jnp.sum(loss_part).reshape(1, 1), ((0, 0), (0, d - 1)))
    rep = _pad_rows(jnp.concatenate(rep_rows + [tail, loss_row], axis=0), 16)
    (rep_hop,), _ = copies_start("rep_start", [[place_shard("place_rep", rep[None], 0, F32)]], _first_hop, 4)
    rep_layout = [(0, 0), (2, 0), (4, 0), (6, 0)]
    for offset in (0, d_pool, d_pool + d_conv, d_pool + 2 * d_conv):
        rep_layout.append((len(rep_rows) + offset // d, offset % d))
    loss_at = (len(rep_rows) + tail.shape[0], 0)

    def upd(name, w, m, v, contribs):
        shape = w.shape
        flat2 = lambda a: a.reshape(-1, shape[-1])
        outs = adamw(name, flat2(w), flat2(m), flat2(v), contribs)
        return [o.reshape(shape) for o in outs]

    g_ff2 = [reduce_end("ff2_0", tok), reduce_end("ff2_1", tok)]
    o_ff2 = upd("adam_ffn_w2", ffn_w2, m_ffn_w2, v_ffn_w2, g_ff2)
    (rep_zone,) = copies_wait("rep_wait", rep_hop, _first_hop, o_ff2[0])
    (rep_hop,), _ = copies_start("rep_forward_start", [[rep_zone]], _second_hop, 3)
    g_ff1 = [reduce_end("ff1_0", o_ff2[0]), reduce_end("ff1_1", o_ff2[0])]
    o_ff1 = upd("adam_ffn_w1", ffn_w1, m_ffn_w1, v_ffn_w1, g_ff1)
    (rep_all,) = copies_wait("rep_forward_wait", rep_hop, _second_hop, o_ff1[0])
    loss_sum, *o_rep = adamw_replicated(
        "adam_replicated",
        [mix_pre_g, mix_post_g, ffn_pre_g, ffn_post_g, pool_scale, conv_b, conv_ln_g, conv_ln_b],
        [m_mix_pre_g, m_mix_post_g, m_ffn_pre_g, m_ffn_post_g, m_pool_scale, m_conv_b, m_conv_ln_g, m_conv_ln_b],
        [v_mix_pre_g, v_mix_post_g, v_ffn_pre_g, v_ffn_post_g, v_pool_scale, v_conv_b, v_conv_ln_g, v_conv_ln_b],
        rep_all, rep_layout, loss_at)
    loss = loss_sum[0, 0] * (0.5 / d)
    o_sc_out = upd("adam_sc_out", sc_w_out, m_sc_w_out, v_sc_w_out, [reduce_end("sc_out", o_ff1[0])])
    o_sc_in = upd("adam_sc_in", sc_w_in, m_sc_w_in, v_sc_w_in, [reduce_end("sc_in", o_sc_out[0])])
    o_ab_out = upd("adam_ab_out", ab_w_out, m_ab_w_out, v_ab_w_out, [reduce_end("ab_out", o_sc_in[0])])
    o_small = adamw("adam_small", pack_small(conv_w, sc_conv_w, pool_w), pack_small(m_conv_w, m_sc_conv_w, m_pool_w),
                    pack_small(v_conv_w, v_sc_conv_w, v_pool_w), [reduce_end("small", o_ab_out[0])])
    o_ab_in = upd("adam_ab_in", ab_w_in, m_ab_w_in, v_ab_w_in, [reduce_end("ab_in", o_small[0])])

    def unpack_small(o):
        return o[:r0].reshape(conv_w.shape), o[r0:r1].reshape(sc_conv_w.shape), o[r1:r2].reshape(pool_w.shape)

    results = []
    for kind in range(4):
        g_mix_pre, g_mix_post, g_ffn_pre, g_ffn_post, g_scale, g_conv_b, g_ln_g, g_ln_b = o_rep[kind::4]
        s_conv, s_sconv, s_pool = unpack_small(o_small[kind])
        results.append([
            g_mix_pre, g_mix_post, g_ffn_pre, g_ffn_post,
            o_ab_in[kind], s_pool, g_scale, s_conv, g_conv_b, g_ln_g, g_ln_b,
            o_ab_out[kind], o_sc_in[kind], s_sconv, o_sc_out[kind], o_ff1[kind], o_ff2[kind]])

    return (loss, grad_x[None], *results[0], *results[1], *results[2], *results[3])
```

```python
import jax
import jax.numpy as jnp
from jax import lax
from jax.experimental import pallas as pl
from jax.experimental.pallas import tpu as pltpu

F32 = jnp.float32
BF16 = jnp.bfloat16
MESH = pl.DeviceIdType.MESH
ANY = pl.BlockSpec(memory_space=pl.ANY)

NORM_EPS = 1e-6
POOL_WINDOWS = (2, 4, 8, 16)
ADAM_LR = 0.001
ADAM_B1 = 0.9
ADAM_B2 = 0.999
ADAM_EPS = 1e-08
ADAM_WD = 0.01
ADAM_STEP = 10

N_DEV = 8
VMEM_LIMIT = 56 * 1024 * 1024
PAIR_ADD_BLOCK = 1 << 20
MATMUL_ROWS = 2048
ROW_TILE = 256
CHANNEL_TILE = 256
TIME_CHUNK = 64
HALO = 32

NN = (((1,), (0,)), ((), ()))
NT = (((1,), (1,)), ((), ()))
TN = (((0,), (0,)), ((), ()))


def _params(sem):
    return pltpu.CompilerParams(dimension_semantics=sem, vmem_limit_bytes=VMEM_LIMIT)


def _place():
    x, y, c = lax.axis_index("x"), lax.axis_index("y"), lax.axis_index("c")
    return x, y, c


def _slot(px, py, pc):
    return 4 * px + 2 * py + pc


HBM = pl.BlockSpec(memory_space=pltpu.HBM)
SEM = pl.BlockSpec(memory_space=pltpu.SEMAPHORE)
EFFECT = pltpu.SideEffectType.DATAFLOW_SIDE_EFFECTING
TOKEN = jax.ShapeDtypeStruct((8, 128), F32)


def _in_hbm(a):
    return pltpu.with_memory_space_constraint(a, pltpu.HBM)


CHIPS = [(0, 0), (0, 1), (1, 0), (1, 1)]
N_CHIP = len(CHIPS)


def _chip(px, py):
    return 2 * px + py


def _first_hop(bufs, sends, recvs, waiting):
    (land,) = bufs
    x, y, c = _place()
    me = _slot(x, y, c)
    peers = [(x, y, 1 - c), (1 - x, y, c), (x, 1 - y, c), (1 - x, 1 - y, c)]
    return [pltpu.make_async_remote_copy(
        src_ref=land.at[me], dst_ref=land.at[_slot(*p) if waiting else me],
        send_sem=sends.at[k], recv_sem=recvs.at[k], device_id=p, device_id_type=MESH) for k, p in enumerate(peers)]


def _second_hop(bufs, sends, recvs, waiting):
    (land,) = bufs
    x, y, c = _place()
    return [pltpu.make_async_remote_copy(
        src_ref=land.at[_slot(px, py, c)], dst_ref=land.at[_slot(px, py, 1 - c if waiting else c)],
        send_sem=sends.at[k], recv_sem=recvs.at[k], device_id=(x, y, 1 - c), device_id_type=MESH)
        for k, (px, py) in enumerate([(1 - x, y), (x, 1 - y), (1 - x, 1 - y)])]


def _ring_hop1(bufs, sends, recvs, waiting):
    (land,) = bufs
    x, y, c = _place()
    me = _slot(x, y, c)
    peers = [(1 - x, y, c), (x, 1 - y, c), (x, y, 1 - c)]
    return [pltpu.make_async_remote_copy(
        src_ref=land.at[me], dst_ref=land.at[_slot(*p) if waiting else me],
        send_sem=sends.at[k], recv_sem=recvs.at[k], device_id=p, device_id_type=MESH) for k, p in enumerate(peers)]


def _ring_hop2(bufs, sends, recvs, waiting):
    (land,) = bufs
    x, y, c = _place()
    half = land.shape[1] // 2
    first, second = pl.ds(0, half), pl.ds(half, half)
    nx, ny, diag = _slot(1 - x, y, c), _slot(x, 1 - y, c), _slot(1 - x, 1 - y, c)
    plan = [
        (land.at[ny, first], land.at[diag, first], (1 - x, y, c)),
        (land.at[nx, second], land.at[diag, second], (x, 1 - y, c)),
        (land.at[nx], land.at[_slot(1 - x, y, 1 - c)], (x, y, 1 - c)),
        (land.at[ny], land.at[_slot(x, 1 - y, 1 - c)], (x, y, 1 - c))]
    return [pltpu.make_async_remote_copy(
        src_ref=src, dst_ref=mine if waiting else src, send_sem=sends.at[k], recv_sem=recvs.at[k],
        device_id=to, device_id_type=MESH) for k, (src, mine, to) in enumerate(plan)]


def _ring_hop3(bufs, sends, recvs, waiting):
    (land,) = bufs
    x, y, c = _place()
    return [pltpu.make_async_remote_copy(
        src_ref=land.at[_slot(1 - x, 1 - y, c)], dst_ref=land.at[_slot(1 - x, 1 - y, 1 - c if waiting else c)],
        send_sem=sends.at[0], recv_sem=recvs.at[0], device_id=(x, y, 1 - c), device_id_type=MESH)]


def _pair_hop(bufs, sends, recvs, waiting):
    g, land = bufs
    x, y, c = _place()
    return [pltpu.make_async_remote_copy(
        src_ref=g.at[_slot(qx, qy, 1 - c)], dst_ref=land.at[q],
        send_sem=sends.at[q], recv_sem=recvs.at[q], device_id=(x, y, 1 - c), device_id_type=MESH)
        for q, (qx, qy) in enumerate(CHIPS)]


def _chip_hop(bufs, sends, recvs, waiting):
    p, land = bufs
    x, y, c = _place()
    return [pltpu.make_async_remote_copy(
        src_ref=p.at[_chip(px, py)], dst_ref=land.at[_chip(px, py) if waiting else _chip(x, y)],
        send_sem=sends.at[k], recv_sem=recvs.at[k], device_id=(px, py, c), device_id_type=MESH)
        for k, (px, py) in enumerate([(1 - x, y), (x, 1 - y), (1 - x, 1 - y)])]


def copies_start(name, groups, hop, n_copies, deps=()):
    flat = [b for grp in groups for b in grp]
    nb, ng = len(flat), len(groups)
    deps = list(deps)
    hops = list(hop) if isinstance(hop, (list, tuple)) else [hop] * ng
    counts = list(n_copies) if isinstance(n_copies, (list, tuple)) else [n_copies] * ng

    def body(*refs):
        ins, token = refs[:nb], refs[-1]
        sems = refs[nb + len(deps):nb + len(deps) + 2 * ng]
        i = 0
        for gi, grp in enumerate(groups):
            for cp in hops[gi](ins[i:i + len(grp)], sems[2 * gi], sems[2 * gi + 1], False):
                cp.start()
            i += len(grp)
        token[...] = jnp.zeros_like(token)

    outs = pl.pallas_call(
        body, name=name,
        out_shape=([pltpu.SemaphoreType.DMA((n,)) for n in counts for _ in range(2)]
                   + [pltpu.HBM(b.shape, b.dtype) for b in flat] + [TOKEN]),
        in_specs=[HBM] * nb + [ANY] * len(deps),
        out_specs=[SEM] * (2 * ng) + [HBM] * nb + [pl.BlockSpec(memory_space=pltpu.VMEM)],
        input_output_aliases={i: 2 * ng + i for i in range(nb)},
        compiler_params=pltpu.CompilerParams(has_side_effects=EFFECT),
    )(*[_in_hbm(b) for b in flat], *deps)
    started, i = [], 0
    for gi, grp in enumerate(groups):
        started.append((outs[2 * gi], outs[2 * gi + 1], list(outs[2 * ng + i:2 * ng + i + len(grp)])))
        i += len(grp)
    return started, outs[-1]


def copies_wait(name, started, hop, after):
    sends, recvs, bufs = started
    nb = len(bufs)

    def body(*refs):
        for cp in hop(refs[:nb], refs[nb], refs[nb + 1], True):
            cp.wait_send()
            cp.wait_recv()

    outs = pl.pallas_call(
        body, name=name,
        out_shape=[pltpu.HBM(b.shape, b.dtype) for b in bufs],
        in_specs=[HBM] * nb + [SEM, SEM, ANY], out_specs=[HBM] * nb,
        input_output_aliases={i: i for i in range(nb)},
        compiler_params=pltpu.CompilerParams(has_side_effects=EFFECT),
    )(*bufs, sends, recvs, after)
    return list(outs)


def place_shard(name, w, layer, dtype, deps=()):
    _, r, c = w.shape
    tr = _tile(r, 1024)
    x, y, core = _place()
    me = _slot(x, y, core).astype(jnp.int32).reshape(1)

    def body(me_ref, w_ref, *rest):
        rest[-1][...] = w_ref[...].astype(dtype)

    return pl.pallas_call(
        body, name=name,
        grid_spec=pltpu.PrefetchScalarGridSpec(
            num_scalar_prefetch=1, grid=(r // tr,),
            in_specs=[pl.BlockSpec((None, tr, c), lambda i, me_ref: (layer, i, 0))] + [ANY] * len(deps),
            out_specs=pl.BlockSpec((None, tr, c), lambda i, me_ref: (me_ref[0], i, 0))),
        out_shape=jax.ShapeDtypeStruct((N_DEV, r, c), dtype),
        compiler_params=_params(("parallel",)),
    )(me, w, *deps)


def tie(name, x, *deps):
    def body(*refs):
        del refs

    return pl.pallas_call(
        body, name=name, out_shape=jax.ShapeDtypeStruct(x.shape, x.dtype),
        in_specs=[ANY] * (1 + len(deps)), out_specs=ANY, input_output_aliases={0: 0},
    )(x, *deps)


def pair_add(name, g, from_sibling):
    _, r, c_dim = g.shape
    tr = r
    while tr * c_dim > PAIR_ADD_BLOCK and tr % 16 == 0:
        tr //= 2
    x, y, core = _place()
    where = jnp.stack([core, _chip(x, y)]).astype(jnp.int32)

    def body(where_ref, g_ref, s_ref, o_ref, zone_ref):
        total = (g_ref[...].astype(F32) + s_ref[...].astype(F32)).astype(o_ref.dtype)
        o_ref[...] = total

        @pl.when(pl.program_id(1) == where_ref[1])
        def _():
            zone_ref[...] = total

    blk = pl.BlockSpec((None, tr, c_dim), lambda i, q, where_ref: (q, i, 0))
    return pl.pallas_call(
        body, name=name,
        grid_spec=pltpu.PrefetchScalarGridSpec(
            num_scalar_prefetch=1, grid=(r // tr, N_CHIP),
            in_specs=[pl.BlockSpec((None, None, tr, c_dim), lambda i, q, where_ref: (q, where_ref[0], i, 0)), blk],
            out_specs=[blk, pl.BlockSpec((None, tr, c_dim), lambda i, q, where_ref: (where_ref[1], i, 0))]),
        out_shape=[jax.ShapeDtypeStruct((N_CHIP, r, c_dim), g.dtype)] * 2,
        compiler_params=_params(("parallel", "arbitrary")),
    )(where, g.reshape(N_CHIP, 2, r, c_dim), from_sibling)


def _matmul(name, lhs, rhs, *, out_shape, out_dtype, grid, lhs_spec, rhs_spec, out_spec, acc_shape,
            lhs_fn=None, epilogue=None):
    nk = grid[2]

    def body(lhs_ref, rhs_ref, out_ref, *scratch):
        def product():
            a = lhs_ref[...]
            if lhs_fn is not None:
                a = lhs_fn(a)
            return lax.dot_general(a, rhs_ref[...], NN, preferred_element_type=F32)

        def finish(r):
            if epilogue is not None:
                r = epilogue(r)
            out_ref[...] = r.astype(out_dtype)

        if nk == 1:
            finish(product())
        else:
            (acc_ref,) = scratch
            k = pl.program_id(2)

            @pl.when(k == 0)
            def _():
                acc_ref[...] = product()

            @pl.when(jnp.logical_and(k > 0, k < nk - 1))
            def _():
                acc_ref[...] += product()

            @pl.when(k == nk - 1)
            def _():
                finish(acc_ref[...] + product())

    return pl.pallas_call(
        body, name=name, grid=grid,
        out_shape=jax.ShapeDtypeStruct(out_shape, out_dtype),
        in_specs=[lhs_spec, rhs_spec], out_specs=out_spec,
        scratch_shapes=[pltpu.VMEM(acc_shape, F32)] if nk > 1 else [],
        compiler_params=_params(("parallel", "parallel", "arbitrary")),
    )(lhs, rhs)


def _tile(n, want):
    return want if n % want == 0 else n


def mm_nn(name, x, w, *, out_dtype, tn=512, tk=None, lhs_fn=None, epilogue=None):
    t, kdim = x.shape
    n = w.shape[1]
    tm, tn = _tile(t, MATMUL_ROWS), _tile(n, tn)
    tk = kdim if tk is None else _tile(kdim, tk)
    return _matmul(
        name, x, w, out_shape=(t, n), out_dtype=out_dtype, grid=(t // tm, n // tn, kdim // tk),
        lhs_spec=pl.BlockSpec((tm, tk), lambda i, j, k: (i, k)),
        rhs_spec=pl.BlockSpec((tk, tn), lambda i, j, k: (k, j)),
        out_spec=pl.BlockSpec((tm, tn), lambda i, j, k: (i, j)),
        acc_shape=(tm, tn), lhs_fn=lhs_fn, epilogue=epilogue)


def mm_nn_blocked(name, x, w, *, out_dtype, epilogue=None):
    t, kdim = x.shape
    nb = w.shape[2]
    tm = _tile(t, MATMUL_ROWS)
    tn = nb // 2 if nb >= 1024 else nb
    sub = nb // tn
    return _matmul(
        name, x, w, out_shape=(t, N_DEV * nb), out_dtype=out_dtype, grid=(t // tm, N_DEV * sub, 1),
        lhs_spec=pl.BlockSpec((tm, kdim), lambda i, j, k: (i, k)),
        rhs_spec=pl.BlockSpec((None, kdim, tn), lambda i, j, k: (j // sub, k, j % sub)),
        out_spec=pl.BlockSpec((tm, tn), lambda i, j, k: (i, j)),
        acc_shape=(tm, tn), epilogue=epilogue)


def mm_bwd_pair(name, dy, w, act, *, out_dtype, tile=512, act_fn=None, epilogue=None):
    t, n = dy.shape
    kdim = w.shape[0]
    tile = _tile(kdim, tile)

    def body(dy_ref, w_ref, act_ref, dx_ref, dw_ref):
        a = act_ref[...]
        dx = lax.dot_general(dy_ref[...], w_ref[...], NT, preferred_element_type=F32)
        if epilogue is not None:
            dx = epilogue(dx, a)
        dx_ref[...] = dx.astype(out_dtype)
        if act_fn is not None:
            a = act_fn(a)
        dw_ref[...] = lax.dot_general(a, dy_ref[...], TN, preferred_element_type=F32).astype(out_dtype)

    return pl.pallas_call(
        body, name=name, grid=(kdim // tile,),
        in_specs=[pl.BlockSpec((t, n), lambda j: (0, 0)), pl.BlockSpec((tile, n), lambda j: (j, 0)),
                  pl.BlockSpec((t, tile), lambda j: (0, j))],
        out_specs=[pl.BlockSpec((t, tile), lambda j: (0, j)), pl.BlockSpec((tile, n), lambda j: (j, 0))],
        out_shape=[jax.ShapeDtypeStruct((t, kdim), out_dtype), jax.ShapeDtypeStruct((kdim, n), out_dtype)],
        compiler_params=_params(("parallel",)),
    )(dy, w, act)


def mm_bwd_pair_blocked(name, dz, w, act, *, out_dtype, tile=1024):
    t = dz.shape[0]
    kdim, nb = w.shape[1], w.shape[2]
    tile = _tile(kdim, tile)

    def body(dz_ref, w_ref, act_ref, dx_ref, dw_ref, acc_ref):
        j = pl.program_id(1)
        dw_ref[...] = lax.dot_general(act_ref[...], dz_ref[...], TN, preferred_element_type=F32).astype(out_dtype)

        def product():
            return lax.dot_general(dz_ref[...], w_ref[...], NT, preferred_element_type=F32)

        @pl.when(j == 0)
        def _():
            acc_ref[...] = product()

        @pl.when(jnp.logical_and(j > 0, j < N_DEV - 1))
        def _():
            acc_ref[...] += product()

        @pl.when(j == N_DEV - 1)
        def _():
            dx_ref[...] = (acc_ref[...] + product()).astype(out_dtype)

    return pl.pallas_call(
        body, name=name, grid=(kdim // tile, N_DEV),
        in_specs=[pl.BlockSpec((t, nb), lambda i, j: (0, j)), pl.BlockSpec((None, tile, nb), lambda i, j: (j, i, 0)),
                  pl.BlockSpec((t, tile), lambda i, j: (0, i))],
        out_specs=[pl.BlockSpec((t, tile), lambda i, j: (0, i)),
                   pl.BlockSpec((None, tile, nb), lambda i, j: (j, i, 0))],
        out_shape=[jax.ShapeDtypeStruct((t, kdim), out_dtype), jax.ShapeDtypeStruct((N_DEV, kdim, nb), out_dtype)],
        scratch_shapes=[pltpu.VMEM((t, tile), F32)],
        compiler_params=_params(("parallel", "arbitrary")),
    )(dz, w, act)


def _rstd(v):
    return lax.rsqrt(jnp.mean(v * v, axis=-1, keepdims=True) + NORM_EPS)


def _rms_bwd(v, g, dy):
    r = _rstd(v)
    vhat = v * r
    dvh = dy * g
    dv = r * (dvh - vhat * jnp.mean(dvh * vhat, axis=-1, keepdims=True))
    return dv, dy * vhat


def _fold8(v):
    rows, n = v.shape
    return jnp.sum(v.reshape(rows // 8, 8, n), axis=0)


def _fold_lanes(v):
    out = v[:, 0:128]
    for i in range(1, v.shape[1] // 128):
        out = out + v[:, 128 * i:128 * (i + 1)]
    return out


def _accumulate(ref, v):
    i = pl.program_id(0)

    @pl.when(i == 0)
    def _():
        ref[...] = v

    @pl.when(i > 0)
    def _():
        ref[...] += v


def _row_call(body, name, t, ins, row_in, outs, acc_outs=(), tr=ROW_TILE):
    tr = _tile(t, tr)

    def in_spec(a, tiled):
        if isinstance(tiled, tuple):
            width, j = tiled
            return pl.BlockSpec((tr, width), lambda i: (i, j))
        return pl.BlockSpec((tr, a.shape[1]), lambda i: (i, 0)) if tiled else pl.BlockSpec(a.shape, lambda i: (0, 0))

    in_specs = [in_spec(a, tiled) for a, tiled in zip(ins, row_in)]
    out_specs = [pl.BlockSpec((tr, n), lambda i: (i, 0)) for n, _ in outs]
    out_specs += [pl.BlockSpec((8, n), lambda i: (0, 0)) for n in acc_outs]
    out_shape = [jax.ShapeDtypeStruct((t, n), dt) for n, dt in outs]
    out_shape += [jax.ShapeDtypeStruct((8, n), F32) for n in acc_outs]
    return pl.pallas_call(
        body, name=name, grid=(t // tr,), in_specs=in_specs, out_specs=out_specs, out_shape=out_shape,
        compiler_params=_params(("arbitrary",) if acc_outs else ("parallel",)),
    )(*ins)


def norm_pre(name, x, g):
    t, d = x.shape

    def body(x_ref, g_ref, h_ref):
        v = x_ref[...]
        h_ref[...] = (v * _rstd(v) * g_ref[...]).astype(BF16)

    return _row_call(body, name, t, [x, g], [True, False], [(d, BF16)])[0]


def post_pre(name, x, m, g_post, g_pre):
    t, d = x.shape

    def body(x_ref, m_ref, gp_ref, gn_ref, xo_ref, h_ref):
        mv = m_ref[...]
        xn = x_ref[...] + mv * _rstd(mv) * gp_ref[...]
        xo_ref[...] = xn
        h_ref[...] = (xn * _rstd(xn) * gn_ref[...]).astype(BF16)

    return _row_call(body, name, t, [x, m, g_post, g_pre], [True, True, False, False], [(d, F32), (d, BF16)])


def post_loss(name, x, f, g_post, target):
    t, d = x.shape

    def body(x_ref, f_ref, g_ref, t_ref, dx_ref, df_ref, loss_ref, dg_ref):
        fv = f_ref[...]
        g = g_ref[...]
        out = x_ref[...] + fv * _rstd(fv) * g
        err = out - t_ref[...]
        dx = err * (1.0 / d)
        dx_ref[...] = dx
        dfv, dg_rows = _rms_bwd(fv, g, dx)
        df_ref[...] = dfv.astype(BF16)
        _accumulate(loss_ref, _fold8(_fold_lanes(err * err)))
        _accumulate(dg_ref, _fold8(dg_rows))

    return _row_call(body, name, t, [x, f, g_post, target], [True, True, False, True],
                     [(d, F32), (d, BF16)], acc_outs=(128, d))


def bwd_pre_post(name, dx_out, x_in, g_pre, dh, f_prev, g_post_prev):
    t, d = x_in.shape

    def body(dxo_ref, x_ref, gpre_ref, dh_ref, f_ref, gpost_ref, dxi_ref, df_ref, dgpre_ref, dgpost_ref):
        dxv, dgpre_rows = _rms_bwd(x_ref[...], gpre_ref[...], dh_ref[...].astype(F32))
        dxi = dxo_ref[...] + dxv
        dxi_ref[...] = dxi
        dfv, dgpost_rows = _rms_bwd(f_ref[...], gpost_ref[...], dxi)
        df_ref[...] = dfv.astype(BF16)
        _accumulate(dgpre_ref, _fold8(dgpre_rows))
        _accumulate(dgpost_ref, _fold8(dgpost_rows))

    return _row_call(body, name, t, [dx_out, x_in, g_pre, dh, f_prev, g_post_prev],
                     [True, True, False, True, True, False], [(d, F32), (d, BF16)], acc_outs=(d, d))


def bwd_pre_final(name, dx_out, x_in, g_pre, dh):
    t, d = x_in.shape

    def body(dxo_ref, x_ref, gpre_ref, dh_ref, dxi_ref, dgpre_ref):
        dxv, dgpre_rows = _rms_bwd(x_ref[...], gpre_ref[...], dh_ref[...].astype(F32))
        dxi_ref[...] = dxo_ref[...] + dxv
        _accumulate(dgpre_ref, _fold8(dgpre_rows))

    return _row_call(body, name, t, [dx_out, x_in, g_pre, dh], [True, True, False, True], [(d, F32)], acc_outs=(d,))


def _layer_norm_parts(cv):
    mu = jnp.mean(cv, axis=-1, keepdims=True)
    xc = cv - mu
    rstd = lax.rsqrt(jnp.mean(xc * xc, axis=-1, keepdims=True) + NORM_EPS)
    return xc * rstd, rstd


def ln_silu(name, cv, g, b, y, y_block):
    t, n = cv.shape
    tr = _tile(t, ROW_TILE)

    def body(c_ref, g_ref, b_ref, y_in_ref, y_ref):
        chat, _ = _layer_norm_parts(c_ref[...])
        ln = chat * g_ref[...] + b_ref[...]
        y_ref[...] = (ln * jax.nn.sigmoid(ln)).astype(BF16)

    vec = pl.BlockSpec((1, n), lambda i: (0, 0))
    return pl.pallas_call(
        body, name=name, grid=(t // tr,),
        in_specs=[pl.BlockSpec((tr, n), lambda i: (i, 0)), vec, vec, ANY],
        out_specs=pl.BlockSpec((tr, n), lambda i: (i, y_block)),
        out_shape=jax.ShapeDtypeStruct(y.shape, y.dtype), input_output_aliases={3: 0},
        compiler_params=_params(("parallel",)),
    )(cv, g, b, y)


def ln_silu_bwd(name, cv, g, b, dy, dy_block):
    t, n = cv.shape

    def body(c_ref, g_ref, b_ref, dy_ref, dc_ref, dg_ref, db_ref):
        chat, rstd = _layer_norm_parts(c_ref[...])
        g = g_ref[...]
        ln = chat * g + b_ref[...]
        s = jax.nn.sigmoid(ln)
        dln = dy_ref[...].astype(F32) * (s * (1.0 + ln * (1.0 - s)))
        dchat = dln * g
        dc_ref[...] = rstd * (dchat - jnp.mean(dchat, axis=-1, keepdims=True)
                              - chat * jnp.mean(dchat * chat, axis=-1, keepdims=True))
        _accumulate(dg_ref, _fold8(dln * chat))
        _accumulate(db_ref, _fold8(dln))

    return _row_call(body, name, t, [cv, g, b, dy], [True, False, False, (n, dy_block)], [(n, F32)], acc_outs=(n, n))


def _chunks(t, fn, tc=TIME_CHUNK):
    tc = _tile(t, tc)

    def step(i, carry):
        fn(pl.multiple_of(i * tc, tc), tc)
        return carry

    lax.fori_loop(0, t // tc, step, 0)


def _rows_from(v, start, n):
    res = start % 8
    base = v if res == 0 else pltpu.roll(v, v.shape[0] - res, axis=0)
    return base[start - res:start - res + n, :]


def _shifted(window, offsets, tc):
    rows = window.shape[0]
    by_residue = {}
    for k, off in enumerate(offsets):
        by_residue.setdefault(off % 8, []).append((k, off))
    for res, taps in by_residue.items():
        base = window if res == 0 else pltpu.roll(window, rows - res, axis=0)
        for k, off in taps:
            yield k, base[off - res:off - res + tc, :]


def _taps(window, w_ref, offsets, tc, flip=False):
    acc = None
    for k, rows in _shifted(window, offsets, tc):
        kk = len(offsets) - 1 - k if flip else k
        term = w_ref[kk:kk + 1, :] * rows
        acc = term if acc is None else acc + term
    return acc


def _window_sums(win, tc, causal):
    sums = []
    cur, rows, step = win, tc + HALO, 1
    for _ in POOL_WINDOWS:
        rows -= 8
        if causal:
            cur = cur[8:8 + rows, :] + _rows_from(cur, 8 - step, rows)
            sums.append(cur[rows - tc:rows, :])
        else:
            cur = cur[0:rows, :] + _rows_from(cur, step, rows)
            sums.append(cur[0:tc, :])
        step *= 2
    return sums


def _pick(vals, g):
    out = vals[-1]
    for i in range(len(vals) - 2, -1, -1):
        out = jnp.where(g == i, vals[i], out)
    return out


def _pool_count(s, tc, g):
    t1 = (lax.broadcasted_iota(jnp.int32, (tc, 1), 0) + (s + 1)).astype(F32)
    width = _pick([float(w) for w in POOL_WINDOWS], g)
    return jnp.minimum(t1, width)


def pool_fwd(name, z, pool_w, pool_scale, d_pool, y_width):
    t = z.shape[0]
    ng, pg = pool_w.shape[0], pool_w.shape[1]

    def body(u_ref, w_ref, s_ref, pooled_ref, y_ref, pad):
        g = pl.program_id(0)
        pad[pl.ds(0, HALO), :] = jnp.zeros((HALO, pg), F32)

        def fill(s, tc):
            pad[pl.ds(HALO + s, tc), :] = u_ref[pl.ds(s, tc), :].astype(F32)

        def chunk(s, tc):
            win = pad[pl.ds(s, tc + HALO), :]
            total = _pick(_window_sums(win, tc, causal=True), g)
            pooled = total / _pool_count(s, tc, g) - win[HALO:HALO + tc, :]
            pooled_ref[pl.ds(s, tc), :] = pooled.astype(BF16)

        _chunks(t, fill)
        _chunks(t, chunk)
        mixed = jnp.dot(pooled_ref[...], w_ref[...], preferred_element_type=F32)
        y_ref[...] = (mixed * s_ref[...]).astype(BF16)

    col = pl.BlockSpec((t, pg), lambda g: (0, g))
    return pl.pallas_call(
        body, name=name, grid=(ng,),
        in_specs=[col, pl.BlockSpec((None, pg, pg), lambda g: (g, 0, 0)), pl.BlockSpec((1, pg), lambda g: (0, g))],
        out_specs=[col, col],
        out_shape=[jax.ShapeDtypeStruct((t, d_pool), BF16), jax.ShapeDtypeStruct((t, y_width), BF16)],
        scratch_shapes=[pltpu.VMEM((t + HALO, pg), F32)],
        compiler_params=_params(("parallel",)),
    )(z, pool_w, pool_scale)


def pool_bwd(name, pooled, dy, pool_w, pool_scale, dz):
    t, d_pool = pooled.shape
    ng, pg = pool_w.shape[0], pool_w.shape[1]

    def body(p_ref, dy_ref, w_ref, s_ref, dz_ref, du_ref, dw_ref, ds_ref, pad):
        g = pl.program_id(0)
        w = w_ref[...]
        dyv = dy_ref[...].astype(F32)
        mixed = jnp.dot(p_ref[...], w, preferred_element_type=F32)
        ds_ref[...] = jnp.sum(dyv * mixed, axis=0, keepdims=True)
        dmixed = (dyv * s_ref[...]).astype(BF16)
        dw_ref[...] = lax.dot_general(p_ref[...], dmixed, TN, preferred_element_type=F32)
        pad[...] = jnp.zeros((t + HALO, pg), F32)
        pad[pl.ds(0, t), :] = lax.dot_general(dmixed, w, NT, preferred_element_type=F32)

        def scale(s, tc):
            pad[pl.ds(s, tc), :] = pad[pl.ds(s, tc), :] / _pool_count(s, tc, g)

        def chunk(s, tc):
            win = pad[pl.ds(s, tc + HALO), :]
            total = _pick(_window_sums(win, tc, causal=False), g)
            du_ref[pl.ds(s, tc), :] = (total - win[0:tc, :] * _pool_count(s, tc, g)).astype(BF16)

        _chunks(t, scale)
        _chunks(t, chunk)

    col = pl.BlockSpec((t, pg), lambda g: (0, g))
    vec = pl.BlockSpec((1, pg), lambda g: (0, g))
    mat = pl.BlockSpec((None, pg, pg), lambda g: (g, 0, 0))
    return pl.pallas_call(
        body, name=name, grid=(ng,),
        in_specs=[col, col, mat, vec, ANY], out_specs=[col, mat, vec],
        out_shape=[jax.ShapeDtypeStruct(dz.shape, dz.dtype), jax.ShapeDtypeStruct((ng, pg, pg), F32),
                   jax.ShapeDtypeStruct((1, d_pool), F32)],
        input_output_aliases={4: 0},
        scratch_shapes=[pltpu.VMEM((t + HALO, pg), F32)],
        compiler_params=_params(("parallel",)),
    )(pooled, dy, pool_w, pool_scale, dz)


def conv_fwd(name, z, conv_w, conv_b, d_pool, d_conv):
    t = z.shape[0]
    kw = conv_w.shape[0]
    tc_ch = _tile(d_conv, CHANNEL_TILE)
    v0, g0 = d_pool // tc_ch, (d_pool + d_conv) // tc_ch

    def body(v_ref, g_ref, w_ref, b_ref, c_ref, pad):
        pad[pl.ds(0, HALO), :] = jnp.zeros((HALO, tc_ch), F32)

        def fill(s, tc):
            pad[pl.ds(HALO + s, tc), :] = v_ref[pl.ds(s, tc), :].astype(F32) * jax.nn.sigmoid(g_ref[pl.ds(s, tc), :].astype(F32))

        def chunk(s, tc):
            win = pad[pl.ds(s, tc + HALO), :]
            c_ref[pl.ds(s, tc), :] = _taps(win, w_ref, [HALO - (kw - 1) + k for k in range(kw)], tc) + b_ref[...]

        _chunks(t, fill)
        _chunks(t, chunk)

    return pl.pallas_call(
        body, name=name, grid=(d_conv // tc_ch,),
        in_specs=[pl.BlockSpec((t, tc_ch), lambda j: (0, v0 + j)), pl.BlockSpec((t, tc_ch), lambda j: (0, g0 + j)),
                  pl.BlockSpec((kw, tc_ch), lambda j: (0, j)), pl.BlockSpec((1, tc_ch), lambda j: (0, j))],
        out_specs=pl.BlockSpec((t, tc_ch), lambda j: (0, j)),
        out_shape=jax.ShapeDtypeStruct((t, d_conv), F32),
        scratch_shapes=[pltpu.VMEM((t + HALO, tc_ch), F32)],
        compiler_params=_params(("parallel",)),
    )(z, z, conv_w, conv_b)


def conv_bwd(name, z, dc, conv_w, d_pool, d_conv):
    t = z.shape[0]
    kw = conv_w.shape[0]
    tc_ch = _tile(d_conv, CHANNEL_TILE)
    v0, g0 = d_pool // tc_ch, (d_pool + d_conv) // tc_ch

    def body(v_ref, g_ref, dc_ref, w_ref, dz_ref, dw_ref, db_ref, pad_a, pad_dc, acc_w, acc_b, tiles, sems):
        j = pl.program_id(0)
        dv_ref, dg_ref = tiles.at[0], tiles.at[1]
        writes = [pltpu.make_async_copy(tiles.at[p], dz_ref.at[:, pl.ds((first + j) * tc_ch, tc_ch)], sems.at[p])
                  for p, first in enumerate([v0, g0])]

        def wait_writes():
            for cp in writes:
                cp.wait()

        pad_a[pl.ds(0, HALO), :] = jnp.zeros((HALO, tc_ch), F32)
        pad_dc[pl.ds(t, HALO), :] = jnp.zeros((HALO, tc_ch), F32)
        acc_w[...] = jnp.zeros_like(acc_w)
        acc_b[...] = jnp.zeros_like(acc_b)

        def fill(s, tc):
            pad_a[pl.ds(HALO + s, tc), :] = v_ref[pl.ds(s, tc), :].astype(F32) * jax.nn.sigmoid(g_ref[pl.ds(s, tc), :].astype(F32))
            pad_dc[pl.ds(s, tc), :] = dc_ref[pl.ds(s, tc), :]

        def chunk(s, tc):
            dcv = pad_dc[pl.ds(s, tc), :]
            win_a = pad_a[pl.ds(s, tc + HALO), :]
            for k, rows in _shifted(win_a, [HALO - (kw - 1) + k for k in range(kw)], tc):
                acc_w[pl.ds(8 * k, 8), :] += _fold8(dcv * rows)
            acc_b[...] += _fold8(dcv)
            da = _taps(pad_dc[pl.ds(s, tc + HALO), :], w_ref, list(range(kw)), tc, flip=True)
            vv = v_ref[pl.ds(s, tc), :].astype(F32)
            sg = jax.nn.sigmoid(g_ref[pl.ds(s, tc), :].astype(F32))
            dv_ref[pl.ds(s, tc), :] = (da * sg).astype(BF16)
            dg_ref[pl.ds(s, tc), :] = (da * vv * sg * (1.0 - sg)).astype(BF16)

        _chunks(t, fill)
        pl.when(j > 0)(wait_writes)
        _chunks(t, chunk)
        for cp in writes:
            cp.start()
        pl.when(j == n_tiles - 1)(wait_writes)
        for k in range(kw):
            dw_ref[k:k + 1, :] = jnp.sum(acc_w[pl.ds(8 * k, 8), :], axis=0, keepdims=True)
        db_ref[...] = jnp.sum(acc_b[...], axis=0, keepdims=True)

    n_tiles = d_conv // tc_ch
    return pl.pallas_call(
        body, name=name, grid=(n_tiles,),
        in_specs=[pl.BlockSpec((t, tc_ch), lambda j: (0, v0 + j)), pl.BlockSpec((t, tc_ch), lambda j: (0, g0 + j)),
                  pl.BlockSpec((t, tc_ch), lambda j: (0, j)), pl.BlockSpec((kw, tc_ch), lambda j: (0, j))],
        out_specs=[ANY, pl.BlockSpec((kw, tc_ch), lambda j: (0, j)), pl.BlockSpec((1, tc_ch), lambda j: (0, j))],
        out_shape=[jax.ShapeDtypeStruct((t, d_pool + 2 * d_conv), BF16),
                   jax.ShapeDtypeStruct((kw, d_conv), F32), jax.ShapeDtypeStruct((1, d_conv), F32)],
        scratch_shapes=[pltpu.VMEM((t + HALO, tc_ch), F32), pltpu.VMEM((t + HALO, tc_ch), F32),
                        pltpu.VMEM((8 * kw, tc_ch), F32), pltpu.VMEM((8, tc_ch), F32),
                        pltpu.VMEM((2, t, tc_ch), BF16), pltpu.SemaphoreType.DMA((2,))],
        compiler_params=_params(("arbitrary",)),
    )(z, z, dc, conv_w)


def short_fwd(name, z, conv_w, d_short):
    t = z.shape[0]
    kw = conv_w.shape[0]
    tc_ch = _tile(d_short, CHANNEL_TILE)
    nt = d_short // tc_ch

    def body(b_ref, c_ref, u_ref, w_ref, y_ref, pad):
        pad[pl.ds(0, HALO), :] = jnp.zeros((HALO, tc_ch), F32)

        def fill(s, tc):
            pad[pl.ds(HALO + s, tc), :] = c_ref[pl.ds(s, tc), :].astype(F32) * u_ref[pl.ds(s, tc), :].astype(F32)

        def chunk(s, tc):
            win = pad[pl.ds(s, tc + HALO), :]
            cq = _taps(win, w_ref, [HALO - (kw - 1) + k for k in range(kw)], tc)
            y_ref[pl.ds(s, tc), :] = (b_ref[pl.ds(s, tc), :].astype(F32) * cq).astype(BF16)

        _chunks(t, fill)
        _chunks(t, chunk)

    return pl.pallas_call(
        body, name=name, grid=(nt,),
        in_specs=[pl.BlockSpec((t, tc_ch), lambda j: (0, j)), pl.BlockSpec((t, tc_ch), lambda j: (0, nt + j)),
                  pl.BlockSpec((t, tc_ch), lambda j: (0, 2 * nt + j)), pl.BlockSpec((kw, tc_ch), lambda j: (0, j))],
        out_specs=pl.BlockSpec((t, tc_ch), lambda j: (0, j)),
        out_shape=jax.ShapeDtypeStruct((t, d_short), BF16),
        scratch_shapes=[pltpu.VMEM((t + HALO, tc_ch), F32)],
        compiler_params=_params(("parallel",)),
    )(z, z, z, conv_w)


def short_bwd(name, z, dy, conv_w, d_short):
    t = z.shape[0]
    kw = conv_w.shape[0]
    tc_ch = _tile(d_short, CHANNEL_TILE)
    nt = d_short // tc_ch

    def body(b_ref, c_ref, u_ref, dy_ref, w_ref, dz_ref, dw_ref, pad_q, pad_dcq, acc_w, tiles, sems):
        j = pl.program_id(0)
        db_ref, dcg_ref, du_ref = tiles.at[0], tiles.at[1], tiles.at[2]
        writes = [pltpu.make_async_copy(tiles.at[p], dz_ref.at[:, pl.ds((p * nt + j) * tc_ch, tc_ch)], sems.at[p])
                  for p in range(3)]

        def wait_writes():
            for cp in writes:
                cp.wait()

        pad_q[pl.ds(0, HALO), :] = jnp.zeros((HALO, tc_ch), F32)
        pad_dcq[pl.ds(t, HALO), :] = jnp.zeros((HALO, tc_ch), F32)
        acc_w[...] = jnp.zeros_like(acc_w)

        def fill(s, tc):
            rows = pl.ds(s, tc)
            pad_q[pl.ds(HALO + s, tc), :] = c_ref[rows, :].astype(F32) * u_ref[rows, :].astype(F32)
            pad_dcq[rows, :] = dy_ref[rows, :].astype(F32) * b_ref[rows, :].astype(F32)

        def chunk(s, tc):
            rows = pl.ds(s, tc)
            win_q = pad_q[pl.ds(s, tc + HALO), :]
            dcq = pad_dcq[rows, :]
            cq = None
            for k, shifted in _shifted(win_q, [HALO - (kw - 1) + k for k in range(kw)], tc):
                acc_w[pl.ds(8 * k, 8), :] += _fold8(dcq * shifted)
                term = w_ref[k:k + 1, :] * shifted
                cq = term if cq is None else cq + term
            db_ref[rows, :] = (dy_ref[rows, :].astype(F32) * cq).astype(BF16)
            dq = _taps(pad_dcq[pl.ds(s, tc + HALO), :], w_ref, list(range(kw)), tc, flip=True)
            dcg_ref[rows, :] = (dq * u_ref[rows, :].astype(F32)).astype(BF16)
            du_ref[rows, :] = (dq * c_ref[rows, :].astype(F32)).astype(BF16)

        _chunks(t, fill)
        pl.when(j > 0)(wait_writes)
        _chunks(t, chunk)
        for cp in writes:
            cp.start()
        pl.when(j == nt - 1)(wait_writes)
        for k in range(kw):
            dw_ref[k:k + 1, :] = jnp.sum(acc_w[pl.ds(8 * k, 8), :], axis=0, keepdims=True)

    zspec = [pl.BlockSpec((t, tc_ch), lambda j, o=o: (0, o * nt + j)) for o in range(3)]
    return pl.pallas_call(
        body, name=name, grid=(nt,),
        in_specs=[*zspec, pl.BlockSpec((t, tc_ch), lambda j: (0, j)), pl.BlockSpec((kw, tc_ch), lambda j: (0, j))],
        out_specs=[ANY, pl.BlockSpec((kw, tc_ch), lambda j: (0, j))],
        out_shape=[jax.ShapeDtypeStruct((t, 3 * d_short), BF16), jax.ShapeDtypeStruct((kw, d_short), F32)],
        scratch_shapes=[pltpu.VMEM((t + HALO, tc_ch), F32), pltpu.VMEM((t + HALO, tc_ch), F32),
                        pltpu.VMEM((8 * kw, tc_ch), F32), pltpu.VMEM((3, t, tc_ch), BF16),
                        pltpu.SemaphoreType.DMA((3,))],
        compiler_params=_params(("arbitrary",)),
    )(z, z, z, dy, conv_w)


def _adamw_update(w, m, v, g):
    nm = ADAM_B1 * m + (1.0 - ADAM_B1) * g
    nv = ADAM_B2 * v + (1.0 - ADAM_B2) * (g * g)
    m_hat = nm / (1.0 - ADAM_B1 ** ADAM_STEP)
    v_hat = nv / (1.0 - ADAM_B2 ** ADAM_STEP)
    return -ADAM_LR * (m_hat / (jnp.sqrt(v_hat) + ADAM_EPS) + ADAM_WD * w), nm, nv


def adamw_replicated(name, params, first_moments, second_moments, contributions, layout, scalar_at):
    n = len(params)
    n_slots = contributions.shape[0]

    def total(c_ref, row, lane, rows, lanes):
        acc = c_ref[0, row:row + rows, lane:lane + lanes]
        for slot in range(1, n_slots):
            acc = acc + c_ref[slot, row:row + rows, lane:lane + lanes]
        return acc

    def body(*refs):
        ws, ms, vs, c_ref = refs[:n], refs[n:2 * n], refs[2 * n:3 * n], refs[3 * n]
        outs = refs[3 * n + 1:]
        outs[0][...] = total(c_ref, *scalar_at, 1, 128)
        for i, (row, lane) in enumerate(layout):
            g = total(c_ref, row, lane, *params[i].shape)
            grad_ref, delta_ref, nm_ref, nv_ref = outs[1 + 4 * i:5 + 4 * i]
            grad_ref[...] = g
            delta_ref[...], nm_ref[...], nv_ref[...] = _adamw_update(ws[i][...], ms[i][...], vs[i][...], g)

    out_shape = [jax.ShapeDtypeStruct((1, 128), F32)]
    for p in params:
        out_shape += [jax.ShapeDtypeStruct(p.shape, F32)] * 4
    return pl.pallas_call(body, name=name, out_shape=out_shape)(*params, *first_moments, *second_moments, contributions)


def adamw(name, w, m, v, contributions):
    r, c = w.shape
    nc = len(contributions)
    n_slots = contributions[0].shape[0]
    tr = 256 if c <= 1024 else 128
    if any(a.shape[1] % tr for a in contributions):
        assert nc == 1
        tr = r
    tiles = [a.shape[1] // tr for a in contributions]
    first = [sum(tiles[:j]) for j in range(nc)]

    def body(w_ref, m_ref, v_ref, *rest):
        g_refs, (grad_ref, delta_ref, nm_ref, nv_ref) = rest[:nc], rest[nc:]
        i = pl.program_id(0)
        g = None
        for j, g_ref in enumerate(g_refs):
            s = g_ref[0].astype(F32)
            for slot in range(1, n_slots):
                s = s + g_ref[slot].astype(F32)
            g = s if g is None else jnp.where(i >= first[j], s, g)
        grad_ref[...] = g
        delta_ref[...], nm_ref[...], nv_ref[...] = _adamw_update(w_ref[...], m_ref[...], v_ref[...], g)

    blk = pl.BlockSpec((tr, c), lambda i: (i, 0))
    g_specs = [pl.BlockSpec((n_slots, tr, c), lambda i, j=j: (0, jnp.clip(i - first[j], 0, tiles[j] - 1), 0))
               for j in range(nc)]
    return pl.pallas_call(
        body, name=name, grid=(r // tr,),
        in_specs=[blk, blk, blk, *g_specs],
        out_specs=[blk] * 4, out_shape=[jax.ShapeDtypeStruct((r, c), F32)] * 4,
        compiler_params=_params(("parallel",)),
    )(w, m, v, *contributions)


def _pad_rows(a, rows):
    return jnp.pad(a, ((0, rows - a.shape[0]), (0, 0)))


def kernel(x, mix_pre_g, mix_post_g, ffn_pre_g, ffn_post_g, ab_w_in, pool_w, pool_scale, conv_w, conv_b, conv_ln_g, conv_ln_b, ab_w_out, sc_w_in, sc_conv_w, sc_w_out, ffn_w1, ffn_w2, loss_target, m_mix_pre_g, m_mix_post_g, m_ffn_pre_g, m_ffn_post_g, m_ab_w_in, m_pool_w, m_pool_scale, m_conv_w, m_conv_b, m_conv_ln_g, m_conv_ln_b, m_ab_w_out, m_sc_w_in, m_sc_conv_w, m_sc_w_out, m_ffn_w1, m_ffn_w2, v_mix_pre_g, v_mix_post_g, v_ffn_pre_g, v_ffn_post_g, v_ab_w_in, v_pool_w, v_pool_scale, v_conv_w, v_conv_b, v_conv_ln_g, v_conv_ln_b, v_ab_w_out, v_sc_w_in, v_sc_conv_w, v_sc_w_out, v_ffn_w1, v_ffn_w2):
    t, d = x.shape[1], x.shape[2]
    d_pool = pool_scale.shape[1]
    d_conv = conv_b.shape[1]
    d_short = d
    ng, pg = pool_w.shape[1], pool_w.shape[3]
    kw, ks = conv_w.shape[1], sc_conv_w.shape[1]
    nb_ab, nb_sc, nb_ff = ab_w_in.shape[2], sc_w_in.shape[2], ffn_w1.shape[2]

    xs = x[0]
    target = loss_target[0]

    lanes = min(128, d_conv // N_DEV)
    small_rows = [kw * (d_conv // N_DEV) // lanes, ks * (d_short // N_DEV) // lanes, ng * (pg // N_DEV) * pg // lanes]
    small_total = -(-sum(small_rows) // 8) * 8
    r0, r1, r2 = small_rows[0], small_rows[0] + small_rows[1], sum(small_rows)

    def pack_small(a_conv, a_sconv, a_pool):
        parts = [a_conv[0].reshape(-1, lanes), a_sconv[0].reshape(-1, lanes), a_pool[0].reshape(-1, lanes)]
        return _pad_rows(jnp.concatenate(parts, axis=0), small_total)

    shards = {
        "ab_in": (ab_w_in, 0, BF16), "small": (pack_small(conv_w, sc_conv_w, pool_w)[None], 0, F32),
        "ab_out": (ab_w_out, 0, BF16), "ff1_0": (ffn_w1, 0, BF16), "ff2_0": (ffn_w2, 0, BF16),
        "sc_in": (sc_w_in, 0, BF16), "sc_out": (sc_w_out, 0, BF16),
        "ff1_1": (ffn_w1, 1, BF16), "ff2_1": (ffn_w2, 1, BF16)}
    direct = ["ab_in", "small", "ab_out"]
    zones = {nm: place_shard("place_" + nm, *shards[nm]) for nm in direct}
    started, token = copies_start("gather_start", [[zones[nm]] for nm in direct], _first_hop, 4)
    started = dict(zip(direct, started))
    ring = {}
    for nm in ["ff1_0", "ff2_0"]:
        zones[nm] = place_shard("place_" + nm, *shards[nm], deps=[token])
        (ring[nm],), token = copies_start("ring_start_" + nm, [[zones[nm]]], _ring_hop1, 3, deps=[token])
    for nm in shards:
        if nm not in zones:
            zones[nm] = place_shard("place_" + nm, *shards[nm], deps=[token])

    ties = [0]

    def after(v, *deps):
        ties[0] += 1
        return tie(f"tie_{ties[0]}", v, *deps)

    def fetch_begin(nm, dep):
        (zone,) = copies_wait("gather_wait_" + nm, started[nm], _first_hop, dep)
        (hop,), tok = copies_start("forward_start_" + nm, [[zone]], _second_hop, 3)
        return hop, tok

    def fetch_end(nm, hop, dep):
        return copies_wait("forward_wait_" + nm, hop, _second_hop, dep)[0]

    def ring_step(tag, dep, second=None, first=None, third=None):
        names, groups, hops, counts = [], [], [], []
        if second is not None:
            groups.append(copies_wait("ring1_wait_" + second, ring[second], _ring_hop1, dep))
            names, hops, counts = names + [second], hops + [_ring_hop2], counts + [4]
        if first is not None:
            groups.append([zones[first]])
            names, hops, counts = names + [first], hops + [_ring_hop1], counts + [3]
        if third is not None:
            groups.append(copies_wait("ring2_wait_" + third, ring[third], _ring_hop2, dep))
            names, hops, counts = names + [third], hops + [_ring_hop3], counts + [1]
        begun, tok = copies_start("ring_start_" + tag, groups, hops, counts, deps=[dep])
        ring.update(zip(names, begun))
        return tok

    def ring_done(nm, dep):
        return copies_wait("ring3_wait_" + nm, ring[nm], _ring_hop3, dep)[0]

    relu = lambda r: jnp.maximum(r, 0.0)
    square = lambda a: a * a
    relu2_bwd = lambda r, a: r * (2.0 * a.astype(F32))

    def row(vec, l):
        return vec[l:l + 1]

    hop_small, _ = fetch_begin("small", token)
    hop_ab_in, tok = fetch_begin("ab_in", token)
    w_small = fetch_end("small", hop_small, tok)
    w_ab_in = fetch_end("ab_in", hop_ab_in, tok)
    w_conv = w_small[:, :r0].reshape(N_DEV, kw, -1).transpose(1, 0, 2).reshape(kw, d_conv)
    w_sconv = w_small[:, r0:r1].reshape(N_DEV, ks, -1).transpose(1, 0, 2).reshape(ks, d_short)
    w_pool = w_small[:, r1:r2].reshape(N_DEV, ng, -1, pg).transpose(1, 0, 2, 3).reshape(ng, pg, pg).astype(BF16)
    h0 = norm_pre("norm_pre", xs, after(row(mix_pre_g, 0), token))
    z0 = mm_nn_blocked("ab_in", h0, w_ab_in, out_dtype=BF16)
    hop, tok = fetch_begin("ab_out", z0)
    z0 = after(z0, tok)
    pooled, y0 = pool_fwd("pool_fwd", z0, w_pool, pool_scale, d_pool, d_pool + d_conv)
    cv = conv_fwd("conv_fwd", z0, w_conv, conv_b, d_pool, d_conv)
    y0 = ln_silu("ln_silu", cv, conv_ln_g, conv_ln_b, y0, d_pool // d_conv)
    w_ab_out = fetch_end("ab_out", hop, y0)
    tok = ring_step("a", w_ab_out, second="ff1_0", first="sc_in")
    y0 = after(y0, tok)
    m0 = mm_nn("ab_out", y0, w_ab_out.reshape(d_pool + d_conv, d), out_dtype=F32)
    x1, h1 = post_pre("post_pre_0", xs, m0, row(mix_post_g, 0), row(ffn_pre_g, 0))
    tok = ring_step("b", h1, second="ff2_0", first="sc_out", third="ff1_0")
    w_ff1_0 = ring_done("ff1_0", tok)
    a0 = mm_nn_blocked("ffn0_up", h1, w_ff1_0, out_dtype=BF16, epilogue=relu)
    tok = ring_step("c", a0, second="sc_in", first="ff1_1", third="ff2_0")
    w_ff2_0 = ring_done("ff2_0", tok).reshape(-1, d)
    f0 = mm_nn("ffn0_down", a0, w_ff2_0, out_dtype=F32, tk=2048, lhs_fn=square)
    tok = ring_step("d", f0, second="sc_out", first="ff2_1", third="sc_in")
    f0 = after(f0, tok)
    x2, h2 = post_pre("post_pre_1", x1, f0, row(ffn_post_g, 0), row(mix_pre_g, 1))
    w_sc_in = ring_done("sc_in", h2)
    z1 = mm_nn_blocked("sc_in", h2, w_sc_in, out_dtype=BF16)
    tok = ring_step("e", z1, second="ff1_1", third="sc_out")
    z1 = after(z1, tok)
    y1 = short_fwd("short_fwd", z1, w_sconv, d_short)
    w_sc_out = ring_done("sc_out", y1).reshape(d_short, d)
    m1 = mm_nn("sc_out", y1, w_sc_out, out_dtype=F32)
    tok = ring_step("f", m1, second="ff2_1")
    m1 = after(m1, tok)
    x3, h3 = post_pre("post_pre_2", x2, m1, row(mix_post_g, 1), row(ffn_pre_g, 1))
    tok = ring_step("g", h3, third="ff1_1")
    w_ff1_1 = ring_done("ff1_1", tok)
    a1 = mm_nn_blocked("ffn1_up", h3, w_ff1_1, out_dtype=BF16, epilogue=relu)
    tok = ring_step("h", a1, third="ff2_1")
    w_ff2_1 = ring_done("ff2_1", tok).reshape(-1, d)
    f1 = mm_nn("ffn1_down", a1, w_ff2_1, out_dtype=F32, tk=2048, lhs_fn=square)
    dx4, df1, loss_part, dg_ffn_post1 = post_loss("post_loss", x3, f1, row(ffn_post_g, 1), target)

    red = {}

    def reduce_step(dep, begin=None, middle=None):
        tags, groups, hops, counts = [], [], [], []
        if begin is not None:
            tag, g = begin
            tags, groups = tags + [tag], groups + [[g, lax.empty((N_CHIP,) + g.shape[1:], g.dtype)]]
            hops, counts = hops + [_pair_hop], counts + [N_CHIP]
        if middle is not None:
            g, from_sibling = copies_wait("pair_wait_" + middle, red[middle], _pair_hop, dep)
            tags, groups = tags + [middle], groups + [list(pair_add("pair_add_" + middle, g, from_sibling))]
            hops, counts = hops + [_chip_hop], counts + [3]
        begun, tok = copies_start("reduce_start_" + "_".join(tags), groups, hops, counts, deps=[dep])
        red.update(zip(tags, begun))
        return tok

    def reduce_end(tag, dep):
        return copies_wait("chips_wait_" + tag, red[tag], _chip_hop, dep)[1]

    dpre, dw = mm_bwd_pair("ffn1_da_dw2", df1, w_ff2_1, a1, out_dtype=BF16, act_fn=square, epilogue=relu2_bwd)
    dpre = after(dpre, reduce_step(dpre, begin=("ff2_1", dw.reshape(N_DEV, -1, d))))
    dh3, dw = mm_bwd_pair_blocked("ffn1_dh_dw1", dpre, w_ff1_1, h3, out_dtype=BF16)
    dh3 = after(dh3, reduce_step(dh3, begin=("ff1_1", dw), middle="ff2_1"))
    dx3, dm1, dg_ffn_pre1, dg_mix_post1 = bwd_pre_post("bwd_3", dx4, x3, row(ffn_pre_g, 1), dh3, m1, row(mix_post_g, 1))

    dy1, dw = mm_bwd_pair("sc_dy_dwout", dm1, w_sc_out, y1, out_dtype=BF16)
    dy1 = after(dy1, reduce_step(dy1, begin=("sc_out", dw.reshape(N_DEV, -1, d)), middle="ff1_1"))
    dz1, dw_sconv = short_bwd("short_bwd", z1, dy1, w_sconv, d_short)
    dh2, dw = mm_bwd_pair_blocked("sc_dh_dwin", dz1, w_sc_in, h2, out_dtype=BF16)
    dh2 = after(dh2, reduce_step(dh2, begin=("sc_in", dw), middle="sc_out"))
    dx2, df0, dg_mix_pre1, dg_ffn_post0 = bwd_pre_post("bwd_2", dx3, x2, row(mix_pre_g, 1), dh2, f0, row(ffn_post_g, 0))

    dpre, dw = mm_bwd_pair("ffn0_da_dw2", df0, w_ff2_0, a0, out_dtype=BF16, act_fn=square, epilogue=relu2_bwd)
    dpre = after(dpre, reduce_step(dpre, begin=("ff2_0", dw.reshape(N_DEV, -1, d)), middle="sc_in"))
    dh1, dw = mm_bwd_pair_blocked("ffn0_dh_dw1", dpre, w_ff1_0, h1, out_dtype=BF16)
    dh1 = after(dh1, reduce_step(dh1, begin=("ff1_0", dw), middle="ff2_0"))
    dx1, dm0, dg_ffn_pre0, dg_mix_post0 = bwd_pre_post("bwd_1", dx2, x1, row(ffn_pre_g, 0), dh1, m0, row(mix_post_g, 0))

    dy0, dw = mm_bwd_pair("ab_dy_dwout", dm0, w_ab_out.reshape(d_pool + d_conv, d), y0, out_dtype=BF16)
    dy0 = after(dy0, reduce_step(dy0, begin=("ab_out", dw.reshape(N_DEV, -1, d)), middle="ff1_0"))
    dcv, dg_ln_g, dg_ln_b = ln_silu_bwd("ln_silu_bwd", cv, conv_ln_g, conv_ln_b, dy0, d_pool // d_conv)
    dz0, dw_conv, dg_conv_b = conv_bwd("conv_bwd", z0, dcv, w_conv, d_pool, d_conv)
    dz0, dw_pool, dg_pool_scale = pool_bwd("pool_bwd", pooled, dy0, w_pool, pool_scale, dz0)
    small_parts = [
        dw_conv.reshape(kw, N_DEV, -1).transpose(1, 0, 2).reshape(N_DEV, -1, lanes),
        dw_sconv.reshape(ks, N_DEV, -1).transpose(1, 0, 2).reshape(N_DEV, -1, lanes),
        dw_pool.reshape(ng, N_DEV, pg // N_DEV, pg).transpose(1, 0, 2, 3).reshape(N_DEV, -1, lanes),
    ]
    small = jnp.pad(jnp.concatenate(small_parts, axis=1), ((0, 0), (0, small_total - r2), (0, 0)))
    dz0 = after(dz0, reduce_step(dz0, begin=("small", small), middle="ab_out"))
    dh0, dw = mm_bwd_pair_blocked("ab_dh_dwin", dz0, w_ab_in, h0, out_dtype=BF16)
    dh0 = after(dh0, reduce_step(dh0, begin=("ab_in", dw), middle="small"))
    grad_x, dg_mix_pre0 = bwd_pre_final("bwd_0", dx1, xs, row(mix_pre_g, 0), dh0)
    tok = reduce_step(grad_x, middle="ab_in")

    fold = lambda a: jnp.sum(a, axis=0, keepdims=True)
    rep_rows = [fold(dg_mix_pre0), fold(dg_mix_pre1), fold(dg_mix_post0), fold(dg_mix_post1),
                fold(dg_ffn_pre0), fold(dg_ffn_pre1), fold(dg_ffn_post0), fold(dg_ffn_post1)]
    tail = jnp.concatenate([dg_pool_scale, dg_conv_b, fold(dg_ln_g), fold(dg_ln_b)], axis=1).reshape(-1, d)
    loss_row = jnp.pad(jnp.sum(loss_part).reshape(1, 1), ((0, 0), (0, d - 1)))
    rep = _pad_rows(jnp.concatenate(rep_rows + [tail, loss_row], axis=0), 16)
    (rep_hop,), tok = copies_start("rep_start", [[place_shard("place_rep", rep[None], 0, F32)]], _first_hop, 4,
                                   deps=[tok])
    rep_layout = [(0, 0), (2, 0), (4, 0), (6, 0)]
    for offset in (0, d_pool, d_pool + d_conv, d_pool + 2 * d_conv):
        rep_layout.append((len(rep_rows) + offset // d, offset % d))
    loss_at = (len(rep_rows) + tail.shape[0], 0)

    def upd(name, w, m, v, contribs):
        shape = w.shape
        flat2 = lambda a: a.reshape(-1, shape[-1])
        outs = adamw(name, flat2(w), flat2(m), flat2(v), contribs)
        return [o.reshape(shape) for o in outs]

    g_ff2 = [reduce_end("ff2_0", tok), reduce_end("ff2_1", tok)]
    o_ff2 = upd("adam_ffn_w2", ffn_w2, m_ffn_w2, v_ffn_w2, g_ff2)
    (rep_zone,) = copies_wait("rep_wait", rep_hop, _first_hop, o_ff2[0])
    (rep_hop,), _ = copies_start("rep_forward_start", [[rep_zone]], _second_hop, 3)
    g_ff1 = [reduce_end("ff1_0", o_ff2[0]), reduce_end("ff1_1", o_ff2[0])]
    o_ff1 = upd("adam_ffn_w1", ffn_w1, m_ffn_w1, v_ffn_w1, g_ff1)
    (rep_all,) = copies_wait("rep_forward_wait", rep_hop, _second_hop, o_ff1[0])
    loss_sum, *o_rep = adamw_replicated(
        "adam_replicated",
        [mix_pre_g, mix_post_g, ffn_pre_g, ffn_post_g, pool_scale, conv_b, conv_ln_g, conv_ln_b],
        [m_mix_pre_g, m_mix_post_g, m_ffn_pre_g, m_ffn_post_g, m_pool_scale, m_conv_b, m_conv_ln_g, m_conv_ln_b],
        [v_mix_pre_g, v_mix_post_g, v_ffn_pre_g, v_ffn_post_g, v_pool_scale, v_conv_b, v_conv_ln_g, v_conv_ln_b],
        rep_all, rep_layout, loss_at)
    loss = loss_sum[0, 0] * (0.5 / d)
    o_sc_out = upd("adam_sc_out", sc_w_out, m_sc_w_out, v_sc_w_out, [reduce_end("sc_out", o_ff1[0])])
    o_sc_in = upd("adam_sc_in", sc_w_in, m_sc_w_in, v_sc_w_in, [reduce_end("sc_in", o_sc_out[0])])
    o_ab_out = upd("adam_ab_out", ab_w_out, m_ab_w_out, v_ab_w_out, [reduce_end("ab_out", o_sc_in[0])])
    o_small = adamw("adam_small", pack_small(conv_w, sc_conv_w, pool_w), pack_small(m_conv_w, m_sc_conv_w, m_pool_w),
                    pack_small(v_conv_w, v_sc_conv_w, v_pool_w), [reduce_end("small", o_ab_out[0])])
    o_ab_in = upd("adam_ab_in", ab_w_in, m_ab_w_in, v_ab_w_in, [reduce_end("ab_in", o_small[0])])

    def unpack_small(o):
        return o[:r0].reshape(conv_w.shape), o[r0:r1].reshape(sc_conv_w.shape), o[r1:r2].reshape(pool_w.shape)

    results = []
    for kind in range(4):
        g_mix_pre, g_mix_post, g_ffn_pre, g_ffn_post, g_scale, g_conv_b, g_ln_g, g_ln_b = o_rep[kind::4]
        s_conv, s_sconv, s_pool = unpack_small(o_small[kind])
        results.append([
            g_mix_pre, g_mix_post, g_ffn_pre, g_ffn_post,
            o_ab_in[kind], s_pool, g_scale, s_conv, g_conv_b, g_ln_g, g_ln_b,
            o_ab_out[kind], o_sc_in[kind], s_sconv, o_sc_out[kind], o_ff1[kind], o_ff2[kind]])

    return (loss, grad_x[None], *results[0], *results[1], *results[2], *results[3])
```

```python
import jax
import jax.numpy as jnp
from jax import lax
from jax.experimental import pallas as pl
from jax.experimental.pallas import tpu as pltpu

F32 = jnp.float32
BF16 = jnp.bfloat16
MESH = pl.DeviceIdType.MESH
ANY = pl.BlockSpec(memory_space=pl.ANY)

NORM_EPS = 1e-6
POOL_WINDOWS = (2, 4, 8, 16)
ADAM_LR = 0.001
ADAM_B1 = 0.9
ADAM_B2 = 0.999
ADAM_EPS = 1e-08
ADAM_WD = 0.01
ADAM_STEP = 10

N_DEV = 8
VMEM_LIMIT = 56 * 1024 * 1024
PAIR_ADD_BLOCK = 1 << 20
MATMUL_ROWS = 2048
ROW_TILE = 256
CHANNEL_TILE = 256
TIME_CHUNK = 64
HALO = 32

NN = (((1,), (0,)), ((), ()))
NT = (((1,), (1,)), ((), ()))
TN = (((0,), (0,)), ((), ()))


def _params(sem):
    return pltpu.CompilerParams(dimension_semantics=sem, vmem_limit_bytes=VMEM_LIMIT)


def _place():
    x, y, c = lax.axis_index("x"), lax.axis_index("y"), lax.axis_index("c")
    return x, y, c


def _slot(px, py, pc):
    return 4 * px + 2 * py + pc


HBM = pl.BlockSpec(memory_space=pltpu.HBM)
SEM = pl.BlockSpec(memory_space=pltpu.SEMAPHORE)
EFFECT = pltpu.SideEffectType.DATAFLOW_SIDE_EFFECTING
TOKEN = jax.ShapeDtypeStruct((8, 128), F32)


def _in_hbm(a):
    return pltpu.with_memory_space_constraint(a, pltpu.HBM)


CHIPS = [(0, 0), (0, 1), (1, 0), (1, 1)]
N_CHIP = len(CHIPS)


def _chip(px, py):
    return 2 * px + py


def _first_hop(bufs, sends, recvs, waiting):
    (land,) = bufs
    x, y, c = _place()
    me = _slot(x, y, c)
    peers = [(x, y, 1 - c), (1 - x, y, c), (x, 1 - y, c), (1 - x, 1 - y, c)]
    return [pltpu.make_async_remote_copy(
        src_ref=land.at[me], dst_ref=land.at[_slot(*p) if waiting else me],
        send_sem=sends.at[k], recv_sem=recvs.at[k], device_id=p, device_id_type=MESH) for k, p in enumerate(peers)]


def _second_hop(bufs, sends, recvs, waiting):
    (land,) = bufs
    x, y, c = _place()
    return [pltpu.make_async_remote_copy(
        src_ref=land.at[_slot(px, py, c)], dst_ref=land.at[_slot(px, py, 1 - c if waiting else c)],
        send_sem=sends.at[k], recv_sem=recvs.at[k], device_id=(x, y, 1 - c), device_id_type=MESH)
        for k, (px, py) in enumerate([(1 - x, y), (x, 1 - y), (1 - x, 1 - y)])]


def _ring_hop1(bufs, sends, recvs, waiting):
    (land,) = bufs
    x, y, c = _place()
    me = _slot(x, y, c)
    peers = [(1 - x, y, c), (x, 1 - y, c), (x, y, 1 - c)]
    return [pltpu.make_async_remote_copy(
        src_ref=land.at[me], dst_ref=land.at[_slot(*p) if waiting else me],
        send_sem=sends.at[k], recv_sem=recvs.at[k], device_id=p, device_id_type=MESH) for k, p in enumerate(peers)]


def _ring_hop2(bufs, sends, recvs, waiting):
    (land,) = bufs
    x, y, c = _place()
    half = land.shape[1] // 2
    first, second = pl.ds(0, half), pl.ds(half, half)
    nx, ny, diag = _slot(1 - x, y, c), _slot(x, 1 - y, c), _slot(1 - x, 1 - y, c)
    plan = [
        (land.at[ny, first], land.at[diag, first], (1 - x, y, c)),
        (land.at[nx, second], land.at[diag, second], (x, 1 - y, c)),
        (land.at[nx], land.at[_slot(1 - x, y, 1 - c)], (x, y, 1 - c)),
        (land.at[ny], land.at[_slot(x, 1 - y, 1 - c)], (x, y, 1 - c))]
    return [pltpu.make_async_remote_copy(
        src_ref=src, dst_ref=mine if waiting else src, send_sem=sends.at[k], recv_sem=recvs.at[k],
        device_id=to, device_id_type=MESH) for k, (src, mine, to) in enumerate(plan)]


def _ring_hop3(bufs, sends, recvs, waiting):
    (land,) = bufs
    x, y, c = _place()
    return [pltpu.make_async_remote_copy(
        src_ref=land.at[_slot(1 - x, 1 - y, c)], dst_ref=land.at[_slot(1 - x, 1 - y, 1 - c if waiting else c)],
        send_sem=sends.at[0], recv_sem=recvs.at[0], device_id=(x, y, 1 - c), device_id_type=MESH)]


def _pair_hop(bufs, sends, recvs, waiting):
    g, land = bufs
    x, y, c = _place()
    return [pltpu.make_async_remote_copy(
        src_ref=g.at[_slot(qx, qy, 1 - c)], dst_ref=land.at[q],
        send_sem=sends.at[q], recv_sem=recvs.at[q], device_id=(x, y, 1 - c), device_id_type=MESH)
        for q, (qx, qy) in enumerate(CHIPS)]


def _chip_hop(bufs, sends, recvs, waiting):
    p, land = bufs
    x, y, c = _place()
    return [pltpu.make_async_remote_copy(
        src_ref=p.at[_chip(px, py)], dst_ref=land.at[_chip(px, py) if waiting else _chip(x, y)],
        send_sem=sends.at[k], recv_sem=recvs.at[k], device_id=(px, py, c), device_id_type=MESH)
        for k, (px, py) in enumerate([(1 - x, y), (x, 1 - y), (1 - x, 1 - y)])]


def copies_start(name, groups, hop, n_copies, deps=()):
    flat = [b for grp in groups for b in grp]
    nb, ng = len(flat), len(groups)
    deps = list(deps)
    hops = list(hop) if isinstance(hop, (list, tuple)) else [hop] * ng
    counts = list(n_copies) if isinstance(n_copies, (list, tuple)) else [n_copies] * ng

    def body(*refs):
        ins, token = refs[:nb], refs[-1]
        sems = refs[nb + len(deps):nb + len(deps) + 2 * ng]
        i = 0
        for gi, grp in enumerate(groups):
            for cp in hops[gi](ins[i:i + len(grp)], sems[2 * gi], sems[2 * gi + 1], False):
                cp.start()
            i += len(grp)
        token[...] = jnp.zeros_like(token)

    outs = pl.pallas_call(
        body, name=name,
        out_shape=([pltpu.SemaphoreType.DMA((n,)) for n in counts for _ in range(2)]
                   + [pltpu.HBM(b.shape, b.dtype) for b in flat] + [TOKEN]),
        in_specs=[HBM] * nb + [ANY] * len(deps),
        out_specs=[SEM] * (2 * ng) + [HBM] * nb + [pl.BlockSpec(memory_space=pltpu.VMEM)],
        input_output_aliases={i: 2 * ng + i for i in range(nb)},
        compiler_params=pltpu.CompilerParams(has_side_effects=EFFECT),
    )(*[_in_hbm(b) for b in flat], *deps)
    started, i = [], 0
    for gi, grp in enumerate(groups):
        started.append((outs[2 * gi], outs[2 * gi + 1], list(outs[2 * ng + i:2 * ng + i + len(grp)])))
        i += len(grp)
    return started, outs[-1]


def copies_wait(name, started, hop, after):
    sends, recvs, bufs = started
    nb = len(bufs)

    def body(*refs):
        for cp in hop(refs[:nb], refs[nb], refs[nb + 1], True):
            cp.wait_send()
            cp.wait_recv()

    outs = pl.pallas_call(
        body, name=name,
        out_shape=[pltpu.HBM(b.shape, b.dtype) for b in bufs],
        in_specs=[HBM] * nb + [SEM, SEM, ANY], out_specs=[HBM] * nb,
        input_output_aliases={i: i for i in range(nb)},
        compiler_params=pltpu.CompilerParams(has_side_effects=EFFECT),
    )(*bufs, sends, recvs, after)
    return list(outs)


def place_shard(name, w, layer, dtype, deps=()):
    _, r, c = w.shape
    tr = _tile(r, 1024)
    x, y, core = _place()
    me = _slot(x, y, core).astype(jnp.int32).reshape(1)

    def body(me_ref, w_ref, *rest):
        rest[-1][...] = w_ref[...].astype(dtype)

    return pl.pallas_call(
        body, name=name,
        grid_spec=pltpu.PrefetchScalarGridSpec(
            num_scalar_prefetch=1, grid=(r // tr,),
            in_specs=[pl.BlockSpec((None, tr, c), lambda i, me_ref: (layer, i, 0))] + [ANY] * len(deps),
            out_specs=pl.BlockSpec((None, tr, c), lambda i, me_ref: (me_ref[0], i, 0))),
        out_shape=jax.ShapeDtypeStruct((N_DEV, r, c), dtype),
        compiler_params=_params(("parallel",)),
    )(me, w, *deps)


def tie(name, x, *deps):
    def body(*refs):
        del refs

    return pl.pallas_call(
        body, name=name, out_shape=jax.ShapeDtypeStruct(x.shape, x.dtype),
        in_specs=[ANY] * (1 + len(deps)), out_specs=ANY, input_output_aliases={0: 0},
    )(x, *deps)


def pair_add(name, g, from_sibling):
    _, r, c_dim = g.shape
    tr = r
    while tr * c_dim > PAIR_ADD_BLOCK and tr % 16 == 0:
        tr //= 2
    x, y, core = _place()
    where = jnp.stack([core, _chip(x, y)]).astype(jnp.int32)

    def body(where_ref, g_ref, s_ref, o_ref, zone_ref):
        total = (g_ref[...].astype(F32) + s_ref[...].astype(F32)).astype(o_ref.dtype)
        o_ref[...] = total

        @pl.when(pl.program_id(1) == where_ref[1])
        def _():
            zone_ref[...] = total

    blk = pl.BlockSpec((None, tr, c_dim), lambda i, q, where_ref: (q, i, 0))
    return pl.pallas_call(
        body, name=name,
        grid_spec=pltpu.PrefetchScalarGridSpec(
            num_scalar_prefetch=1, grid=(r // tr, N_CHIP),
            in_specs=[pl.BlockSpec((None, None, tr, c_dim), lambda i, q, where_ref: (q, where_ref[0], i, 0)), blk],
            out_specs=[blk, pl.BlockSpec((None, tr, c_dim), lambda i, q, where_ref: (where_ref[1], i, 0))]),
        out_shape=[jax.ShapeDtypeStruct((N_CHIP, r, c_dim), g.dtype)] * 2,
        compiler_params=_params(("parallel", "arbitrary")),
    )(where, g.reshape(N_CHIP, 2, r, c_dim), from_sibling)


def _matmul(name, lhs, rhs, *, out_shape, out_dtype, grid, lhs_spec, rhs_spec, out_spec, acc_shape,
            lhs_fn=None, epilogue=None):
    nk = grid[2]

    def body(lhs_ref, rhs_ref, out_ref, *scratch):
        def product():
            a = lhs_ref[...]
            if lhs_fn is not None:
                a = lhs_fn(a)
            return lax.dot_general(a, rhs_ref[...], NN, preferred_element_type=F32)

        def finish(r):
            if epilogue is not None:
                r = epilogue(r)
            out_ref[...] = r.astype(out_dtype)

        if nk == 1:
            finish(product())
        else:
            (acc_ref,) = scratch
            k = pl.program_id(2)

            @pl.when(k == 0)
            def _():
                acc_ref[...] = product()

            @pl.when(jnp.logical_and(k > 0, k < nk - 1))
            def _():
                acc_ref[...] += product()

            @pl.when(k == nk - 1)
            def _():
                finish(acc_ref[...] + product())

    return pl.pallas_call(
        body, name=name, grid=grid,
        out_shape=jax.ShapeDtypeStruct(out_shape, out_dtype),
        in_specs=[lhs_spec, rhs_spec], out_specs=out_spec,
        scratch_shapes=[pltpu.VMEM(acc_shape, F32)] if nk > 1 else [],
        compiler_params=_params(("parallel", "parallel", "arbitrary")),
    )(lhs, rhs)


def _tile(n, want):
    return want if n % want == 0 else n


def mm_nn(name, x, w, *, out_dtype, tn=512, tk=None, lhs_fn=None, epilogue=None):
    t, kdim = x.shape
    n = w.shape[1]
    tm, tn = _tile(t, MATMUL_ROWS), _tile(n, tn)
    tk = kdim if tk is None else _tile(kdim, tk)
    return _matmul(
        name, x, w, out_shape=(t, n), out_dtype=out_dtype, grid=(t // tm, n // tn, kdim // tk),
        lhs_spec=pl.BlockSpec((tm, tk), lambda i, j, k: (i, k)),
        rhs_spec=pl.BlockSpec((tk, tn), lambda i, j, k: (k, j)),
        out_spec=pl.BlockSpec((tm, tn), lambda i, j, k: (i, j)),
        acc_shape=(tm, tn), lhs_fn=lhs_fn, epilogue=epilogue)


def mm_nn_blocked(name, x, w, *, out_dtype, epilogue=None):
    t, kdim = x.shape
    nb = w.shape[2]
    tm = _tile(t, MATMUL_ROWS)
    tn = nb // 2 if nb >= 1024 else nb
    sub = nb // tn
    return _matmul(
        name, x, w, out_shape=(t, N_DEV * nb), out_dtype=out_dtype, grid=(t // tm, N_DEV * sub, 1),
        lhs_spec=pl.BlockSpec((tm, kdim), lambda i, j, k: (i, k)),
        rhs_spec=pl.BlockSpec((None, kdim, tn), lambda i, j, k: (j // sub, k, j % sub)),
        out_spec=pl.BlockSpec((tm, tn), lambda i, j, k: (i, j)),
        acc_shape=(tm, tn), epilogue=epilogue)


def mm_bwd_pair(name, dy, w, act, *, out_dtype, tile=512, act_fn=None, epilogue=None):
    t, n = dy.shape
    kdim = w.shape[0]
    tile = _tile(kdim, tile)

    def body(dy_ref, w_ref, act_ref, dx_ref, dw_ref):
        a = act_ref[...]
        dx = lax.dot_general(dy_ref[...], w_ref[...], NT, preferred_element_type=F32)
        if epilogue is not None:
            dx = epilogue(dx, a)
        dx_ref[...] = dx.astype(out_dtype)
        if act_fn is not None:
            a = act_fn(a)
        dw_ref[...] = lax.dot_general(a, dy_ref[...], TN, preferred_element_type=F32).astype(out_dtype)

    return pl.pallas_call(
        body, name=name, grid=(kdim // tile,),
        in_specs=[pl.BlockSpec((t, n), lambda j: (0, 0)), pl.BlockSpec((tile, n), lambda j: (j, 0)),
                  pl.BlockSpec((t, tile), lambda j: (0, j))],
        out_specs=[pl.BlockSpec((t, tile), lambda j: (0, j)), pl.BlockSpec((tile, n), lambda j: (j, 0))],
        out_shape=[jax.ShapeDtypeStruct((t, kdim), out_dtype), jax.ShapeDtypeStruct((kdim, n), out_dtype)],
        compiler_params=_params(("parallel",)),
    )(dy, w, act)


def mm_bwd_pair_blocked(name, dz, w, act, *, out_dtype, tile=1024):
    t = dz.shape[0]
    kdim, nb = w.shape[1], w.shape[2]
    tile = _tile(kdim, tile)

    def body(dz_ref, w_ref, act_ref, dx_ref, dw_ref, acc_ref):
        j = pl.program_id(1)
        dw_ref[...] = lax.dot_general(act_ref[...], dz_ref[...], TN, preferred_element_type=F32).astype(out_dtype)

        def product():
            return lax.dot_general(dz_ref[...], w_ref[...], NT, preferred_element_type=F32)

        @pl.when(j == 0)
        def _():
            acc_ref[...] = product()

        @pl.when(jnp.logical_and(j > 0, j < N_DEV - 1))
        def _():
            acc_ref[...] += product()

        @pl.when(j == N_DEV - 1)
        def _():
            dx_ref[...] = (acc_ref[...] + product()).astype(out_dtype)

    return pl.pallas_call(
        body, name=name, grid=(kdim // tile, N_DEV),
        in_specs=[pl.BlockSpec((t, nb), lambda i, j: (0, j)), pl.BlockSpec((None, tile, nb), lambda i, j: (j, i, 0)),
                  pl.BlockSpec((t, tile), lambda i, j: (0, i))],
        out_specs=[pl.BlockSpec((t, tile), lambda i, j: (0, i)),
                   pl.BlockSpec((None, tile, nb), lambda i, j: (j, i, 0))],
        out_shape=[jax.ShapeDtypeStruct((t, kdim), out_dtype), jax.ShapeDtypeStruct((N_DEV, kdim, nb), out_dtype)],
        scratch_shapes=[pltpu.VMEM((t, tile), F32)],
        compiler_params=_params(("parallel", "arbitrary")),
    )(dz, w, act)


def _rstd(v):
    return lax.rsqrt(jnp.mean(v * v, axis=-1, keepdims=True) + NORM_EPS)


def _rms_bwd(v, g, dy):
    r = _rstd(v)
    vhat = v * r
    dvh = dy * g
    dv = r * (dvh - vhat * jnp.mean(dvh * vhat, axis=-1, keepdims=True))
    return dv, dy * vhat


def _fold8(v):
    rows, n = v.shape
    return jnp.sum(v.reshape(rows // 8, 8, n), axis=0)


def _fold_lanes(v):
    out = v[:, 0:128]
    for i in range(1, v.shape[1] // 128):
        out = out + v[:, 128 * i:128 * (i + 1)]
    return out


def _accumulate(ref, v):
    i = pl.program_id(0)

    @pl.when(i == 0)
    def _():
        ref[...] = v

    @pl.when(i > 0)
    def _():
        ref[...] += v


def _row_call(body, name, t, ins, row_in, outs, acc_outs=(), tr=ROW_TILE):
    tr = _tile(t, tr)

    def in_spec(a, tiled):
        if isinstance(tiled, tuple):
            width, j = tiled
            return pl.BlockSpec((tr, width), lambda i: (i, j))
        return pl.BlockSpec((tr, a.shape[1]), lambda i: (i, 0)) if tiled else pl.BlockSpec(a.shape, lambda i: (0, 0))

    in_specs = [in_spec(a, tiled) for a, tiled in zip(ins, row_in)]
    out_specs = [pl.BlockSpec((tr, n), lambda i: (i, 0)) for n, _ in outs]
    out_specs += [pl.BlockSpec((8, n), lambda i: (0, 0)) for n in acc_outs]
    out_shape = [jax.ShapeDtypeStruct((t, n), dt) for n, dt in outs]
    out_shape += [jax.ShapeDtypeStruct((8, n), F32) for n in acc_outs]
    return pl.pallas_call(
        body, name=name, grid=(t // tr,), in_specs=in_specs, out_specs=out_specs, out_shape=out_shape,
        compiler_params=_params(("arbitrary",) if acc_outs else ("parallel",)),
    )(*ins)


def norm_pre(name, x, g):
    t, d = x.shape

    def body(x_ref, g_ref, h_ref):
        v = x_ref[...]
        h_ref[...] = (v * _rstd(v) * g_ref[...]).astype(BF16)

    return _row_call(body, name, t, [x, g], [True, False], [(d, BF16)])[0]


def post_pre(name, x, m, g_post, g_pre):
    t, d = x.shape

    def body(x_ref, m_ref, gp_ref, gn_ref, xo_ref, h_ref):
        mv = m_ref[...]
        xn = x_ref[...] + mv * _rstd(mv) * gp_ref[...]
        xo_ref[...] = xn
        h_ref[...] = (xn * _rstd(xn) * gn_ref[...]).astype(BF16)

    return _row_call(body, name, t, [x, m, g_post, g_pre], [True, True, False, False], [(d, F32), (d, BF16)])


def post_loss(name, x, f, g_post, target):
    t, d = x.shape

    def body(x_ref, f_ref, g_ref, t_ref, dx_ref, df_ref, loss_ref, dg_ref):
        fv = f_ref[...]
        g = g_ref[...]
        out = x_ref[...] + fv * _rstd(fv) * g
        err = out - t_ref[...]
        dx = err * (1.0 / d)
        dx_ref[...] = dx
        dfv, dg_rows = _rms_bwd(fv, g, dx)
        df_ref[...] = dfv.astype(BF16)
        _accumulate(loss_ref, _fold8(_fold_lanes(err * err)))
        _accumulate(dg_ref, _fold8(dg_rows))

    return _row_call(body, name, t, [x, f, g_post, target], [True, True, False, True],
                     [(d, F32), (d, BF16)], acc_outs=(128, d))


def bwd_pre_post(name, dx_out, x_in, g_pre, dh, f_prev, g_post_prev):
    t, d = x_in.shape

    def body(dxo_ref, x_ref, gpre_ref, dh_ref, f_ref, gpost_ref, dxi_ref, df_ref, dgpre_ref, dgpost_ref):
        dxv, dgpre_rows = _rms_bwd(x_ref[...], gpre_ref[...], dh_ref[...].astype(F32))
        dxi = dxo_ref[...] + dxv
        dxi_ref[...] = dxi
        dfv, dgpost_rows = _rms_bwd(f_ref[...], gpost_ref[...], dxi)
        df_ref[...] = dfv.astype(BF16)
        _accumulate(dgpre_ref, _fold8(dgpre_rows))
        _accumulate(dgpost_ref, _fold8(dgpost_rows))

    return _row_call(body, name, t, [dx_out, x_in, g_pre, dh, f_prev, g_post_prev],
                     [True, True, False, True, True, False], [(d, F32), (d, BF16)], acc_outs=(d, d))


def bwd_pre_final(name, dx_out, x_in, g_pre, dh):
    t, d = x_in.shape

    def body(dxo_ref, x_ref, gpre_ref, dh_ref, dxi_ref, dgpre_ref):
        dxv, dgpre_rows = _rms_bwd(x_ref[...], gpre_ref[...], dh_ref[...].astype(F32))
        dxi_ref[...] = dxo_ref[...] + dxv
        _accumulate(dgpre_ref, _fold8(dgpre_rows))

    return _row_call(body, name, t, [dx_out, x_in, g_pre, dh], [True, True, False, True], [(d, F32)], acc_outs=(d,))


def _layer_norm_parts(cv):
    mu = jnp.mean(cv, axis=-1, keepdims=True)
    xc = cv - mu
    rstd = lax.rsqrt(jnp.mean(xc * xc, axis=-1, keepdims=True) + NORM_EPS)
    return xc * rstd, rstd


def ln_silu(name, cv, g, b, y, y_block):
    t, n = cv.shape
    tr = _tile(t, ROW_TILE)

    def body(c_ref, g_ref, b_ref, y_in_ref, y_ref):
        chat, _ = _layer_norm_parts(c_ref[...])
        ln = chat * g_ref[...] + b_ref[...]
        y_ref[...] = (ln * jax.nn.sigmoid(ln)).astype(BF16)

    vec = pl.BlockSpec((1, n), lambda i: (0, 0))
    return pl.pallas_call(
        body, name=name, grid=(t // tr,),
        in_specs=[pl.BlockSpec((tr, n), lambda i: (i, 0)), vec, vec, ANY],
        out_specs=pl.BlockSpec((tr, n), lambda i: (i, y_block)),
        out_shape=jax.ShapeDtypeStruct(y.shape, y.dtype), input_output_aliases={3: 0},
        compiler_params=_params(("parallel",)),
    )(cv, g, b, y)


def ln_silu_bwd(name, cv, g, b, dy, dy_block):
    t, n = cv.shape

    def body(c_ref, g_ref, b_ref, dy_ref, dc_ref, dg_ref, db_ref):
        chat, rstd = _layer_norm_parts(c_ref[...])
        g = g_ref[...]
        ln = chat * g + b_ref[...]
        s = jax.nn.sigmoid(ln)
        dln = dy_ref[...].astype(F32) * (s * (1.0 + ln * (1.0 - s)))
        dchat = dln * g
        dc_ref[...] = rstd * (dchat - jnp.mean(dchat, axis=-1, keepdims=True)
                              - chat * jnp.mean(dchat * chat, axis=-1, keepdims=True))
        _accumulate(dg_ref, _fold8(dln * chat))
        _accumulate(db_ref, _fold8(dln))

    return _row_call(body, name, t, [cv, g, b, dy], [True, False, False, (n, dy_block)], [(n, F32)], acc_outs=(n, n))


def _chunks(t, fn, tc=TIME_CHUNK):
    tc = _tile(t, tc)

    def step(i, carry):
        fn(pl.multiple_of(i * tc, tc), tc)
        return carry

    lax.fori_loop(0, t // tc, step, 0)


def _rows_from(v, start, n):
    res = start % 8
    base = v if res == 0 else pltpu.roll(v, v.shape[0] - res, axis=0)
    return base[start - res:start - res + n, :]


def _shifted(window, offsets, tc):
    rows = window.shape[0]
    by_residue = {}
    for k, off in enumerate(offsets):
        by_residue.setdefault(off % 8, []).append((k, off))
    for res, taps in by_residue.items():
        base = window if res == 0 else pltpu.roll(window, rows - res, axis=0)
        for k, off in taps:
            yield k, base[off - res:off - res + tc, :]


def _taps(window, w_ref, offsets, tc, flip=False):
    acc = None
    for k, rows in _shifted(window, offsets, tc):
        kk = len(offsets) - 1 - k if flip else k
        term = w_ref[kk:kk + 1, :] * rows
        acc = term if acc is None else acc + term
    return acc


def _window_sums(win, tc, causal):
    sums = []
    cur, rows, step = win, tc + HALO, 1
    for _ in POOL_WINDOWS:
        rows -= 8
        if causal:
            cur = cur[8:8 + rows, :] + _rows_from(cur, 8 - step, rows)
            sums.append(cur[rows - tc:rows, :])
        else:
            cur = cur[0:rows, :] + _rows_from(cur, step, rows)
            sums.append(cur[0:tc, :])
        step *= 2
    return sums


def _pick(vals, g):
    out = vals[-1]
    for i in range(len(vals) - 2, -1, -1):
        out = jnp.where(g == i, vals[i], out)
    return out


def _pool_count(s, tc, g):
    t1 = (lax.broadcasted_iota(jnp.int32, (tc, 1), 0) + (s + 1)).astype(F32)
    width = _pick([float(w) for w in POOL_WINDOWS], g)
    return jnp.minimum(t1, width)


def pool_fwd(name, z, pool_w, pool_scale, d_pool, y_width):
    t = z.shape[0]
    ng, pg = pool_w.shape[0], pool_w.shape[1]

    def body(u_ref, w_ref, s_ref, pooled_ref, y_ref, pad):
        g = pl.program_id(0)
        pad[pl.ds(0, HALO), :] = jnp.zeros((HALO, pg), F32)

        def fill(s, tc):
            pad[pl.ds(HALO + s, tc), :] = u_ref[pl.ds(s, tc), :].astype(F32)

        def chunk(s, tc):
            win = pad[pl.ds(s, tc + HALO), :]
            total = _pick(_window_sums(win, tc, causal=True), g)
            pooled = total / _pool_count(s, tc, g) - win[HALO:HALO + tc, :]
            pooled_ref[pl.ds(s, tc), :] = pooled.astype(BF16)

        _chunks(t, fill)
        _chunks(t, chunk)
        mixed = jnp.dot(pooled_ref[...], w_ref[...], preferred_element_type=F32)
        y_ref[...] = (mixed * s_ref[...]).astype(BF16)

    col = pl.BlockSpec((t, pg), lambda g: (0, g))
    return pl.pallas_call(
        body, name=name, grid=(ng,),
        in_specs=[col, pl.BlockSpec((None, pg, pg), lambda g: (g, 0, 0)), pl.BlockSpec((1, pg), lambda g: (0, g))],
        out_specs=[col, col],
        out_shape=[jax.ShapeDtypeStruct((t, d_pool), BF16), jax.ShapeDtypeStruct((t, y_width), BF16)],
        scratch_shapes=[pltpu.VMEM((t + HALO, pg), F32)],
        compiler_params=_params(("parallel",)),
    )(z, pool_w, pool_scale)


def pool_bwd(name, pooled, dy, pool_w, pool_scale, dz):
    t, d_pool = pooled.shape
    ng, pg = pool_w.shape[0], pool_w.shape[1]

    def body(p_ref, dy_ref, w_ref, s_ref, dz_ref, du_ref, dw_ref, ds_ref, pad):
        g = pl.program_id(0)
        w = w_ref[...]
        dyv = dy_ref[...].astype(F32)
        mixed = jnp.dot(p_ref[...], w, preferred_element_type=F32)
        ds_ref[...] = jnp.sum(dyv * mixed, axis=0, keepdims=True)
        dmixed = (dyv * s_ref[...]).astype(BF16)
        dw_ref[...] = lax.dot_general(p_ref[...], dmixed, TN, preferred_element_type=F32)
        pad[...] = jnp.zeros((t + HALO, pg), F32)
        pad[pl.ds(0, t), :] = lax.dot_general(dmixed, w, NT, preferred_element_type=F32)

        def scale(s, tc):
            pad[pl.ds(s, tc), :] = pad[pl.ds(s, tc), :] / _pool_count(s, tc, g)

        def chunk(s, tc):
            win = pad[pl.ds(s, tc + HALO), :]
            total = _pick(_window_sums(win, tc, causal=False), g)
            du_ref[pl.ds(s, tc), :] = (total - win[0:tc, :] * _pool_count(s, tc, g)).astype(BF16)

        _chunks(t, scale)
        _chunks(t, chunk)

    col = pl.BlockSpec((t, pg), lambda g: (0, g))
    vec = pl.BlockSpec((1, pg), lambda g: (0, g))
    mat = pl.BlockSpec((None, pg, pg), lambda g: (g, 0, 0))
    return pl.pallas_call(
        body, name=name, grid=(ng,),
        in_specs=[col, col, mat, vec, ANY], out_specs=[col, mat, vec],
        out_shape=[jax.ShapeDtypeStruct(dz.shape, dz.dtype), jax.ShapeDtypeStruct((ng, pg, pg), F32),
                   jax.ShapeDtypeStruct((1, d_pool), F32)],
        input_output_aliases={4: 0},
        scratch_shapes=[pltpu.VMEM((t + HALO, pg), F32)],
        compiler_params=_params(("parallel",)),
    )(pooled, dy, pool_w, pool_scale, dz)


def conv_fwd(name, z, conv_w, conv_b, d_pool, d_conv):
    t = z.shape[0]
    kw = conv_w.shape[0]
    tc_ch = _tile(d_conv, CHANNEL_TILE)
    v0, g0 = d_pool // tc_ch, (d_pool + d_conv) // tc_ch

    def body(v_ref, g_ref, w_ref, b_ref, c_ref, pad):
        pad[pl.ds(0, HALO), :] = jnp.zeros((HALO, tc_ch), F32)

        def fill(s, tc):
            pad[pl.ds(HALO + s, tc), :] = v_ref[pl.ds(s, tc), :].astype(F32) * jax.nn.sigmoid(g_ref[pl.ds(s, tc), :].astype(F32))

        def chunk(s, tc):
            win = pad[pl.ds(s, tc + HALO), :]
            c_ref[pl.ds(s, tc), :] = _taps(win, w_ref, [HALO - (kw - 1) + k for k in range(kw)], tc) + b_ref[...]

        _chunks(t, fill)
        _chunks(t, chunk)

    return pl.pallas_call(
        body, name=name, grid=(d_conv // tc_ch,),
        in_specs=[pl.BlockSpec((t, tc_ch), lambda j: (0, v0 + j)), pl.BlockSpec((t, tc_ch), lambda j: (0, g0 + j)),
                  pl.BlockSpec((kw, tc_ch), lambda j: (0, j)), pl.BlockSpec((1, tc_ch), lambda j: (0, j))],
        out_specs=pl.BlockSpec((t, tc_ch), lambda j: (0, j)),
        out_shape=jax.ShapeDtypeStruct((t, d_conv), F32),
        scratch_shapes=[pltpu.VMEM((t + HALO, tc_ch), F32)],
        compiler_params=_params(("parallel",)),
    )(z, z, conv_w, conv_b)


def conv_bwd(name, z, dc, conv_w, d_pool, d_conv):
    t = z.shape[0]
    kw = conv_w.shape[0]
    tc_ch = _tile(d_conv, CHANNEL_TILE)
    v0, g0 = d_pool // tc_ch, (d_pool + d_conv) // tc_ch

    def body(v_ref, g_ref, dc_ref, w_ref, dz_ref, dw_ref, db_ref, pad_a, pad_dc, acc_w, acc_b, tiles, sems):
        j = pl.program_id(0)
        dv_ref, dg_ref = tiles.at[0], tiles.at[1]
        writes = [pltpu.make_async_copy(tiles.at[p], dz_ref.at[:, pl.ds((first + j) * tc_ch, tc_ch)], sems.at[p])
                  for p, first in enumerate([v0, g0])]

        def wait_writes():
            for cp in writes:
                cp.wait()

        pad_a[pl.ds(0, HALO), :] = jnp.zeros((HALO, tc_ch), F32)
        pad_dc[pl.ds(t, HALO), :] = jnp.zeros((HALO, tc_ch), F32)
        acc_w[...] = jnp.zeros_like(acc_w)
        acc_b[...] = jnp.zeros_like(acc_b)

        def fill(s, tc):
            pad_a[pl.ds(HALO + s, tc), :] = v_ref[pl.ds(s, tc), :].astype(F32) * jax.nn.sigmoid(g_ref[pl.ds(s, tc), :].astype(F32))
            pad_dc[pl.ds(s, tc), :] = dc_ref[pl.ds(s, tc), :]

        def chunk(s, tc):
            dcv = pad_dc[pl.ds(s, tc), :]
            win_a = pad_a[pl.ds(s, tc + HALO), :]
            for k, rows in _shifted(win_a, [HALO - (kw - 1) + k for k in range(kw)], tc):
                acc_w[pl.ds(8 * k, 8), :] += _fold8(dcv * rows)
            acc_b[...] += _fold8(dcv)
            da = _taps(pad_dc[pl.ds(s, tc + HALO), :], w_ref, list(range(kw)), tc, flip=True)
            vv = v_ref[pl.ds(s, tc), :].astype(F32)
            sg = jax.nn.sigmoid(g_ref[pl.ds(s, tc), :].astype(F32))
            dv_ref[pl.ds(s, tc), :] = (da * sg).astype(BF16)
            dg_ref[pl.ds(s, tc), :] = (da * vv * sg * (1.0 - sg)).astype(BF16)

        _chunks(t, fill)
        pl.when(j > 0)(wait_writes)
        _chunks(t, chunk)
        for cp in writes:
            cp.start()
        pl.when(j == n_tiles - 1)(wait_writes)
        for k in range(kw):
            dw_ref[k:k + 1, :] = jnp.sum(acc_w[pl.ds(8 * k, 8), :], axis=0, keepdims=True)
        db_ref[...] = jnp.sum(acc_b[...], axis=0, keepdims=True)

    n_tiles = d_conv // tc_ch
    return pl.pallas_call(
        body, name=name, grid=(n_tiles,),
        in_specs=[pl.BlockSpec((t, tc_ch), lambda j: (0, v0 + j)), pl.BlockSpec((t, tc_ch), lambda j: (0, g0 + j)),
                  pl.BlockSpec((t, tc_ch), lambda j: (0, j)), pl.BlockSpec((kw, tc_ch), lambda j: (0, j))],
        out_specs=[ANY, pl.BlockSpec((kw, tc_ch), lambda j: (0, j)), pl.BlockSpec((1, tc_ch), lambda j: (0, j))],
        out_shape=[jax.ShapeDtypeStruct((t, d_pool + 2 * d_conv), BF16),
                   jax.ShapeDtypeStruct((kw, d_conv), F32), jax.ShapeDtypeStruct((1, d_conv), F32)],
        scratch_shapes=[pltpu.VMEM((t + HALO, tc_ch), F32), pltpu.VMEM((t + HALO, tc_ch), F32),
                        pltpu.VMEM((8 * kw, tc_ch), F32), pltpu.VMEM((8, tc_ch), F32),
                        pltpu.VMEM((2, t, tc_ch), BF16), pltpu.SemaphoreType.DMA((2,))],
        compiler_params=_params(("arbitrary",)),
    )(z, z, dc, conv_w)


def short_fwd(name, z, conv_w, d_short):
    t = z.shape[0]
    kw = conv_w.shape[0]
    tc_ch = _tile(d_short, CHANNEL_TILE)
    nt = d_short // tc_ch

    def body(b_ref, c_ref, u_ref, w_ref, y_ref, pad):
        pad[pl.ds(0, HALO), :] = jnp.zeros((HALO, tc_ch), F32)

        def fill(s, tc):
            pad[pl.ds(HALO + s, tc), :] = c_ref[pl.ds(s, tc), :].astype(F32) * u_ref[pl.ds(s, tc), :].astype(F32)

        def chunk(s, tc):
            win = pad[pl.ds(s, tc + HALO), :]
            cq = _taps(win, w_ref, [HALO - (kw - 1) + k for k in range(kw)], tc)
            y_ref[pl.ds(s, tc), :] = (b_ref[pl.ds(s, tc), :].astype(F32) * cq).astype(BF16)

        _chunks(t, fill)
        _chunks(t, chunk)

    return pl.pallas_call(
        body, name=name, grid=(nt,),
        in_specs=[pl.BlockSpec((t, tc_ch), lambda j: (0, j)), pl.BlockSpec((t, tc_ch), lambda j: (0, nt + j)),
                  pl.BlockSpec((t, tc_ch), lambda j: (0, 2 * nt + j)), pl.BlockSpec((kw, tc_ch), lambda j: (0, j))],
        out_specs=pl.BlockSpec((t, tc_ch), lambda j: (0, j)),
        out_shape=jax.ShapeDtypeStruct((t, d_short), BF16),
        scratch_shapes=[pltpu.VMEM((t + HALO, tc_ch), F32)],
        compiler_params=_params(("parallel",)),
    )(z, z, z, conv_w)


def short_bwd(name, z, dy, conv_w, d_short):
    t = z.shape[0]
    kw = conv_w.shape[0]
    tc_ch = _tile(d_short, CHANNEL_TILE)
    nt = d_short // tc_ch

    def body(b_ref, c_ref, u_ref, dy_ref, w_ref, dz_ref, dw_ref, pad_q, pad_dcq, acc_w, tiles, sems):
        j = pl.program_id(0)
        db_ref, dcg_ref, du_ref = tiles.at[0], tiles.at[1], tiles.at[2]
        writes = [pltpu.make_async_copy(tiles.at[p], dz_ref.at[:, pl.ds((p * nt + j) * tc_ch, tc_ch)], sems.at[p])
                  for p in range(3)]

        def wait_writes():
            for cp in writes:
                cp.wait()

        pad_q[pl.ds(0, HALO), :] = jnp.zeros((HALO, tc_ch), F32)
        pad_dcq[pl.ds(t, HALO), :] = jnp.zeros((HALO, tc_ch), F32)
        acc_w[...] = jnp.zeros_like(acc_w)

        def fill(s, tc):
            rows = pl.ds(s, tc)
            pad_q[pl.ds(HALO + s, tc), :] = c_ref[rows, :].astype(F32) * u_ref[rows, :].astype(F32)
            pad_dcq[rows, :] = dy_ref[rows, :].astype(F32) * b_ref[rows, :].astype(F32)

        def chunk(s, tc):
            rows = pl.ds(s, tc)
            win_q = pad_q[pl.ds(s, tc + HALO), :]
            dcq = pad_dcq[rows, :]
            cq = None
            for k, shifted in _shifted(win_q, [HALO - (kw - 1) + k for k in range(kw)], tc):
                acc_w[pl.ds(8 * k, 8), :] += _fold8(dcq * shifted)
                term = w_ref[k:k + 1, :] * shifted
                cq = term if cq is None else cq + term
            db_ref[rows, :] = (dy_ref[rows, :].astype(F32) * cq).astype(BF16)
            dq = _taps(pad_dcq[pl.ds(s, tc + HALO), :], w_ref, list(range(kw)), tc, flip=True)
            dcg_ref[rows, :] = (dq * u_ref[rows, :].astype(F32)).astype(BF16)
            du_ref[rows, :] = (dq * c_ref[rows, :].astype(F32)).astype(BF16)

        _chunks(t, fill)
        pl.when(j > 0)(wait_writes)
        _chunks(t, chunk)
        for cp in writes:
            cp.start()
        pl.when(j == nt - 1)(wait_writes)
        for k in range(kw):
            dw_ref[k:k + 1, :] = jnp.sum(acc_w[pl.ds(8 * k, 8), :], axis=0, keepdims=True)

    zspec = [pl.BlockSpec((t, tc_ch), lambda j, o=o: (0, o * nt + j)) for o in range(3)]
    return pl.pallas_call(
        body, name=name, grid=(nt,),
        in_specs=[*zspec, pl.BlockSpec((t, tc_ch), lambda j: (0, j)), pl.BlockSpec((kw, tc_ch), lambda j: (0, j))],
        out_specs=[ANY, pl.BlockSpec((kw, tc_ch), lambda j: (0, j))],
        out_shape=[jax.ShapeDtypeStruct((t, 3 * d_short), BF16), jax.ShapeDtypeStruct((kw, d_short), F32)],
        scratch_shapes=[pltpu.VMEM((t + HALO, tc_ch), F32), pltpu.VMEM((t + HALO, tc_ch), F32),
                        pltpu.VMEM((8 * kw, tc_ch), F32), pltpu.VMEM((3, t, tc_ch), BF16),
                        pltpu.SemaphoreType.DMA((3,))],
        compiler_params=_params(("arbitrary",)),
    )(z, z, z, dy, conv_w)


def _adamw_update(w, m, v, g):
    nm = ADAM_B1 * m + (1.0 - ADAM_B1) * g
    nv = ADAM_B2 * v + (1.0 - ADAM_B2) * (g * g)
    m_hat = nm / (1.0 - ADAM_B1 ** ADAM_STEP)
    v_hat = nv / (1.0 - ADAM_B2 ** ADAM_STEP)
    return -ADAM_LR * (m_hat / (jnp.sqrt(v_hat) + ADAM_EPS) + ADAM_WD * w), nm, nv


def adamw_replicated(name, params, first_moments, second_moments, contributions, layout, scalar_at):
    n = len(params)
    n_slots = contributions.shape[0]

    def total(c_ref, row, lane, rows, lanes):
        acc = c_ref[0, row:row + rows, lane:lane + lanes]
        for slot in range(1, n_slots):
            acc = acc + c_ref[slot, row:row + rows, lane:lane + lanes]
        return acc

    def body(*refs):
        ws, ms, vs, c_ref = refs[:n], refs[n:2 * n], refs[2 * n:3 * n], refs[3 * n]
        outs = refs[3 * n + 1:]
        outs[0][...] = total(c_ref, *scalar_at, 1, 128)
        for i, (row, lane) in enumerate(layout):
            g = total(c_ref, row, lane, *params[i].shape)
            grad_ref, delta_ref, nm_ref, nv_ref = outs[1 + 4 * i:5 + 4 * i]
            grad_ref[...] = g
            delta_ref[...], nm_ref[...], nv_ref[...] = _adamw_update(ws[i][...], ms[i][...], vs[i][...], g)

    out_shape = [jax.ShapeDtypeStruct((1, 128), F32)]
    for p in params:
        out_shape += [jax.ShapeDtypeStruct(p.shape, F32)] * 4
    return pl.pallas_call(body, name=name, out_shape=out_shape)(*params, *first_moments, *second_moments, contributions)


def adamw(name, w, m, v, contributions):
    r, c = w.shape
    nc = len(contributions)
    n_slots = contributions[0].shape[0]
    tr = 256 if c <= 1024 else 128
    if any(a.shape[1] % tr for a in contributions):
        assert nc == 1
        tr = r
    tiles = [a.shape[1] // tr for a in contributions]
    first = [sum(tiles[:j]) for j in range(nc)]

    def body(w_ref, m_ref, v_ref, *rest):
        g_refs, (grad_ref, delta_ref, nm_ref, nv_ref) = rest[:nc], rest[nc:]
        i = pl.program_id(0)
        g = None
        for j, g_ref in enumerate(g_refs):
            s = g_ref[0].astype(F32)
            for slot in range(1, n_slots):
                s = s + g_ref[slot].astype(F32)
            g = s if g is None else jnp.where(i >= first[j], s, g)
        grad_ref[...] = g
        delta_ref[...], nm_ref[...], nv_ref[...] = _adamw_update(w_ref[...], m_ref[...], v_ref[...], g)

    blk = pl.BlockSpec((tr, c), lambda i: (i, 0))
    g_specs = [pl.BlockSpec((n_slots, tr, c), lambda i, j=j: (0, jnp.clip(i - first[j], 0, tiles[j] - 1), 0))
               for j in range(nc)]
    return pl.pallas_call(
        body, name=name, grid=(r // tr,),
        in_specs=[blk, blk, blk, *g_specs],
        out_specs=[blk] * 4, out_shape=[jax.ShapeDtypeStruct((r, c), F32)] * 4,
        compiler_params=_params(("parallel",)),
    )(w, m, v, *contributions)


def _pad_rows(a, rows):
    return jnp.pad(a, ((0, rows - a.shape[0]), (0, 0)))


def kernel(x, mix_pre_g, mix_post_g, ffn_pre_g, ffn_post_g, ab_w_in, pool_w, pool_scale, conv_w, conv_b, conv_ln_g, conv_ln_b, ab_w_out, sc_w_in, sc_conv_w, sc_w_out, ffn_w1, ffn_w2, loss_target, m_mix_pre_g, m_mix_post_g, m_ffn_pre_g, m_ffn_post_g, m_ab_w_in, m_pool_w, m_pool_scale, m_conv_w, m_conv_b, m_conv_ln_g, m_conv_ln_b, m_ab_w_out, m_sc_w_in, m_sc_conv_w, m_sc_w_out, m_ffn_w1, m_ffn_w2, v_mix_pre_g, v_mix_post_g, v_ffn_pre_g, v_ffn_post_g, v_ab_w_in, v_pool_w, v_pool_scale, v_conv_w, v_conv_b, v_conv_ln_g, v_conv_ln_b, v_ab_w_out, v_sc_w_in, v_sc_conv_w, v_sc_w_out, v_ffn_w1, v_ffn_w2):
    t, d = x.shape[1], x.shape[2]
    d_pool = pool_scale.shape[1]
    d_conv = conv_b.shape[1]
    d_short = d
    ng, pg = pool_w.shape[1], pool_w.shape[3]
    kw, ks = conv_w.shape[1], sc_conv_w.shape[1]
    nb_ab, nb_sc, nb_ff = ab_w_in.shape[2], sc_w_in.shape[2], ffn_w1.shape[2]

    xs = x[0]
    target = loss_target[0]

    lanes = min(128, d_conv // N_DEV)
    small_rows = [kw * (d_conv // N_DEV) // lanes, ks * (d_short // N_DEV) // lanes, ng * (pg // N_DEV) * pg // lanes]
    small_total = -(-sum(small_rows) // 8) * 8
    r0, r1, r2 = small_rows[0], small_rows[0] + small_rows[1], sum(small_rows)

    def pack_small(a_conv, a_sconv, a_pool):
        parts = [a_conv[0].reshape(-1, lanes), a_sconv[0].reshape(-1, lanes), a_pool[0].reshape(-1, lanes)]
        return _pad_rows(jnp.concatenate(parts, axis=0), small_total)

    shards = {
        "ab_in": (ab_w_in, 0, BF16), "small": (pack_small(conv_w, sc_conv_w, pool_w)[None], 0, F32),
        "ab_out": (ab_w_out, 0, BF16), "ff1_0": (ffn_w1, 0, BF16), "ff2_0": (ffn_w2, 0, BF16),
        "sc_in": (sc_w_in, 0, BF16), "sc_out": (sc_w_out, 0, BF16),
        "ff1_1": (ffn_w1, 1, BF16), "ff2_1": (ffn_w2, 1, BF16)}
    direct = ["ab_in", "small", "ab_out"]
    zones = {nm: place_shard("place_" + nm, *shards[nm]) for nm in direct}
    started, token = copies_start("gather_start", [[zones[nm]] for nm in direct], _first_hop, 4)
    started = dict(zip(direct, started))
    ring = {}
    for nm in ["ff1_0", "ff2_0"]:
        zones[nm] = place_shard("place_" + nm, *shards[nm], deps=[token])
        (ring[nm],), token = copies_start("ring_start_" + nm, [[zones[nm]]], _ring_hop1, 3, deps=[token])
    for nm in shards:
        if nm not in zones:
            zones[nm] = place_shard("place_" + nm, *shards[nm], deps=[token])

    ties = [0]

    def after(v, *deps):
        ties[0] += 1
        return tie(f"tie_{ties[0]}", v, *deps)

    def fetch_begin(nm, dep):
        (zone,) = copies_wait("gather_wait_" + nm, started[nm], _first_hop, dep)
        (hop,), tok = copies_start("forward_start_" + nm, [[zone]], _second_hop, 3)
        return hop, tok

    def fetch_end(nm, hop, dep):
        return copies_wait("forward_wait_" + nm, hop, _second_hop, dep)[0]

    def ring_step(tag, dep, second=None, first=None, third=None):
        names, groups, hops, counts = [], [], [], []
        if second is not None:
            groups.append(copies_wait("ring1_wait_" + second, ring[second], _ring_hop1, dep))
            names, hops, counts = names + [second], hops + [_ring_hop2], counts + [4]
        if first is not None:
            groups.append([zones[first]])
            names, hops, counts = names + [first], hops + [_ring_hop1], counts + [3]
        if third is not None:
            groups.append(copies_wait("ring2_wait_" + third, ring[third], _ring_hop2, dep))
            names, hops, counts = names + [third], hops + [_ring_hop3], counts + [1]
        begun, tok = copies_start("ring_start_" + tag, groups, hops, counts, deps=[dep])
        ring.update(zip(names, begun))
        return tok

    def ring_done(nm, dep):
        return copies_wait("ring3_wait_" + nm, ring[nm], _ring_hop3, dep)[0]

    relu = lambda r: jnp.maximum(r, 0.0)
    square = lambda a: a * a
    relu2_bwd = lambda r, a: r * (2.0 * a.astype(F32))

    def row(vec, l):
        return vec[l:l + 1]

    hop_small, _ = fetch_begin("small", token)
    hop_ab_in, tok = fetch_begin("ab_in", token)
    w_small = fetch_end("small", hop_small, tok)
    w_ab_in = fetch_end("ab_in", hop_ab_in, tok)
    w_conv = w_small[:, :r0].reshape(N_DEV, kw, -1).transpose(1, 0, 2).reshape(kw, d_conv)
    w_sconv = w_small[:, r0:r1].reshape(N_DEV, ks, -1).transpose(1, 0, 2).reshape(ks, d_short)
    w_pool = w_small[:, r1:r2].reshape(N_DEV, ng, -1, pg).transpose(1, 0, 2, 3).reshape(ng, pg, pg).astype(BF16)
    h0 = norm_pre("norm_pre", xs, after(row(mix_pre_g, 0), token))
    z0 = mm_nn_blocked("ab_in", h0, w_ab_in, out_dtype=BF16)
    hop, tok = fetch_begin("ab_out", z0)
    z0 = after(z0, tok)
    pooled, y0 = pool_fwd("pool_fwd", z0, w_pool, pool_scale, d_pool, d_pool + d_conv)
    cv = conv_fwd("conv_fwd", z0, w_conv, conv_b, d_pool, d_conv)
    y0 = ln_silu("ln_silu", cv, conv_ln_g, conv_ln_b, y0, d_pool // d_conv)
    w_ab_out = fetch_end("ab_out", hop, y0)
    tok = ring_step("a", w_ab_out, second="ff1_0", first="sc_in")
    y0 = after(y0, tok)
    m0 = mm_nn("ab_out", y0, w_ab_out.reshape(d_pool + d_conv, d), out_dtype=F32)
    x1, h1 = post_pre("post_pre_0", xs, m0, row(mix_post_g, 0), row(ffn_pre_g, 0))
    tok = ring_step("b", h1, second="ff2_0", first="sc_out", third="ff1_0")
    w_ff1_0 = ring_done("ff1_0", tok)
    a0 = mm_nn_blocked("ffn0_up", h1, w_ff1_0, out_dtype=BF16, epilogue=relu)
    tok = ring_step("c", a0, second="sc_in", first="ff1_1", third="ff2_0")
    w_ff2_0 = ring_done("ff2_0", tok).reshape(-1, d)
    f0 = mm_nn("ffn0_down", a0, w_ff2_0, out_dtype=F32, tk=2048, lhs_fn=square)
    tok = ring_step("d", f0, second="sc_out", first="ff2_1", third="sc_in")
    f0 = after(f0, tok)
    x2, h2 = post_pre("post_pre_1", x1, f0, row(ffn_post_g, 0), row(mix_pre_g, 1))
    w_sc_in = ring_done("sc_in", h2)
    z1 = mm_nn_blocked("sc_in", h2, w_sc_in, out_dtype=BF16)
    tok = ring_step("e", z1, second="ff1_1", third="sc_out")
    z1 = after(z1, tok)
    y1 = short_fwd("short_fwd", z1, w_sconv, d_short)
    w_sc_out = ring_done("sc_out", y1).reshape(d_short, d)
    m1 = mm_nn("sc_out", y1, w_sc_out, out_dtype=F32)
    tok = ring_step("f", m1, second="ff2_1")
    m1 = after(m1, tok)
    x3, h3 = post_pre("post_pre_2", x2, m1, row(mix_post_g, 1), row(ffn_pre_g, 1))
    tok = ring_step("g", h3, third="ff1_1")
    w_ff1_1 = ring_done("ff1_1", tok)
    a1 = mm_nn_blocked("ffn1_up", h3, w_ff1_1, out_dtype=BF16, epilogue=relu)
    tok = ring_step("h", a1, third="ff2_1")
    w_ff2_1 = ring_done("ff2_1", tok).reshape(-1, d)
    f1 = mm_nn("ffn1_down", a1, w_ff2_1, out_dtype=F32, tk=2048, lhs_fn=square)
    dx4, df1, loss_part, dg_ffn_post1 = post_loss("post_loss", x3, f1, row(ffn_post_g, 1), target)

    red = {}

    def reduce_step(dep, begin=None, middle=None):
        tags, groups, hops, counts = [], [], [], []
        if begin is not None:
            tag, g = begin
            tags, groups = tags + [tag], groups + [[g, lax.empty((N_CHIP,) + g.shape[1:], g.dtype)]]
            hops, counts = hops + [_pair_hop], counts + [N_CHIP]
        if middle is not None:
            g, from_sibling = copies_wait("pair_wait_" + middle, red[middle], _pair_hop, dep)
            tags, groups = tags + [middle], groups + [list(pair_add("pair_add_" + middle, g, from_sibling))]
            hops, counts = hops + [_chip_hop], counts + [3]
        begun, tok = copies_start("reduce_start_" + "_".join(tags), groups, hops, counts, deps=[dep])
        red.update(zip(tags, begun))
        return tok

    def reduce_end(tag, dep):
        return copies_wait("chips_wait_" + tag, red[tag], _chip_hop, dep)[1]

    dpre, dw = mm_bwd_pair("ffn1_da_dw2", df1, w_ff2_1, a1, out_dtype=BF16, act_fn=square, epilogue=relu2_bwd)
    dpre = after(dpre, reduce_step(dpre, begin=("ff2_1", dw.reshape(N_DEV, -1, d))))
    dh3, dw = mm_bwd_pair_blocked("ffn1_dh_dw1", dpre, w_ff1_1, h3, out_dtype=BF16)
    dx3, dm1, dg_ffn_pre1, dg_mix_post1 = bwd_pre_post("bwd_3", dx4, x3, row(ffn_pre_g, 1), dh3, m1, row(mix_post_g, 1))
    dm1 = after(dm1, reduce_step(dm1, begin=("ff1_1", dw), middle="ff2_1"))

    dy1, dw = mm_bwd_pair("sc_dy_dwout", dm1, w_sc_out, y1, out_dtype=BF16)
    dy1 = after(dy1, reduce_step(dy1, begin=("sc_out", dw.reshape(N_DEV, -1, d)), middle="ff1_1"))
    dz1, dw_sconv = short_bwd("short_bwd", z1, dy1, w_sconv, d_short)
    dh2, dw = mm_bwd_pair_blocked("sc_dh_dwin", dz1, w_sc_in, h2, out_dtype=BF16)
    dx2, df0, dg_mix_pre1, dg_ffn_post0 = bwd_pre_post("bwd_2", dx3, x2, row(mix_pre_g, 1), dh2, f0, row(ffn_post_g, 0))
    df0 = after(df0, reduce_step(df0, begin=("sc_in", dw), middle="sc_out"))

    dpre, dw = mm_bwd_pair("ffn0_da_dw2", df0, w_ff2_0, a0, out_dtype=BF16, act_fn=square, epilogue=relu2_bwd)
    dpre = after(dpre, reduce_step(dpre, begin=("ff2_0", dw.reshape(N_DEV, -1, d)), middle="sc_in"))
    dh1, dw = mm_bwd_pair_blocked("ffn0_dh_dw1", dpre, w_ff1_0, h1, out_dtype=BF16)
    dx1, dm0, dg_ffn_pre0, dg_mix_post0 = bwd_pre_post("bwd_1", dx2, x1, row(ffn_pre_g, 0), dh1, m0, row(mix_post_g, 0))
    dm0 = after(dm0, reduce_step(dm0, begin=("ff1_0", dw), middle="ff2_0"))

    dy0, dw = mm_bwd_pair("ab_dy_dwout", dm0, w_ab_out.reshape(d_pool + d_conv, d), y0, out_dtype=BF16)
    dy0 = after(dy0, reduce_step(dy0, begin=("ab_out", dw.reshape(N_DEV, -1, d)), middle="ff1_0"))
    dcv, dg_ln_g, dg_ln_b = ln_silu_bwd("ln_silu_bwd", cv, conv_ln_g, conv_ln_b, dy0, d_pool // d_conv)
    dz0, dw_conv, dg_conv_b = conv_bwd("conv_bwd", z0, dcv, w_conv, d_pool, d_conv)
    dz0, dw_pool, dg_pool_scale = pool_bwd("pool_bwd", pooled, dy0, w_pool, pool_scale, dz0)
    small_parts = [
        dw_conv.reshape(kw, N_DEV, -1).transpose(1, 0, 2).reshape(N_DEV, -1, lanes),
        dw_sconv.reshape(ks, N_DEV, -1).transpose(1, 0, 2).reshape(N_DEV, -1, lanes),
        dw_pool.reshape(ng, N_DEV, pg // N_DEV, pg).transpose(1, 0, 2, 3).reshape(N_DEV, -1, lanes),
    ]
    small = jnp.pad(jnp.concatenate(small_parts, axis=1), ((0, 0), (0, small_total - r2), (0, 0)))
    dz0 = after(dz0, reduce_step(dz0, begin=("small", small), middle="ab_out"))
    dh0, dw = mm_bwd_pair_blocked("ab_dh_dwin", dz0, w_ab_in, h0, out_dtype=BF16)
    dh0 = after(dh0, reduce_step(dh0, begin=("ab_in", dw), middle="small"))
    grad_x, dg_mix_pre0 = bwd_pre_final("bwd_0", dx1, xs, row(mix_pre_g, 0), dh0)
    tok = reduce_step(grad_x, middle="ab_in")

    fold = lambda a: jnp.sum(a, axis=0, keepdims=True)
    rep_rows = [fold(dg_mix_pre0), fold(dg_mix_pre1), fold(dg_mix_post0), fold(dg_mix_post1),
                fold(dg_ffn_pre0), fold(dg_ffn_pre1), fold(dg_ffn_post0), fold(dg_ffn_post1)]
    tail = jnp.concatenate([dg_pool_scale, dg_conv_b, fold(dg_ln_g), fold(dg_ln_b)], axis=1).reshape(-1, d)
    loss_row = jnp.pad(jnp.sum(loss_part).reshape(1, 1), ((0, 0), (0, d - 1)))
    rep = _pad_rows(jnp.concatenate(rep_rows + [tail, loss_row], axis=0), 16)
    (rep_hop,), tok = copies_start("rep_start", [[place_shard("place_rep", rep[None], 0, F32)]], _first_hop, 4,
                                   deps=[tok])
    rep_layout = [(0, 0), (2, 0), (4, 0), (6, 0)]
    for offset in (0, d_pool, d_pool + d_conv, d_pool + 2 * d_conv):
        rep_layout.append((len(rep_rows) + offset // d, offset % d))
    loss_at = (len(rep_rows) + tail.shape[0], 0)

    def upd(name, w, m, v, contribs):
        shape = w.shape
        flat2 = lambda a: a.reshape(-1, shape[-1])
        outs = adamw(name, flat2(w), flat2(m), flat2(v), contribs)
        return [o.reshape(shape) for o in outs]

    g_ff2 = [reduce_end("ff2_0", tok), reduce_end("ff2_1", tok)]
    o_ff2 = upd("adam_ffn_w2", ffn_w2, m_ffn_w2, v_ffn_w2, g_ff2)
    (rep_zone,) = copies_wait("rep_wait", rep_hop, _first_hop, o_ff2[0])
    (rep_hop,), _ = copies_start("rep_forward_start", [[rep_zone]], _second_hop, 3)
    g_ff1 = [reduce_end("ff1_0", o_ff2[0]), reduce_end("ff1_1", o_ff2[0])]
    o_ff1 = upd("adam_ffn_w1", ffn_w1, m_ffn_w1, v_ffn_w1, g_ff1)
    (rep_all,) = copies_wait("rep_forward_wait", rep_hop, _second_hop, o_ff1[0])
    loss_sum, *o_rep = adamw_replicated(
        "adam_replicated",
        [mix_pre_g, mix_post_g, ffn_pre_g, ffn_post_g, pool_scale, conv_b, conv_ln_g, conv_ln_b],
        [m_mix_pre_g, m_mix_post_g, m_ffn_pre_g, m_ffn_post_g, m_pool_scale, m_conv_b, m_conv_ln_g, m_conv_ln_b],
        [v_mix_pre_g, v_mix_post_g, v_ffn_pre_g, v_ffn_post_g, v_pool_scale, v_conv_b, v_conv_ln_g, v_conv_ln_b],
        rep_all, rep_layout, loss_at)
    loss = loss_sum[0, 0] * (0.5 / d)
    o_sc_out = upd("adam_sc_out", sc_w_out, m_sc_w_out, v_sc_w_out, [reduce_end("sc_out", o_ff1[0])])
    o_sc_in = upd("adam_sc_in", sc_w_in, m_sc_w_in, v_sc_w_in, [reduce_end("sc_in", o_sc_out[0])])
    o_ab_out = upd("adam_ab_out", ab_w_out, m_ab_w_out, v_ab_w_out, [reduce_end("ab_out", o_sc_in[0])])
    o_small = adamw("adam_small", pack_small(conv_w, sc_conv_w, pool_w), pack_small(m_conv_w, m_sc_conv_w, m_pool_w),
                    pack_small(v_conv_w, v_sc_conv_w, v_pool_w), [reduce_end("small", o_ab_out[0])])
    o_ab_in = upd("adam_ab_in", ab_w_in, m_ab_w_in, v_ab_w_in, [reduce_end("ab_in", o_small[0])])

    def unpack_small(o):
        return o[:r0].reshape(conv_w.shape), o[r0:r1].reshape(sc_conv_w.shape), o[r1:r2].reshape(pool_w.shape)

    results = []
    for kind in range(4):
        g_mix_pre, g_mix_post, g_ffn_pre, g_ffn_post, g_scale, g_conv_b, g_ln_g, g_ln_b = o_rep[kind::4]
        s_conv, s_sconv, s_pool = unpack_small(o_small[kind])
        results.append([
            g_mix_pre, g_mix_post, g_ffn_pre, g_ffn_post,
            o_ab_in[kind], s_pool, g_scale, s_conv, g_conv_b, g_ln_g, g_ln_b,
            o_ab_out[kind], o_sc_in[kind], s_sconv, o_sc_out[kind], o_ff1[kind], o_ff2[kind]])

    return (loss, grad_x[None], *results[0], *results[1], *results[2], *results[3])
```

```python
import jax
import jax.numpy as jnp
from jax import lax
from jax.experimental import pallas as pl
from jax.experimental.pallas import tpu as pltpu

F32 = jnp.float32
BF16 = jnp.bfloat16
MESH = pl.DeviceIdType.MESH
ANY = pl.BlockSpec(memory_space=pl.ANY)

NORM_EPS = 1e-6
POOL_WINDOWS = (2, 4, 8, 16)
ADAM_LR = 0.001
ADAM_B1 = 0.9
ADAM_B2 = 0.999
ADAM_EPS = 1e-08
ADAM_WD = 0.01
ADAM_STEP = 10

N_DEV = 8
VMEM_LIMIT = 56 * 1024 * 1024
PAIR_ADD_BLOCK = 1 << 20
MATMUL_ROWS = 2048
ROW_TILE = 256
CHANNEL_TILE = 256
TIME_CHUNK = 64
HALO = 32

NN = (((1,), (0,)), ((), ()))
NT = (((1,), (1,)), ((), ()))
TN = (((0,), (0,)), ((), ()))


def _params(sem):
    return pltpu.CompilerParams(dimension_semantics=sem, vmem_limit_bytes=VMEM_LIMIT)


def _place():
    x, y, c = lax.axis_index("x"), lax.axis_index("y"), lax.axis_index("c")
    return x, y, c


def _slot(px, py, pc):
    return 4 * px + 2 * py + pc


HBM = pl.BlockSpec(memory_space=pltpu.HBM)
SEM = pl.BlockSpec(memory_space=pltpu.SEMAPHORE)
EFFECT = pltpu.SideEffectType.DATAFLOW_SIDE_EFFECTING
TOKEN = jax.ShapeDtypeStruct((8, 128), F32)


def _in_hbm(a):
    return pltpu.with_memory_space_constraint(a, pltpu.HBM)


CHIPS = [(0, 0), (0, 1), (1, 0), (1, 1)]
N_CHIP = len(CHIPS)


def _chip(px, py):
    return 2 * px + py


def _first_hop(bufs, sends, recvs, waiting):
    (land,) = bufs
    x, y, c = _place()
    me = _slot(x, y, c)
    peers = [(x, y, 1 - c), (1 - x, y, c), (x, 1 - y, c), (1 - x, 1 - y, c)]
    return [pltpu.make_async_remote_copy(
        src_ref=land.at[me], dst_ref=land.at[_slot(*p) if waiting else me],
        send_sem=sends.at[k], recv_sem=recvs.at[k], device_id=p, device_id_type=MESH) for k, p in enumerate(peers)]


def _second_hop(bufs, sends, recvs, waiting):
    (land,) = bufs
    x, y, c = _place()
    return [pltpu.make_async_remote_copy(
        src_ref=land.at[_slot(px, py, c)], dst_ref=land.at[_slot(px, py, 1 - c if waiting else c)],
        send_sem=sends.at[k], recv_sem=recvs.at[k], device_id=(x, y, 1 - c), device_id_type=MESH)
        for k, (px, py) in enumerate([(1 - x, y), (x, 1 - y), (1 - x, 1 - y)])]


def _ring_hop1(bufs, sends, recvs, waiting):
    (land,) = bufs
    x, y, c = _place()
    me = _slot(x, y, c)
    peers = [(1 - x, y, c), (x, 1 - y, c), (x, y, 1 - c)]
    return [pltpu.make_async_remote_copy(
        src_ref=land.at[me], dst_ref=land.at[_slot(*p) if waiting else me],
        send_sem=sends.at[k], recv_sem=recvs.at[k], device_id=p, device_id_type=MESH) for k, p in enumerate(peers)]


def _ring_hop2(bufs, sends, recvs, waiting):
    (land,) = bufs
    x, y, c = _place()
    half = land.shape[1] // 2
    first, second = pl.ds(0, half), pl.ds(half, half)
    nx, ny, diag = _slot(1 - x, y, c), _slot(x, 1 - y, c), _slot(1 - x, 1 - y, c)
    plan = [
        (land.at[ny, first], land.at[diag, first], (1 - x, y, c)),
        (land.at[nx, second], land.at[diag, second], (x, 1 - y, c)),
        (land.at[nx], land.at[_slot(1 - x, y, 1 - c)], (x, y, 1 - c)),
        (land.at[ny], land.at[_slot(x, 1 - y, 1 - c)], (x, y, 1 - c))]
    return [pltpu.make_async_remote_copy(
        src_ref=src, dst_ref=mine if waiting else src, send_sem=sends.at[k], recv_sem=recvs.at[k],
        device_id=to, device_id_type=MESH) for k, (src, mine, to) in enumerate(plan)]


def _ring_hop3(bufs, sends, recvs, waiting):
    (land,) = bufs
    x, y, c = _place()
    return [pltpu.make_async_remote_copy(
        src_ref=land.at[_slot(1 - x, 1 - y, c)], dst_ref=land.at[_slot(1 - x, 1 - y, 1 - c if waiting else c)],
        send_sem=sends.at[0], recv_sem=recvs.at[0], device_id=(x, y, 1 - c), device_id_type=MESH)]


def _pair_hop(bufs, sends, recvs, waiting):
    g, land = bufs
    x, y, c = _place()
    return [pltpu.make_async_remote_copy(
        src_ref=g.at[_slot(qx, qy, 1 - c)], dst_ref=land.at[q],
        send_sem=sends.at[q], recv_sem=recvs.at[q], device_id=(x, y, 1 - c), device_id_type=MESH)
        for q, (qx, qy) in enumerate(CHIPS)]


def _chip_hop(bufs, sends, recvs, waiting):
    p, land = bufs
    x, y, c = _place()
    return [pltpu.make_async_remote_copy(
        src_ref=p.at[_chip(px, py)], dst_ref=land.at[_chip(px, py) if waiting else _chip(x, y)],
        send_sem=sends.at[k], recv_sem=recvs.at[k], device_id=(px, py, c), device_id_type=MESH)
        for k, (px, py) in enumerate([(1 - x, y), (x, 1 - y), (1 - x, 1 - y)])]


def copies_start(name, groups, hop, n_copies, deps=()):
    flat = [b for grp in groups for b in grp]
    nb, ng = len(flat), len(groups)
    deps = list(deps)
    hops = list(hop) if isinstance(hop, (list, tuple)) else [hop] * ng
    counts = list(n_copies) if isinstance(n_copies, (list, tuple)) else [n_copies] * ng

    def body(*refs):
        ins, token = refs[:nb], refs[-1]
        sems = refs[nb + len(deps):nb + len(deps) + 2 * ng]
        i = 0
        for gi, grp in enumerate(groups):
            for cp in hops[gi](ins[i:i + len(grp)], sems[2 * gi], sems[2 * gi + 1], False):
                cp.start()
            i += len(grp)
        token[...] = jnp.zeros_like(token)

    outs = pl.pallas_call(
        body, name=name,
        out_shape=([pltpu.SemaphoreType.DMA((n,)) for n in counts for _ in range(2)]
                   + [pltpu.HBM(b.shape, b.dtype) for b in flat] + [TOKEN]),
        in_specs=[HBM] * nb + [ANY] * len(deps),
        out_specs=[SEM] * (2 * ng) + [HBM] * nb + [pl.BlockSpec(memory_space=pltpu.VMEM)],
        input_output_aliases={i: 2 * ng + i for i in range(nb)},
        compiler_params=pltpu.CompilerParams(has_side_effects=EFFECT),
    )(*[_in_hbm(b) for b in flat], *deps)
    started, i = [], 0
    for gi, grp in enumerate(groups):
        started.append((outs[2 * gi], outs[2 * gi + 1], list(outs[2 * ng + i:2 * ng + i + len(grp)])))
        i += len(grp)
    return started, outs[-1]


def copies_wait(name, started, hop, after):
    sends, recvs, bufs = started
    nb = len(bufs)

    def body(*refs):
        for cp in hop(refs[:nb], refs[nb], refs[nb + 1], True):
            cp.wait_send()
            cp.wait_recv()

    outs = pl.pallas_call(
        body, name=name,
        out_shape=[pltpu.HBM(b.shape, b.dtype) for b in bufs],
        in_specs=[HBM] * nb + [SEM, SEM, ANY], out_specs=[HBM] * nb,
        input_output_aliases={i: i for i in range(nb)},
        compiler_params=pltpu.CompilerParams(has_side_effects=EFFECT),
    )(*bufs, sends, recvs, after)
    return list(outs)


def place_shard(name, w, layer, dtype, deps=()):
    _, r, c = w.shape
    tr = _tile(r, 1024)
    x, y, core = _place()
    me = _slot(x, y, core).astype(jnp.int32).reshape(1)

    def body(me_ref, w_ref, *rest):
        rest[-1][...] = w_ref[...].astype(dtype)

    return pl.pallas_call(
        body, name=name,
        grid_spec=pltpu.PrefetchScalarGridSpec(
            num_scalar_prefetch=1, grid=(r // tr,),
            in_specs=[pl.BlockSpec((None, tr, c), lambda i, me_ref: (layer, i, 0))] + [ANY] * len(deps),
            out_specs=pl.BlockSpec((None, tr, c), lambda i, me_ref: (me_ref[0], i, 0))),
        out_shape=jax.ShapeDtypeStruct((N_DEV, r, c), dtype),
        compiler_params=_params(("parallel",)),
    )(me, w, *deps)


def place_sheet(name, pieces, total_of, total_at, rows, width):
    x, y, core = _place()
    me = _slot(x, y, core).astype(jnp.int32).reshape(1)

    def body(me_ref, *refs):
        o_ref = refs[-1]
        o_ref[...] = jnp.zeros_like(o_ref)
        for ref, (_, row, lane) in zip(refs, pieces):
            o_ref[row:row + 1, lane:lane + ref.shape[1]] = jnp.sum(ref[...], axis=0, keepdims=True)
        total = jnp.sum(jnp.sum(refs[len(pieces)][...], axis=0, keepdims=True), axis=1, keepdims=True)
        o_ref[total_at[0]:total_at[0] + 1, total_at[1]:total_at[1] + 128] = jnp.broadcast_to(total, (1, 128))

    arrays = [a for a, _, _ in pieces] + [total_of]
    return pl.pallas_call(
        body, name=name,
        grid_spec=pltpu.PrefetchScalarGridSpec(
            num_scalar_prefetch=1, grid=(1,),
            in_specs=[pl.BlockSpec(a.shape, lambda i, me_ref: (0, 0)) for a in arrays],
            out_specs=pl.BlockSpec((None, rows, width), lambda i, me_ref: (me_ref[0], 0, 0))),
        out_shape=jax.ShapeDtypeStruct((N_DEV, rows, width), F32),
    )(me, *arrays)


def tie(name, x, *deps):
    def body(*refs):
        del refs

    return pl.pallas_call(
        body, name=name, out_shape=jax.ShapeDtypeStruct(x.shape, x.dtype),
        in_specs=[ANY] * (1 + len(deps)), out_specs=ANY, input_output_aliases={0: 0},
    )(x, *deps)


def pair_add(name, g, from_sibling):
    _, r, c_dim = g.shape
    tr = r
    while tr * c_dim > PAIR_ADD_BLOCK and tr % 16 == 0:
        tr //= 2
    x, y, core = _place()
    where = jnp.stack([core, _chip(x, y)]).astype(jnp.int32)

    def body(where_ref, g_ref, s_ref, o_ref, zone_ref):
        total = (g_ref[...].astype(F32) + s_ref[...].astype(F32)).astype(o_ref.dtype)
        o_ref[...] = total

        @pl.when(pl.program_id(1) == where_ref[1])
        def _():
            zone_ref[...] = total

    blk = pl.BlockSpec((None, tr, c_dim), lambda i, q, where_ref: (q, i, 0))
    return pl.pallas_call(
        body, name=name,
        grid_spec=pltpu.PrefetchScalarGridSpec(
            num_scalar_prefetch=1, grid=(r // tr, N_CHIP),
            in_specs=[pl.BlockSpec((None, None, tr, c_dim), lambda i, q, where_ref: (q, where_ref[0], i, 0)), blk],
            out_specs=[blk, pl.BlockSpec((None, tr, c_dim), lambda i, q, where_ref: (where_ref[1], i, 0))]),
        out_shape=[jax.ShapeDtypeStruct((N_CHIP, r, c_dim), g.dtype)] * 2,
        compiler_params=_params(("parallel", "arbitrary")),
    )(where, g.reshape(N_CHIP, 2, r, c_dim), from_sibling)


def _matmul(name, lhs, rhs, *, out_shape, out_dtype, grid, lhs_spec, rhs_spec, out_spec, acc_shape,
            lhs_fn=None, epilogue=None):
    nk = grid[2]

    def body(lhs_ref, rhs_ref, out_ref, *scratch):
        def product():
            a = lhs_ref[...]
            if lhs_fn is not None:
                a = lhs_fn(a)
            return lax.dot_general(a, rhs_ref[...], NN, preferred_element_type=F32)

        def finish(r):
            if epilogue is not None:
                r = epilogue(r)
            out_ref[...] = r.astype(out_dtype)

        if nk == 1:
            finish(product())
        else:
            (acc_ref,) = scratch
            k = pl.program_id(2)

            @pl.when(k == 0)
            def _():
                acc_ref[...] = product()

            @pl.when(jnp.logical_and(k > 0, k < nk - 1))
            def _():
                acc_ref[...] += product()

            @pl.when(k == nk - 1)
            def _():
                finish(acc_ref[...] + product())

    return pl.pallas_call(
        body, name=name, grid=grid,
        out_shape=jax.ShapeDtypeStruct(out_shape, out_dtype),
        in_specs=[lhs_spec, rhs_spec], out_specs=out_spec,
        scratch_shapes=[pltpu.VMEM(acc_shape, F32)] if nk > 1 else [],
        compiler_params=_params(("parallel", "parallel", "arbitrary")),
    )(lhs, rhs)


def _tile(n, want):
    return want if n % want == 0 else n


def mm_nn(name, x, w, *, out_dtype, tn=512, tk=None, lhs_fn=None, epilogue=None):
    t, kdim = x.shape
    n = w.shape[1]
    tm, tn = _tile(t, MATMUL_ROWS), _tile(n, tn)
    tk = kdim if tk is None else _tile(kdim, tk)
    return _matmul(
        name, x, w, out_shape=(t, n), out_dtype=out_dtype, grid=(t // tm, n // tn, kdim // tk),
        lhs_spec=pl.BlockSpec((tm, tk), lambda i, j, k: (i, k)),
        rhs_spec=pl.BlockSpec((tk, tn), lambda i, j, k: (k, j)),
        out_spec=pl.BlockSpec((tm, tn), lambda i, j, k: (i, j)),
        acc_shape=(tm, tn), lhs_fn=lhs_fn, epilogue=epilogue)


def mm_nn_blocked(name, x, w, *, out_dtype, epilogue=None):
    t, kdim = x.shape
    nb = w.shape[2]
    tm = _tile(t, MATMUL_ROWS)
    tn = nb // 2 if nb >= 1024 else nb
    sub = nb // tn
    return _matmul(
        name, x, w, out_shape=(t, N_DEV * nb), out_dtype=out_dtype, grid=(t // tm, N_DEV * sub, 1),
        lhs_spec=pl.BlockSpec((tm, kdim), lambda i, j, k: (i, k)),
        rhs_spec=pl.BlockSpec((None, kdim, tn), lambda i, j, k: (j // sub, k, j % sub)),
        out_spec=pl.BlockSpec((tm, tn), lambda i, j, k: (i, j)),
        acc_shape=(tm, tn), epilogue=epilogue)


def mm_bwd_pair(name, dy, w, act, *, out_dtype, tile=512, act_fn=None, epilogue=None):
    t, n = dy.shape
    kdim = w.shape[0]
    tile = _tile(kdim, tile)

    def body(dy_ref, w_ref, act_ref, dx_ref, dw_ref):
        a = act_ref[...]
        dx = lax.dot_general(dy_ref[...], w_ref[...], NT, preferred_element_type=F32)
        if epilogue is not None:
            dx = epilogue(dx, a)
        dx_ref[...] = dx.astype(out_dtype)
        if act_fn is not None:
            a = act_fn(a)
        dw_ref[...] = lax.dot_general(a, dy_ref[...], TN, preferred_element_type=F32).astype(out_dtype)

    return pl.pallas_call(
        body, name=name, grid=(kdim // tile,),
        in_specs=[pl.BlockSpec((t, n), lambda j: (0, 0)), pl.BlockSpec((tile, n), lambda j: (j, 0)),
                  pl.BlockSpec((t, tile), lambda j: (0, j))],
        out_specs=[pl.BlockSpec((t, tile), lambda j: (0, j)), pl.BlockSpec((tile, n), lambda j: (j, 0))],
        out_shape=[jax.ShapeDtypeStruct((t, kdim), out_dtype), jax.ShapeDtypeStruct((kdim, n), out_dtype)],
        compiler_params=_params(("parallel",)),
    )(dy, w, act)


def mm_bwd_pair_blocked(name, dz, w, act, *, out_dtype, tile=1024):
    t = dz.shape[0]
    kdim, nb = w.shape[1], w.shape[2]
    tile = _tile(kdim, tile)

    def body(dz_ref, w_ref, act_ref, dx_ref, dw_ref, acc_ref):
        j = pl.program_id(1)
        dw_ref[...] = lax.dot_general(act_ref[...], dz_ref[...], TN, preferred_element_type=F32).astype(out_dtype)

        def product():
            return lax.dot_general(dz_ref[...], w_ref[...], NT, preferred_element_type=F32)

        @pl.when(j == 0)
        def _():
            acc_ref[...] = product()

        @pl.when(jnp.logical_and(j > 0, j < N_DEV - 1))
        def _():
            acc_ref[...] += product()

        @pl.when(j == N_DEV - 1)
        def _():
            dx_ref[...] = (acc_ref[...] + product()).astype(out_dtype)

    return pl.pallas_call(
        body, name=name, grid=(kdim // tile, N_DEV),
        in_specs=[pl.BlockSpec((t, nb), lambda i, j: (0, j)), pl.BlockSpec((None, tile, nb), lambda i, j: (j, i, 0)),
                  pl.BlockSpec((t, tile), lambda i, j: (0, i))],
        out_specs=[pl.BlockSpec((t, tile), lambda i, j: (0, i)),
                   pl.BlockSpec((None, tile, nb), lambda i, j: (j, i, 0))],
        out_shape=[jax.ShapeDtypeStruct((t, kdim), out_dtype), jax.ShapeDtypeStruct((N_DEV, kdim, nb), out_dtype)],
        scratch_shapes=[pltpu.VMEM((t, tile), F32)],
        compiler_params=_params(("parallel", "arbitrary")),
    )(dz, w, act)


def _rstd(v):
    return lax.rsqrt(jnp.mean(v * v, axis=-1, keepdims=True) + NORM_EPS)


def _rms_bwd(v, g, dy):
    r = _rstd(v)
    vhat = v * r
    dvh = dy * g
    dv = r * (dvh - vhat * jnp.mean(dvh * vhat, axis=-1, keepdims=True))
    return dv, dy * vhat


def _fold8(v):
    rows, n = v.shape
    return jnp.sum(v.reshape(rows // 8, 8, n), axis=0)


def _fold_lanes(v):
    out = v[:, 0:128]
    for i in range(1, v.shape[1] // 128):
        out = out + v[:, 128 * i:128 * (i + 1)]
    return out


def _accumulate(ref, v):
    i = pl.program_id(0)

    @pl.when(i == 0)
    def _():
        ref[...] = v

    @pl.when(i > 0)
    def _():
        ref[...] += v


def _row_call(body, name, t, ins, row_in, outs, acc_outs=(), tr=ROW_TILE):
    tr = _tile(t, tr)

    def in_spec(a, tiled):
        if isinstance(tiled, tuple):
            width, j = tiled
            return pl.BlockSpec((tr, width), lambda i: (i, j))
        return pl.BlockSpec((tr, a.shape[1]), lambda i: (i, 0)) if tiled else pl.BlockSpec(a.shape, lambda i: (0, 0))

    in_specs = [in_spec(a, tiled) for a, tiled in zip(ins, row_in)]
    out_specs = [pl.BlockSpec((tr, n), lambda i: (i, 0)) for n, _ in outs]
    out_specs += [pl.BlockSpec((8, n), lambda i: (0, 0)) for n in acc_outs]
    out_shape = [jax.ShapeDtypeStruct((t, n), dt) for n, dt in outs]
    out_shape += [jax.ShapeDtypeStruct((8, n), F32) for n in acc_outs]
    return pl.pallas_call(
        body, name=name, grid=(t // tr,), in_specs=in_specs, out_specs=out_specs, out_shape=out_shape,
        compiler_params=_params(("arbitrary",) if acc_outs else ("parallel",)),
    )(*ins)


def norm_pre(name, x, g):
    t, d = x.shape

    def body(x_ref, g_ref, h_ref):
        v = x_ref[...]
        h_ref[...] = (v * _rstd(v) * g_ref[...]).astype(BF16)

    return _row_call(body, name, t, [x, g], [True, False], [(d, BF16)])[0]


def post_pre(name, x, m, g_post, g_pre):
    t, d = x.shape

    def body(x_ref, m_ref, gp_ref, gn_ref, xo_ref, h_ref):
        mv = m_ref[...]
        xn = x_ref[...] + mv * _rstd(mv) * gp_ref[...]
        xo_ref[...] = xn
        h_ref[...] = (xn * _rstd(xn) * gn_ref[...]).astype(BF16)

    return _row_call(body, name, t, [x, m, g_post, g_pre], [True, True, False, False], [(d, F32), (d, BF16)])


def post_loss(name, x, f, g_post, target):
    t, d = x.shape

    def body(x_ref, f_ref, g_ref, t_ref, dx_ref, df_ref, loss_ref, dg_ref):
        fv = f_ref[...]
        g = g_ref[...]
        out = x_ref[...] + fv * _rstd(fv) * g
        err = out - t_ref[...]
        dx = err * (1.0 / d)
        dx_ref[...] = dx
        dfv, dg_rows = _rms_bwd(fv, g, dx)
        df_ref[...] = dfv.astype(BF16)
        _accumulate(loss_ref, _fold8(_fold_lanes(err * err)))
        _accumulate(dg_ref, _fold8(dg_rows))

    return _row_call(body, name, t, [x, f, g_post, target], [True, True, False, True],
                     [(d, F32), (d, BF16)], acc_outs=(128, d))


def bwd_pre_post(name, dx_out, x_in, g_pre, dh, f_prev, g_post_prev):
    t, d = x_in.shape

    def body(dxo_ref, x_ref, gpre_ref, dh_ref, f_ref, gpost_ref, dxi_ref, df_ref, dgpre_ref, dgpost_ref):
        dxv, dgpre_rows = _rms_bwd(x_ref[...], gpre_ref[...], dh_ref[...].astype(F32))
        dxi = dxo_ref[...] + dxv
        dxi_ref[...] = dxi
        dfv, dgpost_rows = _rms_bwd(f_ref[...], gpost_ref[...], dxi)
        df_ref[...] = dfv.astype(BF16)
        _accumulate(dgpre_ref, _fold8(dgpre_rows))
        _accumulate(dgpost_ref, _fold8(dgpost_rows))

    return _row_call(body, name, t, [dx_out, x_in, g_pre, dh, f_prev, g_post_prev],
                     [True, True, False, True, True, False], [(d, F32), (d, BF16)], acc_outs=(d, d))


def bwd_pre_final(name, dx_out, x_in, g_pre, dh):
    t, d = x_in.shape

    def body(dxo_ref, x_ref, gpre_ref, dh_ref, dxi_ref, dgpre_ref):
        dxv, dgpre_rows = _rms_bwd(x_ref[...], gpre_ref[...], dh_ref[...].astype(F32))
        dxi_ref[...] = dxo_ref[...] + dxv
        _accumulate(dgpre_ref, _fold8(dgpre_rows))

    return _row_call(body, name, t, [dx_out, x_in, g_pre, dh], [True, True, False, True], [(d, F32)], acc_outs=(d,))


def _layer_norm_parts(cv):
    mu = jnp.mean(cv, axis=-1, keepdims=True)
    xc = cv - mu
    rstd = lax.rsqrt(jnp.mean(xc * xc, axis=-1, keepdims=True) + NORM_EPS)
    return xc * rstd, rstd


def ln_silu(name, cv, g, b, y, y_block):
    t, n = cv.shape
    tr = _tile(t, ROW_TILE)

    def body(c_ref, g_ref, b_ref, y_in_ref, y_ref):
        chat, _ = _layer_norm_parts(c_ref[...])
        ln = chat * g_ref[...] + b_ref[...]
        y_ref[...] = (ln * jax.nn.sigmoid(ln)).astype(BF16)

    vec = pl.BlockSpec((1, n), lambda i: (0, 0))
    return pl.pallas_call(
        body, name=name, grid=(t // tr,),
        in_specs=[pl.BlockSpec((tr, n), lambda i: (i, 0)), vec, vec, ANY],
        out_specs=pl.BlockSpec((tr, n), lambda i: (i, y_block)),
        out_shape=jax.ShapeDtypeStruct(y.shape, y.dtype), input_output_aliases={3: 0},
        compiler_params=_params(("parallel",)),
    )(cv, g, b, y)


def ln_silu_bwd(name, cv, g, b, dy, dy_block):
    t, n = cv.shape

    def body(c_ref, g_ref, b_ref, dy_ref, dc_ref, dg_ref, db_ref):
        chat, rstd = _layer_norm_parts(c_ref[...])
        g = g_ref[...]
        ln = chat * g + b_ref[...]
        s = jax.nn.sigmoid(ln)
        dln = dy_ref[...].astype(F32) * (s * (1.0 + ln * (1.0 - s)))
        dchat = dln * g
        dc_ref[...] = rstd * (dchat - jnp.mean(dchat, axis=-1, keepdims=True)
                              - chat * jnp.mean(dchat * chat, axis=-1, keepdims=True))
        _accumulate(dg_ref, _fold8(dln * chat))
        _accumulate(db_ref, _fold8(dln))

    return _row_call(body, name, t, [cv, g, b, dy], [True, False, False, (n, dy_block)], [(n, F32)], acc_outs=(n, n))


def _chunks(t, fn, tc=TIME_CHUNK):
    tc = _tile(t, tc)

    def step(i, carry):
        fn(pl.multiple_of(i * tc, tc), tc)
        return carry

    lax.fori_loop(0, t // tc, step, 0)


def _rows_from(v, start, n):
    res = start % 8
    base = v if res == 0 else pltpu.roll(v, v.shape[0] - res, axis=0)
    return base[start - res:start - res + n, :]


def _shifted(window, offsets, tc):
    rows = window.shape[0]
    by_residue = {}
    for k, off in enumerate(offsets):
        by_residue.setdefault(off % 8, []).append((k, off))
    for res, taps in by_residue.items():
        base = window if res == 0 else pltpu.roll(window, rows - res, axis=0)
        for k, off in taps:
            yield k, base[off - res:off - res + tc, :]


def _taps(window, w_ref, offsets, tc, flip=False):
    acc = None
    for k, rows in _shifted(window, offsets, tc):
        kk = len(offsets) - 1 - k if flip else k
        term = w_ref[kk:kk + 1, :] * rows
        acc = term if acc is None else acc + term
    return acc


def _window_sums(win, tc, causal):
    sums = []
    cur, rows, step = win, tc + HALO, 1
    for _ in POOL_WINDOWS:
        rows -= 8
        if causal:
            cur = cur[8:8 + rows, :] + _rows_from(cur, 8 - step, rows)
            sums.append(cur[rows - tc:rows, :])
        else:
            cur = cur[0:rows, :] + _rows_from(cur, step, rows)
            sums.append(cur[0:tc, :])
        step *= 2
    return sums


def _pick(vals, g):
    out = vals[-1]
    for i in range(len(vals) - 2, -1, -1):
        out = jnp.where(g == i, vals[i], out)
    return out


def _pool_count(s, tc, g):
    t1 = (lax.broadcasted_iota(jnp.int32, (tc, 1), 0) + (s + 1)).astype(F32)
    width = _pick([float(w) for w in POOL_WINDOWS], g)
    return jnp.minimum(t1, width)


def pool_fwd(name, z, pool_w, pool_scale, d_pool, y_width):
    t = z.shape[0]
    ng, pg = pool_w.shape[0], pool_w.shape[1]

    def body(u_ref, w_ref, s_ref, pooled_ref, y_ref, pad):
        g = pl.program_id(0)
        pad[pl.ds(0, HALO), :] = jnp.zeros((HALO, pg), F32)

        def fill(s, tc):
            pad[pl.ds(HALO + s, tc), :] = u_ref[pl.ds(s, tc), :].astype(F32)

        def chunk(s, tc):
            win = pad[pl.ds(s, tc + HALO), :]
            total = _pick(_window_sums(win, tc, causal=True), g)
            pooled = total / _pool_count(s, tc, g) - win[HALO:HALO + tc, :]
            pooled_ref[pl.ds(s, tc), :] = pooled.astype(BF16)

        _chunks(t, fill)
        _chunks(t, chunk)
        mixed = jnp.dot(pooled_ref[...], w_ref[...], preferred_element_type=F32)
        y_ref[...] = (mixed * s_ref[...]).astype(BF16)

    col = pl.BlockSpec((t, pg), lambda g: (0, g))
    return pl.pallas_call(
        body, name=name, grid=(ng,),
        in_specs=[col, pl.BlockSpec((None, pg, pg), lambda g: (g, 0, 0)), pl.BlockSpec((1, pg), lambda g: (0, g))],
        out_specs=[col, col],
        out_shape=[jax.ShapeDtypeStruct((t, d_pool), BF16), jax.ShapeDtypeStruct((t, y_width), BF16)],
        scratch_shapes=[pltpu.VMEM((t + HALO, pg), F32)],
        compiler_params=_params(("parallel",)),
    )(z, pool_w, pool_scale)


def pool_bwd(name, pooled, dy, pool_w, pool_scale, dz):
    t, d_pool = pooled.shape
    ng, pg = pool_w.shape[0], pool_w.shape[1]

    def body(p_ref, dy_ref, w_ref, s_ref, dz_ref, du_ref, dw_ref, ds_ref, pad):
        g = pl.program_id(0)
        w = w_ref[...]
        dyv = dy_ref[...].astype(F32)
        mixed = jnp.dot(p_ref[...], w, preferred_element_type=F32)
        ds_ref[...] = jnp.sum(dyv * mixed, axis=0, keepdims=True)
        dmixed = (dyv * s_ref[...]).astype(BF16)
        dw_ref[...] = lax.dot_general(p_ref[...], dmixed, TN, preferred_element_type=F32)
        pad[...] = jnp.zeros((t + HALO, pg), F32)
        pad[pl.ds(0, t), :] = lax.dot_general(dmixed, w, NT, preferred_element_type=F32)

        def scale(s, tc):
            pad[pl.ds(s, tc), :] = pad[pl.ds(s, tc), :] / _pool_count(s, tc, g)

        def chunk(s, tc):
            win = pad[pl.ds(s, tc + HALO), :]
            total = _pick(_window_sums(win, tc, causal=False), g)
            du_ref[pl.ds(s, tc), :] = (total - win[0:tc, :] * _pool_count(s, tc, g)).astype(BF16)

        _chunks(t, scale)
        _chunks(t, chunk)

    col = pl.BlockSpec((t, pg), lambda g: (0, g))
    vec = pl.BlockSpec((1, pg), lambda g: (0, g))
    mat = pl.BlockSpec((None, pg, pg), lambda g: (g, 0, 0))
    return pl.pallas_call(
        body, name=name, grid=(ng,),
        in_specs=[col, col, mat, vec, ANY], out_specs=[col, mat, vec],
        out_shape=[jax.ShapeDtypeStruct(dz.shape, dz.dtype), jax.ShapeDtypeStruct((ng, pg, pg), F32),
                   jax.ShapeDtypeStruct((1, d_pool), F32)],
        input_output_aliases={4: 0},
        scratch_shapes=[pltpu.VMEM((t + HALO, pg), F32)],
        compiler_params=_params(("parallel",)),
    )(pooled, dy, pool_w, pool_scale, dz)


def conv_fwd(name, z, conv_w, conv_b, d_pool, d_conv):
    t = z.shape[0]
    kw = conv_w.shape[0]
    tc_ch = _tile(d_conv, CHANNEL_TILE)
    v0, g0 = d_pool // tc_ch, (d_pool + d_conv) // tc_ch

    def body(v_ref, g_ref, w_ref, b_ref, c_ref, pad):
        pad[pl.ds(0, HALO), :] = jnp.zeros((HALO, tc_ch), F32)

        def fill(s, tc):
            pad[pl.ds(HALO + s, tc), :] = v_ref[pl.ds(s, tc), :].astype(F32) * jax.nn.sigmoid(g_ref[pl.ds(s, tc), :].astype(F32))

        def chunk(s, tc):
            win = pad[pl.ds(s, tc + HALO), :]
            c_ref[pl.ds(s, tc), :] = _taps(win, w_ref, [HALO - (kw - 1) + k for k in range(kw)], tc) + b_ref[...]

        _chunks(t, fill)
        _chunks(t, chunk)

    return pl.pallas_call(
        body, name=name, grid=(d_conv // tc_ch,),
        in_specs=[pl.BlockSpec((t, tc_ch), lambda j: (0, v0 + j)), pl.BlockSpec((t, tc_ch), lambda j: (0, g0 + j)),
                  pl.BlockSpec((kw, tc_ch), lambda j: (0, j)), pl.BlockSpec((1, tc_ch), lambda j: (0, j))],
        out_specs=pl.BlockSpec((t, tc_ch), lambda j: (0, j)),
        out_shape=jax.ShapeDtypeStruct((t, d_conv), F32),
        scratch_shapes=[pltpu.VMEM((t + HALO, tc_ch), F32)],
        compiler_params=_params(("parallel",)),
    )(z, z, conv_w, conv_b)


def conv_bwd(name, z, dc, conv_w, d_pool, d_conv):
    t = z.shape[0]
    kw = conv_w.shape[0]
    tc_ch = _tile(d_conv, CHANNEL_TILE)
    v0, g0 = d_pool // tc_ch, (d_pool + d_conv) // tc_ch

    def body(v_ref, g_ref, dc_ref, w_ref, dz_ref, dw_ref, db_ref, pad_a, pad_dc, acc_w, acc_b, tiles, sems):
        j = pl.program_id(0)
        dv_ref, dg_ref = tiles.at[0], tiles.at[1]
        writes = [pltpu.make_async_copy(tiles.at[p], dz_ref.at[:, pl.ds((first + j) * tc_ch, tc_ch)], sems.at[p])
                  for p, first in enumerate([v0, g0])]

        def wait_writes():
            for cp in writes:
                cp.wait()

        pad_a[pl.ds(0, HALO), :] = jnp.zeros((HALO, tc_ch), F32)
        pad_dc[pl.ds(t, HALO), :] = jnp.zeros((HALO, tc_ch), F32)
        acc_w[...] = jnp.zeros_like(acc_w)
        acc_b[...] = jnp.zeros_like(acc_b)

        def fill(s, tc):
            pad_a[pl.ds(HALO + s, tc), :] = v_ref[pl.ds(s, tc), :].astype(F32) * jax.nn.sigmoid(g_ref[pl.ds(s, tc), :].astype(F32))
            pad_dc[pl.ds(s, tc), :] = dc_ref[pl.ds(s, tc), :]

        def chunk(s, tc):
            dcv = pad_dc[pl.ds(s, tc), :]
            win_a = pad_a[pl.ds(s, tc + HALO), :]
            for k, rows in _shifted(win_a, [HALO - (kw - 1) + k for k in range(kw)], tc):
                acc_w[pl.ds(8 * k, 8), :] += _fold8(dcv * rows)
            acc_b[...] += _fold8(dcv)
            da = _taps(pad_dc[pl.ds(s, tc + HALO), :], w_ref, list(range(kw)), tc, flip=True)
            vv = v_ref[pl.ds(s, tc), :].astype(F32)
            sg = jax.nn.sigmoid(g_ref[pl.ds(s, tc), :].astype(F32))
            dv_ref[pl.ds(s, tc), :] = (da * sg).astype(BF16)
            dg_ref[pl.ds(s, tc), :] = (da * vv * sg * (1.0 - sg)).astype(BF16)

        _chunks(t, fill)
        pl.when(j > 0)(wait_writes)
        _chunks(t, chunk)
        for cp in writes:
            cp.start()
        pl.when(j == n_tiles - 1)(wait_writes)
        for k in range(kw):
            dw_ref[k:k + 1, :] = jnp.sum(acc_w[pl.ds(8 * k, 8), :], axis=0, keepdims=True)
        db_ref[...] = jnp.sum(acc_b[...], axis=0, keepdims=True)

    n_tiles = d_conv // tc_ch
    return pl.pallas_call(
        body, name=name, grid=(n_tiles,),
        in_specs=[pl.BlockSpec((t, tc_ch), lambda j: (0, v0 + j)), pl.BlockSpec((t, tc_ch), lambda j: (0, g0 + j)),
                  pl.BlockSpec((t, tc_ch), lambda j: (0, j)), pl.BlockSpec((kw, tc_ch), lambda j: (0, j))],
        out_specs=[ANY, pl.BlockSpec((kw, tc_ch), lambda j: (0, j)), pl.BlockSpec((1, tc_ch), lambda j: (0, j))],
        out_shape=[jax.ShapeDtypeStruct((t, d_pool + 2 * d_conv), BF16),
                   jax.ShapeDtypeStruct((kw, d_conv), F32), jax.ShapeDtypeStruct((1, d_conv), F32)],
        scratch_shapes=[pltpu.VMEM((t + HALO, tc_ch), F32), pltpu.VMEM((t + HALO, tc_ch), F32),
                        pltpu.VMEM((8 * kw, tc_ch), F32), pltpu.VMEM((8, tc_ch), F32),
                        pltpu.VMEM((2, t, tc_ch), BF16), pltpu.SemaphoreType.DMA((2,))],
        compiler_params=_params(("arbitrary",)),
    )(z, z, dc, conv_w)


def short_fwd(name, z, conv_w, d_short):
    t = z.shape[0]
    kw = conv_w.shape[0]
    tc_ch = _tile(d_short, CHANNEL_TILE)
    nt = d_short // tc_ch

    def body(b_ref, c_ref, u_ref, w_ref, y_ref, pad):
        pad[pl.ds(0, HALO), :] = jnp.zeros((HALO, tc_ch), F32)

        def fill(s, tc):
            pad[pl.ds(HALO + s, tc), :] = c_ref[pl.ds(s, tc), :].astype(F32) * u_ref[pl.ds(s, tc), :].astype(F32)

        def chunk(s, tc):
            win = pad[pl.ds(s, tc + HALO), :]
            cq = _taps(win, w_ref, [HALO - (kw - 1) + k for k in range(kw)], tc)
            y_ref[pl.ds(s, tc), :] = (b_ref[pl.ds(s, tc), :].astype(F32) * cq).astype(BF16)

        _chunks(t, fill)
        _chunks(t, chunk)

    return pl.pallas_call(
        body, name=name, grid=(nt,),
        in_specs=[pl.BlockSpec((t, tc_ch), lambda j: (0, j)), pl.BlockSpec((t, tc_ch), lambda j: (0, nt + j)),
                  pl.BlockSpec((t, tc_ch), lambda j: (0, 2 * nt + j)), pl.BlockSpec((kw, tc_ch), lambda j: (0, j))],
        out_specs=pl.BlockSpec((t, tc_ch), lambda j: (0, j)),
        out_shape=jax.ShapeDtypeStruct((t, d_short), BF16),
        scratch_shapes=[pltpu.VMEM((t + HALO, tc_ch), F32)],
        compiler_params=_params(("parallel",)),
    )(z, z, z, conv_w)


def short_bwd(name, z, dy, conv_w, d_short):
    t = z.shape[0]
    kw = conv_w.shape[0]
    tc_ch = _tile(d_short, CHANNEL_TILE)
    nt = d_short // tc_ch

    def body(b_ref, c_ref, u_ref, dy_ref, w_ref, dz_ref, dw_ref, pad_q, pad_dcq, acc_w, tiles, sems):
        j = pl.program_id(0)
        db_ref, dcg_ref, du_ref = tiles.at[0], tiles.at[1], tiles.at[2]
        writes = [pltpu.make_async_copy(tiles.at[p], dz_ref.at[:, pl.ds((p * nt + j) * tc_ch, tc_ch)], sems.at[p])
                  for p in range(3)]

        def wait_writes():
            for cp in writes:
                cp.wait()

        pad_q[pl.ds(0, HALO), :] = jnp.zeros((HALO, tc_ch), F32)
        pad_dcq[pl.ds(t, HALO), :] = jnp.zeros((HALO, tc_ch), F32)
        acc_w[...] = jnp.zeros_like(acc_w)

        def fill(s, tc):
            rows = pl.ds(s, tc)
            pad_q[pl.ds(HALO + s, tc), :] = c_ref[rows, :].astype(F32) * u_ref[rows, :].astype(F32)
            pad_dcq[rows, :] = dy_ref[rows, :].astype(F32) * b_ref[rows, :].astype(F32)

        def chunk(s, tc):
            rows = pl.ds(s, tc)
            win_q = pad_q[pl.ds(s, tc + HALO), :]
            dcq = pad_dcq[rows, :]
            cq = None
            for k, shifted in _shifted(win_q, [HALO - (kw - 1) + k for k in range(kw)], tc):
                acc_w[pl.ds(8 * k, 8), :] += _fold8(dcq * shifted)
                term = w_ref[k:k + 1, :] * shifted
                cq = term if cq is None else cq + term
            db_ref[rows, :] = (dy_ref[rows, :].astype(F32) * cq).astype(BF16)
            dq = _taps(pad_dcq[pl.ds(s, tc + HALO), :], w_ref, list(range(kw)), tc, flip=True)
            dcg_ref[rows, :] = (dq * u_ref[rows, :].astype(F32)).astype(BF16)
            du_ref[rows, :] = (dq * c_ref[rows, :].astype(F32)).astype(BF16)

        _chunks(t, fill)
        pl.when(j > 0)(wait_writes)
        _chunks(t, chunk)
        for cp in writes:
            cp.start()
        pl.when(j == nt - 1)(wait_writes)
        for k in range(kw):
            dw_ref[k:k + 1, :] = jnp.sum(acc_w[pl.ds(8 * k, 8), :], axis=0, keepdims=True)

    zspec = [pl.BlockSpec((t, tc_ch), lambda j, o=o: (0, o * nt + j)) for o in range(3)]
    return pl.pallas_call(
        body, name=name, grid=(nt,),
        in_specs=[*zspec, pl.BlockSpec((t, tc_ch), lambda j: (0, j)), pl.BlockSpec((kw, tc_ch), lambda j: (0, j))],
        out_specs=[ANY, pl.BlockSpec((kw, tc_ch), lambda j: (0, j))],
        out_shape=[jax.ShapeDtypeStruct((t, 3 * d_short), BF16), jax.ShapeDtypeStruct((kw, d_short), F32)],
        scratch_shapes=[pltpu.VMEM((t + HALO, tc_ch), F32), pltpu.VMEM((t + HALO, tc_ch), F32),
                        pltpu.VMEM((8 * kw, tc_ch), F32), pltpu.VMEM((3, t, tc_ch), BF16),
                        pltpu.SemaphoreType.DMA((3,))],
        compiler_params=_params(("arbitrary",)),
    )(z, z, z, dy, conv_w)


def _adamw_update(w, m, v, g):
    nm = ADAM_B1 * m + (1.0 - ADAM_B1) * g
    nv = ADAM_B2 * v + (1.0 - ADAM_B2) * (g * g)
    m_hat = nm / (1.0 - ADAM_B1 ** ADAM_STEP)
    v_hat = nv / (1.0 - ADAM_B2 ** ADAM_STEP)
    return -ADAM_LR * (m_hat / (jnp.sqrt(v_hat) + ADAM_EPS) + ADAM_WD * w), nm, nv


def adamw_replicated(name, params, first_moments, second_moments, contributions, layout, scalar_at):
    n = len(params)
    n_slots = contributions.shape[0]

    def total(c_ref, row, lane, rows, lanes):
        acc = c_ref[0, row:row + rows, lane:lane + lanes]
        for slot in range(1, n_slots):
            acc = acc + c_ref[slot, row:row + rows, lane:lane + lanes]
        return acc

    def body(*refs):
        ws, ms, vs, c_ref = refs[:n], refs[n:2 * n], refs[2 * n:3 * n], refs[3 * n]
        outs = refs[3 * n + 1:]
        outs[0][...] = total(c_ref, *scalar_at, 1, 128)
        for i, (row, lane) in enumerate(layout):
            g = total(c_ref, row, lane, *params[i].shape)
            grad_ref, delta_ref, nm_ref, nv_ref = outs[1 + 4 * i:5 + 4 * i]
            grad_ref[...] = g
            delta_ref[...], nm_ref[...], nv_ref[...] = _adamw_update(ws[i][...], ms[i][...], vs[i][...], g)

    out_shape = [jax.ShapeDtypeStruct((1, 128), F32)]
    for p in params:
        out_shape += [jax.ShapeDtypeStruct(p.shape, F32)] * 4
    return pl.pallas_call(body, name=name, out_shape=out_shape)(*params, *first_moments, *second_moments, contributions)


def adamw(name, w, m, v, contributions):
    r, c = w.shape
    nc = len(contributions)
    n_slots = contributions[0].shape[0]
    tr = 256 if c <= 1024 else 128
    if any(a.shape[1] % tr for a in contributions):
        assert nc == 1
        tr = r
    tiles = [a.shape[1] // tr for a in contributions]
    first = [sum(tiles[:j]) for j in range(nc)]

    def body(w_ref, m_ref, v_ref, *rest):
        g_refs, (grad_ref, delta_ref, nm_ref, nv_ref) = rest[:nc], rest[nc:]
        i = pl.program_id(0)
        g = None
        for j, g_ref in enumerate(g_refs):
            s = g_ref[0].astype(F32)
            for slot in range(1, n_slots):
                s = s + g_ref[slot].astype(F32)
            g = s if g is None else jnp.where(i >= first[j], s, g)
        grad_ref[...] = g
        delta_ref[...], nm_ref[...], nv_ref[...] = _adamw_update(w_ref[...], m_ref[...], v_ref[...], g)

    blk = pl.BlockSpec((tr, c), lambda i: (i, 0))
    g_specs = [pl.BlockSpec((n_slots, tr, c), lambda i, j=j: (0, jnp.clip(i - first[j], 0, tiles[j] - 1), 0))
               for j in range(nc)]
    return pl.pallas_call(
        body, name=name, grid=(r // tr,),
        in_specs=[blk, blk, blk, *g_specs],
        out_specs=[blk] * 4, out_shape=[jax.ShapeDtypeStruct((r, c), F32)] * 4,
        compiler_params=_params(("parallel",)),
    )(w, m, v, *contributions)


def _pad_rows(a, rows):
    return jnp.pad(a, ((0, rows - a.shape[0]), (0, 0)))


def kernel(x, mix_pre_g, mix_post_g, ffn_pre_g, ffn_post_g, ab_w_in, pool_w, pool_scale, conv_w, conv_b, conv_ln_g, conv_ln_b, ab_w_out, sc_w_in, sc_conv_w, sc_w_out, ffn_w1, ffn_w2, loss_target, m_mix_pre_g, m_mix_post_g, m_ffn_pre_g, m_ffn_post_g, m_ab_w_in, m_pool_w, m_pool_scale, m_conv_w, m_conv_b, m_conv_ln_g, m_conv_ln_b, m_ab_w_out, m_sc_w_in, m_sc_conv_w, m_sc_w_out, m_ffn_w1, m_ffn_w2, v_mix_pre_g, v_mix_post_g, v_ffn_pre_g, v_ffn_post_g, v_ab_w_in, v_pool_w, v_pool_scale, v_conv_w, v_conv_b, v_conv_ln_g, v_conv_ln_b, v_ab_w_out, v_sc_w_in, v_sc_conv_w, v_sc_w_out, v_ffn_w1, v_ffn_w2):
    t, d = x.shape[1], x.shape[2]
    d_pool = pool_scale.shape[1]
    d_conv = conv_b.shape[1]
    d_short = d
    ng, pg = pool_w.shape[1], pool_w.shape[3]
    kw, ks = conv_w.shape[1], sc_conv_w.shape[1]
    nb_ab, nb_sc, nb_ff = ab_w_in.shape[2], sc_w_in.shape[2], ffn_w1.shape[2]

    xs = x[0]
    target = loss_target[0]

    lanes = min(128, d_conv // N_DEV)
    small_rows = [kw * (d_conv // N_DEV) // lanes, ks * (d_short // N_DEV) // lanes, ng * (pg // N_DEV) * pg // lanes]
    small_total = -(-sum(small_rows) // 8) * 8
    r0, r1, r2 = small_rows[0], small_rows[0] + small_rows[1], sum(small_rows)

    def pack_small(a_conv, a_sconv, a_pool):
        parts = [a_conv[0].reshape(-1, lanes), a_sconv[0].reshape(-1, lanes), a_pool[0].reshape(-1, lanes)]
        return _pad_rows(jnp.concatenate(parts, axis=0), small_total)

    shards = {
        "ab_in": (ab_w_in, 0, BF16), "small": (pack_small(conv_w, sc_conv_w, pool_w)[None], 0, F32),
        "ab_out": (ab_w_out, 0, BF16), "ff1_0": (ffn_w1, 0, BF16), "ff2_0": (ffn_w2, 0, BF16),
        "sc_in": (sc_w_in, 0, BF16), "sc_out": (sc_w_out, 0, BF16),
        "ff1_1": (ffn_w1, 1, BF16), "ff2_1": (ffn_w2, 1, BF16)}
    direct = ["ab_in", "small", "ab_out"]
    zones = {nm: place_shard("place_" + nm, *shards[nm]) for nm in direct}
    started, token = copies_start("gather_start", [[zones[nm]] for nm in direct], _first_hop, 4)
    started = dict(zip(direct, started))
    ring = {}
    for nm in ["ff1_0", "ff2_0"]:
        zones[nm] = place_shard("place_" + nm, *shards[nm], deps=[token])
        (ring[nm],), token = copies_start("ring_start_" + nm, [[zones[nm]]], _ring_hop1, 3, deps=[token])
    for nm in shards:
        if nm not in zones:
            zones[nm] = place_shard("place_" + nm, *shards[nm], deps=[token])

    ties = [0]

    def after(v, *deps):
        ties[0] += 1
        return tie(f"tie_{ties[0]}", v, *deps)

    def fetch_begin(nm, dep):
        (zone,) = copies_wait("gather_wait_" + nm, started[nm], _first_hop, dep)
        (hop,), tok = copies_start("forward_start_" + nm, [[zone]], _second_hop, 3)
        return hop, tok

    def fetch_end(nm, hop, dep):
        return copies_wait("forward_wait_" + nm, hop, _second_hop, dep)[0]

    def ring_step(tag, dep, second=None, first=None, third=None):
        names, groups, hops, counts = [], [], [], []
        if second is not None:
            groups.append(copies_wait("ring1_wait_" + second, ring[second], _ring_hop1, dep))
            names, hops, counts = names + [second], hops + [_ring_hop2], counts + [4]
        if first is not None:
            groups.append([zones[first]])
            names, hops, counts = names + [first], hops + [_ring_hop1], counts + [3]
        if third is not None:
            groups.append(copies_wait("ring2_wait_" + third, ring[third], _ring_hop2, dep))
            names, hops, counts = names + [third], hops + [_ring_hop3], counts + [1]
        begun, tok = copies_start("ring_start_" + tag, groups, hops, counts, deps=[dep])
        ring.update(zip(names, begun))
        return tok

    def ring_done(nm, dep):
        return copies_wait("ring3_wait_" + nm, ring[nm], _ring_hop3, dep)[0]

    relu = lambda r: jnp.maximum(r, 0.0)
    square = lambda a: a * a
    relu2_bwd = lambda r, a: r * (2.0 * a.astype(F32))

    def row(vec, l):
        return vec[l:l + 1]

    hop_small, _ = fetch_begin("small", token)
    hop_ab_in, tok = fetch_begin("ab_in", token)
    w_small = fetch_end("small", hop_small, tok)
    w_ab_in = fetch_end("ab_in", hop_ab_in, tok)
    w_conv = w_small[:, :r0].reshape(N_DEV, kw, -1).transpose(1, 0, 2).reshape(kw, d_conv)
    w_sconv = w_small[:, r0:r1].reshape(N_DEV, ks, -1).transpose(1, 0, 2).reshape(ks, d_short)
    w_pool = w_small[:, r1:r2].reshape(N_DEV, ng, -1, pg).transpose(1, 0, 2, 3).reshape(ng, pg, pg).astype(BF16)
    h0 = norm_pre("norm_pre", xs, after(row(mix_pre_g, 0), token))
    z0 = mm_nn_blocked("ab_in", h0, w_ab_in, out_dtype=BF16)
    hop, tok = fetch_begin("ab_out", z0)
    z0 = after(z0, tok)
    pooled, y0 = pool_fwd("pool_fwd", z0, w_pool, pool_scale, d_pool, d_pool + d_conv)
    cv = conv_fwd("conv_fwd", z0, w_conv, conv_b, d_pool, d_conv)
    y0 = ln_silu("ln_silu", cv, conv_ln_g, conv_ln_b, y0, d_pool // d_conv)
    w_ab_out = fetch_end("ab_out", hop, y0)
    tok = ring_step("a", w_ab_out, second="ff1_0", first="sc_in")
    y0 = after(y0, tok)
    m0 = mm_nn("ab_out", y0, w_ab_out.reshape(d_pool + d_conv, d), out_dtype=F32)
    x1, h1 = post_pre("post_pre_0", xs, m0, row(mix_post_g, 0), row(ffn_pre_g, 0))
    tok = ring_step("b", h1, second="ff2_0", first="sc_out", third="ff1_0")
    w_ff1_0 = ring_done("ff1_0", tok)
    a0 = mm_nn_blocked("ffn0_up", h1, w_ff1_0, out_dtype=BF16, epilogue=relu)
    tok = ring_step("c", a0, second="sc_in", first="ff1_1", third="ff2_0")
    w_ff2_0 = ring_done("ff2_0", tok).reshape(-1, d)
    f0 = mm_nn("ffn0_down", a0, w_ff2_0, out_dtype=F32, tk=2048, lhs_fn=square)
    tok = ring_step("d", f0, second="sc_out", first="ff2_1", third="sc_in")
    f0 = after(f0, tok)
    x2, h2 = post_pre("post_pre_1", x1, f0, row(ffn_post_g, 0), row(mix_pre_g, 1))
    w_sc_in = ring_done("sc_in", h2)
    z1 = mm_nn_blocked("sc_in", h2, w_sc_in, out_dtype=BF16)
    tok = ring_step("e", z1, second="ff1_1", third="sc_out")
    z1 = after(z1, tok)
    y1 = short_fwd("short_fwd", z1, w_sconv, d_short)
    w_sc_out = ring_done("sc_out", y1).reshape(d_short, d)
    m1 = mm_nn("sc_out", y1, w_sc_out, out_dtype=F32)
    tok = ring_step("f", m1, second="ff2_1")
    m1 = after(m1, tok)
    x3, h3 = post_pre("post_pre_2", x2, m1, row(mix_post_g, 1), row(ffn_pre_g, 1))
    tok = ring_step("g", h3, third="ff1_1")
    w_ff1_1 = ring_done("ff1_1", tok)
    a1 = mm_nn_blocked("ffn1_up", h3, w_ff1_1, out_dtype=BF16, epilogue=relu)
    tok = ring_step("h", a1, third="ff2_1")
    w_ff2_1 = ring_done("ff2_1", tok).reshape(-1, d)
    f1 = mm_nn("ffn1_down", a1, w_ff2_1, out_dtype=F32, tk=2048, lhs_fn=square)
    dx4, df1, loss_part, dg_ffn_post1 = post_loss("post_loss", x3, f1, row(ffn_post_g, 1), target)

    red = {}

    def reduce_step(dep, begin=None, middle=None):
        tags, groups, hops, counts = [], [], [], []
        if begin is not None:
            tag, g = begin
            tags, groups = tags + [tag], groups + [[g, lax.empty((N_CHIP,) + g.shape[1:], g.dtype)]]
            hops, counts = hops + [_pair_hop], counts + [N_CHIP]
        if middle is not None:
            g, from_sibling = copies_wait("pair_wait_" + middle, red[middle], _pair_hop, dep)
            tags, groups = tags + [middle], groups + [list(pair_add("pair_add_" + middle, g, from_sibling))]
            hops, counts = hops + [_chip_hop], counts + [3]
        begun, tok = copies_start("reduce_start_" + "_".join(tags), groups, hops, counts, deps=[dep])
        red.update(zip(tags, begun))
        return tok

    def reduce_end(tag, dep):
        return copies_wait("chips_wait_" + tag, red[tag], _chip_hop, dep)[1]

    dpre, dw = mm_bwd_pair("ffn1_da_dw2", df1, w_ff2_1, a1, out_dtype=BF16, act_fn=square, epilogue=relu2_bwd)
    dpre = after(dpre, reduce_step(dpre, begin=("ff2_1", dw.reshape(N_DEV, -1, d))))
    dh3, dw = mm_bwd_pair_blocked("ffn1_dh_dw1", dpre, w_ff1_1, h3, out_dtype=BF16)
    dx3, dm1, dg_ffn_pre1, dg_mix_post1 = bwd_pre_post("bwd_3", dx4, x3, row(ffn_pre_g, 1), dh3, m1, row(mix_post_g, 1))
    dm1 = after(dm1, reduce_step(dm1, begin=("ff1_1", dw), middle="ff2_1"))

    dy1, dw = mm_bwd_pair("sc_dy_dwout", dm1, w_sc_out, y1, out_dtype=BF16)
    dy1 = after(dy1, reduce_step(dy1, begin=("sc_out", dw.reshape(N_DEV, -1, d)), middle="ff1_1"))
    dz1, dw_sconv = short_bwd("short_bwd", z1, dy1, w_sconv, d_short)
    dh2, dw = mm_bwd_pair_blocked("sc_dh_dwin", dz1, w_sc_in, h2, out_dtype=BF16)
    dx2, df0, dg_mix_pre1, dg_ffn_post0 = bwd_pre_post("bwd_2", dx3, x2, row(mix_pre_g, 1), dh2, f0, row(ffn_post_g, 0))
    df0 = after(df0, reduce_step(df0, begin=("sc_in", dw), middle="sc_out"))

    dpre, dw = mm_bwd_pair("ffn0_da_dw2", df0, w_ff2_0, a0, out_dtype=BF16, act_fn=square, epilogue=relu2_bwd)
    dpre = after(dpre, reduce_step(dpre, begin=("ff2_0", dw.reshape(N_DEV, -1, d)), middle="sc_in"))
    dh1, dw = mm_bwd_pair_blocked("ffn0_dh_dw1", dpre, w_ff1_0, h1, out_dtype=BF16)
    dx1, dm0, dg_ffn_pre0, dg_mix_post0 = bwd_pre_post("bwd_1", dx2, x1, row(ffn_pre_g, 0), dh1, m0, row(mix_post_g, 0))
    dm0 = after(dm0, reduce_step(dm0, begin=("ff1_0", dw), middle="ff2_0"))

    dy0, dw = mm_bwd_pair("ab_dy_dwout", dm0, w_ab_out.reshape(d_pool + d_conv, d), y0, out_dtype=BF16)
    dy0 = after(dy0, reduce_step(dy0, begin=("ab_out", dw.reshape(N_DEV, -1, d)), middle="ff1_0"))
    dcv, dg_ln_g, dg_ln_b = ln_silu_bwd("ln_silu_bwd", cv, conv_ln_g, conv_ln_b, dy0, d_pool // d_conv)
    dz0, dw_conv, dg_conv_b = conv_bwd("conv_bwd", z0, dcv, w_conv, d_pool, d_conv)
    dz0, dw_pool, dg_pool_scale = pool_bwd("pool_bwd", pooled, dy0, w_pool, pool_scale, dz0)
    small_parts = [
        dw_conv.reshape(kw, N_DEV, -1).transpose(1, 0, 2).reshape(N_DEV, -1, lanes),
        dw_sconv.reshape(ks, N_DEV, -1).transpose(1, 0, 2).reshape(N_DEV, -1, lanes),
        dw_pool.reshape(ng, N_DEV, pg // N_DEV, pg).transpose(1, 0, 2, 3).reshape(N_DEV, -1, lanes),
    ]
    small = jnp.pad(jnp.concatenate(small_parts, axis=1), ((0, 0), (0, small_total - r2), (0, 0)))
    dz0 = after(dz0, reduce_step(dz0, begin=("small", small), middle="ab_out"))
    dh0, dw = mm_bwd_pair_blocked("ab_dh_dwin", dz0, w_ab_in, h0, out_dtype=BF16)
    dh0 = after(dh0, reduce_step(dh0, begin=("ab_in", dw), middle="small"))
    grad_x, dg_mix_pre0 = bwd_pre_final("bwd_0", dx1, xs, row(mix_pre_g, 0), dh0)
    tok = reduce_step(grad_x, middle="ab_in")

    gains = [dg_mix_pre0, dg_mix_pre1, dg_mix_post0, dg_mix_post1, dg_ffn_pre0, dg_ffn_pre1, dg_ffn_post0, dg_ffn_post1]
    pieces = [(g, i, 0) for i, g in enumerate(gains)]
    rep_layout = [(0, 0), (2, 0), (4, 0), (6, 0)]
    offset = 0
    for g in (dg_pool_scale, dg_conv_b, dg_ln_g, dg_ln_b):
        at = (len(gains) + offset // d, offset % d)
        pieces.append((g, *at))
        rep_layout.append(at)
        offset += g.shape[1]
    loss_at = (len(gains) + -(-offset // d), 0)
    rep_zone = place_sheet("place_rep", pieces, loss_part, loss_at, 16, d)
    (rep_hop,), tok = copies_start("rep_start", [[rep_zone]], _first_hop, 4,
                                   deps=[tok])

    def upd(name, w, m, v, contribs):
        shape = w.shape
        flat2 = lambda a: a.reshape(-1, shape[-1])
        outs = adamw(name, flat2(w), flat2(m), flat2(v), contribs)
        return [o.reshape(shape) for o in outs]

    g_ff2 = [reduce_end("ff2_0", tok), reduce_end("ff2_1", tok)]
    o_ff2 = upd("adam_ffn_w2", ffn_w2, m_ffn_w2, v_ffn_w2, g_ff2)
    (rep_zone,) = copies_wait("rep_wait", rep_hop, _first_hop, o_ff2[0])
    (rep_hop,), _ = copies_start("rep_forward_start", [[rep_zone]], _second_hop, 3)
    g_ff1 = [reduce_end("ff1_0", o_ff2[0]), reduce_end("ff1_1", o_ff2[0])]
    o_ff1 = upd("adam_ffn_w1", ffn_w1, m_ffn_w1, v_ffn_w1, g_ff1)
    (rep_all,) = copies_wait("rep_forward_wait", rep_hop, _second_hop, o_ff1[0])
    loss_sum, *o_rep = adamw_replicated(
        "adam_replicated",
        [mix_pre_g, mix_post_g, ffn_pre_g, ffn_post_g, pool_scale, conv_b, conv_ln_g, conv_ln_b],
        [m_mix_pre_g, m_mix_post_g, m_ffn_pre_g, m_ffn_post_g, m_pool_scale, m_conv_b, m_conv_ln_g, m_conv_ln_b],
        [v_mix_pre_g, v_mix_post_g, v_ffn_pre_g, v_ffn_post_g, v_pool_scale, v_conv_b, v_conv_ln_g, v_conv_ln_b],
        rep_all, rep_layout, loss_at)
    loss = loss_sum[0, 0] * (0.5 / d)
    o_sc_out = upd("adam_sc_out", sc_w_out, m_sc_w_out, v_sc_w_out, [reduce_end("sc_out", o_ff1[0])])
    o_sc_in = upd("adam_sc_in", sc_w_in, m_sc_w_in, v_sc_w_in, [reduce_end("sc_in", o_sc_out[0])])
    o_ab_out = upd("adam_ab_out", ab_w_out, m_ab_w_out, v_ab_w_out, [reduce_end("ab_out", o_sc_in[0])])
    o_small = adamw("adam_small", pack_small(conv_w, sc_conv_w, pool_w), pack_small(m_conv_w, m_sc_conv_w, m_pool_w),
                    pack_small(v_conv_w, v_sc_conv_w, v_pool_w), [reduce_end("small", o_ab_out[0])])
    o_ab_in = upd("adam_ab_in", ab_w_in, m_ab_w_in, v_ab_w_in, [reduce_end("ab_in", o_small[0])])

    def unpack_small(o):
        return o[:r0].reshape(conv_w.shape), o[r0:r1].reshape(sc_conv_w.shape), o[r1:r2].reshape(pool_w.shape)

    results = []
    for kind in range(4):
        g_mix_pre, g_mix_post, g_ffn_pre, g_ffn_post, g_scale, g_conv_b, g_ln_g, g_ln_b = o_rep[kind::4]
        s_conv, s_sconv, s_pool = unpack_small(o_small[kind])
        results.append([
            g_mix_pre, g_mix_post, g_ffn_pre, g_ffn_post,
            o_ab_in[kind], s_pool, g_scale, s_conv, g_conv_b, g_ln_g, g_ln_b,
            o_ab_out[kind], o_sc_in[kind], s_sconv, o_sc_out[kind], o_ff1[kind], o_ff2[kind]])

    return (loss, grad_x[None], *results[0], *results[1], *results[2], *results[3])
```

```python
import jax
import jax.numpy as jnp
from jax import lax
from jax.experimental import pallas as pl
from jax.experimental.pallas import tpu as pltpu

F32 = jnp.float32
BF16 = jnp.bfloat16
MESH = pl.DeviceIdType.MESH
ANY = pl.BlockSpec(memory_space=pl.ANY)

NORM_EPS = 1e-6
POOL_WINDOWS = (2, 4, 8, 16)
ADAM_LR = 0.001
ADAM_B1 = 0.9
ADAM_B2 = 0.999
ADAM_EPS = 1e-08
ADAM_WD = 0.01
ADAM_STEP = 10

N_DEV = 8
VMEM_LIMIT = 56 * 1024 * 1024
PAIR_ADD_BLOCK = 1 << 20
MATMUL_ROWS = 2048
ROW_TILE = 256
CHANNEL_TILE = 256
TIME_CHUNK = 64
HALO = 32

NN = (((1,), (0,)), ((), ()))
NT = (((1,), (1,)), ((), ()))
TN = (((0,), (0,)), ((), ()))


def _params(sem):
    return pltpu.CompilerParams(dimension_semantics=sem, vmem_limit_bytes=VMEM_LIMIT)


def _place():
    x, y, c = lax.axis_index("x"), lax.axis_index("y"), lax.axis_index("c")
    return x, y, c


def _slot(px, py, pc):
    return 4 * px + 2 * py + pc


HBM = pl.BlockSpec(memory_space=pltpu.HBM)
SEM = pl.BlockSpec(memory_space=pltpu.SEMAPHORE)
EFFECT = pltpu.SideEffectType.DATAFLOW_SIDE_EFFECTING
TOKEN = jax.ShapeDtypeStruct((8, 128), F32)


def _in_hbm(a):
    return pltpu.with_memory_space_constraint(a, pltpu.HBM)


CHIPS = [(0, 0), (0, 1), (1, 0), (1, 1)]
N_CHIP = len(CHIPS)


def _chip(px, py):
    return 2 * px + py


def _first_hop(bufs, sends, recvs, waiting):
    (land,) = bufs
    x, y, c = _place()
    me = _slot(x, y, c)
    peers = [(x, y, 1 - c), (1 - x, y, c), (x, 1 - y, c), (1 - x, 1 - y, c)]
    return [pltpu.make_async_remote_copy(
        src_ref=land.at[me], dst_ref=land.at[_slot(*p) if waiting else me],
        send_sem=sends.at[k], recv_sem=recvs.at[k], device_id=p, device_id_type=MESH) for k, p in enumerate(peers)]


def _second_hop(bufs, sends, recvs, waiting):
    (land,) = bufs
    x, y, c = _place()
    return [pltpu.make_async_remote_copy(
        src_ref=land.at[_slot(px, py, c)], dst_ref=land.at[_slot(px, py, 1 - c if waiting else c)],
        send_sem=sends.at[k], recv_sem=recvs.at[k], device_id=(x, y, 1 - c), device_id_type=MESH)
        for k, (px, py) in enumerate([(1 - x, y), (x, 1 - y), (1 - x, 1 - y)])]


def _ring_hop1(bufs, sends, recvs, waiting):
    (land,) = bufs
    x, y, c = _place()
    me = _slot(x, y, c)
    peers = [(1 - x, y, c), (x, 1 - y, c), (x, y, 1 - c)]
    return [pltpu.make_async_remote_copy(
        src_ref=land.at[me], dst_ref=land.at[_slot(*p) if waiting else me],
        send_sem=sends.at[k], recv_sem=recvs.at[k], device_id=p, device_id_type=MESH) for k, p in enumerate(peers)]


def _ring_hop2(bufs, sends, recvs, waiting):
    (land,) = bufs
    x, y, c = _place()
    half = land.shape[1] // 2
    first, second = pl.ds(0, half), pl.ds(half, half)
    nx, ny, diag = _slot(1 - x, y, c), _slot(x, 1 - y, c), _slot(1 - x, 1 - y, c)
    plan = [
        (land.at[ny, first], land.at[diag, first], (1 - x, y, c)),
        (land.at[nx, second], land.at[diag, second], (x, 1 - y, c)),
        (land.at[nx], land.at[_slot(1 - x, y, 1 - c)], (x, y, 1 - c)),
        (land.at[ny], land.at[_slot(x, 1 - y, 1 - c)], (x, y, 1 - c))]
    return [pltpu.make_async_remote_copy(
        src_ref=src, dst_ref=mine if waiting else src, send_sem=sends.at[k], recv_sem=recvs.at[k],
        device_id=to, device_id_type=MESH) for k, (src, mine, to) in enumerate(plan)]


def _ring_hop3(bufs, sends, recvs, waiting):
    (land,) = bufs
    x, y, c = _place()
    return [pltpu.make_async_remote_copy(
        src_ref=land.at[_slot(1 - x, 1 - y, c)], dst_ref=land.at[_slot(1 - x, 1 - y, 1 - c if waiting else c)],
        send_sem=sends.at[0], recv_sem=recvs.at[0], device_id=(x, y, 1 - c), device_id_type=MESH)]


def _pair_hop(bufs, sends, recvs, waiting):
    g, land = bufs
    x, y, c = _place()
    return [pltpu.make_async_remote_copy(
        src_ref=g.at[_slot(qx, qy, 1 - c)], dst_ref=land.at[q],
        send_sem=sends.at[q], recv_sem=recvs.at[q], device_id=(x, y, 1 - c), device_id_type=MESH)
        for q, (qx, qy) in enumerate(CHIPS)]


def _chip_hop(bufs, sends, recvs, waiting):
    p, land = bufs
    x, y, c = _place()
    return [pltpu.make_async_remote_copy(
        src_ref=p.at[_chip(px, py)], dst_ref=land.at[_chip(px, py) if waiting else _chip(x, y)],
        send_sem=sends.at[k], recv_sem=recvs.at[k], device_id=(px, py, c), device_id_type=MESH)
        for k, (px, py) in enumerate([(1 - x, y), (x, 1 - y), (1 - x, 1 - y)])]


def copies_start(name, groups, hop, n_copies, deps=()):
    flat = [b for grp in groups for b in grp]
    nb, ng = len(flat), len(groups)
    deps = list(deps)
    hops = list(hop) if isinstance(hop, (list, tuple)) else [hop] * ng
    counts = list(n_copies) if isinstance(n_copies, (list, tuple)) else [n_copies] * ng

    def body(*refs):
        ins, token = refs[:nb], refs[-1]
        sems = refs[nb + len(deps):nb + len(deps) + 2 * ng]
        i = 0
        for gi, grp in enumerate(groups):
            for cp in hops[gi](ins[i:i + len(grp)], sems[2 * gi], sems[2 * gi + 1], False):
                cp.start()
            i += len(grp)
        token[...] = jnp.zeros_like(token)

    outs = pl.pallas_call(
        body, name=name,
        out_shape=([pltpu.SemaphoreType.DMA((n,)) for n in counts for _ in range(2)]
                   + [pltpu.HBM(b.shape, b.dtype) for b in flat] + [TOKEN]),
        in_specs=[HBM] * nb + [ANY] * len(deps),
        out_specs=[SEM] * (2 * ng) + [HBM] * nb + [pl.BlockSpec(memory_space=pltpu.VMEM)],
        input_output_aliases={i: 2 * ng + i for i in range(nb)},
        compiler_params=pltpu.CompilerParams(has_side_effects=EFFECT),
    )(*[_in_hbm(b) for b in flat], *deps)
    started, i = [], 0
    for gi, grp in enumerate(groups):
        started.append((outs[2 * gi], outs[2 * gi + 1], list(outs[2 * ng + i:2 * ng + i + len(grp)])))
        i += len(grp)
    return started, outs[-1]


def copies_wait(name, started, hop, after):
    sends, recvs, bufs = started
    nb = len(bufs)

    def body(*refs):
        for cp in hop(refs[:nb], refs[nb], refs[nb + 1], True):
            cp.wait_send()
            cp.wait_recv()

    outs = pl.pallas_call(
        body, name=name,
        out_shape=[pltpu.HBM(b.shape, b.dtype) for b in bufs],
        in_specs=[HBM] * nb + [SEM, SEM, ANY], out_specs=[HBM] * nb,
        input_output_aliases={i: i for i in range(nb)},
        compiler_params=pltpu.CompilerParams(has_side_effects=EFFECT),
    )(*bufs, sends, recvs, after)
    return list(outs)


def place_shard(name, w, layer, dtype, deps=()):
    _, r, c = w.shape
    tr = _tile(r, 1024)
    x, y, core = _place()
    me = _slot(x, y, core).astype(jnp.int32).reshape(1)

    def body(me_ref, w_ref, *rest):
        rest[-1][...] = w_ref[...].astype(dtype)

    return pl.pallas_call(
        body, name=name,
        grid_spec=pltpu.PrefetchScalarGridSpec(
            num_scalar_prefetch=1, grid=(r // tr,),
            in_specs=[pl.BlockSpec((None, tr, c), lambda i, me_ref: (layer, i, 0))] + [ANY] * len(deps),
            out_specs=pl.BlockSpec((None, tr, c), lambda i, me_ref: (me_ref[0], i, 0))),
        out_shape=jax.ShapeDtypeStruct((N_DEV, r, c), dtype),
        compiler_params=_params(("parallel",)),
    )(me, w, *deps)


def place_sheet(name, pieces, total_of, total_at, rows, width):
    x, y, core = _place()
    me = _slot(x, y, core).astype(jnp.int32).reshape(1)

    def body(me_ref, *refs):
        o_ref = refs[-1]
        o_ref[...] = jnp.zeros_like(o_ref)
        for ref, (_, row, lane) in zip(refs, pieces):
            o_ref[row:row + 1, lane:lane + ref.shape[1]] = jnp.sum(ref[...], axis=0, keepdims=True)
        total = jnp.sum(jnp.sum(refs[len(pieces)][...], axis=0, keepdims=True), axis=1, keepdims=True)
        o_ref[total_at[0]:total_at[0] + 1, total_at[1]:total_at[1] + 128] = jnp.broadcast_to(total, (1, 128))

    arrays = [a for a, _, _ in pieces] + [total_of]
    return pl.pallas_call(
        body, name=name,
        grid_spec=pltpu.PrefetchScalarGridSpec(
            num_scalar_prefetch=1, grid=(1,),
            in_specs=[pl.BlockSpec(a.shape, lambda i, me_ref: (0, 0)) for a in arrays],
            out_specs=pl.BlockSpec((None, rows, width), lambda i, me_ref: (me_ref[0], 0, 0))),
        out_shape=jax.ShapeDtypeStruct((N_DEV, rows, width), F32),
    )(me, *arrays)


def tie(name, x, *deps):
    def body(*refs):
        del refs

    return pl.pallas_call(
        body, name=name, out_shape=jax.ShapeDtypeStruct(x.shape, x.dtype),
        in_specs=[ANY] * (1 + len(deps)), out_specs=ANY, input_output_aliases={0: 0},
    )(x, *deps)


def pair_add(name, g, from_sibling):
    _, r, c_dim = g.shape
    tr = r
    while tr * c_dim > PAIR_ADD_BLOCK and tr % 16 == 0:
        tr //= 2
    x, y, core = _place()
    where = jnp.stack([core, _chip(x, y)]).astype(jnp.int32)

    def body(where_ref, g_ref, s_ref, o_ref, zone_ref):
        total = (g_ref[...].astype(F32) + s_ref[...].astype(F32)).astype(o_ref.dtype)
        o_ref[...] = total

        @pl.when(pl.program_id(1) == where_ref[1])
        def _():
            zone_ref[...] = total

    blk = pl.BlockSpec((None, tr, c_dim), lambda i, q, where_ref: (q, i, 0))
    return pl.pallas_call(
        body, name=name,
        grid_spec=pltpu.PrefetchScalarGridSpec(
            num_scalar_prefetch=1, grid=(r // tr, N_CHIP),
            in_specs=[pl.BlockSpec((None, None, tr, c_dim), lambda i, q, where_ref: (q, where_ref[0], i, 0)), blk],
            out_specs=[blk, pl.BlockSpec((None, tr, c_dim), lambda i, q, where_ref: (where_ref[1], i, 0))]),
        out_shape=[jax.ShapeDtypeStruct((N_CHIP, r, c_dim), g.dtype)] * 2,
        compiler_params=_params(("parallel", "arbitrary")),
    )(where, g.reshape(N_CHIP, 2, r, c_dim), from_sibling)


def _matmul(name, lhs, rhs, *, out_shape, out_dtype, grid, lhs_spec, rhs_spec, out_spec, acc_shape,
            lhs_fn=None, epilogue=None):
    nk = grid[2]

    def body(lhs_ref, rhs_ref, out_ref, *scratch):
        def product():
            a = lhs_ref[...]
            if lhs_fn is not None:
                a = lhs_fn(a)
            return lax.dot_general(a, rhs_ref[...], NN, preferred_element_type=F32)

        def finish(r):
            if epilogue is not None:
                r = epilogue(r)
            out_ref[...] = r.astype(out_dtype)

        if nk == 1:
            finish(product())
        else:
            (acc_ref,) = scratch
            k = pl.program_id(2)

            @pl.when(k == 0)
            def _():
                acc_ref[...] = product()

            @pl.when(jnp.logical_and(k > 0, k < nk - 1))
            def _():
                acc_ref[...] += product()

            @pl.when(k == nk - 1)
            def _():
                finish(acc_ref[...] + product())

    return pl.pallas_call(
        body, name=name, grid=grid,
        out_shape=jax.ShapeDtypeStruct(out_shape, out_dtype),
        in_specs=[lhs_spec, rhs_spec], out_specs=out_spec,
        scratch_shapes=[pltpu.VMEM(acc_shape, F32)] if nk > 1 else [],
        compiler_params=_params(("parallel", "parallel", "arbitrary")),
    )(lhs, rhs)


def _tile(n, want):
    return want if n % want == 0 else n


def mm_nn(name, x, w, *, out_dtype, tn=512, tk=None, lhs_fn=None, epilogue=None):
    t, kdim = x.shape
    n = w.shape[1]
    tm, tn = _tile(t, MATMUL_ROWS), _tile(n, tn)
    tk = kdim if tk is None else _tile(kdim, tk)
    return _matmul(
        name, x, w, out_shape=(t, n), out_dtype=out_dtype, grid=(t // tm, n // tn, kdim // tk),
        lhs_spec=pl.BlockSpec((tm, tk), lambda i, j, k: (i, k)),
        rhs_spec=pl.BlockSpec((tk, tn), lambda i, j, k: (k, j)),
        out_spec=pl.BlockSpec((tm, tn), lambda i, j, k: (i, j)),
        acc_shape=(tm, tn), lhs_fn=lhs_fn, epilogue=epilogue)


def mm_nn_blocked(name, x, w, *, out_dtype, epilogue=None):
    t, kdim = x.shape
    nb = w.shape[2]
    tm = _tile(t, MATMUL_ROWS)
    tn = nb // 2 if nb >= 1024 else nb
    sub = nb // tn
    return _matmul(
        name, x, w, out_shape=(t, N_DEV * nb), out_dtype=out_dtype, grid=(t // tm, N_DEV * sub, 1),
        lhs_spec=pl.BlockSpec((tm, kdim), lambda i, j, k: (i, k)),
        rhs_spec=pl.BlockSpec((None, kdim, tn), lambda i, j, k: (j // sub, k, j % sub)),
        out_spec=pl.BlockSpec((tm, tn), lambda i, j, k: (i, j)),
        acc_shape=(tm, tn), epilogue=epilogue)


def mm_bwd_pair(name, dy, w, act, *, out_dtype, tile=512, act_fn=None, epilogue=None):
    t, n = dy.shape
    kdim = w.shape[0]
    tile = _tile(kdim, tile)

    def body(dy_ref, w_ref, act_ref, dx_ref, dw_ref):
        a = act_ref[...]
        dx = lax.dot_general(dy_ref[...], w_ref[...], NT, preferred_element_type=F32)
        if epilogue is not None:
            dx = epilogue(dx, a)
        dx_ref[...] = dx.astype(out_dtype)
        if act_fn is not None:
            a = act_fn(a)
        dw_ref[...] = lax.dot_general(a, dy_ref[...], TN, preferred_element_type=F32).astype(out_dtype)

    return pl.pallas_call(
        body, name=name, grid=(kdim // tile,),
        in_specs=[pl.BlockSpec((t, n), lambda j: (0, 0)), pl.BlockSpec((tile, n), lambda j: (j, 0)),
                  pl.BlockSpec((t, tile), lambda j: (0, j))],
        out_specs=[pl.BlockSpec((t, tile), lambda j: (0, j)), pl.BlockSpec((tile, n), lambda j: (j, 0))],
        out_shape=[jax.ShapeDtypeStruct((t, kdim), out_dtype), jax.ShapeDtypeStruct((kdim, n), out_dtype)],
        compiler_params=_params(("parallel",)),
    )(dy, w, act)


def mm_bwd_pair_blocked(name, dz, w, act, *, out_dtype, tile=1024):
    t = dz.shape[0]
    kdim, nb = w.shape[1], w.shape[2]
    tile = _tile(kdim, tile)

    def body(dz_ref, w_ref, act_ref, dx_ref, dw_ref, acc_ref):
        j = pl.program_id(1)
        dw_ref[...] = lax.dot_general(act_ref[...], dz_ref[...], TN, preferred_element_type=F32).astype(out_dtype)

        def product():
            return lax.dot_general(dz_ref[...], w_ref[...], NT, preferred_element_type=F32)

        @pl.when(j == 0)
        def _():
            acc_ref[...] = product()

        @pl.when(jnp.logical_and(j > 0, j < N_DEV - 1))
        def _():
            acc_ref[...] += product()

        @pl.when(j == N_DEV - 1)
        def _():
            dx_ref[...] = (acc_ref[...] + product()).astype(out_dtype)

    return pl.pallas_call(
        body, name=name, grid=(kdim // tile, N_DEV),
        in_specs=[pl.BlockSpec((t, nb), lambda i, j: (0, j)), pl.BlockSpec((None, tile, nb), lambda i, j: (j, i, 0)),
                  pl.BlockSpec((t, tile), lambda i, j: (0, i))],
        out_specs=[pl.BlockSpec((t, tile), lambda i, j: (0, i)),
                   pl.BlockSpec((None, tile, nb), lambda i, j: (j, i, 0))],
        out_shape=[jax.ShapeDtypeStruct((t, kdim), out_dtype), jax.ShapeDtypeStruct((N_DEV, kdim, nb), out_dtype)],
        scratch_shapes=[pltpu.VMEM((t, tile), F32)],
        compiler_params=_params(("parallel", "arbitrary")),
    )(dz, w, act)


def _rstd(v):
    return lax.rsqrt(jnp.mean(v * v, axis=-1, keepdims=True) + NORM_EPS)


def _rms_bwd(v, g, dy):
    r = _rstd(v)
    vhat = v * r
    dvh = dy * g
    dv = r * (dvh - vhat * jnp.mean(dvh * vhat, axis=-1, keepdims=True))
    return dv, dy * vhat


def _fold8(v):
    rows, n = v.shape
    return jnp.sum(v.reshape(rows // 8, 8, n), axis=0)


def _fold_lanes(v):
    out = v[:, 0:128]
    for i in range(1, v.shape[1] // 128):
        out = out + v[:, 128 * i:128 * (i + 1)]
    return out


def _accumulate(ref, v):
    i = pl.program_id(0)

    @pl.when(i == 0)
    def _():
        ref[...] = v

    @pl.when(i > 0)
    def _():
        ref[...] += v


def _row_call(body, name, t, ins, row_in, outs, acc_outs=(), tr=ROW_TILE):
    tr = _tile(t, tr)

    def in_spec(a, tiled):
        if isinstance(tiled, tuple):
            width, j = tiled
            return pl.BlockSpec((tr, width), lambda i: (i, j))
        return pl.BlockSpec((tr, a.shape[1]), lambda i: (i, 0)) if tiled else pl.BlockSpec(a.shape, lambda i: (0, 0))

    in_specs = [in_spec(a, tiled) for a, tiled in zip(ins, row_in)]
    out_specs = [pl.BlockSpec((tr, n), lambda i: (i, 0)) for n, _ in outs]
    out_specs += [pl.BlockSpec((8, n), lambda i: (0, 0)) for n in acc_outs]
    out_shape = [jax.ShapeDtypeStruct((t, n), dt) for n, dt in outs]
    out_shape += [jax.ShapeDtypeStruct((8, n), F32) for n in acc_outs]
    return pl.pallas_call(
        body, name=name, grid=(t // tr,), in_specs=in_specs, out_specs=out_specs, out_shape=out_shape,
        compiler_params=_params(("arbitrary",) if acc_outs else ("parallel",)),
    )(*ins)


def norm_pre(name, x, g):
    t, d = x.shape

    def body(x_ref, g_ref, h_ref):
        v = x_ref[...]
        h_ref[...] = (v * _rstd(v) * g_ref[...]).astype(BF16)

    return _row_call(body, name, t, [x, g], [True, False], [(d, BF16)])[0]


def post_pre(name, x, m, g_post, g_pre):
    t, d = x.shape

    def body(x_ref, m_ref, gp_ref, gn_ref, xo_ref, h_ref):
        mv = m_ref[...]
        xn = x_ref[...] + mv * _rstd(mv) * gp_ref[...]
        xo_ref[...] = xn
        h_ref[...] = (xn * _rstd(xn) * gn_ref[...]).astype(BF16)

    return _row_call(body, name, t, [x, m, g_post, g_pre], [True, True, False, False], [(d, F32), (d, BF16)])


def post_loss(name, x, f, g_post, target):
    t, d = x.shape

    def body(x_ref, f_ref, g_ref, t_ref, dx_ref, df_ref, loss_ref, dg_ref):
        fv = f_ref[...]
        g = g_ref[...]
        out = x_ref[...] + fv * _rstd(fv) * g
        err = out - t_ref[...]
        dx = err * (1.0 / d)
        dx_ref[...] = dx
        dfv, dg_rows = _rms_bwd(fv, g, dx)
        df_ref[...] = dfv.astype(BF16)
        _accumulate(loss_ref, _fold8(_fold_lanes(err * err)))
        _accumulate(dg_ref, _fold8(dg_rows))

    return _row_call(body, name, t, [x, f, g_post, target], [True, True, False, True],
                     [(d, F32), (d, BF16)], acc_outs=(128, d))


def bwd_pre_post(name, dx_out, x_in, g_pre, dh, f_prev, g_post_prev):
    t, d = x_in.shape

    def body(dxo_ref, x_ref, gpre_ref, dh_ref, f_ref, gpost_ref, dxi_ref, df_ref, dgpre_ref, dgpost_ref):
        dxv, dgpre_rows = _rms_bwd(x_ref[...], gpre_ref[...], dh_ref[...].astype(F32))
        dxi = dxo_ref[...] + dxv
        dxi_ref[...] = dxi
        dfv, dgpost_rows = _rms_bwd(f_ref[...], gpost_ref[...], dxi)
        df_ref[...] = dfv.astype(BF16)
        _accumulate(dgpre_ref, _fold8(dgpre_rows))
        _accumulate(dgpost_ref, _fold8(dgpost_rows))

    return _row_call(body, name, t, [dx_out, x_in, g_pre, dh, f_prev, g_post_prev],
                     [True, True, False, True, True, False], [(d, F32), (d, BF16)], acc_outs=(d, d))


def bwd_pre_final(name, dx_out, x_in, g_pre, dh):
    t, d = x_in.shape

    def body(dxo_ref, x_ref, gpre_ref, dh_ref, dxi_ref, dgpre_ref):
        dxv, dgpre_rows = _rms_bwd(x_ref[...], gpre_ref[...], dh_ref[...].astype(F32))
        dxi_ref[...] = dxo_ref[...] + dxv
        _accumulate(dgpre_ref, _fold8(dgpre_rows))

    return _row_call(body, name, t, [dx_out, x_in, g_pre, dh], [True, True, False, True], [(d, F32)], acc_outs=(d,))


def _layer_norm_parts(cv):
    mu = jnp.mean(cv, axis=-1, keepdims=True)
    xc = cv - mu
    rstd = lax.rsqrt(jnp.mean(xc * xc, axis=-1, keepdims=True) + NORM_EPS)
    return xc * rstd, rstd


def ln_silu(name, cv, g, b, y, y_block):
    t, n = cv.shape
    tr = _tile(t, ROW_TILE)

    def body(c_ref, g_ref, b_ref, y_in_ref, y_ref):
        chat, _ = _layer_norm_parts(c_ref[...])
        ln = chat * g_ref[...] + b_ref[...]
        y_ref[...] = (ln * jax.nn.sigmoid(ln)).astype(BF16)

    vec = pl.BlockSpec((1, n), lambda i: (0, 0))
    return pl.pallas_call(
        body, name=name, grid=(t // tr,),
        in_specs=[pl.BlockSpec((tr, n), lambda i: (i, 0)), vec, vec, ANY],
        out_specs=pl.BlockSpec((tr, n), lambda i: (i, y_block)),
        out_shape=jax.ShapeDtypeStruct(y.shape, y.dtype), input_output_aliases={3: 0},
        compiler_params=_params(("parallel",)),
    )(cv, g, b, y)


def ln_silu_bwd(name, cv, g, b, dy, dy_block):
    t, n = cv.shape

    def body(c_ref, g_ref, b_ref, dy_ref, dc_ref, dg_ref, db_ref):
        chat, rstd = _layer_norm_parts(c_ref[...])
        g = g_ref[...]
        ln = chat * g + b_ref[...]
        s = jax.nn.sigmoid(ln)
        dln = dy_ref[...].astype(F32) * (s * (1.0 + ln * (1.0 - s)))
        dchat = dln * g
        dc_ref[...] = rstd * (dchat - jnp.mean(dchat, axis=-1, keepdims=True)
                              - chat * jnp.mean(dchat * chat, axis=-1, keepdims=True))
        _accumulate(dg_ref, _fold8(dln * chat))
        _accumulate(db_ref, _fold8(dln))

    return _row_call(body, name, t, [cv, g, b, dy], [True, False, False, (n, dy_block)], [(n, F32)], acc_outs=(n, n))


def _chunks(t, fn, tc=TIME_CHUNK):
    tc = _tile(t, tc)

    def step(i, carry):
        fn(pl.multiple_of(i * tc, tc), tc)
        return carry

    lax.fori_loop(0, t // tc, step, 0)


def _rows_from(v, start, n):
    res = start % 8
    base = v if res == 0 else pltpu.roll(v, v.shape[0] - res, axis=0)
    return base[start - res:start - res + n, :]


def _shifted(window, offsets, tc):
    rows = window.shape[0]
    by_residue = {}
    for k, off in enumerate(offsets):
        by_residue.setdefault(off % 8, []).append((k, off))
    for res, taps in by_residue.items():
        base = window if res == 0 else pltpu.roll(window, rows - res, axis=0)
        for k, off in taps:
            yield k, base[off - res:off - res + tc, :]


def _taps(window, w_ref, offsets, tc, flip=False):
    acc = None
    for k, rows in _shifted(window, offsets, tc):
        kk = len(offsets) - 1 - k if flip else k
        term = w_ref[kk:kk + 1, :] * rows
        acc = term if acc is None else acc + term
    return acc


def _window_sums(win, tc, causal):
    sums = []
    cur, rows, step = win, tc + HALO, 1
    for _ in POOL_WINDOWS:
        rows -= 8
        if causal:
            cur = cur[8:8 + rows, :] + _rows_from(cur, 8 - step, rows)
            sums.append(cur[rows - tc:rows, :])
        else:
            cur = cur[0:rows, :] + _rows_from(cur, step, rows)
            sums.append(cur[0:tc, :])
        step *= 2
    return sums


def _pick(vals, g):
    out = vals[-1]
    for i in range(len(vals) - 2, -1, -1):
        out = jnp.where(g == i, vals[i], out)
    return out


def _pool_count(s, tc, g):
    t1 = (lax.broadcasted_iota(jnp.int32, (tc, 1), 0) + (s + 1)).astype(F32)
    width = _pick([float(w) for w in POOL_WINDOWS], g)
    return jnp.minimum(t1, width)


def pool_fwd(name, z, pool_w, pool_scale, d_pool, y_width):
    t = z.shape[0]
    ng, pg = pool_w.shape[0], pool_w.shape[1]

    def body(u_ref, w_ref, s_ref, pooled_ref, y_ref, pad):
        g = pl.program_id(0)
        pad[pl.ds(0, HALO), :] = jnp.zeros((HALO, pg), F32)

        def fill(s, tc):
            pad[pl.ds(HALO + s, tc), :] = u_ref[pl.ds(s, tc), :].astype(F32)

        def chunk(s, tc):
            win = pad[pl.ds(s, tc + HALO), :]
            total = _pick(_window_sums(win, tc, causal=True), g)
            pooled = total / _pool_count(s, tc, g) - win[HALO:HALO + tc, :]
            pooled_ref[pl.ds(s, tc), :] = pooled.astype(BF16)

        _chunks(t, fill)
        _chunks(t, chunk)
        mixed = jnp.dot(pooled_ref[...], w_ref[...], preferred_element_type=F32)
        y_ref[...] = (mixed * s_ref[...]).astype(BF16)

    col = pl.BlockSpec((t, pg), lambda g: (0, g))
    return pl.pallas_call(
        body, name=name, grid=(ng,),
        in_specs=[col, pl.BlockSpec((None, pg, pg), lambda g: (g, 0, 0)), pl.BlockSpec((1, pg), lambda g: (0, g))],
        out_specs=[col, col],
        out_shape=[jax.ShapeDtypeStruct((t, d_pool), BF16), jax.ShapeDtypeStruct((t, y_width), BF16)],
        scratch_shapes=[pltpu.VMEM((t + HALO, pg), F32)],
        compiler_params=_params(("parallel",)),
    )(z, pool_w, pool_scale)


def pool_bwd(name, pooled, dy, pool_w, pool_scale, dz):
    t, d_pool = pooled.shape
    ng, pg = pool_w.shape[0], pool_w.shape[1]

    def body(p_ref, dy_ref, w_ref, s_ref, dz_ref, du_ref, dw_ref, ds_ref, pad):
        g = pl.program_id(0)
        w = w_ref[...]
        dyv = dy_ref[...].astype(F32)
        mixed = jnp.dot(p_ref[...], w, preferred_element_type=F32)
        ds_ref[...] = jnp.sum(dyv * mixed, axis=0, keepdims=True)
        dmixed = (dyv * s_ref[...]).astype(BF16)
        dw_ref[...] = lax.dot_general(p_ref[...], dmixed, TN, preferred_element_type=F32)
        pad[...] = jnp.zeros((t + HALO, pg), F32)
        pad[pl.ds(0, t), :] = lax.dot_general(dmixed, w, NT, preferred_element_type=F32)

        def scale(s, tc):
            pad[pl.ds(s, tc), :] = pad[pl.ds(s, tc), :] / _pool_count(s, tc, g)

        def chunk(s, tc):
            win = pad[pl.ds(s, tc + HALO), :]
            total = _pick(_window_sums(win, tc, causal=False), g)
            du_ref[pl.ds(s, tc), :] = (total - win[0:tc, :] * _pool_count(s, tc, g)).astype(BF16)

        _chunks(t, scale)
        _chunks(t, chunk)

    col = pl.BlockSpec((t, pg), lambda g: (0, g))
    vec = pl.BlockSpec((1, pg), lambda g: (0, g))
    mat = pl.BlockSpec((None, pg, pg), lambda g: (g, 0, 0))
    return pl.pallas_call(
        body, name=name, grid=(ng,),
        in_specs=[col, col, mat, vec, ANY], out_specs=[col, mat, vec],
        out_shape=[jax.ShapeDtypeStruct(dz.shape, dz.dtype), jax.ShapeDtypeStruct((ng, pg, pg), F32),
                   jax.ShapeDtypeStruct((1, d_pool), F32)],
        input_output_aliases={4: 0},
        scratch_shapes=[pltpu.VMEM((t + HALO, pg), F32)],
        compiler_params=_params(("parallel",)),
    )(pooled, dy, pool_w, pool_scale, dz)


def conv_fwd(name, z, conv_w, conv_b, d_pool, d_conv):
    t = z.shape[0]
    kw = conv_w.shape[0]
    tc_ch = _tile(d_conv, CHANNEL_TILE)
    v0, g0 = d_pool // tc_ch, (d_pool + d_conv) // tc_ch

    def body(v_ref, g_ref, w_ref, b_ref, c_ref, pad):
        pad[pl.ds(0, HALO), :] = jnp.zeros((HALO, tc_ch), F32)

        def fill(s, tc):
            pad[pl.ds(HALO + s, tc), :] = v_ref[pl.ds(s, tc), :].astype(F32) * jax.nn.sigmoid(g_ref[pl.ds(s, tc), :].astype(F32))

        def chunk(s, tc):
            win = pad[pl.ds(s, tc + HALO), :]
            c_ref[pl.ds(s, tc), :] = _taps(win, w_ref, [HALO - (kw - 1) + k for k in range(kw)], tc) + b_ref[...]

        _chunks(t, fill)
        _chunks(t, chunk)

    return pl.pallas_call(
        body, name=name, grid=(d_conv // tc_ch,),
        in_specs=[pl.BlockSpec((t, tc_ch), lambda j: (0, v0 + j)), pl.BlockSpec((t, tc_ch), lambda j: (0, g0 + j)),
                  pl.BlockSpec((kw, tc_ch), lambda j: (0, j)), pl.BlockSpec((1, tc_ch), lambda j: (0, j))],
        out_specs=pl.BlockSpec((t, tc_ch), lambda j: (0, j)),
        out_shape=jax.ShapeDtypeStruct((t, d_conv), F32),
        scratch_shapes=[pltpu.VMEM((t + HALO, tc_ch), F32)],
        compiler_params=_params(("parallel",)),
    )(z, z, conv_w, conv_b)


def conv_bwd(name, z, dc, conv_w, d_pool, d_conv):
    t = z.shape[0]
    kw = conv_w.shape[0]
    tc_ch = _tile(d_conv, CHANNEL_TILE)
    v0, g0 = d_pool // tc_ch, (d_pool + d_conv) // tc_ch

    def body(v_ref, g_ref, dc_ref, w_ref, dz_ref, dw_ref, db_ref, pad_a, pad_dc, acc_w, acc_b, tiles, sems):
        j = pl.program_id(0)
        dv_ref, dg_ref = tiles.at[0], tiles.at[1]
        writes = [pltpu.make_async_copy(tiles.at[p], dz_ref.at[:, pl.ds((first + j) * tc_ch, tc_ch)], sems.at[p])
                  for p, first in enumerate([v0, g0])]

        def wait_writes():
            for cp in writes:
                cp.wait()

        pad_a[pl.ds(0, HALO), :] = jnp.zeros((HALO, tc_ch), F32)
        pad_dc[pl.ds(t, HALO), :] = jnp.zeros((HALO, tc_ch), F32)
        acc_w[...] = jnp.zeros_like(acc_w)
        acc_b[...] = jnp.zeros_like(acc_b)

        def fill(s, tc):
            pad_a[pl.ds(HALO + s, tc), :] = v_ref[pl.ds(s, tc), :].astype(F32) * jax.nn.sigmoid(g_ref[pl.ds(s, tc), :].astype(F32))
            pad_dc[pl.ds(s, tc), :] = dc_ref[pl.ds(s, tc), :]

        def chunk(s, tc):
            dcv = pad_dc[pl.ds(s, tc), :]
            win_a = pad_a[pl.ds(s, tc + HALO), :]
            for k, rows in _shifted(win_a, [HALO - (kw - 1) + k for k in range(kw)], tc):
                acc_w[pl.ds(8 * k, 8), :] += _fold8(dcv * rows)
            acc_b[...] += _fold8(dcv)
            da = _taps(pad_dc[pl.ds(s, tc + HALO), :], w_ref, list(range(kw)), tc, flip=True)
            vv = v_ref[pl.ds(s, tc), :].astype(F32)
            sg = jax.nn.sigmoid(g_ref[pl.ds(s, tc), :].astype(F32))
            dv_ref[pl.ds(s, tc), :] = (da * sg).astype(BF16)
            dg_ref[pl.ds(s, tc), :] = (da * vv * sg * (1.0 - sg)).astype(BF16)

        _chunks(t, fill)
        pl.when(j > 0)(wait_writes)
        _chunks(t, chunk)
        for cp in writes:
            cp.start()
        pl.when(j == n_tiles - 1)(wait_writes)
        for k in range(kw):
            dw_ref[k:k + 1, :] = jnp.sum(acc_w[pl.ds(8 * k, 8), :], axis=0, keepdims=True)
        db_ref[...] = jnp.sum(acc_b[...], axis=0, keepdims=True)

    n_tiles = d_conv // tc_ch
    return pl.pallas_call(
        body, name=name, grid=(n_tiles,),
        in_specs=[pl.BlockSpec((t, tc_ch), lambda j: (0, v0 + j)), pl.BlockSpec((t, tc_ch), lambda j: (0, g0 + j)),
                  pl.BlockSpec((t, tc_ch), lambda j: (0, j)), pl.BlockSpec((kw, tc_ch), lambda j: (0, j))],
        out_specs=[ANY, pl.BlockSpec((kw, tc_ch), lambda j: (0, j)), pl.BlockSpec((1, tc_ch), lambda j: (0, j))],
        out_shape=[jax.ShapeDtypeStruct((t, d_pool + 2 * d_conv), BF16),
                   jax.ShapeDtypeStruct((kw, d_conv), F32), jax.ShapeDtypeStruct((1, d_conv), F32)],
        scratch_shapes=[pltpu.VMEM((t + HALO, tc_ch), F32), pltpu.VMEM((t + HALO, tc_ch), F32),
                        pltpu.VMEM((8 * kw, tc_ch), F32), pltpu.VMEM((8, tc_ch), F32),
                        pltpu.VMEM((2, t, tc_ch), BF16), pltpu.SemaphoreType.DMA((2,))],
        compiler_params=_params(("arbitrary",)),
    )(z, z, dc, conv_w)


def short_fwd(name, z, conv_w, d_short):
    t = z.shape[0]
    kw = conv_w.shape[0]
    tc_ch = _tile(d_short, CHANNEL_TILE)
    nt = d_short // tc_ch

    def body(b_ref, c_ref, u_ref, w_ref, y_ref, pad):
        pad[pl.ds(0, HALO), :] = jnp.zeros((HALO, tc_ch), F32)

        def fill(s, tc):
            pad[pl.ds(HALO + s, tc), :] = c_ref[pl.ds(s, tc), :].astype(F32) * u_ref[pl.ds(s, tc), :].astype(F32)

        def chunk(s, tc):
            win = pad[pl.ds(s, tc + HALO), :]
            cq = _taps(win, w_ref, [HALO - (kw - 1) + k for k in range(kw)], tc)
            y_ref[pl.ds(s, tc), :] = (b_ref[pl.ds(s, tc), :].astype(F32) * cq).astype(BF16)

        _chunks(t, fill)
        _chunks(t, chunk)

    return pl.pallas_call(
        body, name=name, grid=(nt,),
        in_specs=[pl.BlockSpec((t, tc_ch), lambda j: (0, j)), pl.BlockSpec((t, tc_ch), lambda j: (0, nt + j)),
                  pl.BlockSpec((t, tc_ch), lambda j: (0, 2 * nt + j)), pl.BlockSpec((kw, tc_ch), lambda j: (0, j))],
        out_specs=pl.BlockSpec((t, tc_ch), lambda j: (0, j)),
        out_shape=jax.ShapeDtypeStruct((t, d_short), BF16),
        scratch_shapes=[pltpu.VMEM((t + HALO, tc_ch), F32)],
        compiler_params=_params(("parallel",)),
    )(z, z, z, conv_w)


def short_bwd(name, z, dy, conv_w, d_short):
    t = z.shape[0]
    kw = conv_w.shape[0]
    tc_ch = _tile(d_short, CHANNEL_TILE)
    nt = d_short // tc_ch

    def body(b_ref, c_ref, u_ref, dy_ref, w_ref, dz_ref, dw_ref, pad_q, pad_dcq, acc_w, tiles, sems):
        j = pl.program_id(0)
        db_ref, dcg_ref, du_ref = tiles.at[0], tiles.at[1], tiles.at[2]
        writes = [pltpu.make_async_copy(tiles.at[p], dz_ref.at[:, pl.ds((p * nt + j) * tc_ch, tc_ch)], sems.at[p])
                  for p in range(3)]

        def wait_writes():
            for cp in writes:
                cp.wait()

        pad_q[pl.ds(0, HALO), :] = jnp.zeros((HALO, tc_ch), F32)
        pad_dcq[pl.ds(t, HALO), :] = jnp.zeros((HALO, tc_ch), F32)
        acc_w[...] = jnp.zeros_like(acc_w)

        def fill(s, tc):
            rows = pl.ds(s, tc)
            pad_q[pl.ds(HALO + s, tc), :] = c_ref[rows, :].astype(F32) * u_ref[rows, :].astype(F32)
            pad_dcq[rows, :] = dy_ref[rows, :].astype(F32) * b_ref[rows, :].astype(F32)

        def chunk(s, tc):
            rows = pl.ds(s, tc)
            win_q = pad_q[pl.ds(s, tc + HALO), :]
            dcq = pad_dcq[rows, :]
            cq = None
            for k, shifted in _shifted(win_q, [HALO - (kw - 1) + k for k in range(kw)], tc):
                acc_w[pl.ds(8 * k, 8), :] += _fold8(dcq * shifted)
                term = w_ref[k:k + 1, :] * shifted
                cq = term if cq is None else cq + term
            db_ref[rows, :] = (dy_ref[rows, :].astype(F32) * cq).astype(BF16)
            dq = _taps(pad_dcq[pl.ds(s, tc + HALO), :], w_ref, list(range(kw)), tc, flip=True)
            dcg_ref[rows, :] = (dq * u_ref[rows, :].astype(F32)).astype(BF16)
            du_ref[rows, :] = (dq * c_ref[rows, :].astype(F32)).astype(BF16)

        _chunks(t, fill)
        pl.when(j > 0)(wait_writes)
        _chunks(t, chunk)
        for cp in writes:
            cp.start()
        pl.when(j == nt - 1)(wait_writes)
        for k in range(kw):
            dw_ref[k:k + 1, :] = jnp.sum(acc_w[pl.ds(8 * k, 8), :], axis=0, keepdims=True)

    zspec = [pl.BlockSpec((t, tc_ch), lambda j, o=o: (0, o * nt + j)) for o in range(3)]
    return pl.pallas_call(
        body, name=name, grid=(nt,),
        in_specs=[*zspec, pl.BlockSpec((t, tc_ch), lambda j: (0, j)), pl.BlockSpec((kw, tc_ch), lambda j: (0, j))],
        out_specs=[ANY, pl.BlockSpec((kw, tc_ch), lambda j: (0, j))],
        out_shape=[jax.ShapeDtypeStruct((t, 3 * d_short), BF16), jax.ShapeDtypeStruct((kw, d_short), F32)],
        scratch_shapes=[pltpu.VMEM((t + HALO, tc_ch), F32), pltpu.VMEM((t + HALO, tc_ch), F32),
                        pltpu.VMEM((8 * kw, tc_ch), F32), pltpu.VMEM((3, t, tc_ch), BF16),
                        pltpu.SemaphoreType.DMA((3,))],
        compiler_params=_params(("arbitrary",)),
    )(z, z, z, dy, conv_w)


def _adamw_update(w, m, v, g):
    nm = ADAM_B1 * m + (1.0 - ADAM_B1) * g
    nv = ADAM_B2 * v + (1.0 - ADAM_B2) * (g * g)
    m_hat = nm / (1.0 - ADAM_B1 ** ADAM_STEP)
    v_hat = nv / (1.0 - ADAM_B2 ** ADAM_STEP)
    return -ADAM_LR * (m_hat / (jnp.sqrt(v_hat) + ADAM_EPS) + ADAM_WD * w), nm, nv


def adamw_replicated(name, params, first_moments, second_moments, contributions, layout, scalar_at):
    n = len(params)
    n_slots = contributions.shape[0]

    def total(c_ref, row, lane, rows, lanes):
        acc = c_ref[0, row:row + rows, lane:lane + lanes]
        for slot in range(1, n_slots):
            acc = acc + c_ref[slot, row:row + rows, lane:lane + lanes]
        return acc

    def body(*refs):
        ws, ms, vs, c_ref = refs[:n], refs[n:2 * n], refs[2 * n:3 * n], refs[3 * n]
        outs = refs[3 * n + 1:]
        outs[0][...] = total(c_ref, *scalar_at, 1, 128)
        for i, (row, lane) in enumerate(layout):
            g = total(c_ref, row, lane, *params[i].shape)
            grad_ref, delta_ref, nm_ref, nv_ref = outs[1 + 4 * i:5 + 4 * i]
            grad_ref[...] = g
            delta_ref[...], nm_ref[...], nv_ref[...] = _adamw_update(ws[i][...], ms[i][...], vs[i][...], g)

    out_shape = [jax.ShapeDtypeStruct((1, 128), F32)]
    for p in params:
        out_shape += [jax.ShapeDtypeStruct(p.shape, F32)] * 4
    return pl.pallas_call(body, name=name, out_shape=out_shape)(*params, *first_moments, *second_moments, contributions)


def adamw(name, w, m, v, contributions):
    r, c = w.shape
    nc = len(contributions)
    n_slots = contributions[0].shape[0]
    tr = 256 if c <= 1024 else 128
    if any(a.shape[1] % tr for a in contributions):
        assert nc == 1
        tr = r
    tiles = [a.shape[1] // tr for a in contributions]
    first = [sum(tiles[:j]) for j in range(nc)]

    def body(w_ref, m_ref, v_ref, *rest):
        g_refs, (grad_ref, delta_ref, nm_ref, nv_ref) = rest[:nc], rest[nc:]
        i = pl.program_id(0)
        g = None
        for j, g_ref in enumerate(g_refs):
            s = g_ref[0].astype(F32)
            for slot in range(1, n_slots):
                s = s + g_ref[slot].astype(F32)
            g = s if g is None else jnp.where(i >= first[j], s, g)
        grad_ref[...] = g
        delta_ref[...], nm_ref[...], nv_ref[...] = _adamw_update(w_ref[...], m_ref[...], v_ref[...], g)

    blk = pl.BlockSpec((tr, c), lambda i: (i, 0))
    g_specs = [pl.BlockSpec((n_slots, tr, c), lambda i, j=j: (0, jnp.clip(i - first[j], 0, tiles[j] - 1), 0))
               for j in range(nc)]
    return pl.pallas_call(
        body, name=name, grid=(r // tr,),
        in_specs=[blk, blk, blk, *g_specs],
        out_specs=[blk] * 4, out_shape=[jax.ShapeDtypeStruct((r, c), F32)] * 4,
        compiler_params=_params(("parallel",)),
    )(w, m, v, *contributions)


def _pad_rows(a, rows):
    return jnp.pad(a, ((0, rows - a.shape[0]), (0, 0)))


def kernel(x, mix_pre_g, mix_post_g, ffn_pre_g, ffn_post_g, ab_w_in, pool_w, pool_scale, conv_w, conv_b, conv_ln_g, conv_ln_b, ab_w_out, sc_w_in, sc_conv_w, sc_w_out, ffn_w1, ffn_w2, loss_target, m_mix_pre_g, m_mix_post_g, m_ffn_pre_g, m_ffn_post_g, m_ab_w_in, m_pool_w, m_pool_scale, m_conv_w, m_conv_b, m_conv_ln_g, m_conv_ln_b, m_ab_w_out, m_sc_w_in, m_sc_conv_w, m_sc_w_out, m_ffn_w1, m_ffn_w2, v_mix_pre_g, v_mix_post_g, v_ffn_pre_g, v_ffn_post_g, v_ab_w_in, v_pool_w, v_pool_scale, v_conv_w, v_conv_b, v_conv_ln_g, v_conv_ln_b, v_ab_w_out, v_sc_w_in, v_sc_conv_w, v_sc_w_out, v_ffn_w1, v_ffn_w2):
    t, d = x.shape[1], x.shape[2]
    d_pool = pool_scale.shape[1]
    d_conv = conv_b.shape[1]
    d_short = d
    ng, pg = pool_w.shape[1], pool_w.shape[3]
    kw, ks = conv_w.shape[1], sc_conv_w.shape[1]
    nb_ab, nb_sc, nb_ff = ab_w_in.shape[2], sc_w_in.shape[2], ffn_w1.shape[2]

    xs = x[0]
    target = loss_target[0]

    lanes = min(128, d_conv // N_DEV)
    small_rows = [kw * (d_conv // N_DEV) // lanes, ks * (d_short // N_DEV) // lanes, ng * (pg // N_DEV) * pg // lanes]
    small_total = -(-sum(small_rows) // 8) * 8
    r0, r1, r2 = small_rows[0], small_rows[0] + small_rows[1], sum(small_rows)

    def pack_small(a_conv, a_sconv, a_pool):
        parts = [a_conv[0].reshape(-1, lanes), a_sconv[0].reshape(-1, lanes), a_pool[0].reshape(-1, lanes)]
        return _pad_rows(jnp.concatenate(parts, axis=0), small_total)

    shards = {
        "ab_in": (ab_w_in, 0, BF16), "small": (pack_small(conv_w, sc_conv_w, pool_w)[None], 0, F32),
        "ab_out": (ab_w_out, 0, BF16), "ff1_0": (ffn_w1, 0, BF16), "ff2_0": (ffn_w2, 0, BF16),
        "sc_in": (sc_w_in, 0, BF16), "sc_out": (sc_w_out, 0, BF16),
        "ff1_1": (ffn_w1, 1, BF16), "ff2_1": (ffn_w2, 1, BF16)}
    direct = ["ab_in", "small", "ab_out"]
    zones = {nm: place_shard("place_" + nm, *shards[nm]) for nm in direct}
    started, token = copies_start("gather_start", [[zones[nm]] for nm in direct], _first_hop, 4)
    started = dict(zip(direct, started))
    ring = {}
    for nm in ["ff1_0", "ff2_0"]:
        zones[nm] = place_shard("place_" + nm, *shards[nm], deps=[token])
        (ring[nm],), token = copies_start("ring_start_" + nm, [[zones[nm]]], _ring_hop1, 3, deps=[token])
    for nm in shards:
        if nm not in zones:
            zones[nm] = place_shard("place_" + nm, *shards[nm], deps=[token])

    ties = [0]

    def after(v, *deps):
        ties[0] += 1
        return tie(f"tie_{ties[0]}", v, *deps)

    def fetch_begin(nm, dep):
        (zone,) = copies_wait("gather_wait_" + nm, started[nm], _first_hop, dep)
        (hop,), tok = copies_start("forward_start_" + nm, [[zone]], _second_hop, 3)
        return hop, tok

    def fetch_end(nm, hop, dep):
        return copies_wait("forward_wait_" + nm, hop, _second_hop, dep)[0]

    def ring_step(tag, dep, second=None, first=None, third=None):
        names, groups, hops, counts = [], [], [], []
        if third is not None:
            groups.append(copies_wait("ring2_wait_" + third, ring[third], _ring_hop2, dep))
            names, hops, counts = names + [third], hops + [_ring_hop3], counts + [1]
        if second is not None:
            groups.append(copies_wait("ring1_wait_" + second, ring[second], _ring_hop1, dep))
            names, hops, counts = names + [second], hops + [_ring_hop2], counts + [4]
        if first is not None:
            groups.append([zones[first]])
            names, hops, counts = names + [first], hops + [_ring_hop1], counts + [3]
        begun, tok = copies_start("ring_start_" + tag, groups, hops, counts, deps=[dep])
        ring.update(zip(names, begun))
        return tok

    def ring_done(nm, dep):
        return copies_wait("ring3_wait_" + nm, ring[nm], _ring_hop3, dep)[0]

    relu = lambda r: jnp.maximum(r, 0.0)
    square = lambda a: a * a
    relu2_bwd = lambda r, a: r * (2.0 * a.astype(F32))

    def row(vec, l):
        return vec[l:l + 1]

    hop_small, _ = fetch_begin("small", token)
    hop_ab_in, tok = fetch_begin("ab_in", token)
    w_small = fetch_end("small", hop_small, tok)
    w_ab_in = fetch_end("ab_in", hop_ab_in, tok)
    w_conv = w_small[:, :r0].reshape(N_DEV, kw, -1).transpose(1, 0, 2).reshape(kw, d_conv)
    w_sconv = w_small[:, r0:r1].reshape(N_DEV, ks, -1).transpose(1, 0, 2).reshape(ks, d_short)
    w_pool = w_small[:, r1:r2].reshape(N_DEV, ng, -1, pg).transpose(1, 0, 2, 3).reshape(ng, pg, pg).astype(BF16)
    h0 = norm_pre("norm_pre", xs, after(row(mix_pre_g, 0), token))
    z0 = mm_nn_blocked("ab_in", h0, w_ab_in, out_dtype=BF16)
    hop, tok = fetch_begin("ab_out", z0)
    z0 = after(z0, tok)
    pooled, y0 = pool_fwd("pool_fwd", z0, w_pool, pool_scale, d_pool, d_pool + d_conv)
    cv = conv_fwd("conv_fwd", z0, w_conv, conv_b, d_pool, d_conv)
    y0 = ln_silu("ln_silu", cv, conv_ln_g, conv_ln_b, y0, d_pool // d_conv)
    w_ab_out = fetch_end("ab_out", hop, y0)
    tok = ring_step("a", w_ab_out, second="ff1_0", first="sc_in")
    y0 = after(y0, tok)
    m0 = mm_nn("ab_out", y0, w_ab_out.reshape(d_pool + d_conv, d), out_dtype=F32)
    x1, h1 = post_pre("post_pre_0", xs, m0, row(mix_post_g, 0), row(ffn_pre_g, 0))
    tok = ring_step("b", h1, second="ff2_0", first="sc_out", third="ff1_0")
    w_ff1_0 = ring_done("ff1_0", tok)
    a0 = mm_nn_blocked("ffn0_up", h1, w_ff1_0, out_dtype=BF16, epilogue=relu)
    tok = ring_step("c", a0, second="sc_in", first="ff1_1", third="ff2_0")
    w_ff2_0 = ring_done("ff2_0", tok).reshape(-1, d)
    f0 = mm_nn("ffn0_down", a0, w_ff2_0, out_dtype=F32, tk=2048, lhs_fn=square)
    tok = ring_step("d", f0, second="sc_out", first="ff2_1", third="sc_in")
    f0 = after(f0, tok)
    x2, h2 = post_pre("post_pre_1", x1, f0, row(ffn_post_g, 0), row(mix_pre_g, 1))
    w_sc_in = ring_done("sc_in", h2)
    z1 = mm_nn_blocked("sc_in", h2, w_sc_in, out_dtype=BF16)
    tok = ring_step("e", z1, second="ff1_1", third="sc_out")
    z1 = after(z1, tok)
    y1 = short_fwd("short_fwd", z1, w_sconv, d_short)
    w_sc_out = ring_done("sc_out", y1).reshape(d_short, d)
    m1 = mm_nn("sc_out", y1, w_sc_out, out_dtype=F32)
    tok = ring_step("f", m1, second="ff2_1")
    m1 = after(m1, tok)
    x3, h3 = post_pre("post_pre_2", x2, m1, row(mix_post_g, 1), row(ffn_pre_g, 1))
    tok = ring_step("g", h3, third="ff1_1")
    w_ff1_1 = ring_done("ff1_1", tok)
    a1 = mm_nn_blocked("ffn1_up", h3, w_ff1_1, out_dtype=BF16, epilogue=relu)
    tok = ring_step("h", a1, third="ff2_1")
    w_ff2_1 = ring_done("ff2_1", tok).reshape(-1, d)
    f1 = mm_nn("ffn1_down", a1, w_ff2_1, out_dtype=F32, tk=2048, lhs_fn=square)
    dx4, df1, loss_part, dg_ffn_post1 = post_loss("post_loss", x3, f1, row(ffn_post_g, 1), target)

    red = {}

    def reduce_step(dep, begin=None, middle=None):
        tags, groups, hops, counts = [], [], [], []
        if begin is not None:
            tag, g = begin
            tags, groups = tags + [tag], groups + [[g, lax.empty((N_CHIP,) + g.shape[1:], g.dtype)]]
            hops, counts = hops + [_pair_hop], counts + [N_CHIP]
        if middle is not None:
            g, from_sibling = copies_wait("pair_wait_" + middle, red[middle], _pair_hop, dep)
            tags, groups = tags + [middle], groups + [list(pair_add("pair_add_" + middle, g, from_sibling))]
            hops, counts = hops + [_chip_hop], counts + [3]
        begun, tok = copies_start("reduce_start_" + "_".join(tags), groups, hops, counts, deps=[dep])
        red.update(zip(tags, begun))
        return tok

    def reduce_end(tag, dep):
        return copies_wait("chips_wait_" + tag, red[tag], _chip_hop, dep)[1]

    dpre, dw = mm_bwd_pair("ffn1_da_dw2", df1, w_ff2_1, a1, out_dtype=BF16, act_fn=square, epilogue=relu2_bwd)
    dpre = after(dpre, reduce_step(dpre, begin=("ff2_1", dw.reshape(N_DEV, -1, d))))
    dh3, dw = mm_bwd_pair_blocked("ffn1_dh_dw1", dpre, w_ff1_1, h3, out_dtype=BF16)
    dx3, dm1, dg_ffn_pre1, dg_mix_post1 = bwd_pre_post("bwd_3", dx4, x3, row(ffn_pre_g, 1), dh3, m1, row(mix_post_g, 1))
    dm1 = after(dm1, reduce_step(dm1, begin=("ff1_1", dw), middle="ff2_1"))

    dy1, dw = mm_bwd_pair("sc_dy_dwout", dm1, w_sc_out, y1, out_dtype=BF16)
    dy1 = after(dy1, reduce_step(dy1, begin=("sc_out", dw.reshape(N_DEV, -1, d)), middle="ff1_1"))
    dz1, dw_sconv = short_bwd("short_bwd", z1, dy1, w_sconv, d_short)
    dh2, dw = mm_bwd_pair_blocked("sc_dh_dwin", dz1, w_sc_in, h2, out_dtype=BF16)
    dx2, df0, dg_mix_pre1, dg_ffn_post0 = bwd_pre_post("bwd_2", dx3, x2, row(mix_pre_g, 1), dh2, f0, row(ffn_post_g, 0))
    df0 = after(df0, reduce_step(df0, begin=("sc_in", dw), middle="sc_out"))

    dpre, dw = mm_bwd_pair("ffn0_da_dw2", df0, w_ff2_0, a0, out_dtype=BF16, act_fn=square, epilogue=relu2_bwd)
    dpre = after(dpre, reduce_step(dpre, begin=("ff2_0", dw.reshape(N_DEV, -1, d)), middle="sc_in"))
    dh1, dw = mm_bwd_pair_blocked("ffn0_dh_dw1", dpre, w_ff1_0, h1, out_dtype=BF16)
    dx1, dm0, dg_ffn_pre0, dg_mix_post0 = bwd_pre_post("bwd_1", dx2, x1, row(ffn_pre_g, 0), dh1, m0, row(mix_post_g, 0))
    dm0 = after(dm0, reduce_step(dm0, begin=("ff1_0", dw), middle="ff2_0"))

    dy0, dw = mm_bwd_pair("ab_dy_dwout", dm0, w_ab_out.reshape(d_pool + d_conv, d), y0, out_dtype=BF16)
    dy0 = after(dy0, reduce_step(dy0, begin=("ab_out", dw.reshape(N_DEV, -1, d)), middle="ff1_0"))
    dcv, dg_ln_g, dg_ln_b = ln_silu_bwd("ln_silu_bwd", cv, conv_ln_g, conv_ln_b, dy0, d_pool // d_conv)
    dz0, dw_conv, dg_conv_b = conv_bwd("conv_bwd", z0, dcv, w_conv, d_pool, d_conv)
    dz0, dw_pool, dg_pool_scale = pool_bwd("pool_bwd", pooled, dy0, w_pool, pool_scale, dz0)
    small_parts = [
        dw_conv.reshape(kw, N_DEV, -1).transpose(1, 0, 2).reshape(N_DEV, -1, lanes),
        dw_sconv.reshape(ks, N_DEV, -1).transpose(1, 0, 2).reshape(N_DEV, -1, lanes),
        dw_pool.reshape(ng, N_DEV, pg // N_DEV, pg).transpose(1, 0, 2, 3).reshape(N_DEV, -1, lanes),
    ]
    small = jnp.pad(jnp.concatenate(small_parts, axis=1), ((0, 0), (0, small_total - r2), (0, 0)))
    dz0 = after(dz0, reduce_step(dz0, begin=("small", small), middle="ab_out"))
    dh0, dw = mm_bwd_pair_blocked("ab_dh_dwin", dz0, w_ab_in, h0, out_dtype=BF16)
    dh0 = after(dh0, reduce_step(dh0, begin=("ab_in", dw), middle="small"))
    grad_x, dg_mix_pre0 = bwd_pre_final("bwd_0", dx1, xs, row(mix_pre_g, 0), dh0)
    tok = reduce_step(grad_x, middle="ab_in")

    gains = [dg_mix_pre0, dg_mix_pre1, dg_mix_post0, dg_mix_post1, dg_ffn_pre0, dg_ffn_pre1, dg_ffn_post0, dg_ffn_post1]
    pieces = [(g, i, 0) for i, g in enumerate(gains)]
    rep_layout = [(0, 0), (2, 0), (4, 0), (6, 0)]
    offset = 0
    for g in (dg_pool_scale, dg_conv_b, dg_ln_g, dg_ln_b):
        at = (len(gains) + offset // d, offset % d)
        pieces.append((g, *at))
        rep_layout.append(at)
        offset += g.shape[1]
    loss_at = (len(gains) + -(-offset // d), 0)
    rep_zone = place_sheet("place_rep", pieces, loss_part, loss_at, 16, d)
    (rep_hop,), tok = copies_start("rep_start", [[rep_zone]], _first_hop, 4,
                                   deps=[tok])

    def upd(name, w, m, v, contribs):
        shape = w.shape
        flat2 = lambda a: a.reshape(-1, shape[-1])
        outs = adamw(name, flat2(w), flat2(m), flat2(v), contribs)
        return [o.reshape(shape) for o in outs]

    g_ff2 = [reduce_end("ff2_0", tok), reduce_end("ff2_1", tok)]
    o_ff2 = upd("adam_ffn_w2", ffn_w2, m_ffn_w2, v_ffn_w2, g_ff2)
    (rep_zone,) = copies_wait("rep_wait", rep_hop, _first_hop, o_ff2[0])
    (rep_hop,), _ = copies_start("rep_forward_start", [[rep_zone]], _second_hop, 3)
    g_ff1 = [reduce_end("ff1_0", o_ff2[0]), reduce_end("ff1_1", o_ff2[0])]
    o_ff1 = upd("adam_ffn_w1", ffn_w1, m_ffn_w1, v_ffn_w1, g_ff1)
    (rep_all,) = copies_wait("rep_forward_wait", rep_hop, _second_hop, o_ff1[0])
    loss_sum, *o_rep = adamw_replicated(
        "adam_replicated",
        [mix_pre_g, mix_post_g, ffn_pre_g, ffn_post_g, pool_scale, conv_b, conv_ln_g, conv_ln_b],
        [m_mix_pre_g, m_mix_post_g, m_ffn_pre_g, m_ffn_post_g, m_pool_scale, m_conv_b, m_conv_ln_g, m_conv_ln_b],
        [v_mix_pre_g, v_mix_post_g, v_ffn_pre_g, v_ffn_post_g, v_pool_scale, v_conv_b, v_conv_ln_g, v_conv_ln_b],
        rep_all, rep_layout, loss_at)
    loss = loss_sum[0, 0] * (0.5 / d)
    o_sc_out = upd("adam_sc_out", sc_w_out, m_sc_w_out, v_sc_w_out, [reduce_end("sc_out", o_ff1[0])])
    o_sc_in = upd("adam_sc_in", sc_w_in, m_sc_w_in, v_sc_w_in, [reduce_end("sc_in", o_sc_out[0])])
    o_ab_out = upd("adam_ab_out", ab_w_out, m_ab_w_out, v_ab_w_out, [reduce_end("ab_out", o_sc_in[0])])
    o_small = adamw("adam_small", pack_small(conv_w, sc_conv_w, pool_w), pack_small(m_conv_w, m_sc_conv_w, m_pool_w),
                    pack_small(v_conv_w, v_sc_conv_w, v_pool_w), [reduce_end("small", o_ab_out[0])])
    o_ab_in = upd("adam_ab_in", ab_w_in, m_ab_w_in, v_ab_w_in, [reduce_end("ab_in", o_small[0])])

    def unpack_small(o):
        return o[:r0].reshape(conv_w.shape), o[r0:r1].reshape(sc_conv_w.shape), o[r1:r2].reshape(pool_w.shape)

    results = []
    for kind in range(4):
        g_mix_pre, g_mix_post, g_ffn_pre, g_ffn_post, g_scale, g_conv_b, g_ln_g, g_ln_b = o_rep[kind::4]
        s_conv, s_sconv, s_pool = unpack_small(o_small[kind])
        results.append([
            g_mix_pre, g_mix_post, g_ffn_pre, g_ffn_post,
            o_ab_in[kind], s_pool, g_scale, s_conv, g_conv_b, g_ln_g, g_ln_b,
            o_ab_out[kind], o_sc_in[kind], s_sconv, o_sc_out[kind], o_ff1[kind], o_ff2[kind]])

    return (loss, grad_x[None], *results[0], *results[1], *results[2], *results[3])
```

```python
import jax
import jax.numpy as jnp
from jax import lax
from jax.experimental import pallas as pl
from jax.experimental.pallas import tpu as pltpu

F32 = jnp.float32
BF16 = jnp.bfloat16
MESH = pl.DeviceIdType.MESH
ANY = pl.BlockSpec(memory_space=pl.ANY)

NORM_EPS = 1e-6
POOL_WINDOWS = (2, 4, 8, 16)
ADAM_LR = 0.001
ADAM_B1 = 0.9
ADAM_B2 = 0.999
ADAM_EPS = 1e-08
ADAM_WD = 0.01
ADAM_STEP = 10

N_DEV = 8
VMEM_LIMIT = 56 * 1024 * 1024
PAIR_ADD_BLOCK = 1 << 20
MATMUL_ROWS = 2048
UP_FIRST_BLOCKS = 6
ROW_TILE = 256
CHANNEL_TILE = 256
TIME_CHUNK = 64
HALO = 32

NN = (((1,), (0,)), ((), ()))
NT = (((1,), (1,)), ((), ()))
TN = (((0,), (0,)), ((), ()))


def _params(sem):
    return pltpu.CompilerParams(dimension_semantics=sem, vmem_limit_bytes=VMEM_LIMIT)


def _place():
    x, y, c = lax.axis_index("x"), lax.axis_index("y"), lax.axis_index("c")
    return x, y, c


def _slot(px, py, pc):
    return 4 * px + 2 * py + pc


HBM = pl.BlockSpec(memory_space=pltpu.HBM)
SEM = pl.BlockSpec(memory_space=pltpu.SEMAPHORE)
EFFECT = pltpu.SideEffectType.DATAFLOW_SIDE_EFFECTING
TOKEN = jax.ShapeDtypeStruct((8, 128), F32)


def _in_hbm(a):
    return pltpu.with_memory_space_constraint(a, pltpu.HBM)


CHIPS = [(0, 0), (0, 1), (1, 0), (1, 1)]
N_CHIP = len(CHIPS)


def _chip(px, py):
    return 2 * px + py


def _first_hop(bufs, sends, recvs, waiting):
    (land,) = bufs
    x, y, c = _place()
    me = _slot(x, y, c)
    peers = [(x, y, 1 - c), (1 - x, y, c), (x, 1 - y, c), (1 - x, 1 - y, c)]
    return [pltpu.make_async_remote_copy(
        src_ref=land.at[me], dst_ref=land.at[_slot(*p) if waiting else me],
        send_sem=sends.at[k], recv_sem=recvs.at[k], device_id=p, device_id_type=MESH) for k, p in enumerate(peers)]


def _second_hop(bufs, sends, recvs, waiting):
    (land,) = bufs
    x, y, c = _place()
    return [pltpu.make_async_remote_copy(
        src_ref=land.at[_slot(px, py, c)], dst_ref=land.at[_slot(px, py, 1 - c if waiting else c)],
        send_sem=sends.at[k], recv_sem=recvs.at[k], device_id=(x, y, 1 - c), device_id_type=MESH)
        for k, (px, py) in enumerate([(1 - x, y), (x, 1 - y), (1 - x, 1 - y)])]


def _ring_hop1(bufs, sends, recvs, waiting):
    (land,) = bufs
    x, y, c = _place()
    me = _slot(x, y, c)
    peers = [(1 - x, y, c), (x, 1 - y, c), (x, y, 1 - c)]
    return [pltpu.make_async_remote_copy(
        src_ref=land.at[me], dst_ref=land.at[_slot(*p) if waiting else me],
        send_sem=sends.at[k], recv_sem=recvs.at[k], device_id=p, device_id_type=MESH) for k, p in enumerate(peers)]


def _ring_hop2(bufs, sends, recvs, waiting):
    (land,) = bufs
    x, y, c = _place()
    half = land.shape[1] // 2
    first, second = pl.ds(0, half), pl.ds(half, half)
    nx, ny, diag = _slot(1 - x, y, c), _slot(x, 1 - y, c), _slot(1 - x, 1 - y, c)
    plan = [
        (land.at[ny, first], land.at[diag, first], (1 - x, y, c)),
        (land.at[nx, second], land.at[diag, second], (x, 1 - y, c)),
        (land.at[nx], land.at[_slot(1 - x, y, 1 - c)], (x, y, 1 - c)),
        (land.at[ny], land.at[_slot(x, 1 - y, 1 - c)], (x, y, 1 - c))]
    return [pltpu.make_async_remote_copy(
        src_ref=src, dst_ref=mine if waiting else src, send_sem=sends.at[k], recv_sem=recvs.at[k],
        device_id=to, device_id_type=MESH) for k, (src, mine, to) in enumerate(plan)]


def _ring_hop3(bufs, sends, recvs, waiting):
    (land,) = bufs
    x, y, c = _place()
    return [pltpu.make_async_remote_copy(
        src_ref=land.at[_slot(1 - x, 1 - y, c)], dst_ref=land.at[_slot(1 - x, 1 - y, 1 - c if waiting else c)],
        send_sem=sends.at[0], recv_sem=recvs.at[0], device_id=(x, y, 1 - c), device_id_type=MESH)]


def _pair_hop(bufs, sends, recvs, waiting):
    g, land = bufs
    x, y, c = _place()
    return [pltpu.make_async_remote_copy(
        src_ref=g.at[_slot(qx, qy, 1 - c)], dst_ref=land.at[q],
        send_sem=sends.at[q], recv_sem=recvs.at[q], device_id=(x, y, 1 - c), device_id_type=MESH)
        for q, (qx, qy) in enumerate(CHIPS)]


def _chip_hop(bufs, sends, recvs, waiting):
    p, land = bufs
    x, y, c = _place()
    return [pltpu.make_async_remote_copy(
        src_ref=p.at[_chip(px, py)], dst_ref=land.at[_chip(px, py) if waiting else _chip(x, y)],
        send_sem=sends.at[k], recv_sem=recvs.at[k], device_id=(px, py, c), device_id_type=MESH)
        for k, (px, py) in enumerate([(1 - x, y), (x, 1 - y), (1 - x, 1 - y)])]


def copies_start(name, groups, hop, n_copies, deps=()):
    flat = [b for grp in groups for b in grp]
    nb, ng = len(flat), len(groups)
    deps = list(deps)
    hops = list(hop) if isinstance(hop, (list, tuple)) else [hop] * ng
    counts = list(n_copies) if isinstance(n_copies, (list, tuple)) else [n_copies] * ng

    def body(*refs):
        ins, token = refs[:nb], refs[-1]
        sems = refs[nb + len(deps):nb + len(deps) + 2 * ng]
        i = 0
        for gi, grp in enumerate(groups):
            for cp in hops[gi](ins[i:i + len(grp)], sems[2 * gi], sems[2 * gi + 1], False):
                cp.start()
            i += len(grp)
        token[...] = jnp.zeros_like(token)

    outs = pl.pallas_call(
        body, name=name,
        out_shape=([pltpu.SemaphoreType.DMA((n,)) for n in counts for _ in range(2)]
                   + [pltpu.HBM(b.shape, b.dtype) for b in flat] + [TOKEN]),
        in_specs=[HBM] * nb + [ANY] * len(deps),
        out_specs=[SEM] * (2 * ng) + [HBM] * nb + [pl.BlockSpec(memory_space=pltpu.VMEM)],
        input_output_aliases={i: 2 * ng + i for i in range(nb)},
        compiler_params=pltpu.CompilerParams(has_side_effects=EFFECT),
    )(*[_in_hbm(b) for b in flat], *deps)
    started, i = [], 0
    for gi, grp in enumerate(groups):
        started.append((outs[2 * gi], outs[2 * gi + 1], list(outs[2 * ng + i:2 * ng + i + len(grp)])))
        i += len(grp)
    return started, outs[-1]


def copies_wait(name, started, hop, after):
    sends, recvs, bufs = started
    nb = len(bufs)

    def body(*refs):
        for cp in hop(refs[:nb], refs[nb], refs[nb + 1], True):
            cp.wait_send()
            cp.wait_recv()

    outs = pl.pallas_call(
        body, name=name,
        out_shape=[pltpu.HBM(b.shape, b.dtype) for b in bufs],
        in_specs=[HBM] * nb + [SEM, SEM, ANY], out_specs=[HBM] * nb,
        input_output_aliases={i: i for i in range(nb)},
        compiler_params=pltpu.CompilerParams(has_side_effects=EFFECT),
    )(*bufs, sends, recvs, after)
    return list(outs)


def place_shard(name, w, layer, dtype, deps=()):
    _, r, c = w.shape
    tr = _tile(r, 1024)
    x, y, core = _place()
    me = _slot(x, y, core).astype(jnp.int32).reshape(1)

    def body(me_ref, w_ref, *rest):
        rest[-1][...] = w_ref[...].astype(dtype)

    return pl.pallas_call(
        body, name=name,
        grid_spec=pltpu.PrefetchScalarGridSpec(
            num_scalar_prefetch=1, grid=(r // tr,),
            in_specs=[pl.BlockSpec((None, tr, c), lambda i, me_ref: (layer, i, 0))] + [ANY] * len(deps),
            out_specs=pl.BlockSpec((None, tr, c), lambda i, me_ref: (me_ref[0], i, 0))),
        out_shape=jax.ShapeDtypeStruct((N_DEV, r, c), dtype),
        compiler_params=_params(("parallel",)),
    )(me, w, *deps)


def place_sheet(name, pieces, total_of, total_at, rows, width):
    x, y, core = _place()
    me = _slot(x, y, core).astype(jnp.int32).reshape(1)

    def body(me_ref, *refs):
        o_ref = refs[-1]
        o_ref[...] = jnp.zeros_like(o_ref)
        for ref, (_, row, lane) in zip(refs, pieces):
            o_ref[row:row + 1, lane:lane + ref.shape[1]] = jnp.sum(ref[...], axis=0, keepdims=True)
        total = jnp.sum(jnp.sum(refs[len(pieces)][...], axis=0, keepdims=True), axis=1, keepdims=True)
        o_ref[total_at[0]:total_at[0] + 1, total_at[1]:total_at[1] + 128] = jnp.broadcast_to(total, (1, 128))

    arrays = [a for a, _, _ in pieces] + [total_of]
    return pl.pallas_call(
        body, name=name,
        grid_spec=pltpu.PrefetchScalarGridSpec(
            num_scalar_prefetch=1, grid=(1,),
            in_specs=[pl.BlockSpec(a.shape, lambda i, me_ref: (0, 0)) for a in arrays],
            out_specs=pl.BlockSpec((None, rows, width), lambda i, me_ref: (me_ref[0], 0, 0))),
        out_shape=jax.ShapeDtypeStruct((N_DEV, rows, width), F32),
    )(me, *arrays)


def tie(name, x, *deps):
    def body(*refs):
        del refs

    return pl.pallas_call(
        body, name=name, out_shape=jax.ShapeDtypeStruct(x.shape, x.dtype),
        in_specs=[ANY] * (1 + len(deps)), out_specs=ANY, input_output_aliases={0: 0},
    )(x, *deps)


def pair_add(name, g, from_sibling):
    _, r, c_dim = g.shape
    tr = r
    while tr * c_dim > PAIR_ADD_BLOCK and tr % 16 == 0:
        tr //= 2
    x, y, core = _place()
    where = jnp.stack([core, _chip(x, y)]).astype(jnp.int32)

    def body(where_ref, g_ref, s_ref, o_ref, zone_ref):
        total = (g_ref[...].astype(F32) + s_ref[...].astype(F32)).astype(o_ref.dtype)
        o_ref[...] = total

        @pl.when(pl.program_id(1) == where_ref[1])
        def _():
            zone_ref[...] = total

    blk = pl.BlockSpec((None, tr, c_dim), lambda i, q, where_ref: (q, i, 0))
    return pl.pallas_call(
        body, name=name,
        grid_spec=pltpu.PrefetchScalarGridSpec(
            num_scalar_prefetch=1, grid=(r // tr, N_CHIP),
            in_specs=[pl.BlockSpec((None, None, tr, c_dim), lambda i, q, where_ref: (q, where_ref[0], i, 0)), blk],
            out_specs=[blk, pl.BlockSpec((None, tr, c_dim), lambda i, q, where_ref: (where_ref[1], i, 0))]),
        out_shape=[jax.ShapeDtypeStruct((N_CHIP, r, c_dim), g.dtype)] * 2,
        compiler_params=_params(("parallel", "arbitrary")),
    )(where, g.reshape(N_CHIP, 2, r, c_dim), from_sibling)


def _matmul(name, lhs, rhs, *, out_shape, out_dtype, grid, lhs_spec, rhs_spec, out_spec, acc_shape,
            lhs_fn=None, epilogue=None, into=None):
    nk = grid[2]
    extra = [] if into is None else [into]

    def body(lhs_ref, rhs_ref, *rest):
        out_ref, scratch = rest[len(extra)], rest[len(extra) + 1:]

        def product():
            a = lhs_ref[...]
            if lhs_fn is not None:
                a = lhs_fn(a)
            return lax.dot_general(a, rhs_ref[...], NN, preferred_element_type=F32)

        def finish(r):
            if epilogue is not None:
                r = epilogue(r)
            out_ref[...] = r.astype(out_dtype)

        if nk == 1:
            finish(product())
        else:
            (acc_ref,) = scratch
            k = pl.program_id(2)

            @pl.when(k == 0)
            def _():
                acc_ref[...] = product()

            @pl.when(jnp.logical_and(k > 0, k < nk - 1))
            def _():
                acc_ref[...] += product()

            @pl.when(k == nk - 1)
            def _():
                finish(acc_ref[...] + product())

    return pl.pallas_call(
        body, name=name, grid=grid,
        out_shape=jax.ShapeDtypeStruct(out_shape, out_dtype),
        in_specs=[lhs_spec, rhs_spec] + [ANY] * len(extra), out_specs=out_spec,
        input_output_aliases={2: 0} if extra else {},
        scratch_shapes=[pltpu.VMEM(acc_shape, F32)] if nk > 1 else [],
        compiler_params=_params(("parallel", "parallel", "arbitrary")),
    )(lhs, rhs, *extra)


def _tile(n, want):
    return want if n % want == 0 else n


def mm_nn(name, x, w, *, out_dtype, tn=512, tk=None, lhs_fn=None, epilogue=None):
    t, kdim = x.shape
    n = w.shape[1]
    tm, tn = _tile(t, MATMUL_ROWS), _tile(n, tn)
    tk = kdim if tk is None else _tile(kdim, tk)
    return _matmul(
        name, x, w, out_shape=(t, n), out_dtype=out_dtype, grid=(t // tm, n // tn, kdim // tk),
        lhs_spec=pl.BlockSpec((tm, tk), lambda i, j, k: (i, k)),
        rhs_spec=pl.BlockSpec((tk, tn), lambda i, j, k: (k, j)),
        out_spec=pl.BlockSpec((tm, tn), lambda i, j, k: (i, j)),
        acc_shape=(tm, tn), lhs_fn=lhs_fn, epilogue=epilogue)


def mm_nn_blocked(name, x, w, *, out_dtype, epilogue=None, blocks=(0, N_DEV), into=None):
    t, kdim = x.shape
    nb = w.shape[2]
    tm = _tile(t, MATMUL_ROWS)
    tn = nb // 2 if nb >= 1024 else nb
    sub = nb // tn
    first, count = blocks
    return _matmul(
        name, x, w, out_shape=(t, N_DEV * nb), out_dtype=out_dtype, grid=(t // tm, count * sub, 1),
        lhs_spec=pl.BlockSpec((tm, kdim), lambda i, j, k: (i, k)),
        rhs_spec=pl.BlockSpec((None, kdim, tn), lambda i, j, k: (first + j // sub, k, j % sub)),
        out_spec=pl.BlockSpec((tm, tn), lambda i, j, k: (i, first * sub + j)),
        acc_shape=(tm, tn), epilogue=epilogue, into=into)


def mm_bwd_pair(name, dy, w, act, *, out_dtype, tile=512, act_fn=None, epilogue=None):
    t, n = dy.shape
    kdim = w.shape[0]
    tile = _tile(kdim, tile)

    def body(dy_ref, w_ref, act_ref, dx_ref, dw_ref):
        a = act_ref[...]
        dx = lax.dot_general(dy_ref[...], w_ref[...], NT, preferred_element_type=F32)
        if epilogue is not None:
            dx = epilogue(dx, a)
        dx_ref[...] = dx.astype(out_dtype)
        if act_fn is not None:
            a = act_fn(a)
        dw_ref[...] = lax.dot_general(a, dy_ref[...], TN, preferred_element_type=F32).astype(out_dtype)

    return pl.pallas_call(
        body, name=name, grid=(kdim // tile,),
        in_specs=[pl.BlockSpec((t, n), lambda j: (0, 0)), pl.BlockSpec((tile, n), lambda j: (j, 0)),
                  pl.BlockSpec((t, tile), lambda j: (0, j))],
        out_specs=[pl.BlockSpec((t, tile), lambda j: (0, j)), pl.BlockSpec((tile, n), lambda j: (j, 0))],
        out_shape=[jax.ShapeDtypeStruct((t, kdim), out_dtype), jax.ShapeDtypeStruct((kdim, n), out_dtype)],
        compiler_params=_params(("parallel",)),
    )(dy, w, act)


def mm_bwd_pair_blocked(name, dz, w, act, *, out_dtype, tile=1024):
    t = dz.shape[0]
    kdim, nb = w.shape[1], w.shape[2]
    tile = _tile(kdim, tile)

    def body(dz_ref, w_ref, act_ref, dx_ref, dw_ref, acc_ref):
        j = pl.program_id(1)
        dw_ref[...] = lax.dot_general(act_ref[...], dz_ref[...], TN, preferred_element_type=F32).astype(out_dtype)

        def product():
            return lax.dot_general(dz_ref[...], w_ref[...], NT, preferred_element_type=F32)

        @pl.when(j == 0)
        def _():
            acc_ref[...] = product()

        @pl.when(jnp.logical_and(j > 0, j < N_DEV - 1))
        def _():
            acc_ref[...] += product()

        @pl.when(j == N_DEV - 1)
        def _():
            dx_ref[...] = (acc_ref[...] + product()).astype(out_dtype)

    return pl.pallas_call(
        body, name=name, grid=(kdim // tile, N_DEV),
        in_specs=[pl.BlockSpec((t, nb), lambda i, j: (0, j)), pl.BlockSpec((None, tile, nb), lambda i, j: (j, i, 0)),
                  pl.BlockSpec((t, tile), lambda i, j: (0, i))],
        out_specs=[pl.BlockSpec((t, tile), lambda i, j: (0, i)),
                   pl.BlockSpec((None, tile, nb), lambda i, j: (j, i, 0))],
        out_shape=[jax.ShapeDtypeStruct((t, kdim), out_dtype), jax.ShapeDtypeStruct((N_DEV, kdim, nb), out_dtype)],
        scratch_shapes=[pltpu.VMEM((t, tile), F32)],
        compiler_params=_params(("parallel", "arbitrary")),
    )(dz, w, act)


def _rstd(v):
    return lax.rsqrt(jnp.mean(v * v, axis=-1, keepdims=True) + NORM_EPS)


def _rms_bwd(v, g, dy):
    r = _rstd(v)
    vhat = v * r
    dvh = dy * g
    dv = r * (dvh - vhat * jnp.mean(dvh * vhat, axis=-1, keepdims=True))
    return dv, dy * vhat


def _fold8(v):
    rows, n = v.shape
    return jnp.sum(v.reshape(rows // 8, 8, n), axis=0)


def _fold_lanes(v):
    out = v[:, 0:128]
    for i in range(1, v.shape[1] // 128):
        out = out + v[:, 128 * i:128 * (i + 1)]
    return out


def _accumulate(ref, v):
    i = pl.program_id(0)

    @pl.when(i == 0)
    def _():
        ref[...] = v

    @pl.when(i > 0)
    def _():
        ref[...] += v


def _row_call(body, name, t, ins, row_in, outs, acc_outs=(), tr=ROW_TILE):
    tr = _tile(t, tr)

    def in_spec(a, tiled):
        if isinstance(tiled, tuple):
            width, j = tiled
            return pl.BlockSpec((tr, width), lambda i: (i, j))
        return pl.BlockSpec((tr, a.shape[1]), lambda i: (i, 0)) if tiled else pl.BlockSpec(a.shape, lambda i: (0, 0))

    in_specs = [in_spec(a, tiled) for a, tiled in zip(ins, row_in)]
    out_specs = [pl.BlockSpec((tr, n), lambda i: (i, 0)) for n, _ in outs]
    out_specs += [pl.BlockSpec((8, n), lambda i: (0, 0)) for n in acc_outs]
    out_shape = [jax.ShapeDtypeStruct((t, n), dt) for n, dt in outs]
    out_shape += [jax.ShapeDtypeStruct((8, n), F32) for n in acc_outs]
    return pl.pallas_call(
        body, name=name, grid=(t // tr,), in_specs=in_specs, out_specs=out_specs, out_shape=out_shape,
        compiler_params=_params(("arbitrary",) if acc_outs else ("parallel",)),
    )(*ins)


def norm_pre(name, x, g):
    t, d = x.shape

    def body(x_ref, g_ref, h_ref):
        v = x_ref[...]
        h_ref[...] = (v * _rstd(v) * g_ref[...]).astype(BF16)

    return _row_call(body, name, t, [x, g], [True, False], [(d, BF16)])[0]


def post_pre(name, x, m, g_post, g_pre):
    t, d = x.shape

    def body(x_ref, m_ref, gp_ref, gn_ref, xo_ref, h_ref):
        mv = m_ref[...]
        xn = x_ref[...] + mv * _rstd(mv) * gp_ref[...]
        xo_ref[...] = xn
        h_ref[...] = (xn * _rstd(xn) * gn_ref[...]).astype(BF16)

    return _row_call(body, name, t, [x, m, g_post, g_pre], [True, True, False, False], [(d, F32), (d, BF16)])


def post_loss(name, x, f, g_post, target):
    t, d = x.shape

    def body(x_ref, f_ref, g_ref, t_ref, dx_ref, df_ref, loss_ref, dg_ref):
        fv = f_ref[...]
        g = g_ref[...]
        out = x_ref[...] + fv * _rstd(fv) * g
        err = out - t_ref[...]
        dx = err * (1.0 / d)
        dx_ref[...] = dx
        dfv, dg_rows = _rms_bwd(fv, g, dx)
        df_ref[...] = dfv.astype(BF16)
        _accumulate(loss_ref, _fold8(_fold_lanes(err * err)))
        _accumulate(dg_ref, _fold8(dg_rows))

    return _row_call(body, name, t, [x, f, g_post, target], [True, True, False, True],
                     [(d, F32), (d, BF16)], acc_outs=(128, d))


def bwd_pre_post(name, dx_out, x_in, g_pre, dh, f_prev, g_post_prev):
    t, d = x_in.shape

    def body(dxo_ref, x_ref, gpre_ref, dh_ref, f_ref, gpost_ref, dxi_ref, df_ref, dgpre_ref, dgpost_ref):
        dxv, dgpre_rows = _rms_bwd(x_ref[...], gpre_ref[...], dh_ref[...].astype(F32))
        dxi = dxo_ref[...] + dxv
        dxi_ref[...] = dxi
        dfv, dgpost_rows = _rms_bwd(f_ref[...], gpost_ref[...], dxi)
        df_ref[...] = dfv.astype(BF16)
        _accumulate(dgpre_ref, _fold8(dgpre_rows))
        _accumulate(dgpost_ref, _fold8(dgpost_rows))

    return _row_call(body, name, t, [dx_out, x_in, g_pre, dh, f_prev, g_post_prev],
                     [True, True, False, True, True, False], [(d, F32), (d, BF16)], acc_outs=(d, d))


def bwd_pre_final(name, dx_out, x_in, g_pre, dh):
    t, d = x_in.shape

    def body(dxo_ref, x_ref, gpre_ref, dh_ref, dxi_ref, dgpre_ref):
        dxv, dgpre_rows = _rms_bwd(x_ref[...], gpre_ref[...], dh_ref[...].astype(F32))
        dxi_ref[...] = dxo_ref[...] + dxv
        _accumulate(dgpre_ref, _fold8(dgpre_rows))

    return _row_call(body, name, t, [dx_out, x_in, g_pre, dh], [True, True, False, True], [(d, F32)], acc_outs=(d,))


def _layer_norm_parts(cv):
    mu = jnp.mean(cv, axis=-1, keepdims=True)
    xc = cv - mu
    rstd = lax.rsqrt(jnp.mean(xc * xc, axis=-1, keepdims=True) + NORM_EPS)
    return xc * rstd, rstd


def ln_silu(name, cv, g, b, y, y_block):
    t, n = cv.shape
    tr = _tile(t, ROW_TILE)

    def body(c_ref, g_ref, b_ref, y_in_ref, y_ref):
        chat, _ = _layer_norm_parts(c_ref[...])
        ln = chat * g_ref[...] + b_ref[...]
        y_ref[...] = (ln * jax.nn.sigmoid(ln)).astype(BF16)

    vec = pl.BlockSpec((1, n), lambda i: (0, 0))
    return pl.pallas_call(
        body, name=name, grid=(t // tr,),
        in_specs=[pl.BlockSpec((tr, n), lambda i: (i, 0)), vec, vec, ANY],
        out_specs=pl.BlockSpec((tr, n), lambda i: (i, y_block)),
        out_shape=jax.ShapeDtypeStruct(y.shape, y.dtype), input_output_aliases={3: 0},
        compiler_params=_params(("parallel",)),
    )(cv, g, b, y)


def ln_silu_bwd(name, cv, g, b, dy, dy_block):
    t, n = cv.shape

    def body(c_ref, g_ref, b_ref, dy_ref, dc_ref, dg_ref, db_ref):
        chat, rstd = _layer_norm_parts(c_ref[...])
        g = g_ref[...]
        ln = chat * g + b_ref[...]
        s = jax.nn.sigmoid(ln)
        dln = dy_ref[...].astype(F32) * (s * (1.0 + ln * (1.0 - s)))
        dchat = dln * g
        dc_ref[...] = rstd * (dchat - jnp.mean(dchat, axis=-1, keepdims=True)
                              - chat * jnp.mean(dchat * chat, axis=-1, keepdims=True))
        _accumulate(dg_ref, _fold8(dln * chat))
        _accumulate(db_ref, _fold8(dln))

    return _row_call(body, name, t, [cv, g, b, dy], [True, False, False, (n, dy_block)], [(n, F32)], acc_outs=(n, n))


def _chunks(t, fn, tc=TIME_CHUNK):
    tc = _tile(t, tc)

    def step(i, carry):
        fn(pl.multiple_of(i * tc, tc), tc)
        return carry

    lax.fori_loop(0, t // tc, step, 0)


def _rows_from(v, start, n):
    res = start % 8
    base = v if res == 0 else pltpu.roll(v, v.shape[0] - res, axis=0)
    return base[start - res:start - res + n, :]


def _shifted(window, offsets, tc):
    rows = window.shape[0]
    by_residue = {}
    for k, off in enumerate(offsets):
        by_residue.setdefault(off % 8, []).append((k, off))
    for res, taps in by_residue.items():
        base = window if res == 0 else pltpu.roll(window, rows - res, axis=0)
        for k, off in taps:
            yield k, base[off - res:off - res + tc, :]


def _taps(window, w_ref, offsets, tc, flip=False):
    acc = None
    for k, rows in _shifted(window, offsets, tc):
        kk = len(offsets) - 1 - k if flip else k
        term = w_ref[kk:kk + 1, :] * rows
        acc = term if acc is None else acc + term
    return acc


def _window_sums(win, tc, causal):
    sums = []
    cur, rows, step = win, tc + HALO, 1
    for _ in POOL_WINDOWS:
        rows -= 8
        if causal:
            cur = cur[8:8 + rows, :] + _rows_from(cur, 8 - step, rows)
            sums.append(cur[rows - tc:rows, :])
        else:
            cur = cur[0:rows, :] + _rows_from(cur, step, rows)
            sums.append(cur[0:tc, :])
        step *= 2
    return sums


def _pick(vals, g):
    out = vals[-1]
    for i in range(len(vals) - 2, -1, -1):
        out = jnp.where(g == i, vals[i], out)
    return out


def _pool_count(s, tc, g):
    t1 = (lax.broadcasted_iota(jnp.int32, (tc, 1), 0) + (s + 1)).astype(F32)
    width = _pick([float(w) for w in POOL_WINDOWS], g)
    return jnp.minimum(t1, width)


def pool_fwd(name, z, pool_w, pool_scale, d_pool, y_width):
    t = z.shape[0]
    ng, pg = pool_w.shape[0], pool_w.shape[1]

    def body(u_ref, w_ref, s_ref, pooled_ref, y_ref, pad):
        g = pl.program_id(0)
        pad[pl.ds(0, HALO), :] = jnp.zeros((HALO, pg), F32)

        def fill(s, tc):
            pad[pl.ds(HALO + s, tc), :] = u_ref[pl.ds(s, tc), :].astype(F32)

        def chunk(s, tc):
            win = pad[pl.ds(s, tc + HALO), :]
            total = _pick(_window_sums(win, tc, causal=True), g)
            pooled = total / _pool_count(s, tc, g) - win[HALO:HALO + tc, :]
            pooled_ref[pl.ds(s, tc), :] = pooled.astype(BF16)

        _chunks(t, fill)
        _chunks(t, chunk)
        mixed = jnp.dot(pooled_ref[...], w_ref[...], preferred_element_type=F32)
        y_ref[...] = (mixed * s_ref[...]).astype(BF16)

    col = pl.BlockSpec((t, pg), lambda g: (0, g))
    return pl.pallas_call(
        body, name=name, grid=(ng,),
        in_specs=[col, pl.BlockSpec((None, pg, pg), lambda g: (g, 0, 0)), pl.BlockSpec((1, pg), lambda g: (0, g))],
        out_specs=[col, col],
        out_shape=[jax.ShapeDtypeStruct((t, d_pool), BF16), jax.ShapeDtypeStruct((t, y_width), BF16)],
        scratch_shapes=[pltpu.VMEM((t + HALO, pg), F32)],
        compiler_params=_params(("parallel",)),
    )(z, pool_w, pool_scale)


def pool_bwd(name, pooled, dy, pool_w, pool_scale, dz):
    t, d_pool = pooled.shape
    ng, pg = pool_w.shape[0], pool_w.shape[1]

    def body(p_ref, dy_ref, w_ref, s_ref, dz_ref, du_ref, dw_ref, ds_ref, pad):
        g = pl.program_id(0)
        w = w_ref[...]
        dyv = dy_ref[...].astype(F32)
        mixed = jnp.dot(p_ref[...], w, preferred_element_type=F32)
        ds_ref[...] = jnp.sum(dyv * mixed, axis=0, keepdims=True)
        dmixed = (dyv * s_ref[...]).astype(BF16)
        dw_ref[...] = lax.dot_general(p_ref[...], dmixed, TN, preferred_element_type=F32)
        pad[...] = jnp.zeros((t + HALO, pg), F32)
        pad[pl.ds(0, t), :] = lax.dot_general(dmixed, w, NT, preferred_element_type=F32)

        def scale(s, tc):
            pad[pl.ds(s, tc), :] = pad[pl.ds(s, tc), :] / _pool_count(s, tc, g)

        def chunk(s, tc):
            win = pad[pl.ds(s, tc + HALO), :]
            total = _pick(_window_sums(win, tc, causal=False), g)
            du_ref[pl.ds(s, tc), :] = (total - win[0:tc, :] * _pool_count(s, tc, g)).astype(BF16)

        _chunks(t, scale)
        _chunks(t, chunk)

    col = pl.BlockSpec((t, pg), lambda g: (0, g))
    vec = pl.BlockSpec((1, pg), lambda g: (0, g))
    mat = pl.BlockSpec((None, pg, pg), lambda g: (g, 0, 0))
    return pl.pallas_call(
        body, name=name, grid=(ng,),
        in_specs=[col, col, mat, vec, ANY], out_specs=[col, mat, vec],
        out_shape=[jax.ShapeDtypeStruct(dz.shape, dz.dtype), jax.ShapeDtypeStruct((ng, pg, pg), F32),
                   jax.ShapeDtypeStruct((1, d_pool), F32)],
        input_output_aliases={4: 0},
        scratch_shapes=[pltpu.VMEM((t + HALO, pg), F32)],
        compiler_params=_params(("parallel",)),
    )(pooled, dy, pool_w, pool_scale, dz)


def conv_fwd(name, z, conv_w, conv_b, d_pool, d_conv):
    t = z.shape[0]
    kw = conv_w.shape[0]
    tc_ch = _tile(d_conv, CHANNEL_TILE)
    v0, g0 = d_pool // tc_ch, (d_pool + d_conv) // tc_ch

    def body(v_ref, g_ref, w_ref, b_ref, c_ref, pad):
        pad[pl.ds(0, HALO), :] = jnp.zeros((HALO, tc_ch), F32)

        def fill(s, tc):
            pad[pl.ds(HALO + s, tc), :] = v_ref[pl.ds(s, tc), :].astype(F32) * jax.nn.sigmoid(g_ref[pl.ds(s, tc), :].astype(F32))

        def chunk(s, tc):
            win = pad[pl.ds(s, tc + HALO), :]
            c_ref[pl.ds(s, tc), :] = _taps(win, w_ref, [HALO - (kw - 1) + k for k in range(kw)], tc) + b_ref[...]

        _chunks(t, fill)
        _chunks(t, chunk)

    return pl.pallas_call(
        body, name=name, grid=(d_conv // tc_ch,),
        in_specs=[pl.BlockSpec((t, tc_ch), lambda j: (0, v0 + j)), pl.BlockSpec((t, tc_ch), lambda j: (0, g0 + j)),
                  pl.BlockSpec((kw, tc_ch), lambda j: (0, j)), pl.BlockSpec((1, tc_ch), lambda j: (0, j))],
        out_specs=pl.BlockSpec((t, tc_ch), lambda j: (0, j)),
        out_shape=jax.ShapeDtypeStruct((t, d_conv), F32),
        scratch_shapes=[pltpu.VMEM((t + HALO, tc_ch), F32)],
        compiler_params=_params(("parallel",)),
    )(z, z, conv_w, conv_b)


def conv_bwd(name, z, dc, conv_w, d_pool, d_conv):
    t = z.shape[0]
    kw = conv_w.shape[0]
    tc_ch = _tile(d_conv, CHANNEL_TILE)
    v0, g0 = d_pool // tc_ch, (d_pool + d_conv) // tc_ch

    def body(v_ref, g_ref, dc_ref, w_ref, dz_ref, dw_ref, db_ref, pad_a, pad_dc, acc_w, acc_b, tiles, sems):
        j = pl.program_id(0)
        dv_ref, dg_ref = tiles.at[0], tiles.at[1]
        writes = [pltpu.make_async_copy(tiles.at[p], dz_ref.at[:, pl.ds((first + j) * tc_ch, tc_ch)], sems.at[p])
                  for p, first in enumerate([v0, g0])]

        def wait_writes():
            for cp in writes:
                cp.wait()

        pad_a[pl.ds(0, HALO), :] = jnp.zeros((HALO, tc_ch), F32)
        pad_dc[pl.ds(t, HALO), :] = jnp.zeros((HALO, tc_ch), F32)
        acc_w[...] = jnp.zeros_like(acc_w)
        acc_b[...] = jnp.zeros_like(acc_b)

        def fill(s, tc):
            pad_a[pl.ds(HALO + s, tc), :] = v_ref[pl.ds(s, tc), :].astype(F32) * jax.nn.sigmoid(g_ref[pl.ds(s, tc), :].astype(F32))
            pad_dc[pl.ds(s, tc), :] = dc_ref[pl.ds(s, tc), :]

        def chunk(s, tc):
            dcv = pad_dc[pl.ds(s, tc), :]
            win_a = pad_a[pl.ds(s, tc + HALO), :]
            for k, rows in _shifted(win_a, [HALO - (kw - 1) + k for k in range(kw)], tc):
                acc_w[pl.ds(8 * k, 8), :] += _fold8(dcv * rows)
            acc_b[...] += _fold8(dcv)
            da = _taps(pad_dc[pl.ds(s, tc + HALO), :], w_ref, list(range(kw)), tc, flip=True)
            vv = v_ref[pl.ds(s, tc), :].astype(F32)
            sg = jax.nn.sigmoid(g_ref[pl.ds(s, tc), :].astype(F32))
            dv_ref[pl.ds(s, tc), :] = (da * sg).astype(BF16)
            dg_ref[pl.ds(s, tc), :] = (da * vv * sg * (1.0 - sg)).astype(BF16)

        _chunks(t, fill)
        pl.when(j > 0)(wait_writes)
        _chunks(t, chunk)
        for cp in writes:
            cp.start()
        pl.when(j == n_tiles - 1)(wait_writes)
        for k in range(kw):
            dw_ref[k:k + 1, :] = jnp.sum(acc_w[pl.ds(8 * k, 8), :], axis=0, keepdims=True)
        db_ref[...] = jnp.sum(acc_b[...], axis=0, keepdims=True)

    n_tiles = d_conv // tc_ch
    return pl.pallas_call(
        body, name=name, grid=(n_tiles,),
        in_specs=[pl.BlockSpec((t, tc_ch), lambda j: (0, v0 + j)), pl.BlockSpec((t, tc_ch), lambda j: (0, g0 + j)),
                  pl.BlockSpec((t, tc_ch), lambda j: (0, j)), pl.BlockSpec((kw, tc_ch), lambda j: (0, j))],
        out_specs=[ANY, pl.BlockSpec((kw, tc_ch), lambda j: (0, j)), pl.BlockSpec((1, tc_ch), lambda j: (0, j))],
        out_shape=[jax.ShapeDtypeStruct((t, d_pool + 2 * d_conv), BF16),
                   jax.ShapeDtypeStruct((kw, d_conv), F32), jax.ShapeDtypeStruct((1, d_conv), F32)],
        scratch_shapes=[pltpu.VMEM((t + HALO, tc_ch), F32), pltpu.VMEM((t + HALO, tc_ch), F32),
                        pltpu.VMEM((8 * kw, tc_ch), F32), pltpu.VMEM((8, tc_ch), F32),
                        pltpu.VMEM((2, t, tc_ch), BF16), pltpu.SemaphoreType.DMA((2,))],
        compiler_params=_params(("arbitrary",)),
    )(z, z, dc, conv_w)


def short_fwd(name, z, conv_w, d_short):
    t = z.shape[0]
    kw = conv_w.shape[0]
    tc_ch = _tile(d_short, CHANNEL_TILE)
    nt = d_short // tc_ch

    def body(b_ref, c_ref, u_ref, w_ref, y_ref, pad):
        pad[pl.ds(0, HALO), :] = jnp.zeros((HALO, tc_ch), F32)

        def fill(s, tc):
            pad[pl.ds(HALO + s, tc), :] = c_ref[pl.ds(s, tc), :].astype(F32) * u_ref[pl.ds(s, tc), :].astype(F32)

        def chunk(s, tc):
            win = pad[pl.ds(s, tc + HALO), :]
            cq = _taps(win, w_ref, [HALO - (kw - 1) + k for k in range(kw)], tc)
            y_ref[pl.ds(s, tc), :] = (b_ref[pl.ds(s, tc), :].astype(F32) * cq).astype(BF16)

        _chunks(t, fill)
        _chunks(t, chunk)

    return pl.pallas_call(
        body, name=name, grid=(nt,),
        in_specs=[pl.BlockSpec((t, tc_ch), lambda j: (0, j)), pl.BlockSpec((t, tc_ch), lambda j: (0, nt + j)),
                  pl.BlockSpec((t, tc_ch), lambda j: (0, 2 * nt + j)), pl.BlockSpec((kw, tc_ch), lambda j: (0, j))],
        out_specs=pl.BlockSpec((t, tc_ch), lambda j: (0, j)),
        out_shape=jax.ShapeDtypeStruct((t, d_short), BF16),
        scratch_shapes=[pltpu.VMEM((t + HALO, tc_ch), F32)],
        compiler_params=_params(("parallel",)),
    )(z, z, z, conv_w)


def short_bwd(name, z, dy, conv_w, d_short):
    t = z.shape[0]
    kw = conv_w.shape[0]
    tc_ch = _tile(d_short, CHANNEL_TILE)
    nt = d_short // tc_ch

    def body(b_ref, c_ref, u_ref, dy_ref, w_ref, dz_ref, dw_ref, pad_q, pad_dcq, acc_w, tiles, sems):
        j = pl.program_id(0)
        db_ref, dcg_ref, du_ref = tiles.at[0], tiles.at[1], tiles.at[2]
        writes = [pltpu.make_async_copy(tiles.at[p], dz_ref.at[:, pl.ds((p * nt + j) * tc_ch, tc_ch)], sems.at[p])
                  for p in range(3)]

        def wait_writes():
            for cp in writes:
                cp.wait()

        pad_q[pl.ds(0, HALO), :] = jnp.zeros((HALO, tc_ch), F32)
        pad_dcq[pl.ds(t, HALO), :] = jnp.zeros((HALO, tc_ch), F32)
        acc_w[...] = jnp.zeros_like(acc_w)

        def fill(s, tc):
            rows = pl.ds(s, tc)
            pad_q[pl.ds(HALO + s, tc), :] = c_ref[rows, :].astype(F32) * u_ref[rows, :].astype(F32)
            pad_dcq[rows, :] = dy_ref[rows, :].astype(F32) * b_ref[rows, :].astype(F32)

        def chunk(s, tc):
            rows = pl.ds(s, tc)
            win_q = pad_q[pl.ds(s, tc + HALO), :]
            dcq = pad_dcq[rows, :]
            cq = None
            for k, shifted in _shifted(win_q, [HALO - (kw - 1) + k for k in range(kw)], tc):
                acc_w[pl.ds(8 * k, 8), :] += _fold8(dcq * shifted)
                term = w_ref[k:k + 1, :] * shifted
                cq = term if cq is None else cq + term
            db_ref[rows, :] = (dy_ref[rows, :].astype(F32) * cq).astype(BF16)
            dq = _taps(pad_dcq[pl.ds(s, tc + HALO), :], w_ref, list(range(kw)), tc, flip=True)
            dcg_ref[rows, :] = (dq * u_ref[rows, :].astype(F32)).astype(BF16)
            du_ref[rows, :] = (dq * c_ref[rows, :].astype(F32)).astype(BF16)

        _chunks(t, fill)
        pl.when(j > 0)(wait_writes)
        _chunks(t, chunk)
        for cp in writes:
            cp.start()
        pl.when(j == nt - 1)(wait_writes)
        for k in range(kw):
            dw_ref[k:k + 1, :] = jnp.sum(acc_w[pl.ds(8 * k, 8), :], axis=0, keepdims=True)

    zspec = [pl.BlockSpec((t, tc_ch), lambda j, o=o: (0, o * nt + j)) for o in range(3)]
    return pl.pallas_call(
        body, name=name, grid=(nt,),
        in_specs=[*zspec, pl.BlockSpec((t, tc_ch), lambda j: (0, j)), pl.BlockSpec((kw, tc_ch), lambda j: (0, j))],
        out_specs=[ANY, pl.BlockSpec((kw, tc_ch), lambda j: (0, j))],
        out_shape=[jax.ShapeDtypeStruct((t, 3 * d_short), BF16), jax.ShapeDtypeStruct((kw, d_short), F32)],
        scratch_shapes=[pltpu.VMEM((t + HALO, tc_ch), F32), pltpu.VMEM((t + HALO, tc_ch), F32),
                        pltpu.VMEM((8 * kw, tc_ch), F32), pltpu.VMEM((3, t, tc_ch), BF16),
                        pltpu.SemaphoreType.DMA((3,))],
        compiler_params=_params(("arbitrary",)),
    )(z, z, z, dy, conv_w)


def _adamw_update(w, m, v, g):
    nm = ADAM_B1 * m + (1.0 - ADAM_B1) * g
    nv = ADAM_B2 * v + (1.0 - ADAM_B2) * (g * g)
    m_hat = nm / (1.0 - ADAM_B1 ** ADAM_STEP)
    v_hat = nv / (1.0 - ADAM_B2 ** ADAM_STEP)
    return -ADAM_LR * (m_hat / (jnp.sqrt(v_hat) + ADAM_EPS) + ADAM_WD * w), nm, nv


def adamw_replicated(name, params, first_moments, second_moments, contributions, layout, scalar_at):
    n = len(params)
    n_slots = contributions.shape[0]

    def total(c_ref, row, lane, rows, lanes):
        acc = c_ref[0, row:row + rows, lane:lane + lanes]
        for slot in range(1, n_slots):
            acc = acc + c_ref[slot, row:row + rows, lane:lane + lanes]
        return acc

    def body(*refs):
        ws, ms, vs, c_ref = refs[:n], refs[n:2 * n], refs[2 * n:3 * n], refs[3 * n]
        outs = refs[3 * n + 1:]
        outs[0][...] = total(c_ref, *scalar_at, 1, 128)
        for i, (row, lane) in enumerate(layout):
            g = total(c_ref, row, lane, *params[i].shape)
            grad_ref, delta_ref, nm_ref, nv_ref = outs[1 + 4 * i:5 + 4 * i]
            grad_ref[...] = g
            delta_ref[...], nm_ref[...], nv_ref[...] = _adamw_update(ws[i][...], ms[i][...], vs[i][...], g)

    out_shape = [jax.ShapeDtypeStruct((1, 128), F32)]
    for p in params:
        out_shape += [jax.ShapeDtypeStruct(p.shape, F32)] * 4
    return pl.pallas_call(body, name=name, out_shape=out_shape)(*params, *first_moments, *second_moments, contributions)


def adamw(name, w, m, v, contributions):
    r, c = w.shape
    nc = len(contributions)
    n_slots = contributions[0].shape[0]
    tr = 256 if c <= 1024 else 128
    if any(a.shape[1] % tr for a in contributions):
        assert nc == 1
        tr = r
    tiles = [a.shape[1] // tr for a in contributions]
    first = [sum(tiles[:j]) for j in range(nc)]

    def body(w_ref, m_ref, v_ref, *rest):
        g_refs, (grad_ref, delta_ref, nm_ref, nv_ref) = rest[:nc], rest[nc:]
        i = pl.program_id(0)
        g = None
        for j, g_ref in enumerate(g_refs):
            s = g_ref[0].astype(F32)
            for slot in range(1, n_slots):
                s = s + g_ref[slot].astype(F32)
            g = s if g is None else jnp.where(i >= first[j], s, g)
        grad_ref[...] = g
        delta_ref[...], nm_ref[...], nv_ref[...] = _adamw_update(w_ref[...], m_ref[...], v_ref[...], g)

    blk = pl.BlockSpec((tr, c), lambda i: (i, 0))
    g_specs = [pl.BlockSpec((n_slots, tr, c), lambda i, j=j: (0, jnp.clip(i - first[j], 0, tiles[j] - 1), 0))
               for j in range(nc)]
    return pl.pallas_call(
        body, name=name, grid=(r // tr,),
        in_specs=[blk, blk, blk, *g_specs],
        out_specs=[blk] * 4, out_shape=[jax.ShapeDtypeStruct((r, c), F32)] * 4,
        compiler_params=_params(("parallel",)),
    )(w, m, v, *contributions)


def _pad_rows(a, rows):
    return jnp.pad(a, ((0, rows - a.shape[0]), (0, 0)))


def kernel(x, mix_pre_g, mix_post_g, ffn_pre_g, ffn_post_g, ab_w_in, pool_w, pool_scale, conv_w, conv_b, conv_ln_g, conv_ln_b, ab_w_out, sc_w_in, sc_conv_w, sc_w_out, ffn_w1, ffn_w2, loss_target, m_mix_pre_g, m_mix_post_g, m_ffn_pre_g, m_ffn_post_g, m_ab_w_in, m_pool_w, m_pool_scale, m_conv_w, m_conv_b, m_conv_ln_g, m_conv_ln_b, m_ab_w_out, m_sc_w_in, m_sc_conv_w, m_sc_w_out, m_ffn_w1, m_ffn_w2, v_mix_pre_g, v_mix_post_g, v_ffn_pre_g, v_ffn_post_g, v_ab_w_in, v_pool_w, v_pool_scale, v_conv_w, v_conv_b, v_conv_ln_g, v_conv_ln_b, v_ab_w_out, v_sc_w_in, v_sc_conv_w, v_sc_w_out, v_ffn_w1, v_ffn_w2):
    t, d = x.shape[1], x.shape[2]
    d_pool = pool_scale.shape[1]
    d_conv = conv_b.shape[1]
    d_short = d
    ng, pg = pool_w.shape[1], pool_w.shape[3]
    kw, ks = conv_w.shape[1], sc_conv_w.shape[1]
    nb_ab, nb_sc, nb_ff = ab_w_in.shape[2], sc_w_in.shape[2], ffn_w1.shape[2]

    xs = x[0]
    target = loss_target[0]

    lanes = min(128, d_conv // N_DEV)
    small_rows = [kw * (d_conv // N_DEV) // lanes, ks * (d_short // N_DEV) // lanes, ng * (pg // N_DEV) * pg // lanes]
    small_total = -(-sum(small_rows) // 8) * 8
    r0, r1, r2 = small_rows[0], small_rows[0] + small_rows[1], sum(small_rows)

    def pack_small(a_conv, a_sconv, a_pool):
        parts = [a_conv[0].reshape(-1, lanes), a_sconv[0].reshape(-1, lanes), a_pool[0].reshape(-1, lanes)]
        return _pad_rows(jnp.concatenate(parts, axis=0), small_total)

    shards = {
        "ab_in": (ab_w_in, 0, BF16), "small": (pack_small(conv_w, sc_conv_w, pool_w)[None], 0, F32),
        "ab_out": (ab_w_out, 0, BF16), "ff1_0": (ffn_w1, 0, BF16), "ff2_0": (ffn_w2, 0, BF16),
        "sc_in": (sc_w_in, 0, BF16), "sc_out": (sc_w_out, 0, BF16),
        "ff1_1": (ffn_w1, 1, BF16), "ff2_1": (ffn_w2, 1, BF16)}
    direct = ["ab_in", "small", "ab_out"]
    zones = {nm: place_shard("place_" + nm, *shards[nm]) for nm in direct}
    started, token = copies_start("gather_start", [[zones[nm]] for nm in direct], _first_hop, 4)
    started = dict(zip(direct, started))
    ring = {}
    for nm in ["ff1_0"]:
        zones[nm] = place_shard("place_" + nm, *shards[nm], deps=[token])
        (ring[nm],), token = copies_start("ring_start_" + nm, [[zones[nm]]], _ring_hop1, 3, deps=[token])
    for nm in shards:
        if nm not in zones:
            zones[nm] = place_shard("place_" + nm, *shards[nm], deps=[token])

    ties = [0]

    def after(v, *deps):
        ties[0] += 1
        return tie(f"tie_{ties[0]}", v, *deps)

    def fetch_begin(nm, dep):
        (zone,) = copies_wait("gather_wait_" + nm, started[nm], _first_hop, dep)
        (hop,), tok = copies_start("forward_start_" + nm, [[zone]], _second_hop, 3)
        return hop, tok

    def fetch_end(nm, hop, dep):
        return copies_wait("forward_wait_" + nm, hop, _second_hop, dep)[0]

    def ring_step(tag, dep, third=(), second=(), first=()):
        names, groups, hops, counts = [], [], [], []
        for nm in third:
            groups.append(copies_wait("ring2_wait_" + nm, ring[nm], _ring_hop2, dep))
            names, hops, counts = names + [nm], hops + [_ring_hop3], counts + [1]
        for nm in second:
            groups.append(copies_wait("ring1_wait_" + nm, ring[nm], _ring_hop1, dep))
            names, hops, counts = names + [nm], hops + [_ring_hop2], counts + [4]
        for nm in first:
            groups.append([zones[nm]])
            names, hops, counts = names + [nm], hops + [_ring_hop1], counts + [3]
        begun, tok = copies_start("ring_start_" + tag, groups, hops, counts, deps=[dep])
        ring.update(zip(names, begun))
        return tok

    def ring_done(nm, dep):
        return copies_wait("ring3_wait_" + nm, ring[nm], _ring_hop3, dep)[0]

    relu = lambda r: jnp.maximum(r, 0.0)
    square = lambda a: a * a
    relu2_bwd = lambda r, a: r * (2.0 * a.astype(F32))

    def row(vec, l):
        return vec[l:l + 1]

    hop_small, _ = fetch_begin("small", token)
    hop_ab_in, tok = fetch_begin("ab_in", token)
    w_small = fetch_end("small", hop_small, tok)
    w_ab_in = fetch_end("ab_in", hop_ab_in, tok)
    w_conv = w_small[:, :r0].reshape(N_DEV, kw, -1).transpose(1, 0, 2).reshape(kw, d_conv)
    w_sconv = w_small[:, r0:r1].reshape(N_DEV, ks, -1).transpose(1, 0, 2).reshape(ks, d_short)
    w_pool = w_small[:, r1:r2].reshape(N_DEV, ng, -1, pg).transpose(1, 0, 2, 3).reshape(ng, pg, pg).astype(BF16)
    h0 = norm_pre("norm_pre", xs, after(row(mix_pre_g, 0), token))
    z0 = mm_nn_blocked("ab_in", h0, w_ab_in, out_dtype=BF16)
    hop, tok = fetch_begin("ab_out", z0)
    z0 = after(z0, tok)
    pooled, y0 = pool_fwd("pool_fwd", z0, w_pool, pool_scale, d_pool, d_pool + d_conv)
    cv = conv_fwd("conv_fwd", z0, w_conv, conv_b, d_pool, d_conv)
    y0 = ln_silu("ln_silu", cv, conv_ln_g, conv_ln_b, y0, d_pool // d_conv)
    w_ab_out = fetch_end("ab_out", hop, y0)
    def ffn_up(name, h, w, second, first=()):
        a = mm_nn_blocked(name, h, w, out_dtype=BF16, epilogue=relu, blocks=(0, UP_FIRST_BLOCKS))
        tok = ring_step(name, a, second=second, first=first)
        return mm_nn_blocked(name + "_rest", h, w, out_dtype=BF16, epilogue=relu,
                             blocks=(UP_FIRST_BLOCKS, N_DEV - UP_FIRST_BLOCKS), into=after(a, tok))

    tok = ring_step("a", w_ab_out, second=["ff1_0"], first=["ff2_0"])
    y0 = after(y0, tok)
    m0 = mm_nn("ab_out", y0, w_ab_out.reshape(d_pool + d_conv, d), out_dtype=F32)
    x1, h1 = post_pre("post_pre_0", xs, m0, row(mix_post_g, 0), row(ffn_pre_g, 0))
    tok = ring_step("b", h1, third=["ff1_0"])
    w_ff1_0 = ring_done("ff1_0", tok)
    a0 = ffn_up("ffn0_up", h1, w_ff1_0, ["ff2_0"], first=["sc_in", "sc_out"])
    tok = ring_step("c", a0, third=["ff2_0"])
    w_ff2_0 = ring_done("ff2_0", tok).reshape(-1, d)
    f0 = mm_nn("ffn0_down", a0, w_ff2_0, out_dtype=F32, tk=2048, lhs_fn=square)
    tok = ring_step("d", f0, second=["sc_in", "sc_out"], first=["ff1_1"])
    f0 = after(f0, tok)
    x2, h2 = post_pre("post_pre_1", x1, f0, row(ffn_post_g, 0), row(mix_pre_g, 1))
    tok = ring_step("e", h2, third=["sc_in"])
    w_sc_in = ring_done("sc_in", tok)
    z1 = mm_nn_blocked("sc_in", h2, w_sc_in, out_dtype=BF16)
    y1 = short_fwd("short_fwd", z1, w_sconv, d_short)
    tok = ring_step("f", y1, third=["sc_out"], second=["ff1_1"], first=["ff2_1"])
    w_sc_out = ring_done("sc_out", tok).reshape(d_short, d)
    m1 = mm_nn("sc_out", y1, w_sc_out, out_dtype=F32)
    x3, h3 = post_pre("post_pre_2", x2, m1, row(mix_post_g, 1), row(ffn_pre_g, 1))
    tok = ring_step("g", h3, third=["ff1_1"])
    w_ff1_1 = ring_done("ff1_1", tok)
    a1 = ffn_up("ffn1_up", h3, w_ff1_1, ["ff2_1"])
    tok = ring_step("h", a1, third=["ff2_1"])
    w_ff2_1 = ring_done("ff2_1", tok).reshape(-1, d)
    f1 = mm_nn("ffn1_down", a1, w_ff2_1, out_dtype=F32, tk=2048, lhs_fn=square)
    dx4, df1, loss_part, dg_ffn_post1 = post_loss("post_loss", x3, f1, row(ffn_post_g, 1), target)

    red = {}

    def reduce_step(dep, begin=None, middle=None):
        tags, groups, hops, counts = [], [], [], []
        if begin is not None:
            tag, g = begin
            tags, groups = tags + [tag], groups + [[g, lax.empty((N_CHIP,) + g.shape[1:], g.dtype)]]
            hops, counts = hops + [_pair_hop], counts + [N_CHIP]
        if middle is not None:
            g, from_sibling = copies_wait("pair_wait_" + middle, red[middle], _pair_hop, dep)
            tags, groups = tags + [middle], groups + [list(pair_add("pair_add_" + middle, g, from_sibling))]
            hops, counts = hops + [_chip_hop], counts + [3]
        begun, tok = copies_start("reduce_start_" + "_".join(tags), groups, hops, counts, deps=[dep])
        red.update(zip(tags, begun))
        return tok

    def reduce_end(tag, dep):
        return copies_wait("chips_wait_" + tag, red[tag], _chip_hop, dep)[1]

    dpre, dw = mm_bwd_pair("ffn1_da_dw2", df1, w_ff2_1, a1, out_dtype=BF16, act_fn=square, epilogue=relu2_bwd)
    dpre = after(dpre, reduce_step(dpre, begin=("ff2_1", dw.reshape(N_DEV, -1, d))))
    dh3, dw = mm_bwd_pair_blocked("ffn1_dh_dw1", dpre, w_ff1_1, h3, out_dtype=BF16)
    dx3, dm1, dg_ffn_pre1, dg_mix_post1 = bwd_pre_post("bwd_3", dx4, x3, row(ffn_pre_g, 1), dh3, m1, row(mix_post_g, 1))
    dm1 = after(dm1, reduce_step(dm1, begin=("ff1_1", dw), middle="ff2_1"))

    dy1, dw = mm_bwd_pair("sc_dy_dwout", dm1, w_sc_out, y1, out_dtype=BF16)
    dy1 = after(dy1, reduce_step(dy1, begin=("sc_out", dw.reshape(N_DEV, -1, d)), middle="ff1_1"))
    dz1, dw_sconv = short_bwd("short_bwd", z1, dy1, w_sconv, d_short)
    dh2, dw = mm_bwd_pair_blocked("sc_dh_dwin", dz1, w_sc_in, h2, out_dtype=BF16)
    dx2, df0, dg_mix_pre1, dg_ffn_post0 = bwd_pre_post("bwd_2", dx3, x2, row(mix_pre_g, 1), dh2, f0, row(ffn_post_g, 0))
    df0 = after(df0, reduce_step(df0, begin=("sc_in", dw), middle="sc_out"))

    dpre, dw = mm_bwd_pair("ffn0_da_dw2", df0, w_ff2_0, a0, out_dtype=BF16, act_fn=square, epilogue=relu2_bwd)
    dpre = after(dpre, reduce_step(dpre, begin=("ff2_0", dw.reshape(N_DEV, -1, d)), middle="sc_in"))
    dh1, dw = mm_bwd_pair_blocked("ffn0_dh_dw1", dpre, w_ff1_0, h1, out_dtype=BF16)
    dx1, dm0, dg_ffn_pre0, dg_mix_post0 = bwd_pre_post("bwd_1", dx2, x1, row(ffn_pre_g, 0), dh1, m0, row(mix_post_g, 0))
    dm0 = after(dm0, reduce_step(dm0, begin=("ff1_0", dw), middle="ff2_0"))

    dy0, dw = mm_bwd_pair("ab_dy_dwout", dm0, w_ab_out.reshape(d_pool + d_conv, d), y0, out_dtype=BF16)
    dy0 = after(dy0, reduce_step(dy0, begin=("ab_out", dw.reshape(N_DEV, -1, d)), middle="ff1_0"))
    dcv, dg_ln_g, dg_ln_b = ln_silu_bwd("ln_silu_bwd", cv, conv_ln_g, conv_ln_b, dy0, d_pool // d_conv)
    dz0, dw_conv, dg_conv_b = conv_bwd("conv_bwd", z0, dcv, w_conv, d_pool, d_conv)
    dz0, dw_pool, dg_pool_scale = pool_bwd("pool_bwd", pooled, dy0, w_pool, pool_scale, dz0)
    small_parts = [
        dw_conv.reshape(kw, N_DEV, -1).transpose(1, 0, 2).reshape(N_DEV, -1, lanes),
        dw_sconv.reshape(ks, N_DEV, -1).transpose(1, 0, 2).reshape(N_DEV, -1, lanes),
        dw_pool.reshape(ng, N_DEV, pg // N_DEV, pg).transpose(1, 0, 2, 3).reshape(N_DEV, -1, lanes),
    ]
    small = jnp.pad(jnp.concatenate(small_parts, axis=1), ((0, 0), (0, small_total - r2), (0, 0)))
    dz0 = after(dz0, reduce_step(dz0, begin=("small", small), middle="ab_out"))
    dh0, dw = mm_bwd_pair_blocked("ab_dh_dwin", dz0, w_ab_in, h0, out_dtype=BF16)
    dh0 = after(dh0, reduce_step(dh0, begin=("ab_in", dw), middle="small"))
    grad_x, dg_mix_pre0 = bwd_pre_final("bwd_0", dx1, xs, row(mix_pre_g, 0), dh0)
    tok = reduce_step(grad_x, middle="ab_in")

    gains = [dg_mix_pre0, dg_mix_pre1, dg_mix_post0, dg_mix_post1, dg_ffn_pre0, dg_ffn_pre1, dg_ffn_post0, dg_ffn_post1]
    pieces = [(g, i, 0) for i, g in enumerate(gains)]
    rep_layout = [(0, 0), (2, 0), (4, 0), (6, 0)]
    offset = 0
    for g in (dg_pool_scale, dg_conv_b, dg_ln_g, dg_ln_b):
        at = (len(gains) + offset // d, offset % d)
        pieces.append((g, *at))
        rep_layout.append(at)
        offset += g.shape[1]
    loss_at = (len(gains) + -(-offset // d), 0)
    rep_zone = place_sheet("place_rep", pieces, loss_part, loss_at, 16, d)
    (rep_hop,), tok = copies_start("rep_start", [[rep_zone]], _first_hop, 4,
                                   deps=[tok])

    def upd(name, w, m, v, contribs):
        shape = w.shape
        flat2 = lambda a: a.reshape(-1, shape[-1])
        outs = adamw(name, flat2(w), flat2(m), flat2(v), contribs)
        return [o.reshape(shape) for o in outs]

    g_ff2 = [reduce_end("ff2_0", tok), reduce_end("ff2_1", tok)]
    o_ff2 = upd("adam_ffn_w2", ffn_w2, m_ffn_w2, v_ffn_w2, g_ff2)
    (rep_zone,) = copies_wait("rep_wait", rep_hop, _first_hop, o_ff2[0])
    (rep_hop,), _ = copies_start("rep_forward_start", [[rep_zone]], _second_hop, 3)
    g_ff1 = [reduce_end("ff1_0", o_ff2[0]), reduce_end("ff1_1", o_ff2[0])]
    o_ff1 = upd("adam_ffn_w1", ffn_w1, m_ffn_w1, v_ffn_w1, g_ff1)
    (rep_all,) = copies_wait("rep_forward_wait", rep_hop, _second_hop, o_ff1[0])
    loss_sum, *o_rep = adamw_replicated(
        "adam_replicated",
        [mix_pre_g, mix_post_g, ffn_pre_g, ffn_post_g, pool_scale, conv_b, conv_ln_g, conv_ln_b],
        [m_mix_pre_g, m_mix_post_g, m_ffn_pre_g, m_ffn_post_g, m_pool_scale, m_conv_b, m_conv_ln_g, m_conv_ln_b],
        [v_mix_pre_g, v_mix_post_g, v_ffn_pre_g, v_ffn_post_g, v_pool_scale, v_conv_b, v_conv_ln_g, v_conv_ln_b],
        rep_all, rep_layout, loss_at)
    loss = loss_sum[0, 0] * (0.5 / d)
    o_sc_out = upd("adam_sc_out", sc_w_out, m_sc_w_out, v_sc_w_out, [reduce_end("sc_out", o_ff1[0])])
    o_sc_in = upd("adam_sc_in", sc_w_in, m_sc_w_in, v_sc_w_in, [reduce_end("sc_in", o_sc_out[0])])
    o_ab_out = upd("adam_ab_out", ab_w_out, m_ab_w_out, v_ab_w_out, [reduce_end("ab_out", o_sc_in[0])])
    o_small = adamw("adam_small", pack_small(conv_w, sc_conv_w, pool_w), pack_small(m_conv_w, m_sc_conv_w, m_pool_w),
                    pack_small(v_conv_w, v_sc_conv_w, v_pool_w), [reduce_end("small", o_ab_out[0])])
    o_ab_in = upd("adam_ab_in", ab_w_in, m_ab_w_in, v_ab_w_in, [reduce_end("ab_in", o_small[0])])

    def unpack_small(o):
        return o[:r0].reshape(conv_w.shape), o[r0:r1].reshape(sc_conv_w.shape), o[r1:r2].reshape(pool_w.shape)

    results = []
    for kind in range(4):
        g_mix_pre, g_mix_post, g_ffn_pre, g_ffn_post, g_scale, g_conv_b, g_ln_g, g_ln_b = o_rep[kind::4]
        s_conv, s_sconv, s_pool = unpack_small(o_small[kind])
        results.append([
            g_mix_pre, g_mix_post, g_ffn_pre, g_ffn_post,
            o_ab_in[kind], s_pool, g_scale, s_conv, g_conv_b, g_ln_g, g_ln_b,
            o_ab_out[kind], o_sc_in[kind], s_sconv, o_sc_out[kind], o_ff1[kind], o_ff2[kind]])

    return (loss, grad_x[None], *results[0], *results[1], *results[2], *results[3])
```

```python
import jax
import jax.numpy as jnp
from jax import lax
from jax.experimental import pallas as pl
from jax.experimental.pallas import tpu as pltpu

F32 = jnp.float32
BF16 = jnp.bfloat16
MESH = pl.DeviceIdType.MESH
ANY = pl.BlockSpec(memory_space=pl.ANY)

NORM_EPS = 1e-6
POOL_WINDOWS = (2, 4, 8, 16)
ADAM_LR = 0.001
ADAM_B1 = 0.9
ADAM_B2 = 0.999
ADAM_EPS = 1e-08
ADAM_WD = 0.01
ADAM_STEP = 10

N_DEV = 8
VMEM_LIMIT = 56 * 1024 * 1024
PAIR_ADD_BLOCK = 1 << 20
MATMUL_ROWS = 2048
UP_FIRST_BLOCKS = 6
ROW_TILE = 256
CHANNEL_TILE = 256
TIME_CHUNK = 64
HALO = 32

NN = (((1,), (0,)), ((), ()))
NT = (((1,), (1,)), ((), ()))
TN = (((0,), (0,)), ((), ()))


def _params(sem):
    return pltpu.CompilerParams(dimension_semantics=sem, vmem_limit_bytes=VMEM_LIMIT)


def _place():
    x, y, c = lax.axis_index("x"), lax.axis_index("y"), lax.axis_index("c")
    return x, y, c


def _slot(px, py, pc):
    return 4 * px + 2 * py + pc


HBM = pl.BlockSpec(memory_space=pltpu.HBM)
SEM = pl.BlockSpec(memory_space=pltpu.SEMAPHORE)
EFFECT = pltpu.SideEffectType.DATAFLOW_SIDE_EFFECTING
TOKEN = jax.ShapeDtypeStruct((8, 128), F32)


def _in_hbm(a):
    return pltpu.with_memory_space_constraint(a, pltpu.HBM)


CHIPS = [(0, 0), (0, 1), (1, 0), (1, 1)]
N_CHIP = len(CHIPS)


def _chip(px, py):
    return 2 * px + py


def _first_hop(bufs, sends, recvs, waiting):
    (land,) = bufs
    x, y, c = _place()
    me = _slot(x, y, c)
    peers = [(x, y, 1 - c), (1 - x, y, c), (x, 1 - y, c), (1 - x, 1 - y, c)]
    return [pltpu.make_async_remote_copy(
        src_ref=land.at[me], dst_ref=land.at[_slot(*p) if waiting else me],
        send_sem=sends.at[k], recv_sem=recvs.at[k], device_id=p, device_id_type=MESH) for k, p in enumerate(peers)]


def _second_hop(bufs, sends, recvs, waiting):
    (land,) = bufs
    x, y, c = _place()
    return [pltpu.make_async_remote_copy(
        src_ref=land.at[_slot(px, py, c)], dst_ref=land.at[_slot(px, py, 1 - c if waiting else c)],
        send_sem=sends.at[k], recv_sem=recvs.at[k], device_id=(x, y, 1 - c), device_id_type=MESH)
        for k, (px, py) in enumerate([(1 - x, y), (x, 1 - y), (1 - x, 1 - y)])]


def _ring_hop1(bufs, sends, recvs, waiting):
    (land,) = bufs
    x, y, c = _place()
    me = _slot(x, y, c)
    peers = [(1 - x, y, c), (x, 1 - y, c), (x, y, 1 - c)]
    return [pltpu.make_async_remote_copy(
        src_ref=land.at[me], dst_ref=land.at[_slot(*p) if waiting else me],
        send_sem=sends.at[k], recv_sem=recvs.at[k], device_id=p, device_id_type=MESH) for k, p in enumerate(peers)]


def _ring_hop2(bufs, sends, recvs, waiting):
    (land,) = bufs
    x, y, c = _place()
    half = land.shape[1] // 2
    first, second = pl.ds(0, half), pl.ds(half, half)
    nx, ny, diag = _slot(1 - x, y, c), _slot(x, 1 - y, c), _slot(1 - x, 1 - y, c)
    plan = [
        (land.at[ny, first], land.at[diag, first], (1 - x, y, c)),
        (land.at[nx, second], land.at[diag, second], (x, 1 - y, c)),
        (land.at[nx], land.at[_slot(1 - x, y, 1 - c)], (x, y, 1 - c)),
        (land.at[ny], land.at[_slot(x, 1 - y, 1 - c)], (x, y, 1 - c))]
    return [pltpu.make_async_remote_copy(
        src_ref=src, dst_ref=mine if waiting else src, send_sem=sends.at[k], recv_sem=recvs.at[k],
        device_id=to, device_id_type=MESH) for k, (src, mine, to) in enumerate(plan)]


def _ring_hop3(bufs, sends, recvs, waiting):
    (land,) = bufs
    x, y, c = _place()
    return [pltpu.make_async_remote_copy(
        src_ref=land.at[_slot(1 - x, 1 - y, c)], dst_ref=land.at[_slot(1 - x, 1 - y, 1 - c if waiting else c)],
        send_sem=sends.at[0], recv_sem=recvs.at[0], device_id=(x, y, 1 - c), device_id_type=MESH)]


RING_HOPS = [(_ring_hop1, 3), (_ring_hop2, 4), (_ring_hop3, 1)]

def _pair_hop(bufs, sends, recvs, waiting):
    g, land = bufs
    x, y, c = _place()
    return [pltpu.make_async_remote_copy(
        src_ref=g.at[_slot(qx, qy, 1 - c)], dst_ref=land.at[q],
        send_sem=sends.at[q], recv_sem=recvs.at[q], device_id=(x, y, 1 - c), device_id_type=MESH)
        for q, (qx, qy) in enumerate(CHIPS)]


def _chip_hop(bufs, sends, recvs, waiting):
    p, land = bufs
    x, y, c = _place()
    return [pltpu.make_async_remote_copy(
        src_ref=p.at[_chip(px, py)], dst_ref=land.at[_chip(px, py) if waiting else _chip(x, y)],
        send_sem=sends.at[k], recv_sem=recvs.at[k], device_id=(px, py, c), device_id_type=MESH)
        for k, (px, py) in enumerate([(1 - x, y), (x, 1 - y), (1 - x, 1 - y)])]


def copies_start(name, groups, hop, n_copies, deps=()):
    flat = [b for grp in groups for b in grp]
    nb, ng = len(flat), len(groups)
    deps = list(deps)
    hops = list(hop) if isinstance(hop, (list, tuple)) else [hop] * ng
    counts = list(n_copies) if isinstance(n_copies, (list, tuple)) else [n_copies] * ng

    def body(*refs):
        ins, token = refs[:nb], refs[-1]
        sems = refs[nb + len(deps):nb + len(deps) + 2 * ng]
        i = 0
        for gi, grp in enumerate(groups):
            for cp in hops[gi](ins[i:i + len(grp)], sems[2 * gi], sems[2 * gi + 1], False):
                cp.start()
            i += len(grp)
        token[...] = jnp.zeros_like(token)

    outs = pl.pallas_call(
        body, name=name,
        out_shape=([pltpu.SemaphoreType.DMA((n,)) for n in counts for _ in range(2)]
                   + [pltpu.HBM(b.shape, b.dtype) for b in flat] + [TOKEN]),
        in_specs=[HBM] * nb + [ANY] * len(deps),
        out_specs=[SEM] * (2 * ng) + [HBM] * nb + [pl.BlockSpec(memory_space=pltpu.VMEM)],
        input_output_aliases={i: 2 * ng + i for i in range(nb)},
        compiler_params=pltpu.CompilerParams(has_side_effects=EFFECT),
    )(*[_in_hbm(b) for b in flat], *deps)
    started, i = [], 0
    for gi, grp in enumerate(groups):
        started.append((outs[2 * gi], outs[2 * gi + 1], list(outs[2 * ng + i:2 * ng + i + len(grp)])))
        i += len(grp)
    return started, outs[-1]


def copies_wait(name, started, hop, after):
    sends, recvs, bufs = started
    nb = len(bufs)

    def body(*refs):
        for cp in hop(refs[:nb], refs[nb], refs[nb + 1], True):
            cp.wait_send()
            cp.wait_recv()

    outs = pl.pallas_call(
        body, name=name,
        out_shape=[pltpu.HBM(b.shape, b.dtype) for b in bufs],
        in_specs=[HBM] * nb + [SEM, SEM, ANY], out_specs=[HBM] * nb,
        input_output_aliases={i: i for i in range(nb)},
        compiler_params=pltpu.CompilerParams(has_side_effects=EFFECT),
    )(*bufs, sends, recvs, after)
    return list(outs)


def place_shard(name, w, layer, dtype, deps=()):
    _, r, c = w.shape
    tr = _tile(r, 1024)
    x, y, core = _place()
    me = _slot(x, y, core).astype(jnp.int32).reshape(1)

    def body(me_ref, w_ref, *rest):
        rest[-1][...] = w_ref[...].astype(dtype)

    return pl.pallas_call(
        body, name=name,
        grid_spec=pltpu.PrefetchScalarGridSpec(
            num_scalar_prefetch=1, grid=(r // tr,),
            in_specs=[pl.BlockSpec((None, tr, c), lambda i, me_ref: (layer, i, 0))] + [ANY] * len(deps),
            out_specs=pl.BlockSpec((None, tr, c), lambda i, me_ref: (me_ref[0], i, 0))),
        out_shape=jax.ShapeDtypeStruct((N_DEV, r, c), dtype),
        compiler_params=_params(("parallel",)),
    )(me, w, *deps)


def place_sheet(name, pieces, total_of, total_at, rows, width):
    x, y, core = _place()
    me = _slot(x, y, core).astype(jnp.int32).reshape(1)

    def body(me_ref, *refs):
        o_ref = refs[-1]
        o_ref[...] = jnp.zeros_like(o_ref)
        for ref, (_, row, lane) in zip(refs, pieces):
            o_ref[row:row + 1, lane:lane + ref.shape[1]] = jnp.sum(ref[...], axis=0, keepdims=True)
        total = jnp.sum(jnp.sum(refs[len(pieces)][...], axis=0, keepdims=True), axis=1, keepdims=True)
        o_ref[total_at[0]:total_at[0] + 1, total_at[1]:total_at[1] + 128] = jnp.broadcast_to(total, (1, 128))

    arrays = [a for a, _, _ in pieces] + [total_of]
    return pl.pallas_call(
        body, name=name,
        grid_spec=pltpu.PrefetchScalarGridSpec(
            num_scalar_prefetch=1, grid=(1,),
            in_specs=[pl.BlockSpec(a.shape, lambda i, me_ref: (0, 0)) for a in arrays],
            out_specs=pl.BlockSpec((None, rows, width), lambda i, me_ref: (me_ref[0], 0, 0))),
        out_shape=jax.ShapeDtypeStruct((N_DEV, rows, width), F32),
    )(me, *arrays)


def tie(name, x, *deps):
    def body(*refs):
        del refs

    return pl.pallas_call(
        body, name=name, out_shape=jax.ShapeDtypeStruct(x.shape, x.dtype),
        in_specs=[ANY] * (1 + len(deps)), out_specs=ANY, input_output_aliases={0: 0},
    )(x, *deps)


def pair_add(name, g, from_sibling):
    _, r, c_dim = g.shape
    tr = r
    while tr * c_dim > PAIR_ADD_BLOCK and tr % 16 == 0:
        tr //= 2
    x, y, core = _place()
    where = jnp.stack([core, _chip(x, y)]).astype(jnp.int32)

    def body(where_ref, g_ref, s_ref, o_ref, zone_ref):
        total = (g_ref[...].astype(F32) + s_ref[...].astype(F32)).astype(o_ref.dtype)
        o_ref[...] = total

        @pl.when(pl.program_id(1) == where_ref[1])
        def _():
            zone_ref[...] = total

    blk = pl.BlockSpec((None, tr, c_dim), lambda i, q, where_ref: (q, i, 0))
    return pl.pallas_call(
        body, name=name,
        grid_spec=pltpu.PrefetchScalarGridSpec(
            num_scalar_prefetch=1, grid=(r // tr, N_CHIP),
            in_specs=[pl.BlockSpec((None, None, tr, c_dim), lambda i, q, where_ref: (q, where_ref[0], i, 0)), blk],
            out_specs=[blk, pl.BlockSpec((None, tr, c_dim), lambda i, q, where_ref: (where_ref[1], i, 0))]),
        out_shape=[jax.ShapeDtypeStruct((N_CHIP, r, c_dim), g.dtype)] * 2,
        compiler_params=_params(("parallel", "arbitrary")),
    )(where, g.reshape(N_CHIP, 2, r, c_dim), from_sibling)


def _matmul(name, lhs, rhs, *, out_shape, out_dtype, grid, lhs_spec, rhs_spec, out_spec, acc_shape,
            lhs_fn=None, epilogue=None, into=None):
    nk = grid[2]
    extra = [] if into is None else [into]

    def body(lhs_ref, rhs_ref, *rest):
        out_ref, scratch = rest[len(extra)], rest[len(extra) + 1:]

        def product():
            a = lhs_ref[...]
            if lhs_fn is not None:
                a = lhs_fn(a)
            return lax.dot_general(a, rhs_ref[...], NN, preferred_element_type=F32)

        def finish(r):
            if epilogue is not None:
                r = epilogue(r)
            out_ref[...] = r.astype(out_dtype)

        if nk == 1:
            finish(product())
        else:
            (acc_ref,) = scratch
            k = pl.program_id(2)

            @pl.when(k == 0)
            def _():
                acc_ref[...] = product()

            @pl.when(jnp.logical_and(k > 0, k < nk - 1))
            def _():
                acc_ref[...] += product()

            @pl.when(k == nk - 1)
            def _():
                finish(acc_ref[...] + product())

    return pl.pallas_call(
        body, name=name, grid=grid,
        out_shape=jax.ShapeDtypeStruct(out_shape, out_dtype),
        in_specs=[lhs_spec, rhs_spec] + [ANY] * len(extra), out_specs=out_spec,
        input_output_aliases={2: 0} if extra else {},
        scratch_shapes=[pltpu.VMEM(acc_shape, F32)] if nk > 1 else [],
        compiler_params=_params(("parallel", "parallel", "arbitrary")),
    )(lhs, rhs, *extra)


def _tile(n, want):
    return want if n % want == 0 else n


def mm_nn(name, x, w, *, out_dtype, tn=512, tk=None, lhs_fn=None, epilogue=None):
    t, kdim = x.shape
    n = w.shape[1]
    tm, tn = _tile(t, MATMUL_ROWS), _tile(n, tn)
    tk = kdim if tk is None else _tile(kdim, tk)
    return _matmul(
        name, x, w, out_shape=(t, n), out_dtype=out_dtype, grid=(t // tm, n // tn, kdim // tk),
        lhs_spec=pl.BlockSpec((tm, tk), lambda i, j, k: (i, k)),
        rhs_spec=pl.BlockSpec((tk, tn), lambda i, j, k: (k, j)),
        out_spec=pl.BlockSpec((tm, tn), lambda i, j, k: (i, j)),
        acc_shape=(tm, tn), lhs_fn=lhs_fn, epilogue=epilogue)


def mm_nn_blocked(name, x, w, *, out_dtype, epilogue=None, blocks=(0, N_DEV), into=None):
    t, kdim = x.shape
    nb = w.shape[2]
    tm = _tile(t, MATMUL_ROWS)
    tn = nb // 2 if nb >= 1024 else nb
    sub = nb // tn
    first, count = blocks
    return _matmul(
        name, x, w, out_shape=(t, N_DEV * nb), out_dtype=out_dtype, grid=(t // tm, count * sub, 1),
        lhs_spec=pl.BlockSpec((tm, kdim), lambda i, j, k: (i, k)),
        rhs_spec=pl.BlockSpec((None, kdim, tn), lambda i, j, k: (first + j // sub, k, j % sub)),
        out_spec=pl.BlockSpec((tm, tn), lambda i, j, k: (i, first * sub + j)),
        acc_shape=(tm, tn), epilogue=epilogue, into=into)


def mm_bwd_pair(name, dy, w, act, *, out_dtype, tile=512, act_fn=None, epilogue=None):
    t, n = dy.shape
    kdim = w.shape[0]
    tile = _tile(kdim, tile)

    def body(dy_ref, w_ref, act_ref, dx_ref, dw_ref):
        a = act_ref[...]
        dx = lax.dot_general(dy_ref[...], w_ref[...], NT, preferred_element_type=F32)
        if epilogue is not None:
            dx = epilogue(dx, a)
        dx_ref[...] = dx.astype(out_dtype)
        if act_fn is not None:
            a = act_fn(a)
        dw_ref[...] = lax.dot_general(a, dy_ref[...], TN, preferred_element_type=F32).astype(out_dtype)

    return pl.pallas_call(
        body, name=name, grid=(kdim // tile,),
        in_specs=[pl.BlockSpec((t, n), lambda j: (0, 0)), pl.BlockSpec((tile, n), lambda j: (j, 0)),
                  pl.BlockSpec((t, tile), lambda j: (0, j))],
        out_specs=[pl.BlockSpec((t, tile), lambda j: (0, j)), pl.BlockSpec((tile, n), lambda j: (j, 0))],
        out_shape=[jax.ShapeDtypeStruct((t, kdim), out_dtype), jax.ShapeDtypeStruct((kdim, n), out_dtype)],
        compiler_params=_params(("parallel",)),
    )(dy, w, act)


def mm_bwd_pair_blocked(name, dz, w, act, *, out_dtype, tile=1024):
    t = dz.shape[0]
    kdim, nb = w.shape[1], w.shape[2]
    tile = _tile(kdim, tile)

    def body(dz_ref, w_ref, act_ref, dx_ref, dw_ref, acc_ref):
        j = pl.program_id(1)
        dw_ref[...] = lax.dot_general(act_ref[...], dz_ref[...], TN, preferred_element_type=F32).astype(out_dtype)

        def product():
            return lax.dot_general(dz_ref[...], w_ref[...], NT, preferred_element_type=F32)

        @pl.when(j == 0)
        def _():
            acc_ref[...] = product()

        @pl.when(jnp.logical_and(j > 0, j < N_DEV - 1))
        def _():
            acc_ref[...] += product()

        @pl.when(j == N_DEV - 1)
        def _():
            dx_ref[...] = (acc_ref[...] + product()).astype(out_dtype)

    return pl.pallas_call(
        body, name=name, grid=(kdim // tile, N_DEV),
        in_specs=[pl.BlockSpec((t, nb), lambda i, j: (0, j)), pl.BlockSpec((None, tile, nb), lambda i, j: (j, i, 0)),
                  pl.BlockSpec((t, tile), lambda i, j: (0, i))],
        out_specs=[pl.BlockSpec((t, tile), lambda i, j: (0, i)),
                   pl.BlockSpec((None, tile, nb), lambda i, j: (j, i, 0))],
        out_shape=[jax.ShapeDtypeStruct((t, kdim), out_dtype), jax.ShapeDtypeStruct((N_DEV, kdim, nb), out_dtype)],
        scratch_shapes=[pltpu.VMEM((t, tile), F32)],
        compiler_params=_params(("parallel", "arbitrary")),
    )(dz, w, act)


def _rstd(v):
    return lax.rsqrt(jnp.mean(v * v, axis=-1, keepdims=True) + NORM_EPS)


def _rms_bwd(v, g, dy):
    r = _rstd(v)
    vhat = v * r
    dvh = dy * g
    dv = r * (dvh - vhat * jnp.mean(dvh * vhat, axis=-1, keepdims=True))
    return dv, dy * vhat


def _fold8(v):
    rows, n = v.shape
    return jnp.sum(v.reshape(rows // 8, 8, n), axis=0)


def _fold_lanes(v):
    out = v[:, 0:128]
    for i in range(1, v.shape[1] // 128):
        out = out + v[:, 128 * i:128 * (i + 1)]
    return out


def _accumulate(ref, v):
    i = pl.program_id(0)

    @pl.when(i == 0)
    def _():
        ref[...] = v

    @pl.when(i > 0)
    def _():
        ref[...] += v


def _row_call(body, name, t, ins, row_in, outs, acc_outs=(), tr=ROW_TILE):
    tr = _tile(t, tr)

    def in_spec(a, tiled):
        if isinstance(tiled, tuple):
            width, j = tiled
            return pl.BlockSpec((tr, width), lambda i: (i, j))
        return pl.BlockSpec((tr, a.shape[1]), lambda i: (i, 0)) if tiled else pl.BlockSpec(a.shape, lambda i: (0, 0))

    in_specs = [in_spec(a, tiled) for a, tiled in zip(ins, row_in)]
    out_specs = [pl.BlockSpec((tr, n), lambda i: (i, 0)) for n, _ in outs]
    out_specs += [pl.BlockSpec((8, n), lambda i: (0, 0)) for n in acc_outs]
    out_shape = [jax.ShapeDtypeStruct((t, n), dt) for n, dt in outs]
    out_shape += [jax.ShapeDtypeStruct((8, n), F32) for n in acc_outs]
    return pl.pallas_call(
        body, name=name, grid=(t // tr,), in_specs=in_specs, out_specs=out_specs, out_shape=out_shape,
        compiler_params=_params(("arbitrary",) if acc_outs else ("parallel",)),
    )(*ins)


def norm_pre(name, x, g):
    t, d = x.shape

    def body(x_ref, g_ref, h_ref):
        v = x_ref[...]
        h_ref[...] = (v * _rstd(v) * g_ref[...]).astype(BF16)

    return _row_call(body, name, t, [x, g], [True, False], [(d, BF16)])[0]


def post_pre(name, x, m, g_post, g_pre):
    t, d = x.shape

    def body(x_ref, m_ref, gp_ref, gn_ref, xo_ref, h_ref):
        mv = m_ref[...]
        xn = x_ref[...] + mv * _rstd(mv) * gp_ref[...]
        xo_ref[...] = xn
        h_ref[...] = (xn * _rstd(xn) * gn_ref[...]).astype(BF16)

    return _row_call(body, name, t, [x, m, g_post, g_pre], [True, True, False, False], [(d, F32), (d, BF16)])


def post_loss(name, x, f, g_post, target):
    t, d = x.shape

    def body(x_ref, f_ref, g_ref, t_ref, dx_ref, df_ref, loss_ref, dg_ref):
        fv = f_ref[...]
        g = g_ref[...]
        out = x_ref[...] + fv * _rstd(fv) * g
        err = out - t_ref[...]
        dx = err * (1.0 / d)
        dx_ref[...] = dx
        dfv, dg_rows = _rms_bwd(fv, g, dx)
        df_ref[...] = dfv.astype(BF16)
        _accumulate(loss_ref, _fold8(_fold_lanes(err * err)))
        _accumulate(dg_ref, _fold8(dg_rows))

    return _row_call(body, name, t, [x, f, g_post, target], [True, True, False, True],
                     [(d, F32), (d, BF16)], acc_outs=(128, d))


def bwd_pre_post(name, dx_out, x_in, g_pre, dh, f_prev, g_post_prev):
    t, d = x_in.shape

    def body(dxo_ref, x_ref, gpre_ref, dh_ref, f_ref, gpost_ref, dxi_ref, df_ref, dgpre_ref, dgpost_ref):
        dxv, dgpre_rows = _rms_bwd(x_ref[...], gpre_ref[...], dh_ref[...].astype(F32))
        dxi = dxo_ref[...] + dxv
        dxi_ref[...] = dxi
        dfv, dgpost_rows = _rms_bwd(f_ref[...], gpost_ref[...], dxi)
        df_ref[...] = dfv.astype(BF16)
        _accumulate(dgpre_ref, _fold8(dgpre_rows))
        _accumulate(dgpost_ref, _fold8(dgpost_rows))

    return _row_call(body, name, t, [dx_out, x_in, g_pre, dh, f_prev, g_post_prev],
                     [True, True, False, True, True, False], [(d, F32), (d, BF16)], acc_outs=(d, d))


def bwd_pre_final(name, dx_out, x_in, g_pre, dh):
    t, d = x_in.shape

    def body(dxo_ref, x_ref, gpre_ref, dh_ref, dxi_ref, dgpre_ref):
        dxv, dgpre_rows = _rms_bwd(x_ref[...], gpre_ref[...], dh_ref[...].astype(F32))
        dxi_ref[...] = dxo_ref[...] + dxv
        _accumulate(dgpre_ref, _fold8(dgpre_rows))

    return _row_call(body, name, t, [dx_out, x_in, g_pre, dh], [True, True, False, True], [(d, F32)], acc_outs=(d,))


def _layer_norm_parts(cv):
    mu = jnp.mean(cv, axis=-1, keepdims=True)
    xc = cv - mu
    rstd = lax.rsqrt(jnp.mean(xc * xc, axis=-1, keepdims=True) + NORM_EPS)
    return xc * rstd, rstd


def ln_silu(name, cv, g, b, y, y_block):
    t, n = cv.shape
    tr = _tile(t, ROW_TILE)

    def body(c_ref, g_ref, b_ref, y_in_ref, y_ref):
        chat, _ = _layer_norm_parts(c_ref[...])
        ln = chat * g_ref[...] + b_ref[...]
        y_ref[...] = (ln * jax.nn.sigmoid(ln)).astype(BF16)

    vec = pl.BlockSpec((1, n), lambda i: (0, 0))
    return pl.pallas_call(
        body, name=name, grid=(t // tr,),
        in_specs=[pl.BlockSpec((tr, n), lambda i: (i, 0)), vec, vec, ANY],
        out_specs=pl.BlockSpec((tr, n), lambda i: (i, y_block)),
        out_shape=jax.ShapeDtypeStruct(y.shape, y.dtype), input_output_aliases={3: 0},
        compiler_params=_params(("parallel",)),
    )(cv, g, b, y)


def ln_silu_bwd(name, cv, g, b, dy, dy_block):
    t, n = cv.shape

    def body(c_ref, g_ref, b_ref, dy_ref, dc_ref, dg_ref, db_ref):
        chat, rstd = _layer_norm_parts(c_ref[...])
        g = g_ref[...]
        ln = chat * g + b_ref[...]
        s = jax.nn.sigmoid(ln)
        dln = dy_ref[...].astype(F32) * (s * (1.0 + ln * (1.0 - s)))
        dchat = dln * g
        dc_ref[...] = rstd * (dchat - jnp.mean(dchat, axis=-1, keepdims=True)
                              - chat * jnp.mean(dchat * chat, axis=-1, keepdims=True))
        _accumulate(dg_ref, _fold8(dln * chat))
        _accumulate(db_ref, _fold8(dln))

    return _row_call(body, name, t, [cv, g, b, dy], [True, False, False, (n, dy_block)], [(n, F32)], acc_outs=(n, n))


def _chunks(t, fn, tc=TIME_CHUNK):
    tc = _tile(t, tc)

    def step(i, carry):
        fn(pl.multiple_of(i * tc, tc), tc)
        return carry

    lax.fori_loop(0, t // tc, step, 0)


def _rows_from(v, start, n):
    res = start % 8
    base = v if res == 0 else pltpu.roll(v, v.shape[0] - res, axis=0)
    return base[start - res:start - res + n, :]


def _shifted(window, offsets, tc):
    rows = window.shape[0]
    by_residue = {}
    for k, off in enumerate(offsets):
        by_residue.setdefault(off % 8, []).append((k, off))
    for res, taps in by_residue.items():
        base = window if res == 0 else pltpu.roll(window, rows - res, axis=0)
        for k, off in taps:
            yield k, base[off - res:off - res + tc, :]


def _taps(window, w_ref, offsets, tc, flip=False):
    acc = None
    for k, rows in _shifted(window, offsets, tc):
        kk = len(offsets) - 1 - k if flip else k
        term = w_ref[kk:kk + 1, :] * rows
        acc = term if acc is None else acc + term
    return acc


def _window_sums(win, tc, causal):
    sums = []
    cur, rows, step = win, tc + HALO, 1
    for _ in POOL_WINDOWS:
        rows -= 8
        if causal:
            cur = cur[8:8 + rows, :] + _rows_from(cur, 8 - step, rows)
            sums.append(cur[rows - tc:rows, :])
        else:
            cur = cur[0:rows, :] + _rows_from(cur, step, rows)
            sums.append(cur[0:tc, :])
        step *= 2
    return sums


def _pick(vals, g):
    out = vals[-1]
    for i in range(len(vals) - 2, -1, -1):
        out = jnp.where(g == i, vals[i], out)
    return out


def _pool_count(s, tc, g):
    t1 = (lax.broadcasted_iota(jnp.int32, (tc, 1), 0) + (s + 1)).astype(F32)
    width = _pick([float(w) for w in POOL_WINDOWS], g)
    return jnp.minimum(t1, width)


def pool_fwd(name, z, pool_w, pool_scale, d_pool, y_width):
    t = z.shape[0]
    ng, pg = pool_w.shape[0], pool_w.shape[1]

    def body(u_ref, w_ref, s_ref, pooled_ref, y_ref, pad):
        g = pl.program_id(0)
        pad[pl.ds(0, HALO), :] = jnp.zeros((HALO, pg), F32)

        def fill(s, tc):
            pad[pl.ds(HALO + s, tc), :] = u_ref[pl.ds(s, tc), :].astype(F32)

        def chunk(s, tc):
            win = pad[pl.ds(s, tc + HALO), :]
            total = _pick(_window_sums(win, tc, causal=True), g)
            pooled = total / _pool_count(s, tc, g) - win[HALO:HALO + tc, :]
            pooled_ref[pl.ds(s, tc), :] = pooled.astype(BF16)

        _chunks(t, fill)
        _chunks(t, chunk)
        mixed = jnp.dot(pooled_ref[...], w_ref[...], preferred_element_type=F32)
        y_ref[...] = (mixed * s_ref[...]).astype(BF16)

    col = pl.BlockSpec((t, pg), lambda g: (0, g))
    return pl.pallas_call(
        body, name=name, grid=(ng,),
        in_specs=[col, pl.BlockSpec((None, pg, pg), lambda g: (g, 0, 0)), pl.BlockSpec((1, pg), lambda g: (0, g))],
        out_specs=[col, col],
        out_shape=[jax.ShapeDtypeStruct((t, d_pool), BF16), jax.ShapeDtypeStruct((t, y_width), BF16)],
        scratch_shapes=[pltpu.VMEM((t + HALO, pg), F32)],
        compiler_params=_params(("parallel",)),
    )(z, pool_w, pool_scale)


def pool_bwd(name, pooled, dy, pool_w, pool_scale, dz):
    t, d_pool = pooled.shape
    ng, pg = pool_w.shape[0], pool_w.shape[1]

    def body(p_ref, dy_ref, w_ref, s_ref, dz_ref, du_ref, dw_ref, ds_ref, pad):
        g = pl.program_id(0)
        w = w_ref[...]
        dyv = dy_ref[...].astype(F32)
        mixed = jnp.dot(p_ref[...], w, preferred_element_type=F32)
        ds_ref[...] = jnp.sum(dyv * mixed, axis=0, keepdims=True)
        dmixed = (dyv * s_ref[...]).astype(BF16)
        dw_ref[...] = lax.dot_general(p_ref[...], dmixed, TN, preferred_element_type=F32)
        pad[...] = jnp.zeros((t + HALO, pg), F32)
        pad[pl.ds(0, t), :] = lax.dot_general(dmixed, w, NT, preferred_element_type=F32)

        def scale(s, tc):
            pad[pl.ds(s, tc), :] = pad[pl.ds(s, tc), :] / _pool_count(s, tc, g)

        def chunk(s, tc):
            win = pad[pl.ds(s, tc + HALO), :]
            total = _pick(_window_sums(win, tc, causal=False), g)
            du_ref[pl.ds(s, tc), :] = (total - win[0:tc, :] * _pool_count(s, tc, g)).astype(BF16)

        _chunks(t, scale)
        _chunks(t, chunk)

    col = pl.BlockSpec((t, pg), lambda g: (0, g))
    vec = pl.BlockSpec((1, pg), lambda g: (0, g))
    mat = pl.BlockSpec((None, pg, pg), lambda g: (g, 0, 0))
    return pl.pallas_call(
        body, name=name, grid=(ng,),
        in_specs=[col, col, mat, vec, ANY], out_specs=[col, mat, vec],
        out_shape=[jax.ShapeDtypeStruct(dz.shape, dz.dtype), jax.ShapeDtypeStruct((ng, pg, pg), F32),
                   jax.ShapeDtypeStruct((1, d_pool), F32)],
        input_output_aliases={4: 0},
        scratch_shapes=[pltpu.VMEM((t + HALO, pg), F32)],
        compiler_params=_params(("parallel",)),
    )(pooled, dy, pool_w, pool_scale, dz)


def conv_fwd(name, z, conv_w, conv_b, d_pool, d_conv):
    t = z.shape[0]
    kw = conv_w.shape[0]
    tc_ch = _tile(d_conv, CHANNEL_TILE)
    v0, g0 = d_pool // tc_ch, (d_pool + d_conv) // tc_ch

    def body(v_ref, g_ref, w_ref, b_ref, c_ref, pad):
        pad[pl.ds(0, HALO), :] = jnp.zeros((HALO, tc_ch), F32)

        def fill(s, tc):
            pad[pl.ds(HALO + s, tc), :] = v_ref[pl.ds(s, tc), :].astype(F32) * jax.nn.sigmoid(g_ref[pl.ds(s, tc), :].astype(F32))

        def chunk(s, tc):
            win = pad[pl.ds(s, tc + HALO), :]
            c_ref[pl.ds(s, tc), :] = _taps(win, w_ref, [HALO - (kw - 1) + k for k in range(kw)], tc) + b_ref[...]

        _chunks(t, fill)
        _chunks(t, chunk)

    return pl.pallas_call(
        body, name=name, grid=(d_conv // tc_ch,),
        in_specs=[pl.BlockSpec((t, tc_ch), lambda j: (0, v0 + j)), pl.BlockSpec((t, tc_ch), lambda j: (0, g0 + j)),
                  pl.BlockSpec((kw, tc_ch), lambda j: (0, j)), pl.BlockSpec((1, tc_ch), lambda j: (0, j))],
        out_specs=pl.BlockSpec((t, tc_ch), lambda j: (0, j)),
        out_shape=jax.ShapeDtypeStruct((t, d_conv), F32),
        scratch_shapes=[pltpu.VMEM((t + HALO, tc_ch), F32)],
        compiler_params=_params(("parallel",)),
    )(z, z, conv_w, conv_b)


def conv_bwd(name, z, dc, conv_w, d_pool, d_conv):
    t = z.shape[0]
    kw = conv_w.shape[0]
    tc_ch = _tile(d_conv, CHANNEL_TILE)
    v0, g0 = d_pool // tc_ch, (d_pool + d_conv) // tc_ch

    def body(v_ref, g_ref, dc_ref, w_ref, dz_ref, dw_ref, db_ref, pad_a, pad_dc, acc_w, acc_b, tiles, sems):
        j = pl.program_id(0)
        dv_ref, dg_ref = tiles.at[0], tiles.at[1]
        writes = [pltpu.make_async_copy(tiles.at[p], dz_ref.at[:, pl.ds((first + j) * tc_ch, tc_ch)], sems.at[p])
                  for p, first in enumerate([v0, g0])]

        def wait_writes():
            for cp in writes:
                cp.wait()

        pad_a[pl.ds(0, HALO), :] = jnp.zeros((HALO, tc_ch), F32)
        pad_dc[pl.ds(t, HALO), :] = jnp.zeros((HALO, tc_ch), F32)
        acc_w[...] = jnp.zeros_like(acc_w)
        acc_b[...] = jnp.zeros_like(acc_b)

        def fill(s, tc):
            pad_a[pl.ds(HALO + s, tc), :] = v_ref[pl.ds(s, tc), :].astype(F32) * jax.nn.sigmoid(g_ref[pl.ds(s, tc), :].astype(F32))
            pad_dc[pl.ds(s, tc), :] = dc_ref[pl.ds(s, tc), :]

        def chunk(s, tc):
            dcv = pad_dc[pl.ds(s, tc), :]
            win_a = pad_a[pl.ds(s, tc + HALO), :]
            for k, rows in _shifted(win_a, [HALO - (kw - 1) + k for k in range(kw)], tc):
                acc_w[pl.ds(8 * k, 8), :] += _fold8(dcv * rows)
            acc_b[...] += _fold8(dcv)
            da = _taps(pad_dc[pl.ds(s, tc + HALO), :], w_ref, list(range(kw)), tc, flip=True)
            vv = v_ref[pl.ds(s, tc), :].astype(F32)
            sg = jax.nn.sigmoid(g_ref[pl.ds(s, tc), :].astype(F32))
            dv_ref[pl.ds(s, tc), :] = (da * sg).astype(BF16)
            dg_ref[pl.ds(s, tc), :] = (da * vv * sg * (1.0 - sg)).astype(BF16)

        _chunks(t, fill)
        pl.when(j > 0)(wait_writes)
        _chunks(t, chunk)
        for cp in writes:
            cp.start()
        pl.when(j == n_tiles - 1)(wait_writes)
        for k in range(kw):
            dw_ref[k:k + 1, :] = jnp.sum(acc_w[pl.ds(8 * k, 8), :], axis=0, keepdims=True)
        db_ref[...] = jnp.sum(acc_b[...], axis=0, keepdims=True)

    n_tiles = d_conv // tc_ch
    return pl.pallas_call(
        body, name=name, grid=(n_tiles,),
        in_specs=[pl.BlockSpec((t, tc_ch), lambda j: (0, v0 + j)), pl.BlockSpec((t, tc_ch), lambda j: (0, g0 + j)),
                  pl.BlockSpec((t, tc_ch), lambda j: (0, j)), pl.BlockSpec((kw, tc_ch), lambda j: (0, j))],
        out_specs=[ANY, pl.BlockSpec((kw, tc_ch), lambda j: (0, j)), pl.BlockSpec((1, tc_ch), lambda j: (0, j))],
        out_shape=[jax.ShapeDtypeStruct((t, d_pool + 2 * d_conv), BF16),
                   jax.ShapeDtypeStruct((kw, d_conv), F32), jax.ShapeDtypeStruct((1, d_conv), F32)],
        scratch_shapes=[pltpu.VMEM((t + HALO, tc_ch), F32), pltpu.VMEM((t + HALO, tc_ch), F32),
                        pltpu.VMEM((8 * kw, tc_ch), F32), pltpu.VMEM((8, tc_ch), F32),
                        pltpu.VMEM((2, t, tc_ch), BF16), pltpu.SemaphoreType.DMA((2,))],
        compiler_params=_params(("arbitrary",)),
    )(z, z, dc, conv_w)


def short_fwd(name, z, conv_w, d_short):
    t = z.shape[0]
    kw = conv_w.shape[0]
    tc_ch = _tile(d_short, CHANNEL_TILE)
    nt = d_short // tc_ch

    def body(b_ref, c_ref, u_ref, w_ref, y_ref, pad):
        pad[pl.ds(0, HALO), :] = jnp.zeros((HALO, tc_ch), F32)

        def fill(s, tc):
            pad[pl.ds(HALO + s, tc), :] = c_ref[pl.ds(s, tc), :].astype(F32) * u_ref[pl.ds(s, tc), :].astype(F32)

        def chunk(s, tc):
            win = pad[pl.ds(s, tc + HALO), :]
            cq = _taps(win, w_ref, [HALO - (kw - 1) + k for k in range(kw)], tc)
            y_ref[pl.ds(s, tc), :] = (b_ref[pl.ds(s, tc), :].astype(F32) * cq).astype(BF16)

        _chunks(t, fill)
        _chunks(t, chunk)

    return pl.pallas_call(
        body, name=name, grid=(nt,),
        in_specs=[pl.BlockSpec((t, tc_ch), lambda j: (0, j)), pl.BlockSpec((t, tc_ch), lambda j: (0, nt + j)),
                  pl.BlockSpec((t, tc_ch), lambda j: (0, 2 * nt + j)), pl.BlockSpec((kw, tc_ch), lambda j: (0, j))],
        out_specs=pl.BlockSpec((t, tc_ch), lambda j: (0, j)),
        out_shape=jax.ShapeDtypeStruct((t, d_short), BF16),
        scratch_shapes=[pltpu.VMEM((t + HALO, tc_ch), F32)],
        compiler_params=_params(("parallel",)),
    )(z, z, z, conv_w)


def short_bwd(name, z, dy, conv_w, d_short):
    t = z.shape[0]
    kw = conv_w.shape[0]
    tc_ch = _tile(d_short, CHANNEL_TILE)
    nt = d_short // tc_ch

    def body(b_ref, c_ref, u_ref, dy_ref, w_ref, dz_ref, dw_ref, pad_q, pad_dcq, acc_w, tiles, sems):
        j = pl.program_id(0)
        db_ref, dcg_ref, du_ref = tiles.at[0], tiles.at[1], tiles.at[2]
        writes = [pltpu.make_async_copy(tiles.at[p], dz_ref.at[:, pl.ds((p * nt + j) * tc_ch, tc_ch)], sems.at[p])
                  for p in range(3)]

        def wait_writes():
            for cp in writes:
                cp.wait()

        pad_q[pl.ds(0, HALO), :] = jnp.zeros((HALO, tc_ch), F32)
        pad_dcq[pl.ds(t, HALO), :] = jnp.zeros((HALO, tc_ch), F32)
        acc_w[...] = jnp.zeros_like(acc_w)

        def fill(s, tc):
            rows = pl.ds(s, tc)
            pad_q[pl.ds(HALO + s, tc), :] = c_ref[rows, :].astype(F32) * u_ref[rows, :].astype(F32)
            pad_dcq[rows, :] = dy_ref[rows, :].astype(F32) * b_ref[rows, :].astype(F32)

        def chunk(s, tc):
            rows = pl.ds(s, tc)
            win_q = pad_q[pl.ds(s, tc + HALO), :]
            dcq = pad_dcq[rows, :]
            cq = None
            for k, shifted in _shifted(win_q, [HALO - (kw - 1) + k for k in range(kw)], tc):
                acc_w[pl.ds(8 * k, 8), :] += _fold8(dcq * shifted)
                term = w_ref[k:k + 1, :] * shifted
                cq = term if cq is None else cq + term
            db_ref[rows, :] = (dy_ref[rows, :].astype(F32) * cq).astype(BF16)
            dq = _taps(pad_dcq[pl.ds(s, tc + HALO), :], w_ref, list(range(kw)), tc, flip=True)
            dcg_ref[rows, :] = (dq * u_ref[rows, :].astype(F32)).astype(BF16)
            du_ref[rows, :] = (dq * c_ref[rows, :].astype(F32)).astype(BF16)

        _chunks(t, fill)
        pl.when(j > 0)(wait_writes)
        _chunks(t, chunk)
        for cp in writes:
            cp.start()
        pl.when(j == nt - 1)(wait_writes)
        for k in range(kw):
            dw_ref[k:k + 1, :] = jnp.sum(acc_w[pl.ds(8 * k, 8), :], axis=0, keepdims=True)

    zspec = [pl.BlockSpec((t, tc_ch), lambda j, o=o: (0, o * nt + j)) for o in range(3)]
    return pl.pallas_call(
        body, name=name, grid=(nt,),
        in_specs=[*zspec, pl.BlockSpec((t, tc_ch), lambda j: (0, j)), pl.BlockSpec((kw, tc_ch), lambda j: (0, j))],
        out_specs=[ANY, pl.BlockSpec((kw, tc_ch), lambda j: (0, j))],
        out_shape=[jax.ShapeDtypeStruct((t, 3 * d_short), BF16), jax.ShapeDtypeStruct((kw, d_short), F32)],
        scratch_shapes=[pltpu.VMEM((t + HALO, tc_ch), F32), pltpu.VMEM((t + HALO, tc_ch), F32),
                        pltpu.VMEM((8 * kw, tc_ch), F32), pltpu.VMEM((3, t, tc_ch), BF16),
                        pltpu.SemaphoreType.DMA((3,))],
        compiler_params=_params(("arbitrary",)),
    )(z, z, z, dy, conv_w)


def _adamw_update(w, m, v, g):
    nm = ADAM_B1 * m + (1.0 - ADAM_B1) * g
    nv = ADAM_B2 * v + (1.0 - ADAM_B2) * (g * g)
    m_hat = nm / (1.0 - ADAM_B1 ** ADAM_STEP)
    v_hat = nv / (1.0 - ADAM_B2 ** ADAM_STEP)
    return -ADAM_LR * (m_hat / (jnp.sqrt(v_hat) + ADAM_EPS) + ADAM_WD * w), nm, nv


def adamw_replicated(name, params, first_moments, second_moments, contributions, layout, scalar_at):
    n = len(params)
    n_slots = contributions.shape[0]

    def total(c_ref, row, lane, rows, lanes):
        acc = c_ref[0, row:row + rows, lane:lane + lanes]
        for slot in range(1, n_slots):
            acc = acc + c_ref[slot, row:row + rows, lane:lane + lanes]
        return acc

    def body(*refs):
        ws, ms, vs, c_ref = refs[:n], refs[n:2 * n], refs[2 * n:3 * n], refs[3 * n]
        outs = refs[3 * n + 1:]
        outs[0][...] = total(c_ref, *scalar_at, 1, 128)
        for i, (row, lane) in enumerate(layout):
            g = total(c_ref, row, lane, *params[i].shape)
            grad_ref, delta_ref, nm_ref, nv_ref = outs[1 + 4 * i:5 + 4 * i]
            grad_ref[...] = g
            delta_ref[...], nm_ref[...], nv_ref[...] = _adamw_update(ws[i][...], ms[i][...], vs[i][...], g)

    out_shape = [jax.ShapeDtypeStruct((1, 128), F32)]
    for p in params:
        out_shape += [jax.ShapeDtypeStruct(p.shape, F32)] * 4
    return pl.pallas_call(body, name=name, out_shape=out_shape)(*params, *first_moments, *second_moments, contributions)


def adamw(name, w, m, v, contributions):
    r, c = w.shape
    nc = len(contributions)
    n_slots = contributions[0].shape[0]
    tr = 256 if c <= 1024 else 128
    if any(a.shape[1] % tr for a in contributions):
        assert nc == 1
        tr = r
    tiles = [a.shape[1] // tr for a in contributions]
    first = [sum(tiles[:j]) for j in range(nc)]

    def body(w_ref, m_ref, v_ref, *rest):
        g_refs, (grad_ref, delta_ref, nm_ref, nv_ref) = rest[:nc], rest[nc:]
        i = pl.program_id(0)
        g = None
        for j, g_ref in enumerate(g_refs):
            s = g_ref[0].astype(F32)
            for slot in range(1, n_slots):
                s = s + g_ref[slot].astype(F32)
            g = s if g is None else jnp.where(i >= first[j], s, g)
        grad_ref[...] = g
        delta_ref[...], nm_ref[...], nv_ref[...] = _adamw_update(w_ref[...], m_ref[...], v_ref[...], g)

    blk = pl.BlockSpec((tr, c), lambda i: (i, 0))
    g_specs = [pl.BlockSpec((n_slots, tr, c), lambda i, j=j: (0, jnp.clip(i - first[j], 0, tiles[j] - 1), 0))
               for j in range(nc)]
    return pl.pallas_call(
        body, name=name, grid=(r // tr,),
        in_specs=[blk, blk, blk, *g_specs],
        out_specs=[blk] * 4, out_shape=[jax.ShapeDtypeStruct((r, c), F32)] * 4,
        compiler_params=_params(("parallel",)),
    )(w, m, v, *contributions)


def _pad_rows(a, rows):
    return jnp.pad(a, ((0, rows - a.shape[0]), (0, 0)))


def kernel(x, mix_pre_g, mix_post_g, ffn_pre_g, ffn_post_g, ab_w_in, pool_w, pool_scale, conv_w, conv_b, conv_ln_g, conv_ln_b, ab_w_out, sc_w_in, sc_conv_w, sc_w_out, ffn_w1, ffn_w2, loss_target, m_mix_pre_g, m_mix_post_g, m_ffn_pre_g, m_ffn_post_g, m_ab_w_in, m_pool_w, m_pool_scale, m_conv_w, m_conv_b, m_conv_ln_g, m_conv_ln_b, m_ab_w_out, m_sc_w_in, m_sc_conv_w, m_sc_w_out, m_ffn_w1, m_ffn_w2, v_mix_pre_g, v_mix_post_g, v_ffn_pre_g, v_ffn_post_g, v_ab_w_in, v_pool_w, v_pool_scale, v_conv_w, v_conv_b, v_conv_ln_g, v_conv_ln_b, v_ab_w_out, v_sc_w_in, v_sc_conv_w, v_sc_w_out, v_ffn_w1, v_ffn_w2):
    t, d = x.shape[1], x.shape[2]
    d_pool = pool_scale.shape[1]
    d_conv = conv_b.shape[1]
    d_short = d
    ng, pg = pool_w.shape[1], pool_w.shape[3]
    kw, ks = conv_w.shape[1], sc_conv_w.shape[1]
    nb_ab, nb_sc, nb_ff = ab_w_in.shape[2], sc_w_in.shape[2], ffn_w1.shape[2]

    xs = x[0]
    target = loss_target[0]

    lanes = min(128, d_conv // N_DEV)
    small_rows = [kw * (d_conv // N_DEV) // lanes, ks * (d_short // N_DEV) // lanes, ng * (pg // N_DEV) * pg // lanes]
    small_total = -(-sum(small_rows) // 8) * 8
    r0, r1, r2 = small_rows[0], small_rows[0] + small_rows[1], sum(small_rows)

    def pack_small(a_conv, a_sconv, a_pool):
        parts = [a_conv[0].reshape(-1, lanes), a_sconv[0].reshape(-1, lanes), a_pool[0].reshape(-1, lanes)]
        return _pad_rows(jnp.concatenate(parts, axis=0), small_total)

    shards = {
        "ab_in": (ab_w_in, 0, BF16), "small": (pack_small(conv_w, sc_conv_w, pool_w)[None], 0, F32),
        "ab_out": (ab_w_out, 0, BF16), "ff1_0": (ffn_w1, 0, BF16), "ff2_0": (ffn_w2, 0, BF16),
        "sc_in": (sc_w_in, 0, BF16), "sc_out": (sc_w_out, 0, BF16),
        "ff1_1": (ffn_w1, 1, BF16), "ff2_1": (ffn_w2, 1, BF16)}
    direct = ["ab_in", "small", "ab_out"]
    zones = {nm: place_shard("place_" + nm, *shards[nm]) for nm in direct}
    started, token = copies_start("gather_start", [[zones[nm]] for nm in direct], _first_hop, 4)
    started = dict(zip(direct, started))
    zones["ff1_0"] = place_shard("place_ff1_0", *shards["ff1_0"], deps=[token])
    (head,), token = copies_start("ring_start_ff1_0", [[zones["ff1_0"]]], _ring_hop1, 3, deps=[token])
    ring = {"ff1_0": head}

    ties = [0]

    def after(v, *deps):
        ties[0] += 1
        return tie(f"tie_{ties[0]}", v, *deps)

    def fetch_begin(nm, dep):
        (zone,) = copies_wait("gather_wait_" + nm, started[nm], _first_hop, dep)
        (hop,), tok = copies_start("forward_start_" + nm, [[zone]], _second_hop, 3)
        return hop, tok

    def fetch_end(nm, hop, dep):
        return copies_wait("forward_wait_" + nm, hop, _second_hop, dep)[0]

    def ring_step(tag, dep, *starts, place=()):
        place = list(place) + [nm for n, nm in starts if n == 1 and nm not in zones and nm not in place]
        for nm in place:
            zones[nm] = place_shard("place_" + nm, *shards[nm], deps=[dep])
        if place:
            dep = after(dep, *[zones[nm] for nm in place])
        names, groups, hops, counts = [], [], [], []
        for n, nm in starts:
            if n == 1:
                groups.append([zones[nm]])
            elif n == 2:
                groups.append(copies_wait("ring1_wait_" + nm, ring[nm], _ring_hop1, dep))
            else:
                groups.append(copies_wait("ring2_wait_" + nm, ring[nm], _ring_hop2, dep))
            hop, n_copies = RING_HOPS[n - 1]
            names, hops, counts = names + [nm], hops + [hop], counts + [n_copies]
        begun, tok = copies_start("ring_start_" + tag, groups, hops, counts, deps=[dep])
        ring.update(zip(names, begun))
        return tok

    def ring_done(nm, dep):
        return copies_wait("ring3_wait_" + nm, ring[nm], _ring_hop3, dep)[0]

    relu = lambda r: jnp.maximum(r, 0.0)
    square = lambda a: a * a
    relu2_bwd = lambda r, a: r * (2.0 * a.astype(F32))

    def row(vec, l):
        return vec[l:l + 1]

    hop_small, _ = fetch_begin("small", token)
    hop_ab_in, tok = fetch_begin("ab_in", token)
    w_small = fetch_end("small", hop_small, tok)
    w_ab_in = fetch_end("ab_in", hop_ab_in, tok)
    w_conv = w_small[:, :r0].reshape(N_DEV, kw, -1).transpose(1, 0, 2).reshape(kw, d_conv)
    w_sconv = w_small[:, r0:r1].reshape(N_DEV, ks, -1).transpose(1, 0, 2).reshape(ks, d_short)
    w_pool = w_small[:, r1:r2].reshape(N_DEV, ng, -1, pg).transpose(1, 0, 2, 3).reshape(ng, pg, pg).astype(BF16)
    h0 = norm_pre("norm_pre", xs, after(row(mix_pre_g, 0), token))
    z0 = mm_nn_blocked("ab_in", h0, w_ab_in, out_dtype=BF16)
    hop, tok = fetch_begin("ab_out", z0)
    z0 = after(z0, tok)
    pooled, y0 = pool_fwd("pool_fwd", z0, w_pool, pool_scale, d_pool, d_pool + d_conv)
    cv = conv_fwd("conv_fwd", z0, w_conv, conv_b, d_pool, d_conv)
    y0 = ln_silu("ln_silu", cv, conv_ln_g, conv_ln_b, y0, d_pool // d_conv)
    w_ab_out = fetch_end("ab_out", hop, y0)

    def ffn_up(name, h, w, *starts):
        a = mm_nn_blocked(name, h, w, out_dtype=BF16, epilogue=relu, blocks=(0, UP_FIRST_BLOCKS))
        tok = ring_step(name, a, *starts)
        return mm_nn_blocked(name + "_rest", h, w, out_dtype=BF16, epilogue=relu,
                             blocks=(UP_FIRST_BLOCKS, N_DEV - UP_FIRST_BLOCKS), into=after(a, tok))

    tok = ring_step("a", w_ab_out, (2, "ff1_0"), (1, "ff2_0"))
    y0 = after(y0, tok)
    m0 = mm_nn("ab_out", y0, w_ab_out.reshape(d_pool + d_conv, d), out_dtype=F32)
    x1, h1 = post_pre("post_pre_0", xs, m0, row(mix_post_g, 0), row(ffn_pre_g, 0))
    tok = ring_step("b", h1, (3, "ff1_0"))
    w_ff1_0 = ring_done("ff1_0", tok)
    a0 = ffn_up("ffn0_up", h1, w_ff1_0, (2, "ff2_0"), (1, "sc_in"), (1, "sc_out"))
    tok = ring_step("c", a0, (3, "ff2_0"), place=["ff1_1"])
    w_ff2_0 = ring_done("ff2_0", tok).reshape(-1, d)
    f0 = mm_nn("ffn0_down", a0, w_ff2_0, out_dtype=F32, tk=2048, lhs_fn=square)
    tok = ring_step("d", f0, (2, "sc_in"), (2, "sc_out"), (1, "ff1_1"))
    f0 = after(f0, tok)
    x2, h2 = post_pre("post_pre_1", x1, f0, row(ffn_post_g, 0), row(mix_pre_g, 1))
    tok = ring_step("e", h2, (3, "sc_in"), place=["ff2_1"])
    w_sc_in = ring_done("sc_in", tok)
    z1 = mm_nn_blocked("sc_in", h2, w_sc_in, out_dtype=BF16)
    y1 = short_fwd("short_fwd", z1, w_sconv, d_short)
    tok = ring_step("f", y1, (3, "sc_out"), (2, "ff1_1"), (1, "ff2_1"))
    w_sc_out = ring_done("sc_out", tok).reshape(d_short, d)
    m1 = mm_nn("sc_out", y1, w_sc_out, out_dtype=F32)
    x3, h3 = post_pre("post_pre_2", x2, m1, row(mix_post_g, 1), row(ffn_pre_g, 1))
    tok = ring_step("g", h3, (3, "ff1_1"))
    w_ff1_1 = ring_done("ff1_1", tok)
    a1 = ffn_up("ffn1_up", h3, w_ff1_1, (2, "ff2_1"))
    tok = ring_step("h", a1, (3, "ff2_1"))
    w_ff2_1 = ring_done("ff2_1", tok).reshape(-1, d)
    f1 = mm_nn("ffn1_down", a1, w_ff2_1, out_dtype=F32, tk=2048, lhs_fn=square)
    dx4, df1, loss_part, dg_ffn_post1 = post_loss("post_loss", x3, f1, row(ffn_post_g, 1), target)

    red = {}

    def reduce_step(dep, begin=None, middle=None):
        tags, groups, hops, counts = [], [], [], []
        if begin is not None:
            tag, g = begin
            tags, groups = tags + [tag], groups + [[g, lax.empty((N_CHIP,) + g.shape[1:], g.dtype)]]
            hops, counts = hops + [_pair_hop], counts + [N_CHIP]
        if middle is not None:
            g, from_sibling = copies_wait("pair_wait_" + middle, red[middle], _pair_hop, dep)
            tags, groups = tags + [middle], groups + [list(pair_add("pair_add_" + middle, g, from_sibling))]
            hops, counts = hops + [_chip_hop], counts + [3]
        begun, tok = copies_start("reduce_start_" + "_".join(tags), groups, hops, counts, deps=[dep])
        red.update(zip(tags, begun))
        return tok

    def reduce_end(tag, dep):
        return copies_wait("chips_wait_" + tag, red[tag], _chip_hop, dep)[1]

    dpre, dw = mm_bwd_pair("ffn1_da_dw2", df1, w_ff2_1, a1, out_dtype=BF16, act_fn=square, epilogue=relu2_bwd)
    dpre = after(dpre, reduce_step(dpre, begin=("ff2_1", dw.reshape(N_DEV, -1, d))))
    dh3, dw = mm_bwd_pair_blocked("ffn1_dh_dw1", dpre, w_ff1_1, h3, out_dtype=BF16)
    dx3, dm1, dg_ffn_pre1, dg_mix_post1 = bwd_pre_post("bwd_3", dx4, x3, row(ffn_pre_g, 1), dh3, m1, row(mix_post_g, 1))
    dm1 = after(dm1, reduce_step(dm1, begin=("ff1_1", dw), middle="ff2_1"))

    dy1, dw = mm_bwd_pair("sc_dy_dwout", dm1, w_sc_out, y1, out_dtype=BF16)
    dy1 = after(dy1, reduce_step(dy1, begin=("sc_out", dw.reshape(N_DEV, -1, d)), middle="ff1_1"))
    dz1, dw_sconv = short_bwd("short_bwd", z1, dy1, w_sconv, d_short)
    dh2, dw = mm_bwd_pair_blocked("sc_dh_dwin", dz1, w_sc_in, h2, out_dtype=BF16)
    dx2, df0, dg_mix_pre1, dg_ffn_post0 = bwd_pre_post("bwd_2", dx3, x2, row(mix_pre_g, 1), dh2, f0, row(ffn_post_g, 0))
    df0 = after(df0, reduce_step(df0, begin=("sc_in", dw), middle="sc_out"))

    dpre, dw = mm_bwd_pair("ffn0_da_dw2", df0, w_ff2_0, a0, out_dtype=BF16, act_fn=square, epilogue=relu2_bwd)
    dpre = after(dpre, reduce_step(dpre, begin=("ff2_0", dw.reshape(N_DEV, -1, d)), middle="sc_in"))
    dh1, dw = mm_bwd_pair_blocked("ffn0_dh_dw1", dpre, w_ff1_0, h1, out_dtype=BF16)
    dx1, dm0, dg_ffn_pre0, dg_mix_post0 = bwd_pre_post("bwd_1", dx2, x1, row(ffn_pre_g, 0), dh1, m0, row(mix_post_g, 0))
    dm0 = after(dm0, reduce_step(dm0, begin=("ff1_0", dw), middle="ff2_0"))

    dy0, dw = mm_bwd_pair("ab_dy_dwout", dm0, w_ab_out.reshape(d_pool + d_conv, d), y0, out_dtype=BF16)
    dy0 = after(dy0, reduce_step(dy0, begin=("ab_out", dw.reshape(N_DEV, -1, d)), middle="ff1_0"))
    dcv, dg_ln_g, dg_ln_b = ln_silu_bwd("ln_silu_bwd", cv, conv_ln_g, conv_ln_b, dy0, d_pool // d_conv)
    dz0, dw_conv, dg_conv_b = conv_bwd("conv_bwd", z0, dcv, w_conv, d_pool, d_conv)
    dz0, dw_pool, dg_pool_scale = pool_bwd("pool_bwd", pooled, dy0, w_pool, pool_scale, dz0)
    small_parts = [
        dw_conv.reshape(kw, N_DEV, -1).transpose(1, 0, 2).reshape(N_DEV, -1, lanes),
        dw_sconv.reshape(ks, N_DEV, -1).transpose(1, 0, 2).reshape(N_DEV, -1, lanes),
        dw_pool.reshape(ng, N_DEV, pg // N_DEV, pg).transpose(1, 0, 2, 3).reshape(N_DEV, -1, lanes),
    ]
    small = jnp.pad(jnp.concatenate(small_parts, axis=1), ((0, 0), (0, small_total - r2), (0, 0)))
    dz0 = after(dz0, reduce_step(dz0, begin=("small", small), middle="ab_out"))
    dh0, dw = mm_bwd_pair_blocked("ab_dh_dwin", dz0, w_ab_in, h0, out_dtype=BF16)
    dh0 = after(dh0, reduce_step(dh0, begin=("ab_in", dw), middle="small"))
    grad_x, dg_mix_pre0 = bwd_pre_final("bwd_0", dx1, xs, row(mix_pre_g, 0), dh0)
    tok = reduce_step(grad_x, middle="ab_in")

    gains = [dg_mix_pre0, dg_mix_pre1, dg_mix_post0, dg_mix_post1, dg_ffn_pre0, dg_ffn_pre1, dg_ffn_post0, dg_ffn_post1]
    pieces = [(g, i, 0) for i, g in enumerate(gains)]
    rep_layout = [(0, 0), (2, 0), (4, 0), (6, 0)]
    offset = 0
    for g in (dg_pool_scale, dg_conv_b, dg_ln_g, dg_ln_b):
        at = (len(gains) + offset // d, offset % d)
        pieces.append((g, *at))
        rep_layout.append(at)
        offset += g.shape[1]
    loss_at = (len(gains) + -(-offset // d), 0)
    rep_zone = place_sheet("place_rep", pieces, loss_part, loss_at, 16, d)
    (rep_hop,), tok = copies_start("rep_start", [[rep_zone]], _first_hop, 4,
                                   deps=[tok])

    def upd(name, w, m, v, contribs):
        shape = w.shape
        flat2 = lambda a: a.reshape(-1, shape[-1])
        outs = adamw(name, flat2(w), flat2(m), flat2(v), contribs)
        return [o.reshape(shape) for o in outs]

    g_ff2 = [reduce_end("ff2_0", tok), reduce_end("ff2_1", tok)]
    o_ff2 = upd("adam_ffn_w2", ffn_w2, m_ffn_w2, v_ffn_w2, g_ff2)
    (rep_zone,) = copies_wait("rep_wait", rep_hop, _first_hop, o_ff2[0])
    (rep_hop,), _ = copies_start("rep_forward_start", [[rep_zone]], _second_hop, 3)
    g_ff1 = [reduce_end("ff1_0", o_ff2[0]), reduce_end("ff1_1", o_ff2[0])]
    o_ff1 = upd("adam_ffn_w1", ffn_w1, m_ffn_w1, v_ffn_w1, g_ff1)
    (rep_all,) = copies_wait("rep_forward_wait", rep_hop, _second_hop, o_ff1[0])
    loss_sum, *o_rep = adamw_replicated(
        "adam_replicated",
        [mix_pre_g, mix_post_g, ffn_pre_g, ffn_post_g, pool_scale, conv_b, conv_ln_g, conv_ln_b],
        [m_mix_pre_g, m_mix_post_g, m_ffn_pre_g, m_ffn_post_g, m_pool_scale, m_conv_b, m_conv_ln_g, m_conv_ln_b],
        [v_mix_pre_g, v_mix_post_g, v_ffn_pre_g, v_ffn_post_g, v_pool_scale, v_conv_b, v_conv_ln_g, v_conv_ln_b],
        rep_all, rep_layout, loss_at)
    loss = loss_sum[0, 0] * (0.5 / d)
    o_sc_out = upd("adam_sc_out", sc_w_out, m_sc_w_out, v_sc_w_out, [reduce_end("sc_out", o_ff1[0])])
    o_sc_in = upd("adam_sc_in", sc_w_in, m_sc_w_in, v_sc_w_in, [reduce_end("sc_in", o_sc_out[0])])
    o_ab_out = upd("adam_ab_out", ab_w_out, m_ab_w_out, v_ab_w_out, [reduce_end("ab_out", o_sc_in[0])])
    o_small = adamw("adam_small", pack_small(conv_w, sc_conv_w, pool_w), pack_small(m_conv_w, m_sc_conv_w, m_pool_w),
                    pack_small(v_conv_w, v_sc_conv_w, v_pool_w), [reduce_end("small", o_ab_out[0])])
    o_ab_in = upd("adam_ab_in", ab_w_in, m_ab_w_in, v_ab_w_in, [reduce_end("ab_in", o_small[0])])

    def unpack_small(o):
        return o[:r0].reshape(conv_w.shape), o[r0:r1].reshape(sc_conv_w.shape), o[r1:r2].reshape(pool_w.shape)

    results = []
    for kind in range(4):
        g_mix_pre, g_mix_post, g_ffn_pre, g_ffn_post, g_scale, g_conv_b, g_ln_g, g_ln_b = o_rep[kind::4]
        s_conv, s_sconv, s_pool = unpack_small(o_small[kind])
        results.append([
            g_mix_pre, g_mix_post, g_ffn_pre, g_ffn_post,
            o_ab_in[kind], s_pool, g_scale, s_conv, g_conv_b, g_ln_g, g_ln_b,
            o_ab_out[kind], o_sc_in[kind], s_sconv, o_sc_out[kind], o_ff1[kind], o_ff2[kind]])

    return (loss, grad_x[None], *results[0], *results[1], *results[2], *results[3])
```

```python
import jax
import jax.numpy as jnp
from jax import lax
from jax.experimental import pallas as pl
from jax.experimental.pallas import tpu as pltpu

F32 = jnp.float32
BF16 = jnp.bfloat16
MESH = pl.DeviceIdType.MESH
ANY = pl.BlockSpec(memory_space=pl.ANY)

NORM_EPS = 1e-6
POOL_WINDOWS = (2, 4, 8, 16)
ADAM_LR = 0.001
ADAM_B1 = 0.9
ADAM_B2 = 0.999
ADAM_EPS = 1e-08
ADAM_WD = 0.01
ADAM_STEP = 10

N_DEV = 8
VMEM_LIMIT = 56 * 1024 * 1024
PAIR_ADD_BLOCK = 1 << 20
MATMUL_ROWS = 2048
UP_FIRST_BLOCKS = 6
ROW_TILE = 256
CHANNEL_TILE = 256
TIME_CHUNK = 64
HALO = 32

NN = (((1,), (0,)), ((), ()))
NT = (((1,), (1,)), ((), ()))
TN = (((0,), (0,)), ((), ()))


def _params(sem):
    return pltpu.CompilerParams(dimension_semantics=sem, vmem_limit_bytes=VMEM_LIMIT)


def _place():
    x, y, c = lax.axis_index("x"), lax.axis_index("y"), lax.axis_index("c")
    return x, y, c


def _slot(px, py, pc):
    return 4 * px + 2 * py + pc


HBM = pl.BlockSpec(memory_space=pltpu.HBM)
SEM = pl.BlockSpec(memory_space=pltpu.SEMAPHORE)
EFFECT = pltpu.SideEffectType.DATAFLOW_SIDE_EFFECTING
TOKEN = jax.ShapeDtypeStruct((8, 128), F32)


def _in_hbm(a):
    return pltpu.with_memory_space_constraint(a, pltpu.HBM)


CHIPS = [(0, 0), (0, 1), (1, 0), (1, 1)]
N_CHIP = len(CHIPS)


def _chip(px, py):
    return 2 * px + py


def _first_hop(bufs, sends, recvs, waiting):
    (land,) = bufs
    x, y, c = _place()
    me = _slot(x, y, c)
    peers = [(x, y, 1 - c), (1 - x, y, c), (x, 1 - y, c), (1 - x, 1 - y, c)]
    return [pltpu.make_async_remote_copy(
        src_ref=land.at[me], dst_ref=land.at[_slot(*p) if waiting else me],
        send_sem=sends.at[k], recv_sem=recvs.at[k], device_id=p, device_id_type=MESH) for k, p in enumerate(peers)]


def _second_hop(bufs, sends, recvs, waiting):
    (land,) = bufs
    x, y, c = _place()
    return [pltpu.make_async_remote_copy(
        src_ref=land.at[_slot(px, py, c)], dst_ref=land.at[_slot(px, py, 1 - c if waiting else c)],
        send_sem=sends.at[k], recv_sem=recvs.at[k], device_id=(x, y, 1 - c), device_id_type=MESH)
        for k, (px, py) in enumerate([(1 - x, y), (x, 1 - y), (1 - x, 1 - y)])]


def _ring_hop1(bufs, sends, recvs, waiting):
    (land,) = bufs
    x, y, c = _place()
    me = _slot(x, y, c)
    peers = [(1 - x, y, c), (x, 1 - y, c), (x, y, 1 - c)]
    return [pltpu.make_async_remote_copy(
        src_ref=land.at[me], dst_ref=land.at[_slot(*p) if waiting else me],
        send_sem=sends.at[k], recv_sem=recvs.at[k], device_id=p, device_id_type=MESH) for k, p in enumerate(peers)]


def _ring_hop2(bufs, sends, recvs, waiting):
    (land,) = bufs
    x, y, c = _place()
    half = land.shape[1] // 2
    first, second = pl.ds(0, half), pl.ds(half, half)
    nx, ny, diag = _slot(1 - x, y, c), _slot(x, 1 - y, c), _slot(1 - x, 1 - y, c)
    plan = [
        (land.at[ny, first], land.at[diag, first], (1 - x, y, c)),
        (land.at[nx, second], land.at[diag, second], (x, 1 - y, c)),
        (land.at[nx], land.at[_slot(1 - x, y, 1 - c)], (x, y, 1 - c)),
        (land.at[ny], land.at[_slot(x, 1 - y, 1 - c)], (x, y, 1 - c))]
    return [pltpu.make_async_remote_copy(
        src_ref=src, dst_ref=mine if waiting else src, send_sem=sends.at[k], recv_sem=recvs.at[k],
        device_id=to, device_id_type=MESH) for k, (src, mine, to) in enumerate(plan)]


def _ring_hop3(bufs, sends, recvs, waiting):
    (land,) = bufs
    x, y, c = _place()
    return [pltpu.make_async_remote_copy(
        src_ref=land.at[_slot(1 - x, 1 - y, c)], dst_ref=land.at[_slot(1 - x, 1 - y, 1 - c if waiting else c)],
        send_sem=sends.at[0], recv_sem=recvs.at[0], device_id=(x, y, 1 - c), device_id_type=MESH)]


RING_HOPS = [(_ring_hop1, 3), (_ring_hop2, 4), (_ring_hop3, 1)]

def _pair_hop(bufs, sends, recvs, waiting):
    g, land = bufs
    x, y, c = _place()
    return [pltpu.make_async_remote_copy(
        src_ref=g.at[_slot(qx, qy, 1 - c)], dst_ref=land.at[q],
        send_sem=sends.at[q], recv_sem=recvs.at[q], device_id=(x, y, 1 - c), device_id_type=MESH)
        for q, (qx, qy) in enumerate(CHIPS)]


def _chip_hop(bufs, sends, recvs, waiting):
    p, land = bufs
    x, y, c = _place()
    return [pltpu.make_async_remote_copy(
        src_ref=p.at[_chip(px, py)], dst_ref=land.at[_chip(px, py) if waiting else _chip(x, y)],
        send_sem=sends.at[k], recv_sem=recvs.at[k], device_id=(px, py, c), device_id_type=MESH)
        for k, (px, py) in enumerate([(1 - x, y), (x, 1 - y), (1 - x, 1 - y)])]


def copies_start(name, groups, hop, n_copies, deps=()):
    flat = [b for grp in groups for b in grp]
    nb, ng = len(flat), len(groups)
    deps = list(deps)
    hops = list(hop) if isinstance(hop, (list, tuple)) else [hop] * ng
    counts = list(n_copies) if isinstance(n_copies, (list, tuple)) else [n_copies] * ng

    def body(*refs):
        ins, token = refs[:nb], refs[-1]
        sems = refs[nb + len(deps):nb + len(deps) + 2 * ng]
        i = 0
        for gi, grp in enumerate(groups):
            for cp in hops[gi](ins[i:i + len(grp)], sems[2 * gi], sems[2 * gi + 1], False):
                cp.start()
            i += len(grp)
        token[...] = jnp.zeros_like(token)

    outs = pl.pallas_call(
        body, name=name,
        out_shape=([pltpu.SemaphoreType.DMA((n,)) for n in counts for _ in range(2)]
                   + [pltpu.HBM(b.shape, b.dtype) for b in flat] + [TOKEN]),
        in_specs=[HBM] * nb + [ANY] * len(deps),
        out_specs=[SEM] * (2 * ng) + [HBM] * nb + [pl.BlockSpec(memory_space=pltpu.VMEM)],
        input_output_aliases={i: 2 * ng + i for i in range(nb)},
        compiler_params=pltpu.CompilerParams(has_side_effects=EFFECT),
    )(*[_in_hbm(b) for b in flat], *deps)
    started, i = [], 0
    for gi, grp in enumerate(groups):
        started.append((outs[2 * gi], outs[2 * gi + 1], list(outs[2 * ng + i:2 * ng + i + len(grp)])))
        i += len(grp)
    return started, outs[-1]


def copies_wait(name, started, hop, after):
    sends, recvs, bufs = started
    nb = len(bufs)

    def body(*refs):
        for cp in hop(refs[:nb], refs[nb], refs[nb + 1], True):
            cp.wait_send()
            cp.wait_recv()

    outs = pl.pallas_call(
        body, name=name,
        out_shape=[pltpu.HBM(b.shape, b.dtype) for b in bufs],
        in_specs=[HBM] * nb + [SEM, SEM, ANY], out_specs=[HBM] * nb,
        input_output_aliases={i: i for i in range(nb)},
        compiler_params=pltpu.CompilerParams(has_side_effects=EFFECT),
    )(*bufs, sends, recvs, after)
    return list(outs)


def place_shard(name, w, layer, dtype, deps=()):
    _, r, c = w.shape
    tr = _tile(r, 1024)
    x, y, core = _place()
    me = _slot(x, y, core).astype(jnp.int32).reshape(1)

    def body(me_ref, w_ref, *rest):
        rest[-1][...] = w_ref[...].astype(dtype)

    return pl.pallas_call(
        body, name=name,
        grid_spec=pltpu.PrefetchScalarGridSpec(
            num_scalar_prefetch=1, grid=(r // tr,),
            in_specs=[pl.BlockSpec((None, tr, c), lambda i, me_ref: (layer, i, 0))] + [ANY] * len(deps),
            out_specs=pl.BlockSpec((None, tr, c), lambda i, me_ref: (me_ref[0], i, 0))),
        out_shape=jax.ShapeDtypeStruct((N_DEV, r, c), dtype),
        compiler_params=_params(("parallel",)),
    )(me, w, *deps)


def place_sheet(name, pieces, total_of, total_at, rows, width):
    x, y, core = _place()
    me = _slot(x, y, core).astype(jnp.int32).reshape(1)

    def body(me_ref, *refs):
        o_ref = refs[-1]
        o_ref[...] = jnp.zeros_like(o_ref)
        for ref, (_, row, lane) in zip(refs, pieces):
            o_ref[row:row + 1, lane:lane + ref.shape[1]] = jnp.sum(ref[...], axis=0, keepdims=True)
        total = jnp.sum(jnp.sum(refs[len(pieces)][...], axis=0, keepdims=True), axis=1, keepdims=True)
        o_ref[total_at[0]:total_at[0] + 1, total_at[1]:total_at[1] + 128] = jnp.broadcast_to(total, (1, 128))

    arrays = [a for a, _, _ in pieces] + [total_of]
    return pl.pallas_call(
        body, name=name,
        grid_spec=pltpu.PrefetchScalarGridSpec(
            num_scalar_prefetch=1, grid=(1,),
            in_specs=[pl.BlockSpec(a.shape, lambda i, me_ref: (0, 0)) for a in arrays],
            out_specs=pl.BlockSpec((None, rows, width), lambda i, me_ref: (me_ref[0], 0, 0))),
        out_shape=jax.ShapeDtypeStruct((N_DEV, rows, width), F32),
    )(me, *arrays)


def tie(name, x, *deps):
    def body(*refs):
        del refs

    return pl.pallas_call(
        body, name=name, out_shape=jax.ShapeDtypeStruct(x.shape, x.dtype),
        in_specs=[ANY] * (1 + len(deps)), out_specs=ANY, input_output_aliases={0: 0},
    )(x, *deps)


def pair_add(name, g, from_sibling):
    _, r, c_dim = g.shape
    tr = r
    while tr * c_dim > PAIR_ADD_BLOCK and tr % 16 == 0:
        tr //= 2
    x, y, core = _place()
    where = jnp.stack([core, _chip(x, y)]).astype(jnp.int32)

    def body(where_ref, g_ref, s_ref, o_ref, zone_ref):
        total = (g_ref[...].astype(F32) + s_ref[...].astype(F32)).astype(o_ref.dtype)
        o_ref[...] = total

        @pl.when(pl.program_id(1) == where_ref[1])
        def _():
            zone_ref[...] = total

    blk = pl.BlockSpec((None, tr, c_dim), lambda i, q, where_ref: (q, i, 0))
    return pl.pallas_call(
        body, name=name,
        grid_spec=pltpu.PrefetchScalarGridSpec(
            num_scalar_prefetch=1, grid=(r // tr, N_CHIP),
            in_specs=[pl.BlockSpec((None, None, tr, c_dim), lambda i, q, where_ref: (q, where_ref[0], i, 0)), blk],
            out_specs=[blk, pl.BlockSpec((None, tr, c_dim), lambda i, q, where_ref: (where_ref[1], i, 0))]),
        out_shape=[jax.ShapeDtypeStruct((N_CHIP, r, c_dim), g.dtype)] * 2,
        compiler_params=_params(("parallel", "arbitrary")),
    )(where, g.reshape(N_CHIP, 2, r, c_dim), from_sibling)


def _matmul(name, lhs, rhs, *, out_shape, out_dtype, grid, lhs_spec, rhs_spec, out_spec, acc_shape,
            lhs_fn=None, epilogue=None, into=None):
    nk = grid[2]
    extra = [] if into is None else [into]

    def body(lhs_ref, rhs_ref, *rest):
        out_ref, scratch = rest[len(extra)], rest[len(extra) + 1:]

        def product():
            a = lhs_ref[...]
            if lhs_fn is not None:
                a = lhs_fn(a)
            return lax.dot_general(a, rhs_ref[...], NN, preferred_element_type=F32)

        def finish(r):
            if epilogue is not None:
                r = epilogue(r)
            out_ref[...] = r.astype(out_dtype)

        if nk == 1:
            finish(product())
        else:
            (acc_ref,) = scratch
            k = pl.program_id(2)

            @pl.when(k == 0)
            def _():
                acc_ref[...] = product()

            @pl.when(jnp.logical_and(k > 0, k < nk - 1))
            def _():
                acc_ref[...] += product()

            @pl.when(k == nk - 1)
            def _():
                finish(acc_ref[...] + product())

    return pl.pallas_call(
        body, name=name, grid=grid,
        out_shape=jax.ShapeDtypeStruct(out_shape, out_dtype),
        in_specs=[lhs_spec, rhs_spec] + [ANY] * len(extra), out_specs=out_spec,
        input_output_aliases={2: 0} if extra else {},
        scratch_shapes=[pltpu.VMEM(acc_shape, F32)] if nk > 1 else [],
        compiler_params=_params(("parallel", "parallel", "arbitrary")),
    )(lhs, rhs, *extra)


def _tile(n, want):
    return want if n % want == 0 else n


def mm_nn(name, x, w, *, out_dtype, tn=512, tk=None, lhs_fn=None, epilogue=None):
    t, kdim = x.shape
    n = w.shape[1]
    tm, tn = _tile(t, MATMUL_ROWS), _tile(n, tn)
    tk = kdim if tk is None else _tile(kdim, tk)
    return _matmul(
        name, x, w, out_shape=(t, n), out_dtype=out_dtype, grid=(t // tm, n // tn, kdim // tk),
        lhs_spec=pl.BlockSpec((tm, tk), lambda i, j, k: (i, k)),
        rhs_spec=pl.BlockSpec((tk, tn), lambda i, j, k: (k, j)),
        out_spec=pl.BlockSpec((tm, tn), lambda i, j, k: (i, j)),
        acc_shape=(tm, tn), lhs_fn=lhs_fn, epilogue=epilogue)


def mm_nn_blocked(name, x, w, *, out_dtype, epilogue=None, blocks=(0, N_DEV), into=None):
    t, kdim = x.shape
    nb = w.shape[2]
    tm = _tile(t, MATMUL_ROWS)
    tn = nb // 2 if nb >= 1024 else nb
    sub = nb // tn
    first, count = blocks
    return _matmul(
        name, x, w, out_shape=(t, N_DEV * nb), out_dtype=out_dtype, grid=(t // tm, count * sub, 1),
        lhs_spec=pl.BlockSpec((tm, kdim), lambda i, j, k: (i, k)),
        rhs_spec=pl.BlockSpec((None, kdim, tn), lambda i, j, k: (first + j // sub, k, j % sub)),
        out_spec=pl.BlockSpec((tm, tn), lambda i, j, k: (i, first * sub + j)),
        acc_shape=(tm, tn), epilogue=epilogue, into=into)


def mm_bwd_pair(name, dy, w, act, *, out_dtype, tile=512, act_fn=None, epilogue=None):
    t, n = dy.shape
    kdim = w.shape[0]
    tile = _tile(kdim, tile)

    def body(dy_ref, w_ref, act_ref, dx_ref, dw_ref):
        a = act_ref[...]
        dx = lax.dot_general(dy_ref[...], w_ref[...], NT, preferred_element_type=F32)
        if epilogue is not None:
            dx = epilogue(dx, a)
        dx_ref[...] = dx.astype(out_dtype)
        if act_fn is not None:
            a = act_fn(a)
        dw_ref[...] = lax.dot_general(a, dy_ref[...], TN, preferred_element_type=F32).astype(out_dtype)

    return pl.pallas_call(
        body, name=name, grid=(kdim // tile,),
        in_specs=[pl.BlockSpec((t, n), lambda j: (0, 0)), pl.BlockSpec((tile, n), lambda j: (j, 0)),
                  pl.BlockSpec((t, tile), lambda j: (0, j))],
        out_specs=[pl.BlockSpec((t, tile), lambda j: (0, j)), pl.BlockSpec((tile, n), lambda j: (j, 0))],
        out_shape=[jax.ShapeDtypeStruct((t, kdim), out_dtype), jax.ShapeDtypeStruct((kdim, n), out_dtype)],
        compiler_params=_params(("parallel",)),
    )(dy, w, act)


def mm_bwd_pair_blocked(name, dz, w, act, *, out_dtype, tile=1024):
    t = dz.shape[0]
    kdim, nb = w.shape[1], w.shape[2]
    tile = _tile(kdim, tile)

    def body(dz_ref, w_ref, act_ref, dx_ref, dw_ref, acc_ref):
        j = pl.program_id(1)
        dw_ref[...] = lax.dot_general(act_ref[...], dz_ref[...], TN, preferred_element_type=F32).astype(out_dtype)

        def product():
            return lax.dot_general(dz_ref[...], w_ref[...], NT, preferred_element_type=F32)

        @pl.when(j == 0)
        def _():
            acc_ref[...] = product()

        @pl.when(jnp.logical_and(j > 0, j < N_DEV - 1))
        def _():
            acc_ref[...] += product()

        @pl.when(j == N_DEV - 1)
        def _():
            dx_ref[...] = (acc_ref[...] + product()).astype(out_dtype)

    return pl.pallas_call(
        body, name=name, grid=(kdim // tile, N_DEV),
        in_specs=[pl.BlockSpec((t, nb), lambda i, j: (0, j)), pl.BlockSpec((None, tile, nb), lambda i, j: (j, i, 0)),
                  pl.BlockSpec((t, tile), lambda i, j: (0, i))],
        out_specs=[pl.BlockSpec((t, tile), lambda i, j: (0, i)),
                   pl.BlockSpec((None, tile, nb), lambda i, j: (j, i, 0))],
        out_shape=[jax.ShapeDtypeStruct((t, kdim), out_dtype), jax.ShapeDtypeStruct((N_DEV, kdim, nb), out_dtype)],
        scratch_shapes=[pltpu.VMEM((t, tile), F32)],
        compiler_params=_params(("parallel", "arbitrary")),
    )(dz, w, act)


def _rstd(v):
    return lax.rsqrt(jnp.mean(v * v, axis=-1, keepdims=True) + NORM_EPS)


def _rms_bwd(v, g, dy):
    r = _rstd(v)
    vhat = v * r
    dvh = dy * g
    dv = r * (dvh - vhat * jnp.mean(dvh * vhat, axis=-1, keepdims=True))
    return dv, dy * vhat


def _fold8(v):
    rows, n = v.shape
    return jnp.sum(v.reshape(rows // 8, 8, n), axis=0)


def _fold_lanes(v):
    out = v[:, 0:128]
    for i in range(1, v.shape[1] // 128):
        out = out + v[:, 128 * i:128 * (i + 1)]
    return out


def _accumulate(ref, v):
    i = pl.program_id(0)

    @pl.when(i == 0)
    def _():
        ref[...] = v

    @pl.when(i > 0)
    def _():
        ref[...] += v


def _row_call(body, name, t, ins, row_in, outs, acc_outs=(), tr=ROW_TILE):
    tr = _tile(t, tr)

    def in_spec(a, tiled):
        if isinstance(tiled, tuple):
            width, j = tiled
            return pl.BlockSpec((tr, width), lambda i: (i, j))
        return pl.BlockSpec((tr, a.shape[1]), lambda i: (i, 0)) if tiled else pl.BlockSpec(a.shape, lambda i: (0, 0))

    in_specs = [in_spec(a, tiled) for a, tiled in zip(ins, row_in)]
    out_specs = [pl.BlockSpec((tr, n), lambda i: (i, 0)) for n, _ in outs]
    out_specs += [pl.BlockSpec((8, n), lambda i: (0, 0)) for n in acc_outs]
    out_shape = [jax.ShapeDtypeStruct((t, n), dt) for n, dt in outs]
    out_shape += [jax.ShapeDtypeStruct((8, n), F32) for n in acc_outs]
    return pl.pallas_call(
        body, name=name, grid=(t // tr,), in_specs=in_specs, out_specs=out_specs, out_shape=out_shape,
        compiler_params=_params(("arbitrary",) if acc_outs else ("parallel",)),
    )(*ins)


def norm_pre(name, x, g):
    t, d = x.shape

    def body(x_ref, g_ref, h_ref):
        v = x_ref[...]
        h_ref[...] = (v * _rstd(v) * g_ref[...]).astype(BF16)

    return _row_call(body, name, t, [x, g], [True, False], [(d, BF16)])[0]


def post_pre(name, x, m, g_post, g_pre):
    t, d = x.shape

    def body(x_ref, m_ref, gp_ref, gn_ref, xo_ref, h_ref):
        mv = m_ref[...]
        xn = x_ref[...] + mv * _rstd(mv) * gp_ref[...]
        xo_ref[...] = xn
        h_ref[...] = (xn * _rstd(xn) * gn_ref[...]).astype(BF16)

    return _row_call(body, name, t, [x, m, g_post, g_pre], [True, True, False, False], [(d, F32), (d, BF16)])


def post_loss(name, x, f, g_post, target):
    t, d = x.shape

    def body(x_ref, f_ref, g_ref, t_ref, dx_ref, df_ref, loss_ref, dg_ref):
        fv = f_ref[...]
        g = g_ref[...]
        out = x_ref[...] + fv * _rstd(fv) * g
        err = out - t_ref[...]
        dx = err * (1.0 / d)
        dx_ref[...] = dx
        dfv, dg_rows = _rms_bwd(fv, g, dx)
        df_ref[...] = dfv.astype(BF16)
        _accumulate(loss_ref, _fold8(_fold_lanes(err * err)))
        _accumulate(dg_ref, _fold8(dg_rows))

    return _row_call(body, name, t, [x, f, g_post, target], [True, True, False, True],
                     [(d, F32), (d, BF16)], acc_outs=(128, d))


def bwd_pre_post(name, dx_out, x_in, g_pre, dh, f_prev, g_post_prev):
    t, d = x_in.shape

    def body(dxo_ref, x_ref, gpre_ref, dh_ref, f_ref, gpost_ref, dxi_ref, df_ref, dgpre_ref, dgpost_ref):
        dxv, dgpre_rows = _rms_bwd(x_ref[...], gpre_ref[...], dh_ref[...].astype(F32))
        dxi = dxo_ref[...] + dxv
        dxi_ref[...] = dxi
        dfv, dgpost_rows = _rms_bwd(f_ref[...], gpost_ref[...], dxi)
        df_ref[...] = dfv.astype(BF16)
        _accumulate(dgpre_ref, _fold8(dgpre_rows))
        _accumulate(dgpost_ref, _fold8(dgpost_rows))

    return _row_call(body, name, t, [dx_out, x_in, g_pre, dh, f_prev, g_post_prev],
                     [True, True, False, True, True, False], [(d, F32), (d, BF16)], acc_outs=(d, d))


def bwd_pre_final(name, dx_out, x_in, g_pre, dh):
    t, d = x_in.shape

    def body(dxo_ref, x_ref, gpre_ref, dh_ref, dxi_ref, dgpre_ref):
        dxv, dgpre_rows = _rms_bwd(x_ref[...], gpre_ref[...], dh_ref[...].astype(F32))
        dxi_ref[...] = dxo_ref[...] + dxv
        _accumulate(dgpre_ref, _fold8(dgpre_rows))

    return _row_call(body, name, t, [dx_out, x_in, g_pre, dh], [True, True, False, True], [(d, F32)], acc_outs=(d,))


def _layer_norm_parts(cv):
    mu = jnp.mean(cv, axis=-1, keepdims=True)
    xc = cv - mu
    rstd = lax.rsqrt(jnp.mean(xc * xc, axis=-1, keepdims=True) + NORM_EPS)
    return xc * rstd, rstd


def ln_silu(name, cv, g, b, y, y_block):
    t, n = cv.shape
    tr = _tile(t, ROW_TILE)

    def body(c_ref, g_ref, b_ref, y_in_ref, y_ref):
        chat, _ = _layer_norm_parts(c_ref[...])
        ln = chat * g_ref[...] + b_ref[...]
        y_ref[...] = (ln * jax.nn.sigmoid(ln)).astype(BF16)

    vec = pl.BlockSpec((1, n), lambda i: (0, 0))
    return pl.pallas_call(
        body, name=name, grid=(t // tr,),
        in_specs=[pl.BlockSpec((tr, n), lambda i: (i, 0)), vec, vec, ANY],
        out_specs=pl.BlockSpec((tr, n), lambda i: (i, y_block)),
        out_shape=jax.ShapeDtypeStruct(y.shape, y.dtype), input_output_aliases={3: 0},
        compiler_params=_params(("parallel",)),
    )(cv, g, b, y)


def ln_silu_bwd(name, cv, g, b, dy, dy_block):
    t, n = cv.shape

    def body(c_ref, g_ref, b_ref, dy_ref, dc_ref, dg_ref, db_ref):
        chat, rstd = _layer_norm_parts(c_ref[...])
        g = g_ref[...]
        ln = chat * g + b_ref[...]
        s = jax.nn.sigmoid(ln)
        dln = dy_ref[...].astype(F32) * (s * (1.0 + ln * (1.0 - s)))
        dchat = dln * g
        dc_ref[...] = rstd * (dchat - jnp.mean(dchat, axis=-1, keepdims=True)
                              - chat * jnp.mean(dchat * chat, axis=-1, keepdims=True))
        _accumulate(dg_ref, _fold8(dln * chat))
        _accumulate(db_ref, _fold8(dln))

    return _row_call(body, name, t, [cv, g, b, dy], [True, False, False, (n, dy_block)], [(n, F32)], acc_outs=(n, n))


def _chunks(t, fn, tc=TIME_CHUNK):
    tc = _tile(t, tc)

    def step(i, carry):
        fn(pl.multiple_of(i * tc, tc), tc)
        return carry

    lax.fori_loop(0, t // tc, step, 0)


def _rows_from(v, start, n):
    res = start % 8
    base = v if res == 0 else pltpu.roll(v, v.shape[0] - res, axis=0)
    return base[start - res:start - res + n, :]


def _shifted(window, offsets, tc):
    rows = window.shape[0]
    by_residue = {}
    for k, off in enumerate(offsets):
        by_residue.setdefault(off % 8, []).append((k, off))
    for res, taps in by_residue.items():
        base = window if res == 0 else pltpu.roll(window, rows - res, axis=0)
        for k, off in taps:
            yield k, base[off - res:off - res + tc, :]


def _taps(window, w_ref, offsets, tc, flip=False):
    acc = None
    for k, rows in _shifted(window, offsets, tc):
        kk = len(offsets) - 1 - k if flip else k
        term = w_ref[kk:kk + 1, :] * rows
        acc = term if acc is None else acc + term
    return acc


def _window_sums(win, tc, causal):
    sums = []
    cur, rows, step = win, tc + HALO, 1
    for _ in POOL_WINDOWS:
        rows -= 8
        if causal:
            cur = cur[8:8 + rows, :] + _rows_from(cur, 8 - step, rows)
            sums.append(cur[rows - tc:rows, :])
        else:
            cur = cur[0:rows, :] + _rows_from(cur, step, rows)
            sums.append(cur[0:tc, :])
        step *= 2
    return sums


def _pick(vals, g):
    out = vals[-1]
    for i in range(len(vals) - 2, -1, -1):
        out = jnp.where(g == i, vals[i], out)
    return out


def _pool_count(s, tc, g):
    t1 = (lax.broadcasted_iota(jnp.int32, (tc, 1), 0) + (s + 1)).astype(F32)
    width = _pick([float(w) for w in POOL_WINDOWS], g)
    return jnp.minimum(t1, width)


def pool_fwd(name, z, pool_w, pool_scale, d_pool, y_width):
    t = z.shape[0]
    ng, pg = pool_w.shape[0], pool_w.shape[1]

    def body(u_ref, w_ref, s_ref, pooled_ref, y_ref, pad):
        g = pl.program_id(0)
        pad[pl.ds(0, HALO), :] = jnp.zeros((HALO, pg), F32)

        def fill(s, tc):
            pad[pl.ds(HALO + s, tc), :] = u_ref[pl.ds(s, tc), :].astype(F32)

        def chunk(s, tc):
            win = pad[pl.ds(s, tc + HALO), :]
            total = _pick(_window_sums(win, tc, causal=True), g)
            pooled = total / _pool_count(s, tc, g) - win[HALO:HALO + tc, :]
            pooled_ref[pl.ds(s, tc), :] = pooled.astype(BF16)

        _chunks(t, fill)
        _chunks(t, chunk)
        mixed = jnp.dot(pooled_ref[...], w_ref[...], preferred_element_type=F32)
        y_ref[...] = (mixed * s_ref[...]).astype(BF16)

    col = pl.BlockSpec((t, pg), lambda g: (0, g))
    return pl.pallas_call(
        body, name=name, grid=(ng,),
        in_specs=[col, pl.BlockSpec((None, pg, pg), lambda g: (g, 0, 0)), pl.BlockSpec((1, pg), lambda g: (0, g))],
        out_specs=[col, col],
        out_shape=[jax.ShapeDtypeStruct((t, d_pool), BF16), jax.ShapeDtypeStruct((t, y_width), BF16)],
        scratch_shapes=[pltpu.VMEM((t + HALO, pg), F32)],
        compiler_params=_params(("parallel",)),
    )(z, pool_w, pool_scale)


def pool_bwd(name, pooled, dy, pool_w, pool_scale, dz):
    t, d_pool = pooled.shape
    ng, pg = pool_w.shape[0], pool_w.shape[1]

    def body(p_ref, dy_ref, w_ref, s_ref, dz_ref, du_ref, dw_ref, ds_ref, pad):
        g = pl.program_id(0)
        w = w_ref[...]
        dyv = dy_ref[...].astype(F32)
        mixed = jnp.dot(p_ref[...], w, preferred_element_type=F32)
        ds_ref[...] = jnp.sum(dyv * mixed, axis=0, keepdims=True)
        dmixed = (dyv * s_ref[...]).astype(BF16)
        dw_ref[...] = lax.dot_general(p_ref[...], dmixed, TN, preferred_element_type=F32)
        pad[...] = jnp.zeros((t + HALO, pg), F32)
        pad[pl.ds(0, t), :] = lax.dot_general(dmixed, w, NT, preferred_element_type=F32)

        def scale(s, tc):
            pad[pl.ds(s, tc), :] = pad[pl.ds(s, tc), :] / _pool_count(s, tc, g)

        def chunk(s, tc):
            win = pad[pl.ds(s, tc + HALO), :]
            total = _pick(_window_sums(win, tc, causal=False), g)
            du_ref[pl.ds(s, tc), :] = (total - win[0:tc, :] * _pool_count(s, tc, g)).astype(BF16)

        _chunks(t, scale)
        _chunks(t, chunk)

    col = pl.BlockSpec((t, pg), lambda g: (0, g))
    vec = pl.BlockSpec((1, pg), lambda g: (0, g))
    mat = pl.BlockSpec((None, pg, pg), lambda g: (g, 0, 0))
    return pl.pallas_call(
        body, name=name, grid=(ng,),
        in_specs=[col, col, mat, vec, ANY], out_specs=[col, mat, vec],
        out_shape=[jax.ShapeDtypeStruct(dz.shape, dz.dtype), jax.ShapeDtypeStruct((ng, pg, pg), F32),
                   jax.ShapeDtypeStruct((1, d_pool), F32)],
        input_output_aliases={4: 0},
        scratch_shapes=[pltpu.VMEM((t + HALO, pg), F32)],
        compiler_params=_params(("parallel",)),
    )(pooled, dy, pool_w, pool_scale, dz)


def conv_fwd(name, z, conv_w, conv_b, d_pool, d_conv):
    t = z.shape[0]
    kw = conv_w.shape[0]
    tc_ch = _tile(d_conv, CHANNEL_TILE)
    v0, g0 = d_pool // tc_ch, (d_pool + d_conv) // tc_ch

    def body(v_ref, g_ref, w_ref, b_ref, c_ref, pad):
        pad[pl.ds(0, HALO), :] = jnp.zeros((HALO, tc_ch), F32)

        def fill(s, tc):
            pad[pl.ds(HALO + s, tc), :] = v_ref[pl.ds(s, tc), :].astype(F32) * jax.nn.sigmoid(g_ref[pl.ds(s, tc), :].astype(F32))

        def chunk(s, tc):
            win = pad[pl.ds(s, tc + HALO), :]
            c_ref[pl.ds(s, tc), :] = _taps(win, w_ref, [HALO - (kw - 1) + k for k in range(kw)], tc) + b_ref[...]

        _chunks(t, fill)
        _chunks(t, chunk)

    return pl.pallas_call(
        body, name=name, grid=(d_conv // tc_ch,),
        in_specs=[pl.BlockSpec((t, tc_ch), lambda j: (0, v0 + j)), pl.BlockSpec((t, tc_ch), lambda j: (0, g0 + j)),
                  pl.BlockSpec((kw, tc_ch), lambda j: (0, j)), pl.BlockSpec((1, tc_ch), lambda j: (0, j))],
        out_specs=pl.BlockSpec((t, tc_ch), lambda j: (0, j)),
        out_shape=jax.ShapeDtypeStruct((t, d_conv), F32),
        scratch_shapes=[pltpu.VMEM((t + HALO, tc_ch), F32)],
        compiler_params=_params(("parallel",)),
    )(z, z, conv_w, conv_b)


def conv_bwd(name, z, dc, conv_w, d_pool, d_conv):
    t = z.shape[0]
    kw = conv_w.shape[0]
    tc_ch = _tile(d_conv, CHANNEL_TILE)
    v0, g0 = d_pool // tc_ch, (d_pool + d_conv) // tc_ch

    def body(v_ref, g_ref, dc_ref, w_ref, dz_ref, dw_ref, db_ref, pad_a, pad_dc, acc_w, acc_b, tiles, sems):
        j = pl.program_id(0)
        dv_ref, dg_ref = tiles.at[0], tiles.at[1]
        writes = [pltpu.make_async_copy(tiles.at[p], dz_ref.at[:, pl.ds((first + j) * tc_ch, tc_ch)], sems.at[p])
                  for p, first in enumerate([v0, g0])]

        def wait_writes():
            for cp in writes:
                cp.wait()

        pad_a[pl.ds(0, HALO), :] = jnp.zeros((HALO, tc_ch), F32)
        pad_dc[pl.ds(t, HALO), :] = jnp.zeros((HALO, tc_ch), F32)
        acc_w[...] = jnp.zeros_like(acc_w)
        acc_b[...] = jnp.zeros_like(acc_b)

        def fill(s, tc):
            pad_a[pl.ds(HALO + s, tc), :] = v_ref[pl.ds(s, tc), :].astype(F32) * jax.nn.sigmoid(g_ref[pl.ds(s, tc), :].astype(F32))
            pad_dc[pl.ds(s, tc), :] = dc_ref[pl.ds(s, tc), :]

        def chunk(s, tc):
            dcv = pad_dc[pl.ds(s, tc), :]
            win_a = pad_a[pl.ds(s, tc + HALO), :]
            for k, rows in _shifted(win_a, [HALO - (kw - 1) + k for k in range(kw)], tc):
                acc_w[pl.ds(8 * k, 8), :] += _fold8(dcv * rows)
            acc_b[...] += _fold8(dcv)
            da = _taps(pad_dc[pl.ds(s, tc + HALO), :], w_ref, list(range(kw)), tc, flip=True)
            vv = v_ref[pl.ds(s, tc), :].astype(F32)
            sg = jax.nn.sigmoid(g_ref[pl.ds(s, tc), :].astype(F32))
            dv_ref[pl.ds(s, tc), :] = (da * sg).astype(BF16)
            dg_ref[pl.ds(s, tc), :] = (da * vv * sg * (1.0 - sg)).astype(BF16)

        _chunks(t, fill)
        pl.when(j > 0)(wait_writes)
        _chunks(t, chunk)
        for cp in writes:
            cp.start()
        pl.when(j == n_tiles - 1)(wait_writes)
        for k in range(kw):
            dw_ref[k:k + 1, :] = jnp.sum(acc_w[pl.ds(8 * k, 8), :], axis=0, keepdims=True)
        db_ref[...] = jnp.sum(acc_b[...], axis=0, keepdims=True)

    n_tiles = d_conv // tc_ch
    return pl.pallas_call(
        body, name=name, grid=(n_tiles,),
        in_specs=[pl.BlockSpec((t, tc_ch), lambda j: (0, v0 + j)), pl.BlockSpec((t, tc_ch), lambda j: (0, g0 + j)),
                  pl.BlockSpec((t, tc_ch), lambda j: (0, j)), pl.BlockSpec((kw, tc_ch), lambda j: (0, j))],
        out_specs=[ANY, pl.BlockSpec((kw, tc_ch), lambda j: (0, j)), pl.BlockSpec((1, tc_ch), lambda j: (0, j))],
        out_shape=[jax.ShapeDtypeStruct((t, d_pool + 2 * d_conv), BF16),
                   jax.ShapeDtypeStruct((kw, d_conv), F32), jax.ShapeDtypeStruct((1, d_conv), F32)],
        scratch_shapes=[pltpu.VMEM((t + HALO, tc_ch), F32), pltpu.VMEM((t + HALO, tc_ch), F32),
                        pltpu.VMEM((8 * kw, tc_ch), F32), pltpu.VMEM((8, tc_ch), F32),
                        pltpu.VMEM((2, t, tc_ch), BF16), pltpu.SemaphoreType.DMA((2,))],
        compiler_params=_params(("arbitrary",)),
    )(z, z, dc, conv_w)


def short_fwd(name, z, conv_w, d_short):
    t = z.shape[0]
    kw = conv_w.shape[0]
    tc_ch = _tile(d_short, CHANNEL_TILE)
    nt = d_short // tc_ch

    def body(b_ref, c_ref, u_ref, w_ref, y_ref, pad):
        pad[pl.ds(0, HALO), :] = jnp.zeros((HALO, tc_ch), F32)

        def fill(s, tc):
            pad[pl.ds(HALO + s, tc), :] = c_ref[pl.ds(s, tc), :].astype(F32) * u_ref[pl.ds(s, tc), :].astype(F32)

        def chunk(s, tc):
            win = pad[pl.ds(s, tc + HALO), :]
            cq = _taps(win, w_ref, [HALO - (kw - 1) + k for k in range(kw)], tc)
            y_ref[pl.ds(s, tc), :] = (b_ref[pl.ds(s, tc), :].astype(F32) * cq).astype(BF16)

        _chunks(t, fill)
        _chunks(t, chunk)

    return pl.pallas_call(
        body, name=name, grid=(nt,),
        in_specs=[pl.BlockSpec((t, tc_ch), lambda j: (0, j)), pl.BlockSpec((t, tc_ch), lambda j: (0, nt + j)),
                  pl.BlockSpec((t, tc_ch), lambda j: (0, 2 * nt + j)), pl.BlockSpec((kw, tc_ch), lambda j: (0, j))],
        out_specs=pl.BlockSpec((t, tc_ch), lambda j: (0, j)),
        out_shape=jax.ShapeDtypeStruct((t, d_short), BF16),
        scratch_shapes=[pltpu.VMEM((t + HALO, tc_ch), F32)],
        compiler_params=_params(("parallel",)),
    )(z, z, z, conv_w)


def short_bwd(name, z, dy, conv_w, d_short):
    t = z.shape[0]
    kw = conv_w.shape[0]
    tc_ch = _tile(d_short, CHANNEL_TILE)
    nt = d_short // tc_ch

    def body(b_ref, c_ref, u_ref, dy_ref, w_ref, dz_ref, dw_ref, pad_q, pad_dcq, acc_w, tiles, sems):
        j = pl.program_id(0)
        db_ref, dcg_ref, du_ref = tiles.at[0], tiles.at[1], tiles.at[2]
        writes = [pltpu.make_async_copy(tiles.at[p], dz_ref.at[:, pl.ds((p * nt + j) * tc_ch, tc_ch)], sems.at[p])
                  for p in range(3)]

        def wait_writes():
            for cp in writes:
                cp.wait()

        pad_q[pl.ds(0, HALO), :] = jnp.zeros((HALO, tc_ch), F32)
        pad_dcq[pl.ds(t, HALO), :] = jnp.zeros((HALO, tc_ch), F32)
        acc_w[...] = jnp.zeros_like(acc_w)

        def fill(s, tc):
            rows = pl.ds(s, tc)
            pad_q[pl.ds(HALO + s, tc), :] = c_ref[rows, :].astype(F32) * u_ref[rows, :].astype(F32)
            pad_dcq[rows, :] = dy_ref[rows, :].astype(F32) * b_ref[rows, :].astype(F32)

        def chunk(s, tc):
            rows = pl.ds(s, tc)
            win_q = pad_q[pl.ds(s, tc + HALO), :]
            dcq = pad_dcq[rows, :]
            cq = None
            for k, shifted in _shifted(win_q, [HALO - (kw - 1) + k for k in range(kw)], tc):
                acc_w[pl.ds(8 * k, 8), :] += _fold8(dcq * shifted)
                term = w_ref[k:k + 1, :] * shifted
                cq = term if cq is None else cq + term
            db_ref[rows, :] = (dy_ref[rows, :].astype(F32) * cq).astype(BF16)
            dq = _taps(pad_dcq[pl.ds(s, tc + HALO), :], w_ref, list(range(kw)), tc, flip=True)
            dcg_ref[rows, :] = (dq * u_ref[rows, :].astype(F32)).astype(BF16)
            du_ref[rows, :] = (dq * c_ref[rows, :].astype(F32)).astype(BF16)

        _chunks(t, fill)
        pl.when(j > 0)(wait_writes)
        _chunks(t, chunk)
        for cp in writes:
            cp.start()
        pl.when(j == nt - 1)(wait_writes)
        for k in range(kw):
            dw_ref[k:k + 1, :] = jnp.sum(acc_w[pl.ds(8 * k, 8), :], axis=0, keepdims=True)

    zspec = [pl.BlockSpec((t, tc_ch), lambda j, o=o: (0, o * nt + j)) for o in range(3)]
    return pl.pallas_call(
        body, name=name, grid=(nt,),
        in_specs=[*zspec, pl.BlockSpec((t, tc_ch), lambda j: (0, j)), pl.BlockSpec((kw, tc_ch), lambda j: (0, j))],
        out_specs=[ANY, pl.BlockSpec((kw, tc_ch), lambda j: (0, j))],
        out_shape=[jax.ShapeDtypeStruct((t, 3 * d_short), BF16), jax.ShapeDtypeStruct((kw, d_short), F32)],
        scratch_shapes=[pltpu.VMEM((t + HALO, tc_ch), F32), pltpu.VMEM((t + HALO, tc_ch), F32),
                        pltpu.VMEM((8 * kw, tc_ch), F32), pltpu.VMEM((3, t, tc_ch), BF16),
                        pltpu.SemaphoreType.DMA((3,))],
        compiler_params=_params(("arbitrary",)),
    )(z, z, z, dy, conv_w)


def _adamw_update(w, m, v, g):
    nm = ADAM_B1 * m + (1.0 - ADAM_B1) * g
    nv = ADAM_B2 * v + (1.0 - ADAM_B2) * (g * g)
    m_hat = nm / (1.0 - ADAM_B1 ** ADAM_STEP)
    v_hat = nv / (1.0 - ADAM_B2 ** ADAM_STEP)
    return -ADAM_LR * (m_hat / (jnp.sqrt(v_hat) + ADAM_EPS) + ADAM_WD * w), nm, nv


def adamw_replicated(name, params, first_moments, second_moments, contributions, layout, scalar_at):
    n = len(params)
    n_slots = contributions.shape[0]

    def total(c_ref, row, lane, rows, lanes):
        acc = c_ref[0, row:row + rows, lane:lane + lanes]
        for slot in range(1, n_slots):
            acc = acc + c_ref[slot, row:row + rows, lane:lane + lanes]
        return acc

    def body(*refs):
        ws, ms, vs, c_ref = refs[:n], refs[n:2 * n], refs[2 * n:3 * n], refs[3 * n]
        outs = refs[3 * n + 1:]
        outs[0][...] = total(c_ref, *scalar_at, 1, 128)
        for i, (row, lane) in enumerate(layout):
            g = total(c_ref, row, lane, *params[i].shape)
            grad_ref, delta_ref, nm_ref, nv_ref = outs[1 + 4 * i:5 + 4 * i]
            grad_ref[...] = g
            delta_ref[...], nm_ref[...], nv_ref[...] = _adamw_update(ws[i][...], ms[i][...], vs[i][...], g)

    out_shape = [jax.ShapeDtypeStruct((1, 128), F32)]
    for p in params:
        out_shape += [jax.ShapeDtypeStruct(p.shape, F32)] * 4
    return pl.pallas_call(body, name=name, out_shape=out_shape)(*params, *first_moments, *second_moments, contributions)


def adamw(name, w, m, v, contributions):
    r, c = w.shape
    nc = len(contributions)
    n_slots = contributions[0].shape[0]
    tr = 256 if c <= 1024 else 128
    if any(a.shape[1] % tr for a in contributions):
        assert nc == 1
        tr = r
    tiles = [a.shape[1] // tr for a in contributions]
    first = [sum(tiles[:j]) for j in range(nc)]

    def body(w_ref, m_ref, v_ref, *rest):
        g_refs, (grad_ref, delta_ref, nm_ref, nv_ref) = rest[:nc], rest[nc:]
        i = pl.program_id(0)
        g = None
        for j, g_ref in enumerate(g_refs):
            s = g_ref[0].astype(F32)
            for slot in range(1, n_slots):
                s = s + g_ref[slot].astype(F32)
            g = s if g is None else jnp.where(i >= first[j], s, g)
        grad_ref[...] = g
        delta_ref[...], nm_ref[...], nv_ref[...] = _adamw_update(w_ref[...], m_ref[...], v_ref[...], g)

    blk = pl.BlockSpec((tr, c), lambda i: (i, 0))
    g_specs = [pl.BlockSpec((n_slots, tr, c), lambda i, j=j: (0, jnp.clip(i - first[j], 0, tiles[j] - 1), 0))
               for j in range(nc)]
    return pl.pallas_call(
        body, name=name, grid=(r // tr,),
        in_specs=[blk, blk, blk, *g_specs],
        out_specs=[blk] * 4, out_shape=[jax.ShapeDtypeStruct((r, c), F32)] * 4,
        compiler_params=_params(("parallel",)),
    )(w, m, v, *contributions)


def _pad_rows(a, rows):
    return jnp.pad(a, ((0, rows - a.shape[0]), (0, 0)))


def kernel(x, mix_pre_g, mix_post_g, ffn_pre_g, ffn_post_g, ab_w_in, pool_w, pool_scale, conv_w, conv_b, conv_ln_g, conv_ln_b, ab_w_out, sc_w_in, sc_conv_w, sc_w_out, ffn_w1, ffn_w2, loss_target, m_mix_pre_g, m_mix_post_g, m_ffn_pre_g, m_ffn_post_g, m_ab_w_in, m_pool_w, m_pool_scale, m_conv_w, m_conv_b, m_conv_ln_g, m_conv_ln_b, m_ab_w_out, m_sc_w_in, m_sc_conv_w, m_sc_w_out, m_ffn_w1, m_ffn_w2, v_mix_pre_g, v_mix_post_g, v_ffn_pre_g, v_ffn_post_g, v_ab_w_in, v_pool_w, v_pool_scale, v_conv_w, v_conv_b, v_conv_ln_g, v_conv_ln_b, v_ab_w_out, v_sc_w_in, v_sc_conv_w, v_sc_w_out, v_ffn_w1, v_ffn_w2):
    t, d = x.shape[1], x.shape[2]
    d_pool = pool_scale.shape[1]
    d_conv = conv_b.shape[1]
    d_short = d
    ng, pg = pool_w.shape[1], pool_w.shape[3]
    kw, ks = conv_w.shape[1], sc_conv_w.shape[1]
    nb_ab, nb_sc, nb_ff = ab_w_in.shape[2], sc_w_in.shape[2], ffn_w1.shape[2]

    xs = x[0]
    target = loss_target[0]

    lanes = min(128, d_conv // N_DEV)
    small_rows = [kw * (d_conv // N_DEV) // lanes, ks * (d_short // N_DEV) // lanes, ng * (pg // N_DEV) * pg // lanes]
    small_total = -(-sum(small_rows) // 8) * 8
    r0, r1, r2 = small_rows[0], small_rows[0] + small_rows[1], sum(small_rows)

    def pack_small(a_conv, a_sconv, a_pool):
        parts = [a_conv[0].reshape(-1, lanes), a_sconv[0].reshape(-1, lanes), a_pool[0].reshape(-1, lanes)]
        return _pad_rows(jnp.concatenate(parts, axis=0), small_total)

    shards = {
        "ab_in": (ab_w_in, 0, BF16), "small": (pack_small(conv_w, sc_conv_w, pool_w)[None], 0, F32),
        "ab_out": (ab_w_out, 0, BF16), "ff1_0": (ffn_w1, 0, BF16), "ff2_0": (ffn_w2, 0, BF16),
        "sc_in": (sc_w_in, 0, BF16), "sc_out": (sc_w_out, 0, BF16),
        "ff1_1": (ffn_w1, 1, BF16), "ff2_1": (ffn_w2, 1, BF16)}
    direct = ["ab_in", "small", "ab_out"]
    zones = {nm: place_shard("place_" + nm, *shards[nm]) for nm in direct}
    started, token = copies_start("gather_start", [[zones[nm]] for nm in direct], _first_hop, 4)
    started = dict(zip(direct, started))
    zones["ff1_0"] = place_shard("place_ff1_0", *shards["ff1_0"], deps=[token])
    (head,), token = copies_start("ring_start_ff1_0", [[zones["ff1_0"]]], _ring_hop1, 3, deps=[token])
    ring = {"ff1_0": head}

    ties = [0]

    def after(v, *deps):
        ties[0] += 1
        return tie(f"tie_{ties[0]}", v, *deps)

    def fetch_begin(nm, dep):
        (zone,) = copies_wait("gather_wait_" + nm, started[nm], _first_hop, dep)
        (hop,), tok = copies_start("forward_start_" + nm, [[zone]], _second_hop, 3)
        return hop, tok

    def fetch_end(nm, hop, dep):
        return copies_wait("forward_wait_" + nm, hop, _second_hop, dep)[0]

    def ring_step(tag, dep, *starts, place=()):
        place = list(place) + [nm for n, nm in starts if n == 1 and nm not in zones and nm not in place]
        before_waits = dep
        for nm in place:
            before_waits = zones[nm] = place_shard("place_" + nm, *shards[nm], deps=[before_waits])
        names, groups, hops, counts = [], [], [], []
        for n, nm in starts:
            if n == 1:
                groups.append([zones[nm]])
            elif n == 2:
                groups.append(copies_wait("ring1_wait_" + nm, ring[nm], _ring_hop1, before_waits))
            else:
                groups.append(copies_wait("ring2_wait_" + nm, ring[nm], _ring_hop2, before_waits))
            hop, n_copies = RING_HOPS[n - 1]
            names, hops, counts = names + [nm], hops + [hop], counts + [n_copies]
        begun, tok = copies_start("ring_start_" + tag, groups, hops, counts, deps=[dep])
        ring.update(zip(names, begun))
        return tok

    def ring_done(nm, dep):
        return copies_wait("ring3_wait_" + nm, ring[nm], _ring_hop3, dep)[0]

    relu = lambda r: jnp.maximum(r, 0.0)
    square = lambda a: a * a
    relu2_bwd = lambda r, a: r * (2.0 * a.astype(F32))

    def row(vec, l):
        return vec[l:l + 1]

    hop_small, _ = fetch_begin("small", token)
    hop_ab_in, tok = fetch_begin("ab_in", token)
    w_small = fetch_end("small", hop_small, tok)
    w_ab_in = fetch_end("ab_in", hop_ab_in, tok)
    w_conv = w_small[:, :r0].reshape(N_DEV, kw, -1).transpose(1, 0, 2).reshape(kw, d_conv)
    w_sconv = w_small[:, r0:r1].reshape(N_DEV, ks, -1).transpose(1, 0, 2).reshape(ks, d_short)
    w_pool = w_small[:, r1:r2].reshape(N_DEV, ng, -1, pg).transpose(1, 0, 2, 3).reshape(ng, pg, pg).astype(BF16)
    h0 = norm_pre("norm_pre", xs, after(row(mix_pre_g, 0), token))
    z0 = mm_nn_blocked("ab_in", h0, w_ab_in, out_dtype=BF16)
    hop, tok = fetch_begin("ab_out", z0)
    z0 = after(z0, tok)
    pooled, y0 = pool_fwd("pool_fwd", z0, w_pool, pool_scale, d_pool, d_pool + d_conv)
    cv = conv_fwd("conv_fwd", z0, w_conv, conv_b, d_pool, d_conv)
    y0 = ln_silu("ln_silu", cv, conv_ln_g, conv_ln_b, y0, d_pool // d_conv)
    w_ab_out = fetch_end("ab_out", hop, y0)

    def ffn_up(name, h, w, *starts):
        a = mm_nn_blocked(name, h, w, out_dtype=BF16, epilogue=relu, blocks=(0, UP_FIRST_BLOCKS))
        tok = ring_step(name, a, *starts)
        return mm_nn_blocked(name + "_rest", h, w, out_dtype=BF16, epilogue=relu,
                             blocks=(UP_FIRST_BLOCKS, N_DEV - UP_FIRST_BLOCKS), into=after(a, tok))

    tok = ring_step("a", w_ab_out, (2, "ff1_0"), (1, "ff2_0"))
    y0 = after(y0, tok)
    m0 = mm_nn("ab_out", y0, w_ab_out.reshape(d_pool + d_conv, d), out_dtype=F32)
    x1, h1 = post_pre("post_pre_0", xs, m0, row(mix_post_g, 0), row(ffn_pre_g, 0))
    tok = ring_step("b", h1, (3, "ff1_0"))
    w_ff1_0 = ring_done("ff1_0", tok)
    a0 = ffn_up("ffn0_up", h1, w_ff1_0, (2, "ff2_0"), (1, "sc_in"), (1, "sc_out"))
    tok = ring_step("c", a0, (3, "ff2_0"), place=["ff1_1"])
    w_ff2_0 = ring_done("ff2_0", tok).reshape(-1, d)
    f0 = mm_nn("ffn0_down", a0, w_ff2_0, out_dtype=F32, tk=2048, lhs_fn=square)
    tok = ring_step("d", f0, (2, "sc_in"), (2, "sc_out"), (1, "ff1_1"))
    f0 = after(f0, tok)
    x2, h2 = post_pre("post_pre_1", x1, f0, row(ffn_post_g, 0), row(mix_pre_g, 1))
    tok = ring_step("e", h2, (3, "sc_in"), place=["ff2_1"])
    w_sc_in = ring_done("sc_in", tok)
    z1 = mm_nn_blocked("sc_in", h2, w_sc_in, out_dtype=BF16)
    y1 = short_fwd("short_fwd", z1, w_sconv, d_short)
    tok = ring_step("f", y1, (3, "sc_out"), (2, "ff1_1"), (1, "ff2_1"))
    w_sc_out = ring_done("sc_out", tok).reshape(d_short, d)
    m1 = mm_nn("sc_out", y1, w_sc_out, out_dtype=F32)
    x3, h3 = post_pre("post_pre_2", x2, m1, row(mix_post_g, 1), row(ffn_pre_g, 1))
    tok = ring_step("g", h3, (3, "ff1_1"))
    w_ff1_1 = ring_done("ff1_1", tok)
    a1 = ffn_up("ffn1_up", h3, w_ff1_1, (2, "ff2_1"))
    tok = ring_step("h", a1, (3, "ff2_1"))
    w_ff2_1 = ring_done("ff2_1", tok).reshape(-1, d)
    f1 = mm_nn("ffn1_down", a1, w_ff2_1, out_dtype=F32, tk=2048, lhs_fn=square)
    dx4, df1, loss_part, dg_ffn_post1 = post_loss("post_loss", x3, f1, row(ffn_post_g, 1), target)

    red = {}

    def reduce_step(dep, begin=None, middle=None):
        tags, groups, hops, counts = [], [], [], []
        if begin is not None:
            tag, g = begin
            tags, groups = tags + [tag], groups + [[g, lax.empty((N_CHIP,) + g.shape[1:], g.dtype)]]
            hops, counts = hops + [_pair_hop], counts + [N_CHIP]
        if middle is not None:
            g, from_sibling = copies_wait("pair_wait_" + middle, red[middle], _pair_hop, dep)
            tags, groups = tags + [middle], groups + [list(pair_add("pair_add_" + middle, g, from_sibling))]
            hops, counts = hops + [_chip_hop], counts + [3]
        begun, tok = copies_start("reduce_start_" + "_".join(tags), groups, hops, counts, deps=[dep])
        red.update(zip(tags, begun))
        return tok

    def reduce_end(tag, dep):
        return copies_wait("chips_wait_" + tag, red[tag], _chip_hop, dep)[1]

    dpre, dw = mm_bwd_pair("ffn1_da_dw2", df1, w_ff2_1, a1, out_dtype=BF16, act_fn=square, epilogue=relu2_bwd)
    dpre = after(dpre, reduce_step(dpre, begin=("ff2_1", dw.reshape(N_DEV, -1, d))))
    dh3, dw = mm_bwd_pair_blocked("ffn1_dh_dw1", dpre, w_ff1_1, h3, out_dtype=BF16)
    dx3, dm1, dg_ffn_pre1, dg_mix_post1 = bwd_pre_post("bwd_3", dx4, x3, row(ffn_pre_g, 1), dh3, m1, row(mix_post_g, 1))
    dm1 = after(dm1, reduce_step(dm1, begin=("ff1_1", dw), middle="ff2_1"))

    dy1, dw = mm_bwd_pair("sc_dy_dwout", dm1, w_sc_out, y1, out_dtype=BF16)
    dy1 = after(dy1, reduce_step(dy1, begin=("sc_out", dw.reshape(N_DEV, -1, d)), middle="ff1_1"))
    dz1, dw_sconv = short_bwd("short_bwd", z1, dy1, w_sconv, d_short)
    dh2, dw = mm_bwd_pair_blocked("sc_dh_dwin", dz1, w_sc_in, h2, out_dtype=BF16)
    dx2, df0, dg_mix_pre1, dg_ffn_post0 = bwd_pre_post("bwd_2", dx3, x2, row(mix_pre_g, 1), dh2, f0, row(ffn_post_g, 0))
    df0 = after(df0, reduce_step(df0, begin=("sc_in", dw), middle="sc_out"))

    dpre, dw = mm_bwd_pair("ffn0_da_dw2", df0, w_ff2_0, a0, out_dtype=BF16, act_fn=square, epilogue=relu2_bwd)
    dpre = after(dpre, reduce_step(dpre, begin=("ff2_0", dw.reshape(N_DEV, -1, d)), middle="sc_in"))
    dh1, dw = mm_bwd_pair_blocked("ffn0_dh_dw1", dpre, w_ff1_0, h1, out_dtype=BF16)
    dx1, dm0, dg_ffn_pre0, dg_mix_post0 = bwd_pre_post("bwd_1", dx2, x1, row(ffn_pre_g, 0), dh1, m0, row(mix_post_g, 0))
    dm0 = after(dm0, reduce_step(dm0, begin=("ff1_0", dw), middle="ff2_0"))

    dy0, dw = mm_bwd_pair("ab_dy_dwout", dm0, w_ab_out.reshape(d_pool + d_conv, d), y0, out_dtype=BF16)
    dy0 = after(dy0, reduce_step(dy0, begin=("ab_out", dw.reshape(N_DEV, -1, d)), middle="ff1_0"))
    dcv, dg_ln_g, dg_ln_b = ln_silu_bwd("ln_silu_bwd", cv, conv_ln_g, conv_ln_b, dy0, d_pool // d_conv)
    dz0, dw_conv, dg_conv_b = conv_bwd("conv_bwd", z0, dcv, w_conv, d_pool, d_conv)
    dz0, dw_pool, dg_pool_scale = pool_bwd("pool_bwd", pooled, dy0, w_pool, pool_scale, dz0)
    small_parts = [
        dw_conv.reshape(kw, N_DEV, -1).transpose(1, 0, 2).reshape(N_DEV, -1, lanes),
        dw_sconv.reshape(ks, N_DEV, -1).transpose(1, 0, 2).reshape(N_DEV, -1, lanes),
        dw_pool.reshape(ng, N_DEV, pg // N_DEV, pg).transpose(1, 0, 2, 3).reshape(N_DEV, -1, lanes),
    ]
    small = jnp.pad(jnp.concatenate(small_parts, axis=1), ((0, 0), (0, small_total - r2), (0, 0)))
    dz0 = after(dz0, reduce_step(dz0, begin=("small", small), middle="ab_out"))
    dh0, dw = mm_bwd_pair_blocked("ab_dh_dwin", dz0, w_ab_in, h0, out_dtype=BF16)
    dh0 = after(dh0, reduce_step(dh0, begin=("ab_in", dw), middle="small"))
    grad_x, dg_mix_pre0 = bwd_pre_final("bwd_0", dx1, xs, row(mix_pre_g, 0), dh0)
    tok = reduce_step(grad_x, middle="ab_in")

    gains = [dg_mix_pre0, dg_mix_pre1, dg_mix_post0, dg_mix_post1, dg_ffn_pre0, dg_ffn_pre1, dg_ffn_post0, dg_ffn_post1]
    pieces = [(g, i, 0) for i, g in enumerate(gains)]
    rep_layout = [(0, 0), (2, 0), (4, 0), (6, 0)]
    offset = 0
    for g in (dg_pool_scale, dg_conv_b, dg_ln_g, dg_ln_b):
        at = (len(gains) + offset // d, offset % d)
        pieces.append((g, *at))
        rep_layout.append(at)
        offset += g.shape[1]
    loss_at = (len(gains) + -(-offset // d), 0)
    rep_zone = place_sheet("place_rep", pieces, loss_part, loss_at, 16, d)
    (rep_hop,), tok = copies_start("rep_start", [[rep_zone]], _first_hop, 4,
                                   deps=[tok])

    def upd(name, w, m, v, contribs):
        shape = w.shape
        flat2 = lambda a: a.reshape(-1, shape[-1])
        outs = adamw(name, flat2(w), flat2(m), flat2(v), contribs)
        return [o.reshape(shape) for o in outs]

    g_ff2 = [reduce_end("ff2_0", tok), reduce_end("ff2_1", tok)]
    o_ff2 = upd("adam_ffn_w2", ffn_w2, m_ffn_w2, v_ffn_w2, g_ff2)
    (rep_zone,) = copies_wait("rep_wait", rep_hop, _first_hop, o_ff2[0])
    (rep_hop,), _ = copies_start("rep_forward_start", [[rep_zone]], _second_hop, 3)
    g_ff1 = [reduce_end("ff1_0", o_ff2[0]), reduce_end("ff1_1", o_ff2[0])]
    o_ff1 = upd("adam_ffn_w1", ffn_w1, m_ffn_w1, v_ffn_w1, g_ff1)
    (rep_all,) = copies_wait("rep_forward_wait", rep_hop, _second_hop, o_ff1[0])
    loss_sum, *o_rep = adamw_replicated(
        "adam_replicated",
        [mix_pre_g, mix_post_g, ffn_pre_g, ffn_post_g, pool_scale, conv_b, conv_ln_g, conv_ln_b],
        [m_mix_pre_g, m_mix_post_g, m_ffn_pre_g, m_ffn_post_g, m_pool_scale, m_conv_b, m_conv_ln_g, m_conv_ln_b],
        [v_mix_pre_g, v_mix_post_g, v_ffn_pre_g, v_ffn_post_g, v_pool_scale, v_conv_b, v_conv_ln_g, v_conv_ln_b],
        rep_all, rep_layout, loss_at)
    loss = loss_sum[0, 0] * (0.5 / d)
    o_sc_out = upd("adam_sc_out", sc_w_out, m_sc_w_out, v_sc_w_out, [reduce_end("sc_out", o_ff1[0])])
    o_sc_in = upd("adam_sc_in", sc_w_in, m_sc_w_in, v_sc_w_in, [reduce_end("sc_in", o_sc_out[0])])
    o_ab_out = upd("adam_ab_out", ab_w_out, m_ab_w_out, v_ab_w_out, [reduce_end("ab_out", o_sc_in[0])])
    o_small = adamw("adam_small", pack_small(conv_w, sc_conv_w, pool_w), pack_small(m_conv_w, m_sc_conv_w, m_pool_w),
                    pack_small(v_conv_w, v_sc_conv_w, v_pool_w), [reduce_end("small", o_ab_out[0])])
    o_ab_in = upd("adam_ab_in", ab_w_in, m_ab_w_in, v_ab_w_in, [reduce_end("ab_in", o_small[0])])

    def unpack_small(o):
        return o[:r0].reshape(conv_w.shape), o[r0:r1].reshape(sc_conv_w.shape), o[r1:r2].reshape(pool_w.shape)

    results = []
    for kind in range(4):
        g_mix_pre, g_mix_post, g_ffn_pre, g_ffn_post, g_scale, g_conv_b, g_ln_g, g_ln_b = o_rep[kind::4]
        s_conv, s_sconv, s_pool = unpack_small(o_small[kind])
        results.append([
            g_mix_pre, g_mix_post, g_ffn_pre, g_ffn_post,
            o_ab_in[kind], s_pool, g_scale, s_conv, g_conv_b, g_ln_g, g_ln_b,
            o_ab_out[kind], o_sc_in[kind], s_sconv, o_sc_out[kind], o_ff1[kind], o_ff2[kind]])

    return (loss, grad_x[None], *results[0], *results[1], *results[2], *results[3])
```

```python
import jax
import jax.numpy as jnp
from jax import lax
from jax.experimental import pallas as pl
from jax.experimental.pallas import tpu as pltpu

F32 = jnp.float32
BF16 = jnp.bfloat16
MESH = pl.DeviceIdType.MESH
ANY = pl.BlockSpec(memory_space=pl.ANY)

NORM_EPS = 1e-6
POOL_WINDOWS = (2, 4, 8, 16)
ADAM_LR = 0.001
ADAM_B1 = 0.9
ADAM_B2 = 0.999
ADAM_EPS = 1e-08
ADAM_WD = 0.01
ADAM_STEP = 10

N_DEV = 8
VMEM_LIMIT = 56 * 1024 * 1024
PAIR_ADD_BLOCK = 1 << 20
MATMUL_ROWS = 2048
UP_FIRST_BLOCKS = 6
ROW_TILE = 256
CHANNEL_TILE = 256
TIME_CHUNK = 64
HALO = 32

NN = (((1,), (0,)), ((), ()))
NT = (((1,), (1,)), ((), ()))
TN = (((0,), (0,)), ((), ()))


def _params(sem):
    return pltpu.CompilerParams(dimension_semantics=sem, vmem_limit_bytes=VMEM_LIMIT)


def _place():
    x, y, c = lax.axis_index("x"), lax.axis_index("y"), lax.axis_index("c")
    return x, y, c


def _slot(px, py, pc):
    return 4 * px + 2 * py + pc


HBM = pl.BlockSpec(memory_space=pltpu.HBM)
SEM = pl.BlockSpec(memory_space=pltpu.SEMAPHORE)
EFFECT = pltpu.SideEffectType.DATAFLOW_SIDE_EFFECTING
TOKEN = jax.ShapeDtypeStruct((8, 128), F32)


def _in_hbm(a):
    return pltpu.with_memory_space_constraint(a, pltpu.HBM)


CHIPS = [(0, 0), (0, 1), (1, 0), (1, 1)]
N_CHIP = len(CHIPS)


def _chip(px, py):
    return 2 * px + py


def _first_hop(bufs, sends, recvs, waiting):
    (land,) = bufs
    x, y, c = _place()
    me = _slot(x, y, c)
    peers = [(x, y, 1 - c), (1 - x, y, c), (x, 1 - y, c), (1 - x, 1 - y, c)]
    return [pltpu.make_async_remote_copy(
        src_ref=land.at[me], dst_ref=land.at[_slot(*p) if waiting else me],
        send_sem=sends.at[k], recv_sem=recvs.at[k], device_id=p, device_id_type=MESH) for k, p in enumerate(peers)]


def _second_hop(bufs, sends, recvs, waiting):
    (land,) = bufs
    x, y, c = _place()
    return [pltpu.make_async_remote_copy(
        src_ref=land.at[_slot(px, py, c)], dst_ref=land.at[_slot(px, py, 1 - c if waiting else c)],
        send_sem=sends.at[k], recv_sem=recvs.at[k], device_id=(x, y, 1 - c), device_id_type=MESH)
        for k, (px, py) in enumerate([(1 - x, y), (x, 1 - y), (1 - x, 1 - y)])]


def _ring_hop1(bufs, sends, recvs, waiting):
    (land,) = bufs
    x, y, c = _place()
    me = _slot(x, y, c)
    peers = [(1 - x, y, c), (x, 1 - y, c), (x, y, 1 - c)]
    return [pltpu.make_async_remote_copy(
        src_ref=land.at[me], dst_ref=land.at[_slot(*p) if waiting else me],
        send_sem=sends.at[k], recv_sem=recvs.at[k], device_id=p, device_id_type=MESH) for k, p in enumerate(peers)]


def _ring_hop2(bufs, sends, recvs, waiting):
    (land,) = bufs
    x, y, c = _place()
    half = land.shape[1] // 2
    first, second = pl.ds(0, half), pl.ds(half, half)
    nx, ny, diag = _slot(1 - x, y, c), _slot(x, 1 - y, c), _slot(1 - x, 1 - y, c)
    plan = [
        (land.at[ny, first], land.at[diag, first], (1 - x, y, c)),
        (land.at[nx, second], land.at[diag, second], (x, 1 - y, c)),
        (land.at[nx], land.at[_slot(1 - x, y, 1 - c)], (x, y, 1 - c)),
        (land.at[ny], land.at[_slot(x, 1 - y, 1 - c)], (x, y, 1 - c))]
    return [pltpu.make_async_remote_copy(
        src_ref=src, dst_ref=mine if waiting else src, send_sem=sends.at[k], recv_sem=recvs.at[k],
        device_id=to, device_id_type=MESH) for k, (src, mine, to) in enumerate(plan)]


def _ring_hop3(bufs, sends, recvs, waiting):
    (land,) = bufs
    x, y, c = _place()
    return [pltpu.make_async_remote_copy(
        src_ref=land.at[_slot(1 - x, 1 - y, c)], dst_ref=land.at[_slot(1 - x, 1 - y, 1 - c if waiting else c)],
        send_sem=sends.at[0], recv_sem=recvs.at[0], device_id=(x, y, 1 - c), device_id_type=MESH)]


RING_HOPS = [(_ring_hop1, 3), (_ring_hop2, 4), (_ring_hop3, 1)]

def _pair_hop(bufs, sends, recvs, waiting):
    g, land = bufs
    x, y, c = _place()
    return [pltpu.make_async_remote_copy(
        src_ref=g.at[_slot(qx, qy, 1 - c)], dst_ref=land.at[q],
        send_sem=sends.at[q], recv_sem=recvs.at[q], device_id=(x, y, 1 - c), device_id_type=MESH)
        for q, (qx, qy) in enumerate(CHIPS)]


def _chip_hop(bufs, sends, recvs, waiting):
    p, land = bufs
    x, y, c = _place()
    return [pltpu.make_async_remote_copy(
        src_ref=p.at[_chip(px, py)], dst_ref=land.at[_chip(px, py) if waiting else _chip(x, y)],
        send_sem=sends.at[k], recv_sem=recvs.at[k], device_id=(px, py, c), device_id_type=MESH)
        for k, (px, py) in enumerate([(1 - x, y), (x, 1 - y), (1 - x, 1 - y)])]


def copies_start(name, groups, hop, n_copies, deps=()):
    flat = [b for grp in groups for b in grp]
    nb, ng = len(flat), len(groups)
    deps = list(deps)
    hops = list(hop) if isinstance(hop, (list, tuple)) else [hop] * ng
    counts = list(n_copies) if isinstance(n_copies, (list, tuple)) else [n_copies] * ng

    def body(*refs):
        ins, token = refs[:nb], refs[-1]
        sems = refs[nb + len(deps):nb + len(deps) + 2 * ng]
        i = 0
        for gi, grp in enumerate(groups):
            for cp in hops[gi](ins[i:i + len(grp)], sems[2 * gi], sems[2 * gi + 1], False):
                cp.start()
            i += len(grp)
        token[...] = jnp.zeros_like(token)

    outs = pl.pallas_call(
        body, name=name,
        out_shape=([pltpu.SemaphoreType.DMA((n,)) for n in counts for _ in range(2)]
                   + [pltpu.HBM(b.shape, b.dtype) for b in flat] + [TOKEN]),
        in_specs=[HBM] * nb + [ANY] * len(deps),
        out_specs=[SEM] * (2 * ng) + [HBM] * nb + [pl.BlockSpec(memory_space=pltpu.VMEM)],
        input_output_aliases={i: 2 * ng + i for i in range(nb)},
        compiler_params=pltpu.CompilerParams(has_side_effects=EFFECT),
    )(*[_in_hbm(b) for b in flat], *deps)
    started, i = [], 0
    for gi, grp in enumerate(groups):
        started.append((outs[2 * gi], outs[2 * gi + 1], list(outs[2 * ng + i:2 * ng + i + len(grp)])))
        i += len(grp)
    return started, outs[-1]


def copies_wait(name, started, hop, after):
    sends, recvs, bufs = started
    nb = len(bufs)

    def body(*refs):
        for cp in hop(refs[:nb], refs[nb], refs[nb + 1], True):
            cp.wait_send()
            cp.wait_recv()

    outs = pl.pallas_call(
        body, name=name,
        out_shape=[pltpu.HBM(b.shape, b.dtype) for b in bufs],
        in_specs=[HBM] * nb + [SEM, SEM, ANY], out_specs=[HBM] * nb,
        input_output_aliases={i: i for i in range(nb)},
        compiler_params=pltpu.CompilerParams(has_side_effects=EFFECT),
    )(*bufs, sends, recvs, after)
    return list(outs)


def place_shard(name, w, layer, dtype, deps=()):
    _, r, c = w.shape
    tr = _tile(r, 1024)
    x, y, core = _place()
    me = _slot(x, y, core).astype(jnp.int32).reshape(1)

    def body(me_ref, w_ref, *rest):
        rest[-1][...] = w_ref[...].astype(dtype)

    return pl.pallas_call(
        body, name=name,
        grid_spec=pltpu.PrefetchScalarGridSpec(
            num_scalar_prefetch=1, grid=(r // tr,),
            in_specs=[pl.BlockSpec((None, tr, c), lambda i, me_ref: (layer, i, 0))] + [ANY] * len(deps),
            out_specs=pl.BlockSpec((None, tr, c), lambda i, me_ref: (me_ref[0], i, 0))),
        out_shape=jax.ShapeDtypeStruct((N_DEV, r, c), dtype),
        compiler_params=_params(("parallel",)),
    )(me, w, *deps)


def place_sheet(name, pieces, total_of, total_at, rows, width):
    x, y, core = _place()
    me = _slot(x, y, core).astype(jnp.int32).reshape(1)

    def body(me_ref, *refs):
        o_ref = refs[-1]
        o_ref[...] = jnp.zeros_like(o_ref)
        for ref, (_, row, lane) in zip(refs, pieces):
            o_ref[row:row + 1, lane:lane + ref.shape[1]] = jnp.sum(ref[...], axis=0, keepdims=True)
        total = jnp.sum(jnp.sum(refs[len(pieces)][...], axis=0, keepdims=True), axis=1, keepdims=True)
        o_ref[total_at[0]:total_at[0] + 1, total_at[1]:total_at[1] + 128] = jnp.broadcast_to(total, (1, 128))

    arrays = [a for a, _, _ in pieces] + [total_of]
    return pl.pallas_call(
        body, name=name,
        grid_spec=pltpu.PrefetchScalarGridSpec(
            num_scalar_prefetch=1, grid=(1,),
            in_specs=[pl.BlockSpec(a.shape, lambda i, me_ref: (0, 0)) for a in arrays],
            out_specs=pl.BlockSpec((None, rows, width), lambda i, me_ref: (me_ref[0], 0, 0))),
        out_shape=jax.ShapeDtypeStruct((N_DEV, rows, width), F32),
    )(me, *arrays)


def tie(name, x, *deps):
    def body(*refs):
        del refs

    return pl.pallas_call(
        body, name=name, out_shape=jax.ShapeDtypeStruct(x.shape, x.dtype),
        in_specs=[ANY] * (1 + len(deps)), out_specs=ANY, input_output_aliases={0: 0},
    )(x, *deps)


def pair_add(name, g, from_sibling):
    _, r, c_dim = g.shape
    tr = r
    while tr * c_dim > PAIR_ADD_BLOCK and tr % 16 == 0:
        tr //= 2
    x, y, core = _place()
    where = jnp.stack([core, _chip(x, y)]).astype(jnp.int32)

    def body(where_ref, g_ref, s_ref, o_ref, zone_ref):
        total = (g_ref[...].astype(F32) + s_ref[...].astype(F32)).astype(o_ref.dtype)
        o_ref[...] = total

        @pl.when(pl.program_id(1) == where_ref[1])
        def _():
            zone_ref[...] = total

    blk = pl.BlockSpec((None, tr, c_dim), lambda i, q, where_ref: (q, i, 0))
    return pl.pallas_call(
        body, name=name,
        grid_spec=pltpu.PrefetchScalarGridSpec(
            num_scalar_prefetch=1, grid=(r // tr, N_CHIP),
            in_specs=[pl.BlockSpec((None, None, tr, c_dim), lambda i, q, where_ref: (q, where_ref[0], i, 0)), blk],
            out_specs=[blk, pl.BlockSpec((None, tr, c_dim), lambda i, q, where_ref: (where_ref[1], i, 0))]),
        out_shape=[jax.ShapeDtypeStruct((N_CHIP, r, c_dim), g.dtype)] * 2,
        compiler_params=_params(("parallel", "arbitrary")),
    )(where, g.reshape(N_CHIP, 2, r, c_dim), from_sibling)


def _matmul(name, lhs, rhs, *, out_shape, out_dtype, grid, lhs_spec, rhs_spec, out_spec, acc_shape,
            lhs_fn=None, epilogue=None, into=None):
    nk = grid[2]
    extra = [] if into is None else [into]

    def body(lhs_ref, rhs_ref, *rest):
        out_ref, scratch = rest[len(extra)], rest[len(extra) + 1:]

        def product():
            a = lhs_ref[...]
            if lhs_fn is not None:
                a = lhs_fn(a)
            return lax.dot_general(a, rhs_ref[...], NN, preferred_element_type=F32)

        def finish(r):
            if epilogue is not None:
                r = epilogue(r)
            out_ref[...] = r.astype(out_dtype)

        if nk == 1:
            finish(product())
        else:
            (acc_ref,) = scratch
            k = pl.program_id(2)

            @pl.when(k == 0)
            def _():
                acc_ref[...] = product()

            @pl.when(jnp.logical_and(k > 0, k < nk - 1))
            def _():
                acc_ref[...] += product()

            @pl.when(k == nk - 1)
            def _():
                finish(acc_ref[...] + product())

    return pl.pallas_call(
        body, name=name, grid=grid,
        out_shape=jax.ShapeDtypeStruct(out_shape, out_dtype),
        in_specs=[lhs_spec, rhs_spec] + [ANY] * len(extra), out_specs=out_spec,
        input_output_aliases={2: 0} if extra else {},
        scratch_shapes=[pltpu.VMEM(acc_shape, F32)] if nk > 1 else [],
        compiler_params=_params(("parallel", "parallel", "arbitrary")),
    )(lhs, rhs, *extra)


def _tile(n, want):
    return want if n % want == 0 else n


def mm_nn(name, x, w, *, out_dtype, tn=512, tk=None, lhs_fn=None, epilogue=None):
    t, kdim = x.shape
    n = w.shape[1]
    tm, tn = _tile(t, MATMUL_ROWS), _tile(n, tn)
    tk = kdim if tk is None else _tile(kdim, tk)
    return _matmul(
        name, x, w, out_shape=(t, n), out_dtype=out_dtype, grid=(t // tm, n // tn, kdim // tk),
        lhs_spec=pl.BlockSpec((tm, tk), lambda i, j, k: (i, k)),
        rhs_spec=pl.BlockSpec((tk, tn), lambda i, j, k: (k, j)),
        out_spec=pl.BlockSpec((tm, tn), lambda i, j, k: (i, j)),
        acc_shape=(tm, tn), lhs_fn=lhs_fn, epilogue=epilogue)


def mm_nn_blocked(name, x, w, *, out_dtype, epilogue=None, blocks=(0, N_DEV), into=None):
    t, kdim = x.shape
    nb = w.shape[2]
    tm = _tile(t, MATMUL_ROWS)
    tn = nb // 2 if nb >= 1024 else nb
    sub = nb // tn
    first, count = blocks
    return _matmul(
        name, x, w, out_shape=(t, N_DEV * nb), out_dtype=out_dtype, grid=(t // tm, count * sub, 1),
        lhs_spec=pl.BlockSpec((tm, kdim), lambda i, j, k: (i, k)),
        rhs_spec=pl.BlockSpec((None, kdim, tn), lambda i, j, k: (first + j // sub, k, j % sub)),
        out_spec=pl.BlockSpec((tm, tn), lambda i, j, k: (i, first * sub + j)),
        acc_shape=(tm, tn), epilogue=epilogue, into=into)


def mm_bwd_pair(name, dy, w, act, *, out_dtype, tile=512, act_fn=None, epilogue=None):
    t, n = dy.shape
    kdim = w.shape[0]
    tile = _tile(kdim, tile)

    def body(dy_ref, w_ref, act_ref, dx_ref, dw_ref):
        a = act_ref[...]
        dx = lax.dot_general(dy_ref[...], w_ref[...], NT, preferred_element_type=F32)
        if epilogue is not None:
            dx = epilogue(dx, a)
        dx_ref[...] = dx.astype(out_dtype)
        if act_fn is not None:
            a = act_fn(a)
        dw_ref[...] = lax.dot_general(a, dy_ref[...], TN, preferred_element_type=F32).astype(out_dtype)

    return pl.pallas_call(
        body, name=name, grid=(kdim // tile,),
        in_specs=[pl.BlockSpec((t, n), lambda j: (0, 0)), pl.BlockSpec((tile, n), lambda j: (j, 0)),
                  pl.BlockSpec((t, tile), lambda j: (0, j))],
        out_specs=[pl.BlockSpec((t, tile), lambda j: (0, j)), pl.BlockSpec((tile, n), lambda j: (j, 0))],
        out_shape=[jax.ShapeDtypeStruct((t, kdim), out_dtype), jax.ShapeDtypeStruct((kdim, n), out_dtype)],
        compiler_params=_params(("parallel",)),
    )(dy, w, act)


def mm_bwd_pair_blocked(name, dz, w, act, *, out_dtype, tile=1024):
    t = dz.shape[0]
    kdim, nb = w.shape[1], w.shape[2]
    tile = _tile(kdim, tile)

    def body(dz_ref, w_ref, act_ref, dx_ref, dw_ref, acc_ref):
        j = pl.program_id(1)
        dw_ref[...] = lax.dot_general(act_ref[...], dz_ref[...], TN, preferred_element_type=F32).astype(out_dtype)

        def product():
            return lax.dot_general(dz_ref[...], w_ref[...], NT, preferred_element_type=F32)

        @pl.when(j == 0)
        def _():
            acc_ref[...] = product()

        @pl.when(jnp.logical_and(j > 0, j < N_DEV - 1))
        def _():
            acc_ref[...] += product()

        @pl.when(j == N_DEV - 1)
        def _():
            dx_ref[...] = (acc_ref[...] + product()).astype(out_dtype)

    return pl.pallas_call(
        body, name=name, grid=(kdim // tile, N_DEV),
        in_specs=[pl.BlockSpec((t, nb), lambda i, j: (0, j)), pl.BlockSpec((None, tile, nb), lambda i, j: (j, i, 0)),
                  pl.BlockSpec((t, tile), lambda i, j: (0, i))],
        out_specs=[pl.BlockSpec((t, tile), lambda i, j: (0, i)),
                   pl.BlockSpec((None, tile, nb), lambda i, j: (j, i, 0))],
        out_shape=[jax.ShapeDtypeStruct((t, kdim), out_dtype), jax.ShapeDtypeStruct((N_DEV, kdim, nb), out_dtype)],
        scratch_shapes=[pltpu.VMEM((t, tile), F32)],
        compiler_params=_params(("parallel", "arbitrary")),
    )(dz, w, act)


def _rstd(v):
    return lax.rsqrt(jnp.mean(v * v, axis=-1, keepdims=True) + NORM_EPS)


def _rms_bwd(v, g, dy):
    r = _rstd(v)
    vhat = v * r
    dvh = dy * g
    dv = r * (dvh - vhat * jnp.mean(dvh * vhat, axis=-1, keepdims=True))
    return dv, dy * vhat


def _fold8(v):
    rows, n = v.shape
    return jnp.sum(v.reshape(rows // 8, 8, n), axis=0)


def _fold_lanes(v):
    out = v[:, 0:128]
    for i in range(1, v.shape[1] // 128):
        out = out + v[:, 128 * i:128 * (i + 1)]
    return out


def _accumulate(ref, v):
    i = pl.program_id(0)

    @pl.when(i == 0)
    def _():
        ref[...] = v

    @pl.when(i > 0)
    def _():
        ref[...] += v


def _row_call(body, name, t, ins, row_in, outs, acc_outs=(), tr=ROW_TILE):
    tr = _tile(t, tr)

    def in_spec(a, tiled):
        if isinstance(tiled, tuple):
            width, j = tiled
            return pl.BlockSpec((tr, width), lambda i: (i, j))
        return pl.BlockSpec((tr, a.shape[1]), lambda i: (i, 0)) if tiled else pl.BlockSpec(a.shape, lambda i: (0, 0))

    in_specs = [in_spec(a, tiled) for a, tiled in zip(ins, row_in)]
    out_specs = [pl.BlockSpec((tr, n), lambda i: (i, 0)) for n, _ in outs]
    out_specs += [pl.BlockSpec((8, n), lambda i: (0, 0)) for n in acc_outs]
    out_shape = [jax.ShapeDtypeStruct((t, n), dt) for n, dt in outs]
    out_shape += [jax.ShapeDtypeStruct((8, n), F32) for n in acc_outs]
    return pl.pallas_call(
        body, name=name, grid=(t // tr,), in_specs=in_specs, out_specs=out_specs, out_shape=out_shape,
        compiler_params=_params(("arbitrary",) if acc_outs else ("parallel",)),
    )(*ins)


def norm_pre(name, x, g):
    t, d = x.shape

    def body(x_ref, g_ref, h_ref):
        v = x_ref[...]
        h_ref[...] = (v * _rstd(v) * g_ref[...]).astype(BF16)

    return _row_call(body, name, t, [x, g], [True, False], [(d, BF16)])[0]


def post_pre(name, x, m, g_post, g_pre):
    t, d = x.shape

    def body(x_ref, m_ref, gp_ref, gn_ref, xo_ref, h_ref):
        mv = m_ref[...]
        xn = x_ref[...] + mv * _rstd(mv) * gp_ref[...]
        xo_ref[...] = xn
        h_ref[...] = (xn * _rstd(xn) * gn_ref[...]).astype(BF16)

    return _row_call(body, name, t, [x, m, g_post, g_pre], [True, True, False, False], [(d, F32), (d, BF16)])


def post_loss(name, x, f, g_post, target):
    t, d = x.shape

    def body(x_ref, f_ref, g_ref, t_ref, dx_ref, df_ref, loss_ref, dg_ref):
        fv = f_ref[...]
        g = g_ref[...]
        out = x_ref[...] + fv * _rstd(fv) * g
        err = out - t_ref[...]
        dx = err * (1.0 / d)
        dx_ref[...] = dx
        dfv, dg_rows = _rms_bwd(fv, g, dx)
        df_ref[...] = dfv.astype(BF16)
        _accumulate(loss_ref, _fold8(_fold_lanes(err * err)))
        _accumulate(dg_ref, _fold8(dg_rows))

    return _row_call(body, name, t, [x, f, g_post, target], [True, True, False, True],
                     [(d, F32), (d, BF16)], acc_outs=(128, d))


def bwd_pre_post(name, dx_out, x_in, g_pre, dh, f_prev, g_post_prev):
    t, d = x_in.shape

    def body(dxo_ref, x_ref, gpre_ref, dh_ref, f_ref, gpost_ref, dxi_ref, df_ref, dgpre_ref, dgpost_ref):
        dxv, dgpre_rows = _rms_bwd(x_ref[...], gpre_ref[...], dh_ref[...].astype(F32))
        dxi = dxo_ref[...] + dxv
        dxi_ref[...] = dxi
        dfv, dgpost_rows = _rms_bwd(f_ref[...], gpost_ref[...], dxi)
        df_ref[...] = dfv.astype(BF16)
        _accumulate(dgpre_ref, _fold8(dgpre_rows))
        _accumulate(dgpost_ref, _fold8(dgpost_rows))

    return _row_call(body, name, t, [dx_out, x_in, g_pre, dh, f_prev, g_post_prev],
                     [True, True, False, True, True, False], [(d, F32), (d, BF16)], acc_outs=(d, d))


def bwd_pre_final(name, dx_out, x_in, g_pre, dh):
    t, d = x_in.shape

    def body(dxo_ref, x_ref, gpre_ref, dh_ref, dxi_ref, dgpre_ref):
        dxv, dgpre_rows = _rms_bwd(x_ref[...], gpre_ref[...], dh_ref[...].astype(F32))
        dxi_ref[...] = dxo_ref[...] + dxv
        _accumulate(dgpre_ref, _fold8(dgpre_rows))

    return _row_call(body, name, t, [dx_out, x_in, g_pre, dh], [True, True, False, True], [(d, F32)], acc_outs=(d,))


def _layer_norm_parts(cv):
    mu = jnp.mean(cv, axis=-1, keepdims=True)
    xc = cv - mu
    rstd = lax.rsqrt(jnp.mean(xc * xc, axis=-1, keepdims=True) + NORM_EPS)
    return xc * rstd, rstd


def ln_silu(name, cv, g, b, y, y_block):
    t, n = cv.shape
    tr = _tile(t, ROW_TILE)

    def body(c_ref, g_ref, b_ref, y_in_ref, y_ref):
        chat, _ = _layer_norm_parts(c_ref[...])
        ln = chat * g_ref[...] + b_ref[...]
        y_ref[...] = (ln * jax.nn.sigmoid(ln)).astype(BF16)

    vec = pl.BlockSpec((1, n), lambda i: (0, 0))
    return pl.pallas_call(
        body, name=name, grid=(t // tr,),
        in_specs=[pl.BlockSpec((tr, n), lambda i: (i, 0)), vec, vec, ANY],
        out_specs=pl.BlockSpec((tr, n), lambda i: (i, y_block)),
        out_shape=jax.ShapeDtypeStruct(y.shape, y.dtype), input_output_aliases={3: 0},
        compiler_params=_params(("parallel",)),
    )(cv, g, b, y)


def ln_silu_bwd(name, cv, g, b, dy, dy_block):
    t, n = cv.shape

    def body(c_ref, g_ref, b_ref, dy_ref, dc_ref, dg_ref, db_ref):
        chat, rstd = _layer_norm_parts(c_ref[...])
        g = g_ref[...]
        ln = chat * g + b_ref[...]
        s = jax.nn.sigmoid(ln)
        dln = dy_ref[...].astype(F32) * (s * (1.0 + ln * (1.0 - s)))
        dchat = dln * g
        dc_ref[...] = rstd * (dchat - jnp.mean(dchat, axis=-1, keepdims=True)
                              - chat * jnp.mean(dchat * chat, axis=-1, keepdims=True))
        _accumulate(dg_ref, _fold8(dln * chat))
        _accumulate(db_ref, _fold8(dln))

    return _row_call(body, name, t, [cv, g, b, dy], [True, False, False, (n, dy_block)], [(n, F32)], acc_outs=(n, n))


def _chunks(t, fn, tc=TIME_CHUNK):
    tc = _tile(t, tc)

    def step(i, carry):
        fn(pl.multiple_of(i * tc, tc), tc)
        return carry

    lax.fori_loop(0, t // tc, step, 0)


def _rows_from(v, start, n):
    res = start % 8
    base = v if res == 0 else pltpu.roll(v, v.shape[0] - res, axis=0)
    return base[start - res:start - res + n, :]


def _shifted(window, offsets, tc):
    rows = window.shape[0]
    by_residue = {}
    for k, off in enumerate(offsets):
        by_residue.setdefault(off % 8, []).append((k, off))
    for res, taps in by_residue.items():
        base = window if res == 0 else pltpu.roll(window, rows - res, axis=0)
        for k, off in taps:
            yield k, base[off - res:off - res + tc, :]


def _taps(window, w_ref, offsets, tc, flip=False):
    acc = None
    for k, rows in _shifted(window, offsets, tc):
        kk = len(offsets) - 1 - k if flip else k
        term = w_ref[kk:kk + 1, :] * rows
        acc = term if acc is None else acc + term
    return acc


def _window_sums(win, tc, causal):
    sums = []
    cur, rows, step = win, tc + HALO, 1
    for _ in POOL_WINDOWS:
        rows -= 8
        if causal:
            cur = cur[8:8 + rows, :] + _rows_from(cur, 8 - step, rows)
            sums.append(cur[rows - tc:rows, :])
        else:
            cur = cur[0:rows, :] + _rows_from(cur, step, rows)
            sums.append(cur[0:tc, :])
        step *= 2
    return sums


def _pick(vals, g):
    out = vals[-1]
    for i in range(len(vals) - 2, -1, -1):
        out = jnp.where(g == i, vals[i], out)
    return out


def _pool_count(s, tc, g):
    t1 = (lax.broadcasted_iota(jnp.int32, (tc, 1), 0) + (s + 1)).astype(F32)
    width = _pick([float(w) for w in POOL_WINDOWS], g)
    return jnp.minimum(t1, width)


def pool_fwd(name, z, pool_w, pool_scale, d_pool, y_width):
    t = z.shape[0]
    ng, pg = pool_w.shape[0], pool_w.shape[1]

    def body(u_ref, w_ref, s_ref, pooled_ref, y_ref, pad):
        g = pl.program_id(0)
        pad[pl.ds(0, HALO), :] = jnp.zeros((HALO, pg), F32)

        def fill(s, tc):
            pad[pl.ds(HALO + s, tc), :] = u_ref[pl.ds(s, tc), :].astype(F32)

        def chunk(s, tc):
            win = pad[pl.ds(s, tc + HALO), :]
            total = _pick(_window_sums(win, tc, causal=True), g)
            pooled = total / _pool_count(s, tc, g) - win[HALO:HALO + tc, :]
            pooled_ref[pl.ds(s, tc), :] = pooled.astype(BF16)

        _chunks(t, fill)
        _chunks(t, chunk)
        mixed = jnp.dot(pooled_ref[...], w_ref[...], preferred_element_type=F32)
        y_ref[...] = (mixed * s_ref[...]).astype(BF16)

    col = pl.BlockSpec((t, pg), lambda g: (0, g))
    return pl.pallas_call(
        body, name=name, grid=(ng,),
        in_specs=[col, pl.BlockSpec((None, pg, pg), lambda g: (g, 0, 0)), pl.BlockSpec((1, pg), lambda g: (0, g))],
        out_specs=[col, col],
        out_shape=[jax.ShapeDtypeStruct((t, d_pool), BF16), jax.ShapeDtypeStruct((t, y_width), BF16)],
        scratch_shapes=[pltpu.VMEM((t + HALO, pg), F32)],
        compiler_params=_params(("parallel",)),
    )(z, pool_w, pool_scale)


def pool_bwd(name, pooled, dy, pool_w, pool_scale, dz):
    t, d_pool = pooled.shape
    ng, pg = pool_w.shape[0], pool_w.shape[1]

    def body(p_ref, dy_ref, w_ref, s_ref, dz_ref, du_ref, dw_ref, ds_ref, pad):
        g = pl.program_id(0)
        w = w_ref[...]
        dyv = dy_ref[...].astype(F32)
        mixed = jnp.dot(p_ref[...], w, preferred_element_type=F32)
        ds_ref[...] = jnp.sum(dyv * mixed, axis=0, keepdims=True)
        dmixed = (dyv * s_ref[...]).astype(BF16)
        dw_ref[...] = lax.dot_general(p_ref[...], dmixed, TN, preferred_element_type=F32)
        pad[...] = jnp.zeros((t + HALO, pg), F32)
        pad[pl.ds(0, t), :] = lax.dot_general(dmixed, w, NT, preferred_element_type=F32)

        def scale(s, tc):
            pad[pl.ds(s, tc), :] = pad[pl.ds(s, tc), :] / _pool_count(s, tc, g)

        def chunk(s, tc):
            win = pad[pl.ds(s, tc + HALO), :]
            total = _pick(_window_sums(win, tc, causal=False), g)
            du_ref[pl.ds(s, tc), :] = (total - win[0:tc, :] * _pool_count(s, tc, g)).astype(BF16)

        _chunks(t, scale)
        _chunks(t, chunk)

    col = pl.BlockSpec((t, pg), lambda g: (0, g))
    vec = pl.BlockSpec((1, pg), lambda g: (0, g))
    mat = pl.BlockSpec((None, pg, pg), lambda g: (g, 0, 0))
    return pl.pallas_call(
        body, name=name, grid=(ng,),
        in_specs=[col, col, mat, vec, ANY], out_specs=[col, mat, vec],
        out_shape=[jax.ShapeDtypeStruct(dz.shape, dz.dtype), jax.ShapeDtypeStruct((ng, pg, pg), F32),
                   jax.ShapeDtypeStruct((1, d_pool), F32)],
        input_output_aliases={4: 0},
        scratch_shapes=[pltpu.VMEM((t + HALO, pg), F32)],
        compiler_params=_params(("parallel",)),
    )(pooled, dy, pool_w, pool_scale, dz)


def conv_fwd(name, z, conv_w, conv_b, d_pool, d_conv):
    t = z.shape[0]
    kw = conv_w.shape[0]
    tc_ch = _tile(d_conv, CHANNEL_TILE)
    v0, g0 = d_pool // tc_ch, (d_pool + d_conv) // tc_ch

    def body(v_ref, g_ref, w_ref, b_ref, c_ref, pad):
        pad[pl.ds(0, HALO), :] = jnp.zeros((HALO, tc_ch), F32)

        def fill(s, tc):
            pad[pl.ds(HALO + s, tc), :] = v_ref[pl.ds(s, tc), :].astype(F32) * jax.nn.sigmoid(g_ref[pl.ds(s, tc), :].astype(F32))

        def chunk(s, tc):
            win = pad[pl.ds(s, tc + HALO), :]
            c_ref[pl.ds(s, tc), :] = _taps(win, w_ref, [HALO - (kw - 1) + k for k in range(kw)], tc) + b_ref[...]

        _chunks(t, fill)
        _chunks(t, chunk)

    return pl.pallas_call(
        body, name=name, grid=(d_conv // tc_ch,),
        in_specs=[pl.BlockSpec((t, tc_ch), lambda j: (0, v0 + j)), pl.BlockSpec((t, tc_ch), lambda j: (0, g0 + j)),
                  pl.BlockSpec((kw, tc_ch), lambda j: (0, j)), pl.BlockSpec((1, tc_ch), lambda j: (0, j))],
        out_specs=pl.BlockSpec((t, tc_ch), lambda j: (0, j)),
        out_shape=jax.ShapeDtypeStruct((t, d_conv), F32),
        scratch_shapes=[pltpu.VMEM((t + HALO, tc_ch), F32)],
        compiler_params=_params(("parallel",)),
    )(z, z, conv_w, conv_b)


def conv_bwd(name, z, dc, conv_w, d_pool, d_conv):
    t = z.shape[0]
    kw = conv_w.shape[0]
    tc_ch = _tile(d_conv, CHANNEL_TILE)
    v0, g0 = d_pool // tc_ch, (d_pool + d_conv) // tc_ch

    def body(v_ref, g_ref, dc_ref, w_ref, dz_ref, dw_ref, db_ref, pad_a, pad_dc, acc_w, acc_b, tiles, sems):
        j = pl.program_id(0)
        dv_ref, dg_ref = tiles.at[0], tiles.at[1]
        writes = [pltpu.make_async_copy(tiles.at[p], dz_ref.at[:, pl.ds((first + j) * tc_ch, tc_ch)], sems.at[p])
                  for p, first in enumerate([v0, g0])]

        def wait_writes():
            for cp in writes:
                cp.wait()

        pad_a[pl.ds(0, HALO), :] = jnp.zeros((HALO, tc_ch), F32)
        pad_dc[pl.ds(t, HALO), :] = jnp.zeros((HALO, tc_ch), F32)
        acc_w[...] = jnp.zeros_like(acc_w)
        acc_b[...] = jnp.zeros_like(acc_b)

        def fill(s, tc):
            pad_a[pl.ds(HALO + s, tc), :] = v_ref[pl.ds(s, tc), :].astype(F32) * jax.nn.sigmoid(g_ref[pl.ds(s, tc), :].astype(F32))
            pad_dc[pl.ds(s, tc), :] = dc_ref[pl.ds(s, tc), :]

        def chunk(s, tc):
            dcv = pad_dc[pl.ds(s, tc), :]
            win_a = pad_a[pl.ds(s, tc + HALO), :]
            for k, rows in _shifted(win_a, [HALO - (kw - 1) + k for k in range(kw)], tc):
                acc_w[pl.ds(8 * k, 8), :] += _fold8(dcv * rows)
            acc_b[...] += _fold8(dcv)
            da = _taps(pad_dc[pl.ds(s, tc + HALO), :], w_ref, list(range(kw)), tc, flip=True)
            vv = v_ref[pl.ds(s, tc), :].astype(F32)
            sg = jax.nn.sigmoid(g_ref[pl.ds(s, tc), :].astype(F32))
            dv_ref[pl.ds(s, tc), :] = (da * sg).astype(BF16)
            dg_ref[pl.ds(s, tc), :] = (da * vv * sg * (1.0 - sg)).astype(BF16)

        _chunks(t, fill)
        pl.when(j > 0)(wait_writes)
        _chunks(t, chunk)
        for cp in writes:
            cp.start()
        pl.when(j == n_tiles - 1)(wait_writes)
        for k in range(kw):
            dw_ref[k:k + 1, :] = jnp.sum(acc_w[pl.ds(8 * k, 8), :], axis=0, keepdims=True)
        db_ref[...] = jnp.sum(acc_b[...], axis=0, keepdims=True)

    n_tiles = d_conv // tc_ch
    return pl.pallas_call(
        body, name=name, grid=(n_tiles,),
        in_specs=[pl.BlockSpec((t, tc_ch), lambda j: (0, v0 + j)), pl.BlockSpec((t, tc_ch), lambda j: (0, g0 + j)),
                  pl.BlockSpec((t, tc_ch), lambda j: (0, j)), pl.BlockSpec((kw, tc_ch), lambda j: (0, j))],
        out_specs=[ANY, pl.BlockSpec((kw, tc_ch), lambda j: (0, j)), pl.BlockSpec((1, tc_ch), lambda j: (0, j))],
        out_shape=[jax.ShapeDtypeStruct((t, d_pool + 2 * d_conv), BF16),
                   jax.ShapeDtypeStruct((kw, d_conv), F32), jax.ShapeDtypeStruct((1, d_conv), F32)],
        scratch_shapes=[pltpu.VMEM((t + HALO, tc_ch), F32), pltpu.VMEM((t + HALO, tc_ch), F32),
                        pltpu.VMEM((8 * kw, tc_ch), F32), pltpu.VMEM((8, tc_ch), F32),
                        pltpu.VMEM((2, t, tc_ch), BF16), pltpu.SemaphoreType.DMA((2,))],
        compiler_params=_params(("arbitrary",)),
    )(z, z, dc, conv_w)


def short_fwd(name, z, conv_w, d_short):
    t = z.shape[0]
    kw = conv_w.shape[0]
    tc_ch = _tile(d_short, CHANNEL_TILE)
    nt = d_short // tc_ch

    def body(b_ref, c_ref, u_ref, w_ref, y_ref, pad):
        pad[pl.ds(0, HALO), :] = jnp.zeros((HALO, tc_ch), F32)

        def fill(s, tc):
            pad[pl.ds(HALO + s, tc), :] = c_ref[pl.ds(s, tc), :].astype(F32) * u_ref[pl.ds(s, tc), :].astype(F32)

        def chunk(s, tc):
            win = pad[pl.ds(s, tc + HALO), :]
            cq = _taps(win, w_ref, [HALO - (kw - 1) + k for k in range(kw)], tc)
            y_ref[pl.ds(s, tc), :] = (b_ref[pl.ds(s, tc), :].astype(F32) * cq).astype(BF16)

        _chunks(t, fill)
        _chunks(t, chunk)

    return pl.pallas_call(
        body, name=name, grid=(nt,),
        in_specs=[pl.BlockSpec((t, tc_ch), lambda j: (0, j)), pl.BlockSpec((t, tc_ch), lambda j: (0, nt + j)),
                  pl.BlockSpec((t, tc_ch), lambda j: (0, 2 * nt + j)), pl.BlockSpec((kw, tc_ch), lambda j: (0, j))],
        out_specs=pl.BlockSpec((t, tc_ch), lambda j: (0, j)),
        out_shape=jax.ShapeDtypeStruct((t, d_short), BF16),
        scratch_shapes=[pltpu.VMEM((t + HALO, tc_ch), F32)],
        compiler_params=_params(("parallel",)),
    )(z, z, z, conv_w)


def short_bwd(name, z, dy, conv_w, d_short):
    t = z.shape[0]
    kw = conv_w.shape[0]
    tc_ch = _tile(d_short, CHANNEL_TILE)
    nt = d_short // tc_ch

    def body(b_ref, c_ref, u_ref, dy_ref, w_ref, dz_ref, dw_ref, pad_q, pad_dcq, acc_w, tiles, sems):
        j = pl.program_id(0)
        db_ref, dcg_ref, du_ref = tiles.at[0], tiles.at[1], tiles.at[2]
        writes = [pltpu.make_async_copy(tiles.at[p], dz_ref.at[:, pl.ds((p * nt + j) * tc_ch, tc_ch)], sems.at[p])
                  for p in range(3)]

        def wait_writes():
            for cp in writes:
                cp.wait()

        pad_q[pl.ds(0, HALO), :] = jnp.zeros((HALO, tc_ch), F32)
        pad_dcq[pl.ds(t, HALO), :] = jnp.zeros((HALO, tc_ch), F32)
        acc_w[...] = jnp.zeros_like(acc_w)

        def fill(s, tc):
            rows = pl.ds(s, tc)
            pad_q[pl.ds(HALO + s, tc), :] = c_ref[rows, :].astype(F32) * u_ref[rows, :].astype(F32)
            pad_dcq[rows, :] = dy_ref[rows, :].astype(F32) * b_ref[rows, :].astype(F32)

        def chunk(s, tc):
            rows = pl.ds(s, tc)
            win_q = pad_q[pl.ds(s, tc + HALO), :]
            dcq = pad_dcq[rows, :]
            cq = None
            for k, shifted in _shifted(win_q, [HALO - (kw - 1) + k for k in range(kw)], tc):
                acc_w[pl.ds(8 * k, 8), :] += _fold8(dcq * shifted)
                term = w_ref[k:k + 1, :] * shifted
                cq = term if cq is None else cq + term
            db_ref[rows, :] = (dy_ref[rows, :].astype(F32) * cq).astype(BF16)
            dq = _taps(pad_dcq[pl.ds(s, tc + HALO), :], w_ref, list(range(kw)), tc, flip=True)
            dcg_ref[rows, :] = (dq * u_ref[rows, :].astype(F32)).astype(BF16)
            du_ref[rows, :] = (dq * c_ref[rows, :].astype(F32)).astype(BF16)

        _chunks(t, fill)
        pl.when(j > 0)(wait_writes)
        _chunks(t, chunk)
        for cp in writes:
            cp.start()
        pl.when(j == nt - 1)(wait_writes)
        for k in range(kw):
            dw_ref[k:k + 1, :] = jnp.sum(acc_w[pl.ds(8 * k, 8), :], axis=0, keepdims=True)

    zspec = [pl.BlockSpec((t, tc_ch), lambda j, o=o: (0, o * nt + j)) for o in range(3)]
    return pl.pallas_call(
        body, name=name, grid=(nt,),
        in_specs=[*zspec, pl.BlockSpec((t, tc_ch), lambda j: (0, j)), pl.BlockSpec((kw, tc_ch), lambda j: (0, j))],
        out_specs=[ANY, pl.BlockSpec((kw, tc_ch), lambda j: (0, j))],
        out_shape=[jax.ShapeDtypeStruct((t, 3 * d_short), BF16), jax.ShapeDtypeStruct((kw, d_short), F32)],
        scratch_shapes=[pltpu.VMEM((t + HALO, tc_ch), F32), pltpu.VMEM((t + HALO, tc_ch), F32),
                        pltpu.VMEM((8 * kw, tc_ch), F32), pltpu.VMEM((3, t, tc_ch), BF16),
                        pltpu.SemaphoreType.DMA((3,))],
        compiler_params=_params(("arbitrary",)),
    )(z, z, z, dy, conv_w)


def _adamw_update(w, m, v, g):
    nm = ADAM_B1 * m + (1.0 - ADAM_B1) * g
    nv = ADAM_B2 * v + (1.0 - ADAM_B2) * (g * g)
    m_hat = nm / (1.0 - ADAM_B1 ** ADAM_STEP)
    v_hat = nv / (1.0 - ADAM_B2 ** ADAM_STEP)
    return -ADAM_LR * (m_hat / (jnp.sqrt(v_hat) + ADAM_EPS) + ADAM_WD * w), nm, nv


def adamw_replicated(name, params, first_moments, second_moments, contributions, layout, scalar_at):
    n = len(params)
    n_slots = contributions.shape[0]

    def total(c_ref, row, lane, rows, lanes):
        acc = c_ref[0, row:row + rows, lane:lane + lanes]
        for slot in range(1, n_slots):
            acc = acc + c_ref[slot, row:row + rows, lane:lane + lanes]
        return acc

    def body(*refs):
        ws, ms, vs, c_ref = refs[:n], refs[n:2 * n], refs[2 * n:3 * n], refs[3 * n]
        outs = refs[3 * n + 1:]
        outs[0][...] = total(c_ref, *scalar_at, 1, 128)
        for i, (row, lane) in enumerate(layout):
            g = total(c_ref, row, lane, *params[i].shape)
            grad_ref, delta_ref, nm_ref, nv_ref = outs[1 + 4 * i:5 + 4 * i]
            grad_ref[...] = g
            delta_ref[...], nm_ref[...], nv_ref[...] = _adamw_update(ws[i][...], ms[i][...], vs[i][...], g)

    out_shape = [jax.ShapeDtypeStruct((1, 128), F32)]
    for p in params:
        out_shape += [jax.ShapeDtypeStruct(p.shape, F32)] * 4
    return pl.pallas_call(body, name=name, out_shape=out_shape)(*params, *first_moments, *second_moments, contributions)


def adamw(name, w, m, v, contributions):
    r, c = w.shape
    nc = len(contributions)
    n_slots = contributions[0].shape[0]
    tr = 256 if c <= 1024 else 128
    if any(a.shape[1] % tr for a in contributions):
        assert nc == 1
        tr = r
    tiles = [a.shape[1] // tr for a in contributions]
    first = [sum(tiles[:j]) for j in range(nc)]

    def body(w_ref, m_ref, v_ref, *rest):
        g_refs, (grad_ref, delta_ref, nm_ref, nv_ref) = rest[:nc], rest[nc:]
        i = pl.program_id(0)
        g = None
        for j, g_ref in enumerate(g_refs):
            s = g_ref[0].astype(F32)
            for slot in range(1, n_slots):
                s = s + g_ref[slot].astype(F32)
            g = s if g is None else jnp.where(i >= first[j], s, g)
        grad_ref[...] = g
        delta_ref[...], nm_ref[...], nv_ref[...] = _adamw_update(w_ref[...], m_ref[...], v_ref[...], g)

    blk = pl.BlockSpec((tr, c), lambda i: (i, 0))
    g_specs = [pl.BlockSpec((n_slots, tr, c), lambda i, j=j: (0, jnp.clip(i - first[j], 0, tiles[j] - 1), 0))
               for j in range(nc)]
    return pl.pallas_call(
        body, name=name, grid=(r // tr,),
        in_specs=[blk, blk, blk, *g_specs],
        out_specs=[blk] * 4, out_shape=[jax.ShapeDtypeStruct((r, c), F32)] * 4,
        compiler_params=_params(("parallel",)),
    )(w, m, v, *contributions)


def _pad_rows(a, rows):
    return jnp.pad(a, ((0, rows - a.shape[0]), (0, 0)))


def kernel(x, mix_pre_g, mix_post_g, ffn_pre_g, ffn_post_g, ab_w_in, pool_w, pool_scale, conv_w, conv_b, conv_ln_g, conv_ln_b, ab_w_out, sc_w_in, sc_conv_w, sc_w_out, ffn_w1, ffn_w2, loss_target, m_mix_pre_g, m_mix_post_g, m_ffn_pre_g, m_ffn_post_g, m_ab_w_in, m_pool_w, m_pool_scale, m_conv_w, m_conv_b, m_conv_ln_g, m_conv_ln_b, m_ab_w_out, m_sc_w_in, m_sc_conv_w, m_sc_w_out, m_ffn_w1, m_ffn_w2, v_mix_pre_g, v_mix_post_g, v_ffn_pre_g, v_ffn_post_g, v_ab_w_in, v_pool_w, v_pool_scale, v_conv_w, v_conv_b, v_conv_ln_g, v_conv_ln_b, v_ab_w_out, v_sc_w_in, v_sc_conv_w, v_sc_w_out, v_ffn_w1, v_ffn_w2):
    t, d = x.shape[1], x.shape[2]
    d_pool = pool_scale.shape[1]
    d_conv = conv_b.shape[1]
    d_short = d
    ng, pg = pool_w.shape[1], pool_w.shape[3]
    kw, ks = conv_w.shape[1], sc_conv_w.shape[1]
    nb_ab, nb_sc, nb_ff = ab_w_in.shape[2], sc_w_in.shape[2], ffn_w1.shape[2]

    xs = x[0]
    target = loss_target[0]

    lanes = min(128, d_conv // N_DEV)
    small_rows = [kw * (d_conv // N_DEV) // lanes, ks * (d_short // N_DEV) // lanes, ng * (pg // N_DEV) * pg // lanes]
    small_total = -(-sum(small_rows) // 8) * 8
    r0, r1, r2 = small_rows[0], small_rows[0] + small_rows[1], sum(small_rows)

    def pack_small(a_conv, a_sconv, a_pool):
        parts = [a_conv[0].reshape(-1, lanes), a_sconv[0].reshape(-1, lanes), a_pool[0].reshape(-1, lanes)]
        return _pad_rows(jnp.concatenate(parts, axis=0), small_total)

    shards = {
        "ab_in": (ab_w_in, 0, BF16), "small": (pack_small(conv_w, sc_conv_w, pool_w)[None], 0, F32),
        "ab_out": (ab_w_out, 0, BF16), "ff1_0": (ffn_w1, 0, BF16), "ff2_0": (ffn_w2, 0, BF16),
        "sc_in": (sc_w_in, 0, BF16), "sc_out": (sc_w_out, 0, BF16),
        "ff1_1": (ffn_w1, 1, BF16), "ff2_1": (ffn_w2, 1, BF16)}
    direct = ["ab_in", "small", "ab_out"]
    zones = {nm: place_shard("place_" + nm, *shards[nm]) for nm in direct}
    started, token = copies_start("gather_start", [[zones[nm]] for nm in direct], _first_hop, 4)
    started = dict(zip(direct, started))
    zones["ff1_0"] = place_shard("place_ff1_0", *shards["ff1_0"], deps=[token])
    (head,), token = copies_start("ring_start_ff1_0", [[zones["ff1_0"]]], _ring_hop1, 3, deps=[token])
    ring = {"ff1_0": head}

    ties = [0]

    def after(v, *deps):
        ties[0] += 1
        return tie(f"tie_{ties[0]}", v, *deps)

    def fetch_begin(nm, dep):
        (zone,) = copies_wait("gather_wait_" + nm, started[nm], _first_hop, dep)
        (hop,), tok = copies_start("forward_start_" + nm, [[zone]], _second_hop, 3)
        return hop, tok

    def fetch_end(nm, hop, dep):
        return copies_wait("forward_wait_" + nm, hop, _second_hop, dep)[0]

    def ring_step(tag, dep, *starts, place=()):
        place = list(place) + [nm for n, nm in starts if n == 1 and nm not in zones and nm not in place]
        before_waits = dep
        for nm in place:
            before_waits = zones[nm] = place_shard("place_" + nm, *shards[nm], deps=[before_waits])
        names, groups, hops, counts = [], [], [], []
        for n, nm in starts:
            if n == 1:
                groups.append([zones[nm]])
            elif n == 2:
                groups.append(copies_wait("ring1_wait_" + nm, ring[nm], _ring_hop1, before_waits))
            else:
                groups.append(copies_wait("ring2_wait_" + nm, ring[nm], _ring_hop2, before_waits))
            hop, n_copies = RING_HOPS[n - 1]
            names, hops, counts = names + [nm], hops + [hop], counts + [n_copies]
        begun, tok = copies_start("ring_start_" + tag, groups, hops, counts, deps=[dep])
        ring.update(zip(names, begun))
        return tok

    def ring_done(nm, dep):
        return copies_wait("ring3_wait_" + nm, ring[nm], _ring_hop3, dep)[0]

    relu = lambda r: jnp.maximum(r, 0.0)
    square = lambda a: a * a
    relu2_bwd = lambda r, a: r * (2.0 * a.astype(F32))

    def row(vec, l):
        return vec[l:l + 1]

    hop_small, _ = fetch_begin("small", token)
    hop_ab_in, tok = fetch_begin("ab_in", token)
    w_small = fetch_end("small", hop_small, tok)
    w_ab_in = fetch_end("ab_in", hop_ab_in, tok)
    w_conv = w_small[:, :r0].reshape(N_DEV, kw, -1).transpose(1, 0, 2).reshape(kw, d_conv)
    w_sconv = w_small[:, r0:r1].reshape(N_DEV, ks, -1).transpose(1, 0, 2).reshape(ks, d_short)
    w_pool = w_small[:, r1:r2].reshape(N_DEV, ng, -1, pg).transpose(1, 0, 2, 3).reshape(ng, pg, pg).astype(BF16)
    h0 = norm_pre("norm_pre", xs, after(row(mix_pre_g, 0), token))
    z0 = mm_nn_blocked("ab_in", h0, w_ab_in, out_dtype=BF16)
    hop, tok = fetch_begin("ab_out", z0)
    z0 = after(z0, tok)
    pooled, y0 = pool_fwd("pool_fwd", z0, w_pool, pool_scale, d_pool, d_pool + d_conv)
    cv = conv_fwd("conv_fwd", z0, w_conv, conv_b, d_pool, d_conv)
    y0 = ln_silu("ln_silu", cv, conv_ln_g, conv_ln_b, y0, d_pool // d_conv)
    w_ab_out = fetch_end("ab_out", hop, y0)

    def ffn_up(name, h, w, *starts):
        a = mm_nn_blocked(name, h, w, out_dtype=BF16, epilogue=relu, blocks=(0, UP_FIRST_BLOCKS))
        tok = ring_step(name, a, *starts)
        return mm_nn_blocked(name + "_rest", h, w, out_dtype=BF16, epilogue=relu,
                             blocks=(UP_FIRST_BLOCKS, N_DEV - UP_FIRST_BLOCKS), into=after(a, tok))

    tok = ring_step("a", w_ab_out, (2, "ff1_0"), (1, "ff2_0"))
    y0 = after(y0, tok)
    m0 = mm_nn("ab_out", y0, w_ab_out.reshape(d_pool + d_conv, d), out_dtype=F32)
    x1, h1 = post_pre("post_pre_0", xs, m0, row(mix_post_g, 0), row(ffn_pre_g, 0))
    tok = ring_step("b", h1, (3, "ff1_0"))
    w_ff1_0 = ring_done("ff1_0", tok)
    a0 = ffn_up("ffn0_up", h1, w_ff1_0, (2, "ff2_0"), (1, "sc_in"), (1, "sc_out"))
    tok = ring_step("c", a0, (3, "ff2_0"), place=["ff1_1"])
    w_ff2_0 = ring_done("ff2_0", tok).reshape(-1, d)
    f0 = mm_nn("ffn0_down", a0, w_ff2_0, out_dtype=F32, tk=2048, lhs_fn=square)
    tok = ring_step("d", f0, (2, "sc_in"), (2, "sc_out"), (1, "ff1_1"))
    f0 = after(f0, tok)
    x2, h2 = post_pre("post_pre_1", x1, f0, row(ffn_post_g, 0), row(mix_pre_g, 1))
    tok = ring_step("e", h2, (3, "sc_in"), place=["ff2_1"])
    w_sc_in = ring_done("sc_in", tok)
    z1 = mm_nn_blocked("sc_in", h2, w_sc_in, out_dtype=BF16)
    y1 = short_fwd("short_fwd", z1, w_sconv, d_short)
    tok = ring_step("f", y1, (3, "sc_out"), (2, "ff1_1"), (1, "ff2_1"))
    w_sc_out = ring_done("sc_out", tok).reshape(d_short, d)
    m1 = mm_nn("sc_out", y1, w_sc_out, out_dtype=F32)
    x3, h3 = post_pre("post_pre_2", x2, m1, row(mix_post_g, 1), row(ffn_pre_g, 1))
    tok = ring_step("g", h3, (3, "ff1_1"))
    w_ff1_1 = ring_done("ff1_1", tok)
    a1 = ffn_up("ffn1_up", h3, w_ff1_1, (2, "ff2_1"))
    tok = ring_step("h", a1, (3, "ff2_1"))
    w_ff2_1 = ring_done("ff2_1", tok).reshape(-1, d)
    f1 = mm_nn("ffn1_down", a1, w_ff2_1, out_dtype=F32, tk=2048, lhs_fn=square)
    dx4, df1, loss_part, dg_ffn_post1 = post_loss("post_loss", x3, f1, row(ffn_post_g, 1), target)

    red = {}

    def reduce_step(dep, begin=None, middle=None):
        tags, groups, hops, counts = [], [], [], []
        if begin is not None:
            tag, g = begin
            tags, groups = tags + [tag], groups + [[g, lax.empty((N_CHIP,) + g.shape[1:], g.dtype)]]
            hops, counts = hops + [_pair_hop], counts + [N_CHIP]
        if middle is not None:
            g, from_sibling = copies_wait("pair_wait_" + middle, red[middle], _pair_hop, dep)
            tags, groups = tags + [middle], groups + [list(pair_add("pair_add_" + middle, g, from_sibling))]
            hops, counts = hops + [_chip_hop], counts + [3]
        begun, tok = copies_start("reduce_start_" + "_".join(tags), groups, hops, counts, deps=[dep])
        red.update(zip(tags, begun))
        return tok

    def reduce_end(tag, dep):
        return copies_wait("chips_wait_" + tag, red[tag], _chip_hop, dep)[1]

    dpre, dw = mm_bwd_pair("ffn1_da_dw2", df1, w_ff2_1, a1, out_dtype=BF16, act_fn=square, epilogue=relu2_bwd)
    dpre = after(dpre, reduce_step(dpre, begin=("ff2_1", dw.reshape(N_DEV, -1, d))))
    dh3, dw = mm_bwd_pair_blocked("ffn1_dh_dw1", dpre, w_ff1_1, h3, out_dtype=BF16)
    dx3, dm1, dg_ffn_pre1, dg_mix_post1 = bwd_pre_post("bwd_3", dx4, x3, row(ffn_pre_g, 1), dh3, m1, row(mix_post_g, 1))
    dm1 = after(dm1, reduce_step(dm1, begin=("ff1_1", dw), middle="ff2_1"))

    dy1, dw = mm_bwd_pair("sc_dy_dwout", dm1, w_sc_out, y1, out_dtype=BF16)
    dy1 = after(dy1, reduce_step(dy1, begin=("sc_out", dw.reshape(N_DEV, -1, d)), middle="ff1_1"))
    dz1, dw_sconv = short_bwd("short_bwd", z1, dy1, w_sconv, d_short)
    dh2, dw = mm_bwd_pair_blocked("sc_dh_dwin", dz1, w_sc_in, h2, out_dtype=BF16)
    dx2, df0, dg_mix_pre1, dg_ffn_post0 = bwd_pre_post("bwd_2", dx3, x2, row(mix_pre_g, 1), dh2, f0, row(ffn_post_g, 0))
    df0 = after(df0, reduce_step(df0, begin=("sc_in", dw), middle="sc_out"))

    dpre, dw = mm_bwd_pair("ffn0_da_dw2", df0, w_ff2_0, a0, out_dtype=BF16, act_fn=square, epilogue=relu2_bwd)
    dpre = after(dpre, reduce_step(dpre, begin=("ff2_0", dw.reshape(N_DEV, -1, d)), middle="sc_in"))
    dh1, dw = mm_bwd_pair_blocked("ffn0_dh_dw1", dpre, w_ff1_0, h1, out_dtype=BF16)
    dx1, dm0, dg_ffn_pre0, dg_mix_post0 = bwd_pre_post("bwd_1", dx2, x1, row(ffn_pre_g, 0), dh1, m0, row(mix_post_g, 0))
    dm0 = after(dm0, reduce_step(dm0, begin=("ff1_0", dw), middle="ff2_0"))

    dy0, dw = mm_bwd_pair("ab_dy_dwout", dm0, w_ab_out.reshape(d_pool + d_conv, d), y0, out_dtype=BF16)
    dy0 = after(dy0, reduce_step(dy0, begin=("ab_out", dw.reshape(N_DEV, -1, d)), middle="ff1_0"))
    dcv, dg_ln_g, dg_ln_b = ln_silu_bwd("ln_silu_bwd", cv, conv_ln_g, conv_ln_b, dy0, d_pool // d_conv)
    dz0, dw_conv, dg_conv_b = conv_bwd("conv_bwd", z0, dcv, w_conv, d_pool, d_conv)
    dz0, dw_pool, dg_pool_scale = pool_bwd("pool_bwd", pooled, dy0, w_pool, pool_scale, dz0)
    small_parts = [
        dw_conv.reshape(kw, N_DEV, -1).transpose(1, 0, 2).reshape(N_DEV, -1, lanes),
        dw_sconv.reshape(ks, N_DEV, -1).transpose(1, 0, 2).reshape(N_DEV, -1, lanes),
        dw_pool.reshape(ng, N_DEV, pg // N_DEV, pg).transpose(1, 0, 2, 3).reshape(N_DEV, -1, lanes),
    ]
    small = jnp.pad(jnp.concatenate(small_parts, axis=1), ((0, 0), (0, small_total - r2), (0, 0)))
    dz0 = after(dz0, reduce_step(dz0, begin=("small", small), middle="ab_out"))
    dh0, dw = mm_bwd_pair_blocked("ab_dh_dwin", dz0, w_ab_in, h0, out_dtype=BF16)
    dh0 = after(dh0, reduce_step(dh0, begin=("ab_in", dw), middle="small"))
    grad_x, dg_mix_pre0 = bwd_pre_final("bwd_0", dx1, xs, row(mix_pre_g, 0), dh0)
    tok = reduce_step(grad_x, middle="ab_in")

    gains = [dg_mix_pre0, dg_mix_pre1, dg_mix_post0, dg_mix_post1, dg_ffn_pre0, dg_ffn_pre1, dg_ffn_post0, dg_ffn_post1]
    pieces = [(g, i, 0) for i, g in enumerate(gains)]
    rep_layout = [(0, 0), (2, 0), (4, 0), (6, 0)]
    offset = 0
    for g in (dg_pool_scale, dg_conv_b, dg_ln_g, dg_ln_b):
        at = (len(gains) + offset // d, offset % d)
        pieces.append((g, *at))
        rep_layout.append(at)
        offset += g.shape[1]
    loss_at = (len(gains) + -(-offset // d), 0)
    rep_zone = place_sheet("place_rep", pieces, loss_part, loss_at, 16, d)
    (rep_hop,), tok = copies_start("rep_start", [[rep_zone]], _first_hop, 4,
                                   deps=[tok])

    def upd(name, w, m, v, contribs):
        shape = w.shape
        flat2 = lambda a: a.reshape(-1, shape[-1])
        outs = adamw(name, flat2(w), flat2(m), flat2(v), contribs)
        return [o.reshape(shape) for o in outs]

    g_ff2 = [reduce_end("ff2_0", tok), reduce_end("ff2_1", tok)]
    o_ff2 = upd("adam_ffn_w2", ffn_w2, m_ffn_w2, v_ffn_w2, g_ff2)
    (rep_zone,) = copies_wait("rep_wait", rep_hop, _first_hop, o_ff2[0])
    (rep_hop,), _ = copies_start("rep_forward_start", [[rep_zone]], _second_hop, 3)
    g_ff1 = [reduce_end("ff1_0", o_ff2[0]), reduce_end("ff1_1", o_ff2[0])]
    o_ff1 = upd("adam_ffn_w1", ffn_w1, m_ffn_w1, v_ffn_w1, g_ff1)
    (rep_all,) = copies_wait("rep_forward_wait", rep_hop, _second_hop, o_ff1[0])
    loss_sum, *o_rep = adamw_replicated(
        "adam_replicated",
        [mix_pre_g, mix_post_g, ffn_pre_g, ffn_post_g, pool_scale, conv_b, conv_ln_g, conv_ln_b],
        [m_mix_pre_g, m_mix_post_g, m_ffn_pre_g, m_ffn_post_g, m_pool_scale, m_conv_b, m_conv_ln_g, m_conv_ln_b],
        [v_mix_pre_g, v_mix_post_g, v_ffn_pre_g, v_ffn_post_g, v_pool_scale, v_conv_b, v_conv_ln_g, v_conv_ln_b],
        rep_all, rep_layout, loss_at)
    loss = loss_sum[0, 0] * (0.5 / d)
    o_sc_out = upd("adam_sc_out", sc_w_out, m_sc_w_out, v_sc_w_out, [reduce_end("sc_out", o_ff1[0])])
    o_sc_in = upd("adam_sc_in", sc_w_in, m_sc_w_in, v_sc_w_in, [reduce_end("sc_in", o_sc_out[0])])
    o_ab_out = upd("adam_ab_out", ab_w_out, m_ab_w_out, v_ab_w_out, [reduce_end("ab_out", o_sc_in[0])])
    o_small = adamw("adam_small", pack_small(conv_w, sc_conv_w, pool_w), pack_small(m_conv_w, m_sc_conv_w, m_pool_w),
                    pack_small(v_conv_w, v_sc_conv_w, v_pool_w), [reduce_end("small", o_ab_out[0])])
    o_ab_in = upd("adam_ab_in", ab_w_in, m_ab_w_in, v_ab_w_in, [reduce_end("ab_in", after(o_small[0], o_rep[0]))])

    def unpack_small(o):
        return o[:r0].reshape(conv_w.shape), o[r0:r1].reshape(sc_conv_w.shape), o[r1:r2].reshape(pool_w.shape)

    results = []
    for kind in range(4):
        g_mix_pre, g_mix_post, g_ffn_pre, g_ffn_post, g_scale, g_conv_b, g_ln_g, g_ln_b = o_rep[kind::4]
        s_conv, s_sconv, s_pool = unpack_small(o_small[kind])
        results.append([
            g_mix_pre, g_mix_post, g_ffn_pre, g_ffn_post,
            o_ab_in[kind], s_pool, g_scale, s_conv, g_conv_b, g_ln_g, g_ln_b,
            o_ab_out[kind], o_sc_in[kind], s_sconv, o_sc_out[kind], o_ff1[kind], o_ff2[kind]])

    return (loss, grad_x[None], *results[0], *results[1], *results[2], *results[3])
```

```python
import jax
import jax.numpy as jnp
from jax import lax
from jax.experimental import pallas as pl
from jax.experimental.pallas import tpu as pltpu

F32 = jnp.float32
BF16 = jnp.bfloat16
MESH = pl.DeviceIdType.MESH
ANY = pl.BlockSpec(memory_space=pl.ANY)

NORM_EPS = 1e-6
POOL_WINDOWS = (2, 4, 8, 16)
ADAM_LR = 0.001
ADAM_B1 = 0.9
ADAM_B2 = 0.999
ADAM_EPS = 1e-08
ADAM_WD = 0.01
ADAM_STEP = 10

N_DEV = 8
VMEM_LIMIT = 56 * 1024 * 1024
PAIR_ADD_BLOCK = 1 << 20
MATMUL_ROWS = 2048
UP_FIRST_BLOCKS = 6
ROW_TILE = 256
CHANNEL_TILE = 256
TIME_CHUNK = 64
HALO = 32

NN = (((1,), (0,)), ((), ()))
NT = (((1,), (1,)), ((), ()))
TN = (((0,), (0,)), ((), ()))


def _params(sem):
    return pltpu.CompilerParams(dimension_semantics=sem, vmem_limit_bytes=VMEM_LIMIT)


def _place():
    x, y, c = lax.axis_index("x"), lax.axis_index("y"), lax.axis_index("c")
    return x, y, c


def _slot(px, py, pc):
    return 4 * px + 2 * py + pc


HBM = pl.BlockSpec(memory_space=pltpu.HBM)
SEM = pl.BlockSpec(memory_space=pltpu.SEMAPHORE)
EFFECT = pltpu.SideEffectType.DATAFLOW_SIDE_EFFECTING
TOKEN = jax.ShapeDtypeStruct((8, 128), F32)


def _in_hbm(a):
    return pltpu.with_memory_space_constraint(a, pltpu.HBM)


CHIPS = [(0, 0), (0, 1), (1, 0), (1, 1)]
N_CHIP = len(CHIPS)


def _chip(px, py):
    return 2 * px + py


def _first_hop(bufs, sends, recvs, waiting):
    (land,) = bufs
    x, y, c = _place()
    me = _slot(x, y, c)
    peers = [(x, y, 1 - c), (1 - x, y, c), (x, 1 - y, c), (1 - x, 1 - y, c)]
    return [pltpu.make_async_remote_copy(
        src_ref=land.at[me], dst_ref=land.at[_slot(*p) if waiting else me],
        send_sem=sends.at[k], recv_sem=recvs.at[k], device_id=p, device_id_type=MESH) for k, p in enumerate(peers)]


def _second_hop(bufs, sends, recvs, waiting):
    (land,) = bufs
    x, y, c = _place()
    return [pltpu.make_async_remote_copy(
        src_ref=land.at[_slot(px, py, c)], dst_ref=land.at[_slot(px, py, 1 - c if waiting else c)],
        send_sem=sends.at[k], recv_sem=recvs.at[k], device_id=(x, y, 1 - c), device_id_type=MESH)
        for k, (px, py) in enumerate([(1 - x, y), (x, 1 - y), (1 - x, 1 - y)])]


def _ring_hop1(bufs, sends, recvs, waiting):
    (land,) = bufs
    x, y, c = _place()
    me = _slot(x, y, c)
    peers = [(1 - x, y, c), (x, 1 - y, c), (x, y, 1 - c)]
    return [pltpu.make_async_remote_copy(
        src_ref=land.at[me], dst_ref=land.at[_slot(*p) if waiting else me],
        send_sem=sends.at[k], recv_sem=recvs.at[k], device_id=p, device_id_type=MESH) for k, p in enumerate(peers)]


def _ring_hop2(bufs, sends, recvs, waiting):
    (land,) = bufs
    x, y, c = _place()
    half = land.shape[1] // 2
    first, second = pl.ds(0, half), pl.ds(half, half)
    nx, ny, diag = _slot(1 - x, y, c), _slot(x, 1 - y, c), _slot(1 - x, 1 - y, c)
    plan = [
        (land.at[ny, first], land.at[diag, first], (1 - x, y, c)),
        (land.at[nx, second], land.at[diag, second], (x, 1 - y, c)),
        (land.at[nx], land.at[_slot(1 - x, y, 1 - c)], (x, y, 1 - c)),
        (land.at[ny], land.at[_slot(x, 1 - y, 1 - c)], (x, y, 1 - c))]
    return [pltpu.make_async_remote_copy(
        src_ref=src, dst_ref=mine if waiting else src, send_sem=sends.at[k], recv_sem=recvs.at[k],
        device_id=to, device_id_type=MESH) for k, (src, mine, to) in enumerate(plan)]


def _ring_hop3(bufs, sends, recvs, waiting):
    (land,) = bufs
    x, y, c = _place()
    return [pltpu.make_async_remote_copy(
        src_ref=land.at[_slot(1 - x, 1 - y, c)], dst_ref=land.at[_slot(1 - x, 1 - y, 1 - c if waiting else c)],
        send_sem=sends.at[0], recv_sem=recvs.at[0], device_id=(x, y, 1 - c), device_id_type=MESH)]


RING_HOPS = [(_ring_hop1, 3), (_ring_hop2, 4), (_ring_hop3, 1)]

def _pair_hop(bufs, sends, recvs, waiting):
    g, land = bufs
    x, y, c = _place()
    return [pltpu.make_async_remote_copy(
        src_ref=g.at[_slot(qx, qy, 1 - c)], dst_ref=land.at[q],
        send_sem=sends.at[q], recv_sem=recvs.at[q], device_id=(x, y, 1 - c), device_id_type=MESH)
        for q, (qx, qy) in enumerate(CHIPS)]


def _chip_hop(bufs, sends, recvs, waiting):
    p, land = bufs
    x, y, c = _place()
    return [pltpu.make_async_remote_copy(
        src_ref=p.at[_chip(px, py)], dst_ref=land.at[_chip(px, py) if waiting else _chip(x, y)],
        send_sem=sends.at[k], recv_sem=recvs.at[k], device_id=(px, py, c), device_id_type=MESH)
        for k, (px, py) in enumerate([(1 - x, y), (x, 1 - y), (1 - x, 1 - y)])]


def copies_start(name, groups, hop, n_copies, deps=()):
    flat = [b for grp in groups for b in grp]
    nb, ng = len(flat), len(groups)
    deps = list(deps)
    hops = list(hop) if isinstance(hop, (list, tuple)) else [hop] * ng
    counts = list(n_copies) if isinstance(n_copies, (list, tuple)) else [n_copies] * ng

    def body(*refs):
        ins, token = refs[:nb], refs[-1]
        sems = refs[nb + len(deps):nb + len(deps) + 2 * ng]
        i = 0
        for gi, grp in enumerate(groups):
            for cp in hops[gi](ins[i:i + len(grp)], sems[2 * gi], sems[2 * gi + 1], False):
                cp.start()
            i += len(grp)
        token[...] = jnp.zeros_like(token)

    outs = pl.pallas_call(
        body, name=name,
        out_shape=([pltpu.SemaphoreType.DMA((n,)) for n in counts for _ in range(2)]
                   + [pltpu.HBM(b.shape, b.dtype) for b in flat] + [TOKEN]),
        in_specs=[HBM] * nb + [ANY] * len(deps),
        out_specs=[SEM] * (2 * ng) + [HBM] * nb + [pl.BlockSpec(memory_space=pltpu.VMEM)],
        input_output_aliases={i: 2 * ng + i for i in range(nb)},
        compiler_params=pltpu.CompilerParams(has_side_effects=EFFECT),
    )(*[_in_hbm(b) for b in flat], *deps)
    started, i = [], 0
    for gi, grp in enumerate(groups):
        started.append((outs[2 * gi], outs[2 * gi + 1], list(outs[2 * ng + i:2 * ng + i + len(grp)])))
        i += len(grp)
    return started, outs[-1]


def copies_wait(name, started, hop, after):
    sends, recvs, bufs = started
    nb = len(bufs)

    def body(*refs):
        for cp in hop(refs[:nb], refs[nb], refs[nb + 1], True):
            cp.wait_send()
            cp.wait_recv()

    outs = pl.pallas_call(
        body, name=name,
        out_shape=[pltpu.HBM(b.shape, b.dtype) for b in bufs],
        in_specs=[HBM] * nb + [SEM, SEM, ANY], out_specs=[HBM] * nb,
        input_output_aliases={i: i for i in range(nb)},
        compiler_params=pltpu.CompilerParams(has_side_effects=EFFECT),
    )(*bufs, sends, recvs, after)
    return list(outs)


def place_shard(name, w, layer, dtype, deps=()):
    _, r, c = w.shape
    tr = _tile(r, 1024)
    x, y, core = _place()
    me = _slot(x, y, core).astype(jnp.int32).reshape(1)

    def body(me_ref, w_ref, *rest):
        rest[-1][...] = w_ref[...].astype(dtype)

    return pl.pallas_call(
        body, name=name,
        grid_spec=pltpu.PrefetchScalarGridSpec(
            num_scalar_prefetch=1, grid=(r // tr,),
            in_specs=[pl.BlockSpec((None, tr, c), lambda i, me_ref: (layer, i, 0))] + [ANY] * len(deps),
            out_specs=pl.BlockSpec((None, tr, c), lambda i, me_ref: (me_ref[0], i, 0))),
        out_shape=jax.ShapeDtypeStruct((N_DEV, r, c), dtype),
        compiler_params=_params(("parallel",)),
    )(me, w, *deps)


def place_sheet(name, pieces, total_of, total_at, rows, width):
    x, y, core = _place()
    me = _slot(x, y, core).astype(jnp.int32).reshape(1)

    def body(me_ref, *refs):
        o_ref = refs[-1]
        o_ref[...] = jnp.zeros_like(o_ref)
        for ref, (_, row, lane) in zip(refs, pieces):
            o_ref[row:row + 1, lane:lane + ref.shape[1]] = jnp.sum(ref[...], axis=0, keepdims=True)
        total = jnp.sum(jnp.sum(refs[len(pieces)][...], axis=0, keepdims=True), axis=1, keepdims=True)
        o_ref[total_at[0]:total_at[0] + 1, total_at[1]:total_at[1] + 128] = jnp.broadcast_to(total, (1, 128))

    arrays = [a for a, _, _ in pieces] + [total_of]
    return pl.pallas_call(
        body, name=name,
        grid_spec=pltpu.PrefetchScalarGridSpec(
            num_scalar_prefetch=1, grid=(1,),
            in_specs=[pl.BlockSpec(a.shape, lambda i, me_ref: (0, 0)) for a in arrays],
            out_specs=pl.BlockSpec((None, rows, width), lambda i, me_ref: (me_ref[0], 0, 0))),
        out_shape=jax.ShapeDtypeStruct((N_DEV, rows, width), F32),
    )(me, *arrays)


def tie(name, x, *deps):
    def body(*refs):
        del refs

    return pl.pallas_call(
        body, name=name, out_shape=jax.ShapeDtypeStruct(x.shape, x.dtype),
        in_specs=[ANY] * (1 + len(deps)), out_specs=ANY, input_output_aliases={0: 0},
    )(x, *deps)


def pair_add(name, g, from_sibling):
    _, r, c_dim = g.shape
    tr = r
    while tr * c_dim > PAIR_ADD_BLOCK and tr % 16 == 0:
        tr //= 2
    x, y, core = _place()
    where = jnp.stack([core, _chip(x, y)]).astype(jnp.int32)

    def body(where_ref, g_ref, s_ref, o_ref, zone_ref):
        total = (g_ref[...].astype(F32) + s_ref[...].astype(F32)).astype(o_ref.dtype)
        o_ref[...] = total

        @pl.when(pl.program_id(1) == where_ref[1])
        def _():
            zone_ref[...] = total

    blk = pl.BlockSpec((None, tr, c_dim), lambda i, q, where_ref: (q, i, 0))
    return pl.pallas_call(
        body, name=name,
        grid_spec=pltpu.PrefetchScalarGridSpec(
            num_scalar_prefetch=1, grid=(r // tr, N_CHIP),
            in_specs=[pl.BlockSpec((None, None, tr, c_dim), lambda i, q, where_ref: (q, where_ref[0], i, 0)), blk],
            out_specs=[blk, pl.BlockSpec((None, tr, c_dim), lambda i, q, where_ref: (where_ref[1], i, 0))]),
        out_shape=[jax.ShapeDtypeStruct((N_CHIP, r, c_dim), g.dtype)] * 2,
        compiler_params=_params(("parallel", "arbitrary")),
    )(where, g.reshape(N_CHIP, 2, r, c_dim), from_sibling)


def _matmul(name, lhs, rhs, *, out_shape, out_dtype, grid, lhs_spec, rhs_spec, out_spec, acc_shape,
            lhs_fn=None, epilogue=None, into=None):
    nk = grid[2]
    extra = [] if into is None else [into]

    def body(lhs_ref, rhs_ref, *rest):
        out_ref, scratch = rest[len(extra)], rest[len(extra) + 1:]

        def product():
            a = lhs_ref[...]
            if lhs_fn is not None:
                a = lhs_fn(a)
            return lax.dot_general(a, rhs_ref[...], NN, preferred_element_type=F32)

        def finish(r):
            if epilogue is not None:
                r = epilogue(r)
            out_ref[...] = r.astype(out_dtype)

        if nk == 1:
            finish(product())
        else:
            (acc_ref,) = scratch
            k = pl.program_id(2)

            @pl.when(k == 0)
            def _():
                acc_ref[...] = product()

            @pl.when(jnp.logical_and(k > 0, k < nk - 1))
            def _():
                acc_ref[...] += product()

            @pl.when(k == nk - 1)
            def _():
                finish(acc_ref[...] + product())

    return pl.pallas_call(
        body, name=name, grid=grid,
        out_shape=jax.ShapeDtypeStruct(out_shape, out_dtype),
        in_specs=[lhs_spec, rhs_spec] + [ANY] * len(extra), out_specs=out_spec,
        input_output_aliases={2: 0} if extra else {},
        scratch_shapes=[pltpu.VMEM(acc_shape, F32)] if nk > 1 else [],
        compiler_params=_params(("parallel", "parallel", "arbitrary")),
    )(lhs, rhs, *extra)


def _tile(n, want):
    return want if n % want == 0 else n


def mm_nn(name, x, w, *, out_dtype, tn=512, tk=None, lhs_fn=None, epilogue=None):
    t, kdim = x.shape
    n = w.shape[1]
    tm, tn = _tile(t, MATMUL_ROWS), _tile(n, tn)
    tk = kdim if tk is None else _tile(kdim, tk)
    return _matmul(
        name, x, w, out_shape=(t, n), out_dtype=out_dtype, grid=(t // tm, n // tn, kdim // tk),
        lhs_spec=pl.BlockSpec((tm, tk), lambda i, j, k: (i, k)),
        rhs_spec=pl.BlockSpec((tk, tn), lambda i, j, k: (k, j)),
        out_spec=pl.BlockSpec((tm, tn), lambda i, j, k: (i, j)),
        acc_shape=(tm, tn), lhs_fn=lhs_fn, epilogue=epilogue)


def mm_nn_blocked(name, x, w, *, out_dtype, epilogue=None, blocks=(0, N_DEV), into=None):
    t, kdim = x.shape
    nb = w.shape[2]
    tm = _tile(t, MATMUL_ROWS)
    tn = nb // 2 if nb >= 1024 else nb
    sub = nb // tn
    first, count = blocks
    return _matmul(
        name, x, w, out_shape=(t, N_DEV * nb), out_dtype=out_dtype, grid=(t // tm, count * sub, 1),
        lhs_spec=pl.BlockSpec((tm, kdim), lambda i, j, k: (i, k)),
        rhs_spec=pl.BlockSpec((None, kdim, tn), lambda i, j, k: (first + j // sub, k, j % sub)),
        out_spec=pl.BlockSpec((tm, tn), lambda i, j, k: (i, first * sub + j)),
        acc_shape=(tm, tn), epilogue=epilogue, into=into)


def mm_bwd_pair(name, dy, w, act, *, out_dtype, tile=512, act_fn=None, epilogue=None):
    t, n = dy.shape
    kdim = w.shape[0]
    tile = _tile(kdim, tile)

    def body(dy_ref, w_ref, act_ref, dx_ref, dw_ref):
        a = act_ref[...]
        dx = lax.dot_general(dy_ref[...], w_ref[...], NT, preferred_element_type=F32)
        if epilogue is not None:
            dx = epilogue(dx, a)
        dx_ref[...] = dx.astype(out_dtype)
        if act_fn is not None:
            a = act_fn(a)
        dw_ref[...] = lax.dot_general(a, dy_ref[...], TN, preferred_element_type=F32).astype(out_dtype)

    return pl.pallas_call(
        body, name=name, grid=(kdim // tile,),
        in_specs=[pl.BlockSpec((t, n), lambda j: (0, 0)), pl.BlockSpec((tile, n), lambda j: (j, 0)),
                  pl.BlockSpec((t, tile), lambda j: (0, j))],
        out_specs=[pl.BlockSpec((t, tile), lambda j: (0, j)), pl.BlockSpec((tile, n), lambda j: (j, 0))],
        out_shape=[jax.ShapeDtypeStruct((t, kdim), out_dtype), jax.ShapeDtypeStruct((kdim, n), out_dtype)],
        compiler_params=_params(("parallel",)),
    )(dy, w, act)


def mm_bwd_pair_blocked(name, dz, w, act, *, out_dtype, tile=1024):
    t = dz.shape[0]
    kdim, nb = w.shape[1], w.shape[2]
    tile = _tile(kdim, tile)

    def body(dz_ref, w_ref, act_ref, dx_ref, dw_ref, acc_ref):
        j = pl.program_id(1)
        dw_ref[...] = lax.dot_general(act_ref[...], dz_ref[...], TN, preferred_element_type=F32).astype(out_dtype)

        def product():
            return lax.dot_general(dz_ref[...], w_ref[...], NT, preferred_element_type=F32)

        @pl.when(j == 0)
        def _():
            acc_ref[...] = product()

        @pl.when(jnp.logical_and(j > 0, j < N_DEV - 1))
        def _():
            acc_ref[...] += product()

        @pl.when(j == N_DEV - 1)
        def _():
            dx_ref[...] = (acc_ref[...] + product()).astype(out_dtype)

    return pl.pallas_call(
        body, name=name, grid=(kdim // tile, N_DEV),
        in_specs=[pl.BlockSpec((t, nb), lambda i, j: (0, j)), pl.BlockSpec((None, tile, nb), lambda i, j: (j, i, 0)),
                  pl.BlockSpec((t, tile), lambda i, j: (0, i))],
        out_specs=[pl.BlockSpec((t, tile), lambda i, j: (0, i)),
                   pl.BlockSpec((None, tile, nb), lambda i, j: (j, i, 0))],
        out_shape=[jax.ShapeDtypeStruct((t, kdim), out_dtype), jax.ShapeDtypeStruct((N_DEV, kdim, nb), out_dtype)],
        scratch_shapes=[pltpu.VMEM((t, tile), F32)],
        compiler_params=_params(("parallel", "arbitrary")),
    )(dz, w, act)


def _rstd(v):
    return lax.rsqrt(jnp.mean(v * v, axis=-1, keepdims=True) + NORM_EPS)


def _rms_bwd(v, g, dy):
    r = _rstd(v)
    vhat = v * r
    dvh = dy * g
    dv = r * (dvh - vhat * jnp.mean(dvh * vhat, axis=-1, keepdims=True))
    return dv, dy * vhat


def _fold8(v):
    rows, n = v.shape
    return jnp.sum(v.reshape(rows // 8, 8, n), axis=0)


def _fold_lanes(v):
    out = v[:, 0:128]
    for i in range(1, v.shape[1] // 128):
        out = out + v[:, 128 * i:128 * (i + 1)]
    return out


def _accumulate(ref, v):
    i = pl.program_id(0)

    @pl.when(i == 0)
    def _():
        ref[...] = v

    @pl.when(i > 0)
    def _():
        ref[...] += v


def _row_call(body, name, t, ins, row_in, outs, acc_outs=(), tr=ROW_TILE):
    tr = _tile(t, tr)

    def in_spec(a, tiled):
        if isinstance(tiled, tuple):
            width, j = tiled
            return pl.BlockSpec((tr, width), lambda i: (i, j))
        return pl.BlockSpec((tr, a.shape[1]), lambda i: (i, 0)) if tiled else pl.BlockSpec(a.shape, lambda i: (0, 0))

    in_specs = [in_spec(a, tiled) for a, tiled in zip(ins, row_in)]
    out_specs = [pl.BlockSpec((tr, n), lambda i: (i, 0)) for n, _ in outs]
    out_specs += [pl.BlockSpec((8, n), lambda i: (0, 0)) for n in acc_outs]
    out_shape = [jax.ShapeDtypeStruct((t, n), dt) for n, dt in outs]
    out_shape += [jax.ShapeDtypeStruct((8, n), F32) for n in acc_outs]
    return pl.pallas_call(
        body, name=name, grid=(t // tr,), in_specs=in_specs, out_specs=out_specs, out_shape=out_shape,
        compiler_params=_params(("arbitrary",) if acc_outs else ("parallel",)),
    )(*ins)


def norm_pre(name, x, g):
    t, d = x.shape

    def body(x_ref, g_ref, h_ref):
        v = x_ref[...]
        h_ref[...] = (v * _rstd(v) * g_ref[...]).astype(BF16)

    return _row_call(body, name, t, [x, g], [True, False], [(d, BF16)])[0]


def post_pre(name, x, m, g_post, g_pre):
    t, d = x.shape

    def body(x_ref, m_ref, gp_ref, gn_ref, xo_ref, h_ref):
        mv = m_ref[...]
        xn = x_ref[...] + mv * _rstd(mv) * gp_ref[...]
        xo_ref[...] = xn
        h_ref[...] = (xn * _rstd(xn) * gn_ref[...]).astype(BF16)

    return _row_call(body, name, t, [x, m, g_post, g_pre], [True, True, False, False], [(d, F32), (d, BF16)])


def post_loss(name, x, f, g_post, target):
    t, d = x.shape

    def body(x_ref, f_ref, g_ref, t_ref, dx_ref, df_ref, loss_ref, dg_ref):
        fv = f_ref[...]
        g = g_ref[...]
        out = x_ref[...] + fv * _rstd(fv) * g
        err = out - t_ref[...]
        dx = err * (1.0 / d)
        dx_ref[...] = dx
        dfv, dg_rows = _rms_bwd(fv, g, dx)
        df_ref[...] = dfv.astype(BF16)
        _accumulate(loss_ref, _fold8(_fold_lanes(err * err)))
        _accumulate(dg_ref, _fold8(dg_rows))

    return _row_call(body, name, t, [x, f, g_post, target], [True, True, False, True],
                     [(d, F32), (d, BF16)], acc_outs=(128, d))


def bwd_pre_post(name, dx_out, x_in, g_pre, dh, f_prev, g_post_prev):
    t, d = x_in.shape

    def body(dxo_ref, x_ref, gpre_ref, dh_ref, f_ref, gpost_ref, dxi_ref, df_ref, dgpre_ref, dgpost_ref):
        dxv, dgpre_rows = _rms_bwd(x_ref[...], gpre_ref[...], dh_ref[...].astype(F32))
        dxi = dxo_ref[...] + dxv
        dxi_ref[...] = dxi
        dfv, dgpost_rows = _rms_bwd(f_ref[...], gpost_ref[...], dxi)
        df_ref[...] = dfv.astype(BF16)
        _accumulate(dgpre_ref, _fold8(dgpre_rows))
        _accumulate(dgpost_ref, _fold8(dgpost_rows))

    return _row_call(body, name, t, [dx_out, x_in, g_pre, dh, f_prev, g_post_prev],
                     [True, True, False, True, True, False], [(d, F32), (d, BF16)], acc_outs=(d, d))


def bwd_pre_final(name, dx_out, x_in, g_pre, dh):
    t, d = x_in.shape

    def body(dxo_ref, x_ref, gpre_ref, dh_ref, dxi_ref, dgpre_ref):
        dxv, dgpre_rows = _rms_bwd(x_ref[...], gpre_ref[...], dh_ref[...].astype(F32))
        dxi_ref[...] = dxo_ref[...] + dxv
        _accumulate(dgpre_ref, _fold8(dgpre_rows))

    return _row_call(body, name, t, [dx_out, x_in, g_pre, dh], [True, True, False, True], [(d, F32)], acc_outs=(d,))


def _layer_norm_parts(cv):
    mu = jnp.mean(cv, axis=-1, keepdims=True)
    xc = cv - mu
    rstd = lax.rsqrt(jnp.mean(xc * xc, axis=-1, keepdims=True) + NORM_EPS)
    return xc * rstd, rstd


def ln_silu(name, cv, g, b, y, y_block):
    t, n = cv.shape
    tr = _tile(t, ROW_TILE)

    def body(c_ref, g_ref, b_ref, y_in_ref, y_ref):
        chat, _ = _layer_norm_parts(c_ref[...])
        ln = chat * g_ref[...] + b_ref[...]
        y_ref[...] = (ln * jax.nn.sigmoid(ln)).astype(BF16)

    vec = pl.BlockSpec((1, n), lambda i: (0, 0))
    return pl.pallas_call(
        body, name=name, grid=(t // tr,),
        in_specs=[pl.BlockSpec((tr, n), lambda i: (i, 0)), vec, vec, ANY],
        out_specs=pl.BlockSpec((tr, n), lambda i: (i, y_block)),
        out_shape=jax.ShapeDtypeStruct(y.shape, y.dtype), input_output_aliases={3: 0},
        compiler_params=_params(("parallel",)),
    )(cv, g, b, y)


def ln_silu_bwd(name, cv, g, b, dy, dy_block):
    t, n = cv.shape

    def body(c_ref, g_ref, b_ref, dy_ref, dc_ref, dg_ref, db_ref):
        chat, rstd = _layer_norm_parts(c_ref[...])
        g = g_ref[...]
        ln = chat * g + b_ref[...]
        s = jax.nn.sigmoid(ln)
        dln = dy_ref[...].astype(F32) * (s * (1.0 + ln * (1.0 - s)))
        dchat = dln * g
        dc_ref[...] = rstd * (dchat - jnp.mean(dchat, axis=-1, keepdims=True)
                              - chat * jnp.mean(dchat * chat, axis=-1, keepdims=True))
        _accumulate(dg_ref, _fold8(dln * chat))
        _accumulate(db_ref, _fold8(dln))

    return _row_call(body, name, t, [cv, g, b, dy], [True, False, False, (n, dy_block)], [(n, F32)], acc_outs=(n, n))


def _chunks(t, fn, tc=TIME_CHUNK):
    tc = _tile(t, tc)

    def step(i, carry):
        fn(pl.multiple_of(i * tc, tc), tc)
        return carry

    lax.fori_loop(0, t // tc, step, 0)


def _rows_from(v, start, n):
    res = start % 8
    base = v if res == 0 else pltpu.roll(v, v.shape[0] - res, axis=0)
    return base[start - res:start - res + n, :]


def _shifted(window, offsets, tc):
    rows = window.shape[0]
    by_residue = {}
    for k, off in enumerate(offsets):
        by_residue.setdefault(off % 8, []).append((k, off))
    for res, taps in by_residue.items():
        base = window if res == 0 else pltpu.roll(window, rows - res, axis=0)
        for k, off in taps:
            yield k, base[off - res:off - res + tc, :]


def _taps(window, w_ref, offsets, tc, flip=False):
    acc = None
    for k, rows in _shifted(window, offsets, tc):
        kk = len(offsets) - 1 - k if flip else k
        term = w_ref[kk:kk + 1, :] * rows
        acc = term if acc is None else acc + term
    return acc


def _window_sums(win, tc, causal):
    sums = []
    cur, rows, step = win, tc + HALO, 1
    for _ in POOL_WINDOWS:
        rows -= 8
        if causal:
            cur = cur[8:8 + rows, :] + _rows_from(cur, 8 - step, rows)
            sums.append(cur[rows - tc:rows, :])
        else:
            cur = cur[0:rows, :] + _rows_from(cur, step, rows)
            sums.append(cur[0:tc, :])
        step *= 2
    return sums


def _pick(vals, g):
    out = vals[-1]
    for i in range(len(vals) - 2, -1, -1):
        out = jnp.where(g == i, vals[i], out)
    return out


def _pool_count(s, tc, g):
    t1 = (lax.broadcasted_iota(jnp.int32, (tc, 1), 0) + (s + 1)).astype(F32)
    width = _pick([float(w) for w in POOL_WINDOWS], g)
    return jnp.minimum(t1, width)


def pool_fwd(name, z, pool_w, pool_scale, d_pool, y_width):
    t = z.shape[0]
    ng, pg = pool_w.shape[0], pool_w.shape[1]

    def body(u_ref, w_ref, s_ref, pooled_ref, y_ref, pad):
        g = pl.program_id(0)
        pad[pl.ds(0, HALO), :] = jnp.zeros((HALO, pg), F32)

        def fill(s, tc):
            pad[pl.ds(HALO + s, tc), :] = u_ref[pl.ds(s, tc), :].astype(F32)

        def chunk(s, tc):
            win = pad[pl.ds(s, tc + HALO), :]
            total = _pick(_window_sums(win, tc, causal=True), g)
            pooled = total / _pool_count(s, tc, g) - win[HALO:HALO + tc, :]
            pooled_ref[pl.ds(s, tc), :] = pooled.astype(BF16)

        _chunks(t, fill)
        _chunks(t, chunk)
        mixed = jnp.dot(pooled_ref[...], w_ref[...], preferred_element_type=F32)
        y_ref[...] = (mixed * s_ref[...]).astype(BF16)

    col = pl.BlockSpec((t, pg), lambda g: (0, g))
    return pl.pallas_call(
        body, name=name, grid=(ng,),
        in_specs=[col, pl.BlockSpec((None, pg, pg), lambda g: (g, 0, 0)), pl.BlockSpec((1, pg), lambda g: (0, g))],
        out_specs=[col, col],
        out_shape=[jax.ShapeDtypeStruct((t, d_pool), BF16), jax.ShapeDtypeStruct((t, y_width), BF16)],
        scratch_shapes=[pltpu.VMEM((t + HALO, pg), F32)],
        compiler_params=_params(("parallel",)),
    )(z, pool_w, pool_scale)


def pool_bwd(name, pooled, dy, pool_w, pool_scale, dz):
    t, d_pool = pooled.shape
    ng, pg = pool_w.shape[0], pool_w.shape[1]

    def body(p_ref, dy_ref, w_ref, s_ref, dz_ref, du_ref, dw_ref, ds_ref, pad):
        g = pl.program_id(0)
        w = w_ref[...]
        dyv = dy_ref[...].astype(F32)
        mixed = jnp.dot(p_ref[...], w, preferred_element_type=F32)
        ds_ref[...] = jnp.sum(dyv * mixed, axis=0, keepdims=True)
        dmixed = (dyv * s_ref[...]).astype(BF16)
        dw_ref[...] = lax.dot_general(p_ref[...], dmixed, TN, preferred_element_type=F32)
        pad[...] = jnp.zeros((t + HALO, pg), F32)
        pad[pl.ds(0, t), :] = lax.dot_general(dmixed, w, NT, preferred_element_type=F32)

        def scale(s, tc):
            pad[pl.ds(s, tc), :] = pad[pl.ds(s, tc), :] / _pool_count(s, tc, g)

        def chunk(s, tc):
            win = pad[pl.ds(s, tc + HALO), :]
            total = _pick(_window_sums(win, tc, causal=False), g)
            du_ref[pl.ds(s, tc), :] = (total - win[0:tc, :] * _pool_count(s, tc, g)).astype(BF16)

        _chunks(t, scale)
        _chunks(t, chunk)

    col = pl.BlockSpec((t, pg), lambda g: (0, g))
    vec = pl.BlockSpec((1, pg), lambda g: (0, g))
    mat = pl.BlockSpec((None, pg, pg), lambda g: (g, 0, 0))
    return pl.pallas_call(
        body, name=name, grid=(ng,),
        in_specs=[col, col, mat, vec, ANY], out_specs=[col, mat, vec],
        out_shape=[jax.ShapeDtypeStruct(dz.shape, dz.dtype), jax.ShapeDtypeStruct((ng, pg, pg), F32),
                   jax.ShapeDtypeStruct((1, d_pool), F32)],
        input_output_aliases={4: 0},
        scratch_shapes=[pltpu.VMEM((t + HALO, pg), F32)],
        compiler_params=_params(("parallel",)),
    )(pooled, dy, pool_w, pool_scale, dz)


def conv_fwd(name, z, conv_w, conv_b, d_pool, d_conv):
    t = z.shape[0]
    kw = conv_w.shape[0]
    tc_ch = _tile(d_conv, CHANNEL_TILE)
    v0, g0 = d_pool // tc_ch, (d_pool + d_conv) // tc_ch

    def body(v_ref, g_ref, w_ref, b_ref, c_ref, pad):
        pad[pl.ds(0, HALO), :] = jnp.zeros((HALO, tc_ch), F32)

        def fill(s, tc):
            pad[pl.ds(HALO + s, tc), :] = v_ref[pl.ds(s, tc), :].astype(F32) * jax.nn.sigmoid(g_ref[pl.ds(s, tc), :].astype(F32))

        def chunk(s, tc):
            win = pad[pl.ds(s, tc + HALO), :]
            c_ref[pl.ds(s, tc), :] = _taps(win, w_ref, [HALO - (kw - 1) + k for k in range(kw)], tc) + b_ref[...]

        _chunks(t, fill)
        _chunks(t, chunk)

    return pl.pallas_call(
        body, name=name, grid=(d_conv // tc_ch,),
        in_specs=[pl.BlockSpec((t, tc_ch), lambda j: (0, v0 + j)), pl.BlockSpec((t, tc_ch), lambda j: (0, g0 + j)),
                  pl.BlockSpec((kw, tc_ch), lambda j: (0, j)), pl.BlockSpec((1, tc_ch), lambda j: (0, j))],
        out_specs=pl.BlockSpec((t, tc_ch), lambda j: (0, j)),
        out_shape=jax.ShapeDtypeStruct((t, d_conv), F32),
        scratch_shapes=[pltpu.VMEM((t + HALO, tc_ch), F32)],
        compiler_params=_params(("parallel",)),
    )(z, z, conv_w, conv_b)


def conv_bwd(name, z, dc, conv_w, d_pool, d_conv):
    t = z.shape[0]
    kw = conv_w.shape[0]
    tc_ch = _tile(d_conv, CHANNEL_TILE)
    v0, g0 = d_pool // tc_ch, (d_pool + d_conv) // tc_ch

    def body(v_ref, g_ref, dc_ref, w_ref, dz_ref, dw_ref, db_ref, pad_a, pad_dc, acc_w, acc_b, tiles, sems):
        j = pl.program_id(0)
        dv_ref, dg_ref = tiles.at[0], tiles.at[1]
        writes = [pltpu.make_async_copy(tiles.at[p], dz_ref.at[:, pl.ds((first + j) * tc_ch, tc_ch)], sems.at[p])
                  for p, first in enumerate([v0, g0])]

        def wait_writes():
            for cp in writes:
                cp.wait()

        pad_a[pl.ds(0, HALO), :] = jnp.zeros((HALO, tc_ch), F32)
        pad_dc[pl.ds(t, HALO), :] = jnp.zeros((HALO, tc_ch), F32)
        acc_w[...] = jnp.zeros_like(acc_w)
        acc_b[...] = jnp.zeros_like(acc_b)

        def fill(s, tc):
            pad_a[pl.ds(HALO + s, tc), :] = v_ref[pl.ds(s, tc), :].astype(F32) * jax.nn.sigmoid(g_ref[pl.ds(s, tc), :].astype(F32))
            pad_dc[pl.ds(s, tc), :] = dc_ref[pl.ds(s, tc), :]

        def chunk(s, tc):
            dcv = pad_dc[pl.ds(s, tc), :]
            win_a = pad_a[pl.ds(s, tc + HALO), :]
            for k, rows in _shifted(win_a, [HALO - (kw - 1) + k for k in range(kw)], tc):
                acc_w[pl.ds(8 * k, 8), :] += _fold8(dcv * rows)
            acc_b[...] += _fold8(dcv)
            da = _taps(pad_dc[pl.ds(s, tc + HALO), :], w_ref, list(range(kw)), tc, flip=True)
            vv = v_ref[pl.ds(s, tc), :].astype(F32)
            sg = jax.nn.sigmoid(g_ref[pl.ds(s, tc), :].astype(F32))
            dv_ref[pl.ds(s, tc), :] = (da * sg).astype(BF16)
            dg_ref[pl.ds(s, tc), :] = (da * vv * sg * (1.0 - sg)).astype(BF16)

        _chunks(t, fill)
        pl.when(j > 0)(wait_writes)
        _chunks(t, chunk)
        for cp in writes:
            cp.start()
        pl.when(j == n_tiles - 1)(wait_writes)
        for k in range(kw):
            dw_ref[k:k + 1, :] = jnp.sum(acc_w[pl.ds(8 * k, 8), :], axis=0, keepdims=True)
        db_ref[...] = jnp.sum(acc_b[...], axis=0, keepdims=True)

    n_tiles = d_conv // tc_ch
    return pl.pallas_call(
        body, name=name, grid=(n_tiles,),
        in_specs=[pl.BlockSpec((t, tc_ch), lambda j: (0, v0 + j)), pl.BlockSpec((t, tc_ch), lambda j: (0, g0 + j)),
                  pl.BlockSpec((t, tc_ch), lambda j: (0, j)), pl.BlockSpec((kw, tc_ch), lambda j: (0, j))],
        out_specs=[ANY, pl.BlockSpec((kw, tc_ch), lambda j: (0, j)), pl.BlockSpec((1, tc_ch), lambda j: (0, j))],
        out_shape=[jax.ShapeDtypeStruct((t, d_pool + 2 * d_conv), BF16),
                   jax.ShapeDtypeStruct((kw, d_conv), F32), jax.ShapeDtypeStruct((1, d_conv), F32)],
        scratch_shapes=[pltpu.VMEM((t + HALO, tc_ch), F32), pltpu.VMEM((t + HALO, tc_ch), F32),
                        pltpu.VMEM((8 * kw, tc_ch), F32), pltpu.VMEM((8, tc_ch), F32),
                        pltpu.VMEM((2, t, tc_ch), BF16), pltpu.SemaphoreType.DMA((2,))],
        compiler_params=_params(("arbitrary",)),
    )(z, z, dc, conv_w)


def short_fwd(name, z, conv_w, d_short):
    t = z.shape[0]
    kw = conv_w.shape[0]
    tc_ch = _tile(d_short, CHANNEL_TILE)
    nt = d_short // tc_ch

    def body(b_ref, c_ref, u_ref, w_ref, y_ref, pad):
        pad[pl.ds(0, HALO), :] = jnp.zeros((HALO, tc_ch), F32)

        def fill(s, tc):
            pad[pl.ds(HALO + s, tc), :] = c_ref[pl.ds(s, tc), :].astype(F32) * u_ref[pl.ds(s, tc), :].astype(F32)

        def chunk(s, tc):
            win = pad[pl.ds(s, tc + HALO), :]
            cq = _taps(win, w_ref, [HALO - (kw - 1) + k for k in range(kw)], tc)
            y_ref[pl.ds(s, tc), :] = (b_ref[pl.ds(s, tc), :].astype(F32) * cq).astype(BF16)

        _chunks(t, fill)
        _chunks(t, chunk)

    return pl.pallas_call(
        body, name=name, grid=(nt,),
        in_specs=[pl.BlockSpec((t, tc_ch), lambda j: (0, j)), pl.BlockSpec((t, tc_ch), lambda j: (0, nt + j)),
                  pl.BlockSpec((t, tc_ch), lambda j: (0, 2 * nt + j)), pl.BlockSpec((kw, tc_ch), lambda j: (0, j))],
        out_specs=pl.BlockSpec((t, tc_ch), lambda j: (0, j)),
        out_shape=jax.ShapeDtypeStruct((t, d_short), BF16),
        scratch_shapes=[pltpu.VMEM((t + HALO, tc_ch), F32)],
        compiler_params=_params(("parallel",)),
    )(z, z, z, conv_w)


def short_bwd(name, z, dy, conv_w, d_short):
    t = z.shape[0]
    kw = conv_w.shape[0]
    tc_ch = _tile(d_short, CHANNEL_TILE)
    nt = d_short // tc_ch

    def body(b_ref, c_ref, u_ref, dy_ref, w_ref, dz_ref, dw_ref, pad_q, pad_dcq, acc_w, tiles, sems):
        j = pl.program_id(0)
        db_ref, dcg_ref, du_ref = tiles.at[0], tiles.at[1], tiles.at[2]
        writes = [pltpu.make_async_copy(tiles.at[p], dz_ref.at[:, pl.ds((p * nt + j) * tc_ch, tc_ch)], sems.at[p])
                  for p in range(3)]

        def wait_writes():
            for cp in writes:
                cp.wait()

        pad_q[pl.ds(0, HALO), :] = jnp.zeros((HALO, tc_ch), F32)
        pad_dcq[pl.ds(t, HALO), :] = jnp.zeros((HALO, tc_ch), F32)
        acc_w[...] = jnp.zeros_like(acc_w)

        def fill(s, tc):
            rows = pl.ds(s, tc)
            pad_q[pl.ds(HALO + s, tc), :] = c_ref[rows, :].astype(F32) * u_ref[rows, :].astype(F32)
            pad_dcq[rows, :] = dy_ref[rows, :].astype(F32) * b_ref[rows, :].astype(F32)

        def chunk(s, tc):
            rows = pl.ds(s, tc)
            win_q = pad_q[pl.ds(s, tc + HALO), :]
            dcq = pad_dcq[rows, :]
            cq = None
            for k, shifted in _shifted(win_q, [HALO - (kw - 1) + k for k in range(kw)], tc):
                acc_w[pl.ds(8 * k, 8), :] += _fold8(dcq * shifted)
                term = w_ref[k:k + 1, :] * shifted
                cq = term if cq is None else cq + term
            db_ref[rows, :] = (dy_ref[rows, :].astype(F32) * cq).astype(BF16)
            dq = _taps(pad_dcq[pl.ds(s, tc + HALO), :], w_ref, list(range(kw)), tc, flip=True)
            dcg_ref[rows, :] = (dq * u_ref[rows, :].astype(F32)).astype(BF16)
            du_ref[rows, :] = (dq * c_ref[rows, :].astype(F32)).astype(BF16)

        _chunks(t, fill)
        pl.when(j > 0)(wait_writes)
        _chunks(t, chunk)
        for cp in writes:
            cp.start()
        pl.when(j == nt - 1)(wait_writes)
        for k in range(kw):
            dw_ref[k:k + 1, :] = jnp.sum(acc_w[pl.ds(8 * k, 8), :], axis=0, keepdims=True)

    zspec = [pl.BlockSpec((t, tc_ch), lambda j, o=o: (0, o * nt + j)) for o in range(3)]
    return pl.pallas_call(
        body, name=name, grid=(nt,),
        in_specs=[*zspec, pl.BlockSpec((t, tc_ch), lambda j: (0, j)), pl.BlockSpec((kw, tc_ch), lambda j: (0, j))],
        out_specs=[ANY, pl.BlockSpec((kw, tc_ch), lambda j: (0, j))],
        out_shape=[jax.ShapeDtypeStruct((t, 3 * d_short), BF16), jax.ShapeDtypeStruct((kw, d_short), F32)],
        scratch_shapes=[pltpu.VMEM((t + HALO, tc_ch), F32), pltpu.VMEM((t + HALO, tc_ch), F32),
                        pltpu.VMEM((8 * kw, tc_ch), F32), pltpu.VMEM((3, t, tc_ch), BF16),
                        pltpu.SemaphoreType.DMA((3,))],
        compiler_params=_params(("arbitrary",)),
    )(z, z, z, dy, conv_w)


def _adamw_update(w, m, v, g):
    nm = ADAM_B1 * m + (1.0 - ADAM_B1) * g
    nv = ADAM_B2 * v + (1.0 - ADAM_B2) * (g * g)
    m_hat = nm / (1.0 - ADAM_B1 ** ADAM_STEP)
    v_hat = nv / (1.0 - ADAM_B2 ** ADAM_STEP)
    return -ADAM_LR * (m_hat / (jnp.sqrt(v_hat) + ADAM_EPS) + ADAM_WD * w), nm, nv


def adamw_replicated(name, params, first_moments, second_moments, contributions, layout, scalar_at):
    n = len(params)
    n_slots = contributions.shape[0]

    def total(c_ref, row, lane, rows, lanes):
        acc = c_ref[0, row:row + rows, lane:lane + lanes]
        for slot in range(1, n_slots):
            acc = acc + c_ref[slot, row:row + rows, lane:lane + lanes]
        return acc

    def body(*refs):
        ws, ms, vs, c_ref = refs[:n], refs[n:2 * n], refs[2 * n:3 * n], refs[3 * n]
        outs = refs[3 * n + 1:]
        outs[0][...] = total(c_ref, *scalar_at, 1, 128)
        for i, (row, lane) in enumerate(layout):
            g = total(c_ref, row, lane, *params[i].shape)
            grad_ref, delta_ref, nm_ref, nv_ref = outs[1 + 4 * i:5 + 4 * i]
            grad_ref[...] = g
            delta_ref[...], nm_ref[...], nv_ref[...] = _adamw_update(ws[i][...], ms[i][...], vs[i][...], g)

    out_shape = [jax.ShapeDtypeStruct((1, 128), F32)]
    for p in params:
        out_shape += [jax.ShapeDtypeStruct(p.shape, F32)] * 4
    return pl.pallas_call(body, name=name, out_shape=out_shape)(*params, *first_moments, *second_moments, contributions)


def adamw(name, w, m, v, contributions):
    r, c = w.shape
    nc = len(contributions)
    n_slots = contributions[0].shape[0]
    tr = 256 if c <= 1024 else 128
    if any(a.shape[1] % tr for a in contributions):
        assert nc == 1
        tr = r
    tiles = [a.shape[1] // tr for a in contributions]
    first = [sum(tiles[:j]) for j in range(nc)]

    def body(w_ref, m_ref, v_ref, *rest):
        g_refs, (grad_ref, delta_ref, nm_ref, nv_ref) = rest[:nc], rest[nc:]
        i = pl.program_id(0)
        g = None
        for j, g_ref in enumerate(g_refs):
            s = g_ref[0].astype(F32)
            for slot in range(1, n_slots):
                s = s + g_ref[slot].astype(F32)
            g = s if g is None else jnp.where(i >= first[j], s, g)
        grad_ref[...] = g
        delta_ref[...], nm_ref[...], nv_ref[...] = _adamw_update(w_ref[...], m_ref[...], v_ref[...], g)

    blk = pl.BlockSpec((tr, c), lambda i: (i, 0))
    g_specs = [pl.BlockSpec((n_slots, tr, c), lambda i, j=j: (0, jnp.clip(i - first[j], 0, tiles[j] - 1), 0))
               for j in range(nc)]
    return pl.pallas_call(
        body, name=name, grid=(r // tr,),
        in_specs=[blk, blk, blk, *g_specs],
        out_specs=[blk] * 4, out_shape=[jax.ShapeDtypeStruct((r, c), F32)] * 4,
        compiler_params=_params(("parallel",)),
    )(w, m, v, *contributions)


def _pad_rows(a, rows):
    return jnp.pad(a, ((0, rows - a.shape[0]), (0, 0)))


def kernel(x, mix_pre_g, mix_post_g, ffn_pre_g, ffn_post_g, ab_w_in, pool_w, pool_scale, conv_w, conv_b, conv_ln_g, conv_ln_b, ab_w_out, sc_w_in, sc_conv_w, sc_w_out, ffn_w1, ffn_w2, loss_target, m_mix_pre_g, m_mix_post_g, m_ffn_pre_g, m_ffn_post_g, m_ab_w_in, m_pool_w, m_pool_scale, m_conv_w, m_conv_b, m_conv_ln_g, m_conv_ln_b, m_ab_w_out, m_sc_w_in, m_sc_conv_w, m_sc_w_out, m_ffn_w1, m_ffn_w2, v_mix_pre_g, v_mix_post_g, v_ffn_pre_g, v_ffn_post_g, v_ab_w_in, v_pool_w, v_pool_scale, v_conv_w, v_conv_b, v_conv_ln_g, v_conv_ln_b, v_ab_w_out, v_sc_w_in, v_sc_conv_w, v_sc_w_out, v_ffn_w1, v_ffn_w2):
    t, d = x.shape[1], x.shape[2]
    d_pool = pool_scale.shape[1]
    d_conv = conv_b.shape[1]
    d_short = d
    ng, pg = pool_w.shape[1], pool_w.shape[3]
    kw, ks = conv_w.shape[1], sc_conv_w.shape[1]
    nb_ab, nb_sc, nb_ff = ab_w_in.shape[2], sc_w_in.shape[2], ffn_w1.shape[2]

    xs = x[0]
    target = loss_target[0]

    lanes = min(128, d_conv // N_DEV)
    small_rows = [kw * (d_conv // N_DEV) // lanes, ks * (d_short // N_DEV) // lanes, ng * (pg // N_DEV) * pg // lanes]
    small_total = -(-sum(small_rows) // 8) * 8
    r0, r1, r2 = small_rows[0], small_rows[0] + small_rows[1], sum(small_rows)

    def pack_small(a_conv, a_sconv, a_pool):
        parts = [a_conv[0].reshape(-1, lanes), a_sconv[0].reshape(-1, lanes), a_pool[0].reshape(-1, lanes)]
        return _pad_rows(jnp.concatenate(parts, axis=0), small_total)

    shards = {
        "ab_in": (ab_w_in, 0, BF16), "small": (pack_small(conv_w, sc_conv_w, pool_w)[None], 0, F32),
        "ab_out": (ab_w_out, 0, BF16), "ff1_0": (ffn_w1, 0, BF16), "ff2_0": (ffn_w2, 0, BF16),
        "sc_in": (sc_w_in, 0, BF16), "sc_out": (sc_w_out, 0, BF16),
        "ff1_1": (ffn_w1, 1, BF16), "ff2_1": (ffn_w2, 1, BF16)}
    direct = ["ab_in", "small", "ab_out"]
    zones = {nm: place_shard("place_" + nm, *shards[nm]) for nm in direct}
    started, token = copies_start("gather_start", [[zones[nm]] for nm in direct], _first_hop, 4)
    started = dict(zip(direct, started))
    zones["ff1_0"] = place_shard("place_ff1_0", *shards["ff1_0"], deps=[token])
    (head,), token = copies_start("ring_start_ff1_0", [[zones["ff1_0"]]], _ring_hop1, 3, deps=[token])
    ring = {"ff1_0": head}

    ties = [0]

    def after(v, *deps):
        ties[0] += 1
        return tie(f"tie_{ties[0]}", v, *deps)

    def fetch_begin(nm, dep):
        (zone,) = copies_wait("gather_wait_" + nm, started[nm], _first_hop, dep)
        (hop,), tok = copies_start("forward_start_" + nm, [[zone]], _second_hop, 3)
        return hop, tok

    def fetch_end(nm, hop, dep):
        return copies_wait("forward_wait_" + nm, hop, _second_hop, dep)[0]

    def ring_step(tag, dep, *starts, place=()):
        place = list(place) + [nm for n, nm in starts if n == 1 and nm not in zones and nm not in place]
        before_waits = dep
        for nm in place:
            before_waits = zones[nm] = place_shard("place_" + nm, *shards[nm], deps=[before_waits])
        names, groups, hops, counts = [], [], [], []
        for n, nm in starts:
            if n == 1:
                groups.append([zones[nm]])
            elif n == 2:
                groups.append(copies_wait("ring1_wait_" + nm, ring[nm], _ring_hop1, before_waits))
            else:
                groups.append(copies_wait("ring2_wait_" + nm, ring[nm], _ring_hop2, before_waits))
            hop, n_copies = RING_HOPS[n - 1]
            names, hops, counts = names + [nm], hops + [hop], counts + [n_copies]
        begun, tok = copies_start("ring_start_" + tag, groups, hops, counts, deps=[dep])
        ring.update(zip(names, begun))
        return tok

    def ring_done(nm, dep):
        return copies_wait("ring3_wait_" + nm, ring[nm], _ring_hop3, dep)[0]

    relu = lambda r: jnp.maximum(r, 0.0)
    square = lambda a: a * a
    relu2_bwd = lambda r, a: r * (2.0 * a.astype(F32))

    def row(vec, l):
        return vec[l:l + 1]

    hop_small, _ = fetch_begin("small", token)
    hop_ab_in, tok = fetch_begin("ab_in", token)
    w_small = fetch_end("small", hop_small, tok)
    w_ab_in = fetch_end("ab_in", hop_ab_in, tok)
    w_conv = w_small[:, :r0].reshape(N_DEV, kw, -1).transpose(1, 0, 2).reshape(kw, d_conv)
    w_sconv = w_small[:, r0:r1].reshape(N_DEV, ks, -1).transpose(1, 0, 2).reshape(ks, d_short)
    w_pool = w_small[:, r1:r2].reshape(N_DEV, ng, -1, pg).transpose(1, 0, 2, 3).reshape(ng, pg, pg).astype(BF16)
    h0 = norm_pre("norm_pre", xs, after(row(mix_pre_g, 0), token))
    z0 = mm_nn_blocked("ab_in", h0, w_ab_in, out_dtype=BF16)
    hop, tok = fetch_begin("ab_out", z0)
    z0 = after(z0, tok)
    pooled, y0 = pool_fwd("pool_fwd", z0, w_pool, pool_scale, d_pool, d_pool + d_conv)
    cv = conv_fwd("conv_fwd", z0, w_conv, conv_b, d_pool, d_conv)
    y0 = ln_silu("ln_silu", cv, conv_ln_g, conv_ln_b, y0, d_pool // d_conv)
    w_ab_out = fetch_end("ab_out", hop, y0)

    def ffn_up(name, h, w, *starts):
        a = mm_nn_blocked(name, h, w, out_dtype=BF16, epilogue=relu, blocks=(0, UP_FIRST_BLOCKS))
        tok = ring_step(name, a, *starts)
        return mm_nn_blocked(name + "_rest", h, w, out_dtype=BF16, epilogue=relu,
                             blocks=(UP_FIRST_BLOCKS, N_DEV - UP_FIRST_BLOCKS), into=after(a, tok))

    tok = ring_step("a", w_ab_out, (2, "ff1_0"), (1, "ff2_0"))
    y0 = after(y0, tok)
    m0 = mm_nn("ab_out", y0, w_ab_out.reshape(d_pool + d_conv, d), out_dtype=F32)
    x1, h1 = post_pre("post_pre_0", xs, m0, row(mix_post_g, 0), row(ffn_pre_g, 0))
    tok = ring_step("b", h1, (3, "ff1_0"))
    w_ff1_0 = ring_done("ff1_0", tok)
    a0 = ffn_up("ffn0_up", h1, w_ff1_0, (2, "ff2_0"), (1, "sc_in"), (1, "sc_out"))
    tok = ring_step("c", a0, (3, "ff2_0"), place=["ff1_1"])
    w_ff2_0 = ring_done("ff2_0", tok).reshape(-1, d)
    f0 = mm_nn("ffn0_down", a0, w_ff2_0, out_dtype=F32, tk=2048, lhs_fn=square)
    tok = ring_step("d", f0, (2, "sc_in"), (2, "sc_out"), (1, "ff1_1"))
    f0 = after(f0, tok)
    x2, h2 = post_pre("post_pre_1", x1, f0, row(ffn_post_g, 0), row(mix_pre_g, 1))
    tok = ring_step("e", h2, (3, "sc_in"), place=["ff2_1"])
    w_sc_in = ring_done("sc_in", tok)
    z1 = mm_nn_blocked("sc_in", h2, w_sc_in, out_dtype=BF16)
    y1 = short_fwd("short_fwd", z1, w_sconv, d_short)
    tok = ring_step("f", y1, (3, "sc_out"), (2, "ff1_1"), (1, "ff2_1"))
    w_sc_out = ring_done("sc_out", tok).reshape(d_short, d)
    m1 = mm_nn("sc_out", y1, w_sc_out, out_dtype=F32)
    x3, h3 = post_pre("post_pre_2", x2, m1, row(mix_post_g, 1), row(ffn_pre_g, 1))
    tok = ring_step("g", h3, (3, "ff1_1"))
    w_ff1_1 = ring_done("ff1_1", tok)
    a1 = ffn_up("ffn1_up", h3, w_ff1_1, (2, "ff2_1"))
    tok = ring_step("h", a1, (3, "ff2_1"))
    w_ff2_1 = ring_done("ff2_1", tok).reshape(-1, d)
    f1 = mm_nn("ffn1_down", a1, w_ff2_1, out_dtype=F32, tk=2048, lhs_fn=square)
    dx4, df1, loss_part, dg_ffn_post1 = post_loss("post_loss", x3, f1, row(ffn_post_g, 1), target)

    red = {}

    def reduce_step(dep, begin=None, middle=None):
        tags, groups, hops, counts = [], [], [], []
        if begin is not None:
            tag, g = begin
            tags, groups = tags + [tag], groups + [[g, lax.empty((N_CHIP,) + g.shape[1:], g.dtype)]]
            hops, counts = hops + [_pair_hop], counts + [N_CHIP]
        if middle is not None:
            g, from_sibling = copies_wait("pair_wait_" + middle, red[middle], _pair_hop, dep)
            tags, groups = tags + [middle], groups + [list(pair_add("pair_add_" + middle, g, from_sibling))]
            hops, counts = hops + [_chip_hop], counts + [3]
        begun, tok = copies_start("reduce_start_" + "_".join(tags), groups, hops, counts, deps=[dep])
        red.update(zip(tags, begun))
        return tok

    def reduce_end(tag, dep):
        return copies_wait("chips_wait_" + tag, red[tag], _chip_hop, dep)[1]

    dpre, dw = mm_bwd_pair("ffn1_da_dw2", df1, w_ff2_1, a1, out_dtype=BF16, act_fn=square, epilogue=relu2_bwd)
    dpre = after(dpre, reduce_step(dpre, begin=("ff2_1", dw.reshape(N_DEV, -1, d))))
    dh3, dw = mm_bwd_pair_blocked("ffn1_dh_dw1", dpre, w_ff1_1, h3, out_dtype=BF16)
    dx3, dm1, dg_ffn_pre1, dg_mix_post1 = bwd_pre_post("bwd_3", dx4, x3, row(ffn_pre_g, 1), dh3, m1, row(mix_post_g, 1))
    dm1 = after(dm1, reduce_step(dm1, begin=("ff1_1", dw), middle="ff2_1"))

    dy1, dw = mm_bwd_pair("sc_dy_dwout", dm1, w_sc_out, y1, out_dtype=BF16)
    dy1 = after(dy1, reduce_step(dy1, begin=("sc_out", dw.reshape(N_DEV, -1, d)), middle="ff1_1"))
    dz1, dw_sconv = short_bwd("short_bwd", z1, dy1, w_sconv, d_short)
    dh2, dw = mm_bwd_pair_blocked("sc_dh_dwin", dz1, w_sc_in, h2, out_dtype=BF16)
    dx2, df0, dg_mix_pre1, dg_ffn_post0 = bwd_pre_post("bwd_2", dx3, x2, row(mix_pre_g, 1), dh2, f0, row(ffn_post_g, 0))
    df0 = after(df0, reduce_step(df0, begin=("sc_in", dw), middle="sc_out"))

    dpre, dw = mm_bwd_pair("ffn0_da_dw2", df0, w_ff2_0, a0, out_dtype=BF16, act_fn=square, epilogue=relu2_bwd)
    dpre = after(dpre, reduce_step(dpre, begin=("ff2_0", dw.reshape(N_DEV, -1, d)), middle="sc_in"))
    dh1, dw = mm_bwd_pair_blocked("ffn0_dh_dw1", dpre, w_ff1_0, h1, out_dtype=BF16)
    dh1 = after(dh1, reduce_step(dh1, begin=("ff1_0", dw), middle="ff2_0"))
    dx1, dm0, dg_ffn_pre0, dg_mix_post0 = bwd_pre_post("bwd_1", dx2, x1, row(ffn_pre_g, 0), dh1, m0, row(mix_post_g, 0))

    dy0, dw = mm_bwd_pair("ab_dy_dwout", dm0, w_ab_out.reshape(d_pool + d_conv, d), y0, out_dtype=BF16)
    dy0 = after(dy0, reduce_step(dy0, begin=("ab_out", dw.reshape(N_DEV, -1, d)), middle="ff1_0"))
    dcv, dg_ln_g, dg_ln_b = ln_silu_bwd("ln_silu_bwd", cv, conv_ln_g, conv_ln_b, dy0, d_pool // d_conv)
    dz0, dw_conv, dg_conv_b = conv_bwd("conv_bwd", z0, dcv, w_conv, d_pool, d_conv)
    dz0, dw_pool, dg_pool_scale = pool_bwd("pool_bwd", pooled, dy0, w_pool, pool_scale, dz0)
    small_parts = [
        dw_conv.reshape(kw, N_DEV, -1).transpose(1, 0, 2).reshape(N_DEV, -1, lanes),
        dw_sconv.reshape(ks, N_DEV, -1).transpose(1, 0, 2).reshape(N_DEV, -1, lanes),
        dw_pool.reshape(ng, N_DEV, pg // N_DEV, pg).transpose(1, 0, 2, 3).reshape(N_DEV, -1, lanes),
    ]
    small = jnp.pad(jnp.concatenate(small_parts, axis=1), ((0, 0), (0, small_total - r2), (0, 0)))
    dz0 = after(dz0, reduce_step(dz0, begin=("small", small), middle="ab_out"))
    dh0, dw = mm_bwd_pair_blocked("ab_dh_dwin", dz0, w_ab_in, h0, out_dtype=BF16)
    dh0 = after(dh0, reduce_step(dh0, begin=("ab_in", dw), middle="small"))
    grad_x, dg_mix_pre0 = bwd_pre_final("bwd_0", dx1, xs, row(mix_pre_g, 0), dh0)
    tok = reduce_step(grad_x, middle="ab_in")

    gains = [dg_mix_pre0, dg_mix_pre1, dg_mix_post0, dg_mix_post1, dg_ffn_pre0, dg_ffn_pre1, dg_ffn_post0, dg_ffn_post1]
    pieces = [(g, i, 0) for i, g in enumerate(gains)]
    rep_layout = [(0, 0), (2, 0), (4, 0), (6, 0)]
    offset = 0
    for g in (dg_pool_scale, dg_conv_b, dg_ln_g, dg_ln_b):
        at = (len(gains) + offset // d, offset % d)
        pieces.append((g, *at))
        rep_layout.append(at)
        offset += g.shape[1]
    loss_at = (len(gains) + -(-offset // d), 0)
    rep_zone = place_sheet("place_rep", pieces, loss_part, loss_at, 16, d)
    (rep_hop,), tok = copies_start("rep_start", [[rep_zone]], _first_hop, 4,
                                   deps=[tok])

    def upd(name, w, m, v, contribs):
        shape = w.shape
        flat2 = lambda a: a.reshape(-1, shape[-1])
        outs = adamw(name, flat2(w), flat2(m), flat2(v), contribs)
        return [o.reshape(shape) for o in outs]

    g_ff2 = [reduce_end("ff2_0", tok), reduce_end("ff2_1", tok)]
    o_ff2 = upd("adam_ffn_w2", ffn_w2, m_ffn_w2, v_ffn_w2, g_ff2)
    (rep_zone,) = copies_wait("rep_wait", rep_hop, _first_hop, o_ff2[0])
    (rep_hop,), _ = copies_start("rep_forward_start", [[rep_zone]], _second_hop, 3)
    g_ff1 = [reduce_end("ff1_0", o_ff2[0]), reduce_end("ff1_1", o_ff2[0])]
    o_ff1 = upd("adam_ffn_w1", ffn_w1, m_ffn_w1, v_ffn_w1, g_ff1)
    (rep_all,) = copies_wait("rep_forward_wait", rep_hop, _second_hop, o_ff1[0])
    loss_sum, *o_rep = adamw_replicated(
        "adam_replicated",
        [mix_pre_g, mix_post_g, ffn_pre_g, ffn_post_g, pool_scale, conv_b, conv_ln_g, conv_ln_b],
        [m_mix_pre_g, m_mix_post_g, m_ffn_pre_g, m_ffn_post_g, m_pool_scale, m_conv_b, m_conv_ln_g, m_conv_ln_b],
        [v_mix_pre_g, v_mix_post_g, v_ffn_pre_g, v_ffn_post_g, v_pool_scale, v_conv_b, v_conv_ln_g, v_conv_ln_b],
        rep_all, rep_layout, loss_at)
    loss = loss_sum[0, 0] * (0.5 / d)
    o_sc_out = upd("adam_sc_out", sc_w_out, m_sc_w_out, v_sc_w_out, [reduce_end("sc_out", o_ff1[0])])
    o_sc_in = upd("adam_sc_in", sc_w_in, m_sc_w_in, v_sc_w_in, [reduce_end("sc_in", o_sc_out[0])])
    o_ab_out = upd("adam_ab_out", ab_w_out, m_ab_w_out, v_ab_w_out, [reduce_end("ab_out", o_sc_in[0])])
    o_small = adamw("adam_small", pack_small(conv_w, sc_conv_w, pool_w), pack_small(m_conv_w, m_sc_conv_w, m_pool_w),
                    pack_small(v_conv_w, v_sc_conv_w, v_pool_w), [reduce_end("small", o_ab_out[0])])
    o_ab_in = upd("adam_ab_in", ab_w_in, m_ab_w_in, v_ab_w_in, [reduce_end("ab_in", o_small[0])])

    def unpack_small(o):
        return o[:r0].reshape(conv_w.shape), o[r0:r1].reshape(sc_conv_w.shape), o[r1:r2].reshape(pool_w.shape)

    results = []
    for kind in range(4):
        g_mix_pre, g_mix_post, g_ffn_pre, g_ffn_post, g_scale, g_conv_b, g_ln_g, g_ln_b = o_rep[kind::4]
        s_conv, s_sconv, s_pool = unpack_small(o_small[kind])
        results.append([
            g_mix_pre, g_mix_post, g_ffn_pre, g_ffn_post,
            o_ab_in[kind], s_pool, g_scale, s_conv, g_conv_b, g_ln_g, g_ln_b,
            o_ab_out[kind], o_sc_in[kind], s_sconv, o_sc_out[kind], o_ff1[kind], o_ff2[kind]])

    return (loss, grad_x[None], *results[0], *results[1], *results[2], *results[3])
```

```python
import jax
import jax.numpy as jnp
from jax import lax
from jax.experimental import pallas as pl
from jax.experimental.pallas import tpu as pltpu

F32 = jnp.float32
BF16 = jnp.bfloat16
MESH = pl.DeviceIdType.MESH
ANY = pl.BlockSpec(memory_space=pl.ANY)

NORM_EPS = 1e-6
POOL_WINDOWS = (2, 4, 8, 16)
ADAM_LR = 0.001
ADAM_B1 = 0.9
ADAM_B2 = 0.999
ADAM_EPS = 1e-08
ADAM_WD = 0.01
ADAM_STEP = 10

N_DEV = 8
VMEM_LIMIT = 56 * 1024 * 1024
PAIR_ADD_BLOCK = 1 << 20
MATMUL_ROWS = 2048
UP_FIRST_BLOCKS = 6
ROW_TILE = 256
CHANNEL_TILE = 256
TIME_CHUNK = 64
HALO = 32

NN = (((1,), (0,)), ((), ()))
NT = (((1,), (1,)), ((), ()))
TN = (((0,), (0,)), ((), ()))


def _params(sem):
    return pltpu.CompilerParams(dimension_semantics=sem, vmem_limit_bytes=VMEM_LIMIT)


def _place():
    x, y, c = lax.axis_index("x"), lax.axis_index("y"), lax.axis_index("c")
    return x, y, c


def _slot(px, py, pc):
    return 4 * px + 2 * py + pc


HBM = pl.BlockSpec(memory_space=pltpu.HBM)
SEM = pl.BlockSpec(memory_space=pltpu.SEMAPHORE)
EFFECT = pltpu.SideEffectType.DATAFLOW_SIDE_EFFECTING
TOKEN = jax.ShapeDtypeStruct((8, 128), F32)


def _in_hbm(a):
    return pltpu.with_memory_space_constraint(a, pltpu.HBM)


CHIPS = [(0, 0), (0, 1), (1, 0), (1, 1)]
N_CHIP = len(CHIPS)


def _chip(px, py):
    return 2 * px + py


def _first_hop(bufs, sends, recvs, waiting):
    (land,) = bufs
    x, y, c = _place()
    me = _slot(x, y, c)
    peers = [(x, y, 1 - c), (1 - x, y, c), (x, 1 - y, c), (1 - x, 1 - y, c)]
    return [pltpu.make_async_remote_copy(
        src_ref=land.at[me], dst_ref=land.at[_slot(*p) if waiting else me],
        send_sem=sends.at[k], recv_sem=recvs.at[k], device_id=p, device_id_type=MESH) for k, p in enumerate(peers)]


def _second_hop(bufs, sends, recvs, waiting):
    (land,) = bufs
    x, y, c = _place()
    return [pltpu.make_async_remote_copy(
        src_ref=land.at[_slot(px, py, c)], dst_ref=land.at[_slot(px, py, 1 - c if waiting else c)],
        send_sem=sends.at[k], recv_sem=recvs.at[k], device_id=(x, y, 1 - c), device_id_type=MESH)
        for k, (px, py) in enumerate([(1 - x, y), (x, 1 - y), (1 - x, 1 - y)])]


def _ring_hop1(bufs, sends, recvs, waiting):
    (land,) = bufs
    x, y, c = _place()
    me = _slot(x, y, c)
    peers = [(1 - x, y, c), (x, 1 - y, c), (x, y, 1 - c)]
    return [pltpu.make_async_remote_copy(
        src_ref=land.at[me], dst_ref=land.at[_slot(*p) if waiting else me],
        send_sem=sends.at[k], recv_sem=recvs.at[k], device_id=p, device_id_type=MESH) for k, p in enumerate(peers)]


def _ring_hop2(bufs, sends, recvs, waiting):
    (land,) = bufs
    x, y, c = _place()
    half = land.shape[1] // 2
    first, second = pl.ds(0, half), pl.ds(half, half)
    nx, ny, diag = _slot(1 - x, y, c), _slot(x, 1 - y, c), _slot(1 - x, 1 - y, c)
    plan = [
        (land.at[ny, first], land.at[diag, first], (1 - x, y, c)),
        (land.at[nx, second], land.at[diag, second], (x, 1 - y, c)),
        (land.at[nx], land.at[_slot(1 - x, y, 1 - c)], (x, y, 1 - c)),
        (land.at[ny], land.at[_slot(x, 1 - y, 1 - c)], (x, y, 1 - c))]
    return [pltpu.make_async_remote_copy(
        src_ref=src, dst_ref=mine if waiting else src, send_sem=sends.at[k], recv_sem=recvs.at[k],
        device_id=to, device_id_type=MESH) for k, (src, mine, to) in enumerate(plan)]


def _ring_hop3(bufs, sends, recvs, waiting):
    (land,) = bufs
    x, y, c = _place()
    return [pltpu.make_async_remote_copy(
        src_ref=land.at[_slot(1 - x, 1 - y, c)], dst_ref=land.at[_slot(1 - x, 1 - y, 1 - c if waiting else c)],
        send_sem=sends.at[0], recv_sem=recvs.at[0], device_id=(x, y, 1 - c), device_id_type=MESH)]


RING_HOPS = [(_ring_hop1, 3), (_ring_hop2, 4), (_ring_hop3, 1)]

def _pair_hop(bufs, sends, recvs, waiting):
    g, land = bufs
    x, y, c = _place()
    return [pltpu.make_async_remote_copy(
        src_ref=g.at[_slot(qx, qy, 1 - c)], dst_ref=land.at[q],
        send_sem=sends.at[q], recv_sem=recvs.at[q], device_id=(x, y, 1 - c), device_id_type=MESH)
        for q, (qx, qy) in enumerate(CHIPS)]


def _chip_hop(bufs, sends, recvs, waiting):
    p, land = bufs
    x, y, c = _place()
    return [pltpu.make_async_remote_copy(
        src_ref=p.at[_chip(px, py)], dst_ref=land.at[_chip(px, py) if waiting else _chip(x, y)],
        send_sem=sends.at[k], recv_sem=recvs.at[k], device_id=(px, py, c), device_id_type=MESH)
        for k, (px, py) in enumerate([(1 - x, y), (x, 1 - y), (1 - x, 1 - y)])]


def copies_start(name, groups, hop, n_copies, deps=()):
    flat = [b for grp in groups for b in grp]
    nb, ng = len(flat), len(groups)
    deps = list(deps)
    hops = list(hop) if isinstance(hop, (list, tuple)) else [hop] * ng
    counts = list(n_copies) if isinstance(n_copies, (list, tuple)) else [n_copies] * ng

    def body(*refs):
        ins, token = refs[:nb], refs[-1]
        sems = refs[nb + len(deps):nb + len(deps) + 2 * ng]
        i = 0
        for gi, grp in enumerate(groups):
            for cp in hops[gi](ins[i:i + len(grp)], sems[2 * gi], sems[2 * gi + 1], False):
                cp.start()
            i += len(grp)
        token[...] = jnp.zeros_like(token)

    outs = pl.pallas_call(
        body, name=name,
        out_shape=([pltpu.SemaphoreType.DMA((n,)) for n in counts for _ in range(2)]
                   + [pltpu.HBM(b.shape, b.dtype) for b in flat] + [TOKEN]),
        in_specs=[HBM] * nb + [ANY] * len(deps),
        out_specs=[SEM] * (2 * ng) + [HBM] * nb + [pl.BlockSpec(memory_space=pltpu.VMEM)],
        input_output_aliases={i: 2 * ng + i for i in range(nb)},
        compiler_params=pltpu.CompilerParams(has_side_effects=EFFECT),
    )(*[_in_hbm(b) for b in flat], *deps)
    started, i = [], 0
    for gi, grp in enumerate(groups):
        started.append((outs[2 * gi], outs[2 * gi + 1], list(outs[2 * ng + i:2 * ng + i + len(grp)])))
        i += len(grp)
    return started, outs[-1]


def copies_wait(name, started, hop, after):
    sends, recvs, bufs = started
    nb = len(bufs)

    def body(*refs):
        for cp in hop(refs[:nb], refs[nb], refs[nb + 1], True):
            cp.wait_send()
            cp.wait_recv()

    outs = pl.pallas_call(
        body, name=name,
        out_shape=[pltpu.HBM(b.shape, b.dtype) for b in bufs],
        in_specs=[HBM] * nb + [SEM, SEM, ANY], out_specs=[HBM] * nb,
        input_output_aliases={i: i for i in range(nb)},
        compiler_params=pltpu.CompilerParams(has_side_effects=EFFECT),
    )(*bufs, sends, recvs, after)
    return list(outs)


def place_shard(name, w, layer, dtype, deps=()):
    _, r, c = w.shape
    tr = _tile(r, 1024)
    x, y, core = _place()
    me = _slot(x, y, core).astype(jnp.int32).reshape(1)

    def body(me_ref, w_ref, *rest):
        rest[-1][...] = w_ref[...].astype(dtype)

    return pl.pallas_call(
        body, name=name,
        grid_spec=pltpu.PrefetchScalarGridSpec(
            num_scalar_prefetch=1, grid=(r // tr,),
            in_specs=[pl.BlockSpec((None, tr, c), lambda i, me_ref: (layer, i, 0))] + [ANY] * len(deps),
            out_specs=pl.BlockSpec((None, tr, c), lambda i, me_ref: (me_ref[0], i, 0))),
        out_shape=jax.ShapeDtypeStruct((N_DEV, r, c), dtype),
        compiler_params=_params(("parallel",)),
    )(me, w, *deps)


def place_sheet(name, pieces, total_of, total_at, rows, width):
    x, y, core = _place()
    me = _slot(x, y, core).astype(jnp.int32).reshape(1)

    def body(me_ref, *refs):
        o_ref = refs[-1]
        o_ref[...] = jnp.zeros_like(o_ref)
        for ref, (_, row, lane) in zip(refs, pieces):
            o_ref[row:row + 1, lane:lane + ref.shape[1]] = jnp.sum(ref[...], axis=0, keepdims=True)
        total = jnp.sum(jnp.sum(refs[len(pieces)][...], axis=0, keepdims=True), axis=1, keepdims=True)
        o_ref[total_at[0]:total_at[0] + 1, total_at[1]:total_at[1] + 128] = jnp.broadcast_to(total, (1, 128))

    arrays = [a for a, _, _ in pieces] + [total_of]
    return pl.pallas_call(
        body, name=name,
        grid_spec=pltpu.PrefetchScalarGridSpec(
            num_scalar_prefetch=1, grid=(1,),
            in_specs=[pl.BlockSpec(a.shape, lambda i, me_ref: (0, 0)) for a in arrays],
            out_specs=pl.BlockSpec((None, rows, width), lambda i, me_ref: (me_ref[0], 0, 0))),
        out_shape=jax.ShapeDtypeStruct((N_DEV, rows, width), F32),
    )(me, *arrays)


def tie(name, x, *deps):
    def body(*refs):
        del refs

    return pl.pallas_call(
        body, name=name, out_shape=jax.ShapeDtypeStruct(x.shape, x.dtype),
        in_specs=[ANY] * (1 + len(deps)), out_specs=ANY, input_output_aliases={0: 0},
    )(x, *deps)


def pair_add(name, g, from_sibling):
    _, r, c_dim = g.shape
    tr = r
    while tr * c_dim > PAIR_ADD_BLOCK and tr % 16 == 0:
        tr //= 2
    x, y, core = _place()
    where = jnp.stack([core, _chip(x, y)]).astype(jnp.int32)

    def body(where_ref, g_ref, s_ref, o_ref, zone_ref):
        total = (g_ref[...].astype(F32) + s_ref[...].astype(F32)).astype(o_ref.dtype)
        o_ref[...] = total

        @pl.when(pl.program_id(1) == where_ref[1])
        def _():
            zone_ref[...] = total

    blk = pl.BlockSpec((None, tr, c_dim), lambda i, q, where_ref: (q, i, 0))
    return pl.pallas_call(
        body, name=name,
        grid_spec=pltpu.PrefetchScalarGridSpec(
            num_scalar_prefetch=1, grid=(r // tr, N_CHIP),
            in_specs=[pl.BlockSpec((None, None, tr, c_dim), lambda i, q, where_ref: (q, where_ref[0], i, 0)), blk],
            out_specs=[blk, pl.BlockSpec((None, tr, c_dim), lambda i, q, where_ref: (where_ref[1], i, 0))]),
        out_shape=[jax.ShapeDtypeStruct((N_CHIP, r, c_dim), g.dtype)] * 2,
        compiler_params=_params(("parallel", "arbitrary")),
    )(where, g.reshape(N_CHIP, 2, r, c_dim), from_sibling)


def _matmul(name, lhs, rhs, *, out_shape, out_dtype, grid, lhs_spec, rhs_spec, out_spec, acc_shape,
            lhs_fn=None, epilogue=None, into=None):
    nk = grid[2]
    extra = [] if into is None else [into]

    def body(lhs_ref, rhs_ref, *rest):
        out_ref, scratch = rest[len(extra)], rest[len(extra) + 1:]

        def product():
            a = lhs_ref[...]
            if lhs_fn is not None:
                a = lhs_fn(a)
            return lax.dot_general(a, rhs_ref[...], NN, preferred_element_type=F32)

        def finish(r):
            if epilogue is not None:
                r = epilogue(r)
            out_ref[...] = r.astype(out_dtype)

        if nk == 1:
            finish(product())
        else:
            (acc_ref,) = scratch
            k = pl.program_id(2)

            @pl.when(k == 0)
            def _():
                acc_ref[...] = product()

            @pl.when(jnp.logical_and(k > 0, k < nk - 1))
            def _():
                acc_ref[...] += product()

            @pl.when(k == nk - 1)
            def _():
                finish(acc_ref[...] + product())

    return pl.pallas_call(
        body, name=name, grid=grid,
        out_shape=jax.ShapeDtypeStruct(out_shape, out_dtype),
        in_specs=[lhs_spec, rhs_spec] + [ANY] * len(extra), out_specs=out_spec,
        input_output_aliases={2: 0} if extra else {},
        scratch_shapes=[pltpu.VMEM(acc_shape, F32)] if nk > 1 else [],
        compiler_params=_params(("parallel", "parallel", "arbitrary")),
    )(lhs, rhs, *extra)


def _tile(n, want):
    return want if n % want == 0 else n


def mm_nn(name, x, w, *, out_dtype, tn=512, tk=None, lhs_fn=None, epilogue=None):
    t, kdim = x.shape
    n = w.shape[1]
    tm, tn = _tile(t, MATMUL_ROWS), _tile(n, tn)
    tk = kdim if tk is None else _tile(kdim, tk)
    return _matmul(
        name, x, w, out_shape=(t, n), out_dtype=out_dtype, grid=(t // tm, n // tn, kdim // tk),
        lhs_spec=pl.BlockSpec((tm, tk), lambda i, j, k: (i, k)),
        rhs_spec=pl.BlockSpec((tk, tn), lambda i, j, k: (k, j)),
        out_spec=pl.BlockSpec((tm, tn), lambda i, j, k: (i, j)),
        acc_shape=(tm, tn), lhs_fn=lhs_fn, epilogue=epilogue)


def mm_nn_blocked(name, x, w, *, out_dtype, epilogue=None, blocks=(0, N_DEV), into=None):
    t, kdim = x.shape
    nb = w.shape[2]
    tm = _tile(t, MATMUL_ROWS)
    tn = nb // 2 if nb >= 1024 else nb
    sub = nb // tn
    first, count = blocks
    return _matmul(
        name, x, w, out_shape=(t, N_DEV * nb), out_dtype=out_dtype, grid=(t // tm, count * sub, 1),
        lhs_spec=pl.BlockSpec((tm, kdim), lambda i, j, k: (i, k)),
        rhs_spec=pl.BlockSpec((None, kdim, tn), lambda i, j, k: (first + j // sub, k, j % sub)),
        out_spec=pl.BlockSpec((tm, tn), lambda i, j, k: (i, first * sub + j)),
        acc_shape=(tm, tn), epilogue=epilogue, into=into)


def mm_bwd_pair(name, dy, w, act, *, out_dtype, tile=512, act_fn=None, epilogue=None):
    t, n = dy.shape
    kdim = w.shape[0]
    tile = _tile(kdim, tile)

    def body(dy_ref, w_ref, act_ref, dx_ref, dw_ref):
        a = act_ref[...]
        dx = lax.dot_general(dy_ref[...], w_ref[...], NT, preferred_element_type=F32)
        if epilogue is not None:
            dx = epilogue(dx, a)
        dx_ref[...] = dx.astype(out_dtype)
        if act_fn is not None:
            a = act_fn(a)
        dw_ref[...] = lax.dot_general(a, dy_ref[...], TN, preferred_element_type=F32).astype(out_dtype)

    return pl.pallas_call(
        body, name=name, grid=(kdim // tile,),
        in_specs=[pl.BlockSpec((t, n), lambda j: (0, 0)), pl.BlockSpec((tile, n), lambda j: (j, 0)),
                  pl.BlockSpec((t, tile), lambda j: (0, j))],
        out_specs=[pl.BlockSpec((t, tile), lambda j: (0, j)), pl.BlockSpec((tile, n), lambda j: (j, 0))],
        out_shape=[jax.ShapeDtypeStruct((t, kdim), out_dtype), jax.ShapeDtypeStruct((kdim, n), out_dtype)],
        compiler_params=_params(("parallel",)),
    )(dy, w, act)


def mm_bwd_pair_blocked(name, dz, w, act, *, out_dtype, tile=1024):
    t = dz.shape[0]
    kdim, nb = w.shape[1], w.shape[2]
    tile = _tile(kdim, tile)

    def body(dz_ref, w_ref, act_ref, dx_ref, dw_ref, acc_ref):
        j = pl.program_id(1)
        dw_ref[...] = lax.dot_general(act_ref[...], dz_ref[...], TN, preferred_element_type=F32).astype(out_dtype)

        def product():
            return lax.dot_general(dz_ref[...], w_ref[...], NT, preferred_element_type=F32)

        @pl.when(j == 0)
        def _():
            acc_ref[...] = product()

        @pl.when(jnp.logical_and(j > 0, j < N_DEV - 1))
        def _():
            acc_ref[...] += product()

        @pl.when(j == N_DEV - 1)
        def _():
            dx_ref[...] = (acc_ref[...] + product()).astype(out_dtype)

    return pl.pallas_call(
        body, name=name, grid=(kdim // tile, N_DEV),
        in_specs=[pl.BlockSpec((t, nb), lambda i, j: (0, j)), pl.BlockSpec((None, tile, nb), lambda i, j: (j, i, 0)),
                  pl.BlockSpec((t, tile), lambda i, j: (0, i))],
        out_specs=[pl.BlockSpec((t, tile), lambda i, j: (0, i)),
                   pl.BlockSpec((None, tile, nb), lambda i, j: (j, i, 0))],
        out_shape=[jax.ShapeDtypeStruct((t, kdim), out_dtype), jax.ShapeDtypeStruct((N_DEV, kdim, nb), out_dtype)],
        scratch_shapes=[pltpu.VMEM((t, tile), F32)],
        compiler_params=_params(("parallel", "arbitrary")),
    )(dz, w, act)


def _rstd(v):
    return lax.rsqrt(jnp.mean(v * v, axis=-1, keepdims=True) + NORM_EPS)


def _rms_bwd(v, g, dy):
    r = _rstd(v)
    vhat = v * r
    dvh = dy * g
    dv = r * (dvh - vhat * jnp.mean(dvh * vhat, axis=-1, keepdims=True))
    return dv, dy * vhat


def _fold8(v):
    rows, n = v.shape
    return jnp.sum(v.reshape(rows // 8, 8, n), axis=0)


def _fold_lanes(v):
    out = v[:, 0:128]
    for i in range(1, v.shape[1] // 128):
        out = out + v[:, 128 * i:128 * (i + 1)]
    return out


def _accumulate(ref, v):
    i = pl.program_id(0)

    @pl.when(i == 0)
    def _():
        ref[...] = v

    @pl.when(i > 0)
    def _():
        ref[...] += v


def _row_call(body, name, t, ins, row_in, outs, acc_outs=(), tr=ROW_TILE):
    tr = _tile(t, tr)

    def in_spec(a, tiled):
        if isinstance(tiled, tuple):
            width, j = tiled
            return pl.BlockSpec((tr, width), lambda i: (i, j))
        return pl.BlockSpec((tr, a.shape[1]), lambda i: (i, 0)) if tiled else pl.BlockSpec(a.shape, lambda i: (0, 0))

    in_specs = [in_spec(a, tiled) for a, tiled in zip(ins, row_in)]
    out_specs = [pl.BlockSpec((tr, n), lambda i: (i, 0)) for n, _ in outs]
    out_specs += [pl.BlockSpec((8, n), lambda i: (0, 0)) for n in acc_outs]
    out_shape = [jax.ShapeDtypeStruct((t, n), dt) for n, dt in outs]
    out_shape += [jax.ShapeDtypeStruct((8, n), F32) for n in acc_outs]
    return pl.pallas_call(
        body, name=name, grid=(t // tr,), in_specs=in_specs, out_specs=out_specs, out_shape=out_shape,
        compiler_params=_params(("arbitrary",) if acc_outs else ("parallel",)),
    )(*ins)


def norm_pre(name, x, g):
    t, d = x.shape

    def body(x_ref, g_ref, h_ref):
        v = x_ref[...]
        h_ref[...] = (v * _rstd(v) * g_ref[...]).astype(BF16)

    return _row_call(body, name, t, [x, g], [True, False], [(d, BF16)])[0]


def post_pre(name, x, m, g_post, g_pre):
    t, d = x.shape

    def body(x_ref, m_ref, gp_ref, gn_ref, xo_ref, h_ref):
        mv = m_ref[...]
        xn = x_ref[...] + mv * _rstd(mv) * gp_ref[...]
        xo_ref[...] = xn
        h_ref[...] = (xn * _rstd(xn) * gn_ref[...]).astype(BF16)

    return _row_call(body, name, t, [x, m, g_post, g_pre], [True, True, False, False], [(d, F32), (d, BF16)])


def post_loss(name, x, f, g_post, target):
    t, d = x.shape

    def body(x_ref, f_ref, g_ref, t_ref, dx_ref, df_ref, loss_ref, dg_ref):
        fv = f_ref[...]
        g = g_ref[...]
        out = x_ref[...] + fv * _rstd(fv) * g
        err = out - t_ref[...]
        dx = err * (1.0 / d)
        dx_ref[...] = dx
        dfv, dg_rows = _rms_bwd(fv, g, dx)
        df_ref[...] = dfv.astype(BF16)
        _accumulate(loss_ref, _fold8(_fold_lanes(err * err)))
        _accumulate(dg_ref, _fold8(dg_rows))

    return _row_call(body, name, t, [x, f, g_post, target], [True, True, False, True],
                     [(d, F32), (d, BF16)], acc_outs=(128, d))


def bwd_pre_post(name, dx_out, x_in, g_pre, dh, f_prev, g_post_prev):
    t, d = x_in.shape

    def body(dxo_ref, x_ref, gpre_ref, dh_ref, f_ref, gpost_ref, dxi_ref, df_ref, dgpre_ref, dgpost_ref):
        dxv, dgpre_rows = _rms_bwd(x_ref[...], gpre_ref[...], dh_ref[...].astype(F32))
        dxi = dxo_ref[...] + dxv
        dxi_ref[...] = dxi
        dfv, dgpost_rows = _rms_bwd(f_ref[...], gpost_ref[...], dxi)
        df_ref[...] = dfv.astype(BF16)
        _accumulate(dgpre_ref, _fold8(dgpre_rows))
        _accumulate(dgpost_ref, _fold8(dgpost_rows))

    return _row_call(body, name, t, [dx_out, x_in, g_pre, dh, f_prev, g_post_prev],
                     [True, True, False, True, True, False], [(d, F32), (d, BF16)], acc_outs=(d, d))


def bwd_pre_final(name, dx_out, x_in, g_pre, dh):
    t, d = x_in.shape

    def body(dxo_ref, x_ref, gpre_ref, dh_ref, dxi_ref, dgpre_ref):
        dxv, dgpre_rows = _rms_bwd(x_ref[...], gpre_ref[...], dh_ref[...].astype(F32))
        dxi_ref[...] = dxo_ref[...] + dxv
        _accumulate(dgpre_ref, _fold8(dgpre_rows))

    return _row_call(body, name, t, [dx_out, x_in, g_pre, dh], [True, True, False, True], [(d, F32)], acc_outs=(d,))


def _layer_norm_parts(cv):
    mu = jnp.mean(cv, axis=-1, keepdims=True)
    xc = cv - mu
    rstd = lax.rsqrt(jnp.mean(xc * xc, axis=-1, keepdims=True) + NORM_EPS)
    return xc * rstd, rstd


def ln_silu(name, cv, g, b, y, y_block):
    t, n = cv.shape
    tr = _tile(t, ROW_TILE)

    def body(c_ref, g_ref, b_ref, y_in_ref, y_ref):
        chat, _ = _layer_norm_parts(c_ref[...])
        ln = chat * g_ref[...] + b_ref[...]
        y_ref[...] = (ln * jax.nn.sigmoid(ln)).astype(BF16)

    vec = pl.BlockSpec((1, n), lambda i: (0, 0))
    return pl.pallas_call(
        body, name=name, grid=(t // tr,),
        in_specs=[pl.BlockSpec((tr, n), lambda i: (i, 0)), vec, vec, ANY],
        out_specs=pl.BlockSpec((tr, n), lambda i: (i, y_block)),
        out_shape=jax.ShapeDtypeStruct(y.shape, y.dtype), input_output_aliases={3: 0},
        compiler_params=_params(("parallel",)),
    )(cv, g, b, y)


def ln_silu_bwd(name, cv, g, b, dy, dy_block):
    t, n = cv.shape

    def body(c_ref, g_ref, b_ref, dy_ref, dc_ref, dg_ref, db_ref):
        chat, rstd = _layer_norm_parts(c_ref[...])
        g = g_ref[...]
        ln = chat * g + b_ref[...]
        s = jax.nn.sigmoid(ln)
        dln = dy_ref[...].astype(F32) * (s * (1.0 + ln * (1.0 - s)))
        dchat = dln * g
        dc_ref[...] = rstd * (dchat - jnp.mean(dchat, axis=-1, keepdims=True)
                              - chat * jnp.mean(dchat * chat, axis=-1, keepdims=True))
        _accumulate(dg_ref, _fold8(dln * chat))
        _accumulate(db_ref, _fold8(dln))

    return _row_call(body, name, t, [cv, g, b, dy], [True, False, False, (n, dy_block)], [(n, F32)], acc_outs=(n, n))


def _chunks(t, fn, tc=TIME_CHUNK):
    tc = _tile(t, tc)

    def step(i, carry):
        fn(pl.multiple_of(i * tc, tc), tc)
        return carry

    lax.fori_loop(0, t // tc, step, 0)


def _rows_from(v, start, n):
    res = start % 8
    base = v if res == 0 else pltpu.roll(v, v.shape[0] - res, axis=0)
    return base[start - res:start - res + n, :]


def _shifted(window, offsets, tc):
    rows = window.shape[0]
    by_residue = {}
    for k, off in enumerate(offsets):
        by_residue.setdefault(off % 8, []).append((k, off))
    for res, taps in by_residue.items():
        base = window if res == 0 else pltpu.roll(window, rows - res, axis=0)
        for k, off in taps:
            yield k, base[off - res:off - res + tc, :]


def _taps(window, w_ref, offsets, tc, flip=False):
    acc = None
    for k, rows in _shifted(window, offsets, tc):
        kk = len(offsets) - 1 - k if flip else k
        term = w_ref[kk:kk + 1, :] * rows
        acc = term if acc is None else acc + term
    return acc


def _window_sums(win, tc, causal):
    sums = []
    cur, rows, step = win, tc + HALO, 1
    for _ in POOL_WINDOWS:
        rows -= 8
        if causal:
            cur = cur[8:8 + rows, :] + _rows_from(cur, 8 - step, rows)
            sums.append(cur[rows - tc:rows, :])
        else:
            cur = cur[0:rows, :] + _rows_from(cur, step, rows)
            sums.append(cur[0:tc, :])
        step *= 2
    return sums


def _pick(vals, g):
    out = vals[-1]
    for i in range(len(vals) - 2, -1, -1):
        out = jnp.where(g == i, vals[i], out)
    return out


def _pool_count(s, tc, g):
    t1 = (lax.broadcasted_iota(jnp.int32, (tc, 1), 0) + (s + 1)).astype(F32)
    width = _pick([float(w) for w in POOL_WINDOWS], g)
    return jnp.minimum(t1, width)


def pool_fwd(name, z, pool_w, pool_scale, d_pool, y_width):
    t = z.shape[0]
    ng, pg = pool_w.shape[0], pool_w.shape[1]

    def body(u_ref, w_ref, s_ref, pooled_ref, y_ref, pad):
        g = pl.program_id(0)
        pad[pl.ds(0, HALO), :] = jnp.zeros((HALO, pg), F32)

        def fill(s, tc):
            pad[pl.ds(HALO + s, tc), :] = u_ref[pl.ds(s, tc), :].astype(F32)

        def chunk(s, tc):
            win = pad[pl.ds(s, tc + HALO), :]
            total = _pick(_window_sums(win, tc, causal=True), g)
            pooled = total / _pool_count(s, tc, g) - win[HALO:HALO + tc, :]
            pooled_ref[pl.ds(s, tc), :] = pooled.astype(BF16)

        _chunks(t, fill)
        _chunks(t, chunk)
        mixed = jnp.dot(pooled_ref[...], w_ref[...], preferred_element_type=F32)
        y_ref[...] = (mixed * s_ref[...]).astype(BF16)

    col = pl.BlockSpec((t, pg), lambda g: (0, g))
    return pl.pallas_call(
        body, name=name, grid=(ng,),
        in_specs=[col, pl.BlockSpec((None, pg, pg), lambda g: (g, 0, 0)), pl.BlockSpec((1, pg), lambda g: (0, g))],
        out_specs=[col, col],
        out_shape=[jax.ShapeDtypeStruct((t, d_pool), BF16), jax.ShapeDtypeStruct((t, y_width), BF16)],
        scratch_shapes=[pltpu.VMEM((t + HALO, pg), F32)],
        compiler_params=_params(("parallel",)),
    )(z, pool_w, pool_scale)


def pool_bwd(name, pooled, dy, pool_w, pool_scale, dz):
    t, d_pool = pooled.shape
    ng, pg = pool_w.shape[0], pool_w.shape[1]

    def body(p_ref, dy_ref, w_ref, s_ref, dz_ref, du_ref, dw_ref, ds_ref, pad):
        g = pl.program_id(0)
        w = w_ref[...]
        dyv = dy_ref[...].astype(F32)
        mixed = jnp.dot(p_ref[...], w, preferred_element_type=F32)
        ds_ref[...] = jnp.sum(dyv * mixed, axis=0, keepdims=True)
        dmixed = (dyv * s_ref[...]).astype(BF16)
        dw_ref[...] = lax.dot_general(p_ref[...], dmixed, TN, preferred_element_type=F32)
        pad[...] = jnp.zeros((t + HALO, pg), F32)
        pad[pl.ds(0, t), :] = lax.dot_general(dmixed, w, NT, preferred_element_type=F32)

        def scale(s, tc):
            pad[pl.ds(s, tc), :] = pad[pl.ds(s, tc), :] / _pool_count(s, tc, g)

        def chunk(s, tc):
            win = pad[pl.ds(s, tc + HALO), :]
            total = _pick(_window_sums(win, tc, causal=False), g)
            du_ref[pl.ds(s, tc), :] = (total - win[0:tc, :] * _pool_count(s, tc, g)).astype(BF16)

        _chunks(t, scale)
        _chunks(t, chunk)

    col = pl.BlockSpec((t, pg), lambda g: (0, g))
    vec = pl.BlockSpec((1, pg), lambda g: (0, g))
    mat = pl.BlockSpec((None, pg, pg), lambda g: (g, 0, 0))
    return pl.pallas_call(
        body, name=name, grid=(ng,),
        in_specs=[col, col, mat, vec, ANY], out_specs=[col, mat, vec],
        out_shape=[jax.ShapeDtypeStruct(dz.shape, dz.dtype), jax.ShapeDtypeStruct((ng, pg, pg), F32),
                   jax.ShapeDtypeStruct((1, d_pool), F32)],
        input_output_aliases={4: 0},
        scratch_shapes=[pltpu.VMEM((t + HALO, pg), F32)],
        compiler_params=_params(("parallel",)),
    )(pooled, dy, pool_w, pool_scale, dz)


def conv_fwd(name, z, conv_w, conv_b, d_pool, d_conv):
    t = z.shape[0]
    kw = conv_w.shape[0]
    tc_ch = _tile(d_conv, CHANNEL_TILE)
    v0, g0 = d_pool // tc_ch, (d_pool + d_conv) // tc_ch

    def body(v_ref, g_ref, w_ref, b_ref, c_ref, pad):
        pad[pl.ds(0, HALO), :] = jnp.zeros((HALO, tc_ch), F32)

        def fill(s, tc):
            pad[pl.ds(HALO + s, tc), :] = v_ref[pl.ds(s, tc), :].astype(F32) * jax.nn.sigmoid(g_ref[pl.ds(s, tc), :].astype(F32))

        def chunk(s, tc):
            win = pad[pl.ds(s, tc + HALO), :]
            c_ref[pl.ds(s, tc), :] = _taps(win, w_ref, [HALO - (kw - 1) + k for k in range(kw)], tc) + b_ref[...]

        _chunks(t, fill)
        _chunks(t, chunk)

    return pl.pallas_call(
        body, name=name, grid=(d_conv // tc_ch,),
        in_specs=[pl.BlockSpec((t, tc_ch), lambda j: (0, v0 + j)), pl.BlockSpec((t, tc_ch), lambda j: (0, g0 + j)),
                  pl.BlockSpec((kw, tc_ch), lambda j: (0, j)), pl.BlockSpec((1, tc_ch), lambda j: (0, j))],
        out_specs=pl.BlockSpec((t, tc_ch), lambda j: (0, j)),
        out_shape=jax.ShapeDtypeStruct((t, d_conv), F32),
        scratch_shapes=[pltpu.VMEM((t + HALO, tc_ch), F32)],
        compiler_params=_params(("parallel",)),
    )(z, z, conv_w, conv_b)


def conv_bwd(name, z, dc, conv_w, d_pool, d_conv):
    t = z.shape[0]
    kw = conv_w.shape[0]
    tc_ch = _tile(d_conv, CHANNEL_TILE)
    v0, g0 = d_pool // tc_ch, (d_pool + d_conv) // tc_ch

    def body(v_ref, g_ref, dc_ref, w_ref, dz_ref, dw_ref, db_ref, pad_a, pad_dc, acc_w, acc_b, tiles, sems):
        j = pl.program_id(0)
        dv_ref, dg_ref = tiles.at[0], tiles.at[1]
        writes = [pltpu.make_async_copy(tiles.at[p], dz_ref.at[:, pl.ds((first + j) * tc_ch, tc_ch)], sems.at[p])
                  for p, first in enumerate([v0, g0])]

        def wait_writes():
            for cp in writes:
                cp.wait()

        pad_a[pl.ds(0, HALO), :] = jnp.zeros((HALO, tc_ch), F32)
        pad_dc[pl.ds(t, HALO), :] = jnp.zeros((HALO, tc_ch), F32)
        acc_w[...] = jnp.zeros_like(acc_w)
        acc_b[...] = jnp.zeros_like(acc_b)

        def fill(s, tc):
            pad_a[pl.ds(HALO + s, tc), :] = v_ref[pl.ds(s, tc), :].astype(F32) * jax.nn.sigmoid(g_ref[pl.ds(s, tc), :].astype(F32))
            pad_dc[pl.ds(s, tc), :] = dc_ref[pl.ds(s, tc), :]

        def chunk(s, tc):
            dcv = pad_dc[pl.ds(s, tc), :]
            win_a = pad_a[pl.ds(s, tc + HALO), :]
            for k, rows in _shifted(win_a, [HALO - (kw - 1) + k for k in range(kw)], tc):
                acc_w[pl.ds(8 * k, 8), :] += _fold8(dcv * rows)
            acc_b[...] += _fold8(dcv)
            da = _taps(pad_dc[pl.ds(s, tc + HALO), :], w_ref, list(range(kw)), tc, flip=True)
            vv = v_ref[pl.ds(s, tc), :].astype(F32)
            sg = jax.nn.sigmoid(g_ref[pl.ds(s, tc), :].astype(F32))
            dv_ref[pl.ds(s, tc), :] = (da * sg).astype(BF16)
            dg_ref[pl.ds(s, tc), :] = (da * vv * sg * (1.0 - sg)).astype(BF16)

        _chunks(t, fill)
        pl.when(j > 0)(wait_writes)
        _chunks(t, chunk)
        for cp in writes:
            cp.start()
        pl.when(j == n_tiles - 1)(wait_writes)
        for k in range(kw):
            dw_ref[k:k + 1, :] = jnp.sum(acc_w[pl.ds(8 * k, 8), :], axis=0, keepdims=True)
        db_ref[...] = jnp.sum(acc_b[...], axis=0, keepdims=True)

    n_tiles = d_conv // tc_ch
    return pl.pallas_call(
        body, name=name, grid=(n_tiles,),
        in_specs=[pl.BlockSpec((t, tc_ch), lambda j: (0, v0 + j)), pl.BlockSpec((t, tc_ch), lambda j: (0, g0 + j)),
                  pl.BlockSpec((t, tc_ch), lambda j: (0, j)), pl.BlockSpec((kw, tc_ch), lambda j: (0, j))],
        out_specs=[ANY, pl.BlockSpec((kw, tc_ch), lambda j: (0, j)), pl.BlockSpec((1, tc_ch), lambda j: (0, j))],
        out_shape=[jax.ShapeDtypeStruct((t, d_pool + 2 * d_conv), BF16),
                   jax.ShapeDtypeStruct((kw, d_conv), F32), jax.ShapeDtypeStruct((1, d_conv), F32)],
        scratch_shapes=[pltpu.VMEM((t + HALO, tc_ch), F32), pltpu.VMEM((t + HALO, tc_ch), F32),
                        pltpu.VMEM((8 * kw, tc_ch), F32), pltpu.VMEM((8, tc_ch), F32),
                        pltpu.VMEM((2, t, tc_ch), BF16), pltpu.SemaphoreType.DMA((2,))],
        compiler_params=_params(("arbitrary",)),
    )(z, z, dc, conv_w)


def short_fwd(name, z, conv_w, d_short):
    t = z.shape[0]
    kw = conv_w.shape[0]
    tc_ch = _tile(d_short, CHANNEL_TILE)
    nt = d_short // tc_ch

    def body(b_ref, c_ref, u_ref, w_ref, y_ref, pad):
        pad[pl.ds(0, HALO), :] = jnp.zeros((HALO, tc_ch), F32)

        def fill(s, tc):
            pad[pl.ds(HALO + s, tc), :] = c_ref[pl.ds(s, tc), :].astype(F32) * u_ref[pl.ds(s, tc), :].astype(F32)

        def chunk(s, tc):
            win = pad[pl.ds(s, tc + HALO), :]
            cq = _taps(win, w_ref, [HALO - (kw - 1) + k for k in range(kw)], tc)
            y_ref[pl.ds(s, tc), :] = (b_ref[pl.ds(s, tc), :].astype(F32) * cq).astype(BF16)

        _chunks(t, fill)
        _chunks(t, chunk)

    return pl.pallas_call(
        body, name=name, grid=(nt,),
        in_specs=[pl.BlockSpec((t, tc_ch), lambda j: (0, j)), pl.BlockSpec((t, tc_ch), lambda j: (0, nt + j)),
                  pl.BlockSpec((t, tc_ch), lambda j: (0, 2 * nt + j)), pl.BlockSpec((kw, tc_ch), lambda j: (0, j))],
        out_specs=pl.BlockSpec((t, tc_ch), lambda j: (0, j)),
        out_shape=jax.ShapeDtypeStruct((t, d_short), BF16),
        scratch_shapes=[pltpu.VMEM((t + HALO, tc_ch), F32)],
        compiler_params=_params(("parallel",)),
    )(z, z, z, conv_w)


def short_bwd(name, z, dy, conv_w, d_short):
    t = z.shape[0]
    kw = conv_w.shape[0]
    tc_ch = _tile(d_short, CHANNEL_TILE)
    nt = d_short // tc_ch

    def body(b_ref, c_ref, u_ref, dy_ref, w_ref, dz_ref, dw_ref, pad_q, pad_dcq, acc_w, tiles, sems):
        j = pl.program_id(0)
        db_ref, dcg_ref, du_ref = tiles.at[0], tiles.at[1], tiles.at[2]
        writes = [pltpu.make_async_copy(tiles.at[p], dz_ref.at[:, pl.ds((p * nt + j) * tc_ch, tc_ch)], sems.at[p])
                  for p in range(3)]

        def wait_writes():
            for cp in writes:
                cp.wait()

        pad_q[pl.ds(0, HALO), :] = jnp.zeros((HALO, tc_ch), F32)
        pad_dcq[pl.ds(t, HALO), :] = jnp.zeros((HALO, tc_ch), F32)
        acc_w[...] = jnp.zeros_like(acc_w)

        def fill(s, tc):
            rows = pl.ds(s, tc)
            pad_q[pl.ds(HALO + s, tc), :] = c_ref[rows, :].astype(F32) * u_ref[rows, :].astype(F32)
            pad_dcq[rows, :] = dy_ref[rows, :].astype(F32) * b_ref[rows, :].astype(F32)

        def chunk(s, tc):
            rows = pl.ds(s, tc)
            win_q = pad_q[pl.ds(s, tc + HALO), :]
            dcq = pad_dcq[rows, :]
            cq = None
            for k, shifted in _shifted(win_q, [HALO - (kw - 1) + k for k in range(kw)], tc):
                acc_w[pl.ds(8 * k, 8), :] += _fold8(dcq * shifted)
                term = w_ref[k:k + 1, :] * shifted
                cq = term if cq is None else cq + term
            db_ref[rows, :] = (dy_ref[rows, :].astype(F32) * cq).astype(BF16)
            dq = _taps(pad_dcq[pl.ds(s, tc + HALO), :], w_ref, list(range(kw)), tc, flip=True)
            dcg_ref[rows, :] = (dq * u_ref[rows, :].astype(F32)).astype(BF16)
            du_ref[rows, :] = (dq * c_ref[rows, :].astype(F32)).astype(BF16)

        _chunks(t, fill)
        pl.when(j > 0)(wait_writes)
        _chunks(t, chunk)
        for cp in writes:
            cp.start()
        pl.when(j == nt - 1)(wait_writes)
        for k in range(kw):
            dw_ref[k:k + 1, :] = jnp.sum(acc_w[pl.ds(8 * k, 8), :], axis=0, keepdims=True)

    zspec = [pl.BlockSpec((t, tc_ch), lambda j, o=o: (0, o * nt + j)) for o in range(3)]
    return pl.pallas_call(
        body, name=name, grid=(nt,),
        in_specs=[*zspec, pl.BlockSpec((t, tc_ch), lambda j: (0, j)), pl.BlockSpec((kw, tc_ch), lambda j: (0, j))],
        out_specs=[ANY, pl.BlockSpec((kw, tc_ch), lambda j: (0, j))],
        out_shape=[jax.ShapeDtypeStruct((t, 3 * d_short), BF16), jax.ShapeDtypeStruct((kw, d_short), F32)],
        scratch_shapes=[pltpu.VMEM((t + HALO, tc_ch), F32), pltpu.VMEM((t + HALO, tc_ch), F32),
                        pltpu.VMEM((8 * kw, tc_ch), F32), pltpu.VMEM((3, t, tc_ch), BF16),
                        pltpu.SemaphoreType.DMA((3,))],
        compiler_params=_params(("arbitrary",)),
    )(z, z, z, dy, conv_w)


def _adamw_update(w, m, v, g):
    nm = ADAM_B1 * m + (1.0 - ADAM_B1) * g
    nv = ADAM_B2 * v + (1.0 - ADAM_B2) * (g * g)
    m_hat = nm / (1.0 - ADAM_B1 ** ADAM_STEP)
    v_hat = nv / (1.0 - ADAM_B2 ** ADAM_STEP)
    return -ADAM_LR * (m_hat / (jnp.sqrt(v_hat) + ADAM_EPS) + ADAM_WD * w), nm, nv


def adamw_replicated(name, params, first_moments, second_moments, contributions, layout, scalar_at):
    n = len(params)
    n_slots = contributions.shape[0]

    def total(c_ref, row, lane, rows, lanes):
        acc = c_ref[0, row:row + rows, lane:lane + lanes]
        for slot in range(1, n_slots):
            acc = acc + c_ref[slot, row:row + rows, lane:lane + lanes]
        return acc

    def body(*refs):
        ws, ms, vs, c_ref = refs[:n], refs[n:2 * n], refs[2 * n:3 * n], refs[3 * n]
        outs = refs[3 * n + 1:]
        outs[0][...] = total(c_ref, *scalar_at, 1, 128)
        for i, (row, lane) in enumerate(layout):
            g = total(c_ref, row, lane, *params[i].shape)
            grad_ref, delta_ref, nm_ref, nv_ref = outs[1 + 4 * i:5 + 4 * i]
            grad_ref[...] = g
            delta_ref[...], nm_ref[...], nv_ref[...] = _adamw_update(ws[i][...], ms[i][...], vs[i][...], g)

    out_shape = [jax.ShapeDtypeStruct((1, 128), F32)]
    for p in params:
        out_shape += [jax.ShapeDtypeStruct(p.shape, F32)] * 4
    return pl.pallas_call(body, name=name, out_shape=out_shape)(*params, *first_moments, *second_moments, contributions)


def adamw(name, w, m, v, contributions):
    r, c = w.shape
    nc = len(contributions)
    n_slots = contributions[0].shape[0]
    tr = 256 if c <= 1024 else 128
    if any(a.shape[1] % tr for a in contributions):
        assert nc == 1
        tr = r
    tiles = [a.shape[1] // tr for a in contributions]
    first = [sum(tiles[:j]) for j in range(nc)]

    def body(w_ref, m_ref, v_ref, *rest):
        g_refs, (grad_ref, delta_ref, nm_ref, nv_ref) = rest[:nc], rest[nc:]
        i = pl.program_id(0)
        g = None
        for j, g_ref in enumerate(g_refs):
            s = g_ref[0].astype(F32)
            for slot in range(1, n_slots):
                s = s + g_ref[slot].astype(F32)
            g = s if g is None else jnp.where(i >= first[j], s, g)
        grad_ref[...] = g
        delta_ref[...], nm_ref[...], nv_ref[...] = _adamw_update(w_ref[...], m_ref[...], v_ref[...], g)

    blk = pl.BlockSpec((tr, c), lambda i: (i, 0))
    g_specs = [pl.BlockSpec((n_slots, tr, c), lambda i, j=j: (0, jnp.clip(i - first[j], 0, tiles[j] - 1), 0))
               for j in range(nc)]
    return pl.pallas_call(
        body, name=name, grid=(r // tr,),
        in_specs=[blk, blk, blk, *g_specs],
        out_specs=[blk] * 4, out_shape=[jax.ShapeDtypeStruct((r, c), F32)] * 4,
        compiler_params=_params(("parallel",)),
    )(w, m, v, *contributions)


def _pad_rows(a, rows):
    return jnp.pad(a, ((0, rows - a.shape[0]), (0, 0)))


def kernel(x, mix_pre_g, mix_post_g, ffn_pre_g, ffn_post_g, ab_w_in, pool_w, pool_scale, conv_w, conv_b, conv_ln_g, conv_ln_b, ab_w_out, sc_w_in, sc_conv_w, sc_w_out, ffn_w1, ffn_w2, loss_target, m_mix_pre_g, m_mix_post_g, m_ffn_pre_g, m_ffn_post_g, m_ab_w_in, m_pool_w, m_pool_scale, m_conv_w, m_conv_b, m_conv_ln_g, m_conv_ln_b, m_ab_w_out, m_sc_w_in, m_sc_conv_w, m_sc_w_out, m_ffn_w1, m_ffn_w2, v_mix_pre_g, v_mix_post_g, v_ffn_pre_g, v_ffn_post_g, v_ab_w_in, v_pool_w, v_pool_scale, v_conv_w, v_conv_b, v_conv_ln_g, v_conv_ln_b, v_ab_w_out, v_sc_w_in, v_sc_conv_w, v_sc_w_out, v_ffn_w1, v_ffn_w2):
    t, d = x.shape[1], x.shape[2]
    d_pool = pool_scale.shape[1]
    d_conv = conv_b.shape[1]
    d_short = d
    ng, pg = pool_w.shape[1], pool_w.shape[3]
    kw, ks = conv_w.shape[1], sc_conv_w.shape[1]
    nb_ab, nb_sc, nb_ff = ab_w_in.shape[2], sc_w_in.shape[2], ffn_w1.shape[2]

    xs = x[0]
    target = loss_target[0]

    lanes = min(128, d_conv // N_DEV)
    small_rows = [kw * (d_conv // N_DEV) // lanes, ks * (d_short // N_DEV) // lanes, ng * (pg // N_DEV) * pg // lanes]
    small_total = -(-sum(small_rows) // 8) * 8
    r0, r1, r2 = small_rows[0], small_rows[0] + small_rows[1], sum(small_rows)

    def pack_small(a_conv, a_sconv, a_pool):
        parts = [a_conv[0].reshape(-1, lanes), a_sconv[0].reshape(-1, lanes), a_pool[0].reshape(-1, lanes)]
        return _pad_rows(jnp.concatenate(parts, axis=0), small_total)

    shards = {
        "ab_in": (ab_w_in, 0, BF16), "small": (pack_small(conv_w, sc_conv_w, pool_w)[None], 0, F32),
        "ab_out": (ab_w_out, 0, BF16), "ff1_0": (ffn_w1, 0, BF16), "ff2_0": (ffn_w2, 0, BF16),
        "sc_in": (sc_w_in, 0, BF16), "sc_out": (sc_w_out, 0, BF16),
        "ff1_1": (ffn_w1, 1, BF16), "ff2_1": (ffn_w2, 1, BF16)}
    direct = ["ab_in", "small", "ab_out"]
    zones = {nm: place_shard("place_" + nm, *shards[nm]) for nm in direct}
    started, token = copies_start("gather_start", [[zones[nm]] for nm in direct], _first_hop, 4)
    started = dict(zip(direct, started))
    zones["ff1_0"] = place_shard("place_ff1_0", *shards["ff1_0"], deps=[token])
    (head,), token = copies_start("ring_start_ff1_0", [[zones["ff1_0"]]], _ring_hop1, 3, deps=[token])
    ring = {"ff1_0": head}

    ties = [0]

    def after(v, *deps):
        ties[0] += 1
        return tie(f"tie_{ties[0]}", v, *deps)

    def fetch_begin(nm, dep):
        (zone,) = copies_wait("gather_wait_" + nm, started[nm], _first_hop, dep)
        (hop,), tok = copies_start("forward_start_" + nm, [[zone]], _second_hop, 3)
        return hop, tok

    def fetch_end(nm, hop, dep):
        return copies_wait("forward_wait_" + nm, hop, _second_hop, dep)[0]

    def ring_step(tag, dep, *starts, place=()):
        place = list(place) + [nm for n, nm in starts if n == 1 and nm not in zones and nm not in place]
        before_waits = dep
        for nm in place:
            before_waits = zones[nm] = place_shard("place_" + nm, *shards[nm], deps=[before_waits])
        names, groups, hops, counts = [], [], [], []
        for n, nm in starts:
            if n == 1:
                groups.append([zones[nm]])
            elif n == 2:
                groups.append(copies_wait("ring1_wait_" + nm, ring[nm], _ring_hop1, before_waits))
            else:
                groups.append(copies_wait("ring2_wait_" + nm, ring[nm], _ring_hop2, before_waits))
            hop, n_copies = RING_HOPS[n - 1]
            names, hops, counts = names + [nm], hops + [hop], counts + [n_copies]
        begun, tok = copies_start("ring_start_" + tag, groups, hops, counts, deps=[dep])
        ring.update(zip(names, begun))
        return tok

    def ring_done(nm, dep):
        return copies_wait("ring3_wait_" + nm, ring[nm], _ring_hop3, dep)[0]

    relu = lambda r: jnp.maximum(r, 0.0)
    square = lambda a: a * a
    relu2_bwd = lambda r, a: r * (2.0 * a.astype(F32))

    def row(vec, l):
        return vec[l:l + 1]

    hop_small, _ = fetch_begin("small", token)
    hop_ab_in, tok = fetch_begin("ab_in", token)
    w_small = fetch_end("small", hop_small, tok)
    w_ab_in = fetch_end("ab_in", hop_ab_in, tok)
    w_conv = w_small[:, :r0].reshape(N_DEV, kw, -1).transpose(1, 0, 2).reshape(kw, d_conv)
    w_sconv = w_small[:, r0:r1].reshape(N_DEV, ks, -1).transpose(1, 0, 2).reshape(ks, d_short)
    w_pool = w_small[:, r1:r2].reshape(N_DEV, ng, -1, pg).transpose(1, 0, 2, 3).reshape(ng, pg, pg).astype(BF16)
    h0 = norm_pre("norm_pre", xs, after(row(mix_pre_g, 0), token))
    z0 = mm_nn_blocked("ab_in", h0, w_ab_in, out_dtype=BF16)
    hop, tok = fetch_begin("ab_out", z0)
    z0 = after(z0, tok)
    pooled, y0 = pool_fwd("pool_fwd", z0, w_pool, pool_scale, d_pool, d_pool + d_conv)
    cv = conv_fwd("conv_fwd", z0, w_conv, conv_b, d_pool, d_conv)
    y0 = ln_silu("ln_silu", cv, conv_ln_g, conv_ln_b, y0, d_pool // d_conv)
    w_ab_out = fetch_end("ab_out", hop, y0)

    def ffn_up(name, h, w, *starts):
        a = mm_nn_blocked(name, h, w, out_dtype=BF16, epilogue=relu, blocks=(0, UP_FIRST_BLOCKS))
        tok = ring_step(name, a, *starts)
        return mm_nn_blocked(name + "_rest", h, w, out_dtype=BF16, epilogue=relu,
                             blocks=(UP_FIRST_BLOCKS, N_DEV - UP_FIRST_BLOCKS), into=after(a, tok))

    tok = ring_step("a", w_ab_out, (2, "ff1_0"), (1, "ff2_0"))
    y0 = after(y0, tok)
    m0 = mm_nn("ab_out", y0, w_ab_out.reshape(d_pool + d_conv, d), out_dtype=F32)
    x1, h1 = post_pre("post_pre_0", xs, m0, row(mix_post_g, 0), row(ffn_pre_g, 0))
    tok = ring_step("b", h1, (3, "ff1_0"))
    w_ff1_0 = ring_done("ff1_0", tok)
    a0 = ffn_up("ffn0_up", h1, w_ff1_0, (2, "ff2_0"), (1, "sc_in"), (1, "sc_out"))
    tok = ring_step("c", a0, (3, "ff2_0"), place=["ff1_1"])
    w_ff2_0 = ring_done("ff2_0", tok).reshape(-1, d)
    f0 = mm_nn("ffn0_down", a0, w_ff2_0, out_dtype=F32, tk=2048, lhs_fn=square)
    tok = ring_step("d", f0, (2, "sc_in"), (2, "sc_out"), (1, "ff1_1"))
    f0 = after(f0, tok)
    x2, h2 = post_pre("post_pre_1", x1, f0, row(ffn_post_g, 0), row(mix_pre_g, 1))
    tok = ring_step("e", h2, (3, "sc_in"), place=["ff2_1"])
    w_sc_in = ring_done("sc_in", tok)
    z1 = mm_nn_blocked("sc_in", h2, w_sc_in, out_dtype=BF16)
    y1 = short_fwd("short_fwd", z1, w_sconv, d_short)
    tok = ring_step("f", y1, (3, "sc_out"), (2, "ff1_1"), (1, "ff2_1"))
    w_sc_out = ring_done("sc_out", tok).reshape(d_short, d)
    m1 = mm_nn("sc_out", y1, w_sc_out, out_dtype=F32)
    x3, h3 = post_pre("post_pre_2", x2, m1, row(mix_post_g, 1), row(ffn_pre_g, 1))
    tok = ring_step("g", h3, (3, "ff1_1"))
    w_ff1_1 = ring_done("ff1_1", tok)
    a1 = ffn_up("ffn1_up", h3, w_ff1_1, (2, "ff2_1"))
    tok = ring_step("h", a1, (3, "ff2_1"))
    w_ff2_1 = ring_done("ff2_1", tok).reshape(-1, d)
    f1 = mm_nn("ffn1_down", a1, w_ff2_1, out_dtype=F32, tk=2048, lhs_fn=square)
    dx4, df1, loss_part, dg_ffn_post1 = post_loss("post_loss", x3, f1, row(ffn_post_g, 1), target)

    red = {}

    def reduce_step(dep, begin=None, middle=None):
        tags, groups, hops, counts = [], [], [], []
        if begin is not None:
            tag, g = begin
            tags, groups = tags + [tag], groups + [[g, lax.empty((N_CHIP,) + g.shape[1:], g.dtype)]]
            hops, counts = hops + [_pair_hop], counts + [N_CHIP]
        if middle is not None:
            g, from_sibling = copies_wait("pair_wait_" + middle, red[middle], _pair_hop, dep)
            tags, groups = tags + [middle], groups + [list(pair_add("pair_add_" + middle, g, from_sibling))]
            hops, counts = hops + [_chip_hop], counts + [3]
        begun, tok = copies_start("reduce_start_" + "_".join(tags), groups, hops, counts, deps=[dep])
        red.update(zip(tags, begun))
        return tok

    def reduce_end(tag, dep):
        return copies_wait("chips_wait_" + tag, red[tag], _chip_hop, dep)[1]

    dpre, dw = mm_bwd_pair("ffn1_da_dw2", df1, w_ff2_1, a1, out_dtype=BF16, act_fn=square, epilogue=relu2_bwd)
    dpre = after(dpre, reduce_step(dpre, begin=("ff2_1", dw.reshape(N_DEV, -1, d))))
    dh3, dw = mm_bwd_pair_blocked("ffn1_dh_dw1", dpre, w_ff1_1, h3, out_dtype=BF16)
    dx3, dm1, dg_ffn_pre1, dg_mix_post1 = bwd_pre_post("bwd_3", dx4, x3, row(ffn_pre_g, 1), dh3, m1, row(mix_post_g, 1))
    dm1 = after(dm1, reduce_step(dm1, begin=("ff1_1", dw), middle="ff2_1"))

    dy1, dw = mm_bwd_pair("sc_dy_dwout", dm1, w_sc_out, y1, out_dtype=BF16)
    dy1 = after(dy1, reduce_step(dy1, begin=("sc_out", dw.reshape(N_DEV, -1, d)), middle="ff1_1"))
    dz1, dw_sconv = short_bwd("short_bwd", z1, dy1, w_sconv, d_short)
    dh2, dw = mm_bwd_pair_blocked("sc_dh_dwin", dz1, w_sc_in, h2, out_dtype=BF16)
    dx2, df0, dg_mix_pre1, dg_ffn_post0 = bwd_pre_post("bwd_2", dx3, x2, row(mix_pre_g, 1), dh2, f0, row(ffn_post_g, 0))
    df0 = after(df0, reduce_step(df0, begin=("sc_in", dw), middle="sc_out"))

    dpre, dw = mm_bwd_pair("ffn0_da_dw2", df0, w_ff2_0, a0, out_dtype=BF16, act_fn=square, epilogue=relu2_bwd)
    dpre = after(dpre, reduce_step(dpre, begin=("ff2_0", dw.reshape(N_DEV, -1, d)), middle="sc_in"))
    dh1, dw = mm_bwd_pair_blocked("ffn0_dh_dw1", dpre, w_ff1_0, h1, out_dtype=BF16)
    dh1 = after(dh1, reduce_step(dh1, middle="ff2_0"))
    dx1, dm0, dg_ffn_pre0, dg_mix_post0 = bwd_pre_post("bwd_1", dx2, x1, row(ffn_pre_g, 0), dh1, m0, row(mix_post_g, 0))
    dm0 = after(dm0, reduce_step(dm0, begin=("ff1_0", dw)))

    dy0, dw = mm_bwd_pair("ab_dy_dwout", dm0, w_ab_out.reshape(d_pool + d_conv, d), y0, out_dtype=BF16)
    dy0 = after(dy0, reduce_step(dy0, begin=("ab_out", dw.reshape(N_DEV, -1, d)), middle="ff1_0"))
    dcv, dg_ln_g, dg_ln_b = ln_silu_bwd("ln_silu_bwd", cv, conv_ln_g, conv_ln_b, dy0, d_pool // d_conv)
    dz0, dw_conv, dg_conv_b = conv_bwd("conv_bwd", z0, dcv, w_conv, d_pool, d_conv)
    dz0, dw_pool, dg_pool_scale = pool_bwd("pool_bwd", pooled, dy0, w_pool, pool_scale, dz0)
    small_parts = [
        dw_conv.reshape(kw, N_DEV, -1).transpose(1, 0, 2).reshape(N_DEV, -1, lanes),
        dw_sconv.reshape(ks, N_DEV, -1).transpose(1, 0, 2).reshape(N_DEV, -1, lanes),
        dw_pool.reshape(ng, N_DEV, pg // N_DEV, pg).transpose(1, 0, 2, 3).reshape(N_DEV, -1, lanes),
    ]
    small = jnp.pad(jnp.concatenate(small_parts, axis=1), ((0, 0), (0, small_total - r2), (0, 0)))
    dz0 = after(dz0, reduce_step(dz0, begin=("small", small), middle="ab_out"))
    dh0, dw = mm_bwd_pair_blocked("ab_dh_dwin", dz0, w_ab_in, h0, out_dtype=BF16)
    dh0 = after(dh0, reduce_step(dh0, begin=("ab_in", dw), middle="small"))
    grad_x, dg_mix_pre0 = bwd_pre_final("bwd_0", dx1, xs, row(mix_pre_g, 0), dh0)
    tok = reduce_step(grad_x, middle="ab_in")

    gains = [dg_mix_pre0, dg_mix_pre1, dg_mix_post0, dg_mix_post1, dg_ffn_pre0, dg_ffn_pre1, dg_ffn_post0, dg_ffn_post1]
    pieces = [(g, i, 0) for i, g in enumerate(gains)]
    rep_layout = [(0, 0), (2, 0), (4, 0), (6, 0)]
    offset = 0
    for g in (dg_pool_scale, dg_conv_b, dg_ln_g, dg_ln_b):
        at = (len(gains) + offset // d, offset % d)
        pieces.append((g, *at))
        rep_layout.append(at)
        offset += g.shape[1]
    loss_at = (len(gains) + -(-offset // d), 0)
    rep_zone = place_sheet("place_rep", pieces, loss_part, loss_at, 16, d)
    (rep_hop,), tok = copies_start("rep_start", [[rep_zone]], _first_hop, 4,
                                   deps=[tok])

    def upd(name, w, m, v, contribs):
        shape = w.shape
        flat2 = lambda a: a.reshape(-1, shape[-1])
        outs = adamw(name, flat2(w), flat2(m), flat2(v), contribs)
        return [o.reshape(shape) for o in outs]

    g_ff2 = [reduce_end("ff2_0", tok), reduce_end("ff2_1", tok)]
    o_ff2 = upd("adam_ffn_w2", ffn_w2, m_ffn_w2, v_ffn_w2, g_ff2)
    (rep_zone,) = copies_wait("rep_wait", rep_hop, _first_hop, o_ff2[0])
    (rep_hop,), _ = copies_start("rep_forward_start", [[rep_zone]], _second_hop, 3)
    g_ff1 = [reduce_end("ff1_0", o_ff2[0]), reduce_end("ff1_1", o_ff2[0])]
    o_ff1 = upd("adam_ffn_w1", ffn_w1, m_ffn_w1, v_ffn_w1, g_ff1)
    (rep_all,) = copies_wait("rep_forward_wait", rep_hop, _second_hop, o_ff1[0])
    loss_sum, *o_rep = adamw_replicated(
        "adam_replicated",
        [mix_pre_g, mix_post_g, ffn_pre_g, ffn_post_g, pool_scale, conv_b, conv_ln_g, conv_ln_b],
        [m_mix_pre_g, m_mix_post_g, m_ffn_pre_g, m_ffn_post_g, m_pool_scale, m_conv_b, m_conv_ln_g, m_conv_ln_b],
        [v_mix_pre_g, v_mix_post_g, v_ffn_pre_g, v_ffn_post_g, v_pool_scale, v_conv_b, v_conv_ln_g, v_conv_ln_b],
        rep_all, rep_layout, loss_at)
    loss = loss_sum[0, 0] * (0.5 / d)
    o_sc_out = upd("adam_sc_out", sc_w_out, m_sc_w_out, v_sc_w_out, [reduce_end("sc_out", o_ff1[0])])
    o_sc_in = upd("adam_sc_in", sc_w_in, m_sc_w_in, v_sc_w_in, [reduce_end("sc_in", o_sc_out[0])])
    o_ab_out = upd("adam_ab_out", ab_w_out, m_ab_w_out, v_ab_w_out, [reduce_end("ab_out", o_sc_in[0])])
    o_small = adamw("adam_small", pack_small(conv_w, sc_conv_w, pool_w), pack_small(m_conv_w, m_sc_conv_w, m_pool_w),
                    pack_small(v_conv_w, v_sc_conv_w, v_pool_w), [reduce_end("small", o_ab_out[0])])
    o_ab_in = upd("adam_ab_in", ab_w_in, m_ab_w_in, v_ab_w_in, [reduce_end("ab_in", o_small[0])])

    def unpack_small(o):
        return o[:r0].reshape(conv_w.shape), o[r0:r1].reshape(sc_conv_w.shape), o[r1:r2].reshape(pool_w.shape)

    results = []
    for kind in range(4):
        g_mix_pre, g_mix_post, g_ffn_pre, g_ffn_post, g_scale, g_conv_b, g_ln_g, g_ln_b = o_rep[kind::4]
        s_conv, s_sconv, s_pool = unpack_small(o_small[kind])
        results.append([
            g_mix_pre, g_mix_post, g_ffn_pre, g_ffn_post,
            o_ab_in[kind], s_pool, g_scale, s_conv, g_conv_b, g_ln_g, g_ln_b,
            o_ab_out[kind], o_sc_in[kind], s_sconv, o_sc_out[kind], o_ff1[kind], o_ff2[kind]])

    return (loss, grad_x[None], *results[0], *results[1], *results[2], *results[3])
```

```python
import jax
import jax.numpy as jnp
from jax import lax
from jax.experimental import pallas as pl
from jax.experimental.pallas import tpu as pltpu

F32 = jnp.float32
BF16 = jnp.bfloat16
MESH = pl.DeviceIdType.MESH
ANY = pl.BlockSpec(memory_space=pl.ANY)

NORM_EPS = 1e-6
POOL_WINDOWS = (2, 4, 8, 16)
ADAM_LR = 0.001
ADAM_B1 = 0.9
ADAM_B2 = 0.999
ADAM_EPS = 1e-08
ADAM_WD = 0.01
ADAM_STEP = 10

N_DEV = 8
VMEM_LIMIT = 56 * 1024 * 1024
PAIR_ADD_BLOCK = 1 << 20
MATMUL_ROWS = 2048
UP_FIRST_BLOCKS = 6
ROW_TILE = 256
CHANNEL_TILE = 256
TIME_CHUNK = 64
HALO = 32

NN = (((1,), (0,)), ((), ()))
NT = (((1,), (1,)), ((), ()))
TN = (((0,), (0,)), ((), ()))


def _params(sem):
    return pltpu.CompilerParams(dimension_semantics=sem, vmem_limit_bytes=VMEM_LIMIT)


def _place():
    x, y, c = lax.axis_index("x"), lax.axis_index("y"), lax.axis_index("c")
    return x, y, c


def _slot(px, py, pc):
    return 4 * px + 2 * py + pc


HBM = pl.BlockSpec(memory_space=pltpu.HBM)
SEM = pl.BlockSpec(memory_space=pltpu.SEMAPHORE)
EFFECT = pltpu.SideEffectType.DATAFLOW_SIDE_EFFECTING
TOKEN = jax.ShapeDtypeStruct((8, 128), F32)


def _in_hbm(a):
    return pltpu.with_memory_space_constraint(a, pltpu.HBM)


CHIPS = [(0, 0), (0, 1), (1, 0), (1, 1)]
N_CHIP = len(CHIPS)


def _chip(px, py):
    return 2 * px + py


def _first_hop(bufs, sends, recvs, waiting):
    (land,) = bufs
    x, y, c = _place()
    me = _slot(x, y, c)
    peers = [(x, y, 1 - c), (1 - x, y, c), (x, 1 - y, c), (1 - x, 1 - y, c)]
    return [pltpu.make_async_remote_copy(
        src_ref=land.at[me], dst_ref=land.at[_slot(*p) if waiting else me],
        send_sem=sends.at[k], recv_sem=recvs.at[k], device_id=p, device_id_type=MESH) for k, p in enumerate(peers)]


def _second_hop(bufs, sends, recvs, waiting):
    (land,) = bufs
    x, y, c = _place()
    return [pltpu.make_async_remote_copy(
        src_ref=land.at[_slot(px, py, c)], dst_ref=land.at[_slot(px, py, 1 - c if waiting else c)],
        send_sem=sends.at[k], recv_sem=recvs.at[k], device_id=(x, y, 1 - c), device_id_type=MESH)
        for k, (px, py) in enumerate([(1 - x, y), (x, 1 - y), (1 - x, 1 - y)])]


def _ring_hop1(bufs, sends, recvs, waiting):
    (land,) = bufs
    x, y, c = _place()
    me = _slot(x, y, c)
    peers = [(1 - x, y, c), (x, 1 - y, c), (x, y, 1 - c)]
    return [pltpu.make_async_remote_copy(
        src_ref=land.at[me], dst_ref=land.at[_slot(*p) if waiting else me],
        send_sem=sends.at[k], recv_sem=recvs.at[k], device_id=p, device_id_type=MESH) for k, p in enumerate(peers)]


def _ring_hop2(bufs, sends, recvs, waiting):
    (land,) = bufs
    x, y, c = _place()
    half = land.shape[1] // 2
    first, second = pl.ds(0, half), pl.ds(half, half)
    nx, ny, diag = _slot(1 - x, y, c), _slot(x, 1 - y, c), _slot(1 - x, 1 - y, c)
    plan = [
        (land.at[ny, first], land.at[diag, first], (1 - x, y, c)),
        (land.at[nx, second], land.at[diag, second], (x, 1 - y, c)),
        (land.at[nx], land.at[_slot(1 - x, y, 1 - c)], (x, y, 1 - c)),
        (land.at[ny], land.at[_slot(x, 1 - y, 1 - c)], (x, y, 1 - c))]
    return [pltpu.make_async_remote_copy(
        src_ref=src, dst_ref=mine if waiting else src, send_sem=sends.at[k], recv_sem=recvs.at[k],
        device_id=to, device_id_type=MESH) for k, (src, mine, to) in enumerate(plan)]


def _ring_hop3(bufs, sends, recvs, waiting):
    (land,) = bufs
    x, y, c = _place()
    return [pltpu.make_async_remote_copy(
        src_ref=land.at[_slot(1 - x, 1 - y, c)], dst_ref=land.at[_slot(1 - x, 1 - y, 1 - c if waiting else c)],
        send_sem=sends.at[0], recv_sem=recvs.at[0], device_id=(x, y, 1 - c), device_id_type=MESH)]


RING_HOPS = [(_ring_hop1, 3), (_ring_hop2, 4), (_ring_hop3, 1)]

def _pair_hop(bufs, sends, recvs, waiting):
    g, land = bufs
    x, y, c = _place()
    return [pltpu.make_async_remote_copy(
        src_ref=g.at[_slot(qx, qy, 1 - c)], dst_ref=land.at[q],
        send_sem=sends.at[q], recv_sem=recvs.at[q], device_id=(x, y, 1 - c), device_id_type=MESH)
        for q, (qx, qy) in enumerate(CHIPS)]


def _chip_hop(bufs, sends, recvs, waiting):
    p, land = bufs
    x, y, c = _place()
    return [pltpu.make_async_remote_copy(
        src_ref=p.at[_chip(px, py)], dst_ref=land.at[_chip(px, py) if waiting else _chip(x, y)],
        send_sem=sends.at[k], recv_sem=recvs.at[k], device_id=(px, py, c), device_id_type=MESH)
        for k, (px, py) in enumerate([(1 - x, y), (x, 1 - y), (1 - x, 1 - y)])]


def copies_start(name, groups, hop, n_copies, deps=()):
    flat = [b for grp in groups for b in grp]
    nb, ng = len(flat), len(groups)
    deps = list(deps)
    hops = list(hop) if isinstance(hop, (list, tuple)) else [hop] * ng
    counts = list(n_copies) if isinstance(n_copies, (list, tuple)) else [n_copies] * ng

    def body(*refs):
        ins, token = refs[:nb], refs[-1]
        sems = refs[nb + len(deps):nb + len(deps) + 2 * ng]
        i = 0
        for gi, grp in enumerate(groups):
            for cp in hops[gi](ins[i:i + len(grp)], sems[2 * gi], sems[2 * gi + 1], False):
                cp.start()
            i += len(grp)
        token[...] = jnp.zeros_like(token)

    outs = pl.pallas_call(
        body, name=name,
        out_shape=([pltpu.SemaphoreType.DMA((n,)) for n in counts for _ in range(2)]
                   + [pltpu.HBM(b.shape, b.dtype) for b in flat] + [TOKEN]),
        in_specs=[HBM] * nb + [ANY] * len(deps),
        out_specs=[SEM] * (2 * ng) + [HBM] * nb + [pl.BlockSpec(memory_space=pltpu.VMEM)],
        input_output_aliases={i: 2 * ng + i for i in range(nb)},
        compiler_params=pltpu.CompilerParams(has_side_effects=EFFECT),
    )(*[_in_hbm(b) for b in flat], *deps)
    started, i = [], 0
    for gi, grp in enumerate(groups):
        started.append((outs[2 * gi], outs[2 * gi + 1], list(outs[2 * ng + i:2 * ng + i + len(grp)])))
        i += len(grp)
    return started, outs[-1]


def copies_wait(name, started, hop, after):
    sends, recvs, bufs = started
    nb = len(bufs)

    def body(*refs):
        for cp in hop(refs[:nb], refs[nb], refs[nb + 1], True):
            cp.wait_send()
            cp.wait_recv()

    outs = pl.pallas_call(
        body, name=name,
        out_shape=[pltpu.HBM(b.shape, b.dtype) for b in bufs],
        in_specs=[HBM] * nb + [SEM, SEM, ANY], out_specs=[HBM] * nb,
        input_output_aliases={i: i for i in range(nb)},
        compiler_params=pltpu.CompilerParams(has_side_effects=EFFECT),
    )(*bufs, sends, recvs, after)
    return list(outs)


def place_shard(name, w, layer, dtype, deps=()):
    _, r, c = w.shape
    tr = _tile(r, 1024)
    x, y, core = _place()
    me = _slot(x, y, core).astype(jnp.int32).reshape(1)

    def body(me_ref, w_ref, *rest):
        rest[-1][...] = w_ref[...].astype(dtype)

    return pl.pallas_call(
        body, name=name,
        grid_spec=pltpu.PrefetchScalarGridSpec(
            num_scalar_prefetch=1, grid=(r // tr,),
            in_specs=[pl.BlockSpec((None, tr, c), lambda i, me_ref: (layer, i, 0))] + [ANY] * len(deps),
            out_specs=pl.BlockSpec((None, tr, c), lambda i, me_ref: (me_ref[0], i, 0))),
        out_shape=jax.ShapeDtypeStruct((N_DEV, r, c), dtype),
        compiler_params=_params(("parallel",)),
    )(me, w, *deps)


def place_sheet(name, pieces, total_of, total_at, rows, width):
    x, y, core = _place()
    me = _slot(x, y, core).astype(jnp.int32).reshape(1)

    def body(me_ref, *refs):
        o_ref = refs[-1]
        o_ref[...] = jnp.zeros_like(o_ref)
        for ref, (_, row, lane) in zip(refs, pieces):
            o_ref[row:row + 1, lane:lane + ref.shape[1]] = jnp.sum(ref[...], axis=0, keepdims=True)
        total = jnp.sum(jnp.sum(refs[len(pieces)][...], axis=0, keepdims=True), axis=1, keepdims=True)
        o_ref[total_at[0]:total_at[0] + 1, total_at[1]:total_at[1] + 128] = jnp.broadcast_to(total, (1, 128))

    arrays = [a for a, _, _ in pieces] + [total_of]
    return pl.pallas_call(
        body, name=name,
        grid_spec=pltpu.PrefetchScalarGridSpec(
            num_scalar_prefetch=1, grid=(1,),
            in_specs=[pl.BlockSpec(a.shape, lambda i, me_ref: (0, 0)) for a in arrays],
            out_specs=pl.BlockSpec((None, rows, width), lambda i, me_ref: (me_ref[0], 0, 0))),
        out_shape=jax.ShapeDtypeStruct((N_DEV, rows, width), F32),
    )(me, *arrays)


def tie(name, x, *deps):
    def body(*refs):
        del refs

    return pl.pallas_call(
        body, name=name, out_shape=jax.ShapeDtypeStruct(x.shape, x.dtype),
        in_specs=[ANY] * (1 + len(deps)), out_specs=ANY, input_output_aliases={0: 0},
    )(x, *deps)


def pair_add(name, g, from_sibling):
    _, r, c_dim = g.shape
    tr = r
    while tr * c_dim > PAIR_ADD_BLOCK and tr % 16 == 0:
        tr //= 2
    x, y, core = _place()
    where = jnp.stack([core, _chip(x, y)]).astype(jnp.int32)

    def body(where_ref, g_ref, s_ref, o_ref, zone_ref):
        total = (g_ref[...].astype(F32) + s_ref[...].astype(F32)).astype(o_ref.dtype)
        o_ref[...] = total

        @pl.when(pl.program_id(1) == where_ref[1])
        def _():
            zone_ref[...] = total

    blk = pl.BlockSpec((None, tr, c_dim), lambda i, q, where_ref: (q, i, 0))
    return pl.pallas_call(
        body, name=name,
        grid_spec=pltpu.PrefetchScalarGridSpec(
            num_scalar_prefetch=1, grid=(r // tr, N_CHIP),
            in_specs=[pl.BlockSpec((None, None, tr, c_dim), lambda i, q, where_ref: (q, where_ref[0], i, 0)), blk],
            out_specs=[blk, pl.BlockSpec((None, tr, c_dim), lambda i, q, where_ref: (where_ref[1], i, 0))]),
        out_shape=[jax.ShapeDtypeStruct((N_CHIP, r, c_dim), g.dtype)] * 2,
        compiler_params=_params(("parallel", "arbitrary")),
    )(where, g.reshape(N_CHIP, 2, r, c_dim), from_sibling)


def _matmul(name, lhs, rhs, *, out_shape, out_dtype, grid, lhs_spec, rhs_spec, out_spec, acc_shape,
            lhs_fn=None, epilogue=None, into=None):
    nk = grid[2]
    extra = [] if into is None else [into]

    def body(lhs_ref, rhs_ref, *rest):
        out_ref, scratch = rest[len(extra)], rest[len(extra) + 1:]

        def product():
            a = lhs_ref[...]
            if lhs_fn is not None:
                a = lhs_fn(a)
            return lax.dot_general(a, rhs_ref[...], NN, preferred_element_type=F32)

        def finish(r):
            if epilogue is not None:
                r = epilogue(r)
            out_ref[...] = r.astype(out_dtype)

        if nk == 1:
            finish(product())
        else:
            (acc_ref,) = scratch
            k = pl.program_id(2)

            @pl.when(k == 0)
            def _():
                acc_ref[...] = product()

            @pl.when(jnp.logical_and(k > 0, k < nk - 1))
            def _():
                acc_ref[...] += product()

            @pl.when(k == nk - 1)
            def _():
                finish(acc_ref[...] + product())

    return pl.pallas_call(
        body, name=name, grid=grid,
        out_shape=jax.ShapeDtypeStruct(out_shape, out_dtype),
        in_specs=[lhs_spec, rhs_spec] + [ANY] * len(extra), out_specs=out_spec,
        input_output_aliases={2: 0} if extra else {},
        scratch_shapes=[pltpu.VMEM(acc_shape, F32)] if nk > 1 else [],
        compiler_params=_params(("parallel", "parallel", "arbitrary")),
    )(lhs, rhs, *extra)


def _tile(n, want):
    return want if n % want == 0 else n


def mm_nn(name, x, w, *, out_dtype, tn=512, tk=None, lhs_fn=None, epilogue=None):
    t, kdim = x.shape
    n = w.shape[1]
    tm, tn = _tile(t, MATMUL_ROWS), _tile(n, tn)
    tk = kdim if tk is None else _tile(kdim, tk)
    return _matmul(
        name, x, w, out_shape=(t, n), out_dtype=out_dtype, grid=(t // tm, n // tn, kdim // tk),
        lhs_spec=pl.BlockSpec((tm, tk), lambda i, j, k: (i, k)),
        rhs_spec=pl.BlockSpec((tk, tn), lambda i, j, k: (k, j)),
        out_spec=pl.BlockSpec((tm, tn), lambda i, j, k: (i, j)),
        acc_shape=(tm, tn), lhs_fn=lhs_fn, epilogue=epilogue)


def mm_nn_blocked(name, x, w, *, out_dtype, epilogue=None, blocks=(0, N_DEV), into=None):
    t, kdim = x.shape
    nb = w.shape[2]
    tm = _tile(t, MATMUL_ROWS)
    tn = nb // 2 if nb >= 1024 else nb
    sub = nb // tn
    first, count = blocks
    return _matmul(
        name, x, w, out_shape=(t, N_DEV * nb), out_dtype=out_dtype, grid=(t // tm, count * sub, 1),
        lhs_spec=pl.BlockSpec((tm, kdim), lambda i, j, k: (i, k)),
        rhs_spec=pl.BlockSpec((None, kdim, tn), lambda i, j, k: (first + j // sub, k, j % sub)),
        out_spec=pl.BlockSpec((tm, tn), lambda i, j, k: (i, first * sub + j)),
        acc_shape=(tm, tn), epilogue=epilogue, into=into)


def mm_bwd_pair(name, dy, w, act, *, out_dtype, tile=512, act_fn=None, epilogue=None):
    t, n = dy.shape
    kdim = w.shape[0]
    tile = _tile(kdim, tile)

    def body(dy_ref, w_ref, act_ref, dx_ref, dw_ref):
        a = act_ref[...]
        dx = lax.dot_general(dy_ref[...], w_ref[...], NT, preferred_element_type=F32)
        if epilogue is not None:
            dx = epilogue(dx, a)
        dx_ref[...] = dx.astype(out_dtype)
        if act_fn is not None:
            a = act_fn(a)
        dw_ref[...] = lax.dot_general(a, dy_ref[...], TN, preferred_element_type=F32).astype(out_dtype)

    return pl.pallas_call(
        body, name=name, grid=(kdim // tile,),
        in_specs=[pl.BlockSpec((t, n), lambda j: (0, 0)), pl.BlockSpec((tile, n), lambda j: (j, 0)),
                  pl.BlockSpec((t, tile), lambda j: (0, j))],
        out_specs=[pl.BlockSpec((t, tile), lambda j: (0, j)), pl.BlockSpec((tile, n), lambda j: (j, 0))],
        out_shape=[jax.ShapeDtypeStruct((t, kdim), out_dtype), jax.ShapeDtypeStruct((kdim, n), out_dtype)],
        compiler_params=_params(("parallel",)),
    )(dy, w, act)


def mm_bwd_pair_blocked(name, dz, w, act, *, out_dtype, tile=1024):
    t = dz.shape[0]
    kdim, nb = w.shape[1], w.shape[2]
    tile = _tile(kdim, tile)

    def body(dz_ref, w_ref, act_ref, dx_ref, dw_ref, acc_ref):
        j = pl.program_id(1)
        dw_ref[...] = lax.dot_general(act_ref[...], dz_ref[...], TN, preferred_element_type=F32).astype(out_dtype)

        def product():
            return lax.dot_general(dz_ref[...], w_ref[...], NT, preferred_element_type=F32)

        @pl.when(j == 0)
        def _():
            acc_ref[...] = product()

        @pl.when(jnp.logical_and(j > 0, j < N_DEV - 1))
        def _():
            acc_ref[...] += product()

        @pl.when(j == N_DEV - 1)
        def _():
            dx_ref[...] = (acc_ref[...] + product()).astype(out_dtype)

    return pl.pallas_call(
        body, name=name, grid=(kdim // tile, N_DEV),
        in_specs=[pl.BlockSpec((t, nb), lambda i, j: (0, j)), pl.BlockSpec((None, tile, nb), lambda i, j: (j, i, 0)),
                  pl.BlockSpec((t, tile), lambda i, j: (0, i))],
        out_specs=[pl.BlockSpec((t, tile), lambda i, j: (0, i)),
                   pl.BlockSpec((None, tile, nb), lambda i, j: (j, i, 0))],
        out_shape=[jax.ShapeDtypeStruct((t, kdim), out_dtype), jax.ShapeDtypeStruct((N_DEV, kdim, nb), out_dtype)],
        scratch_shapes=[pltpu.VMEM((t, tile), F32)],
        compiler_params=_params(("parallel", "arbitrary")),
    )(dz, w, act)


def _rstd(v):
    return lax.rsqrt(jnp.mean(v * v, axis=-1, keepdims=True) + NORM_EPS)


def _rms_bwd(v, g, dy):
    r = _rstd(v)
    vhat = v * r
    dvh = dy * g
    dv = r * (dvh - vhat * jnp.mean(dvh * vhat, axis=-1, keepdims=True))
    return dv, dy * vhat


def _fold8(v):
    rows, n = v.shape
    return jnp.sum(v.reshape(rows // 8, 8, n), axis=0)


def _fold_lanes(v):
    out = v[:, 0:128]
    for i in range(1, v.shape[1] // 128):
        out = out + v[:, 128 * i:128 * (i + 1)]
    return out


def _accumulate(ref, v):
    i = pl.program_id(0)

    @pl.when(i == 0)
    def _():
        ref[...] = v

    @pl.when(i > 0)
    def _():
        ref[...] += v


def _row_call(body, name, t, ins, row_in, outs, acc_outs=(), tr=ROW_TILE):
    tr = _tile(t, tr)

    def in_spec(a, tiled):
        if isinstance(tiled, tuple):
            width, j = tiled
            return pl.BlockSpec((tr, width), lambda i: (i, j))
        return pl.BlockSpec((tr, a.shape[1]), lambda i: (i, 0)) if tiled else pl.BlockSpec(a.shape, lambda i: (0, 0))

    in_specs = [in_spec(a, tiled) for a, tiled in zip(ins, row_in)]
    out_specs = [pl.BlockSpec((tr, n), lambda i: (i, 0)) for n, _ in outs]
    out_specs += [pl.BlockSpec((8, n), lambda i: (0, 0)) for n in acc_outs]
    out_shape = [jax.ShapeDtypeStruct((t, n), dt) for n, dt in outs]
    out_shape += [jax.ShapeDtypeStruct((8, n), F32) for n in acc_outs]
    return pl.pallas_call(
        body, name=name, grid=(t // tr,), in_specs=in_specs, out_specs=out_specs, out_shape=out_shape,
        compiler_params=_params(("arbitrary",) if acc_outs else ("parallel",)),
    )(*ins)


def norm_pre(name, x, g):
    t, d = x.shape

    def body(x_ref, g_ref, h_ref):
        v = x_ref[...]
        h_ref[...] = (v * _rstd(v) * g_ref[...]).astype(BF16)

    return _row_call(body, name, t, [x, g], [True, False], [(d, BF16)])[0]


def post_pre(name, x, m, g_post, g_pre):
    t, d = x.shape

    def body(x_ref, m_ref, gp_ref, gn_ref, xo_ref, h_ref):
        mv = m_ref[...]
        xn = x_ref[...] + mv * _rstd(mv) * gp_ref[...]
        xo_ref[...] = xn
        h_ref[...] = (xn * _rstd(xn) * gn_ref[...]).astype(BF16)

    return _row_call(body, name, t, [x, m, g_post, g_pre], [True, True, False, False], [(d, F32), (d, BF16)])


def post_loss(name, x, f, g_post, target):
    t, d = x.shape

    def body(x_ref, f_ref, g_ref, t_ref, dx_ref, df_ref, loss_ref, dg_ref):
        fv = f_ref[...]
        g = g_ref[...]
        out = x_ref[...] + fv * _rstd(fv) * g
        err = out - t_ref[...]
        dx = err * (1.0 / d)
        dx_ref[...] = dx
        dfv, dg_rows = _rms_bwd(fv, g, dx)
        df_ref[...] = dfv.astype(BF16)
        _accumulate(loss_ref, _fold8(_fold_lanes(err * err)))
        _accumulate(dg_ref, _fold8(dg_rows))

    return _row_call(body, name, t, [x, f, g_post, target], [True, True, False, True],
                     [(d, F32), (d, BF16)], acc_outs=(128, d))


def bwd_pre_post(name, dx_out, x_in, g_pre, dh, f_prev, g_post_prev):
    t, d = x_in.shape

    def body(dxo_ref, x_ref, gpre_ref, dh_ref, f_ref, gpost_ref, dxi_ref, df_ref, dgpre_ref, dgpost_ref):
        dxv, dgpre_rows = _rms_bwd(x_ref[...], gpre_ref[...], dh_ref[...].astype(F32))
        dxi = dxo_ref[...] + dxv
        dxi_ref[...] = dxi
        dfv, dgpost_rows = _rms_bwd(f_ref[...], gpost_ref[...], dxi)
        df_ref[...] = dfv.astype(BF16)
        _accumulate(dgpre_ref, _fold8(dgpre_rows))
        _accumulate(dgpost_ref, _fold8(dgpost_rows))

    return _row_call(body, name, t, [dx_out, x_in, g_pre, dh, f_prev, g_post_prev],
                     [True, True, False, True, True, False], [(d, F32), (d, BF16)], acc_outs=(d, d))


def bwd_pre_final(name, dx_out, x_in, g_pre, dh):
    t, d = x_in.shape

    def body(dxo_ref, x_ref, gpre_ref, dh_ref, dxi_ref, dgpre_ref):
        dxv, dgpre_rows = _rms_bwd(x_ref[...], gpre_ref[...], dh_ref[...].astype(F32))
        dxi_ref[...] = dxo_ref[...] + dxv
        _accumulate(dgpre_ref, _fold8(dgpre_rows))

    return _row_call(body, name, t, [dx_out, x_in, g_pre, dh], [True, True, False, True], [(d, F32)], acc_outs=(d,))


def _layer_norm_parts(cv):
    mu = jnp.mean(cv, axis=-1, keepdims=True)
    xc = cv - mu
    rstd = lax.rsqrt(jnp.mean(xc * xc, axis=-1, keepdims=True) + NORM_EPS)
    return xc * rstd, rstd


def ln_silu(name, cv, g, b, y, y_block):
    t, n = cv.shape
    tr = _tile(t, ROW_TILE)

    def body(c_ref, g_ref, b_ref, y_in_ref, y_ref):
        chat, _ = _layer_norm_parts(c_ref[...])
        ln = chat * g_ref[...] + b_ref[...]
        y_ref[...] = (ln * jax.nn.sigmoid(ln)).astype(BF16)

    vec = pl.BlockSpec((1, n), lambda i: (0, 0))
    return pl.pallas_call(
        body, name=name, grid=(t // tr,),
        in_specs=[pl.BlockSpec((tr, n), lambda i: (i, 0)), vec, vec, ANY],
        out_specs=pl.BlockSpec((tr, n), lambda i: (i, y_block)),
        out_shape=jax.ShapeDtypeStruct(y.shape, y.dtype), input_output_aliases={3: 0},
        compiler_params=_params(("parallel",)),
    )(cv, g, b, y)


def ln_silu_bwd(name, cv, g, b, dy, dy_block):
    t, n = cv.shape

    def body(c_ref, g_ref, b_ref, dy_ref, dc_ref, dg_ref, db_ref):
        chat, rstd = _layer_norm_parts(c_ref[...])
        g = g_ref[...]
        ln = chat * g + b_ref[...]
        s = jax.nn.sigmoid(ln)
        dln = dy_ref[...].astype(F32) * (s * (1.0 + ln * (1.0 - s)))
        dchat = dln * g
        dc_ref[...] = rstd * (dchat - jnp.mean(dchat, axis=-1, keepdims=True)
                              - chat * jnp.mean(dchat * chat, axis=-1, keepdims=True))
        _accumulate(dg_ref, _fold8(dln * chat))
        _accumulate(db_ref, _fold8(dln))

    return _row_call(body, name, t, [cv, g, b, dy], [True, False, False, (n, dy_block)], [(n, F32)], acc_outs=(n, n))


def _chunks(t, fn, tc=TIME_CHUNK):
    tc = _tile(t, tc)

    def step(i, carry):
        fn(pl.multiple_of(i * tc, tc), tc)
        return carry

    lax.fori_loop(0, t // tc, step, 0)


def _rows_from(v, start, n):
    res = start % 8
    base = v if res == 0 else pltpu.roll(v, v.shape[0] - res, axis=0)
    return base[start - res:start - res + n, :]


def _shifted(window, offsets, tc):
    rows = window.shape[0]
    by_residue = {}
    for k, off in enumerate(offsets):
        by_residue.setdefault(off % 8, []).append((k, off))
    for res, taps in by_residue.items():
        base = window if res == 0 else pltpu.roll(window, rows - res, axis=0)
        for k, off in taps:
            yield k, base[off - res:off - res + tc, :]


def _taps(window, w_ref, offsets, tc, flip=False):
    acc = None
    for k, rows in _shifted(window, offsets, tc):
        kk = len(offsets) - 1 - k if flip else k
        term = w_ref[kk:kk + 1, :] * rows
        acc = term if acc is None else acc + term
    return acc


def _window_sums(win, tc, causal):
    sums = []
    cur, rows, step = win, tc + HALO, 1
    for _ in POOL_WINDOWS:
        rows -= 8
        if causal:
            cur = cur[8:8 + rows, :] + _rows_from(cur, 8 - step, rows)
            sums.append(cur[rows - tc:rows, :])
        else:
            cur = cur[0:rows, :] + _rows_from(cur, step, rows)
            sums.append(cur[0:tc, :])
        step *= 2
    return sums


def _pick(vals, g):
    out = vals[-1]
    for i in range(len(vals) - 2, -1, -1):
        out = jnp.where(g == i, vals[i], out)
    return out


def _pool_count(s, tc, g):
    t1 = (lax.broadcasted_iota(jnp.int32, (tc, 1), 0) + (s + 1)).astype(F32)
    width = _pick([float(w) for w in POOL_WINDOWS], g)
    return jnp.minimum(t1, width)


def pool_fwd(name, z, pool_w, pool_scale, d_pool, y_width):
    t = z.shape[0]
    ng, pg = pool_w.shape[0], pool_w.shape[1]

    def body(u_ref, w_ref, s_ref, pooled_ref, y_ref, pad):
        g = pl.program_id(0)
        pad[pl.ds(0, HALO), :] = jnp.zeros((HALO, pg), F32)

        def fill(s, tc):
            pad[pl.ds(HALO + s, tc), :] = u_ref[pl.ds(s, tc), :].astype(F32)

        def chunk(s, tc):
            win = pad[pl.ds(s, tc + HALO), :]
            total = _pick(_window_sums(win, tc, causal=True), g)
            pooled = total / _pool_count(s, tc, g) - win[HALO:HALO + tc, :]
            pooled_ref[pl.ds(s, tc), :] = pooled.astype(BF16)

        _chunks(t, fill)
        _chunks(t, chunk)
        mixed = jnp.dot(pooled_ref[...], w_ref[...], preferred_element_type=F32)
        y_ref[...] = (mixed * s_ref[...]).astype(BF16)

    col = pl.BlockSpec((t, pg), lambda g: (0, g))
    return pl.pallas_call(
        body, name=name, grid=(ng,),
        in_specs=[col, pl.BlockSpec((None, pg, pg), lambda g: (g, 0, 0)), pl.BlockSpec((1, pg), lambda g: (0, g))],
        out_specs=[col, col],
        out_shape=[jax.ShapeDtypeStruct((t, d_pool), BF16), jax.ShapeDtypeStruct((t, y_width), BF16)],
        scratch_shapes=[pltpu.VMEM((t + HALO, pg), F32)],
        compiler_params=_params(("parallel",)),
    )(z, pool_w, pool_scale)


def pool_bwd(name, pooled, dy, pool_w, pool_scale, dz):
    t, d_pool = pooled.shape
    ng, pg = pool_w.shape[0], pool_w.shape[1]

    def body(p_ref, dy_ref, w_ref, s_ref, dz_ref, du_ref, dw_ref, ds_ref, pad):
        g = pl.program_id(0)
        w = w_ref[...]
        dyv = dy_ref[...].astype(F32)
        mixed = jnp.dot(p_ref[...], w, preferred_element_type=F32)
        ds_ref[...] = jnp.sum(dyv * mixed, axis=0, keepdims=True)
        dmixed = (dyv * s_ref[...]).astype(BF16)
        dw_ref[...] = lax.dot_general(p_ref[...], dmixed, TN, preferred_element_type=F32)
        pad[...] = jnp.zeros((t + HALO, pg), F32)
        pad[pl.ds(0, t), :] = lax.dot_general(dmixed, w, NT, preferred_element_type=F32)

        def scale(s, tc):
            pad[pl.ds(s, tc), :] = pad[pl.ds(s, tc), :] / _pool_count(s, tc, g)

        def chunk(s, tc):
            win = pad[pl.ds(s, tc + HALO), :]
            total = _pick(_window_sums(win, tc, causal=False), g)
            du_ref[pl.ds(s, tc), :] = (total - win[0:tc, :] * _pool_count(s, tc, g)).astype(BF16)

        _chunks(t, scale)
        _chunks(t, chunk)

    col = pl.BlockSpec((t, pg), lambda g: (0, g))
    vec = pl.BlockSpec((1, pg), lambda g: (0, g))
    mat = pl.BlockSpec((None, pg, pg), lambda g: (g, 0, 0))
    return pl.pallas_call(
        body, name=name, grid=(ng,),
        in_specs=[col, col, mat, vec, ANY], out_specs=[col, mat, vec],
        out_shape=[jax.ShapeDtypeStruct(dz.shape, dz.dtype), jax.ShapeDtypeStruct((ng, pg, pg), F32),
                   jax.ShapeDtypeStruct((1, d_pool), F32)],
        input_output_aliases={4: 0},
        scratch_shapes=[pltpu.VMEM((t + HALO, pg), F32)],
        compiler_params=_params(("parallel",)),
    )(pooled, dy, pool_w, pool_scale, dz)


def conv_fwd(name, z, conv_w, conv_b, d_pool, d_conv):
    t = z.shape[0]
    kw = conv_w.shape[0]
    tc_ch = _tile(d_conv, CHANNEL_TILE)
    v0, g0 = d_pool // tc_ch, (d_pool + d_conv) // tc_ch

    def body(v_ref, g_ref, w_ref, b_ref, c_ref, pad):
        pad[pl.ds(0, HALO), :] = jnp.zeros((HALO, tc_ch), F32)

        def fill(s, tc):
            pad[pl.ds(HALO + s, tc), :] = v_ref[pl.ds(s, tc), :].astype(F32) * jax.nn.sigmoid(g_ref[pl.ds(s, tc), :].astype(F32))

        def chunk(s, tc):
            win = pad[pl.ds(s, tc + HALO), :]
            c_ref[pl.ds(s, tc), :] = _taps(win, w_ref, [HALO - (kw - 1) + k for k in range(kw)], tc) + b_ref[...]

        _chunks(t, fill)
        _chunks(t, chunk)

    return pl.pallas_call(
        body, name=name, grid=(d_conv // tc_ch,),
        in_specs=[pl.BlockSpec((t, tc_ch), lambda j: (0, v0 + j)), pl.BlockSpec((t, tc_ch), lambda j: (0, g0 + j)),
                  pl.BlockSpec((kw, tc_ch), lambda j: (0, j)), pl.BlockSpec((1, tc_ch), lambda j: (0, j))],
        out_specs=pl.BlockSpec((t, tc_ch), lambda j: (0, j)),
        out_shape=jax.ShapeDtypeStruct((t, d_conv), F32),
        scratch_shapes=[pltpu.VMEM((t + HALO, tc_ch), F32)],
        compiler_params=_params(("parallel",)),
    )(z, z, conv_w, conv_b)


def conv_bwd(name, z, dc, conv_w, d_pool, d_conv):
    t = z.shape[0]
    kw = conv_w.shape[0]
    tc_ch = _tile(d_conv, CHANNEL_TILE)
    v0, g0 = d_pool // tc_ch, (d_pool + d_conv) // tc_ch

    def body(v_ref, g_ref, dc_ref, w_ref, dz_ref, dw_ref, db_ref, pad_a, pad_dc, acc_w, acc_b, tiles, sems):
        j = pl.program_id(0)
        dv_ref, dg_ref = tiles.at[0], tiles.at[1]
        writes = [pltpu.make_async_copy(tiles.at[p], dz_ref.at[:, pl.ds((first + j) * tc_ch, tc_ch)], sems.at[p])
                  for p, first in enumerate([v0, g0])]

        def wait_writes():
            for cp in writes:
                cp.wait()

        pad_a[pl.ds(0, HALO), :] = jnp.zeros((HALO, tc_ch), F32)
        pad_dc[pl.ds(t, HALO), :] = jnp.zeros((HALO, tc_ch), F32)
        acc_w[...] = jnp.zeros_like(acc_w)
        acc_b[...] = jnp.zeros_like(acc_b)

        def fill(s, tc):
            pad_a[pl.ds(HALO + s, tc), :] = v_ref[pl.ds(s, tc), :].astype(F32) * jax.nn.sigmoid(g_ref[pl.ds(s, tc), :].astype(F32))
            pad_dc[pl.ds(s, tc), :] = dc_ref[pl.ds(s, tc), :]

        def chunk(s, tc):
            dcv = pad_dc[pl.ds(s, tc), :]
            win_a = pad_a[pl.ds(s, tc + HALO), :]
            for k, rows in _shifted(win_a, [HALO - (kw - 1) + k for k in range(kw)], tc):
                acc_w[pl.ds(8 * k, 8), :] += _fold8(dcv * rows)
            acc_b[...] += _fold8(dcv)
            da = _taps(pad_dc[pl.ds(s, tc + HALO), :], w_ref, list(range(kw)), tc, flip=True)
            vv = v_ref[pl.ds(s, tc), :].astype(F32)
            sg = jax.nn.sigmoid(g_ref[pl.ds(s, tc), :].astype(F32))
            dv_ref[pl.ds(s, tc), :] = (da * sg).astype(BF16)
            dg_ref[pl.ds(s, tc), :] = (da * vv * sg * (1.0 - sg)).astype(BF16)

        _chunks(t, fill)
        pl.when(j > 0)(wait_writes)
        _chunks(t, chunk)
        for cp in writes:
            cp.start()
        pl.when(j == n_tiles - 1)(wait_writes)
        for k in range(kw):
            dw_ref[k:k + 1, :] = jnp.sum(acc_w[pl.ds(8 * k, 8), :], axis=0, keepdims=True)
        db_ref[...] = jnp.sum(acc_b[...], axis=0, keepdims=True)

    n_tiles = d_conv // tc_ch
    return pl.pallas_call(
        body, name=name, grid=(n_tiles,),
        in_specs=[pl.BlockSpec((t, tc_ch), lambda j: (0, v0 + j)), pl.BlockSpec((t, tc_ch), lambda j: (0, g0 + j)),
                  pl.BlockSpec((t, tc_ch), lambda j: (0, j)), pl.BlockSpec((kw, tc_ch), lambda j: (0, j))],
        out_specs=[ANY, pl.BlockSpec((kw, tc_ch), lambda j: (0, j)), pl.BlockSpec((1, tc_ch), lambda j: (0, j))],
        out_shape=[jax.ShapeDtypeStruct((t, d_pool + 2 * d_conv), BF16),
                   jax.ShapeDtypeStruct((kw, d_conv), F32), jax.ShapeDtypeStruct((1, d_conv), F32)],
        scratch_shapes=[pltpu.VMEM((t + HALO, tc_ch), F32), pltpu.VMEM((t + HALO, tc_ch), F32),
                        pltpu.VMEM((8 * kw, tc_ch), F32), pltpu.VMEM((8, tc_ch), F32),
                        pltpu.VMEM((2, t, tc_ch), BF16), pltpu.SemaphoreType.DMA((2,))],
        compiler_params=_params(("arbitrary",)),
    )(z, z, dc, conv_w)


def short_fwd(name, z, conv_w, d_short):
    t = z.shape[0]
    kw = conv_w.shape[0]
    tc_ch = _tile(d_short, CHANNEL_TILE)
    nt = d_short // tc_ch

    def body(b_ref, c_ref, u_ref, w_ref, y_ref, pad):
        pad[pl.ds(0, HALO), :] = jnp.zeros((HALO, tc_ch), F32)

        def fill(s, tc):
            pad[pl.ds(HALO + s, tc), :] = c_ref[pl.ds(s, tc), :].astype(F32) * u_ref[pl.ds(s, tc), :].astype(F32)

        def chunk(s, tc):
            win = pad[pl.ds(s, tc + HALO), :]
            cq = _taps(win, w_ref, [HALO - (kw - 1) + k for k in range(kw)], tc)
            y_ref[pl.ds(s, tc), :] = (b_ref[pl.ds(s, tc), :].astype(F32) * cq).astype(BF16)

        _chunks(t, fill)
        _chunks(t, chunk)

    return pl.pallas_call(
        body, name=name, grid=(nt,),
        in_specs=[pl.BlockSpec((t, tc_ch), lambda j: (0, j)), pl.BlockSpec((t, tc_ch), lambda j: (0, nt + j)),
                  pl.BlockSpec((t, tc_ch), lambda j: (0, 2 * nt + j)), pl.BlockSpec((kw, tc_ch), lambda j: (0, j))],
        out_specs=pl.BlockSpec((t, tc_ch), lambda j: (0, j)),
        out_shape=jax.ShapeDtypeStruct((t, d_short), BF16),
        scratch_shapes=[pltpu.VMEM((t + HALO, tc_ch), F32)],
        compiler_params=_params(("parallel",)),
    )(z, z, z, conv_w)


def short_bwd(name, z, dy, conv_w, d_short):
    t = z.shape[0]
    kw = conv_w.shape[0]
    tc_ch = _tile(d_short, CHANNEL_TILE)
    nt = d_short // tc_ch

    def body(b_ref, c_ref, u_ref, dy_ref, w_ref, dz_ref, dw_ref, pad_q, pad_dcq, acc_w, tiles, sems):
        j = pl.program_id(0)
        db_ref, dcg_ref, du_ref = tiles.at[0], tiles.at[1], tiles.at[2]
        writes = [pltpu.make_async_copy(tiles.at[p], dz_ref.at[:, pl.ds((p * nt + j) * tc_ch, tc_ch)], sems.at[p])
                  for p in range(3)]

        def wait_writes():
            for cp in writes:
                cp.wait()

        pad_q[pl.ds(0, HALO), :] = jnp.zeros((HALO, tc_ch), F32)
        pad_dcq[pl.ds(t, HALO), :] = jnp.zeros((HALO, tc_ch), F32)
        acc_w[...] = jnp.zeros_like(acc_w)

        def fill(s, tc):
            rows = pl.ds(s, tc)
            pad_q[pl.ds(HALO + s, tc), :] = c_ref[rows, :].astype(F32) * u_ref[rows, :].astype(F32)
            pad_dcq[rows, :] = dy_ref[rows, :].astype(F32) * b_ref[rows, :].astype(F32)

        def chunk(s, tc):
            rows = pl.ds(s, tc)
            win_q = pad_q[pl.ds(s, tc + HALO), :]
            dcq = pad_dcq[rows, :]
            cq = None
            for k, shifted in _shifted(win_q, [HALO - (kw - 1) + k for k in range(kw)], tc):
                acc_w[pl.ds(8 * k, 8), :] += _fold8(dcq * shifted)
                term = w_ref[k:k + 1, :] * shifted
                cq = term if cq is None else cq + term
            db_ref[rows, :] = (dy_ref[rows, :].astype(F32) * cq).astype(BF16)
            dq = _taps(pad_dcq[pl.ds(s, tc + HALO), :], w_ref, list(range(kw)), tc, flip=True)
            dcg_ref[rows, :] = (dq * u_ref[rows, :].astype(F32)).astype(BF16)
            du_ref[rows, :] = (dq * c_ref[rows, :].astype(F32)).astype(BF16)

        _chunks(t, fill)
        pl.when(j > 0)(wait_writes)
        _chunks(t, chunk)
        for cp in writes:
            cp.start()
        pl.when(j == nt - 1)(wait_writes)
        for k in range(kw):
            dw_ref[k:k + 1, :] = jnp.sum(acc_w[pl.ds(8 * k, 8), :], axis=0, keepdims=True)

    zspec = [pl.BlockSpec((t, tc_ch), lambda j, o=o: (0, o * nt + j)) for o in range(3)]
    return pl.pallas_call(
        body, name=name, grid=(nt,),
        in_specs=[*zspec, pl.BlockSpec((t, tc_ch), lambda j: (0, j)), pl.BlockSpec((kw, tc_ch), lambda j: (0, j))],
        out_specs=[ANY, pl.BlockSpec((kw, tc_ch), lambda j: (0, j))],
        out_shape=[jax.ShapeDtypeStruct((t, 3 * d_short), BF16), jax.ShapeDtypeStruct((kw, d_short), F32)],
        scratch_shapes=[pltpu.VMEM((t + HALO, tc_ch), F32), pltpu.VMEM((t + HALO, tc_ch), F32),
                        pltpu.VMEM((8 * kw, tc_ch), F32), pltpu.VMEM((3, t, tc_ch), BF16),
                        pltpu.SemaphoreType.DMA((3,))],
        compiler_params=_params(("arbitrary",)),
    )(z, z, z, dy, conv_w)


def _adamw_update(w, m, v, g):
    nm = ADAM_B1 * m + (1.0 - ADAM_B1) * g
    nv = ADAM_B2 * v + (1.0 - ADAM_B2) * (g * g)
    m_hat = nm / (1.0 - ADAM_B1 ** ADAM_STEP)
    v_hat = nv / (1.0 - ADAM_B2 ** ADAM_STEP)
    return -ADAM_LR * (m_hat / (jnp.sqrt(v_hat) + ADAM_EPS) + ADAM_WD * w), nm, nv


def adamw_replicated(name, params, first_moments, second_moments, contributions, layout, scalar_at):
    n = len(params)
    n_slots = contributions.shape[0]

    def total(c_ref, row, lane, rows, lanes):
        acc = c_ref[0, row:row + rows, lane:lane + lanes]
        for slot in range(1, n_slots):
            acc = acc + c_ref[slot, row:row + rows, lane:lane + lanes]
        return acc

    def body(*refs):
        ws, ms, vs, c_ref = refs[:n], refs[n:2 * n], refs[2 * n:3 * n], refs[3 * n]
        outs = refs[3 * n + 1:]
        outs[0][...] = total(c_ref, *scalar_at, 1, 128)
        for i, (row, lane) in enumerate(layout):
            g = total(c_ref, row, lane, *params[i].shape)
            grad_ref, delta_ref, nm_ref, nv_ref = outs[1 + 4 * i:5 + 4 * i]
            grad_ref[...] = g
            delta_ref[...], nm_ref[...], nv_ref[...] = _adamw_update(ws[i][...], ms[i][...], vs[i][...], g)

    out_shape = [jax.ShapeDtypeStruct((1, 128), F32)]
    for p in params:
        out_shape += [jax.ShapeDtypeStruct(p.shape, F32)] * 4
    return pl.pallas_call(body, name=name, out_shape=out_shape)(*params, *first_moments, *second_moments, contributions)


def adamw(name, w, m, v, contributions):
    r, c = w.shape
    nc = len(contributions)
    n_slots = contributions[0].shape[0]
    tr = 256 if c <= 1024 else 128
    if any(a.shape[1] % tr for a in contributions):
        assert nc == 1
        tr = r
    tiles = [a.shape[1] // tr for a in contributions]
    first = [sum(tiles[:j]) for j in range(nc)]

    def body(w_ref, m_ref, v_ref, *rest):
        g_refs, (grad_ref, delta_ref, nm_ref, nv_ref) = rest[:nc], rest[nc:]
        i = pl.program_id(0)
        g = None
        for j, g_ref in enumerate(g_refs):
            s = g_ref[0].astype(F32)
            for slot in range(1, n_slots):
                s = s + g_ref[slot].astype(F32)
            g = s if g is None else jnp.where(i >= first[j], s, g)
        grad_ref[...] = g
        delta_ref[...], nm_ref[...], nv_ref[...] = _adamw_update(w_ref[...], m_ref[...], v_ref[...], g)

    blk = pl.BlockSpec((tr, c), lambda i: (i, 0))
    g_specs = [pl.BlockSpec((n_slots, tr, c), lambda i, j=j: (0, jnp.clip(i - first[j], 0, tiles[j] - 1), 0))
               for j in range(nc)]
    return pl.pallas_call(
        body, name=name, grid=(r // tr,),
        in_specs=[blk, blk, blk, *g_specs],
        out_specs=[blk] * 4, out_shape=[jax.ShapeDtypeStruct((r, c), F32)] * 4,
        compiler_params=_params(("parallel",)),
    )(w, m, v, *contributions)


def _pad_rows(a, rows):
    return jnp.pad(a, ((0, rows - a.shape[0]), (0, 0)))


def kernel(x, mix_pre_g, mix_post_g, ffn_pre_g, ffn_post_g, ab_w_in, pool_w, pool_scale, conv_w, conv_b, conv_ln_g, conv_ln_b, ab_w_out, sc_w_in, sc_conv_w, sc_w_out, ffn_w1, ffn_w2, loss_target, m_mix_pre_g, m_mix_post_g, m_ffn_pre_g, m_ffn_post_g, m_ab_w_in, m_pool_w, m_pool_scale, m_conv_w, m_conv_b, m_conv_ln_g, m_conv_ln_b, m_ab_w_out, m_sc_w_in, m_sc_conv_w, m_sc_w_out, m_ffn_w1, m_ffn_w2, v_mix_pre_g, v_mix_post_g, v_ffn_pre_g, v_ffn_post_g, v_ab_w_in, v_pool_w, v_pool_scale, v_conv_w, v_conv_b, v_conv_ln_g, v_conv_ln_b, v_ab_w_out, v_sc_w_in, v_sc_conv_w, v_sc_w_out, v_ffn_w1, v_ffn_w2):
    t, d = x.shape[1], x.shape[2]
    d_pool = pool_scale.shape[1]
    d_conv = conv_b.shape[1]
    d_short = d
    ng, pg = pool_w.shape[1], pool_w.shape[3]
    kw, ks = conv_w.shape[1], sc_conv_w.shape[1]
    nb_ab, nb_sc, nb_ff = ab_w_in.shape[2], sc_w_in.shape[2], ffn_w1.shape[2]

    xs = x[0]
    target = loss_target[0]

    lanes = min(128, d_conv // N_DEV)
    small_rows = [kw * (d_conv // N_DEV) // lanes, ks * (d_short // N_DEV) // lanes, ng * (pg // N_DEV) * pg // lanes]
    small_total = -(-sum(small_rows) // 8) * 8
    r0, r1, r2 = small_rows[0], small_rows[0] + small_rows[1], sum(small_rows)

    def pack_small(a_conv, a_sconv, a_pool):
        parts = [a_conv[0].reshape(-1, lanes), a_sconv[0].reshape(-1, lanes), a_pool[0].reshape(-1, lanes)]
        return _pad_rows(jnp.concatenate(parts, axis=0), small_total)

    shards = {
        "ab_in": (ab_w_in, 0, BF16), "small": (pack_small(conv_w, sc_conv_w, pool_w)[None], 0, F32),
        "ab_out": (ab_w_out, 0, BF16), "ff1_0": (ffn_w1, 0, BF16), "ff2_0": (ffn_w2, 0, BF16),
        "sc_in": (sc_w_in, 0, BF16), "sc_out": (sc_w_out, 0, BF16),
        "ff1_1": (ffn_w1, 1, BF16), "ff2_1": (ffn_w2, 1, BF16)}
    zones = {"ab_in": place_shard("place_ab_in", *shards["ab_in"])}
    (first_started,), token = copies_start("gather_start_ab_in", [[zones["ab_in"]]], _first_hop, 4)
    direct = ["small", "ab_out"]
    for nm in direct:
        zones[nm] = place_shard("place_" + nm, *shards[nm], deps=[token])
    started, token = copies_start("gather_start", [[zones[nm]] for nm in direct], _first_hop, 4, deps=[token])
    started = dict(zip(direct, started), ab_in=first_started)
    zones["ff1_0"] = place_shard("place_ff1_0", *shards["ff1_0"], deps=[token])
    (head,), token = copies_start("ring_start_ff1_0", [[zones["ff1_0"]]], _ring_hop1, 3, deps=[token])
    ring = {"ff1_0": head}

    ties = [0]

    def after(v, *deps):
        ties[0] += 1
        return tie(f"tie_{ties[0]}", v, *deps)

    def fetch_begin(nm, dep):
        (zone,) = copies_wait("gather_wait_" + nm, started[nm], _first_hop, dep)
        (hop,), tok = copies_start("forward_start_" + nm, [[zone]], _second_hop, 3)
        return hop, tok

    def fetch_end(nm, hop, dep):
        return copies_wait("forward_wait_" + nm, hop, _second_hop, dep)[0]

    def ring_step(tag, dep, *starts, place=()):
        place = list(place) + [nm for n, nm in starts if n == 1 and nm not in zones and nm not in place]
        before_waits = dep
        for nm in place:
            before_waits = zones[nm] = place_shard("place_" + nm, *shards[nm], deps=[before_waits])
        names, groups, hops, counts = [], [], [], []
        for n, nm in starts:
            if n == 1:
                groups.append([zones[nm]])
            elif n == 2:
                groups.append(copies_wait("ring1_wait_" + nm, ring[nm], _ring_hop1, before_waits))
            else:
                groups.append(copies_wait("ring2_wait_" + nm, ring[nm], _ring_hop2, before_waits))
            hop, n_copies = RING_HOPS[n - 1]
            names, hops, counts = names + [nm], hops + [hop], counts + [n_copies]
        begun, tok = copies_start("ring_start_" + tag, groups, hops, counts, deps=[dep])
        ring.update(zip(names, begun))
        return tok

    def ring_done(nm, dep):
        return copies_wait("ring3_wait_" + nm, ring[nm], _ring_hop3, dep)[0]

    relu = lambda r: jnp.maximum(r, 0.0)
    square = lambda a: a * a
    relu2_bwd = lambda r, a: r * (2.0 * a.astype(F32))

    def row(vec, l):
        return vec[l:l + 1]

    h0 = norm_pre("norm_pre", xs, after(row(mix_pre_g, 0), token))
    zones["ff2_0"] = place_shard("place_ff2_0", *shards["ff2_0"], deps=[h0])
    hop_small, _ = fetch_begin("small", zones["ff2_0"])
    hop_ab_in, tok = fetch_begin("ab_in", zones["ff2_0"])
    w_small = fetch_end("small", hop_small, tok)
    w_ab_in = fetch_end("ab_in", hop_ab_in, tok)
    w_conv = w_small[:, :r0].reshape(N_DEV, kw, -1).transpose(1, 0, 2).reshape(kw, d_conv)
    w_sconv = w_small[:, r0:r1].reshape(N_DEV, ks, -1).transpose(1, 0, 2).reshape(ks, d_short)
    w_pool = w_small[:, r1:r2].reshape(N_DEV, ng, -1, pg).transpose(1, 0, 2, 3).reshape(ng, pg, pg).astype(BF16)
    z0 = mm_nn_blocked("ab_in", h0, w_ab_in, out_dtype=BF16)
    hop, tok = fetch_begin("ab_out", z0)
    z0 = after(z0, tok)
    pooled, y0 = pool_fwd("pool_fwd", z0, w_pool, pool_scale, d_pool, d_pool + d_conv)
    cv = conv_fwd("conv_fwd", z0, w_conv, conv_b, d_pool, d_conv)
    y0 = ln_silu("ln_silu", cv, conv_ln_g, conv_ln_b, y0, d_pool // d_conv)
    w_ab_out = fetch_end("ab_out", hop, y0)

    def ffn_up(name, h, w, *starts):
        a = mm_nn_blocked(name, h, w, out_dtype=BF16, epilogue=relu, blocks=(0, UP_FIRST_BLOCKS))
        tok = ring_step(name, a, *starts)
        return mm_nn_blocked(name + "_rest", h, w, out_dtype=BF16, epilogue=relu,
                             blocks=(UP_FIRST_BLOCKS, N_DEV - UP_FIRST_BLOCKS), into=after(a, tok))

    tok = ring_step("a", w_ab_out, (2, "ff1_0"), (1, "ff2_0"))
    y0 = after(y0, tok)
    m0 = mm_nn("ab_out", y0, w_ab_out.reshape(d_pool + d_conv, d), out_dtype=F32)
    x1, h1 = post_pre("post_pre_0", xs, m0, row(mix_post_g, 0), row(ffn_pre_g, 0))
    tok = ring_step("b", h1, (3, "ff1_0"))
    w_ff1_0 = ring_done("ff1_0", tok)
    a0 = ffn_up("ffn0_up", h1, w_ff1_0, (2, "ff2_0"), (1, "sc_in"), (1, "sc_out"))
    tok = ring_step("c", a0, (3, "ff2_0"), place=["ff1_1"])
    w_ff2_0 = ring_done("ff2_0", tok).reshape(-1, d)
    f0 = mm_nn("ffn0_down", a0, w_ff2_0, out_dtype=F32, tk=2048, lhs_fn=square)
    tok = ring_step("d", f0, (2, "sc_in"), (2, "sc_out"), (1, "ff1_1"))
    f0 = after(f0, tok)
    x2, h2 = post_pre("post_pre_1", x1, f0, row(ffn_post_g, 0), row(mix_pre_g, 1))
    tok = ring_step("e", h2, (3, "sc_in"), place=["ff2_1"])
    w_sc_in = ring_done("sc_in", tok)
    z1 = mm_nn_blocked("sc_in", h2, w_sc_in, out_dtype=BF16)
    y1 = short_fwd("short_fwd", z1, w_sconv, d_short)
    tok = ring_step("f", y1, (3, "sc_out"), (2, "ff1_1"), (1, "ff2_1"))
    w_sc_out = ring_done("sc_out", tok).reshape(d_short, d)
    m1 = mm_nn("sc_out", y1, w_sc_out, out_dtype=F32)
    x3, h3 = post_pre("post_pre_2", x2, m1, row(mix_post_g, 1), row(ffn_pre_g, 1))
    tok = ring_step("g", h3, (3, "ff1_1"))
    w_ff1_1 = ring_done("ff1_1", tok)
    a1 = ffn_up("ffn1_up", h3, w_ff1_1, (2, "ff2_1"))
    tok = ring_step("h", a1, (3, "ff2_1"))
    w_ff2_1 = ring_done("ff2_1", tok).reshape(-1, d)
    f1 = mm_nn("ffn1_down", a1, w_ff2_1, out_dtype=F32, tk=2048, lhs_fn=square)
    dx4, df1, loss_part, dg_ffn_post1 = post_loss("post_loss", x3, f1, row(ffn_post_g, 1), target)

    red = {}

    def reduce_step(dep, begin=None, middle=None):
        tags, groups, hops, counts = [], [], [], []
        if begin is not None:
            tag, g = begin
            tags, groups = tags + [tag], groups + [[g, lax.empty((N_CHIP,) + g.shape[1:], g.dtype)]]
            hops, counts = hops + [_pair_hop], counts + [N_CHIP]
        if middle is not None:
            g, from_sibling = copies_wait("pair_wait_" + middle, red[middle], _pair_hop, dep)
            tags, groups = tags + [middle], groups + [list(pair_add("pair_add_" + middle, g, from_sibling))]
            hops, counts = hops + [_chip_hop], counts + [3]
        begun, tok = copies_start("reduce_start_" + "_".join(tags), groups, hops, counts, deps=[dep])
        red.update(zip(tags, begun))
        return tok

    def reduce_end(tag, dep):
        return copies_wait("chips_wait_" + tag, red[tag], _chip_hop, dep)[1]

    dpre, dw = mm_bwd_pair("ffn1_da_dw2", df1, w_ff2_1, a1, out_dtype=BF16, act_fn=square, epilogue=relu2_bwd)
    dpre = after(dpre, reduce_step(dpre, begin=("ff2_1", dw.reshape(N_DEV, -1, d))))
    dh3, dw = mm_bwd_pair_blocked("ffn1_dh_dw1", dpre, w_ff1_1, h3, out_dtype=BF16)
    dx3, dm1, dg_ffn_pre1, dg_mix_post1 = bwd_pre_post("bwd_3", dx4, x3, row(ffn_pre_g, 1), dh3, m1, row(mix_post_g, 1))
    dm1 = after(dm1, reduce_step(dm1, begin=("ff1_1", dw), middle="ff2_1"))

    dy1, dw = mm_bwd_pair("sc_dy_dwout", dm1, w_sc_out, y1, out_dtype=BF16)
    dy1 = after(dy1, reduce_step(dy1, begin=("sc_out", dw.reshape(N_DEV, -1, d)), middle="ff1_1"))
    dz1, dw_sconv = short_bwd("short_bwd", z1, dy1, w_sconv, d_short)
    dh2, dw = mm_bwd_pair_blocked("sc_dh_dwin", dz1, w_sc_in, h2, out_dtype=BF16)
    dx2, df0, dg_mix_pre1, dg_ffn_post0 = bwd_pre_post("bwd_2", dx3, x2, row(mix_pre_g, 1), dh2, f0, row(ffn_post_g, 0))
    df0 = after(df0, reduce_step(df0, begin=("sc_in", dw), middle="sc_out"))

    dpre, dw = mm_bwd_pair("ffn0_da_dw2", df0, w_ff2_0, a0, out_dtype=BF16, act_fn=square, epilogue=relu2_bwd)
    dpre = after(dpre, reduce_step(dpre, begin=("ff2_0", dw.reshape(N_DEV, -1, d)), middle="sc_in"))
    dh1, dw = mm_bwd_pair_blocked("ffn0_dh_dw1", dpre, w_ff1_0, h1, out_dtype=BF16)
    dh1 = after(dh1, reduce_step(dh1, middle="ff2_0"))
    dx1, dm0, dg_ffn_pre0, dg_mix_post0 = bwd_pre_post("bwd_1", dx2, x1, row(ffn_pre_g, 0), dh1, m0, row(mix_post_g, 0))
    dm0 = after(dm0, reduce_step(dm0, begin=("ff1_0", dw)))

    dy0, dw = mm_bwd_pair("ab_dy_dwout", dm0, w_ab_out.reshape(d_pool + d_conv, d), y0, out_dtype=BF16)
    dy0 = after(dy0, reduce_step(dy0, begin=("ab_out", dw.reshape(N_DEV, -1, d)), middle="ff1_0"))
    dcv, dg_ln_g, dg_ln_b = ln_silu_bwd("ln_silu_bwd", cv, conv_ln_g, conv_ln_b, dy0, d_pool // d_conv)
    dz0, dw_conv, dg_conv_b = conv_bwd("conv_bwd", z0, dcv, w_conv, d_pool, d_conv)
    dz0, dw_pool, dg_pool_scale = pool_bwd("pool_bwd", pooled, dy0, w_pool, pool_scale, dz0)
    small_parts = [
        dw_conv.reshape(kw, N_DEV, -1).transpose(1, 0, 2).reshape(N_DEV, -1, lanes),
        dw_sconv.reshape(ks, N_DEV, -1).transpose(1, 0, 2).reshape(N_DEV, -1, lanes),
        dw_pool.reshape(ng, N_DEV, pg // N_DEV, pg).transpose(1, 0, 2, 3).reshape(N_DEV, -1, lanes),
    ]
    small = jnp.pad(jnp.concatenate(small_parts, axis=1), ((0, 0), (0, small_total - r2), (0, 0)))
    dz0 = after(dz0, reduce_step(dz0, begin=("small", small), middle="ab_out"))
    dh0, dw = mm_bwd_pair_blocked("ab_dh_dwin", dz0, w_ab_in, h0, out_dtype=BF16)
    dh0 = after(dh0, reduce_step(dh0, begin=("ab_in", dw), middle="small"))
    grad_x, dg_mix_pre0 = bwd_pre_final("bwd_0", dx1, xs, row(mix_pre_g, 0), dh0)
    tok = reduce_step(grad_x, middle="ab_in")

    gains = [dg_mix_pre0, dg_mix_pre1, dg_mix_post0, dg_mix_post1, dg_ffn_pre0, dg_ffn_pre1, dg_ffn_post0, dg_ffn_post1]
    pieces = [(g, i, 0) for i, g in enumerate(gains)]
    rep_layout = [(0, 0), (2, 0), (4, 0), (6, 0)]
    offset = 0
    for g in (dg_pool_scale, dg_conv_b, dg_ln_g, dg_ln_b):
        at = (len(gains) + offset // d, offset % d)
        pieces.append((g, *at))
        rep_layout.append(at)
        offset += g.shape[1]
    loss_at = (len(gains) + -(-offset // d), 0)
    rep_zone = place_sheet("place_rep", pieces, loss_part, loss_at, 16, d)
    (rep_hop,), tok = copies_start("rep_start", [[rep_zone]], _first_hop, 4,
                                   deps=[tok])

    def upd(name, w, m, v, contribs):
        shape = w.shape
        flat2 = lambda a: a.reshape(-1, shape[-1])
        outs = adamw(name, flat2(w), flat2(m), flat2(v), contribs)
        return [o.reshape(shape) for o in outs]

    g_ff2 = [reduce_end("ff2_0", tok), reduce_end("ff2_1", tok)]
    o_ff2 = upd("adam_ffn_w2", ffn_w2, m_ffn_w2, v_ffn_w2, g_ff2)
    (rep_zone,) = copies_wait("rep_wait", rep_hop, _first_hop, o_ff2[0])
    (rep_hop,), _ = copies_start("rep_forward_start", [[rep_zone]], _second_hop, 3)
    g_ff1 = [reduce_end("ff1_0", o_ff2[0]), reduce_end("ff1_1", o_ff2[0])]
    o_ff1 = upd("adam_ffn_w1", ffn_w1, m_ffn_w1, v_ffn_w1, g_ff1)
    (rep_all,) = copies_wait("rep_forward_wait", rep_hop, _second_hop, o_ff1[0])
    loss_sum, *o_rep = adamw_replicated(
        "adam_replicated",
        [mix_pre_g, mix_post_g, ffn_pre_g, ffn_post_g, pool_scale, conv_b, conv_ln_g, conv_ln_b],
        [m_mix_pre_g, m_mix_post_g, m_ffn_pre_g, m_ffn_post_g, m_pool_scale, m_conv_b, m_conv_ln_g, m_conv_ln_b],
        [v_mix_pre_g, v_mix_post_g, v_ffn_pre_g, v_ffn_post_g, v_pool_scale, v_conv_b, v_conv_ln_g, v_conv_ln_b],
        rep_all, rep_layout, loss_at)
    loss = loss_sum[0, 0] * (0.5 / d)
    o_sc_out = upd("adam_sc_out", sc_w_out, m_sc_w_out, v_sc_w_out, [reduce_end("sc_out", o_ff1[0])])
    o_sc_in = upd("adam_sc_in", sc_w_in, m_sc_w_in, v_sc_w_in, [reduce_end("sc_in", o_sc_out[0])])
    o_ab_out = upd("adam_ab_out", ab_w_out, m_ab_w_out, v_ab_w_out, [reduce_end("ab_out", o_sc_in[0])])
    o_small = adamw("adam_small", pack_small(conv_w, sc_conv_w, pool_w), pack_small(m_conv_w, m_sc_conv_w, m_pool_w),
                    pack_small(v_conv_w, v_sc_conv_w, v_pool_w), [reduce_end("small", o_ab_out[0])])
    o_ab_in = upd("adam_ab_in", ab_w_in, m_ab_w_in, v_ab_w_in, [reduce_end("ab_in", o_small[0])])

    def unpack_small(o):
        return o[:r0].reshape(conv_w.shape), o[r0:r1].reshape(sc_conv_w.shape), o[r1:r2].reshape(pool_w.shape)

    results = []
    for kind in range(4):
        g_mix_pre, g_mix_post, g_ffn_pre, g_ffn_post, g_scale, g_conv_b, g_ln_g, g_ln_b = o_rep[kind::4]
        s_conv, s_sconv, s_pool = unpack_small(o_small[kind])
        results.append([
            g_mix_pre, g_mix_post, g_ffn_pre, g_ffn_post,
            o_ab_in[kind], s_pool, g_scale, s_conv, g_conv_b, g_ln_g, g_ln_b,
            o_ab_out[kind], o_sc_in[kind], s_sconv, o_sc_out[kind], o_ff1[kind], o_ff2[kind]])

    return (loss, grad_x[None], *results[0], *results[1], *results[2], *results[3])
```

```python
import jax
import jax.numpy as jnp
from jax import lax
from jax.experimental import pallas as pl
from jax.experimental.pallas import tpu as pltpu

F32 = jnp.float32
BF16 = jnp.bfloat16
MESH = pl.DeviceIdType.MESH
ANY = pl.BlockSpec(memory_space=pl.ANY)

NORM_EPS = 1e-6
POOL_WINDOWS = (2, 4, 8, 16)
ADAM_LR = 0.001
ADAM_B1 = 0.9
ADAM_B2 = 0.999
ADAM_EPS = 1e-08
ADAM_WD = 0.01
ADAM_STEP = 10

N_DEV = 8
VMEM_LIMIT = 56 * 1024 * 1024
PAIR_ADD_BLOCK = 1 << 20
MATMUL_ROWS = 2048
UP_FIRST_BLOCKS = 6
ROW_TILE = 256
CHANNEL_TILE = 256
TIME_CHUNK = 64
HALO = 32

NN = (((1,), (0,)), ((), ()))
NT = (((1,), (1,)), ((), ()))
TN = (((0,), (0,)), ((), ()))


def _params(sem):
    return pltpu.CompilerParams(dimension_semantics=sem, vmem_limit_bytes=VMEM_LIMIT)


def _place():
    x, y, c = lax.axis_index("x"), lax.axis_index("y"), lax.axis_index("c")
    return x, y, c


def _slot(px, py, pc):
    return 4 * px + 2 * py + pc


HBM = pl.BlockSpec(memory_space=pltpu.HBM)
SEM = pl.BlockSpec(memory_space=pltpu.SEMAPHORE)
EFFECT = pltpu.SideEffectType.DATAFLOW_SIDE_EFFECTING
TOKEN = jax.ShapeDtypeStruct((8, 128), F32)


def _in_hbm(a):
    return pltpu.with_memory_space_constraint(a, pltpu.HBM)


CHIPS = [(0, 0), (0, 1), (1, 0), (1, 1)]
N_CHIP = len(CHIPS)


def _chip(px, py):
    return 2 * px + py


def _first_hop(bufs, sends, recvs, waiting):
    (land,) = bufs
    x, y, c = _place()
    me = _slot(x, y, c)
    peers = [(x, y, 1 - c), (1 - x, y, c), (x, 1 - y, c), (1 - x, 1 - y, c)]
    return [pltpu.make_async_remote_copy(
        src_ref=land.at[me], dst_ref=land.at[_slot(*p) if waiting else me],
        send_sem=sends.at[k], recv_sem=recvs.at[k], device_id=p, device_id_type=MESH) for k, p in enumerate(peers)]


def _second_hop(bufs, sends, recvs, waiting):
    (land,) = bufs
    x, y, c = _place()
    return [pltpu.make_async_remote_copy(
        src_ref=land.at[_slot(px, py, c)], dst_ref=land.at[_slot(px, py, 1 - c if waiting else c)],
        send_sem=sends.at[k], recv_sem=recvs.at[k], device_id=(x, y, 1 - c), device_id_type=MESH)
        for k, (px, py) in enumerate([(1 - x, y), (x, 1 - y), (1 - x, 1 - y)])]


def _ring_hop1(bufs, sends, recvs, waiting):
    (land,) = bufs
    x, y, c = _place()
    me = _slot(x, y, c)
    peers = [(1 - x, y, c), (x, 1 - y, c), (x, y, 1 - c)]
    return [pltpu.make_async_remote_copy(
        src_ref=land.at[me], dst_ref=land.at[_slot(*p) if waiting else me],
        send_sem=sends.at[k], recv_sem=recvs.at[k], device_id=p, device_id_type=MESH) for k, p in enumerate(peers)]


def _ring_hop2(bufs, sends, recvs, waiting):
    (land,) = bufs
    x, y, c = _place()
    half = land.shape[1] // 2
    first, second = pl.ds(0, half), pl.ds(half, half)
    nx, ny, diag = _slot(1 - x, y, c), _slot(x, 1 - y, c), _slot(1 - x, 1 - y, c)
    plan = [
        (land.at[ny, first], land.at[diag, first], (1 - x, y, c)),
        (land.at[nx, second], land.at[diag, second], (x, 1 - y, c)),
        (land.at[nx], land.at[_slot(1 - x, y, 1 - c)], (x, y, 1 - c)),
        (land.at[ny], land.at[_slot(x, 1 - y, 1 - c)], (x, y, 1 - c))]
    return [pltpu.make_async_remote_copy(
        src_ref=src, dst_ref=mine if waiting else src, send_sem=sends.at[k], recv_sem=recvs.at[k],
        device_id=to, device_id_type=MESH) for k, (src, mine, to) in enumerate(plan)]


def _ring_hop3(bufs, sends, recvs, waiting):
    (land,) = bufs
    x, y, c = _place()
    return [pltpu.make_async_remote_copy(
        src_ref=land.at[_slot(1 - x, 1 - y, c)], dst_ref=land.at[_slot(1 - x, 1 - y, 1 - c if waiting else c)],
        send_sem=sends.at[0], recv_sem=recvs.at[0], device_id=(x, y, 1 - c), device_id_type=MESH)]


RING_HOPS = [(_ring_hop1, 3), (_ring_hop2, 4), (_ring_hop3, 1)]

def _pair_hop(bufs, sends, recvs, waiting):
    g, land = bufs
    x, y, c = _place()
    return [pltpu.make_async_remote_copy(
        src_ref=g.at[_slot(qx, qy, 1 - c)], dst_ref=land.at[q],
        send_sem=sends.at[q], recv_sem=recvs.at[q], device_id=(x, y, 1 - c), device_id_type=MESH)
        for q, (qx, qy) in enumerate(CHIPS)]


def _chip_hop(bufs, sends, recvs, waiting):
    p, land = bufs
    x, y, c = _place()
    return [pltpu.make_async_remote_copy(
        src_ref=p.at[_chip(px, py)], dst_ref=land.at[_chip(px, py) if waiting else _chip(x, y)],
        send_sem=sends.at[k], recv_sem=recvs.at[k], device_id=(px, py, c), device_id_type=MESH)
        for k, (px, py) in enumerate([(1 - x, y), (x, 1 - y), (1 - x, 1 - y)])]


def copies_start(name, groups, hop, n_copies, deps=()):
    flat = [b for grp in groups for b in grp]
    nb, ng = len(flat), len(groups)
    deps = list(deps)
    hops = list(hop) if isinstance(hop, (list, tuple)) else [hop] * ng
    counts = list(n_copies) if isinstance(n_copies, (list, tuple)) else [n_copies] * ng

    def body(*refs):
        ins, token = refs[:nb], refs[-1]
        sems = refs[nb + len(deps):nb + len(deps) + 2 * ng]
        i = 0
        for gi, grp in enumerate(groups):
            for cp in hops[gi](ins[i:i + len(grp)], sems[2 * gi], sems[2 * gi + 1], False):
                cp.start()
            i += len(grp)
        token[...] = jnp.zeros_like(token)

    outs = pl.pallas_call(
        body, name=name,
        out_shape=([pltpu.SemaphoreType.DMA((n,)) for n in counts for _ in range(2)]
                   + [pltpu.HBM(b.shape, b.dtype) for b in flat] + [TOKEN]),
        in_specs=[HBM] * nb + [ANY] * len(deps),
        out_specs=[SEM] * (2 * ng) + [HBM] * nb + [pl.BlockSpec(memory_space=pltpu.VMEM)],
        input_output_aliases={i: 2 * ng + i for i in range(nb)},
        compiler_params=pltpu.CompilerParams(has_side_effects=EFFECT),
    )(*[_in_hbm(b) for b in flat], *deps)
    started, i = [], 0
    for gi, grp in enumerate(groups):
        started.append((outs[2 * gi], outs[2 * gi + 1], list(outs[2 * ng + i:2 * ng + i + len(grp)])))
        i += len(grp)
    return started, outs[-1]


def copies_wait(name, started, hop, after):
    sends, recvs, bufs = started
    nb = len(bufs)

    def body(*refs):
        for cp in hop(refs[:nb], refs[nb], refs[nb + 1], True):
            cp.wait_send()
            cp.wait_recv()

    outs = pl.pallas_call(
        body, name=name,
        out_shape=[pltpu.HBM(b.shape, b.dtype) for b in bufs],
        in_specs=[HBM] * nb + [SEM, SEM, ANY], out_specs=[HBM] * nb,
        input_output_aliases={i: i for i in range(nb)},
        compiler_params=pltpu.CompilerParams(has_side_effects=EFFECT),
    )(*bufs, sends, recvs, after)
    return list(outs)


def place_shard(name, w, layer, dtype, deps=()):
    _, r, c = w.shape
    tr = _tile(r, 1024)
    x, y, core = _place()
    me = _slot(x, y, core).astype(jnp.int32).reshape(1)

    def body(me_ref, w_ref, *rest):
        rest[-1][...] = w_ref[...].astype(dtype)

    return pl.pallas_call(
        body, name=name,
        grid_spec=pltpu.PrefetchScalarGridSpec(
            num_scalar_prefetch=1, grid=(r // tr,),
            in_specs=[pl.BlockSpec((None, tr, c), lambda i, me_ref: (layer, i, 0))] + [ANY] * len(deps),
            out_specs=pl.BlockSpec((None, tr, c), lambda i, me_ref: (me_ref[0], i, 0))),
        out_shape=jax.ShapeDtypeStruct((N_DEV, r, c), dtype),
        compiler_params=_params(("parallel",)),
    )(me, w, *deps)


def place_sheet(name, pieces, total_of, total_at, rows, width):
    x, y, core = _place()
    me = _slot(x, y, core).astype(jnp.int32).reshape(1)

    def body(me_ref, *refs):
        o_ref = refs[-1]
        o_ref[...] = jnp.zeros_like(o_ref)
        for ref, (_, row, lane) in zip(refs, pieces):
            o_ref[row:row + 1, lane:lane + ref.shape[1]] = jnp.sum(ref[...], axis=0, keepdims=True)
        total = jnp.sum(jnp.sum(refs[len(pieces)][...], axis=0, keepdims=True), axis=1, keepdims=True)
        o_ref[total_at[0]:total_at[0] + 1, total_at[1]:total_at[1] + 128] = jnp.broadcast_to(total, (1, 128))

    arrays = [a for a, _, _ in pieces] + [total_of]
    return pl.pallas_call(
        body, name=name,
        grid_spec=pltpu.PrefetchScalarGridSpec(
            num_scalar_prefetch=1, grid=(1,),
            in_specs=[pl.BlockSpec(a.shape, lambda i, me_ref: (0, 0)) for a in arrays],
            out_specs=pl.BlockSpec((None, rows, width), lambda i, me_ref: (me_ref[0], 0, 0))),
        out_shape=jax.ShapeDtypeStruct((N_DEV, rows, width), F32),
    )(me, *arrays)


def tie(name, x, *deps):
    def body(*refs):
        del refs

    return pl.pallas_call(
        body, name=name, out_shape=jax.ShapeDtypeStruct(x.shape, x.dtype),
        in_specs=[ANY] * (1 + len(deps)), out_specs=ANY, input_output_aliases={0: 0},
    )(x, *deps)


def pair_add(name, g, from_sibling):
    _, r, c_dim = g.shape
    tr = r
    while tr * c_dim > PAIR_ADD_BLOCK and tr % 16 == 0:
        tr //= 2
    x, y, core = _place()
    where = jnp.stack([core, _chip(x, y)]).astype(jnp.int32)

    def body(where_ref, g_ref, s_ref, o_ref, zone_ref):
        total = (g_ref[...].astype(F32) + s_ref[...].astype(F32)).astype(o_ref.dtype)
        o_ref[...] = total

        @pl.when(pl.program_id(1) == where_ref[1])
        def _():
            zone_ref[...] = total

    blk = pl.BlockSpec((None, tr, c_dim), lambda i, q, where_ref: (q, i, 0))
    return pl.pallas_call(
        body, name=name,
        grid_spec=pltpu.PrefetchScalarGridSpec(
            num_scalar_prefetch=1, grid=(r // tr, N_CHIP),
            in_specs=[pl.BlockSpec((None, None, tr, c_dim), lambda i, q, where_ref: (q, where_ref[0], i, 0)), blk],
            out_specs=[blk, pl.BlockSpec((None, tr, c_dim), lambda i, q, where_ref: (where_ref[1], i, 0))]),
        out_shape=[jax.ShapeDtypeStruct((N_CHIP, r, c_dim), g.dtype)] * 2,
        compiler_params=_params(("parallel", "arbitrary")),
    )(where, g.reshape(N_CHIP, 2, r, c_dim), from_sibling)


def _matmul(name, lhs, rhs, *, out_shape, out_dtype, grid, lhs_spec, rhs_spec, out_spec, acc_shape,
            lhs_fn=None, epilogue=None, into=None):
    nk = grid[2]
    extra = [] if into is None else [into]

    def body(lhs_ref, rhs_ref, *rest):
        out_ref, scratch = rest[len(extra)], rest[len(extra) + 1:]

        def product():
            a = lhs_ref[...]
            if lhs_fn is not None:
                a = lhs_fn(a)
            return lax.dot_general(a, rhs_ref[...], NN, preferred_element_type=F32)

        def finish(r):
            if epilogue is not None:
                r = epilogue(r)
            out_ref[...] = r.astype(out_dtype)

        if nk == 1:
            finish(product())
        else:
            (acc_ref,) = scratch
            k = pl.program_id(2)

            @pl.when(k == 0)
            def _():
                acc_ref[...] = product()

            @pl.when(jnp.logical_and(k > 0, k < nk - 1))
            def _():
                acc_ref[...] += product()

            @pl.when(k == nk - 1)
            def _():
                finish(acc_ref[...] + product())

    return pl.pallas_call(
        body, name=name, grid=grid,
        out_shape=jax.ShapeDtypeStruct(out_shape, out_dtype),
        in_specs=[lhs_spec, rhs_spec] + [ANY] * len(extra), out_specs=out_spec,
        input_output_aliases={2: 0} if extra else {},
        scratch_shapes=[pltpu.VMEM(acc_shape, F32)] if nk > 1 else [],
        compiler_params=_params(("parallel", "parallel", "arbitrary")),
    )(lhs, rhs, *extra)


def _tile(n, want):
    return want if n % want == 0 else n


def mm_nn(name, x, w, *, out_dtype, tn=512, tk=None, lhs_fn=None, epilogue=None):
    t, kdim = x.shape
    n = w.shape[1]
    tm, tn = _tile(t, MATMUL_ROWS), _tile(n, tn)
    tk = kdim if tk is None else _tile(kdim, tk)
    return _matmul(
        name, x, w, out_shape=(t, n), out_dtype=out_dtype, grid=(t // tm, n // tn, kdim // tk),
        lhs_spec=pl.BlockSpec((tm, tk), lambda i, j, k: (i, k)),
        rhs_spec=pl.BlockSpec((tk, tn), lambda i, j, k: (k, j)),
        out_spec=pl.BlockSpec((tm, tn), lambda i, j, k: (i, j)),
        acc_shape=(tm, tn), lhs_fn=lhs_fn, epilogue=epilogue)


def mm_nn_blocked(name, x, w, *, out_dtype, epilogue=None, blocks=(0, N_DEV), into=None):
    t, kdim = x.shape
    nb = w.shape[2]
    tm = _tile(t, MATMUL_ROWS)
    tn = nb // 2 if nb >= 1024 else nb
    sub = nb // tn
    first, count = blocks
    return _matmul(
        name, x, w, out_shape=(t, N_DEV * nb), out_dtype=out_dtype, grid=(t // tm, count * sub, 1),
        lhs_spec=pl.BlockSpec((tm, kdim), lambda i, j, k: (i, k)),
        rhs_spec=pl.BlockSpec((None, kdim, tn), lambda i, j, k: (first + j // sub, k, j % sub)),
        out_spec=pl.BlockSpec((tm, tn), lambda i, j, k: (i, first * sub + j)),
        acc_shape=(tm, tn), epilogue=epilogue, into=into)


def mm_bwd_pair(name, dy, w, act, *, out_dtype, tile=512, act_fn=None, epilogue=None):
    t, n = dy.shape
    kdim = w.shape[0]
    tile = _tile(kdim, tile)

    def body(dy_ref, w_ref, act_ref, dx_ref, dw_ref):
        a = act_ref[...]
        dx = lax.dot_general(dy_ref[...], w_ref[...], NT, preferred_element_type=F32)
        if epilogue is not None:
            dx = epilogue(dx, a)
        dx_ref[...] = dx.astype(out_dtype)
        if act_fn is not None:
            a = act_fn(a)
        dw_ref[...] = lax.dot_general(a, dy_ref[...], TN, preferred_element_type=F32).astype(out_dtype)

    return pl.pallas_call(
        body, name=name, grid=(kdim // tile,),
        in_specs=[pl.BlockSpec((t, n), lambda j: (0, 0)), pl.BlockSpec((tile, n), lambda j: (j, 0)),
                  pl.BlockSpec((t, tile), lambda j: (0, j))],
        out_specs=[pl.BlockSpec((t, tile), lambda j: (0, j)), pl.BlockSpec((tile, n), lambda j: (j, 0))],
        out_shape=[jax.ShapeDtypeStruct((t, kdim), out_dtype), jax.ShapeDtypeStruct((kdim, n), out_dtype)],
        compiler_params=_params(("parallel",)),
    )(dy, w, act)


def mm_bwd_pair_blocked(name, dz, w, act, *, out_dtype, tile=1024):
    t = dz.shape[0]
    kdim, nb = w.shape[1], w.shape[2]
    tile = _tile(kdim, tile)

    def body(dz_ref, w_ref, act_ref, dx_ref, dw_ref, acc_ref):
        j = pl.program_id(1)
        dw_ref[...] = lax.dot_general(act_ref[...], dz_ref[...], TN, preferred_element_type=F32).astype(out_dtype)

        def product():
            return lax.dot_general(dz_ref[...], w_ref[...], NT, preferred_element_type=F32)

        @pl.when(j == 0)
        def _():
            acc_ref[...] = product()

        @pl.when(jnp.logical_and(j > 0, j < N_DEV - 1))
        def _():
            acc_ref[...] += product()

        @pl.when(j == N_DEV - 1)
        def _():
            dx_ref[...] = (acc_ref[...] + product()).astype(out_dtype)

    return pl.pallas_call(
        body, name=name, grid=(kdim // tile, N_DEV),
        in_specs=[pl.BlockSpec((t, nb), lambda i, j: (0, j)), pl.BlockSpec((None, tile, nb), lambda i, j: (j, i, 0)),
                  pl.BlockSpec((t, tile), lambda i, j: (0, i))],
        out_specs=[pl.BlockSpec((t, tile), lambda i, j: (0, i)),
                   pl.BlockSpec((None, tile, nb), lambda i, j: (j, i, 0))],
        out_shape=[jax.ShapeDtypeStruct((t, kdim), out_dtype), jax.ShapeDtypeStruct((N_DEV, kdim, nb), out_dtype)],
        scratch_shapes=[pltpu.VMEM((t, tile), F32)],
        compiler_params=_params(("parallel", "arbitrary")),
    )(dz, w, act)


def _rstd(v):
    return lax.rsqrt(jnp.mean(v * v, axis=-1, keepdims=True) + NORM_EPS)


def _rms_bwd(v, g, dy):
    r = _rstd(v)
    vhat = v * r
    dvh = dy * g
    dv = r * (dvh - vhat * jnp.mean(dvh * vhat, axis=-1, keepdims=True))
    return dv, dy * vhat


def _fold8(v):
    rows, n = v.shape
    return jnp.sum(v.reshape(rows // 8, 8, n), axis=0)


def _fold_lanes(v):
    out = v[:, 0:128]
    for i in range(1, v.shape[1] // 128):
        out = out + v[:, 128 * i:128 * (i + 1)]
    return out


def _accumulate(ref, v):
    i = pl.program_id(0)

    @pl.when(i == 0)
    def _():
        ref[...] = v

    @pl.when(i > 0)
    def _():
        ref[...] += v


def _row_call(body, name, t, ins, row_in, outs, acc_outs=(), tr=ROW_TILE):
    tr = _tile(t, tr)

    def in_spec(a, tiled):
        if isinstance(tiled, tuple):
            width, j = tiled
            return pl.BlockSpec((tr, width), lambda i: (i, j))
        return pl.BlockSpec((tr, a.shape[1]), lambda i: (i, 0)) if tiled else pl.BlockSpec(a.shape, lambda i: (0, 0))

    in_specs = [in_spec(a, tiled) for a, tiled in zip(ins, row_in)]
    out_specs = [pl.BlockSpec((tr, n), lambda i: (i, 0)) for n, _ in outs]
    out_specs += [pl.BlockSpec((8, n), lambda i: (0, 0)) for n in acc_outs]
    out_shape = [jax.ShapeDtypeStruct((t, n), dt) for n, dt in outs]
    out_shape += [jax.ShapeDtypeStruct((8, n), F32) for n in acc_outs]
    return pl.pallas_call(
        body, name=name, grid=(t // tr,), in_specs=in_specs, out_specs=out_specs, out_shape=out_shape,
        compiler_params=_params(("arbitrary",) if acc_outs else ("parallel",)),
    )(*ins)


def norm_pre(name, x, g):
    t, d = x.shape

    def body(x_ref, g_ref, h_ref):
        v = x_ref[...]
        h_ref[...] = (v * _rstd(v) * g_ref[...]).astype(BF16)

    return _row_call(body, name, t, [x, g], [True, False], [(d, BF16)])[0]


def post_pre(name, x, m, g_post, g_pre):
    t, d = x.shape

    def body(x_ref, m_ref, gp_ref, gn_ref, xo_ref, h_ref):
        mv = m_ref[...]
        xn = x_ref[...] + mv * _rstd(mv) * gp_ref[...]
        xo_ref[...] = xn
        h_ref[...] = (xn * _rstd(xn) * gn_ref[...]).astype(BF16)

    return _row_call(body, name, t, [x, m, g_post, g_pre], [True, True, False, False], [(d, F32), (d, BF16)])


def post_loss(name, x, f, g_post, target):
    t, d = x.shape

    def body(x_ref, f_ref, g_ref, t_ref, dx_ref, df_ref, loss_ref, dg_ref):
        fv = f_ref[...]
        g = g_ref[...]
        out = x_ref[...] + fv * _rstd(fv) * g
        err = out - t_ref[...]
        dx = err * (1.0 / d)
        dx_ref[...] = dx
        dfv, dg_rows = _rms_bwd(fv, g, dx)
        df_ref[...] = dfv.astype(BF16)
        _accumulate(loss_ref, _fold8(_fold_lanes(err * err)))
        _accumulate(dg_ref, _fold8(dg_rows))

    return _row_call(body, name, t, [x, f, g_post, target], [True, True, False, True],
                     [(d, F32), (d, BF16)], acc_outs=(128, d))


def bwd_pre_post(name, dx_out, x_in, g_pre, dh, f_prev, g_post_prev):
    t, d = x_in.shape

    def body(dxo_ref, x_ref, gpre_ref, dh_ref, f_ref, gpost_ref, dxi_ref, df_ref, dgpre_ref, dgpost_ref):
        dxv, dgpre_rows = _rms_bwd(x_ref[...], gpre_ref[...], dh_ref[...].astype(F32))
        dxi = dxo_ref[...] + dxv
        dxi_ref[...] = dxi
        dfv, dgpost_rows = _rms_bwd(f_ref[...], gpost_ref[...], dxi)
        df_ref[...] = dfv.astype(BF16)
        _accumulate(dgpre_ref, _fold8(dgpre_rows))
        _accumulate(dgpost_ref, _fold8(dgpost_rows))

    return _row_call(body, name, t, [dx_out, x_in, g_pre, dh, f_prev, g_post_prev],
                     [True, True, False, True, True, False], [(d, F32), (d, BF16)], acc_outs=(d, d))


def bwd_pre_final(name, dx_out, x_in, g_pre, dh):
    t, d = x_in.shape

    def body(dxo_ref, x_ref, gpre_ref, dh_ref, dxi_ref, dgpre_ref):
        dxv, dgpre_rows = _rms_bwd(x_ref[...], gpre_ref[...], dh_ref[...].astype(F32))
        dxi_ref[...] = dxo_ref[...] + dxv
        _accumulate(dgpre_ref, _fold8(dgpre_rows))

    return _row_call(body, name, t, [dx_out, x_in, g_pre, dh], [True, True, False, True], [(d, F32)], acc_outs=(d,))


def _layer_norm_parts(cv):
    mu = jnp.mean(cv, axis=-1, keepdims=True)
    xc = cv - mu
    rstd = lax.rsqrt(jnp.mean(xc * xc, axis=-1, keepdims=True) + NORM_EPS)
    return xc * rstd, rstd


def ln_silu(name, cv, g, b, y, y_block):
    t, n = cv.shape
    tr = _tile(t, ROW_TILE)

    def body(c_ref, g_ref, b_ref, y_in_ref, y_ref):
        chat, _ = _layer_norm_parts(c_ref[...])
        ln = chat * g_ref[...] + b_ref[...]
        y_ref[...] = (ln * jax.nn.sigmoid(ln)).astype(BF16)

    vec = pl.BlockSpec((1, n), lambda i: (0, 0))
    return pl.pallas_call(
        body, name=name, grid=(t // tr,),
        in_specs=[pl.BlockSpec((tr, n), lambda i: (i, 0)), vec, vec, ANY],
        out_specs=pl.BlockSpec((tr, n), lambda i: (i, y_block)),
        out_shape=jax.ShapeDtypeStruct(y.shape, y.dtype), input_output_aliases={3: 0},
        compiler_params=_params(("parallel",)),
    )(cv, g, b, y)


def ln_silu_bwd(name, cv, g, b, dy, dy_block):
    t, n = cv.shape

    def body(c_ref, g_ref, b_ref, dy_ref, dc_ref, dg_ref, db_ref):
        chat, rstd = _layer_norm_parts(c_ref[...])
        g = g_ref[...]
        ln = chat * g + b_ref[...]
        s = jax.nn.sigmoid(ln)
        dln = dy_ref[...].astype(F32) * (s * (1.0 + ln * (1.0 - s)))
        dchat = dln * g
        dc_ref[...] = rstd * (dchat - jnp.mean(dchat, axis=-1, keepdims=True)
                              - chat * jnp.mean(dchat * chat, axis=-1, keepdims=True))
        _accumulate(dg_ref, _fold8(dln * chat))
        _accumulate(db_ref, _fold8(dln))

    return _row_call(body, name, t, [cv, g, b, dy], [True, False, False, (n, dy_block)], [(n, F32)], acc_outs=(n, n))


def _chunks(t, fn, tc=TIME_CHUNK):
    tc = _tile(t, tc)

    def step(i, carry):
        fn(pl.multiple_of(i * tc, tc), tc)
        return carry

    lax.fori_loop(0, t // tc, step, 0)


def _rows_from(v, start, n):
    res = start % 8
    base = v if res == 0 else pltpu.roll(v, v.shape[0] - res, axis=0)
    return base[start - res:start - res + n, :]


def _shifted(window, offsets, tc):
    rows = window.shape[0]
    by_residue = {}
    for k, off in enumerate(offsets):
        by_residue.setdefault(off % 8, []).append((k, off))
    for res, taps in by_residue.items():
        base = window if res == 0 else pltpu.roll(window, rows - res, axis=0)
        for k, off in taps:
            yield k, base[off - res:off - res + tc, :]


def _taps(window, w_ref, offsets, tc, flip=False):
    acc = None
    for k, rows in _shifted(window, offsets, tc):
        kk = len(offsets) - 1 - k if flip else k
        term = w_ref[kk:kk + 1, :] * rows
        acc = term if acc is None else acc + term
    return acc


def _window_sums(win, tc, causal):
    sums = []
    cur, rows, step = win, tc + HALO, 1
    for _ in POOL_WINDOWS:
        rows -= 8
        if causal:
            cur = cur[8:8 + rows, :] + _rows_from(cur, 8 - step, rows)
            sums.append(cur[rows - tc:rows, :])
        else:
            cur = cur[0:rows, :] + _rows_from(cur, step, rows)
            sums.append(cur[0:tc, :])
        step *= 2
    return sums


def _pick(vals, g):
    out = vals[-1]
    for i in range(len(vals) - 2, -1, -1):
        out = jnp.where(g == i, vals[i], out)
    return out


def _pool_count(s, tc, g):
    t1 = (lax.broadcasted_iota(jnp.int32, (tc, 1), 0) + (s + 1)).astype(F32)
    width = _pick([float(w) for w in POOL_WINDOWS], g)
    return jnp.minimum(t1, width)


def pool_fwd(name, z, pool_w, pool_scale, d_pool, y_width):
    t = z.shape[0]
    ng, pg = pool_w.shape[0], pool_w.shape[1]

    def body(u_ref, w_ref, s_ref, pooled_ref, y_ref, pad):
        g = pl.program_id(0)
        pad[pl.ds(0, HALO), :] = jnp.zeros((HALO, pg), F32)

        def fill(s, tc):
            pad[pl.ds(HALO + s, tc), :] = u_ref[pl.ds(s, tc), :].astype(F32)

        def chunk(s, tc):
            win = pad[pl.ds(s, tc + HALO), :]
            total = _pick(_window_sums(win, tc, causal=True), g)
            pooled = total / _pool_count(s, tc, g) - win[HALO:HALO + tc, :]
            pooled_ref[pl.ds(s, tc), :] = pooled.astype(BF16)

        _chunks(t, fill)
        _chunks(t, chunk)
        mixed = jnp.dot(pooled_ref[...], w_ref[...], preferred_element_type=F32)
        y_ref[...] = (mixed * s_ref[...]).astype(BF16)

    col = pl.BlockSpec((t, pg), lambda g: (0, g))
    return pl.pallas_call(
        body, name=name, grid=(ng,),
        in_specs=[col, pl.BlockSpec((None, pg, pg), lambda g: (g, 0, 0)), pl.BlockSpec((1, pg), lambda g: (0, g))],
        out_specs=[col, col],
        out_shape=[jax.ShapeDtypeStruct((t, d_pool), BF16), jax.ShapeDtypeStruct((t, y_width), BF16)],
        scratch_shapes=[pltpu.VMEM((t + HALO, pg), F32)],
        compiler_params=_params(("parallel",)),
    )(z, pool_w, pool_scale)


def pool_bwd(name, pooled, dy, pool_w, pool_scale, dz):
    t, d_pool = pooled.shape
    ng, pg = pool_w.shape[0], pool_w.shape[1]

    def body(p_ref, dy_ref, w_ref, s_ref, dz_ref, du_ref, dw_ref, ds_ref, pad):
        g = pl.program_id(0)
        w = w_ref[...]
        dyv = dy_ref[...].astype(F32)
        mixed = jnp.dot(p_ref[...], w, preferred_element_type=F32)
        ds_ref[...] = jnp.sum(dyv * mixed, axis=0, keepdims=True)
        dmixed = (dyv * s_ref[...]).astype(BF16)
        dw_ref[...] = lax.dot_general(p_ref[...], dmixed, TN, preferred_element_type=F32)
        pad[...] = jnp.zeros((t + HALO, pg), F32)
        pad[pl.ds(0, t), :] = lax.dot_general(dmixed, w, NT, preferred_element_type=F32)

        def scale(s, tc):
            pad[pl.ds(s, tc), :] = pad[pl.ds(s, tc), :] / _pool_count(s, tc, g)

        def chunk(s, tc):
            win = pad[pl.ds(s, tc + HALO), :]
            total = _pick(_window_sums(win, tc, causal=False), g)
            du_ref[pl.ds(s, tc), :] = (total - win[0:tc, :] * _pool_count(s, tc, g)).astype(BF16)

        _chunks(t, scale)
        _chunks(t, chunk)

    col = pl.BlockSpec((t, pg), lambda g: (0, g))
    vec = pl.BlockSpec((1, pg), lambda g: (0, g))
    mat = pl.BlockSpec((None, pg, pg), lambda g: (g, 0, 0))
    return pl.pallas_call(
        body, name=name, grid=(ng,),
        in_specs=[col, col, mat, vec, ANY], out_specs=[col, mat, vec],
        out_shape=[jax.ShapeDtypeStruct(dz.shape, dz.dtype), jax.ShapeDtypeStruct((ng, pg, pg), F32),
                   jax.ShapeDtypeStruct((1, d_pool), F32)],
        input_output_aliases={4: 0},
        scratch_shapes=[pltpu.VMEM((t + HALO, pg), F32)],
        compiler_params=_params(("parallel",)),
    )(pooled, dy, pool_w, pool_scale, dz)


def conv_fwd(name, z, conv_w, conv_b, d_pool, d_conv):
    t = z.shape[0]
    kw = conv_w.shape[0]
    tc_ch = _tile(d_conv, CHANNEL_TILE)
    v0, g0 = d_pool // tc_ch, (d_pool + d_conv) // tc_ch

    def body(v_ref, g_ref, w_ref, b_ref, c_ref, pad):
        pad[pl.ds(0, HALO), :] = jnp.zeros((HALO, tc_ch), F32)

        def fill(s, tc):
            pad[pl.ds(HALO + s, tc), :] = v_ref[pl.ds(s, tc), :].astype(F32) * jax.nn.sigmoid(g_ref[pl.ds(s, tc), :].astype(F32))

        def chunk(s, tc):
            win = pad[pl.ds(s, tc + HALO), :]
            c_ref[pl.ds(s, tc), :] = _taps(win, w_ref, [HALO - (kw - 1) + k for k in range(kw)], tc) + b_ref[...]

        _chunks(t, fill)
        _chunks(t, chunk)

    return pl.pallas_call(
        body, name=name, grid=(d_conv // tc_ch,),
        in_specs=[pl.BlockSpec((t, tc_ch), lambda j: (0, v0 + j)), pl.BlockSpec((t, tc_ch), lambda j: (0, g0 + j)),
                  pl.BlockSpec((kw, tc_ch), lambda j: (0, j)), pl.BlockSpec((1, tc_ch), lambda j: (0, j))],
        out_specs=pl.BlockSpec((t, tc_ch), lambda j: (0, j)),
        out_shape=jax.ShapeDtypeStruct((t, d_conv), F32),
        scratch_shapes=[pltpu.VMEM((t + HALO, tc_ch), F32)],
        compiler_params=_params(("parallel",)),
    )(z, z, conv_w, conv_b)


def conv_bwd(name, z, dc, conv_w, d_pool, d_conv):
    t = z.shape[0]
    kw = conv_w.shape[0]
    tc_ch = _tile(d_conv, CHANNEL_TILE)
    v0, g0 = d_pool // tc_ch, (d_pool + d_conv) // tc_ch

    def body(v_ref, g_ref, dc_ref, w_ref, dz_ref, dw_ref, db_ref, pad_a, pad_dc, acc_w, acc_b, tiles, sems):
        j = pl.program_id(0)
        dv_ref, dg_ref = tiles.at[0], tiles.at[1]
        writes = [pltpu.make_async_copy(tiles.at[p], dz_ref.at[:, pl.ds((first + j) * tc_ch, tc_ch)], sems.at[p])
                  for p, first in enumerate([v0, g0])]

        def wait_writes():
            for cp in writes:
                cp.wait()

        pad_a[pl.ds(0, HALO), :] = jnp.zeros((HALO, tc_ch), F32)
        pad_dc[pl.ds(t, HALO), :] = jnp.zeros((HALO, tc_ch), F32)
        acc_w[...] = jnp.zeros_like(acc_w)
        acc_b[...] = jnp.zeros_like(acc_b)

        def fill(s, tc):
            pad_a[pl.ds(HALO + s, tc), :] = v_ref[pl.ds(s, tc), :].astype(F32) * jax.nn.sigmoid(g_ref[pl.ds(s, tc), :].astype(F32))
            pad_dc[pl.ds(s, tc), :] = dc_ref[pl.ds(s, tc), :]

        def chunk(s, tc):
            dcv = pad_dc[pl.ds(s, tc), :]
            win_a = pad_a[pl.ds(s, tc + HALO), :]
            for k, rows in _shifted(win_a, [HALO - (kw - 1) + k for k in range(kw)], tc):
                acc_w[pl.ds(8 * k, 8), :] += _fold8(dcv * rows)
            acc_b[...] += _fold8(dcv)
            da = _taps(pad_dc[pl.ds(s, tc + HALO), :], w_ref, list(range(kw)), tc, flip=True)
            vv = v_ref[pl.ds(s, tc), :].astype(F32)
            sg = jax.nn.sigmoid(g_ref[pl.ds(s, tc), :].astype(F32))
            dv_ref[pl.ds(s, tc), :] = (da * sg).astype(BF16)
            dg_ref[pl.ds(s, tc), :] = (da * vv * sg * (1.0 - sg)).astype(BF16)

        _chunks(t, fill)
        pl.when(j > 0)(wait_writes)
        _chunks(t, chunk)
        for cp in writes:
            cp.start()
        pl.when(j == n_tiles - 1)(wait_writes)
        for k in range(kw):
            dw_ref[k:k + 1, :] = jnp.sum(acc_w[pl.ds(8 * k, 8), :], axis=0, keepdims=True)
        db_ref[...] = jnp.sum(acc_b[...], axis=0, keepdims=True)

    n_tiles = d_conv // tc_ch
    return pl.pallas_call(
        body, name=name, grid=(n_tiles,),
        in_specs=[pl.BlockSpec((t, tc_ch), lambda j: (0, v0 + j)), pl.BlockSpec((t, tc_ch), lambda j: (0, g0 + j)),
                  pl.BlockSpec((t, tc_ch), lambda j: (0, j)), pl.BlockSpec((kw, tc_ch), lambda j: (0, j))],
        out_specs=[ANY, pl.BlockSpec((kw, tc_ch), lambda j: (0, j)), pl.BlockSpec((1, tc_ch), lambda j: (0, j))],
        out_shape=[jax.ShapeDtypeStruct((t, d_pool + 2 * d_conv), BF16),
                   jax.ShapeDtypeStruct((kw, d_conv), F32), jax.ShapeDtypeStruct((1, d_conv), F32)],
        scratch_shapes=[pltpu.VMEM((t + HALO, tc_ch), F32), pltpu.VMEM((t + HALO, tc_ch), F32),
                        pltpu.VMEM((8 * kw, tc_ch), F32), pltpu.VMEM((8, tc_ch), F32),
                        pltpu.VMEM((2, t, tc_ch), BF16), pltpu.SemaphoreType.DMA((2,))],
        compiler_params=_params(("arbitrary",)),
    )(z, z, dc, conv_w)


def short_fwd(name, z, conv_w, d_short):
    t = z.shape[0]
    kw = conv_w.shape[0]
    tc_ch = _tile(d_short, CHANNEL_TILE)
    nt = d_short // tc_ch

    def body(b_ref, c_ref, u_ref, w_ref, y_ref, pad):
        pad[pl.ds(0, HALO), :] = jnp.zeros((HALO, tc_ch), F32)

        def fill(s, tc):
            pad[pl.ds(HALO + s, tc), :] = c_ref[pl.ds(s, tc), :].astype(F32) * u_ref[pl.ds(s, tc), :].astype(F32)

        def chunk(s, tc):
            win = pad[pl.ds(s, tc + HALO), :]
            cq = _taps(win, w_ref, [HALO - (kw - 1) + k for k in range(kw)], tc)
            y_ref[pl.ds(s, tc), :] = (b_ref[pl.ds(s, tc), :].astype(F32) * cq).astype(BF16)

        _chunks(t, fill)
        _chunks(t, chunk)

    return pl.pallas_call(
        body, name=name, grid=(nt,),
        in_specs=[pl.BlockSpec((t, tc_ch), lambda j: (0, j)), pl.BlockSpec((t, tc_ch), lambda j: (0, nt + j)),
                  pl.BlockSpec((t, tc_ch), lambda j: (0, 2 * nt + j)), pl.BlockSpec((kw, tc_ch), lambda j: (0, j))],
        out_specs=pl.BlockSpec((t, tc_ch), lambda j: (0, j)),
        out_shape=jax.ShapeDtypeStruct((t, d_short), BF16),
        scratch_shapes=[pltpu.VMEM((t + HALO, tc_ch), F32)],
        compiler_params=_params(("parallel",)),
    )(z, z, z, conv_w)


def short_bwd(name, z, dy, conv_w, d_short):
    t = z.shape[0]
    kw = conv_w.shape[0]
    tc_ch = _tile(d_short, CHANNEL_TILE)
    nt = d_short // tc_ch

    def body(b_ref, c_ref, u_ref, dy_ref, w_ref, dz_ref, dw_ref, pad_q, pad_dcq, acc_w, tiles, sems):
        j = pl.program_id(0)
        db_ref, dcg_ref, du_ref = tiles.at[0], tiles.at[1], tiles.at[2]
        writes = [pltpu.make_async_copy(tiles.at[p], dz_ref.at[:, pl.ds((p * nt + j) * tc_ch, tc_ch)], sems.at[p])
                  for p in range(3)]

        def wait_writes():
            for cp in writes:
                cp.wait()

        pad_q[pl.ds(0, HALO), :] = jnp.zeros((HALO, tc_ch), F32)
        pad_dcq[pl.ds(t, HALO), :] = jnp.zeros((HALO, tc_ch), F32)
        acc_w[...] = jnp.zeros_like(acc_w)

        def fill(s, tc):
            rows = pl.ds(s, tc)
            pad_q[pl.ds(HALO + s, tc), :] = c_ref[rows, :].astype(F32) * u_ref[rows, :].astype(F32)
            pad_dcq[rows, :] = dy_ref[rows, :].astype(F32) * b_ref[rows, :].astype(F32)

        def chunk(s, tc):
            rows = pl.ds(s, tc)
            win_q = pad_q[pl.ds(s, tc + HALO), :]
            dcq = pad_dcq[rows, :]
            cq = None
            for k, shifted in _shifted(win_q, [HALO - (kw - 1) + k for k in range(kw)], tc):
                acc_w[pl.ds(8 * k, 8), :] += _fold8(dcq * shifted)
                term = w_ref[k:k + 1, :] * shifted
                cq = term if cq is None else cq + term
            db_ref[rows, :] = (dy_ref[rows, :].astype(F32) * cq).astype(BF16)
            dq = _taps(pad_dcq[pl.ds(s, tc + HALO), :], w_ref, list(range(kw)), tc, flip=True)
            dcg_ref[rows, :] = (dq * u_ref[rows, :].astype(F32)).astype(BF16)
            du_ref[rows, :] = (dq * c_ref[rows, :].astype(F32)).astype(BF16)

        _chunks(t, fill)
        pl.when(j > 0)(wait_writes)
        _chunks(t, chunk)
        for cp in writes:
            cp.start()
        pl.when(j == nt - 1)(wait_writes)
        for k in range(kw):
            dw_ref[k:k + 1, :] = jnp.sum(acc_w[pl.ds(8 * k, 8), :], axis=0, keepdims=True)

    zspec = [pl.BlockSpec((t, tc_ch), lambda j, o=o: (0, o * nt + j)) for o in range(3)]
    return pl.pallas_call(
        body, name=name, grid=(nt,),
        in_specs=[*zspec, pl.BlockSpec((t, tc_ch), lambda j: (0, j)), pl.BlockSpec((kw, tc_ch), lambda j: (0, j))],
        out_specs=[ANY, pl.BlockSpec((kw, tc_ch), lambda j: (0, j))],
        out_shape=[jax.ShapeDtypeStruct((t, 3 * d_short), BF16), jax.ShapeDtypeStruct((kw, d_short), F32)],
        scratch_shapes=[pltpu.VMEM((t + HALO, tc_ch), F32), pltpu.VMEM((t + HALO, tc_ch), F32),
                        pltpu.VMEM((8 * kw, tc_ch), F32), pltpu.VMEM((3, t, tc_ch), BF16),
                        pltpu.SemaphoreType.DMA((3,))],
        compiler_params=_params(("arbitrary",)),
    )(z, z, z, dy, conv_w)


def _adamw_update(w, m, v, g):
    nm = ADAM_B1 * m + (1.0 - ADAM_B1) * g
    nv = ADAM_B2 * v + (1.0 - ADAM_B2) * (g * g)
    m_hat = nm / (1.0 - ADAM_B1 ** ADAM_STEP)
    v_hat = nv / (1.0 - ADAM_B2 ** ADAM_STEP)
    return -ADAM_LR * (m_hat / (jnp.sqrt(v_hat) + ADAM_EPS) + ADAM_WD * w), nm, nv


def adamw_replicated(name, params, first_moments, second_moments, contributions, layout, scalar_at):
    n = len(params)
    n_slots = contributions.shape[0]

    def total(c_ref, row, lane, rows, lanes):
        acc = c_ref[0, row:row + rows, lane:lane + lanes]
        for slot in range(1, n_slots):
            acc = acc + c_ref[slot, row:row + rows, lane:lane + lanes]
        return acc

    def body(*refs):
        ws, ms, vs, c_ref = refs[:n], refs[n:2 * n], refs[2 * n:3 * n], refs[3 * n]
        outs = refs[3 * n + 1:]
        outs[0][...] = total(c_ref, *scalar_at, 1, 128)
        for i, (row, lane) in enumerate(layout):
            g = total(c_ref, row, lane, *params[i].shape)
            grad_ref, delta_ref, nm_ref, nv_ref = outs[1 + 4 * i:5 + 4 * i]
            grad_ref[...] = g
            delta_ref[...], nm_ref[...], nv_ref[...] = _adamw_update(ws[i][...], ms[i][...], vs[i][...], g)

    out_shape = [jax.ShapeDtypeStruct((1, 128), F32)]
    for p in params:
        out_shape += [jax.ShapeDtypeStruct(p.shape, F32)] * 4
    return pl.pallas_call(body, name=name, out_shape=out_shape)(*params, *first_moments, *second_moments, contributions)


def adamw(name, w, m, v, contributions):
    r, c = w.shape
    nc = len(contributions)
    n_slots = contributions[0].shape[0]
    tr = 256 if c <= 1024 else 128
    if any(a.shape[1] % tr for a in contributions):
        assert nc == 1
        tr = r
    tiles = [a.shape[1] // tr for a in contributions]
    first = [sum(tiles[:j]) for j in range(nc)]

    def body(w_ref, m_ref, v_ref, *rest):
        g_refs, (grad_ref, delta_ref, nm_ref, nv_ref) = rest[:nc], rest[nc:]
        i = pl.program_id(0)
        g = None
        for j, g_ref in enumerate(g_refs):
            s = g_ref[0].astype(F32)
            for slot in range(1, n_slots):
                s = s + g_ref[slot].astype(F32)
            g = s if g is None else jnp.where(i >= first[j], s, g)
        grad_ref[...] = g
        delta_ref[...], nm_ref[...], nv_ref[...] = _adamw_update(w_ref[...], m_ref[...], v_ref[...], g)

    blk = pl.BlockSpec((tr, c), lambda i: (i, 0))
    g_specs = [pl.BlockSpec((n_slots, tr, c), lambda i, j=j: (0, jnp.clip(i - first[j], 0, tiles[j] - 1), 0))
               for j in range(nc)]
    return pl.pallas_call(
        body, name=name, grid=(r // tr,),
        in_specs=[blk, blk, blk, *g_specs],
        out_specs=[blk] * 4, out_shape=[jax.ShapeDtypeStruct((r, c), F32)] * 4,
        compiler_params=_params(("parallel",)),
    )(w, m, v, *contributions)


def _pad_rows(a, rows):
    return jnp.pad(a, ((0, rows - a.shape[0]), (0, 0)))


def kernel(x, mix_pre_g, mix_post_g, ffn_pre_g, ffn_post_g, ab_w_in, pool_w, pool_scale, conv_w, conv_b, conv_ln_g, conv_ln_b, ab_w_out, sc_w_in, sc_conv_w, sc_w_out, ffn_w1, ffn_w2, loss_target, m_mix_pre_g, m_mix_post_g, m_ffn_pre_g, m_ffn_post_g, m_ab_w_in, m_pool_w, m_pool_scale, m_conv_w, m_conv_b, m_conv_ln_g, m_conv_ln_b, m_ab_w_out, m_sc_w_in, m_sc_conv_w, m_sc_w_out, m_ffn_w1, m_ffn_w2, v_mix_pre_g, v_mix_post_g, v_ffn_pre_g, v_ffn_post_g, v_ab_w_in, v_pool_w, v_pool_scale, v_conv_w, v_conv_b, v_conv_ln_g, v_conv_ln_b, v_ab_w_out, v_sc_w_in, v_sc_conv_w, v_sc_w_out, v_ffn_w1, v_ffn_w2):
    t, d = x.shape[1], x.shape[2]
    d_pool = pool_scale.shape[1]
    d_conv = conv_b.shape[1]
    d_short = d
    ng, pg = pool_w.shape[1], pool_w.shape[3]
    kw, ks = conv_w.shape[1], sc_conv_w.shape[1]
    nb_ab, nb_sc, nb_ff = ab_w_in.shape[2], sc_w_in.shape[2], ffn_w1.shape[2]

    xs = x[0]
    target = loss_target[0]

    lanes = min(128, d_conv // N_DEV)
    small_rows = [kw * (d_conv // N_DEV) // lanes, ks * (d_short // N_DEV) // lanes, ng * (pg // N_DEV) * pg // lanes]
    small_total = -(-sum(small_rows) // 8) * 8
    r0, r1, r2 = small_rows[0], small_rows[0] + small_rows[1], sum(small_rows)

    def pack_small(a_conv, a_sconv, a_pool):
        parts = [a_conv[0].reshape(-1, lanes), a_sconv[0].reshape(-1, lanes), a_pool[0].reshape(-1, lanes)]
        return _pad_rows(jnp.concatenate(parts, axis=0), small_total)

    shards = {
        "ab_in": (ab_w_in, 0, BF16), "small": (pack_small(conv_w, sc_conv_w, pool_w)[None], 0, F32),
        "ab_out": (ab_w_out, 0, BF16), "ff1_0": (ffn_w1, 0, BF16), "ff2_0": (ffn_w2, 0, BF16),
        "sc_in": (sc_w_in, 0, BF16), "sc_out": (sc_w_out, 0, BF16),
        "ff1_1": (ffn_w1, 1, BF16), "ff2_1": (ffn_w2, 1, BF16)}
    zones = {"ab_in": place_shard("place_ab_in", *shards["ab_in"])}
    (first_started,), token = copies_start("gather_start_ab_in", [[zones["ab_in"]]], _first_hop, 4)
    direct = ["small", "ab_out"]
    for nm in direct:
        zones[nm] = place_shard("place_" + nm, *shards[nm], deps=[token])
    started, token = copies_start("gather_start", [[zones[nm]] for nm in direct], _first_hop, 4, deps=[token])
    started = dict(zip(direct, started), ab_in=first_started)
    zones["ff1_0"] = place_shard("place_ff1_0", *shards["ff1_0"], deps=[token])
    (head,), token = copies_start("ring_start_ff1_0", [[zones["ff1_0"]]], _ring_hop1, 3, deps=[token])
    ring = {"ff1_0": head}

    ties = [0]

    def after(v, *deps):
        ties[0] += 1
        return tie(f"tie_{ties[0]}", v, *deps)

    def fetch_begin(nm, dep):
        (zone,) = copies_wait("gather_wait_" + nm, started[nm], _first_hop, dep)
        (hop,), tok = copies_start("forward_start_" + nm, [[zone]], _second_hop, 3)
        return hop, tok

    def fetch_end(nm, hop, dep):
        return copies_wait("forward_wait_" + nm, hop, _second_hop, dep)[0]

    def ring_step(tag, dep, *starts):
        names, groups, hops, counts = [], [], [], []
        for n, nm in starts:
            if n == 1:
                groups.append([zones[nm]])
            elif n == 2:
                groups.append(copies_wait("ring1_wait_" + nm, ring[nm], _ring_hop1, dep))
            else:
                groups.append(copies_wait("ring2_wait_" + nm, ring[nm], _ring_hop2, dep))
            hop, n_copies = RING_HOPS[n - 1]
            names, hops, counts = names + [nm], hops + [hop], counts + [n_copies]
        begun, tok = copies_start("ring_start_" + tag, groups, hops, counts, deps=[dep])
        ring.update(zip(names, begun))
        return tok

    def ring_done(nm, dep):
        return copies_wait("ring3_wait_" + nm, ring[nm], _ring_hop3, dep)[0]

    relu = lambda r: jnp.maximum(r, 0.0)
    square = lambda a: a * a
    relu2_bwd = lambda r, a: r * (2.0 * a.astype(F32))

    def row(vec, l):
        return vec[l:l + 1]

    h0 = norm_pre("norm_pre", xs, after(row(mix_pre_g, 0), token))
    before_waits = h0
    for nm in ["ff2_0", "sc_in", "sc_out", "ff1_1", "ff2_1"]:
        before_waits = zones[nm] = place_shard("place_" + nm, *shards[nm], deps=[before_waits])
    hop_small, _ = fetch_begin("small", before_waits)
    hop_ab_in, tok = fetch_begin("ab_in", before_waits)
    w_small = fetch_end("small", hop_small, tok)
    w_ab_in = fetch_end("ab_in", hop_ab_in, tok)
    w_conv = w_small[:, :r0].reshape(N_DEV, kw, -1).transpose(1, 0, 2).reshape(kw, d_conv)
    w_sconv = w_small[:, r0:r1].reshape(N_DEV, ks, -1).transpose(1, 0, 2).reshape(ks, d_short)
    w_pool = w_small[:, r1:r2].reshape(N_DEV, ng, -1, pg).transpose(1, 0, 2, 3).reshape(ng, pg, pg).astype(BF16)
    z0 = mm_nn_blocked("ab_in", h0, w_ab_in, out_dtype=BF16)
    hop, tok = fetch_begin("ab_out", z0)
    z0 = after(z0, tok)
    pooled, y0 = pool_fwd("pool_fwd", z0, w_pool, pool_scale, d_pool, d_pool + d_conv)
    cv = conv_fwd("conv_fwd", z0, w_conv, conv_b, d_pool, d_conv)
    y0 = ln_silu("ln_silu", cv, conv_ln_g, conv_ln_b, y0, d_pool // d_conv)
    w_ab_out = fetch_end("ab_out", hop, y0)

    def ffn_up(name, h, w, *starts):
        a = mm_nn_blocked(name, h, w, out_dtype=BF16, epilogue=relu, blocks=(0, UP_FIRST_BLOCKS))
        tok = ring_step(name, a, *starts)
        return mm_nn_blocked(name + "_rest", h, w, out_dtype=BF16, epilogue=relu,
                             blocks=(UP_FIRST_BLOCKS, N_DEV - UP_FIRST_BLOCKS), into=after(a, tok))

    tok = ring_step("a", w_ab_out, (2, "ff1_0"), (1, "ff2_0"))
    y0 = after(y0, tok)
    m0 = mm_nn("ab_out", y0, w_ab_out.reshape(d_pool + d_conv, d), out_dtype=F32)
    x1, h1 = post_pre("post_pre_0", xs, m0, row(mix_post_g, 0), row(ffn_pre_g, 0))
    tok = ring_step("b", h1, (3, "ff1_0"))
    w_ff1_0 = ring_done("ff1_0", tok)
    a0 = ffn_up("ffn0_up", h1, w_ff1_0, (2, "ff2_0"), (1, "sc_in"), (1, "sc_out"))
    tok = ring_step("c", a0, (3, "ff2_0"))
    w_ff2_0 = ring_done("ff2_0", tok).reshape(-1, d)
    f0 = mm_nn("ffn0_down", a0, w_ff2_0, out_dtype=F32, tk=2048, lhs_fn=square)
    tok = ring_step("d", f0, (2, "sc_in"), (2, "sc_out"), (1, "ff1_1"))
    f0 = after(f0, tok)
    x2, h2 = post_pre("post_pre_1", x1, f0, row(ffn_post_g, 0), row(mix_pre_g, 1))
    tok = ring_step("e", h2, (3, "sc_in"))
    w_sc_in = ring_done("sc_in", tok)
    z1 = mm_nn_blocked("sc_in", h2, w_sc_in, out_dtype=BF16)
    y1 = short_fwd("short_fwd", z1, w_sconv, d_short)
    tok = ring_step("f", y1, (3, "sc_out"), (2, "ff1_1"), (1, "ff2_1"))
    w_sc_out = ring_done("sc_out", tok).reshape(d_short, d)
    m1 = mm_nn("sc_out", y1, w_sc_out, out_dtype=F32)
    x3, h3 = post_pre("post_pre_2", x2, m1, row(mix_post_g, 1), row(ffn_pre_g, 1))
    tok = ring_step("g", h3, (3, "ff1_1"))
    w_ff1_1 = ring_done("ff1_1", tok)
    a1 = ffn_up("ffn1_up", h3, w_ff1_1, (2, "ff2_1"))
    tok = ring_step("h", a1, (3, "ff2_1"))
    w_ff2_1 = ring_done("ff2_1", tok).reshape(-1, d)
    f1 = mm_nn("ffn1_down", a1, w_ff2_1, out_dtype=F32, tk=2048, lhs_fn=square)
    dx4, df1, loss_part, dg_ffn_post1 = post_loss("post_loss", x3, f1, row(ffn_post_g, 1), target)

    red = {}

    def reduce_step(dep, begin=None, middle=None):
        tags, groups, hops, counts = [], [], [], []
        if begin is not None:
            tag, g = begin
            tags, groups = tags + [tag], groups + [[g, lax.empty((N_CHIP,) + g.shape[1:], g.dtype)]]
            hops, counts = hops + [_pair_hop], counts + [N_CHIP]
        if middle is not None:
            g, from_sibling = copies_wait("pair_wait_" + middle, red[middle], _pair_hop, dep)
            tags, groups = tags + [middle], groups + [list(pair_add("pair_add_" + middle, g, from_sibling))]
            hops, counts = hops + [_chip_hop], counts + [3]
        begun, tok = copies_start("reduce_start_" + "_".join(tags), groups, hops, counts, deps=[dep])
        red.update(zip(tags, begun))
        return tok

    def reduce_end(tag, dep):
        return copies_wait("chips_wait_" + tag, red[tag], _chip_hop, dep)[1]

    dpre, dw = mm_bwd_pair("ffn1_da_dw2", df1, w_ff2_1, a1, out_dtype=BF16, act_fn=square, epilogue=relu2_bwd)
    dpre = after(dpre, reduce_step(dpre, begin=("ff2_1", dw.reshape(N_DEV, -1, d))))
    dh3, dw = mm_bwd_pair_blocked("ffn1_dh_dw1", dpre, w_ff1_1, h3, out_dtype=BF16)
    dx3, dm1, dg_ffn_pre1, dg_mix_post1 = bwd_pre_post("bwd_3", dx4, x3, row(ffn_pre_g, 1), dh3, m1, row(mix_post_g, 1))
    dm1 = after(dm1, reduce_step(dm1, begin=("ff1_1", dw), middle="ff2_1"))

    dy1, dw = mm_bwd_pair("sc_dy_dwout", dm1, w_sc_out, y1, out_dtype=BF16)
    dy1 = after(dy1, reduce_step(dy1, begin=("sc_out", dw.reshape(N_DEV, -1, d)), middle="ff1_1"))
    dz1, dw_sconv = short_bwd("short_bwd", z1, dy1, w_sconv, d_short)
    dh2, dw = mm_bwd_pair_blocked("sc_dh_dwin", dz1, w_sc_in, h2, out_dtype=BF16)
    dx2, df0, dg_mix_pre1, dg_ffn_post0 = bwd_pre_post("bwd_2", dx3, x2, row(mix_pre_g, 1), dh2, f0, row(ffn_post_g, 0))
    df0 = after(df0, reduce_step(df0, begin=("sc_in", dw), middle="sc_out"))

    dpre, dw = mm_bwd_pair("ffn0_da_dw2", df0, w_ff2_0, a0, out_dtype=BF16, act_fn=square, epilogue=relu2_bwd)
    dpre = after(dpre, reduce_step(dpre, begin=("ff2_0", dw.reshape(N_DEV, -1, d)), middle="sc_in"))
    dh1, dw = mm_bwd_pair_blocked("ffn0_dh_dw1", dpre, w_ff1_0, h1, out_dtype=BF16)
    dh1 = after(dh1, reduce_step(dh1, middle="ff2_0"))
    dx1, dm0, dg_ffn_pre0, dg_mix_post0 = bwd_pre_post("bwd_1", dx2, x1, row(ffn_pre_g, 0), dh1, m0, row(mix_post_g, 0))
    dm0 = after(dm0, reduce_step(dm0, begin=("ff1_0", dw)))

    dy0, dw = mm_bwd_pair("ab_dy_dwout", dm0, w_ab_out.reshape(d_pool + d_conv, d), y0, out_dtype=BF16)
    dy0 = after(dy0, reduce_step(dy0, begin=("ab_out", dw.reshape(N_DEV, -1, d)), middle="ff1_0"))
    dcv, dg_ln_g, dg_ln_b = ln_silu_bwd("ln_silu_bwd", cv, conv_ln_g, conv_ln_b, dy0, d_pool // d_conv)
    dz0, dw_conv, dg_conv_b = conv_bwd("conv_bwd", z0, dcv, w_conv, d_pool, d_conv)
    dz0, dw_pool, dg_pool_scale = pool_bwd("pool_bwd", pooled, dy0, w_pool, pool_scale, dz0)
    small_parts = [
        dw_conv.reshape(kw, N_DEV, -1).transpose(1, 0, 2).reshape(N_DEV, -1, lanes),
        dw_sconv.reshape(ks, N_DEV, -1).transpose(1, 0, 2).reshape(N_DEV, -1, lanes),
        dw_pool.reshape(ng, N_DEV, pg // N_DEV, pg).transpose(1, 0, 2, 3).reshape(N_DEV, -1, lanes),
    ]
    small = jnp.pad(jnp.concatenate(small_parts, axis=1), ((0, 0), (0, small_total - r2), (0, 0)))
    dz0 = after(dz0, reduce_step(dz0, begin=("small", small), middle="ab_out"))
    dh0, dw = mm_bwd_pair_blocked("ab_dh_dwin", dz0, w_ab_in, h0, out_dtype=BF16)
    dh0 = after(dh0, reduce_step(dh0, begin=("ab_in", dw), middle="small"))
    grad_x, dg_mix_pre0 = bwd_pre_final("bwd_0", dx1, xs, row(mix_pre_g, 0), dh0)
    tok = reduce_step(grad_x, middle="ab_in")

    gains = [dg_mix_pre0, dg_mix_pre1, dg_mix_post0, dg_mix_post1, dg_ffn_pre0, dg_ffn_pre1, dg_ffn_post0, dg_ffn_post1]
    pieces = [(g, i, 0) for i, g in enumerate(gains)]
    rep_layout = [(0, 0), (2, 0), (4, 0), (6, 0)]
    offset = 0
    for g in (dg_pool_scale, dg_conv_b, dg_ln_g, dg_ln_b):
        at = (len(gains) + offset // d, offset % d)
        pieces.append((g, *at))
        rep_layout.append(at)
        offset += g.shape[1]
    loss_at = (len(gains) + -(-offset // d), 0)
    rep_zone = place_sheet("place_rep", pieces, loss_part, loss_at, 16, d)
    (rep_hop,), tok = copies_start("rep_start", [[rep_zone]], _first_hop, 4,
                                   deps=[tok])

    def upd(name, w, m, v, contribs):
        shape = w.shape
        flat2 = lambda a: a.reshape(-1, shape[-1])
        outs = adamw(name, flat2(w), flat2(m), flat2(v), contribs)
        return [o.reshape(shape) for o in outs]

    g_ff2 = [reduce_end("ff2_0", tok), reduce_end("ff2_1", tok)]
    o_ff2 = upd("adam_ffn_w2", ffn_w2, m_ffn_w2, v_ffn_w2, g_ff2)
    (rep_zone,) = copies_wait("rep_wait", rep_hop, _first_hop, o_ff2[0])
    (rep_hop,), _ = copies_start("rep_forward_start", [[rep_zone]], _second_hop, 3)
    g_ff1 = [reduce_end("ff1_0", o_ff2[0]), reduce_end("ff1_1", o_ff2[0])]
    o_ff1 = upd("adam_ffn_w1", ffn_w1, m_ffn_w1, v_ffn_w1, g_ff1)
    (rep_all,) = copies_wait("rep_forward_wait", rep_hop, _second_hop, o_ff1[0])
    loss_sum, *o_rep = adamw_replicated(
        "adam_replicated",
        [mix_pre_g, mix_post_g, ffn_pre_g, ffn_post_g, pool_scale, conv_b, conv_ln_g, conv_ln_b],
        [m_mix_pre_g, m_mix_post_g, m_ffn_pre_g, m_ffn_post_g, m_pool_scale, m_conv_b, m_conv_ln_g, m_conv_ln_b],
        [v_mix_pre_g, v_mix_post_g, v_ffn_pre_g, v_ffn_post_g, v_pool_scale, v_conv_b, v_conv_ln_g, v_conv_ln_b],
        rep_all, rep_layout, loss_at)
    loss = loss_sum[0, 0] * (0.5 / d)
    o_sc_out = upd("adam_sc_out", sc_w_out, m_sc_w_out, v_sc_w_out, [reduce_end("sc_out", o_ff1[0])])
    o_sc_in = upd("adam_sc_in", sc_w_in, m_sc_w_in, v_sc_w_in, [reduce_end("sc_in", o_sc_out[0])])
    o_ab_out = upd("adam_ab_out", ab_w_out, m_ab_w_out, v_ab_w_out, [reduce_end("ab_out", o_sc_in[0])])
    o_small = adamw("adam_small", pack_small(conv_w, sc_conv_w, pool_w), pack_small(m_conv_w, m_sc_conv_w, m_pool_w),
                    pack_small(v_conv_w, v_sc_conv_w, v_pool_w), [reduce_end("small", o_ab_out[0])])
    o_ab_in = upd("adam_ab_in", ab_w_in, m_ab_w_in, v_ab_w_in, [reduce_end("ab_in", o_small[0])])

    def unpack_small(o):
        return o[:r0].reshape(conv_w.shape), o[r0:r1].reshape(sc_conv_w.shape), o[r1:r2].reshape(pool_w.shape)

    results = []
    for kind in range(4):
        g_mix_pre, g_mix_post, g_ffn_pre, g_ffn_post, g_scale, g_conv_b, g_ln_g, g_ln_b = o_rep[kind::4]
        s_conv, s_sconv, s_pool = unpack_small(o_small[kind])
        results.append([
            g_mix_pre, g_mix_post, g_ffn_pre, g_ffn_post,
            o_ab_in[kind], s_pool, g_scale, s_conv, g_conv_b, g_ln_g, g_ln_b,
            o_ab_out[kind], o_sc_in[kind], s_sconv, o_sc_out[kind], o_ff1[kind], o_ff2[kind]])

    return (loss, grad_x[None], *results[0], *results[1], *results[2], *results[3])
```

```python
import jax
import jax.numpy as jnp
from jax import lax
from jax.experimental import pallas as pl
from jax.experimental.pallas import tpu as pltpu

F32 = jnp.float32
BF16 = jnp.bfloat16
MESH = pl.DeviceIdType.MESH
ANY = pl.BlockSpec(memory_space=pl.ANY)

NORM_EPS = 1e-6
POOL_WINDOWS = (2, 4, 8, 16)
ADAM_LR = 0.001
ADAM_B1 = 0.9
ADAM_B2 = 0.999
ADAM_EPS = 1e-08
ADAM_WD = 0.01
ADAM_STEP = 10

N_DEV = 8
VMEM_LIMIT = 56 * 1024 * 1024
PAIR_ADD_BLOCK = 1 << 20
MATMUL_ROWS = 2048
UP_FIRST_BLOCKS = 6
ROW_TILE = 256
CHANNEL_TILE = 256
TIME_CHUNK = 64
HALO = 32

NN = (((1,), (0,)), ((), ()))
NT = (((1,), (1,)), ((), ()))
TN = (((0,), (0,)), ((), ()))


def _params(sem):
    return pltpu.CompilerParams(dimension_semantics=sem, vmem_limit_bytes=VMEM_LIMIT)


def _place():
    x, y, c = lax.axis_index("x"), lax.axis_index("y"), lax.axis_index("c")
    return x, y, c


def _slot(px, py, pc):
    return 4 * px + 2 * py + pc


HBM = pl.BlockSpec(memory_space=pltpu.HBM)
SEM = pl.BlockSpec(memory_space=pltpu.SEMAPHORE)
EFFECT = pltpu.SideEffectType.DATAFLOW_SIDE_EFFECTING
TOKEN = jax.ShapeDtypeStruct((8, 128), F32)


def _in_hbm(a):
    return pltpu.with_memory_space_constraint(a, pltpu.HBM)


CHIPS = [(0, 0), (0, 1), (1, 0), (1, 1)]
N_CHIP = len(CHIPS)


def _chip(px, py):
    return 2 * px + py


def _first_hop(bufs, sends, recvs, waiting):
    (land,) = bufs
    x, y, c = _place()
    me = _slot(x, y, c)
    peers = [(x, y, 1 - c), (1 - x, y, c), (x, 1 - y, c), (1 - x, 1 - y, c)]
    return [pltpu.make_async_remote_copy(
        src_ref=land.at[me], dst_ref=land.at[_slot(*p) if waiting else me],
        send_sem=sends.at[k], recv_sem=recvs.at[k], device_id=p, device_id_type=MESH) for k, p in enumerate(peers)]


def _second_hop(bufs, sends, recvs, waiting):
    (land,) = bufs
    x, y, c = _place()
    return [pltpu.make_async_remote_copy(
        src_ref=land.at[_slot(px, py, c)], dst_ref=land.at[_slot(px, py, 1 - c if waiting else c)],
        send_sem=sends.at[k], recv_sem=recvs.at[k], device_id=(x, y, 1 - c), device_id_type=MESH)
        for k, (px, py) in enumerate([(1 - x, y), (x, 1 - y), (1 - x, 1 - y)])]


def _ring_hop1(bufs, sends, recvs, waiting):
    (land,) = bufs
    x, y, c = _place()
    me = _slot(x, y, c)
    peers = [(1 - x, y, c), (x, 1 - y, c), (x, y, 1 - c)]
    return [pltpu.make_async_remote_copy(
        src_ref=land.at[me], dst_ref=land.at[_slot(*p) if waiting else me],
        send_sem=sends.at[k], recv_sem=recvs.at[k], device_id=p, device_id_type=MESH) for k, p in enumerate(peers)]


def _ring_hop2(bufs, sends, recvs, waiting):
    (land,) = bufs
    x, y, c = _place()
    half = land.shape[1] // 2
    first, second = pl.ds(0, half), pl.ds(half, half)
    nx, ny, diag = _slot(1 - x, y, c), _slot(x, 1 - y, c), _slot(1 - x, 1 - y, c)
    plan = [
        (land.at[ny, first], land.at[diag, first], (1 - x, y, c)),
        (land.at[nx, second], land.at[diag, second], (x, 1 - y, c)),
        (land.at[nx], land.at[_slot(1 - x, y, 1 - c)], (x, y, 1 - c)),
        (land.at[ny], land.at[_slot(x, 1 - y, 1 - c)], (x, y, 1 - c))]
    return [pltpu.make_async_remote_copy(
        src_ref=src, dst_ref=mine if waiting else src, send_sem=sends.at[k], recv_sem=recvs.at[k],
        device_id=to, device_id_type=MESH) for k, (src, mine, to) in enumerate(plan)]


def _ring_hop3(bufs, sends, recvs, waiting):
    (land,) = bufs
    x, y, c = _place()
    return [pltpu.make_async_remote_copy(
        src_ref=land.at[_slot(1 - x, 1 - y, c)], dst_ref=land.at[_slot(1 - x, 1 - y, 1 - c if waiting else c)],
        send_sem=sends.at[0], recv_sem=recvs.at[0], device_id=(x, y, 1 - c), device_id_type=MESH)]


RING_HOPS = [(_ring_hop1, 3), (_ring_hop2, 4), (_ring_hop3, 1)]

def _pair_hop(bufs, sends, recvs, waiting):
    g, land = bufs
    x, y, c = _place()
    return [pltpu.make_async_remote_copy(
        src_ref=g.at[_slot(qx, qy, 1 - c)], dst_ref=land.at[q],
        send_sem=sends.at[q], recv_sem=recvs.at[q], device_id=(x, y, 1 - c), device_id_type=MESH)
        for q, (qx, qy) in enumerate(CHIPS)]


def _chip_hop(bufs, sends, recvs, waiting):
    p, land = bufs
    x, y, c = _place()
    return [pltpu.make_async_remote_copy(
        src_ref=p.at[_chip(px, py)], dst_ref=land.at[_chip(px, py) if waiting else _chip(x, y)],
        send_sem=sends.at[k], recv_sem=recvs.at[k], device_id=(px, py, c), device_id_type=MESH)
        for k, (px, py) in enumerate([(1 - x, y), (x, 1 - y), (1 - x, 1 - y)])]


def copies_start(name, groups, hop, n_copies, deps=()):
    flat = [b for grp in groups for b in grp]
    nb, ng = len(flat), len(groups)
    deps = list(deps)
    hops = list(hop) if isinstance(hop, (list, tuple)) else [hop] * ng
    counts = list(n_copies) if isinstance(n_copies, (list, tuple)) else [n_copies] * ng

    def body(*refs):
        ins, token = refs[:nb], refs[-1]
        sems = refs[nb + len(deps):nb + len(deps) + 2 * ng]
        i = 0
        for gi, grp in enumerate(groups):
            for cp in hops[gi](ins[i:i + len(grp)], sems[2 * gi], sems[2 * gi + 1], False):
                cp.start()
            i += len(grp)
        token[...] = jnp.zeros_like(token)

    outs = pl.pallas_call(
        body, name=name,
        out_shape=([pltpu.SemaphoreType.DMA((n,)) for n in counts for _ in range(2)]
                   + [pltpu.HBM(b.shape, b.dtype) for b in flat] + [TOKEN]),
        in_specs=[HBM] * nb + [ANY] * len(deps),
        out_specs=[SEM] * (2 * ng) + [HBM] * nb + [pl.BlockSpec(memory_space=pltpu.VMEM)],
        input_output_aliases={i: 2 * ng + i for i in range(nb)},
        compiler_params=pltpu.CompilerParams(has_side_effects=EFFECT),
    )(*[_in_hbm(b) for b in flat], *deps)
    started, i = [], 0
    for gi, grp in enumerate(groups):
        started.append((outs[2 * gi], outs[2 * gi + 1], list(outs[2 * ng + i:2 * ng + i + len(grp)])))
        i += len(grp)
    return started, outs[-1]


def copies_wait(name, started, hop, after):
    sends, recvs, bufs = started
    nb = len(bufs)

    def body(*refs):
        for cp in hop(refs[:nb], refs[nb], refs[nb + 1], True):
            cp.wait_send()
            cp.wait_recv()

    outs = pl.pallas_call(
        body, name=name,
        out_shape=[pltpu.HBM(b.shape, b.dtype) for b in bufs],
        in_specs=[HBM] * nb + [SEM, SEM, ANY], out_specs=[HBM] * nb,
        input_output_aliases={i: i for i in range(nb)},
        compiler_params=pltpu.CompilerParams(has_side_effects=EFFECT),
    )(*bufs, sends, recvs, after)
    return list(outs)


def place_shard(name, w, layer, dtype, deps=()):
    _, r, c = w.shape
    tr = _tile(r, 1024)
    x, y, core = _place()
    me = _slot(x, y, core).astype(jnp.int32).reshape(1)

    def body(me_ref, w_ref, *rest):
        rest[-1][...] = w_ref[...].astype(dtype)

    return pl.pallas_call(
        body, name=name,
        grid_spec=pltpu.PrefetchScalarGridSpec(
            num_scalar_prefetch=1, grid=(r // tr,),
            in_specs=[pl.BlockSpec((None, tr, c), lambda i, me_ref: (layer, i, 0))] + [ANY] * len(deps),
            out_specs=pl.BlockSpec((None, tr, c), lambda i, me_ref: (me_ref[0], i, 0))),
        out_shape=jax.ShapeDtypeStruct((N_DEV, r, c), dtype),
        compiler_params=_params(("parallel",)),
    )(me, w, *deps)


def place_sheet(name, pieces, total_of, total_at, rows, width):
    x, y, core = _place()
    me = _slot(x, y, core).astype(jnp.int32).reshape(1)

    def body(me_ref, *refs):
        o_ref = refs[-1]
        o_ref[...] = jnp.zeros_like(o_ref)
        for ref, (_, row, lane) in zip(refs, pieces):
            o_ref[row:row + 1, lane:lane + ref.shape[1]] = jnp.sum(ref[...], axis=0, keepdims=True)
        total = jnp.sum(jnp.sum(refs[len(pieces)][...], axis=0, keepdims=True), axis=1, keepdims=True)
        o_ref[total_at[0]:total_at[0] + 1, total_at[1]:total_at[1] + 128] = jnp.broadcast_to(total, (1, 128))

    arrays = [a for a, _, _ in pieces] + [total_of]
    return pl.pallas_call(
        body, name=name,
        grid_spec=pltpu.PrefetchScalarGridSpec(
            num_scalar_prefetch=1, grid=(1,),
            in_specs=[pl.BlockSpec(a.shape, lambda i, me_ref: (0, 0)) for a in arrays],
            out_specs=pl.BlockSpec((None, rows, width), lambda i, me_ref: (me_ref[0], 0, 0))),
        out_shape=jax.ShapeDtypeStruct((N_DEV, rows, width), F32),
    )(me, *arrays)


def tie(name, x, *deps):
    def body(*refs):
        del refs

    return pl.pallas_call(
        body, name=name, out_shape=jax.ShapeDtypeStruct(x.shape, x.dtype),
        in_specs=[ANY] * (1 + len(deps)), out_specs=ANY, input_output_aliases={0: 0},
    )(x, *deps)


def pair_add(name, g, from_sibling):
    _, r, c_dim = g.shape
    tr = r
    while tr * c_dim > PAIR_ADD_BLOCK and tr % 16 == 0:
        tr //= 2
    x, y, core = _place()
    where = jnp.stack([core, _chip(x, y)]).astype(jnp.int32)

    def body(where_ref, g_ref, s_ref, o_ref, zone_ref):
        total = (g_ref[...].astype(F32) + s_ref[...].astype(F32)).astype(o_ref.dtype)
        o_ref[...] = total

        @pl.when(pl.program_id(1) == where_ref[1])
        def _():
            zone_ref[...] = total

    blk = pl.BlockSpec((None, tr, c_dim), lambda i, q, where_ref: (q, i, 0))
    return pl.pallas_call(
        body, name=name,
        grid_spec=pltpu.PrefetchScalarGridSpec(
            num_scalar_prefetch=1, grid=(r // tr, N_CHIP),
            in_specs=[pl.BlockSpec((None, None, tr, c_dim), lambda i, q, where_ref: (q, where_ref[0], i, 0)), blk],
            out_specs=[blk, pl.BlockSpec((None, tr, c_dim), lambda i, q, where_ref: (where_ref[1], i, 0))]),
        out_shape=[jax.ShapeDtypeStruct((N_CHIP, r, c_dim), g.dtype)] * 2,
        compiler_params=_params(("parallel", "arbitrary")),
    )(where, g.reshape(N_CHIP, 2, r, c_dim), from_sibling)


def _matmul(name, lhs, rhs, *, out_shape, out_dtype, grid, lhs_spec, rhs_spec, out_spec, acc_shape,
            lhs_fn=None, epilogue=None, into=None):
    nk = grid[2]
    extra = [] if into is None else [into]

    def body(lhs_ref, rhs_ref, *rest):
        out_ref, scratch = rest[len(extra)], rest[len(extra) + 1:]

        def product():
            a = lhs_ref[...]
            if lhs_fn is not None:
                a = lhs_fn(a)
            return lax.dot_general(a, rhs_ref[...], NN, preferred_element_type=F32)

        def finish(r):
            if epilogue is not None:
                r = epilogue(r)
            out_ref[...] = r.astype(out_dtype)

        if nk == 1:
            finish(product())
        else:
            (acc_ref,) = scratch
            k = pl.program_id(2)

            @pl.when(k == 0)
            def _():
                acc_ref[...] = product()

            @pl.when(jnp.logical_and(k > 0, k < nk - 1))
            def _():
                acc_ref[...] += product()

            @pl.when(k == nk - 1)
            def _():
                finish(acc_ref[...] + product())

    return pl.pallas_call(
        body, name=name, grid=grid,
        out_shape=jax.ShapeDtypeStruct(out_shape, out_dtype),
        in_specs=[lhs_spec, rhs_spec] + [ANY] * len(extra), out_specs=out_spec,
        input_output_aliases={2: 0} if extra else {},
        scratch_shapes=[pltpu.VMEM(acc_shape, F32)] if nk > 1 else [],
        compiler_params=_params(("parallel", "parallel", "arbitrary")),
    )(lhs, rhs, *extra)


def _tile(n, want):
    return want if n % want == 0 else n


def mm_nn(name, x, w, *, out_dtype, tn=512, tk=None, lhs_fn=None, epilogue=None):
    t, kdim = x.shape
    n = w.shape[1]
    tm, tn = _tile(t, MATMUL_ROWS), _tile(n, tn)
    tk = kdim if tk is None else _tile(kdim, tk)
    return _matmul(
        name, x, w, out_shape=(t, n), out_dtype=out_dtype, grid=(t // tm, n // tn, kdim // tk),
        lhs_spec=pl.BlockSpec((tm, tk), lambda i, j, k: (i, k)),
        rhs_spec=pl.BlockSpec((tk, tn), lambda i, j, k: (k, j)),
        out_spec=pl.BlockSpec((tm, tn), lambda i, j, k: (i, j)),
        acc_shape=(tm, tn), lhs_fn=lhs_fn, epilogue=epilogue)


def mm_nn_blocked(name, x, w, *, out_dtype, epilogue=None, blocks=(0, N_DEV), into=None):
    t, kdim = x.shape
    nb = w.shape[2]
    tm = _tile(t, MATMUL_ROWS)
    tn = nb // 2 if nb >= 1024 else nb
    sub = nb // tn
    first, count = blocks
    return _matmul(
        name, x, w, out_shape=(t, N_DEV * nb), out_dtype=out_dtype, grid=(t // tm, count * sub, 1),
        lhs_spec=pl.BlockSpec((tm, kdim), lambda i, j, k: (i, k)),
        rhs_spec=pl.BlockSpec((None, kdim, tn), lambda i, j, k: (first + j // sub, k, j % sub)),
        out_spec=pl.BlockSpec((tm, tn), lambda i, j, k: (i, first * sub + j)),
        acc_shape=(tm, tn), epilogue=epilogue, into=into)


def mm_bwd_pair(name, dy, w, act, *, out_dtype, tile=512, act_fn=None, epilogue=None):
    t, n = dy.shape
    kdim = w.shape[0]
    tile = _tile(kdim, tile)

    def body(dy_ref, w_ref, act_ref, dx_ref, dw_ref):
        a = act_ref[...]
        dx = lax.dot_general(dy_ref[...], w_ref[...], NT, preferred_element_type=F32)
        if epilogue is not None:
            dx = epilogue(dx, a)
        dx_ref[...] = dx.astype(out_dtype)
        if act_fn is not None:
            a = act_fn(a)
        dw_ref[...] = lax.dot_general(a, dy_ref[...], TN, preferred_element_type=F32).astype(out_dtype)

    return pl.pallas_call(
        body, name=name, grid=(kdim // tile,),
        in_specs=[pl.BlockSpec((t, n), lambda j: (0, 0)), pl.BlockSpec((tile, n), lambda j: (j, 0)),
                  pl.BlockSpec((t, tile), lambda j: (0, j))],
        out_specs=[pl.BlockSpec((t, tile), lambda j: (0, j)), pl.BlockSpec((tile, n), lambda j: (j, 0))],
        out_shape=[jax.ShapeDtypeStruct((t, kdim), out_dtype), jax.ShapeDtypeStruct((kdim, n), out_dtype)],
        compiler_params=_params(("parallel",)),
    )(dy, w, act)


def mm_bwd_pair_blocked(name, dz, w, act, *, out_dtype, tile=1024):
    t = dz.shape[0]
    kdim, nb = w.shape[1], w.shape[2]
    tile = _tile(kdim, tile)

    def body(dz_ref, w_ref, act_ref, dx_ref, dw_ref, acc_ref):
        j = pl.program_id(1)
        dw_ref[...] = lax.dot_general(act_ref[...], dz_ref[...], TN, preferred_element_type=F32).astype(out_dtype)

        def product():
            return lax.dot_general(dz_ref[...], w_ref[...], NT, preferred_element_type=F32)

        @pl.when(j == 0)
        def _():
            acc_ref[...] = product()

        @pl.when(jnp.logical_and(j > 0, j < N_DEV - 1))
        def _():
            acc_ref[...] += product()

        @pl.when(j == N_DEV - 1)
        def _():
            dx_ref[...] = (acc_ref[...] + product()).astype(out_dtype)

    return pl.pallas_call(
        body, name=name, grid=(kdim // tile, N_DEV),
        in_specs=[pl.BlockSpec((t, nb), lambda i, j: (0, j)), pl.BlockSpec((None, tile, nb), lambda i, j: (j, i, 0)),
                  pl.BlockSpec((t, tile), lambda i, j: (0, i))],
        out_specs=[pl.BlockSpec((t, tile), lambda i, j: (0, i)),
                   pl.BlockSpec((None, tile, nb), lambda i, j: (j, i, 0))],
        out_shape=[jax.ShapeDtypeStruct((t, kdim), out_dtype), jax.ShapeDtypeStruct((N_DEV, kdim, nb), out_dtype)],
        scratch_shapes=[pltpu.VMEM((t, tile), F32)],
        compiler_params=_params(("parallel", "arbitrary")),
    )(dz, w, act)


def _rstd(v):
    return lax.rsqrt(jnp.mean(v * v, axis=-1, keepdims=True) + NORM_EPS)


def _rms_bwd(v, g, dy):
    r = _rstd(v)
    vhat = v * r
    dvh = dy * g
    dv = r * (dvh - vhat * jnp.mean(dvh * vhat, axis=-1, keepdims=True))
    return dv, dy * vhat


def _fold8(v):
    rows, n = v.shape
    return jnp.sum(v.reshape(rows // 8, 8, n), axis=0)


def _fold_lanes(v):
    out = v[:, 0:128]
    for i in range(1, v.shape[1] // 128):
        out = out + v[:, 128 * i:128 * (i + 1)]
    return out


def _accumulate(ref, v):
    i = pl.program_id(0)

    @pl.when(i == 0)
    def _():
        ref[...] = v

    @pl.when(i > 0)
    def _():
        ref[...] += v


def _row_call(body, name, t, ins, row_in, outs, acc_outs=(), tr=ROW_TILE):
    tr = _tile(t, tr)

    def in_spec(a, tiled):
        if isinstance(tiled, tuple):
            width, j = tiled
            return pl.BlockSpec((tr, width), lambda i: (i, j))
        return pl.BlockSpec((tr, a.shape[1]), lambda i: (i, 0)) if tiled else pl.BlockSpec(a.shape, lambda i: (0, 0))

    in_specs = [in_spec(a, tiled) for a, tiled in zip(ins, row_in)]
    out_specs = [pl.BlockSpec((tr, n), lambda i: (i, 0)) for n, _ in outs]
    out_specs += [pl.BlockSpec((8, n), lambda i: (0, 0)) for n in acc_outs]
    out_shape = [jax.ShapeDtypeStruct((t, n), dt) for n, dt in outs]
    out_shape += [jax.ShapeDtypeStruct((8, n), F32) for n in acc_outs]
    return pl.pallas_call(
        body, name=name, grid=(t // tr,), in_specs=in_specs, out_specs=out_specs, out_shape=out_shape,
        compiler_params=_params(("arbitrary",) if acc_outs else ("parallel",)),
    )(*ins)


def norm_pre(name, x, g):
    t, d = x.shape

    def body(x_ref, g_ref, h_ref):
        v = x_ref[...]
        h_ref[...] = (v * _rstd(v) * g_ref[...]).astype(BF16)

    return _row_call(body, name, t, [x, g], [True, False], [(d, BF16)])[0]


def post_pre(name, x, m, g_post, g_pre):
    t, d = x.shape

    def body(x_ref, m_ref, gp_ref, gn_ref, xo_ref, h_ref):
        mv = m_ref[...]
        xn = x_ref[...] + mv * _rstd(mv) * gp_ref[...]
        xo_ref[...] = xn
        h_ref[...] = (xn * _rstd(xn) * gn_ref[...]).astype(BF16)

    return _row_call(body, name, t, [x, m, g_post, g_pre], [True, True, False, False], [(d, F32), (d, BF16)])


def post_loss(name, x, f, g_post, target):
    t, d = x.shape

    def body(x_ref, f_ref, g_ref, t_ref, dx_ref, df_ref, loss_ref, dg_ref):
        fv = f_ref[...]
        g = g_ref[...]
        out = x_ref[...] + fv * _rstd(fv) * g
        err = out - t_ref[...]
        dx = err * (1.0 / d)
        dx_ref[...] = dx
        dfv, dg_rows = _rms_bwd(fv, g, dx)
        df_ref[...] = dfv.astype(BF16)
        _accumulate(loss_ref, _fold8(_fold_lanes(err * err)))
        _accumulate(dg_ref, _fold8(dg_rows))

    return _row_call(body, name, t, [x, f, g_post, target], [True, True, False, True],
                     [(d, F32), (d, BF16)], acc_outs=(128, d))


def bwd_pre_post(name, dx_out, x_in, g_pre, dh, f_prev, g_post_prev):
    t, d = x_in.shape

    def body(dxo_ref, x_ref, gpre_ref, dh_ref, f_ref, gpost_ref, dxi_ref, df_ref, dgpre_ref, dgpost_ref):
        dxv, dgpre_rows = _rms_bwd(x_ref[...], gpre_ref[...], dh_ref[...].astype(F32))
        dxi = dxo_ref[...] + dxv
        dxi_ref[...] = dxi
        dfv, dgpost_rows = _rms_bwd(f_ref[...], gpost_ref[...], dxi)
        df_ref[...] = dfv.astype(BF16)
        _accumulate(dgpre_ref, _fold8(dgpre_rows))
        _accumulate(dgpost_ref, _fold8(dgpost_rows))

    return _row_call(body, name, t, [dx_out, x_in, g_pre, dh, f_prev, g_post_prev],
                     [True, True, False, True, True, False], [(d, F32), (d, BF16)], acc_outs=(d, d))


def bwd_pre_final(name, dx_out, x_in, g_pre, dh):
    t, d = x_in.shape

    def body(dxo_ref, x_ref, gpre_ref, dh_ref, dxi_ref, dgpre_ref):
        dxv, dgpre_rows = _rms_bwd(x_ref[...], gpre_ref[...], dh_ref[...].astype(F32))
        dxi_ref[...] = dxo_ref[...] + dxv
        _accumulate(dgpre_ref, _fold8(dgpre_rows))

    return _row_call(body, name, t, [dx_out, x_in, g_pre, dh], [True, True, False, True], [(d, F32)], acc_outs=(d,))


def _layer_norm_parts(cv):
    mu = jnp.mean(cv, axis=-1, keepdims=True)
    xc = cv - mu
    rstd = lax.rsqrt(jnp.mean(xc * xc, axis=-1, keepdims=True) + NORM_EPS)
    return xc * rstd, rstd


def ln_silu(name, cv, g, b, y, y_block):
    t, n = cv.shape
    tr = _tile(t, ROW_TILE)

    def body(c_ref, g_ref, b_ref, y_in_ref, y_ref):
        chat, _ = _layer_norm_parts(c_ref[...])
        ln = chat * g_ref[...] + b_ref[...]
        y_ref[...] = (ln * jax.nn.sigmoid(ln)).astype(BF16)

    vec = pl.BlockSpec((1, n), lambda i: (0, 0))
    return pl.pallas_call(
        body, name=name, grid=(t // tr,),
        in_specs=[pl.BlockSpec((tr, n), lambda i: (i, 0)), vec, vec, ANY],
        out_specs=pl.BlockSpec((tr, n), lambda i: (i, y_block)),
        out_shape=jax.ShapeDtypeStruct(y.shape, y.dtype), input_output_aliases={3: 0},
        compiler_params=_params(("parallel",)),
    )(cv, g, b, y)


def ln_silu_bwd(name, cv, g, b, dy, dy_block):
    t, n = cv.shape

    def body(c_ref, g_ref, b_ref, dy_ref, dc_ref, dg_ref, db_ref):
        chat, rstd = _layer_norm_parts(c_ref[...])
        g = g_ref[...]
        ln = chat * g + b_ref[...]
        s = jax.nn.sigmoid(ln)
        dln = dy_ref[...].astype(F32) * (s * (1.0 + ln * (1.0 - s)))
        dchat = dln * g
        dc_ref[...] = rstd * (dchat - jnp.mean(dchat, axis=-1, keepdims=True)
                              - chat * jnp.mean(dchat * chat, axis=-1, keepdims=True))
        _accumulate(dg_ref, _fold8(dln * chat))
        _accumulate(db_ref, _fold8(dln))

    return _row_call(body, name, t, [cv, g, b, dy], [True, False, False, (n, dy_block)], [(n, F32)], acc_outs=(n, n))


def _chunks(t, fn, tc=TIME_CHUNK):
    tc = _tile(t, tc)

    def step(i, carry):
        fn(pl.multiple_of(i * tc, tc), tc)
        return carry

    lax.fori_loop(0, t // tc, step, 0)


def _rows_from(v, start, n):
    res = start % 8
    base = v if res == 0 else pltpu.roll(v, v.shape[0] - res, axis=0)
    return base[start - res:start - res + n, :]


def _shifted(window, offsets, tc):
    rows = window.shape[0]
    by_residue = {}
    for k, off in enumerate(offsets):
        by_residue.setdefault(off % 8, []).append((k, off))
    for res, taps in by_residue.items():
        base = window if res == 0 else pltpu.roll(window, rows - res, axis=0)
        for k, off in taps:
            yield k, base[off - res:off - res + tc, :]


def _taps(window, w_ref, offsets, tc, flip=False):
    acc = None
    for k, rows in _shifted(window, offsets, tc):
        kk = len(offsets) - 1 - k if flip else k
        term = w_ref[kk:kk + 1, :] * rows
        acc = term if acc is None else acc + term
    return acc


def _window_sums(win, tc, causal):
    sums = []
    cur, rows, step = win, tc + HALO, 1
    for _ in POOL_WINDOWS:
        rows -= 8
        if causal:
            cur = cur[8:8 + rows, :] + _rows_from(cur, 8 - step, rows)
            sums.append(cur[rows - tc:rows, :])
        else:
            cur = cur[0:rows, :] + _rows_from(cur, step, rows)
            sums.append(cur[0:tc, :])
        step *= 2
    return sums


def _pick(vals, g):
    out = vals[-1]
    for i in range(len(vals) - 2, -1, -1):
        out = jnp.where(g == i, vals[i], out)
    return out


def _pool_count(s, tc, g):
    t1 = (lax.broadcasted_iota(jnp.int32, (tc, 1), 0) + (s + 1)).astype(F32)
    width = _pick([float(w) for w in POOL_WINDOWS], g)
    return jnp.minimum(t1, width)


def pool_fwd(name, z, pool_w, pool_scale, d_pool, y_width):
    t = z.shape[0]
    ng, pg = pool_w.shape[0], pool_w.shape[1]

    def body(u_ref, w_ref, s_ref, pooled_ref, y_ref, pad):
        g = pl.program_id(0)
        pad[pl.ds(0, HALO), :] = jnp.zeros((HALO, pg), F32)

        def fill(s, tc):
            pad[pl.ds(HALO + s, tc), :] = u_ref[pl.ds(s, tc), :].astype(F32)

        def chunk(s, tc):
            win = pad[pl.ds(s, tc + HALO), :]
            total = _pick(_window_sums(win, tc, causal=True), g)
            pooled = total / _pool_count(s, tc, g) - win[HALO:HALO + tc, :]
            pooled_ref[pl.ds(s, tc), :] = pooled.astype(BF16)

        _chunks(t, fill)
        _chunks(t, chunk)
        mixed = jnp.dot(pooled_ref[...], w_ref[...], preferred_element_type=F32)
        y_ref[...] = (mixed * s_ref[...]).astype(BF16)

    col = pl.BlockSpec((t, pg), lambda g: (0, g))
    return pl.pallas_call(
        body, name=name, grid=(ng,),
        in_specs=[col, pl.BlockSpec((None, pg, pg), lambda g: (g, 0, 0)), pl.BlockSpec((1, pg), lambda g: (0, g))],
        out_specs=[col, col],
        out_shape=[jax.ShapeDtypeStruct((t, d_pool), BF16), jax.ShapeDtypeStruct((t, y_width), BF16)],
        scratch_shapes=[pltpu.VMEM((t + HALO, pg), F32)],
        compiler_params=_params(("parallel",)),
    )(z, pool_w, pool_scale)


def pool_bwd(name, pooled, dy, pool_w, pool_scale, dz):
    t, d_pool = pooled.shape
    ng, pg = pool_w.shape[0], pool_w.shape[1]

    def body(p_ref, dy_ref, w_ref, s_ref, dz_ref, du_ref, dw_ref, ds_ref, pad):
        g = pl.program_id(0)
        w = w_ref[...]
        dyv = dy_ref[...].astype(F32)
        mixed = jnp.dot(p_ref[...], w, preferred_element_type=F32)
        ds_ref[...] = jnp.sum(dyv * mixed, axis=0, keepdims=True)
        dmixed = (dyv * s_ref[...]).astype(BF16)
        dw_ref[...] = lax.dot_general(p_ref[...], dmixed, TN, preferred_element_type=F32)
        pad[...] = jnp.zeros((t + HALO, pg), F32)
        pad[pl.ds(0, t), :] = lax.dot_general(dmixed, w, NT, preferred_element_type=F32)

        def scale(s, tc):
            pad[pl.ds(s, tc), :] = pad[pl.ds(s, tc), :] / _pool_count(s, tc, g)

        def chunk(s, tc):
            win = pad[pl.ds(s, tc + HALO), :]
            total = _pick(_window_sums(win, tc, causal=False), g)
            du_ref[pl.ds(s, tc), :] = (total - win[0:tc, :] * _pool_count(s, tc, g)).astype(BF16)

        _chunks(t, scale)
        _chunks(t, chunk)

    col = pl.BlockSpec((t, pg), lambda g: (0, g))
    vec = pl.BlockSpec((1, pg), lambda g: (0, g))
    mat = pl.BlockSpec((None, pg, pg), lambda g: (g, 0, 0))
    return pl.pallas_call(
        body, name=name, grid=(ng,),
        in_specs=[col, col, mat, vec, ANY], out_specs=[col, mat, vec],
        out_shape=[jax.ShapeDtypeStruct(dz.shape, dz.dtype), jax.ShapeDtypeStruct((ng, pg, pg), F32),
                   jax.ShapeDtypeStruct((1, d_pool), F32)],
        input_output_aliases={4: 0},
        scratch_shapes=[pltpu.VMEM((t + HALO, pg), F32)],
        compiler_params=_params(("parallel",)),
    )(pooled, dy, pool_w, pool_scale, dz)


def conv_fwd(name, z, conv_w, conv_b, d_pool, d_conv):
    t = z.shape[0]
    kw = conv_w.shape[0]
    tc_ch = _tile(d_conv, CHANNEL_TILE)
    v0, g0 = d_pool // tc_ch, (d_pool + d_conv) // tc_ch

    def body(v_ref, g_ref, w_ref, b_ref, c_ref, pad):
        pad[pl.ds(0, HALO), :] = jnp.zeros((HALO, tc_ch), F32)

        def fill(s, tc):
            pad[pl.ds(HALO + s, tc), :] = v_ref[pl.ds(s, tc), :].astype(F32) * jax.nn.sigmoid(g_ref[pl.ds(s, tc), :].astype(F32))

        def chunk(s, tc):
            win = pad[pl.ds(s, tc + HALO), :]
            c_ref[pl.ds(s, tc), :] = _taps(win, w_ref, [HALO - (kw - 1) + k for k in range(kw)], tc) + b_ref[...]

        _chunks(t, fill)
        _chunks(t, chunk)

    return pl.pallas_call(
        body, name=name, grid=(d_conv // tc_ch,),
        in_specs=[pl.BlockSpec((t, tc_ch), lambda j: (0, v0 + j)), pl.BlockSpec((t, tc_ch), lambda j: (0, g0 + j)),
                  pl.BlockSpec((kw, tc_ch), lambda j: (0, j)), pl.BlockSpec((1, tc_ch), lambda j: (0, j))],
        out_specs=pl.BlockSpec((t, tc_ch), lambda j: (0, j)),
        out_shape=jax.ShapeDtypeStruct((t, d_conv), F32),
        scratch_shapes=[pltpu.VMEM((t + HALO, tc_ch), F32)],
        compiler_params=_params(("parallel",)),
    )(z, z, conv_w, conv_b)


def conv_bwd(name, z, dc, conv_w, d_pool, d_conv):
    t = z.shape[0]
    kw = conv_w.shape[0]
    tc_ch = _tile(d_conv, CHANNEL_TILE)
    v0, g0 = d_pool // tc_ch, (d_pool + d_conv) // tc_ch

    def body(v_ref, g_ref, dc_ref, w_ref, dz_ref, dw_ref, db_ref, pad_a, pad_dc, acc_w, acc_b, tiles, sems):
        j = pl.program_id(0)
        dv_ref, dg_ref = tiles.at[0], tiles.at[1]
        writes = [pltpu.make_async_copy(tiles.at[p], dz_ref.at[:, pl.ds((first + j) * tc_ch, tc_ch)], sems.at[p])
                  for p, first in enumerate([v0, g0])]

        def wait_writes():
            for cp in writes:
                cp.wait()

        pad_a[pl.ds(0, HALO), :] = jnp.zeros((HALO, tc_ch), F32)
        pad_dc[pl.ds(t, HALO), :] = jnp.zeros((HALO, tc_ch), F32)
        acc_w[...] = jnp.zeros_like(acc_w)
        acc_b[...] = jnp.zeros_like(acc_b)

        def fill(s, tc):
            pad_a[pl.ds(HALO + s, tc), :] = v_ref[pl.ds(s, tc), :].astype(F32) * jax.nn.sigmoid(g_ref[pl.ds(s, tc), :].astype(F32))
            pad_dc[pl.ds(s, tc), :] = dc_ref[pl.ds(s, tc), :]

        def chunk(s, tc):
            dcv = pad_dc[pl.ds(s, tc), :]
            win_a = pad_a[pl.ds(s, tc + HALO), :]
            for k, rows in _shifted(win_a, [HALO - (kw - 1) + k for k in range(kw)], tc):
                acc_w[pl.ds(8 * k, 8), :] += _fold8(dcv * rows)
            acc_b[...] += _fold8(dcv)
            da = _taps(pad_dc[pl.ds(s, tc + HALO), :], w_ref, list(range(kw)), tc, flip=True)
            vv = v_ref[pl.ds(s, tc), :].astype(F32)
            sg = jax.nn.sigmoid(g_ref[pl.ds(s, tc), :].astype(F32))
            dv_ref[pl.ds(s, tc), :] = (da * sg).astype(BF16)
            dg_ref[pl.ds(s, tc), :] = (da * vv * sg * (1.0 - sg)).astype(BF16)

        _chunks(t, fill)
        pl.when(j > 0)(wait_writes)
        _chunks(t, chunk)
        for cp in writes:
            cp.start()
        pl.when(j == n_tiles - 1)(wait_writes)
        for k in range(kw):
            dw_ref[k:k + 1, :] = jnp.sum(acc_w[pl.ds(8 * k, 8), :], axis=0, keepdims=True)
        db_ref[...] = jnp.sum(acc_b[...], axis=0, keepdims=True)

    n_tiles = d_conv // tc_ch
    return pl.pallas_call(
        body, name=name, grid=(n_tiles,),
        in_specs=[pl.BlockSpec((t, tc_ch), lambda j: (0, v0 + j)), pl.BlockSpec((t, tc_ch), lambda j: (0, g0 + j)),
                  pl.BlockSpec((t, tc_ch), lambda j: (0, j)), pl.BlockSpec((kw, tc_ch), lambda j: (0, j))],
        out_specs=[ANY, pl.BlockSpec((kw, tc_ch), lambda j: (0, j)), pl.BlockSpec((1, tc_ch), lambda j: (0, j))],
        out_shape=[jax.ShapeDtypeStruct((t, d_pool + 2 * d_conv), BF16),
                   jax.ShapeDtypeStruct((kw, d_conv), F32), jax.ShapeDtypeStruct((1, d_conv), F32)],
        scratch_shapes=[pltpu.VMEM((t + HALO, tc_ch), F32), pltpu.VMEM((t + HALO, tc_ch), F32),
                        pltpu.VMEM((8 * kw, tc_ch), F32), pltpu.VMEM((8, tc_ch), F32),
                        pltpu.VMEM((2, t, tc_ch), BF16), pltpu.SemaphoreType.DMA((2,))],
        compiler_params=_params(("arbitrary",)),
    )(z, z, dc, conv_w)


def short_fwd(name, z, conv_w, d_short):
    t = z.shape[0]
    kw = conv_w.shape[0]
    tc_ch = _tile(d_short, CHANNEL_TILE)
    nt = d_short // tc_ch

    def body(b_ref, c_ref, u_ref, w_ref, y_ref, pad):
        pad[pl.ds(0, HALO), :] = jnp.zeros((HALO, tc_ch), F32)

        def fill(s, tc):
            pad[pl.ds(HALO + s, tc), :] = c_ref[pl.ds(s, tc), :].astype(F32) * u_ref[pl.ds(s, tc), :].astype(F32)

        def chunk(s, tc):
            win = pad[pl.ds(s, tc + HALO), :]
            cq = _taps(win, w_ref, [HALO - (kw - 1) + k for k in range(kw)], tc)
            y_ref[pl.ds(s, tc), :] = (b_ref[pl.ds(s, tc), :].astype(F32) * cq).astype(BF16)

        _chunks(t, fill)
        _chunks(t, chunk)

    return pl.pallas_call(
        body, name=name, grid=(nt,),
        in_specs=[pl.BlockSpec((t, tc_ch), lambda j: (0, j)), pl.BlockSpec((t, tc_ch), lambda j: (0, nt + j)),
                  pl.BlockSpec((t, tc_ch), lambda j: (0, 2 * nt + j)), pl.BlockSpec((kw, tc_ch), lambda j: (0, j))],
        out_specs=pl.BlockSpec((t, tc_ch), lambda j: (0, j)),
        out_shape=jax.ShapeDtypeStruct((t, d_short), BF16),
        scratch_shapes=[pltpu.VMEM((t + HALO, tc_ch), F32)],
        compiler_params=_params(("parallel",)),
    )(z, z, z, conv_w)


def short_bwd(name, z, dy, conv_w, d_short):
    t = z.shape[0]
    kw = conv_w.shape[0]
    tc_ch = _tile(d_short, CHANNEL_TILE)
    nt = d_short // tc_ch

    def body(b_ref, c_ref, u_ref, dy_ref, w_ref, dz_ref, dw_ref, pad_q, pad_dcq, acc_w, tiles, sems):
        j = pl.program_id(0)
        db_ref, dcg_ref, du_ref = tiles.at[0], tiles.at[1], tiles.at[2]
        writes = [pltpu.make_async_copy(tiles.at[p], dz_ref.at[:, pl.ds((p * nt + j) * tc_ch, tc_ch)], sems.at[p])
                  for p in range(3)]

        def wait_writes():
            for cp in writes:
                cp.wait()

        pad_q[pl.ds(0, HALO), :] = jnp.zeros((HALO, tc_ch), F32)
        pad_dcq[pl.ds(t, HALO), :] = jnp.zeros((HALO, tc_ch), F32)
        acc_w[...] = jnp.zeros_like(acc_w)

        def fill(s, tc):
            rows = pl.ds(s, tc)
            pad_q[pl.ds(HALO + s, tc), :] = c_ref[rows, :].astype(F32) * u_ref[rows, :].astype(F32)
            pad_dcq[rows, :] = dy_ref[rows, :].astype(F32) * b_ref[rows, :].astype(F32)

        def chunk(s, tc):
            rows = pl.ds(s, tc)
            win_q = pad_q[pl.ds(s, tc + HALO), :]
            dcq = pad_dcq[rows, :]
            cq = None
            for k, shifted in _shifted(win_q, [HALO - (kw - 1) + k for k in range(kw)], tc):
                acc_w[pl.ds(8 * k, 8), :] += _fold8(dcq * shifted)
                term = w_ref[k:k + 1, :] * shifted
                cq = term if cq is None else cq + term
            db_ref[rows, :] = (dy_ref[rows, :].astype(F32) * cq).astype(BF16)
            dq = _taps(pad_dcq[pl.ds(s, tc + HALO), :], w_ref, list(range(kw)), tc, flip=True)
            dcg_ref[rows, :] = (dq * u_ref[rows, :].astype(F32)).astype(BF16)
            du_ref[rows, :] = (dq * c_ref[rows, :].astype(F32)).astype(BF16)

        _chunks(t, fill)
        pl.when(j > 0)(wait_writes)
        _chunks(t, chunk)
        for cp in writes:
            cp.start()
        pl.when(j == nt - 1)(wait_writes)
        for k in range(kw):
            dw_ref[k:k + 1, :] = jnp.sum(acc_w[pl.ds(8 * k, 8), :], axis=0, keepdims=True)

    zspec = [pl.BlockSpec((t, tc_ch), lambda j, o=o: (0, o * nt + j)) for o in range(3)]
    return pl.pallas_call(
        body, name=name, grid=(nt,),
        in_specs=[*zspec, pl.BlockSpec((t, tc_ch), lambda j: (0, j)), pl.BlockSpec((kw, tc_ch), lambda j: (0, j))],
        out_specs=[ANY, pl.BlockSpec((kw, tc_ch), lambda j: (0, j))],
        out_shape=[jax.ShapeDtypeStruct((t, 3 * d_short), BF16), jax.ShapeDtypeStruct((kw, d_short), F32)],
        scratch_shapes=[pltpu.VMEM((t + HALO, tc_ch), F32), pltpu.VMEM((t + HALO, tc_ch), F32),
                        pltpu.VMEM((8 * kw, tc_ch), F32), pltpu.VMEM((3, t, tc_ch), BF16),
                        pltpu.SemaphoreType.DMA((3,))],
        compiler_params=_params(("arbitrary",)),
    )(z, z, z, dy, conv_w)


def _adamw_update(w, m, v, g):
    nm = ADAM_B1 * m + (1.0 - ADAM_B1) * g
    nv = ADAM_B2 * v + (1.0 - ADAM_B2) * (g * g)
    m_hat = nm / (1.0 - ADAM_B1 ** ADAM_STEP)
    v_hat = nv / (1.0 - ADAM_B2 ** ADAM_STEP)
    return -ADAM_LR * (m_hat / (jnp.sqrt(v_hat) + ADAM_EPS) + ADAM_WD * w), nm, nv


def adamw_replicated(name, params, first_moments, second_moments, contributions, layout, scalar_at):
    n = len(params)
    n_slots = contributions.shape[0]

    def total(c_ref, row, lane, rows, lanes):
        acc = c_ref[0, row:row + rows, lane:lane + lanes]
        for slot in range(1, n_slots):
            acc = acc + c_ref[slot, row:row + rows, lane:lane + lanes]
        return acc

    def body(*refs):
        ws, ms, vs, c_ref = refs[:n], refs[n:2 * n], refs[2 * n:3 * n], refs[3 * n]
        outs = refs[3 * n + 1:]
        outs[0][...] = total(c_ref, *scalar_at, 1, 128)
        for i, (row, lane) in enumerate(layout):
            g = total(c_ref, row, lane, *params[i].shape)
            grad_ref, delta_ref, nm_ref, nv_ref = outs[1 + 4 * i:5 + 4 * i]
            grad_ref[...] = g
            delta_ref[...], nm_ref[...], nv_ref[...] = _adamw_update(ws[i][...], ms[i][...], vs[i][...], g)

    out_shape = [jax.ShapeDtypeStruct((1, 128), F32)]
    for p in params:
        out_shape += [jax.ShapeDtypeStruct(p.shape, F32)] * 4
    return pl.pallas_call(body, name=name, out_shape=out_shape)(*params, *first_moments, *second_moments, contributions)


def adamw(name, w, m, v, contributions):
    r, c = w.shape
    nc = len(contributions)
    n_slots = contributions[0].shape[0]
    tr = 256 if c <= 1024 else 128
    if any(a.shape[1] % tr for a in contributions):
        assert nc == 1
        tr = r
    tiles = [a.shape[1] // tr for a in contributions]
    first = [sum(tiles[:j]) for j in range(nc)]

    def body(w_ref, m_ref, v_ref, *rest):
        g_refs, (grad_ref, delta_ref, nm_ref, nv_ref) = rest[:nc], rest[nc:]
        i = pl.program_id(0)
        g = None
        for j, g_ref in enumerate(g_refs):
            s = g_ref[0].astype(F32)
            for slot in range(1, n_slots):
                s = s + g_ref[slot].astype(F32)
            g = s if g is None else jnp.where(i >= first[j], s, g)
        grad_ref[...] = g
        delta_ref[...], nm_ref[...], nv_ref[...] = _adamw_update(w_ref[...], m_ref[...], v_ref[...], g)

    blk = pl.BlockSpec((tr, c), lambda i: (i, 0))
    g_specs = [pl.BlockSpec((n_slots, tr, c), lambda i, j=j: (0, jnp.clip(i - first[j], 0, tiles[j] - 1), 0))
               for j in range(nc)]
    return pl.pallas_call(
        body, name=name, grid=(r // tr,),
        in_specs=[blk, blk, blk, *g_specs],
        out_specs=[blk] * 4, out_shape=[jax.ShapeDtypeStruct((r, c), F32)] * 4,
        compiler_params=_params(("parallel",)),
    )(w, m, v, *contributions)


def _pad_rows(a, rows):
    return jnp.pad(a, ((0, rows - a.shape[0]), (0, 0)))


def kernel(x, mix_pre_g, mix_post_g, ffn_pre_g, ffn_post_g, ab_w_in, pool_w, pool_scale, conv_w, conv_b, conv_ln_g, conv_ln_b, ab_w_out, sc_w_in, sc_conv_w, sc_w_out, ffn_w1, ffn_w2, loss_target, m_mix_pre_g, m_mix_post_g, m_ffn_pre_g, m_ffn_post_g, m_ab_w_in, m_pool_w, m_pool_scale, m_conv_w, m_conv_b, m_conv_ln_g, m_conv_ln_b, m_ab_w_out, m_sc_w_in, m_sc_conv_w, m_sc_w_out, m_ffn_w1, m_ffn_w2, v_mix_pre_g, v_mix_post_g, v_ffn_pre_g, v_ffn_post_g, v_ab_w_in, v_pool_w, v_pool_scale, v_conv_w, v_conv_b, v_conv_ln_g, v_conv_ln_b, v_ab_w_out, v_sc_w_in, v_sc_conv_w, v_sc_w_out, v_ffn_w1, v_ffn_w2):
    t, d = x.shape[1], x.shape[2]
    d_pool = pool_scale.shape[1]
    d_conv = conv_b.shape[1]
    d_short = d
    ng, pg = pool_w.shape[1], pool_w.shape[3]
    kw, ks = conv_w.shape[1], sc_conv_w.shape[1]
    nb_ab, nb_sc, nb_ff = ab_w_in.shape[2], sc_w_in.shape[2], ffn_w1.shape[2]

    xs = x[0]
    target = loss_target[0]

    lanes = min(128, d_conv // N_DEV)
    small_rows = [kw * (d_conv // N_DEV) // lanes, ks * (d_short // N_DEV) // lanes, ng * (pg // N_DEV) * pg // lanes]
    small_total = -(-sum(small_rows) // 8) * 8
    r0, r1, r2 = small_rows[0], small_rows[0] + small_rows[1], sum(small_rows)

    def pack_small(a_conv, a_sconv, a_pool):
        parts = [a_conv[0].reshape(-1, lanes), a_sconv[0].reshape(-1, lanes), a_pool[0].reshape(-1, lanes)]
        return _pad_rows(jnp.concatenate(parts, axis=0), small_total)

    shards = {
        "ab_in": (ab_w_in, 0, BF16), "small": (pack_small(conv_w, sc_conv_w, pool_w)[None], 0, F32),
        "ab_out": (ab_w_out, 0, BF16), "ff1_0": (ffn_w1, 0, BF16), "ff2_0": (ffn_w2, 0, BF16),
        "sc_in": (sc_w_in, 0, BF16), "sc_out": (sc_w_out, 0, BF16),
        "ff1_1": (ffn_w1, 1, BF16), "ff2_1": (ffn_w2, 1, BF16)}
    zones = {"ab_in": place_shard("place_ab_in", *shards["ab_in"])}
    (first_started,), token = copies_start("gather_start_ab_in", [[zones["ab_in"]]], _first_hop, 4)
    zones["small"] = place_shard("place_small", *shards["small"], deps=[token])
    (small_started,), token = copies_start("gather_start", [[zones["small"]]], _first_hop, 4, deps=[token])
    started = {"small": small_started, "ab_in": first_started}
    ring = {}
    for nm in ["ab_out", "ff1_0"]:
        zones[nm] = place_shard("place_" + nm, *shards[nm], deps=[token])
        (ring[nm],), token = copies_start("ring_start_" + nm, [[zones[nm]]], _ring_hop1, 3, deps=[token])

    ties = [0]

    def after(v, *deps):
        ties[0] += 1
        return tie(f"tie_{ties[0]}", v, *deps)

    def fetch_begin(nm, dep):
        (zone,) = copies_wait("gather_wait_" + nm, started[nm], _first_hop, dep)
        (hop,), tok = copies_start("forward_start_" + nm, [[zone]], _second_hop, 3)
        return hop, tok

    def fetch_end(nm, hop, dep):
        return copies_wait("forward_wait_" + nm, hop, _second_hop, dep)[0]

    def ring_step(tag, dep, *starts):
        names, groups, hops, counts = [], [], [], []
        for n, nm in starts:
            if n == 1:
                groups.append([zones[nm]])
            elif n == 2:
                groups.append(copies_wait("ring1_wait_" + nm, ring[nm], _ring_hop1, dep))
            else:
                groups.append(copies_wait("ring2_wait_" + nm, ring[nm], _ring_hop2, dep))
            hop, n_copies = RING_HOPS[n - 1]
            names, hops, counts = names + [nm], hops + [hop], counts + [n_copies]
        begun, tok = copies_start("ring_start_" + tag, groups, hops, counts, deps=[dep])
        ring.update(zip(names, begun))
        return tok

    def ring_done(nm, dep):
        return copies_wait("ring3_wait_" + nm, ring[nm], _ring_hop3, dep)[0]

    relu = lambda r: jnp.maximum(r, 0.0)
    square = lambda a: a * a
    relu2_bwd = lambda r, a: r * (2.0 * a.astype(F32))

    def row(vec, l):
        return vec[l:l + 1]

    h0 = norm_pre("norm_pre", xs, after(row(mix_pre_g, 0), token))
    before_waits = h0
    for nm in ["ff2_0", "sc_in", "sc_out", "ff1_1", "ff2_1"]:
        before_waits = zones[nm] = place_shard("place_" + nm, *shards[nm], deps=[before_waits])
    hop_small, _ = fetch_begin("small", before_waits)
    hop_ab_in, tok = fetch_begin("ab_in", before_waits)
    w_small = fetch_end("small", hop_small, tok)
    w_ab_in = fetch_end("ab_in", hop_ab_in, tok)
    w_conv = w_small[:, :r0].reshape(N_DEV, kw, -1).transpose(1, 0, 2).reshape(kw, d_conv)
    w_sconv = w_small[:, r0:r1].reshape(N_DEV, ks, -1).transpose(1, 0, 2).reshape(ks, d_short)
    w_pool = w_small[:, r1:r2].reshape(N_DEV, ng, -1, pg).transpose(1, 0, 2, 3).reshape(ng, pg, pg).astype(BF16)
    z0 = mm_nn_blocked("ab_in", h0, w_ab_in, out_dtype=BF16)
    z0 = after(z0, ring_step("z", z0, (2, "ab_out")))
    pooled, y0 = pool_fwd("pool_fwd", z0, w_pool, pool_scale, d_pool, d_pool + d_conv)
    cv = conv_fwd("conv_fwd", z0, w_conv, conv_b, d_pool, d_conv)
    y0 = ln_silu("ln_silu", cv, conv_ln_g, conv_ln_b, y0, d_pool // d_conv)

    def ffn_up(name, h, w, *starts):
        a = mm_nn_blocked(name, h, w, out_dtype=BF16, epilogue=relu, blocks=(0, UP_FIRST_BLOCKS))
        tok = ring_step(name, a, *starts)
        return mm_nn_blocked(name + "_rest", h, w, out_dtype=BF16, epilogue=relu,
                             blocks=(UP_FIRST_BLOCKS, N_DEV - UP_FIRST_BLOCKS), into=after(a, tok))

    tok = ring_step("a", y0, (3, "ab_out"), (2, "ff1_0"), (1, "ff2_0"))
    w_ab_out = ring_done("ab_out", tok)
    m0 = mm_nn("ab_out", y0, w_ab_out.reshape(d_pool + d_conv, d), out_dtype=F32)
    x1, h1 = post_pre("post_pre_0", xs, m0, row(mix_post_g, 0), row(ffn_pre_g, 0))
    tok = ring_step("b", h1, (3, "ff1_0"))
    w_ff1_0 = ring_done("ff1_0", tok)
    a0 = ffn_up("ffn0_up", h1, w_ff1_0, (2, "ff2_0"), (1, "sc_in"), (1, "sc_out"))
    tok = ring_step("c", a0, (3, "ff2_0"))
    w_ff2_0 = ring_done("ff2_0", tok).reshape(-1, d)
    f0 = mm_nn("ffn0_down", a0, w_ff2_0, out_dtype=F32, tk=2048, lhs_fn=square)
    tok = ring_step("d", f0, (2, "sc_in"), (2, "sc_out"), (1, "ff1_1"))
    f0 = after(f0, tok)
    x2, h2 = post_pre("post_pre_1", x1, f0, row(ffn_post_g, 0), row(mix_pre_g, 1))
    tok = ring_step("e", h2, (3, "sc_in"))
    w_sc_in = ring_done("sc_in", tok)
    z1 = mm_nn_blocked("sc_in", h2, w_sc_in, out_dtype=BF16)
    y1 = short_fwd("short_fwd", z1, w_sconv, d_short)
    tok = ring_step("f", y1, (3, "sc_out"), (2, "ff1_1"), (1, "ff2_1"))
    w_sc_out = ring_done("sc_out", tok).reshape(d_short, d)
    m1 = mm_nn("sc_out", y1, w_sc_out, out_dtype=F32)
    x3, h3 = post_pre("post_pre_2", x2, m1, row(mix_post_g, 1), row(ffn_pre_g, 1))
    tok = ring_step("g", h3, (3, "ff1_1"))
    w_ff1_1 = ring_done("ff1_1", tok)
    a1 = ffn_up("ffn1_up", h3, w_ff1_1, (2, "ff2_1"))
    tok = ring_step("h", a1, (3, "ff2_1"))
    w_ff2_1 = ring_done("ff2_1", tok).reshape(-1, d)
    f1 = mm_nn("ffn1_down", a1, w_ff2_1, out_dtype=F32, tk=2048, lhs_fn=square)
    dx4, df1, loss_part, dg_ffn_post1 = post_loss("post_loss", x3, f1, row(ffn_post_g, 1), target)

    red = {}

    def reduce_step(dep, begin=None, middle=None):
        tags, groups, hops, counts = [], [], [], []
        if begin is not None:
            tag, g = begin
            tags, groups = tags + [tag], groups + [[g, lax.empty((N_CHIP,) + g.shape[1:], g.dtype)]]
            hops, counts = hops + [_pair_hop], counts + [N_CHIP]
        if middle is not None:
            g, from_sibling = copies_wait("pair_wait_" + middle, red[middle], _pair_hop, dep)
            tags, groups = tags + [middle], groups + [list(pair_add("pair_add_" + middle, g, from_sibling))]
            hops, counts = hops + [_chip_hop], counts + [3]
        begun, tok = copies_start("reduce_start_" + "_".join(tags), groups, hops, counts, deps=[dep])
        red.update(zip(tags, begun))
        return tok

    def reduce_end(tag, dep):
        return copies_wait("chips_wait_" + tag, red[tag], _chip_hop, dep)[1]

    dpre, dw = mm_bwd_pair("ffn1_da_dw2", df1, w_ff2_1, a1, out_dtype=BF16, act_fn=square, epilogue=relu2_bwd)
    dpre = after(dpre, reduce_step(dpre, begin=("ff2_1", dw.reshape(N_DEV, -1, d))))
    dh3, dw = mm_bwd_pair_blocked("ffn1_dh_dw1", dpre, w_ff1_1, h3, out_dtype=BF16)
    dx3, dm1, dg_ffn_pre1, dg_mix_post1 = bwd_pre_post("bwd_3", dx4, x3, row(ffn_pre_g, 1), dh3, m1, row(mix_post_g, 1))
    dm1 = after(dm1, reduce_step(dm1, begin=("ff1_1", dw), middle="ff2_1"))

    dy1, dw = mm_bwd_pair("sc_dy_dwout", dm1, w_sc_out, y1, out_dtype=BF16)
    dy1 = after(dy1, reduce_step(dy1, begin=("sc_out", dw.reshape(N_DEV, -1, d)), middle="ff1_1"))
    dz1, dw_sconv = short_bwd("short_bwd", z1, dy1, w_sconv, d_short)
    dh2, dw = mm_bwd_pair_blocked("sc_dh_dwin", dz1, w_sc_in, h2, out_dtype=BF16)
    dx2, df0, dg_mix_pre1, dg_ffn_post0 = bwd_pre_post("bwd_2", dx3, x2, row(mix_pre_g, 1), dh2, f0, row(ffn_post_g, 0))
    df0 = after(df0, reduce_step(df0, begin=("sc_in", dw), middle="sc_out"))

    dpre, dw = mm_bwd_pair("ffn0_da_dw2", df0, w_ff2_0, a0, out_dtype=BF16, act_fn=square, epilogue=relu2_bwd)
    dpre = after(dpre, reduce_step(dpre, begin=("ff2_0", dw.reshape(N_DEV, -1, d)), middle="sc_in"))
    dh1, dw = mm_bwd_pair_blocked("ffn0_dh_dw1", dpre, w_ff1_0, h1, out_dtype=BF16)
    dh1 = after(dh1, reduce_step(dh1, middle="ff2_0"))
    dx1, dm0, dg_ffn_pre0, dg_mix_post0 = bwd_pre_post("bwd_1", dx2, x1, row(ffn_pre_g, 0), dh1, m0, row(mix_post_g, 0))
    dm0 = after(dm0, reduce_step(dm0, begin=("ff1_0", dw)))

    dy0, dw = mm_bwd_pair("ab_dy_dwout", dm0, w_ab_out.reshape(d_pool + d_conv, d), y0, out_dtype=BF16)
    dy0 = after(dy0, reduce_step(dy0, begin=("ab_out", dw.reshape(N_DEV, -1, d)), middle="ff1_0"))
    dcv, dg_ln_g, dg_ln_b = ln_silu_bwd("ln_silu_bwd", cv, conv_ln_g, conv_ln_b, dy0, d_pool // d_conv)
    dz0, dw_conv, dg_conv_b = conv_bwd("conv_bwd", z0, dcv, w_conv, d_pool, d_conv)
    dz0, dw_pool, dg_pool_scale = pool_bwd("pool_bwd", pooled, dy0, w_pool, pool_scale, dz0)
    small_parts = [
        dw_conv.reshape(kw, N_DEV, -1).transpose(1, 0, 2).reshape(N_DEV, -1, lanes),
        dw_sconv.reshape(ks, N_DEV, -1).transpose(1, 0, 2).reshape(N_DEV, -1, lanes),
        dw_pool.reshape(ng, N_DEV, pg // N_DEV, pg).transpose(1, 0, 2, 3).reshape(N_DEV, -1, lanes),
    ]
    small = jnp.pad(jnp.concatenate(small_parts, axis=1), ((0, 0), (0, small_total - r2), (0, 0)))
    dz0 = after(dz0, reduce_step(dz0, begin=("small", small), middle="ab_out"))
    dh0, dw = mm_bwd_pair_blocked("ab_dh_dwin", dz0, w_ab_in, h0, out_dtype=BF16)
    dh0 = after(dh0, reduce_step(dh0, begin=("ab_in", dw), middle="small"))
    grad_x, dg_mix_pre0 = bwd_pre_final("bwd_0", dx1, xs, row(mix_pre_g, 0), dh0)
    tok = reduce_step(grad_x, middle="ab_in")

    gains = [dg_mix_pre0, dg_mix_pre1, dg_mix_post0, dg_mix_post1, dg_ffn_pre0, dg_ffn_pre1, dg_ffn_post0, dg_ffn_post1]
    pieces = [(g, i, 0) for i, g in enumerate(gains)]
    rep_layout = [(0, 0), (2, 0), (4, 0), (6, 0)]
    offset = 0
    for g in (dg_pool_scale, dg_conv_b, dg_ln_g, dg_ln_b):
        at = (len(gains) + offset // d, offset % d)
        pieces.append((g, *at))
        rep_layout.append(at)
        offset += g.shape[1]
    loss_at = (len(gains) + -(-offset // d), 0)
    rep_zone = place_sheet("place_rep", pieces, loss_part, loss_at, 16, d)
    (rep_hop,), tok = copies_start("rep_start", [[rep_zone]], _first_hop, 4,
                                   deps=[tok])

    def upd(name, w, m, v, contribs):
        shape = w.shape
        flat2 = lambda a: a.reshape(-1, shape[-1])
        outs = adamw(name, flat2(w), flat2(m), flat2(v), contribs)
        return [o.reshape(shape) for o in outs]

    g_ff2 = [reduce_end("ff2_0", tok), reduce_end("ff2_1", tok)]
    o_ff2 = upd("adam_ffn_w2", ffn_w2, m_ffn_w2, v_ffn_w2, g_ff2)
    (rep_zone,) = copies_wait("rep_wait", rep_hop, _first_hop, o_ff2[0])
    (rep_hop,), _ = copies_start("rep_forward_start", [[rep_zone]], _second_hop, 3)
    g_ff1 = [reduce_end("ff1_0", o_ff2[0]), reduce_end("ff1_1", o_ff2[0])]
    o_ff1 = upd("adam_ffn_w1", ffn_w1, m_ffn_w1, v_ffn_w1, g_ff1)
    (rep_all,) = copies_wait("rep_forward_wait", rep_hop, _second_hop, o_ff1[0])
    loss_sum, *o_rep = adamw_replicated(
        "adam_replicated",
        [mix_pre_g, mix_post_g, ffn_pre_g, ffn_post_g, pool_scale, conv_b, conv_ln_g, conv_ln_b],
        [m_mix_pre_g, m_mix_post_g, m_ffn_pre_g, m_ffn_post_g, m_pool_scale, m_conv_b, m_conv_ln_g, m_conv_ln_b],
        [v_mix_pre_g, v_mix_post_g, v_ffn_pre_g, v_ffn_post_g, v_pool_scale, v_conv_b, v_conv_ln_g, v_conv_ln_b],
        rep_all, rep_layout, loss_at)
    loss = loss_sum[0, 0] * (0.5 / d)
    o_sc_out = upd("adam_sc_out", sc_w_out, m_sc_w_out, v_sc_w_out, [reduce_end("sc_out", o_ff1[0])])
    o_sc_in = upd("adam_sc_in", sc_w_in, m_sc_w_in, v_sc_w_in, [reduce_end("sc_in", o_sc_out[0])])
    o_ab_out = upd("adam_ab_out", ab_w_out, m_ab_w_out, v_ab_w_out, [reduce_end("ab_out", o_sc_in[0])])
    o_small = adamw("adam_small", pack_small(conv_w, sc_conv_w, pool_w), pack_small(m_conv_w, m_sc_conv_w, m_pool_w),
                    pack_small(v_conv_w, v_sc_conv_w, v_pool_w), [reduce_end("small", o_ab_out[0])])
    o_ab_in = upd("adam_ab_in", ab_w_in, m_ab_w_in, v_ab_w_in, [reduce_end("ab_in", o_small[0])])

    def unpack_small(o):
        return o[:r0].reshape(conv_w.shape), o[r0:r1].reshape(sc_conv_w.shape), o[r1:r2].reshape(pool_w.shape)

    results = []
    for kind in range(4):
        g_mix_pre, g_mix_post, g_ffn_pre, g_ffn_post, g_scale, g_conv_b, g_ln_g, g_ln_b = o_rep[kind::4]
        s_conv, s_sconv, s_pool = unpack_small(o_small[kind])
        results.append([
            g_mix_pre, g_mix_post, g_ffn_pre, g_ffn_post,
            o_ab_in[kind], s_pool, g_scale, s_conv, g_conv_b, g_ln_g, g_ln_b,
            o_ab_out[kind], o_sc_in[kind], s_sconv, o_sc_out[kind], o_ff1[kind], o_ff2[kind]])

    return (loss, grad_x[None], *results[0], *results[1], *results[2], *results[3])
```
